```python
import jax, jax.numpy as jnp
from jax import lax
import numpy as np

D_MODEL = 2048
BATCH = 8
SEQ = 4096
DEPTH = 2

N_MIXERS = 2
BLOCK = 128
NEG_INF = -1e30
RMS_EPS = 1e-6
FOX_HEAD_DIM = 64
FOX_HEADS = D_MODEL // FOX_HEAD_DIM
FOX_WIDTH = FOX_HEADS * FOX_HEAD_DIM
FOX_IN = 4 * FOX_WIDTH + FOX_HEADS
SWA_HEAD_DIM = 64
SWA_Q_HEADS = D_MODEL // SWA_HEAD_DIM
SWA_KV_HEADS = SWA_Q_HEADS // 8
SWA_GROUP = SWA_Q_HEADS // SWA_KV_HEADS
SWA_WINDOW = 128
SWA_WIDTH = SWA_Q_HEADS * SWA_HEAD_DIM
SWA_KV_WIDTH = SWA_KV_HEADS * SWA_HEAD_DIM
SWA_IN = 2 * SWA_WIDTH + 2 * SWA_KV_WIDTH
ROPE_THETA = 500000.0
ROT_DIM = SWA_HEAD_DIM // 4
N_FOX_LAYERS = (DEPTH + 1) // 2
N_SWA_LAYERS = DEPTH // 2

kernel_name = "fox_swa_sink_interleaved_gated_hybrid"


def rmsnorm(x, g):
    x32 = x.astype(jnp.float32)
    y = x32 * lax.rsqrt(jnp.mean(x32 * x32, axis=-1, keepdims=True) + RMS_EPS)
    return (y * g.astype(jnp.float32)).astype(x.dtype)


def partial_rope(x, pos):
    half = ROT_DIM // 2
    inv_freq = ROPE_THETA ** (-jnp.arange(half, dtype=jnp.float32) / half)
    ang = pos[:, None] * inv_freq[None, :]
    cos = jnp.cos(ang)[None, :, None, :]
    sin = jnp.sin(ang)[None, :, None, :]
    x32 = x.astype(jnp.float32)
    x1, x2 = x32[..., :half], x32[..., half:ROT_DIM]
    rot = jnp.concatenate([x1 * cos - x2 * sin, x2 * cos + x1 * sin], axis=-1)
    return jnp.concatenate([rot.astype(x.dtype), x[..., ROT_DIM:]], axis=-1)


def fox_attention(q, k, v, log_f):
    B, S, H, d = q.shape
    nb = S // BLOCK
    scale = d ** -0.5
    c = jnp.cumsum(log_f, axis=1).transpose(0, 2, 1)
    kf = k.astype(jnp.float32)
    key_pos = jnp.arange(S)
    qb = q.reshape(B, nb, BLOCK, H, d).transpose(1, 0, 2, 3, 4)
    cb = c.reshape(B, H, nb, BLOCK).transpose(2, 0, 1, 3)

    def one_block(args):
        qi, ci, i = args
        s = jnp.einsum('bqhd,bkhd->bhqk', qi.astype(jnp.float32), kf) * scale
        s = s + ci[..., None] - c[:, :, None, :]
        qpos = i * BLOCK + jnp.arange(BLOCK)
        mask = key_pos[None, :] <= qpos[:, None]
        s = jnp.where(mask[None, None], s, NEG_INF)
        p = jax.nn.softmax(s, axis=-1).astype(v.dtype)
        return jnp.einsum('bhqk,bkhd->bqhd', p, v)

    out = lax.map(one_block, (qb, cb, jnp.arange(nb)))
    return out.transpose(1, 0, 2, 3, 4).reshape(B, S, H * d)


def swa_attention(q, k, v, sinks):
    B, S = q.shape[:2]
    nb = S // BLOCK
    scale = SWA_HEAD_DIM ** -0.5
    qb = q.reshape(B, nb, BLOCK, SWA_KV_HEADS, SWA_GROUP, SWA_HEAD_DIM)

    def band(t):
        tp = jnp.pad(t, ((0, 0), (BLOCK, 0), (0, 0), (0, 0)))
        tp = tp.reshape(B, nb + 1, BLOCK, SWA_KV_HEADS, SWA_HEAD_DIM)
        return jnp.concatenate([tp[:, :-1], tp[:, 1:]], axis=2)

    kw, vw = band(k), band(v)
    s = jnp.einsum('bnqhgd,bnkhd->bnhgqk', qb.astype(jnp.float32), kw.astype(jnp.float32)) * scale
    t_loc = jnp.arange(BLOCK)[:, None]
    j_loc = jnp.arange(2 * BLOCK)[None, :]
    diff = t_loc + BLOCK - j_loc
    key_abs = (jnp.arange(nb)[:, None, None] - 1) * BLOCK + j_loc[None]
    mask = (diff >= 0)[None] & (diff < SWA_WINDOW)[None] & (key_abs >= 0)
    s = jnp.where(mask[None, :, None, None], s, NEG_INF)
    sink = sinks.astype(jnp.float32).reshape(SWA_KV_HEADS, SWA_GROUP)[None, None, :, :, None, None]
    m = jnp.maximum(jnp.max(s, axis=-1, keepdims=True), sink)
    e = jnp.exp(s - m)
    denom = jnp.sum(e, axis=-1, keepdims=True) + jnp.exp(sink - m)
    p = (e / denom).astype(v.dtype)
    o = jnp.einsum('bnhgqk,bnkhd->bnqhgd', p, vw)
    return o.reshape(B, S, SWA_WIDTH)


def _fwd_setup_inputs(seed: int = 0) -> dict:
    key = jax.random.key(seed)
    ks = jax.random.split(key, 10)
    x = jax.random.normal(ks[0], (BATCH, SEQ, D_MODEL), jnp.float32)
    norm_g = 1.0 + 0.02 * jax.random.normal(ks[1], (DEPTH, D_MODEL), jnp.float32)
    fox_w_in = jax.random.normal(ks[2], (N_FOX_LAYERS, D_MODEL, FOX_IN), jnp.float32) * D_MODEL ** -0.5
    fox_b_f = (jnp.linspace(1.0, 6.0, FOX_HEADS, dtype=jnp.float32)[None, :]
               + 0.1 * jax.random.normal(ks[3], (N_FOX_LAYERS, FOX_HEADS), jnp.float32))
    fox_w_out = jax.random.normal(ks[4], (N_FOX_LAYERS, FOX_WIDTH, D_MODEL), jnp.float32) * FOX_WIDTH ** -0.5
    swa_w_in = jax.random.normal(ks[5], (N_SWA_LAYERS, D_MODEL, SWA_IN), jnp.float32) * D_MODEL ** -0.5
    swa_sinks = 0.5 * jax.random.normal(ks[6], (N_SWA_LAYERS, SWA_Q_HEADS), jnp.float32)
    swa_w_out = jax.random.normal(ks[7], (N_SWA_LAYERS, SWA_WIDTH, D_MODEL), jnp.float32) * SWA_WIDTH ** -0.5
    final_g = 1.0 + 0.02 * jax.random.normal(ks[8], (D_MODEL,), jnp.float32)
    return {"x": x, "norm_g": norm_g, "fox_w_in": fox_w_in, "fox_b_f": fox_b_f,
            "fox_w_out": fox_w_out, "swa_w_in": swa_w_in, "swa_sinks": swa_sinks,
            "swa_w_out": swa_w_out, "final_g": final_g}


def _fwd_reference(x, norm_g, fox_w_in, fox_b_f, fox_w_out, swa_w_in, swa_sinks, swa_w_out, final_g):
    B, S, _ = x.shape
    pos = jnp.arange(S, dtype=jnp.float32)
    for i in range(DEPTH):
        h = rmsnorm(x, norm_g[i])
        j = i // N_MIXERS
        if i % N_MIXERS == 0:
            p = h @ fox_w_in[j]
            W = FOX_WIDTH
            q = p[..., :W].reshape(B, S, FOX_HEADS, FOX_HEAD_DIM)
            k = p[..., W:2 * W].reshape(B, S, FOX_HEADS, FOX_HEAD_DIM)
            v = p[..., 2 * W:3 * W].reshape(B, S, FOX_HEADS, FOX_HEAD_DIM)
            gate = p[..., 3 * W:4 * W]
            log_f = jax.nn.log_sigmoid(p[..., 4 * W:].astype(jnp.float32) + fox_b_f[j].astype(jnp.float32))
            y = fox_attention(q, k, v, log_f)
            w_out = fox_w_out[j]
        else:
            p = h @ swa_w_in[j]
            WQ, WK = SWA_WIDTH, SWA_KV_WIDTH
            q = p[..., :WQ].reshape(B, S, SWA_Q_HEADS, SWA_HEAD_DIM)
            k = p[..., WQ:WQ + WK].reshape(B, S, SWA_KV_HEADS, SWA_HEAD_DIM)
            v = p[..., WQ + WK:WQ + 2 * WK].reshape(B, S, SWA_KV_HEADS, SWA_HEAD_DIM)
            gate = p[..., WQ + 2 * WK:]
            q = partial_rope(q, pos)
            k = partial_rope(k, pos)
            y = swa_attention(q, k, v, swa_sinks[j])
            w_out = swa_w_out[j]
        y = y * jax.nn.silu(gate)
        x = x + y @ w_out
    return rmsnorm(x, final_g)


import jax as _jax
import jax.numpy as _jnp

TWIN_FORMAT = 'train_step'
FWD_PARAMS = ['x', 'norm_g', 'fox_w_in', 'fox_b_f', 'fox_w_out', 'swa_w_in', 'swa_sinks', 'swa_w_out', 'final_g']
TWIN_WEIGHTS = ['norm_g', 'fox_w_in', 'fox_b_f', 'fox_w_out', 'swa_w_in', 'swa_sinks', 'swa_w_out', 'final_g']
TWIN_DIFF_INPUT = 'x'
TWIN_INPUTS = ['x', 'norm_g', 'fox_w_in', 'fox_b_f', 'fox_w_out', 'swa_w_in', 'swa_sinks', 'swa_w_out', 'final_g', 'loss_target', 'm_norm_g', 'm_fox_w_in', 'm_fox_b_f', 'm_fox_w_out', 'm_swa_w_in', 'm_swa_sinks', 'm_swa_w_out', 'm_final_g', 'v_norm_g', 'v_fox_w_in', 'v_fox_b_f', 'v_fox_w_out', 'v_swa_w_in', 'v_swa_sinks', 'v_swa_w_out', 'v_final_g']
TWIN_OUTPUTS = ['loss', 'grad_x', 'grad_norm_g', 'grad_fox_w_in', 'grad_fox_b_f', 'grad_fox_w_out', 'grad_swa_w_in', 'grad_swa_sinks', 'grad_swa_w_out', 'grad_final_g', 'delta_norm_g', 'delta_fox_w_in', 'delta_fox_b_f', 'delta_fox_w_out', 'delta_swa_w_in', 'delta_swa_sinks', 'delta_swa_w_out', 'delta_final_g', 'new_m_norm_g', 'new_m_fox_w_in', 'new_m_fox_b_f', 'new_m_fox_w_out', 'new_m_swa_w_in', 'new_m_swa_sinks', 'new_m_swa_w_out', 'new_m_final_g', 'new_v_norm_g', 'new_v_fox_w_in', 'new_v_fox_b_f', 'new_v_fox_w_out', 'new_v_swa_w_in', 'new_v_swa_sinks', 'new_v_swa_w_out', 'new_v_final_g']
TWIN_LEAF_KINDS = {'loss': 'loss', 'grad_x': 'grad_x', 'grad_norm_g': 'grad_w', 'grad_fox_w_in': 'grad_w', 'grad_fox_b_f': 'grad_w', 'grad_fox_w_out': 'grad_w', 'grad_swa_w_in': 'grad_w', 'grad_swa_sinks': 'grad_w', 'grad_swa_w_out': 'grad_w', 'grad_final_g': 'grad_w', 'delta_norm_g': 'delta_w', 'delta_fox_w_in': 'delta_w', 'delta_fox_b_f': 'delta_w', 'delta_fox_w_out': 'delta_w', 'delta_swa_w_in': 'delta_w', 'delta_swa_sinks': 'delta_w', 'delta_swa_w_out': 'delta_w', 'delta_final_g': 'delta_w', 'new_m_norm_g': 'new_m', 'new_m_fox_w_in': 'new_m', 'new_m_fox_b_f': 'new_m', 'new_m_fox_w_out': 'new_m', 'new_m_swa_w_in': 'new_m', 'new_m_swa_sinks': 'new_m', 'new_m_swa_w_out': 'new_m', 'new_m_final_g': 'new_m', 'new_v_norm_g': 'new_v', 'new_v_fox_w_in': 'new_v', 'new_v_fox_b_f': 'new_v', 'new_v_fox_w_out': 'new_v', 'new_v_swa_w_in': 'new_v', 'new_v_swa_sinks': 'new_v', 'new_v_swa_w_out': 'new_v', 'new_v_final_g': 'new_v'}


def _forward(args):
    return _fwd_reference(*[args[k] for k in FWD_PARAMS])


def _output_shape():
    def fwd():
        inp = _fwd_setup_inputs(0)
        return _fwd_reference(*[inp[k] for k in FWD_PARAMS])
    out = _jax.eval_shape(fwd)
    return out.shape, out.dtype

N_MICROBATCH = 1
ADAM_LR = 0.001
ADAM_B1 = 0.9
ADAM_B2 = 0.999
ADAM_EPS = 1e-08
ADAM_WD = 0.01
ADAM_STEP = 10
PER_EXAMPLE_BATCH_AXIS = {'x': 0, 'loss_target': 0}
SHARED_INPUTS = []
_WEIGHT_DTYPES = {'norm_g': _jnp.float32, 'fox_w_in': _jnp.float32, 'fox_b_f': _jnp.float32, 'fox_w_out': _jnp.float32, 'swa_w_in': _jnp.float32, 'swa_sinks': _jnp.float32, 'swa_w_out': _jnp.float32, 'final_g': _jnp.float32}
MOMENT_SCALE = {'norm_g': 2.616688e-02, 'fox_w_in': 1.580481e-02, 'fox_b_f': 6.408182e-02, 'fox_w_out': 1.759521e-02, 'swa_w_in': 1.311948e-02, 'swa_sinks': 7.714042e-03, 'swa_w_out': 1.028651e-02, 'final_g': 1.599688e+01}


def _to_microbatches(a, axis):
    t = _jnp.moveaxis(a, axis, 0)
    t = t.reshape((N_MICROBATCH, t.shape[0] // N_MICROBATCH) + t.shape[1:])
    return _jnp.moveaxis(t, 1, axis + 1)


def setup_inputs(seed: int = 0) -> dict:
    inp = _fwd_setup_inputs(seed)
    key = _jax.random.fold_in(_jax.random.key(seed), 7919)
    shape, _ = _output_shape()
    out = dict(inp)
    out["loss_target"] = _jax.random.normal(_jax.random.fold_in(key, 0), shape, _jnp.float32)
    for i, name in enumerate(TWIN_WEIGHTS):
        w = inp[name].astype(_jnp.float32)
        if MOMENT_SCALE is None:
            s = _jnp.sqrt(_jnp.mean(_jnp.square(w)) + 1e-30)
        else:
            s = MOMENT_SCALE[name]
        km, kv = _jax.random.split(_jax.random.fold_in(key, i + 1))
        out[name] = w
        out["m_" + name] = s * _jax.random.normal(km, w.shape, _jnp.float32)
        out["v_" + name] = (s * s) * _jax.random.uniform(kv, w.shape, _jnp.float32, 0.5, 1.5)
    if N_MICROBATCH > 1:
        for name, axis in PER_EXAMPLE_BATCH_AXIS.items():
            out[name] = _to_microbatches(out[name], axis)
    return {'x': out['x'], 'norm_g': out['norm_g'], 'fox_w_in': out['fox_w_in'], 'fox_b_f': out['fox_b_f'], 'fox_w_out': out['fox_w_out'], 'swa_w_in': out['swa_w_in'], 'swa_sinks': out['swa_sinks'], 'swa_w_out': out['swa_w_out'], 'final_g': out['final_g'], 'loss_target': out['loss_target'], 'm_norm_g': out['m_norm_g'], 'm_fox_w_in': out['m_fox_w_in'], 'm_fox_b_f': out['m_fox_b_f'], 'm_fox_w_out': out['m_fox_w_out'], 'm_swa_w_in': out['m_swa_w_in'], 'm_swa_sinks': out['m_swa_sinks'], 'm_swa_w_out': out['m_swa_w_out'], 'm_final_g': out['m_final_g'], 'v_norm_g': out['v_norm_g'], 'v_fox_w_in': out['v_fox_w_in'], 'v_fox_b_f': out['v_fox_b_f'], 'v_fox_w_out': out['v_fox_w_out'], 'v_swa_w_in': out['v_swa_w_in'], 'v_swa_sinks': out['v_swa_sinks'], 'v_swa_w_out': out['v_swa_w_out'], 'v_final_g': out['v_final_g']}


def _loss(weights, diff, rest, loss_target):
    with _jax.named_scope("forward"):
        args = {**rest, TWIN_DIFF_INPUT: diff, **{k: w.astype(_WEIGHT_DTYPES[k]) for k, w in weights.items()}}
        y = _forward(args)
    with _jax.named_scope("loss_head"):
        err = _jnp.square(y.astype(_jnp.float32) - loss_target)
        return 0.5 * _jnp.sum(_jnp.mean(err, axis=-1)) if err.ndim else 0.5 * err


def _adamw(w, g, m, v):
    m = ADAM_B1 * m + (1.0 - ADAM_B1) * g
    v = ADAM_B2 * v + (1.0 - ADAM_B2) * _jnp.square(g)
    m_hat = m / (1.0 - ADAM_B1 ** ADAM_STEP)
    v_hat = v / (1.0 - ADAM_B2 ** ADAM_STEP)
    delta = -ADAM_LR * (m_hat / (_jnp.sqrt(v_hat) + ADAM_EPS) + ADAM_WD * w)
    return delta, m, v


def reference(x, norm_g, fox_w_in, fox_b_f, fox_w_out, swa_w_in, swa_sinks, swa_w_out, final_g, loss_target, m_norm_g, m_fox_w_in, m_fox_b_f, m_fox_w_out, m_swa_w_in, m_swa_sinks, m_swa_w_out, m_final_g, v_norm_g, v_fox_w_in, v_fox_b_f, v_fox_w_out, v_swa_w_in, v_swa_sinks, v_swa_w_out, v_final_g):
    given = dict(x=x, norm_g=norm_g, fox_w_in=fox_w_in, fox_b_f=fox_b_f, fox_w_out=fox_w_out, swa_w_in=swa_w_in, swa_sinks=swa_sinks, swa_w_out=swa_w_out, final_g=final_g, loss_target=loss_target, m_norm_g=m_norm_g, m_fox_w_in=m_fox_w_in, m_fox_b_f=m_fox_b_f, m_fox_w_out=m_fox_w_out, m_swa_w_in=m_swa_w_in, m_swa_sinks=m_swa_sinks, m_swa_w_out=m_swa_w_out, m_final_g=m_final_g, v_norm_g=v_norm_g, v_fox_w_in=v_fox_w_in, v_fox_b_f=v_fox_b_f, v_fox_w_out=v_fox_w_out, v_swa_w_in=v_swa_w_in, v_swa_sinks=v_swa_sinks, v_swa_w_out=v_swa_w_out, v_final_g=v_final_g)
    weights = {n: given[n] for n in TWIN_WEIGHTS}
    shared = {n: given[n] for n in SHARED_INPUTS}
    per_example = {n: given[n] for n in ['x']}
    grad_fn = _jax.value_and_grad(_loss, argnums=(0, 1))

    def one_microbatch(ex, loss_target):
        ex = dict(ex)
        diff = ex.pop(TWIN_DIFF_INPUT)
        return grad_fn(weights, diff, {**shared, **ex}, loss_target)

    if N_MICROBATCH == 1:
        loss, (grad_w, grad_x) = one_microbatch(per_example, given["loss_target"])
    else:
        def body(carry, xs):
            loss_sum, grad_sum = carry
            l_k, (gw_k, gx_k) = one_microbatch(xs[0], xs[1])
            with _jax.named_scope("update"):
                return (loss_sum + l_k, _jax.tree.map(_jnp.add, grad_sum, gw_k)), gx_k

        init = (_jnp.zeros((), _jnp.float32), _jax.tree.map(_jnp.zeros_like, weights))
        (loss, grad_w), grad_x = _jax.lax.scan(body, init, (per_example, given["loss_target"]))
    with _jax.named_scope("update"):
        delta_w, new_m, new_v = {}, {}, {}
        for n in TWIN_WEIGHTS:
            delta_w[n], new_m[n], new_v[n] = _adamw(weights[n], grad_w[n], given["m_" + n], given["v_" + n])
    return (loss, grad_x, *[grad_w[n] for n in TWIN_WEIGHTS], *[delta_w[n] for n in TWIN_WEIGHTS],
            *[new_m[n] for n in TWIN_WEIGHTS], *[new_v[n] for n in TWIN_WEIGHTS])
```

```python
import functools
import math

import jax
import jax.numpy as jnp
from jax import lax
from jax.experimental import pallas as pl
from jax.experimental.pallas import tpu as pltpu

F32 = jnp.float32
BF16 = jnp.bfloat16
MESH = pl.DeviceIdType.MESH

N_DEV = 8
LANE = 128
HEAD_DIM = 64
SWA_BLOCK = 128
NEG_INF = -1e30
RMS_EPS = 1e-6
ROPE_THETA = 500000.0
ROT_DIM = HEAD_DIM // 4
ADAM_LR, ADAM_B1, ADAM_B2, ADAM_EPS, ADAM_WD, ADAM_STEP = 0.001, 0.9, 0.999, 1e-08, 0.01, 10
VMEM_LIMIT = 56 * 1024 * 1024


def _cparams(**kw):
    return pltpu.CompilerParams(vmem_limit_bytes=VMEM_LIMIT, **kw)


def _tile(n, cap):
    if n <= cap:
        return n
    t = (cap // LANE) * LANE
    while t > LANE and n % t:
        t -= LANE
    assert n % t == 0, (n, cap)
    return t


def _row_tile(n, cap):
    t = min(n, cap)
    while n % t:
        t //= 2
    return t


def _dot_nn(a, b):
    return jnp.dot(a, b, preferred_element_type=F32)


def _dot_nt(a, b):
    return lax.dot_general(a, b, (((1,), (1,)), ((), ())), preferred_element_type=F32)


def _dot_tn(a, b):
    return lax.dot_general(a, b, (((0,), (0,)), ((), ())), preferred_element_type=F32)


def _sigmoid(g):
    return 1.0 / (1.0 + jnp.exp(-g))


def _slab_geom(w):
    starts = [w * i for i in range(N_DEV)]
    aligned = [LANE * (s // LANE) for s in starts]
    offs = [s - a for s, a in zip(starts, aligned)]
    sw = LANE * (-(-(max(offs) + w) // LANE))
    return aligned, sw, aligned[-1] + sw


def _my_index():
    return 4 * lax.axis_index("x") + 2 * lax.axis_index("y") + lax.axis_index("c")


def _all_gather(arrs):
    n = len(arrs)

    def body(*refs):
        ins, outs = refs[:n], refs[n:2 * n]
        send_sems, recv_sems, local_sems = refs[2 * n:]
        x, y, c = lax.axis_index("x"), lax.axis_index("y"), lax.axis_index("c")
        me, sib = (x, y, c), (x, y, 1 - c)
        chips = [(1 - x, y), (x, 1 - y), (1 - x, 1 - y)]

        def idx(px, py, pc):
            return 4 * px + 2 * py + pc

        def copy(a, k, block, to, src=None):
            dst = outs[a].at[idx(*block)]
            return pltpu.make_async_remote_copy(
                src_ref=dst if src is None else src, dst_ref=dst,
                send_sem=send_sems.at[a, k], recv_sem=recv_sems.at[a, k],
                device_id=to, device_id_type=MESH)

        mine = [pltpu.make_async_copy(ins[a], outs[a].at[idx(*me)], local_sems.at[a]) for a in range(n)]
        for m in mine:
            m.start()
        first = []
        for a in range(n):
            first.append(copy(a, 0, me, sib, src=ins[a]))
            for j, chip in enumerate(chips):
                first.append(copy(a, 1 + j, me, (*chip, c), src=ins[a]))
        for cp in first:
            cp.start()
        passed = []
        for j, chip in enumerate(chips):
            for a in range(n):
                copy(a, 1 + j, (*chip, c), me).wait_recv()
                p = copy(a, 4 + j, (*chip, c), sib)
                p.start()
                passed.append(p)
        for a in range(n):
            copy(a, 0, sib, me).wait_recv()
        for j, chip in enumerate(chips):
            for a in range(n):
                copy(a, 4 + j, (*chip, 1 - c), me).wait_recv()
        for cp in first + passed:
            cp.wait_send()
        for m in mine:
            m.wait()

    any_spec = pl.BlockSpec(memory_space=pl.ANY)
    return pl.pallas_call(
        body, name="weights_all_gather",
        out_shape=[jax.ShapeDtypeStruct((N_DEV,) + a.shape, a.dtype) for a in arrs],
        in_specs=[any_spec] * n, out_specs=[any_spec] * n,
        scratch_shapes=[pltpu.SemaphoreType.DMA((n, 7)), pltpu.SemaphoreType.DMA((n, 7)),
                        pltpu.SemaphoreType.DMA((n,))],
    )(*arrs)


def _rel_chip(r):
    x, y = lax.axis_index("x"), lax.axis_index("y")
    return (x ^ (r >> 1), y ^ (r & 1))


def _rs_windows(specs):
    def window(ref, spec, blk):
        kind, n = spec
        if kind == "col":
            _, sw, _ = _slab_geom(n)
            start = pl.multiple_of(LANE * ((n * blk) // LANE), LANE)
            return ref.at[:, pl.ds(start, sw)]
        start = pl.multiple_of(n * blk, n)
        return ref.at[pl.ds(start, n), :]
    return window


def _reduce_scatter_stage1(grads, specs):
    n = len(grads)
    window = _rs_windows(specs)

    def blk_shape(g, spec):
        kind, w = spec
        return (g.shape[0], _slab_geom(w)[1]) if kind == "col" else (w, g.shape[1])

    shapes = [blk_shape(g, s) for g, s in zip(grads, specs)]

    def body(*refs):
        ins, owns, recvs = refs[:n], refs[n:2 * n], refs[2 * n:3 * n]
        send_sems, recv_sems, local_sems = refs[3 * n:]
        x, y, c = lax.axis_index("x"), lax.axis_index("y"), lax.axis_index("c")
        sib = (x, y, 1 - c)
        locals_, remotes = [], []
        for a in range(n):
            for r in range(4):
                px, py = _rel_chip(r)
                own_blk = 4 * px + 2 * py + c
                sib_blk = 4 * px + 2 * py + (1 - c)
                locals_.append(pltpu.make_async_copy(window(ins[a], specs[a], own_blk), owns[a].at[r],
                                                     local_sems.at[a, r]))
                remotes.append(pltpu.make_async_remote_copy(
                    src_ref=window(ins[a], specs[a], sib_blk), dst_ref=recvs[a].at[r],
                    send_sem=send_sems.at[a, r], recv_sem=recv_sems.at[a, r],
                    device_id=sib, device_id_type=MESH))
        for cp in remotes + locals_:
            cp.start()
        for cp in remotes:
            cp.wait_recv()
        for cp in remotes:
            cp.wait_send()
        for cp in locals_:
            cp.wait()

    any_spec = pl.BlockSpec(memory_space=pl.ANY)
    outs = pl.pallas_call(
        body, name="grads_rs_sibling",
        out_shape=[jax.ShapeDtypeStruct((4,) + s, F32) for s in shapes] * 2,
        in_specs=[any_spec] * n, out_specs=[any_spec] * (2 * n),
        scratch_shapes=[pltpu.SemaphoreType.DMA((n, 4)), pltpu.SemaphoreType.DMA((n, 4)),
                        pltpu.SemaphoreType.DMA((n, 4))],
    )(*grads)
    return outs[:n], outs[n:]


def _pair_sum(own, recv):
    _, R, C = own.shape
    tr = _row_tile(R, 256)

    def body(a_ref, b_ref, f_ref, h_ref):
        s = a_ref[...] + b_ref[...]
        f_ref[...] = s
        h_ref[...] = s.astype(BF16)

    spec = pl.BlockSpec((1, tr, C), lambda r, i: (r, i, 0))
    return pl.pallas_call(
        body, name="grads_pair_sum", grid=(4, R // tr),
        in_specs=[spec, spec], out_specs=[spec, spec],
        out_shape=[jax.ShapeDtypeStruct(own.shape, F32), jax.ShapeDtypeStruct(own.shape, BF16)],
        compiler_params=_cparams(),
    )(own, recv)


def _reduce_scatter_stage2(parts):
    n = len(parts)

    def body(*refs):
        ins, recvs = refs[:n], refs[n:2 * n]
        send_sems, recv_sems = refs[2 * n:]
        c = lax.axis_index("c")
        copies = []
        for a in range(n):
            for r in range(1, 4):
                px, py = _rel_chip(r)
                copies.append(pltpu.make_async_remote_copy(
                    src_ref=ins[a].at[r], dst_ref=recvs[a].at[r - 1],
                    send_sem=send_sems.at[a, r - 1], recv_sem=recv_sems.at[a, r - 1],
                    device_id=(px, py, c), device_id_type=MESH))
        for cp in copies:
            cp.start()
        for cp in copies:
            cp.wait_recv()
        for cp in copies:
            cp.wait_send()

    any_spec = pl.BlockSpec(memory_space=pl.ANY)
    return pl.pallas_call(
        body, name="grads_rs_chips",
        out_shape=[jax.ShapeDtypeStruct((3,) + p.shape[1:], BF16) for p in parts],
        in_specs=[any_spec] * n, out_specs=[any_spec] * n,
        scratch_shapes=[pltpu.SemaphoreType.DMA((n, 3)), pltpu.SemaphoreType.DMA((n, 3))],
    )(*parts)


def _final_sum(psum, recv):
    _, R, C = psum.shape
    tr = _row_tile(R, 256)

    def body(p_ref, r_ref, o_ref):
        r = r_ref[...].astype(F32)
        o_ref[...] = ((p_ref[0] + r[0]) + r[1]) + r[2]

    return pl.pallas_call(
        body, name="grads_final_sum", grid=(R // tr,),
        in_specs=[pl.BlockSpec((1, tr, C), lambda i: (0, i, 0)), pl.BlockSpec((3, tr, C), lambda i: (0, i, 0))],
        out_specs=pl.BlockSpec((tr, C), lambda i: (i, 0)),
        out_shape=jax.ShapeDtypeStruct((R, C), F32),
        compiler_params=_cparams(),
    )(psum, recv)


def _all_reduce_small(pack):
    R, P = pack.shape

    def body(x_ref, o_ref, gat_ref, send_sems, recv_sems):
        x, y, c = lax.axis_index("x"), lax.axis_index("y"), lax.axis_index("c")
        me = 4 * x + 2 * y + c
        gat_ref[me] = x_ref[...]
        copies = []
        for k in range(1, N_DEV):
            peer = (x ^ (k >> 2), y ^ ((k >> 1) & 1), c ^ (k & 1))
            copies.append(pltpu.make_async_remote_copy(
                src_ref=x_ref, dst_ref=gat_ref.at[me],
                send_sem=send_sems.at[k - 1], recv_sem=recv_sems.at[k - 1],
                device_id=peer, device_id_type=MESH))
        for cp in copies:
            cp.start()
        for cp in copies:
            cp.wait_recv()
        for cp in copies:
            cp.wait_send()
        acc = gat_ref[0]
        for d in range(1, N_DEV):
            acc = acc + gat_ref[d]
        o_ref[...] = acc

    vm = pl.BlockSpec(memory_space=pltpu.VMEM)
    return pl.pallas_call(
        body, name="small_all_reduce",
        out_shape=jax.ShapeDtypeStruct((R, P), F32),
        in_specs=[vm], out_specs=vm,
        scratch_shapes=[pltpu.VMEM((N_DEV, R, P), F32),
                        pltpu.SemaphoreType.DMA((N_DEV - 1,)), pltpu.SemaphoreType.DMA((N_DEV - 1,))],
    )(pack)


def _assemble(slabs, w):
    aligned, sw, total = _slab_geom(w)
    K = slabs.shape[1]
    tr = _row_tile(K, 256)

    def body(s_ref, o_ref):
        o_ref[...] = jnp.zeros(o_ref.shape, BF16)
        for i in range(N_DEV):
            a = aligned[i]
            o_ref[:, a:a + sw] = o_ref[:, a:a + sw] + s_ref[i]

    return pl.pallas_call(
        body, name="assemble_w_in", grid=(K // tr,),
        in_specs=[pl.BlockSpec((N_DEV, tr, sw), lambda i: (0, i, 0))],
        out_specs=pl.BlockSpec((tr, total), lambda i: (i, 0)),
        out_shape=jax.ShapeDtypeStruct((K, total), BF16),
        compiler_params=_cparams(),
    )(slabs)


def _rmsnorm_fwd(x, g, name):
    S, D = x.shape
    tm = _row_tile(S, 256)

    def body(x_ref, g_ref, h_ref):
        xv = x_ref[...]
        r = lax.rsqrt(jnp.mean(xv * xv, axis=-1, keepdims=True) + RMS_EPS)
        h_ref[...] = ((xv * r) * g_ref[...]).astype(BF16)

    return pl.pallas_call(
        body, name=name, grid=(S // tm,),
        in_specs=[pl.BlockSpec((tm, D), lambda i: (i, 0)), pl.BlockSpec((1, D), lambda i: (0, 0))],
        out_specs=pl.BlockSpec((tm, D), lambda i: (i, 0)),
        out_shape=jax.ShapeDtypeStruct((S, D), BF16),
        compiler_params=_cparams(),
    )(x, g)


def _rmsnorm_bwd(dh, x, g, dres, name):
    S, D = x.shape
    tm = _row_tile(S, 256)

    def body(dh_ref, x_ref, g_ref, dr_ref, dx_ref, dxb_ref, dg_ref):
        xv = x_ref[...]
        r = lax.rsqrt(jnp.mean(xv * xv, axis=-1, keepdims=True) + RMS_EPS)
        xhat = xv * r
        d = dh_ref[...]
        gd = d * g_ref[...]
        dx = r * (gd - xhat * jnp.mean(gd * xhat, axis=-1, keepdims=True)) + dr_ref[...]
        dx_ref[...] = dx
        dxb_ref[...] = dx.astype(BF16)

        @pl.when(pl.program_id(0) == 0)
        def _():
            dg_ref[...] = jnp.zeros(dg_ref.shape, F32)
        dg_ref[...] += jnp.sum(d * xhat, axis=0, keepdims=True)

    row = pl.BlockSpec((tm, D), lambda i: (i, 0))
    vec = pl.BlockSpec((1, D), lambda i: (0, 0))
    return pl.pallas_call(
        body, name=name, grid=(S // tm,),
        in_specs=[row, row, vec, row], out_specs=[row, row, vec],
        out_shape=[jax.ShapeDtypeStruct((S, D), F32), jax.ShapeDtypeStruct((S, D), BF16),
                   jax.ShapeDtypeStruct((1, D), F32)],
        compiler_params=_cparams(),
    )(dh, x, g, dres)


def _loss_head(x, tgt, g):
    S, D = x.shape
    tm = _row_tile(S, 256)

    def body(x_ref, t_ref, g_ref, dx_ref, dxb_ref, dg_ref, loss_ref):
        xv = x_ref[...]
        r = lax.rsqrt(jnp.mean(xv * xv, axis=-1, keepdims=True) + RMS_EPS)
        xhat = xv * r
        gv = g_ref[...]
        err = xhat * gv - t_ref[...]
        d = err * (1.0 / D)
        gd = d * gv
        dx = r * (gd - xhat * jnp.mean(gd * xhat, axis=-1, keepdims=True))
        dx_ref[...] = dx
        dxb_ref[...] = dx.astype(BF16)

        @pl.when(pl.program_id(0) == 0)
        def _():
            dg_ref[...] = jnp.zeros(dg_ref.shape, F32)
            loss_ref[...] = jnp.zeros(loss_ref.shape, F32)
        dg_ref[...] += jnp.sum(d * xhat, axis=0, keepdims=True)
        per_tok = jnp.sum(err * err, axis=-1, keepdims=True) * (1.0 / D)
        loss_ref[...] += 0.5 * jnp.sum(per_tok, axis=0, keepdims=True)

    row = pl.BlockSpec((tm, D), lambda i: (i, 0))
    vec = pl.BlockSpec((1, D), lambda i: (0, 0))
    return pl.pallas_call(
        body, name="loss_head", grid=(S // tm,),
        in_specs=[row, row, vec],
        out_specs=[row, row, vec, pl.BlockSpec((1, LANE), lambda i: (0, 0))],
        out_shape=[jax.ShapeDtypeStruct((S, D), F32), jax.ShapeDtypeStruct((S, D), BF16),
                   jax.ShapeDtypeStruct((1, D), F32), jax.ShapeDtypeStruct((1, LANE), F32)],
        compiler_params=_cparams(),
    )(x, tgt, g)


def _adamw(w, g, m, v, name):
    R, C = w.shape
    tr = _row_tile(R, 256)
    c1 = 1.0 - ADAM_B1 ** ADAM_STEP
    c2 = 1.0 - ADAM_B2 ** ADAM_STEP

    def body(w_ref, g_ref, m_ref, v_ref, d_ref, nm_ref, nv_ref):
        gv = g_ref[...]
        nm = ADAM_B1 * m_ref[...] + (1.0 - ADAM_B1) * gv
        nv = ADAM_B2 * v_ref[...] + (1.0 - ADAM_B2) * (gv * gv)
        d_ref[...] = -ADAM_LR * ((nm / c1) / (jnp.sqrt(nv / c2) + ADAM_EPS) + ADAM_WD * w_ref[...])
        nm_ref[...] = nm
        nv_ref[...] = nv

    spec = pl.BlockSpec((tr, C), lambda i: (i, 0))
    return pl.pallas_call(
        body, name=name, grid=(R // tr,),
        in_specs=[spec] * 4, out_specs=[spec] * 3,
        out_shape=[jax.ShapeDtypeStruct((R, C), F32)] * 3,
        compiler_params=_cparams(),
    )(w, g, m, v)


def _proj(h, wfull, col0, ncols, out_dtype, name, rope=None):
    S, K = h.shape
    tm = _row_tile(S, 512)
    tn = math.gcd(_tile(ncols, 512), col0) if col0 else _tile(ncols, 512)
    if rope is not None:
        tn = rope[0].shape[2]
    assert ncols % tn == 0 and col0 % tn == 0
    cb = col0 // tn

    def body(*refs):
        if rope is None:
            a_ref, b_ref, o_ref = refs
        else:
            a_ref, b_ref, t_ref, o_ref = refs
        acc = _dot_nn(a_ref[...], b_ref[...])
        if rope is not None:
            roped = (acc * t_ref[0] + pltpu.roll(acc, tn - ROT_DIM // 2, axis=1) * t_ref[1]
                     + pltpu.roll(acc, ROT_DIM // 2, axis=1) * t_ref[2])
            acc = jnp.where(pl.program_id(1) < rope[1] // tn, roped, acc)
        o_ref[...] = acc.astype(out_dtype)

    in_specs = [pl.BlockSpec((tm, K), lambda i, j: (i, 0)), pl.BlockSpec((K, tn), lambda i, j: (0, cb + j))]
    args = [h, wfull]
    if rope is not None:
        in_specs.append(pl.BlockSpec((3, tm, tn), lambda i, j: (0, i, 0)))
        args.append(rope[0])
    return pl.pallas_call(
        body, name=name, grid=(S // tm, ncols // tn),
        in_specs=in_specs, out_specs=pl.BlockSpec((tm, tn), lambda i, j: (i, j)),
        out_shape=jax.ShapeDtypeStruct((S, ncols), out_dtype),
        compiler_params=_cparams(),
    )(*args)


def _out_proj_res(y, wo, xres, name):
    S, W = y.shape
    D = wo.shape[1]
    tm, tn = _row_tile(S, 512), _tile(D, 512)

    def body(a_ref, b_ref, r_ref, o_ref):
        o_ref[...] = r_ref[...] + _dot_nn(a_ref[...], b_ref[...])

    return pl.pallas_call(
        body, name=name, grid=(S // tm, D // tn),
        in_specs=[pl.BlockSpec((tm, W), lambda i, j: (i, 0)), pl.BlockSpec((W, tn), lambda i, j: (0, j)),
                  pl.BlockSpec((tm, tn), lambda i, j: (i, j))],
        out_specs=pl.BlockSpec((tm, tn), lambda i, j: (i, j)),
        out_shape=jax.ShapeDtypeStruct((S, D), F32),
        compiler_params=_cparams(),
    )(y, wo, xres)


def _matmul_nt(parts, wfull, out_rows, name):
    S = parts[0][0].shape[-2]
    tm, tn = _row_tile(S, 512), _tile(out_rows, 512)
    plan, lo = [], 0
    for arr, lead, col0 in parts:
        n_p = arr.shape[-1]
        tk = math.gcd(_tile(n_p, 1024), col0) if col0 else _tile(n_p, 1024)
        steps = n_p // tk
        plan.append((lead, col0 // tk, tk, lo, lo + steps))
        lo += steps
    nk = lo
    npart = len(parts)

    def body(*refs):
        a_refs, w_refs = refs[:npart], refs[npart:2 * npart]
        o_ref, acc_ref = refs[2 * npart], refs[2 * npart + 1]
        k = pl.program_id(2)

        @pl.when(k == 0)
        def _():
            acc_ref[...] = jnp.zeros(acc_ref.shape, F32)
        for p, (_, _, _, lo_p, hi_p) in enumerate(plan):
            @pl.when((k >= lo_p) & (k < hi_p))
            def _(p=p):
                acc_ref[...] += _dot_nt(a_refs[p][...], w_refs[p][...])

        @pl.when(k == nk - 1)
        def _():
            o_ref[...] = acc_ref[...]

    in_specs, args = [], []
    for (arr, lead, col0), (_, cb, tk, lo_p, hi_p) in zip(parts, plan):
        def kk(k, lo_p=lo_p, hi_p=hi_p):
            return jnp.clip(k - lo_p, 0, hi_p - lo_p - 1)
        if lead is None:
            in_specs.append(pl.BlockSpec((tm, tk), lambda i, j, k, kk=kk: (i, kk(k))))
        else:
            in_specs.append(pl.BlockSpec((None, tm, tk), lambda i, j, k, kk=kk, lead=lead: (lead, i, kk(k))))
        args.append(arr)
    for (_, cb, tk, lo_p, hi_p) in plan:
        def kk(k, lo_p=lo_p, hi_p=hi_p):
            return jnp.clip(k - lo_p, 0, hi_p - lo_p - 1)
        in_specs.append(pl.BlockSpec((tn, tk), lambda i, j, k, kk=kk, cb=cb: (j, cb + kk(k))))
        args.append(wfull)
    return pl.pallas_call(
        body, name=name, grid=(S // tm, out_rows // tn, nk),
        in_specs=in_specs, out_specs=pl.BlockSpec((tm, tn), lambda i, j, k: (i, j)),
        out_shape=jax.ShapeDtypeStruct((S, out_rows), F32),
        scratch_shapes=[pltpu.VMEM((tm, tn), F32)],
        compiler_params=_cparams(),
    )(*args)


def _matmul_tn(a, parts, total, name):
    S, M = a.shape
    tm, ts = _tile(M, 512), _row_tile(S, 512)
    out = None
    for idx, (arr, lead, col0) in enumerate(parts):
        n_p = arr.shape[-1]
        tn = math.gcd(_tile(n_p, 512), col0) if col0 else _tile(n_p, 512)
        cb = col0 // tn
        nk = S // ts

        def body(*refs, nk=nk):
            a_ref, b_ref = refs[0], refs[1]
            o_ref, acc_ref = refs[-2], refs[-1]
            k = pl.program_id(2)

            @pl.when(k == 0)
            def _():
                acc_ref[...] = jnp.zeros(acc_ref.shape, F32)
            acc_ref[...] += _dot_tn(a_ref[...], b_ref[...])

            @pl.when(k == nk - 1)
            def _():
                o_ref[...] = acc_ref[...]

        in_specs = [pl.BlockSpec((ts, tm), lambda i, j, k: (k, i))]
        if lead is None:
            in_specs.append(pl.BlockSpec((ts, tn), lambda i, j, k: (k, j)))
        else:
            in_specs.append(pl.BlockSpec((None, ts, tn), lambda i, j, k, lead=lead: (lead, k, j)))
        args = [a, arr]
        aliases = {}
        if out is not None:
            in_specs.append(pl.BlockSpec(memory_space=pl.ANY))
            args.append(out)
            aliases = {2: 0}
        out = pl.pallas_call(
            body, name=f"{name}_{idx}", grid=(M // tm, n_p // tn, nk),
            in_specs=in_specs, out_specs=pl.BlockSpec((tm, tn), lambda i, j, k, cb=cb: (i, cb + j)),
            out_shape=jax.ShapeDtypeStruct((M, total), F32),
            scratch_shapes=[pltpu.VMEM((tm, tn), F32)],
            input_output_aliases=aliases,
            compiler_params=_cparams(),
        )(*args)
    return out


def _log_sigmoid(z):
    e = jnp.exp(-jnp.abs(z))
    return jnp.minimum(z, 0.0) - jnp.where(e < 1e-4, e * (1.0 - 0.5 * e), jnp.log(1.0 + e))


def _fox_gate_fwd(fl, bias):
    S = fl.shape[0]

    def body(f_ref, b_ref, c_ref, ct_ref):
        row = lax.broadcasted_iota(jnp.int32, (8, LANE), 0)

        def step(i, carry):
            r0 = pl.multiple_of(i * 8, 8)
            t = _log_sigmoid(f_ref[pl.ds(r0, 8), :] + b_ref[...])
            for sh in (1, 2, 4):
                t = t + jnp.where(row >= sh, pltpu.roll(t, sh, axis=0), 0.0)
            t = t + carry
            c_ref[pl.ds(r0, 8), :] = t
            return jnp.sum(jnp.where(row == 7, t, 0.0), axis=0, keepdims=True)

        lax.fori_loop(0, S // 8, step, jnp.zeros((1, LANE), F32))
        ct_ref[...] = c_ref[...].T

    vm = pl.BlockSpec(memory_space=pltpu.VMEM)
    return pl.pallas_call(
        body, name="fox_gate_fwd", in_specs=[vm, vm], out_specs=[vm, vm],
        out_shape=[jax.ShapeDtypeStruct((S, LANE), F32), jax.ShapeDtypeStruct((LANE, S), F32)],
        compiler_params=_cparams(),
    )(fl, bias)


def _fox_gate_bwd(fl, bias, dc):
    S = fl.shape[0]

    def body(f_ref, b_ref, d_ref, o_ref, db_ref, acc_ref):
        row = lax.broadcasted_iota(jnp.int32, (8, LANE), 0)
        nt = S // 8

        def step(ii, carry):
            carry_c, carry_b = carry
            r0 = pl.multiple_of((nt - 1 - ii) * 8, 8)
            t = d_ref[pl.ds(r0, 8), :]
            for sh in (1, 2, 4):
                t = t + jnp.where(row < 8 - sh, pltpu.roll(t, 8 - sh, axis=0), 0.0)
            t = t + carry_c
            z = f_ref[pl.ds(r0, 8), :] + b_ref[...]
            dz = t * _sigmoid(-z)
            acc_ref[pl.ds(r0, 8), :] = dz
            first = jnp.sum(jnp.where(row == 0, t, 0.0), axis=0, keepdims=True)
            return first, carry_b + jnp.sum(dz, axis=0, keepdims=True)

        zero = jnp.zeros((1, LANE), F32)
        _, db = lax.fori_loop(0, nt, step, (zero, zero))
        db_ref[...] = db
        o_ref[...] = acc_ref[...].astype(BF16)

    vm = pl.BlockSpec(memory_space=pltpu.VMEM)
    return pl.pallas_call(
        body, name="fox_gate_bwd", in_specs=[vm, vm, vm], out_specs=[vm, vm],
        out_shape=[jax.ShapeDtypeStruct((S, LANE), BF16), jax.ShapeDtypeStruct((1, LANE), F32)],
        scratch_shapes=[pltpu.VMEM((S, LANE), F32)],
        compiler_params=_cparams(),
    )(fl, bias, dc)


def _fox_fwd(qkv, gate, c, ct, H):
    S = qkv.shape[0]
    W = H * HEAD_DIM
    HP = H // 2
    tq = _row_tile(S, 512)
    nq = S // tq
    wb = W // LANE
    scale = HEAD_DIM ** -0.5

    def body(q_ref, k_ref, v_ref, g_ref, c_ref, ct_ref, y_ref, o_ref, a_ref, m_sc, l_sc, acc_sc):
        hp, qi = pl.program_id(0), pl.program_id(1)
        lane_q = lax.broadcasted_iota(jnp.int32, (tq, LANE), 1)
        lo_q = lane_q < HEAD_DIM
        rows = lax.broadcasted_iota(jnp.int32, (tq, tq), 0)
        cols = lax.broadcasted_iota(jnp.int32, (tq, tq), 1)
        q = q_ref[...] * jnp.asarray(scale, BF16)
        crow = c_ref[pl.ds(pl.multiple_of(qi * tq, tq), tq), :]
        qm = [jnp.where(lo_q, q, jnp.zeros_like(q)), jnp.where(lo_q, jnp.zeros_like(q), q)]
        ctq = [jnp.sum(jnp.where(lane_q == 2 * hp + e, crow, 0.0), axis=1, keepdims=True) for e in range(2)]
        m_sc[...] = jnp.full(m_sc.shape, NEG_INF, F32)
        l_sc[...] = jnp.zeros(l_sc.shape, F32)
        acc_sc[...] = jnp.zeros(acc_sc.shape, F32)

        def step(j, masked):
            k0 = pl.multiple_of(j * tq, tq)
            kblk = k_ref[pl.ds(k0, tq), :]
            vblk = v_ref[pl.ds(k0, tq), :]
            for e in range(2):
                cs = ct_ref[pl.ds(2 * hp + e, 1), pl.ds(k0, tq)]
                s = _dot_nt(qm[e], kblk) + (ctq[e] - cs)
                if masked:
                    s = jnp.where(rows >= cols, s, NEG_INF)
                m_prev = m_sc[e]
                m_new = jnp.maximum(m_prev, jnp.max(s, axis=1, keepdims=True))
                alpha = jnp.exp(m_prev - m_new)
                p = jnp.exp(s - m_new)
                l_sc[e] = alpha * l_sc[e] + jnp.sum(p, axis=1, keepdims=True)
                acc_sc[e] = alpha * acc_sc[e] + _dot_nn(p.astype(BF16), vblk)
                m_sc[e] = m_new

        def loop_body(j, carry):
            step(j, False)
            return carry

        lax.fori_loop(0, qi, loop_body, 0)
        step(qi, True)
        inv = [1.0 / l_sc[e] for e in range(2)]
        o = jnp.where(lo_q, acc_sc[0] * inv[0], acc_sc[1] * inv[1])
        a = [ctq[e] - (m_sc[e] + jnp.log(l_sc[e])) for e in range(2)]
        g = g_ref[...]
        y_ref[...] = (o * (g * _sigmoid(g))).astype(BF16)
        o_ref[...] = o.astype(BF16)
        a_ref[0] = jnp.where(lo_q, a[0], a[1])

    return pl.pallas_call(
        body, name="fox_attn_fwd", grid=(HP, nq),
        in_specs=[pl.BlockSpec((tq, LANE), lambda h, i: (i, h)),
                  pl.BlockSpec((S, LANE), lambda h, i: (0, wb + h)),
                  pl.BlockSpec((S, LANE), lambda h, i: (0, 2 * wb + h)),
                  pl.BlockSpec((tq, LANE), lambda h, i: (i, h)),
                  pl.BlockSpec((S, LANE), lambda h, i: (0, 0)),
                  pl.BlockSpec((LANE, S), lambda h, i: (0, 0))],
        out_specs=[pl.BlockSpec((tq, LANE), lambda h, i: (i, h)),
                   pl.BlockSpec((tq, LANE), lambda h, i: (i, h)),
                   pl.BlockSpec((1, tq, LANE), lambda h, i: (h, i, 0))],
        out_shape=[jax.ShapeDtypeStruct((S, W), BF16), jax.ShapeDtypeStruct((S, W), BF16),
                   jax.ShapeDtypeStruct((HP, S, LANE), F32)],
        scratch_shapes=[pltpu.VMEM((2, tq, 1), F32), pltpu.VMEM((2, tq, 1), F32), pltpu.VMEM((2, tq, LANE), F32)],
        compiler_params=_cparams(),
    )(qkv, qkv, qkv, gate, c, ct)


def _fox_bwd(qkv, dy, gate, o, a, ct, H):
    S = qkv.shape[0]
    W = H * HEAD_DIM
    HP = H // 2
    tq = _row_tile(S, 512)
    nq = S // tq
    wb = W // LANE
    scale = HEAD_DIM ** -0.5

    def body(q_ref, k_ref, v_ref, dy_ref, g_ref, o_ref, a_ref, ct_ref, out_ref, dcs_ref, dcr_ref,
             do_sc, delta_sc, dq_sc, dk_sc, dv_sc, drow_sc):
        hp, kj = pl.program_id(0), pl.program_id(1)
        lane = lax.broadcasted_iota(jnp.int32, (tq, LANE), 1)
        lo = lane < HEAD_DIM
        rows = lax.broadcasted_iota(jnp.int32, (tq, tq), 0)
        cols = lax.broadcasted_iota(jnp.int32, (tq, tq), 1)

        @pl.when(kj == 0)
        def _():
            def chunk(i, carry):
                r0 = pl.multiple_of(i * tq, tq)
                dyv = dy_ref[pl.ds(r0, tq), :]
                g = g_ref[pl.ds(r0, tq), :]
                ov = o_ref[pl.ds(r0, tq), :].astype(F32)
                sg = _sigmoid(g)
                dob = (dyv * (g * sg)).astype(BF16)
                out_ref[3, pl.ds(r0, tq), :] = (dyv * ov * (sg * (1.0 + g * (1.0 - sg)))).astype(BF16)
                do_sc[pl.ds(r0, tq), :] = dob
                prod = dob.astype(F32) * ov
                d0 = jnp.sum(jnp.where(lo, prod, 0.0), axis=1, keepdims=True)
                d1 = jnp.sum(jnp.where(lo, 0.0, prod), axis=1, keepdims=True)
                delta_sc[pl.ds(r0, tq), :] = jnp.where(lo, d0, d1)
                dq_sc[pl.ds(r0, tq), :] = jnp.zeros((tq, LANE), F32)
                drow_sc[pl.ds(r0, tq), :] = jnp.zeros((tq, LANE), F32)
                return carry
            lax.fori_loop(0, nq, chunk, 0)

        @pl.when((kj == 0) & (hp == 0))
        def _():
            dcr_ref[...] = jnp.zeros(dcr_ref.shape, F32)

        kblk = k_ref[...]
        vblk = v_ref[...]
        zb = jnp.zeros_like(kblk)
        km = [jnp.where(lo, kblk, zb), jnp.where(lo, zb, kblk)]
        vm = [jnp.where(lo, vblk, zb), jnp.where(lo, zb, vblk)]
        dk_sc[...] = jnp.zeros(dk_sc.shape, F32)
        dv_sc[...] = jnp.zeros(dv_sc.shape, F32)
        dcs_ref[...] = jnp.zeros(dcs_ref.shape, F32)

        def step(i, masked):
            r0 = pl.multiple_of(i * tq, tq)
            qt = q_ref[pl.ds(r0, tq), :] * jnp.asarray(scale, BF16)
            dot = do_sc[pl.ds(r0, tq), :]
            at = a_ref[0, pl.ds(r0, tq), :]
            dl = delta_sc[pl.ds(r0, tq), :]
            dq_new, dk_new, dv_new, dr_new = [], [], [], []
            for e in range(2):
                sel = lo if e == 0 else jnp.logical_not(lo)
                a_col = jnp.max(jnp.where(sel, at, -jnp.inf), axis=1, keepdims=True)
                d_col = jnp.max(jnp.where(sel, dl, -jnp.inf), axis=1, keepdims=True)
                cs = ct_ref[pl.ds(2 * hp + e, 1), :]
                s = _dot_nt(qt, km[e]) + (a_col - cs)
                p = jnp.exp(s)
                if masked:
                    p = jnp.where(rows >= cols, p, 0.0)
                dp = _dot_nt(dot, vm[e])
                ds = p * (dp - d_col)
                pb, dsb = p.astype(BF16), ds.astype(BF16)
                dv_new.append(_dot_tn(pb, dot))
                dk_new.append(_dot_tn(dsb, qt))
                dq_new.append(_dot_nn(dsb, kblk))
                dcs_ref[0, pl.ds(e, 1), :] += jnp.sum(ds, axis=0, keepdims=True)
                dr_new.append(jnp.sum(ds, axis=1, keepdims=True))
            drow_sc[pl.ds(r0, tq), :] += jnp.where(lo, dr_new[0], dr_new[1])
            dv_sc[...] += jnp.where(lo, dv_new[0], dv_new[1])
            dk_sc[...] += jnp.where(lo, dk_new[0], dk_new[1])
            dq_sc[pl.ds(r0, tq), :] += jnp.where(lo, dq_new[0], dq_new[1])

        step(kj, True)

        def loop_body(i, carry):
            step(i, False)
            return carry

        lax.fori_loop(kj + 1, nq, loop_body, 0)
        k0 = pl.multiple_of(kj * tq, tq)
        out_ref[1, pl.ds(k0, tq), :] = dk_sc[...].astype(BF16)
        out_ref[2, pl.ds(k0, tq), :] = dv_sc[...].astype(BF16)

        @pl.when(kj == nq - 1)
        def _():
            out_ref[0] = (dq_sc[...] * scale).astype(BF16)

            def chunk(i, carry):
                r0 = pl.multiple_of(i * tq, tq)
                dr = drow_sc[pl.ds(r0, tq), :]
                acc = dcr_ref[pl.ds(r0, tq), :]
                for e in range(2):
                    sel = lo if e == 0 else jnp.logical_not(lo)
                    col = jnp.max(jnp.where(sel, dr, -jnp.inf), axis=1, keepdims=True)
                    acc = jnp.where(lane == 2 * hp + e, col, acc)
                dcr_ref[pl.ds(r0, tq), :] = acc
                return carry
            lax.fori_loop(0, nq, chunk, 0)

    full = lambda cb: pl.BlockSpec((S, LANE), lambda h, j, cb=cb: (0, cb + h))
    return pl.pallas_call(
        body, name="fox_attn_bwd", grid=(HP, nq),
        in_specs=[full(0),
                  pl.BlockSpec((tq, LANE), lambda h, j: (j, wb + h)),
                  pl.BlockSpec((tq, LANE), lambda h, j: (j, 2 * wb + h)),
                  full(0), full(0), full(0),
                  pl.BlockSpec((1, S, LANE), lambda h, j: (h, 0, 0)),
                  pl.BlockSpec((LANE, tq), lambda h, j: (0, j))],
        out_specs=[pl.BlockSpec((4, S, LANE), lambda h, j: (0, 0, h)),
                   pl.BlockSpec((1, 8, tq), lambda h, j: (h, 0, j)),
                   pl.BlockSpec((S, LANE), lambda h, j: (0, 0))],
        out_shape=[jax.ShapeDtypeStruct((4, S, W), BF16), jax.ShapeDtypeStruct((HP, 8, S), F32),
                   jax.ShapeDtypeStruct((S, LANE), F32)],
        scratch_shapes=[pltpu.VMEM((S, LANE), BF16), pltpu.VMEM((S, LANE), F32), pltpu.VMEM((S, LANE), F32),
                        pltpu.VMEM((tq, LANE), F32), pltpu.VMEM((tq, LANE), F32), pltpu.VMEM((S, LANE), F32)],
        compiler_params=_cparams(),
    )(qkv, qkv, qkv, dy, gate, o, a, ct)


def _swa_pick(blk, half, lane):
    b = blk.astype(F32)
    r = pltpu.roll(b, HEAD_DIM, axis=1)
    return jnp.where(jnp.logical_xor(lane < HEAD_DIM, half == 1), b, r).astype(BF16)


def _swa_stack(t, lane, G):
    pieces = []
    z = jnp.zeros((SWA_BLOCK, LANE), t.dtype)
    for j in range(G // 2):
        tile = t[:, LANE * j:LANE * (j + 1)]
        pieces += [jnp.where(lane < HEAD_DIM, tile, z), jnp.where(lane < HEAD_DIM, z, tile)]
    return jnp.concatenate(pieces, axis=0)


def _swa_unstack(st, lane, G):
    tiles = []
    for j in range(G // 2):
        a = st[2 * j * SWA_BLOCK:(2 * j + 1) * SWA_BLOCK]
        b = st[(2 * j + 1) * SWA_BLOCK:(2 * j + 2) * SWA_BLOCK]
        tiles.append(jnp.where(lane < HEAD_DIM, a, b))
    return jnp.concatenate(tiles, axis=1)


def _swa_scores(q_ref, kp_ref, kc_ref, vp_ref, vc_ref, sink_ref, kvh, n, G):
    R = G * SWA_BLOCK
    lane = lax.broadcasted_iota(jnp.int32, (SWA_BLOCK, LANE), 1)
    half = kvh % 2
    kk = jnp.concatenate([_swa_pick(kp_ref[...], half, lane), _swa_pick(kc_ref[...], half, lane)], axis=0)
    vv = jnp.concatenate([_swa_pick(vp_ref[...], half, lane), _swa_pick(vc_ref[...], half, lane)], axis=0)
    qstack = _swa_stack(q_ref[...], lane, G) * jnp.asarray(HEAD_DIM ** -0.5, BF16)
    s = _dot_nt(qstack, kk)
    t_loc = lax.broadcasted_iota(jnp.int32, (R, 2 * SWA_BLOCK), 0) & (SWA_BLOCK - 1)
    j_loc = lax.broadcasted_iota(jnp.int32, (R, 2 * SWA_BLOCK), 1)
    diff = t_loc + SWA_BLOCK - j_loc
    mask = (diff >= 0) & (diff < SWA_BLOCK) & ((n > 0) | (j_loc >= SWA_BLOCK))
    s = jnp.where(mask, s, NEG_INF)
    srow = sink_ref[...]
    lane1 = lax.broadcasted_iota(jnp.int32, (1, LANE), 1)
    sink = jnp.concatenate(
        [jnp.broadcast_to(jnp.sum(jnp.where(lane1 == kvh * G + g, srow, 0.0), axis=1, keepdims=True), (SWA_BLOCK, 1))
         for g in range(G)], axis=0)
    m = jnp.maximum(jnp.max(s, axis=1, keepdims=True), sink)
    e = jnp.exp(s - m)
    es = jnp.exp(sink - m)
    den = jnp.sum(e, axis=1, keepdims=True) + es
    return qstack, kk, vv, e / den, es / den, lane


def _swa_fwd(qkv, gate, sinks, HQ, HKV):
    S = qkv.shape[0]
    G = HQ // HKV
    WQ, KVW = HQ * HEAD_DIM, HKV * HEAD_DIM
    nb = S // SWA_BLOCK
    GW = G * HEAD_DIM
    kb, vb = WQ // LANE, (WQ + KVW) // LANE

    def body(q_ref, kp_ref, kc_ref, vp_ref, vc_ref, g_ref, sink_ref, y_ref, o_ref):
        kvh, n = pl.program_id(0), pl.program_id(1)
        _, _, vv, p, _, lane = _swa_scores(q_ref, kp_ref, kc_ref, vp_ref, vc_ref, sink_ref, kvh, n, G)
        o = _swa_unstack(_dot_nn(p.astype(BF16), vv), lane, G)
        g = g_ref[...]
        y_ref[...] = (o * (g * _sigmoid(g))).astype(BF16)
        o_ref[...] = o.astype(BF16)

    blk = lambda cb, prev: pl.BlockSpec(
        (SWA_BLOCK, LANE), lambda h, n, cb=cb, prev=prev: (jnp.maximum(n - prev, 0), cb + h // 2))
    qspec = pl.BlockSpec((SWA_BLOCK, GW), lambda h, n: (n, h))
    return pl.pallas_call(
        body, name="swa_attn_fwd", grid=(HKV, nb),
        in_specs=[qspec, blk(kb, 1), blk(kb, 0), blk(vb, 1), blk(vb, 0), qspec,
                  pl.BlockSpec((1, LANE), lambda h, n: (0, 0))],
        out_specs=[qspec, qspec],
        out_shape=[jax.ShapeDtypeStruct((S, WQ), BF16), jax.ShapeDtypeStruct((S, WQ), BF16)],
        compiler_params=_cparams(),
    )(qkv, qkv, qkv, qkv, qkv, gate, sinks)


def _swa_bwd(qkv, dy, gate, o, sinks, tables, HQ, HKV):
    S = qkv.shape[0]
    G = HQ // HKV
    WQ, KVW = HQ * HEAD_DIM, HKV * HEAD_DIM
    nb = S // SWA_BLOCK
    GW = G * HEAD_DIM
    R = G * SWA_BLOCK
    kb, vb = WQ // LANE, (WQ + KVW) // LANE
    scale = HEAD_DIM ** -0.5
    assert G == 8

    def body(q_ref, kp_ref, kc_ref, vp_ref, vc_ref, dy_ref, g_ref, o_ref, sink_ref, t_ref,
             dqg_ref, dkv_ref, dsink_ref, carry_sc):
        kvh, n = pl.program_id(0), pl.program_id(1)

        @pl.when(n == 0)
        def _():
            carry_sc[...] = jnp.zeros(carry_sc.shape, F32)
            dsink_ref[...] = jnp.zeros(dsink_ref.shape, F32)

        @pl.when(n < nb)
        def _():
            qstack, kk, vv, p, psink, lane = _swa_scores(q_ref, kp_ref, kc_ref, vp_ref, vc_ref, sink_ref, kvh, n, G)
            dyv, g, ov = dy_ref[...], g_ref[...], o_ref[...].astype(F32)
            sg = _sigmoid(g)
            dob = (dyv * (g * sg)).astype(BF16)
            dqg_ref[1] = (dyv * ov * (sg * (1.0 + g * (1.0 - sg)))).astype(BF16)
            prod = dob.astype(F32) * ov
            dparts = []
            for j in range(G // 2):
                tile = prod[:, LANE * j:LANE * (j + 1)]
                dparts += [jnp.sum(jnp.where(lane < HEAD_DIM, tile, 0.0), axis=1, keepdims=True),
                           jnp.sum(jnp.where(lane < HEAD_DIM, 0.0, tile), axis=1, keepdims=True)]
            delta = jnp.concatenate(dparts, axis=0)
            dostack = _swa_stack(dob, lane, G)
            dp = _dot_nt(dostack, vv)
            ds = p * (dp - delta)
            dsb, pb = ds.astype(BF16), p.astype(BF16)
            dq = _swa_unstack(_dot_nn(dsb, kk), lane, G) * scale
            dq = (dq * t_ref[0] + pltpu.roll(dq * t_ref[1], ROT_DIM // 2, axis=1)
                  + pltpu.roll(dq * t_ref[2], GW - ROT_DIM // 2, axis=1))
            dqg_ref[0] = dq.astype(BF16)
            dkk = _dot_tn(dsb, qstack)
            dvv = _dot_tn(pb, dostack)
            dkk = dkk + pltpu.roll(dkk, HEAD_DIM, axis=1)
            dvv = dvv + pltpu.roll(dvv, HEAD_DIM, axis=1)
            lane2 = lax.broadcasted_iota(jnp.int32, (2 * SWA_BLOCK, LANE), 1)
            comb = jnp.where(lane2 < HEAD_DIM, dkk, dvv)
            dkv_ref[0] = carry_sc[...] + comb[:SWA_BLOCK]
            carry_sc[...] = comb[SWA_BLOCK:]
            sk = psink * delta
            rows = [jnp.broadcast_to(-jnp.sum(sk[g_ * SWA_BLOCK:(g_ + 1) * SWA_BLOCK], axis=0, keepdims=True), (1, LANE))
                    for g_ in range(G)]
            dsink_ref[0] += jnp.concatenate(rows, axis=0)

        @pl.when(n == nb)
        def _():
            dkv_ref[0] = carry_sc[...]

    cl = lambda n: jnp.minimum(n, nb - 1)
    blk = lambda cb, prev: pl.BlockSpec(
        (SWA_BLOCK, LANE), lambda h, n, cb=cb, prev=prev: (jnp.maximum(cl(n) - prev, 0), cb + h // 2))
    qspec = pl.BlockSpec((SWA_BLOCK, GW), lambda h, n: (cl(n), h))
    return pl.pallas_call(
        body, name="swa_attn_bwd", grid=(HKV, nb + 1),
        in_specs=[qspec, blk(kb, 1), blk(kb, 0), blk(vb, 1), blk(vb, 0), qspec, qspec, qspec,
                  pl.BlockSpec((1, LANE), lambda h, n: (0, 0)),
                  pl.BlockSpec((3, SWA_BLOCK, GW), lambda h, n: (0, cl(n), 0))],
        out_specs=[pl.BlockSpec((2, SWA_BLOCK, GW), lambda h, n: (0, cl(n), h)),
                   pl.BlockSpec((1, SWA_BLOCK, LANE), lambda h, n: (h, jnp.maximum(n - 1, 0), 0)),
                   pl.BlockSpec((1, 8, LANE), lambda h, n: (h, 0, 0))],
        out_shape=[jax.ShapeDtypeStruct((2, S, WQ), BF16), jax.ShapeDtypeStruct((HKV, S, LANE), F32),
                   jax.ShapeDtypeStruct((HKV, 8, LANE), F32)],
        scratch_shapes=[pltpu.VMEM((SWA_BLOCK, LANE), F32)],
        compiler_params=_cparams(),
    )(qkv, qkv, qkv, qkv, qkv, dy, gate, o, sinks, tables)


def _swa_dkv_finish(dkv, tables):
    HKV, S, _ = dkv.shape
    KVW = HKV * HEAD_DIM
    tm = _row_tile(S, 512)
    npair = HKV // 2

    def body(d_ref, t_ref, o_ref):
        lane = lax.broadcasted_iota(jnp.int32, (tm, LANE), 1)
        lo = lane < HEAD_DIM
        for p in range(npair):
            a, b = d_ref[2 * p], d_ref[2 * p + 1]
            tk = jnp.where(lo, a, pltpu.roll(b, HEAD_DIM, axis=1))
            tv = jnp.where(lo, pltpu.roll(a, HEAD_DIM, axis=1), b)
            tk = (tk * t_ref[0] + pltpu.roll(tk * t_ref[1], ROT_DIM // 2, axis=1)
                  + pltpu.roll(tk * t_ref[2], LANE - ROT_DIM // 2, axis=1))
            o_ref[:, LANE * p:LANE * (p + 1)] = tk.astype(BF16)
            o_ref[:, KVW + LANE * p:KVW + LANE * (p + 1)] = tv.astype(BF16)

    return pl.pallas_call(
        body, name="swa_dkv_finish", grid=(S // tm,),
        in_specs=[pl.BlockSpec((HKV, tm, LANE), lambda i: (0, i, 0)), pl.BlockSpec((3, tm, LANE), lambda i: (0, i, 0))],
        out_specs=pl.BlockSpec((tm, 2 * KVW), lambda i: (i, 0)),
        out_shape=jax.ShapeDtypeStruct((S, 2 * KVW), BF16),
        compiler_params=_cparams(),
    )(dkv, tables)


def _rope_tables(S, width):
    half = ROT_DIM // 2
    pos = jnp.arange(S, dtype=F32)
    inv_freq = ROPE_THETA ** (-jnp.arange(half, dtype=F32) / half)
    ang = pos[:, None] * inv_freq[None, :]
    cos, sin = jnp.cos(ang), jnp.sin(ang)
    one = jnp.ones((S, HEAD_DIM - ROT_DIM), F32)
    zero = jnp.zeros((S, HEAD_DIM - ROT_DIM), F32)
    zh = jnp.zeros((S, half), F32)
    t0 = jnp.concatenate([cos, cos, one], axis=1)
    t1 = jnp.concatenate([-sin, zh, zero], axis=1)
    t2 = jnp.concatenate([zh, sin, zero], axis=1)
    return jnp.stack([jnp.tile(t, (1, width // HEAD_DIM)) for t in (t0, t1, t2)])


def _pad_rows(v, row, total_rows=8):
    return jnp.pad(v, ((row, total_rows - row - v.shape[0]), (0, 0)))


def _pad_lanes(v, off, width):
    return jnp.pad(v, ((0, 0), (off, width - off - v.shape[1])))


def kernel(x, norm_g, fox_w_in, fox_b_f, fox_w_out, swa_w_in, swa_sinks, swa_w_out, final_g, loss_target, m_norm_g, m_fox_w_in, m_fox_b_f, m_fox_w_out, m_swa_w_in, m_swa_sinks, m_swa_w_out, m_final_g, v_norm_g, v_fox_w_in, v_fox_b_f, v_fox_w_out, v_swa_w_in, v_swa_sinks, v_swa_w_out, v_final_g):
    S, D = x.shape[1], x.shape[2]
    H = fox_b_f.shape[1]
    W = H * HEAD_DIM
    wf = fox_w_in.shape[2]
    ws = swa_w_in.shape[2]
    HQ = swa_sinks.shape[1]
    WQ = HQ * HEAD_DIM
    KVW = (ws * N_DEV - 2 * WQ) // 2
    HKV = KVW // HEAD_DIM
    rows_o = fox_w_out.shape[1]
    assert wf * N_DEV == 4 * W + H and rows_o * N_DEV == W and H <= LANE and HQ <= LANE
    me = _my_index()

    _, sw_f, np_f = _slab_geom(wf)
    _, sw_s, np_s = _slab_geom(ws)

    def slab(w2d, w, sw):
        off = (w * me) % LANE
        return lax.dynamic_update_slice(jnp.zeros((w2d.shape[0], sw), BF16), w2d.astype(BF16), (0, off))

    fi_all, si_all, fo_all, so_all = _all_gather([
        slab(fox_w_in[0], wf, sw_f), slab(swa_w_in[0], ws, sw_s),
        fox_w_out[0].astype(BF16), swa_w_out[0].astype(BF16)])
    w_fi = _assemble(fi_all, wf)
    w_si = _assemble(si_all, ws)
    w_fo = fo_all.reshape(W, D)
    w_so = so_all.reshape(WQ, D)

    x0 = x[0]
    g0, g1, gf = norm_g[0:1], norm_g[1:2], final_g[None, :]
    bias = _pad_lanes(fox_b_f, 0, LANE)
    sinks = _pad_lanes(swa_sinks, 0, LANE)
    tab_q = _rope_tables(S, 8 * HEAD_DIM)
    tab_k = tab_q[:, :, :LANE]

    h0 = _rmsnorm_fwd(x0, g0, "rmsnorm0")
    qkv0 = _proj(h0, w_fi, 0, 3 * W, BF16, "fox_in_qkv")
    gate0 = _proj(h0, w_fi, 3 * W, W, F32, "fox_in_gate")
    fl = _proj(h0, w_fi, 4 * W, LANE, F32, "fox_in_f")
    c, ct = _fox_gate_fwd(fl, bias)
    y0, o0, a0 = _fox_fwd(qkv0, gate0, c, ct, H)
    x1 = _out_proj_res(y0, w_fo, x0, "fox_out")

    h1 = _rmsnorm_fwd(x1, g1, "rmsnorm1")
    qkv1 = _proj(h1, w_si, 0, WQ + 2 * KVW, BF16, "swa_in_qkv", rope=(tab_q[:, :, :_tile(math.gcd(WQ, KVW), 512)], WQ + KVW))
    gate1 = _proj(h1, w_si, WQ + 2 * KVW, WQ, F32, "swa_in_gate")
    y1, o1 = _swa_fwd(qkv1, gate1, sinks, HQ, HKV)
    x2 = _out_proj_res(y1, w_so, x1, "swa_out")

    dx2, dx2b, dgf, loss_p = _loss_head(x2, loss_target[0], gf)

    dy1 = _matmul_nt([(dx2b, None, 0)], w_so, WQ, "swa_out_bwd")
    g_so = _matmul_tn(y1, [(dx2b, None, 0)], D, "swa_out_wgrad")
    dqg1, dkv1, dsink = _swa_bwd(qkv1, dy1, gate1, o1, sinks, tab_q, HQ, HKV)
    dkv1f = _swa_dkv_finish(dkv1, tab_k)
    parts1 = [(dqg1, 0, 0), (dkv1f, None, WQ), (dqg1, 1, WQ + 2 * KVW)]
    g_si = _matmul_tn(h1, parts1, np_s, "swa_in_wgrad")
    dh1 = _matmul_nt(parts1, w_si, D, "swa_in_bwd")
    dx1, dx1b, dg1 = _rmsnorm_bwd(dh1, x1, g1, dx2, "rmsnorm1_bwd")

    dy0 = _matmul_nt([(dx1b, None, 0)], w_fo, W, "fox_out_bwd")
    g_fo = _matmul_tn(y0, [(dx1b, None, 0)], D, "fox_out_wgrad")
    dqkvg0, dcs, dcr = _fox_bwd(qkv0, dy0, gate0, o0, a0, ct, H)
    dcs_t = _pad_lanes(dcs[:, :2, :].reshape(H, S).T, 0, LANE)
    dfl, dbf = _fox_gate_bwd(fl, bias, dcr - dcs_t)
    parts0 = [(dqkvg0, p, p * W) for p in range(4)] + [(dfl, None, 4 * W)]
    g_fi = _matmul_tn(h0, parts0, np_f, "fox_in_wgrad")
    dh0 = _matmul_nt(parts0, w_fi, D, "fox_in_bwd")
    dx0, _, dg0 = _rmsnorm_bwd(dh0, x0, g0, dx1, "rmsnorm0_bwd")

    specs = [("col", wf), ("col", ws), ("row", rows_o), ("row", rows_o)]
    own, recv = _reduce_scatter_stage1([g_fi, g_si, g_fo, g_so], specs)
    sums = [_pair_sum(o_, r_) for o_, r_ in zip(own, recv)]
    recv2 = _reduce_scatter_stage2([s_[1] for s_ in sums])
    red = [_final_sum(s_[0], r_) for s_, r_ in zip(sums, recv2)]
    gw_fi = lax.dynamic_slice(red[0], (0, (wf * me) % LANE), (D, wf))
    gw_si = lax.dynamic_slice(red[1], (0, (ws * me) % LANE), (D, ws))
    gw_fo, gw_so = red[2], red[3]

    P = D
    dsink_v = dsink[:, :, 0].reshape(1, HQ)
    row3 = _pad_lanes(dbf[:, :H], 0, P) + _pad_lanes(dsink_v, LANE, P) + _pad_lanes(loss_p[:, :1], 2 * LANE, P)
    pack = _pad_rows(dg0, 0) + _pad_rows(dg1, 1) + _pad_rows(dgf, 2) + _pad_rows(row3, 3)
    tot = _all_reduce_small(pack)
    loss = tot[3, 2 * LANE]
    g_norm = tot[0:2]
    g_final = tot[2]
    g_bf = tot[3:4, 0:H]
    g_sinks = tot[3:4, LANE:LANE + HQ]

    def small_pack(ng, fg, bf, sk):
        r3 = _pad_lanes(bf, 0, P) + _pad_lanes(sk, LANE, P)
        return _pad_rows(ng, 0) + _pad_rows(fg[None, :], 2) + _pad_rows(r3, 3)

    sd, sm, sv = _adamw(small_pack(norm_g, final_g, fox_b_f, swa_sinks), tot,
                        small_pack(m_norm_g, m_final_g, m_fox_b_f, m_swa_sinks),
                        small_pack(v_norm_g, v_final_g, v_fox_b_f, v_swa_sinks), "adamw_small")

    def unpack(t):
        return t[0:2], t[3:4, 0:H], t[3:4, LANE:LANE + HQ], t[2]

    d_fi, m_fi, v_fi = _adamw(fox_w_in[0], gw_fi, m_fox_w_in[0], v_fox_w_in[0], "adamw_fox_in")
    d_fo, m_fo, v_fo = _adamw(fox_w_out[0], gw_fo, m_fox_w_out[0], v_fox_w_out[0], "adamw_fox_out")
    d_si, m_si, v_si = _adamw(swa_w_in[0], gw_si, m_swa_w_in[0], v_swa_w_in[0], "adamw_swa_in")
    d_so, m_so, v_so = _adamw(swa_w_out[0], gw_so, m_swa_w_out[0], v_swa_w_out[0], "adamw_swa_out")

    def group(small, fi, fo, si, so):
        ng, bf, sk, fg = unpack(small)
        return (ng, fi[None], bf, fo[None], si[None], sk, so[None], fg)

    grads = (g_norm, gw_fi[None], g_bf, gw_fo[None], gw_si[None], g_sinks, gw_so[None], g_final)
    return (loss, dx0[None], *grads, *group(sd, d_fi, d_fo, d_si, d_so),
            *group(sm, m_fi, m_fo, m_si, m_so), *group(sv, v_fi, v_fo, v_si, v_so))
```

```python
import functools
import math

import jax
import jax.numpy as jnp
from jax import lax
from jax.experimental import pallas as pl
from jax.experimental.pallas import tpu as pltpu

F32 = jnp.float32
BF16 = jnp.bfloat16
MESH = pl.DeviceIdType.MESH

N_DEV = 8
LANE = 128
HEAD_DIM = 64
SWA_BLOCK = 128
NEG_INF = -1e30
RMS_EPS = 1e-6
ROPE_THETA = 500000.0
ROT_DIM = HEAD_DIM // 4
ADAM_LR, ADAM_B1, ADAM_B2, ADAM_EPS, ADAM_WD, ADAM_STEP = 0.001, 0.9, 0.999, 1e-08, 0.01, 10
VMEM_LIMIT = 56 * 1024 * 1024
MM_TILE = 1024


def _cparams(**kw):
    return pltpu.CompilerParams(vmem_limit_bytes=VMEM_LIMIT, **kw)


def _tile(n, cap):
    if n <= cap:
        return n
    t = (cap // LANE) * LANE
    while t > LANE and n % t:
        t -= LANE
    assert n % t == 0, (n, cap)
    return t


def _row_tile(n, cap):
    t = min(n, cap)
    while n % t:
        t //= 2
    return t


def _dot_nn(a, b):
    return jnp.dot(a, b, preferred_element_type=F32)


def _dot_nt(a, b):
    return lax.dot_general(a, b, (((1,), (1,)), ((), ())), preferred_element_type=F32)


def _dot_tn(a, b):
    return lax.dot_general(a, b, (((0,), (0,)), ((), ())), preferred_element_type=F32)


def _sigmoid(g):
    return 1.0 / (1.0 + jnp.exp(-g))


def _slab_geom(w):
    starts = [w * i for i in range(N_DEV)]
    aligned = [LANE * (s // LANE) for s in starts]
    offs = [s - a for s, a in zip(starts, aligned)]
    sw = LANE * (-(-(max(offs) + w) // LANE))
    return aligned, sw, aligned[-1] + sw


def _my_index():
    return 4 * lax.axis_index("x") + 2 * lax.axis_index("y") + lax.axis_index("c")


def _all_gather(arrs):
    n = len(arrs)

    def body(*refs):
        ins, outs = refs[:n], refs[n:2 * n]
        send_sems, recv_sems, local_sems = refs[2 * n:]
        x, y, c = lax.axis_index("x"), lax.axis_index("y"), lax.axis_index("c")
        me, sib = (x, y, c), (x, y, 1 - c)
        chips = [(1 - x, y), (x, 1 - y), (1 - x, 1 - y)]

        def idx(px, py, pc):
            return 4 * px + 2 * py + pc

        def copy(a, k, block, to, src=None):
            dst = outs[a].at[idx(*block)]
            return pltpu.make_async_remote_copy(
                src_ref=dst if src is None else src, dst_ref=dst,
                send_sem=send_sems.at[a, k], recv_sem=recv_sems.at[a, k],
                device_id=to, device_id_type=MESH)

        mine = [pltpu.make_async_copy(ins[a], outs[a].at[idx(*me)], local_sems.at[a]) for a in range(n)]
        for m in mine:
            m.start()
        first = []
        for a in range(n):
            first.append(copy(a, 0, me, sib, src=ins[a]))
            for j, chip in enumerate(chips):
                first.append(copy(a, 1 + j, me, (*chip, c), src=ins[a]))
        for cp in first:
            cp.start()
        passed = []
        for j, chip in enumerate(chips):
            for a in range(n):
                copy(a, 1 + j, (*chip, c), me).wait_recv()
                p = copy(a, 4 + j, (*chip, c), sib)
                p.start()
                passed.append(p)
        for a in range(n):
            copy(a, 0, sib, me).wait_recv()
        for j, chip in enumerate(chips):
            for a in range(n):
                copy(a, 4 + j, (*chip, 1 - c), me).wait_recv()
        for cp in first + passed:
            cp.wait_send()
        for m in mine:
            m.wait()

    any_spec = pl.BlockSpec(memory_space=pl.ANY)
    return pl.pallas_call(
        body, name="weights_all_gather",
        out_shape=[jax.ShapeDtypeStruct((N_DEV,) + a.shape, a.dtype) for a in arrs],
        in_specs=[any_spec] * n, out_specs=[any_spec] * n,
        scratch_shapes=[pltpu.SemaphoreType.DMA((n, 7)), pltpu.SemaphoreType.DMA((n, 7)),
                        pltpu.SemaphoreType.DMA((n,))],
    )(*arrs)


def _rel_chip(r):
    x, y = lax.axis_index("x"), lax.axis_index("y")
    return (x ^ (r >> 1), y ^ (r & 1))


def _rs_windows(specs):
    def window(ref, spec, blk):
        kind, n = spec
        if kind == "col":
            _, sw, _ = _slab_geom(n)
            return ref.at[pl.ds((n * blk) // LANE, sw // LANE)]
        start = pl.multiple_of(n * blk, n)
        return ref.at[pl.ds(start, n), :]
    return window


def _reduce_scatter_stage1(grads, specs):
    n = len(grads)
    window = _rs_windows(specs)

    def blk_shape(g, spec):
        kind, w = spec
        return (_slab_geom(w)[1] // LANE, g.shape[1], LANE) if kind == "col" else (w, g.shape[1])

    shapes = [blk_shape(g, s) for g, s in zip(grads, specs)]

    def body(*refs):
        ins, owns, recvs = refs[:n], refs[n:2 * n], refs[2 * n:3 * n]
        send_sems, recv_sems, local_sems = refs[3 * n:]
        x, y, c = lax.axis_index("x"), lax.axis_index("y"), lax.axis_index("c")
        sib = (x, y, 1 - c)
        locals_, remotes = [], []
        for a in range(n):
            for r in range(4):
                px, py = _rel_chip(r)
                own_blk = 4 * px + 2 * py + c
                sib_blk = 4 * px + 2 * py + (1 - c)
                locals_.append(pltpu.make_async_copy(window(ins[a], specs[a], own_blk), owns[a].at[r],
                                                     local_sems.at[a, r]))
                remotes.append(pltpu.make_async_remote_copy(
                    src_ref=window(ins[a], specs[a], sib_blk), dst_ref=recvs[a].at[r],
                    send_sem=send_sems.at[a, r], recv_sem=recv_sems.at[a, r],
                    device_id=sib, device_id_type=MESH))
        for cp in remotes + locals_:
            cp.start()
        for cp in remotes:
            cp.wait_recv()
        for cp in remotes:
            cp.wait_send()
        for cp in locals_:
            cp.wait()

    any_spec = pl.BlockSpec(memory_space=pl.ANY)
    outs = pl.pallas_call(
        body, name="grads_rs_sibling",
        out_shape=[jax.ShapeDtypeStruct((4,) + s, F32) for s in shapes] * 2,
        in_specs=[any_spec] * n, out_specs=[any_spec] * (2 * n),
        scratch_shapes=[pltpu.SemaphoreType.DMA((n, 4)), pltpu.SemaphoreType.DMA((n, 4)),
                        pltpu.SemaphoreType.DMA((n, 4))],
    )(*grads)
    return outs[:n], outs[n:]


def _pair_sum(own, recv):
    _, R, C = own.shape
    tr = _row_tile(R, max(8, (1 << 19) // C))

    def body(a_ref, b_ref, f_ref, h_ref):
        s = a_ref[...] + b_ref[...]
        f_ref[...] = s
        h_ref[...] = s.astype(BF16)

    spec = pl.BlockSpec((1, tr, C), lambda r, i: (r, i, 0))
    return pl.pallas_call(
        body, name="grads_pair_sum", grid=(4, R // tr),
        in_specs=[spec, spec], out_specs=[spec, spec],
        out_shape=[jax.ShapeDtypeStruct(own.shape, F32), jax.ShapeDtypeStruct(own.shape, BF16)],
        compiler_params=_cparams(),
    )(own, recv)


def _reduce_scatter_stage2(parts):
    n = len(parts)

    def body(*refs):
        ins, recvs = refs[:n], refs[n:2 * n]
        send_sems, recv_sems = refs[2 * n:]
        c = lax.axis_index("c")
        copies = []
        for a in range(n):
            for r in range(1, 4):
                px, py = _rel_chip(r)
                copies.append(pltpu.make_async_remote_copy(
                    src_ref=ins[a].at[r], dst_ref=recvs[a].at[r - 1],
                    send_sem=send_sems.at[a, r - 1], recv_sem=recv_sems.at[a, r - 1],
                    device_id=(px, py, c), device_id_type=MESH))
        for cp in copies:
            cp.start()
        for cp in copies:
            cp.wait_recv()
        for cp in copies:
            cp.wait_send()

    any_spec = pl.BlockSpec(memory_space=pl.ANY)
    return pl.pallas_call(
        body, name="grads_rs_chips",
        out_shape=[jax.ShapeDtypeStruct((3,) + p.shape[1:], BF16) for p in parts],
        in_specs=[any_spec] * n, out_specs=[any_spec] * n,
        scratch_shapes=[pltpu.SemaphoreType.DMA((n, 3)), pltpu.SemaphoreType.DMA((n, 3))],
    )(*parts)


def _final_sum(psum, recv):
    _, R, C = psum.shape
    tr = _row_tile(R, 256)

    def body(p_ref, r_ref, o_ref):
        r = r_ref[...].astype(F32)
        o_ref[...] = ((p_ref[0] + r[0]) + r[1]) + r[2]

    return pl.pallas_call(
        body, name="grads_final_sum", grid=(R // tr,),
        in_specs=[pl.BlockSpec((1, tr, C), lambda i: (0, i, 0)), pl.BlockSpec((3, tr, C), lambda i: (0, i, 0))],
        out_specs=pl.BlockSpec((tr, C), lambda i: (i, 0)),
        out_shape=jax.ShapeDtypeStruct((R, C), F32),
        compiler_params=_cparams(),
    )(psum, recv)


def _final_sum_cols(psum, recv, M):
    T = psum.shape[1] // M

    def body(p_ref, r_ref, o_ref):
        r = r_ref[...].astype(F32)
        o_ref[...] = ((p_ref[0] + r[0]) + r[1]) + r[2]

    return pl.pallas_call(
        body, name="grads_final_sum_cols", grid=(T,),
        in_specs=[pl.BlockSpec((1, M, LANE), lambda t: (0, t, 0)), pl.BlockSpec((3, M, LANE), lambda t: (0, t, 0))],
        out_specs=pl.BlockSpec((M, LANE), lambda t: (0, t)),
        out_shape=jax.ShapeDtypeStruct((M, T * LANE), F32),
        compiler_params=_cparams(),
    )(psum, recv)


def _all_reduce_small(pack):
    R, P = pack.shape

    def body(x_ref, o_ref, gat_ref, send_sems, recv_sems):
        x, y, c = lax.axis_index("x"), lax.axis_index("y"), lax.axis_index("c")
        me = 4 * x + 2 * y + c
        gat_ref[me] = x_ref[...]
        copies = []
        for k in range(1, N_DEV):
            peer = (x ^ (k >> 2), y ^ ((k >> 1) & 1), c ^ (k & 1))
            copies.append(pltpu.make_async_remote_copy(
                src_ref=x_ref, dst_ref=gat_ref.at[me],
                send_sem=send_sems.at[k - 1], recv_sem=recv_sems.at[k - 1],
                device_id=peer, device_id_type=MESH))
        for cp in copies:
            cp.start()
        for cp in copies:
            cp.wait_recv()
        for cp in copies:
            cp.wait_send()
        acc = gat_ref[0]
        for d in range(1, N_DEV):
            acc = acc + gat_ref[d]
        o_ref[...] = acc

    vm = pl.BlockSpec(memory_space=pltpu.VMEM)
    return pl.pallas_call(
        body, name="small_all_reduce",
        out_shape=jax.ShapeDtypeStruct((R, P), F32),
        in_specs=[vm], out_specs=vm,
        scratch_shapes=[pltpu.VMEM((N_DEV, R, P), F32),
                        pltpu.SemaphoreType.DMA((N_DEV - 1,)), pltpu.SemaphoreType.DMA((N_DEV - 1,))],
    )(pack)


def _assemble(slabs, w):
    aligned, sw, total = _slab_geom(w)
    K = slabs.shape[1]
    tr = _row_tile(K, 256)

    def body(s_ref, o_ref):
        o_ref[...] = jnp.zeros(o_ref.shape, BF16)
        for i in range(N_DEV):
            a = aligned[i]
            o_ref[:, a:a + sw] = o_ref[:, a:a + sw] + s_ref[i]

    return pl.pallas_call(
        body, name="assemble_w_in", grid=(K // tr,),
        in_specs=[pl.BlockSpec((N_DEV, tr, sw), lambda i: (0, i, 0))],
        out_specs=pl.BlockSpec((tr, total), lambda i: (i, 0)),
        out_shape=jax.ShapeDtypeStruct((K, total), BF16),
        compiler_params=_cparams(),
    )(slabs)


def _rmsnorm_fwd(x, g, name):
    S, D = x.shape
    tm = _row_tile(S, 256)

    def body(x_ref, g_ref, h_ref):
        xv = x_ref[...]
        r = lax.rsqrt(jnp.mean(xv * xv, axis=-1, keepdims=True) + RMS_EPS)
        h_ref[...] = ((xv * r) * g_ref[...]).astype(BF16)

    return pl.pallas_call(
        body, name=name, grid=(S // tm,),
        in_specs=[pl.BlockSpec((tm, D), lambda i: (i, 0)), pl.BlockSpec((1, D), lambda i: (0, 0))],
        out_specs=pl.BlockSpec((tm, D), lambda i: (i, 0)),
        out_shape=jax.ShapeDtypeStruct((S, D), BF16),
        compiler_params=_cparams(),
    )(x, g)


def _rmsnorm_bwd(dh, x, g, dres, name):
    S, D = x.shape
    tm = _row_tile(S, 256)

    def body(dh_ref, x_ref, g_ref, dr_ref, dx_ref, dxb_ref, dg_ref):
        xv = x_ref[...]
        r = lax.rsqrt(jnp.mean(xv * xv, axis=-1, keepdims=True) + RMS_EPS)
        xhat = xv * r
        d = dh_ref[...]
        gd = d * g_ref[...]
        dx = r * (gd - xhat * jnp.mean(gd * xhat, axis=-1, keepdims=True)) + dr_ref[...]
        dx_ref[...] = dx
        dxb_ref[...] = dx.astype(BF16)

        @pl.when(pl.program_id(0) == 0)
        def _():
            dg_ref[...] = jnp.zeros(dg_ref.shape, F32)
        dg_ref[...] += jnp.sum(d * xhat, axis=0, keepdims=True)

    row = pl.BlockSpec((tm, D), lambda i: (i, 0))
    vec = pl.BlockSpec((1, D), lambda i: (0, 0))
    return pl.pallas_call(
        body, name=name, grid=(S // tm,),
        in_specs=[row, row, vec, row], out_specs=[row, row, vec],
        out_shape=[jax.ShapeDtypeStruct((S, D), F32), jax.ShapeDtypeStruct((S, D), BF16),
                   jax.ShapeDtypeStruct((1, D), F32)],
        compiler_params=_cparams(),
    )(dh, x, g, dres)


def _loss_head(x, tgt, g):
    S, D = x.shape
    tm = _row_tile(S, 256)

    def body(x_ref, t_ref, g_ref, dx_ref, dxb_ref, dg_ref, loss_ref):
        xv = x_ref[...]
        r = lax.rsqrt(jnp.mean(xv * xv, axis=-1, keepdims=True) + RMS_EPS)
        xhat = xv * r
        gv = g_ref[...]
        err = xhat * gv - t_ref[...]
        d = err * (1.0 / D)
        gd = d * gv
        dx = r * (gd - xhat * jnp.mean(gd * xhat, axis=-1, keepdims=True))
        dx_ref[...] = dx
        dxb_ref[...] = dx.astype(BF16)

        @pl.when(pl.program_id(0) == 0)
        def _():
            dg_ref[...] = jnp.zeros(dg_ref.shape, F32)
            loss_ref[...] = jnp.zeros(loss_ref.shape, F32)
        dg_ref[...] += jnp.sum(d * xhat, axis=0, keepdims=True)
        per_tok = jnp.sum(err * err, axis=-1, keepdims=True) * (1.0 / D)
        loss_ref[...] += 0.5 * jnp.sum(per_tok, axis=0, keepdims=True)

    row = pl.BlockSpec((tm, D), lambda i: (i, 0))
    vec = pl.BlockSpec((1, D), lambda i: (0, 0))
    return pl.pallas_call(
        body, name="loss_head", grid=(S // tm,),
        in_specs=[row, row, vec],
        out_specs=[row, row, vec, pl.BlockSpec((1, LANE), lambda i: (0, 0))],
        out_shape=[jax.ShapeDtypeStruct((S, D), F32), jax.ShapeDtypeStruct((S, D), BF16),
                   jax.ShapeDtypeStruct((1, D), F32), jax.ShapeDtypeStruct((1, LANE), F32)],
        compiler_params=_cparams(),
    )(x, tgt, g)


def _adamw(w, g, m, v, name):
    R, C = w.shape
    tr = _row_tile(R, 256)
    c1 = 1.0 - ADAM_B1 ** ADAM_STEP
    c2 = 1.0 - ADAM_B2 ** ADAM_STEP

    def body(w_ref, g_ref, m_ref, v_ref, d_ref, nm_ref, nv_ref):
        gv = g_ref[...]
        nm = ADAM_B1 * m_ref[...] + (1.0 - ADAM_B1) * gv
        nv = ADAM_B2 * v_ref[...] + (1.0 - ADAM_B2) * (gv * gv)
        d_ref[...] = -ADAM_LR * ((nm / c1) / (jnp.sqrt(nv / c2) + ADAM_EPS) + ADAM_WD * w_ref[...])
        nm_ref[...] = nm
        nv_ref[...] = nv

    spec = pl.BlockSpec((tr, C), lambda i: (i, 0))
    return pl.pallas_call(
        body, name=name, grid=(R // tr,),
        in_specs=[spec] * 4, out_specs=[spec] * 3,
        out_shape=[jax.ShapeDtypeStruct((R, C), F32)] * 3,
        compiler_params=_cparams(),
    )(w, g, m, v)


def _proj(h, wfull, col0, ncols, out_dtype, name, rope=None):
    S, K = h.shape
    tm = _row_tile(S, MM_TILE)
    tn = math.gcd(_tile(ncols, MM_TILE), col0) if col0 else _tile(ncols, MM_TILE)
    if rope is not None:
        tn = _tile(math.gcd(ncols, rope[1]), MM_TILE)
    assert ncols % tn == 0 and col0 % tn == 0
    cb = col0 // tn

    def body(*refs):
        if rope is None:
            a_ref, b_ref, o_ref = refs
        else:
            a_ref, b_ref, t_ref, o_ref = refs
        acc = _dot_nn(a_ref[...], b_ref[...])
        if rope is not None:
            t0, t1, t2 = (jnp.tile(t_ref[i], (1, tn // LANE)) for i in range(3))
            roped = (acc * t0 + pltpu.roll(acc, tn - ROT_DIM // 2, axis=1) * t1
                     + pltpu.roll(acc, ROT_DIM // 2, axis=1) * t2)
            acc = jnp.where(pl.program_id(1) < rope[1] // tn, roped, acc)
        o_ref[...] = acc.astype(out_dtype)

    in_specs = [pl.BlockSpec((tm, K), lambda i, j: (i, 0)), pl.BlockSpec((K, tn), lambda i, j: (0, cb + j))]
    args = [h, wfull]
    if rope is not None:
        in_specs.append(pl.BlockSpec((3, tm, LANE), lambda i, j: (0, i, 0)))
        args.append(rope[0])
    return pl.pallas_call(
        body, name=name, grid=(S // tm, ncols // tn),
        in_specs=in_specs, out_specs=pl.BlockSpec((tm, tn), lambda i, j: (i, j)),
        out_shape=jax.ShapeDtypeStruct((S, ncols), out_dtype),
        compiler_params=_cparams(),
    )(*args)


def _out_proj_res(y, wo, xres, name):
    S, W = y.shape
    D = wo.shape[1]
    tm, tn = _row_tile(S, MM_TILE), _tile(D, MM_TILE)

    def body(a_ref, b_ref, r_ref, o_ref):
        o_ref[...] = r_ref[...] + _dot_nn(a_ref[...], b_ref[...])

    return pl.pallas_call(
        body, name=name, grid=(S // tm, D // tn),
        in_specs=[pl.BlockSpec((tm, W), lambda i, j: (i, 0)), pl.BlockSpec((W, tn), lambda i, j: (0, j)),
                  pl.BlockSpec((tm, tn), lambda i, j: (i, j))],
        out_specs=pl.BlockSpec((tm, tn), lambda i, j: (i, j)),
        out_shape=jax.ShapeDtypeStruct((S, D), F32),
        compiler_params=_cparams(),
    )(y, wo, xres)


def _matmul_nt(parts, wfull, out_rows, name):
    S = parts[0][0].shape[-2]
    tm, tn = _row_tile(S, MM_TILE), _tile(out_rows, MM_TILE)
    plan, lo = [], 0
    for arr, lead, col0 in parts:
        n_p = arr.shape[-1]
        tk = math.gcd(_tile(n_p, 1024), col0) if col0 else _tile(n_p, 1024)
        steps = n_p // tk
        plan.append((lead, col0 // tk, tk, lo, lo + steps))
        lo += steps
    nk = lo
    npart = len(parts)

    def body(*refs):
        a_refs, w_refs = refs[:npart], refs[npart:2 * npart]
        o_ref, acc_ref = refs[2 * npart], refs[2 * npart + 1]
        k = pl.program_id(2)

        @pl.when(k == 0)
        def _():
            acc_ref[...] = jnp.zeros(acc_ref.shape, F32)
        for p, (_, _, _, lo_p, hi_p) in enumerate(plan):
            @pl.when((k >= lo_p) & (k < hi_p))
            def _(p=p):
                acc_ref[...] += _dot_nt(a_refs[p][...], w_refs[p][...])

        @pl.when(k == nk - 1)
        def _():
            o_ref[...] = acc_ref[...]

    in_specs, args = [], []
    for (arr, lead, col0), (_, cb, tk, lo_p, hi_p) in zip(parts, plan):
        def kk(k, lo_p=lo_p, hi_p=hi_p):
            return jnp.clip(k - lo_p, 0, hi_p - lo_p - 1)
        if lead is None:
            in_specs.append(pl.BlockSpec((tm, tk), lambda i, j, k, kk=kk: (i, kk(k))))
        else:
            in_specs.append(pl.BlockSpec((None, tm, tk), lambda i, j, k, kk=kk, lead=lead: (lead, i, kk(k))))
        args.append(arr)
    for (_, cb, tk, lo_p, hi_p) in plan:
        def kk(k, lo_p=lo_p, hi_p=hi_p):
            return jnp.clip(k - lo_p, 0, hi_p - lo_p - 1)
        in_specs.append(pl.BlockSpec((tn, tk), lambda i, j, k, kk=kk, cb=cb: (j, cb + kk(k))))
        args.append(wfull)
    return pl.pallas_call(
        body, name=name, grid=(S // tm, out_rows // tn, nk),
        in_specs=in_specs, out_specs=pl.BlockSpec((tm, tn), lambda i, j, k: (i, j)),
        out_shape=jax.ShapeDtypeStruct((S, out_rows), F32),
        scratch_shapes=[pltpu.VMEM((tm, tn), F32)],
        compiler_params=_cparams(),
    )(*args)


def _matmul_tn(a, parts, total, name, tile_major=False):
    S, M = a.shape
    tm, ts = _tile(M, MM_TILE), _row_tile(S, 512)
    out = None
    for idx, (arr, lead, col0) in enumerate(parts):
        n_p = arr.shape[-1]
        tn = math.gcd(_tile(n_p, MM_TILE), col0) if col0 else _tile(n_p, MM_TILE)
        cb = col0 // tn
        nk = S // ts

        def body(*refs, nk=nk, tn=tn):
            a_ref, b_ref = refs[0], refs[1]
            o_ref, acc_ref = refs[-2], refs[-1]
            k = pl.program_id(2)

            @pl.when(k == 0)
            def _():
                acc_ref[...] = jnp.zeros(acc_ref.shape, F32)
            acc_ref[...] += _dot_tn(a_ref[...], b_ref[...])

            @pl.when(k == nk - 1)
            def _():
                if tile_major:
                    for t in range(tn // LANE):
                        o_ref[t] = acc_ref[:, LANE * t:LANE * (t + 1)]
                else:
                    o_ref[...] = acc_ref[...]

        in_specs = [pl.BlockSpec((ts, tm), lambda i, j, k: (k, i))]
        if lead is None:
            in_specs.append(pl.BlockSpec((ts, tn), lambda i, j, k: (k, j)))
        else:
            in_specs.append(pl.BlockSpec((None, ts, tn), lambda i, j, k, lead=lead: (lead, k, j)))
        args = [a, arr]
        aliases = {}
        if out is not None:
            in_specs.append(pl.BlockSpec(memory_space=pl.ANY))
            args.append(out)
            aliases = {2: 0}
        if tile_major:
            out_spec = pl.BlockSpec((tn // LANE, tm, LANE), lambda i, j, k, cb=cb: (cb + j, i, 0))
            out_shape = jax.ShapeDtypeStruct((total // LANE, M, LANE), F32)
        else:
            out_spec = pl.BlockSpec((tm, tn), lambda i, j, k, cb=cb: (i, cb + j))
            out_shape = jax.ShapeDtypeStruct((M, total), F32)
        out = pl.pallas_call(
            body, name=f"{name}_{idx}", grid=(M // tm, n_p // tn, nk),
            in_specs=in_specs, out_specs=out_spec, out_shape=out_shape,
            scratch_shapes=[pltpu.VMEM((tm, tn), F32)],
            input_output_aliases=aliases,
            compiler_params=_cparams(),
        )(*args)
    return out


def _log_sigmoid(z):
    e = jnp.exp(-jnp.abs(z))
    return jnp.minimum(z, 0.0) - jnp.where(e < 1e-4, e * (1.0 - 0.5 * e), jnp.log(1.0 + e))


def _fox_gate_fwd(fl, bias):
    S = fl.shape[0]

    def body(f_ref, b_ref, c_ref, ct_ref):
        row = lax.broadcasted_iota(jnp.int32, (8, LANE), 0)

        def step(i, carry):
            r0 = pl.multiple_of(i * 8, 8)
            t = _log_sigmoid(f_ref[pl.ds(r0, 8), :] + b_ref[...])
            for sh in (1, 2, 4):
                t = t + jnp.where(row >= sh, pltpu.roll(t, sh, axis=0), 0.0)
            t = t + carry
            c_ref[pl.ds(r0, 8), :] = t
            return jnp.sum(jnp.where(row == 7, t, 0.0), axis=0, keepdims=True)

        lax.fori_loop(0, S // 8, step, jnp.zeros((1, LANE), F32))
        ct_ref[...] = c_ref[...].T

    vm = pl.BlockSpec(memory_space=pltpu.VMEM)
    return pl.pallas_call(
        body, name="fox_gate_fwd", in_specs=[vm, vm], out_specs=[vm, vm],
        out_shape=[jax.ShapeDtypeStruct((S, LANE), F32), jax.ShapeDtypeStruct((LANE, S), F32)],
        compiler_params=_cparams(),
    )(fl, bias)


def _fox_gate_bwd(fl, bias, dc):
    S = fl.shape[0]

    def body(f_ref, b_ref, d_ref, o_ref, db_ref, acc_ref):
        row = lax.broadcasted_iota(jnp.int32, (8, LANE), 0)
        nt = S // 8

        def step(ii, carry):
            carry_c, carry_b = carry
            r0 = pl.multiple_of((nt - 1 - ii) * 8, 8)
            t = d_ref[pl.ds(r0, 8), :]
            for sh in (1, 2, 4):
                t = t + jnp.where(row < 8 - sh, pltpu.roll(t, 8 - sh, axis=0), 0.0)
            t = t + carry_c
            z = f_ref[pl.ds(r0, 8), :] + b_ref[...]
            dz = t * _sigmoid(-z)
            acc_ref[pl.ds(r0, 8), :] = dz
            first = jnp.sum(jnp.where(row == 0, t, 0.0), axis=0, keepdims=True)
            return first, carry_b + jnp.sum(dz, axis=0, keepdims=True)

        zero = jnp.zeros((1, LANE), F32)
        _, db = lax.fori_loop(0, nt, step, (zero, zero))
        db_ref[...] = db
        o_ref[...] = acc_ref[...].astype(BF16)

    vm = pl.BlockSpec(memory_space=pltpu.VMEM)
    return pl.pallas_call(
        body, name="fox_gate_bwd", in_specs=[vm, vm, vm], out_specs=[vm, vm],
        out_shape=[jax.ShapeDtypeStruct((S, LANE), BF16), jax.ShapeDtypeStruct((1, LANE), F32)],
        scratch_shapes=[pltpu.VMEM((S, LANE), F32)],
        compiler_params=_cparams(),
    )(fl, bias, dc)


def _fox_fwd(qkv, gate, c, ct, H):
    S = qkv.shape[0]
    W = H * HEAD_DIM
    HP = H // 2
    tq = _row_tile(S, 512)
    nq = S // tq
    wb = W // LANE
    scale = HEAD_DIM ** -0.5

    def body(q_ref, k_ref, v_ref, g_ref, c_ref, ct_ref, y_ref, o_ref, a_ref, m_sc, l_sc, acc_sc):
        hp, qi = pl.program_id(0), pl.program_id(1)
        lane_q = lax.broadcasted_iota(jnp.int32, (tq, LANE), 1)
        lo_q = lane_q < HEAD_DIM
        rows = lax.broadcasted_iota(jnp.int32, (tq, tq), 0)
        cols = lax.broadcasted_iota(jnp.int32, (tq, tq), 1)
        q = q_ref[...] * jnp.asarray(scale, BF16)
        crow = c_ref[pl.ds(pl.multiple_of(qi * tq, tq), tq), :]
        qm = [jnp.where(lo_q, q, jnp.zeros_like(q)), jnp.where(lo_q, jnp.zeros_like(q), q)]
        ctq = [jnp.sum(jnp.where(lane_q == 2 * hp + e, crow, 0.0), axis=1, keepdims=True) for e in range(2)]
        m_sc[...] = jnp.full(m_sc.shape, NEG_INF, F32)
        l_sc[...] = jnp.zeros(l_sc.shape, F32)
        acc_sc[...] = jnp.zeros(acc_sc.shape, F32)

        def step(j, masked):
            k0 = pl.multiple_of(j * tq, tq)
            kblk = k_ref[pl.ds(k0, tq), :]
            vblk = v_ref[pl.ds(k0, tq), :]
            for e in range(2):
                cs = ct_ref[pl.ds(2 * hp + e, 1), pl.ds(k0, tq)]
                s = _dot_nt(qm[e], kblk) + (ctq[e] - cs)
                if masked:
                    s = jnp.where(rows >= cols, s, NEG_INF)
                m_prev = m_sc[e]
                m_new = jnp.maximum(m_prev, jnp.max(s, axis=1, keepdims=True))
                alpha = jnp.exp(m_prev - m_new)
                p = jnp.exp(s - m_new)
                l_sc[e] = alpha * l_sc[e] + jnp.sum(p, axis=1, keepdims=True)
                acc_sc[e] = alpha * acc_sc[e] + _dot_nn(p.astype(BF16), vblk)
                m_sc[e] = m_new

        def loop_body(j, carry):
            step(j, False)
            return carry

        lax.fori_loop(0, qi, loop_body, 0)
        step(qi, True)
        inv = [1.0 / l_sc[e] for e in range(2)]
        o = jnp.where(lo_q, acc_sc[0] * inv[0], acc_sc[1] * inv[1])
        a = [ctq[e] - (m_sc[e] + jnp.log(l_sc[e])) for e in range(2)]
        g = g_ref[...]
        y_ref[...] = (o * (g * _sigmoid(g))).astype(BF16)
        o_ref[...] = o.astype(BF16)
        a_ref[0] = jnp.where(lo_q, a[0], a[1])

    return pl.pallas_call(
        body, name="fox_attn_fwd", grid=(HP, nq),
        in_specs=[pl.BlockSpec((tq, LANE), lambda h, i: (i, h)),
                  pl.BlockSpec((S, LANE), lambda h, i: (0, wb + h)),
                  pl.BlockSpec((S, LANE), lambda h, i: (0, 2 * wb + h)),
                  pl.BlockSpec((tq, LANE), lambda h, i: (i, h)),
                  pl.BlockSpec((S, LANE), lambda h, i: (0, 0)),
                  pl.BlockSpec((LANE, S), lambda h, i: (0, 0))],
        out_specs=[pl.BlockSpec((tq, LANE), lambda h, i: (i, h)),
                   pl.BlockSpec((tq, LANE), lambda h, i: (i, h)),
                   pl.BlockSpec((1, tq, LANE), lambda h, i: (h, i, 0))],
        out_shape=[jax.ShapeDtypeStruct((S, W), BF16), jax.ShapeDtypeStruct((S, W), BF16),
                   jax.ShapeDtypeStruct((HP, S, LANE), F32)],
        scratch_shapes=[pltpu.VMEM((2, tq, 1), F32), pltpu.VMEM((2, tq, 1), F32), pltpu.VMEM((2, tq, LANE), F32)],
        compiler_params=_cparams(),
    )(qkv, qkv, qkv, gate, c, ct)


def _fox_bwd(qkv, dy, gate, o, a, ct, H):
    S = qkv.shape[0]
    W = H * HEAD_DIM
    HP = H // 2
    tq = _row_tile(S, 512)
    nq = S // tq
    wb = W // LANE
    scale = HEAD_DIM ** -0.5

    def body(q_ref, k_ref, v_ref, dy_ref, g_ref, o_ref, a_ref, ct_ref, out_ref, dcs_ref, dcr_ref,
             do_sc, delta_sc, dq_sc, dk_sc, dv_sc, drow_sc):
        hp, kj = pl.program_id(0), pl.program_id(1)
        lane = lax.broadcasted_iota(jnp.int32, (tq, LANE), 1)
        lo = lane < HEAD_DIM
        rows = lax.broadcasted_iota(jnp.int32, (tq, tq), 0)
        cols = lax.broadcasted_iota(jnp.int32, (tq, tq), 1)

        @pl.when(kj == 0)
        def _():
            def chunk(i, carry):
                r0 = pl.multiple_of(i * tq, tq)
                dyv = dy_ref[pl.ds(r0, tq), :]
                g = g_ref[pl.ds(r0, tq), :]
                ov = o_ref[pl.ds(r0, tq), :].astype(F32)
                sg = _sigmoid(g)
                dob = (dyv * (g * sg)).astype(BF16)
                out_ref[3, pl.ds(r0, tq), :] = (dyv * ov * (sg * (1.0 + g * (1.0 - sg)))).astype(BF16)
                do_sc[pl.ds(r0, tq), :] = dob
                prod = dob.astype(F32) * ov
                d0 = jnp.sum(jnp.where(lo, prod, 0.0), axis=1, keepdims=True)
                d1 = jnp.sum(jnp.where(lo, 0.0, prod), axis=1, keepdims=True)
                delta_sc[pl.ds(r0, tq), :] = jnp.where(lo, d0, d1)
                dq_sc[pl.ds(r0, tq), :] = jnp.zeros((tq, LANE), F32)
                drow_sc[pl.ds(r0, tq), :] = jnp.zeros((tq, LANE), F32)
                return carry
            lax.fori_loop(0, nq, chunk, 0)

        @pl.when((kj == 0) & (hp == 0))
        def _():
            dcr_ref[...] = jnp.zeros(dcr_ref.shape, F32)

        kblk = k_ref[...]
        vblk = v_ref[...]
        zb = jnp.zeros_like(kblk)
        km = [jnp.where(lo, kblk, zb), jnp.where(lo, zb, kblk)]
        vm = [jnp.where(lo, vblk, zb), jnp.where(lo, zb, vblk)]
        dk_sc[...] = jnp.zeros(dk_sc.shape, F32)
        dv_sc[...] = jnp.zeros(dv_sc.shape, F32)
        dcs_ref[...] = jnp.zeros(dcs_ref.shape, F32)

        def step(i, masked):
            r0 = pl.multiple_of(i * tq, tq)
            qt = q_ref[pl.ds(r0, tq), :] * jnp.asarray(scale, BF16)
            dot = do_sc[pl.ds(r0, tq), :]
            at = a_ref[0, pl.ds(r0, tq), :]
            dl = delta_sc[pl.ds(r0, tq), :]
            dq_new, dk_new, dv_new, dr_new = [], [], [], []
            for e in range(2):
                sel = lo if e == 0 else jnp.logical_not(lo)
                a_col = jnp.max(jnp.where(sel, at, -jnp.inf), axis=1, keepdims=True)
                d_col = jnp.max(jnp.where(sel, dl, -jnp.inf), axis=1, keepdims=True)
                cs = ct_ref[pl.ds(2 * hp + e, 1), :]
                s = _dot_nt(qt, km[e]) + (a_col - cs)
                p = jnp.exp(s)
                if masked:
                    p = jnp.where(rows >= cols, p, 0.0)
                dp = _dot_nt(dot, vm[e])
                ds = p * (dp - d_col)
                pb, dsb = p.astype(BF16), ds.astype(BF16)
                dv_new.append(_dot_tn(pb, dot))
                dk_new.append(_dot_tn(dsb, qt))
                dq_new.append(_dot_nn(dsb, kblk))
                dcs_ref[0, pl.ds(e, 1), :] += jnp.sum(ds, axis=0, keepdims=True)
                dr_new.append(jnp.sum(ds, axis=1, keepdims=True))
            drow_sc[pl.ds(r0, tq), :] += jnp.where(lo, dr_new[0], dr_new[1])
            dv_sc[...] += jnp.where(lo, dv_new[0], dv_new[1])
            dk_sc[...] += jnp.where(lo, dk_new[0], dk_new[1])
            dq_sc[pl.ds(r0, tq), :] += jnp.where(lo, dq_new[0], dq_new[1])

        step(kj, True)

        def loop_body(i, carry):
            step(i, False)
            return carry

        lax.fori_loop(kj + 1, nq, loop_body, 0)
        k0 = pl.multiple_of(kj * tq, tq)
        out_ref[1, pl.ds(k0, tq), :] = dk_sc[...].astype(BF16)
        out_ref[2, pl.ds(k0, tq), :] = dv_sc[...].astype(BF16)

        @pl.when(kj == nq - 1)
        def _():
            out_ref[0] = (dq_sc[...] * scale).astype(BF16)

            def chunk(i, carry):
                r0 = pl.multiple_of(i * tq, tq)
                dr = drow_sc[pl.ds(r0, tq), :]
                acc = dcr_ref[pl.ds(r0, tq), :]
                for e in range(2):
                    sel = lo if e == 0 else jnp.logical_not(lo)
                    col = jnp.max(jnp.where(sel, dr, -jnp.inf), axis=1, keepdims=True)
                    acc = jnp.where(lane == 2 * hp + e, col, acc)
                dcr_ref[pl.ds(r0, tq), :] = acc
                return carry
            lax.fori_loop(0, nq, chunk, 0)

    full = lambda cb: pl.BlockSpec((S, LANE), lambda h, j, cb=cb: (0, cb + h))
    return pl.pallas_call(
        body, name="fox_attn_bwd", grid=(HP, nq),
        in_specs=[full(0),
                  pl.BlockSpec((tq, LANE), lambda h, j: (j, wb + h)),
                  pl.BlockSpec((tq, LANE), lambda h, j: (j, 2 * wb + h)),
                  full(0), full(0), full(0),
                  pl.BlockSpec((1, S, LANE), lambda h, j: (h, 0, 0)),
                  pl.BlockSpec((LANE, tq), lambda h, j: (0, j))],
        out_specs=[pl.BlockSpec((4, S, LANE), lambda h, j: (0, 0, h)),
                   pl.BlockSpec((1, 8, tq), lambda h, j: (h, 0, j)),
                   pl.BlockSpec((S, LANE), lambda h, j: (0, 0))],
        out_shape=[jax.ShapeDtypeStruct((4, S, W), BF16), jax.ShapeDtypeStruct((HP, 8, S), F32),
                   jax.ShapeDtypeStruct((S, LANE), F32)],
        scratch_shapes=[pltpu.VMEM((S, LANE), BF16), pltpu.VMEM((S, LANE), F32), pltpu.VMEM((S, LANE), F32),
                        pltpu.VMEM((tq, LANE), F32), pltpu.VMEM((tq, LANE), F32), pltpu.VMEM((S, LANE), F32)],
        compiler_params=_cparams(),
    )(qkv, qkv, qkv, dy, gate, o, a, ct)


def _swa_pick(blk, half, lane):
    b = blk.astype(F32)
    r = pltpu.roll(b, HEAD_DIM, axis=1)
    return jnp.where(jnp.logical_xor(lane < HEAD_DIM, half == 1), b, r).astype(BF16)


def _swa_stack(t, lane, G):
    pieces = []
    z = jnp.zeros((SWA_BLOCK, LANE), t.dtype)
    for j in range(G // 2):
        tile = t[:, LANE * j:LANE * (j + 1)]
        pieces += [jnp.where(lane < HEAD_DIM, tile, z), jnp.where(lane < HEAD_DIM, z, tile)]
    return jnp.concatenate(pieces, axis=0)


def _swa_unstack(st, lane, G):
    tiles = []
    for j in range(G // 2):
        a = st[2 * j * SWA_BLOCK:(2 * j + 1) * SWA_BLOCK]
        b = st[(2 * j + 1) * SWA_BLOCK:(2 * j + 2) * SWA_BLOCK]
        tiles.append(jnp.where(lane < HEAD_DIM, a, b))
    return jnp.concatenate(tiles, axis=1)


def _swa_scores(q_ref, kp_ref, kc_ref, vp_ref, vc_ref, sink_ref, kvh, n, G):
    R = G * SWA_BLOCK
    lane = lax.broadcasted_iota(jnp.int32, (SWA_BLOCK, LANE), 1)
    half = kvh % 2
    kk = jnp.concatenate([_swa_pick(kp_ref[...], half, lane), _swa_pick(kc_ref[...], half, lane)], axis=0)
    vv = jnp.concatenate([_swa_pick(vp_ref[...], half, lane), _swa_pick(vc_ref[...], half, lane)], axis=0)
    qstack = _swa_stack(q_ref[...], lane, G) * jnp.asarray(HEAD_DIM ** -0.5, BF16)
    s = _dot_nt(qstack, kk)
    t_loc = lax.broadcasted_iota(jnp.int32, (R, 2 * SWA_BLOCK), 0) & (SWA_BLOCK - 1)
    j_loc = lax.broadcasted_iota(jnp.int32, (R, 2 * SWA_BLOCK), 1)
    diff = t_loc + SWA_BLOCK - j_loc
    mask = (diff >= 0) & (diff < SWA_BLOCK) & ((n > 0) | (j_loc >= SWA_BLOCK))
    s = jnp.where(mask, s, NEG_INF)
    srow = sink_ref[...]
    lane1 = lax.broadcasted_iota(jnp.int32, (1, LANE), 1)
    sink = jnp.concatenate(
        [jnp.broadcast_to(jnp.sum(jnp.where(lane1 == kvh * G + g, srow, 0.0), axis=1, keepdims=True), (SWA_BLOCK, 1))
         for g in range(G)], axis=0)
    m = jnp.maximum(jnp.max(s, axis=1, keepdims=True), sink)
    e = jnp.exp(s - m)
    es = jnp.exp(sink - m)
    den = jnp.sum(e, axis=1, keepdims=True) + es
    return qstack, kk, vv, e / den, es / den, lane


def _swa_fwd(qkv, gate, sinks, HQ, HKV):
    S = qkv.shape[0]
    G = HQ // HKV
    WQ, KVW = HQ * HEAD_DIM, HKV * HEAD_DIM
    nb = S // SWA_BLOCK
    GW = G * HEAD_DIM
    kb, vb = WQ // LANE, (WQ + KVW) // LANE

    def body(q_ref, kp_ref, kc_ref, vp_ref, vc_ref, g_ref, sink_ref, y_ref, o_ref):
        kvh, n = pl.program_id(0), pl.program_id(1)
        _, _, vv, p, _, lane = _swa_scores(q_ref, kp_ref, kc_ref, vp_ref, vc_ref, sink_ref, kvh, n, G)
        o = _swa_unstack(_dot_nn(p.astype(BF16), vv), lane, G)
        g = g_ref[...]
        y_ref[...] = (o * (g * _sigmoid(g))).astype(BF16)
        o_ref[...] = o.astype(BF16)

    blk = lambda cb, prev: pl.BlockSpec(
        (SWA_BLOCK, LANE), lambda h, n, cb=cb, prev=prev: (jnp.maximum(n - prev, 0), cb + h // 2))
    qspec = pl.BlockSpec((SWA_BLOCK, GW), lambda h, n: (n, h))
    return pl.pallas_call(
        body, name="swa_attn_fwd", grid=(HKV, nb),
        in_specs=[qspec, blk(kb, 1), blk(kb, 0), blk(vb, 1), blk(vb, 0), qspec,
                  pl.BlockSpec((1, LANE), lambda h, n: (0, 0))],
        out_specs=[qspec, qspec],
        out_shape=[jax.ShapeDtypeStruct((S, WQ), BF16), jax.ShapeDtypeStruct((S, WQ), BF16)],
        compiler_params=_cparams(),
    )(qkv, qkv, qkv, qkv, qkv, gate, sinks)


def _swa_bwd(qkv, dy, gate, o, sinks, tables, HQ, HKV):
    S = qkv.shape[0]
    G = HQ // HKV
    WQ, KVW = HQ * HEAD_DIM, HKV * HEAD_DIM
    nb = S // SWA_BLOCK
    GW = G * HEAD_DIM
    R = G * SWA_BLOCK
    kb, vb = WQ // LANE, (WQ + KVW) // LANE
    scale = HEAD_DIM ** -0.5
    assert G == 8

    def body(q_ref, kp_ref, kc_ref, vp_ref, vc_ref, dy_ref, g_ref, o_ref, sink_ref, t_ref,
             dqg_ref, dkv_ref, dsink_ref, carry_sc):
        kvh, n = pl.program_id(0), pl.program_id(1)

        @pl.when(n == 0)
        def _():
            carry_sc[...] = jnp.zeros(carry_sc.shape, F32)
            dsink_ref[...] = jnp.zeros(dsink_ref.shape, F32)

        @pl.when(n < nb)
        def _():
            qstack, kk, vv, p, psink, lane = _swa_scores(q_ref, kp_ref, kc_ref, vp_ref, vc_ref, sink_ref, kvh, n, G)
            dyv, g, ov = dy_ref[...], g_ref[...], o_ref[...].astype(F32)
            sg = _sigmoid(g)
            dob = (dyv * (g * sg)).astype(BF16)
            dqg_ref[1] = (dyv * ov * (sg * (1.0 + g * (1.0 - sg)))).astype(BF16)
            prod = dob.astype(F32) * ov
            dparts = []
            for j in range(G // 2):
                tile = prod[:, LANE * j:LANE * (j + 1)]
                dparts += [jnp.sum(jnp.where(lane < HEAD_DIM, tile, 0.0), axis=1, keepdims=True),
                           jnp.sum(jnp.where(lane < HEAD_DIM, 0.0, tile), axis=1, keepdims=True)]
            delta = jnp.concatenate(dparts, axis=0)
            dostack = _swa_stack(dob, lane, G)
            dp = _dot_nt(dostack, vv)
            ds = p * (dp - delta)
            dsb, pb = ds.astype(BF16), p.astype(BF16)
            dq = _swa_unstack(_dot_nn(dsb, kk), lane, G) * scale
            dq = (dq * t_ref[0] + pltpu.roll(dq * t_ref[1], ROT_DIM // 2, axis=1)
                  + pltpu.roll(dq * t_ref[2], GW - ROT_DIM // 2, axis=1))
            dqg_ref[0] = dq.astype(BF16)
            dkk = _dot_tn(dsb, qstack)
            dvv = _dot_tn(pb, dostack)
            dkk = dkk + pltpu.roll(dkk, HEAD_DIM, axis=1)
            dvv = dvv + pltpu.roll(dvv, HEAD_DIM, axis=1)
            lane2 = lax.broadcasted_iota(jnp.int32, (2 * SWA_BLOCK, LANE), 1)
            comb = jnp.where(lane2 < HEAD_DIM, dkk, dvv)
            dkv_ref[0] = carry_sc[...] + comb[:SWA_BLOCK]
            carry_sc[...] = comb[SWA_BLOCK:]
            sk = psink * delta
            rows = [jnp.broadcast_to(-jnp.sum(sk[g_ * SWA_BLOCK:(g_ + 1) * SWA_BLOCK], axis=0, keepdims=True), (1, LANE))
                    for g_ in range(G)]
            dsink_ref[0] += jnp.concatenate(rows, axis=0)

        @pl.when(n == nb)
        def _():
            dkv_ref[0] = carry_sc[...]

    cl = lambda n: jnp.minimum(n, nb - 1)
    blk = lambda cb, prev: pl.BlockSpec(
        (SWA_BLOCK, LANE), lambda h, n, cb=cb, prev=prev: (jnp.maximum(cl(n) - prev, 0), cb + h // 2))
    qspec = pl.BlockSpec((SWA_BLOCK, GW), lambda h, n: (cl(n), h))
    return pl.pallas_call(
        body, name="swa_attn_bwd", grid=(HKV, nb + 1),
        in_specs=[qspec, blk(kb, 1), blk(kb, 0), blk(vb, 1), blk(vb, 0), qspec, qspec, qspec,
                  pl.BlockSpec((1, LANE), lambda h, n: (0, 0)),
                  pl.BlockSpec((3, SWA_BLOCK, GW), lambda h, n: (0, cl(n), 0))],
        out_specs=[pl.BlockSpec((2, SWA_BLOCK, GW), lambda h, n: (0, cl(n), h)),
                   pl.BlockSpec((1, SWA_BLOCK, LANE), lambda h, n: (h, jnp.maximum(n - 1, 0), 0)),
                   pl.BlockSpec((1, 8, LANE), lambda h, n: (h, 0, 0))],
        out_shape=[jax.ShapeDtypeStruct((2, S, WQ), BF16), jax.ShapeDtypeStruct((HKV, S, LANE), F32),
                   jax.ShapeDtypeStruct((HKV, 8, LANE), F32)],
        scratch_shapes=[pltpu.VMEM((SWA_BLOCK, LANE), F32)],
        compiler_params=_cparams(),
    )(qkv, qkv, qkv, qkv, qkv, dy, gate, o, sinks, tables)


def _swa_dkv_finish(dkv, tables):
    HKV, S, _ = dkv.shape
    KVW = HKV * HEAD_DIM
    tm = _row_tile(S, 512)
    npair = HKV // 2

    def body(d_ref, t_ref, o_ref):
        lane = lax.broadcasted_iota(jnp.int32, (tm, LANE), 1)
        lo = lane < HEAD_DIM
        for p in range(npair):
            a, b = d_ref[2 * p], d_ref[2 * p + 1]
            tk = jnp.where(lo, a, pltpu.roll(b, HEAD_DIM, axis=1))
            tv = jnp.where(lo, pltpu.roll(a, HEAD_DIM, axis=1), b)
            tk = (tk * t_ref[0] + pltpu.roll(tk * t_ref[1], ROT_DIM // 2, axis=1)
                  + pltpu.roll(tk * t_ref[2], LANE - ROT_DIM // 2, axis=1))
            o_ref[:, LANE * p:LANE * (p + 1)] = tk.astype(BF16)
            o_ref[:, KVW + LANE * p:KVW + LANE * (p + 1)] = tv.astype(BF16)

    return pl.pallas_call(
        body, name="swa_dkv_finish", grid=(S // tm,),
        in_specs=[pl.BlockSpec((HKV, tm, LANE), lambda i: (0, i, 0)), pl.BlockSpec((3, tm, LANE), lambda i: (0, i, 0))],
        out_specs=pl.BlockSpec((tm, 2 * KVW), lambda i: (i, 0)),
        out_shape=jax.ShapeDtypeStruct((S, 2 * KVW), BF16),
        compiler_params=_cparams(),
    )(dkv, tables)


def _rope_tables(S, width):
    half = ROT_DIM // 2
    pos = jnp.arange(S, dtype=F32)
    inv_freq = ROPE_THETA ** (-jnp.arange(half, dtype=F32) / half)
    ang = pos[:, None] * inv_freq[None, :]
    cos, sin = jnp.cos(ang), jnp.sin(ang)
    one = jnp.ones((S, HEAD_DIM - ROT_DIM), F32)
    zero = jnp.zeros((S, HEAD_DIM - ROT_DIM), F32)
    zh = jnp.zeros((S, half), F32)
    t0 = jnp.concatenate([cos, cos, one], axis=1)
    t1 = jnp.concatenate([-sin, zh, zero], axis=1)
    t2 = jnp.concatenate([zh, sin, zero], axis=1)
    return jnp.stack([jnp.tile(t, (1, width // HEAD_DIM)) for t in (t0, t1, t2)])


def _pad_rows(v, row, total_rows=8):
    return jnp.pad(v, ((row, total_rows - row - v.shape[0]), (0, 0)))


def _pad_lanes(v, off, width):
    return jnp.pad(v, ((0, 0), (off, width - off - v.shape[1])))


def kernel(x, norm_g, fox_w_in, fox_b_f, fox_w_out, swa_w_in, swa_sinks, swa_w_out, final_g, loss_target, m_norm_g, m_fox_w_in, m_fox_b_f, m_fox_w_out, m_swa_w_in, m_swa_sinks, m_swa_w_out, m_final_g, v_norm_g, v_fox_w_in, v_fox_b_f, v_fox_w_out, v_swa_w_in, v_swa_sinks, v_swa_w_out, v_final_g):
    S, D = x.shape[1], x.shape[2]
    H = fox_b_f.shape[1]
    W = H * HEAD_DIM
    wf = fox_w_in.shape[2]
    ws = swa_w_in.shape[2]
    HQ = swa_sinks.shape[1]
    WQ = HQ * HEAD_DIM
    KVW = (ws * N_DEV - 2 * WQ) // 2
    HKV = KVW // HEAD_DIM
    rows_o = fox_w_out.shape[1]
    assert wf * N_DEV == 4 * W + H and rows_o * N_DEV == W and H <= LANE and HQ <= LANE
    me = _my_index()

    _, sw_f, np_f = _slab_geom(wf)
    _, sw_s, np_s = _slab_geom(ws)

    def slab(w2d, w, sw):
        off = (w * me) % LANE
        return lax.dynamic_update_slice(jnp.zeros((w2d.shape[0], sw), BF16), w2d.astype(BF16), (0, off))

    fi_all, si_all, fo_all, so_all = _all_gather([
        slab(fox_w_in[0], wf, sw_f), slab(swa_w_in[0], ws, sw_s),
        fox_w_out[0].astype(BF16), swa_w_out[0].astype(BF16)])
    w_fi = _assemble(fi_all, wf)
    w_si = _assemble(si_all, ws)
    w_fo = fo_all.reshape(W, D)
    w_so = so_all.reshape(WQ, D)

    x0 = x[0]
    g0, g1, gf = norm_g[0:1], norm_g[1:2], final_g[None, :]
    bias = _pad_lanes(fox_b_f, 0, LANE)
    sinks = _pad_lanes(swa_sinks, 0, LANE)
    tab_q = _rope_tables(S, 8 * HEAD_DIM)
    tab_k = tab_q[:, :, :LANE]

    h0 = _rmsnorm_fwd(x0, g0, "rmsnorm0")
    qkv0 = _proj(h0, w_fi, 0, 3 * W, BF16, "fox_in_qkv")
    gate0 = _proj(h0, w_fi, 3 * W, W, F32, "fox_in_gate")
    fl = _proj(h0, w_fi, 4 * W, LANE, F32, "fox_in_f")
    c, ct = _fox_gate_fwd(fl, bias)
    y0, o0, a0 = _fox_fwd(qkv0, gate0, c, ct, H)
    x1 = _out_proj_res(y0, w_fo, x0, "fox_out")

    h1 = _rmsnorm_fwd(x1, g1, "rmsnorm1")
    qkv1 = _proj(h1, w_si, 0, WQ + 2 * KVW, BF16, "swa_in_qkv", rope=(tab_k, WQ + KVW))
    gate1 = _proj(h1, w_si, WQ + 2 * KVW, WQ, F32, "swa_in_gate")
    y1, o1 = _swa_fwd(qkv1, gate1, sinks, HQ, HKV)
    x2 = _out_proj_res(y1, w_so, x1, "swa_out")

    dx2, dx2b, dgf, loss_p = _loss_head(x2, loss_target[0], gf)

    dy1 = _matmul_nt([(dx2b, None, 0)], w_so, WQ, "swa_out_bwd")
    g_so = _matmul_tn(y1, [(dx2b, None, 0)], D, "swa_out_wgrad")
    dqg1, dkv1, dsink = _swa_bwd(qkv1, dy1, gate1, o1, sinks, tab_q, HQ, HKV)
    dkv1f = _swa_dkv_finish(dkv1, tab_k)
    parts1 = [(dqg1, 0, 0), (dkv1f, None, WQ), (dqg1, 1, WQ + 2 * KVW)]
    g_si = _matmul_tn(h1, parts1, np_s, "swa_in_wgrad", tile_major=True)
    dh1 = _matmul_nt(parts1, w_si, D, "swa_in_bwd")
    dx1, dx1b, dg1 = _rmsnorm_bwd(dh1, x1, g1, dx2, "rmsnorm1_bwd")

    dy0 = _matmul_nt([(dx1b, None, 0)], w_fo, W, "fox_out_bwd")
    g_fo = _matmul_tn(y0, [(dx1b, None, 0)], D, "fox_out_wgrad")
    dqkvg0, dcs, dcr = _fox_bwd(qkv0, dy0, gate0, o0, a0, ct, H)
    dcs_t = _pad_lanes(dcs[:, :2, :].reshape(H, S).T, 0, LANE)
    dfl, dbf = _fox_gate_bwd(fl, bias, dcr - dcs_t)
    parts0 = [(dqkvg0, p, p * W) for p in range(4)] + [(dfl, None, 4 * W)]
    g_fi = _matmul_tn(h0, parts0, np_f, "fox_in_wgrad", tile_major=True)
    dh0 = _matmul_nt(parts0, w_fi, D, "fox_in_bwd")
    dx0, _, dg0 = _rmsnorm_bwd(dh0, x0, g0, dx1, "rmsnorm0_bwd")

    specs = [("col", wf), ("col", ws), ("row", rows_o), ("row", rows_o)]
    own, recv = _reduce_scatter_stage1([g_fi, g_si, g_fo, g_so], specs)
    flat = lambda t: t.reshape(t.shape[0], -1, t.shape[-1])
    sums = [_pair_sum(flat(o_), flat(r_)) for o_, r_ in zip(own, recv)]
    recv2 = _reduce_scatter_stage2([s_[1] for s_ in sums])
    red_fi = _final_sum_cols(sums[0][0], recv2[0], D)
    red_si = _final_sum_cols(sums[1][0], recv2[1], D)
    gw_fi = lax.dynamic_slice(red_fi, (0, (wf * me) % LANE), (D, wf))
    gw_si = lax.dynamic_slice(red_si, (0, (ws * me) % LANE), (D, ws))
    gw_fo, gw_so = _final_sum(sums[2][0], recv2[2]), _final_sum(sums[3][0], recv2[3])

    P = D
    dsink_v = dsink[:, :, 0].reshape(1, HQ)
    row3 = _pad_lanes(dbf[:, :H], 0, P) + _pad_lanes(dsink_v, LANE, P) + _pad_lanes(loss_p[:, :1], 2 * LANE, P)
    pack = _pad_rows(dg0, 0) + _pad_rows(dg1, 1) + _pad_rows(dgf, 2) + _pad_rows(row3, 3)
    tot = _all_reduce_small(pack)
    loss = tot[3, 2 * LANE]
    g_norm = tot[0:2]
    g_final = tot[2]
    g_bf = tot[3:4, 0:H]
    g_sinks = tot[3:4, LANE:LANE + HQ]

    def small_pack(ng, fg, bf, sk):
        r3 = _pad_lanes(bf, 0, P) + _pad_lanes(sk, LANE, P)
        return _pad_rows(ng, 0) + _pad_rows(fg[None, :], 2) + _pad_rows(r3, 3)

    sd, sm, sv = _adamw(small_pack(norm_g, final_g, fox_b_f, swa_sinks), tot,
                        small_pack(m_norm_g, m_final_g, m_fox_b_f, m_swa_sinks),
                        small_pack(v_norm_g, v_final_g, v_fox_b_f, v_swa_sinks), "adamw_small")

    def unpack(t):
        return t[0:2], t[3:4, 0:H], t[3:4, LANE:LANE + HQ], t[2]

    d_fi, m_fi, v_fi = _adamw(fox_w_in[0], gw_fi, m_fox_w_in[0], v_fox_w_in[0], "adamw_fox_in")
    d_fo, m_fo, v_fo = _adamw(fox_w_out[0], gw_fo, m_fox_w_out[0], v_fox_w_out[0], "adamw_fox_out")
    d_si, m_si, v_si = _adamw(swa_w_in[0], gw_si, m_swa_w_in[0], v_swa_w_in[0], "adamw_swa_in")
    d_so, m_so, v_so = _adamw(swa_w_out[0], gw_so, m_swa_w_out[0], v_swa_w_out[0], "adamw_swa_out")

    def group(small, fi, fo, si, so):
        ng, bf, sk, fg = unpack(small)
        return (ng, fi[None], bf, fo[None], si[None], sk, so[None], fg)

    grads = (g_norm, gw_fi[None], g_bf, gw_fo[None], gw_si[None], g_sinks, gw_so[None], g_final)
    return (loss, dx0[None], *grads, *group(sd, d_fi, d_fo, d_si, d_so),
            *group(sm, m_fi, m_fo, m_si, m_so), *group(sv, v_fi, v_fo, v_si, v_so))
```

```python
import functools
import math

import jax
import jax.numpy as jnp
from jax import lax
from jax.experimental import pallas as pl
from jax.experimental.pallas import tpu as pltpu

F32 = jnp.float32
BF16 = jnp.bfloat16
MESH = pl.DeviceIdType.MESH

N_DEV = 8
LANE = 128
HEAD_DIM = 64
SWA_BLOCK = 128
NEG_INF = -1e30
RMS_EPS = 1e-6
ROPE_THETA = 500000.0
ROT_DIM = HEAD_DIM // 4
ADAM_LR, ADAM_B1, ADAM_B2, ADAM_EPS, ADAM_WD, ADAM_STEP = 0.001, 0.9, 0.999, 1e-08, 0.01, 10
VMEM_LIMIT = 56 * 1024 * 1024
MM_TILE = 1024


def _cparams(**kw):
    return pltpu.CompilerParams(vmem_limit_bytes=VMEM_LIMIT, **kw)


def _tile(n, cap):
    if n <= cap:
        return n
    t = (cap // LANE) * LANE
    while t > LANE and n % t:
        t -= LANE
    assert n % t == 0, (n, cap)
    return t


def _row_tile(n, cap):
    t = min(n, cap)
    while n % t:
        t //= 2
    return t


def _dot_nn(a, b):
    return jnp.dot(a, b, preferred_element_type=F32)


def _dot_nt(a, b):
    return lax.dot_general(a, b, (((1,), (1,)), ((), ())), preferred_element_type=F32)


def _dot_tn(a, b):
    return lax.dot_general(a, b, (((0,), (0,)), ((), ())), preferred_element_type=F32)


def _sigmoid(g):
    return 1.0 / (1.0 + jnp.exp(-g))


def _slab_geom(w):
    starts = [w * i for i in range(N_DEV)]
    aligned = [LANE * (s // LANE) for s in starts]
    offs = [s - a for s, a in zip(starts, aligned)]
    sw = LANE * (-(-(max(offs) + w) // LANE))
    return aligned, sw, aligned[-1] + sw


def _my_index():
    return 4 * lax.axis_index("x") + 2 * lax.axis_index("y") + lax.axis_index("c")


def _all_gather(arrs):
    n = len(arrs)

    def body(*refs):
        ins, outs = refs[:n], refs[n:2 * n]
        send_sems, recv_sems, local_sems = refs[2 * n:]
        x, y, c = lax.axis_index("x"), lax.axis_index("y"), lax.axis_index("c")
        me, sib = (x, y, c), (x, y, 1 - c)
        chips = [(1 - x, y), (x, 1 - y), (1 - x, 1 - y)]

        def idx(px, py, pc):
            return 4 * px + 2 * py + pc

        def copy(a, k, block, to, src=None):
            dst = outs[a].at[idx(*block)]
            return pltpu.make_async_remote_copy(
                src_ref=dst if src is None else src, dst_ref=dst,
                send_sem=send_sems.at[a, k], recv_sem=recv_sems.at[a, k],
                device_id=to, device_id_type=MESH)

        mine = [pltpu.make_async_copy(ins[a], outs[a].at[idx(*me)], local_sems.at[a]) for a in range(n)]
        for m in mine:
            m.start()
        first = []
        for a in range(n):
            first.append(copy(a, 0, me, sib, src=ins[a]))
            for j, chip in enumerate(chips):
                first.append(copy(a, 1 + j, me, (*chip, c), src=ins[a]))
        for cp in first:
            cp.start()
        passed = []
        for j, chip in enumerate(chips):
            for a in range(n):
                copy(a, 1 + j, (*chip, c), me).wait_recv()
                p = copy(a, 4 + j, (*chip, c), sib)
                p.start()
                passed.append(p)
        for a in range(n):
            copy(a, 0, sib, me).wait_recv()
        for j, chip in enumerate(chips):
            for a in range(n):
                copy(a, 4 + j, (*chip, 1 - c), me).wait_recv()
        for cp in first + passed:
            cp.wait_send()
        for m in mine:
            m.wait()

    any_spec = pl.BlockSpec(memory_space=pl.ANY)
    return pl.pallas_call(
        body, name="weights_all_gather",
        out_shape=[jax.ShapeDtypeStruct((N_DEV,) + a.shape, a.dtype) for a in arrs],
        in_specs=[any_spec] * n, out_specs=[any_spec] * n,
        scratch_shapes=[pltpu.SemaphoreType.DMA((n, 7)), pltpu.SemaphoreType.DMA((n, 7)),
                        pltpu.SemaphoreType.DMA((n,))],
    )(*arrs)


def _rel_chip(r):
    x, y = lax.axis_index("x"), lax.axis_index("y")
    return (x ^ (r >> 1), y ^ (r & 1))


def _rs_windows(specs):
    def window(ref, spec, blk):
        kind, n = spec
        if kind == "col":
            _, sw, _ = _slab_geom(n)
            return ref.at[pl.ds((n * blk) // LANE, sw // LANE)]
        start = pl.multiple_of(n * blk, n)
        return ref.at[pl.ds(start, n), :]
    return window


def _reduce_scatter_stage1(grads, specs):
    n = len(grads)
    window = _rs_windows(specs)

    def blk_shape(g, spec):
        kind, w = spec
        return (_slab_geom(w)[1] // LANE, g.shape[1], LANE) if kind == "col" else (w, g.shape[1])

    shapes = [blk_shape(g, s) for g, s in zip(grads, specs)]

    def body(*refs):
        ins, recvs = refs[:n], refs[n:2 * n]
        send_sems, recv_sems = refs[2 * n:]
        x, y, c = lax.axis_index("x"), lax.axis_index("y"), lax.axis_index("c")
        sib = (x, y, 1 - c)
        remotes = []
        for a in range(n):
            for r in range(4):
                px, py = _rel_chip(r)
                sib_blk = 4 * px + 2 * py + (1 - c)
                remotes.append(pltpu.make_async_remote_copy(
                    src_ref=window(ins[a], specs[a], sib_blk), dst_ref=recvs[a].at[r],
                    send_sem=send_sems.at[a, r], recv_sem=recv_sems.at[a, r],
                    device_id=sib, device_id_type=MESH))
        for cp in remotes:
            cp.start()
        for cp in remotes:
            cp.wait_recv()
        for cp in remotes:
            cp.wait_send()

    any_spec = pl.BlockSpec(memory_space=pl.ANY)
    return pl.pallas_call(
        body, name="grads_rs_sibling",
        out_shape=[jax.ShapeDtypeStruct((4,) + s, F32) for s in shapes],
        in_specs=[any_spec] * n, out_specs=[any_spec] * n,
        scratch_shapes=[pltpu.SemaphoreType.DMA((n, 4)), pltpu.SemaphoreType.DMA((n, 4))],
    )(*grads)


def _own_block_offsets(spec):
    kind, n = spec
    c = lax.axis_index("c")
    offs = []
    for r in range(4):
        px, py = _rel_chip(r)
        blk = 4 * px + 2 * py + c
        offs.append((n * blk) // LANE if kind == "col" else blk)
    return jnp.stack(offs).astype(jnp.int32)


def _pair_sum(g, recv, spec):
    kind, _ = spec
    offs = _own_block_offsets(spec)
    if kind == "col":
        _, T, M, _ = recv.shape
        grid = (4, T)
        g_spec = pl.BlockSpec((1, M, LANE), lambda r, t, o: (o[r] + t, 0, 0))
        r_spec = pl.BlockSpec((1, 1, M, LANE), lambda r, t, o: (r, t, 0, 0))
    else:
        _, nrow, C = recv.shape
        grid = (4,)
        g_spec = pl.BlockSpec((nrow, C), lambda r, o: (o[r], 0))
        r_spec = pl.BlockSpec((1, nrow, C), lambda r, o: (r, 0, 0))

    def body(o_ref, g_ref, r_ref, f_ref, h_ref):
        if kind == "col":
            s = g_ref[0] + r_ref[0, 0]
            f_ref[0, 0] = s
            h_ref[0, 0] = s.astype(BF16)
        else:
            s = g_ref[...] + r_ref[0]
            f_ref[0] = s
            h_ref[0] = s.astype(BF16)

    return pl.pallas_call(
        body, name="grads_pair_sum",
        grid_spec=pltpu.PrefetchScalarGridSpec(num_scalar_prefetch=1, grid=grid, in_specs=[g_spec, r_spec],
                                               out_specs=[r_spec, r_spec]),
        out_shape=[jax.ShapeDtypeStruct(recv.shape, F32), jax.ShapeDtypeStruct(recv.shape, BF16)],
        compiler_params=_cparams(),
    )(offs, g, recv)


def _reduce_scatter_stage2(parts):
    n = len(parts)

    def body(*refs):
        ins, recvs = refs[:n], refs[n:2 * n]
        send_sems, recv_sems = refs[2 * n:]
        c = lax.axis_index("c")
        copies = []
        for a in range(n):
            for r in range(1, 4):
                px, py = _rel_chip(r)
                copies.append(pltpu.make_async_remote_copy(
                    src_ref=ins[a].at[r], dst_ref=recvs[a].at[r - 1],
                    send_sem=send_sems.at[a, r - 1], recv_sem=recv_sems.at[a, r - 1],
                    device_id=(px, py, c), device_id_type=MESH))
        for cp in copies:
            cp.start()
        for cp in copies:
            cp.wait_recv()
        for cp in copies:
            cp.wait_send()

    any_spec = pl.BlockSpec(memory_space=pl.ANY)
    return pl.pallas_call(
        body, name="grads_rs_chips",
        out_shape=[jax.ShapeDtypeStruct((3,) + p.shape[1:], BF16) for p in parts],
        in_specs=[any_spec] * n, out_specs=[any_spec] * n,
        scratch_shapes=[pltpu.SemaphoreType.DMA((n, 3)), pltpu.SemaphoreType.DMA((n, 3))],
    )(*parts)


def _final_sum(psum, recv):
    _, R, C = psum.shape
    tr = _row_tile(R, 256)

    def body(p_ref, r_ref, o_ref):
        r = r_ref[...].astype(F32)
        o_ref[...] = ((p_ref[0] + r[0]) + r[1]) + r[2]

    return pl.pallas_call(
        body, name="grads_final_sum", grid=(R // tr,),
        in_specs=[pl.BlockSpec((1, tr, C), lambda i: (0, i, 0)), pl.BlockSpec((3, tr, C), lambda i: (0, i, 0))],
        out_specs=pl.BlockSpec((tr, C), lambda i: (i, 0)),
        out_shape=jax.ShapeDtypeStruct((R, C), F32),
        compiler_params=_cparams(),
    )(psum, recv)


def _final_sum_cols(psum, recv):
    _, T, M, _ = psum.shape

    def body(p_ref, r_ref, o_ref):
        r = r_ref[...].astype(F32)
        o_ref[...] = ((p_ref[0, 0] + r[0, 0]) + r[1, 0]) + r[2, 0]

    return pl.pallas_call(
        body, name="grads_final_sum_cols", grid=(T,),
        in_specs=[pl.BlockSpec((1, 1, M, LANE), lambda t: (0, t, 0, 0)),
                  pl.BlockSpec((3, 1, M, LANE), lambda t: (0, t, 0, 0))],
        out_specs=pl.BlockSpec((M, LANE), lambda t: (0, t)),
        out_shape=jax.ShapeDtypeStruct((M, T * LANE), F32),
        compiler_params=_cparams(),
    )(psum, recv)


def _all_reduce_small(pack):
    R, P = pack.shape

    def body(x_ref, o_ref, gat_ref, send_sems, recv_sems):
        x, y, c = lax.axis_index("x"), lax.axis_index("y"), lax.axis_index("c")
        me = 4 * x + 2 * y + c
        gat_ref[me] = x_ref[...]
        copies = []
        for k in range(1, N_DEV):
            peer = (x ^ (k >> 2), y ^ ((k >> 1) & 1), c ^ (k & 1))
            copies.append(pltpu.make_async_remote_copy(
                src_ref=x_ref, dst_ref=gat_ref.at[me],
                send_sem=send_sems.at[k - 1], recv_sem=recv_sems.at[k - 1],
                device_id=peer, device_id_type=MESH))
        for cp in copies:
            cp.start()
        for cp in copies:
            cp.wait_recv()
        for cp in copies:
            cp.wait_send()
        acc = gat_ref[0]
        for d in range(1, N_DEV):
            acc = acc + gat_ref[d]
        o_ref[...] = acc

    vm = pl.BlockSpec(memory_space=pltpu.VMEM)
    return pl.pallas_call(
        body, name="small_all_reduce",
        out_shape=jax.ShapeDtypeStruct((R, P), F32),
        in_specs=[vm], out_specs=vm,
        scratch_shapes=[pltpu.VMEM((N_DEV, R, P), F32),
                        pltpu.SemaphoreType.DMA((N_DEV - 1,)), pltpu.SemaphoreType.DMA((N_DEV - 1,))],
    )(pack)


def _assemble(slabs, w):
    aligned, sw, total = _slab_geom(w)
    K = slabs.shape[1]
    tr = _row_tile(K, 256)

    def body(s_ref, o_ref):
        o_ref[...] = jnp.zeros(o_ref.shape, BF16)
        for i in range(N_DEV):
            a = aligned[i]
            o_ref[:, a:a + sw] = o_ref[:, a:a + sw] + s_ref[i]

    return pl.pallas_call(
        body, name="assemble_w_in", grid=(K // tr,),
        in_specs=[pl.BlockSpec((N_DEV, tr, sw), lambda i: (0, i, 0))],
        out_specs=pl.BlockSpec((tr, total), lambda i: (i, 0)),
        out_shape=jax.ShapeDtypeStruct((K, total), BF16),
        compiler_params=_cparams(),
    )(slabs)


def _rmsnorm_fwd(x, g, name):
    S, D = x.shape
    tm = _row_tile(S, 256)

    def body(x_ref, g_ref, h_ref):
        xv = x_ref[...]
        r = lax.rsqrt(jnp.mean(xv * xv, axis=-1, keepdims=True) + RMS_EPS)
        h_ref[...] = ((xv * r) * g_ref[...]).astype(BF16)

    return pl.pallas_call(
        body, name=name, grid=(S // tm,),
        in_specs=[pl.BlockSpec((tm, D), lambda i: (i, 0)), pl.BlockSpec((1, D), lambda i: (0, 0))],
        out_specs=pl.BlockSpec((tm, D), lambda i: (i, 0)),
        out_shape=jax.ShapeDtypeStruct((S, D), BF16),
        compiler_params=_cparams(),
    )(x, g)


def _rmsnorm_bwd(dh, x, g, dres, name):
    S, D = x.shape
    tm = _row_tile(S, 256)

    def body(dh_ref, x_ref, g_ref, dr_ref, dx_ref, dxb_ref, dg_ref):
        xv = x_ref[...]
        r = lax.rsqrt(jnp.mean(xv * xv, axis=-1, keepdims=True) + RMS_EPS)
        xhat = xv * r
        d = dh_ref[...]
        gd = d * g_ref[...]
        dx = r * (gd - xhat * jnp.mean(gd * xhat, axis=-1, keepdims=True)) + dr_ref[...]
        dx_ref[...] = dx
        dxb_ref[...] = dx.astype(BF16)

        @pl.when(pl.program_id(0) == 0)
        def _():
            dg_ref[...] = jnp.zeros(dg_ref.shape, F32)
        dg_ref[...] += jnp.sum(d * xhat, axis=0, keepdims=True)

    row = pl.BlockSpec((tm, D), lambda i: (i, 0))
    vec = pl.BlockSpec((1, D), lambda i: (0, 0))
    return pl.pallas_call(
        body, name=name, grid=(S // tm,),
        in_specs=[row, row, vec, row], out_specs=[row, row, vec],
        out_shape=[jax.ShapeDtypeStruct((S, D), F32), jax.ShapeDtypeStruct((S, D), BF16),
                   jax.ShapeDtypeStruct((1, D), F32)],
        compiler_params=_cparams(),
    )(dh, x, g, dres)


def _loss_head(x, tgt, g):
    S, D = x.shape
    tm = _row_tile(S, 256)

    def body(x_ref, t_ref, g_ref, dx_ref, dxb_ref, dg_ref, loss_ref):
        xv = x_ref[...]
        r = lax.rsqrt(jnp.mean(xv * xv, axis=-1, keepdims=True) + RMS_EPS)
        xhat = xv * r
        gv = g_ref[...]
        err = xhat * gv - t_ref[...]
        d = err * (1.0 / D)
        gd = d * gv
        dx = r * (gd - xhat * jnp.mean(gd * xhat, axis=-1, keepdims=True))
        dx_ref[...] = dx
        dxb_ref[...] = dx.astype(BF16)

        @pl.when(pl.program_id(0) == 0)
        def _():
            dg_ref[...] = jnp.zeros(dg_ref.shape, F32)
            loss_ref[...] = jnp.zeros(loss_ref.shape, F32)
        dg_ref[...] += jnp.sum(d * xhat, axis=0, keepdims=True)
        per_tok = jnp.sum(err * err, axis=-1, keepdims=True) * (1.0 / D)
        loss_ref[...] += 0.5 * jnp.sum(per_tok, axis=0, keepdims=True)

    row = pl.BlockSpec((tm, D), lambda i: (i, 0))
    vec = pl.BlockSpec((1, D), lambda i: (0, 0))
    return pl.pallas_call(
        body, name="loss_head", grid=(S // tm,),
        in_specs=[row, row, vec],
        out_specs=[row, row, vec, pl.BlockSpec((1, LANE), lambda i: (0, 0))],
        out_shape=[jax.ShapeDtypeStruct((S, D), F32), jax.ShapeDtypeStruct((S, D), BF16),
                   jax.ShapeDtypeStruct((1, D), F32), jax.ShapeDtypeStruct((1, LANE), F32)],
        compiler_params=_cparams(),
    )(x, tgt, g)


def _adamw(w, g, m, v, name):
    R, C = w.shape
    tr = _row_tile(R, 256)
    c1 = 1.0 - ADAM_B1 ** ADAM_STEP
    c2 = 1.0 - ADAM_B2 ** ADAM_STEP

    def body(w_ref, g_ref, m_ref, v_ref, d_ref, nm_ref, nv_ref):
        gv = g_ref[...]
        nm = ADAM_B1 * m_ref[...] + (1.0 - ADAM_B1) * gv
        nv = ADAM_B2 * v_ref[...] + (1.0 - ADAM_B2) * (gv * gv)
        d_ref[...] = -ADAM_LR * ((nm / c1) / (jnp.sqrt(nv / c2) + ADAM_EPS) + ADAM_WD * w_ref[...])
        nm_ref[...] = nm
        nv_ref[...] = nv

    spec = pl.BlockSpec((tr, C), lambda i: (i, 0))
    return pl.pallas_call(
        body, name=name, grid=(R // tr,),
        in_specs=[spec] * 4, out_specs=[spec] * 3,
        out_shape=[jax.ShapeDtypeStruct((R, C), F32)] * 3,
        compiler_params=_cparams(),
    )(w, g, m, v)


def _proj(h, wfull, col0, ncols, out_dtype, name, rope=None):
    S, K = h.shape
    tm = _row_tile(S, MM_TILE)
    tn = math.gcd(_tile(ncols, MM_TILE), col0) if col0 else _tile(ncols, MM_TILE)
    if rope is not None:
        tn = _tile(math.gcd(ncols, rope[1]), MM_TILE)
    assert ncols % tn == 0 and col0 % tn == 0
    cb = col0 // tn

    def body(*refs):
        if rope is None:
            a_ref, b_ref, o_ref = refs
        else:
            a_ref, b_ref, t_ref, o_ref = refs
        acc = _dot_nn(a_ref[...], b_ref[...])
        if rope is not None:
            t0, t1, t2 = (jnp.tile(t_ref[i], (1, tn // LANE)) for i in range(3))
            roped = (acc * t0 + pltpu.roll(acc, tn - ROT_DIM // 2, axis=1) * t1
                     + pltpu.roll(acc, ROT_DIM // 2, axis=1) * t2)
            acc = jnp.where(pl.program_id(1) < rope[1] // tn, roped, acc)
        o_ref[...] = acc.astype(out_dtype)

    in_specs = [pl.BlockSpec((tm, K), lambda i, j: (i, 0)), pl.BlockSpec((K, tn), lambda i, j: (0, cb + j))]
    args = [h, wfull]
    if rope is not None:
        in_specs.append(pl.BlockSpec((3, tm, LANE), lambda i, j: (0, i, 0)))
        args.append(rope[0])
    return pl.pallas_call(
        body, name=name, grid=(S // tm, ncols // tn),
        in_specs=in_specs, out_specs=pl.BlockSpec((tm, tn), lambda i, j: (i, j)),
        out_shape=jax.ShapeDtypeStruct((S, ncols), out_dtype),
        compiler_params=_cparams(),
    )(*args)


def _out_proj_res(y, wo, xres, name):
    S, W = y.shape
    D = wo.shape[1]
    tm, tn = _row_tile(S, MM_TILE), _tile(D, MM_TILE)

    def body(a_ref, b_ref, r_ref, o_ref):
        o_ref[...] = r_ref[...] + _dot_nn(a_ref[...], b_ref[...])

    return pl.pallas_call(
        body, name=name, grid=(S // tm, D // tn),
        in_specs=[pl.BlockSpec((tm, W), lambda i, j: (i, 0)), pl.BlockSpec((W, tn), lambda i, j: (0, j)),
                  pl.BlockSpec((tm, tn), lambda i, j: (i, j))],
        out_specs=pl.BlockSpec((tm, tn), lambda i, j: (i, j)),
        out_shape=jax.ShapeDtypeStruct((S, D), F32),
        compiler_params=_cparams(),
    )(y, wo, xres)


def _matmul_nt(parts, wfull, out_rows, name):
    S = parts[0][0].shape[-2]
    tm, tn = _row_tile(S, MM_TILE), _tile(out_rows, MM_TILE)
    plan, lo = [], 0
    for arr, lead, col0 in parts:
        n_p = arr.shape[-1]
        tk = math.gcd(_tile(n_p, 1024), col0) if col0 else _tile(n_p, 1024)
        steps = n_p // tk
        plan.append((lead, col0 // tk, tk, lo, lo + steps))
        lo += steps
    nk = lo
    npart = len(parts)

    def body(*refs):
        a_refs, w_refs = refs[:npart], refs[npart:2 * npart]
        o_ref, acc_ref = refs[2 * npart], refs[2 * npart + 1]
        k = pl.program_id(2)

        @pl.when(k == 0)
        def _():
            acc_ref[...] = jnp.zeros(acc_ref.shape, F32)
        for p, (_, _, _, lo_p, hi_p) in enumerate(plan):
            @pl.when((k >= lo_p) & (k < hi_p))
            def _(p=p):
                acc_ref[...] += _dot_nt(a_refs[p][...], w_refs[p][...])

        @pl.when(k == nk - 1)
        def _():
            o_ref[...] = acc_ref[...]

    in_specs, args = [], []
    for (arr, lead, col0), (_, cb, tk, lo_p, hi_p) in zip(parts, plan):
        def kk(k, lo_p=lo_p, hi_p=hi_p):
            return jnp.clip(k - lo_p, 0, hi_p - lo_p - 1)
        if lead is None:
            in_specs.append(pl.BlockSpec((tm, tk), lambda i, j, k, kk=kk: (i, kk(k))))
        else:
            in_specs.append(pl.BlockSpec((None, tm, tk), lambda i, j, k, kk=kk, lead=lead: (lead, i, kk(k))))
        args.append(arr)
    for (_, cb, tk, lo_p, hi_p) in plan:
        def kk(k, lo_p=lo_p, hi_p=hi_p):
            return jnp.clip(k - lo_p, 0, hi_p - lo_p - 1)
        in_specs.append(pl.BlockSpec((tn, tk), lambda i, j, k, kk=kk, cb=cb: (j, cb + kk(k))))
        args.append(wfull)
    return pl.pallas_call(
        body, name=name, grid=(S // tm, out_rows // tn, nk),
        in_specs=in_specs, out_specs=pl.BlockSpec((tm, tn), lambda i, j, k: (i, j)),
        out_shape=jax.ShapeDtypeStruct((S, out_rows), F32),
        scratch_shapes=[pltpu.VMEM((tm, tn), F32)],
        compiler_params=_cparams(),
    )(*args)


def _matmul_tn(a, parts, total, name, tile_major=False):
    S, M = a.shape
    tm, ts = _tile(M, MM_TILE), _row_tile(S, 512)
    out = None
    for idx, (arr, lead, col0) in enumerate(parts):
        n_p = arr.shape[-1]
        tn = math.gcd(_tile(n_p, MM_TILE), col0) if col0 else _tile(n_p, MM_TILE)
        cb = col0 // tn
        nk = S // ts

        def body(*refs, nk=nk, tn=tn):
            a_ref, b_ref = refs[0], refs[1]
            o_ref, acc_ref = refs[-2], refs[-1]
            k = pl.program_id(2)

            @pl.when(k == 0)
            def _():
                acc_ref[...] = jnp.zeros(acc_ref.shape, F32)
            acc_ref[...] += _dot_tn(a_ref[...], b_ref[...])

            @pl.when(k == nk - 1)
            def _():
                if tile_major:
                    for t in range(tn // LANE):
                        o_ref[t] = acc_ref[:, LANE * t:LANE * (t + 1)]
                else:
                    o_ref[...] = acc_ref[...]

        in_specs = [pl.BlockSpec((ts, tm), lambda i, j, k: (k, i))]
        if lead is None:
            in_specs.append(pl.BlockSpec((ts, tn), lambda i, j, k: (k, j)))
        else:
            in_specs.append(pl.BlockSpec((None, ts, tn), lambda i, j, k, lead=lead: (lead, k, j)))
        args = [a, arr]
        aliases = {}
        if out is not None:
            in_specs.append(pl.BlockSpec(memory_space=pl.ANY))
            args.append(out)
            aliases = {2: 0}
        if tile_major:
            out_spec = pl.BlockSpec((tn // LANE, tm, LANE), lambda i, j, k, cb=cb: (cb + j, i, 0))
            out_shape = jax.ShapeDtypeStruct((total // LANE, M, LANE), F32)
        else:
            out_spec = pl.BlockSpec((tm, tn), lambda i, j, k, cb=cb: (i, cb + j))
            out_shape = jax.ShapeDtypeStruct((M, total), F32)
        out = pl.pallas_call(
            body, name=f"{name}_{idx}", grid=(M // tm, n_p // tn, nk),
            in_specs=in_specs, out_specs=out_spec, out_shape=out_shape,
            scratch_shapes=[pltpu.VMEM((tm, tn), F32)],
            input_output_aliases=aliases,
            compiler_params=_cparams(),
        )(*args)
    return out


def _log_sigmoid(z):
    e = jnp.exp(-jnp.abs(z))
    return jnp.minimum(z, 0.0) - jnp.where(e < 1e-4, e * (1.0 - 0.5 * e), jnp.log(1.0 + e))


def _fox_gate_fwd(fl, bias):
    S = fl.shape[0]

    def body(f_ref, b_ref, c_ref, ct_ref):
        row = lax.broadcasted_iota(jnp.int32, (8, LANE), 0)

        def step(i, carry):
            r0 = pl.multiple_of(i * 8, 8)
            t = _log_sigmoid(f_ref[pl.ds(r0, 8), :] + b_ref[...])
            for sh in (1, 2, 4):
                t = t + jnp.where(row >= sh, pltpu.roll(t, sh, axis=0), 0.0)
            t = t + carry
            c_ref[pl.ds(r0, 8), :] = t
            return jnp.sum(jnp.where(row == 7, t, 0.0), axis=0, keepdims=True)

        lax.fori_loop(0, S // 8, step, jnp.zeros((1, LANE), F32))
        ct_ref[...] = c_ref[...].T

    vm = pl.BlockSpec(memory_space=pltpu.VMEM)
    return pl.pallas_call(
        body, name="fox_gate_fwd", in_specs=[vm, vm], out_specs=[vm, vm],
        out_shape=[jax.ShapeDtypeStruct((S, LANE), F32), jax.ShapeDtypeStruct((LANE, S), F32)],
        compiler_params=_cparams(),
    )(fl, bias)


def _fox_gate_bwd(fl, bias, dc):
    S = fl.shape[0]

    def body(f_ref, b_ref, d_ref, o_ref, db_ref, acc_ref):
        row = lax.broadcasted_iota(jnp.int32, (8, LANE), 0)
        nt = S // 8

        def step(ii, carry):
            carry_c, carry_b = carry
            r0 = pl.multiple_of((nt - 1 - ii) * 8, 8)
            t = d_ref[pl.ds(r0, 8), :]
            for sh in (1, 2, 4):
                t = t + jnp.where(row < 8 - sh, pltpu.roll(t, 8 - sh, axis=0), 0.0)
            t = t + carry_c
            z = f_ref[pl.ds(r0, 8), :] + b_ref[...]
            dz = t * _sigmoid(-z)
            acc_ref[pl.ds(r0, 8), :] = dz
            first = jnp.sum(jnp.where(row == 0, t, 0.0), axis=0, keepdims=True)
            return first, carry_b + jnp.sum(dz, axis=0, keepdims=True)

        zero = jnp.zeros((1, LANE), F32)
        _, db = lax.fori_loop(0, nt, step, (zero, zero))
        db_ref[...] = db
        o_ref[...] = acc_ref[...].astype(BF16)

    vm = pl.BlockSpec(memory_space=pltpu.VMEM)
    return pl.pallas_call(
        body, name="fox_gate_bwd", in_specs=[vm, vm, vm], out_specs=[vm, vm],
        out_shape=[jax.ShapeDtypeStruct((S, LANE), BF16), jax.ShapeDtypeStruct((1, LANE), F32)],
        scratch_shapes=[pltpu.VMEM((S, LANE), F32)],
        compiler_params=_cparams(),
    )(fl, bias, dc)


def _fox_fwd(qkv, gate, c, ct, H):
    S = qkv.shape[0]
    W = H * HEAD_DIM
    HP = H // 2
    tq = _row_tile(S, 512)
    nq = S // tq
    wb = W // LANE
    scale = HEAD_DIM ** -0.5

    def body(q_ref, k_ref, v_ref, g_ref, c_ref, ct_ref, y_ref, o_ref, a_ref, m_sc, l_sc, acc_sc):
        hp, qi = pl.program_id(0), pl.program_id(1)
        lane_q = lax.broadcasted_iota(jnp.int32, (tq, LANE), 1)
        lo_q = lane_q < HEAD_DIM
        rows = lax.broadcasted_iota(jnp.int32, (tq, tq), 0)
        cols = lax.broadcasted_iota(jnp.int32, (tq, tq), 1)
        q = q_ref[...] * jnp.asarray(scale, BF16)
        crow = c_ref[pl.ds(pl.multiple_of(qi * tq, tq), tq), :]
        qm = [jnp.where(lo_q, q, jnp.zeros_like(q)), jnp.where(lo_q, jnp.zeros_like(q), q)]
        ctq = [jnp.sum(jnp.where(lane_q == 2 * hp + e, crow, 0.0), axis=1, keepdims=True) for e in range(2)]
        m_sc[...] = jnp.full(m_sc.shape, NEG_INF, F32)
        l_sc[...] = jnp.zeros(l_sc.shape, F32)
        acc_sc[...] = jnp.zeros(acc_sc.shape, F32)

        def step(j, masked):
            k0 = pl.multiple_of(j * tq, tq)
            kblk = k_ref[pl.ds(k0, tq), :]
            vblk = v_ref[pl.ds(k0, tq), :]
            for e in range(2):
                cs = ct_ref[pl.ds(2 * hp + e, 1), pl.ds(k0, tq)]
                s = _dot_nt(qm[e], kblk) + (ctq[e] - cs)
                if masked:
                    s = jnp.where(rows >= cols, s, NEG_INF)
                m_prev = m_sc[e]
                m_new = jnp.maximum(m_prev, jnp.max(s, axis=1, keepdims=True))
                alpha = jnp.exp(m_prev - m_new)
                p = jnp.exp(s - m_new)
                l_sc[e] = alpha * l_sc[e] + jnp.sum(p, axis=1, keepdims=True)
                acc_sc[e] = alpha * acc_sc[e] + _dot_nn(p.astype(BF16), vblk)
                m_sc[e] = m_new

        def loop_body(j, carry):
            step(j, False)
            return carry

        lax.fori_loop(0, qi, loop_body, 0)
        step(qi, True)
        inv = [1.0 / l_sc[e] for e in range(2)]
        o = jnp.where(lo_q, acc_sc[0] * inv[0], acc_sc[1] * inv[1])
        a = [ctq[e] - (m_sc[e] + jnp.log(l_sc[e])) for e in range(2)]
        g = g_ref[...]
        y_ref[...] = (o * (g * _sigmoid(g))).astype(BF16)
        o_ref[...] = o.astype(BF16)
        a_ref[0] = jnp.where(lo_q, a[0], a[1])

    return pl.pallas_call(
        body, name="fox_attn_fwd", grid=(HP, nq),
        in_specs=[pl.BlockSpec((tq, LANE), lambda h, i: (i, h)),
                  pl.BlockSpec((S, LANE), lambda h, i: (0, wb + h)),
                  pl.BlockSpec((S, LANE), lambda h, i: (0, 2 * wb + h)),
                  pl.BlockSpec((tq, LANE), lambda h, i: (i, h)),
                  pl.BlockSpec((S, LANE), lambda h, i: (0, 0)),
                  pl.BlockSpec((LANE, S), lambda h, i: (0, 0))],
        out_specs=[pl.BlockSpec((tq, LANE), lambda h, i: (i, h)),
                   pl.BlockSpec((tq, LANE), lambda h, i: (i, h)),
                   pl.BlockSpec((1, tq, LANE), lambda h, i: (h, i, 0))],
        out_shape=[jax.ShapeDtypeStruct((S, W), BF16), jax.ShapeDtypeStruct((S, W), BF16),
                   jax.ShapeDtypeStruct((HP, S, LANE), F32)],
        scratch_shapes=[pltpu.VMEM((2, tq, 1), F32), pltpu.VMEM((2, tq, 1), F32), pltpu.VMEM((2, tq, LANE), F32)],
        compiler_params=_cparams(),
    )(qkv, qkv, qkv, gate, c, ct)


def _fox_bwd(qkv, dy, gate, o, a, ct, H):
    S = qkv.shape[0]
    W = H * HEAD_DIM
    HP = H // 2
    tq = _row_tile(S, 512)
    nq = S // tq
    wb = W // LANE
    scale = HEAD_DIM ** -0.5

    def body(q_ref, k_ref, v_ref, dy_ref, g_ref, o_ref, a_ref, ct_ref, out_ref, dcs_ref, dcr_ref,
             do_sc, delta_sc, dq_sc, dk_sc, dv_sc, drow_sc):
        hp, kj = pl.program_id(0), pl.program_id(1)
        lane = lax.broadcasted_iota(jnp.int32, (tq, LANE), 1)
        lo = lane < HEAD_DIM
        rows = lax.broadcasted_iota(jnp.int32, (tq, tq), 0)
        cols = lax.broadcasted_iota(jnp.int32, (tq, tq), 1)

        @pl.when(kj == 0)
        def _():
            def chunk(i, carry):
                r0 = pl.multiple_of(i * tq, tq)
                dyv = dy_ref[pl.ds(r0, tq), :]
                g = g_ref[pl.ds(r0, tq), :]
                ov = o_ref[pl.ds(r0, tq), :].astype(F32)
                sg = _sigmoid(g)
                dob = (dyv * (g * sg)).astype(BF16)
                out_ref[3, pl.ds(r0, tq), :] = (dyv * ov * (sg * (1.0 + g * (1.0 - sg)))).astype(BF16)
                do_sc[pl.ds(r0, tq), :] = dob
                prod = dob.astype(F32) * ov
                d0 = jnp.sum(jnp.where(lo, prod, 0.0), axis=1, keepdims=True)
                d1 = jnp.sum(jnp.where(lo, 0.0, prod), axis=1, keepdims=True)
                delta_sc[pl.ds(r0, tq), :] = jnp.where(lo, d0, d1)
                dq_sc[pl.ds(r0, tq), :] = jnp.zeros((tq, LANE), F32)
                drow_sc[pl.ds(r0, tq), :] = jnp.zeros((tq, LANE), F32)
                return carry
            lax.fori_loop(0, nq, chunk, 0)

        @pl.when((kj == 0) & (hp == 0))
        def _():
            dcr_ref[...] = jnp.zeros(dcr_ref.shape, F32)

        kblk = k_ref[...]
        vblk = v_ref[...]
        zb = jnp.zeros_like(kblk)
        km = [jnp.where(lo, kblk, zb), jnp.where(lo, zb, kblk)]
        vm = [jnp.where(lo, vblk, zb), jnp.where(lo, zb, vblk)]
        dk_sc[...] = jnp.zeros(dk_sc.shape, F32)
        dv_sc[...] = jnp.zeros(dv_sc.shape, F32)
        dcs_ref[...] = jnp.zeros(dcs_ref.shape, F32)

        def step(i, masked):
            r0 = pl.multiple_of(i * tq, tq)
            qt = q_ref[pl.ds(r0, tq), :] * jnp.asarray(scale, BF16)
            dot = do_sc[pl.ds(r0, tq), :]
            at = a_ref[0, pl.ds(r0, tq), :]
            dl = delta_sc[pl.ds(r0, tq), :]
            dq_new, dk_new, dv_new, dr_new = [], [], [], []
            for e in range(2):
                sel = lo if e == 0 else jnp.logical_not(lo)
                a_col = jnp.max(jnp.where(sel, at, -jnp.inf), axis=1, keepdims=True)
                d_col = jnp.max(jnp.where(sel, dl, -jnp.inf), axis=1, keepdims=True)
                cs = ct_ref[pl.ds(2 * hp + e, 1), :]
                s = _dot_nt(qt, km[e]) + (a_col - cs)
                p = jnp.exp(s)
                if masked:
                    p = jnp.where(rows >= cols, p, 0.0)
                dp = _dot_nt(dot, vm[e])
                ds = p * (dp - d_col)
                pb, dsb = p.astype(BF16), ds.astype(BF16)
                dv_new.append(_dot_tn(pb, dot))
                dk_new.append(_dot_tn(dsb, qt))
                dq_new.append(_dot_nn(dsb, kblk))
                dcs_ref[0, pl.ds(e, 1), :] += jnp.sum(ds, axis=0, keepdims=True)
                dr_new.append(jnp.sum(ds, axis=1, keepdims=True))
            drow_sc[pl.ds(r0, tq), :] += jnp.where(lo, dr_new[0], dr_new[1])
            dv_sc[...] += jnp.where(lo, dv_new[0], dv_new[1])
            dk_sc[...] += jnp.where(lo, dk_new[0], dk_new[1])
            dq_sc[pl.ds(r0, tq), :] += jnp.where(lo, dq_new[0], dq_new[1])

        step(kj, True)

        def loop_body(i, carry):
            step(i, False)
            return carry

        lax.fori_loop(kj + 1, nq, loop_body, 0)
        k0 = pl.multiple_of(kj * tq, tq)
        out_ref[1, pl.ds(k0, tq), :] = dk_sc[...].astype(BF16)
        out_ref[2, pl.ds(k0, tq), :] = dv_sc[...].astype(BF16)

        @pl.when(kj == nq - 1)
        def _():
            out_ref[0] = (dq_sc[...] * scale).astype(BF16)

            def chunk(i, carry):
                r0 = pl.multiple_of(i * tq, tq)
                dr = drow_sc[pl.ds(r0, tq), :]
                acc = dcr_ref[pl.ds(r0, tq), :]
                for e in range(2):
                    sel = lo if e == 0 else jnp.logical_not(lo)
                    col = jnp.max(jnp.where(sel, dr, -jnp.inf), axis=1, keepdims=True)
                    acc = jnp.where(lane == 2 * hp + e, col, acc)
                dcr_ref[pl.ds(r0, tq), :] = acc
                return carry
            lax.fori_loop(0, nq, chunk, 0)

    full = lambda cb: pl.BlockSpec((S, LANE), lambda h, j, cb=cb: (0, cb + h))
    return pl.pallas_call(
        body, name="fox_attn_bwd", grid=(HP, nq),
        in_specs=[full(0),
                  pl.BlockSpec((tq, LANE), lambda h, j: (j, wb + h)),
                  pl.BlockSpec((tq, LANE), lambda h, j: (j, 2 * wb + h)),
                  full(0), full(0), full(0),
                  pl.BlockSpec((1, S, LANE), lambda h, j: (h, 0, 0)),
                  pl.BlockSpec((LANE, tq), lambda h, j: (0, j))],
        out_specs=[pl.BlockSpec((4, S, LANE), lambda h, j: (0, 0, h)),
                   pl.BlockSpec((1, 8, tq), lambda h, j: (h, 0, j)),
                   pl.BlockSpec((S, LANE), lambda h, j: (0, 0))],
        out_shape=[jax.ShapeDtypeStruct((4, S, W), BF16), jax.ShapeDtypeStruct((HP, 8, S), F32),
                   jax.ShapeDtypeStruct((S, LANE), F32)],
        scratch_shapes=[pltpu.VMEM((S, LANE), BF16), pltpu.VMEM((S, LANE), F32), pltpu.VMEM((S, LANE), F32),
                        pltpu.VMEM((tq, LANE), F32), pltpu.VMEM((tq, LANE), F32), pltpu.VMEM((S, LANE), F32)],
        compiler_params=_cparams(),
    )(qkv, qkv, qkv, dy, gate, o, a, ct)


def _swa_pick(blk, half, lane):
    b = blk.astype(F32)
    r = pltpu.roll(b, HEAD_DIM, axis=1)
    return jnp.where(jnp.logical_xor(lane < HEAD_DIM, half == 1), b, r).astype(BF16)


def _swa_stack(t, lane, G):
    pieces = []
    z = jnp.zeros((SWA_BLOCK, LANE), t.dtype)
    for j in range(G // 2):
        tile = t[:, LANE * j:LANE * (j + 1)]
        pieces += [jnp.where(lane < HEAD_DIM, tile, z), jnp.where(lane < HEAD_DIM, z, tile)]
    return jnp.concatenate(pieces, axis=0)


def _swa_unstack(st, lane, G):
    tiles = []
    for j in range(G // 2):
        a = st[2 * j * SWA_BLOCK:(2 * j + 1) * SWA_BLOCK]
        b = st[(2 * j + 1) * SWA_BLOCK:(2 * j + 2) * SWA_BLOCK]
        tiles.append(jnp.where(lane < HEAD_DIM, a, b))
    return jnp.concatenate(tiles, axis=1)


def _swa_scores(q_ref, kp_ref, kc_ref, vp_ref, vc_ref, sink_ref, kvh, n, G):
    R = G * SWA_BLOCK
    lane = lax.broadcasted_iota(jnp.int32, (SWA_BLOCK, LANE), 1)
    half = kvh % 2
    kk = jnp.concatenate([_swa_pick(kp_ref[...], half, lane), _swa_pick(kc_ref[...], half, lane)], axis=0)
    vv = jnp.concatenate([_swa_pick(vp_ref[...], half, lane), _swa_pick(vc_ref[...], half, lane)], axis=0)
    qstack = _swa_stack(q_ref[...], lane, G) * jnp.asarray(HEAD_DIM ** -0.5, BF16)
    s = _dot_nt(qstack, kk)
    t_loc = lax.broadcasted_iota(jnp.int32, (R, 2 * SWA_BLOCK), 0) & (SWA_BLOCK - 1)
    j_loc = lax.broadcasted_iota(jnp.int32, (R, 2 * SWA_BLOCK), 1)
    diff = t_loc + SWA_BLOCK - j_loc
    mask = (diff >= 0) & (diff < SWA_BLOCK) & ((n > 0) | (j_loc >= SWA_BLOCK))
    s = jnp.where(mask, s, NEG_INF)
    srow = sink_ref[...]
    lane1 = lax.broadcasted_iota(jnp.int32, (1, LANE), 1)
    sink = jnp.concatenate(
        [jnp.broadcast_to(jnp.sum(jnp.where(lane1 == kvh * G + g, srow, 0.0), axis=1, keepdims=True), (SWA_BLOCK, 1))
         for g in range(G)], axis=0)
    m = jnp.maximum(jnp.max(s, axis=1, keepdims=True), sink)
    e = jnp.exp(s - m)
    es = jnp.exp(sink - m)
    den = jnp.sum(e, axis=1, keepdims=True) + es
    return qstack, kk, vv, e / den, es / den, lane


def _swa_fwd(qkv, gate, sinks, HQ, HKV):
    S = qkv.shape[0]
    G = HQ // HKV
    WQ, KVW = HQ * HEAD_DIM, HKV * HEAD_DIM
    nb = S // SWA_BLOCK
    GW = G * HEAD_DIM
    kb, vb = WQ // LANE, (WQ + KVW) // LANE

    def body(q_ref, kp_ref, kc_ref, vp_ref, vc_ref, g_ref, sink_ref, y_ref, o_ref):
        kvh, n = pl.program_id(0), pl.program_id(1)
        _, _, vv, p, _, lane = _swa_scores(q_ref, kp_ref, kc_ref, vp_ref, vc_ref, sink_ref, kvh, n, G)
        o = _swa_unstack(_dot_nn(p.astype(BF16), vv), lane, G)
        g = g_ref[...]
        y_ref[...] = (o * (g * _sigmoid(g))).astype(BF16)
        o_ref[...] = o.astype(BF16)

    blk = lambda cb, prev: pl.BlockSpec(
        (SWA_BLOCK, LANE), lambda h, n, cb=cb, prev=prev: (jnp.maximum(n - prev, 0), cb + h // 2))
    qspec = pl.BlockSpec((SWA_BLOCK, GW), lambda h, n: (n, h))
    return pl.pallas_call(
        body, name="swa_attn_fwd", grid=(HKV, nb),
        in_specs=[qspec, blk(kb, 1), blk(kb, 0), blk(vb, 1), blk(vb, 0), qspec,
                  pl.BlockSpec((1, LANE), lambda h, n: (0, 0))],
        out_specs=[qspec, qspec],
        out_shape=[jax.ShapeDtypeStruct((S, WQ), BF16), jax.ShapeDtypeStruct((S, WQ), BF16)],
        compiler_params=_cparams(),
    )(qkv, qkv, qkv, qkv, qkv, gate, sinks)


def _swa_bwd(qkv, dy, gate, o, sinks, tables, HQ, HKV):
    S = qkv.shape[0]
    G = HQ // HKV
    WQ, KVW = HQ * HEAD_DIM, HKV * HEAD_DIM
    nb = S // SWA_BLOCK
    GW = G * HEAD_DIM
    R = G * SWA_BLOCK
    kb, vb = WQ // LANE, (WQ + KVW) // LANE
    scale = HEAD_DIM ** -0.5
    assert G == 8

    def body(q_ref, kp_ref, kc_ref, vp_ref, vc_ref, dy_ref, g_ref, o_ref, sink_ref, t_ref,
             dqg_ref, dkv_ref, dsink_ref, carry_sc):
        kvh, n = pl.program_id(0), pl.program_id(1)

        @pl.when(n == 0)
        def _():
            carry_sc[...] = jnp.zeros(carry_sc.shape, F32)
            dsink_ref[...] = jnp.zeros(dsink_ref.shape, F32)

        @pl.when(n < nb)
        def _():
            qstack, kk, vv, p, psink, lane = _swa_scores(q_ref, kp_ref, kc_ref, vp_ref, vc_ref, sink_ref, kvh, n, G)
            dyv, g, ov = dy_ref[...], g_ref[...], o_ref[...].astype(F32)
            sg = _sigmoid(g)
            dob = (dyv * (g * sg)).astype(BF16)
            dqg_ref[1] = (dyv * ov * (sg * (1.0 + g * (1.0 - sg)))).astype(BF16)
            prod = dob.astype(F32) * ov
            dparts = []
            for j in range(G // 2):
                tile = prod[:, LANE * j:LANE * (j + 1)]
                dparts += [jnp.sum(jnp.where(lane < HEAD_DIM, tile, 0.0), axis=1, keepdims=True),
                           jnp.sum(jnp.where(lane < HEAD_DIM, 0.0, tile), axis=1, keepdims=True)]
            delta = jnp.concatenate(dparts, axis=0)
            dostack = _swa_stack(dob, lane, G)
            dp = _dot_nt(dostack, vv)
            ds = p * (dp - delta)
            dsb, pb = ds.astype(BF16), p.astype(BF16)
            dq = _swa_unstack(_dot_nn(dsb, kk), lane, G) * scale
            dq = (dq * t_ref[0] + pltpu.roll(dq * t_ref[1], ROT_DIM // 2, axis=1)
                  + pltpu.roll(dq * t_ref[2], GW - ROT_DIM // 2, axis=1))
            dqg_ref[0] = dq.astype(BF16)
            dkk = _dot_tn(dsb, qstack)
            dvv = _dot_tn(pb, dostack)
            dkk = dkk + pltpu.roll(dkk, HEAD_DIM, axis=1)
            dvv = dvv + pltpu.roll(dvv, HEAD_DIM, axis=1)
            lane2 = lax.broadcasted_iota(jnp.int32, (2 * SWA_BLOCK, LANE), 1)
            comb = jnp.where(lane2 < HEAD_DIM, dkk, dvv)
            dkv_ref[0] = carry_sc[...] + comb[:SWA_BLOCK]
            carry_sc[...] = comb[SWA_BLOCK:]
            sk = psink * delta
            rows = [jnp.broadcast_to(-jnp.sum(sk[g_ * SWA_BLOCK:(g_ + 1) * SWA_BLOCK], axis=0, keepdims=True), (1, LANE))
                    for g_ in range(G)]
            dsink_ref[0] += jnp.concatenate(rows, axis=0)

        @pl.when(n == nb)
        def _():
            dkv_ref[0] = carry_sc[...]

    cl = lambda n: jnp.minimum(n, nb - 1)
    blk = lambda cb, prev: pl.BlockSpec(
        (SWA_BLOCK, LANE), lambda h, n, cb=cb, prev=prev: (jnp.maximum(cl(n) - prev, 0), cb + h // 2))
    qspec = pl.BlockSpec((SWA_BLOCK, GW), lambda h, n: (cl(n), h))
    return pl.pallas_call(
        body, name="swa_attn_bwd", grid=(HKV, nb + 1),
        in_specs=[qspec, blk(kb, 1), blk(kb, 0), blk(vb, 1), blk(vb, 0), qspec, qspec, qspec,
                  pl.BlockSpec((1, LANE), lambda h, n: (0, 0)),
                  pl.BlockSpec((3, SWA_BLOCK, GW), lambda h, n: (0, cl(n), 0))],
        out_specs=[pl.BlockSpec((2, SWA_BLOCK, GW), lambda h, n: (0, cl(n), h)),
                   pl.BlockSpec((1, SWA_BLOCK, LANE), lambda h, n: (h, jnp.maximum(n - 1, 0), 0)),
                   pl.BlockSpec((1, 8, LANE), lambda h, n: (h, 0, 0))],
        out_shape=[jax.ShapeDtypeStruct((2, S, WQ), BF16), jax.ShapeDtypeStruct((HKV, S, LANE), F32),
                   jax.ShapeDtypeStruct((HKV, 8, LANE), F32)],
        scratch_shapes=[pltpu.VMEM((SWA_BLOCK, LANE), F32)],
        compiler_params=_cparams(),
    )(qkv, qkv, qkv, qkv, qkv, dy, gate, o, sinks, tables)


def _swa_dkv_finish(dkv, tables):
    HKV, S, _ = dkv.shape
    KVW = HKV * HEAD_DIM
    tm = _row_tile(S, 512)
    npair = HKV // 2

    def body(d_ref, t_ref, o_ref):
        lane = lax.broadcasted_iota(jnp.int32, (tm, LANE), 1)
        lo = lane < HEAD_DIM
        for p in range(npair):
            a, b = d_ref[2 * p], d_ref[2 * p + 1]
            tk = jnp.where(lo, a, pltpu.roll(b, HEAD_DIM, axis=1))
            tv = jnp.where(lo, pltpu.roll(a, HEAD_DIM, axis=1), b)
            tk = (tk * t_ref[0] + pltpu.roll(tk * t_ref[1], ROT_DIM // 2, axis=1)
                  + pltpu.roll(tk * t_ref[2], LANE - ROT_DIM // 2, axis=1))
            o_ref[:, LANE * p:LANE * (p + 1)] = tk.astype(BF16)
            o_ref[:, KVW + LANE * p:KVW + LANE * (p + 1)] = tv.astype(BF16)

    return pl.pallas_call(
        body, name="swa_dkv_finish", grid=(S // tm,),
        in_specs=[pl.BlockSpec((HKV, tm, LANE), lambda i: (0, i, 0)), pl.BlockSpec((3, tm, LANE), lambda i: (0, i, 0))],
        out_specs=pl.BlockSpec((tm, 2 * KVW), lambda i: (i, 0)),
        out_shape=jax.ShapeDtypeStruct((S, 2 * KVW), BF16),
        compiler_params=_cparams(),
    )(dkv, tables)


def _rope_tables(S, width):
    half = ROT_DIM // 2
    pos = jnp.arange(S, dtype=F32)
    inv_freq = ROPE_THETA ** (-jnp.arange(half, dtype=F32) / half)
    ang = pos[:, None] * inv_freq[None, :]
    cos, sin = jnp.cos(ang), jnp.sin(ang)
    one = jnp.ones((S, HEAD_DIM - ROT_DIM), F32)
    zero = jnp.zeros((S, HEAD_DIM - ROT_DIM), F32)
    zh = jnp.zeros((S, half), F32)
    t0 = jnp.concatenate([cos, cos, one], axis=1)
    t1 = jnp.concatenate([-sin, zh, zero], axis=1)
    t2 = jnp.concatenate([zh, sin, zero], axis=1)
    return jnp.stack([jnp.tile(t, (1, width // HEAD_DIM)) for t in (t0, t1, t2)])


def _pad_rows(v, row, total_rows=8):
    return jnp.pad(v, ((row, total_rows - row - v.shape[0]), (0, 0)))


def _pad_lanes(v, off, width):
    return jnp.pad(v, ((0, 0), (off, width - off - v.shape[1])))


def kernel(x, norm_g, fox_w_in, fox_b_f, fox_w_out, swa_w_in, swa_sinks, swa_w_out, final_g, loss_target, m_norm_g, m_fox_w_in, m_fox_b_f, m_fox_w_out, m_swa_w_in, m_swa_sinks, m_swa_w_out, m_final_g, v_norm_g, v_fox_w_in, v_fox_b_f, v_fox_w_out, v_swa_w_in, v_swa_sinks, v_swa_w_out, v_final_g):
    S, D = x.shape[1], x.shape[2]
    H = fox_b_f.shape[1]
    W = H * HEAD_DIM
    wf = fox_w_in.shape[2]
    ws = swa_w_in.shape[2]
    HQ = swa_sinks.shape[1]
    WQ = HQ * HEAD_DIM
    KVW = (ws * N_DEV - 2 * WQ) // 2
    HKV = KVW // HEAD_DIM
    rows_o = fox_w_out.shape[1]
    assert wf * N_DEV == 4 * W + H and rows_o * N_DEV == W and H <= LANE and HQ <= LANE
    me = _my_index()

    _, sw_f, np_f = _slab_geom(wf)
    _, sw_s, np_s = _slab_geom(ws)

    def slab(w2d, w, sw):
        off = (w * me) % LANE
        return lax.dynamic_update_slice(jnp.zeros((w2d.shape[0], sw), BF16), w2d.astype(BF16), (0, off))

    fi_all, si_all, fo_all, so_all = _all_gather([
        slab(fox_w_in[0], wf, sw_f), slab(swa_w_in[0], ws, sw_s),
        fox_w_out[0].astype(BF16), swa_w_out[0].astype(BF16)])
    w_fi = _assemble(fi_all, wf)
    w_si = _assemble(si_all, ws)
    w_fo = fo_all.reshape(W, D)
    w_so = so_all.reshape(WQ, D)

    x0 = x[0]
    g0, g1, gf = norm_g[0:1], norm_g[1:2], final_g[None, :]
    bias = _pad_lanes(fox_b_f, 0, LANE)
    sinks = _pad_lanes(swa_sinks, 0, LANE)
    tab_q = _rope_tables(S, 8 * HEAD_DIM)
    tab_k = tab_q[:, :, :LANE]

    h0 = _rmsnorm_fwd(x0, g0, "rmsnorm0")
    qkv0 = _proj(h0, w_fi, 0, 3 * W, BF16, "fox_in_qkv")
    gate0 = _proj(h0, w_fi, 3 * W, W, F32, "fox_in_gate")
    fl = _proj(h0, w_fi, 4 * W, LANE, F32, "fox_in_f")
    c, ct = _fox_gate_fwd(fl, bias)
    y0, o0, a0 = _fox_fwd(qkv0, gate0, c, ct, H)
    x1 = _out_proj_res(y0, w_fo, x0, "fox_out")

    h1 = _rmsnorm_fwd(x1, g1, "rmsnorm1")
    qkv1 = _proj(h1, w_si, 0, WQ + 2 * KVW, BF16, "swa_in_qkv", rope=(tab_k, WQ + KVW))
    gate1 = _proj(h1, w_si, WQ + 2 * KVW, WQ, F32, "swa_in_gate")
    y1, o1 = _swa_fwd(qkv1, gate1, sinks, HQ, HKV)
    x2 = _out_proj_res(y1, w_so, x1, "swa_out")

    dx2, dx2b, dgf, loss_p = _loss_head(x2, loss_target[0], gf)

    dy1 = _matmul_nt([(dx2b, None, 0)], w_so, WQ, "swa_out_bwd")
    g_so = _matmul_tn(y1, [(dx2b, None, 0)], D, "swa_out_wgrad")
    dqg1, dkv1, dsink = _swa_bwd(qkv1, dy1, gate1, o1, sinks, tab_q, HQ, HKV)
    dkv1f = _swa_dkv_finish(dkv1, tab_k)
    parts1 = [(dqg1, 0, 0), (dkv1f, None, WQ), (dqg1, 1, WQ + 2 * KVW)]
    g_si = _matmul_tn(h1, parts1, np_s, "swa_in_wgrad", tile_major=True)
    dh1 = _matmul_nt(parts1, w_si, D, "swa_in_bwd")
    dx1, dx1b, dg1 = _rmsnorm_bwd(dh1, x1, g1, dx2, "rmsnorm1_bwd")

    dy0 = _matmul_nt([(dx1b, None, 0)], w_fo, W, "fox_out_bwd")
    g_fo = _matmul_tn(y0, [(dx1b, None, 0)], D, "fox_out_wgrad")
    dqkvg0, dcs, dcr = _fox_bwd(qkv0, dy0, gate0, o0, a0, ct, H)
    dcs_t = _pad_lanes(dcs[:, :2, :].reshape(H, S).T, 0, LANE)
    dfl, dbf = _fox_gate_bwd(fl, bias, dcr - dcs_t)
    parts0 = [(dqkvg0, p, p * W) for p in range(4)] + [(dfl, None, 4 * W)]
    g_fi = _matmul_tn(h0, parts0, np_f, "fox_in_wgrad", tile_major=True)
    dh0 = _matmul_nt(parts0, w_fi, D, "fox_in_bwd")
    dx0, _, dg0 = _rmsnorm_bwd(dh0, x0, g0, dx1, "rmsnorm0_bwd")

    specs = [("col", wf), ("col", ws), ("row", rows_o), ("row", rows_o)]
    gfull = [g_fi, g_si, g_fo, g_so]
    recv = _reduce_scatter_stage1(gfull, specs)
    sums = [_pair_sum(g_, r_, s_) for g_, r_, s_ in zip(gfull, recv, specs)]
    recv2 = _reduce_scatter_stage2([s_[1] for s_ in sums])
    red_fi = _final_sum_cols(sums[0][0], recv2[0])
    red_si = _final_sum_cols(sums[1][0], recv2[1])
    gw_fi = lax.dynamic_slice(red_fi, (0, (wf * me) % LANE), (D, wf))
    gw_si = lax.dynamic_slice(red_si, (0, (ws * me) % LANE), (D, ws))
    gw_fo, gw_so = _final_sum(sums[2][0], recv2[2]), _final_sum(sums[3][0], recv2[3])

    P = D
    dsink_v = dsink[:, :, 0].reshape(1, HQ)
    row3 = _pad_lanes(dbf[:, :H], 0, P) + _pad_lanes(dsink_v, LANE, P) + _pad_lanes(loss_p[:, :1], 2 * LANE, P)
    pack = _pad_rows(dg0, 0) + _pad_rows(dg1, 1) + _pad_rows(dgf, 2) + _pad_rows(row3, 3)
    tot = _all_reduce_small(pack)
    loss = tot[3, 2 * LANE]
    g_norm = tot[0:2]
    g_final = tot[2]
    g_bf = tot[3:4, 0:H]
    g_sinks = tot[3:4, LANE:LANE + HQ]

    def small_pack(ng, fg, bf, sk):
        r3 = _pad_lanes(bf, 0, P) + _pad_lanes(sk, LANE, P)
        return _pad_rows(ng, 0) + _pad_rows(fg[None, :], 2) + _pad_rows(r3, 3)

    sd, sm, sv = _adamw(small_pack(norm_g, final_g, fox_b_f, swa_sinks), tot,
                        small_pack(m_norm_g, m_final_g, m_fox_b_f, m_swa_sinks),
                        small_pack(v_norm_g, v_final_g, v_fox_b_f, v_swa_sinks), "adamw_small")

    def unpack(t):
        return t[0:2], t[3:4, 0:H], t[3:4, LANE:LANE + HQ], t[2]

    d_fi, m_fi, v_fi = _adamw(fox_w_in[0], gw_fi, m_fox_w_in[0], v_fox_w_in[0], "adamw_fox_in")
    d_fo, m_fo, v_fo = _adamw(fox_w_out[0], gw_fo, m_fox_w_out[0], v_fox_w_out[0], "adamw_fox_out")
    d_si, m_si, v_si = _adamw(swa_w_in[0], gw_si, m_swa_w_in[0], v_swa_w_in[0], "adamw_swa_in")
    d_so, m_so, v_so = _adamw(swa_w_out[0], gw_so, m_swa_w_out[0], v_swa_w_out[0], "adamw_swa_out")

    def group(small, fi, fo, si, so):
        ng, bf, sk, fg = unpack(small)
        return (ng, fi[None], bf, fo[None], si[None], sk, so[None], fg)

    grads = (g_norm, gw_fi[None], g_bf, gw_fo[None], gw_si[None], g_sinks, gw_so[None], g_final)
    return (loss, dx0[None], *grads, *group(sd, d_fi, d_fo, d_si, d_so),
            *group(sm, m_fi, m_fo, m_si, m_so), *group(sv, v_fi, v_fo, v_si, v_so))
```

```python
import functools
import math

import jax
import jax.numpy as jnp
from jax import lax
from jax.experimental import pallas as pl
from jax.experimental.pallas import tpu as pltpu

F32 = jnp.float32
BF16 = jnp.bfloat16
MESH = pl.DeviceIdType.MESH

N_DEV = 8
LANE = 128
HEAD_DIM = 64
SWA_BLOCK = 128
NEG_INF = -1e30
RMS_EPS = 1e-6
ROPE_THETA = 500000.0
ROT_DIM = HEAD_DIM // 4
ADAM_LR, ADAM_B1, ADAM_B2, ADAM_EPS, ADAM_WD, ADAM_STEP = 0.001, 0.9, 0.999, 1e-08, 0.01, 10
VMEM_LIMIT = 56 * 1024 * 1024
MM_TILE = 1024


def _cparams(**kw):
    return pltpu.CompilerParams(vmem_limit_bytes=VMEM_LIMIT, **kw)


def _tile(n, cap):
    if n <= cap:
        return n
    t = (cap // LANE) * LANE
    while t > LANE and n % t:
        t -= LANE
    assert n % t == 0, (n, cap)
    return t


def _row_tile(n, cap):
    t = min(n, cap)
    while n % t:
        t //= 2
    return t


def _dot_nn(a, b):
    return jnp.dot(a, b, preferred_element_type=F32)


def _dot_nt(a, b):
    return lax.dot_general(a, b, (((1,), (1,)), ((), ())), preferred_element_type=F32)


def _dot_tn(a, b):
    return lax.dot_general(a, b, (((0,), (0,)), ((), ())), preferred_element_type=F32)


def _sigmoid(g):
    return 1.0 / (1.0 + jnp.exp(-g))


def _slab_geom(w):
    starts = [w * i for i in range(N_DEV)]
    aligned = [LANE * (s // LANE) for s in starts]
    offs = [s - a for s, a in zip(starts, aligned)]
    sw = LANE * (-(-(max(offs) + w) // LANE))
    return aligned, sw, aligned[-1] + sw


def _my_index():
    return 4 * lax.axis_index("x") + 2 * lax.axis_index("y") + lax.axis_index("c")


def _all_gather(arrs):
    n = len(arrs)

    def body(*refs):
        ins, outs = refs[:n], refs[n:2 * n]
        send_sems, recv_sems, local_sems = refs[2 * n:]
        x, y, c = lax.axis_index("x"), lax.axis_index("y"), lax.axis_index("c")
        me, sib = (x, y, c), (x, y, 1 - c)
        chips = [(1 - x, y), (x, 1 - y), (1 - x, 1 - y)]

        def idx(px, py, pc):
            return 4 * px + 2 * py + pc

        def copy(a, k, block, to, src=None):
            dst = outs[a].at[idx(*block)]
            return pltpu.make_async_remote_copy(
                src_ref=dst if src is None else src, dst_ref=dst,
                send_sem=send_sems.at[a, k], recv_sem=recv_sems.at[a, k],
                device_id=to, device_id_type=MESH)

        mine = [pltpu.make_async_copy(ins[a], outs[a].at[idx(*me)], local_sems.at[a]) for a in range(n)]
        for m in mine:
            m.start()
        first = []
        for a in range(n):
            first.append(copy(a, 0, me, sib, src=ins[a]))
            for j, chip in enumerate(chips):
                first.append(copy(a, 1 + j, me, (*chip, c), src=ins[a]))
        for cp in first:
            cp.start()
        passed = []
        for j, chip in enumerate(chips):
            for a in range(n):
                copy(a, 1 + j, (*chip, c), me).wait_recv()
                p = copy(a, 4 + j, (*chip, c), sib)
                p.start()
                passed.append(p)
        for a in range(n):
            copy(a, 0, sib, me).wait_recv()
        for j, chip in enumerate(chips):
            for a in range(n):
                copy(a, 4 + j, (*chip, 1 - c), me).wait_recv()
        for cp in first + passed:
            cp.wait_send()
        for m in mine:
            m.wait()

    any_spec = pl.BlockSpec(memory_space=pl.ANY)
    return pl.pallas_call(
        body, name="weights_all_gather",
        out_shape=[jax.ShapeDtypeStruct((N_DEV,) + a.shape, a.dtype) for a in arrs],
        in_specs=[any_spec] * n, out_specs=[any_spec] * n,
        scratch_shapes=[pltpu.SemaphoreType.DMA((n, 7)), pltpu.SemaphoreType.DMA((n, 7)),
                        pltpu.SemaphoreType.DMA((n,))],
    )(*arrs)


def _rel_chip(r):
    x, y = lax.axis_index("x"), lax.axis_index("y")
    return (x ^ (r >> 1), y ^ (r & 1))


def _rs_windows(specs):
    def window(ref, spec, blk):
        kind, n = spec
        if kind == "col":
            _, sw, _ = _slab_geom(n)
            return ref.at[pl.ds((n * blk) // LANE, sw // LANE)]
        start = pl.multiple_of(n * blk, n)
        return ref.at[pl.ds(start, n), :]
    return window


def _reduce_scatter_stage1(grads, specs):
    n = len(grads)
    window = _rs_windows(specs)

    def blk_shape(g, spec):
        kind, w = spec
        return (_slab_geom(w)[1] // LANE, g.shape[1], LANE) if kind == "col" else (w, g.shape[1])

    shapes = [blk_shape(g, s) for g, s in zip(grads, specs)]

    def body(*refs):
        ins, recvs = refs[:n], refs[n:2 * n]
        send_sems, recv_sems = refs[2 * n:]
        x, y, c = lax.axis_index("x"), lax.axis_index("y"), lax.axis_index("c")
        sib = (x, y, 1 - c)
        remotes = []
        for a in range(n):
            for r in range(4):
                px, py = _rel_chip(r)
                sib_blk = 4 * px + 2 * py + (1 - c)
                remotes.append(pltpu.make_async_remote_copy(
                    src_ref=window(ins[a], specs[a], sib_blk), dst_ref=recvs[a].at[r],
                    send_sem=send_sems.at[a, r], recv_sem=recv_sems.at[a, r],
                    device_id=sib, device_id_type=MESH))
        for cp in remotes:
            cp.start()
        for cp in remotes:
            cp.wait_recv()
        for cp in remotes:
            cp.wait_send()

    any_spec = pl.BlockSpec(memory_space=pl.ANY)
    return pl.pallas_call(
        body, name="grads_rs_sibling",
        out_shape=[jax.ShapeDtypeStruct((4,) + s, F32) for s in shapes],
        in_specs=[any_spec] * n, out_specs=[any_spec] * n,
        scratch_shapes=[pltpu.SemaphoreType.DMA((n, 4)), pltpu.SemaphoreType.DMA((n, 4))],
    )(*grads)


def _own_block_offsets(spec):
    kind, n = spec
    c = lax.axis_index("c")
    offs = []
    for r in range(4):
        px, py = _rel_chip(r)
        blk = 4 * px + 2 * py + c
        offs.append((n * blk) // LANE if kind == "col" else blk)
    return jnp.stack(offs).astype(jnp.int32)


def _pair_sum(g, recv, spec):
    kind, _ = spec
    offs = _own_block_offsets(spec)
    if kind == "col":
        _, T, M, _ = recv.shape
        grid = (4, T)
        g_spec = pl.BlockSpec((1, M, LANE), lambda r, t, o: (o[r] + t, 0, 0))
        r_spec = pl.BlockSpec((1, 1, M, LANE), lambda r, t, o: (r, t, 0, 0))
    else:
        _, nrow, C = recv.shape
        grid = (4,)
        g_spec = pl.BlockSpec((nrow, C), lambda r, o: (o[r], 0))
        r_spec = pl.BlockSpec((1, nrow, C), lambda r, o: (r, 0, 0))

    def body(o_ref, g_ref, r_ref, f_ref, h_ref):
        if kind == "col":
            s = g_ref[0] + r_ref[0, 0]
            f_ref[0, 0] = s
            h_ref[0, 0] = s.astype(BF16)
        else:
            s = g_ref[...] + r_ref[0]
            f_ref[0] = s
            h_ref[0] = s.astype(BF16)

    return pl.pallas_call(
        body, name="grads_pair_sum",
        grid_spec=pltpu.PrefetchScalarGridSpec(num_scalar_prefetch=1, grid=grid, in_specs=[g_spec, r_spec],
                                               out_specs=[r_spec, r_spec]),
        out_shape=[jax.ShapeDtypeStruct(recv.shape, F32), jax.ShapeDtypeStruct(recv.shape, BF16)],
        compiler_params=_cparams(),
    )(offs, g, recv)


def _reduce_scatter_stage2(parts):
    n = len(parts)

    def body(*refs):
        ins, recvs = refs[:n], refs[n:2 * n]
        send_sems, recv_sems = refs[2 * n:]
        c = lax.axis_index("c")
        copies = []
        for a in range(n):
            for r in range(1, 4):
                px, py = _rel_chip(r)
                copies.append(pltpu.make_async_remote_copy(
                    src_ref=ins[a].at[r], dst_ref=recvs[a].at[r - 1],
                    send_sem=send_sems.at[a, r - 1], recv_sem=recv_sems.at[a, r - 1],
                    device_id=(px, py, c), device_id_type=MESH))
        for cp in copies:
            cp.start()
        for cp in copies:
            cp.wait_recv()
        for cp in copies:
            cp.wait_send()

    any_spec = pl.BlockSpec(memory_space=pl.ANY)
    return pl.pallas_call(
        body, name="grads_rs_chips",
        out_shape=[jax.ShapeDtypeStruct((3,) + p.shape[1:], BF16) for p in parts],
        in_specs=[any_spec] * n, out_specs=[any_spec] * n,
        scratch_shapes=[pltpu.SemaphoreType.DMA((n, 3)), pltpu.SemaphoreType.DMA((n, 3))],
    )(*parts)


def _final_sum(psum, recv):
    _, R, C = psum.shape
    tr = _row_tile(R, 256)

    def body(p_ref, r_ref, o_ref):
        r = r_ref[...].astype(F32)
        o_ref[...] = ((p_ref[0] + r[0]) + r[1]) + r[2]

    return pl.pallas_call(
        body, name="grads_final_sum", grid=(R // tr,),
        in_specs=[pl.BlockSpec((1, tr, C), lambda i: (0, i, 0)), pl.BlockSpec((3, tr, C), lambda i: (0, i, 0))],
        out_specs=pl.BlockSpec((tr, C), lambda i: (i, 0)),
        out_shape=jax.ShapeDtypeStruct((R, C), F32),
        compiler_params=_cparams(),
    )(psum, recv)


def _final_sum_cols(psum, recv):
    _, T, M, _ = psum.shape

    def body(p_ref, r_ref, o_ref):
        r = r_ref[...].astype(F32)
        o_ref[...] = ((p_ref[0, 0] + r[0, 0]) + r[1, 0]) + r[2, 0]

    return pl.pallas_call(
        body, name="grads_final_sum_cols", grid=(T,),
        in_specs=[pl.BlockSpec((1, 1, M, LANE), lambda t: (0, t, 0, 0)),
                  pl.BlockSpec((3, 1, M, LANE), lambda t: (0, t, 0, 0))],
        out_specs=pl.BlockSpec((M, LANE), lambda t: (0, t)),
        out_shape=jax.ShapeDtypeStruct((M, T * LANE), F32),
        compiler_params=_cparams(),
    )(psum, recv)


def _all_reduce_small(pack):
    R, P = pack.shape

    def body(x_ref, o_ref, gat_ref, send_sems, recv_sems):
        x, y, c = lax.axis_index("x"), lax.axis_index("y"), lax.axis_index("c")
        me = 4 * x + 2 * y + c
        gat_ref[me] = x_ref[...]
        copies = []
        for k in range(1, N_DEV):
            peer = (x ^ (k >> 2), y ^ ((k >> 1) & 1), c ^ (k & 1))
            copies.append(pltpu.make_async_remote_copy(
                src_ref=x_ref, dst_ref=gat_ref.at[me],
                send_sem=send_sems.at[k - 1], recv_sem=recv_sems.at[k - 1],
                device_id=peer, device_id_type=MESH))
        for cp in copies:
            cp.start()
        for cp in copies:
            cp.wait_recv()
        for cp in copies:
            cp.wait_send()
        acc = gat_ref[0]
        for d in range(1, N_DEV):
            acc = acc + gat_ref[d]
        o_ref[...] = acc

    vm = pl.BlockSpec(memory_space=pltpu.VMEM)
    return pl.pallas_call(
        body, name="small_all_reduce",
        out_shape=jax.ShapeDtypeStruct((R, P), F32),
        in_specs=[vm], out_specs=vm,
        scratch_shapes=[pltpu.VMEM((N_DEV, R, P), F32),
                        pltpu.SemaphoreType.DMA((N_DEV - 1,)), pltpu.SemaphoreType.DMA((N_DEV - 1,))],
    )(pack)


def _assemble(slabs, w):
    aligned, sw, total = _slab_geom(w)
    K = slabs.shape[1]
    tr = _row_tile(K, 256)

    def body(s_ref, o_ref):
        o_ref[...] = jnp.zeros(o_ref.shape, BF16)
        for i in range(N_DEV):
            a = aligned[i]
            o_ref[:, a:a + sw] = o_ref[:, a:a + sw] + s_ref[i]

    return pl.pallas_call(
        body, name="assemble_w_in", grid=(K // tr,),
        in_specs=[pl.BlockSpec((N_DEV, tr, sw), lambda i: (0, i, 0))],
        out_specs=pl.BlockSpec((tr, total), lambda i: (i, 0)),
        out_shape=jax.ShapeDtypeStruct((K, total), BF16),
        compiler_params=_cparams(),
    )(slabs)


def _rmsnorm_fwd(x, g, name):
    S, D = x.shape
    tm = _row_tile(S, 256)

    def body(x_ref, g_ref, h_ref):
        xv = x_ref[...]
        r = lax.rsqrt(jnp.mean(xv * xv, axis=-1, keepdims=True) + RMS_EPS)
        h_ref[...] = ((xv * r) * g_ref[...]).astype(BF16)

    return pl.pallas_call(
        body, name=name, grid=(S // tm,),
        in_specs=[pl.BlockSpec((tm, D), lambda i: (i, 0)), pl.BlockSpec((1, D), lambda i: (0, 0))],
        out_specs=pl.BlockSpec((tm, D), lambda i: (i, 0)),
        out_shape=jax.ShapeDtypeStruct((S, D), BF16),
        compiler_params=_cparams(),
    )(x, g)


def _rmsnorm_bwd(dh, x, g, dres, name):
    S, D = x.shape
    tm = _row_tile(S, 256)

    def body(dh_ref, x_ref, g_ref, dr_ref, dx_ref, dxb_ref, dg_ref):
        xv = x_ref[...]
        r = lax.rsqrt(jnp.mean(xv * xv, axis=-1, keepdims=True) + RMS_EPS)
        xhat = xv * r
        d = dh_ref[...]
        gd = d * g_ref[...]
        dx = r * (gd - xhat * jnp.mean(gd * xhat, axis=-1, keepdims=True)) + dr_ref[...]
        dx_ref[...] = dx
        dxb_ref[...] = dx.astype(BF16)

        @pl.when(pl.program_id(0) == 0)
        def _():
            dg_ref[...] = jnp.zeros(dg_ref.shape, F32)
        dg_ref[...] += jnp.sum(d * xhat, axis=0, keepdims=True)

    row = pl.BlockSpec((tm, D), lambda i: (i, 0))
    vec = pl.BlockSpec((1, D), lambda i: (0, 0))
    return pl.pallas_call(
        body, name=name, grid=(S // tm,),
        in_specs=[row, row, vec, row], out_specs=[row, row, vec],
        out_shape=[jax.ShapeDtypeStruct((S, D), F32), jax.ShapeDtypeStruct((S, D), BF16),
                   jax.ShapeDtypeStruct((1, D), F32)],
        compiler_params=_cparams(),
    )(dh, x, g, dres)


def _loss_head(x, tgt, g):
    S, D = x.shape
    tm = _row_tile(S, 256)

    def body(x_ref, t_ref, g_ref, dx_ref, dxb_ref, dg_ref, loss_ref):
        xv = x_ref[...]
        r = lax.rsqrt(jnp.mean(xv * xv, axis=-1, keepdims=True) + RMS_EPS)
        xhat = xv * r
        gv = g_ref[...]
        err = xhat * gv - t_ref[...]
        d = err * (1.0 / D)
        gd = d * gv
        dx = r * (gd - xhat * jnp.mean(gd * xhat, axis=-1, keepdims=True))
        dx_ref[...] = dx
        dxb_ref[...] = dx.astype(BF16)

        @pl.when(pl.program_id(0) == 0)
        def _():
            dg_ref[...] = jnp.zeros(dg_ref.shape, F32)
            loss_ref[...] = jnp.zeros(loss_ref.shape, F32)
        dg_ref[...] += jnp.sum(d * xhat, axis=0, keepdims=True)
        per_tok = jnp.sum(err * err, axis=-1, keepdims=True) * (1.0 / D)
        loss_ref[...] += 0.5 * jnp.sum(per_tok, axis=0, keepdims=True)

    row = pl.BlockSpec((tm, D), lambda i: (i, 0))
    vec = pl.BlockSpec((1, D), lambda i: (0, 0))
    return pl.pallas_call(
        body, name="loss_head", grid=(S // tm,),
        in_specs=[row, row, vec],
        out_specs=[row, row, vec, pl.BlockSpec((1, LANE), lambda i: (0, 0))],
        out_shape=[jax.ShapeDtypeStruct((S, D), F32), jax.ShapeDtypeStruct((S, D), BF16),
                   jax.ShapeDtypeStruct((1, D), F32), jax.ShapeDtypeStruct((1, LANE), F32)],
        compiler_params=_cparams(),
    )(x, tgt, g)


def _adamw(w, g, m, v, name):
    R, C = w.shape
    tr = _row_tile(R, 256)
    c1 = 1.0 - ADAM_B1 ** ADAM_STEP
    c2 = 1.0 - ADAM_B2 ** ADAM_STEP

    def body(w_ref, g_ref, m_ref, v_ref, d_ref, nm_ref, nv_ref):
        gv = g_ref[...]
        nm = ADAM_B1 * m_ref[...] + (1.0 - ADAM_B1) * gv
        nv = ADAM_B2 * v_ref[...] + (1.0 - ADAM_B2) * (gv * gv)
        d_ref[...] = -ADAM_LR * ((nm / c1) / (jnp.sqrt(nv / c2) + ADAM_EPS) + ADAM_WD * w_ref[...])
        nm_ref[...] = nm
        nv_ref[...] = nv

    spec = pl.BlockSpec((tr, C), lambda i: (i, 0))
    return pl.pallas_call(
        body, name=name, grid=(R // tr,),
        in_specs=[spec] * 4, out_specs=[spec] * 3,
        out_shape=[jax.ShapeDtypeStruct((R, C), F32)] * 3,
        compiler_params=_cparams(),
    )(w, g, m, v)


def _proj(h, wfull, col0, ncols, out_dtype, name, rope=None):
    S, K = h.shape
    tm = _row_tile(S, MM_TILE)
    tn = math.gcd(_tile(ncols, MM_TILE), col0) if col0 else _tile(ncols, MM_TILE)
    if rope is not None:
        tn = _tile(math.gcd(ncols, rope[1]), MM_TILE)
    assert ncols % tn == 0 and col0 % tn == 0
    cb = col0 // tn

    def body(*refs):
        if rope is None:
            a_ref, b_ref, o_ref = refs
        else:
            a_ref, b_ref, t_ref, o_ref = refs
        acc = _dot_nn(a_ref[...], b_ref[...])
        if rope is not None:
            t0, t1, t2 = (jnp.tile(t_ref[i], (1, tn // LANE)) for i in range(3))
            roped = (acc * t0 + pltpu.roll(acc, tn - ROT_DIM // 2, axis=1) * t1
                     + pltpu.roll(acc, ROT_DIM // 2, axis=1) * t2)
            acc = jnp.where(pl.program_id(1) < rope[1] // tn, roped, acc)
        o_ref[...] = acc.astype(out_dtype)

    in_specs = [pl.BlockSpec((tm, K), lambda i, j: (i, 0)), pl.BlockSpec((K, tn), lambda i, j: (0, cb + j))]
    args = [h, wfull]
    if rope is not None:
        in_specs.append(pl.BlockSpec((3, tm, LANE), lambda i, j: (0, i, 0)))
        args.append(rope[0])
    return pl.pallas_call(
        body, name=name, grid=(S // tm, ncols // tn),
        in_specs=in_specs, out_specs=pl.BlockSpec((tm, tn), lambda i, j: (i, j)),
        out_shape=jax.ShapeDtypeStruct((S, ncols), out_dtype),
        compiler_params=_cparams(),
    )(*args)


def _out_proj_res(y, wo, xres, name):
    S, W = y.shape
    D = wo.shape[1]
    tm, tn = _row_tile(S, MM_TILE), _tile(D, MM_TILE)

    def body(a_ref, b_ref, r_ref, o_ref):
        o_ref[...] = r_ref[...] + _dot_nn(a_ref[...], b_ref[...])

    return pl.pallas_call(
        body, name=name, grid=(S // tm, D // tn),
        in_specs=[pl.BlockSpec((tm, W), lambda i, j: (i, 0)), pl.BlockSpec((W, tn), lambda i, j: (0, j)),
                  pl.BlockSpec((tm, tn), lambda i, j: (i, j))],
        out_specs=pl.BlockSpec((tm, tn), lambda i, j: (i, j)),
        out_shape=jax.ShapeDtypeStruct((S, D), F32),
        compiler_params=_cparams(),
    )(y, wo, xres)


def _matmul_nt(parts, wfull, out_rows, name):
    S = parts[0][0].shape[-2]
    tm, tn = _row_tile(S, MM_TILE), _tile(out_rows, MM_TILE)
    plan, lo = [], 0
    for arr, lead, col0 in parts:
        n_p = arr.shape[-1]
        tk = math.gcd(_tile(n_p, 1024), col0) if col0 else _tile(n_p, 1024)
        steps = n_p // tk
        plan.append((lead, col0 // tk, tk, lo, lo + steps))
        lo += steps
    nk = lo
    npart = len(parts)

    def body(*refs):
        a_refs, w_refs = refs[:npart], refs[npart:2 * npart]
        o_ref, acc_ref = refs[2 * npart], refs[2 * npart + 1]
        k = pl.program_id(2)

        @pl.when(k == 0)
        def _():
            acc_ref[...] = jnp.zeros(acc_ref.shape, F32)
        for p, (_, _, _, lo_p, hi_p) in enumerate(plan):
            @pl.when((k >= lo_p) & (k < hi_p))
            def _(p=p):
                acc_ref[...] += _dot_nt(a_refs[p][...], w_refs[p][...])

        @pl.when(k == nk - 1)
        def _():
            o_ref[...] = acc_ref[...]

    in_specs, args = [], []
    for (arr, lead, col0), (_, cb, tk, lo_p, hi_p) in zip(parts, plan):
        def kk(k, lo_p=lo_p, hi_p=hi_p):
            return jnp.clip(k - lo_p, 0, hi_p - lo_p - 1)
        if lead is None:
            in_specs.append(pl.BlockSpec((tm, tk), lambda i, j, k, kk=kk: (i, kk(k))))
        else:
            in_specs.append(pl.BlockSpec((None, tm, tk), lambda i, j, k, kk=kk, lead=lead: (lead, i, kk(k))))
        args.append(arr)
    for (_, cb, tk, lo_p, hi_p) in plan:
        def kk(k, lo_p=lo_p, hi_p=hi_p):
            return jnp.clip(k - lo_p, 0, hi_p - lo_p - 1)
        in_specs.append(pl.BlockSpec((tn, tk), lambda i, j, k, kk=kk, cb=cb: (j, cb + kk(k))))
        args.append(wfull)
    return pl.pallas_call(
        body, name=name, grid=(S // tm, out_rows // tn, nk),
        in_specs=in_specs, out_specs=pl.BlockSpec((tm, tn), lambda i, j, k: (i, j)),
        out_shape=jax.ShapeDtypeStruct((S, out_rows), F32),
        scratch_shapes=[pltpu.VMEM((tm, tn), F32)],
        compiler_params=_cparams(),
    )(*args)


def _matmul_tn(a, parts, total, name, tile_major=False):
    S, M = a.shape
    tm, ts = _tile(M, MM_TILE), _row_tile(S, 512)
    out = None
    for idx, (arr, lead, col0) in enumerate(parts):
        n_p = arr.shape[-1]
        tn = math.gcd(_tile(n_p, MM_TILE), col0) if col0 else _tile(n_p, MM_TILE)
        cb = col0 // tn
        nk = S // ts

        def body(*refs, nk=nk, tn=tn):
            a_ref, b_ref = refs[0], refs[1]
            o_ref, acc_ref = refs[-2], refs[-1]
            k = pl.program_id(2)

            @pl.when(k == 0)
            def _():
                acc_ref[...] = jnp.zeros(acc_ref.shape, F32)
            acc_ref[...] += _dot_tn(a_ref[...], b_ref[...])

            @pl.when(k == nk - 1)
            def _():
                if tile_major:
                    for t in range(tn // LANE):
                        o_ref[t] = acc_ref[:, LANE * t:LANE * (t + 1)]
                else:
                    o_ref[...] = acc_ref[...]

        in_specs = [pl.BlockSpec((ts, tm), lambda i, j, k: (k, i))]
        if lead is None:
            in_specs.append(pl.BlockSpec((ts, tn), lambda i, j, k: (k, j)))
        else:
            in_specs.append(pl.BlockSpec((None, ts, tn), lambda i, j, k, lead=lead: (lead, k, j)))
        args = [a, arr]
        aliases = {}
        if out is not None:
            in_specs.append(pl.BlockSpec(memory_space=pl.ANY))
            args.append(out)
            aliases = {2: 0}
        if tile_major:
            out_spec = pl.BlockSpec((tn // LANE, tm, LANE), lambda i, j, k, cb=cb: (cb + j, i, 0))
            out_shape = jax.ShapeDtypeStruct((total // LANE, M, LANE), F32)
        else:
            out_spec = pl.BlockSpec((tm, tn), lambda i, j, k, cb=cb: (i, cb + j))
            out_shape = jax.ShapeDtypeStruct((M, total), F32)
        out = pl.pallas_call(
            body, name=f"{name}_{idx}", grid=(M // tm, n_p // tn, nk),
            in_specs=in_specs, out_specs=out_spec, out_shape=out_shape,
            scratch_shapes=[pltpu.VMEM((tm, tn), F32)],
            input_output_aliases=aliases,
            compiler_params=_cparams(),
        )(*args)
    return out


def _log_sigmoid(z):
    e = jnp.exp(-jnp.abs(z))
    return jnp.minimum(z, 0.0) - jnp.where(e < 1e-4, e * (1.0 - 0.5 * e), jnp.log(1.0 + e))


def _fox_gate_fwd(fl, bias):
    S = fl.shape[0]

    def body(f_ref, b_ref, c_ref, ct_ref):
        row = lax.broadcasted_iota(jnp.int32, (8, LANE), 0)

        def step(i, carry):
            r0 = pl.multiple_of(i * 8, 8)
            t = _log_sigmoid(f_ref[pl.ds(r0, 8), :] + b_ref[...])
            for sh in (1, 2, 4):
                t = t + jnp.where(row >= sh, pltpu.roll(t, sh, axis=0), 0.0)
            t = t + carry
            c_ref[pl.ds(r0, 8), :] = t
            return jnp.sum(jnp.where(row == 7, t, 0.0), axis=0, keepdims=True)

        lax.fori_loop(0, S // 8, step, jnp.zeros((1, LANE), F32))
        ct_ref[...] = c_ref[...].T

    vm = pl.BlockSpec(memory_space=pltpu.VMEM)
    return pl.pallas_call(
        body, name="fox_gate_fwd", in_specs=[vm, vm], out_specs=[vm, vm],
        out_shape=[jax.ShapeDtypeStruct((S, LANE), F32), jax.ShapeDtypeStruct((LANE, S), F32)],
        compiler_params=_cparams(),
    )(fl, bias)


def _fox_gate_bwd(fl, bias, dc):
    S = fl.shape[0]

    def body(f_ref, b_ref, d_ref, o_ref, db_ref, acc_ref):
        row = lax.broadcasted_iota(jnp.int32, (8, LANE), 0)
        nt = S // 8

        def step(ii, carry):
            carry_c, carry_b = carry
            r0 = pl.multiple_of((nt - 1 - ii) * 8, 8)
            t = d_ref[pl.ds(r0, 8), :]
            for sh in (1, 2, 4):
                t = t + jnp.where(row < 8 - sh, pltpu.roll(t, 8 - sh, axis=0), 0.0)
            t = t + carry_c
            z = f_ref[pl.ds(r0, 8), :] + b_ref[...]
            dz = t * _sigmoid(-z)
            acc_ref[pl.ds(r0, 8), :] = dz
            first = jnp.sum(jnp.where(row == 0, t, 0.0), axis=0, keepdims=True)
            return first, carry_b + jnp.sum(dz, axis=0, keepdims=True)

        zero = jnp.zeros((1, LANE), F32)
        _, db = lax.fori_loop(0, nt, step, (zero, zero))
        db_ref[...] = db
        o_ref[...] = acc_ref[...].astype(BF16)

    vm = pl.BlockSpec(memory_space=pltpu.VMEM)
    return pl.pallas_call(
        body, name="fox_gate_bwd", in_specs=[vm, vm, vm], out_specs=[vm, vm],
        out_shape=[jax.ShapeDtypeStruct((S, LANE), BF16), jax.ShapeDtypeStruct((1, LANE), F32)],
        scratch_shapes=[pltpu.VMEM((S, LANE), F32)],
        compiler_params=_cparams(),
    )(fl, bias, dc)


def _fox_fwd(qkv, gate, c, H):
    S = qkv.shape[0]
    W = H * HEAD_DIM
    HP = H // 2
    tq = _row_tile(S, 512)
    nq = S // tq
    wb = W // LANE
    scale = HEAD_DIM ** -0.5

    def body(q_ref, k_ref, v_ref, g_ref, c_ref, y_ref, o_ref, a_ref,
             kaug_sc, vaug_sc, qaug_sc, s_sc, mb_sc, m_sc, acc_sc):
        hp, qi = pl.program_id(0), pl.program_id(1)
        lane = lax.broadcasted_iota(jnp.int32, (tq, LANE), 1)
        own = [lane < HEAD_DIM, lane >= HEAD_DIM]
        rows = lax.broadcasted_iota(jnp.int32, (tq, tq), 0)
        cols = lax.broadcasted_iota(jnp.int32, (tq, tq), 1)

        def bias_lanes(col, e, first):
            o0 = HEAD_DIM * (1 - e)
            hi = col.astype(BF16)
            r1 = col - hi.astype(F32)
            mid = r1.astype(BF16)
            lo = (r1 - mid.astype(F32)).astype(BF16)
            d0 = o0 if first else o0 + 3
            one = jnp.ones((tq, LANE), BF16)
            zero = jnp.zeros((tq, LANE), BF16)
            t = jnp.where((lane >= o0) & (lane < o0 + 6), one, zero)
            t = jnp.where(lane == d0, hi, t)
            t = jnp.where(lane == d0 + 1, mid, t)
            return jnp.where(lane == d0 + 2, lo, t)

        def head_col(tile, e):
            return jnp.sum(jnp.where(lane == 2 * hp + e, tile, 0.0), axis=1, keepdims=True)

        @pl.when(qi == 0)
        def _():
            def chunk(i, carry):
                r0 = pl.multiple_of(i * tq, tq)
                kb, vb, cb = k_ref[pl.ds(r0, tq), :], v_ref[pl.ds(r0, tq), :], c_ref[pl.ds(r0, tq), :]
                for e in range(2):
                    kaug_sc[e, pl.ds(r0, tq), :] = jnp.where(own[e], kb, bias_lanes(-head_col(cb, e), e, False))
                    vaug_sc[e, pl.ds(r0, tq), :] = jnp.where(own[e], vb, jnp.ones((tq, LANE), BF16))
                return carry
            lax.fori_loop(0, nq, chunk, 0)

        q = q_ref[...] * jnp.asarray(scale, BF16)
        crow = c_ref[pl.ds(pl.multiple_of(qi * tq, tq), tq), :]
        ctq = [head_col(crow, e) for e in range(2)]
        for e in range(2):
            qaug_sc[e] = jnp.where(own[e], q, bias_lanes(ctq[e], e, True))
        m_sc[...] = jnp.full(m_sc.shape, NEG_INF, F32)
        acc_sc[...] = jnp.zeros(acc_sc.shape, F32)

        def scores(blk, slot, masked):
            k0 = pl.multiple_of(blk * tq, tq)
            for e in range(2):
                s = _dot_nt(qaug_sc[e], kaug_sc[e, pl.ds(k0, tq), :])
                if masked:
                    s = jnp.where(rows >= cols, s, NEG_INF)
                s_sc[slot, e] = s
                mb_sc[slot, e] = jnp.broadcast_to(jnp.max(s, axis=1, keepdims=True), (tq, LANE))

        def accumulate(blk, slot):
            k0 = pl.multiple_of(blk * tq, tq)
            for e in range(2):
                m_prev = m_sc[e]
                m_new = jnp.maximum(m_prev, mb_sc[slot, e])
                p = jnp.exp(s_sc[slot, e] - jnp.tile(m_new, (1, tq // LANE)))
                acc_sc[e] = jnp.exp(m_prev - m_new) * acc_sc[e] + _dot_nn(p.astype(BF16), vaug_sc[e, pl.ds(k0, tq), :])
                m_sc[e] = m_new

        def block_of(t):
            return jnp.where(t == 0, qi, t - 1)

        scores(qi, 0, True)

        def loop_body(t, carry):
            scores(t, (t + 1) % 2, False)
            accumulate(block_of(t), t % 2)
            return carry

        lax.fori_loop(0, qi, loop_body, 0)
        accumulate(block_of(qi), qi % 2)
        o_e, a_e = [], []
        for e in range(2):
            acc = acc_sc[e]
            l = pltpu.roll(acc, HEAD_DIM, axis=1)
            o_e.append(acc / l)
            a_e.append(ctq[e] - (m_sc[e] + jnp.log(l)))
        o = jnp.where(own[0], o_e[0], o_e[1])
        g = g_ref[...]
        y_ref[...] = (o * (g * _sigmoid(g))).astype(BF16)
        o_ref[...] = o.astype(BF16)
        a_ref[0] = jnp.where(own[0], a_e[0], a_e[1])

    return pl.pallas_call(
        body, name="fox_attn_fwd", grid=(HP, nq),
        in_specs=[pl.BlockSpec((tq, LANE), lambda h, i: (i, h)),
                  pl.BlockSpec((S, LANE), lambda h, i: (0, wb + h)),
                  pl.BlockSpec((S, LANE), lambda h, i: (0, 2 * wb + h)),
                  pl.BlockSpec((tq, LANE), lambda h, i: (i, h)),
                  pl.BlockSpec((S, LANE), lambda h, i: (0, 0))],
        out_specs=[pl.BlockSpec((tq, LANE), lambda h, i: (i, h)),
                   pl.BlockSpec((tq, LANE), lambda h, i: (i, h)),
                   pl.BlockSpec((1, tq, LANE), lambda h, i: (h, i, 0))],
        out_shape=[jax.ShapeDtypeStruct((S, W), BF16), jax.ShapeDtypeStruct((S, W), BF16),
                   jax.ShapeDtypeStruct((HP, S, LANE), F32)],
        scratch_shapes=[pltpu.VMEM((2, S, LANE), BF16), pltpu.VMEM((2, S, LANE), BF16),
                        pltpu.VMEM((2, tq, LANE), BF16), pltpu.VMEM((2, 2, tq, tq), F32),
                        pltpu.VMEM((2, 2, tq, LANE), F32), pltpu.VMEM((2, tq, LANE), F32),
                        pltpu.VMEM((2, tq, LANE), F32)],
        compiler_params=_cparams(),
    )(qkv, qkv, qkv, gate, c)


def _fox_bwd(qkv, dy, gate, o, a, ct, H):
    S = qkv.shape[0]
    W = H * HEAD_DIM
    HP = H // 2
    tq = _row_tile(S, 512)
    nq = S // tq
    wb = W // LANE
    scale = HEAD_DIM ** -0.5

    def body(q_ref, k_ref, v_ref, dy_ref, g_ref, o_ref, a_ref, ct_ref, out_ref, dcs_ref, dcr_ref,
             do_sc, delta_sc, dq_sc, dk_sc, dv_sc, drow_sc):
        hp, kj = pl.program_id(0), pl.program_id(1)
        lane = lax.broadcasted_iota(jnp.int32, (tq, LANE), 1)
        lo = lane < HEAD_DIM
        rows = lax.broadcasted_iota(jnp.int32, (tq, tq), 0)
        cols = lax.broadcasted_iota(jnp.int32, (tq, tq), 1)

        @pl.when(kj == 0)
        def _():
            def chunk(i, carry):
                r0 = pl.multiple_of(i * tq, tq)
                dyv = dy_ref[pl.ds(r0, tq), :]
                g = g_ref[pl.ds(r0, tq), :]
                ov = o_ref[pl.ds(r0, tq), :].astype(F32)
                sg = _sigmoid(g)
                dob = (dyv * (g * sg)).astype(BF16)
                out_ref[3, pl.ds(r0, tq), :] = (dyv * ov * (sg * (1.0 + g * (1.0 - sg)))).astype(BF16)
                do_sc[pl.ds(r0, tq), :] = dob
                prod = dob.astype(F32) * ov
                d0 = jnp.sum(jnp.where(lo, prod, 0.0), axis=1, keepdims=True)
                d1 = jnp.sum(jnp.where(lo, 0.0, prod), axis=1, keepdims=True)
                delta_sc[pl.ds(r0, tq), :] = jnp.where(lo, d0, d1)
                dq_sc[pl.ds(r0, tq), :] = jnp.zeros((tq, LANE), F32)
                drow_sc[pl.ds(r0, tq), :] = jnp.zeros((tq, LANE), F32)
                return carry
            lax.fori_loop(0, nq, chunk, 0)

        @pl.when((kj == 0) & (hp == 0))
        def _():
            dcr_ref[...] = jnp.zeros(dcr_ref.shape, F32)

        kblk = k_ref[...]
        vblk = v_ref[...]
        zb = jnp.zeros_like(kblk)
        km = [jnp.where(lo, kblk, zb), jnp.where(lo, zb, kblk)]
        vm = [jnp.where(lo, vblk, zb), jnp.where(lo, zb, vblk)]
        dk_sc[...] = jnp.zeros(dk_sc.shape, F32)
        dv_sc[...] = jnp.zeros(dv_sc.shape, F32)
        dcs_ref[...] = jnp.zeros(dcs_ref.shape, F32)

        def step(i, masked):
            r0 = pl.multiple_of(i * tq, tq)
            qt = q_ref[pl.ds(r0, tq), :] * jnp.asarray(scale, BF16)
            dot = do_sc[pl.ds(r0, tq), :]
            at = a_ref[0, pl.ds(r0, tq), :]
            dl = delta_sc[pl.ds(r0, tq), :]
            dq_new, dk_new, dv_new, dr_new = [], [], [], []
            for e in range(2):
                sel = lo if e == 0 else jnp.logical_not(lo)
                a_col = jnp.max(jnp.where(sel, at, -jnp.inf), axis=1, keepdims=True)
                d_col = jnp.max(jnp.where(sel, dl, -jnp.inf), axis=1, keepdims=True)
                cs = ct_ref[pl.ds(2 * hp + e, 1), :]
                s = _dot_nt(qt, km[e]) + (a_col - cs)
                p = jnp.exp(s)
                if masked:
                    p = jnp.where(rows >= cols, p, 0.0)
                dp = _dot_nt(dot, vm[e])
                ds = p * (dp - d_col)
                pb, dsb = p.astype(BF16), ds.astype(BF16)
                dv_new.append(_dot_tn(pb, dot))
                dk_new.append(_dot_tn(dsb, qt))
                dq_new.append(_dot_nn(dsb, kblk))
                dcs_ref[0, pl.ds(e, 1), :] += jnp.sum(ds, axis=0, keepdims=True)
                dr_new.append(jnp.sum(ds, axis=1, keepdims=True))
            drow_sc[pl.ds(r0, tq), :] += jnp.where(lo, dr_new[0], dr_new[1])
            dv_sc[...] += jnp.where(lo, dv_new[0], dv_new[1])
            dk_sc[...] += jnp.where(lo, dk_new[0], dk_new[1])
            dq_sc[pl.ds(r0, tq), :] += jnp.where(lo, dq_new[0], dq_new[1])

        step(kj, True)

        def loop_body(i, carry):
            step(i, False)
            return carry

        lax.fori_loop(kj + 1, nq, loop_body, 0)
        k0 = pl.multiple_of(kj * tq, tq)
        out_ref[1, pl.ds(k0, tq), :] = dk_sc[...].astype(BF16)
        out_ref[2, pl.ds(k0, tq), :] = dv_sc[...].astype(BF16)

        @pl.when(kj == nq - 1)
        def _():
            out_ref[0] = (dq_sc[...] * scale).astype(BF16)

            def chunk(i, carry):
                r0 = pl.multiple_of(i * tq, tq)
                dr = drow_sc[pl.ds(r0, tq), :]
                acc = dcr_ref[pl.ds(r0, tq), :]
                for e in range(2):
                    sel = lo if e == 0 else jnp.logical_not(lo)
                    col = jnp.max(jnp.where(sel, dr, -jnp.inf), axis=1, keepdims=True)
                    acc = jnp.where(lane == 2 * hp + e, col, acc)
                dcr_ref[pl.ds(r0, tq), :] = acc
                return carry
            lax.fori_loop(0, nq, chunk, 0)

    full = lambda cb: pl.BlockSpec((S, LANE), lambda h, j, cb=cb: (0, cb + h))
    return pl.pallas_call(
        body, name="fox_attn_bwd", grid=(HP, nq),
        in_specs=[full(0),
                  pl.BlockSpec((tq, LANE), lambda h, j: (j, wb + h)),
                  pl.BlockSpec((tq, LANE), lambda h, j: (j, 2 * wb + h)),
                  full(0), full(0), full(0),
                  pl.BlockSpec((1, S, LANE), lambda h, j: (h, 0, 0)),
                  pl.BlockSpec((LANE, tq), lambda h, j: (0, j))],
        out_specs=[pl.BlockSpec((4, S, LANE), lambda h, j: (0, 0, h)),
                   pl.BlockSpec((1, 8, tq), lambda h, j: (h, 0, j)),
                   pl.BlockSpec((S, LANE), lambda h, j: (0, 0))],
        out_shape=[jax.ShapeDtypeStruct((4, S, W), BF16), jax.ShapeDtypeStruct((HP, 8, S), F32),
                   jax.ShapeDtypeStruct((S, LANE), F32)],
        scratch_shapes=[pltpu.VMEM((S, LANE), BF16), pltpu.VMEM((S, LANE), F32), pltpu.VMEM((S, LANE), F32),
                        pltpu.VMEM((tq, LANE), F32), pltpu.VMEM((tq, LANE), F32), pltpu.VMEM((S, LANE), F32)],
        compiler_params=_cparams(),
    )(qkv, qkv, qkv, dy, gate, o, a, ct)


def _swa_pick(blk, half, lane):
    b = blk.astype(F32)
    r = pltpu.roll(b, HEAD_DIM, axis=1)
    return jnp.where(jnp.logical_xor(lane < HEAD_DIM, half == 1), b, r).astype(BF16)


def _swa_stack(t, lane, G):
    pieces = []
    z = jnp.zeros((SWA_BLOCK, LANE), t.dtype)
    for j in range(G // 2):
        tile = t[:, LANE * j:LANE * (j + 1)]
        pieces += [jnp.where(lane < HEAD_DIM, tile, z), jnp.where(lane < HEAD_DIM, z, tile)]
    return jnp.concatenate(pieces, axis=0)


def _swa_unstack(st, lane, G):
    tiles = []
    for j in range(G // 2):
        a = st[2 * j * SWA_BLOCK:(2 * j + 1) * SWA_BLOCK]
        b = st[(2 * j + 1) * SWA_BLOCK:(2 * j + 2) * SWA_BLOCK]
        tiles.append(jnp.where(lane < HEAD_DIM, a, b))
    return jnp.concatenate(tiles, axis=1)


def _swa_scores(q_ref, kp_ref, kc_ref, vp_ref, vc_ref, sink_ref, kvh, n, G):
    R = G * SWA_BLOCK
    lane = lax.broadcasted_iota(jnp.int32, (SWA_BLOCK, LANE), 1)
    half = kvh % 2
    kk = jnp.concatenate([_swa_pick(kp_ref[...], half, lane), _swa_pick(kc_ref[...], half, lane)], axis=0)
    vv = jnp.concatenate([_swa_pick(vp_ref[...], half, lane), _swa_pick(vc_ref[...], half, lane)], axis=0)
    qstack = _swa_stack(q_ref[...], lane, G) * jnp.asarray(HEAD_DIM ** -0.5, BF16)
    s = _dot_nt(qstack, kk)
    t_loc = lax.broadcasted_iota(jnp.int32, (R, 2 * SWA_BLOCK), 0) & (SWA_BLOCK - 1)
    j_loc = lax.broadcasted_iota(jnp.int32, (R, 2 * SWA_BLOCK), 1)
    diff = t_loc + SWA_BLOCK - j_loc
    mask = (diff >= 0) & (diff < SWA_BLOCK) & ((n > 0) | (j_loc >= SWA_BLOCK))
    s = jnp.where(mask, s, NEG_INF)
    srow = sink_ref[...]
    lane1 = lax.broadcasted_iota(jnp.int32, (1, LANE), 1)
    sink = jnp.concatenate(
        [jnp.broadcast_to(jnp.sum(jnp.where(lane1 == kvh * G + g, srow, 0.0), axis=1, keepdims=True), (SWA_BLOCK, 1))
         for g in range(G)], axis=0)
    m = jnp.maximum(jnp.max(s, axis=1, keepdims=True), sink)
    e = jnp.exp(s - m)
    es = jnp.exp(sink - m)
    den = jnp.sum(e, axis=1, keepdims=True) + es
    return qstack, kk, vv, e / den, es / den, lane


def _swa_fwd(qkv, gate, sinks, HQ, HKV):
    S = qkv.shape[0]
    G = HQ // HKV
    WQ, KVW = HQ * HEAD_DIM, HKV * HEAD_DIM
    nb = S // SWA_BLOCK
    GW = G * HEAD_DIM
    kb, vb = WQ // LANE, (WQ + KVW) // LANE

    def body(q_ref, kp_ref, kc_ref, vp_ref, vc_ref, g_ref, sink_ref, y_ref, o_ref):
        kvh, n = pl.program_id(0), pl.program_id(1)
        _, _, vv, p, _, lane = _swa_scores(q_ref, kp_ref, kc_ref, vp_ref, vc_ref, sink_ref, kvh, n, G)
        o = _swa_unstack(_dot_nn(p.astype(BF16), vv), lane, G)
        g = g_ref[...]
        y_ref[...] = (o * (g * _sigmoid(g))).astype(BF16)
        o_ref[...] = o.astype(BF16)

    blk = lambda cb, prev: pl.BlockSpec(
        (SWA_BLOCK, LANE), lambda h, n, cb=cb, prev=prev: (jnp.maximum(n - prev, 0), cb + h // 2))
    qspec = pl.BlockSpec((SWA_BLOCK, GW), lambda h, n: (n, h))
    return pl.pallas_call(
        body, name="swa_attn_fwd", grid=(HKV, nb),
        in_specs=[qspec, blk(kb, 1), blk(kb, 0), blk(vb, 1), blk(vb, 0), qspec,
                  pl.BlockSpec((1, LANE), lambda h, n: (0, 0))],
        out_specs=[qspec, qspec],
        out_shape=[jax.ShapeDtypeStruct((S, WQ), BF16), jax.ShapeDtypeStruct((S, WQ), BF16)],
        compiler_params=_cparams(),
    )(qkv, qkv, qkv, qkv, qkv, gate, sinks)


def _swa_bwd(qkv, dy, gate, o, sinks, tables, HQ, HKV):
    S = qkv.shape[0]
    G = HQ // HKV
    WQ, KVW = HQ * HEAD_DIM, HKV * HEAD_DIM
    nb = S // SWA_BLOCK
    GW = G * HEAD_DIM
    R = G * SWA_BLOCK
    kb, vb = WQ // LANE, (WQ + KVW) // LANE
    scale = HEAD_DIM ** -0.5
    assert G == 8

    def body(q_ref, kp_ref, kc_ref, vp_ref, vc_ref, dy_ref, g_ref, o_ref, sink_ref, t_ref,
             dqg_ref, dkv_ref, dsink_ref, carry_sc):
        kvh, n = pl.program_id(0), pl.program_id(1)

        @pl.when(n == 0)
        def _():
            carry_sc[...] = jnp.zeros(carry_sc.shape, F32)
            dsink_ref[...] = jnp.zeros(dsink_ref.shape, F32)

        @pl.when(n < nb)
        def _():
            qstack, kk, vv, p, psink, lane = _swa_scores(q_ref, kp_ref, kc_ref, vp_ref, vc_ref, sink_ref, kvh, n, G)
            dyv, g, ov = dy_ref[...], g_ref[...], o_ref[...].astype(F32)
            sg = _sigmoid(g)
            dob = (dyv * (g * sg)).astype(BF16)
            dqg_ref[1] = (dyv * ov * (sg * (1.0 + g * (1.0 - sg)))).astype(BF16)
            prod = dob.astype(F32) * ov
            dparts = []
            for j in range(G // 2):
                tile = prod[:, LANE * j:LANE * (j + 1)]
                dparts += [jnp.sum(jnp.where(lane < HEAD_DIM, tile, 0.0), axis=1, keepdims=True),
                           jnp.sum(jnp.where(lane < HEAD_DIM, 0.0, tile), axis=1, keepdims=True)]
            delta = jnp.concatenate(dparts, axis=0)
            dostack = _swa_stack(dob, lane, G)
            dp = _dot_nt(dostack, vv)
            ds = p * (dp - delta)
            dsb, pb = ds.astype(BF16), p.astype(BF16)
            dq = _swa_unstack(_dot_nn(dsb, kk), lane, G) * scale
            dq = (dq * t_ref[0] + pltpu.roll(dq * t_ref[1], ROT_DIM // 2, axis=1)
                  + pltpu.roll(dq * t_ref[2], GW - ROT_DIM // 2, axis=1))
            dqg_ref[0] = dq.astype(BF16)
            dkk = _dot_tn(dsb, qstack)
            dvv = _dot_tn(pb, dostack)
            dkk = dkk + pltpu.roll(dkk, HEAD_DIM, axis=1)
            dvv = dvv + pltpu.roll(dvv, HEAD_DIM, axis=1)
            lane2 = lax.broadcasted_iota(jnp.int32, (2 * SWA_BLOCK, LANE), 1)
            comb = jnp.where(lane2 < HEAD_DIM, dkk, dvv)
            dkv_ref[0] = carry_sc[...] + comb[:SWA_BLOCK]
            carry_sc[...] = comb[SWA_BLOCK:]
            sk = psink * delta
            rows = [jnp.broadcast_to(-jnp.sum(sk[g_ * SWA_BLOCK:(g_ + 1) * SWA_BLOCK], axis=0, keepdims=True), (1, LANE))
                    for g_ in range(G)]
            dsink_ref[0] += jnp.concatenate(rows, axis=0)

        @pl.when(n == nb)
        def _():
            dkv_ref[0] = carry_sc[...]

    cl = lambda n: jnp.minimum(n, nb - 1)
    blk = lambda cb, prev: pl.BlockSpec(
        (SWA_BLOCK, LANE), lambda h, n, cb=cb, prev=prev: (jnp.maximum(cl(n) - prev, 0), cb + h // 2))
    qspec = pl.BlockSpec((SWA_BLOCK, GW), lambda h, n: (cl(n), h))
    return pl.pallas_call(
        body, name="swa_attn_bwd", grid=(HKV, nb + 1),
        in_specs=[qspec, blk(kb, 1), blk(kb, 0), blk(vb, 1), blk(vb, 0), qspec, qspec, qspec,
                  pl.BlockSpec((1, LANE), lambda h, n: (0, 0)),
                  pl.BlockSpec((3, SWA_BLOCK, GW), lambda h, n: (0, cl(n), 0))],
        out_specs=[pl.BlockSpec((2, SWA_BLOCK, GW), lambda h, n: (0, cl(n), h)),
                   pl.BlockSpec((1, SWA_BLOCK, LANE), lambda h, n: (h, jnp.maximum(n - 1, 0), 0)),
                   pl.BlockSpec((1, 8, LANE), lambda h, n: (h, 0, 0))],
        out_shape=[jax.ShapeDtypeStruct((2, S, WQ), BF16), jax.ShapeDtypeStruct((HKV, S, LANE), F32),
                   jax.ShapeDtypeStruct((HKV, 8, LANE), F32)],
        scratch_shapes=[pltpu.VMEM((SWA_BLOCK, LANE), F32)],
        compiler_params=_cparams(),
    )(qkv, qkv, qkv, qkv, qkv, dy, gate, o, sinks, tables)


def _swa_dkv_finish(dkv, tables):
    HKV, S, _ = dkv.shape
    KVW = HKV * HEAD_DIM
    tm = _row_tile(S, 512)
    npair = HKV // 2

    def body(d_ref, t_ref, o_ref):
        lane = lax.broadcasted_iota(jnp.int32, (tm, LANE), 1)
        lo = lane < HEAD_DIM
        for p in range(npair):
            a, b = d_ref[2 * p], d_ref[2 * p + 1]
            tk = jnp.where(lo, a, pltpu.roll(b, HEAD_DIM, axis=1))
            tv = jnp.where(lo, pltpu.roll(a, HEAD_DIM, axis=1), b)
            tk = (tk * t_ref[0] + pltpu.roll(tk * t_ref[1], ROT_DIM // 2, axis=1)
                  + pltpu.roll(tk * t_ref[2], LANE - ROT_DIM // 2, axis=1))
            o_ref[:, LANE * p:LANE * (p + 1)] = tk.astype(BF16)
            o_ref[:, KVW + LANE * p:KVW + LANE * (p + 1)] = tv.astype(BF16)

    return pl.pallas_call(
        body, name="swa_dkv_finish", grid=(S // tm,),
        in_specs=[pl.BlockSpec((HKV, tm, LANE), lambda i: (0, i, 0)), pl.BlockSpec((3, tm, LANE), lambda i: (0, i, 0))],
        out_specs=pl.BlockSpec((tm, 2 * KVW), lambda i: (i, 0)),
        out_shape=jax.ShapeDtypeStruct((S, 2 * KVW), BF16),
        compiler_params=_cparams(),
    )(dkv, tables)


def _rope_tables(S, width):
    half = ROT_DIM // 2
    pos = jnp.arange(S, dtype=F32)
    inv_freq = ROPE_THETA ** (-jnp.arange(half, dtype=F32) / half)
    ang = pos[:, None] * inv_freq[None, :]
    cos, sin = jnp.cos(ang), jnp.sin(ang)
    one = jnp.ones((S, HEAD_DIM - ROT_DIM), F32)
    zero = jnp.zeros((S, HEAD_DIM - ROT_DIM), F32)
    zh = jnp.zeros((S, half), F32)
    t0 = jnp.concatenate([cos, cos, one], axis=1)
    t1 = jnp.concatenate([-sin, zh, zero], axis=1)
    t2 = jnp.concatenate([zh, sin, zero], axis=1)
    return jnp.stack([jnp.tile(t, (1, width // HEAD_DIM)) for t in (t0, t1, t2)])


def _pad_rows(v, row, total_rows=8):
    return jnp.pad(v, ((row, total_rows - row - v.shape[0]), (0, 0)))


def _pad_lanes(v, off, width):
    return jnp.pad(v, ((0, 0), (off, width - off - v.shape[1])))


def kernel(x, norm_g, fox_w_in, fox_b_f, fox_w_out, swa_w_in, swa_sinks, swa_w_out, final_g, loss_target, m_norm_g, m_fox_w_in, m_fox_b_f, m_fox_w_out, m_swa_w_in, m_swa_sinks, m_swa_w_out, m_final_g, v_norm_g, v_fox_w_in, v_fox_b_f, v_fox_w_out, v_swa_w_in, v_swa_sinks, v_swa_w_out, v_final_g):
    S, D = x.shape[1], x.shape[2]
    H = fox_b_f.shape[1]
    W = H * HEAD_DIM
    wf = fox_w_in.shape[2]
    ws = swa_w_in.shape[2]
    HQ = swa_sinks.shape[1]
    WQ = HQ * HEAD_DIM
    KVW = (ws * N_DEV - 2 * WQ) // 2
    HKV = KVW // HEAD_DIM
    rows_o = fox_w_out.shape[1]
    assert wf * N_DEV == 4 * W + H and rows_o * N_DEV == W and H <= LANE and HQ <= LANE
    me = _my_index()

    _, sw_f, np_f = _slab_geom(wf)
    _, sw_s, np_s = _slab_geom(ws)

    def slab(w2d, w, sw):
        off = (w * me) % LANE
        return lax.dynamic_update_slice(jnp.zeros((w2d.shape[0], sw), BF16), w2d.astype(BF16), (0, off))

    fi_all, si_all, fo_all, so_all = _all_gather([
        slab(fox_w_in[0], wf, sw_f), slab(swa_w_in[0], ws, sw_s),
        fox_w_out[0].astype(BF16), swa_w_out[0].astype(BF16)])
    w_fi = _assemble(fi_all, wf)
    w_si = _assemble(si_all, ws)
    w_fo = fo_all.reshape(W, D)
    w_so = so_all.reshape(WQ, D)

    x0 = x[0]
    g0, g1, gf = norm_g[0:1], norm_g[1:2], final_g[None, :]
    bias = _pad_lanes(fox_b_f, 0, LANE)
    sinks = _pad_lanes(swa_sinks, 0, LANE)
    tab_q = _rope_tables(S, 8 * HEAD_DIM)
    tab_k = tab_q[:, :, :LANE]

    h0 = _rmsnorm_fwd(x0, g0, "rmsnorm0")
    qkv0 = _proj(h0, w_fi, 0, 3 * W, BF16, "fox_in_qkv")
    gate0 = _proj(h0, w_fi, 3 * W, W, F32, "fox_in_gate")
    fl = _proj(h0, w_fi, 4 * W, LANE, F32, "fox_in_f")
    c, ct = _fox_gate_fwd(fl, bias)
    y0, o0, a0 = _fox_fwd(qkv0, gate0, c, H)
    x1 = _out_proj_res(y0, w_fo, x0, "fox_out")

    h1 = _rmsnorm_fwd(x1, g1, "rmsnorm1")
    qkv1 = _proj(h1, w_si, 0, WQ + 2 * KVW, BF16, "swa_in_qkv", rope=(tab_k, WQ + KVW))
    gate1 = _proj(h1, w_si, WQ + 2 * KVW, WQ, F32, "swa_in_gate")
    y1, o1 = _swa_fwd(qkv1, gate1, sinks, HQ, HKV)
    x2 = _out_proj_res(y1, w_so, x1, "swa_out")

    dx2, dx2b, dgf, loss_p = _loss_head(x2, loss_target[0], gf)

    dy1 = _matmul_nt([(dx2b, None, 0)], w_so, WQ, "swa_out_bwd")
    g_so = _matmul_tn(y1, [(dx2b, None, 0)], D, "swa_out_wgrad")
    dqg1, dkv1, dsink = _swa_bwd(qkv1, dy1, gate1, o1, sinks, tab_q, HQ, HKV)
    dkv1f = _swa_dkv_finish(dkv1, tab_k)
    parts1 = [(dqg1, 0, 0), (dkv1f, None, WQ), (dqg1, 1, WQ + 2 * KVW)]
    g_si = _matmul_tn(h1, parts1, np_s, "swa_in_wgrad", tile_major=True)
    dh1 = _matmul_nt(parts1, w_si, D, "swa_in_bwd")
    dx1, dx1b, dg1 = _rmsnorm_bwd(dh1, x1, g1, dx2, "rmsnorm1_bwd")

    dy0 = _matmul_nt([(dx1b, None, 0)], w_fo, W, "fox_out_bwd")
    g_fo = _matmul_tn(y0, [(dx1b, None, 0)], D, "fox_out_wgrad")
    dqkvg0, dcs, dcr = _fox_bwd(qkv0, dy0, gate0, o0, a0, ct, H)
    dcs_t = _pad_lanes(dcs[:, :2, :].reshape(H, S).T, 0, LANE)
    dfl, dbf = _fox_gate_bwd(fl, bias, dcr - dcs_t)
    parts0 = [(dqkvg0, p, p * W) for p in range(4)] + [(dfl, None, 4 * W)]
    g_fi = _matmul_tn(h0, parts0, np_f, "fox_in_wgrad", tile_major=True)
    dh0 = _matmul_nt(parts0, w_fi, D, "fox_in_bwd")
    dx0, _, dg0 = _rmsnorm_bwd(dh0, x0, g0, dx1, "rmsnorm0_bwd")

    specs = [("col", wf), ("col", ws), ("row", rows_o), ("row", rows_o)]
    gfull = [g_fi, g_si, g_fo, g_so]
    recv = _reduce_scatter_stage1(gfull, specs)
    sums = [_pair_sum(g_, r_, s_) for g_, r_, s_ in zip(gfull, recv, specs)]
    recv2 = _reduce_scatter_stage2([s_[1] for s_ in sums])
    red_fi = _final_sum_cols(sums[0][0], recv2[0])
    red_si = _final_sum_cols(sums[1][0], recv2[1])
    gw_fi = lax.dynamic_slice(red_fi, (0, (wf * me) % LANE), (D, wf))
    gw_si = lax.dynamic_slice(red_si, (0, (ws * me) % LANE), (D, ws))
    gw_fo, gw_so = _final_sum(sums[2][0], recv2[2]), _final_sum(sums[3][0], recv2[3])

    P = D
    dsink_v = dsink[:, :, 0].reshape(1, HQ)
    row3 = _pad_lanes(dbf[:, :H], 0, P) + _pad_lanes(dsink_v, LANE, P) + _pad_lanes(loss_p[:, :1], 2 * LANE, P)
    pack = _pad_rows(dg0, 0) + _pad_rows(dg1, 1) + _pad_rows(dgf, 2) + _pad_rows(row3, 3)
    tot = _all_reduce_small(pack)
    loss = tot[3, 2 * LANE]
    g_norm = tot[0:2]
    g_final = tot[2]
    g_bf = tot[3:4, 0:H]
    g_sinks = tot[3:4, LANE:LANE + HQ]

    def small_pack(ng, fg, bf, sk):
        r3 = _pad_lanes(bf, 0, P) + _pad_lanes(sk, LANE, P)
        return _pad_rows(ng, 0) + _pad_rows(fg[None, :], 2) + _pad_rows(r3, 3)

    sd, sm, sv = _adamw(small_pack(norm_g, final_g, fox_b_f, swa_sinks), tot,
                        small_pack(m_norm_g, m_final_g, m_fox_b_f, m_swa_sinks),
                        small_pack(v_norm_g, v_final_g, v_fox_b_f, v_swa_sinks), "adamw_small")

    def unpack(t):
        return t[0:2], t[3:4, 0:H], t[3:4, LANE:LANE + HQ], t[2]

    d_fi, m_fi, v_fi = _adamw(fox_w_in[0], gw_fi, m_fox_w_in[0], v_fox_w_in[0], "adamw_fox_in")
    d_fo, m_fo, v_fo = _adamw(fox_w_out[0], gw_fo, m_fox_w_out[0], v_fox_w_out[0], "adamw_fox_out")
    d_si, m_si, v_si = _adamw(swa_w_in[0], gw_si, m_swa_w_in[0], v_swa_w_in[0], "adamw_swa_in")
    d_so, m_so, v_so = _adamw(swa_w_out[0], gw_so, m_swa_w_out[0], v_swa_w_out[0], "adamw_swa_out")

    def group(small, fi, fo, si, so):
        ng, bf, sk, fg = unpack(small)
        return (ng, fi[None], bf, fo[None], si[None], sk, so[None], fg)

    grads = (g_norm, gw_fi[None], g_bf, gw_fo[None], gw_si[None], g_sinks, gw_so[None], g_final)
    return (loss, dx0[None], *grads, *group(sd, d_fi, d_fo, d_si, d_so),
            *group(sm, m_fi, m_fo, m_si, m_so), *group(sv, v_fi, v_fo, v_si, v_so))
```

```python
import functools
import math

import jax
import jax.numpy as jnp
from jax import lax
from jax.experimental import pallas as pl
from jax.experimental.pallas import tpu as pltpu

F32 = jnp.float32
BF16 = jnp.bfloat16
MESH = pl.DeviceIdType.MESH

N_DEV = 8
LANE = 128
HEAD_DIM = 64
SWA_BLOCK = 128
NEG_INF = -1e30
RMS_EPS = 1e-6
ROPE_THETA = 500000.0
ROT_DIM = HEAD_DIM // 4
ADAM_LR, ADAM_B1, ADAM_B2, ADAM_EPS, ADAM_WD, ADAM_STEP = 0.001, 0.9, 0.999, 1e-08, 0.01, 10
VMEM_LIMIT = 56 * 1024 * 1024
MM_TILE = 1024


def _cparams(**kw):
    return pltpu.CompilerParams(vmem_limit_bytes=VMEM_LIMIT, **kw)


def _tile(n, cap):
    if n <= cap:
        return n
    t = (cap // LANE) * LANE
    while t > LANE and n % t:
        t -= LANE
    assert n % t == 0, (n, cap)
    return t


def _row_tile(n, cap):
    t = min(n, cap)
    while n % t:
        t //= 2
    return t


def _dot_nn(a, b):
    return jnp.dot(a, b, preferred_element_type=F32)


def _dot_nt(a, b):
    return lax.dot_general(a, b, (((1,), (1,)), ((), ())), preferred_element_type=F32)


def _dot_tn(a, b):
    return lax.dot_general(a, b, (((0,), (0,)), ((), ())), preferred_element_type=F32)


def _sigmoid(g):
    return 1.0 / (1.0 + jnp.exp(-g))


def _slab_geom(w):
    starts = [w * i for i in range(N_DEV)]
    aligned = [LANE * (s // LANE) for s in starts]
    offs = [s - a for s, a in zip(starts, aligned)]
    sw = LANE * (-(-(max(offs) + w) // LANE))
    return aligned, sw, aligned[-1] + sw


def _my_index():
    return 4 * lax.axis_index("x") + 2 * lax.axis_index("y") + lax.axis_index("c")


def _all_gather(arrs):
    n = len(arrs)

    def body(*refs):
        ins, outs = refs[:n], refs[n:2 * n]
        send_sems, recv_sems, local_sems = refs[2 * n:]
        x, y, c = lax.axis_index("x"), lax.axis_index("y"), lax.axis_index("c")
        me, sib = (x, y, c), (x, y, 1 - c)
        chips = [(1 - x, y), (x, 1 - y), (1 - x, 1 - y)]

        def idx(px, py, pc):
            return 4 * px + 2 * py + pc

        def copy(a, k, block, to, src=None):
            dst = outs[a].at[idx(*block)]
            return pltpu.make_async_remote_copy(
                src_ref=dst if src is None else src, dst_ref=dst,
                send_sem=send_sems.at[a, k], recv_sem=recv_sems.at[a, k],
                device_id=to, device_id_type=MESH)

        mine = [pltpu.make_async_copy(ins[a], outs[a].at[idx(*me)], local_sems.at[a]) for a in range(n)]
        for m in mine:
            m.start()
        first = []
        for a in range(n):
            first.append(copy(a, 0, me, sib, src=ins[a]))
            for j, chip in enumerate(chips):
                first.append(copy(a, 1 + j, me, (*chip, c), src=ins[a]))
        for cp in first:
            cp.start()
        passed = []
        for j, chip in enumerate(chips):
            for a in range(n):
                copy(a, 1 + j, (*chip, c), me).wait_recv()
                p = copy(a, 4 + j, (*chip, c), sib)
                p.start()
                passed.append(p)
        for a in range(n):
            copy(a, 0, sib, me).wait_recv()
        for j, chip in enumerate(chips):
            for a in range(n):
                copy(a, 4 + j, (*chip, 1 - c), me).wait_recv()
        for cp in first + passed:
            cp.wait_send()
        for m in mine:
            m.wait()

    any_spec = pl.BlockSpec(memory_space=pl.ANY)
    return pl.pallas_call(
        body, name="weights_all_gather",
        out_shape=[jax.ShapeDtypeStruct((N_DEV,) + a.shape, a.dtype) for a in arrs],
        in_specs=[any_spec] * n, out_specs=[any_spec] * n,
        scratch_shapes=[pltpu.SemaphoreType.DMA((n, 7)), pltpu.SemaphoreType.DMA((n, 7)),
                        pltpu.SemaphoreType.DMA((n,))],
    )(*arrs)


def _rel_chip(r):
    x, y = lax.axis_index("x"), lax.axis_index("y")
    return (x ^ (r >> 1), y ^ (r & 1))


def _rs_windows(specs):
    def window(ref, spec, blk):
        kind, n = spec
        if kind == "col":
            _, sw, _ = _slab_geom(n)
            return ref.at[pl.ds((n * blk) // LANE, sw // LANE)]
        start = pl.multiple_of(n * blk, n)
        return ref.at[pl.ds(start, n), :]
    return window


def _reduce_scatter_stage1(grads, specs):
    n = len(grads)
    window = _rs_windows(specs)

    def blk_shape(g, spec):
        kind, w = spec
        return (_slab_geom(w)[1] // LANE, g.shape[1], LANE) if kind == "col" else (w, g.shape[1])

    shapes = [blk_shape(g, s) for g, s in zip(grads, specs)]

    def body(*refs):
        ins, recvs = refs[:n], refs[n:2 * n]
        send_sems, recv_sems = refs[2 * n:]
        x, y, c = lax.axis_index("x"), lax.axis_index("y"), lax.axis_index("c")
        sib = (x, y, 1 - c)
        remotes = []
        for a in range(n):
            for r in range(4):
                px, py = _rel_chip(r)
                sib_blk = 4 * px + 2 * py + (1 - c)
                remotes.append(pltpu.make_async_remote_copy(
                    src_ref=window(ins[a], specs[a], sib_blk), dst_ref=recvs[a].at[r],
                    send_sem=send_sems.at[a, r], recv_sem=recv_sems.at[a, r],
                    device_id=sib, device_id_type=MESH))
        for cp in remotes:
            cp.start()
        for cp in remotes:
            cp.wait_recv()
        for cp in remotes:
            cp.wait_send()

    any_spec = pl.BlockSpec(memory_space=pl.ANY)
    return pl.pallas_call(
        body, name="grads_rs_sibling",
        out_shape=[jax.ShapeDtypeStruct((4,) + s, F32) for s in shapes],
        in_specs=[any_spec] * n, out_specs=[any_spec] * n,
        scratch_shapes=[pltpu.SemaphoreType.DMA((n, 4)), pltpu.SemaphoreType.DMA((n, 4))],
    )(*grads)


def _own_block_offsets(spec):
    kind, n = spec
    c = lax.axis_index("c")
    offs = []
    for r in range(4):
        px, py = _rel_chip(r)
        blk = 4 * px + 2 * py + c
        offs.append((n * blk) // LANE if kind == "col" else blk)
    return jnp.stack(offs).astype(jnp.int32)


def _pair_sum(g, recv, spec):
    kind, _ = spec
    offs = _own_block_offsets(spec)
    if kind == "col":
        _, T, M, _ = recv.shape
        grid = (4, T)
        g_spec = pl.BlockSpec((1, M, LANE), lambda r, t, o: (o[r] + t, 0, 0))
        r_spec = pl.BlockSpec((1, 1, M, LANE), lambda r, t, o: (r, t, 0, 0))
    else:
        _, nrow, C = recv.shape
        grid = (4,)
        g_spec = pl.BlockSpec((nrow, C), lambda r, o: (o[r], 0))
        r_spec = pl.BlockSpec((1, nrow, C), lambda r, o: (r, 0, 0))

    def body(o_ref, g_ref, r_ref, f_ref, h_ref):
        if kind == "col":
            s = g_ref[0] + r_ref[0, 0]
            f_ref[0, 0] = s
            h_ref[0, 0] = s.astype(BF16)
        else:
            s = g_ref[...] + r_ref[0]
            f_ref[0] = s
            h_ref[0] = s.astype(BF16)

    return pl.pallas_call(
        body, name="grads_pair_sum",
        grid_spec=pltpu.PrefetchScalarGridSpec(num_scalar_prefetch=1, grid=grid, in_specs=[g_spec, r_spec],
                                               out_specs=[r_spec, r_spec]),
        out_shape=[jax.ShapeDtypeStruct(recv.shape, F32), jax.ShapeDtypeStruct(recv.shape, BF16)],
        compiler_params=_cparams(),
    )(offs, g, recv)


def _reduce_scatter_stage2(parts):
    n = len(parts)

    def body(*refs):
        ins, recvs = refs[:n], refs[n:2 * n]
        send_sems, recv_sems = refs[2 * n:]
        c = lax.axis_index("c")
        copies = []
        for a in range(n):
            for r in range(1, 4):
                px, py = _rel_chip(r)
                copies.append(pltpu.make_async_remote_copy(
                    src_ref=ins[a].at[r], dst_ref=recvs[a].at[r - 1],
                    send_sem=send_sems.at[a, r - 1], recv_sem=recv_sems.at[a, r - 1],
                    device_id=(px, py, c), device_id_type=MESH))
        for cp in copies:
            cp.start()
        for cp in copies:
            cp.wait_recv()
        for cp in copies:
            cp.wait_send()

    any_spec = pl.BlockSpec(memory_space=pl.ANY)
    return pl.pallas_call(
        body, name="grads_rs_chips",
        out_shape=[jax.ShapeDtypeStruct((3,) + p.shape[1:], BF16) for p in parts],
        in_specs=[any_spec] * n, out_specs=[any_spec] * n,
        scratch_shapes=[pltpu.SemaphoreType.DMA((n, 3)), pltpu.SemaphoreType.DMA((n, 3))],
    )(*parts)


def _final_sum(psum, recv):
    _, R, C = psum.shape
    tr = _row_tile(R, 256)

    def body(p_ref, r_ref, o_ref):
        r = r_ref[...].astype(F32)
        o_ref[...] = ((p_ref[0] + r[0]) + r[1]) + r[2]

    return pl.pallas_call(
        body, name="grads_final_sum", grid=(R // tr,),
        in_specs=[pl.BlockSpec((1, tr, C), lambda i: (0, i, 0)), pl.BlockSpec((3, tr, C), lambda i: (0, i, 0))],
        out_specs=pl.BlockSpec((tr, C), lambda i: (i, 0)),
        out_shape=jax.ShapeDtypeStruct((R, C), F32),
        compiler_params=_cparams(),
    )(psum, recv)


def _final_sum_cols(psum, recv):
    _, T, M, _ = psum.shape

    def body(p_ref, r_ref, o_ref):
        r = r_ref[...].astype(F32)
        o_ref[...] = ((p_ref[0, 0] + r[0, 0]) + r[1, 0]) + r[2, 0]

    return pl.pallas_call(
        body, name="grads_final_sum_cols", grid=(T,),
        in_specs=[pl.BlockSpec((1, 1, M, LANE), lambda t: (0, t, 0, 0)),
                  pl.BlockSpec((3, 1, M, LANE), lambda t: (0, t, 0, 0))],
        out_specs=pl.BlockSpec((M, LANE), lambda t: (0, t)),
        out_shape=jax.ShapeDtypeStruct((M, T * LANE), F32),
        compiler_params=_cparams(),
    )(psum, recv)


def _all_reduce_small(pack):
    R, P = pack.shape

    def body(x_ref, o_ref, gat_ref, send_sems, recv_sems):
        x, y, c = lax.axis_index("x"), lax.axis_index("y"), lax.axis_index("c")
        me = 4 * x + 2 * y + c
        gat_ref[me] = x_ref[...]
        copies = []
        for k in range(1, N_DEV):
            peer = (x ^ (k >> 2), y ^ ((k >> 1) & 1), c ^ (k & 1))
            copies.append(pltpu.make_async_remote_copy(
                src_ref=x_ref, dst_ref=gat_ref.at[me],
                send_sem=send_sems.at[k - 1], recv_sem=recv_sems.at[k - 1],
                device_id=peer, device_id_type=MESH))
        for cp in copies:
            cp.start()
        for cp in copies:
            cp.wait_recv()
        for cp in copies:
            cp.wait_send()
        acc = gat_ref[0]
        for d in range(1, N_DEV):
            acc = acc + gat_ref[d]
        o_ref[...] = acc

    vm = pl.BlockSpec(memory_space=pltpu.VMEM)
    return pl.pallas_call(
        body, name="small_all_reduce",
        out_shape=jax.ShapeDtypeStruct((R, P), F32),
        in_specs=[vm], out_specs=vm,
        scratch_shapes=[pltpu.VMEM((N_DEV, R, P), F32),
                        pltpu.SemaphoreType.DMA((N_DEV - 1,)), pltpu.SemaphoreType.DMA((N_DEV - 1,))],
    )(pack)


def _assemble(slabs, w):
    aligned, sw, total = _slab_geom(w)
    K = slabs.shape[1]
    tr = _row_tile(K, 256)

    def body(s_ref, o_ref):
        o_ref[...] = jnp.zeros(o_ref.shape, BF16)
        for i in range(N_DEV):
            a = aligned[i]
            o_ref[:, a:a + sw] = o_ref[:, a:a + sw] + s_ref[i]

    return pl.pallas_call(
        body, name="assemble_w_in", grid=(K // tr,),
        in_specs=[pl.BlockSpec((N_DEV, tr, sw), lambda i: (0, i, 0))],
        out_specs=pl.BlockSpec((tr, total), lambda i: (i, 0)),
        out_shape=jax.ShapeDtypeStruct((K, total), BF16),
        compiler_params=_cparams(),
    )(slabs)


def _rmsnorm_fwd(x, g, name):
    S, D = x.shape
    tm = _row_tile(S, 256)

    def body(x_ref, g_ref, h_ref):
        xv = x_ref[...]
        r = lax.rsqrt(jnp.mean(xv * xv, axis=-1, keepdims=True) + RMS_EPS)
        h_ref[...] = ((xv * r) * g_ref[...]).astype(BF16)

    return pl.pallas_call(
        body, name=name, grid=(S // tm,),
        in_specs=[pl.BlockSpec((tm, D), lambda i: (i, 0)), pl.BlockSpec((1, D), lambda i: (0, 0))],
        out_specs=pl.BlockSpec((tm, D), lambda i: (i, 0)),
        out_shape=jax.ShapeDtypeStruct((S, D), BF16),
        compiler_params=_cparams(),
    )(x, g)


def _rmsnorm_bwd(dh, x, g, dres, name):
    S, D = x.shape
    tm = _row_tile(S, 256)

    def body(dh_ref, x_ref, g_ref, dr_ref, dx_ref, dxb_ref, dg_ref):
        xv = x_ref[...]
        r = lax.rsqrt(jnp.mean(xv * xv, axis=-1, keepdims=True) + RMS_EPS)
        xhat = xv * r
        d = dh_ref[...]
        gd = d * g_ref[...]
        dx = r * (gd - xhat * jnp.mean(gd * xhat, axis=-1, keepdims=True)) + dr_ref[...]
        dx_ref[...] = dx
        dxb_ref[...] = dx.astype(BF16)

        @pl.when(pl.program_id(0) == 0)
        def _():
            dg_ref[...] = jnp.zeros(dg_ref.shape, F32)
        dg_ref[...] += jnp.sum(d * xhat, axis=0, keepdims=True)

    row = pl.BlockSpec((tm, D), lambda i: (i, 0))
    vec = pl.BlockSpec((1, D), lambda i: (0, 0))
    return pl.pallas_call(
        body, name=name, grid=(S // tm,),
        in_specs=[row, row, vec, row], out_specs=[row, row, vec],
        out_shape=[jax.ShapeDtypeStruct((S, D), F32), jax.ShapeDtypeStruct((S, D), BF16),
                   jax.ShapeDtypeStruct((1, D), F32)],
        compiler_params=_cparams(),
    )(dh, x, g, dres)


def _loss_head(x, tgt, g):
    S, D = x.shape
    tm = _row_tile(S, 256)

    def body(x_ref, t_ref, g_ref, dx_ref, dxb_ref, dg_ref, loss_ref):
        xv = x_ref[...]
        r = lax.rsqrt(jnp.mean(xv * xv, axis=-1, keepdims=True) + RMS_EPS)
        xhat = xv * r
        gv = g_ref[...]
        err = xhat * gv - t_ref[...]
        d = err * (1.0 / D)
        gd = d * gv
        dx = r * (gd - xhat * jnp.mean(gd * xhat, axis=-1, keepdims=True))
        dx_ref[...] = dx
        dxb_ref[...] = dx.astype(BF16)

        @pl.when(pl.program_id(0) == 0)
        def _():
            dg_ref[...] = jnp.zeros(dg_ref.shape, F32)
            loss_ref[...] = jnp.zeros(loss_ref.shape, F32)
        dg_ref[...] += jnp.sum(d * xhat, axis=0, keepdims=True)
        per_tok = jnp.sum(err * err, axis=-1, keepdims=True) * (1.0 / D)
        loss_ref[...] += 0.5 * jnp.sum(per_tok, axis=0, keepdims=True)

    row = pl.BlockSpec((tm, D), lambda i: (i, 0))
    vec = pl.BlockSpec((1, D), lambda i: (0, 0))
    return pl.pallas_call(
        body, name="loss_head", grid=(S // tm,),
        in_specs=[row, row, vec],
        out_specs=[row, row, vec, pl.BlockSpec((1, LANE), lambda i: (0, 0))],
        out_shape=[jax.ShapeDtypeStruct((S, D), F32), jax.ShapeDtypeStruct((S, D), BF16),
                   jax.ShapeDtypeStruct((1, D), F32), jax.ShapeDtypeStruct((1, LANE), F32)],
        compiler_params=_cparams(),
    )(x, tgt, g)


def _adamw(w, g, m, v, name):
    R, C = w.shape
    tr = _row_tile(R, 256)
    c1 = 1.0 - ADAM_B1 ** ADAM_STEP
    c2 = 1.0 - ADAM_B2 ** ADAM_STEP

    def body(w_ref, g_ref, m_ref, v_ref, d_ref, nm_ref, nv_ref):
        gv = g_ref[...]
        nm = ADAM_B1 * m_ref[...] + (1.0 - ADAM_B1) * gv
        nv = ADAM_B2 * v_ref[...] + (1.0 - ADAM_B2) * (gv * gv)
        d_ref[...] = -ADAM_LR * ((nm / c1) / (jnp.sqrt(nv / c2) + ADAM_EPS) + ADAM_WD * w_ref[...])
        nm_ref[...] = nm
        nv_ref[...] = nv

    spec = pl.BlockSpec((tr, C), lambda i: (i, 0))
    return pl.pallas_call(
        body, name=name, grid=(R // tr,),
        in_specs=[spec] * 4, out_specs=[spec] * 3,
        out_shape=[jax.ShapeDtypeStruct((R, C), F32)] * 3,
        compiler_params=_cparams(),
    )(w, g, m, v)


def _proj(h, wfull, col0, ncols, out_dtype, name, rope=None):
    S, K = h.shape
    tm = _row_tile(S, MM_TILE)
    tn = math.gcd(_tile(ncols, MM_TILE), col0) if col0 else _tile(ncols, MM_TILE)
    if rope is not None:
        tn = _tile(math.gcd(ncols, rope[1]), MM_TILE)
    assert ncols % tn == 0 and col0 % tn == 0
    cb = col0 // tn

    def body(*refs):
        if rope is None:
            a_ref, b_ref, o_ref = refs
        else:
            a_ref, b_ref, t_ref, o_ref = refs
        acc = _dot_nn(a_ref[...], b_ref[...])
        if rope is not None:
            t0, t1, t2 = (jnp.tile(t_ref[i], (1, tn // LANE)) for i in range(3))
            roped = (acc * t0 + pltpu.roll(acc, tn - ROT_DIM // 2, axis=1) * t1
                     + pltpu.roll(acc, ROT_DIM // 2, axis=1) * t2)
            acc = jnp.where(pl.program_id(1) < rope[1] // tn, roped, acc)
        o_ref[...] = acc.astype(out_dtype)

    in_specs = [pl.BlockSpec((tm, K), lambda i, j: (i, 0)), pl.BlockSpec((K, tn), lambda i, j: (0, cb + j))]
    args = [h, wfull]
    if rope is not None:
        in_specs.append(pl.BlockSpec((3, tm, LANE), lambda i, j: (0, i, 0)))
        args.append(rope[0])
    return pl.pallas_call(
        body, name=name, grid=(S // tm, ncols // tn),
        in_specs=in_specs, out_specs=pl.BlockSpec((tm, tn), lambda i, j: (i, j)),
        out_shape=jax.ShapeDtypeStruct((S, ncols), out_dtype),
        compiler_params=_cparams(),
    )(*args)


def _out_proj_res(y, wo, xres, name):
    S, W = y.shape
    D = wo.shape[1]
    tm, tn = _row_tile(S, MM_TILE), _tile(D, MM_TILE)

    def body(a_ref, b_ref, r_ref, o_ref):
        o_ref[...] = r_ref[...] + _dot_nn(a_ref[...], b_ref[...])

    return pl.pallas_call(
        body, name=name, grid=(S // tm, D // tn),
        in_specs=[pl.BlockSpec((tm, W), lambda i, j: (i, 0)), pl.BlockSpec((W, tn), lambda i, j: (0, j)),
                  pl.BlockSpec((tm, tn), lambda i, j: (i, j))],
        out_specs=pl.BlockSpec((tm, tn), lambda i, j: (i, j)),
        out_shape=jax.ShapeDtypeStruct((S, D), F32),
        compiler_params=_cparams(),
    )(y, wo, xres)


def _matmul_nt(parts, wfull, out_rows, name):
    S = parts[0][0].shape[-2]
    tm, tn = _row_tile(S, MM_TILE), _tile(out_rows, MM_TILE)
    plan, lo = [], 0
    for arr, lead, col0 in parts:
        n_p = arr.shape[-1]
        tk = math.gcd(_tile(n_p, 1024), col0) if col0 else _tile(n_p, 1024)
        steps = n_p // tk
        plan.append((lead, col0 // tk, tk, lo, lo + steps))
        lo += steps
    nk = lo
    npart = len(parts)

    def body(*refs):
        a_refs, w_refs = refs[:npart], refs[npart:2 * npart]
        o_ref, acc_ref = refs[2 * npart], refs[2 * npart + 1]
        k = pl.program_id(2)

        @pl.when(k == 0)
        def _():
            acc_ref[...] = jnp.zeros(acc_ref.shape, F32)
        for p, (_, _, _, lo_p, hi_p) in enumerate(plan):
            @pl.when((k >= lo_p) & (k < hi_p))
            def _(p=p):
                acc_ref[...] += _dot_nt(a_refs[p][...], w_refs[p][...])

        @pl.when(k == nk - 1)
        def _():
            o_ref[...] = acc_ref[...]

    in_specs, args = [], []
    for (arr, lead, col0), (_, cb, tk, lo_p, hi_p) in zip(parts, plan):
        def kk(k, lo_p=lo_p, hi_p=hi_p):
            return jnp.clip(k - lo_p, 0, hi_p - lo_p - 1)
        if lead is None:
            in_specs.append(pl.BlockSpec((tm, tk), lambda i, j, k, kk=kk: (i, kk(k))))
        else:
            in_specs.append(pl.BlockSpec((None, tm, tk), lambda i, j, k, kk=kk, lead=lead: (lead, i, kk(k))))
        args.append(arr)
    for (_, cb, tk, lo_p, hi_p) in plan:
        def kk(k, lo_p=lo_p, hi_p=hi_p):
            return jnp.clip(k - lo_p, 0, hi_p - lo_p - 1)
        in_specs.append(pl.BlockSpec((tn, tk), lambda i, j, k, kk=kk, cb=cb: (j, cb + kk(k))))
        args.append(wfull)
    return pl.pallas_call(
        body, name=name, grid=(S // tm, out_rows // tn, nk),
        in_specs=in_specs, out_specs=pl.BlockSpec((tm, tn), lambda i, j, k: (i, j)),
        out_shape=jax.ShapeDtypeStruct((S, out_rows), F32),
        scratch_shapes=[pltpu.VMEM((tm, tn), F32)],
        compiler_params=_cparams(),
    )(*args)


def _matmul_tn(a, parts, total, name, tile_major=False):
    S, M = a.shape
    tm, ts = _tile(M, MM_TILE), _row_tile(S, 512)
    out = None
    for idx, (arr, lead, col0) in enumerate(parts):
        n_p = arr.shape[-1]
        tn = math.gcd(_tile(n_p, MM_TILE), col0) if col0 else _tile(n_p, MM_TILE)
        cb = col0 // tn
        nk = S // ts

        def body(*refs, nk=nk, tn=tn):
            a_ref, b_ref = refs[0], refs[1]
            o_ref, acc_ref = refs[-2], refs[-1]
            k = pl.program_id(2)

            @pl.when(k == 0)
            def _():
                acc_ref[...] = jnp.zeros(acc_ref.shape, F32)
            acc_ref[...] += _dot_tn(a_ref[...], b_ref[...])

            @pl.when(k == nk - 1)
            def _():
                if tile_major:
                    for t in range(tn // LANE):
                        o_ref[t] = acc_ref[:, LANE * t:LANE * (t + 1)]
                else:
                    o_ref[...] = acc_ref[...]

        in_specs = [pl.BlockSpec((ts, tm), lambda i, j, k: (k, i))]
        if lead is None:
            in_specs.append(pl.BlockSpec((ts, tn), lambda i, j, k: (k, j)))
        else:
            in_specs.append(pl.BlockSpec((None, ts, tn), lambda i, j, k, lead=lead: (lead, k, j)))
        args = [a, arr]
        aliases = {}
        if out is not None:
            in_specs.append(pl.BlockSpec(memory_space=pl.ANY))
            args.append(out)
            aliases = {2: 0}
        if tile_major:
            out_spec = pl.BlockSpec((tn // LANE, tm, LANE), lambda i, j, k, cb=cb: (cb + j, i, 0))
            out_shape = jax.ShapeDtypeStruct((total // LANE, M, LANE), F32)
        else:
            out_spec = pl.BlockSpec((tm, tn), lambda i, j, k, cb=cb: (i, cb + j))
            out_shape = jax.ShapeDtypeStruct((M, total), F32)
        out = pl.pallas_call(
            body, name=f"{name}_{idx}", grid=(M // tm, n_p // tn, nk),
            in_specs=in_specs, out_specs=out_spec, out_shape=out_shape,
            scratch_shapes=[pltpu.VMEM((tm, tn), F32)],
            input_output_aliases=aliases,
            compiler_params=_cparams(),
        )(*args)
    return out


def _log_sigmoid(z):
    e = jnp.exp(-jnp.abs(z))
    return jnp.minimum(z, 0.0) - jnp.where(e < 1e-4, e * (1.0 - 0.5 * e), jnp.log(1.0 + e))


def _fox_gate_fwd(fl, bias):
    S = fl.shape[0]

    def body(f_ref, b_ref, c_ref):
        row = lax.broadcasted_iota(jnp.int32, (8, LANE), 0)

        def step(i, carry):
            r0 = pl.multiple_of(i * 8, 8)
            t = _log_sigmoid(f_ref[pl.ds(r0, 8), :] + b_ref[...])
            for sh in (1, 2, 4):
                t = t + jnp.where(row >= sh, pltpu.roll(t, sh, axis=0), 0.0)
            t = t + carry
            c_ref[pl.ds(r0, 8), :] = t
            return jnp.sum(jnp.where(row == 7, t, 0.0), axis=0, keepdims=True)

        lax.fori_loop(0, S // 8, step, jnp.zeros((1, LANE), F32))

    vm = pl.BlockSpec(memory_space=pltpu.VMEM)
    return pl.pallas_call(
        body, name="fox_gate_fwd", in_specs=[vm, vm], out_specs=vm,
        out_shape=jax.ShapeDtypeStruct((S, LANE), F32),
        compiler_params=_cparams(),
    )(fl, bias)


def _fox_gate_bwd(fl, bias, dc):
    S = fl.shape[0]

    def body(f_ref, b_ref, d_ref, o_ref, db_ref, acc_ref):
        row = lax.broadcasted_iota(jnp.int32, (8, LANE), 0)
        nt = S // 8

        def step(ii, carry):
            carry_c, carry_b = carry
            r0 = pl.multiple_of((nt - 1 - ii) * 8, 8)
            t = d_ref[pl.ds(r0, 8), :]
            for sh in (1, 2, 4):
                t = t + jnp.where(row < 8 - sh, pltpu.roll(t, 8 - sh, axis=0), 0.0)
            t = t + carry_c
            z = f_ref[pl.ds(r0, 8), :] + b_ref[...]
            dz = t * _sigmoid(-z)
            acc_ref[pl.ds(r0, 8), :] = dz
            first = jnp.sum(jnp.where(row == 0, t, 0.0), axis=0, keepdims=True)
            return first, carry_b + jnp.sum(dz, axis=0, keepdims=True)

        zero = jnp.zeros((1, LANE), F32)
        _, db = lax.fori_loop(0, nt, step, (zero, zero))
        db_ref[...] = db
        o_ref[...] = acc_ref[...].astype(BF16)

    vm = pl.BlockSpec(memory_space=pltpu.VMEM)
    return pl.pallas_call(
        body, name="fox_gate_bwd", in_specs=[vm, vm, vm], out_specs=[vm, vm],
        out_shape=[jax.ShapeDtypeStruct((S, LANE), BF16), jax.ShapeDtypeStruct((1, LANE), F32)],
        scratch_shapes=[pltpu.VMEM((S, LANE), F32)],
        compiler_params=_cparams(),
    )(fl, bias, dc)


def _bias_lanes(col, lane, e, first):
    o0 = HEAD_DIM * (1 - e)
    hi = col.astype(BF16)
    r1 = col - hi.astype(F32)
    mid = r1.astype(BF16)
    lo = (r1 - mid.astype(F32)).astype(BF16)
    d0 = o0 if first else o0 + 3
    t = jnp.where((lane >= o0) & (lane < o0 + 6), jnp.ones(lane.shape, BF16), jnp.zeros(lane.shape, BF16))
    t = jnp.where(lane == d0, hi, t)
    t = jnp.where(lane == d0 + 1, mid, t)
    return jnp.where(lane == d0 + 2, lo, t)


def _fox_fwd(qkv, gate, c, H):
    S = qkv.shape[0]
    W = H * HEAD_DIM
    HP = H // 2
    tq = _row_tile(S, 512)
    nq = S // tq
    wb = W // LANE
    scale = HEAD_DIM ** -0.5

    def body(q_ref, k_ref, v_ref, g_ref, c_ref, y_ref, o_ref, a_ref,
             kaug_sc, vaug_sc, qaug_sc, s_sc, mb_sc, m_sc, acc_sc):
        hp, qi = pl.program_id(0), pl.program_id(1)
        lane = lax.broadcasted_iota(jnp.int32, (tq, LANE), 1)
        own = [lane < HEAD_DIM, lane >= HEAD_DIM]
        rows = lax.broadcasted_iota(jnp.int32, (tq, tq), 0)
        cols = lax.broadcasted_iota(jnp.int32, (tq, tq), 1)

        def bias_lanes(col, e, first):
            return _bias_lanes(col, lane, e, first)

        def head_col(tile, e):
            return jnp.sum(jnp.where(lane == 2 * hp + e, tile, 0.0), axis=1, keepdims=True)

        @pl.when(qi == 0)
        def _():
            def chunk(i, carry):
                r0 = pl.multiple_of(i * tq, tq)
                kb, vb, cb = k_ref[pl.ds(r0, tq), :], v_ref[pl.ds(r0, tq), :], c_ref[pl.ds(r0, tq), :]
                for e in range(2):
                    kaug_sc[e, pl.ds(r0, tq), :] = jnp.where(own[e], kb, bias_lanes(-head_col(cb, e), e, False))
                    vaug_sc[e, pl.ds(r0, tq), :] = jnp.where(own[e], vb, jnp.ones((tq, LANE), BF16))
                return carry
            lax.fori_loop(0, nq, chunk, 0)

        q = q_ref[...] * jnp.asarray(scale, BF16)
        crow = c_ref[pl.ds(pl.multiple_of(qi * tq, tq), tq), :]
        ctq = [head_col(crow, e) for e in range(2)]
        for e in range(2):
            qaug_sc[e] = jnp.where(own[e], q, bias_lanes(ctq[e], e, True))
        m_sc[...] = jnp.full(m_sc.shape, NEG_INF, F32)
        acc_sc[...] = jnp.zeros(acc_sc.shape, F32)

        def scores(blk, slot, masked):
            k0 = pl.multiple_of(blk * tq, tq)
            for e in range(2):
                s = _dot_nt(qaug_sc[e], kaug_sc[e, pl.ds(k0, tq), :])
                if masked:
                    s = jnp.where(rows >= cols, s, NEG_INF)
                s_sc[slot, e] = s
                mb_sc[slot, e] = jnp.broadcast_to(jnp.max(s, axis=1, keepdims=True), (tq, LANE))

        def accumulate(blk, slot):
            k0 = pl.multiple_of(blk * tq, tq)
            for e in range(2):
                m_prev = m_sc[e]
                m_new = jnp.maximum(m_prev, mb_sc[slot, e])
                p = jnp.exp(s_sc[slot, e] - jnp.tile(m_new, (1, tq // LANE)))
                acc_sc[e] = jnp.exp(m_prev - m_new) * acc_sc[e] + _dot_nn(p.astype(BF16), vaug_sc[e, pl.ds(k0, tq), :])
                m_sc[e] = m_new

        def block_of(t):
            return jnp.where(t == 0, qi, t - 1)

        scores(qi, 0, True)

        def loop_body(t, carry):
            scores(t, (t + 1) % 2, False)
            accumulate(block_of(t), t % 2)
            return carry

        lax.fori_loop(0, qi, loop_body, 0)
        accumulate(block_of(qi), qi % 2)
        o_e, a_e = [], []
        for e in range(2):
            acc = acc_sc[e]
            l = pltpu.roll(acc, HEAD_DIM, axis=1)
            o_e.append(acc / l)
            a_e.append(ctq[e] - (m_sc[e] + jnp.log(l)))
        o = jnp.where(own[0], o_e[0], o_e[1])
        g = g_ref[...]
        y_ref[...] = (o * (g * _sigmoid(g))).astype(BF16)
        o_ref[...] = o.astype(BF16)
        a_ref[0] = jnp.where(own[0], a_e[0], a_e[1])

    return pl.pallas_call(
        body, name="fox_attn_fwd", grid=(HP, nq),
        in_specs=[pl.BlockSpec((tq, LANE), lambda h, i: (i, h)),
                  pl.BlockSpec((S, LANE), lambda h, i: (0, wb + h)),
                  pl.BlockSpec((S, LANE), lambda h, i: (0, 2 * wb + h)),
                  pl.BlockSpec((tq, LANE), lambda h, i: (i, h)),
                  pl.BlockSpec((S, LANE), lambda h, i: (0, 0))],
        out_specs=[pl.BlockSpec((tq, LANE), lambda h, i: (i, h)),
                   pl.BlockSpec((tq, LANE), lambda h, i: (i, h)),
                   pl.BlockSpec((1, tq, LANE), lambda h, i: (h, i, 0))],
        out_shape=[jax.ShapeDtypeStruct((S, W), BF16), jax.ShapeDtypeStruct((S, W), BF16),
                   jax.ShapeDtypeStruct((HP, S, LANE), F32)],
        scratch_shapes=[pltpu.VMEM((2, S, LANE), BF16), pltpu.VMEM((2, S, LANE), BF16),
                        pltpu.VMEM((2, tq, LANE), BF16), pltpu.VMEM((2, 2, tq, tq), F32),
                        pltpu.VMEM((2, 2, tq, LANE), F32), pltpu.VMEM((2, tq, LANE), F32),
                        pltpu.VMEM((2, tq, LANE), F32)],
        compiler_params=_cparams(),
    )(qkv, qkv, qkv, gate, c)


def _fox_bwd_prep(qkv, dy, gate, o, a, H):
    S = qkv.shape[0]
    W = H * HEAD_DIM
    HP = H // 2
    tq = _row_tile(S, 512)
    scale = HEAD_DIM ** -0.5

    def body(q_ref, dy_ref, g_ref, o_ref, a_ref, qa_ref, da_ref, dg_ref):
        lane = lax.broadcasted_iota(jnp.int32, (tq, LANE), 1)
        own = [lane < HEAD_DIM, lane >= HEAD_DIM]
        q = q_ref[...] * jnp.asarray(scale, BF16)
        dyv, g, ov, at = dy_ref[...], g_ref[...], o_ref[...].astype(F32), a_ref[0]
        sg = _sigmoid(g)
        dob = (dyv * (g * sg)).astype(BF16)
        dg_ref[...] = (dyv * ov * (sg * (1.0 + g * (1.0 - sg)))).astype(BF16)
        prod = dob.astype(F32) * ov
        for e in range(2):
            a_col = jnp.max(jnp.where(own[e], at, -jnp.inf), axis=1, keepdims=True)
            d_col = jnp.sum(jnp.where(own[e], prod, 0.0), axis=1, keepdims=True)
            qa_ref[e] = jnp.where(own[e], q, _bias_lanes(a_col, lane, e, True))
            da_ref[e] = jnp.where(own[e], dob, _bias_lanes(-d_col, lane, e, True))

    blk = pl.BlockSpec((tq, LANE), lambda h, i: (i, h))
    pair = pl.BlockSpec((2, tq, LANE), lambda h, i: (0, i, h))
    return pl.pallas_call(
        body, name="fox_attn_bwd_prep", grid=(HP, S // tq),
        in_specs=[blk, blk, blk, blk, pl.BlockSpec((1, tq, LANE), lambda h, i: (h, i, 0))],
        out_specs=[pair, pair, blk],
        out_shape=[jax.ShapeDtypeStruct((2, S, W), BF16), jax.ShapeDtypeStruct((2, S, W), BF16),
                   jax.ShapeDtypeStruct((S, W), BF16)],
        compiler_params=_cparams(),
    )(qkv, dy, gate, o, a)


def _fox_bwd(qaug, doaug, qkv, c, H):
    S = qkv.shape[0]
    W = H * HEAD_DIM
    HP = H // 2
    tq = _row_tile(S, 512)
    nq = S // tq
    wb = W // LANE
    scale = HEAD_DIM ** -0.5

    def body(qa_ref, da_ref, k_ref, v_ref, c_ref, out_ref, dcr_ref, dcc_ref, dq_sc, dk_sc, dv_sc):
        hp, kj = pl.program_id(0), pl.program_id(1)
        lane = lax.broadcasted_iota(jnp.int32, (tq, LANE), 1)
        own = [lane < HEAD_DIM, lane >= HEAD_DIM]
        rows = lax.broadcasted_iota(jnp.int32, (tq, tq), 0)
        cols = lax.broadcasted_iota(jnp.int32, (tq, tq), 1)

        @pl.when(kj == 0)
        def _():
            dq_sc[...] = jnp.zeros(dq_sc.shape, F32)

        @pl.when((kj == 0) & (hp == 0))
        def _():
            dcr_ref[...] = jnp.zeros(dcr_ref.shape, F32)
            dcc_ref[...] = jnp.zeros(dcc_ref.shape, F32)

        kblk, vblk, cblk = k_ref[...], v_ref[...], c_ref[...]
        one, zero = jnp.ones((tq, LANE), BF16), jnp.zeros((tq, LANE), BF16)
        ka, va = [], []
        for e in range(2):
            o0 = HEAD_DIM * (1 - e)
            c_col = jnp.sum(jnp.where(lane == 2 * hp + e, cblk, 0.0), axis=1, keepdims=True)
            ka.append(jnp.where(own[e], kblk, _bias_lanes(-c_col, lane, e, False)))
            va.append(jnp.where(own[e], vblk, jnp.where((lane >= o0) & (lane < o0 + 3), one, zero)))
        dk_sc[...] = jnp.zeros(dk_sc.shape, F32)
        dv_sc[...] = jnp.zeros(dv_sc.shape, F32)

        def step(i, masked):
            r0 = pl.multiple_of(i * tq, tq)
            for e in range(2):
                qa = qa_ref[e, pl.ds(r0, tq), :]
                da = da_ref[e, pl.ds(r0, tq), :]
                p = jnp.exp(_dot_nt(qa, ka[e]))
                if masked:
                    p = jnp.where(rows >= cols, p, 0.0)
                ds = p * _dot_nt(da, va[e])
                pb, dsb = p.astype(BF16), ds.astype(BF16)
                dv_sc[e] += _dot_tn(pb, da)
                dk_sc[e] += _dot_tn(dsb, qa)
                dq_sc[e, pl.ds(r0, tq), :] += _dot_nn(dsb, ka[e])

        step(kj, True)

        def loop_body(i, carry):
            step(i, False)
            return carry

        lax.fori_loop(kj + 1, nq, loop_body, 0)
        k0 = pl.multiple_of(kj * tq, tq)
        out_ref[1, pl.ds(k0, tq), :] = jnp.where(own[0], dk_sc[0], dk_sc[1]).astype(BF16)
        out_ref[2, pl.ds(k0, tq), :] = jnp.where(own[0], dv_sc[0], dv_sc[1]).astype(BF16)

        def put_lane(ref, r0, e, tile, src_lane):
            col = jnp.sum(jnp.where(lane == src_lane, tile, 0.0), axis=1, keepdims=True)
            ref[pl.ds(r0, tq), :] = jnp.where(lane == 2 * hp + e, col, ref[pl.ds(r0, tq), :])

        for e in range(2):
            put_lane(dcc_ref, k0, e, dk_sc[e], HEAD_DIM * (1 - e) + 3)

        @pl.when(kj == nq - 1)
        def _():
            def chunk(i, carry):
                r0 = pl.multiple_of(i * tq, tq)
                d0, d1 = dq_sc[0, pl.ds(r0, tq), :], dq_sc[1, pl.ds(r0, tq), :]
                out_ref[0, pl.ds(r0, tq), :] = (jnp.where(own[0], d0, d1) * scale).astype(BF16)
                put_lane(dcr_ref, r0, 0, d0, HEAD_DIM)
                put_lane(dcr_ref, r0, 1, d1, 0)
                return carry
            lax.fori_loop(0, nq, chunk, 0)

    pair = pl.BlockSpec((2, S, LANE), lambda h, j: (0, 0, h))
    vec = pl.BlockSpec((S, LANE), lambda h, j: (0, 0))
    return pl.pallas_call(
        body, name="fox_attn_bwd", grid=(HP, nq),
        in_specs=[pair, pair,
                  pl.BlockSpec((tq, LANE), lambda h, j: (j, wb + h)),
                  pl.BlockSpec((tq, LANE), lambda h, j: (j, 2 * wb + h)),
                  pl.BlockSpec((tq, LANE), lambda h, j: (j, 0))],
        out_specs=[pl.BlockSpec((3, S, LANE), lambda h, j: (0, 0, h)), vec, vec],
        out_shape=[jax.ShapeDtypeStruct((3, S, W), BF16), jax.ShapeDtypeStruct((S, LANE), F32),
                   jax.ShapeDtypeStruct((S, LANE), F32)],
        scratch_shapes=[pltpu.VMEM((2, S, LANE), F32), pltpu.VMEM((2, tq, LANE), F32),
                        pltpu.VMEM((2, tq, LANE), F32)],
        compiler_params=_cparams(),
    )(qaug, doaug, qkv, qkv, c)


def _swa_pick(blk, half, lane):
    b = blk.astype(F32)
    r = pltpu.roll(b, HEAD_DIM, axis=1)
    return jnp.where(jnp.logical_xor(lane < HEAD_DIM, half == 1), b, r).astype(BF16)


def _swa_stack(t, lane, G):
    pieces = []
    z = jnp.zeros((SWA_BLOCK, LANE), t.dtype)
    for j in range(G // 2):
        tile = t[:, LANE * j:LANE * (j + 1)]
        pieces += [jnp.where(lane < HEAD_DIM, tile, z), jnp.where(lane < HEAD_DIM, z, tile)]
    return jnp.concatenate(pieces, axis=0)


def _swa_unstack(st, lane, G):
    tiles = []
    for j in range(G // 2):
        a = st[2 * j * SWA_BLOCK:(2 * j + 1) * SWA_BLOCK]
        b = st[(2 * j + 1) * SWA_BLOCK:(2 * j + 2) * SWA_BLOCK]
        tiles.append(jnp.where(lane < HEAD_DIM, a, b))
    return jnp.concatenate(tiles, axis=1)


def _swa_scores(q_ref, kp_ref, kc_ref, vp_ref, vc_ref, sink_ref, kvh, n, G):
    R = G * SWA_BLOCK
    lane = lax.broadcasted_iota(jnp.int32, (SWA_BLOCK, LANE), 1)
    half = kvh % 2
    kk = jnp.concatenate([_swa_pick(kp_ref[...], half, lane), _swa_pick(kc_ref[...], half, lane)], axis=0)
    vv = jnp.concatenate([_swa_pick(vp_ref[...], half, lane), _swa_pick(vc_ref[...], half, lane)], axis=0)
    qstack = _swa_stack(q_ref[...], lane, G) * jnp.asarray(HEAD_DIM ** -0.5, BF16)
    s = _dot_nt(qstack, kk)
    t_loc = lax.broadcasted_iota(jnp.int32, (R, 2 * SWA_BLOCK), 0) & (SWA_BLOCK - 1)
    j_loc = lax.broadcasted_iota(jnp.int32, (R, 2 * SWA_BLOCK), 1)
    diff = t_loc + SWA_BLOCK - j_loc
    mask = (diff >= 0) & (diff < SWA_BLOCK) & ((n > 0) | (j_loc >= SWA_BLOCK))
    s = jnp.where(mask, s, NEG_INF)
    srow = sink_ref[...]
    lane1 = lax.broadcasted_iota(jnp.int32, (1, LANE), 1)
    sink = jnp.concatenate(
        [jnp.broadcast_to(jnp.sum(jnp.where(lane1 == kvh * G + g, srow, 0.0), axis=1, keepdims=True), (SWA_BLOCK, 1))
         for g in range(G)], axis=0)
    m = jnp.maximum(jnp.max(s, axis=1, keepdims=True), sink)
    e = jnp.exp(s - m)
    es = jnp.exp(sink - m)
    den = jnp.sum(e, axis=1, keepdims=True) + es
    return qstack, kk, vv, e / den, es / den, lane


def _swa_fwd(qkv, gate, sinks, HQ, HKV):
    S = qkv.shape[0]
    G = HQ // HKV
    WQ, KVW = HQ * HEAD_DIM, HKV * HEAD_DIM
    nb = S // SWA_BLOCK
    GW = G * HEAD_DIM
    kb, vb = WQ // LANE, (WQ + KVW) // LANE

    def body(q_ref, kp_ref, kc_ref, vp_ref, vc_ref, g_ref, sink_ref, y_ref, o_ref):
        kvh, n = pl.program_id(0), pl.program_id(1)
        _, _, vv, p, _, lane = _swa_scores(q_ref, kp_ref, kc_ref, vp_ref, vc_ref, sink_ref, kvh, n, G)
        o = _swa_unstack(_dot_nn(p.astype(BF16), vv), lane, G)
        g = g_ref[...]
        y_ref[...] = (o * (g * _sigmoid(g))).astype(BF16)
        o_ref[...] = o.astype(BF16)

    blk = lambda cb, prev: pl.BlockSpec(
        (SWA_BLOCK, LANE), lambda h, n, cb=cb, prev=prev: (jnp.maximum(n - prev, 0), cb + h // 2))
    qspec = pl.BlockSpec((SWA_BLOCK, GW), lambda h, n: (n, h))
    return pl.pallas_call(
        body, name="swa_attn_fwd", grid=(HKV, nb),
        in_specs=[qspec, blk(kb, 1), blk(kb, 0), blk(vb, 1), blk(vb, 0), qspec,
                  pl.BlockSpec((1, LANE), lambda h, n: (0, 0))],
        out_specs=[qspec, qspec],
        out_shape=[jax.ShapeDtypeStruct((S, WQ), BF16), jax.ShapeDtypeStruct((S, WQ), BF16)],
        compiler_params=_cparams(),
    )(qkv, qkv, qkv, qkv, qkv, gate, sinks)


def _swa_bwd(qkv, dy, gate, o, sinks, tables, HQ, HKV):
    S = qkv.shape[0]
    G = HQ // HKV
    WQ, KVW = HQ * HEAD_DIM, HKV * HEAD_DIM
    nb = S // SWA_BLOCK
    GW = G * HEAD_DIM
    R = G * SWA_BLOCK
    kb, vb = WQ // LANE, (WQ + KVW) // LANE
    scale = HEAD_DIM ** -0.5
    assert G == 8

    def body(q_ref, kp_ref, kc_ref, vp_ref, vc_ref, dy_ref, g_ref, o_ref, sink_ref, t_ref,
             dqg_ref, dkv_ref, dsink_ref, carry_sc):
        kvh, n = pl.program_id(0), pl.program_id(1)

        @pl.when(n == 0)
        def _():
            carry_sc[...] = jnp.zeros(carry_sc.shape, F32)
            dsink_ref[...] = jnp.zeros(dsink_ref.shape, F32)

        @pl.when(n < nb)
        def _():
            qstack, kk, vv, p, psink, lane = _swa_scores(q_ref, kp_ref, kc_ref, vp_ref, vc_ref, sink_ref, kvh, n, G)
            dyv, g, ov = dy_ref[...], g_ref[...], o_ref[...].astype(F32)
            sg = _sigmoid(g)
            dob = (dyv * (g * sg)).astype(BF16)
            dqg_ref[1] = (dyv * ov * (sg * (1.0 + g * (1.0 - sg)))).astype(BF16)
            prod = dob.astype(F32) * ov
            dparts = []
            for j in range(G // 2):
                tile = prod[:, LANE * j:LANE * (j + 1)]
                dparts += [jnp.sum(jnp.where(lane < HEAD_DIM, tile, 0.0), axis=1, keepdims=True),
                           jnp.sum(jnp.where(lane < HEAD_DIM, 0.0, tile), axis=1, keepdims=True)]
            delta = jnp.concatenate(dparts, axis=0)
            dostack = _swa_stack(dob, lane, G)
            dp = _dot_nt(dostack, vv)
            ds = p * (dp - delta)
            dsb, pb = ds.astype(BF16), p.astype(BF16)
            dq = _swa_unstack(_dot_nn(dsb, kk), lane, G) * scale
            dq = (dq * t_ref[0] + pltpu.roll(dq * t_ref[1], ROT_DIM // 2, axis=1)
                  + pltpu.roll(dq * t_ref[2], GW - ROT_DIM // 2, axis=1))
            dqg_ref[0] = dq.astype(BF16)
            dkk = _dot_tn(dsb, qstack)
            dvv = _dot_tn(pb, dostack)
            dkk = dkk + pltpu.roll(dkk, HEAD_DIM, axis=1)
            dvv = dvv + pltpu.roll(dvv, HEAD_DIM, axis=1)
            lane2 = lax.broadcasted_iota(jnp.int32, (2 * SWA_BLOCK, LANE), 1)
            comb = jnp.where(lane2 < HEAD_DIM, dkk, dvv)
            dkv_ref[0] = carry_sc[...] + comb[:SWA_BLOCK]
            carry_sc[...] = comb[SWA_BLOCK:]
            sk = psink * delta
            rows = [jnp.broadcast_to(-jnp.sum(sk[g_ * SWA_BLOCK:(g_ + 1) * SWA_BLOCK], axis=0, keepdims=True), (1, LANE))
                    for g_ in range(G)]
            dsink_ref[0] += jnp.concatenate(rows, axis=0)

        @pl.when(n == nb)
        def _():
            dkv_ref[0] = carry_sc[...]

    cl = lambda n: jnp.minimum(n, nb - 1)
    blk = lambda cb, prev: pl.BlockSpec(
        (SWA_BLOCK, LANE), lambda h, n, cb=cb, prev=prev: (jnp.maximum(cl(n) - prev, 0), cb + h // 2))
    qspec = pl.BlockSpec((SWA_BLOCK, GW), lambda h, n: (cl(n), h))
    return pl.pallas_call(
        body, name="swa_attn_bwd", grid=(HKV, nb + 1),
        in_specs=[qspec, blk(kb, 1), blk(kb, 0), blk(vb, 1), blk(vb, 0), qspec, qspec, qspec,
                  pl.BlockSpec((1, LANE), lambda h, n: (0, 0)),
                  pl.BlockSpec((3, SWA_BLOCK, GW), lambda h, n: (0, cl(n), 0))],
        out_specs=[pl.BlockSpec((2, SWA_BLOCK, GW), lambda h, n: (0, cl(n), h)),
                   pl.BlockSpec((1, SWA_BLOCK, LANE), lambda h, n: (h, jnp.maximum(n - 1, 0), 0)),
                   pl.BlockSpec((1, 8, LANE), lambda h, n: (h, 0, 0))],
        out_shape=[jax.ShapeDtypeStruct((2, S, WQ), BF16), jax.ShapeDtypeStruct((HKV, S, LANE), F32),
                   jax.ShapeDtypeStruct((HKV, 8, LANE), F32)],
        scratch_shapes=[pltpu.VMEM((SWA_BLOCK, LANE), F32)],
        compiler_params=_cparams(),
    )(qkv, qkv, qkv, qkv, qkv, dy, gate, o, sinks, tables)


def _swa_dkv_finish(dkv, tables):
    HKV, S, _ = dkv.shape
    KVW = HKV * HEAD_DIM
    tm = _row_tile(S, 512)
    npair = HKV // 2

    def body(d_ref, t_ref, o_ref):
        lane = lax.broadcasted_iota(jnp.int32, (tm, LANE), 1)
        lo = lane < HEAD_DIM
        for p in range(npair):
            a, b = d_ref[2 * p], d_ref[2 * p + 1]
            tk = jnp.where(lo, a, pltpu.roll(b, HEAD_DIM, axis=1))
            tv = jnp.where(lo, pltpu.roll(a, HEAD_DIM, axis=1), b)
            tk = (tk * t_ref[0] + pltpu.roll(tk * t_ref[1], ROT_DIM // 2, axis=1)
                  + pltpu.roll(tk * t_ref[2], LANE - ROT_DIM // 2, axis=1))
            o_ref[:, LANE * p:LANE * (p + 1)] = tk.astype(BF16)
            o_ref[:, KVW + LANE * p:KVW + LANE * (p + 1)] = tv.astype(BF16)

    return pl.pallas_call(
        body, name="swa_dkv_finish", grid=(S // tm,),
        in_specs=[pl.BlockSpec((HKV, tm, LANE), lambda i: (0, i, 0)), pl.BlockSpec((3, tm, LANE), lambda i: (0, i, 0))],
        out_specs=pl.BlockSpec((tm, 2 * KVW), lambda i: (i, 0)),
        out_shape=jax.ShapeDtypeStruct((S, 2 * KVW), BF16),
        compiler_params=_cparams(),
    )(dkv, tables)


def _rope_tables(S, width):
    half = ROT_DIM // 2
    pos = jnp.arange(S, dtype=F32)
    inv_freq = ROPE_THETA ** (-jnp.arange(half, dtype=F32) / half)
    ang = pos[:, None] * inv_freq[None, :]
    cos, sin = jnp.cos(ang), jnp.sin(ang)
    one = jnp.ones((S, HEAD_DIM - ROT_DIM), F32)
    zero = jnp.zeros((S, HEAD_DIM - ROT_DIM), F32)
    zh = jnp.zeros((S, half), F32)
    t0 = jnp.concatenate([cos, cos, one], axis=1)
    t1 = jnp.concatenate([-sin, zh, zero], axis=1)
    t2 = jnp.concatenate([zh, sin, zero], axis=1)
    return jnp.stack([jnp.tile(t, (1, width // HEAD_DIM)) for t in (t0, t1, t2)])


def _pad_rows(v, row, total_rows=8):
    return jnp.pad(v, ((row, total_rows - row - v.shape[0]), (0, 0)))


def _pad_lanes(v, off, width):
    return jnp.pad(v, ((0, 0), (off, width - off - v.shape[1])))


def kernel(x, norm_g, fox_w_in, fox_b_f, fox_w_out, swa_w_in, swa_sinks, swa_w_out, final_g, loss_target, m_norm_g, m_fox_w_in, m_fox_b_f, m_fox_w_out, m_swa_w_in, m_swa_sinks, m_swa_w_out, m_final_g, v_norm_g, v_fox_w_in, v_fox_b_f, v_fox_w_out, v_swa_w_in, v_swa_sinks, v_swa_w_out, v_final_g):
    S, D = x.shape[1], x.shape[2]
    H = fox_b_f.shape[1]
    W = H * HEAD_DIM
    wf = fox_w_in.shape[2]
    ws = swa_w_in.shape[2]
    HQ = swa_sinks.shape[1]
    WQ = HQ * HEAD_DIM
    KVW = (ws * N_DEV - 2 * WQ) // 2
    HKV = KVW // HEAD_DIM
    rows_o = fox_w_out.shape[1]
    assert wf * N_DEV == 4 * W + H and rows_o * N_DEV == W and H <= LANE and HQ <= LANE
    me = _my_index()

    _, sw_f, np_f = _slab_geom(wf)
    _, sw_s, np_s = _slab_geom(ws)

    def slab(w2d, w, sw):
        off = (w * me) % LANE
        return lax.dynamic_update_slice(jnp.zeros((w2d.shape[0], sw), BF16), w2d.astype(BF16), (0, off))

    fi_all, si_all, fo_all, so_all = _all_gather([
        slab(fox_w_in[0], wf, sw_f), slab(swa_w_in[0], ws, sw_s),
        fox_w_out[0].astype(BF16), swa_w_out[0].astype(BF16)])
    w_fi = _assemble(fi_all, wf)
    w_si = _assemble(si_all, ws)
    w_fo = fo_all.reshape(W, D)
    w_so = so_all.reshape(WQ, D)

    x0 = x[0]
    g0, g1, gf = norm_g[0:1], norm_g[1:2], final_g[None, :]
    bias = _pad_lanes(fox_b_f, 0, LANE)
    sinks = _pad_lanes(swa_sinks, 0, LANE)
    tab_q = _rope_tables(S, 8 * HEAD_DIM)
    tab_k = tab_q[:, :, :LANE]

    h0 = _rmsnorm_fwd(x0, g0, "rmsnorm0")
    qkv0 = _proj(h0, w_fi, 0, 3 * W, BF16, "fox_in_qkv")
    gate0 = _proj(h0, w_fi, 3 * W, W, F32, "fox_in_gate")
    fl = _proj(h0, w_fi, 4 * W, LANE, F32, "fox_in_f")
    c = _fox_gate_fwd(fl, bias)
    y0, o0, a0 = _fox_fwd(qkv0, gate0, c, H)
    x1 = _out_proj_res(y0, w_fo, x0, "fox_out")

    h1 = _rmsnorm_fwd(x1, g1, "rmsnorm1")
    qkv1 = _proj(h1, w_si, 0, WQ + 2 * KVW, BF16, "swa_in_qkv", rope=(tab_k, WQ + KVW))
    gate1 = _proj(h1, w_si, WQ + 2 * KVW, WQ, F32, "swa_in_gate")
    y1, o1 = _swa_fwd(qkv1, gate1, sinks, HQ, HKV)
    x2 = _out_proj_res(y1, w_so, x1, "swa_out")

    dx2, dx2b, dgf, loss_p = _loss_head(x2, loss_target[0], gf)

    dy1 = _matmul_nt([(dx2b, None, 0)], w_so, WQ, "swa_out_bwd")
    g_so = _matmul_tn(y1, [(dx2b, None, 0)], D, "swa_out_wgrad")
    dqg1, dkv1, dsink = _swa_bwd(qkv1, dy1, gate1, o1, sinks, tab_q, HQ, HKV)
    dkv1f = _swa_dkv_finish(dkv1, tab_k)
    parts1 = [(dqg1, 0, 0), (dkv1f, None, WQ), (dqg1, 1, WQ + 2 * KVW)]
    g_si = _matmul_tn(h1, parts1, np_s, "swa_in_wgrad", tile_major=True)
    dh1 = _matmul_nt(parts1, w_si, D, "swa_in_bwd")
    dx1, dx1b, dg1 = _rmsnorm_bwd(dh1, x1, g1, dx2, "rmsnorm1_bwd")

    dy0 = _matmul_nt([(dx1b, None, 0)], w_fo, W, "fox_out_bwd")
    g_fo = _matmul_tn(y0, [(dx1b, None, 0)], D, "fox_out_wgrad")
    qaug0, doaug0, dgate0 = _fox_bwd_prep(qkv0, dy0, gate0, o0, a0, H)
    dqkv0, dcr, dcc = _fox_bwd(qaug0, doaug0, qkv0, c, H)
    dfl, dbf = _fox_gate_bwd(fl, bias, dcr - dcc)
    parts0 = [(dqkv0, p, p * W) for p in range(3)] + [(dgate0, None, 3 * W), (dfl, None, 4 * W)]
    g_fi = _matmul_tn(h0, parts0, np_f, "fox_in_wgrad", tile_major=True)
    dh0 = _matmul_nt(parts0, w_fi, D, "fox_in_bwd")
    dx0, _, dg0 = _rmsnorm_bwd(dh0, x0, g0, dx1, "rmsnorm0_bwd")

    specs = [("col", wf), ("col", ws), ("row", rows_o), ("row", rows_o)]
    gfull = [g_fi, g_si, g_fo, g_so]
    recv = _reduce_scatter_stage1(gfull, specs)
    sums = [_pair_sum(g_, r_, s_) for g_, r_, s_ in zip(gfull, recv, specs)]
    recv2 = _reduce_scatter_stage2([s_[1] for s_ in sums])
    red_fi = _final_sum_cols(sums[0][0], recv2[0])
    red_si = _final_sum_cols(sums[1][0], recv2[1])
    gw_fi = lax.dynamic_slice(red_fi, (0, (wf * me) % LANE), (D, wf))
    gw_si = lax.dynamic_slice(red_si, (0, (ws * me) % LANE), (D, ws))
    gw_fo, gw_so = _final_sum(sums[2][0], recv2[2]), _final_sum(sums[3][0], recv2[3])

    P = D
    dsink_v = dsink[:, :, 0].reshape(1, HQ)
    row3 = _pad_lanes(dbf[:, :H], 0, P) + _pad_lanes(dsink_v, LANE, P) + _pad_lanes(loss_p[:, :1], 2 * LANE, P)
    pack = _pad_rows(dg0, 0) + _pad_rows(dg1, 1) + _pad_rows(dgf, 2) + _pad_rows(row3, 3)
    tot = _all_reduce_small(pack)
    loss = tot[3, 2 * LANE]
    g_norm = tot[0:2]
    g_final = tot[2]
    g_bf = tot[3:4, 0:H]
    g_sinks = tot[3:4, LANE:LANE + HQ]

    def small_pack(ng, fg, bf, sk):
        r3 = _pad_lanes(bf, 0, P) + _pad_lanes(sk, LANE, P)
        return _pad_rows(ng, 0) + _pad_rows(fg[None, :], 2) + _pad_rows(r3, 3)

    sd, sm, sv = _adamw(small_pack(norm_g, final_g, fox_b_f, swa_sinks), tot,
                        small_pack(m_norm_g, m_final_g, m_fox_b_f, m_swa_sinks),
                        small_pack(v_norm_g, v_final_g, v_fox_b_f, v_swa_sinks), "adamw_small")

    def unpack(t):
        return t[0:2], t[3:4, 0:H], t[3:4, LANE:LANE + HQ], t[2]

    d_fi, m_fi, v_fi = _adamw(fox_w_in[0], gw_fi, m_fox_w_in[0], v_fox_w_in[0], "adamw_fox_in")
    d_fo, m_fo, v_fo = _adamw(fox_w_out[0], gw_fo, m_fox_w_out[0], v_fox_w_out[0], "adamw_fox_out")
    d_si, m_si, v_si = _adamw(swa_w_in[0], gw_si, m_swa_w_in[0], v_swa_w_in[0], "adamw_swa_in")
    d_so, m_so, v_so = _adamw(swa_w_out[0], gw_so, m_swa_w_out[0], v_swa_w_out[0], "adamw_swa_out")

    def group(small, fi, fo, si, so):
        ng, bf, sk, fg = unpack(small)
        return (ng, fi[None], bf, fo[None], si[None], sk, so[None], fg)

    grads = (g_norm, gw_fi[None], g_bf, gw_fo[None], gw_si[None], g_sinks, gw_so[None], g_final)
    return (loss, dx0[None], *grads, *group(sd, d_fi, d_fo, d_si, d_so),
            *group(sm, m_fi, m_fo, m_si, m_so), *group(sv, v_fi, v_fo, v_si, v_so))
```

```python
import functools
import math

import jax
import jax.numpy as jnp
from jax import lax
from jax.experimental import pallas as pl
from jax.experimental.pallas import tpu as pltpu

F32 = jnp.float32
BF16 = jnp.bfloat16
MESH = pl.DeviceIdType.MESH

N_DEV = 8
LANE = 128
HEAD_DIM = 64
SWA_BLOCK = 128
NEG_INF = -1e30
RMS_EPS = 1e-6
ROPE_THETA = 500000.0
ROT_DIM = HEAD_DIM // 4
ADAM_LR, ADAM_B1, ADAM_B2, ADAM_EPS, ADAM_WD, ADAM_STEP = 0.001, 0.9, 0.999, 1e-08, 0.01, 10
VMEM_LIMIT = 56 * 1024 * 1024
MM_TILE = 1024


def _cparams(**kw):
    return pltpu.CompilerParams(vmem_limit_bytes=VMEM_LIMIT, **kw)


def _tile(n, cap):
    if n <= cap:
        return n
    t = (cap // LANE) * LANE
    while t > LANE and n % t:
        t -= LANE
    assert n % t == 0, (n, cap)
    return t


def _row_tile(n, cap):
    t = min(n, cap)
    while n % t:
        t //= 2
    return t


def _dot_nn(a, b):
    return jnp.dot(a, b, preferred_element_type=F32)


def _dot_nt(a, b):
    return lax.dot_general(a, b, (((1,), (1,)), ((), ())), preferred_element_type=F32)


def _dot_tn(a, b):
    return lax.dot_general(a, b, (((0,), (0,)), ((), ())), preferred_element_type=F32)


def _sigmoid(g):
    return 1.0 / (1.0 + jnp.exp(-g))


def _slab_geom(w):
    starts = [w * i for i in range(N_DEV)]
    aligned = [LANE * (s // LANE) for s in starts]
    offs = [s - a for s, a in zip(starts, aligned)]
    sw = LANE * (-(-(max(offs) + w) // LANE))
    return aligned, sw, aligned[-1] + sw


def _my_index():
    return 4 * lax.axis_index("x") + 2 * lax.axis_index("y") + lax.axis_index("c")


def _all_gather(arrs):
    n = len(arrs)

    def body(*refs):
        ins, outs = refs[:n], refs[n:2 * n]
        send_sems, recv_sems, local_sems = refs[2 * n:]
        x, y, c = lax.axis_index("x"), lax.axis_index("y"), lax.axis_index("c")
        me, sib = (x, y, c), (x, y, 1 - c)
        chips = [(1 - x, y), (x, 1 - y), (1 - x, 1 - y)]

        def idx(px, py, pc):
            return 4 * px + 2 * py + pc

        def copy(a, k, block, to, src=None):
            dst = outs[a].at[idx(*block)]
            return pltpu.make_async_remote_copy(
                src_ref=dst if src is None else src, dst_ref=dst,
                send_sem=send_sems.at[a, k], recv_sem=recv_sems.at[a, k],
                device_id=to, device_id_type=MESH)

        mine = [pltpu.make_async_copy(ins[a], outs[a].at[idx(*me)], local_sems.at[a]) for a in range(n)]
        for m in mine:
            m.start()
        first = []
        for a in range(n):
            first.append(copy(a, 0, me, sib, src=ins[a]))
            for j, chip in enumerate(chips):
                first.append(copy(a, 1 + j, me, (*chip, c), src=ins[a]))
        for cp in first:
            cp.start()
        passed = []
        for j, chip in enumerate(chips):
            for a in range(n):
                copy(a, 1 + j, (*chip, c), me).wait_recv()
                p = copy(a, 4 + j, (*chip, c), sib)
                p.start()
                passed.append(p)
        for a in range(n):
            copy(a, 0, sib, me).wait_recv()
        for j, chip in enumerate(chips):
            for a in range(n):
                copy(a, 4 + j, (*chip, 1 - c), me).wait_recv()
        for cp in first + passed:
            cp.wait_send()
        for m in mine:
            m.wait()

    any_spec = pl.BlockSpec(memory_space=pl.ANY)
    return pl.pallas_call(
        body, name="weights_all_gather",
        out_shape=[jax.ShapeDtypeStruct((N_DEV,) + a.shape, a.dtype) for a in arrs],
        in_specs=[any_spec] * n, out_specs=[any_spec] * n,
        scratch_shapes=[pltpu.SemaphoreType.DMA((n, 7)), pltpu.SemaphoreType.DMA((n, 7)),
                        pltpu.SemaphoreType.DMA((n,))],
    )(*arrs)


def _rel_chip(r):
    x, y = lax.axis_index("x"), lax.axis_index("y")
    return (x ^ (r >> 1), y ^ (r & 1))


def _rs_windows(specs):
    def window(ref, spec, blk):
        kind, n = spec
        if kind == "col":
            _, sw, _ = _slab_geom(n)
            return ref.at[pl.ds((n * blk) // LANE, sw // LANE)]
        start = pl.multiple_of(n * blk, n)
        return ref.at[pl.ds(start, n), :]
    return window


def _peer(k):
    x, y, c = lax.axis_index("x"), lax.axis_index("y"), lax.axis_index("c")
    return (x ^ (k >> 2), y ^ ((k >> 1) & 1), c ^ (k & 1))


def _direct_gather_copies(ins, outs, send_sems, recv_sems, local_sems):
    me = _my_index()
    remote, local = [], []
    for a, (src, dst) in enumerate(zip(ins, outs)):
        local.append(pltpu.make_async_copy(src, dst.at[me], local_sems.at[a]))
        for k in range(1, N_DEV):
            remote.append(pltpu.make_async_remote_copy(
                src_ref=src, dst_ref=dst.at[me], send_sem=send_sems.at[a, k - 1], recv_sem=recv_sems.at[a, k - 1],
                device_id=_peer(k), device_id_type=MESH))
    return remote, local


def _direct_scatter_copies(ins, outs, specs, send_sems, recv_sems):
    window = _rs_windows(specs)
    remote = []
    for a, (src, dst) in enumerate(zip(ins, outs)):
        for k in range(1, N_DEV):
            px, py, pc = _peer(k)
            remote.append(pltpu.make_async_remote_copy(
                src_ref=window(src, specs[a], 4 * px + 2 * py + pc), dst_ref=dst.at[k - 1],
                send_sem=send_sems.at[a, k - 1], recv_sem=recv_sems.at[a, k - 1],
                device_id=(px, py, pc), device_id_type=MESH))
    return remote


def _scatter_block_shape(g, spec):
    kind, w = spec
    return (_slab_geom(w)[1] // LANE, g.shape[1], LANE) if kind == "col" else (w, g.shape[1])


def _wait_all(remote, local=()):
    for cp in remote:
        cp.wait_recv()
    for cp in remote:
        cp.wait_send()
    for cp in local:
        cp.wait()


def _reduce_scatter_stage1(grads, specs):
    n = len(grads)
    window = _rs_windows(specs)

    def blk_shape(g, spec):
        kind, w = spec
        return (_slab_geom(w)[1] // LANE, g.shape[1], LANE) if kind == "col" else (w, g.shape[1])

    shapes = [blk_shape(g, s) for g, s in zip(grads, specs)]

    def body(*refs):
        ins, recvs = refs[:n], refs[n:2 * n]
        send_sems, recv_sems = refs[2 * n:]
        x, y, c = lax.axis_index("x"), lax.axis_index("y"), lax.axis_index("c")
        sib = (x, y, 1 - c)
        remotes = []
        for a in range(n):
            for r in range(4):
                px, py = _rel_chip(r)
                sib_blk = 4 * px + 2 * py + (1 - c)
                remotes.append(pltpu.make_async_remote_copy(
                    src_ref=window(ins[a], specs[a], sib_blk), dst_ref=recvs[a].at[r],
                    send_sem=send_sems.at[a, r], recv_sem=recv_sems.at[a, r],
                    device_id=sib, device_id_type=MESH))
        for cp in remotes:
            cp.start()
        for cp in remotes:
            cp.wait_recv()
        for cp in remotes:
            cp.wait_send()

    any_spec = pl.BlockSpec(memory_space=pl.ANY)
    return pl.pallas_call(
        body, name="grads_rs_sibling",
        out_shape=[jax.ShapeDtypeStruct((4,) + s, F32) for s in shapes],
        in_specs=[any_spec] * n, out_specs=[any_spec] * n,
        scratch_shapes=[pltpu.SemaphoreType.DMA((n, 4)), pltpu.SemaphoreType.DMA((n, 4))],
    )(*grads)


def _own_block_offsets(spec):
    kind, n = spec
    c = lax.axis_index("c")
    offs = []
    for r in range(4):
        px, py = _rel_chip(r)
        blk = 4 * px + 2 * py + c
        offs.append((n * blk) // LANE if kind == "col" else blk)
    return jnp.stack(offs).astype(jnp.int32)


def _pair_sum(g, recv, spec):
    kind, _ = spec
    offs = _own_block_offsets(spec)
    if kind == "col":
        _, T, M, _ = recv.shape
        grid = (4, T)
        g_spec = pl.BlockSpec((1, M, LANE), lambda r, t, o: (o[r] + t, 0, 0))
        r_spec = pl.BlockSpec((1, 1, M, LANE), lambda r, t, o: (r, t, 0, 0))
    else:
        _, nrow, C = recv.shape
        grid = (4,)
        g_spec = pl.BlockSpec((nrow, C), lambda r, o: (o[r], 0))
        r_spec = pl.BlockSpec((1, nrow, C), lambda r, o: (r, 0, 0))

    def body(o_ref, g_ref, r_ref, f_ref, h_ref):
        if kind == "col":
            s = g_ref[0] + r_ref[0, 0]
            f_ref[0, 0] = s
            h_ref[0, 0] = s.astype(BF16)
        else:
            s = g_ref[...] + r_ref[0]
            f_ref[0] = s
            h_ref[0] = s.astype(BF16)

    return pl.pallas_call(
        body, name="grads_pair_sum",
        grid_spec=pltpu.PrefetchScalarGridSpec(num_scalar_prefetch=1, grid=grid, in_specs=[g_spec, r_spec],
                                               out_specs=[r_spec, r_spec]),
        out_shape=[jax.ShapeDtypeStruct(recv.shape, F32), jax.ShapeDtypeStruct(recv.shape, BF16)],
        compiler_params=_cparams(),
    )(offs, g, recv)


def _reduce_scatter_stage2(parts):
    n = len(parts)

    def body(*refs):
        ins, recvs = refs[:n], refs[n:2 * n]
        send_sems, recv_sems = refs[2 * n:]
        c = lax.axis_index("c")
        copies = []
        for a in range(n):
            for r in range(1, 4):
                px, py = _rel_chip(r)
                copies.append(pltpu.make_async_remote_copy(
                    src_ref=ins[a].at[r], dst_ref=recvs[a].at[r - 1],
                    send_sem=send_sems.at[a, r - 1], recv_sem=recv_sems.at[a, r - 1],
                    device_id=(px, py, c), device_id_type=MESH))
        for cp in copies:
            cp.start()
        for cp in copies:
            cp.wait_recv()
        for cp in copies:
            cp.wait_send()

    any_spec = pl.BlockSpec(memory_space=pl.ANY)
    return pl.pallas_call(
        body, name="grads_rs_chips",
        out_shape=[jax.ShapeDtypeStruct((3,) + p.shape[1:], BF16) for p in parts],
        in_specs=[any_spec] * n, out_specs=[any_spec] * n,
        scratch_shapes=[pltpu.SemaphoreType.DMA((n, 3)), pltpu.SemaphoreType.DMA((n, 3))],
    )(*parts)


def _final_sum_cols(psum, recv):
    _, T, M, _ = psum.shape

    def body(p_ref, r_ref, o_ref):
        r = r_ref[...].astype(F32)
        o_ref[...] = ((p_ref[0, 0] + r[0, 0]) + r[1, 0]) + r[2, 0]

    return pl.pallas_call(
        body, name="grads_final_sum_cols", grid=(T,),
        in_specs=[pl.BlockSpec((1, 1, M, LANE), lambda t: (0, t, 0, 0)),
                  pl.BlockSpec((3, 1, M, LANE), lambda t: (0, t, 0, 0))],
        out_specs=pl.BlockSpec((M, LANE), lambda t: (0, t)),
        out_shape=jax.ShapeDtypeStruct((M, T * LANE), F32),
        compiler_params=_cparams(),
    )(psum, recv)


def _final_sum8(g, recv, spec):
    kind, _ = spec
    offs = _own_block_offsets(spec)
    if kind == "col":
        _, T, M, _ = recv.shape
        grid = (T,)
        in_specs = [pl.BlockSpec((1, M, LANE), lambda t, o: (o[0] + t, 0, 0)),
                    pl.BlockSpec((N_DEV - 1, 1, M, LANE), lambda t, o: (0, t, 0, 0))]
        out_spec = pl.BlockSpec((M, LANE), lambda t, o: (0, t))
        out_shape = jax.ShapeDtypeStruct((M, T * LANE), F32)
    else:
        _, nrow, C = recv.shape
        grid = (1,)
        in_specs = [pl.BlockSpec((nrow, C), lambda t, o: (o[0], 0)),
                    pl.BlockSpec((N_DEV - 1, nrow, C), lambda t, o: (0, 0, 0))]
        out_spec = pl.BlockSpec((nrow, C), lambda t, o: (0, 0))
        out_shape = jax.ShapeDtypeStruct((nrow, C), F32)

    def body(o_ref, g_ref, r_ref, out_ref):
        acc = g_ref[0] if kind == "col" else g_ref[...]
        for k in range(N_DEV - 1):
            acc = acc + (r_ref[k, 0] if kind == "col" else r_ref[k]).astype(F32)
        out_ref[...] = acc

    return pl.pallas_call(
        body, name="grads_final_sum8",
        grid_spec=pltpu.PrefetchScalarGridSpec(num_scalar_prefetch=1, grid=grid, in_specs=in_specs,
                                               out_specs=out_spec),
        out_shape=out_shape, compiler_params=_cparams(),
    )(offs, g, recv)


def _all_reduce_small(pack):
    R, P = pack.shape

    def body(x_ref, o_ref, gat_ref, send_sems, recv_sems):
        x, y, c = lax.axis_index("x"), lax.axis_index("y"), lax.axis_index("c")
        me = 4 * x + 2 * y + c
        gat_ref[me] = x_ref[...]
        copies = []
        for k in range(1, N_DEV):
            peer = (x ^ (k >> 2), y ^ ((k >> 1) & 1), c ^ (k & 1))
            copies.append(pltpu.make_async_remote_copy(
                src_ref=x_ref, dst_ref=gat_ref.at[me],
                send_sem=send_sems.at[k - 1], recv_sem=recv_sems.at[k - 1],
                device_id=peer, device_id_type=MESH))
        for cp in copies:
            cp.start()
        for cp in copies:
            cp.wait_recv()
        for cp in copies:
            cp.wait_send()
        acc = gat_ref[0]
        for d in range(1, N_DEV):
            acc = acc + gat_ref[d]
        o_ref[...] = acc

    vm = pl.BlockSpec(memory_space=pltpu.VMEM)
    return pl.pallas_call(
        body, name="small_all_reduce",
        out_shape=jax.ShapeDtypeStruct((R, P), F32),
        in_specs=[vm], out_specs=vm,
        scratch_shapes=[pltpu.VMEM((N_DEV, R, P), F32),
                        pltpu.SemaphoreType.DMA((N_DEV - 1,)), pltpu.SemaphoreType.DMA((N_DEV - 1,))],
    )(pack)


def _assemble(slabs, w):
    aligned, sw, total = _slab_geom(w)
    K = slabs.shape[1]
    tr = _row_tile(K, 256)

    def body(s_ref, o_ref):
        o_ref[...] = jnp.zeros(o_ref.shape, BF16)
        for i in range(N_DEV):
            a = aligned[i]
            o_ref[:, a:a + sw] = o_ref[:, a:a + sw] + s_ref[i]

    return pl.pallas_call(
        body, name="assemble_w_in", grid=(K // tr,),
        in_specs=[pl.BlockSpec((N_DEV, tr, sw), lambda i: (0, i, 0))],
        out_specs=pl.BlockSpec((tr, total), lambda i: (i, 0)),
        out_shape=jax.ShapeDtypeStruct((K, total), BF16),
        compiler_params=_cparams(),
    )(slabs)


def _rmsnorm_fwd(x, g, name):
    S, D = x.shape
    tm = _row_tile(S, 256)

    def body(x_ref, g_ref, h_ref):
        xv = x_ref[...]
        r = lax.rsqrt(jnp.mean(xv * xv, axis=-1, keepdims=True) + RMS_EPS)
        h_ref[...] = ((xv * r) * g_ref[...]).astype(BF16)

    return pl.pallas_call(
        body, name=name, grid=(S // tm,),
        in_specs=[pl.BlockSpec((tm, D), lambda i: (i, 0)), pl.BlockSpec((1, D), lambda i: (0, 0))],
        out_specs=pl.BlockSpec((tm, D), lambda i: (i, 0)),
        out_shape=jax.ShapeDtypeStruct((S, D), BF16),
        compiler_params=_cparams(),
    )(x, g)


def _rmsnorm_bwd(dh, x, g, dres, name):
    S, D = x.shape
    tm = _row_tile(S, 256)

    def body(dh_ref, x_ref, g_ref, dr_ref, dx_ref, dxb_ref, dg_ref):
        xv = x_ref[...]
        r = lax.rsqrt(jnp.mean(xv * xv, axis=-1, keepdims=True) + RMS_EPS)
        xhat = xv * r
        d = dh_ref[...]
        gd = d * g_ref[...]
        dx = r * (gd - xhat * jnp.mean(gd * xhat, axis=-1, keepdims=True)) + dr_ref[...]
        dx_ref[...] = dx
        dxb_ref[...] = dx.astype(BF16)

        @pl.when(pl.program_id(0) == 0)
        def _():
            dg_ref[...] = jnp.zeros(dg_ref.shape, F32)
        dg_ref[...] += jnp.sum(d * xhat, axis=0, keepdims=True)

    row = pl.BlockSpec((tm, D), lambda i: (i, 0))
    vec = pl.BlockSpec((1, D), lambda i: (0, 0))
    return pl.pallas_call(
        body, name=name, grid=(S // tm,),
        in_specs=[row, row, vec, row], out_specs=[row, row, vec],
        out_shape=[jax.ShapeDtypeStruct((S, D), F32), jax.ShapeDtypeStruct((S, D), BF16),
                   jax.ShapeDtypeStruct((1, D), F32)],
        compiler_params=_cparams(),
    )(dh, x, g, dres)


def _loss_head(x, tgt, g):
    S, D = x.shape
    tm = _row_tile(S, 256)

    def body(x_ref, t_ref, g_ref, dx_ref, dxb_ref, dg_ref, loss_ref):
        xv = x_ref[...]
        r = lax.rsqrt(jnp.mean(xv * xv, axis=-1, keepdims=True) + RMS_EPS)
        xhat = xv * r
        gv = g_ref[...]
        err = xhat * gv - t_ref[...]
        d = err * (1.0 / D)
        gd = d * gv
        dx = r * (gd - xhat * jnp.mean(gd * xhat, axis=-1, keepdims=True))
        dx_ref[...] = dx
        dxb_ref[...] = dx.astype(BF16)

        @pl.when(pl.program_id(0) == 0)
        def _():
            dg_ref[...] = jnp.zeros(dg_ref.shape, F32)
            loss_ref[...] = jnp.zeros(loss_ref.shape, F32)
        dg_ref[...] += jnp.sum(d * xhat, axis=0, keepdims=True)
        per_tok = jnp.sum(err * err, axis=-1, keepdims=True) * (1.0 / D)
        loss_ref[...] += 0.5 * jnp.sum(per_tok, axis=0, keepdims=True)

    row = pl.BlockSpec((tm, D), lambda i: (i, 0))
    vec = pl.BlockSpec((1, D), lambda i: (0, 0))
    return pl.pallas_call(
        body, name="loss_head", grid=(S // tm,),
        in_specs=[row, row, vec],
        out_specs=[row, row, vec, pl.BlockSpec((1, LANE), lambda i: (0, 0))],
        out_shape=[jax.ShapeDtypeStruct((S, D), F32), jax.ShapeDtypeStruct((S, D), BF16),
                   jax.ShapeDtypeStruct((1, D), F32), jax.ShapeDtypeStruct((1, LANE), F32)],
        compiler_params=_cparams(),
    )(x, tgt, g)


def _adamw(w, g, m, v, name):
    R, C = w.shape
    tr = _row_tile(R, 256)
    c1 = 1.0 - ADAM_B1 ** ADAM_STEP
    c2 = 1.0 - ADAM_B2 ** ADAM_STEP

    def body(w_ref, g_ref, m_ref, v_ref, d_ref, nm_ref, nv_ref):
        gv = g_ref[...]
        nm = ADAM_B1 * m_ref[...] + (1.0 - ADAM_B1) * gv
        nv = ADAM_B2 * v_ref[...] + (1.0 - ADAM_B2) * (gv * gv)
        d_ref[...] = -ADAM_LR * ((nm / c1) / (jnp.sqrt(nv / c2) + ADAM_EPS) + ADAM_WD * w_ref[...])
        nm_ref[...] = nm
        nv_ref[...] = nv

    spec = pl.BlockSpec((tr, C), lambda i: (i, 0))
    return pl.pallas_call(
        body, name=name, grid=(R // tr,),
        in_specs=[spec] * 4, out_specs=[spec] * 3,
        out_shape=[jax.ShapeDtypeStruct((R, C), F32)] * 3,
        compiler_params=_cparams(),
    )(w, g, m, v)


def _proj(h, wfull, col0, ncols, out_dtype, name, rope=None):
    S, K = h.shape
    tm = _row_tile(S, MM_TILE)
    tn = math.gcd(_tile(ncols, MM_TILE), col0) if col0 else _tile(ncols, MM_TILE)
    if rope is not None:
        tn = _tile(math.gcd(ncols, rope[1]), MM_TILE)
    assert ncols % tn == 0 and col0 % tn == 0
    cb = col0 // tn

    def body(*refs):
        if rope is None:
            a_ref, b_ref, o_ref = refs
        else:
            a_ref, b_ref, t_ref, o_ref = refs
        acc = _dot_nn(a_ref[...], b_ref[...])
        if rope is not None:
            t0, t1, t2 = (jnp.tile(t_ref[i], (1, tn // LANE)) for i in range(3))
            roped = (acc * t0 + pltpu.roll(acc, tn - ROT_DIM // 2, axis=1) * t1
                     + pltpu.roll(acc, ROT_DIM // 2, axis=1) * t2)
            acc = jnp.where(pl.program_id(1) < rope[1] // tn, roped, acc)
        o_ref[...] = acc.astype(out_dtype)

    in_specs = [pl.BlockSpec((tm, K), lambda i, j: (i, 0)), pl.BlockSpec((K, tn), lambda i, j: (0, cb + j))]
    args = [h, wfull]
    if rope is not None:
        in_specs.append(pl.BlockSpec((3, tm, LANE), lambda i, j: (0, i, 0)))
        args.append(rope[0])
    return pl.pallas_call(
        body, name=name, grid=(S // tm, ncols // tn),
        in_specs=in_specs, out_specs=pl.BlockSpec((tm, tn), lambda i, j: (i, j)),
        out_shape=jax.ShapeDtypeStruct((S, ncols), out_dtype),
        compiler_params=_cparams(),
    )(*args)


def _out_proj_res(y, wo, xres, name):
    S, W = y.shape
    D = wo.shape[1]
    tm, tn = _row_tile(S, MM_TILE), _tile(D, MM_TILE)

    def body(a_ref, b_ref, r_ref, o_ref):
        o_ref[...] = r_ref[...] + _dot_nn(a_ref[...], b_ref[...])

    return pl.pallas_call(
        body, name=name, grid=(S // tm, D // tn),
        in_specs=[pl.BlockSpec((tm, W), lambda i, j: (i, 0)), pl.BlockSpec((W, tn), lambda i, j: (0, j)),
                  pl.BlockSpec((tm, tn), lambda i, j: (i, j))],
        out_specs=pl.BlockSpec((tm, tn), lambda i, j: (i, j)),
        out_shape=jax.ShapeDtypeStruct((S, D), F32),
        compiler_params=_cparams(),
    )(y, wo, xres)


def _matmul_nt(parts, wfull, out_rows, name):
    S = parts[0][0].shape[-2]
    tm, tn = _row_tile(S, MM_TILE), _tile(out_rows, MM_TILE)
    plan, lo = [], 0
    for arr, lead, col0 in parts:
        n_p = arr.shape[-1]
        tk = math.gcd(_tile(n_p, 1024), col0) if col0 else _tile(n_p, 1024)
        steps = n_p // tk
        plan.append((lead, col0 // tk, tk, lo, lo + steps))
        lo += steps
    nk = lo
    npart = len(parts)

    def body(*refs):
        a_refs, w_refs = refs[:npart], refs[npart:2 * npart]
        o_ref, acc_ref = refs[2 * npart], refs[2 * npart + 1]
        k = pl.program_id(2)

        @pl.when(k == 0)
        def _():
            acc_ref[...] = jnp.zeros(acc_ref.shape, F32)
        for p, (_, _, _, lo_p, hi_p) in enumerate(plan):
            @pl.when((k >= lo_p) & (k < hi_p))
            def _(p=p):
                acc_ref[...] += _dot_nt(a_refs[p][...], w_refs[p][...])

        @pl.when(k == nk - 1)
        def _():
            o_ref[...] = acc_ref[...]

    in_specs, args = [], []
    for (arr, lead, col0), (_, cb, tk, lo_p, hi_p) in zip(parts, plan):
        def kk(k, lo_p=lo_p, hi_p=hi_p):
            return jnp.clip(k - lo_p, 0, hi_p - lo_p - 1)
        if lead is None:
            in_specs.append(pl.BlockSpec((tm, tk), lambda i, j, k, kk=kk: (i, kk(k))))
        else:
            in_specs.append(pl.BlockSpec((None, tm, tk), lambda i, j, k, kk=kk, lead=lead: (lead, i, kk(k))))
        args.append(arr)
    for (_, cb, tk, lo_p, hi_p) in plan:
        def kk(k, lo_p=lo_p, hi_p=hi_p):
            return jnp.clip(k - lo_p, 0, hi_p - lo_p - 1)
        in_specs.append(pl.BlockSpec((tn, tk), lambda i, j, k, kk=kk, cb=cb: (j, cb + kk(k))))
        args.append(wfull)
    return pl.pallas_call(
        body, name=name, grid=(S // tm, out_rows // tn, nk),
        in_specs=in_specs, out_specs=pl.BlockSpec((tm, tn), lambda i, j, k: (i, j)),
        out_shape=jax.ShapeDtypeStruct((S, out_rows), F32),
        scratch_shapes=[pltpu.VMEM((tm, tn), F32)],
        compiler_params=_cparams(),
    )(*args)


def _matmul_tn(a, parts, total, name, tile_major=False, also_bf16=False):
    S, M = a.shape
    tm, ts = _tile(M, MM_TILE), _row_tile(S, MM_TILE)
    nout = 2 if also_bf16 else 1
    outs = None
    for idx, (arr, lead, col0) in enumerate(parts):
        n_p = arr.shape[-1]
        tn = math.gcd(_tile(n_p, MM_TILE), col0) if col0 else _tile(n_p, MM_TILE)
        cb = col0 // tn
        nk = S // ts

        def body(*refs, nk=nk, tn=tn):
            a_ref, b_ref = refs[0], refs[1]
            o_refs, acc_ref = refs[-1 - nout:-1], refs[-1]
            k = pl.program_id(2)

            @pl.when(k == 0)
            def _():
                acc_ref[...] = jnp.zeros(acc_ref.shape, F32)
            acc_ref[...] += _dot_tn(a_ref[...], b_ref[...])

            @pl.when(k == nk - 1)
            def _():
                for o_ref in o_refs:
                    if tile_major:
                        for t in range(tn // LANE):
                            o_ref[t] = acc_ref[:, LANE * t:LANE * (t + 1)].astype(o_ref.dtype)
                    else:
                        o_ref[...] = acc_ref[...].astype(o_ref.dtype)

        in_specs = [pl.BlockSpec((ts, tm), lambda i, j, k: (k, i))]
        if lead is None:
            in_specs.append(pl.BlockSpec((ts, tn), lambda i, j, k: (k, j)))
        else:
            in_specs.append(pl.BlockSpec((None, ts, tn), lambda i, j, k, lead=lead: (lead, k, j)))
        args = [a, arr]
        aliases = {}
        if outs is not None:
            in_specs += [pl.BlockSpec(memory_space=pl.ANY)] * nout
            args += list(outs)
            aliases = {2 + o: o for o in range(nout)}
        if tile_major:
            out_spec = pl.BlockSpec((tn // LANE, tm, LANE), lambda i, j, k, cb=cb: (cb + j, i, 0))
            shape = (total // LANE, M, LANE)
        else:
            out_spec = pl.BlockSpec((tm, tn), lambda i, j, k, cb=cb: (i, cb + j))
            shape = (M, total)
        outs = pl.pallas_call(
            body, name=f"{name}_{idx}", grid=(M // tm, n_p // tn, nk),
            in_specs=in_specs, out_specs=[out_spec] * nout,
            out_shape=[jax.ShapeDtypeStruct(shape, dt) for dt in (F32, BF16)[:nout]],
            scratch_shapes=[pltpu.VMEM((tm, tn), F32)],
            input_output_aliases=aliases,
            compiler_params=_cparams(),
        )(*args)
    return tuple(outs) if also_bf16 else outs[0]


def _log_sigmoid(z):
    e = jnp.exp(-jnp.abs(z))
    return jnp.minimum(z, 0.0) - jnp.where(e < 1e-4, e * (1.0 - 0.5 * e), jnp.log(1.0 + e))


def _fox_gate_fwd(fl, bias):
    S = fl.shape[0]

    def body(f_ref, b_ref, c_ref):
        row = lax.broadcasted_iota(jnp.int32, (8, LANE), 0)

        def step(i, carry):
            r0 = pl.multiple_of(i * 8, 8)
            t = _log_sigmoid(f_ref[pl.ds(r0, 8), :] + b_ref[...])
            for sh in (1, 2, 4):
                t = t + jnp.where(row >= sh, pltpu.roll(t, sh, axis=0), 0.0)
            t = t + carry
            c_ref[pl.ds(r0, 8), :] = t
            return jnp.sum(jnp.where(row == 7, t, 0.0), axis=0, keepdims=True)

        lax.fori_loop(0, S // 8, step, jnp.zeros((1, LANE), F32))

    vm = pl.BlockSpec(memory_space=pltpu.VMEM)
    return pl.pallas_call(
        body, name="fox_gate_fwd", in_specs=[vm, vm], out_specs=vm,
        out_shape=jax.ShapeDtypeStruct((S, LANE), F32),
        compiler_params=_cparams(),
    )(fl, bias)


def _fox_gate_bwd(fl, bias, dc):
    S = fl.shape[0]

    def body(f_ref, b_ref, d_ref, o_ref, db_ref, acc_ref):
        row = lax.broadcasted_iota(jnp.int32, (8, LANE), 0)
        nt = S // 8

        def step(ii, carry):
            carry_c, carry_b = carry
            r0 = pl.multiple_of((nt - 1 - ii) * 8, 8)
            t = d_ref[pl.ds(r0, 8), :]
            for sh in (1, 2, 4):
                t = t + jnp.where(row < 8 - sh, pltpu.roll(t, 8 - sh, axis=0), 0.0)
            t = t + carry_c
            z = f_ref[pl.ds(r0, 8), :] + b_ref[...]
            dz = t * _sigmoid(-z)
            acc_ref[pl.ds(r0, 8), :] = dz
            first = jnp.sum(jnp.where(row == 0, t, 0.0), axis=0, keepdims=True)
            return first, carry_b + jnp.sum(dz, axis=0, keepdims=True)

        zero = jnp.zeros((1, LANE), F32)
        _, db = lax.fori_loop(0, nt, step, (zero, zero))
        db_ref[...] = db
        o_ref[...] = acc_ref[...].astype(BF16)

    vm = pl.BlockSpec(memory_space=pltpu.VMEM)
    return pl.pallas_call(
        body, name="fox_gate_bwd", in_specs=[vm, vm, vm], out_specs=[vm, vm],
        out_shape=[jax.ShapeDtypeStruct((S, LANE), BF16), jax.ShapeDtypeStruct((1, LANE), F32)],
        scratch_shapes=[pltpu.VMEM((S, LANE), F32)],
        compiler_params=_cparams(),
    )(fl, bias, dc)


def _bias_lanes(col, lane, e, first):
    o0 = HEAD_DIM * (1 - e)
    hi = col.astype(BF16)
    r1 = col - hi.astype(F32)
    mid = r1.astype(BF16)
    lo = (r1 - mid.astype(F32)).astype(BF16)
    d0 = o0 if first else o0 + 3
    t = jnp.where((lane >= o0) & (lane < o0 + 6), jnp.ones(lane.shape, BF16), jnp.zeros(lane.shape, BF16))
    t = jnp.where(lane == d0, hi, t)
    t = jnp.where(lane == d0 + 1, mid, t)
    return jnp.where(lane == d0 + 2, lo, t)


def _fox_fwd(qkv, gate, c, H, gather=()):
    na = len(gather)
    S = qkv.shape[0]
    W = H * HEAD_DIM
    HP = H // 2
    tq = _row_tile(S, 512)
    nq = S // tq
    wb = W // LANE
    scale = HEAD_DIM ** -0.5

    def body(*refs):
        q_ref, k_ref, v_ref, g_ref, c_ref = refs[:5]
        y_ref, o_ref, a_ref = refs[5 + na:8 + na]
        kaug_sc, vaug_sc, qaug_sc, s_sc, mb_sc, m_sc, acc_sc = refs[8 + 2 * na:15 + 2 * na]
        hp, qi = pl.program_id(0), pl.program_id(1)
        if na:
            remote, local = _direct_gather_copies(refs[5:5 + na], refs[8 + na:8 + 2 * na], *refs[15 + 2 * na:])

            @pl.when((hp == 0) & (qi == 0))
            def _():
                for cp in remote + local:
                    cp.start()
        lane = lax.broadcasted_iota(jnp.int32, (tq, LANE), 1)
        own = [lane < HEAD_DIM, lane >= HEAD_DIM]
        rows = lax.broadcasted_iota(jnp.int32, (tq, tq), 0)
        cols = lax.broadcasted_iota(jnp.int32, (tq, tq), 1)

        def bias_lanes(col, e, first):
            return _bias_lanes(col, lane, e, first)

        def head_col(tile, e):
            return jnp.sum(jnp.where(lane == 2 * hp + e, tile, 0.0), axis=1, keepdims=True)

        @pl.when(qi == 0)
        def _():
            def chunk(i, carry):
                r0 = pl.multiple_of(i * tq, tq)
                kb, vb, cb = k_ref[pl.ds(r0, tq), :], v_ref[pl.ds(r0, tq), :], c_ref[pl.ds(r0, tq), :]
                for e in range(2):
                    kaug_sc[e, pl.ds(r0, tq), :] = jnp.where(own[e], kb, bias_lanes(-head_col(cb, e), e, False))
                    vaug_sc[e, pl.ds(r0, tq), :] = jnp.where(own[e], vb, jnp.ones((tq, LANE), BF16))
                return carry
            lax.fori_loop(0, nq, chunk, 0)

        q = q_ref[...] * jnp.asarray(scale, BF16)
        crow = c_ref[pl.ds(pl.multiple_of(qi * tq, tq), tq), :]
        ctq = [head_col(crow, e) for e in range(2)]
        for e in range(2):
            qaug_sc[e] = jnp.where(own[e], q, bias_lanes(ctq[e], e, True))
        m_sc[...] = jnp.full(m_sc.shape, NEG_INF, F32)
        acc_sc[...] = jnp.zeros(acc_sc.shape, F32)

        def scores(blk, slot, masked):
            k0 = pl.multiple_of(blk * tq, tq)
            for e in range(2):
                s = _dot_nt(qaug_sc[e], kaug_sc[e, pl.ds(k0, tq), :])
                if masked:
                    s = jnp.where(rows >= cols, s, NEG_INF)
                s_sc[slot, e] = s
                mb_sc[slot, e] = jnp.broadcast_to(jnp.max(s, axis=1, keepdims=True), (tq, LANE))

        def accumulate(blk, slot):
            k0 = pl.multiple_of(blk * tq, tq)
            for e in range(2):
                m_prev = m_sc[e]
                m_new = jnp.maximum(m_prev, mb_sc[slot, e])
                p = jnp.exp(s_sc[slot, e] - jnp.tile(m_new, (1, tq // LANE)))
                acc_sc[e] = jnp.exp(m_prev - m_new) * acc_sc[e] + _dot_nn(p.astype(BF16), vaug_sc[e, pl.ds(k0, tq), :])
                m_sc[e] = m_new

        def block_of(t):
            return jnp.where(t == 0, qi, t - 1)

        scores(qi, 0, True)

        def loop_body(t, carry):
            scores(t, (t + 1) % 2, False)
            accumulate(block_of(t), t % 2)
            return carry

        lax.fori_loop(0, qi, loop_body, 0)
        accumulate(block_of(qi), qi % 2)
        o_e, a_e = [], []
        for e in range(2):
            acc = acc_sc[e]
            l = pltpu.roll(acc, HEAD_DIM, axis=1)
            o_e.append(acc / l)
            a_e.append(ctq[e] - (m_sc[e] + jnp.log(l)))
        o = jnp.where(own[0], o_e[0], o_e[1])
        g = g_ref[...]
        y_ref[...] = (o * (g * _sigmoid(g))).astype(BF16)
        o_ref[...] = o.astype(BF16)
        a_ref[0] = jnp.where(own[0], a_e[0], a_e[1])
        if na:
            @pl.when((hp == HP - 1) & (qi == nq - 1))
            def _():
                _wait_all(remote, local)

    any_spec = pl.BlockSpec(memory_space=pl.ANY)
    sems = [pltpu.SemaphoreType.DMA((na, N_DEV - 1)), pltpu.SemaphoreType.DMA((na, N_DEV - 1)),
            pltpu.SemaphoreType.DMA((na,))] if na else []
    outs = pl.pallas_call(
        body, name="fox_attn_fwd", grid=(HP, nq),
        in_specs=[pl.BlockSpec((tq, LANE), lambda h, i: (i, h)),
                  pl.BlockSpec((S, LANE), lambda h, i: (0, wb + h)),
                  pl.BlockSpec((S, LANE), lambda h, i: (0, 2 * wb + h)),
                  pl.BlockSpec((tq, LANE), lambda h, i: (i, h)),
                  pl.BlockSpec((S, LANE), lambda h, i: (0, 0))] + [any_spec] * na,
        out_specs=[pl.BlockSpec((tq, LANE), lambda h, i: (i, h)),
                   pl.BlockSpec((tq, LANE), lambda h, i: (i, h)),
                   pl.BlockSpec((1, tq, LANE), lambda h, i: (h, i, 0))] + [any_spec] * na,
        out_shape=[jax.ShapeDtypeStruct((S, W), BF16), jax.ShapeDtypeStruct((S, W), BF16),
                   jax.ShapeDtypeStruct((HP, S, LANE), F32)]
        + [jax.ShapeDtypeStruct((N_DEV,) + g.shape, g.dtype) for g in gather],
        scratch_shapes=[pltpu.VMEM((2, S, LANE), BF16), pltpu.VMEM((2, S, LANE), BF16),
                        pltpu.VMEM((2, tq, LANE), BF16), pltpu.VMEM((2, 2, tq, tq), F32),
                        pltpu.VMEM((2, 2, tq, LANE), F32), pltpu.VMEM((2, tq, LANE), F32),
                        pltpu.VMEM((2, tq, LANE), F32)] + sems,
        compiler_params=_cparams(),
    )(qkv, qkv, qkv, gate, c, *gather)
    return outs[0], outs[1], outs[2], list(outs[3:])


def _fox_bwd_prep(qkv, dy, gate, o, a, H):
    S = qkv.shape[0]
    W = H * HEAD_DIM
    HP = H // 2
    tq = _row_tile(S, 512)
    scale = HEAD_DIM ** -0.5

    def body(q_ref, dy_ref, g_ref, o_ref, a_ref, qa_ref, da_ref, dg_ref):
        lane = lax.broadcasted_iota(jnp.int32, (tq, LANE), 1)
        own = [lane < HEAD_DIM, lane >= HEAD_DIM]
        q = q_ref[...] * jnp.asarray(scale, BF16)
        dyv, g, ov, at = dy_ref[...], g_ref[...], o_ref[...].astype(F32), a_ref[0]
        sg = _sigmoid(g)
        dob = (dyv * (g * sg)).astype(BF16)
        dg_ref[...] = (dyv * ov * (sg * (1.0 + g * (1.0 - sg)))).astype(BF16)
        prod = dob.astype(F32) * ov
        for e in range(2):
            a_col = jnp.max(jnp.where(own[e], at, -jnp.inf), axis=1, keepdims=True)
            d_col = jnp.sum(jnp.where(own[e], prod, 0.0), axis=1, keepdims=True)
            qa_ref[e] = jnp.where(own[e], q, _bias_lanes(a_col, lane, e, True))
            da_ref[e] = jnp.where(own[e], dob, _bias_lanes(-d_col, lane, e, True))

    blk = pl.BlockSpec((tq, LANE), lambda h, i: (i, h))
    pair = pl.BlockSpec((2, tq, LANE), lambda h, i: (0, i, h))
    return pl.pallas_call(
        body, name="fox_attn_bwd_prep", grid=(HP, S // tq),
        in_specs=[blk, blk, blk, blk, pl.BlockSpec((1, tq, LANE), lambda h, i: (h, i, 0))],
        out_specs=[pair, pair, blk],
        out_shape=[jax.ShapeDtypeStruct((2, S, W), BF16), jax.ShapeDtypeStruct((2, S, W), BF16),
                   jax.ShapeDtypeStruct((S, W), BF16)],
        compiler_params=_cparams(),
    )(qkv, dy, gate, o, a)


def _fox_bwd(qaug, doaug, qkv, c, H, scatter=(), scatter_specs=()):
    na = len(scatter)
    S = qkv.shape[0]
    W = H * HEAD_DIM
    HP = H // 2
    tq = _row_tile(S, 512)
    nq = S // tq
    wb = W // LANE
    scale = HEAD_DIM ** -0.5

    def body(*refs):
        qa_ref, da_ref, k_ref, v_ref, c_ref = refs[:5]
        out_ref, dcr_ref, dcc_ref = refs[5 + na:8 + na]
        dq_sc, dk_sc, dv_sc = refs[8 + 2 * na:11 + 2 * na]
        hp, kj = pl.program_id(0), pl.program_id(1)
        if na:
            remote = _direct_scatter_copies(refs[5:5 + na], refs[8 + na:8 + 2 * na], scatter_specs,
                                            *refs[11 + 2 * na:])

            @pl.when((hp == 0) & (kj == 0))
            def _():
                for cp in remote:
                    cp.start()
        lane = lax.broadcasted_iota(jnp.int32, (tq, LANE), 1)
        own = [lane < HEAD_DIM, lane >= HEAD_DIM]
        rows = lax.broadcasted_iota(jnp.int32, (tq, tq), 0)
        cols = lax.broadcasted_iota(jnp.int32, (tq, tq), 1)

        @pl.when(kj == 0)
        def _():
            dq_sc[...] = jnp.zeros(dq_sc.shape, F32)

        @pl.when((kj == 0) & (hp == 0))
        def _():
            dcr_ref[...] = jnp.zeros(dcr_ref.shape, F32)
            dcc_ref[...] = jnp.zeros(dcc_ref.shape, F32)

        kblk, vblk, cblk = k_ref[...], v_ref[...], c_ref[...]
        one, zero = jnp.ones((tq, LANE), BF16), jnp.zeros((tq, LANE), BF16)
        ka, va = [], []
        for e in range(2):
            o0 = HEAD_DIM * (1 - e)
            c_col = jnp.sum(jnp.where(lane == 2 * hp + e, cblk, 0.0), axis=1, keepdims=True)
            ka.append(jnp.where(own[e], kblk, _bias_lanes(-c_col, lane, e, False)))
            va.append(jnp.where(own[e], vblk, jnp.where((lane >= o0) & (lane < o0 + 3), one, zero)))
        dk_sc[...] = jnp.zeros(dk_sc.shape, F32)
        dv_sc[...] = jnp.zeros(dv_sc.shape, F32)

        def step(i, masked):
            r0 = pl.multiple_of(i * tq, tq)
            for e in range(2):
                qa = qa_ref[e, pl.ds(r0, tq), :]
                da = da_ref[e, pl.ds(r0, tq), :]
                p = jnp.exp(_dot_nt(qa, ka[e]))
                if masked:
                    p = jnp.where(rows >= cols, p, 0.0)
                ds = p * _dot_nt(da, va[e])
                pb, dsb = p.astype(BF16), ds.astype(BF16)
                dv_sc[e] += _dot_tn(pb, da)
                dk_sc[e] += _dot_tn(dsb, qa)
                dq_sc[e, pl.ds(r0, tq), :] += _dot_nn(dsb, ka[e])

        step(kj, True)

        def loop_body(i, carry):
            step(i, False)
            return carry

        lax.fori_loop(kj + 1, nq, loop_body, 0)
        k0 = pl.multiple_of(kj * tq, tq)
        out_ref[1, pl.ds(k0, tq), :] = jnp.where(own[0], dk_sc[0], dk_sc[1]).astype(BF16)
        out_ref[2, pl.ds(k0, tq), :] = jnp.where(own[0], dv_sc[0], dv_sc[1]).astype(BF16)

        def put_lane(ref, r0, e, tile, src_lane):
            col = jnp.sum(jnp.where(lane == src_lane, tile, 0.0), axis=1, keepdims=True)
            ref[pl.ds(r0, tq), :] = jnp.where(lane == 2 * hp + e, col, ref[pl.ds(r0, tq), :])

        for e in range(2):
            put_lane(dcc_ref, k0, e, dk_sc[e], HEAD_DIM * (1 - e) + 3)

        @pl.when(kj == nq - 1)
        def _():
            def chunk(i, carry):
                r0 = pl.multiple_of(i * tq, tq)
                d0, d1 = dq_sc[0, pl.ds(r0, tq), :], dq_sc[1, pl.ds(r0, tq), :]
                out_ref[0, pl.ds(r0, tq), :] = (jnp.where(own[0], d0, d1) * scale).astype(BF16)
                put_lane(dcr_ref, r0, 0, d0, HEAD_DIM)
                put_lane(dcr_ref, r0, 1, d1, 0)
                return carry
            lax.fori_loop(0, nq, chunk, 0)

        if na:
            @pl.when((hp == HP - 1) & (kj == nq - 1))
            def _():
                _wait_all(remote)

    pair = pl.BlockSpec((2, S, LANE), lambda h, j: (0, 0, h))
    vec = pl.BlockSpec((S, LANE), lambda h, j: (0, 0))
    any_spec = pl.BlockSpec(memory_space=pl.ANY)
    sems = [pltpu.SemaphoreType.DMA((na, N_DEV - 1)), pltpu.SemaphoreType.DMA((na, N_DEV - 1))] if na else []
    outs = pl.pallas_call(
        body, name="fox_attn_bwd", grid=(HP, nq),
        in_specs=[pair, pair,
                  pl.BlockSpec((tq, LANE), lambda h, j: (j, wb + h)),
                  pl.BlockSpec((tq, LANE), lambda h, j: (j, 2 * wb + h)),
                  pl.BlockSpec((tq, LANE), lambda h, j: (j, 0))] + [any_spec] * na,
        out_specs=[pl.BlockSpec((3, S, LANE), lambda h, j: (0, 0, h)), vec, vec] + [any_spec] * na,
        out_shape=[jax.ShapeDtypeStruct((3, S, W), BF16), jax.ShapeDtypeStruct((S, LANE), F32),
                   jax.ShapeDtypeStruct((S, LANE), F32)]
        + [jax.ShapeDtypeStruct((N_DEV - 1,) + _scatter_block_shape(g, s), g.dtype)
           for g, s in zip(scatter, scatter_specs)],
        scratch_shapes=[pltpu.VMEM((2, S, LANE), F32), pltpu.VMEM((2, tq, LANE), F32),
                        pltpu.VMEM((2, tq, LANE), F32)] + sems,
        compiler_params=_cparams(),
    )(qaug, doaug, qkv, qkv, c, *scatter)
    return outs[0], outs[1], outs[2], list(outs[3:])


def _swa_pick(blk, half, lane):
    b = blk.astype(F32)
    r = pltpu.roll(b, HEAD_DIM, axis=1)
    return jnp.where(jnp.logical_xor(lane < HEAD_DIM, half == 1), b, r).astype(BF16)


def _swa_stack(t, lane, G):
    pieces = []
    z = jnp.zeros((SWA_BLOCK, LANE), t.dtype)
    for j in range(G // 2):
        tile = t[:, LANE * j:LANE * (j + 1)]
        pieces += [jnp.where(lane < HEAD_DIM, tile, z), jnp.where(lane < HEAD_DIM, z, tile)]
    return jnp.concatenate(pieces, axis=0)


def _swa_unstack(st, lane, G):
    tiles = []
    for j in range(G // 2):
        a = st[2 * j * SWA_BLOCK:(2 * j + 1) * SWA_BLOCK]
        b = st[(2 * j + 1) * SWA_BLOCK:(2 * j + 2) * SWA_BLOCK]
        tiles.append(jnp.where(lane < HEAD_DIM, a, b))
    return jnp.concatenate(tiles, axis=1)


def _swa_scores(q_ref, kp_ref, kc_ref, vp_ref, vc_ref, sink_ref, kvh, n, G):
    R = G * SWA_BLOCK
    lane = lax.broadcasted_iota(jnp.int32, (SWA_BLOCK, LANE), 1)
    half = kvh % 2
    kk = jnp.concatenate([_swa_pick(kp_ref[...], half, lane), _swa_pick(kc_ref[...], half, lane)], axis=0)
    vv = jnp.concatenate([_swa_pick(vp_ref[...], half, lane), _swa_pick(vc_ref[...], half, lane)], axis=0)
    qstack = _swa_stack(q_ref[...], lane, G) * jnp.asarray(HEAD_DIM ** -0.5, BF16)
    s = _dot_nt(qstack, kk)
    t_loc = lax.broadcasted_iota(jnp.int32, (R, 2 * SWA_BLOCK), 0) & (SWA_BLOCK - 1)
    j_loc = lax.broadcasted_iota(jnp.int32, (R, 2 * SWA_BLOCK), 1)
    diff = t_loc + SWA_BLOCK - j_loc
    mask = (diff >= 0) & (diff < SWA_BLOCK) & ((n > 0) | (j_loc >= SWA_BLOCK))
    s = jnp.where(mask, s, NEG_INF)
    srow = sink_ref[...]
    lane1 = lax.broadcasted_iota(jnp.int32, (1, LANE), 1)
    sink = jnp.concatenate(
        [jnp.broadcast_to(jnp.sum(jnp.where(lane1 == kvh * G + g, srow, 0.0), axis=1, keepdims=True), (SWA_BLOCK, 1))
         for g in range(G)], axis=0)
    m = jnp.maximum(jnp.max(s, axis=1, keepdims=True), sink)
    e = jnp.exp(s - m)
    es = jnp.exp(sink - m)
    den = jnp.sum(e, axis=1, keepdims=True) + es
    return qstack, kk, vv, e / den, es / den, lane


def _swa_fwd(qkv, gate, sinks, HQ, HKV):
    S = qkv.shape[0]
    G = HQ // HKV
    WQ, KVW = HQ * HEAD_DIM, HKV * HEAD_DIM
    nb = S // SWA_BLOCK
    GW = G * HEAD_DIM
    kb, vb = WQ // LANE, (WQ + KVW) // LANE

    def body(q_ref, kp_ref, kc_ref, vp_ref, vc_ref, g_ref, sink_ref, y_ref, o_ref):
        kvh, n = pl.program_id(0), pl.program_id(1)
        _, _, vv, p, _, lane = _swa_scores(q_ref, kp_ref, kc_ref, vp_ref, vc_ref, sink_ref, kvh, n, G)
        o = _swa_unstack(_dot_nn(p.astype(BF16), vv), lane, G)
        g = g_ref[...]
        y_ref[...] = (o * (g * _sigmoid(g))).astype(BF16)
        o_ref[...] = o.astype(BF16)

    blk = lambda cb, prev: pl.BlockSpec(
        (SWA_BLOCK, LANE), lambda h, n, cb=cb, prev=prev: (jnp.maximum(n - prev, 0), cb + h // 2))
    qspec = pl.BlockSpec((SWA_BLOCK, GW), lambda h, n: (n, h))
    return pl.pallas_call(
        body, name="swa_attn_fwd", grid=(HKV, nb),
        in_specs=[qspec, blk(kb, 1), blk(kb, 0), blk(vb, 1), blk(vb, 0), qspec,
                  pl.BlockSpec((1, LANE), lambda h, n: (0, 0))],
        out_specs=[qspec, qspec],
        out_shape=[jax.ShapeDtypeStruct((S, WQ), BF16), jax.ShapeDtypeStruct((S, WQ), BF16)],
        compiler_params=_cparams(),
    )(qkv, qkv, qkv, qkv, qkv, gate, sinks)


def _swa_bwd(qkv, dy, gate, o, sinks, tables, HQ, HKV):
    S = qkv.shape[0]
    G = HQ // HKV
    WQ, KVW = HQ * HEAD_DIM, HKV * HEAD_DIM
    nb = S // SWA_BLOCK
    GW = G * HEAD_DIM
    R = G * SWA_BLOCK
    kb, vb = WQ // LANE, (WQ + KVW) // LANE
    scale = HEAD_DIM ** -0.5
    assert G == 8

    def body(q_ref, kp_ref, kc_ref, vp_ref, vc_ref, dy_ref, g_ref, o_ref, sink_ref, t_ref,
             dqg_ref, dkv_ref, dsink_ref, carry_sc):
        kvh, n = pl.program_id(0), pl.program_id(1)

        @pl.when(n == 0)
        def _():
            carry_sc[...] = jnp.zeros(carry_sc.shape, F32)
            dsink_ref[...] = jnp.zeros(dsink_ref.shape, F32)

        @pl.when(n < nb)
        def _():
            qstack, kk, vv, p, psink, lane = _swa_scores(q_ref, kp_ref, kc_ref, vp_ref, vc_ref, sink_ref, kvh, n, G)
            dyv, g, ov = dy_ref[...], g_ref[...], o_ref[...].astype(F32)
            sg = _sigmoid(g)
            dob = (dyv * (g * sg)).astype(BF16)
            dqg_ref[1] = (dyv * ov * (sg * (1.0 + g * (1.0 - sg)))).astype(BF16)
            prod = dob.astype(F32) * ov
            dparts = []
            for j in range(G // 2):
                tile = prod[:, LANE * j:LANE * (j + 1)]
                dparts += [jnp.sum(jnp.where(lane < HEAD_DIM, tile, 0.0), axis=1, keepdims=True),
                           jnp.sum(jnp.where(lane < HEAD_DIM, 0.0, tile), axis=1, keepdims=True)]
            delta = jnp.concatenate(dparts, axis=0)
            dostack = _swa_stack(dob, lane, G)
            dp = _dot_nt(dostack, vv)
            ds = p * (dp - delta)
            dsb, pb = ds.astype(BF16), p.astype(BF16)
            dq = _swa_unstack(_dot_nn(dsb, kk), lane, G) * scale
            dq = (dq * t_ref[0] + pltpu.roll(dq * t_ref[1], ROT_DIM // 2, axis=1)
                  + pltpu.roll(dq * t_ref[2], GW - ROT_DIM // 2, axis=1))
            dqg_ref[0] = dq.astype(BF16)
            dkk = _dot_tn(dsb, qstack)
            dvv = _dot_tn(pb, dostack)
            dkk = dkk + pltpu.roll(dkk, HEAD_DIM, axis=1)
            dvv = dvv + pltpu.roll(dvv, HEAD_DIM, axis=1)
            lane2 = lax.broadcasted_iota(jnp.int32, (2 * SWA_BLOCK, LANE), 1)
            comb = jnp.where(lane2 < HEAD_DIM, dkk, dvv)
            dkv_ref[0] = carry_sc[...] + comb[:SWA_BLOCK]
            carry_sc[...] = comb[SWA_BLOCK:]
            sk = psink * delta
            rows = [jnp.broadcast_to(-jnp.sum(sk[g_ * SWA_BLOCK:(g_ + 1) * SWA_BLOCK], axis=0, keepdims=True), (1, LANE))
                    for g_ in range(G)]
            dsink_ref[0] += jnp.concatenate(rows, axis=0)

        @pl.when(n == nb)
        def _():
            dkv_ref[0] = carry_sc[...]

    cl = lambda n: jnp.minimum(n, nb - 1)
    blk = lambda cb, prev: pl.BlockSpec(
        (SWA_BLOCK, LANE), lambda h, n, cb=cb, prev=prev: (jnp.maximum(cl(n) - prev, 0), cb + h // 2))
    qspec = pl.BlockSpec((SWA_BLOCK, GW), lambda h, n: (cl(n), h))
    return pl.pallas_call(
        body, name="swa_attn_bwd", grid=(HKV, nb + 1),
        in_specs=[qspec, blk(kb, 1), blk(kb, 0), blk(vb, 1), blk(vb, 0), qspec, qspec, qspec,
                  pl.BlockSpec((1, LANE), lambda h, n: (0, 0)),
                  pl.BlockSpec((3, SWA_BLOCK, GW), lambda h, n: (0, cl(n), 0))],
        out_specs=[pl.BlockSpec((2, SWA_BLOCK, GW), lambda h, n: (0, cl(n), h)),
                   pl.BlockSpec((1, SWA_BLOCK, LANE), lambda h, n: (h, jnp.maximum(n - 1, 0), 0)),
                   pl.BlockSpec((1, 8, LANE), lambda h, n: (h, 0, 0))],
        out_shape=[jax.ShapeDtypeStruct((2, S, WQ), BF16), jax.ShapeDtypeStruct((HKV, S, LANE), F32),
                   jax.ShapeDtypeStruct((HKV, 8, LANE), F32)],
        scratch_shapes=[pltpu.VMEM((SWA_BLOCK, LANE), F32)],
        compiler_params=_cparams(),
    )(qkv, qkv, qkv, qkv, qkv, dy, gate, o, sinks, tables)


def _swa_dkv_finish(dkv, tables):
    HKV, S, _ = dkv.shape
    KVW = HKV * HEAD_DIM
    tm = _row_tile(S, 512)
    npair = HKV // 2

    def body(d_ref, t_ref, o_ref):
        lane = lax.broadcasted_iota(jnp.int32, (tm, LANE), 1)
        lo = lane < HEAD_DIM
        for p in range(npair):
            a, b = d_ref[2 * p], d_ref[2 * p + 1]
            tk = jnp.where(lo, a, pltpu.roll(b, HEAD_DIM, axis=1))
            tv = jnp.where(lo, pltpu.roll(a, HEAD_DIM, axis=1), b)
            tk = (tk * t_ref[0] + pltpu.roll(tk * t_ref[1], ROT_DIM // 2, axis=1)
                  + pltpu.roll(tk * t_ref[2], LANE - ROT_DIM // 2, axis=1))
            o_ref[:, LANE * p:LANE * (p + 1)] = tk.astype(BF16)
            o_ref[:, KVW + LANE * p:KVW + LANE * (p + 1)] = tv.astype(BF16)

    return pl.pallas_call(
        body, name="swa_dkv_finish", grid=(S // tm,),
        in_specs=[pl.BlockSpec((HKV, tm, LANE), lambda i: (0, i, 0)), pl.BlockSpec((3, tm, LANE), lambda i: (0, i, 0))],
        out_specs=pl.BlockSpec((tm, 2 * KVW), lambda i: (i, 0)),
        out_shape=jax.ShapeDtypeStruct((S, 2 * KVW), BF16),
        compiler_params=_cparams(),
    )(dkv, tables)


def _rope_tables(S, width):
    half = ROT_DIM // 2
    pos = jnp.arange(S, dtype=F32)
    inv_freq = ROPE_THETA ** (-jnp.arange(half, dtype=F32) / half)
    ang = pos[:, None] * inv_freq[None, :]
    cos, sin = jnp.cos(ang), jnp.sin(ang)
    one = jnp.ones((S, HEAD_DIM - ROT_DIM), F32)
    zero = jnp.zeros((S, HEAD_DIM - ROT_DIM), F32)
    zh = jnp.zeros((S, half), F32)
    t0 = jnp.concatenate([cos, cos, one], axis=1)
    t1 = jnp.concatenate([-sin, zh, zero], axis=1)
    t2 = jnp.concatenate([zh, sin, zero], axis=1)
    return jnp.stack([jnp.tile(t, (1, width // HEAD_DIM)) for t in (t0, t1, t2)])


def _pad_rows(v, row, total_rows=8):
    return jnp.pad(v, ((row, total_rows - row - v.shape[0]), (0, 0)))


def _pad_lanes(v, off, width):
    return jnp.pad(v, ((0, 0), (off, width - off - v.shape[1])))


def kernel(x, norm_g, fox_w_in, fox_b_f, fox_w_out, swa_w_in, swa_sinks, swa_w_out, final_g, loss_target, m_norm_g, m_fox_w_in, m_fox_b_f, m_fox_w_out, m_swa_w_in, m_swa_sinks, m_swa_w_out, m_final_g, v_norm_g, v_fox_w_in, v_fox_b_f, v_fox_w_out, v_swa_w_in, v_swa_sinks, v_swa_w_out, v_final_g):
    S, D = x.shape[1], x.shape[2]
    H = fox_b_f.shape[1]
    W = H * HEAD_DIM
    wf = fox_w_in.shape[2]
    ws = swa_w_in.shape[2]
    HQ = swa_sinks.shape[1]
    WQ = HQ * HEAD_DIM
    KVW = (ws * N_DEV - 2 * WQ) // 2
    HKV = KVW // HEAD_DIM
    rows_o = fox_w_out.shape[1]
    assert wf * N_DEV == 4 * W + H and rows_o * N_DEV == W and H <= LANE and HQ <= LANE
    me = _my_index()

    _, sw_f, np_f = _slab_geom(wf)
    _, sw_s, np_s = _slab_geom(ws)

    def slab(w2d, w, sw):
        off = (w * me) % LANE
        return lax.dynamic_update_slice(jnp.zeros((w2d.shape[0], sw), BF16), w2d.astype(BF16), (0, off))

    (fi_all,) = _all_gather([slab(fox_w_in[0], wf, sw_f)])
    w_fi = _assemble(fi_all, wf)
    later = [slab(swa_w_in[0], ws, sw_s), fox_w_out[0].astype(BF16), swa_w_out[0].astype(BF16)]

    x0 = x[0]
    g0, g1, gf = norm_g[0:1], norm_g[1:2], final_g[None, :]
    bias = _pad_lanes(fox_b_f, 0, LANE)
    sinks = _pad_lanes(swa_sinks, 0, LANE)
    tab_q = _rope_tables(S, 8 * HEAD_DIM)
    tab_k = tab_q[:, :, :LANE]

    h0 = _rmsnorm_fwd(x0, g0, "rmsnorm0")
    qkv0 = _proj(h0, w_fi, 0, 3 * W, BF16, "fox_in_qkv")
    gate0 = _proj(h0, w_fi, 3 * W, W, F32, "fox_in_gate")
    fl = _proj(h0, w_fi, 4 * W, LANE, F32, "fox_in_f")
    c = _fox_gate_fwd(fl, bias)
    y0, o0, a0, (si_all, fo_all, so_all) = _fox_fwd(qkv0, gate0, c, H, gather=later)
    w_si = _assemble(si_all, ws)
    w_fo = fo_all.reshape(W, D)
    w_so = so_all.reshape(WQ, D)
    x1 = _out_proj_res(y0, w_fo, x0, "fox_out")

    h1 = _rmsnorm_fwd(x1, g1, "rmsnorm1")
    qkv1 = _proj(h1, w_si, 0, WQ + 2 * KVW, BF16, "swa_in_qkv", rope=(tab_k, WQ + KVW))
    gate1 = _proj(h1, w_si, WQ + 2 * KVW, WQ, F32, "swa_in_gate")
    y1, o1 = _swa_fwd(qkv1, gate1, sinks, HQ, HKV)
    x2 = _out_proj_res(y1, w_so, x1, "swa_out")

    dx2, dx2b, dgf, loss_p = _loss_head(x2, loss_target[0], gf)

    dy1 = _matmul_nt([(dx2b, None, 0)], w_so, WQ, "swa_out_bwd")
    g_so, g_so_h = _matmul_tn(y1, [(dx2b, None, 0)], D, "swa_out_wgrad", also_bf16=True)
    dqg1, dkv1, dsink = _swa_bwd(qkv1, dy1, gate1, o1, sinks, tab_q, HQ, HKV)
    dkv1f = _swa_dkv_finish(dkv1, tab_k)
    parts1 = [(dqg1, 0, 0), (dkv1f, None, WQ), (dqg1, 1, WQ + 2 * KVW)]
    g_si, g_si_h = _matmul_tn(h1, parts1, np_s, "swa_in_wgrad", tile_major=True, also_bf16=True)
    dh1 = _matmul_nt(parts1, w_si, D, "swa_in_bwd")
    dx1, dx1b, dg1 = _rmsnorm_bwd(dh1, x1, g1, dx2, "rmsnorm1_bwd")

    dy0 = _matmul_nt([(dx1b, None, 0)], w_fo, W, "fox_out_bwd")
    g_fo, g_fo_h = _matmul_tn(y0, [(dx1b, None, 0)], D, "fox_out_wgrad", also_bf16=True)
    qaug0, doaug0, dgate0 = _fox_bwd_prep(qkv0, dy0, gate0, o0, a0, H)
    early_specs = [("col", ws), ("row", rows_o), ("row", rows_o)]
    dqkv0, dcr, dcc, early_recv = _fox_bwd(qaug0, doaug0, qkv0, c, H, scatter=[g_si_h, g_fo_h, g_so_h],
                                          scatter_specs=early_specs)
    dfl, dbf = _fox_gate_bwd(fl, bias, dcr - dcc)
    parts0 = [(dqkv0, p, p * W) for p in range(3)] + [(dgate0, None, 3 * W), (dfl, None, 4 * W)]
    g_fi = _matmul_tn(h0, parts0, np_f, "fox_in_wgrad", tile_major=True)
    dh0 = _matmul_nt(parts0, w_fi, D, "fox_in_bwd")
    dx0, _, dg0 = _rmsnorm_bwd(dh0, x0, g0, dx1, "rmsnorm0_bwd")

    red_si, gw_fo, gw_so = [_final_sum8(g_, r_, s_)
                            for g_, r_, s_ in zip([g_si, g_fo, g_so], early_recv, early_specs)]
    spec_fi = ("col", wf)
    (recv,) = _reduce_scatter_stage1([g_fi], [spec_fi])
    psum, psum_h = _pair_sum(g_fi, recv, spec_fi)
    (recv2,) = _reduce_scatter_stage2([psum_h])
    red_fi = _final_sum_cols(psum, recv2)
    gw_fi = lax.dynamic_slice(red_fi, (0, (wf * me) % LANE), (D, wf))
    gw_si = lax.dynamic_slice(red_si, (0, (ws * me) % LANE), (D, ws))

    P = D
    dsink_v = dsink[:, :, 0].reshape(1, HQ)
    row3 = _pad_lanes(dbf[:, :H], 0, P) + _pad_lanes(dsink_v, LANE, P) + _pad_lanes(loss_p[:, :1], 2 * LANE, P)
    pack = _pad_rows(dg0, 0) + _pad_rows(dg1, 1) + _pad_rows(dgf, 2) + _pad_rows(row3, 3)
    tot = _all_reduce_small(pack)
    loss = tot[3, 2 * LANE]
    g_norm = tot[0:2]
    g_final = tot[2]
    g_bf = tot[3:4, 0:H]
    g_sinks = tot[3:4, LANE:LANE + HQ]

    def small_pack(ng, fg, bf, sk):
        r3 = _pad_lanes(bf, 0, P) + _pad_lanes(sk, LANE, P)
        return _pad_rows(ng, 0) + _pad_rows(fg[None, :], 2) + _pad_rows(r3, 3)

    sd, sm, sv = _adamw(small_pack(norm_g, final_g, fox_b_f, swa_sinks), tot,
                        small_pack(m_norm_g, m_final_g, m_fox_b_f, m_swa_sinks),
                        small_pack(v_norm_g, v_final_g, v_fox_b_f, v_swa_sinks), "adamw_small")

    def unpack(t):
        return t[0:2], t[3:4, 0:H], t[3:4, LANE:LANE + HQ], t[2]

    d_fi, m_fi, v_fi = _adamw(fox_w_in[0], gw_fi, m_fox_w_in[0], v_fox_w_in[0], "adamw_fox_in")
    d_fo, m_fo, v_fo = _adamw(fox_w_out[0], gw_fo, m_fox_w_out[0], v_fox_w_out[0], "adamw_fox_out")
    d_si, m_si, v_si = _adamw(swa_w_in[0], gw_si, m_swa_w_in[0], v_swa_w_in[0], "adamw_swa_in")
    d_so, m_so, v_so = _adamw(swa_w_out[0], gw_so, m_swa_w_out[0], v_swa_w_out[0], "adamw_swa_out")

    def group(small, fi, fo, si, so):
        ng, bf, sk, fg = unpack(small)
        return (ng, fi[None], bf, fo[None], si[None], sk, so[None], fg)

    grads = (g_norm, gw_fi[None], g_bf, gw_fo[None], gw_si[None], g_sinks, gw_so[None], g_final)
    return (loss, dx0[None], *grads, *group(sd, d_fi, d_fo, d_si, d_so),
            *group(sm, m_fi, m_fo, m_si, m_so), *group(sv, v_fi, v_fo, v_si, v_so))
```

```python
import math

import jax
import jax.numpy as jnp
from jax import lax
from jax.experimental import pallas as pl
from jax.experimental.pallas import tpu as pltpu

F32 = jnp.float32
BF16 = jnp.bfloat16
MESH = pl.DeviceIdType.MESH

N_DEV = 8
LANE = 128
HEAD_DIM = 64
SWA_BLOCK = 128
NEG_INF = -1e30
RMS_EPS = 1e-6
ROPE_THETA = 500000.0
ROT_DIM = HEAD_DIM // 4
ADAM_LR, ADAM_B1, ADAM_B2, ADAM_EPS, ADAM_WD, ADAM_STEP = 0.001, 0.9, 0.999, 1e-08, 0.01, 10
VMEM_LIMIT = 56 * 1024 * 1024
MM_TILE = 1024


def _cparams(**kw):
    return pltpu.CompilerParams(vmem_limit_bytes=VMEM_LIMIT, **kw)


def _tile(n, cap):
    if n <= cap:
        return n
    t = (cap // LANE) * LANE
    while t > LANE and n % t:
        t -= LANE
    assert n % t == 0, (n, cap)
    return t


def _row_tile(n, cap):
    t = min(n, cap)
    while n % t:
        t //= 2
    return t


def _dot_nn(a, b):
    return jnp.dot(a, b, preferred_element_type=F32)


def _dot_nt(a, b):
    return lax.dot_general(a, b, (((1,), (1,)), ((), ())), preferred_element_type=F32)


def _dot_tn(a, b):
    return lax.dot_general(a, b, (((0,), (0,)), ((), ())), preferred_element_type=F32)


def _sigmoid(g):
    return 1.0 / (1.0 + jnp.exp(-g))


def _slab_geom(w):
    starts = [w * i for i in range(N_DEV)]
    aligned = [LANE * (s // LANE) for s in starts]
    offs = [s - a for s, a in zip(starts, aligned)]
    sw = LANE * (-(-(max(offs) + w) // LANE))
    return aligned, sw, aligned[-1] + sw


def _my_index():
    return 4 * lax.axis_index("x") + 2 * lax.axis_index("y") + lax.axis_index("c")


def _all_gather(arrs):
    n = len(arrs)

    def body(*refs):
        ins, outs = refs[:n], refs[n:2 * n]
        send_sems, recv_sems, local_sems = refs[2 * n:]
        x, y, c = lax.axis_index("x"), lax.axis_index("y"), lax.axis_index("c")
        me, sib = (x, y, c), (x, y, 1 - c)
        chips = [(1 - x, y), (x, 1 - y), (1 - x, 1 - y)]

        def idx(px, py, pc):
            return 4 * px + 2 * py + pc

        def copy(a, k, block, to, src=None):
            dst = outs[a].at[idx(*block)]
            return pltpu.make_async_remote_copy(
                src_ref=dst if src is None else src, dst_ref=dst,
                send_sem=send_sems.at[a, k], recv_sem=recv_sems.at[a, k],
                device_id=to, device_id_type=MESH)

        mine = [pltpu.make_async_copy(ins[a], outs[a].at[idx(*me)], local_sems.at[a]) for a in range(n)]
        for m in mine:
            m.start()
        first = []
        for a in range(n):
            first.append(copy(a, 0, me, sib, src=ins[a]))
            for j, chip in enumerate(chips):
                first.append(copy(a, 1 + j, me, (*chip, c), src=ins[a]))
        for cp in first:
            cp.start()
        passed = []
        for j, chip in enumerate(chips):
            for a in range(n):
                copy(a, 1 + j, (*chip, c), me).wait_recv()
                p = copy(a, 4 + j, (*chip, c), sib)
                p.start()
                passed.append(p)
        for a in range(n):
            copy(a, 0, sib, me).wait_recv()
        for j, chip in enumerate(chips):
            for a in range(n):
                copy(a, 4 + j, (*chip, 1 - c), me).wait_recv()
        for cp in first + passed:
            cp.wait_send()
        for m in mine:
            m.wait()

    any_spec = pl.BlockSpec(memory_space=pl.ANY)
    return pl.pallas_call(
        body, name="weights_all_gather",
        out_shape=[jax.ShapeDtypeStruct((N_DEV,) + a.shape, a.dtype) for a in arrs],
        in_specs=[any_spec] * n, out_specs=[any_spec] * n,
        scratch_shapes=[pltpu.SemaphoreType.DMA((n, 7)), pltpu.SemaphoreType.DMA((n, 7)),
                        pltpu.SemaphoreType.DMA((n,))],
    )(*arrs)


def _rs_windows(specs):
    def window(ref, spec, blk):
        kind, n = spec
        if kind == "col":
            _, sw, _ = _slab_geom(n)
            return ref.at[pl.ds((n * blk) // LANE, sw // LANE)]
        start = pl.multiple_of(n * blk, n)
        return ref.at[pl.ds(start, n), :]
    return window


def _peer(k):
    x, y, c = lax.axis_index("x"), lax.axis_index("y"), lax.axis_index("c")
    return (x ^ (k >> 2), y ^ ((k >> 1) & 1), c ^ (k & 1))


def _direct_gather_copies(ins, outs, send_sems, recv_sems, local_sems):
    me = _my_index()
    remote, local = [], []
    for a, (src, dst) in enumerate(zip(ins, outs)):
        local.append(pltpu.make_async_copy(src, dst.at[me], local_sems.at[a]))
        for k in range(1, N_DEV):
            remote.append(pltpu.make_async_remote_copy(
                src_ref=src, dst_ref=dst.at[me], send_sem=send_sems.at[a, k - 1], recv_sem=recv_sems.at[a, k - 1],
                device_id=_peer(k), device_id_type=MESH))
    return remote, local


def _direct_scatter_copies(ins, outs, specs, send_sems, recv_sems):
    window = _rs_windows(specs)
    remote = []
    for a, (src, dst) in enumerate(zip(ins, outs)):
        for k in range(1, N_DEV):
            px, py, pc = _peer(k)
            remote.append(pltpu.make_async_remote_copy(
                src_ref=window(src, specs[a], 4 * px + 2 * py + pc), dst_ref=dst.at[k - 1],
                send_sem=send_sems.at[a, k - 1], recv_sem=recv_sems.at[a, k - 1],
                device_id=(px, py, pc), device_id_type=MESH))
    return remote


def _scatter_block_shape(g, spec):
    kind, w = spec
    return (_slab_geom(w)[1] // LANE, g.shape[1], LANE) if kind == "col" else (w, g.shape[1])


def _wait_all(remote, local=()):
    for cp in remote:
        cp.wait_recv()
    for cp in remote:
        cp.wait_send()
    for cp in local:
        cp.wait()


def _final_sum8(g, recv, spec):
    kind, n = spec
    me = _my_index()
    offs = jnp.stack([(n * me) // LANE if kind == "col" else me]).astype(jnp.int32)
    if kind == "col":
        _, T, M, _ = recv.shape
        grid = (T,)
        in_specs = [pl.BlockSpec((1, M, LANE), lambda t, o: (o[0] + t, 0, 0)),
                    pl.BlockSpec((N_DEV - 1, 1, M, LANE), lambda t, o: (0, t, 0, 0))]
        out_spec = pl.BlockSpec((M, LANE), lambda t, o: (0, t))
        out_shape = jax.ShapeDtypeStruct((M, T * LANE), F32)
    else:
        _, nrow, C = recv.shape
        grid = (1,)
        in_specs = [pl.BlockSpec((nrow, C), lambda t, o: (o[0], 0)),
                    pl.BlockSpec((N_DEV - 1, nrow, C), lambda t, o: (0, 0, 0))]
        out_spec = pl.BlockSpec((nrow, C), lambda t, o: (0, 0))
        out_shape = jax.ShapeDtypeStruct((nrow, C), F32)

    def body(o_ref, g_ref, r_ref, out_ref):
        acc = g_ref[0] if kind == "col" else g_ref[...]
        for k in range(N_DEV - 1):
            acc = acc + (r_ref[k, 0] if kind == "col" else r_ref[k]).astype(F32)
        out_ref[...] = acc

    return pl.pallas_call(
        body, name="grads_final_sum8",
        grid_spec=pltpu.PrefetchScalarGridSpec(num_scalar_prefetch=1, grid=grid, in_specs=in_specs,
                                               out_specs=out_spec),
        out_shape=out_shape, compiler_params=_cparams(),
    )(offs, g, recv)


def _all_reduce_small(pack):
    R, P = pack.shape

    def body(x_ref, o_ref, gat_ref, send_sems, recv_sems):
        x, y, c = lax.axis_index("x"), lax.axis_index("y"), lax.axis_index("c")
        me = 4 * x + 2 * y + c
        gat_ref[me] = x_ref[...]
        copies = []
        for k in range(1, N_DEV):
            peer = (x ^ (k >> 2), y ^ ((k >> 1) & 1), c ^ (k & 1))
            copies.append(pltpu.make_async_remote_copy(
                src_ref=x_ref, dst_ref=gat_ref.at[me],
                send_sem=send_sems.at[k - 1], recv_sem=recv_sems.at[k - 1],
                device_id=peer, device_id_type=MESH))
        for cp in copies:
            cp.start()
        for cp in copies:
            cp.wait_recv()
        for cp in copies:
            cp.wait_send()
        acc = gat_ref[0]
        for d in range(1, N_DEV):
            acc = acc + gat_ref[d]
        o_ref[...] = acc

    vm = pl.BlockSpec(memory_space=pltpu.VMEM)
    return pl.pallas_call(
        body, name="small_all_reduce",
        out_shape=jax.ShapeDtypeStruct((R, P), F32),
        in_specs=[vm], out_specs=vm,
        scratch_shapes=[pltpu.VMEM((N_DEV, R, P), F32),
                        pltpu.SemaphoreType.DMA((N_DEV - 1,)), pltpu.SemaphoreType.DMA((N_DEV - 1,))],
    )(pack)


def _assemble(slabs, w):
    aligned, sw, total = _slab_geom(w)
    K = slabs.shape[1]
    tr = _row_tile(K, 256)

    def body(s_ref, o_ref):
        o_ref[...] = jnp.zeros(o_ref.shape, BF16)
        for i in range(N_DEV):
            a = aligned[i]
            o_ref[:, a:a + sw] = o_ref[:, a:a + sw] + s_ref[i]

    return pl.pallas_call(
        body, name="assemble_w_in", grid=(K // tr,),
        in_specs=[pl.BlockSpec((N_DEV, tr, sw), lambda i: (0, i, 0))],
        out_specs=pl.BlockSpec((tr, total), lambda i: (i, 0)),
        out_shape=jax.ShapeDtypeStruct((K, total), BF16),
        compiler_params=_cparams(),
    )(slabs)


def _rmsnorm_fwd(x, g, name):
    S, D = x.shape
    tm = _row_tile(S, 256)

    def body(x_ref, g_ref, h_ref):
        xv = x_ref[...]
        r = lax.rsqrt(jnp.mean(xv * xv, axis=-1, keepdims=True) + RMS_EPS)
        h_ref[...] = ((xv * r) * g_ref[...]).astype(BF16)

    return pl.pallas_call(
        body, name=name, grid=(S // tm,),
        in_specs=[pl.BlockSpec((tm, D), lambda i: (i, 0)), pl.BlockSpec((1, D), lambda i: (0, 0))],
        out_specs=pl.BlockSpec((tm, D), lambda i: (i, 0)),
        out_shape=jax.ShapeDtypeStruct((S, D), BF16),
        compiler_params=_cparams(),
    )(x, g)


def _rmsnorm_bwd(dh, x, g, dres, name):
    S, D = x.shape
    tm = _row_tile(S, 256)

    def body(dh_ref, x_ref, g_ref, dr_ref, dx_ref, dxb_ref, dg_ref):
        xv = x_ref[...]
        r = lax.rsqrt(jnp.mean(xv * xv, axis=-1, keepdims=True) + RMS_EPS)
        xhat = xv * r
        d = dh_ref[...]
        gd = d * g_ref[...]
        dx = r * (gd - xhat * jnp.mean(gd * xhat, axis=-1, keepdims=True)) + dr_ref[...]
        dx_ref[...] = dx
        dxb_ref[...] = dx.astype(BF16)

        @pl.when(pl.program_id(0) == 0)
        def _():
            dg_ref[...] = jnp.zeros(dg_ref.shape, F32)
        dg_ref[...] += jnp.sum(d * xhat, axis=0, keepdims=True)

    row = pl.BlockSpec((tm, D), lambda i: (i, 0))
    vec = pl.BlockSpec((1, D), lambda i: (0, 0))
    return pl.pallas_call(
        body, name=name, grid=(S // tm,),
        in_specs=[row, row, vec, row], out_specs=[row, row, vec],
        out_shape=[jax.ShapeDtypeStruct((S, D), F32), jax.ShapeDtypeStruct((S, D), BF16),
                   jax.ShapeDtypeStruct((1, D), F32)],
        compiler_params=_cparams(),
    )(dh, x, g, dres)


def _loss_head(x, tgt, g):
    S, D = x.shape
    tm = _row_tile(S, 256)

    def body(x_ref, t_ref, g_ref, dx_ref, dxb_ref, dg_ref, loss_ref):
        xv = x_ref[...]
        r = lax.rsqrt(jnp.mean(xv * xv, axis=-1, keepdims=True) + RMS_EPS)
        xhat = xv * r
        gv = g_ref[...]
        err = xhat * gv - t_ref[...]
        d = err * (1.0 / D)
        gd = d * gv
        dx = r * (gd - xhat * jnp.mean(gd * xhat, axis=-1, keepdims=True))
        dx_ref[...] = dx
        dxb_ref[...] = dx.astype(BF16)

        @pl.when(pl.program_id(0) == 0)
        def _():
            dg_ref[...] = jnp.zeros(dg_ref.shape, F32)
            loss_ref[...] = jnp.zeros(loss_ref.shape, F32)
        dg_ref[...] += jnp.sum(d * xhat, axis=0, keepdims=True)
        per_tok = jnp.sum(err * err, axis=-1, keepdims=True) * (1.0 / D)
        loss_ref[...] += 0.5 * jnp.sum(per_tok, axis=0, keepdims=True)

    row = pl.BlockSpec((tm, D), lambda i: (i, 0))
    vec = pl.BlockSpec((1, D), lambda i: (0, 0))
    return pl.pallas_call(
        body, name="loss_head", grid=(S // tm,),
        in_specs=[row, row, vec],
        out_specs=[row, row, vec, pl.BlockSpec((1, LANE), lambda i: (0, 0))],
        out_shape=[jax.ShapeDtypeStruct((S, D), F32), jax.ShapeDtypeStruct((S, D), BF16),
                   jax.ShapeDtypeStruct((1, D), F32), jax.ShapeDtypeStruct((1, LANE), F32)],
        compiler_params=_cparams(),
    )(x, tgt, g)


def _adamw(w, g, m, v, name):
    R, C = w.shape
    tr = _row_tile(R, 256)
    c1 = 1.0 - ADAM_B1 ** ADAM_STEP
    c2 = 1.0 - ADAM_B2 ** ADAM_STEP

    def body(w_ref, g_ref, m_ref, v_ref, d_ref, nm_ref, nv_ref):
        gv = g_ref[...]
        nm = ADAM_B1 * m_ref[...] + (1.0 - ADAM_B1) * gv
        nv = ADAM_B2 * v_ref[...] + (1.0 - ADAM_B2) * (gv * gv)
        d_ref[...] = -ADAM_LR * ((nm / c1) / (jnp.sqrt(nv / c2) + ADAM_EPS) + ADAM_WD * w_ref[...])
        nm_ref[...] = nm
        nv_ref[...] = nv

    spec = pl.BlockSpec((tr, C), lambda i: (i, 0))
    return pl.pallas_call(
        body, name=name, grid=(R // tr,),
        in_specs=[spec] * 4, out_specs=[spec] * 3,
        out_shape=[jax.ShapeDtypeStruct((R, C), F32)] * 3,
        compiler_params=_cparams(),
    )(w, g, m, v)


def _proj(h, wfull, col0, ncols, out_dtype, name, rope=None):
    S, K = h.shape
    tm = _row_tile(S, MM_TILE)
    tn = math.gcd(_tile(ncols, MM_TILE), col0) if col0 else _tile(ncols, MM_TILE)
    if rope is not None:
        tn = _tile(math.gcd(ncols, rope[1]), MM_TILE)
    assert ncols % tn == 0 and col0 % tn == 0
    cb = col0 // tn

    def body(*refs):
        if rope is None:
            a_ref, b_ref, o_ref = refs
        else:
            a_ref, b_ref, t_ref, o_ref = refs
        acc = _dot_nn(a_ref[...], b_ref[...])
        if rope is not None:
            t0, t1, t2 = (jnp.tile(t_ref[i], (1, tn // LANE)) for i in range(3))
            roped = (acc * t0 + pltpu.roll(acc, tn - ROT_DIM // 2, axis=1) * t1
                     + pltpu.roll(acc, ROT_DIM // 2, axis=1) * t2)
            acc = jnp.where(pl.program_id(1) < rope[1] // tn, roped, acc)
        o_ref[...] = acc.astype(out_dtype)

    in_specs = [pl.BlockSpec((tm, K), lambda i, j: (i, 0)), pl.BlockSpec((K, tn), lambda i, j: (0, cb + j))]
    args = [h, wfull]
    if rope is not None:
        in_specs.append(pl.BlockSpec((3, tm, LANE), lambda i, j: (0, i, 0)))
        args.append(rope[0])
    return pl.pallas_call(
        body, name=name, grid=(S // tm, ncols // tn),
        in_specs=in_specs, out_specs=pl.BlockSpec((tm, tn), lambda i, j: (i, j)),
        out_shape=jax.ShapeDtypeStruct((S, ncols), out_dtype),
        compiler_params=_cparams(),
    )(*args)


def _out_proj_res(y, wo, xres, name):
    S, W = y.shape
    D = wo.shape[1]
    tm, tn = _row_tile(S, MM_TILE), _tile(D, MM_TILE)

    def body(a_ref, b_ref, r_ref, o_ref):
        o_ref[...] = r_ref[...] + _dot_nn(a_ref[...], b_ref[...])

    return pl.pallas_call(
        body, name=name, grid=(S // tm, D // tn),
        in_specs=[pl.BlockSpec((tm, W), lambda i, j: (i, 0)), pl.BlockSpec((W, tn), lambda i, j: (0, j)),
                  pl.BlockSpec((tm, tn), lambda i, j: (i, j))],
        out_specs=pl.BlockSpec((tm, tn), lambda i, j: (i, j)),
        out_shape=jax.ShapeDtypeStruct((S, D), F32),
        compiler_params=_cparams(),
    )(y, wo, xres)


def _matmul_nt(parts, wfull, out_rows, name, scatter=(), scatter_specs=()):
    na = len(scatter)
    S = parts[0][0].shape[-2]
    tm, tn = _row_tile(S, MM_TILE), _tile(out_rows, MM_TILE)
    plan, lo = [], 0
    for arr, lead, col0 in parts:
        n_p = arr.shape[-1]
        tk = math.gcd(_tile(n_p, 1024), col0) if col0 else _tile(n_p, 1024)
        steps = n_p // tk
        plan.append((lead, col0 // tk, tk, lo, lo + steps))
        lo += steps
    nk = lo
    npart = len(parts)

    def body(*refs):
        a_refs, w_refs = refs[:npart], refs[npart:2 * npart]
        nin = 2 * npart + na
        o_ref, acc_ref = refs[nin], refs[nin + 1 + na]
        i, j, k = pl.program_id(0), pl.program_id(1), pl.program_id(2)
        if na:
            remote = _direct_scatter_copies(refs[2 * npart:nin], refs[nin + 1:nin + 1 + na], scatter_specs,
                                            *refs[nin + 2 + na:])

            @pl.when((i == 0) & (j == 0) & (k == 0))
            def _():
                for cp in remote:
                    cp.start()

        @pl.when(k == 0)
        def _():
            acc_ref[...] = jnp.zeros(acc_ref.shape, F32)
        for p, (_, _, _, lo_p, hi_p) in enumerate(plan):
            @pl.when((k >= lo_p) & (k < hi_p))
            def _(p=p):
                acc_ref[...] += _dot_nt(a_refs[p][...], w_refs[p][...])

        @pl.when(k == nk - 1)
        def _():
            o_ref[...] = acc_ref[...]

        if na:
            @pl.when((i == S // tm - 1) & (j == out_rows // tn - 1) & (k == nk - 1))
            def _():
                _wait_all(remote)

    in_specs, args = [], []
    for (arr, lead, col0), (_, cb, tk, lo_p, hi_p) in zip(parts, plan):
        def kk(k, lo_p=lo_p, hi_p=hi_p):
            return jnp.clip(k - lo_p, 0, hi_p - lo_p - 1)
        if lead is None:
            in_specs.append(pl.BlockSpec((tm, tk), lambda i, j, k, kk=kk: (i, kk(k))))
        else:
            in_specs.append(pl.BlockSpec((None, tm, tk), lambda i, j, k, kk=kk, lead=lead: (lead, i, kk(k))))
        args.append(arr)
    for (_, cb, tk, lo_p, hi_p) in plan:
        def kk(k, lo_p=lo_p, hi_p=hi_p):
            return jnp.clip(k - lo_p, 0, hi_p - lo_p - 1)
        in_specs.append(pl.BlockSpec((tn, tk), lambda i, j, k, kk=kk, cb=cb: (j, cb + kk(k))))
        args.append(wfull)
    any_spec = pl.BlockSpec(memory_space=pl.ANY)
    sems = [pltpu.SemaphoreType.DMA((na, N_DEV - 1)), pltpu.SemaphoreType.DMA((na, N_DEV - 1))] if na else []
    outs = pl.pallas_call(
        body, name=name, grid=(S // tm, out_rows // tn, nk),
        in_specs=in_specs + [any_spec] * na,
        out_specs=[pl.BlockSpec((tm, tn), lambda i, j, k: (i, j))] + [any_spec] * na,
        out_shape=[jax.ShapeDtypeStruct((S, out_rows), F32)]
        + [jax.ShapeDtypeStruct((N_DEV - 1,) + _scatter_block_shape(g, s), g.dtype)
           for g, s in zip(scatter, scatter_specs)],
        scratch_shapes=[pltpu.VMEM((tm, tn), F32)] + sems,
        compiler_params=_cparams(),
    )(*args, *scatter)
    return (outs[0], list(outs[1:])) if na else outs[0]


def _matmul_tn(a, parts, total, name, tile_major=False, also_bf16=False):
    S, M = a.shape
    tm, ts = _tile(M, MM_TILE), _row_tile(S, MM_TILE)
    nout = 2 if also_bf16 else 1
    outs = None
    for idx, (arr, lead, col0) in enumerate(parts):
        n_p = arr.shape[-1]
        tn = math.gcd(_tile(n_p, MM_TILE), col0) if col0 else _tile(n_p, MM_TILE)
        cb = col0 // tn
        nk = S // ts

        def body(*refs, nk=nk, tn=tn):
            a_ref, b_ref = refs[0], refs[1]
            o_refs, acc_ref = refs[-1 - nout:-1], refs[-1]
            k = pl.program_id(2)

            @pl.when(k == 0)
            def _():
                acc_ref[...] = jnp.zeros(acc_ref.shape, F32)
            acc_ref[...] += _dot_tn(a_ref[...], b_ref[...])

            @pl.when(k == nk - 1)
            def _():
                for o_ref in o_refs:
                    if tile_major:
                        for t in range(tn // LANE):
                            o_ref[t] = acc_ref[:, LANE * t:LANE * (t + 1)].astype(o_ref.dtype)
                    else:
                        o_ref[...] = acc_ref[...].astype(o_ref.dtype)

        in_specs = [pl.BlockSpec((ts, tm), lambda i, j, k: (k, i))]
        if lead is None:
            in_specs.append(pl.BlockSpec((ts, tn), lambda i, j, k: (k, j)))
        else:
            in_specs.append(pl.BlockSpec((None, ts, tn), lambda i, j, k, lead=lead: (lead, k, j)))
        args = [a, arr]
        aliases = {}
        if outs is not None:
            in_specs += [pl.BlockSpec(memory_space=pl.ANY)] * nout
            args += list(outs)
            aliases = {2 + o: o for o in range(nout)}
        if tile_major:
            out_spec = pl.BlockSpec((tn // LANE, tm, LANE), lambda i, j, k, cb=cb: (cb + j, i, 0))
            shape = (total // LANE, M, LANE)
        else:
            out_spec = pl.BlockSpec((tm, tn), lambda i, j, k, cb=cb: (i, cb + j))
            shape = (M, total)
        outs = pl.pallas_call(
            body, name=f"{name}_{idx}", grid=(M // tm, n_p // tn, nk),
            in_specs=in_specs, out_specs=[out_spec] * nout,
            out_shape=[jax.ShapeDtypeStruct(shape, dt) for dt in (F32, BF16)[:nout]],
            scratch_shapes=[pltpu.VMEM((tm, tn), F32)],
            input_output_aliases=aliases,
            compiler_params=_cparams(),
        )(*args)
    return tuple(outs) if also_bf16 else outs[0]


def _log_sigmoid(z):
    e = jnp.exp(-jnp.abs(z))
    return jnp.minimum(z, 0.0) - jnp.where(e < 1e-4, e * (1.0 - 0.5 * e), jnp.log(1.0 + e))


def _fox_gate_fwd(fl, bias):
    S = fl.shape[0]

    def body(f_ref, b_ref, c_ref):
        row = lax.broadcasted_iota(jnp.int32, (8, LANE), 0)

        def step(i, carry):
            r0 = pl.multiple_of(i * 8, 8)
            t = _log_sigmoid(f_ref[pl.ds(r0, 8), :] + b_ref[...])
            for sh in (1, 2, 4):
                t = t + jnp.where(row >= sh, pltpu.roll(t, sh, axis=0), 0.0)
            t = t + carry
            c_ref[pl.ds(r0, 8), :] = t
            return jnp.sum(jnp.where(row == 7, t, 0.0), axis=0, keepdims=True)

        lax.fori_loop(0, S // 8, step, jnp.zeros((1, LANE), F32))

    vm = pl.BlockSpec(memory_space=pltpu.VMEM)
    return pl.pallas_call(
        body, name="fox_gate_fwd", in_specs=[vm, vm], out_specs=vm,
        out_shape=jax.ShapeDtypeStruct((S, LANE), F32),
        compiler_params=_cparams(),
    )(fl, bias)


def _fox_gate_bwd(fl, bias, dc):
    S = fl.shape[0]

    def body(f_ref, b_ref, d_ref, o_ref, db_ref, acc_ref):
        row = lax.broadcasted_iota(jnp.int32, (8, LANE), 0)
        nt = S // 8

        def step(ii, carry):
            carry_c, carry_b = carry
            r0 = pl.multiple_of((nt - 1 - ii) * 8, 8)
            t = d_ref[pl.ds(r0, 8), :]
            for sh in (1, 2, 4):
                t = t + jnp.where(row < 8 - sh, pltpu.roll(t, 8 - sh, axis=0), 0.0)
            t = t + carry_c
            z = f_ref[pl.ds(r0, 8), :] + b_ref[...]
            dz = t * _sigmoid(-z)
            acc_ref[pl.ds(r0, 8), :] = dz
            first = jnp.sum(jnp.where(row == 0, t, 0.0), axis=0, keepdims=True)
            return first, carry_b + jnp.sum(dz, axis=0, keepdims=True)

        zero = jnp.zeros((1, LANE), F32)
        _, db = lax.fori_loop(0, nt, step, (zero, zero))
        db_ref[...] = db
        o_ref[...] = acc_ref[...].astype(BF16)

    vm = pl.BlockSpec(memory_space=pltpu.VMEM)
    return pl.pallas_call(
        body, name="fox_gate_bwd", in_specs=[vm, vm, vm], out_specs=[vm, vm],
        out_shape=[jax.ShapeDtypeStruct((S, LANE), BF16), jax.ShapeDtypeStruct((1, LANE), F32)],
        scratch_shapes=[pltpu.VMEM((S, LANE), F32)],
        compiler_params=_cparams(),
    )(fl, bias, dc)


def _bias_lanes(col, lane, e, first):
    o0 = HEAD_DIM * (1 - e)
    hi = col.astype(BF16)
    r1 = col - hi.astype(F32)
    mid = r1.astype(BF16)
    lo = (r1 - mid.astype(F32)).astype(BF16)
    d0 = o0 if first else o0 + 3
    t = jnp.where((lane >= o0) & (lane < o0 + 6), jnp.ones(lane.shape, BF16), jnp.zeros(lane.shape, BF16))
    t = jnp.where(lane == d0, hi, t)
    t = jnp.where(lane == d0 + 1, mid, t)
    return jnp.where(lane == d0 + 2, lo, t)


def _fox_fwd(qkv, gate, c, H, gather=()):
    na = len(gather)
    S = qkv.shape[0]
    W = H * HEAD_DIM
    HP = H // 2
    tq = _row_tile(S, 512)
    nq = S // tq
    wb = W // LANE
    scale = HEAD_DIM ** -0.5

    def body(*refs):
        q_ref, k_ref, v_ref, g_ref, c_ref = refs[:5]
        y_ref, o_ref, a_ref = refs[5 + na:8 + na]
        kaug_sc, vaug_sc, qaug_sc, s_sc, mb_sc, m_sc, acc_sc = refs[8 + 2 * na:15 + 2 * na]
        hp, qi = pl.program_id(0), pl.program_id(1)
        if na:
            remote, local = _direct_gather_copies(refs[5:5 + na], refs[8 + na:8 + 2 * na], *refs[15 + 2 * na:])

            @pl.when((hp == 0) & (qi == 0))
            def _():
                for cp in remote + local:
                    cp.start()
        lane = lax.broadcasted_iota(jnp.int32, (tq, LANE), 1)
        own = [lane < HEAD_DIM, lane >= HEAD_DIM]
        rows = lax.broadcasted_iota(jnp.int32, (tq, tq), 0)
        cols = lax.broadcasted_iota(jnp.int32, (tq, tq), 1)

        def bias_lanes(col, e, first):
            return _bias_lanes(col, lane, e, first)

        def head_col(tile, e):
            return jnp.sum(jnp.where(lane == 2 * hp + e, tile, 0.0), axis=1, keepdims=True)

        @pl.when(qi == 0)
        def _():
            def chunk(i, carry):
                r0 = pl.multiple_of(i * tq, tq)
                kb, vb, cb = k_ref[pl.ds(r0, tq), :], v_ref[pl.ds(r0, tq), :], c_ref[pl.ds(r0, tq), :]
                for e in range(2):
                    kaug_sc[e, pl.ds(r0, tq), :] = jnp.where(own[e], kb, bias_lanes(-head_col(cb, e), e, False))
                    vaug_sc[e, pl.ds(r0, tq), :] = jnp.where(own[e], vb, jnp.ones((tq, LANE), BF16))
                return carry
            lax.fori_loop(0, nq, chunk, 0)

        q = q_ref[...] * jnp.asarray(scale, BF16)
        crow = c_ref[pl.ds(pl.multiple_of(qi * tq, tq), tq), :]
        ctq = [head_col(crow, e) for e in range(2)]
        for e in range(2):
            qaug_sc[e] = jnp.where(own[e], q, bias_lanes(ctq[e], e, True))
        m_sc[...] = jnp.full(m_sc.shape, NEG_INF, F32)
        acc_sc[...] = jnp.zeros(acc_sc.shape, F32)

        def scores(blk, slot, masked):
            k0 = pl.multiple_of(blk * tq, tq)
            for e in range(2):
                s = _dot_nt(qaug_sc[e], kaug_sc[e, pl.ds(k0, tq), :])
                if masked:
                    s = jnp.where(rows >= cols, s, NEG_INF)
                s_sc[slot, e] = s
                mb_sc[slot, e] = jnp.broadcast_to(jnp.max(s, axis=1, keepdims=True), (tq, LANE))

        def accumulate(blk, slot):
            k0 = pl.multiple_of(blk * tq, tq)
            for e in range(2):
                m_prev = m_sc[e]
                m_new = jnp.maximum(m_prev, mb_sc[slot, e])
                p = jnp.exp(s_sc[slot, e] - jnp.tile(m_new, (1, tq // LANE)))
                acc_sc[e] = jnp.exp(m_prev - m_new) * acc_sc[e] + _dot_nn(p.astype(BF16), vaug_sc[e, pl.ds(k0, tq), :])
                m_sc[e] = m_new

        def block_of(t):
            return jnp.where(t == 0, qi, t - 1)

        scores(qi, 0, True)

        def loop_body(t, carry):
            scores(t, (t + 1) % 2, False)
            accumulate(block_of(t), t % 2)
            return carry

        lax.fori_loop(0, qi, loop_body, 0)
        accumulate(block_of(qi), qi % 2)
        o_e, a_e = [], []
        for e in range(2):
            acc = acc_sc[e]
            l = pltpu.roll(acc, HEAD_DIM, axis=1)
            o_e.append(acc / l)
            a_e.append(ctq[e] - (m_sc[e] + jnp.log(l)))
        o = jnp.where(own[0], o_e[0], o_e[1])
        g = g_ref[...]
        y_ref[...] = (o * (g * _sigmoid(g))).astype(BF16)
        o_ref[...] = o.astype(BF16)
        a_ref[0] = jnp.where(own[0], a_e[0], a_e[1])
        if na:
            @pl.when((hp == HP - 1) & (qi == nq - 1))
            def _():
                _wait_all(remote, local)

    any_spec = pl.BlockSpec(memory_space=pl.ANY)
    sems = [pltpu.SemaphoreType.DMA((na, N_DEV - 1)), pltpu.SemaphoreType.DMA((na, N_DEV - 1)),
            pltpu.SemaphoreType.DMA((na,))] if na else []
    outs = pl.pallas_call(
        body, name="fox_attn_fwd", grid=(HP, nq),
        in_specs=[pl.BlockSpec((tq, LANE), lambda h, i: (i, h)),
                  pl.BlockSpec((S, LANE), lambda h, i: (0, wb + h)),
                  pl.BlockSpec((S, LANE), lambda h, i: (0, 2 * wb + h)),
                  pl.BlockSpec((tq, LANE), lambda h, i: (i, h)),
                  pl.BlockSpec((S, LANE), lambda h, i: (0, 0))] + [any_spec] * na,
        out_specs=[pl.BlockSpec((tq, LANE), lambda h, i: (i, h)),
                   pl.BlockSpec((tq, LANE), lambda h, i: (i, h)),
                   pl.BlockSpec((1, tq, LANE), lambda h, i: (h, i, 0))] + [any_spec] * na,
        out_shape=[jax.ShapeDtypeStruct((S, W), BF16), jax.ShapeDtypeStruct((S, W), BF16),
                   jax.ShapeDtypeStruct((HP, S, LANE), F32)]
        + [jax.ShapeDtypeStruct((N_DEV,) + g.shape, g.dtype) for g in gather],
        scratch_shapes=[pltpu.VMEM((2, S, LANE), BF16), pltpu.VMEM((2, S, LANE), BF16),
                        pltpu.VMEM((2, tq, LANE), BF16), pltpu.VMEM((2, 2, tq, tq), F32),
                        pltpu.VMEM((2, 2, tq, LANE), F32), pltpu.VMEM((2, tq, LANE), F32),
                        pltpu.VMEM((2, tq, LANE), F32)] + sems,
        compiler_params=_cparams(),
    )(qkv, qkv, qkv, gate, c, *gather)
    return outs[0], outs[1], outs[2], list(outs[3:])


def _fox_bwd_prep(qkv, dy, gate, o, a, H):
    S = qkv.shape[0]
    W = H * HEAD_DIM
    HP = H // 2
    tq = _row_tile(S, 512)
    scale = HEAD_DIM ** -0.5

    def body(q_ref, dy_ref, g_ref, o_ref, a_ref, qa_ref, da_ref, dg_ref):
        lane = lax.broadcasted_iota(jnp.int32, (tq, LANE), 1)
        own = [lane < HEAD_DIM, lane >= HEAD_DIM]
        q = q_ref[...] * jnp.asarray(scale, BF16)
        dyv, g, ov, at = dy_ref[...], g_ref[...], o_ref[...].astype(F32), a_ref[0]
        sg = _sigmoid(g)
        dob = (dyv * (g * sg)).astype(BF16)
        dg_ref[...] = (dyv * ov * (sg * (1.0 + g * (1.0 - sg)))).astype(BF16)
        prod = dob.astype(F32) * ov
        for e in range(2):
            a_col = jnp.max(jnp.where(own[e], at, -jnp.inf), axis=1, keepdims=True)
            d_col = jnp.sum(jnp.where(own[e], prod, 0.0), axis=1, keepdims=True)
            qa_ref[e] = jnp.where(own[e], q, _bias_lanes(a_col, lane, e, True))
            da_ref[e] = jnp.where(own[e], dob, _bias_lanes(-d_col, lane, e, True))

    blk = pl.BlockSpec((tq, LANE), lambda h, i: (i, h))
    pair = pl.BlockSpec((2, tq, LANE), lambda h, i: (0, i, h))
    return pl.pallas_call(
        body, name="fox_attn_bwd_prep", grid=(HP, S // tq),
        in_specs=[blk, blk, blk, blk, pl.BlockSpec((1, tq, LANE), lambda h, i: (h, i, 0))],
        out_specs=[pair, pair, blk],
        out_shape=[jax.ShapeDtypeStruct((2, S, W), BF16), jax.ShapeDtypeStruct((2, S, W), BF16),
                   jax.ShapeDtypeStruct((S, W), BF16)],
        compiler_params=_cparams(),
    )(qkv, dy, gate, o, a)


def _fox_bwd(qaug, doaug, qkv, c, H, scatter=(), scatter_specs=()):
    na = len(scatter)
    S = qkv.shape[0]
    W = H * HEAD_DIM
    HP = H // 2
    tq = _row_tile(S, 512)
    nq = S // tq
    wb = W // LANE
    scale = HEAD_DIM ** -0.5

    def body(*refs):
        qa_ref, da_ref, k_ref, v_ref, c_ref = refs[:5]
        out_ref, dcr_ref, dcc_ref = refs[5 + na:8 + na]
        dq_sc, dk_sc, dv_sc = refs[8 + 2 * na:11 + 2 * na]
        hp, kj = pl.program_id(0), pl.program_id(1)
        if na:
            remote = _direct_scatter_copies(refs[5:5 + na], refs[8 + na:8 + 2 * na], scatter_specs,
                                            *refs[11 + 2 * na:])

            @pl.when((hp == 0) & (kj == 0))
            def _():
                for cp in remote:
                    cp.start()
        lane = lax.broadcasted_iota(jnp.int32, (tq, LANE), 1)
        own = [lane < HEAD_DIM, lane >= HEAD_DIM]
        rows = lax.broadcasted_iota(jnp.int32, (tq, tq), 0)
        cols = lax.broadcasted_iota(jnp.int32, (tq, tq), 1)

        @pl.when(kj == 0)
        def _():
            dq_sc[...] = jnp.zeros(dq_sc.shape, F32)

        @pl.when((kj == 0) & (hp == 0))
        def _():
            dcr_ref[...] = jnp.zeros(dcr_ref.shape, F32)
            dcc_ref[...] = jnp.zeros(dcc_ref.shape, F32)

        kblk, vblk, cblk = k_ref[...], v_ref[...], c_ref[...]
        one, zero = jnp.ones((tq, LANE), BF16), jnp.zeros((tq, LANE), BF16)
        ka, va = [], []
        for e in range(2):
            o0 = HEAD_DIM * (1 - e)
            c_col = jnp.sum(jnp.where(lane == 2 * hp + e, cblk, 0.0), axis=1, keepdims=True)
            ka.append(jnp.where(own[e], kblk, _bias_lanes(-c_col, lane, e, False)))
            va.append(jnp.where(own[e], vblk, jnp.where((lane >= o0) & (lane < o0 + 3), one, zero)))
        dk_sc[...] = jnp.zeros(dk_sc.shape, F32)
        dv_sc[...] = jnp.zeros(dv_sc.shape, F32)

        def step(i, masked):
            r0 = pl.multiple_of(i * tq, tq)
            for e in range(2):
                qa = qa_ref[e, pl.ds(r0, tq), :]
                da = da_ref[e, pl.ds(r0, tq), :]
                p = jnp.exp(_dot_nt(qa, ka[e]))
                if masked:
                    p = jnp.where(rows >= cols, p, 0.0)
                ds = p * _dot_nt(da, va[e])
                pb, dsb = p.astype(BF16), ds.astype(BF16)
                dv_sc[e] += _dot_tn(pb, da)
                dk_sc[e] += _dot_tn(dsb, qa)
                dq_sc[e, pl.ds(r0, tq), :] += _dot_nn(dsb, ka[e])

        step(kj, True)

        def loop_body(i, carry):
            step(i, False)
            return carry

        lax.fori_loop(kj + 1, nq, loop_body, 0)
        k0 = pl.multiple_of(kj * tq, tq)
        out_ref[1, pl.ds(k0, tq), :] = jnp.where(own[0], dk_sc[0], dk_sc[1]).astype(BF16)
        out_ref[2, pl.ds(k0, tq), :] = jnp.where(own[0], dv_sc[0], dv_sc[1]).astype(BF16)

        def put_lane(ref, r0, e, tile, src_lane):
            col = jnp.sum(jnp.where(lane == src_lane, tile, 0.0), axis=1, keepdims=True)
            ref[pl.ds(r0, tq), :] = jnp.where(lane == 2 * hp + e, col, ref[pl.ds(r0, tq), :])

        for e in range(2):
            put_lane(dcc_ref, k0, e, dk_sc[e], HEAD_DIM * (1 - e) + 3)

        @pl.when(kj == nq - 1)
        def _():
            def chunk(i, carry):
                r0 = pl.multiple_of(i * tq, tq)
                d0, d1 = dq_sc[0, pl.ds(r0, tq), :], dq_sc[1, pl.ds(r0, tq), :]
                out_ref[0, pl.ds(r0, tq), :] = (jnp.where(own[0], d0, d1) * scale).astype(BF16)
                put_lane(dcr_ref, r0, 0, d0, HEAD_DIM)
                put_lane(dcr_ref, r0, 1, d1, 0)
                return carry
            lax.fori_loop(0, nq, chunk, 0)

        if na:
            @pl.when((hp == HP - 1) & (kj == nq - 1))
            def _():
                _wait_all(remote)

    pair = pl.BlockSpec((2, S, LANE), lambda h, j: (0, 0, h))
    vec = pl.BlockSpec((S, LANE), lambda h, j: (0, 0))
    any_spec = pl.BlockSpec(memory_space=pl.ANY)
    sems = [pltpu.SemaphoreType.DMA((na, N_DEV - 1)), pltpu.SemaphoreType.DMA((na, N_DEV - 1))] if na else []
    outs = pl.pallas_call(
        body, name="fox_attn_bwd", grid=(HP, nq),
        in_specs=[pair, pair,
                  pl.BlockSpec((tq, LANE), lambda h, j: (j, wb + h)),
                  pl.BlockSpec((tq, LANE), lambda h, j: (j, 2 * wb + h)),
                  pl.BlockSpec((tq, LANE), lambda h, j: (j, 0))] + [any_spec] * na,
        out_specs=[pl.BlockSpec((3, S, LANE), lambda h, j: (0, 0, h)), vec, vec] + [any_spec] * na,
        out_shape=[jax.ShapeDtypeStruct((3, S, W), BF16), jax.ShapeDtypeStruct((S, LANE), F32),
                   jax.ShapeDtypeStruct((S, LANE), F32)]
        + [jax.ShapeDtypeStruct((N_DEV - 1,) + _scatter_block_shape(g, s), g.dtype)
           for g, s in zip(scatter, scatter_specs)],
        scratch_shapes=[pltpu.VMEM((2, S, LANE), F32), pltpu.VMEM((2, tq, LANE), F32),
                        pltpu.VMEM((2, tq, LANE), F32)] + sems,
        compiler_params=_cparams(),
    )(qaug, doaug, qkv, qkv, c, *scatter)
    return outs[0], outs[1], outs[2], list(outs[3:])


def _swa_pick(blk, half, lane):
    b = blk.astype(F32)
    r = pltpu.roll(b, HEAD_DIM, axis=1)
    return jnp.where(jnp.logical_xor(lane < HEAD_DIM, half == 1), b, r).astype(BF16)


def _swa_stack(t, lane, G):
    pieces = []
    z = jnp.zeros((SWA_BLOCK, LANE), t.dtype)
    for j in range(G // 2):
        tile = t[:, LANE * j:LANE * (j + 1)]
        pieces += [jnp.where(lane < HEAD_DIM, tile, z), jnp.where(lane < HEAD_DIM, z, tile)]
    return jnp.concatenate(pieces, axis=0)


def _swa_unstack(st, lane, G):
    tiles = []
    for j in range(G // 2):
        a = st[2 * j * SWA_BLOCK:(2 * j + 1) * SWA_BLOCK]
        b = st[(2 * j + 1) * SWA_BLOCK:(2 * j + 2) * SWA_BLOCK]
        tiles.append(jnp.where(lane < HEAD_DIM, a, b))
    return jnp.concatenate(tiles, axis=1)


def _swa_mask_bias(G):
    R = G * SWA_BLOCK
    t_loc = jnp.arange(R)[:, None] % SWA_BLOCK
    j_loc = jnp.arange(2 * SWA_BLOCK)[None, :]
    diff = t_loc + SWA_BLOCK - j_loc
    band = (diff >= 0) & (diff < SWA_BLOCK)
    return jnp.stack([jnp.where(band & (j_loc >= SWA_BLOCK), 0.0, NEG_INF),
                      jnp.where(band, 0.0, NEG_INF)]).astype(F32)


def _swa_scores(q, kp, kc, vp, vc, srow, bias, half, head0, G):
    lane = lax.broadcasted_iota(jnp.int32, (SWA_BLOCK, LANE), 1)
    kk = jnp.concatenate([_swa_pick(kp, half, lane), _swa_pick(kc, half, lane)], axis=0)
    vv = jnp.concatenate([_swa_pick(vp, half, lane), _swa_pick(vc, half, lane)], axis=0)
    qstack = _swa_stack(q, lane, G) * jnp.asarray(HEAD_DIM ** -0.5, BF16)
    s = _dot_nt(qstack, kk) + bias
    lane1 = lax.broadcasted_iota(jnp.int32, (1, LANE), 1)
    sink = jnp.concatenate(
        [jnp.broadcast_to(jnp.sum(jnp.where(lane1 == head0 + g, srow, 0.0), axis=1, keepdims=True), (SWA_BLOCK, 1))
         for g in range(G)], axis=0)
    m = jnp.maximum(jnp.max(s, axis=1, keepdims=True), sink)
    e = jnp.exp(s - m)
    es = jnp.exp(sink - m)
    inv = 1.0 / (jnp.sum(e, axis=1, keepdims=True) + es)
    return qstack, kk, vv, e * inv, es * inv, lane


def _swa_fwd(qkv, gate, sinks, mask_bias, HQ, HKV):
    S = qkv.shape[0]
    G = HQ // HKV
    WQ, KVW = HQ * HEAD_DIM, HKV * HEAD_DIM
    nb = S // SWA_BLOCK
    GW = G * HEAD_DIM
    kb, vb = WQ // LANE, (WQ + KVW) // LANE

    def body(q_ref, kp_ref, kc_ref, vp_ref, vc_ref, g_ref, sink_ref, b_ref, y_ref, o_ref):
        pair = pl.program_id(0)
        for half in range(2):
            cols = slice(GW * half, GW * (half + 1))
            _, _, vv, p, _, lane = _swa_scores(q_ref[:, cols], kp_ref[...], kc_ref[...], vp_ref[...], vc_ref[...],
                                               sink_ref[...], b_ref[0], half, (2 * pair + half) * G, G)
            o = _swa_unstack(_dot_nn(p.astype(BF16), vv), lane, G)
            g = g_ref[:, cols]
            y_ref[:, cols] = (o * (g * _sigmoid(g))).astype(BF16)
            o_ref[:, cols] = o.astype(BF16)

    blk = lambda cb, prev: pl.BlockSpec(
        (SWA_BLOCK, LANE), lambda h, n, cb=cb, prev=prev: (jnp.maximum(n - prev, 0), cb + h))
    qspec = pl.BlockSpec((SWA_BLOCK, 2 * GW), lambda h, n: (n, h))
    return pl.pallas_call(
        body, name="swa_attn_fwd", grid=(HKV // 2, nb),
        in_specs=[qspec, blk(kb, 1), blk(kb, 0), blk(vb, 1), blk(vb, 0), qspec,
                  pl.BlockSpec((1, LANE), lambda h, n: (0, 0)),
                  pl.BlockSpec((1, G * SWA_BLOCK, 2 * SWA_BLOCK), lambda h, n: (jnp.minimum(n, 1), 0, 0))],
        out_specs=[qspec, qspec],
        out_shape=[jax.ShapeDtypeStruct((S, WQ), BF16), jax.ShapeDtypeStruct((S, WQ), BF16)],
        compiler_params=_cparams(),
    )(qkv, qkv, qkv, qkv, qkv, gate, sinks, mask_bias)


def _swa_bwd(qkv, dy, gate, o, sinks, tables, mask_bias, HQ, HKV):
    S = qkv.shape[0]
    G = HQ // HKV
    WQ, KVW = HQ * HEAD_DIM, HKV * HEAD_DIM
    nb = S // SWA_BLOCK
    GW = G * HEAD_DIM
    R = G * SWA_BLOCK
    kb, vb = WQ // LANE, (WQ + KVW) // LANE
    scale = HEAD_DIM ** -0.5
    assert G == 8

    def body(q_ref, kp_ref, kc_ref, vp_ref, vc_ref, dy_ref, g_ref, o_ref, sink_ref, t_ref, b_ref,
             dqg_ref, dkv_ref, dsink_ref, carry_sc):
        pair, n = pl.program_id(0), pl.program_id(1)

        @pl.when(n == 0)
        def _():
            carry_sc[...] = jnp.zeros(carry_sc.shape, F32)
            dsink_ref[...] = jnp.zeros(dsink_ref.shape, F32)

        @pl.when(n < nb)
        def _():
            t0, t1, t2 = (jnp.tile(t_ref[i], (1, GW // LANE)) for i in range(3))
            for half in range(2):
                cols = slice(GW * half, GW * (half + 1))
                qstack, kk, vv, p, psink, lane = _swa_scores(
                    q_ref[:, cols], kp_ref[...], kc_ref[...], vp_ref[...], vc_ref[...], sink_ref[...], b_ref[0],
                    half, (2 * pair + half) * G, G)
                dyv, g, ov = dy_ref[:, cols], g_ref[:, cols], o_ref[:, cols].astype(F32)
                sg = _sigmoid(g)
                dob = (dyv * (g * sg)).astype(BF16)
                dqg_ref[1, :, cols] = (dyv * ov * (sg * (1.0 + g * (1.0 - sg)))).astype(BF16)
                prod = dob.astype(F32) * ov
                dparts = []
                for j in range(G // 2):
                    tile = prod[:, LANE * j:LANE * (j + 1)]
                    dparts += [jnp.sum(jnp.where(lane < HEAD_DIM, tile, 0.0), axis=1, keepdims=True),
                               jnp.sum(jnp.where(lane < HEAD_DIM, 0.0, tile), axis=1, keepdims=True)]
                delta = jnp.concatenate(dparts, axis=0)
                dostack = _swa_stack(dob, lane, G)
                ds = p * (_dot_nt(dostack, vv) - delta)
                dsb, pb = ds.astype(BF16), p.astype(BF16)
                dq = _swa_unstack(_dot_nn(dsb, kk), lane, G) * scale
                dq = dq * t0 + pltpu.roll(dq * t1, ROT_DIM // 2, axis=1) + pltpu.roll(dq * t2, GW - ROT_DIM // 2, axis=1)
                dqg_ref[0, :, cols] = dq.astype(BF16)
                dkk = _dot_tn(dsb, qstack)
                dvv = _dot_tn(pb, dostack)
                dkk = dkk + pltpu.roll(dkk, HEAD_DIM, axis=1)
                dvv = dvv + pltpu.roll(dvv, HEAD_DIM, axis=1)
                lane2 = lax.broadcasted_iota(jnp.int32, (2 * SWA_BLOCK, LANE), 1)
                comb = jnp.where(lane2 < HEAD_DIM, dkk, dvv)
                dkv_ref[half] = carry_sc[half] + comb[:SWA_BLOCK]
                carry_sc[half] = comb[SWA_BLOCK:]
                sk = psink * delta
                rows = [jnp.broadcast_to(-jnp.sum(sk[g_ * SWA_BLOCK:(g_ + 1) * SWA_BLOCK], axis=0, keepdims=True),
                                         (1, LANE)) for g_ in range(G)]
                dsink_ref[half] += jnp.concatenate(rows, axis=0)

        @pl.when(n == nb)
        def _():
            dkv_ref[...] = carry_sc[...]

    cl = lambda n: jnp.minimum(n, nb - 1)
    blk = lambda cb, prev: pl.BlockSpec(
        (SWA_BLOCK, LANE), lambda h, n, cb=cb, prev=prev: (jnp.maximum(cl(n) - prev, 0), cb + h))
    qspec = pl.BlockSpec((SWA_BLOCK, 2 * GW), lambda h, n: (cl(n), h))
    return pl.pallas_call(
        body, name="swa_attn_bwd", grid=(HKV // 2, nb + 1),
        in_specs=[qspec, blk(kb, 1), blk(kb, 0), blk(vb, 1), blk(vb, 0), qspec, qspec, qspec,
                  pl.BlockSpec((1, LANE), lambda h, n: (0, 0)),
                  pl.BlockSpec((3, SWA_BLOCK, LANE), lambda h, n: (0, cl(n), 0)),
                  pl.BlockSpec((1, R, 2 * SWA_BLOCK), lambda h, n: (jnp.minimum(n, 1), 0, 0))],
        out_specs=[pl.BlockSpec((2, SWA_BLOCK, 2 * GW), lambda h, n: (0, cl(n), h)),
                   pl.BlockSpec((2, SWA_BLOCK, LANE), lambda h, n: (h, jnp.maximum(n - 1, 0), 0)),
                   pl.BlockSpec((2, 8, LANE), lambda h, n: (h, 0, 0))],
        out_shape=[jax.ShapeDtypeStruct((2, S, WQ), BF16), jax.ShapeDtypeStruct((HKV, S, LANE), F32),
                   jax.ShapeDtypeStruct((HKV, 8, LANE), F32)],
        scratch_shapes=[pltpu.VMEM((2, SWA_BLOCK, LANE), F32)],
        compiler_params=_cparams(),
    )(qkv, qkv, qkv, qkv, qkv, dy, gate, o, sinks, tables, mask_bias)


def _swa_dkv_finish(dkv, tables):
    HKV, S, _ = dkv.shape
    KVW = HKV * HEAD_DIM
    tm = _row_tile(S, 512)
    npair = HKV // 2

    def body(d_ref, t_ref, o_ref):
        lane = lax.broadcasted_iota(jnp.int32, (tm, LANE), 1)
        lo = lane < HEAD_DIM
        for p in range(npair):
            a, b = d_ref[2 * p], d_ref[2 * p + 1]
            tk = jnp.where(lo, a, pltpu.roll(b, HEAD_DIM, axis=1))
            tv = jnp.where(lo, pltpu.roll(a, HEAD_DIM, axis=1), b)
            tk = (tk * t_ref[0] + pltpu.roll(tk * t_ref[1], ROT_DIM // 2, axis=1)
                  + pltpu.roll(tk * t_ref[2], LANE - ROT_DIM // 2, axis=1))
            o_ref[:, LANE * p:LANE * (p + 1)] = tk.astype(BF16)
            o_ref[:, KVW + LANE * p:KVW + LANE * (p + 1)] = tv.astype(BF16)

    return pl.pallas_call(
        body, name="swa_dkv_finish", grid=(S // tm,),
        in_specs=[pl.BlockSpec((HKV, tm, LANE), lambda i: (0, i, 0)), pl.BlockSpec((3, tm, LANE), lambda i: (0, i, 0))],
        out_specs=pl.BlockSpec((tm, 2 * KVW), lambda i: (i, 0)),
        out_shape=jax.ShapeDtypeStruct((S, 2 * KVW), BF16),
        compiler_params=_cparams(),
    )(dkv, tables)


def _rope_tables(S, width):
    half = ROT_DIM // 2
    pos = jnp.arange(S, dtype=F32)
    inv_freq = ROPE_THETA ** (-jnp.arange(half, dtype=F32) / half)
    ang = pos[:, None] * inv_freq[None, :]
    cos, sin = jnp.cos(ang), jnp.sin(ang)
    one = jnp.ones((S, HEAD_DIM - ROT_DIM), F32)
    zero = jnp.zeros((S, HEAD_DIM - ROT_DIM), F32)
    zh = jnp.zeros((S, half), F32)
    t0 = jnp.concatenate([cos, cos, one], axis=1)
    t1 = jnp.concatenate([-sin, zh, zero], axis=1)
    t2 = jnp.concatenate([zh, sin, zero], axis=1)
    return jnp.stack([jnp.tile(t, (1, width // HEAD_DIM)) for t in (t0, t1, t2)])


def _pad_rows(v, row, total_rows=8):
    return jnp.pad(v, ((row, total_rows - row - v.shape[0]), (0, 0)))


def _pad_lanes(v, off, width):
    return jnp.pad(v, ((0, 0), (off, width - off - v.shape[1])))


def kernel(x, norm_g, fox_w_in, fox_b_f, fox_w_out, swa_w_in, swa_sinks, swa_w_out, final_g, loss_target, m_norm_g, m_fox_w_in, m_fox_b_f, m_fox_w_out, m_swa_w_in, m_swa_sinks, m_swa_w_out, m_final_g, v_norm_g, v_fox_w_in, v_fox_b_f, v_fox_w_out, v_swa_w_in, v_swa_sinks, v_swa_w_out, v_final_g):
    S, D = x.shape[1], x.shape[2]
    H = fox_b_f.shape[1]
    W = H * HEAD_DIM
    wf = fox_w_in.shape[2]
    ws = swa_w_in.shape[2]
    HQ = swa_sinks.shape[1]
    WQ = HQ * HEAD_DIM
    KVW = (ws * N_DEV - 2 * WQ) // 2
    HKV = KVW // HEAD_DIM
    rows_o = fox_w_out.shape[1]
    assert wf * N_DEV == 4 * W + H and rows_o * N_DEV == W and H <= LANE and HQ <= LANE
    me = _my_index()

    _, sw_f, np_f = _slab_geom(wf)
    _, sw_s, np_s = _slab_geom(ws)

    def slab(w2d, w, sw):
        off = (w * me) % LANE
        return lax.dynamic_update_slice(jnp.zeros((w2d.shape[0], sw), BF16), w2d.astype(BF16), (0, off))

    (fi_all,) = _all_gather([slab(fox_w_in[0], wf, sw_f)])
    w_fi = _assemble(fi_all, wf)
    later = [slab(swa_w_in[0], ws, sw_s), fox_w_out[0].astype(BF16), swa_w_out[0].astype(BF16)]

    x0 = x[0]
    g0, g1, gf = norm_g[0:1], norm_g[1:2], final_g[None, :]
    bias = _pad_lanes(fox_b_f, 0, LANE)
    sinks = _pad_lanes(swa_sinks, 0, LANE)
    tab_k = _rope_tables(S, LANE)
    mask_bias = _swa_mask_bias(HQ // HKV)

    h0 = _rmsnorm_fwd(x0, g0, "rmsnorm0")
    qkv0 = _proj(h0, w_fi, 0, 3 * W, BF16, "fox_in_qkv")
    gate0 = _proj(h0, w_fi, 3 * W, W, F32, "fox_in_gate")
    fl = _proj(h0, w_fi, 4 * W, LANE, F32, "fox_in_f")
    c = _fox_gate_fwd(fl, bias)
    y0, o0, a0, (si_all, fo_all, so_all) = _fox_fwd(qkv0, gate0, c, H, gather=later)
    w_si = _assemble(si_all, ws)
    w_fo = fo_all.reshape(W, D)
    w_so = so_all.reshape(WQ, D)
    x1 = _out_proj_res(y0, w_fo, x0, "fox_out")

    h1 = _rmsnorm_fwd(x1, g1, "rmsnorm1")
    qkv1 = _proj(h1, w_si, 0, WQ + 2 * KVW, BF16, "swa_in_qkv", rope=(tab_k, WQ + KVW))
    gate1 = _proj(h1, w_si, WQ + 2 * KVW, WQ, F32, "swa_in_gate")
    y1, o1 = _swa_fwd(qkv1, gate1, sinks, mask_bias, HQ, HKV)
    x2 = _out_proj_res(y1, w_so, x1, "swa_out")

    dx2, dx2b, dgf, loss_p = _loss_head(x2, loss_target[0], gf)

    dy1 = _matmul_nt([(dx2b, None, 0)], w_so, WQ, "swa_out_bwd")
    g_so, g_so_h = _matmul_tn(y1, [(dx2b, None, 0)], D, "swa_out_wgrad", also_bf16=True)
    dqg1, dkv1, dsink = _swa_bwd(qkv1, dy1, gate1, o1, sinks, tab_k, mask_bias, HQ, HKV)
    dkv1f = _swa_dkv_finish(dkv1, tab_k)
    parts1 = [(dqg1, 0, 0), (dkv1f, None, WQ), (dqg1, 1, WQ + 2 * KVW)]
    g_si, g_si_h = _matmul_tn(h1, parts1, np_s, "swa_in_wgrad", tile_major=True, also_bf16=True)
    dh1 = _matmul_nt(parts1, w_si, D, "swa_in_bwd")
    dx1, dx1b, dg1 = _rmsnorm_bwd(dh1, x1, g1, dx2, "rmsnorm1_bwd")

    dy0 = _matmul_nt([(dx1b, None, 0)], w_fo, W, "fox_out_bwd")
    g_fo, g_fo_h = _matmul_tn(y0, [(dx1b, None, 0)], D, "fox_out_wgrad", also_bf16=True)
    qaug0, doaug0, dgate0 = _fox_bwd_prep(qkv0, dy0, gate0, o0, a0, H)
    early_specs = [("col", ws), ("row", rows_o), ("row", rows_o)]
    dqkv0, dcr, dcc, early_recv = _fox_bwd(qaug0, doaug0, qkv0, c, H, scatter=[g_si_h, g_fo_h, g_so_h],
                                          scatter_specs=early_specs)
    dfl, dbf = _fox_gate_bwd(fl, bias, dcr - dcc)
    parts0 = [(dqkv0, p, p * W) for p in range(3)] + [(dgate0, None, 3 * W), (dfl, None, 4 * W)]
    g_fi, g_fi_h = _matmul_tn(h0, parts0, np_f, "fox_in_wgrad", tile_major=True, also_bf16=True)
    spec_fi = ("col", wf)
    dh0, (recv_fi,) = _matmul_nt(parts0, w_fi, D, "fox_in_bwd", scatter=[g_fi_h], scatter_specs=[spec_fi])
    dx0, _, dg0 = _rmsnorm_bwd(dh0, x0, g0, dx1, "rmsnorm0_bwd")

    red_si, gw_fo, gw_so = [_final_sum8(g_, r_, s_)
                            for g_, r_, s_ in zip([g_si, g_fo, g_so], early_recv, early_specs)]
    red_fi = _final_sum8(g_fi, recv_fi, spec_fi)
    gw_fi = lax.dynamic_slice(red_fi, (0, (wf * me) % LANE), (D, wf))
    gw_si = lax.dynamic_slice(red_si, (0, (ws * me) % LANE), (D, ws))

    P = D
    dsink_v = dsink[:, :, 0].reshape(1, HQ)
    row3 = _pad_lanes(dbf[:, :H], 0, P) + _pad_lanes(dsink_v, LANE, P) + _pad_lanes(loss_p[:, :1], 2 * LANE, P)
    pack = _pad_rows(dg0, 0) + _pad_rows(dg1, 1) + _pad_rows(dgf, 2) + _pad_rows(row3, 3)
    tot = _all_reduce_small(pack)
    loss = tot[3, 2 * LANE]
    g_norm = tot[0:2]
    g_final = tot[2]
    g_bf = tot[3:4, 0:H]
    g_sinks = tot[3:4, LANE:LANE + HQ]

    def small_pack(ng, fg, bf, sk):
        r3 = _pad_lanes(bf, 0, P) + _pad_lanes(sk, LANE, P)
        return _pad_rows(ng, 0) + _pad_rows(fg[None, :], 2) + _pad_rows(r3, 3)

    sd, sm, sv = _adamw(small_pack(norm_g, final_g, fox_b_f, swa_sinks), tot,
                        small_pack(m_norm_g, m_final_g, m_fox_b_f, m_swa_sinks),
                        small_pack(v_norm_g, v_final_g, v_fox_b_f, v_swa_sinks), "adamw_small")

    def unpack(t):
        return t[0:2], t[3:4, 0:H], t[3:4, LANE:LANE + HQ], t[2]

    d_fi, m_fi, v_fi = _adamw(fox_w_in[0], gw_fi, m_fox_w_in[0], v_fox_w_in[0], "adamw_fox_in")
    d_fo, m_fo, v_fo = _adamw(fox_w_out[0], gw_fo, m_fox_w_out[0], v_fox_w_out[0], "adamw_fox_out")
    d_si, m_si, v_si = _adamw(swa_w_in[0], gw_si, m_swa_w_in[0], v_swa_w_in[0], "adamw_swa_in")
    d_so, m_so, v_so = _adamw(swa_w_out[0], gw_so, m_swa_w_out[0], v_swa_w_out[0], "adamw_swa_out")

    def group(small, fi, fo, si, so):
        ng, bf, sk, fg = unpack(small)
        return (ng, fi[None], bf, fo[None], si[None], sk, so[None], fg)

    grads = (g_norm, gw_fi[None], g_bf, gw_fo[None], gw_si[None], g_sinks, gw_so[None], g_final)
    return (loss, dx0[None], *grads, *group(sd, d_fi, d_fo, d_si, d_so),
            *group(sm, m_fi, m_fo, m_si, m_so), *group(sv, v_fi, v_fo, v_si, v_so))
```

```python
import math

import jax
import jax.numpy as jnp
from jax import lax
from jax.experimental import pallas as pl
from jax.experimental.pallas import tpu as pltpu

F32 = jnp.float32
BF16 = jnp.bfloat16
MESH = pl.DeviceIdType.MESH

N_DEV = 8
LANE = 128
HEAD_DIM = 64
SWA_BLOCK = 128
NEG_INF = -1e30
RMS_EPS = 1e-6
ROPE_THETA = 500000.0
ROT_DIM = HEAD_DIM // 4
ADAM_LR, ADAM_B1, ADAM_B2, ADAM_EPS, ADAM_WD, ADAM_STEP = 0.001, 0.9, 0.999, 1e-08, 0.01, 10
VMEM_LIMIT = 56 * 1024 * 1024
MM_TILE = 1024


def _cparams(**kw):
    return pltpu.CompilerParams(vmem_limit_bytes=VMEM_LIMIT, **kw)


def _tile(n, cap):
    if n <= cap:
        return n
    t = (cap // LANE) * LANE
    while t > LANE and n % t:
        t -= LANE
    assert n % t == 0, (n, cap)
    return t


def _row_tile(n, cap):
    t = min(n, cap)
    while n % t:
        t //= 2
    return t


def _dot_nn(a, b):
    return jnp.dot(a, b, preferred_element_type=F32)


def _dot_nt(a, b):
    return lax.dot_general(a, b, (((1,), (1,)), ((), ())), preferred_element_type=F32)


def _dot_tn(a, b):
    return lax.dot_general(a, b, (((0,), (0,)), ((), ())), preferred_element_type=F32)


def _sigmoid(g):
    return 1.0 / (1.0 + jnp.exp(-g))


def _slab_geom(w):
    starts = [w * i for i in range(N_DEV)]
    aligned = [LANE * (s // LANE) for s in starts]
    offs = [s - a for s, a in zip(starts, aligned)]
    sw = LANE * (-(-(max(offs) + w) // LANE))
    return aligned, sw, aligned[-1] + sw


def _my_index():
    return 4 * lax.axis_index("x") + 2 * lax.axis_index("y") + lax.axis_index("c")


def _all_gather(arrs):
    n = len(arrs)

    def body(*refs):
        ins, outs = refs[:n], refs[n:2 * n]
        send_sems, recv_sems, local_sems = refs[2 * n:]
        x, y, c = lax.axis_index("x"), lax.axis_index("y"), lax.axis_index("c")
        me, sib = (x, y, c), (x, y, 1 - c)
        chips = [(1 - x, y), (x, 1 - y), (1 - x, 1 - y)]

        def idx(px, py, pc):
            return 4 * px + 2 * py + pc

        def copy(a, k, block, to, src=None):
            dst = outs[a].at[idx(*block)]
            return pltpu.make_async_remote_copy(
                src_ref=dst if src is None else src, dst_ref=dst,
                send_sem=send_sems.at[a, k], recv_sem=recv_sems.at[a, k],
                device_id=to, device_id_type=MESH)

        mine = [pltpu.make_async_copy(ins[a], outs[a].at[idx(*me)], local_sems.at[a]) for a in range(n)]
        for m in mine:
            m.start()
        first = []
        for a in range(n):
            first.append(copy(a, 0, me, sib, src=ins[a]))
            for j, chip in enumerate(chips):
                first.append(copy(a, 1 + j, me, (*chip, c), src=ins[a]))
        for cp in first:
            cp.start()
        passed = []
        for j, chip in enumerate(chips):
            for a in range(n):
                copy(a, 1 + j, (*chip, c), me).wait_recv()
                p = copy(a, 4 + j, (*chip, c), sib)
                p.start()
                passed.append(p)
        for a in range(n):
            copy(a, 0, sib, me).wait_recv()
        for j, chip in enumerate(chips):
            for a in range(n):
                copy(a, 4 + j, (*chip, 1 - c), me).wait_recv()
        for cp in first + passed:
            cp.wait_send()
        for m in mine:
            m.wait()

    any_spec = pl.BlockSpec(memory_space=pl.ANY)
    return pl.pallas_call(
        body, name="weights_all_gather",
        out_shape=[jax.ShapeDtypeStruct((N_DEV,) + a.shape, a.dtype) for a in arrs],
        in_specs=[any_spec] * n, out_specs=[any_spec] * n,
        scratch_shapes=[pltpu.SemaphoreType.DMA((n, 7)), pltpu.SemaphoreType.DMA((n, 7)),
                        pltpu.SemaphoreType.DMA((n,))],
    )(*arrs)


def _rs_windows(specs):
    def window(ref, spec, blk):
        kind, n = spec
        if kind == "col":
            _, sw, _ = _slab_geom(n)
            return ref.at[pl.ds((n * blk) // LANE, sw // LANE)]
        start = pl.multiple_of(n * blk, n)
        return ref.at[pl.ds(start, n), :]
    return window


def _peer(k):
    x, y, c = lax.axis_index("x"), lax.axis_index("y"), lax.axis_index("c")
    return (x ^ (k >> 2), y ^ ((k >> 1) & 1), c ^ (k & 1))


def _direct_gather_copies(ins, outs, send_sems, recv_sems, local_sems):
    me = _my_index()
    remote, local = [], []
    for a, (src, dst) in enumerate(zip(ins, outs)):
        local.append(pltpu.make_async_copy(src, dst.at[me], local_sems.at[a]))
        for k in range(1, N_DEV):
            remote.append(pltpu.make_async_remote_copy(
                src_ref=src, dst_ref=dst.at[me], send_sem=send_sems.at[a, k - 1], recv_sem=recv_sems.at[a, k - 1],
                device_id=_peer(k), device_id_type=MESH))
    return remote, local


def _direct_scatter_copies(ins, outs, specs, send_sems, recv_sems):
    window = _rs_windows(specs)
    remote = []
    for a, (src, dst) in enumerate(zip(ins, outs)):
        for k in range(1, N_DEV):
            px, py, pc = _peer(k)
            remote.append(pltpu.make_async_remote_copy(
                src_ref=window(src, specs[a], 4 * px + 2 * py + pc), dst_ref=dst.at[k - 1],
                send_sem=send_sems.at[a, k - 1], recv_sem=recv_sems.at[a, k - 1],
                device_id=(px, py, pc), device_id_type=MESH))
    return remote


def _scatter_block_shape(g, spec):
    kind, w = spec
    return (_slab_geom(w)[1] // LANE, g.shape[1], LANE) if kind == "col" else (w, g.shape[1])


def _wait_all(remote, local=()):
    for cp in remote:
        cp.wait_recv()
    for cp in remote:
        cp.wait_send()
    for cp in local:
        cp.wait()


def _final_sum8(g, recv, spec):
    kind, n = spec
    me = _my_index()
    offs = jnp.stack([(n * me) // LANE if kind == "col" else me]).astype(jnp.int32)
    if kind == "col":
        _, T, M, _ = recv.shape
        grid = (T,)
        in_specs = [pl.BlockSpec((1, M, LANE), lambda t, o: (o[0] + t, 0, 0)),
                    pl.BlockSpec((N_DEV - 1, 1, M, LANE), lambda t, o: (0, t, 0, 0))]
        out_spec = pl.BlockSpec((M, LANE), lambda t, o: (0, t))
        out_shape = jax.ShapeDtypeStruct((M, T * LANE), F32)
    else:
        _, nrow, C = recv.shape
        grid = (1,)
        in_specs = [pl.BlockSpec((nrow, C), lambda t, o: (o[0], 0)),
                    pl.BlockSpec((N_DEV - 1, nrow, C), lambda t, o: (0, 0, 0))]
        out_spec = pl.BlockSpec((nrow, C), lambda t, o: (0, 0))
        out_shape = jax.ShapeDtypeStruct((nrow, C), F32)

    def body(o_ref, g_ref, r_ref, out_ref):
        acc = g_ref[0] if kind == "col" else g_ref[...]
        for k in range(N_DEV - 1):
            acc = acc + (r_ref[k, 0] if kind == "col" else r_ref[k]).astype(F32)
        out_ref[...] = acc

    return pl.pallas_call(
        body, name="grads_final_sum8",
        grid_spec=pltpu.PrefetchScalarGridSpec(num_scalar_prefetch=1, grid=grid, in_specs=in_specs,
                                               out_specs=out_spec),
        out_shape=out_shape, compiler_params=_cparams(),
    )(offs, g, recv)


def _all_reduce_small(pack):
    R, P = pack.shape

    def body(x_ref, o_ref, gat_ref, send_sems, recv_sems):
        x, y, c = lax.axis_index("x"), lax.axis_index("y"), lax.axis_index("c")
        me = 4 * x + 2 * y + c
        gat_ref[me] = x_ref[...]
        copies = []
        for k in range(1, N_DEV):
            peer = (x ^ (k >> 2), y ^ ((k >> 1) & 1), c ^ (k & 1))
            copies.append(pltpu.make_async_remote_copy(
                src_ref=x_ref, dst_ref=gat_ref.at[me],
                send_sem=send_sems.at[k - 1], recv_sem=recv_sems.at[k - 1],
                device_id=peer, device_id_type=MESH))
        for cp in copies:
            cp.start()
        for cp in copies:
            cp.wait_recv()
        for cp in copies:
            cp.wait_send()
        acc = gat_ref[0]
        for d in range(1, N_DEV):
            acc = acc + gat_ref[d]
        o_ref[...] = acc

    vm = pl.BlockSpec(memory_space=pltpu.VMEM)
    return pl.pallas_call(
        body, name="small_all_reduce",
        out_shape=jax.ShapeDtypeStruct((R, P), F32),
        in_specs=[vm], out_specs=vm,
        scratch_shapes=[pltpu.VMEM((N_DEV, R, P), F32),
                        pltpu.SemaphoreType.DMA((N_DEV - 1,)), pltpu.SemaphoreType.DMA((N_DEV - 1,))],
    )(pack)


def _assemble(slabs, w):
    aligned, sw, total = _slab_geom(w)
    K = slabs.shape[1]
    tr = _row_tile(K, 256)

    def body(s_ref, o_ref):
        o_ref[...] = jnp.zeros(o_ref.shape, BF16)
        for i in range(N_DEV):
            a = aligned[i]
            o_ref[:, a:a + sw] = o_ref[:, a:a + sw] + s_ref[i]

    return pl.pallas_call(
        body, name="assemble_w_in", grid=(K // tr,),
        in_specs=[pl.BlockSpec((N_DEV, tr, sw), lambda i: (0, i, 0))],
        out_specs=pl.BlockSpec((tr, total), lambda i: (i, 0)),
        out_shape=jax.ShapeDtypeStruct((K, total), BF16),
        compiler_params=_cparams(),
    )(slabs)


def _rmsnorm_fwd(x, g, name):
    S, D = x.shape
    tm = _row_tile(S, 256)

    def body(x_ref, g_ref, h_ref):
        xv = x_ref[...]
        r = lax.rsqrt(jnp.mean(xv * xv, axis=-1, keepdims=True) + RMS_EPS)
        h_ref[...] = ((xv * r) * g_ref[...]).astype(BF16)

    return pl.pallas_call(
        body, name=name, grid=(S // tm,),
        in_specs=[pl.BlockSpec((tm, D), lambda i: (i, 0)), pl.BlockSpec((1, D), lambda i: (0, 0))],
        out_specs=pl.BlockSpec((tm, D), lambda i: (i, 0)),
        out_shape=jax.ShapeDtypeStruct((S, D), BF16),
        compiler_params=_cparams(),
    )(x, g)


def _rmsnorm_bwd(dh, x, g, dres, name):
    S, D = x.shape
    tm = _row_tile(S, 256)

    def body(dh_ref, x_ref, g_ref, dr_ref, dx_ref, dxb_ref, dg_ref):
        xv = x_ref[...]
        r = lax.rsqrt(jnp.mean(xv * xv, axis=-1, keepdims=True) + RMS_EPS)
        xhat = xv * r
        d = dh_ref[...]
        gd = d * g_ref[...]
        dx = r * (gd - xhat * jnp.mean(gd * xhat, axis=-1, keepdims=True)) + dr_ref[...]
        dx_ref[...] = dx
        dxb_ref[...] = dx.astype(BF16)

        @pl.when(pl.program_id(0) == 0)
        def _():
            dg_ref[...] = jnp.zeros(dg_ref.shape, F32)
        dg_ref[...] += jnp.sum(d * xhat, axis=0, keepdims=True)

    row = pl.BlockSpec((tm, D), lambda i: (i, 0))
    vec = pl.BlockSpec((1, D), lambda i: (0, 0))
    return pl.pallas_call(
        body, name=name, grid=(S // tm,),
        in_specs=[row, row, vec, row], out_specs=[row, row, vec],
        out_shape=[jax.ShapeDtypeStruct((S, D), F32), jax.ShapeDtypeStruct((S, D), BF16),
                   jax.ShapeDtypeStruct((1, D), F32)],
        compiler_params=_cparams(),
    )(dh, x, g, dres)


def _loss_head(x, tgt, g):
    S, D = x.shape
    tm = _row_tile(S, 256)

    def body(x_ref, t_ref, g_ref, dx_ref, dxb_ref, dg_ref, loss_ref):
        xv = x_ref[...]
        r = lax.rsqrt(jnp.mean(xv * xv, axis=-1, keepdims=True) + RMS_EPS)
        xhat = xv * r
        gv = g_ref[...]
        err = xhat * gv - t_ref[...]
        d = err * (1.0 / D)
        gd = d * gv
        dx = r * (gd - xhat * jnp.mean(gd * xhat, axis=-1, keepdims=True))
        dx_ref[...] = dx
        dxb_ref[...] = dx.astype(BF16)

        @pl.when(pl.program_id(0) == 0)
        def _():
            dg_ref[...] = jnp.zeros(dg_ref.shape, F32)
            loss_ref[...] = jnp.zeros(loss_ref.shape, F32)
        dg_ref[...] += jnp.sum(d * xhat, axis=0, keepdims=True)
        per_tok = jnp.sum(err * err, axis=-1, keepdims=True) * (1.0 / D)
        loss_ref[...] += 0.5 * jnp.sum(per_tok, axis=0, keepdims=True)

    row = pl.BlockSpec((tm, D), lambda i: (i, 0))
    vec = pl.BlockSpec((1, D), lambda i: (0, 0))
    return pl.pallas_call(
        body, name="loss_head", grid=(S // tm,),
        in_specs=[row, row, vec],
        out_specs=[row, row, vec, pl.BlockSpec((1, LANE), lambda i: (0, 0))],
        out_shape=[jax.ShapeDtypeStruct((S, D), F32), jax.ShapeDtypeStruct((S, D), BF16),
                   jax.ShapeDtypeStruct((1, D), F32), jax.ShapeDtypeStruct((1, LANE), F32)],
        compiler_params=_cparams(),
    )(x, tgt, g)


def _adamw(w, g, m, v, name):
    R, C = w.shape
    tr = _row_tile(R, 256)
    c1 = 1.0 - ADAM_B1 ** ADAM_STEP
    c2 = 1.0 - ADAM_B2 ** ADAM_STEP

    def body(w_ref, g_ref, m_ref, v_ref, d_ref, nm_ref, nv_ref):
        gv = g_ref[...]
        nm = ADAM_B1 * m_ref[...] + (1.0 - ADAM_B1) * gv
        nv = ADAM_B2 * v_ref[...] + (1.0 - ADAM_B2) * (gv * gv)
        d_ref[...] = -ADAM_LR * ((nm / c1) / (jnp.sqrt(nv / c2) + ADAM_EPS) + ADAM_WD * w_ref[...])
        nm_ref[...] = nm
        nv_ref[...] = nv

    spec = pl.BlockSpec((tr, C), lambda i: (i, 0))
    return pl.pallas_call(
        body, name=name, grid=(R // tr,),
        in_specs=[spec] * 4, out_specs=[spec] * 3,
        out_shape=[jax.ShapeDtypeStruct((R, C), F32)] * 3,
        compiler_params=_cparams(),
    )(w, g, m, v)


def _proj(h, wfull, col0, ncols, out_dtype, name, rope=None):
    S, K = h.shape
    tm = _row_tile(S, MM_TILE)
    tn = math.gcd(_tile(ncols, MM_TILE), col0) if col0 else _tile(ncols, MM_TILE)
    if rope is not None:
        tn = _tile(math.gcd(ncols, rope[1]), MM_TILE)
    assert ncols % tn == 0 and col0 % tn == 0
    cb = col0 // tn

    def body(*refs):
        if rope is None:
            a_ref, b_ref, o_ref = refs
        else:
            a_ref, b_ref, t_ref, o_ref = refs
        acc = _dot_nn(a_ref[...], b_ref[...])
        if rope is not None:
            t0, t1, t2 = (jnp.tile(t_ref[i], (1, tn // LANE)) for i in range(3))
            roped = (acc * t0 + pltpu.roll(acc, tn - ROT_DIM // 2, axis=1) * t1
                     + pltpu.roll(acc, ROT_DIM // 2, axis=1) * t2)
            acc = jnp.where(pl.program_id(1) < rope[1] // tn, roped, acc)
        o_ref[...] = acc.astype(out_dtype)

    in_specs = [pl.BlockSpec((tm, K), lambda i, j: (i, 0)), pl.BlockSpec((K, tn), lambda i, j: (0, cb + j))]
    args = [h, wfull]
    if rope is not None:
        in_specs.append(pl.BlockSpec((3, tm, LANE), lambda i, j: (0, i, 0)))
        args.append(rope[0])
    return pl.pallas_call(
        body, name=name, grid=(S // tm, ncols // tn),
        in_specs=in_specs, out_specs=pl.BlockSpec((tm, tn), lambda i, j: (i, j)),
        out_shape=jax.ShapeDtypeStruct((S, ncols), out_dtype),
        compiler_params=_cparams(),
    )(*args)


def _out_proj_res(y, wo, xres, name):
    S, W = y.shape
    D = wo.shape[1]
    tm, tn = _row_tile(S, MM_TILE), _tile(D, MM_TILE)

    def body(a_ref, b_ref, r_ref, o_ref):
        o_ref[...] = r_ref[...] + _dot_nn(a_ref[...], b_ref[...])

    return pl.pallas_call(
        body, name=name, grid=(S // tm, D // tn),
        in_specs=[pl.BlockSpec((tm, W), lambda i, j: (i, 0)), pl.BlockSpec((W, tn), lambda i, j: (0, j)),
                  pl.BlockSpec((tm, tn), lambda i, j: (i, j))],
        out_specs=pl.BlockSpec((tm, tn), lambda i, j: (i, j)),
        out_shape=jax.ShapeDtypeStruct((S, D), F32),
        compiler_params=_cparams(),
    )(y, wo, xres)


def _matmul_nt(parts, wfull, out_rows, name, scatter=(), scatter_specs=()):
    na = len(scatter)
    S = parts[0][0].shape[-2]
    tm, tn = _row_tile(S, MM_TILE), _tile(out_rows, MM_TILE)
    plan, lo = [], 0
    for arr, lead, col0 in parts:
        n_p = arr.shape[-1]
        tk = math.gcd(_tile(n_p, 1024), col0) if col0 else _tile(n_p, 1024)
        steps = n_p // tk
        plan.append((lead, col0 // tk, tk, lo, lo + steps))
        lo += steps
    nk = lo
    npart = len(parts)

    def body(*refs):
        a_refs, w_refs = refs[:npart], refs[npart:2 * npart]
        nin = 2 * npart + na
        o_ref, acc_ref = refs[nin], refs[nin + 1 + na]
        i, j, k = pl.program_id(0), pl.program_id(1), pl.program_id(2)
        if na:
            remote = _direct_scatter_copies(refs[2 * npart:nin], refs[nin + 1:nin + 1 + na], scatter_specs,
                                            *refs[nin + 2 + na:])

            @pl.when((i == 0) & (j == 0) & (k == 0))
            def _():
                for cp in remote:
                    cp.start()

        @pl.when(k == 0)
        def _():
            acc_ref[...] = jnp.zeros(acc_ref.shape, F32)
        for p, (_, _, _, lo_p, hi_p) in enumerate(plan):
            @pl.when((k >= lo_p) & (k < hi_p))
            def _(p=p):
                acc_ref[...] += _dot_nt(a_refs[p][...], w_refs[p][...])

        @pl.when(k == nk - 1)
        def _():
            o_ref[...] = acc_ref[...]

        if na:
            @pl.when((i == S // tm - 1) & (j == out_rows // tn - 1) & (k == nk - 1))
            def _():
                _wait_all(remote)

    in_specs, args = [], []
    for (arr, lead, col0), (_, cb, tk, lo_p, hi_p) in zip(parts, plan):
        def kk(k, lo_p=lo_p, hi_p=hi_p):
            return jnp.clip(k - lo_p, 0, hi_p - lo_p - 1)
        if lead is None:
            in_specs.append(pl.BlockSpec((tm, tk), lambda i, j, k, kk=kk: (i, kk(k))))
        else:
            in_specs.append(pl.BlockSpec((None, tm, tk), lambda i, j, k, kk=kk, lead=lead: (lead, i, kk(k))))
        args.append(arr)
    for (_, cb, tk, lo_p, hi_p) in plan:
        def kk(k, lo_p=lo_p, hi_p=hi_p):
            return jnp.clip(k - lo_p, 0, hi_p - lo_p - 1)
        in_specs.append(pl.BlockSpec((tn, tk), lambda i, j, k, kk=kk, cb=cb: (j, cb + kk(k))))
        args.append(wfull)
    any_spec = pl.BlockSpec(memory_space=pl.ANY)
    sems = [pltpu.SemaphoreType.DMA((na, N_DEV - 1)), pltpu.SemaphoreType.DMA((na, N_DEV - 1))] if na else []
    outs = pl.pallas_call(
        body, name=name, grid=(S // tm, out_rows // tn, nk),
        in_specs=in_specs + [any_spec] * na,
        out_specs=[pl.BlockSpec((tm, tn), lambda i, j, k: (i, j))] + [any_spec] * na,
        out_shape=[jax.ShapeDtypeStruct((S, out_rows), F32)]
        + [jax.ShapeDtypeStruct((N_DEV - 1,) + _scatter_block_shape(g, s), g.dtype)
           for g, s in zip(scatter, scatter_specs)],
        scratch_shapes=[pltpu.VMEM((tm, tn), F32)] + sems,
        compiler_params=_cparams(),
    )(*args, *scatter)
    return (outs[0], list(outs[1:])) if na else outs[0]


def _matmul_tn(a, parts, total, name, tile_major=False, also_bf16=False):
    S, M = a.shape
    tm, ts = _tile(M, MM_TILE), _row_tile(S, MM_TILE)
    nout = 2 if also_bf16 else 1
    outs = None
    for idx, (arr, lead, col0) in enumerate(parts):
        n_p = arr.shape[-1]
        tn = math.gcd(_tile(n_p, MM_TILE), col0) if col0 else _tile(n_p, MM_TILE)
        cb = col0 // tn
        nk = S // ts

        def body(*refs, nk=nk, tn=tn):
            a_ref, b_ref = refs[0], refs[1]
            o_refs, acc_ref = refs[-1 - nout:-1], refs[-1]
            k = pl.program_id(2)

            @pl.when(k == 0)
            def _():
                acc_ref[...] = jnp.zeros(acc_ref.shape, F32)
            acc_ref[...] += _dot_tn(a_ref[...], b_ref[...])

            @pl.when(k == nk - 1)
            def _():
                for o_ref in o_refs:
                    if tile_major:
                        for t in range(tn // LANE):
                            o_ref[t] = acc_ref[:, LANE * t:LANE * (t + 1)].astype(o_ref.dtype)
                    else:
                        o_ref[...] = acc_ref[...].astype(o_ref.dtype)

        in_specs = [pl.BlockSpec((ts, tm), lambda i, j, k: (k, i))]
        if lead is None:
            in_specs.append(pl.BlockSpec((ts, tn), lambda i, j, k: (k, j)))
        else:
            in_specs.append(pl.BlockSpec((None, ts, tn), lambda i, j, k, lead=lead: (lead, k, j)))
        args = [a, arr]
        aliases = {}
        if outs is not None:
            in_specs += [pl.BlockSpec(memory_space=pl.ANY)] * nout
            args += list(outs)
            aliases = {2 + o: o for o in range(nout)}
        if tile_major:
            out_spec = pl.BlockSpec((tn // LANE, tm, LANE), lambda i, j, k, cb=cb: (cb + j, i, 0))
            shape = (total // LANE, M, LANE)
        else:
            out_spec = pl.BlockSpec((tm, tn), lambda i, j, k, cb=cb: (i, cb + j))
            shape = (M, total)
        outs = pl.pallas_call(
            body, name=f"{name}_{idx}", grid=(M // tm, n_p // tn, nk),
            in_specs=in_specs, out_specs=[out_spec] * nout,
            out_shape=[jax.ShapeDtypeStruct(shape, dt) for dt in (F32, BF16)[:nout]],
            scratch_shapes=[pltpu.VMEM((tm, tn), F32)],
            input_output_aliases=aliases,
            compiler_params=_cparams(),
        )(*args)
    return tuple(outs) if also_bf16 else outs[0]


def _log_sigmoid(z):
    e = jnp.exp(-jnp.abs(z))
    return jnp.minimum(z, 0.0) - jnp.where(e < 1e-4, e * (1.0 - 0.5 * e), jnp.log(1.0 + e))


def _fox_gate_fwd(fl, bias):
    S = fl.shape[0]

    def body(f_ref, b_ref, c_ref):
        row = lax.broadcasted_iota(jnp.int32, (8, LANE), 0)

        def step(i, carry):
            r0 = pl.multiple_of(i * 8, 8)
            t = _log_sigmoid(f_ref[pl.ds(r0, 8), :] + b_ref[...])
            for sh in (1, 2, 4):
                t = t + jnp.where(row >= sh, pltpu.roll(t, sh, axis=0), 0.0)
            t = t + carry
            c_ref[pl.ds(r0, 8), :] = t
            return jnp.sum(jnp.where(row == 7, t, 0.0), axis=0, keepdims=True)

        lax.fori_loop(0, S // 8, step, jnp.zeros((1, LANE), F32))

    vm = pl.BlockSpec(memory_space=pltpu.VMEM)
    return pl.pallas_call(
        body, name="fox_gate_fwd", in_specs=[vm, vm], out_specs=vm,
        out_shape=jax.ShapeDtypeStruct((S, LANE), F32),
        compiler_params=_cparams(),
    )(fl, bias)


def _fox_gate_bwd(fl, bias, dc):
    S = fl.shape[0]

    def body(f_ref, b_ref, d_ref, o_ref, db_ref, acc_ref):
        row = lax.broadcasted_iota(jnp.int32, (8, LANE), 0)
        nt = S // 8

        def step(ii, carry):
            carry_c, carry_b = carry
            r0 = pl.multiple_of((nt - 1 - ii) * 8, 8)
            t = d_ref[pl.ds(r0, 8), :]
            for sh in (1, 2, 4):
                t = t + jnp.where(row < 8 - sh, pltpu.roll(t, 8 - sh, axis=0), 0.0)
            t = t + carry_c
            z = f_ref[pl.ds(r0, 8), :] + b_ref[...]
            dz = t * _sigmoid(-z)
            acc_ref[pl.ds(r0, 8), :] = dz
            first = jnp.sum(jnp.where(row == 0, t, 0.0), axis=0, keepdims=True)
            return first, carry_b + jnp.sum(dz, axis=0, keepdims=True)

        zero = jnp.zeros((1, LANE), F32)
        _, db = lax.fori_loop(0, nt, step, (zero, zero))
        db_ref[...] = db
        o_ref[...] = acc_ref[...].astype(BF16)

    vm = pl.BlockSpec(memory_space=pltpu.VMEM)
    return pl.pallas_call(
        body, name="fox_gate_bwd", in_specs=[vm, vm, vm], out_specs=[vm, vm],
        out_shape=[jax.ShapeDtypeStruct((S, LANE), BF16), jax.ShapeDtypeStruct((1, LANE), F32)],
        scratch_shapes=[pltpu.VMEM((S, LANE), F32)],
        compiler_params=_cparams(),
    )(fl, bias, dc)


def _bias_lanes(col, lane, e, first):
    o0 = HEAD_DIM * (1 - e)
    hi = col.astype(BF16)
    r1 = col - hi.astype(F32)
    mid = r1.astype(BF16)
    lo = (r1 - mid.astype(F32)).astype(BF16)
    d0 = o0 if first else o0 + 3
    t = jnp.where((lane >= o0) & (lane < o0 + 6), jnp.ones(lane.shape, BF16), jnp.zeros(lane.shape, BF16))
    t = jnp.where(lane == d0, hi, t)
    t = jnp.where(lane == d0 + 1, mid, t)
    return jnp.where(lane == d0 + 2, lo, t)


def _fox_fwd(qkv, gate, c, H, gather=()):
    na = len(gather)
    S = qkv.shape[0]
    W = H * HEAD_DIM
    HP = H // 2
    PP = 2 if HP % 2 == 0 else 1
    NE = 2 * PP
    tq = _row_tile(S, 512)
    nq = S // tq
    wb = W // LANE
    scale = HEAD_DIM ** -0.5

    def body(*refs):
        q_ref, k_ref, v_ref, g_ref, c_ref = refs[:5]
        y_ref, o_ref, a_ref = refs[5 + na:8 + na]
        kaug_sc, vaug_sc, qaug_sc, s_sc, mb_sc, m_sc, acc_sc = refs[8 + 2 * na:15 + 2 * na]
        hp, qi = pl.program_id(0), pl.program_id(1)
        if na:
            remote, local = _direct_gather_copies(refs[5:5 + na], refs[8 + na:8 + 2 * na], *refs[15 + 2 * na:])

            @pl.when((hp == 0) & (qi == 0))
            def _():
                for cp in remote + local:
                    cp.start()
        lane = lax.broadcasted_iota(jnp.int32, (tq, LANE), 1)
        own = [lane < HEAD_DIM, lane >= HEAD_DIM]
        rows = lax.broadcasted_iota(jnp.int32, (tq, tq), 0)
        cols = lax.broadcasted_iota(jnp.int32, (tq, tq), 1)

        def bias_lanes(col, e, first):
            return _bias_lanes(col, lane, e % 2, first)

        def head_col(tile, e):
            return jnp.sum(jnp.where(lane == 2 * PP * hp + e, tile, 0.0), axis=1, keepdims=True)

        def tile_of(e):
            return slice(LANE * (e // 2), LANE * (e // 2 + 1))

        @pl.when(qi == 0)
        def _():
            def chunk(i, carry):
                r0 = pl.multiple_of(i * tq, tq)
                cb = c_ref[pl.ds(r0, tq), :]
                for e in range(NE):
                    kb, vb = k_ref[pl.ds(r0, tq), tile_of(e)], v_ref[pl.ds(r0, tq), tile_of(e)]
                    kaug_sc[e, pl.ds(r0, tq), :] = jnp.where(own[e % 2], kb, bias_lanes(-head_col(cb, e), e, False))
                    vaug_sc[e, pl.ds(r0, tq), :] = jnp.where(own[e % 2], vb, jnp.ones((tq, LANE), BF16))
                return carry
            lax.fori_loop(0, nq, chunk, 0)

        crow = c_ref[pl.ds(pl.multiple_of(qi * tq, tq), tq), :]
        ctq = [head_col(crow, e) for e in range(NE)]
        for e in range(NE):
            q = q_ref[:, tile_of(e)] * jnp.asarray(scale, BF16)
            qaug_sc[e] = jnp.where(own[e % 2], q, bias_lanes(ctq[e], e, True))
        m_sc[...] = jnp.full(m_sc.shape, NEG_INF, F32)
        acc_sc[...] = jnp.zeros(acc_sc.shape, F32)

        def scores(blk, slot, masked):
            k0 = pl.multiple_of(blk * tq, tq)
            for e in range(NE):
                s = _dot_nt(qaug_sc[e], kaug_sc[e, pl.ds(k0, tq), :])
                if masked:
                    s = jnp.where(rows >= cols, s, NEG_INF)
                s_sc[slot, e] = s
                mb_sc[slot, e] = jnp.broadcast_to(jnp.max(s, axis=1, keepdims=True), (tq, LANE))

        def accumulate(blk, slot):
            k0 = pl.multiple_of(blk * tq, tq)
            for e in range(NE):
                m_prev = m_sc[e]
                m_new = jnp.maximum(m_prev, mb_sc[slot, e])
                p = jnp.exp(s_sc[slot, e] - jnp.tile(m_new, (1, tq // LANE)))
                acc_sc[e] = jnp.exp(m_prev - m_new) * acc_sc[e] + _dot_nn(p.astype(BF16), vaug_sc[e, pl.ds(k0, tq), :])
                m_sc[e] = m_new

        def block_of(t):
            return jnp.where(t == 0, qi, t - 1)

        scores(qi, 0, True)

        def loop_body(t, carry):
            scores(t, (t + 1) % 2, False)
            accumulate(block_of(t), t % 2)
            return carry

        lax.fori_loop(0, qi, loop_body, 0)
        accumulate(block_of(qi), qi % 2)
        o_e, a_e = [], []
        for e in range(NE):
            acc = acc_sc[e]
            l = pltpu.roll(acc, HEAD_DIM, axis=1)
            o_e.append(acc / l)
            a_e.append(ctq[e] - (m_sc[e] + jnp.log(l)))
        for pp in range(PP):
            o = jnp.where(own[0], o_e[2 * pp], o_e[2 * pp + 1])
            g = g_ref[:, tile_of(2 * pp)]
            y_ref[:, tile_of(2 * pp)] = (o * (g * _sigmoid(g))).astype(BF16)
            o_ref[:, tile_of(2 * pp)] = o.astype(BF16)
            a_ref[pp] = jnp.where(own[0], a_e[2 * pp], a_e[2 * pp + 1])
        if na:
            @pl.when((hp == HP // PP - 1) & (qi == nq - 1))
            def _():
                _wait_all(remote, local)

    any_spec = pl.BlockSpec(memory_space=pl.ANY)
    sems = [pltpu.SemaphoreType.DMA((na, N_DEV - 1)), pltpu.SemaphoreType.DMA((na, N_DEV - 1)),
            pltpu.SemaphoreType.DMA((na,))] if na else []
    wide = PP * LANE
    outs = pl.pallas_call(
        body, name="fox_attn_fwd", grid=(HP // PP, nq),
        in_specs=[pl.BlockSpec((tq, wide), lambda h, i: (i, h)),
                  pl.BlockSpec((S, wide), lambda h, i: (0, wb // PP + h)),
                  pl.BlockSpec((S, wide), lambda h, i: (0, 2 * wb // PP + h)),
                  pl.BlockSpec((tq, wide), lambda h, i: (i, h)),
                  pl.BlockSpec((S, LANE), lambda h, i: (0, 0))] + [any_spec] * na,
        out_specs=[pl.BlockSpec((tq, wide), lambda h, i: (i, h)),
                   pl.BlockSpec((tq, wide), lambda h, i: (i, h)),
                   pl.BlockSpec((PP, tq, LANE), lambda h, i: (h, i, 0))] + [any_spec] * na,
        out_shape=[jax.ShapeDtypeStruct((S, W), BF16), jax.ShapeDtypeStruct((S, W), BF16),
                   jax.ShapeDtypeStruct((HP, S, LANE), F32)]
        + [jax.ShapeDtypeStruct((N_DEV,) + g.shape, g.dtype) for g in gather],
        scratch_shapes=[pltpu.VMEM((NE, S, LANE), BF16), pltpu.VMEM((NE, S, LANE), BF16),
                        pltpu.VMEM((NE, tq, LANE), BF16), pltpu.VMEM((2, NE, tq, tq), F32),
                        pltpu.VMEM((2, NE, tq, LANE), F32), pltpu.VMEM((NE, tq, LANE), F32),
                        pltpu.VMEM((NE, tq, LANE), F32)] + sems,
        compiler_params=_cparams(),
    )(qkv, qkv, qkv, gate, c, *gather)
    return outs[0], outs[1], outs[2], list(outs[3:])


def _fox_bwd_prep(qkv, dy, gate, o, a, H):
    S = qkv.shape[0]
    W = H * HEAD_DIM
    HP = H // 2
    tq = _row_tile(S, 512)
    scale = HEAD_DIM ** -0.5

    def body(q_ref, dy_ref, g_ref, o_ref, a_ref, qa_ref, da_ref, dg_ref):
        lane = lax.broadcasted_iota(jnp.int32, (tq, LANE), 1)
        own = [lane < HEAD_DIM, lane >= HEAD_DIM]
        q = q_ref[...] * jnp.asarray(scale, BF16)
        dyv, g, ov, at = dy_ref[...], g_ref[...], o_ref[...].astype(F32), a_ref[0]
        sg = _sigmoid(g)
        dob = (dyv * (g * sg)).astype(BF16)
        dg_ref[...] = (dyv * ov * (sg * (1.0 + g * (1.0 - sg)))).astype(BF16)
        prod = dob.astype(F32) * ov
        for e in range(2):
            a_col = jnp.max(jnp.where(own[e], at, -jnp.inf), axis=1, keepdims=True)
            d_col = jnp.sum(jnp.where(own[e], prod, 0.0), axis=1, keepdims=True)
            qa_ref[e] = jnp.where(own[e], q, _bias_lanes(a_col, lane, e, True))
            da_ref[e] = jnp.where(own[e], dob, _bias_lanes(-d_col, lane, e, True))

    blk = pl.BlockSpec((tq, LANE), lambda h, i: (i, h))
    pair = pl.BlockSpec((2, tq, LANE), lambda h, i: (0, i, h))
    return pl.pallas_call(
        body, name="fox_attn_bwd_prep", grid=(HP, S // tq),
        in_specs=[blk, blk, blk, blk, pl.BlockSpec((1, tq, LANE), lambda h, i: (h, i, 0))],
        out_specs=[pair, pair, blk],
        out_shape=[jax.ShapeDtypeStruct((2, S, W), BF16), jax.ShapeDtypeStruct((2, S, W), BF16),
                   jax.ShapeDtypeStruct((S, W), BF16)],
        compiler_params=_cparams(),
    )(qkv, dy, gate, o, a)


def _fox_bwd(qaug, doaug, qkv, c, H, scatter=(), scatter_specs=()):
    na = len(scatter)
    S = qkv.shape[0]
    W = H * HEAD_DIM
    HP = H // 2
    tq = _row_tile(S, 512)
    nq = S // tq
    wb = W // LANE
    scale = HEAD_DIM ** -0.5

    def body(*refs):
        qa_ref, da_ref, k_ref, v_ref, c_ref = refs[:5]
        out_ref, dcr_ref, dcc_ref = refs[5 + na:8 + na]
        dq_sc, dk_sc, dv_sc = refs[8 + 2 * na:11 + 2 * na]
        hp, kj = pl.program_id(0), pl.program_id(1)
        if na:
            remote = _direct_scatter_copies(refs[5:5 + na], refs[8 + na:8 + 2 * na], scatter_specs,
                                            *refs[11 + 2 * na:])

            @pl.when((hp == 0) & (kj == 0))
            def _():
                for cp in remote:
                    cp.start()
        lane = lax.broadcasted_iota(jnp.int32, (tq, LANE), 1)
        own = [lane < HEAD_DIM, lane >= HEAD_DIM]
        rows = lax.broadcasted_iota(jnp.int32, (tq, tq), 0)
        cols = lax.broadcasted_iota(jnp.int32, (tq, tq), 1)

        @pl.when(kj == 0)
        def _():
            dq_sc[...] = jnp.zeros(dq_sc.shape, F32)

        @pl.when((kj == 0) & (hp == 0))
        def _():
            dcr_ref[...] = jnp.zeros(dcr_ref.shape, F32)
            dcc_ref[...] = jnp.zeros(dcc_ref.shape, F32)

        kblk, vblk, cblk = k_ref[...], v_ref[...], c_ref[...]
        one, zero = jnp.ones((tq, LANE), BF16), jnp.zeros((tq, LANE), BF16)
        ka, va = [], []
        for e in range(2):
            o0 = HEAD_DIM * (1 - e)
            c_col = jnp.sum(jnp.where(lane == 2 * hp + e, cblk, 0.0), axis=1, keepdims=True)
            ka.append(jnp.where(own[e], kblk, _bias_lanes(-c_col, lane, e, False)))
            va.append(jnp.where(own[e], vblk, jnp.where((lane >= o0) & (lane < o0 + 3), one, zero)))
        dk_sc[...] = jnp.zeros(dk_sc.shape, F32)
        dv_sc[...] = jnp.zeros(dv_sc.shape, F32)

        def step(i, masked):
            r0 = pl.multiple_of(i * tq, tq)
            for e in range(2):
                qa = qa_ref[e, pl.ds(r0, tq), :]
                da = da_ref[e, pl.ds(r0, tq), :]
                p = jnp.exp(_dot_nt(qa, ka[e]))
                if masked:
                    p = jnp.where(rows >= cols, p, 0.0)
                ds = p * _dot_nt(da, va[e])
                pb, dsb = p.astype(BF16), ds.astype(BF16)
                dv_sc[e] += _dot_tn(pb, da)
                dk_sc[e] += _dot_tn(dsb, qa)
                dq_sc[e, pl.ds(r0, tq), :] += _dot_nn(dsb, ka[e])

        step(kj, True)

        def loop_body(i, carry):
            step(i, False)
            return carry

        lax.fori_loop(kj + 1, nq, loop_body, 0)
        k0 = pl.multiple_of(kj * tq, tq)
        out_ref[1, pl.ds(k0, tq), :] = jnp.where(own[0], dk_sc[0], dk_sc[1]).astype(BF16)
        out_ref[2, pl.ds(k0, tq), :] = jnp.where(own[0], dv_sc[0], dv_sc[1]).astype(BF16)

        def put_lane(ref, r0, e, tile, src_lane):
            col = jnp.sum(jnp.where(lane == src_lane, tile, 0.0), axis=1, keepdims=True)
            ref[pl.ds(r0, tq), :] = jnp.where(lane == 2 * hp + e, col, ref[pl.ds(r0, tq), :])

        for e in range(2):
            put_lane(dcc_ref, k0, e, dk_sc[e], HEAD_DIM * (1 - e) + 3)

        @pl.when(kj == nq - 1)
        def _():
            def chunk(i, carry):
                r0 = pl.multiple_of(i * tq, tq)
                d0, d1 = dq_sc[0, pl.ds(r0, tq), :], dq_sc[1, pl.ds(r0, tq), :]
                out_ref[0, pl.ds(r0, tq), :] = (jnp.where(own[0], d0, d1) * scale).astype(BF16)
                put_lane(dcr_ref, r0, 0, d0, HEAD_DIM)
                put_lane(dcr_ref, r0, 1, d1, 0)
                return carry
            lax.fori_loop(0, nq, chunk, 0)

        if na:
            @pl.when((hp == HP - 1) & (kj == nq - 1))
            def _():
                _wait_all(remote)

    pair = pl.BlockSpec((2, S, LANE), lambda h, j: (0, 0, h))
    vec = pl.BlockSpec((S, LANE), lambda h, j: (0, 0))
    any_spec = pl.BlockSpec(memory_space=pl.ANY)
    sems = [pltpu.SemaphoreType.DMA((na, N_DEV - 1)), pltpu.SemaphoreType.DMA((na, N_DEV - 1))] if na else []
    outs = pl.pallas_call(
        body, name="fox_attn_bwd", grid=(HP, nq),
        in_specs=[pair, pair,
                  pl.BlockSpec((tq, LANE), lambda h, j: (j, wb + h)),
                  pl.BlockSpec((tq, LANE), lambda h, j: (j, 2 * wb + h)),
                  pl.BlockSpec((tq, LANE), lambda h, j: (j, 0))] + [any_spec] * na,
        out_specs=[pl.BlockSpec((3, S, LANE), lambda h, j: (0, 0, h)), vec, vec] + [any_spec] * na,
        out_shape=[jax.ShapeDtypeStruct((3, S, W), BF16), jax.ShapeDtypeStruct((S, LANE), F32),
                   jax.ShapeDtypeStruct((S, LANE), F32)]
        + [jax.ShapeDtypeStruct((N_DEV - 1,) + _scatter_block_shape(g, s), g.dtype)
           for g, s in zip(scatter, scatter_specs)],
        scratch_shapes=[pltpu.VMEM((2, S, LANE), F32), pltpu.VMEM((2, tq, LANE), F32),
                        pltpu.VMEM((2, tq, LANE), F32)] + sems,
        compiler_params=_cparams(),
    )(qaug, doaug, qkv, qkv, c, *scatter)
    return outs[0], outs[1], outs[2], list(outs[3:])


def _swa_pick(blk, half, lane):
    b = blk.astype(F32)
    r = pltpu.roll(b, HEAD_DIM, axis=1)
    return jnp.where(jnp.logical_xor(lane < HEAD_DIM, half == 1), b, r).astype(BF16)


def _swa_stack(t, lane, G):
    pieces = []
    z = jnp.zeros((SWA_BLOCK, LANE), t.dtype)
    for j in range(G // 2):
        tile = t[:, LANE * j:LANE * (j + 1)]
        pieces += [jnp.where(lane < HEAD_DIM, tile, z), jnp.where(lane < HEAD_DIM, z, tile)]
    return jnp.concatenate(pieces, axis=0)


def _swa_unstack(st, lane, G):
    tiles = []
    for j in range(G // 2):
        a = st[2 * j * SWA_BLOCK:(2 * j + 1) * SWA_BLOCK]
        b = st[(2 * j + 1) * SWA_BLOCK:(2 * j + 2) * SWA_BLOCK]
        tiles.append(jnp.where(lane < HEAD_DIM, a, b))
    return jnp.concatenate(tiles, axis=1)


def _swa_mask_bias(G):
    R = G * SWA_BLOCK
    t_loc = jnp.arange(R)[:, None] % SWA_BLOCK
    j_loc = jnp.arange(2 * SWA_BLOCK)[None, :]
    diff = t_loc + SWA_BLOCK - j_loc
    band = (diff >= 0) & (diff < SWA_BLOCK)
    return jnp.stack([jnp.where(band & (j_loc >= SWA_BLOCK), 0.0, NEG_INF),
                      jnp.where(band, 0.0, NEG_INF)]).astype(F32)


def _swa_scores(q, kp, kc, vp, vc, srow, bias, half, head0, G):
    lane = lax.broadcasted_iota(jnp.int32, (SWA_BLOCK, LANE), 1)
    kk = jnp.concatenate([_swa_pick(kp, half, lane), _swa_pick(kc, half, lane)], axis=0)
    vv = jnp.concatenate([_swa_pick(vp, half, lane), _swa_pick(vc, half, lane)], axis=0)
    qstack = _swa_stack(q, lane, G) * jnp.asarray(HEAD_DIM ** -0.5, BF16)
    s = _dot_nt(qstack, kk) + bias
    lane1 = lax.broadcasted_iota(jnp.int32, (1, LANE), 1)
    sink = jnp.concatenate(
        [jnp.broadcast_to(jnp.sum(jnp.where(lane1 == head0 + g, srow, 0.0), axis=1, keepdims=True), (SWA_BLOCK, 1))
         for g in range(G)], axis=0)
    m = jnp.maximum(jnp.max(s, axis=1, keepdims=True), sink)
    e = jnp.exp(s - m)
    es = jnp.exp(sink - m)
    inv = 1.0 / (jnp.sum(e, axis=1, keepdims=True) + es)
    return qstack, kk, vv, e * inv, es * inv, lane


def _swa_fwd(qkv, gate, sinks, mask_bias, HQ, HKV):
    S = qkv.shape[0]
    G = HQ // HKV
    WQ, KVW = HQ * HEAD_DIM, HKV * HEAD_DIM
    nb = S // SWA_BLOCK
    GW = G * HEAD_DIM
    kb, vb = WQ // LANE, (WQ + KVW) // LANE

    def body(q_ref, kp_ref, kc_ref, vp_ref, vc_ref, g_ref, sink_ref, b_ref, y_ref, o_ref):
        pair = pl.program_id(0)
        for half in range(2):
            cols = slice(GW * half, GW * (half + 1))
            _, _, vv, p, _, lane = _swa_scores(q_ref[:, cols], kp_ref[...], kc_ref[...], vp_ref[...], vc_ref[...],
                                               sink_ref[...], b_ref[0], half, (2 * pair + half) * G, G)
            o = _swa_unstack(_dot_nn(p.astype(BF16), vv), lane, G)
            g = g_ref[:, cols]
            y_ref[:, cols] = (o * (g * _sigmoid(g))).astype(BF16)
            o_ref[:, cols] = o.astype(BF16)

    blk = lambda cb, prev: pl.BlockSpec(
        (SWA_BLOCK, LANE), lambda h, n, cb=cb, prev=prev: (jnp.maximum(n - prev, 0), cb + h))
    qspec = pl.BlockSpec((SWA_BLOCK, 2 * GW), lambda h, n: (n, h))
    return pl.pallas_call(
        body, name="swa_attn_fwd", grid=(HKV // 2, nb),
        in_specs=[qspec, blk(kb, 1), blk(kb, 0), blk(vb, 1), blk(vb, 0), qspec,
                  pl.BlockSpec((1, LANE), lambda h, n: (0, 0)),
                  pl.BlockSpec((1, G * SWA_BLOCK, 2 * SWA_BLOCK), lambda h, n: (jnp.minimum(n, 1), 0, 0))],
        out_specs=[qspec, qspec],
        out_shape=[jax.ShapeDtypeStruct((S, WQ), BF16), jax.ShapeDtypeStruct((S, WQ), BF16)],
        compiler_params=_cparams(),
    )(qkv, qkv, qkv, qkv, qkv, gate, sinks, mask_bias)


def _swa_bwd(qkv, dy, gate, o, sinks, tables, mask_bias, HQ, HKV):
    S = qkv.shape[0]
    G = HQ // HKV
    WQ, KVW = HQ * HEAD_DIM, HKV * HEAD_DIM
    nb = S // SWA_BLOCK
    GW = G * HEAD_DIM
    R = G * SWA_BLOCK
    kb, vb = WQ // LANE, (WQ + KVW) // LANE
    scale = HEAD_DIM ** -0.5
    assert G == 8

    def body(q_ref, kp_ref, kc_ref, vp_ref, vc_ref, dy_ref, g_ref, o_ref, sink_ref, t_ref, b_ref,
             dqg_ref, dkv_ref, dsink_ref, carry_sc):
        pair, n = pl.program_id(0), pl.program_id(1)

        @pl.when(n == 0)
        def _():
            carry_sc[...] = jnp.zeros(carry_sc.shape, F32)
            dsink_ref[...] = jnp.zeros(dsink_ref.shape, F32)

        @pl.when(n < nb)
        def _():
            t0, t1, t2 = (jnp.tile(t_ref[i], (1, GW // LANE)) for i in range(3))
            for half in range(2):
                cols = slice(GW * half, GW * (half + 1))
                qstack, kk, vv, p, psink, lane = _swa_scores(
                    q_ref[:, cols], kp_ref[...], kc_ref[...], vp_ref[...], vc_ref[...], sink_ref[...], b_ref[0],
                    half, (2 * pair + half) * G, G)
                dyv, g, ov = dy_ref[:, cols], g_ref[:, cols], o_ref[:, cols].astype(F32)
                sg = _sigmoid(g)
                dob = (dyv * (g * sg)).astype(BF16)
                dqg_ref[1, :, cols] = (dyv * ov * (sg * (1.0 + g * (1.0 - sg)))).astype(BF16)
                prod = dob.astype(F32) * ov
                dparts = []
                for j in range(G // 2):
                    tile = prod[:, LANE * j:LANE * (j + 1)]
                    dparts += [jnp.sum(jnp.where(lane < HEAD_DIM, tile, 0.0), axis=1, keepdims=True),
                               jnp.sum(jnp.where(lane < HEAD_DIM, 0.0, tile), axis=1, keepdims=True)]
                delta = jnp.concatenate(dparts, axis=0)
                dostack = _swa_stack(dob, lane, G)
                ds = p * (_dot_nt(dostack, vv) - delta)
                dsb, pb = ds.astype(BF16), p.astype(BF16)
                dq = _swa_unstack(_dot_nn(dsb, kk), lane, G) * scale
                dq = dq * t0 + pltpu.roll(dq * t1, ROT_DIM // 2, axis=1) + pltpu.roll(dq * t2, GW - ROT_DIM // 2, axis=1)
                dqg_ref[0, :, cols] = dq.astype(BF16)
                dkk = _dot_tn(dsb, qstack)
                dvv = _dot_tn(pb, dostack)
                dkk = dkk + pltpu.roll(dkk, HEAD_DIM, axis=1)
                dvv = dvv + pltpu.roll(dvv, HEAD_DIM, axis=1)
                lane2 = lax.broadcasted_iota(jnp.int32, (2 * SWA_BLOCK, LANE), 1)
                comb = jnp.where(lane2 < HEAD_DIM, dkk, dvv)
                dkv_ref[half] = carry_sc[half] + comb[:SWA_BLOCK]
                carry_sc[half] = comb[SWA_BLOCK:]
                sk = psink * delta
                rows = [jnp.broadcast_to(-jnp.sum(sk[g_ * SWA_BLOCK:(g_ + 1) * SWA_BLOCK], axis=0, keepdims=True),
                                         (1, LANE)) for g_ in range(G)]
                dsink_ref[half] += jnp.concatenate(rows, axis=0)

        @pl.when(n == nb)
        def _():
            dkv_ref[...] = carry_sc[...]

    cl = lambda n: jnp.minimum(n, nb - 1)
    blk = lambda cb, prev: pl.BlockSpec(
        (SWA_BLOCK, LANE), lambda h, n, cb=cb, prev=prev: (jnp.maximum(cl(n) - prev, 0), cb + h))
    qspec = pl.BlockSpec((SWA_BLOCK, 2 * GW), lambda h, n: (cl(n), h))
    return pl.pallas_call(
        body, name="swa_attn_bwd", grid=(HKV // 2, nb + 1),
        in_specs=[qspec, blk(kb, 1), blk(kb, 0), blk(vb, 1), blk(vb, 0), qspec, qspec, qspec,
                  pl.BlockSpec((1, LANE), lambda h, n: (0, 0)),
                  pl.BlockSpec((3, SWA_BLOCK, LANE), lambda h, n: (0, cl(n), 0)),
                  pl.BlockSpec((1, R, 2 * SWA_BLOCK), lambda h, n: (jnp.minimum(n, 1), 0, 0))],
        out_specs=[pl.BlockSpec((2, SWA_BLOCK, 2 * GW), lambda h, n: (0, cl(n), h)),
                   pl.BlockSpec((2, SWA_BLOCK, LANE), lambda h, n: (h, jnp.maximum(n - 1, 0), 0)),
                   pl.BlockSpec((2, 8, LANE), lambda h, n: (h, 0, 0))],
        out_shape=[jax.ShapeDtypeStruct((2, S, WQ), BF16), jax.ShapeDtypeStruct((HKV, S, LANE), F32),
                   jax.ShapeDtypeStruct((HKV, 8, LANE), F32)],
        scratch_shapes=[pltpu.VMEM((2, SWA_BLOCK, LANE), F32)],
        compiler_params=_cparams(),
    )(qkv, qkv, qkv, qkv, qkv, dy, gate, o, sinks, tables, mask_bias)


def _swa_dkv_finish(dkv, tables):
    HKV, S, _ = dkv.shape
    KVW = HKV * HEAD_DIM
    tm = _row_tile(S, 512)
    npair = HKV // 2

    def body(d_ref, t_ref, o_ref):
        lane = lax.broadcasted_iota(jnp.int32, (tm, LANE), 1)
        lo = lane < HEAD_DIM
        for p in range(npair):
            a, b = d_ref[2 * p], d_ref[2 * p + 1]
            tk = jnp.where(lo, a, pltpu.roll(b, HEAD_DIM, axis=1))
            tv = jnp.where(lo, pltpu.roll(a, HEAD_DIM, axis=1), b)
            tk = (tk * t_ref[0] + pltpu.roll(tk * t_ref[1], ROT_DIM // 2, axis=1)
                  + pltpu.roll(tk * t_ref[2], LANE - ROT_DIM // 2, axis=1))
            o_ref[:, LANE * p:LANE * (p + 1)] = tk.astype(BF16)
            o_ref[:, KVW + LANE * p:KVW + LANE * (p + 1)] = tv.astype(BF16)

    return pl.pallas_call(
        body, name="swa_dkv_finish", grid=(S // tm,),
        in_specs=[pl.BlockSpec((HKV, tm, LANE), lambda i: (0, i, 0)), pl.BlockSpec((3, tm, LANE), lambda i: (0, i, 0))],
        out_specs=pl.BlockSpec((tm, 2 * KVW), lambda i: (i, 0)),
        out_shape=jax.ShapeDtypeStruct((S, 2 * KVW), BF16),
        compiler_params=_cparams(),
    )(dkv, tables)


def _rope_tables(S, width):
    half = ROT_DIM // 2
    pos = jnp.arange(S, dtype=F32)
    inv_freq = ROPE_THETA ** (-jnp.arange(half, dtype=F32) / half)
    ang = pos[:, None] * inv_freq[None, :]
    cos, sin = jnp.cos(ang), jnp.sin(ang)
    one = jnp.ones((S, HEAD_DIM - ROT_DIM), F32)
    zero = jnp.zeros((S, HEAD_DIM - ROT_DIM), F32)
    zh = jnp.zeros((S, half), F32)
    t0 = jnp.concatenate([cos, cos, one], axis=1)
    t1 = jnp.concatenate([-sin, zh, zero], axis=1)
    t2 = jnp.concatenate([zh, sin, zero], axis=1)
    return jnp.stack([jnp.tile(t, (1, width // HEAD_DIM)) for t in (t0, t1, t2)])


def _pad_rows(v, row, total_rows=8):
    return jnp.pad(v, ((row, total_rows - row - v.shape[0]), (0, 0)))


def _pad_lanes(v, off, width):
    return jnp.pad(v, ((0, 0), (off, width - off - v.shape[1])))


def kernel(x, norm_g, fox_w_in, fox_b_f, fox_w_out, swa_w_in, swa_sinks, swa_w_out, final_g, loss_target, m_norm_g, m_fox_w_in, m_fox_b_f, m_fox_w_out, m_swa_w_in, m_swa_sinks, m_swa_w_out, m_final_g, v_norm_g, v_fox_w_in, v_fox_b_f, v_fox_w_out, v_swa_w_in, v_swa_sinks, v_swa_w_out, v_final_g):
    S, D = x.shape[1], x.shape[2]
    H = fox_b_f.shape[1]
    W = H * HEAD_DIM
    wf = fox_w_in.shape[2]
    ws = swa_w_in.shape[2]
    HQ = swa_sinks.shape[1]
    WQ = HQ * HEAD_DIM
    KVW = (ws * N_DEV - 2 * WQ) // 2
    HKV = KVW // HEAD_DIM
    rows_o = fox_w_out.shape[1]
    assert wf * N_DEV == 4 * W + H and rows_o * N_DEV == W and H <= LANE and HQ <= LANE
    me = _my_index()

    _, sw_f, np_f = _slab_geom(wf)
    _, sw_s, np_s = _slab_geom(ws)

    def slab(w2d, w, sw):
        off = (w * me) % LANE
        return lax.dynamic_update_slice(jnp.zeros((w2d.shape[0], sw), BF16), w2d.astype(BF16), (0, off))

    (fi_all,) = _all_gather([slab(fox_w_in[0], wf, sw_f)])
    w_fi = _assemble(fi_all, wf)
    later = [slab(swa_w_in[0], ws, sw_s), fox_w_out[0].astype(BF16), swa_w_out[0].astype(BF16)]

    x0 = x[0]
    g0, g1, gf = norm_g[0:1], norm_g[1:2], final_g[None, :]
    bias = _pad_lanes(fox_b_f, 0, LANE)
    sinks = _pad_lanes(swa_sinks, 0, LANE)
    tab_k = _rope_tables(S, LANE)
    mask_bias = _swa_mask_bias(HQ // HKV)

    h0 = _rmsnorm_fwd(x0, g0, "rmsnorm0")
    qkv0 = _proj(h0, w_fi, 0, 3 * W, BF16, "fox_in_qkv")
    gate0 = _proj(h0, w_fi, 3 * W, W, F32, "fox_in_gate")
    fl = _proj(h0, w_fi, 4 * W, LANE, F32, "fox_in_f")
    c = _fox_gate_fwd(fl, bias)
    y0, o0, a0, (si_all, fo_all, so_all) = _fox_fwd(qkv0, gate0, c, H, gather=later)
    w_si = _assemble(si_all, ws)
    w_fo = fo_all.reshape(W, D)
    w_so = so_all.reshape(WQ, D)
    x1 = _out_proj_res(y0, w_fo, x0, "fox_out")

    h1 = _rmsnorm_fwd(x1, g1, "rmsnorm1")
    qkv1 = _proj(h1, w_si, 0, WQ + 2 * KVW, BF16, "swa_in_qkv", rope=(tab_k, WQ + KVW))
    gate1 = _proj(h1, w_si, WQ + 2 * KVW, WQ, F32, "swa_in_gate")
    y1, o1 = _swa_fwd(qkv1, gate1, sinks, mask_bias, HQ, HKV)
    x2 = _out_proj_res(y1, w_so, x1, "swa_out")

    dx2, dx2b, dgf, loss_p = _loss_head(x2, loss_target[0], gf)

    dy1 = _matmul_nt([(dx2b, None, 0)], w_so, WQ, "swa_out_bwd")
    g_so, g_so_h = _matmul_tn(y1, [(dx2b, None, 0)], D, "swa_out_wgrad", also_bf16=True)
    dqg1, dkv1, dsink = _swa_bwd(qkv1, dy1, gate1, o1, sinks, tab_k, mask_bias, HQ, HKV)
    dkv1f = _swa_dkv_finish(dkv1, tab_k)
    parts1 = [(dqg1, 0, 0), (dkv1f, None, WQ), (dqg1, 1, WQ + 2 * KVW)]
    g_si, g_si_h = _matmul_tn(h1, parts1, np_s, "swa_in_wgrad", tile_major=True, also_bf16=True)
    dh1 = _matmul_nt(parts1, w_si, D, "swa_in_bwd")
    dx1, dx1b, dg1 = _rmsnorm_bwd(dh1, x1, g1, dx2, "rmsnorm1_bwd")

    dy0 = _matmul_nt([(dx1b, None, 0)], w_fo, W, "fox_out_bwd")
    g_fo, g_fo_h = _matmul_tn(y0, [(dx1b, None, 0)], D, "fox_out_wgrad", also_bf16=True)
    qaug0, doaug0, dgate0 = _fox_bwd_prep(qkv0, dy0, gate0, o0, a0, H)
    early_specs = [("col", ws), ("row", rows_o), ("row", rows_o)]
    dqkv0, dcr, dcc, early_recv = _fox_bwd(qaug0, doaug0, qkv0, c, H, scatter=[g_si_h, g_fo_h, g_so_h],
                                          scatter_specs=early_specs)
    dfl, dbf = _fox_gate_bwd(fl, bias, dcr - dcc)
    parts0 = [(dqkv0, p, p * W) for p in range(3)] + [(dgate0, None, 3 * W), (dfl, None, 4 * W)]
    g_fi, g_fi_h = _matmul_tn(h0, parts0, np_f, "fox_in_wgrad", tile_major=True, also_bf16=True)
    spec_fi = ("col", wf)
    dh0, (recv_fi,) = _matmul_nt(parts0, w_fi, D, "fox_in_bwd", scatter=[g_fi_h], scatter_specs=[spec_fi])
    dx0, _, dg0 = _rmsnorm_bwd(dh0, x0, g0, dx1, "rmsnorm0_bwd")

    red_si, gw_fo, gw_so = [_final_sum8(g_, r_, s_)
                            for g_, r_, s_ in zip([g_si, g_fo, g_so], early_recv, early_specs)]
    red_fi = _final_sum8(g_fi, recv_fi, spec_fi)
    gw_fi = lax.dynamic_slice(red_fi, (0, (wf * me) % LANE), (D, wf))
    gw_si = lax.dynamic_slice(red_si, (0, (ws * me) % LANE), (D, ws))

    P = D
    dsink_v = dsink[:, :, 0].reshape(1, HQ)
    row3 = _pad_lanes(dbf[:, :H], 0, P) + _pad_lanes(dsink_v, LANE, P) + _pad_lanes(loss_p[:, :1], 2 * LANE, P)
    pack = _pad_rows(dg0, 0) + _pad_rows(dg1, 1) + _pad_rows(dgf, 2) + _pad_rows(row3, 3)
    tot = _all_reduce_small(pack)
    loss = tot[3, 2 * LANE]
    g_norm = tot[0:2]
    g_final = tot[2]
    g_bf = tot[3:4, 0:H]
    g_sinks = tot[3:4, LANE:LANE + HQ]

    def small_pack(ng, fg, bf, sk):
        r3 = _pad_lanes(bf, 0, P) + _pad_lanes(sk, LANE, P)
        return _pad_rows(ng, 0) + _pad_rows(fg[None, :], 2) + _pad_rows(r3, 3)

    sd, sm, sv = _adamw(small_pack(norm_g, final_g, fox_b_f, swa_sinks), tot,
                        small_pack(m_norm_g, m_final_g, m_fox_b_f, m_swa_sinks),
                        small_pack(v_norm_g, v_final_g, v_fox_b_f, v_swa_sinks), "adamw_small")

    def unpack(t):
        return t[0:2], t[3:4, 0:H], t[3:4, LANE:LANE + HQ], t[2]

    d_fi, m_fi, v_fi = _adamw(fox_w_in[0], gw_fi, m_fox_w_in[0], v_fox_w_in[0], "adamw_fox_in")
    d_fo, m_fo, v_fo = _adamw(fox_w_out[0], gw_fo, m_fox_w_out[0], v_fox_w_out[0], "adamw_fox_out")
    d_si, m_si, v_si = _adamw(swa_w_in[0], gw_si, m_swa_w_in[0], v_swa_w_in[0], "adamw_swa_in")
    d_so, m_so, v_so = _adamw(swa_w_out[0], gw_so, m_swa_w_out[0], v_swa_w_out[0], "adamw_swa_out")

    def group(small, fi, fo, si, so):
        ng, bf, sk, fg = unpack(small)
        return (ng, fi[None], bf, fo[None], si[None], sk, so[None], fg)

    grads = (g_norm, gw_fi[None], g_bf, gw_fo[None], gw_si[None], g_sinks, gw_so[None], g_final)
    return (loss, dx0[None], *grads, *group(sd, d_fi, d_fo, d_si, d_so),
            *group(sm, m_fi, m_fo, m_si, m_so), *group(sv, v_fi, v_fo, v_si, v_so))
```

```python
import math

import jax
import jax.numpy as jnp
from jax import lax
from jax.experimental import pallas as pl
from jax.experimental.pallas import tpu as pltpu

F32 = jnp.float32
BF16 = jnp.bfloat16
MESH = pl.DeviceIdType.MESH

N_DEV = 8
LANE = 128
HEAD_DIM = 64
SWA_BLOCK = 128
NEG_INF = -1e30
RMS_EPS = 1e-6
ROPE_THETA = 500000.0
ROT_DIM = HEAD_DIM // 4
ADAM_LR, ADAM_B1, ADAM_B2, ADAM_EPS, ADAM_WD, ADAM_STEP = 0.001, 0.9, 0.999, 1e-08, 0.01, 10
VMEM_LIMIT = 56 * 1024 * 1024
MM_TILE = 1024


def _cparams(**kw):
    return pltpu.CompilerParams(vmem_limit_bytes=VMEM_LIMIT, **kw)


def _tile(n, cap):
    if n <= cap:
        return n
    t = (cap // LANE) * LANE
    while t > LANE and n % t:
        t -= LANE
    assert n % t == 0, (n, cap)
    return t


def _row_tile(n, cap):
    t = min(n, cap)
    while n % t:
        t //= 2
    return t


def _dot_nn(a, b):
    return jnp.dot(a, b, preferred_element_type=F32)


def _dot_nt(a, b):
    return lax.dot_general(a, b, (((1,), (1,)), ((), ())), preferred_element_type=F32)


def _dot_tn(a, b):
    return lax.dot_general(a, b, (((0,), (0,)), ((), ())), preferred_element_type=F32)


def _sigmoid(g):
    return 1.0 / (1.0 + jnp.exp(-g))


def _slab_geom(w):
    starts = [w * i for i in range(N_DEV)]
    aligned = [LANE * (s // LANE) for s in starts]
    offs = [s - a for s, a in zip(starts, aligned)]
    sw = LANE * (-(-(max(offs) + w) // LANE))
    return aligned, sw, aligned[-1] + sw


def _my_index():
    return 4 * lax.axis_index("x") + 2 * lax.axis_index("y") + lax.axis_index("c")


def _all_gather(arrs):
    n = len(arrs)

    def body(*refs):
        ins, outs = refs[:n], refs[n:2 * n]
        send_sems, recv_sems, local_sems = refs[2 * n:]
        x, y, c = lax.axis_index("x"), lax.axis_index("y"), lax.axis_index("c")
        me, sib = (x, y, c), (x, y, 1 - c)
        chips = [(1 - x, y), (x, 1 - y), (1 - x, 1 - y)]

        def idx(px, py, pc):
            return 4 * px + 2 * py + pc

        def copy(a, k, block, to, src=None):
            dst = outs[a].at[idx(*block)]
            return pltpu.make_async_remote_copy(
                src_ref=dst if src is None else src, dst_ref=dst,
                send_sem=send_sems.at[a, k], recv_sem=recv_sems.at[a, k],
                device_id=to, device_id_type=MESH)

        mine = [pltpu.make_async_copy(ins[a], outs[a].at[idx(*me)], local_sems.at[a]) for a in range(n)]
        for m in mine:
            m.start()
        first = []
        for a in range(n):
            first.append(copy(a, 0, me, sib, src=ins[a]))
            for j, chip in enumerate(chips):
                first.append(copy(a, 1 + j, me, (*chip, c), src=ins[a]))
        for cp in first:
            cp.start()
        passed = []
        for j, chip in enumerate(chips):
            for a in range(n):
                copy(a, 1 + j, (*chip, c), me).wait_recv()
                p = copy(a, 4 + j, (*chip, c), sib)
                p.start()
                passed.append(p)
        for a in range(n):
            copy(a, 0, sib, me).wait_recv()
        for j, chip in enumerate(chips):
            for a in range(n):
                copy(a, 4 + j, (*chip, 1 - c), me).wait_recv()
        for cp in first + passed:
            cp.wait_send()
        for m in mine:
            m.wait()

    any_spec = pl.BlockSpec(memory_space=pl.ANY)
    return pl.pallas_call(
        body, name="weights_all_gather",
        out_shape=[jax.ShapeDtypeStruct((N_DEV,) + a.shape, a.dtype) for a in arrs],
        in_specs=[any_spec] * n, out_specs=[any_spec] * n,
        scratch_shapes=[pltpu.SemaphoreType.DMA((n, 7)), pltpu.SemaphoreType.DMA((n, 7)),
                        pltpu.SemaphoreType.DMA((n,))],
    )(*arrs)


def _rs_windows(specs):
    def window(ref, spec, blk):
        kind, n = spec
        if kind == "col":
            _, sw, _ = _slab_geom(n)
            return ref.at[pl.ds((n * blk) // LANE, sw // LANE)]
        start = pl.multiple_of(n * blk, n)
        return ref.at[pl.ds(start, n), :]
    return window


def _peer(k):
    x, y, c = lax.axis_index("x"), lax.axis_index("y"), lax.axis_index("c")
    return (x ^ (k >> 2), y ^ ((k >> 1) & 1), c ^ (k & 1))


def _direct_gather_copies(ins, outs, send_sems, recv_sems, local_sems):
    me = _my_index()
    remote, local = [], []
    for a, (src, dst) in enumerate(zip(ins, outs)):
        local.append(pltpu.make_async_copy(src, dst.at[me], local_sems.at[a]))
        for k in range(1, N_DEV):
            remote.append(pltpu.make_async_remote_copy(
                src_ref=src, dst_ref=dst.at[me], send_sem=send_sems.at[a, k - 1], recv_sem=recv_sems.at[a, k - 1],
                device_id=_peer(k), device_id_type=MESH))
    return remote, local


def _direct_scatter_copies(ins, outs, specs, send_sems, recv_sems):
    window = _rs_windows(specs)
    remote = []
    for a, (src, dst) in enumerate(zip(ins, outs)):
        for k in range(1, N_DEV):
            px, py, pc = _peer(k)
            remote.append(pltpu.make_async_remote_copy(
                src_ref=window(src, specs[a], 4 * px + 2 * py + pc), dst_ref=dst.at[k - 1],
                send_sem=send_sems.at[a, k - 1], recv_sem=recv_sems.at[a, k - 1],
                device_id=(px, py, pc), device_id_type=MESH))
    return remote


def _scatter_block_shape(g, spec):
    kind, w = spec
    return (_slab_geom(w)[1] // LANE, g.shape[1], LANE) if kind == "col" else (w, g.shape[1])


def _wait_all(remote, local=()):
    for cp in remote:
        cp.wait_recv()
    for cp in remote:
        cp.wait_send()
    for cp in local:
        cp.wait()


def _final_sum8(g, recv, spec):
    kind, n = spec
    me = _my_index()
    offs = jnp.stack([(n * me) // LANE if kind == "col" else me]).astype(jnp.int32)
    if kind == "col":
        _, T, M, _ = recv.shape
        grid = (T,)
        in_specs = [pl.BlockSpec((1, M, LANE), lambda t, o: (o[0] + t, 0, 0)),
                    pl.BlockSpec((N_DEV - 1, 1, M, LANE), lambda t, o: (0, t, 0, 0))]
        out_spec = pl.BlockSpec((M, LANE), lambda t, o: (0, t))
        out_shape = jax.ShapeDtypeStruct((M, T * LANE), F32)
    else:
        _, nrow, C = recv.shape
        grid = (1,)
        in_specs = [pl.BlockSpec((nrow, C), lambda t, o: (o[0], 0)),
                    pl.BlockSpec((N_DEV - 1, nrow, C), lambda t, o: (0, 0, 0))]
        out_spec = pl.BlockSpec((nrow, C), lambda t, o: (0, 0))
        out_shape = jax.ShapeDtypeStruct((nrow, C), F32)

    def body(o_ref, g_ref, r_ref, out_ref):
        acc = g_ref[0] if kind == "col" else g_ref[...]
        for k in range(N_DEV - 1):
            acc = acc + (r_ref[k, 0] if kind == "col" else r_ref[k]).astype(F32)
        out_ref[...] = acc

    return pl.pallas_call(
        body, name="grads_final_sum8",
        grid_spec=pltpu.PrefetchScalarGridSpec(num_scalar_prefetch=1, grid=grid, in_specs=in_specs,
                                               out_specs=out_spec),
        out_shape=out_shape, compiler_params=_cparams(),
    )(offs, g, recv)


def _all_reduce_small(pack):
    R, P = pack.shape

    def body(x_ref, o_ref, gat_ref, send_sems, recv_sems):
        x, y, c = lax.axis_index("x"), lax.axis_index("y"), lax.axis_index("c")
        me = 4 * x + 2 * y + c
        gat_ref[me] = x_ref[...]
        copies = []
        for k in range(1, N_DEV):
            peer = (x ^ (k >> 2), y ^ ((k >> 1) & 1), c ^ (k & 1))
            copies.append(pltpu.make_async_remote_copy(
                src_ref=x_ref, dst_ref=gat_ref.at[me],
                send_sem=send_sems.at[k - 1], recv_sem=recv_sems.at[k - 1],
                device_id=peer, device_id_type=MESH))
        for cp in copies:
            cp.start()
        for cp in copies:
            cp.wait_recv()
        for cp in copies:
            cp.wait_send()
        acc = gat_ref[0]
        for d in range(1, N_DEV):
            acc = acc + gat_ref[d]
        o_ref[...] = acc

    vm = pl.BlockSpec(memory_space=pltpu.VMEM)
    return pl.pallas_call(
        body, name="small_all_reduce",
        out_shape=jax.ShapeDtypeStruct((R, P), F32),
        in_specs=[vm], out_specs=vm,
        scratch_shapes=[pltpu.VMEM((N_DEV, R, P), F32),
                        pltpu.SemaphoreType.DMA((N_DEV - 1,)), pltpu.SemaphoreType.DMA((N_DEV - 1,))],
    )(pack)


def _assemble(slabs, w):
    aligned, sw, total = _slab_geom(w)
    K = slabs.shape[1]
    tr = _row_tile(K, 256)

    def body(s_ref, o_ref):
        o_ref[...] = jnp.zeros(o_ref.shape, BF16)
        for i in range(N_DEV):
            a = aligned[i]
            o_ref[:, a:a + sw] = o_ref[:, a:a + sw] + s_ref[i]

    return pl.pallas_call(
        body, name="assemble_w_in", grid=(K // tr,),
        in_specs=[pl.BlockSpec((N_DEV, tr, sw), lambda i: (0, i, 0))],
        out_specs=pl.BlockSpec((tr, total), lambda i: (i, 0)),
        out_shape=jax.ShapeDtypeStruct((K, total), BF16),
        compiler_params=_cparams(),
    )(slabs)


def _rmsnorm_fwd(x, g, name):
    S, D = x.shape
    tm = _row_tile(S, 256)

    def body(x_ref, g_ref, h_ref):
        xv = x_ref[...]
        r = lax.rsqrt(jnp.mean(xv * xv, axis=-1, keepdims=True) + RMS_EPS)
        h_ref[...] = ((xv * r) * g_ref[...]).astype(BF16)

    return pl.pallas_call(
        body, name=name, grid=(S // tm,),
        in_specs=[pl.BlockSpec((tm, D), lambda i: (i, 0)), pl.BlockSpec((1, D), lambda i: (0, 0))],
        out_specs=pl.BlockSpec((tm, D), lambda i: (i, 0)),
        out_shape=jax.ShapeDtypeStruct((S, D), BF16),
        compiler_params=_cparams(),
    )(x, g)


def _rmsnorm_bwd(dh, x, g, dres, name):
    S, D = x.shape
    tm = _row_tile(S, 256)

    def body(dh_ref, x_ref, g_ref, dr_ref, dx_ref, dxb_ref, dg_ref):
        xv = x_ref[...]
        r = lax.rsqrt(jnp.mean(xv * xv, axis=-1, keepdims=True) + RMS_EPS)
        xhat = xv * r
        d = dh_ref[...]
        gd = d * g_ref[...]
        dx = r * (gd - xhat * jnp.mean(gd * xhat, axis=-1, keepdims=True)) + dr_ref[...]
        dx_ref[...] = dx
        dxb_ref[...] = dx.astype(BF16)

        @pl.when(pl.program_id(0) == 0)
        def _():
            dg_ref[...] = jnp.zeros(dg_ref.shape, F32)
        dg_ref[...] += jnp.sum(d * xhat, axis=0, keepdims=True)

    row = pl.BlockSpec((tm, D), lambda i: (i, 0))
    vec = pl.BlockSpec((1, D), lambda i: (0, 0))
    return pl.pallas_call(
        body, name=name, grid=(S // tm,),
        in_specs=[row, row, vec, row], out_specs=[row, row, vec],
        out_shape=[jax.ShapeDtypeStruct((S, D), F32), jax.ShapeDtypeStruct((S, D), BF16),
                   jax.ShapeDtypeStruct((1, D), F32)],
        compiler_params=_cparams(),
    )(dh, x, g, dres)


def _loss_head(x, tgt, g):
    S, D = x.shape
    tm = _row_tile(S, 256)

    def body(x_ref, t_ref, g_ref, dx_ref, dxb_ref, dg_ref, loss_ref):
        xv = x_ref[...]
        r = lax.rsqrt(jnp.mean(xv * xv, axis=-1, keepdims=True) + RMS_EPS)
        xhat = xv * r
        gv = g_ref[...]
        err = xhat * gv - t_ref[...]
        d = err * (1.0 / D)
        gd = d * gv
        dx = r * (gd - xhat * jnp.mean(gd * xhat, axis=-1, keepdims=True))
        dx_ref[...] = dx
        dxb_ref[...] = dx.astype(BF16)

        @pl.when(pl.program_id(0) == 0)
        def _():
            dg_ref[...] = jnp.zeros(dg_ref.shape, F32)
            loss_ref[...] = jnp.zeros(loss_ref.shape, F32)
        dg_ref[...] += jnp.sum(d * xhat, axis=0, keepdims=True)
        per_tok = jnp.sum(err * err, axis=-1, keepdims=True) * (1.0 / D)
        loss_ref[...] += 0.5 * jnp.sum(per_tok, axis=0, keepdims=True)

    row = pl.BlockSpec((tm, D), lambda i: (i, 0))
    vec = pl.BlockSpec((1, D), lambda i: (0, 0))
    return pl.pallas_call(
        body, name="loss_head", grid=(S // tm,),
        in_specs=[row, row, vec],
        out_specs=[row, row, vec, pl.BlockSpec((1, LANE), lambda i: (0, 0))],
        out_shape=[jax.ShapeDtypeStruct((S, D), F32), jax.ShapeDtypeStruct((S, D), BF16),
                   jax.ShapeDtypeStruct((1, D), F32), jax.ShapeDtypeStruct((1, LANE), F32)],
        compiler_params=_cparams(),
    )(x, tgt, g)


def _adamw(w, g, m, v, name):
    R, C = w.shape
    tr = _row_tile(R, 256)
    c1 = 1.0 - ADAM_B1 ** ADAM_STEP
    c2 = 1.0 - ADAM_B2 ** ADAM_STEP

    def body(w_ref, g_ref, m_ref, v_ref, d_ref, nm_ref, nv_ref):
        gv = g_ref[...]
        nm = ADAM_B1 * m_ref[...] + (1.0 - ADAM_B1) * gv
        nv = ADAM_B2 * v_ref[...] + (1.0 - ADAM_B2) * (gv * gv)
        d_ref[...] = -ADAM_LR * ((nm / c1) / (jnp.sqrt(nv / c2) + ADAM_EPS) + ADAM_WD * w_ref[...])
        nm_ref[...] = nm
        nv_ref[...] = nv

    spec = pl.BlockSpec((tr, C), lambda i: (i, 0))
    return pl.pallas_call(
        body, name=name, grid=(R // tr,),
        in_specs=[spec] * 4, out_specs=[spec] * 3,
        out_shape=[jax.ShapeDtypeStruct((R, C), F32)] * 3,
        compiler_params=_cparams(),
    )(w, g, m, v)


def _proj(h, wfull, col0, ncols, out_dtype, name, rope=None):
    S, K = h.shape
    tm = _row_tile(S, MM_TILE)
    tn = math.gcd(_tile(ncols, MM_TILE), col0) if col0 else _tile(ncols, MM_TILE)
    if rope is not None:
        tn = _tile(math.gcd(ncols, rope[1]), MM_TILE)
    assert ncols % tn == 0 and col0 % tn == 0
    cb = col0 // tn

    def body(*refs):
        if rope is None:
            a_ref, b_ref, o_ref = refs
        else:
            a_ref, b_ref, t_ref, o_ref = refs
        acc = _dot_nn(a_ref[...], b_ref[...])
        if rope is not None:
            t0, t1, t2 = (jnp.tile(t_ref[i], (1, tn // LANE)) for i in range(3))
            roped = (acc * t0 + pltpu.roll(acc, tn - ROT_DIM // 2, axis=1) * t1
                     + pltpu.roll(acc, ROT_DIM // 2, axis=1) * t2)
            acc = jnp.where(pl.program_id(1) < rope[1] // tn, roped, acc)
        o_ref[...] = acc.astype(out_dtype)

    in_specs = [pl.BlockSpec((tm, K), lambda i, j: (i, 0)), pl.BlockSpec((K, tn), lambda i, j: (0, cb + j))]
    args = [h, wfull]
    if rope is not None:
        in_specs.append(pl.BlockSpec((3, tm, LANE), lambda i, j: (0, i, 0)))
        args.append(rope[0])
    return pl.pallas_call(
        body, name=name, grid=(S // tm, ncols // tn),
        in_specs=in_specs, out_specs=pl.BlockSpec((tm, tn), lambda i, j: (i, j)),
        out_shape=jax.ShapeDtypeStruct((S, ncols), out_dtype),
        compiler_params=_cparams(),
    )(*args)


def _out_proj_res(y, wo, xres, name):
    S, W = y.shape
    D = wo.shape[1]
    tm, tn = _row_tile(S, MM_TILE), _tile(D, MM_TILE)

    def body(a_ref, b_ref, r_ref, o_ref):
        o_ref[...] = r_ref[...] + _dot_nn(a_ref[...], b_ref[...])

    return pl.pallas_call(
        body, name=name, grid=(S // tm, D // tn),
        in_specs=[pl.BlockSpec((tm, W), lambda i, j: (i, 0)), pl.BlockSpec((W, tn), lambda i, j: (0, j)),
                  pl.BlockSpec((tm, tn), lambda i, j: (i, j))],
        out_specs=pl.BlockSpec((tm, tn), lambda i, j: (i, j)),
        out_shape=jax.ShapeDtypeStruct((S, D), F32),
        compiler_params=_cparams(),
    )(y, wo, xres)


def _matmul_nt(parts, wfull, out_rows, name, scatter=(), scatter_specs=()):
    na = len(scatter)
    S = parts[0][0].shape[-2]
    tm, tn = _row_tile(S, MM_TILE), _tile(out_rows, MM_TILE)
    plan, lo = [], 0
    for arr, lead, col0 in parts:
        n_p = arr.shape[-1]
        tk = math.gcd(_tile(n_p, 1024), col0) if col0 else _tile(n_p, 1024)
        steps = n_p // tk
        plan.append((lead, col0 // tk, tk, lo, lo + steps))
        lo += steps
    nk = lo
    npart = len(parts)

    def body(*refs):
        a_refs, w_refs = refs[:npart], refs[npart:2 * npart]
        nin = 2 * npart + na
        o_ref, acc_ref = refs[nin], refs[nin + 1 + na]
        i, j, k = pl.program_id(0), pl.program_id(1), pl.program_id(2)
        if na:
            remote = _direct_scatter_copies(refs[2 * npart:nin], refs[nin + 1:nin + 1 + na], scatter_specs,
                                            *refs[nin + 2 + na:])

            @pl.when((i == 0) & (j == 0) & (k == 0))
            def _():
                for cp in remote:
                    cp.start()

        @pl.when(k == 0)
        def _():
            acc_ref[...] = jnp.zeros(acc_ref.shape, F32)
        for p, (_, _, _, lo_p, hi_p) in enumerate(plan):
            @pl.when((k >= lo_p) & (k < hi_p))
            def _(p=p):
                acc_ref[...] += _dot_nt(a_refs[p][...], w_refs[p][...])

        @pl.when(k == nk - 1)
        def _():
            o_ref[...] = acc_ref[...]

        if na:
            @pl.when((i == S // tm - 1) & (j == out_rows // tn - 1) & (k == nk - 1))
            def _():
                _wait_all(remote)

    in_specs, args = [], []
    for (arr, lead, col0), (_, cb, tk, lo_p, hi_p) in zip(parts, plan):
        def kk(k, lo_p=lo_p, hi_p=hi_p):
            return jnp.clip(k - lo_p, 0, hi_p - lo_p - 1)
        if lead is None:
            in_specs.append(pl.BlockSpec((tm, tk), lambda i, j, k, kk=kk: (i, kk(k))))
        else:
            in_specs.append(pl.BlockSpec((None, tm, tk), lambda i, j, k, kk=kk, lead=lead: (lead, i, kk(k))))
        args.append(arr)
    for (_, cb, tk, lo_p, hi_p) in plan:
        def kk(k, lo_p=lo_p, hi_p=hi_p):
            return jnp.clip(k - lo_p, 0, hi_p - lo_p - 1)
        in_specs.append(pl.BlockSpec((tn, tk), lambda i, j, k, kk=kk, cb=cb: (j, cb + kk(k))))
        args.append(wfull)
    any_spec = pl.BlockSpec(memory_space=pl.ANY)
    sems = [pltpu.SemaphoreType.DMA((na, N_DEV - 1)), pltpu.SemaphoreType.DMA((na, N_DEV - 1))] if na else []
    outs = pl.pallas_call(
        body, name=name, grid=(S // tm, out_rows // tn, nk),
        in_specs=in_specs + [any_spec] * na,
        out_specs=[pl.BlockSpec((tm, tn), lambda i, j, k: (i, j))] + [any_spec] * na,
        out_shape=[jax.ShapeDtypeStruct((S, out_rows), F32)]
        + [jax.ShapeDtypeStruct((N_DEV - 1,) + _scatter_block_shape(g, s), g.dtype)
           for g, s in zip(scatter, scatter_specs)],
        scratch_shapes=[pltpu.VMEM((tm, tn), F32)] + sems,
        compiler_params=_cparams(),
    )(*args, *scatter)
    return (outs[0], list(outs[1:])) if na else outs[0]


def _matmul_tn(a, parts, total, name, tile_major=False, also_bf16=False):
    S, M = a.shape
    tm, ts = _tile(M, MM_TILE), _row_tile(S, MM_TILE)
    nout = 2 if also_bf16 else 1
    outs = None
    for idx, (arr, lead, col0) in enumerate(parts):
        n_p = arr.shape[-1]
        tn = math.gcd(_tile(n_p, MM_TILE), col0) if col0 else _tile(n_p, MM_TILE)
        cb = col0 // tn
        nk = S // ts

        def body(*refs, nk=nk, tn=tn):
            a_ref, b_ref = refs[0], refs[1]
            o_refs, acc_ref = refs[-1 - nout:-1], refs[-1]
            k = pl.program_id(2)

            @pl.when(k == 0)
            def _():
                acc_ref[...] = jnp.zeros(acc_ref.shape, F32)
            acc_ref[...] += _dot_tn(a_ref[...], b_ref[...])

            @pl.when(k == nk - 1)
            def _():
                for o_ref in o_refs:
                    if tile_major:
                        for t in range(tn // LANE):
                            o_ref[t] = acc_ref[:, LANE * t:LANE * (t + 1)].astype(o_ref.dtype)
                    else:
                        o_ref[...] = acc_ref[...].astype(o_ref.dtype)

        in_specs = [pl.BlockSpec((ts, tm), lambda i, j, k: (k, i))]
        if lead is None:
            in_specs.append(pl.BlockSpec((ts, tn), lambda i, j, k: (k, j)))
        else:
            in_specs.append(pl.BlockSpec((None, ts, tn), lambda i, j, k, lead=lead: (lead, k, j)))
        args = [a, arr]
        aliases = {}
        if outs is not None:
            in_specs += [pl.BlockSpec(memory_space=pl.ANY)] * nout
            args += list(outs)
            aliases = {2 + o: o for o in range(nout)}
        if tile_major:
            out_spec = pl.BlockSpec((tn // LANE, tm, LANE), lambda i, j, k, cb=cb: (cb + j, i, 0))
            shape = (total // LANE, M, LANE)
        else:
            out_spec = pl.BlockSpec((tm, tn), lambda i, j, k, cb=cb: (i, cb + j))
            shape = (M, total)
        outs = pl.pallas_call(
            body, name=f"{name}_{idx}", grid=(M // tm, n_p // tn, nk),
            in_specs=in_specs, out_specs=[out_spec] * nout,
            out_shape=[jax.ShapeDtypeStruct(shape, dt) for dt in (F32, BF16)[:nout]],
            scratch_shapes=[pltpu.VMEM((tm, tn), F32)],
            input_output_aliases=aliases,
            compiler_params=_cparams(),
        )(*args)
    return tuple(outs) if also_bf16 else outs[0]


def _log_sigmoid(z):
    e = jnp.exp(-jnp.abs(z))
    return jnp.minimum(z, 0.0) - jnp.where(e < 1e-4, e * (1.0 - 0.5 * e), jnp.log(1.0 + e))


def _fox_gate_fwd(fl, bias):
    S = fl.shape[0]

    def body(f_ref, b_ref, c_ref):
        row = lax.broadcasted_iota(jnp.int32, (8, LANE), 0)

        def step(i, carry):
            r0 = pl.multiple_of(i * 8, 8)
            t = _log_sigmoid(f_ref[pl.ds(r0, 8), :] + b_ref[...])
            for sh in (1, 2, 4):
                t = t + jnp.where(row >= sh, pltpu.roll(t, sh, axis=0), 0.0)
            t = t + carry
            c_ref[pl.ds(r0, 8), :] = t
            return jnp.sum(jnp.where(row == 7, t, 0.0), axis=0, keepdims=True)

        lax.fori_loop(0, S // 8, step, jnp.zeros((1, LANE), F32))

    vm = pl.BlockSpec(memory_space=pltpu.VMEM)
    return pl.pallas_call(
        body, name="fox_gate_fwd", in_specs=[vm, vm], out_specs=vm,
        out_shape=jax.ShapeDtypeStruct((S, LANE), F32),
        compiler_params=_cparams(),
    )(fl, bias)


def _fox_gate_bwd(fl, bias, dc):
    S = fl.shape[0]

    def body(f_ref, b_ref, d_ref, o_ref, db_ref, acc_ref):
        row = lax.broadcasted_iota(jnp.int32, (8, LANE), 0)
        nt = S // 8

        def step(ii, carry):
            carry_c, carry_b = carry
            r0 = pl.multiple_of((nt - 1 - ii) * 8, 8)
            t = d_ref[pl.ds(r0, 8), :]
            for sh in (1, 2, 4):
                t = t + jnp.where(row < 8 - sh, pltpu.roll(t, 8 - sh, axis=0), 0.0)
            t = t + carry_c
            z = f_ref[pl.ds(r0, 8), :] + b_ref[...]
            dz = t * _sigmoid(-z)
            acc_ref[pl.ds(r0, 8), :] = dz
            first = jnp.sum(jnp.where(row == 0, t, 0.0), axis=0, keepdims=True)
            return first, carry_b + jnp.sum(dz, axis=0, keepdims=True)

        zero = jnp.zeros((1, LANE), F32)
        _, db = lax.fori_loop(0, nt, step, (zero, zero))
        db_ref[...] = db
        o_ref[...] = acc_ref[...].astype(BF16)

    vm = pl.BlockSpec(memory_space=pltpu.VMEM)
    return pl.pallas_call(
        body, name="fox_gate_bwd", in_specs=[vm, vm, vm], out_specs=[vm, vm],
        out_shape=[jax.ShapeDtypeStruct((S, LANE), BF16), jax.ShapeDtypeStruct((1, LANE), F32)],
        scratch_shapes=[pltpu.VMEM((S, LANE), F32)],
        compiler_params=_cparams(),
    )(fl, bias, dc)


def _bias_lanes(col, lane, e, first):
    o0 = HEAD_DIM * (1 - e)
    hi = col.astype(BF16)
    r1 = col - hi.astype(F32)
    mid = r1.astype(BF16)
    lo = (r1 - mid.astype(F32)).astype(BF16)
    d0 = o0 if first else o0 + 3
    t = jnp.where((lane >= o0) & (lane < o0 + 6), jnp.ones(lane.shape, BF16), jnp.zeros(lane.shape, BF16))
    t = jnp.where(lane == d0, hi, t)
    t = jnp.where(lane == d0 + 1, mid, t)
    return jnp.where(lane == d0 + 2, lo, t)


def _fox_fwd(qkv, gate, c, H, gather=()):
    na = len(gather)
    S = qkv.shape[0]
    W = H * HEAD_DIM
    HP = H // 2
    PP = 2 if HP % 2 == 0 else 1
    NE = 2 * PP
    tq = _row_tile(S, 512)
    nq = S // tq
    wb = W // LANE
    scale = HEAD_DIM ** -0.5

    def body(*refs):
        q_ref, k_ref, v_ref, g_ref, c_ref = refs[:5]
        y_ref, o_ref, a_ref = refs[5 + na:8 + na]
        kaug_sc, vaug_sc, qaug_sc, s_sc, mb_sc, m_sc, acc_sc = refs[8 + 2 * na:15 + 2 * na]
        hp, qi = pl.program_id(0), pl.program_id(1)
        if na:
            remote, local = _direct_gather_copies(refs[5:5 + na], refs[8 + na:8 + 2 * na], *refs[15 + 2 * na:])

            @pl.when((hp == 0) & (qi == 0))
            def _():
                for cp in remote + local:
                    cp.start()
        lane = lax.broadcasted_iota(jnp.int32, (tq, LANE), 1)
        own = [lane < HEAD_DIM, lane >= HEAD_DIM]
        rows = lax.broadcasted_iota(jnp.int32, (tq, tq), 0)
        cols = lax.broadcasted_iota(jnp.int32, (tq, tq), 1)

        def bias_lanes(col, e, first):
            return _bias_lanes(col, lane, e % 2, first)

        def head_col(tile, e):
            return jnp.sum(jnp.where(lane == 2 * PP * hp + e, tile, 0.0), axis=1, keepdims=True)

        def tile_of(e):
            return slice(LANE * (e // 2), LANE * (e // 2 + 1))

        @pl.when(qi == 0)
        def _():
            def chunk(i, carry):
                r0 = pl.multiple_of(i * tq, tq)
                cb = c_ref[pl.ds(r0, tq), :]
                for e in range(NE):
                    kb, vb = k_ref[pl.ds(r0, tq), tile_of(e)], v_ref[pl.ds(r0, tq), tile_of(e)]
                    kaug_sc[e, pl.ds(r0, tq), :] = jnp.where(own[e % 2], kb, bias_lanes(-head_col(cb, e), e, False))
                    vaug_sc[e, pl.ds(r0, tq), :] = jnp.where(own[e % 2], vb, jnp.ones((tq, LANE), BF16))
                return carry
            lax.fori_loop(0, nq, chunk, 0)

        crow = c_ref[pl.ds(pl.multiple_of(qi * tq, tq), tq), :]
        ctq = [head_col(crow, e) for e in range(NE)]
        for e in range(NE):
            q = q_ref[:, tile_of(e)] * jnp.asarray(scale, BF16)
            qaug_sc[e] = jnp.where(own[e % 2], q, bias_lanes(ctq[e], e, True))
        m_sc[...] = jnp.full(m_sc.shape, NEG_INF, F32)
        acc_sc[...] = jnp.zeros(acc_sc.shape, F32)

        def scores(blk, slot, masked):
            k0 = pl.multiple_of(blk * tq, tq)
            for e in range(NE):
                s = _dot_nt(qaug_sc[e], kaug_sc[e, pl.ds(k0, tq), :])
                if masked:
                    s = jnp.where(rows >= cols, s, NEG_INF)
                s_sc[slot, e] = s
                mb_sc[slot, e] = jnp.broadcast_to(jnp.max(s, axis=1, keepdims=True), (tq, LANE))

        def accumulate(blk, slot):
            k0 = pl.multiple_of(blk * tq, tq)
            for e in range(NE):
                m_prev = m_sc[e]
                m_new = jnp.maximum(m_prev, mb_sc[slot, e])
                p = jnp.exp(s_sc[slot, e] - jnp.tile(m_new, (1, tq // LANE)))
                acc_sc[e] = jnp.exp(m_prev - m_new) * acc_sc[e] + _dot_nn(p.astype(BF16), vaug_sc[e, pl.ds(k0, tq), :])
                m_sc[e] = m_new

        def block_of(t):
            return jnp.where(t == 0, qi, t - 1)

        scores(qi, 0, True)

        def loop_body(t, carry):
            scores(t, (t + 1) % 2, False)
            accumulate(block_of(t), t % 2)
            return carry

        lax.fori_loop(0, qi, loop_body, 0)
        accumulate(block_of(qi), qi % 2)
        o_e, a_e = [], []
        for e in range(NE):
            acc = acc_sc[e]
            l = pltpu.roll(acc, HEAD_DIM, axis=1)
            o_e.append(acc / l)
            a_e.append(ctq[e] - (m_sc[e] + jnp.log(l)))
        for pp in range(PP):
            o = jnp.where(own[0], o_e[2 * pp], o_e[2 * pp + 1])
            g = g_ref[:, tile_of(2 * pp)]
            y_ref[:, tile_of(2 * pp)] = (o * (g * _sigmoid(g))).astype(BF16)
            o_ref[:, tile_of(2 * pp)] = o.astype(BF16)
            a_ref[pp] = jnp.where(own[0], a_e[2 * pp], a_e[2 * pp + 1])
        if na:
            @pl.when((hp == HP // PP - 1) & (qi == nq - 1))
            def _():
                _wait_all(remote, local)

    any_spec = pl.BlockSpec(memory_space=pl.ANY)
    sems = [pltpu.SemaphoreType.DMA((na, N_DEV - 1)), pltpu.SemaphoreType.DMA((na, N_DEV - 1)),
            pltpu.SemaphoreType.DMA((na,))] if na else []
    wide = PP * LANE
    outs = pl.pallas_call(
        body, name="fox_attn_fwd", grid=(HP // PP, nq),
        in_specs=[pl.BlockSpec((tq, wide), lambda h, i: (i, h)),
                  pl.BlockSpec((S, wide), lambda h, i: (0, wb // PP + h)),
                  pl.BlockSpec((S, wide), lambda h, i: (0, 2 * wb // PP + h)),
                  pl.BlockSpec((tq, wide), lambda h, i: (i, h)),
                  pl.BlockSpec((S, LANE), lambda h, i: (0, 0))] + [any_spec] * na,
        out_specs=[pl.BlockSpec((tq, wide), lambda h, i: (i, h)),
                   pl.BlockSpec((tq, wide), lambda h, i: (i, h)),
                   pl.BlockSpec((PP, tq, LANE), lambda h, i: (h, i, 0))] + [any_spec] * na,
        out_shape=[jax.ShapeDtypeStruct((S, W), BF16), jax.ShapeDtypeStruct((S, W), BF16),
                   jax.ShapeDtypeStruct((HP, S, LANE), F32)]
        + [jax.ShapeDtypeStruct((N_DEV,) + g.shape, g.dtype) for g in gather],
        scratch_shapes=[pltpu.VMEM((NE, S, LANE), BF16), pltpu.VMEM((NE, S, LANE), BF16),
                        pltpu.VMEM((NE, tq, LANE), BF16), pltpu.VMEM((2, NE, tq, tq), F32),
                        pltpu.VMEM((2, NE, tq, LANE), F32), pltpu.VMEM((NE, tq, LANE), F32),
                        pltpu.VMEM((NE, tq, LANE), F32)] + sems,
        compiler_params=_cparams(),
    )(qkv, qkv, qkv, gate, c, *gather)
    return outs[0], outs[1], outs[2], list(outs[3:])


def _fox_out_bwd(dxb, wo, qkv, gate, o, a, H):
    S, D = dxb.shape
    W = H * HEAD_DIM
    tm, tn = _row_tile(S, 512), _tile(W, 512)
    npair = tn // LANE
    scale = HEAD_DIM ** -0.5

    def body(dx_ref, w_ref, q_ref, g_ref, o_ref, a_ref, qa_ref, da_ref, dg_ref):
        dy = _dot_nt(dx_ref[...], w_ref[...])
        lane = lax.broadcasted_iota(jnp.int32, (tm, LANE), 1)
        own = [lane < HEAD_DIM, lane >= HEAD_DIM]
        for p in range(npair):
            cols = slice(LANE * p, LANE * (p + 1))
            q = q_ref[:, cols] * jnp.asarray(scale, BF16)
            dyv, g, ov, at = dy[:, cols], g_ref[:, cols], o_ref[:, cols].astype(F32), a_ref[p]
            sg = _sigmoid(g)
            dob = (dyv * (g * sg)).astype(BF16)
            dg_ref[:, cols] = (dyv * ov * (sg * (1.0 + g * (1.0 - sg)))).astype(BF16)
            prod = dob.astype(F32) * ov
            for e in range(2):
                a_col = jnp.max(jnp.where(own[e], at, -jnp.inf), axis=1, keepdims=True)
                d_col = jnp.sum(jnp.where(own[e], prod, 0.0), axis=1, keepdims=True)
                qa_ref[e, :, cols] = jnp.where(own[e], q, _bias_lanes(a_col, lane, e, True))
                da_ref[e, :, cols] = jnp.where(own[e], dob, _bias_lanes(-d_col, lane, e, True))

    blk = pl.BlockSpec((tm, tn), lambda i, j: (i, j))
    pair = pl.BlockSpec((2, tm, tn), lambda i, j: (0, i, j))
    return pl.pallas_call(
        body, name="fox_out_bwd", grid=(S // tm, W // tn),
        in_specs=[pl.BlockSpec((tm, D), lambda i, j: (i, 0)), pl.BlockSpec((tn, D), lambda i, j: (j, 0)),
                  blk, blk, blk, pl.BlockSpec((npair, tm, LANE), lambda i, j: (j, i, 0))],
        out_specs=[pair, pair, blk],
        out_shape=[jax.ShapeDtypeStruct((2, S, W), BF16), jax.ShapeDtypeStruct((2, S, W), BF16),
                   jax.ShapeDtypeStruct((S, W), BF16)],
        compiler_params=_cparams(),
    )(dxb, wo, qkv, gate, o, a)


def _fox_bwd(qaug, doaug, qkv, c, H, scatter=(), scatter_specs=()):
    na = len(scatter)
    S = qkv.shape[0]
    W = H * HEAD_DIM
    HP = H // 2
    tq = _row_tile(S, 512)
    nq = S // tq
    wb = W // LANE
    scale = HEAD_DIM ** -0.5

    def body(*refs):
        qa_ref, da_ref, k_ref, v_ref, c_ref = refs[:5]
        out_ref, dcr_ref, dcc_ref = refs[5 + na:8 + na]
        dq_sc, dk_sc, dv_sc = refs[8 + 2 * na:11 + 2 * na]
        hp, kj = pl.program_id(0), pl.program_id(1)
        if na:
            remote = _direct_scatter_copies(refs[5:5 + na], refs[8 + na:8 + 2 * na], scatter_specs,
                                            *refs[11 + 2 * na:])

            @pl.when((hp == 0) & (kj == 0))
            def _():
                for cp in remote:
                    cp.start()
        lane = lax.broadcasted_iota(jnp.int32, (tq, LANE), 1)
        own = [lane < HEAD_DIM, lane >= HEAD_DIM]
        rows = lax.broadcasted_iota(jnp.int32, (tq, tq), 0)
        cols = lax.broadcasted_iota(jnp.int32, (tq, tq), 1)

        @pl.when(kj == 0)
        def _():
            dq_sc[...] = jnp.zeros(dq_sc.shape, F32)

        @pl.when((kj == 0) & (hp == 0))
        def _():
            dcr_ref[...] = jnp.zeros(dcr_ref.shape, F32)
            dcc_ref[...] = jnp.zeros(dcc_ref.shape, F32)

        kblk, vblk, cblk = k_ref[...], v_ref[...], c_ref[...]
        one, zero = jnp.ones((tq, LANE), BF16), jnp.zeros((tq, LANE), BF16)
        ka, va = [], []
        for e in range(2):
            o0 = HEAD_DIM * (1 - e)
            c_col = jnp.sum(jnp.where(lane == 2 * hp + e, cblk, 0.0), axis=1, keepdims=True)
            ka.append(jnp.where(own[e], kblk, _bias_lanes(-c_col, lane, e, False)))
            va.append(jnp.where(own[e], vblk, jnp.where((lane >= o0) & (lane < o0 + 3), one, zero)))
        dk_sc[...] = jnp.zeros(dk_sc.shape, F32)
        dv_sc[...] = jnp.zeros(dv_sc.shape, F32)

        def step(i, masked):
            r0 = pl.multiple_of(i * tq, tq)
            for e in range(2):
                qa = qa_ref[e, pl.ds(r0, tq), :]
                da = da_ref[e, pl.ds(r0, tq), :]
                p = jnp.exp(_dot_nt(qa, ka[e]))
                if masked:
                    p = jnp.where(rows >= cols, p, 0.0)
                ds = p * _dot_nt(da, va[e])
                pb, dsb = p.astype(BF16), ds.astype(BF16)
                dv_sc[e] += _dot_tn(pb, da)
                dk_sc[e] += _dot_tn(dsb, qa)
                dq_sc[e, pl.ds(r0, tq), :] += _dot_nn(dsb, ka[e])

        step(kj, True)

        def loop_body(i, carry):
            step(i, False)
            return carry

        lax.fori_loop(kj + 1, nq, loop_body, 0)
        k0 = pl.multiple_of(kj * tq, tq)
        out_ref[1, pl.ds(k0, tq), :] = jnp.where(own[0], dk_sc[0], dk_sc[1]).astype(BF16)
        out_ref[2, pl.ds(k0, tq), :] = jnp.where(own[0], dv_sc[0], dv_sc[1]).astype(BF16)

        def put_lane(ref, r0, e, tile, src_lane):
            col = jnp.sum(jnp.where(lane == src_lane, tile, 0.0), axis=1, keepdims=True)
            ref[pl.ds(r0, tq), :] = jnp.where(lane == 2 * hp + e, col, ref[pl.ds(r0, tq), :])

        for e in range(2):
            put_lane(dcc_ref, k0, e, dk_sc[e], HEAD_DIM * (1 - e) + 3)

        @pl.when(kj == nq - 1)
        def _():
            def chunk(i, carry):
                r0 = pl.multiple_of(i * tq, tq)
                d0, d1 = dq_sc[0, pl.ds(r0, tq), :], dq_sc[1, pl.ds(r0, tq), :]
                out_ref[0, pl.ds(r0, tq), :] = (jnp.where(own[0], d0, d1) * scale).astype(BF16)
                put_lane(dcr_ref, r0, 0, d0, HEAD_DIM)
                put_lane(dcr_ref, r0, 1, d1, 0)
                return carry
            lax.fori_loop(0, nq, chunk, 0)

        if na:
            @pl.when((hp == HP - 1) & (kj == nq - 1))
            def _():
                _wait_all(remote)

    pair = pl.BlockSpec((2, S, LANE), lambda h, j: (0, 0, h))
    vec = pl.BlockSpec((S, LANE), lambda h, j: (0, 0))
    any_spec = pl.BlockSpec(memory_space=pl.ANY)
    sems = [pltpu.SemaphoreType.DMA((na, N_DEV - 1)), pltpu.SemaphoreType.DMA((na, N_DEV - 1))] if na else []
    outs = pl.pallas_call(
        body, name="fox_attn_bwd", grid=(HP, nq),
        in_specs=[pair, pair,
                  pl.BlockSpec((tq, LANE), lambda h, j: (j, wb + h)),
                  pl.BlockSpec((tq, LANE), lambda h, j: (j, 2 * wb + h)),
                  pl.BlockSpec((tq, LANE), lambda h, j: (j, 0))] + [any_spec] * na,
        out_specs=[pl.BlockSpec((3, S, LANE), lambda h, j: (0, 0, h)), vec, vec] + [any_spec] * na,
        out_shape=[jax.ShapeDtypeStruct((3, S, W), BF16), jax.ShapeDtypeStruct((S, LANE), F32),
                   jax.ShapeDtypeStruct((S, LANE), F32)]
        + [jax.ShapeDtypeStruct((N_DEV - 1,) + _scatter_block_shape(g, s), g.dtype)
           for g, s in zip(scatter, scatter_specs)],
        scratch_shapes=[pltpu.VMEM((2, S, LANE), F32), pltpu.VMEM((2, tq, LANE), F32),
                        pltpu.VMEM((2, tq, LANE), F32)] + sems,
        compiler_params=_cparams(),
    )(qaug, doaug, qkv, qkv, c, *scatter)
    return outs[0], outs[1], outs[2], list(outs[3:])


def _swa_pick(blk, half, lane):
    b = blk.astype(F32)
    r = pltpu.roll(b, HEAD_DIM, axis=1)
    return jnp.where(jnp.logical_xor(lane < HEAD_DIM, half == 1), b, r).astype(BF16)


def _swa_stack(t, lane, G):
    pieces = []
    z = jnp.zeros((SWA_BLOCK, LANE), t.dtype)
    for j in range(G // 2):
        tile = t[:, LANE * j:LANE * (j + 1)]
        pieces += [jnp.where(lane < HEAD_DIM, tile, z), jnp.where(lane < HEAD_DIM, z, tile)]
    return jnp.concatenate(pieces, axis=0)


def _swa_unstack(st, lane, G):
    tiles = []
    for j in range(G // 2):
        a = st[2 * j * SWA_BLOCK:(2 * j + 1) * SWA_BLOCK]
        b = st[(2 * j + 1) * SWA_BLOCK:(2 * j + 2) * SWA_BLOCK]
        tiles.append(jnp.where(lane < HEAD_DIM, a, b))
    return jnp.concatenate(tiles, axis=1)


def _swa_mask_bias(G):
    R = G * SWA_BLOCK
    t_loc = jnp.arange(R)[:, None] % SWA_BLOCK
    j_loc = jnp.arange(2 * SWA_BLOCK)[None, :]
    diff = t_loc + SWA_BLOCK - j_loc
    band = (diff >= 0) & (diff < SWA_BLOCK)
    return jnp.stack([jnp.where(band & (j_loc >= SWA_BLOCK), 0.0, NEG_INF),
                      jnp.where(band, 0.0, NEG_INF)]).astype(F32)


def _swa_scores(q, kp, kc, vp, vc, srow, bias, half, head0, G):
    lane = lax.broadcasted_iota(jnp.int32, (SWA_BLOCK, LANE), 1)
    kk = jnp.concatenate([_swa_pick(kp, half, lane), _swa_pick(kc, half, lane)], axis=0)
    vv = jnp.concatenate([_swa_pick(vp, half, lane), _swa_pick(vc, half, lane)], axis=0)
    qstack = _swa_stack(q, lane, G) * jnp.asarray(HEAD_DIM ** -0.5, BF16)
    s = _dot_nt(qstack, kk) + bias
    R = G * SWA_BLOCK
    lane1 = lax.broadcasted_iota(jnp.int32, (1, LANE), 1)
    sink = jnp.concatenate(
        [jnp.broadcast_to(jnp.sum(jnp.where(lane1 == head0 + g, srow, 0.0), axis=1, keepdims=True), (SWA_BLOCK, LANE))
         for g in range(G)], axis=0)
    m = jnp.maximum(jnp.broadcast_to(jnp.max(s, axis=1, keepdims=True), (R, LANE)), sink)
    e = jnp.exp(s - jnp.tile(m, (1, 2)))
    es = jnp.exp(sink - m)
    inv = 1.0 / (jnp.broadcast_to(jnp.sum(e, axis=1, keepdims=True), (R, LANE)) + es)
    return qstack, kk, vv, e * jnp.tile(inv, (1, 2)), es * inv, lane


def _swa_fwd(qkv, gate, sinks, mask_bias, HQ, HKV):
    S = qkv.shape[0]
    G = HQ // HKV
    WQ, KVW = HQ * HEAD_DIM, HKV * HEAD_DIM
    nb = S // SWA_BLOCK
    GW = G * HEAD_DIM
    kb, vb = WQ // LANE, (WQ + KVW) // LANE

    def body(q_ref, kp_ref, kc_ref, vp_ref, vc_ref, g_ref, sink_ref, b_ref, y_ref, o_ref):
        pair = pl.program_id(0)
        for half in range(2):
            cols = slice(GW * half, GW * (half + 1))
            _, _, vv, p, _, lane = _swa_scores(q_ref[:, cols], kp_ref[...], kc_ref[...], vp_ref[...], vc_ref[...],
                                               sink_ref[...], b_ref[0], half, (2 * pair + half) * G, G)
            o = _swa_unstack(_dot_nn(p.astype(BF16), vv), lane, G)
            g = g_ref[:, cols]
            y_ref[:, cols] = (o * (g * _sigmoid(g))).astype(BF16)
            o_ref[:, cols] = o.astype(BF16)

    blk = lambda cb, prev: pl.BlockSpec(
        (SWA_BLOCK, LANE), lambda h, n, cb=cb, prev=prev: (jnp.maximum(n - prev, 0), cb + h))
    qspec = pl.BlockSpec((SWA_BLOCK, 2 * GW), lambda h, n: (n, h))
    return pl.pallas_call(
        body, name="swa_attn_fwd", grid=(HKV // 2, nb),
        in_specs=[qspec, blk(kb, 1), blk(kb, 0), blk(vb, 1), blk(vb, 0), qspec,
                  pl.BlockSpec((1, LANE), lambda h, n: (0, 0)),
                  pl.BlockSpec((1, G * SWA_BLOCK, 2 * SWA_BLOCK), lambda h, n: (jnp.minimum(n, 1), 0, 0))],
        out_specs=[qspec, qspec],
        out_shape=[jax.ShapeDtypeStruct((S, WQ), BF16), jax.ShapeDtypeStruct((S, WQ), BF16)],
        compiler_params=_cparams(),
    )(qkv, qkv, qkv, qkv, qkv, gate, sinks, mask_bias)


def _swa_bwd(qkv, dy, gate, o, sinks, tables, mask_bias, HQ, HKV):
    S = qkv.shape[0]
    G = HQ // HKV
    WQ, KVW = HQ * HEAD_DIM, HKV * HEAD_DIM
    nb = S // SWA_BLOCK
    GW = G * HEAD_DIM
    R = G * SWA_BLOCK
    kb, vb = WQ // LANE, (WQ + KVW) // LANE
    scale = HEAD_DIM ** -0.5
    assert G == 8

    def body(q_ref, kp_ref, kc_ref, vp_ref, vc_ref, dy_ref, g_ref, o_ref, sink_ref, t_ref, b_ref,
             dqg_ref, dkv_ref, dsink_ref, carry_sc):
        pair, n = pl.program_id(0), pl.program_id(1)

        @pl.when(n == 0)
        def _():
            carry_sc[...] = jnp.zeros(carry_sc.shape, F32)
            dsink_ref[...] = jnp.zeros(dsink_ref.shape, F32)

        @pl.when(n < nb)
        def _():
            t0, t1, t2 = (jnp.tile(t_ref[i], (1, GW // LANE)) for i in range(3))
            for half in range(2):
                cols = slice(GW * half, GW * (half + 1))
                qstack, kk, vv, p, psink, lane = _swa_scores(
                    q_ref[:, cols], kp_ref[...], kc_ref[...], vp_ref[...], vc_ref[...], sink_ref[...], b_ref[0],
                    half, (2 * pair + half) * G, G)
                dyv, g, ov = dy_ref[:, cols], g_ref[:, cols], o_ref[:, cols].astype(F32)
                sg = _sigmoid(g)
                dob = (dyv * (g * sg)).astype(BF16)
                dqg_ref[1, :, cols] = (dyv * ov * (sg * (1.0 + g * (1.0 - sg)))).astype(BF16)
                prod = dob.astype(F32) * ov
                dparts = []
                for j in range(G // 2):
                    tile = prod[:, LANE * j:LANE * (j + 1)]
                    for sel in (jnp.where(lane < HEAD_DIM, tile, 0.0), jnp.where(lane < HEAD_DIM, 0.0, tile)):
                        dparts.append(jnp.broadcast_to(jnp.sum(sel, axis=1, keepdims=True), (SWA_BLOCK, LANE)))
                delta = jnp.concatenate(dparts, axis=0)
                dostack = _swa_stack(dob, lane, G)
                ds = p * (_dot_nt(dostack, vv) - jnp.tile(delta, (1, 2)))
                dsb, pb = ds.astype(BF16), p.astype(BF16)
                dq = _swa_unstack(_dot_nn(dsb, kk), lane, G) * scale
                dq = dq * t0 + pltpu.roll(dq * t1, ROT_DIM // 2, axis=1) + pltpu.roll(dq * t2, GW - ROT_DIM // 2, axis=1)
                dqg_ref[0, :, cols] = dq.astype(BF16)
                dkk = _dot_tn(dsb, qstack)
                dvv = _dot_tn(pb, dostack)
                dkk = dkk + pltpu.roll(dkk, HEAD_DIM, axis=1)
                dvv = dvv + pltpu.roll(dvv, HEAD_DIM, axis=1)
                lane2 = lax.broadcasted_iota(jnp.int32, (2 * SWA_BLOCK, LANE), 1)
                comb = jnp.where(lane2 < HEAD_DIM, dkk, dvv)
                dkv_ref[half] = carry_sc[half] + comb[:SWA_BLOCK]
                carry_sc[half] = comb[SWA_BLOCK:]
                sk = psink * delta
                rows = [-jnp.sum(sk[g_ * SWA_BLOCK:(g_ + 1) * SWA_BLOCK], axis=0, keepdims=True) for g_ in range(G)]
                dsink_ref[half] += jnp.concatenate(rows, axis=0)

        @pl.when(n == nb)
        def _():
            dkv_ref[...] = carry_sc[...]

    cl = lambda n: jnp.minimum(n, nb - 1)
    blk = lambda cb, prev: pl.BlockSpec(
        (SWA_BLOCK, LANE), lambda h, n, cb=cb, prev=prev: (jnp.maximum(cl(n) - prev, 0), cb + h))
    qspec = pl.BlockSpec((SWA_BLOCK, 2 * GW), lambda h, n: (cl(n), h))
    return pl.pallas_call(
        body, name="swa_attn_bwd", grid=(HKV // 2, nb + 1),
        in_specs=[qspec, blk(kb, 1), blk(kb, 0), blk(vb, 1), blk(vb, 0), qspec, qspec, qspec,
                  pl.BlockSpec((1, LANE), lambda h, n: (0, 0)),
                  pl.BlockSpec((3, SWA_BLOCK, LANE), lambda h, n: (0, cl(n), 0)),
                  pl.BlockSpec((1, R, 2 * SWA_BLOCK), lambda h, n: (jnp.minimum(n, 1), 0, 0))],
        out_specs=[pl.BlockSpec((2, SWA_BLOCK, 2 * GW), lambda h, n: (0, cl(n), h)),
                   pl.BlockSpec((2, SWA_BLOCK, LANE), lambda h, n: (h, jnp.maximum(n - 1, 0), 0)),
                   pl.BlockSpec((2, 8, LANE), lambda h, n: (h, 0, 0))],
        out_shape=[jax.ShapeDtypeStruct((2, S, WQ), BF16), jax.ShapeDtypeStruct((HKV, S, LANE), F32),
                   jax.ShapeDtypeStruct((HKV, 8, LANE), F32)],
        scratch_shapes=[pltpu.VMEM((2, SWA_BLOCK, LANE), F32)],
        compiler_params=_cparams(),
    )(qkv, qkv, qkv, qkv, qkv, dy, gate, o, sinks, tables, mask_bias)


def _swa_dkv_finish(dkv, tables):
    HKV, S, _ = dkv.shape
    KVW = HKV * HEAD_DIM
    tm = _row_tile(S, 512)
    npair = HKV // 2

    def body(d_ref, t_ref, o_ref):
        lane = lax.broadcasted_iota(jnp.int32, (tm, LANE), 1)
        lo = lane < HEAD_DIM
        for p in range(npair):
            a, b = d_ref[2 * p], d_ref[2 * p + 1]
            tk = jnp.where(lo, a, pltpu.roll(b, HEAD_DIM, axis=1))
            tv = jnp.where(lo, pltpu.roll(a, HEAD_DIM, axis=1), b)
            tk = (tk * t_ref[0] + pltpu.roll(tk * t_ref[1], ROT_DIM // 2, axis=1)
                  + pltpu.roll(tk * t_ref[2], LANE - ROT_DIM // 2, axis=1))
            o_ref[:, LANE * p:LANE * (p + 1)] = tk.astype(BF16)
            o_ref[:, KVW + LANE * p:KVW + LANE * (p + 1)] = tv.astype(BF16)

    return pl.pallas_call(
        body, name="swa_dkv_finish", grid=(S // tm,),
        in_specs=[pl.BlockSpec((HKV, tm, LANE), lambda i: (0, i, 0)), pl.BlockSpec((3, tm, LANE), lambda i: (0, i, 0))],
        out_specs=pl.BlockSpec((tm, 2 * KVW), lambda i: (i, 0)),
        out_shape=jax.ShapeDtypeStruct((S, 2 * KVW), BF16),
        compiler_params=_cparams(),
    )(dkv, tables)


def _rope_tables(S, width):
    half = ROT_DIM // 2
    pos = jnp.arange(S, dtype=F32)
    inv_freq = ROPE_THETA ** (-jnp.arange(half, dtype=F32) / half)
    ang = pos[:, None] * inv_freq[None, :]
    cos, sin = jnp.cos(ang), jnp.sin(ang)
    one = jnp.ones((S, HEAD_DIM - ROT_DIM), F32)
    zero = jnp.zeros((S, HEAD_DIM - ROT_DIM), F32)
    zh = jnp.zeros((S, half), F32)
    t0 = jnp.concatenate([cos, cos, one], axis=1)
    t1 = jnp.concatenate([-sin, zh, zero], axis=1)
    t2 = jnp.concatenate([zh, sin, zero], axis=1)
    return jnp.stack([jnp.tile(t, (1, width // HEAD_DIM)) for t in (t0, t1, t2)])


def _pad_rows(v, row, total_rows=8):
    return jnp.pad(v, ((row, total_rows - row - v.shape[0]), (0, 0)))


def _pad_lanes(v, off, width):
    return jnp.pad(v, ((0, 0), (off, width - off - v.shape[1])))


def kernel(x, norm_g, fox_w_in, fox_b_f, fox_w_out, swa_w_in, swa_sinks, swa_w_out, final_g, loss_target, m_norm_g, m_fox_w_in, m_fox_b_f, m_fox_w_out, m_swa_w_in, m_swa_sinks, m_swa_w_out, m_final_g, v_norm_g, v_fox_w_in, v_fox_b_f, v_fox_w_out, v_swa_w_in, v_swa_sinks, v_swa_w_out, v_final_g):
    S, D = x.shape[1], x.shape[2]
    H = fox_b_f.shape[1]
    W = H * HEAD_DIM
    wf = fox_w_in.shape[2]
    ws = swa_w_in.shape[2]
    HQ = swa_sinks.shape[1]
    WQ = HQ * HEAD_DIM
    KVW = (ws * N_DEV - 2 * WQ) // 2
    HKV = KVW // HEAD_DIM
    rows_o = fox_w_out.shape[1]
    assert wf * N_DEV == 4 * W + H and rows_o * N_DEV == W and H <= LANE and HQ <= LANE
    me = _my_index()

    _, sw_f, np_f = _slab_geom(wf)
    _, sw_s, np_s = _slab_geom(ws)

    def slab(w2d, w, sw):
        off = (w * me) % LANE
        return lax.dynamic_update_slice(jnp.zeros((w2d.shape[0], sw), BF16), w2d.astype(BF16), (0, off))

    (fi_all,) = _all_gather([slab(fox_w_in[0], wf, sw_f)])
    w_fi = _assemble(fi_all, wf)
    later = [slab(swa_w_in[0], ws, sw_s), fox_w_out[0].astype(BF16), swa_w_out[0].astype(BF16)]

    x0 = x[0]
    g0, g1, gf = norm_g[0:1], norm_g[1:2], final_g[None, :]
    bias = _pad_lanes(fox_b_f, 0, LANE)
    sinks = _pad_lanes(swa_sinks, 0, LANE)
    tab_k = _rope_tables(S, LANE)
    mask_bias = _swa_mask_bias(HQ // HKV)

    h0 = _rmsnorm_fwd(x0, g0, "rmsnorm0")
    qkv0 = _proj(h0, w_fi, 0, 3 * W, BF16, "fox_in_qkv")
    gate0 = _proj(h0, w_fi, 3 * W, W, F32, "fox_in_gate")
    fl = _proj(h0, w_fi, 4 * W, LANE, F32, "fox_in_f")
    c = _fox_gate_fwd(fl, bias)
    y0, o0, a0, (si_all, fo_all, so_all) = _fox_fwd(qkv0, gate0, c, H, gather=later)
    w_si = _assemble(si_all, ws)
    w_fo = fo_all.reshape(W, D)
    w_so = so_all.reshape(WQ, D)
    x1 = _out_proj_res(y0, w_fo, x0, "fox_out")

    h1 = _rmsnorm_fwd(x1, g1, "rmsnorm1")
    qkv1 = _proj(h1, w_si, 0, WQ + 2 * KVW, BF16, "swa_in_qkv", rope=(tab_k, WQ + KVW))
    gate1 = _proj(h1, w_si, WQ + 2 * KVW, WQ, F32, "swa_in_gate")
    y1, o1 = _swa_fwd(qkv1, gate1, sinks, mask_bias, HQ, HKV)
    x2 = _out_proj_res(y1, w_so, x1, "swa_out")

    dx2, dx2b, dgf, loss_p = _loss_head(x2, loss_target[0], gf)

    dy1 = _matmul_nt([(dx2b, None, 0)], w_so, WQ, "swa_out_bwd")
    g_so, g_so_h = _matmul_tn(y1, [(dx2b, None, 0)], D, "swa_out_wgrad", also_bf16=True)
    dqg1, dkv1, dsink = _swa_bwd(qkv1, dy1, gate1, o1, sinks, tab_k, mask_bias, HQ, HKV)
    dkv1f = _swa_dkv_finish(dkv1, tab_k)
    parts1 = [(dqg1, 0, 0), (dkv1f, None, WQ), (dqg1, 1, WQ + 2 * KVW)]
    g_si, g_si_h = _matmul_tn(h1, parts1, np_s, "swa_in_wgrad", tile_major=True, also_bf16=True)
    dh1 = _matmul_nt(parts1, w_si, D, "swa_in_bwd")
    dx1, dx1b, dg1 = _rmsnorm_bwd(dh1, x1, g1, dx2, "rmsnorm1_bwd")

    qaug0, doaug0, dgate0 = _fox_out_bwd(dx1b, w_fo, qkv0, gate0, o0, a0, H)
    g_fo, g_fo_h = _matmul_tn(y0, [(dx1b, None, 0)], D, "fox_out_wgrad", also_bf16=True)
    early_specs = [("col", ws), ("row", rows_o), ("row", rows_o)]
    dqkv0, dcr, dcc, early_recv = _fox_bwd(qaug0, doaug0, qkv0, c, H, scatter=[g_si_h, g_fo_h, g_so_h],
                                          scatter_specs=early_specs)
    dfl, dbf = _fox_gate_bwd(fl, bias, dcr - dcc)
    parts0 = [(dqkv0, p, p * W) for p in range(3)] + [(dgate0, None, 3 * W), (dfl, None, 4 * W)]
    g_fi, g_fi_h = _matmul_tn(h0, parts0, np_f, "fox_in_wgrad", tile_major=True, also_bf16=True)
    spec_fi = ("col", wf)
    dh0, (recv_fi,) = _matmul_nt(parts0, w_fi, D, "fox_in_bwd", scatter=[g_fi_h], scatter_specs=[spec_fi])
    dx0, _, dg0 = _rmsnorm_bwd(dh0, x0, g0, dx1, "rmsnorm0_bwd")

    red_si, gw_fo, gw_so = [_final_sum8(g_, r_, s_)
                            for g_, r_, s_ in zip([g_si, g_fo, g_so], early_recv, early_specs)]
    red_fi = _final_sum8(g_fi, recv_fi, spec_fi)
    gw_fi = lax.dynamic_slice(red_fi, (0, (wf * me) % LANE), (D, wf))
    gw_si = lax.dynamic_slice(red_si, (0, (ws * me) % LANE), (D, ws))

    P = D
    dsink_v = dsink[:, :, 0].reshape(1, HQ)
    row3 = _pad_lanes(dbf[:, :H], 0, P) + _pad_lanes(dsink_v, LANE, P) + _pad_lanes(loss_p[:, :1], 2 * LANE, P)
    pack = _pad_rows(dg0, 0) + _pad_rows(dg1, 1) + _pad_rows(dgf, 2) + _pad_rows(row3, 3)
    tot = _all_reduce_small(pack)
    loss = tot[3, 2 * LANE]
    g_norm = tot[0:2]
    g_final = tot[2]
    g_bf = tot[3:4, 0:H]
    g_sinks = tot[3:4, LANE:LANE + HQ]

    def small_pack(ng, fg, bf, sk):
        r3 = _pad_lanes(bf, 0, P) + _pad_lanes(sk, LANE, P)
        return _pad_rows(ng, 0) + _pad_rows(fg[None, :], 2) + _pad_rows(r3, 3)

    sd, sm, sv = _adamw(small_pack(norm_g, final_g, fox_b_f, swa_sinks), tot,
                        small_pack(m_norm_g, m_final_g, m_fox_b_f, m_swa_sinks),
                        small_pack(v_norm_g, v_final_g, v_fox_b_f, v_swa_sinks), "adamw_small")

    def unpack(t):
        return t[0:2], t[3:4, 0:H], t[3:4, LANE:LANE + HQ], t[2]

    d_fi, m_fi, v_fi = _adamw(fox_w_in[0], gw_fi, m_fox_w_in[0], v_fox_w_in[0], "adamw_fox_in")
    d_fo, m_fo, v_fo = _adamw(fox_w_out[0], gw_fo, m_fox_w_out[0], v_fox_w_out[0], "adamw_fox_out")
    d_si, m_si, v_si = _adamw(swa_w_in[0], gw_si, m_swa_w_in[0], v_swa_w_in[0], "adamw_swa_in")
    d_so, m_so, v_so = _adamw(swa_w_out[0], gw_so, m_swa_w_out[0], v_swa_w_out[0], "adamw_swa_out")

    def group(small, fi, fo, si, so):
        ng, bf, sk, fg = unpack(small)
        return (ng, fi[None], bf, fo[None], si[None], sk, so[None], fg)

    grads = (g_norm, gw_fi[None], g_bf, gw_fo[None], gw_si[None], g_sinks, gw_so[None], g_final)
    return (loss, dx0[None], *grads, *group(sd, d_fi, d_fo, d_si, d_so),
            *group(sm, m_fi, m_fo, m_si, m_so), *group(sv, v_fi, v_fo, v_si, v_so))
```

```python
import math

import jax
import jax.numpy as jnp
from jax import lax
from jax.experimental import pallas as pl
from jax.experimental.pallas import tpu as pltpu

F32 = jnp.float32
BF16 = jnp.bfloat16
MESH = pl.DeviceIdType.MESH

N_DEV = 8
LANE = 128
HEAD_DIM = 64
SWA_BLOCK = 128
NEG_INF = -1e30
RMS_EPS = 1e-6
ROPE_THETA = 500000.0
ROT_DIM = HEAD_DIM // 4
ADAM_LR, ADAM_B1, ADAM_B2, ADAM_EPS, ADAM_WD, ADAM_STEP = 0.001, 0.9, 0.999, 1e-08, 0.01, 10
VMEM_LIMIT = 56 * 1024 * 1024
MM_TILE = 1024


def _cparams(**kw):
    return pltpu.CompilerParams(vmem_limit_bytes=VMEM_LIMIT, **kw)


def _tile(n, cap):
    if n <= cap:
        return n
    t = (cap // LANE) * LANE
    while t > LANE and n % t:
        t -= LANE
    assert n % t == 0, (n, cap)
    return t


def _row_tile(n, cap):
    t = min(n, cap)
    while n % t:
        t //= 2
    return t


def _dot_nn(a, b):
    return jnp.dot(a, b, preferred_element_type=F32)


def _dot_nt(a, b):
    return lax.dot_general(a, b, (((1,), (1,)), ((), ())), preferred_element_type=F32)


def _dot_tn(a, b):
    return lax.dot_general(a, b, (((0,), (0,)), ((), ())), preferred_element_type=F32)


def _sigmoid(g):
    return 1.0 / (1.0 + jnp.exp(-g))


def _slab_geom(w):
    starts = [w * i for i in range(N_DEV)]
    aligned = [LANE * (s // LANE) for s in starts]
    offs = [s - a for s, a in zip(starts, aligned)]
    sw = LANE * (-(-(max(offs) + w) // LANE))
    return aligned, sw, aligned[-1] + sw


def _my_index():
    return 4 * lax.axis_index("x") + 2 * lax.axis_index("y") + lax.axis_index("c")


def _all_gather(arrs):
    n = len(arrs)

    def body(*refs):
        ins, outs = refs[:n], refs[n:2 * n]
        send_sems, recv_sems, local_sems = refs[2 * n:]
        x, y, c = lax.axis_index("x"), lax.axis_index("y"), lax.axis_index("c")
        me, sib = (x, y, c), (x, y, 1 - c)
        chips = [(1 - x, y), (x, 1 - y), (1 - x, 1 - y)]

        def idx(px, py, pc):
            return 4 * px + 2 * py + pc

        def copy(a, k, block, to, src=None):
            dst = outs[a].at[idx(*block)]
            return pltpu.make_async_remote_copy(
                src_ref=dst if src is None else src, dst_ref=dst,
                send_sem=send_sems.at[a, k], recv_sem=recv_sems.at[a, k],
                device_id=to, device_id_type=MESH)

        mine = [pltpu.make_async_copy(ins[a], outs[a].at[idx(*me)], local_sems.at[a]) for a in range(n)]
        for m in mine:
            m.start()
        first = []
        for a in range(n):
            first.append(copy(a, 0, me, sib, src=ins[a]))
            for j, chip in enumerate(chips):
                first.append(copy(a, 1 + j, me, (*chip, c), src=ins[a]))
        for cp in first:
            cp.start()
        passed = []
        for j, chip in enumerate(chips):
            for a in range(n):
                copy(a, 1 + j, (*chip, c), me).wait_recv()
                p = copy(a, 4 + j, (*chip, c), sib)
                p.start()
                passed.append(p)
        for a in range(n):
            copy(a, 0, sib, me).wait_recv()
        for j, chip in enumerate(chips):
            for a in range(n):
                copy(a, 4 + j, (*chip, 1 - c), me).wait_recv()
        for cp in first + passed:
            cp.wait_send()
        for m in mine:
            m.wait()

    any_spec = pl.BlockSpec(memory_space=pl.ANY)
    return pl.pallas_call(
        body, name="weights_all_gather",
        out_shape=[jax.ShapeDtypeStruct((N_DEV,) + a.shape, a.dtype) for a in arrs],
        in_specs=[any_spec] * n, out_specs=[any_spec] * n,
        scratch_shapes=[pltpu.SemaphoreType.DMA((n, 7)), pltpu.SemaphoreType.DMA((n, 7)),
                        pltpu.SemaphoreType.DMA((n,))],
    )(*arrs)


def _rs_windows(specs):
    def window(ref, spec, blk):
        kind, n = spec
        if kind == "col":
            _, sw, _ = _slab_geom(n)
            return ref.at[pl.ds((n * blk) // LANE, sw // LANE)]
        start = pl.multiple_of(n * blk, n)
        return ref.at[pl.ds(start, n), :]
    return window


def _peer(k):
    x, y, c = lax.axis_index("x"), lax.axis_index("y"), lax.axis_index("c")
    return (x ^ (k >> 2), y ^ ((k >> 1) & 1), c ^ (k & 1))


def _direct_gather_copies(ins, outs, send_sems, recv_sems, local_sems):
    me = _my_index()
    remote, local = [], []
    for a, (src, dst) in enumerate(zip(ins, outs)):
        local.append(pltpu.make_async_copy(src, dst.at[me], local_sems.at[a]))
        for k in range(1, N_DEV):
            remote.append(pltpu.make_async_remote_copy(
                src_ref=src, dst_ref=dst.at[me], send_sem=send_sems.at[a, k - 1], recv_sem=recv_sems.at[a, k - 1],
                device_id=_peer(k), device_id_type=MESH))
    return remote, local


def _direct_scatter_copies(ins, outs, specs, send_sems, recv_sems):
    window = _rs_windows(specs)
    remote = []
    for a, (src, dst) in enumerate(zip(ins, outs)):
        for k in range(1, N_DEV):
            px, py, pc = _peer(k)
            remote.append(pltpu.make_async_remote_copy(
                src_ref=window(src, specs[a], 4 * px + 2 * py + pc), dst_ref=dst.at[k - 1],
                send_sem=send_sems.at[a, k - 1], recv_sem=recv_sems.at[a, k - 1],
                device_id=(px, py, pc), device_id_type=MESH))
    return remote


def _scatter_block_shape(g, spec):
    kind, w = spec
    return (_slab_geom(w)[1] // LANE, g.shape[1], LANE) if kind == "col" else (w, g.shape[1])


def _wait_all(remote, local=()):
    for cp in remote:
        cp.wait_recv()
    for cp in remote:
        cp.wait_send()
    for cp in local:
        cp.wait()


def _final_sum8(g, recv, spec):
    kind, n = spec
    me = _my_index()
    offs = jnp.stack([(n * me) // LANE if kind == "col" else me]).astype(jnp.int32)
    if kind == "col":
        _, T, M, _ = recv.shape
        grid = (T,)
        in_specs = [pl.BlockSpec((1, M, LANE), lambda t, o: (o[0] + t, 0, 0)),
                    pl.BlockSpec((N_DEV - 1, 1, M, LANE), lambda t, o: (0, t, 0, 0))]
        out_spec = pl.BlockSpec((M, LANE), lambda t, o: (0, t))
        out_shape = jax.ShapeDtypeStruct((M, T * LANE), F32)
    else:
        _, nrow, C = recv.shape
        grid = (1,)
        in_specs = [pl.BlockSpec((nrow, C), lambda t, o: (o[0], 0)),
                    pl.BlockSpec((N_DEV - 1, nrow, C), lambda t, o: (0, 0, 0))]
        out_spec = pl.BlockSpec((nrow, C), lambda t, o: (0, 0))
        out_shape = jax.ShapeDtypeStruct((nrow, C), F32)

    def body(o_ref, g_ref, r_ref, out_ref):
        acc = g_ref[0] if kind == "col" else g_ref[...]
        for k in range(N_DEV - 1):
            acc = acc + (r_ref[k, 0] if kind == "col" else r_ref[k]).astype(F32)
        out_ref[...] = acc

    return pl.pallas_call(
        body, name="grads_final_sum8",
        grid_spec=pltpu.PrefetchScalarGridSpec(num_scalar_prefetch=1, grid=grid, in_specs=in_specs,
                                               out_specs=out_spec),
        out_shape=out_shape, compiler_params=_cparams(),
    )(offs, g, recv)


def _all_reduce_small(pack):
    R, P = pack.shape

    def body(x_ref, o_ref, gat_ref, send_sems, recv_sems):
        x, y, c = lax.axis_index("x"), lax.axis_index("y"), lax.axis_index("c")
        me = 4 * x + 2 * y + c
        gat_ref[me] = x_ref[...]
        copies = []
        for k in range(1, N_DEV):
            peer = (x ^ (k >> 2), y ^ ((k >> 1) & 1), c ^ (k & 1))
            copies.append(pltpu.make_async_remote_copy(
                src_ref=x_ref, dst_ref=gat_ref.at[me],
                send_sem=send_sems.at[k - 1], recv_sem=recv_sems.at[k - 1],
                device_id=peer, device_id_type=MESH))
        for cp in copies:
            cp.start()
        for cp in copies:
            cp.wait_recv()
        for cp in copies:
            cp.wait_send()
        acc = gat_ref[0]
        for d in range(1, N_DEV):
            acc = acc + gat_ref[d]
        o_ref[...] = acc

    vm = pl.BlockSpec(memory_space=pltpu.VMEM)
    return pl.pallas_call(
        body, name="small_all_reduce",
        out_shape=jax.ShapeDtypeStruct((R, P), F32),
        in_specs=[vm], out_specs=vm,
        scratch_shapes=[pltpu.VMEM((N_DEV, R, P), F32),
                        pltpu.SemaphoreType.DMA((N_DEV - 1,)), pltpu.SemaphoreType.DMA((N_DEV - 1,))],
    )(pack)


def _assemble(slabs, w):
    aligned, sw, total = _slab_geom(w)
    K = slabs.shape[1]
    tr = _row_tile(K, 256)

    def body(s_ref, o_ref):
        o_ref[...] = jnp.zeros(o_ref.shape, BF16)
        for i in range(N_DEV):
            a, off = aligned[i], w * i - aligned[i]
            x = s_ref[i].astype(F32)
            if off:
                x = pltpu.roll(x, off, axis=1)
            o_ref[:, a:a + sw] = (o_ref[:, a:a + sw].astype(F32) + x).astype(BF16)

    return pl.pallas_call(
        body, name="assemble_w_in", grid=(K // tr,),
        in_specs=[pl.BlockSpec((N_DEV, tr, sw), lambda i: (0, i, 0))],
        out_specs=pl.BlockSpec((tr, total), lambda i: (i, 0)),
        out_shape=jax.ShapeDtypeStruct((K, total), BF16),
        compiler_params=_cparams(),
    )(slabs)


def _rmsnorm_fwd(x, g, name):
    S, D = x.shape
    tm = _row_tile(S, 256)

    def body(x_ref, g_ref, h_ref):
        xv = x_ref[...]
        r = lax.rsqrt(jnp.mean(xv * xv, axis=-1, keepdims=True) + RMS_EPS)
        h_ref[...] = ((xv * r) * g_ref[...]).astype(BF16)

    return pl.pallas_call(
        body, name=name, grid=(S // tm,),
        in_specs=[pl.BlockSpec((tm, D), lambda i: (i, 0)), pl.BlockSpec((1, D), lambda i: (0, 0))],
        out_specs=pl.BlockSpec((tm, D), lambda i: (i, 0)),
        out_shape=jax.ShapeDtypeStruct((S, D), BF16),
        compiler_params=_cparams(),
    )(x, g)


def _rmsnorm_bwd(dh, x, g, dres, name):
    S, D = x.shape
    tm = _row_tile(S, 256)

    def body(dh_ref, x_ref, g_ref, dr_ref, dx_ref, dxb_ref, dg_ref):
        xv = x_ref[...]
        r = lax.rsqrt(jnp.mean(xv * xv, axis=-1, keepdims=True) + RMS_EPS)
        xhat = xv * r
        d = dh_ref[...]
        gd = d * g_ref[...]
        dx = r * (gd - xhat * jnp.mean(gd * xhat, axis=-1, keepdims=True)) + dr_ref[...]
        dx_ref[...] = dx
        dxb_ref[...] = dx.astype(BF16)

        @pl.when(pl.program_id(0) == 0)
        def _():
            dg_ref[...] = jnp.zeros(dg_ref.shape, F32)
        dg_ref[...] += jnp.sum(d * xhat, axis=0, keepdims=True)

    row = pl.BlockSpec((tm, D), lambda i: (i, 0))
    vec = pl.BlockSpec((1, D), lambda i: (0, 0))
    return pl.pallas_call(
        body, name=name, grid=(S // tm,),
        in_specs=[row, row, vec, row], out_specs=[row, row, vec],
        out_shape=[jax.ShapeDtypeStruct((S, D), F32), jax.ShapeDtypeStruct((S, D), BF16),
                   jax.ShapeDtypeStruct((1, D), F32)],
        compiler_params=_cparams(),
    )(dh, x, g, dres)


def _adamw(w, g, m, v, name):
    R, C = w.shape
    tr = _row_tile(R, 256)
    c1 = 1.0 - ADAM_B1 ** ADAM_STEP
    c2 = 1.0 - ADAM_B2 ** ADAM_STEP

    def body(w_ref, g_ref, m_ref, v_ref, d_ref, nm_ref, nv_ref):
        gv = g_ref[...]
        nm = ADAM_B1 * m_ref[...] + (1.0 - ADAM_B1) * gv
        nv = ADAM_B2 * v_ref[...] + (1.0 - ADAM_B2) * (gv * gv)
        d_ref[...] = -ADAM_LR * ((nm / c1) / (jnp.sqrt(nv / c2) + ADAM_EPS) + ADAM_WD * w_ref[...])
        nm_ref[...] = nm
        nv_ref[...] = nv

    spec = pl.BlockSpec((tr, C), lambda i: (i, 0))
    return pl.pallas_call(
        body, name=name, grid=(R // tr,),
        in_specs=[spec] * 4, out_specs=[spec] * 3,
        out_shape=[jax.ShapeDtypeStruct((R, C), F32)] * 3,
        compiler_params=_cparams(),
    )(w, g, m, v)


def _proj(h, wfull, col0, ncols, out_dtype, name, rope=None):
    S, K = h.shape
    tm = _row_tile(S, MM_TILE)
    tn = math.gcd(_tile(ncols, MM_TILE), col0) if col0 else _tile(ncols, MM_TILE)
    if rope is not None:
        tn = _tile(math.gcd(ncols, rope[1]), MM_TILE)
    assert ncols % tn == 0 and col0 % tn == 0
    cb = col0 // tn

    def body(*refs):
        if rope is None:
            a_ref, b_ref, o_ref = refs
        else:
            a_ref, b_ref, t_ref, o_ref = refs
        acc = _dot_nn(a_ref[...], b_ref[...])
        if rope is not None:
            t0, t1, t2 = (jnp.tile(t_ref[i], (1, tn // LANE)) for i in range(3))
            roped = (acc * t0 + pltpu.roll(acc, tn - ROT_DIM // 2, axis=1) * t1
                     + pltpu.roll(acc, ROT_DIM // 2, axis=1) * t2)
            acc = jnp.where(pl.program_id(1) < rope[1] // tn, roped, acc)
        o_ref[...] = acc.astype(out_dtype)

    in_specs = [pl.BlockSpec((tm, K), lambda i, j: (i, 0)), pl.BlockSpec((K, tn), lambda i, j: (0, cb + j))]
    args = [h, wfull]
    if rope is not None:
        in_specs.append(pl.BlockSpec((3, tm, LANE), lambda i, j: (0, i, 0)))
        args.append(rope[0])
    return pl.pallas_call(
        body, name=name, grid=(S // tm, ncols // tn),
        in_specs=in_specs, out_specs=pl.BlockSpec((tm, tn), lambda i, j: (i, j)),
        out_shape=jax.ShapeDtypeStruct((S, ncols), out_dtype),
        compiler_params=_cparams(),
    )(*args)


def _out_proj_norm(y, wo, xres, g, name):
    S, W = y.shape
    D = wo.shape[1]
    tm = _row_tile(S, 512)

    def body(a_ref, b_ref, r_ref, g_ref, x_ref, h_ref):
        xv = r_ref[...] + _dot_nn(a_ref[...], b_ref[...])
        x_ref[...] = xv
        r = lax.rsqrt(jnp.mean(xv * xv, axis=-1, keepdims=True) + RMS_EPS)
        h_ref[...] = ((xv * r) * g_ref[...]).astype(BF16)

    row = pl.BlockSpec((tm, D), lambda i: (i, 0))
    return pl.pallas_call(
        body, name=name, grid=(S // tm,),
        in_specs=[pl.BlockSpec((tm, W), lambda i: (i, 0)), pl.BlockSpec((W, D), lambda i: (0, 0)), row,
                  pl.BlockSpec((1, D), lambda i: (0, 0))],
        out_specs=[row, row],
        out_shape=[jax.ShapeDtypeStruct((S, D), F32), jax.ShapeDtypeStruct((S, D), BF16)],
        compiler_params=_cparams(),
    )(y, wo, xres, g)


def _out_proj_loss(y, wo, xres, tgt, g, name):
    S, W = y.shape
    D = wo.shape[1]
    tm = _row_tile(S, 512)

    def body(a_ref, b_ref, r_ref, t_ref, g_ref, dx_ref, dxb_ref, dg_ref, loss_ref):
        xv = r_ref[...] + _dot_nn(a_ref[...], b_ref[...])
        r = lax.rsqrt(jnp.mean(xv * xv, axis=-1, keepdims=True) + RMS_EPS)
        xhat = xv * r
        gv = g_ref[...]
        err = xhat * gv - t_ref[...]
        d = err * (1.0 / D)
        gd = d * gv
        dx = r * (gd - xhat * jnp.mean(gd * xhat, axis=-1, keepdims=True))
        dx_ref[...] = dx
        dxb_ref[...] = dx.astype(BF16)

        @pl.when(pl.program_id(0) == 0)
        def _():
            dg_ref[...] = jnp.zeros(dg_ref.shape, F32)
            loss_ref[...] = jnp.zeros(loss_ref.shape, F32)
        dg_ref[...] += jnp.sum(d * xhat, axis=0, keepdims=True)
        per_tok = jnp.sum(err * err, axis=-1, keepdims=True) * (1.0 / D)
        loss_ref[...] += 0.5 * jnp.sum(per_tok, axis=0, keepdims=True)

    row = pl.BlockSpec((tm, D), lambda i: (i, 0))
    vec = pl.BlockSpec((1, D), lambda i: (0, 0))
    return pl.pallas_call(
        body, name=name, grid=(S // tm,),
        in_specs=[pl.BlockSpec((tm, W), lambda i: (i, 0)), pl.BlockSpec((W, D), lambda i: (0, 0)), row, row, vec],
        out_specs=[row, row, vec, pl.BlockSpec((1, LANE), lambda i: (0, 0))],
        out_shape=[jax.ShapeDtypeStruct((S, D), F32), jax.ShapeDtypeStruct((S, D), BF16),
                   jax.ShapeDtypeStruct((1, D), F32), jax.ShapeDtypeStruct((1, LANE), F32)],
        compiler_params=_cparams(),
    )(y, wo, xres, tgt, g)


def _matmul_nt(parts, wfull, out_rows, name, scatter=(), scatter_specs=()):
    na = len(scatter)
    S = parts[0][0].shape[-2]
    tm, tn = _row_tile(S, MM_TILE), _tile(out_rows, MM_TILE)
    plan, lo = [], 0
    for arr, lead, col0 in parts:
        n_p = arr.shape[-1]
        tk = math.gcd(_tile(n_p, 1024), col0) if col0 else _tile(n_p, 1024)
        steps = n_p // tk
        plan.append((lead, col0 // tk, tk, lo, lo + steps))
        lo += steps
    nk = lo
    npart = len(parts)

    def body(*refs):
        a_refs, w_refs = refs[:npart], refs[npart:2 * npart]
        nin = 2 * npart + na
        o_ref, acc_ref = refs[nin], refs[nin + 1 + na]
        i, j, k = pl.program_id(0), pl.program_id(1), pl.program_id(2)
        if na:
            remote = _direct_scatter_copies(refs[2 * npart:nin], refs[nin + 1:nin + 1 + na], scatter_specs,
                                            *refs[nin + 2 + na:])

            @pl.when((i == 0) & (j == 0) & (k == 0))
            def _():
                for cp in remote:
                    cp.start()

        @pl.when(k == 0)
        def _():
            acc_ref[...] = jnp.zeros(acc_ref.shape, F32)
        for p, (_, _, _, lo_p, hi_p) in enumerate(plan):
            @pl.when((k >= lo_p) & (k < hi_p))
            def _(p=p):
                acc_ref[...] += _dot_nt(a_refs[p][...], w_refs[p][...])

        @pl.when(k == nk - 1)
        def _():
            o_ref[...] = acc_ref[...]

        if na:
            @pl.when((i == S // tm - 1) & (j == out_rows // tn - 1) & (k == nk - 1))
            def _():
                _wait_all(remote)

    in_specs, args = [], []
    for (arr, lead, col0), (_, cb, tk, lo_p, hi_p) in zip(parts, plan):
        def kk(k, lo_p=lo_p, hi_p=hi_p):
            return jnp.clip(k - lo_p, 0, hi_p - lo_p - 1)
        if lead is None:
            in_specs.append(pl.BlockSpec((tm, tk), lambda i, j, k, kk=kk: (i, kk(k))))
        else:
            in_specs.append(pl.BlockSpec((None, tm, tk), lambda i, j, k, kk=kk, lead=lead: (lead, i, kk(k))))
        args.append(arr)
    for (_, cb, tk, lo_p, hi_p) in plan:
        def kk(k, lo_p=lo_p, hi_p=hi_p):
            return jnp.clip(k - lo_p, 0, hi_p - lo_p - 1)
        in_specs.append(pl.BlockSpec((tn, tk), lambda i, j, k, kk=kk, cb=cb: (j, cb + kk(k))))
        args.append(wfull)
    any_spec = pl.BlockSpec(memory_space=pl.ANY)
    sems = [pltpu.SemaphoreType.DMA((na, N_DEV - 1)), pltpu.SemaphoreType.DMA((na, N_DEV - 1))] if na else []
    outs = pl.pallas_call(
        body, name=name, grid=(S // tm, out_rows // tn, nk),
        in_specs=in_specs + [any_spec] * na,
        out_specs=[pl.BlockSpec((tm, tn), lambda i, j, k: (i, j))] + [any_spec] * na,
        out_shape=[jax.ShapeDtypeStruct((S, out_rows), F32)]
        + [jax.ShapeDtypeStruct((N_DEV - 1,) + _scatter_block_shape(g, s), g.dtype)
           for g, s in zip(scatter, scatter_specs)],
        scratch_shapes=[pltpu.VMEM((tm, tn), F32)] + sems,
        compiler_params=_cparams(),
    )(*args, *scatter)
    return (outs[0], list(outs[1:])) if na else outs[0]


def _matmul_tn(a, parts, total, name, tile_major=False, also_bf16=False):
    S, M = a.shape
    tm, ts = _tile(M, MM_TILE), _row_tile(S, MM_TILE)
    nout = 2 if also_bf16 else 1
    outs = None
    for idx, (arr, lead, col0) in enumerate(parts):
        n_p = arr.shape[-1]
        tn = math.gcd(_tile(n_p, MM_TILE), col0) if col0 else _tile(n_p, MM_TILE)
        cb = col0 // tn
        nk = S // ts

        def body(*refs, nk=nk, tn=tn):
            a_ref, b_ref = refs[0], refs[1]
            o_refs, acc_ref = refs[-1 - nout:-1], refs[-1]
            k = pl.program_id(2)

            @pl.when(k == 0)
            def _():
                acc_ref[...] = jnp.zeros(acc_ref.shape, F32)
            acc_ref[...] += _dot_tn(a_ref[...], b_ref[...])

            @pl.when(k == nk - 1)
            def _():
                for o_ref in o_refs:
                    if tile_major:
                        for t in range(tn // LANE):
                            o_ref[t] = acc_ref[:, LANE * t:LANE * (t + 1)].astype(o_ref.dtype)
                    else:
                        o_ref[...] = acc_ref[...].astype(o_ref.dtype)

        in_specs = [pl.BlockSpec((ts, tm), lambda i, j, k: (k, i))]
        if lead is None:
            in_specs.append(pl.BlockSpec((ts, tn), lambda i, j, k: (k, j)))
        else:
            in_specs.append(pl.BlockSpec((None, ts, tn), lambda i, j, k, lead=lead: (lead, k, j)))
        args = [a, arr]
        aliases = {}
        if outs is not None:
            in_specs += [pl.BlockSpec(memory_space=pl.ANY)] * nout
            args += list(outs)
            aliases = {2 + o: o for o in range(nout)}
        if tile_major:
            out_spec = pl.BlockSpec((tn // LANE, tm, LANE), lambda i, j, k, cb=cb: (cb + j, i, 0))
            shape = (total // LANE, M, LANE)
        else:
            out_spec = pl.BlockSpec((tm, tn), lambda i, j, k, cb=cb: (i, cb + j))
            shape = (M, total)
        outs = pl.pallas_call(
            body, name=f"{name}_{idx}", grid=(M // tm, n_p // tn, nk),
            in_specs=in_specs, out_specs=[out_spec] * nout,
            out_shape=[jax.ShapeDtypeStruct(shape, dt) for dt in (F32, BF16)[:nout]],
            scratch_shapes=[pltpu.VMEM((tm, tn), F32)],
            input_output_aliases=aliases,
            compiler_params=_cparams(),
        )(*args)
    return tuple(outs) if also_bf16 else outs[0]


def _log_sigmoid(z):
    e = jnp.exp(-jnp.abs(z))
    return jnp.minimum(z, 0.0) - jnp.where(e < 1e-4, e * (1.0 - 0.5 * e), jnp.log(1.0 + e))


def _fox_gate_fwd(fl, bias):
    S = fl.shape[0]

    def body(f_ref, b_ref, c_ref):
        row = lax.broadcasted_iota(jnp.int32, (8, LANE), 0)

        def step(i, carry):
            r0 = pl.multiple_of(i * 8, 8)
            t = _log_sigmoid(f_ref[pl.ds(r0, 8), :] + b_ref[...])
            for sh in (1, 2, 4):
                t = t + jnp.where(row >= sh, pltpu.roll(t, sh, axis=0), 0.0)
            t = t + carry
            c_ref[pl.ds(r0, 8), :] = t
            return jnp.sum(jnp.where(row == 7, t, 0.0), axis=0, keepdims=True)

        lax.fori_loop(0, S // 8, step, jnp.zeros((1, LANE), F32))

    vm = pl.BlockSpec(memory_space=pltpu.VMEM)
    return pl.pallas_call(
        body, name="fox_gate_fwd", in_specs=[vm, vm], out_specs=vm,
        out_shape=jax.ShapeDtypeStruct((S, LANE), F32),
        compiler_params=_cparams(),
    )(fl, bias)


def _fox_gate_bwd(fl, bias, dc):
    S = fl.shape[0]

    def body(f_ref, b_ref, d_ref, o_ref, db_ref, acc_ref):
        row = lax.broadcasted_iota(jnp.int32, (8, LANE), 0)
        nt = S // 8

        def step(ii, carry):
            carry_c, carry_b = carry
            r0 = pl.multiple_of((nt - 1 - ii) * 8, 8)
            t = d_ref[pl.ds(r0, 8), :]
            for sh in (1, 2, 4):
                t = t + jnp.where(row < 8 - sh, pltpu.roll(t, 8 - sh, axis=0), 0.0)
            t = t + carry_c
            z = f_ref[pl.ds(r0, 8), :] + b_ref[...]
            dz = t * _sigmoid(-z)
            acc_ref[pl.ds(r0, 8), :] = dz
            first = jnp.sum(jnp.where(row == 0, t, 0.0), axis=0, keepdims=True)
            return first, carry_b + jnp.sum(dz, axis=0, keepdims=True)

        zero = jnp.zeros((1, LANE), F32)
        _, db = lax.fori_loop(0, nt, step, (zero, zero))
        db_ref[...] = db
        o_ref[...] = acc_ref[...].astype(BF16)

    vm = pl.BlockSpec(memory_space=pltpu.VMEM)
    return pl.pallas_call(
        body, name="fox_gate_bwd", in_specs=[vm, vm, vm], out_specs=[vm, vm],
        out_shape=[jax.ShapeDtypeStruct((S, LANE), BF16), jax.ShapeDtypeStruct((1, LANE), F32)],
        scratch_shapes=[pltpu.VMEM((S, LANE), F32)],
        compiler_params=_cparams(),
    )(fl, bias, dc)


def _bias_lanes(col, lane, e, first):
    o0 = HEAD_DIM * (1 - e)
    hi = col.astype(BF16)
    r1 = col - hi.astype(F32)
    mid = r1.astype(BF16)
    lo = (r1 - mid.astype(F32)).astype(BF16)
    d0 = o0 if first else o0 + 3
    t = jnp.where((lane >= o0) & (lane < o0 + 6), jnp.ones(lane.shape, BF16), jnp.zeros(lane.shape, BF16))
    t = jnp.where(lane == d0, hi, t)
    t = jnp.where(lane == d0 + 1, mid, t)
    return jnp.where(lane == d0 + 2, lo, t)


def _fox_fwd(qkv, gate, c, H, gather=()):
    na = len(gather)
    S = qkv.shape[0]
    W = H * HEAD_DIM
    HP = H // 2
    PP = 2 if HP % 2 == 0 else 1
    NE = 2 * PP
    tq = _row_tile(S, 512)
    nq = S // tq
    wb = W // LANE
    scale = HEAD_DIM ** -0.5

    def body(*refs):
        q_ref, k_ref, v_ref, g_ref, c_ref = refs[:5]
        y_ref, o_ref, a_ref = refs[5 + na:8 + na]
        kaug_sc, vaug_sc, qaug_sc, s_sc, mb_sc, m_sc, acc_sc = refs[8 + 2 * na:15 + 2 * na]
        hp, qi = pl.program_id(0), pl.program_id(1)
        if na:
            remote, local = _direct_gather_copies(refs[5:5 + na], refs[8 + na:8 + 2 * na], *refs[15 + 2 * na:])

            @pl.when((hp == 0) & (qi == 0))
            def _():
                for cp in remote + local:
                    cp.start()
        lane = lax.broadcasted_iota(jnp.int32, (tq, LANE), 1)
        own = [lane < HEAD_DIM, lane >= HEAD_DIM]
        rows = lax.broadcasted_iota(jnp.int32, (tq, tq), 0)
        cols = lax.broadcasted_iota(jnp.int32, (tq, tq), 1)

        def bias_lanes(col, e, first):
            return _bias_lanes(col, lane, e % 2, first)

        def head_col(tile, e):
            return jnp.sum(jnp.where(lane == 2 * PP * hp + e, tile, 0.0), axis=1, keepdims=True)

        def tile_of(e):
            return slice(LANE * (e // 2), LANE * (e // 2 + 1))

        @pl.when(qi == 0)
        def _():
            def chunk(i, carry):
                r0 = pl.multiple_of(i * tq, tq)
                cb = c_ref[pl.ds(r0, tq), :]
                for e in range(NE):
                    kb, vb = k_ref[pl.ds(r0, tq), tile_of(e)], v_ref[pl.ds(r0, tq), tile_of(e)]
                    kaug_sc[e, pl.ds(r0, tq), :] = jnp.where(own[e % 2], kb, bias_lanes(-head_col(cb, e), e, False))
                    vaug_sc[e, pl.ds(r0, tq), :] = jnp.where(own[e % 2], vb, jnp.ones((tq, LANE), BF16))
                return carry
            lax.fori_loop(0, nq, chunk, 0)

        crow = c_ref[pl.ds(pl.multiple_of(qi * tq, tq), tq), :]
        ctq = [head_col(crow, e) for e in range(NE)]
        for e in range(NE):
            q = q_ref[:, tile_of(e)] * jnp.asarray(scale, BF16)
            qaug_sc[e] = jnp.where(own[e % 2], q, bias_lanes(ctq[e], e, True))
        m_sc[...] = jnp.full(m_sc.shape, NEG_INF, F32)
        acc_sc[...] = jnp.zeros(acc_sc.shape, F32)

        def scores(blk, slot, masked):
            k0 = pl.multiple_of(blk * tq, tq)
            for e in range(NE):
                s = _dot_nt(qaug_sc[e], kaug_sc[e, pl.ds(k0, tq), :])
                if masked:
                    s = jnp.where(rows >= cols, s, NEG_INF)
                s_sc[slot, e] = s
                mb_sc[slot, e] = jnp.broadcast_to(jnp.max(s, axis=1, keepdims=True), (tq, LANE))

        def accumulate(blk, slot):
            k0 = pl.multiple_of(blk * tq, tq)
            for e in range(NE):
                m_prev = m_sc[e]
                m_new = jnp.maximum(m_prev, mb_sc[slot, e])
                p = jnp.exp(s_sc[slot, e] - jnp.tile(m_new, (1, tq // LANE)))
                acc_sc[e] = jnp.exp(m_prev - m_new) * acc_sc[e] + _dot_nn(p.astype(BF16), vaug_sc[e, pl.ds(k0, tq), :])
                m_sc[e] = m_new

        def block_of(t):
            return jnp.where(t == 0, qi, t - 1)

        scores(qi, 0, True)

        def loop_body(t, carry):
            scores(t, (t + 1) % 2, False)
            accumulate(block_of(t), t % 2)
            return carry

        lax.fori_loop(0, qi, loop_body, 0)
        accumulate(block_of(qi), qi % 2)
        o_e, a_e = [], []
        for e in range(NE):
            acc = acc_sc[e]
            l = pltpu.roll(acc, HEAD_DIM, axis=1)
            o_e.append(acc / l)
            a_e.append(ctq[e] - (m_sc[e] + jnp.log(l)))
        for pp in range(PP):
            o = jnp.where(own[0], o_e[2 * pp], o_e[2 * pp + 1])
            g = g_ref[:, tile_of(2 * pp)]
            y_ref[:, tile_of(2 * pp)] = (o * (g * _sigmoid(g))).astype(BF16)
            o_ref[:, tile_of(2 * pp)] = o.astype(BF16)
            a_ref[pp] = jnp.where(own[0], a_e[2 * pp], a_e[2 * pp + 1])
        if na:
            @pl.when((hp == HP // PP - 1) & (qi == nq - 1))
            def _():
                _wait_all(remote, local)

    any_spec = pl.BlockSpec(memory_space=pl.ANY)
    sems = [pltpu.SemaphoreType.DMA((na, N_DEV - 1)), pltpu.SemaphoreType.DMA((na, N_DEV - 1)),
            pltpu.SemaphoreType.DMA((na,))] if na else []
    wide = PP * LANE
    outs = pl.pallas_call(
        body, name="fox_attn_fwd", grid=(HP // PP, nq),
        in_specs=[pl.BlockSpec((tq, wide), lambda h, i: (i, h)),
                  pl.BlockSpec((S, wide), lambda h, i: (0, wb // PP + h)),
                  pl.BlockSpec((S, wide), lambda h, i: (0, 2 * wb // PP + h)),
                  pl.BlockSpec((tq, wide), lambda h, i: (i, h)),
                  pl.BlockSpec((S, LANE), lambda h, i: (0, 0))] + [any_spec] * na,
        out_specs=[pl.BlockSpec((tq, wide), lambda h, i: (i, h)),
                   pl.BlockSpec((tq, wide), lambda h, i: (i, h)),
                   pl.BlockSpec((PP, tq, LANE), lambda h, i: (h, i, 0))] + [any_spec] * na,
        out_shape=[jax.ShapeDtypeStruct((S, W), BF16), jax.ShapeDtypeStruct((S, W), BF16),
                   jax.ShapeDtypeStruct((HP, S, LANE), F32)]
        + [jax.ShapeDtypeStruct((N_DEV,) + g.shape, g.dtype) for g in gather],
        scratch_shapes=[pltpu.VMEM((NE, S, LANE), BF16), pltpu.VMEM((NE, S, LANE), BF16),
                        pltpu.VMEM((NE, tq, LANE), BF16), pltpu.VMEM((2, NE, tq, tq), F32),
                        pltpu.VMEM((2, NE, tq, LANE), F32), pltpu.VMEM((NE, tq, LANE), F32),
                        pltpu.VMEM((NE, tq, LANE), F32)] + sems,
        compiler_params=_cparams(),
    )(qkv, qkv, qkv, gate, c, *gather)
    return outs[0], outs[1], outs[2], list(outs[3:])


def _fox_out_bwd(dxb, wo, qkv, gate, o, a, H):
    S, D = dxb.shape
    W = H * HEAD_DIM
    tm, tn = _row_tile(S, 512), _tile(W, 512)
    npair = tn // LANE
    scale = HEAD_DIM ** -0.5

    def body(dx_ref, w_ref, q_ref, g_ref, o_ref, a_ref, qa_ref, da_ref, dg_ref):
        dy = _dot_nt(dx_ref[...], w_ref[...])
        lane = lax.broadcasted_iota(jnp.int32, (tm, LANE), 1)
        own = [lane < HEAD_DIM, lane >= HEAD_DIM]
        for p in range(npair):
            cols = slice(LANE * p, LANE * (p + 1))
            q = q_ref[:, cols] * jnp.asarray(scale, BF16)
            dyv, g, ov, at = dy[:, cols], g_ref[:, cols], o_ref[:, cols].astype(F32), a_ref[p]
            sg = _sigmoid(g)
            dob = (dyv * (g * sg)).astype(BF16)
            dg_ref[:, cols] = (dyv * ov * (sg * (1.0 + g * (1.0 - sg)))).astype(BF16)
            prod = dob.astype(F32) * ov
            for e in range(2):
                a_col = jnp.max(jnp.where(own[e], at, -jnp.inf), axis=1, keepdims=True)
                d_col = jnp.sum(jnp.where(own[e], prod, 0.0), axis=1, keepdims=True)
                qa_ref[e, :, cols] = jnp.where(own[e], q, _bias_lanes(a_col, lane, e, True))
                da_ref[e, :, cols] = jnp.where(own[e], dob, _bias_lanes(-d_col, lane, e, True))

    blk = pl.BlockSpec((tm, tn), lambda i, j: (i, j))
    pair = pl.BlockSpec((2, tm, tn), lambda i, j: (0, i, j))
    return pl.pallas_call(
        body, name="fox_out_bwd", grid=(S // tm, W // tn),
        in_specs=[pl.BlockSpec((tm, D), lambda i, j: (i, 0)), pl.BlockSpec((tn, D), lambda i, j: (j, 0)),
                  blk, blk, blk, pl.BlockSpec((npair, tm, LANE), lambda i, j: (j, i, 0))],
        out_specs=[pair, pair, blk],
        out_shape=[jax.ShapeDtypeStruct((2, S, W), BF16), jax.ShapeDtypeStruct((2, S, W), BF16),
                   jax.ShapeDtypeStruct((S, W), BF16)],
        compiler_params=_cparams(),
    )(dxb, wo, qkv, gate, o, a)


def _fox_bwd(qaug, doaug, qkv, c, H, scatter=(), scatter_specs=()):
    na = len(scatter)
    S = qkv.shape[0]
    W = H * HEAD_DIM
    HP = H // 2
    tq = _row_tile(S, 512)
    nq = S // tq
    wb = W // LANE
    scale = HEAD_DIM ** -0.5

    def body(*refs):
        qa_ref, da_ref, k_ref, v_ref, c_ref = refs[:5]
        out_ref, dcr_ref, dcc_ref = refs[5 + na:8 + na]
        dq_sc, dk_sc, dv_sc = refs[8 + 2 * na:11 + 2 * na]
        hp, kj = pl.program_id(0), pl.program_id(1)
        if na:
            remote = _direct_scatter_copies(refs[5:5 + na], refs[8 + na:8 + 2 * na], scatter_specs,
                                            *refs[11 + 2 * na:])

            @pl.when((hp == 0) & (kj == 0))
            def _():
                for cp in remote:
                    cp.start()
        lane = lax.broadcasted_iota(jnp.int32, (tq, LANE), 1)
        own = [lane < HEAD_DIM, lane >= HEAD_DIM]
        rows = lax.broadcasted_iota(jnp.int32, (tq, tq), 0)
        cols = lax.broadcasted_iota(jnp.int32, (tq, tq), 1)

        @pl.when(kj == 0)
        def _():
            dq_sc[...] = jnp.zeros(dq_sc.shape, F32)

        @pl.when((kj == 0) & (hp == 0))
        def _():
            dcr_ref[...] = jnp.zeros(dcr_ref.shape, F32)
            dcc_ref[...] = jnp.zeros(dcc_ref.shape, F32)

        kblk, vblk, cblk = k_ref[...], v_ref[...], c_ref[...]
        one, zero = jnp.ones((tq, LANE), BF16), jnp.zeros((tq, LANE), BF16)
        ka, va = [], []
        for e in range(2):
            o0 = HEAD_DIM * (1 - e)
            c_col = jnp.sum(jnp.where(lane == 2 * hp + e, cblk, 0.0), axis=1, keepdims=True)
            ka.append(jnp.where(own[e], kblk, _bias_lanes(-c_col, lane, e, False)))
            va.append(jnp.where(own[e], vblk, jnp.where((lane >= o0) & (lane < o0 + 3), one, zero)))
        dk_sc[...] = jnp.zeros(dk_sc.shape, F32)
        dv_sc[...] = jnp.zeros(dv_sc.shape, F32)

        def step(i, masked):
            r0 = pl.multiple_of(i * tq, tq)
            for e in range(2):
                qa = qa_ref[e, pl.ds(r0, tq), :]
                da = da_ref[e, pl.ds(r0, tq), :]
                p = jnp.exp(_dot_nt(qa, ka[e]))
                if masked:
                    p = jnp.where(rows >= cols, p, 0.0)
                ds = p * _dot_nt(da, va[e])
                pb, dsb = p.astype(BF16), ds.astype(BF16)
                dv_sc[e] += _dot_tn(pb, da)
                dk_sc[e] += _dot_tn(dsb, qa)
                dq_sc[e, pl.ds(r0, tq), :] += _dot_nn(dsb, ka[e])

        step(kj, True)

        def loop_body(i, carry):
            step(i, False)
            return carry

        lax.fori_loop(kj + 1, nq, loop_body, 0)
        k0 = pl.multiple_of(kj * tq, tq)
        out_ref[1, pl.ds(k0, tq), :] = jnp.where(own[0], dk_sc[0], dk_sc[1]).astype(BF16)
        out_ref[2, pl.ds(k0, tq), :] = jnp.where(own[0], dv_sc[0], dv_sc[1]).astype(BF16)

        def put_lane(ref, r0, e, tile, src_lane):
            col = jnp.sum(jnp.where(lane == src_lane, tile, 0.0), axis=1, keepdims=True)
            ref[pl.ds(r0, tq), :] = jnp.where(lane == 2 * hp + e, col, ref[pl.ds(r0, tq), :])

        for e in range(2):
            put_lane(dcc_ref, k0, e, dk_sc[e], HEAD_DIM * (1 - e) + 3)

        @pl.when(kj == nq - 1)
        def _():
            def chunk(i, carry):
                r0 = pl.multiple_of(i * tq, tq)
                d0, d1 = dq_sc[0, pl.ds(r0, tq), :], dq_sc[1, pl.ds(r0, tq), :]
                out_ref[0, pl.ds(r0, tq), :] = (jnp.where(own[0], d0, d1) * scale).astype(BF16)
                put_lane(dcr_ref, r0, 0, d0, HEAD_DIM)
                put_lane(dcr_ref, r0, 1, d1, 0)
                return carry
            lax.fori_loop(0, nq, chunk, 0)

        if na:
            @pl.when((hp == HP - 1) & (kj == nq - 1))
            def _():
                _wait_all(remote)

    pair = pl.BlockSpec((2, S, LANE), lambda h, j: (0, 0, h))
    vec = pl.BlockSpec((S, LANE), lambda h, j: (0, 0))
    any_spec = pl.BlockSpec(memory_space=pl.ANY)
    sems = [pltpu.SemaphoreType.DMA((na, N_DEV - 1)), pltpu.SemaphoreType.DMA((na, N_DEV - 1))] if na else []
    outs = pl.pallas_call(
        body, name="fox_attn_bwd", grid=(HP, nq),
        in_specs=[pair, pair,
                  pl.BlockSpec((tq, LANE), lambda h, j: (j, wb + h)),
                  pl.BlockSpec((tq, LANE), lambda h, j: (j, 2 * wb + h)),
                  pl.BlockSpec((tq, LANE), lambda h, j: (j, 0))] + [any_spec] * na,
        out_specs=[pl.BlockSpec((3, S, LANE), lambda h, j: (0, 0, h)), vec, vec] + [any_spec] * na,
        out_shape=[jax.ShapeDtypeStruct((3, S, W), BF16), jax.ShapeDtypeStruct((S, LANE), F32),
                   jax.ShapeDtypeStruct((S, LANE), F32)]
        + [jax.ShapeDtypeStruct((N_DEV - 1,) + _scatter_block_shape(g, s), g.dtype)
           for g, s in zip(scatter, scatter_specs)],
        scratch_shapes=[pltpu.VMEM((2, S, LANE), F32), pltpu.VMEM((2, tq, LANE), F32),
                        pltpu.VMEM((2, tq, LANE), F32)] + sems,
        compiler_params=_cparams(),
    )(qaug, doaug, qkv, qkv, c, *scatter)
    return outs[0], outs[1], outs[2], list(outs[3:])


def _swa_pick(blk, half, lane):
    b = blk.astype(F32)
    r = pltpu.roll(b, HEAD_DIM, axis=1)
    return jnp.where(jnp.logical_xor(lane < HEAD_DIM, half == 1), b, r).astype(BF16)


def _swa_stack(t, lane, G):
    pieces = []
    z = jnp.zeros((SWA_BLOCK, LANE), t.dtype)
    for j in range(G // 2):
        tile = t[:, LANE * j:LANE * (j + 1)]
        pieces += [jnp.where(lane < HEAD_DIM, tile, z), jnp.where(lane < HEAD_DIM, z, tile)]
    return jnp.concatenate(pieces, axis=0)


def _swa_unstack(st, lane, G):
    tiles = []
    for j in range(G // 2):
        a = st[2 * j * SWA_BLOCK:(2 * j + 1) * SWA_BLOCK]
        b = st[(2 * j + 1) * SWA_BLOCK:(2 * j + 2) * SWA_BLOCK]
        tiles.append(jnp.where(lane < HEAD_DIM, a, b))
    return jnp.concatenate(tiles, axis=1)


def _swa_mask_bias(G):
    R = G * SWA_BLOCK
    t_loc = jnp.arange(R)[:, None] % SWA_BLOCK
    j_loc = jnp.arange(2 * SWA_BLOCK)[None, :]
    diff = t_loc + SWA_BLOCK - j_loc
    band = (diff >= 0) & (diff < SWA_BLOCK)
    return jnp.stack([jnp.where(band & (j_loc >= SWA_BLOCK), 0.0, NEG_INF),
                      jnp.where(band, 0.0, NEG_INF)]).astype(F32)


def _swa_scores(q, kp, kc, vp, vc, srow, bias, half, head0, G):
    lane = lax.broadcasted_iota(jnp.int32, (SWA_BLOCK, LANE), 1)
    kk = jnp.concatenate([_swa_pick(kp, half, lane), _swa_pick(kc, half, lane)], axis=0)
    vv = jnp.concatenate([_swa_pick(vp, half, lane), _swa_pick(vc, half, lane)], axis=0)
    qstack = _swa_stack(q, lane, G) * jnp.asarray(HEAD_DIM ** -0.5, BF16)
    s = _dot_nt(qstack, kk) + bias
    R = G * SWA_BLOCK
    lane1 = lax.broadcasted_iota(jnp.int32, (1, LANE), 1)
    sink = jnp.concatenate(
        [jnp.broadcast_to(jnp.sum(jnp.where(lane1 == head0 + g, srow, 0.0), axis=1, keepdims=True), (SWA_BLOCK, LANE))
         for g in range(G)], axis=0)
    m = jnp.maximum(jnp.broadcast_to(jnp.max(s, axis=1, keepdims=True), (R, LANE)), sink)
    e = jnp.exp(s - jnp.tile(m, (1, 2)))
    es = jnp.exp(sink - m)
    inv = 1.0 / (jnp.broadcast_to(jnp.sum(e, axis=1, keepdims=True), (R, LANE)) + es)
    return qstack, kk, vv, e * jnp.tile(inv, (1, 2)), es * inv, lane


def _swa_fwd(qkv, gate, sinks, mask_bias, HQ, HKV):
    S = qkv.shape[0]
    G = HQ // HKV
    WQ, KVW = HQ * HEAD_DIM, HKV * HEAD_DIM
    nb = S // SWA_BLOCK
    GW = G * HEAD_DIM
    kb, vb = WQ // LANE, (WQ + KVW) // LANE

    def body(q_ref, kp_ref, kc_ref, vp_ref, vc_ref, g_ref, sink_ref, b_ref, y_ref, o_ref):
        pair = pl.program_id(0)
        for half in range(2):
            cols = slice(GW * half, GW * (half + 1))
            _, _, vv, p, _, lane = _swa_scores(q_ref[:, cols], kp_ref[...], kc_ref[...], vp_ref[...], vc_ref[...],
                                               sink_ref[...], b_ref[0], half, (2 * pair + half) * G, G)
            o = _swa_unstack(_dot_nn(p.astype(BF16), vv), lane, G)
            g = g_ref[:, cols]
            y_ref[:, cols] = (o * (g * _sigmoid(g))).astype(BF16)
            o_ref[:, cols] = o.astype(BF16)

    blk = lambda cb, prev: pl.BlockSpec(
        (SWA_BLOCK, LANE), lambda h, n, cb=cb, prev=prev: (jnp.maximum(n - prev, 0), cb + h))
    qspec = pl.BlockSpec((SWA_BLOCK, 2 * GW), lambda h, n: (n, h))
    return pl.pallas_call(
        body, name="swa_attn_fwd", grid=(HKV // 2, nb),
        in_specs=[qspec, blk(kb, 1), blk(kb, 0), blk(vb, 1), blk(vb, 0), qspec,
                  pl.BlockSpec((1, LANE), lambda h, n: (0, 0)),
                  pl.BlockSpec((1, G * SWA_BLOCK, 2 * SWA_BLOCK), lambda h, n: (jnp.minimum(n, 1), 0, 0))],
        out_specs=[qspec, qspec],
        out_shape=[jax.ShapeDtypeStruct((S, WQ), BF16), jax.ShapeDtypeStruct((S, WQ), BF16)],
        compiler_params=_cparams(),
    )(qkv, qkv, qkv, qkv, qkv, gate, sinks, mask_bias)


def _swa_bwd(qkv, dy, gate, o, sinks, tables, mask_bias, HQ, HKV):
    S = qkv.shape[0]
    G = HQ // HKV
    WQ, KVW = HQ * HEAD_DIM, HKV * HEAD_DIM
    nb = S // SWA_BLOCK
    GW = G * HEAD_DIM
    R = G * SWA_BLOCK
    kb, vb = WQ // LANE, (WQ + KVW) // LANE
    scale = HEAD_DIM ** -0.5
    assert G == 8

    def body(q_ref, kp_ref, kc_ref, vp_ref, vc_ref, dy_ref, g_ref, o_ref, sink_ref, t_ref, b_ref,
             dqg_ref, dkv_ref, dsink_ref, carry_sc):
        pair, n = pl.program_id(0), pl.program_id(1)

        @pl.when(n == 0)
        def _():
            carry_sc[...] = jnp.zeros(carry_sc.shape, F32)
            dsink_ref[...] = jnp.zeros(dsink_ref.shape, F32)

        @pl.when(n < nb)
        def _():
            t0, t1, t2 = (jnp.tile(t_ref[i], (1, GW // LANE)) for i in range(3))
            for half in range(2):
                cols = slice(GW * half, GW * (half + 1))
                qstack, kk, vv, p, psink, lane = _swa_scores(
                    q_ref[:, cols], kp_ref[...], kc_ref[...], vp_ref[...], vc_ref[...], sink_ref[...], b_ref[0],
                    half, (2 * pair + half) * G, G)
                dyv, g, ov = dy_ref[:, cols], g_ref[:, cols], o_ref[:, cols].astype(F32)
                sg = _sigmoid(g)
                dob = (dyv * (g * sg)).astype(BF16)
                dqg_ref[1, :, cols] = (dyv * ov * (sg * (1.0 + g * (1.0 - sg)))).astype(BF16)
                prod = dob.astype(F32) * ov
                dparts = []
                for j in range(G // 2):
                    tile = prod[:, LANE * j:LANE * (j + 1)]
                    for sel in (jnp.where(lane < HEAD_DIM, tile, 0.0), jnp.where(lane < HEAD_DIM, 0.0, tile)):
                        dparts.append(jnp.broadcast_to(jnp.sum(sel, axis=1, keepdims=True), (SWA_BLOCK, LANE)))
                delta = jnp.concatenate(dparts, axis=0)
                dostack = _swa_stack(dob, lane, G)
                ds = p * (_dot_nt(dostack, vv) - jnp.tile(delta, (1, 2)))
                dsb, pb = ds.astype(BF16), p.astype(BF16)
                dq = _swa_unstack(_dot_nn(dsb, kk), lane, G) * scale
                dq = dq * t0 + pltpu.roll(dq * t1, ROT_DIM // 2, axis=1) + pltpu.roll(dq * t2, GW - ROT_DIM // 2, axis=1)
                dqg_ref[0, :, cols] = dq.astype(BF16)
                dkk = _dot_tn(dsb, qstack)
                dvv = _dot_tn(pb, dostack)
                dkk = dkk + pltpu.roll(dkk, HEAD_DIM, axis=1)
                dvv = dvv + pltpu.roll(dvv, HEAD_DIM, axis=1)
                lane2 = lax.broadcasted_iota(jnp.int32, (2 * SWA_BLOCK, LANE), 1)
                comb = jnp.where(lane2 < HEAD_DIM, dkk, dvv)
                dkv_ref[half] = carry_sc[half] + comb[:SWA_BLOCK]
                carry_sc[half] = comb[SWA_BLOCK:]
                sk = psink * delta
                rows = [-jnp.sum(sk[g_ * SWA_BLOCK:(g_ + 1) * SWA_BLOCK], axis=0, keepdims=True) for g_ in range(G)]
                dsink_ref[half] += jnp.concatenate(rows, axis=0)

        @pl.when(n == nb)
        def _():
            dkv_ref[...] = carry_sc[...]

    cl = lambda n: jnp.minimum(n, nb - 1)
    blk = lambda cb, prev: pl.BlockSpec(
        (SWA_BLOCK, LANE), lambda h, n, cb=cb, prev=prev: (jnp.maximum(cl(n) - prev, 0), cb + h))
    qspec = pl.BlockSpec((SWA_BLOCK, 2 * GW), lambda h, n: (cl(n), h))
    return pl.pallas_call(
        body, name="swa_attn_bwd", grid=(HKV // 2, nb + 1),
        in_specs=[qspec, blk(kb, 1), blk(kb, 0), blk(vb, 1), blk(vb, 0), qspec, qspec, qspec,
                  pl.BlockSpec((1, LANE), lambda h, n: (0, 0)),
                  pl.BlockSpec((3, SWA_BLOCK, LANE), lambda h, n: (0, cl(n), 0)),
                  pl.BlockSpec((1, R, 2 * SWA_BLOCK), lambda h, n: (jnp.minimum(n, 1), 0, 0))],
        out_specs=[pl.BlockSpec((2, SWA_BLOCK, 2 * GW), lambda h, n: (0, cl(n), h)),
                   pl.BlockSpec((2, SWA_BLOCK, LANE), lambda h, n: (h, jnp.maximum(n - 1, 0), 0)),
                   pl.BlockSpec((2, 8, LANE), lambda h, n: (h, 0, 0))],
        out_shape=[jax.ShapeDtypeStruct((2, S, WQ), BF16), jax.ShapeDtypeStruct((HKV, S, LANE), F32),
                   jax.ShapeDtypeStruct((HKV, 8, LANE), F32)],
        scratch_shapes=[pltpu.VMEM((2, SWA_BLOCK, LANE), F32)],
        compiler_params=_cparams(),
    )(qkv, qkv, qkv, qkv, qkv, dy, gate, o, sinks, tables, mask_bias)


def _swa_dkv_finish(dkv, tables):
    HKV, S, _ = dkv.shape
    KVW = HKV * HEAD_DIM
    tm = _row_tile(S, 512)
    npair = HKV // 2

    def body(d_ref, t_ref, o_ref):
        lane = lax.broadcasted_iota(jnp.int32, (tm, LANE), 1)
        lo = lane < HEAD_DIM
        for p in range(npair):
            a, b = d_ref[2 * p], d_ref[2 * p + 1]
            tk = jnp.where(lo, a, pltpu.roll(b, HEAD_DIM, axis=1))
            tv = jnp.where(lo, pltpu.roll(a, HEAD_DIM, axis=1), b)
            tk = (tk * t_ref[0] + pltpu.roll(tk * t_ref[1], ROT_DIM // 2, axis=1)
                  + pltpu.roll(tk * t_ref[2], LANE - ROT_DIM // 2, axis=1))
            o_ref[:, LANE * p:LANE * (p + 1)] = tk.astype(BF16)
            o_ref[:, KVW + LANE * p:KVW + LANE * (p + 1)] = tv.astype(BF16)

    return pl.pallas_call(
        body, name="swa_dkv_finish", grid=(S // tm,),
        in_specs=[pl.BlockSpec((HKV, tm, LANE), lambda i: (0, i, 0)), pl.BlockSpec((3, tm, LANE), lambda i: (0, i, 0))],
        out_specs=pl.BlockSpec((tm, 2 * KVW), lambda i: (i, 0)),
        out_shape=jax.ShapeDtypeStruct((S, 2 * KVW), BF16),
        compiler_params=_cparams(),
    )(dkv, tables)


def _rope_tables(S, width):
    half = ROT_DIM // 2
    pos = jnp.arange(S, dtype=F32)
    inv_freq = ROPE_THETA ** (-jnp.arange(half, dtype=F32) / half)
    ang = pos[:, None] * inv_freq[None, :]
    cos, sin = jnp.cos(ang), jnp.sin(ang)
    one = jnp.ones((S, HEAD_DIM - ROT_DIM), F32)
    zero = jnp.zeros((S, HEAD_DIM - ROT_DIM), F32)
    zh = jnp.zeros((S, half), F32)
    t0 = jnp.concatenate([cos, cos, one], axis=1)
    t1 = jnp.concatenate([-sin, zh, zero], axis=1)
    t2 = jnp.concatenate([zh, sin, zero], axis=1)
    return jnp.stack([jnp.tile(t, (1, width // HEAD_DIM)) for t in (t0, t1, t2)])


def _pad_rows(v, row, total_rows=8):
    return jnp.pad(v, ((row, total_rows - row - v.shape[0]), (0, 0)))


def _pad_lanes(v, off, width):
    return jnp.pad(v, ((0, 0), (off, width - off - v.shape[1])))


def kernel(x, norm_g, fox_w_in, fox_b_f, fox_w_out, swa_w_in, swa_sinks, swa_w_out, final_g, loss_target, m_norm_g, m_fox_w_in, m_fox_b_f, m_fox_w_out, m_swa_w_in, m_swa_sinks, m_swa_w_out, m_final_g, v_norm_g, v_fox_w_in, v_fox_b_f, v_fox_w_out, v_swa_w_in, v_swa_sinks, v_swa_w_out, v_final_g):
    S, D = x.shape[1], x.shape[2]
    H = fox_b_f.shape[1]
    W = H * HEAD_DIM
    wf = fox_w_in.shape[2]
    ws = swa_w_in.shape[2]
    HQ = swa_sinks.shape[1]
    WQ = HQ * HEAD_DIM
    KVW = (ws * N_DEV - 2 * WQ) // 2
    HKV = KVW // HEAD_DIM
    rows_o = fox_w_out.shape[1]
    assert wf * N_DEV == 4 * W + H and rows_o * N_DEV == W and H <= LANE and HQ <= LANE
    me = _my_index()

    _, sw_f, np_f = _slab_geom(wf)
    _, sw_s, np_s = _slab_geom(ws)

    def slab(w2d, w, sw):
        return jnp.pad(w2d.astype(BF16), ((0, 0), (0, sw - w)))

    (fi_all,) = _all_gather([slab(fox_w_in[0], wf, sw_f)])
    w_fi = _assemble(fi_all, wf)
    later = [slab(swa_w_in[0], ws, sw_s), fox_w_out[0].astype(BF16), swa_w_out[0].astype(BF16)]

    x0 = x[0]
    g0, g1, gf = norm_g[0:1], norm_g[1:2], final_g[None, :]
    bias = _pad_lanes(fox_b_f, 0, LANE)
    sinks = _pad_lanes(swa_sinks, 0, LANE)
    tab_k = _rope_tables(S, LANE)
    mask_bias = _swa_mask_bias(HQ // HKV)

    h0 = _rmsnorm_fwd(x0, g0, "rmsnorm0")
    qkv0 = _proj(h0, w_fi, 0, 3 * W, BF16, "fox_in_qkv")
    gate0 = _proj(h0, w_fi, 3 * W, W, F32, "fox_in_gate")
    fl = _proj(h0, w_fi, 4 * W, LANE, F32, "fox_in_f")
    c = _fox_gate_fwd(fl, bias)
    y0, o0, a0, (si_all, fo_all, so_all) = _fox_fwd(qkv0, gate0, c, H, gather=later)
    w_si = _assemble(si_all, ws)
    w_fo = fo_all.reshape(W, D)
    w_so = so_all.reshape(WQ, D)
    x1, h1 = _out_proj_norm(y0, w_fo, x0, g1, "fox_out")

    qkv1 = _proj(h1, w_si, 0, WQ + 2 * KVW, BF16, "swa_in_qkv", rope=(tab_k, WQ + KVW))
    gate1 = _proj(h1, w_si, WQ + 2 * KVW, WQ, F32, "swa_in_gate")
    y1, o1 = _swa_fwd(qkv1, gate1, sinks, mask_bias, HQ, HKV)
    dx2, dx2b, dgf, loss_p = _out_proj_loss(y1, w_so, x1, loss_target[0], gf, "swa_out_loss")

    dy1 = _matmul_nt([(dx2b, None, 0)], w_so, WQ, "swa_out_bwd")
    g_so, g_so_h = _matmul_tn(y1, [(dx2b, None, 0)], D, "swa_out_wgrad", also_bf16=True)
    dqg1, dkv1, dsink = _swa_bwd(qkv1, dy1, gate1, o1, sinks, tab_k, mask_bias, HQ, HKV)
    dkv1f = _swa_dkv_finish(dkv1, tab_k)
    parts1 = [(dqg1, 0, 0), (dkv1f, None, WQ), (dqg1, 1, WQ + 2 * KVW)]
    g_si, g_si_h = _matmul_tn(h1, parts1, np_s, "swa_in_wgrad", tile_major=True, also_bf16=True)
    dh1 = _matmul_nt(parts1, w_si, D, "swa_in_bwd")
    dx1, dx1b, dg1 = _rmsnorm_bwd(dh1, x1, g1, dx2, "rmsnorm1_bwd")

    qaug0, doaug0, dgate0 = _fox_out_bwd(dx1b, w_fo, qkv0, gate0, o0, a0, H)
    g_fo, g_fo_h = _matmul_tn(y0, [(dx1b, None, 0)], D, "fox_out_wgrad", also_bf16=True)
    early_specs = [("col", ws), ("row", rows_o), ("row", rows_o)]
    dqkv0, dcr, dcc, early_recv = _fox_bwd(qaug0, doaug0, qkv0, c, H, scatter=[g_si_h, g_fo_h, g_so_h],
                                          scatter_specs=early_specs)
    dfl, dbf = _fox_gate_bwd(fl, bias, dcr - dcc)
    parts0 = [(dqkv0, p, p * W) for p in range(3)] + [(dgate0, None, 3 * W), (dfl, None, 4 * W)]
    g_fi, g_fi_h = _matmul_tn(h0, parts0, np_f, "fox_in_wgrad", tile_major=True, also_bf16=True)
    spec_fi = ("col", wf)
    dh0, (recv_fi,) = _matmul_nt(parts0, w_fi, D, "fox_in_bwd", scatter=[g_fi_h], scatter_specs=[spec_fi])
    dx0, _, dg0 = _rmsnorm_bwd(dh0, x0, g0, dx1, "rmsnorm0_bwd")

    red_si, gw_fo, gw_so = [_final_sum8(g_, r_, s_)
                            for g_, r_, s_ in zip([g_si, g_fo, g_so], early_recv, early_specs)]
    red_fi = _final_sum8(g_fi, recv_fi, spec_fi)
    gw_fi = lax.dynamic_slice(red_fi, (0, (wf * me) % LANE), (D, wf))
    gw_si = lax.dynamic_slice(red_si, (0, (ws * me) % LANE), (D, ws))

    P = D
    dsink_v = dsink[:, :, 0].reshape(1, HQ)
    row3 = _pad_lanes(dbf[:, :H], 0, P) + _pad_lanes(dsink_v, LANE, P) + _pad_lanes(loss_p[:, :1], 2 * LANE, P)
    pack = _pad_rows(dg0, 0) + _pad_rows(dg1, 1) + _pad_rows(dgf, 2) + _pad_rows(row3, 3)
    tot = _all_reduce_small(pack)
    loss = tot[3, 2 * LANE]
    g_norm = tot[0:2]
    g_final = tot[2]
    g_bf = tot[3:4, 0:H]
    g_sinks = tot[3:4, LANE:LANE + HQ]

    def small_pack(ng, fg, bf, sk):
        r3 = _pad_lanes(bf, 0, P) + _pad_lanes(sk, LANE, P)
        return _pad_rows(ng, 0) + _pad_rows(fg[None, :], 2) + _pad_rows(r3, 3)

    sd, sm, sv = _adamw(small_pack(norm_g, final_g, fox_b_f, swa_sinks), tot,
                        small_pack(m_norm_g, m_final_g, m_fox_b_f, m_swa_sinks),
                        small_pack(v_norm_g, v_final_g, v_fox_b_f, v_swa_sinks), "adamw_small")

    def unpack(t):
        return t[0:2], t[3:4, 0:H], t[3:4, LANE:LANE + HQ], t[2]

    d_fi, m_fi, v_fi = _adamw(fox_w_in[0], gw_fi, m_fox_w_in[0], v_fox_w_in[0], "adamw_fox_in")
    d_fo, m_fo, v_fo = _adamw(fox_w_out[0], gw_fo, m_fox_w_out[0], v_fox_w_out[0], "adamw_fox_out")
    d_si, m_si, v_si = _adamw(swa_w_in[0], gw_si, m_swa_w_in[0], v_swa_w_in[0], "adamw_swa_in")
    d_so, m_so, v_so = _adamw(swa_w_out[0], gw_so, m_swa_w_out[0], v_swa_w_out[0], "adamw_swa_out")

    def group(small, fi, fo, si, so):
        ng, bf, sk, fg = unpack(small)
        return (ng, fi[None], bf, fo[None], si[None], sk, so[None], fg)

    grads = (g_norm, gw_fi[None], g_bf, gw_fo[None], gw_si[None], g_sinks, gw_so[None], g_final)
    return (loss, dx0[None], *grads, *group(sd, d_fi, d_fo, d_si, d_so),
            *group(sm, m_fi, m_fo, m_si, m_so), *group(sv, v_fi, v_fo, v_si, v_so))
```

```python
import math

import jax
import jax.numpy as jnp
from jax import lax
from jax.experimental import pallas as pl
from jax.experimental.pallas import tpu as pltpu

F32 = jnp.float32
BF16 = jnp.bfloat16
MESH = pl.DeviceIdType.MESH

N_DEV = 8
LANE = 128
HEAD_DIM = 64
SWA_BLOCK = 128
NEG_INF = -1e30
RMS_EPS = 1e-6
ROPE_THETA = 500000.0
ROT_DIM = HEAD_DIM // 4
ADAM_LR, ADAM_B1, ADAM_B2, ADAM_EPS, ADAM_WD, ADAM_STEP = 0.001, 0.9, 0.999, 1e-08, 0.01, 10
VMEM_LIMIT = 56 * 1024 * 1024
MM_TILE = 1024


def _cparams(**kw):
    return pltpu.CompilerParams(vmem_limit_bytes=VMEM_LIMIT, **kw)


def _tile(n, cap):
    if n <= cap:
        return n
    t = (cap // LANE) * LANE
    while t > LANE and n % t:
        t -= LANE
    assert n % t == 0, (n, cap)
    return t


def _row_tile(n, cap):
    t = min(n, cap)
    while n % t:
        t //= 2
    return t


def _dot_nn(a, b):
    return jnp.dot(a, b, preferred_element_type=F32)


def _dot_nt(a, b):
    return lax.dot_general(a, b, (((1,), (1,)), ((), ())), preferred_element_type=F32)


def _dot_tn(a, b):
    return lax.dot_general(a, b, (((0,), (0,)), ((), ())), preferred_element_type=F32)


def _sigmoid(g):
    return 1.0 / (1.0 + jnp.exp(-g))


def _slab_geom(w):
    starts = [w * i for i in range(N_DEV)]
    aligned = [LANE * (s // LANE) for s in starts]
    offs = [s - a for s, a in zip(starts, aligned)]
    sw = LANE * (-(-(max(offs) + w) // LANE))
    return aligned, sw, aligned[-1] + sw


def _my_index():
    return 4 * lax.axis_index("x") + 2 * lax.axis_index("y") + lax.axis_index("c")


def _all_gather(arrs):
    n = len(arrs)

    def body(*refs):
        ins, outs = refs[:n], refs[n:2 * n]
        send_sems, recv_sems, local_sems = refs[2 * n:]
        x, y, c = lax.axis_index("x"), lax.axis_index("y"), lax.axis_index("c")
        me, sib = (x, y, c), (x, y, 1 - c)
        chips = [(1 - x, y), (x, 1 - y), (1 - x, 1 - y)]

        def idx(px, py, pc):
            return 4 * px + 2 * py + pc

        def copy(a, k, block, to, src=None):
            dst = outs[a].at[idx(*block)]
            return pltpu.make_async_remote_copy(
                src_ref=dst if src is None else src, dst_ref=dst,
                send_sem=send_sems.at[a, k], recv_sem=recv_sems.at[a, k],
                device_id=to, device_id_type=MESH)

        mine = [pltpu.make_async_copy(ins[a], outs[a].at[idx(*me)], local_sems.at[a]) for a in range(n)]
        for m in mine:
            m.start()
        first = []
        for a in range(n):
            first.append(copy(a, 0, me, sib, src=ins[a]))
            for j, chip in enumerate(chips):
                first.append(copy(a, 1 + j, me, (*chip, c), src=ins[a]))
        for cp in first:
            cp.start()
        passed = []
        for j, chip in enumerate(chips):
            for a in range(n):
                copy(a, 1 + j, (*chip, c), me).wait_recv()
                p = copy(a, 4 + j, (*chip, c), sib)
                p.start()
                passed.append(p)
        for a in range(n):
            copy(a, 0, sib, me).wait_recv()
        for j, chip in enumerate(chips):
            for a in range(n):
                copy(a, 4 + j, (*chip, 1 - c), me).wait_recv()
        for cp in first + passed:
            cp.wait_send()
        for m in mine:
            m.wait()

    any_spec = pl.BlockSpec(memory_space=pl.ANY)
    return pl.pallas_call(
        body, name="weights_all_gather",
        out_shape=[jax.ShapeDtypeStruct((N_DEV,) + a.shape, a.dtype) for a in arrs],
        in_specs=[any_spec] * n, out_specs=[any_spec] * n,
        scratch_shapes=[pltpu.SemaphoreType.DMA((n, 7)), pltpu.SemaphoreType.DMA((n, 7)),
                        pltpu.SemaphoreType.DMA((n,))],
    )(*arrs)


def _rs_windows(specs):
    def window(ref, spec, blk):
        kind, n = spec
        if kind == "col":
            _, sw, _ = _slab_geom(n)
            return ref.at[pl.ds((n * blk) // LANE, sw // LANE)]
        start = pl.multiple_of(n * blk, n)
        return ref.at[pl.ds(start, n), :]
    return window


def _peer(k):
    x, y, c = lax.axis_index("x"), lax.axis_index("y"), lax.axis_index("c")
    return (x ^ (k >> 2), y ^ ((k >> 1) & 1), c ^ (k & 1))


def _direct_gather_copies(ins, outs, send_sems, recv_sems, local_sems):
    me = _my_index()
    remote, local = [], []
    for a, (src, dst) in enumerate(zip(ins, outs)):
        local.append(pltpu.make_async_copy(src, dst.at[me], local_sems.at[a]))
        for k in range(1, N_DEV):
            remote.append(pltpu.make_async_remote_copy(
                src_ref=src, dst_ref=dst.at[me], send_sem=send_sems.at[a, k - 1], recv_sem=recv_sems.at[a, k - 1],
                device_id=_peer(k), device_id_type=MESH))
    return remote, local


def _direct_scatter_copies(ins, outs, specs, send_sems, recv_sems):
    window = _rs_windows(specs)
    remote = []
    for a, (src, dst) in enumerate(zip(ins, outs)):
        for k in range(1, N_DEV):
            px, py, pc = _peer(k)
            remote.append(pltpu.make_async_remote_copy(
                src_ref=window(src, specs[a], 4 * px + 2 * py + pc), dst_ref=dst.at[k - 1],
                send_sem=send_sems.at[a, k - 1], recv_sem=recv_sems.at[a, k - 1],
                device_id=(px, py, pc), device_id_type=MESH))
    return remote


def _scatter_block_shape(g, spec):
    kind, w = spec
    return (_slab_geom(w)[1] // LANE, g.shape[1], LANE) if kind == "col" else (w, g.shape[1])


def _wait_all(remote, local=()):
    for cp in remote:
        cp.wait_recv()
    for cp in remote:
        cp.wait_send()
    for cp in local:
        cp.wait()


def _scatter_start(g, spec):
    blk = _scatter_block_shape(g, spec)
    window = _rs_windows([spec])
    npeer = N_DEV - 1

    def body(g_ref, land_ref, *rest):
        sems = rest[:2 * npeer]
        token = rest[2 * npeer + 2]
        for cp in _peer_block_copies(g_ref, land_ref, spec, window, sems[:npeer], sems[npeer:]):
            cp.start()
        token[...] = jnp.zeros(token.shape, token.dtype)

    hbm = pl.BlockSpec(memory_space=pltpu.HBM)
    sem = pl.BlockSpec(memory_space=pltpu.SEMAPHORE)
    land = lax.empty((npeer,) + blk, g.dtype)
    outs = pl.pallas_call(
        body, name="grads_scatter_start",
        out_shape=(pltpu.SemaphoreType.DMA(()),) * (2 * npeer)
        + (pltpu.HBM(g.shape, g.dtype), pltpu.HBM(land.shape, land.dtype), jax.ShapeDtypeStruct((8, LANE), F32)),
        in_specs=(hbm, hbm),
        out_specs=(sem,) * (2 * npeer) + (hbm, hbm, pl.BlockSpec(memory_space=pltpu.VMEM)),
        input_output_aliases={0: 2 * npeer, 1: 2 * npeer + 1},
        compiler_params=pltpu.CompilerParams(has_side_effects=pltpu.SideEffectType.DATAFLOW_SIDE_EFFECTING),
    )(pltpu.with_memory_space_constraint(g, pltpu.HBM), pltpu.with_memory_space_constraint(land, pltpu.HBM))
    return outs[:2 * npeer], outs[2 * npeer], outs[2 * npeer + 1], outs[2 * npeer + 2]


def _peer_block_copies(g_ref, land_ref, spec, window, send_sems, recv_sems):
    copies = []
    for k in range(1, N_DEV):
        px, py, pc = _peer(k)
        copies.append(pltpu.make_async_remote_copy(
            src_ref=window(g_ref, spec, 4 * px + 2 * py + pc), dst_ref=land_ref.at[k - 1],
            send_sem=send_sems[k - 1], recv_sem=recv_sems[k - 1], device_id=(px, py, pc), device_id_type=MESH))
    return copies


def _scatter_wait(sems, g_thru, land_thru, spec, after):
    window = _rs_windows([spec])
    npeer = N_DEV - 1

    def body(g_ref, land_ref, *rest):
        s = rest[:2 * npeer]
        copies = _peer_block_copies(g_ref, land_ref, spec, window, s[:npeer], s[npeer:])
        for cp in copies:
            cp.wait_send()
        for cp in copies:
            cp.wait_recv()

    hbm = pl.BlockSpec(memory_space=pltpu.HBM)
    sem = pl.BlockSpec(memory_space=pltpu.SEMAPHORE)
    return pl.pallas_call(
        body, name="grads_scatter_wait",
        out_shape=(pltpu.HBM(g_thru.shape, g_thru.dtype), pltpu.HBM(land_thru.shape, land_thru.dtype)),
        in_specs=(hbm, hbm) + (sem,) * (2 * npeer) + (pl.BlockSpec(memory_space=pl.ANY),) * len(after),
        out_specs=(hbm, hbm), input_output_aliases={0: 0, 1: 1},
        compiler_params=pltpu.CompilerParams(has_side_effects=pltpu.SideEffectType.DATAFLOW_SIDE_EFFECTING),
    )(g_thru, land_thru, *sems, *after)[1]


def _final_sum8(g, recv, spec):
    kind, n = spec
    me = _my_index()
    offs = jnp.stack([(n * me) // LANE if kind == "col" else me]).astype(jnp.int32)
    if kind == "col":
        _, T, M, _ = recv.shape
        grid = (T,)
        in_specs = [pl.BlockSpec((1, M, LANE), lambda t, o: (o[0] + t, 0, 0)),
                    pl.BlockSpec((N_DEV - 1, 1, M, LANE), lambda t, o: (0, t, 0, 0))]
        out_spec = pl.BlockSpec((M, LANE), lambda t, o: (0, t))
        out_shape = jax.ShapeDtypeStruct((M, T * LANE), F32)
    else:
        _, nrow, C = recv.shape
        grid = (1,)
        in_specs = [pl.BlockSpec((nrow, C), lambda t, o: (o[0], 0)),
                    pl.BlockSpec((N_DEV - 1, nrow, C), lambda t, o: (0, 0, 0))]
        out_spec = pl.BlockSpec((nrow, C), lambda t, o: (0, 0))
        out_shape = jax.ShapeDtypeStruct((nrow, C), F32)

    def body(o_ref, g_ref, r_ref, out_ref):
        acc = g_ref[0] if kind == "col" else g_ref[...]
        for k in range(N_DEV - 1):
            acc = acc + (r_ref[k, 0] if kind == "col" else r_ref[k]).astype(F32)
        out_ref[...] = acc

    return pl.pallas_call(
        body, name="grads_final_sum8",
        grid_spec=pltpu.PrefetchScalarGridSpec(num_scalar_prefetch=1, grid=grid, in_specs=in_specs,
                                               out_specs=out_spec),
        out_shape=out_shape, compiler_params=_cparams(),
    )(offs, g, recv)


def _all_reduce_small(pack):
    R, P = pack.shape

    def body(x_ref, o_ref, gat_ref, send_sems, recv_sems):
        x, y, c = lax.axis_index("x"), lax.axis_index("y"), lax.axis_index("c")
        me = 4 * x + 2 * y + c
        gat_ref[me] = x_ref[...]
        copies = []
        for k in range(1, N_DEV):
            peer = (x ^ (k >> 2), y ^ ((k >> 1) & 1), c ^ (k & 1))
            copies.append(pltpu.make_async_remote_copy(
                src_ref=x_ref, dst_ref=gat_ref.at[me],
                send_sem=send_sems.at[k - 1], recv_sem=recv_sems.at[k - 1],
                device_id=peer, device_id_type=MESH))
        for cp in copies:
            cp.start()
        for cp in copies:
            cp.wait_recv()
        for cp in copies:
            cp.wait_send()
        acc = gat_ref[0]
        for d in range(1, N_DEV):
            acc = acc + gat_ref[d]
        o_ref[...] = acc

    vm = pl.BlockSpec(memory_space=pltpu.VMEM)
    return pl.pallas_call(
        body, name="small_all_reduce",
        out_shape=jax.ShapeDtypeStruct((R, P), F32),
        in_specs=[vm], out_specs=vm,
        scratch_shapes=[pltpu.VMEM((N_DEV, R, P), F32),
                        pltpu.SemaphoreType.DMA((N_DEV - 1,)), pltpu.SemaphoreType.DMA((N_DEV - 1,))],
    )(pack)


def _assemble(slabs, w):
    aligned, sw, total = _slab_geom(w)
    K = slabs.shape[1]
    tr = _row_tile(K, 256)

    def body(s_ref, o_ref):
        o_ref[...] = jnp.zeros(o_ref.shape, BF16)
        for i in range(N_DEV):
            a, off = aligned[i], w * i - aligned[i]
            x = s_ref[i].astype(F32)
            if off:
                x = pltpu.roll(x, off, axis=1)
            o_ref[:, a:a + sw] = (o_ref[:, a:a + sw].astype(F32) + x).astype(BF16)

    return pl.pallas_call(
        body, name="assemble_w_in", grid=(K // tr,),
        in_specs=[pl.BlockSpec((N_DEV, tr, sw), lambda i: (0, i, 0))],
        out_specs=pl.BlockSpec((tr, total), lambda i: (i, 0)),
        out_shape=jax.ShapeDtypeStruct((K, total), BF16),
        compiler_params=_cparams(),
    )(slabs)


def _rmsnorm_fwd(x, g, name):
    S, D = x.shape
    tm = _row_tile(S, 256)

    def body(x_ref, g_ref, h_ref):
        xv = x_ref[...]
        r = lax.rsqrt(jnp.mean(xv * xv, axis=-1, keepdims=True) + RMS_EPS)
        h_ref[...] = ((xv * r) * g_ref[...]).astype(BF16)

    return pl.pallas_call(
        body, name=name, grid=(S // tm,),
        in_specs=[pl.BlockSpec((tm, D), lambda i: (i, 0)), pl.BlockSpec((1, D), lambda i: (0, 0))],
        out_specs=pl.BlockSpec((tm, D), lambda i: (i, 0)),
        out_shape=jax.ShapeDtypeStruct((S, D), BF16),
        compiler_params=_cparams(),
    )(x, g)


def _rmsnorm_bwd(dh, x, g, dres, name):
    S, D = x.shape
    tm = _row_tile(S, 256)

    def body(dh_ref, x_ref, g_ref, dr_ref, dx_ref, dxb_ref, dg_ref):
        xv = x_ref[...]
        r = lax.rsqrt(jnp.mean(xv * xv, axis=-1, keepdims=True) + RMS_EPS)
        xhat = xv * r
        d = dh_ref[...]
        gd = d * g_ref[...]
        dx = r * (gd - xhat * jnp.mean(gd * xhat, axis=-1, keepdims=True)) + dr_ref[...]
        dx_ref[...] = dx
        dxb_ref[...] = dx.astype(BF16)

        @pl.when(pl.program_id(0) == 0)
        def _():
            dg_ref[...] = jnp.zeros(dg_ref.shape, F32)
        dg_ref[...] += jnp.sum(d * xhat, axis=0, keepdims=True)

    row = pl.BlockSpec((tm, D), lambda i: (i, 0))
    vec = pl.BlockSpec((1, D), lambda i: (0, 0))
    return pl.pallas_call(
        body, name=name, grid=(S // tm,),
        in_specs=[row, row, vec, row], out_specs=[row, row, vec],
        out_shape=[jax.ShapeDtypeStruct((S, D), F32), jax.ShapeDtypeStruct((S, D), BF16),
                   jax.ShapeDtypeStruct((1, D), F32)],
        compiler_params=_cparams(),
    )(dh, x, g, dres)


def _adamw(w, g, m, v, name):
    R, C = w.shape
    tr = _row_tile(R, 256)
    c1 = 1.0 - ADAM_B1 ** ADAM_STEP
    c2 = 1.0 - ADAM_B2 ** ADAM_STEP

    def body(w_ref, g_ref, m_ref, v_ref, d_ref, nm_ref, nv_ref):
        gv = g_ref[...]
        nm = ADAM_B1 * m_ref[...] + (1.0 - ADAM_B1) * gv
        nv = ADAM_B2 * v_ref[...] + (1.0 - ADAM_B2) * (gv * gv)
        d_ref[...] = -ADAM_LR * ((nm / c1) / (jnp.sqrt(nv / c2) + ADAM_EPS) + ADAM_WD * w_ref[...])
        nm_ref[...] = nm
        nv_ref[...] = nv

    spec = pl.BlockSpec((tr, C), lambda i: (i, 0))
    return pl.pallas_call(
        body, name=name, grid=(R // tr,),
        in_specs=[spec] * 4, out_specs=[spec] * 3,
        out_shape=[jax.ShapeDtypeStruct((R, C), F32)] * 3,
        compiler_params=_cparams(),
    )(w, g, m, v)


def _proj(h, wfull, col0, ncols, out_dtype, name, rope=None):
    S, K = h.shape
    tm = _row_tile(S, MM_TILE)
    tn = math.gcd(_tile(ncols, MM_TILE), col0) if col0 else _tile(ncols, MM_TILE)
    if rope is not None:
        tn = _tile(math.gcd(ncols, rope[1]), MM_TILE)
    assert ncols % tn == 0 and col0 % tn == 0
    cb = col0 // tn

    def body(*refs):
        if rope is None:
            a_ref, b_ref, o_ref = refs
        else:
            a_ref, b_ref, t_ref, o_ref = refs
        acc = _dot_nn(a_ref[...], b_ref[...])
        if rope is not None:
            t0, t1, t2 = (jnp.tile(t_ref[i], (1, tn // LANE)) for i in range(3))
            roped = (acc * t0 + pltpu.roll(acc, tn - ROT_DIM // 2, axis=1) * t1
                     + pltpu.roll(acc, ROT_DIM // 2, axis=1) * t2)
            acc = jnp.where(pl.program_id(1) < rope[1] // tn, roped, acc)
        o_ref[...] = acc.astype(out_dtype)

    in_specs = [pl.BlockSpec((tm, K), lambda i, j: (i, 0)), pl.BlockSpec((K, tn), lambda i, j: (0, cb + j))]
    args = [h, wfull]
    if rope is not None:
        in_specs.append(pl.BlockSpec((3, tm, LANE), lambda i, j: (0, i, 0)))
        args.append(rope[0])
    return pl.pallas_call(
        body, name=name, grid=(S // tm, ncols // tn),
        in_specs=in_specs, out_specs=pl.BlockSpec((tm, tn), lambda i, j: (i, j)),
        out_shape=jax.ShapeDtypeStruct((S, ncols), out_dtype),
        compiler_params=_cparams(),
    )(*args)


def _out_proj_norm(y, wo, xres, g, name):
    S, W = y.shape
    D = wo.shape[1]
    tm = _row_tile(S, 512)

    def body(a_ref, b_ref, r_ref, g_ref, x_ref, h_ref):
        xv = r_ref[...] + _dot_nn(a_ref[...], b_ref[...])
        x_ref[...] = xv
        r = lax.rsqrt(jnp.mean(xv * xv, axis=-1, keepdims=True) + RMS_EPS)
        h_ref[...] = ((xv * r) * g_ref[...]).astype(BF16)

    row = pl.BlockSpec((tm, D), lambda i: (i, 0))
    return pl.pallas_call(
        body, name=name, grid=(S // tm,),
        in_specs=[pl.BlockSpec((tm, W), lambda i: (i, 0)), pl.BlockSpec((W, D), lambda i: (0, 0)), row,
                  pl.BlockSpec((1, D), lambda i: (0, 0))],
        out_specs=[row, row],
        out_shape=[jax.ShapeDtypeStruct((S, D), F32), jax.ShapeDtypeStruct((S, D), BF16)],
        compiler_params=_cparams(),
    )(y, wo, xres, g)


def _out_proj_loss(y, wo, xres, tgt, g, name):
    S, W = y.shape
    D = wo.shape[1]
    tm = _row_tile(S, 512)

    def body(a_ref, b_ref, r_ref, t_ref, g_ref, dx_ref, dxb_ref, dg_ref, loss_ref):
        xv = r_ref[...] + _dot_nn(a_ref[...], b_ref[...])
        r = lax.rsqrt(jnp.mean(xv * xv, axis=-1, keepdims=True) + RMS_EPS)
        xhat = xv * r
        gv = g_ref[...]
        err = xhat * gv - t_ref[...]
        d = err * (1.0 / D)
        gd = d * gv
        dx = r * (gd - xhat * jnp.mean(gd * xhat, axis=-1, keepdims=True))
        dx_ref[...] = dx
        dxb_ref[...] = dx.astype(BF16)

        @pl.when(pl.program_id(0) == 0)
        def _():
            dg_ref[...] = jnp.zeros(dg_ref.shape, F32)
            loss_ref[...] = jnp.zeros(loss_ref.shape, F32)
        dg_ref[...] += jnp.sum(d * xhat, axis=0, keepdims=True)
        per_tok = jnp.sum(err * err, axis=-1, keepdims=True) * (1.0 / D)
        loss_ref[...] += 0.5 * jnp.sum(per_tok, axis=0, keepdims=True)

    row = pl.BlockSpec((tm, D), lambda i: (i, 0))
    vec = pl.BlockSpec((1, D), lambda i: (0, 0))
    return pl.pallas_call(
        body, name=name, grid=(S // tm,),
        in_specs=[pl.BlockSpec((tm, W), lambda i: (i, 0)), pl.BlockSpec((W, D), lambda i: (0, 0)), row, row, vec],
        out_specs=[row, row, vec, pl.BlockSpec((1, LANE), lambda i: (0, 0))],
        out_shape=[jax.ShapeDtypeStruct((S, D), F32), jax.ShapeDtypeStruct((S, D), BF16),
                   jax.ShapeDtypeStruct((1, D), F32), jax.ShapeDtypeStruct((1, LANE), F32)],
        compiler_params=_cparams(),
    )(y, wo, xres, tgt, g)


def _matmul_nt(parts, wfull, out_rows, name, scatter=(), scatter_specs=()):
    na = len(scatter)
    S = parts[0][0].shape[-2]
    tm, tn = _row_tile(S, MM_TILE), _tile(out_rows, MM_TILE)
    plan, lo = [], 0
    for arr, lead, col0 in parts:
        n_p = arr.shape[-1]
        tk = math.gcd(_tile(n_p, 1024), col0) if col0 else _tile(n_p, 1024)
        steps = n_p // tk
        plan.append((lead, col0 // tk, tk, lo, lo + steps))
        lo += steps
    nk = lo
    npart = len(parts)

    def body(*refs):
        a_refs, w_refs = refs[:npart], refs[npart:2 * npart]
        nin = 2 * npart + na
        o_ref, acc_ref = refs[nin], refs[nin + 1 + na]
        i, j, k = pl.program_id(0), pl.program_id(1), pl.program_id(2)
        if na:
            remote = _direct_scatter_copies(refs[2 * npart:nin], refs[nin + 1:nin + 1 + na], scatter_specs,
                                            *refs[nin + 2 + na:])

            @pl.when((i == 0) & (j == 0) & (k == 0))
            def _():
                for cp in remote:
                    cp.start()

        @pl.when(k == 0)
        def _():
            acc_ref[...] = jnp.zeros(acc_ref.shape, F32)
        for p, (_, _, _, lo_p, hi_p) in enumerate(plan):
            @pl.when((k >= lo_p) & (k < hi_p))
            def _(p=p):
                acc_ref[...] += _dot_nt(a_refs[p][...], w_refs[p][...])

        @pl.when(k == nk - 1)
        def _():
            o_ref[...] = acc_ref[...]

        if na:
            @pl.when((i == S // tm - 1) & (j == out_rows // tn - 1) & (k == nk - 1))
            def _():
                _wait_all(remote)

    in_specs, args = [], []
    for (arr, lead, col0), (_, cb, tk, lo_p, hi_p) in zip(parts, plan):
        def kk(k, lo_p=lo_p, hi_p=hi_p):
            return jnp.clip(k - lo_p, 0, hi_p - lo_p - 1)
        if lead is None:
            in_specs.append(pl.BlockSpec((tm, tk), lambda i, j, k, kk=kk: (i, kk(k))))
        else:
            in_specs.append(pl.BlockSpec((None, tm, tk), lambda i, j, k, kk=kk, lead=lead: (lead, i, kk(k))))
        args.append(arr)
    for (_, cb, tk, lo_p, hi_p) in plan:
        def kk(k, lo_p=lo_p, hi_p=hi_p):
            return jnp.clip(k - lo_p, 0, hi_p - lo_p - 1)
        in_specs.append(pl.BlockSpec((tn, tk), lambda i, j, k, kk=kk, cb=cb: (j, cb + kk(k))))
        args.append(wfull)
    any_spec = pl.BlockSpec(memory_space=pl.ANY)
    sems = [pltpu.SemaphoreType.DMA((na, N_DEV - 1)), pltpu.SemaphoreType.DMA((na, N_DEV - 1))] if na else []
    outs = pl.pallas_call(
        body, name=name, grid=(S // tm, out_rows // tn, nk),
        in_specs=in_specs + [any_spec] * na,
        out_specs=[pl.BlockSpec((tm, tn), lambda i, j, k: (i, j))] + [any_spec] * na,
        out_shape=[jax.ShapeDtypeStruct((S, out_rows), F32)]
        + [jax.ShapeDtypeStruct((N_DEV - 1,) + _scatter_block_shape(g, s), g.dtype)
           for g, s in zip(scatter, scatter_specs)],
        scratch_shapes=[pltpu.VMEM((tm, tn), F32)] + sems,
        compiler_params=_cparams(),
    )(*args, *scatter)
    return (outs[0], list(outs[1:])) if na else outs[0]


def _matmul_tn(a, parts, total, name, tile_major=False, also_bf16=False):
    S, M = a.shape
    tm, ts = _tile(M, MM_TILE), _row_tile(S, MM_TILE)
    nout = 2 if also_bf16 else 1
    outs = None
    for idx, (arr, lead, col0) in enumerate(parts):
        n_p = arr.shape[-1]
        tn = math.gcd(_tile(n_p, MM_TILE), col0) if col0 else _tile(n_p, MM_TILE)
        cb = col0 // tn
        nk = S // ts

        def body(*refs, nk=nk, tn=tn):
            a_ref, b_ref = refs[0], refs[1]
            o_refs, acc_ref = refs[-1 - nout:-1], refs[-1]
            k = pl.program_id(2)

            @pl.when(k == 0)
            def _():
                acc_ref[...] = jnp.zeros(acc_ref.shape, F32)
            acc_ref[...] += _dot_tn(a_ref[...], b_ref[...])

            @pl.when(k == nk - 1)
            def _():
                for o_ref in o_refs:
                    if tile_major:
                        for t in range(tn // LANE):
                            o_ref[t] = acc_ref[:, LANE * t:LANE * (t + 1)].astype(o_ref.dtype)
                    else:
                        o_ref[...] = acc_ref[...].astype(o_ref.dtype)

        in_specs = [pl.BlockSpec((ts, tm), lambda i, j, k: (k, i))]
        if lead is None:
            in_specs.append(pl.BlockSpec((ts, tn), lambda i, j, k: (k, j)))
        else:
            in_specs.append(pl.BlockSpec((None, ts, tn), lambda i, j, k, lead=lead: (lead, k, j)))
        args = [a, arr]
        aliases = {}
        if outs is not None:
            in_specs += [pl.BlockSpec(memory_space=pl.ANY)] * nout
            args += list(outs)
            aliases = {2 + o: o for o in range(nout)}
        if tile_major:
            out_spec = pl.BlockSpec((tn // LANE, tm, LANE), lambda i, j, k, cb=cb: (cb + j, i, 0))
            shape = (total // LANE, M, LANE)
        else:
            out_spec = pl.BlockSpec((tm, tn), lambda i, j, k, cb=cb: (i, cb + j))
            shape = (M, total)
        outs = pl.pallas_call(
            body, name=f"{name}_{idx}", grid=(M // tm, n_p // tn, nk),
            in_specs=in_specs, out_specs=[out_spec] * nout,
            out_shape=[jax.ShapeDtypeStruct(shape, dt) for dt in (F32, BF16)[:nout]],
            scratch_shapes=[pltpu.VMEM((tm, tn), F32)],
            input_output_aliases=aliases,
            compiler_params=_cparams(),
        )(*args)
    return tuple(outs) if also_bf16 else outs[0]


def _log_sigmoid(z):
    e = jnp.exp(-jnp.abs(z))
    return jnp.minimum(z, 0.0) - jnp.where(e < 1e-4, e * (1.0 - 0.5 * e), jnp.log(1.0 + e))


def _fox_gate_fwd(fl, bias):
    S = fl.shape[0]

    def body(f_ref, b_ref, c_ref):
        row = lax.broadcasted_iota(jnp.int32, (8, LANE), 0)

        def step(i, carry):
            r0 = pl.multiple_of(i * 8, 8)
            t = _log_sigmoid(f_ref[pl.ds(r0, 8), :] + b_ref[...])
            for sh in (1, 2, 4):
                t = t + jnp.where(row >= sh, pltpu.roll(t, sh, axis=0), 0.0)
            t = t + carry
            c_ref[pl.ds(r0, 8), :] = t
            return jnp.sum(jnp.where(row == 7, t, 0.0), axis=0, keepdims=True)

        lax.fori_loop(0, S // 8, step, jnp.zeros((1, LANE), F32))

    vm = pl.BlockSpec(memory_space=pltpu.VMEM)
    return pl.pallas_call(
        body, name="fox_gate_fwd", in_specs=[vm, vm], out_specs=vm,
        out_shape=jax.ShapeDtypeStruct((S, LANE), F32),
        compiler_params=_cparams(),
    )(fl, bias)


def _fox_gate_bwd(fl, bias, dc):
    S = fl.shape[0]

    def body(f_ref, b_ref, d_ref, o_ref, db_ref, acc_ref):
        row = lax.broadcasted_iota(jnp.int32, (8, LANE), 0)
        nt = S // 8

        def step(ii, carry):
            carry_c, carry_b = carry
            r0 = pl.multiple_of((nt - 1 - ii) * 8, 8)
            t = d_ref[pl.ds(r0, 8), :]
            for sh in (1, 2, 4):
                t = t + jnp.where(row < 8 - sh, pltpu.roll(t, 8 - sh, axis=0), 0.0)
            t = t + carry_c
            z = f_ref[pl.ds(r0, 8), :] + b_ref[...]
            dz = t * _sigmoid(-z)
            acc_ref[pl.ds(r0, 8), :] = dz
            first = jnp.sum(jnp.where(row == 0, t, 0.0), axis=0, keepdims=True)
            return first, carry_b + jnp.sum(dz, axis=0, keepdims=True)

        zero = jnp.zeros((1, LANE), F32)
        _, db = lax.fori_loop(0, nt, step, (zero, zero))
        db_ref[...] = db
        o_ref[...] = acc_ref[...].astype(BF16)

    vm = pl.BlockSpec(memory_space=pltpu.VMEM)
    return pl.pallas_call(
        body, name="fox_gate_bwd", in_specs=[vm, vm, vm], out_specs=[vm, vm],
        out_shape=[jax.ShapeDtypeStruct((S, LANE), BF16), jax.ShapeDtypeStruct((1, LANE), F32)],
        scratch_shapes=[pltpu.VMEM((S, LANE), F32)],
        compiler_params=_cparams(),
    )(fl, bias, dc)


def _bias_lanes(col, lane, e, first):
    o0 = HEAD_DIM * (1 - e)
    hi = col.astype(BF16)
    r1 = col - hi.astype(F32)
    mid = r1.astype(BF16)
    lo = (r1 - mid.astype(F32)).astype(BF16)
    d0 = o0 if first else o0 + 3
    t = jnp.where((lane >= o0) & (lane < o0 + 6), jnp.ones(lane.shape, BF16), jnp.zeros(lane.shape, BF16))
    t = jnp.where(lane == d0, hi, t)
    t = jnp.where(lane == d0 + 1, mid, t)
    return jnp.where(lane == d0 + 2, lo, t)


def _fox_fwd(qkv, gate, c, H, gather=()):
    na = len(gather)
    S = qkv.shape[0]
    W = H * HEAD_DIM
    HP = H // 2
    PP = 2 if HP % 2 == 0 else 1
    NE = 2 * PP
    tq = _row_tile(S, 512)
    nq = S // tq
    wb = W // LANE
    scale = HEAD_DIM ** -0.5

    def body(*refs):
        q_ref, k_ref, v_ref, g_ref, c_ref = refs[:5]
        y_ref, o_ref, a_ref = refs[5 + na:8 + na]
        kaug_sc, vaug_sc, qaug_sc, s_sc, mb_sc, m_sc, acc_sc = refs[8 + 2 * na:15 + 2 * na]
        hp, qi = pl.program_id(0), pl.program_id(1)
        if na:
            remote, local = _direct_gather_copies(refs[5:5 + na], refs[8 + na:8 + 2 * na], *refs[15 + 2 * na:])

            @pl.when((hp == 0) & (qi == 0))
            def _():
                for cp in remote + local:
                    cp.start()
        lane = lax.broadcasted_iota(jnp.int32, (tq, LANE), 1)
        own = [lane < HEAD_DIM, lane >= HEAD_DIM]
        rows = lax.broadcasted_iota(jnp.int32, (tq, tq), 0)
        cols = lax.broadcasted_iota(jnp.int32, (tq, tq), 1)

        def bias_lanes(col, e, first):
            return _bias_lanes(col, lane, e % 2, first)

        def head_col(tile, e):
            return jnp.sum(jnp.where(lane == 2 * PP * hp + e, tile, 0.0), axis=1, keepdims=True)

        def tile_of(e):
            return slice(LANE * (e // 2), LANE * (e // 2 + 1))

        @pl.when(qi == 0)
        def _():
            def chunk(i, carry):
                r0 = pl.multiple_of(i * tq, tq)
                cb = c_ref[pl.ds(r0, tq), :]
                for e in range(NE):
                    kb, vb = k_ref[pl.ds(r0, tq), tile_of(e)], v_ref[pl.ds(r0, tq), tile_of(e)]
                    kaug_sc[e, pl.ds(r0, tq), :] = jnp.where(own[e % 2], kb, bias_lanes(-head_col(cb, e), e, False))
                    vaug_sc[e, pl.ds(r0, tq), :] = jnp.where(own[e % 2], vb, jnp.ones((tq, LANE), BF16))
                return carry
            lax.fori_loop(0, nq, chunk, 0)

        crow = c_ref[pl.ds(pl.multiple_of(qi * tq, tq), tq), :]
        ctq = [head_col(crow, e) for e in range(NE)]
        for e in range(NE):
            q = q_ref[:, tile_of(e)] * jnp.asarray(scale, BF16)
            qaug_sc[e] = jnp.where(own[e % 2], q, bias_lanes(ctq[e], e, True))
        m_sc[...] = jnp.full(m_sc.shape, NEG_INF, F32)
        acc_sc[...] = jnp.zeros(acc_sc.shape, F32)

        def scores(blk, slot, masked):
            k0 = pl.multiple_of(blk * tq, tq)
            for e in range(NE):
                s = _dot_nt(qaug_sc[e], kaug_sc[e, pl.ds(k0, tq), :])
                if masked:
                    s = jnp.where(rows >= cols, s, NEG_INF)
                s_sc[slot, e] = s
                mb_sc[slot, e] = jnp.broadcast_to(jnp.max(s, axis=1, keepdims=True), (tq, LANE))

        def accumulate(blk, slot):
            k0 = pl.multiple_of(blk * tq, tq)
            for e in range(NE):
                m_prev = m_sc[e]
                m_new = jnp.maximum(m_prev, mb_sc[slot, e])
                p = jnp.exp(s_sc[slot, e] - jnp.tile(m_new, (1, tq // LANE)))
                acc_sc[e] = jnp.exp(m_prev - m_new) * acc_sc[e] + _dot_nn(p.astype(BF16), vaug_sc[e, pl.ds(k0, tq), :])
                m_sc[e] = m_new

        def block_of(t):
            return jnp.where(t == 0, qi, t - 1)

        scores(qi, 0, True)

        def loop_body(t, carry):
            scores(t, (t + 1) % 2, False)
            accumulate(block_of(t), t % 2)
            return carry

        lax.fori_loop(0, qi, loop_body, 0)
        accumulate(block_of(qi), qi % 2)
        o_e, a_e = [], []
        for e in range(NE):
            acc = acc_sc[e]
            l = pltpu.roll(acc, HEAD_DIM, axis=1)
            o_e.append(acc / l)
            a_e.append(ctq[e] - (m_sc[e] + jnp.log(l)))
        for pp in range(PP):
            o = jnp.where(own[0], o_e[2 * pp], o_e[2 * pp + 1])
            g = g_ref[:, tile_of(2 * pp)]
            y_ref[:, tile_of(2 * pp)] = (o * (g * _sigmoid(g))).astype(BF16)
            o_ref[:, tile_of(2 * pp)] = o.astype(BF16)
            a_ref[pp] = jnp.where(own[0], a_e[2 * pp], a_e[2 * pp + 1])
        if na:
            @pl.when((hp == HP // PP - 1) & (qi == nq - 1))
            def _():
                _wait_all(remote, local)

    any_spec = pl.BlockSpec(memory_space=pl.ANY)
    sems = [pltpu.SemaphoreType.DMA((na, N_DEV - 1)), pltpu.SemaphoreType.DMA((na, N_DEV - 1)),
            pltpu.SemaphoreType.DMA((na,))] if na else []
    wide = PP * LANE
    outs = pl.pallas_call(
        body, name="fox_attn_fwd", grid=(HP // PP, nq),
        in_specs=[pl.BlockSpec((tq, wide), lambda h, i: (i, h)),
                  pl.BlockSpec((S, wide), lambda h, i: (0, wb // PP + h)),
                  pl.BlockSpec((S, wide), lambda h, i: (0, 2 * wb // PP + h)),
                  pl.BlockSpec((tq, wide), lambda h, i: (i, h)),
                  pl.BlockSpec((S, LANE), lambda h, i: (0, 0))] + [any_spec] * na,
        out_specs=[pl.BlockSpec((tq, wide), lambda h, i: (i, h)),
                   pl.BlockSpec((tq, wide), lambda h, i: (i, h)),
                   pl.BlockSpec((PP, tq, LANE), lambda h, i: (h, i, 0))] + [any_spec] * na,
        out_shape=[jax.ShapeDtypeStruct((S, W), BF16), jax.ShapeDtypeStruct((S, W), BF16),
                   jax.ShapeDtypeStruct((HP, S, LANE), F32)]
        + [jax.ShapeDtypeStruct((N_DEV,) + g.shape, g.dtype) for g in gather],
        scratch_shapes=[pltpu.VMEM((NE, S, LANE), BF16), pltpu.VMEM((NE, S, LANE), BF16),
                        pltpu.VMEM((NE, tq, LANE), BF16), pltpu.VMEM((2, NE, tq, tq), F32),
                        pltpu.VMEM((2, NE, tq, LANE), F32), pltpu.VMEM((NE, tq, LANE), F32),
                        pltpu.VMEM((NE, tq, LANE), F32)] + sems,
        compiler_params=_cparams(),
    )(qkv, qkv, qkv, gate, c, *gather)
    return outs[0], outs[1], outs[2], list(outs[3:])


def _fox_out_bwd(dxb, wo, qkv, gate, o, a, H):
    S, D = dxb.shape
    W = H * HEAD_DIM
    tm, tn = _row_tile(S, 512), _tile(W, 512)
    npair = tn // LANE
    scale = HEAD_DIM ** -0.5

    def body(dx_ref, w_ref, q_ref, g_ref, o_ref, a_ref, qa_ref, da_ref, dg_ref):
        dy = _dot_nt(dx_ref[...], w_ref[...])
        lane = lax.broadcasted_iota(jnp.int32, (tm, LANE), 1)
        own = [lane < HEAD_DIM, lane >= HEAD_DIM]
        for p in range(npair):
            cols = slice(LANE * p, LANE * (p + 1))
            q = q_ref[:, cols] * jnp.asarray(scale, BF16)
            dyv, g, ov, at = dy[:, cols], g_ref[:, cols], o_ref[:, cols].astype(F32), a_ref[p]
            sg = _sigmoid(g)
            dob = (dyv * (g * sg)).astype(BF16)
            dg_ref[:, cols] = (dyv * ov * (sg * (1.0 + g * (1.0 - sg)))).astype(BF16)
            prod = dob.astype(F32) * ov
            for e in range(2):
                a_col = jnp.max(jnp.where(own[e], at, -jnp.inf), axis=1, keepdims=True)
                d_col = jnp.sum(jnp.where(own[e], prod, 0.0), axis=1, keepdims=True)
                qa_ref[e, :, cols] = jnp.where(own[e], q, _bias_lanes(a_col, lane, e, True))
                da_ref[e, :, cols] = jnp.where(own[e], dob, _bias_lanes(-d_col, lane, e, True))

    blk = pl.BlockSpec((tm, tn), lambda i, j: (i, j))
    pair = pl.BlockSpec((2, tm, tn), lambda i, j: (0, i, j))
    return pl.pallas_call(
        body, name="fox_out_bwd", grid=(S // tm, W // tn),
        in_specs=[pl.BlockSpec((tm, D), lambda i, j: (i, 0)), pl.BlockSpec((tn, D), lambda i, j: (j, 0)),
                  blk, blk, blk, pl.BlockSpec((npair, tm, LANE), lambda i, j: (j, i, 0))],
        out_specs=[pair, pair, blk],
        out_shape=[jax.ShapeDtypeStruct((2, S, W), BF16), jax.ShapeDtypeStruct((2, S, W), BF16),
                   jax.ShapeDtypeStruct((S, W), BF16)],
        compiler_params=_cparams(),
    )(dxb, wo, qkv, gate, o, a)


def _fox_bwd(qaug, doaug, qkv, c, H, scatter=(), scatter_specs=()):
    na = len(scatter)
    S = qkv.shape[0]
    W = H * HEAD_DIM
    HP = H // 2
    tq = _row_tile(S, 512)
    nq = S // tq
    wb = W // LANE
    scale = HEAD_DIM ** -0.5

    def body(*refs):
        qa_ref, da_ref, k_ref, v_ref, c_ref = refs[:5]
        out_ref, dcr_ref, dcc_ref = refs[5 + na:8 + na]
        dq_sc, dk_sc, dv_sc = refs[8 + 2 * na:11 + 2 * na]
        hp, kj = pl.program_id(0), pl.program_id(1)
        if na:
            remote = _direct_scatter_copies(refs[5:5 + na], refs[8 + na:8 + 2 * na], scatter_specs,
                                            *refs[11 + 2 * na:])

            @pl.when((hp == 0) & (kj == 0))
            def _():
                for cp in remote:
                    cp.start()
        lane = lax.broadcasted_iota(jnp.int32, (tq, LANE), 1)
        own = [lane < HEAD_DIM, lane >= HEAD_DIM]
        rows = lax.broadcasted_iota(jnp.int32, (tq, tq), 0)
        cols = lax.broadcasted_iota(jnp.int32, (tq, tq), 1)

        @pl.when(kj == 0)
        def _():
            dq_sc[...] = jnp.zeros(dq_sc.shape, F32)

        @pl.when((kj == 0) & (hp == 0))
        def _():
            dcr_ref[...] = jnp.zeros(dcr_ref.shape, F32)
            dcc_ref[...] = jnp.zeros(dcc_ref.shape, F32)

        kblk, vblk, cblk = k_ref[...], v_ref[...], c_ref[...]
        one, zero = jnp.ones((tq, LANE), BF16), jnp.zeros((tq, LANE), BF16)
        ka, va = [], []
        for e in range(2):
            o0 = HEAD_DIM * (1 - e)
            c_col = jnp.sum(jnp.where(lane == 2 * hp + e, cblk, 0.0), axis=1, keepdims=True)
            ka.append(jnp.where(own[e], kblk, _bias_lanes(-c_col, lane, e, False)))
            va.append(jnp.where(own[e], vblk, jnp.where((lane >= o0) & (lane < o0 + 3), one, zero)))
        dk_sc[...] = jnp.zeros(dk_sc.shape, F32)
        dv_sc[...] = jnp.zeros(dv_sc.shape, F32)

        def step(i, masked):
            r0 = pl.multiple_of(i * tq, tq)
            for e in range(2):
                qa = qa_ref[e, pl.ds(r0, tq), :]
                da = da_ref[e, pl.ds(r0, tq), :]
                p = jnp.exp(_dot_nt(qa, ka[e]))
                if masked:
                    p = jnp.where(rows >= cols, p, 0.0)
                ds = p * _dot_nt(da, va[e])
                pb, dsb = p.astype(BF16), ds.astype(BF16)
                dv_sc[e] += _dot_tn(pb, da)
                dk_sc[e] += _dot_tn(dsb, qa)
                dq_sc[e, pl.ds(r0, tq), :] += _dot_nn(dsb, ka[e])

        step(kj, True)

        def loop_body(i, carry):
            step(i, False)
            return carry

        lax.fori_loop(kj + 1, nq, loop_body, 0)
        k0 = pl.multiple_of(kj * tq, tq)
        out_ref[1, pl.ds(k0, tq), :] = jnp.where(own[0], dk_sc[0], dk_sc[1]).astype(BF16)
        out_ref[2, pl.ds(k0, tq), :] = jnp.where(own[0], dv_sc[0], dv_sc[1]).astype(BF16)

        def put_lane(ref, r0, e, tile, src_lane):
            col = jnp.sum(jnp.where(lane == src_lane, tile, 0.0), axis=1, keepdims=True)
            ref[pl.ds(r0, tq), :] = jnp.where(lane == 2 * hp + e, col, ref[pl.ds(r0, tq), :])

        for e in range(2):
            put_lane(dcc_ref, k0, e, dk_sc[e], HEAD_DIM * (1 - e) + 3)

        @pl.when(kj == nq - 1)
        def _():
            def chunk(i, carry):
                r0 = pl.multiple_of(i * tq, tq)
                d0, d1 = dq_sc[0, pl.ds(r0, tq), :], dq_sc[1, pl.ds(r0, tq), :]
                out_ref[0, pl.ds(r0, tq), :] = (jnp.where(own[0], d0, d1) * scale).astype(BF16)
                put_lane(dcr_ref, r0, 0, d0, HEAD_DIM)
                put_lane(dcr_ref, r0, 1, d1, 0)
                return carry
            lax.fori_loop(0, nq, chunk, 0)

        if na:
            @pl.when((hp == HP - 1) & (kj == nq - 1))
            def _():
                _wait_all(remote)

    pair = pl.BlockSpec((2, S, LANE), lambda h, j: (0, 0, h))
    vec = pl.BlockSpec((S, LANE), lambda h, j: (0, 0))
    any_spec = pl.BlockSpec(memory_space=pl.ANY)
    sems = [pltpu.SemaphoreType.DMA((na, N_DEV - 1)), pltpu.SemaphoreType.DMA((na, N_DEV - 1))] if na else []
    outs = pl.pallas_call(
        body, name="fox_attn_bwd", grid=(HP, nq),
        in_specs=[pair, pair,
                  pl.BlockSpec((tq, LANE), lambda h, j: (j, wb + h)),
                  pl.BlockSpec((tq, LANE), lambda h, j: (j, 2 * wb + h)),
                  pl.BlockSpec((tq, LANE), lambda h, j: (j, 0))] + [any_spec] * na,
        out_specs=[pl.BlockSpec((3, S, LANE), lambda h, j: (0, 0, h)), vec, vec] + [any_spec] * na,
        out_shape=[jax.ShapeDtypeStruct((3, S, W), BF16), jax.ShapeDtypeStruct((S, LANE), F32),
                   jax.ShapeDtypeStruct((S, LANE), F32)]
        + [jax.ShapeDtypeStruct((N_DEV - 1,) + _scatter_block_shape(g, s), g.dtype)
           for g, s in zip(scatter, scatter_specs)],
        scratch_shapes=[pltpu.VMEM((2, S, LANE), F32), pltpu.VMEM((2, tq, LANE), F32),
                        pltpu.VMEM((2, tq, LANE), F32)] + sems,
        compiler_params=_cparams(),
    )(qaug, doaug, qkv, qkv, c, *scatter)
    return outs[0], outs[1], outs[2], list(outs[3:])


def _swa_pick(blk, half, lane):
    b = blk.astype(F32)
    r = pltpu.roll(b, HEAD_DIM, axis=1)
    return jnp.where(jnp.logical_xor(lane < HEAD_DIM, half == 1), b, r).astype(BF16)


def _swa_stack(t, lane, G):
    pieces = []
    z = jnp.zeros((SWA_BLOCK, LANE), t.dtype)
    for j in range(G // 2):
        tile = t[:, LANE * j:LANE * (j + 1)]
        pieces += [jnp.where(lane < HEAD_DIM, tile, z), jnp.where(lane < HEAD_DIM, z, tile)]
    return jnp.concatenate(pieces, axis=0)


def _swa_unstack(st, lane, G):
    tiles = []
    for j in range(G // 2):
        a = st[2 * j * SWA_BLOCK:(2 * j + 1) * SWA_BLOCK]
        b = st[(2 * j + 1) * SWA_BLOCK:(2 * j + 2) * SWA_BLOCK]
        tiles.append(jnp.where(lane < HEAD_DIM, a, b))
    return jnp.concatenate(tiles, axis=1)


def _swa_mask_bias(G):
    R = G * SWA_BLOCK
    t_loc = jnp.arange(R)[:, None] % SWA_BLOCK
    j_loc = jnp.arange(2 * SWA_BLOCK)[None, :]
    diff = t_loc + SWA_BLOCK - j_loc
    band = (diff >= 0) & (diff < SWA_BLOCK)
    return jnp.stack([jnp.where(band & (j_loc >= SWA_BLOCK), 0.0, NEG_INF),
                      jnp.where(band, 0.0, NEG_INF)]).astype(F32)


def _swa_scores(q, kp, kc, vp, vc, srow, bias, half, head0, G):
    lane = lax.broadcasted_iota(jnp.int32, (SWA_BLOCK, LANE), 1)
    kk = jnp.concatenate([_swa_pick(kp, half, lane), _swa_pick(kc, half, lane)], axis=0)
    vv = jnp.concatenate([_swa_pick(vp, half, lane), _swa_pick(vc, half, lane)], axis=0)
    qstack = _swa_stack(q, lane, G) * jnp.asarray(HEAD_DIM ** -0.5, BF16)
    s = _dot_nt(qstack, kk) + bias
    R = G * SWA_BLOCK
    lane1 = lax.broadcasted_iota(jnp.int32, (1, LANE), 1)
    sink = jnp.concatenate(
        [jnp.broadcast_to(jnp.sum(jnp.where(lane1 == head0 + g, srow, 0.0), axis=1, keepdims=True), (SWA_BLOCK, LANE))
         for g in range(G)], axis=0)
    m = jnp.maximum(jnp.broadcast_to(jnp.max(s, axis=1, keepdims=True), (R, LANE)), sink)
    e = jnp.exp(s - jnp.tile(m, (1, 2)))
    es = jnp.exp(sink - m)
    inv = 1.0 / (jnp.broadcast_to(jnp.sum(e, axis=1, keepdims=True), (R, LANE)) + es)
    return qstack, kk, vv, e * jnp.tile(inv, (1, 2)), es * inv, lane


def _swa_fwd(qkv, gate, sinks, mask_bias, HQ, HKV):
    S = qkv.shape[0]
    G = HQ // HKV
    WQ, KVW = HQ * HEAD_DIM, HKV * HEAD_DIM
    nb = S // SWA_BLOCK
    GW = G * HEAD_DIM
    kb, vb = WQ // LANE, (WQ + KVW) // LANE

    def body(q_ref, kp_ref, kc_ref, vp_ref, vc_ref, g_ref, sink_ref, b_ref, y_ref, o_ref):
        pair = pl.program_id(0)
        for half in range(2):
            cols = slice(GW * half, GW * (half + 1))
            _, _, vv, p, _, lane = _swa_scores(q_ref[:, cols], kp_ref[...], kc_ref[...], vp_ref[...], vc_ref[...],
                                               sink_ref[...], b_ref[0], half, (2 * pair + half) * G, G)
            o = _swa_unstack(_dot_nn(p.astype(BF16), vv), lane, G)
            g = g_ref[:, cols]
            y_ref[:, cols] = (o * (g * _sigmoid(g))).astype(BF16)
            o_ref[:, cols] = o.astype(BF16)

    blk = lambda cb, prev: pl.BlockSpec(
        (SWA_BLOCK, LANE), lambda h, n, cb=cb, prev=prev: (jnp.maximum(n - prev, 0), cb + h))
    qspec = pl.BlockSpec((SWA_BLOCK, 2 * GW), lambda h, n: (n, h))
    return pl.pallas_call(
        body, name="swa_attn_fwd", grid=(HKV // 2, nb),
        in_specs=[qspec, blk(kb, 1), blk(kb, 0), blk(vb, 1), blk(vb, 0), qspec,
                  pl.BlockSpec((1, LANE), lambda h, n: (0, 0)),
                  pl.BlockSpec((1, G * SWA_BLOCK, 2 * SWA_BLOCK), lambda h, n: (jnp.minimum(n, 1), 0, 0))],
        out_specs=[qspec, qspec],
        out_shape=[jax.ShapeDtypeStruct((S, WQ), BF16), jax.ShapeDtypeStruct((S, WQ), BF16)],
        compiler_params=_cparams(),
    )(qkv, qkv, qkv, qkv, qkv, gate, sinks, mask_bias)


def _swa_bwd(qkv, dy, gate, o, sinks, tables, mask_bias, HQ, HKV):
    S = qkv.shape[0]
    G = HQ // HKV
    WQ, KVW = HQ * HEAD_DIM, HKV * HEAD_DIM
    nb = S // SWA_BLOCK
    GW = G * HEAD_DIM
    R = G * SWA_BLOCK
    kb, vb = WQ // LANE, (WQ + KVW) // LANE
    scale = HEAD_DIM ** -0.5
    assert G == 8

    def body(q_ref, kp_ref, kc_ref, vp_ref, vc_ref, dy_ref, g_ref, o_ref, sink_ref, t_ref, b_ref,
             dqg_ref, dkv_ref, dsink_ref, carry_sc):
        pair, n = pl.program_id(0), pl.program_id(1)

        @pl.when(n == 0)
        def _():
            carry_sc[...] = jnp.zeros(carry_sc.shape, F32)
            dsink_ref[...] = jnp.zeros(dsink_ref.shape, F32)

        @pl.when(n < nb)
        def _():
            t0, t1, t2 = (jnp.tile(t_ref[i], (1, GW // LANE)) for i in range(3))
            for half in range(2):
                cols = slice(GW * half, GW * (half + 1))
                qstack, kk, vv, p, psink, lane = _swa_scores(
                    q_ref[:, cols], kp_ref[...], kc_ref[...], vp_ref[...], vc_ref[...], sink_ref[...], b_ref[0],
                    half, (2 * pair + half) * G, G)
                dyv, g, ov = dy_ref[:, cols], g_ref[:, cols], o_ref[:, cols].astype(F32)
                sg = _sigmoid(g)
                dob = (dyv * (g * sg)).astype(BF16)
                dqg_ref[1, :, cols] = (dyv * ov * (sg * (1.0 + g * (1.0 - sg)))).astype(BF16)
                prod = dob.astype(F32) * ov
                dparts = []
                for j in range(G // 2):
                    tile = prod[:, LANE * j:LANE * (j + 1)]
                    for sel in (jnp.where(lane < HEAD_DIM, tile, 0.0), jnp.where(lane < HEAD_DIM, 0.0, tile)):
                        dparts.append(jnp.broadcast_to(jnp.sum(sel, axis=1, keepdims=True), (SWA_BLOCK, LANE)))
                delta = jnp.concatenate(dparts, axis=0)
                dostack = _swa_stack(dob, lane, G)
                ds = p * (_dot_nt(dostack, vv) - jnp.tile(delta, (1, 2)))
                dsb, pb = ds.astype(BF16), p.astype(BF16)
                dq = _swa_unstack(_dot_nn(dsb, kk), lane, G) * scale
                dq = dq * t0 + pltpu.roll(dq * t1, ROT_DIM // 2, axis=1) + pltpu.roll(dq * t2, GW - ROT_DIM // 2, axis=1)
                dqg_ref[0, :, cols] = dq.astype(BF16)
                dkk = _dot_tn(dsb, qstack)
                dvv = _dot_tn(pb, dostack)
                dkk = dkk + pltpu.roll(dkk, HEAD_DIM, axis=1)
                dvv = dvv + pltpu.roll(dvv, HEAD_DIM, axis=1)
                lane2 = lax.broadcasted_iota(jnp.int32, (2 * SWA_BLOCK, LANE), 1)
                comb = jnp.where(lane2 < HEAD_DIM, dkk, dvv)
                dkv_ref[half] = carry_sc[half] + comb[:SWA_BLOCK]
                carry_sc[half] = comb[SWA_BLOCK:]
                sk = psink * delta
                rows = [-jnp.sum(sk[g_ * SWA_BLOCK:(g_ + 1) * SWA_BLOCK], axis=0, keepdims=True) for g_ in range(G)]
                dsink_ref[half] += jnp.concatenate(rows, axis=0)

        @pl.when(n == nb)
        def _():
            dkv_ref[...] = carry_sc[...]

    cl = lambda n: jnp.minimum(n, nb - 1)
    blk = lambda cb, prev: pl.BlockSpec(
        (SWA_BLOCK, LANE), lambda h, n, cb=cb, prev=prev: (jnp.maximum(cl(n) - prev, 0), cb + h))
    qspec = pl.BlockSpec((SWA_BLOCK, 2 * GW), lambda h, n: (cl(n), h))
    return pl.pallas_call(
        body, name="swa_attn_bwd", grid=(HKV // 2, nb + 1),
        in_specs=[qspec, blk(kb, 1), blk(kb, 0), blk(vb, 1), blk(vb, 0), qspec, qspec, qspec,
                  pl.BlockSpec((1, LANE), lambda h, n: (0, 0)),
                  pl.BlockSpec((3, SWA_BLOCK, LANE), lambda h, n: (0, cl(n), 0)),
                  pl.BlockSpec((1, R, 2 * SWA_BLOCK), lambda h, n: (jnp.minimum(n, 1), 0, 0))],
        out_specs=[pl.BlockSpec((2, SWA_BLOCK, 2 * GW), lambda h, n: (0, cl(n), h)),
                   pl.BlockSpec((2, SWA_BLOCK, LANE), lambda h, n: (h, jnp.maximum(n - 1, 0), 0)),
                   pl.BlockSpec((2, 8, LANE), lambda h, n: (h, 0, 0))],
        out_shape=[jax.ShapeDtypeStruct((2, S, WQ), BF16), jax.ShapeDtypeStruct((HKV, S, LANE), F32),
                   jax.ShapeDtypeStruct((HKV, 8, LANE), F32)],
        scratch_shapes=[pltpu.VMEM((2, SWA_BLOCK, LANE), F32)],
        compiler_params=_cparams(),
    )(qkv, qkv, qkv, qkv, qkv, dy, gate, o, sinks, tables, mask_bias)


def _swa_dkv_finish(dkv, tables):
    HKV, S, _ = dkv.shape
    KVW = HKV * HEAD_DIM
    tm = _row_tile(S, 512)
    npair = HKV // 2

    def body(d_ref, t_ref, o_ref):
        lane = lax.broadcasted_iota(jnp.int32, (tm, LANE), 1)
        lo = lane < HEAD_DIM
        for p in range(npair):
            a, b = d_ref[2 * p], d_ref[2 * p + 1]
            tk = jnp.where(lo, a, pltpu.roll(b, HEAD_DIM, axis=1))
            tv = jnp.where(lo, pltpu.roll(a, HEAD_DIM, axis=1), b)
            tk = (tk * t_ref[0] + pltpu.roll(tk * t_ref[1], ROT_DIM // 2, axis=1)
                  + pltpu.roll(tk * t_ref[2], LANE - ROT_DIM // 2, axis=1))
            o_ref[:, LANE * p:LANE * (p + 1)] = tk.astype(BF16)
            o_ref[:, KVW + LANE * p:KVW + LANE * (p + 1)] = tv.astype(BF16)

    return pl.pallas_call(
        body, name="swa_dkv_finish", grid=(S // tm,),
        in_specs=[pl.BlockSpec((HKV, tm, LANE), lambda i: (0, i, 0)), pl.BlockSpec((3, tm, LANE), lambda i: (0, i, 0))],
        out_specs=pl.BlockSpec((tm, 2 * KVW), lambda i: (i, 0)),
        out_shape=jax.ShapeDtypeStruct((S, 2 * KVW), BF16),
        compiler_params=_cparams(),
    )(dkv, tables)


def _rope_tables(S, width):
    half = ROT_DIM // 2
    pos = jnp.arange(S, dtype=F32)
    inv_freq = ROPE_THETA ** (-jnp.arange(half, dtype=F32) / half)
    ang = pos[:, None] * inv_freq[None, :]
    cos, sin = jnp.cos(ang), jnp.sin(ang)
    one = jnp.ones((S, HEAD_DIM - ROT_DIM), F32)
    zero = jnp.zeros((S, HEAD_DIM - ROT_DIM), F32)
    zh = jnp.zeros((S, half), F32)
    t0 = jnp.concatenate([cos, cos, one], axis=1)
    t1 = jnp.concatenate([-sin, zh, zero], axis=1)
    t2 = jnp.concatenate([zh, sin, zero], axis=1)
    return jnp.stack([jnp.tile(t, (1, width // HEAD_DIM)) for t in (t0, t1, t2)])


def _pad_rows(v, row, total_rows=8):
    return jnp.pad(v, ((row, total_rows - row - v.shape[0]), (0, 0)))


def _pad_lanes(v, off, width):
    return jnp.pad(v, ((0, 0), (off, width - off - v.shape[1])))


def kernel(x, norm_g, fox_w_in, fox_b_f, fox_w_out, swa_w_in, swa_sinks, swa_w_out, final_g, loss_target, m_norm_g, m_fox_w_in, m_fox_b_f, m_fox_w_out, m_swa_w_in, m_swa_sinks, m_swa_w_out, m_final_g, v_norm_g, v_fox_w_in, v_fox_b_f, v_fox_w_out, v_swa_w_in, v_swa_sinks, v_swa_w_out, v_final_g):
    S, D = x.shape[1], x.shape[2]
    H = fox_b_f.shape[1]
    W = H * HEAD_DIM
    wf = fox_w_in.shape[2]
    ws = swa_w_in.shape[2]
    HQ = swa_sinks.shape[1]
    WQ = HQ * HEAD_DIM
    KVW = (ws * N_DEV - 2 * WQ) // 2
    HKV = KVW // HEAD_DIM
    rows_o = fox_w_out.shape[1]
    assert wf * N_DEV == 4 * W + H and rows_o * N_DEV == W and H <= LANE and HQ <= LANE
    me = _my_index()

    _, sw_f, np_f = _slab_geom(wf)
    _, sw_s, np_s = _slab_geom(ws)

    def slab(w2d, w, sw):
        return jnp.pad(w2d.astype(BF16), ((0, 0), (0, sw - w)))

    (fi_all,) = _all_gather([slab(fox_w_in[0], wf, sw_f)])
    w_fi = _assemble(fi_all, wf)
    later = [slab(swa_w_in[0], ws, sw_s), fox_w_out[0].astype(BF16), swa_w_out[0].astype(BF16)]

    x0 = x[0]
    g0, g1, gf = norm_g[0:1], norm_g[1:2], final_g[None, :]
    bias = _pad_lanes(fox_b_f, 0, LANE)
    sinks = _pad_lanes(swa_sinks, 0, LANE)
    tab_k = _rope_tables(S, LANE)
    mask_bias = _swa_mask_bias(HQ // HKV)

    h0 = _rmsnorm_fwd(x0, g0, "rmsnorm0")
    qkv0 = _proj(h0, w_fi, 0, 3 * W, BF16, "fox_in_qkv")
    gate0 = _proj(h0, w_fi, 3 * W, W, F32, "fox_in_gate")
    fl = _proj(h0, w_fi, 4 * W, LANE, F32, "fox_in_f")
    c = _fox_gate_fwd(fl, bias)
    y0, o0, a0, (si_all, fo_all, so_all) = _fox_fwd(qkv0, gate0, c, H, gather=later)
    w_si = _assemble(si_all, ws)
    w_fo = fo_all.reshape(W, D)
    w_so = so_all.reshape(WQ, D)
    x1, h1 = _out_proj_norm(y0, w_fo, x0, g1, "fox_out")

    qkv1 = _proj(h1, w_si, 0, WQ + 2 * KVW, BF16, "swa_in_qkv", rope=(tab_k, WQ + KVW))
    gate1 = _proj(h1, w_si, WQ + 2 * KVW, WQ, F32, "swa_in_gate")
    y1, o1 = _swa_fwd(qkv1, gate1, sinks, mask_bias, HQ, HKV)
    dx2, dx2b, dgf, loss_p = _out_proj_loss(y1, w_so, x1, loss_target[0], gf, "swa_out_loss")

    dy1 = _matmul_nt([(dx2b, None, 0)], w_so, WQ, "swa_out_bwd")
    g_so, g_so_h = _matmul_tn(y1, [(dx2b, None, 0)], D, "swa_out_wgrad", also_bf16=True)
    dqg1, dkv1, dsink = _swa_bwd(qkv1, dy1, gate1, o1, sinks, tab_k, mask_bias, HQ, HKV)
    dkv1f = _swa_dkv_finish(dkv1, tab_k)
    parts1 = [(dqg1, 0, 0), (dkv1f, None, WQ), (dqg1, 1, WQ + 2 * KVW)]
    g_si, g_si_h = _matmul_tn(h1, parts1, np_s, "swa_in_wgrad", tile_major=True, also_bf16=True)
    dh1 = _matmul_nt(parts1, w_si, D, "swa_in_bwd")
    dx1, dx1b, dg1 = _rmsnorm_bwd(dh1, x1, g1, dx2, "rmsnorm1_bwd")

    qaug0, doaug0, dgate0 = _fox_out_bwd(dx1b, w_fo, qkv0, gate0, o0, a0, H)
    g_fo, g_fo_h = _matmul_tn(y0, [(dx1b, None, 0)], D, "fox_out_wgrad", also_bf16=True)
    early_specs = [("col", ws), ("row", rows_o), ("row", rows_o)]
    dqkv0, dcr, dcc, early_recv = _fox_bwd(qaug0, doaug0, qkv0, c, H, scatter=[g_si_h, g_fo_h, g_so_h],
                                          scatter_specs=early_specs)
    dfl, dbf = _fox_gate_bwd(fl, bias, dcr - dcc)
    parts0 = [(dqkv0, p, p * W) for p in range(3)] + [(dgate0, None, 3 * W), (dfl, None, 4 * W)]
    g_fi, g_fi_h = _matmul_tn(h0, parts0, np_f, "fox_in_wgrad", tile_major=True, also_bf16=True)
    spec_fi = ("col", wf)
    fi_sems, fi_src, fi_land, token = _scatter_start(g_fi_h, spec_fi)
    parts0[-1] = (dfl + token[0, 0].astype(BF16), None, 4 * W)
    dh0 = _matmul_nt(parts0, w_fi, D, "fox_in_bwd")
    dx0, _, dg0 = _rmsnorm_bwd(dh0, x0, g0, dx1, "rmsnorm0_bwd")

    red_si, gw_fo, gw_so = [_final_sum8(g_, r_, s_)
                            for g_, r_, s_ in zip([g_si, g_fo, g_so], early_recv, early_specs)]
    gw_si = lax.dynamic_slice(red_si, (0, (ws * me) % LANE), (D, ws))

    P = D
    dsink_v = dsink[:, :, 0].reshape(1, HQ)
    row3 = _pad_lanes(dbf[:, :H], 0, P) + _pad_lanes(dsink_v, LANE, P) + _pad_lanes(loss_p[:, :1], 2 * LANE, P)
    pack = _pad_rows(dg0, 0) + _pad_rows(dg1, 1) + _pad_rows(dgf, 2) + _pad_rows(row3, 3)
    tot = _all_reduce_small(pack)
    loss = tot[3, 2 * LANE]
    g_norm = tot[0:2]
    g_final = tot[2]
    g_bf = tot[3:4, 0:H]
    g_sinks = tot[3:4, LANE:LANE + HQ]

    def small_pack(ng, fg, bf, sk):
        r3 = _pad_lanes(bf, 0, P) + _pad_lanes(sk, LANE, P)
        return _pad_rows(ng, 0) + _pad_rows(fg[None, :], 2) + _pad_rows(r3, 3)

    sd, sm, sv = _adamw(small_pack(norm_g, final_g, fox_b_f, swa_sinks), tot,
                        small_pack(m_norm_g, m_final_g, m_fox_b_f, m_swa_sinks),
                        small_pack(v_norm_g, v_final_g, v_fox_b_f, v_swa_sinks), "adamw_small")

    def unpack(t):
        return t[0:2], t[3:4, 0:H], t[3:4, LANE:LANE + HQ], t[2]

    d_fo, m_fo, v_fo = _adamw(fox_w_out[0], gw_fo, m_fox_w_out[0], v_fox_w_out[0], "adamw_fox_out")
    d_si, m_si, v_si = _adamw(swa_w_in[0], gw_si, m_swa_w_in[0], v_swa_w_in[0], "adamw_swa_in")
    d_so, m_so, v_so = _adamw(swa_w_out[0], gw_so, m_swa_w_out[0], v_swa_w_out[0], "adamw_swa_out")
    recv_fi = _scatter_wait(fi_sems, fi_src, fi_land, spec_fi, after=[dx0, sd, d_fo, d_si, d_so])
    red_fi = _final_sum8(g_fi, recv_fi, spec_fi)
    gw_fi = lax.dynamic_slice(red_fi, (0, (wf * me) % LANE), (D, wf))
    d_fi, m_fi, v_fi = _adamw(fox_w_in[0], gw_fi, m_fox_w_in[0], v_fox_w_in[0], "adamw_fox_in")

    def group(small, fi, fo, si, so):
        ng, bf, sk, fg = unpack(small)
        return (ng, fi[None], bf, fo[None], si[None], sk, so[None], fg)

    grads = (g_norm, gw_fi[None], g_bf, gw_fo[None], gw_si[None], g_sinks, gw_so[None], g_final)
    return (loss, dx0[None], *grads, *group(sd, d_fi, d_fo, d_si, d_so),
            *group(sm, m_fi, m_fo, m_si, m_so), *group(sv, v_fi, v_fo, v_si, v_so))
```

```python
import math

import jax
import jax.numpy as jnp
from jax import lax
from jax.experimental import pallas as pl
from jax.experimental.pallas import tpu as pltpu

F32 = jnp.float32
BF16 = jnp.bfloat16
MESH = pl.DeviceIdType.MESH

N_DEV = 8
LANE = 128
HEAD_DIM = 64
SWA_BLOCK = 128
NEG_INF = -1e30
RMS_EPS = 1e-6
ROPE_THETA = 500000.0
ROT_DIM = HEAD_DIM // 4
ADAM_LR, ADAM_B1, ADAM_B2, ADAM_EPS, ADAM_WD, ADAM_STEP = 0.001, 0.9, 0.999, 1e-08, 0.01, 10
VMEM_LIMIT = 56 * 1024 * 1024
MM_TILE = 1024


def _cparams(**kw):
    return pltpu.CompilerParams(vmem_limit_bytes=VMEM_LIMIT, **kw)


def _tile(n, cap):
    if n <= cap:
        return n
    t = (cap // LANE) * LANE
    while t > LANE and n % t:
        t -= LANE
    assert n % t == 0, (n, cap)
    return t


def _row_tile(n, cap):
    t = min(n, cap)
    while n % t:
        t //= 2
    return t


def _dot_nn(a, b):
    return jnp.dot(a, b, preferred_element_type=F32)


def _dot_nt(a, b):
    return lax.dot_general(a, b, (((1,), (1,)), ((), ())), preferred_element_type=F32)


def _dot_tn(a, b):
    return lax.dot_general(a, b, (((0,), (0,)), ((), ())), preferred_element_type=F32)


def _sigmoid(g):
    return 1.0 / (1.0 + jnp.exp(-g))


def _slab_geom(w):
    starts = [w * i for i in range(N_DEV)]
    aligned = [LANE * (s // LANE) for s in starts]
    offs = [s - a for s, a in zip(starts, aligned)]
    sw = LANE * (-(-(max(offs) + w) // LANE))
    return aligned, sw, aligned[-1] + sw


def _my_index():
    return 4 * lax.axis_index("x") + 2 * lax.axis_index("y") + lax.axis_index("c")


def _all_gather(arrs):
    n = len(arrs)

    def body(*refs):
        ins, outs = refs[:n], refs[n:2 * n]
        send_sems, recv_sems, local_sems = refs[2 * n:]
        x, y, c = lax.axis_index("x"), lax.axis_index("y"), lax.axis_index("c")
        me, sib = (x, y, c), (x, y, 1 - c)
        chips = [(1 - x, y), (x, 1 - y), (1 - x, 1 - y)]

        def idx(px, py, pc):
            return 4 * px + 2 * py + pc

        def copy(a, k, block, to, src=None):
            dst = outs[a].at[idx(*block)]
            return pltpu.make_async_remote_copy(
                src_ref=dst if src is None else src, dst_ref=dst,
                send_sem=send_sems.at[a, k], recv_sem=recv_sems.at[a, k],
                device_id=to, device_id_type=MESH)

        mine = [pltpu.make_async_copy(ins[a], outs[a].at[idx(*me)], local_sems.at[a]) for a in range(n)]
        for m in mine:
            m.start()
        first = []
        for a in range(n):
            first.append(copy(a, 0, me, sib, src=ins[a]))
            for j, chip in enumerate(chips):
                first.append(copy(a, 1 + j, me, (*chip, c), src=ins[a]))
        for cp in first:
            cp.start()
        passed = []
        for j, chip in enumerate(chips):
            for a in range(n):
                copy(a, 1 + j, (*chip, c), me).wait_recv()
                p = copy(a, 4 + j, (*chip, c), sib)
                p.start()
                passed.append(p)
        for a in range(n):
            copy(a, 0, sib, me).wait_recv()
        for j, chip in enumerate(chips):
            for a in range(n):
                copy(a, 4 + j, (*chip, 1 - c), me).wait_recv()
        for cp in first + passed:
            cp.wait_send()
        for m in mine:
            m.wait()

    any_spec = pl.BlockSpec(memory_space=pl.ANY)
    return pl.pallas_call(
        body, name="weights_all_gather",
        out_shape=[jax.ShapeDtypeStruct((N_DEV,) + a.shape, a.dtype) for a in arrs],
        in_specs=[any_spec] * n, out_specs=[any_spec] * n,
        scratch_shapes=[pltpu.SemaphoreType.DMA((n, 7)), pltpu.SemaphoreType.DMA((n, 7)),
                        pltpu.SemaphoreType.DMA((n,))],
    )(*arrs)


def _rs_windows(specs):
    def window(ref, spec, blk):
        kind, n = spec
        if kind == "col":
            _, sw, _ = _slab_geom(n)
            return ref.at[pl.ds((n * blk) // LANE, sw // LANE)]
        start = pl.multiple_of(n * blk, n)
        return ref.at[pl.ds(start, n), :]
    return window


def _peer(k):
    x, y, c = lax.axis_index("x"), lax.axis_index("y"), lax.axis_index("c")
    return (x ^ (k >> 2), y ^ ((k >> 1) & 1), c ^ (k & 1))


def _direct_gather_copies(ins, outs, send_sems, recv_sems, local_sems):
    me = _my_index()
    remote, local = [], []
    for a, (src, dst) in enumerate(zip(ins, outs)):
        local.append(pltpu.make_async_copy(src, dst.at[me], local_sems.at[a]))
        for k in range(1, N_DEV):
            remote.append(pltpu.make_async_remote_copy(
                src_ref=src, dst_ref=dst.at[me], send_sem=send_sems.at[a, k - 1], recv_sem=recv_sems.at[a, k - 1],
                device_id=_peer(k), device_id_type=MESH))
    return remote, local


def _direct_scatter_copies(ins, outs, specs, send_sems, recv_sems):
    window = _rs_windows(specs)
    remote = []
    for a, (src, dst) in enumerate(zip(ins, outs)):
        for k in range(1, N_DEV):
            px, py, pc = _peer(k)
            remote.append(pltpu.make_async_remote_copy(
                src_ref=window(src, specs[a], 4 * px + 2 * py + pc), dst_ref=dst.at[k - 1],
                send_sem=send_sems.at[a, k - 1], recv_sem=recv_sems.at[a, k - 1],
                device_id=(px, py, pc), device_id_type=MESH))
    return remote


def _scatter_block_shape(g, spec):
    kind, w = spec
    return (_slab_geom(w)[1] // LANE, g.shape[1], LANE) if kind == "col" else (w, g.shape[1])


def _wait_all(remote, local=()):
    for cp in remote:
        cp.wait_recv()
    for cp in remote:
        cp.wait_send()
    for cp in local:
        cp.wait()


def _scatter_start(g, spec):
    blk = _scatter_block_shape(g, spec)
    window = _rs_windows([spec])
    npeer = N_DEV - 1

    def body(g_ref, land_ref, *rest):
        sems = rest[:2 * npeer]
        token = rest[2 * npeer + 2]
        for cp in _peer_block_copies(g_ref, land_ref, spec, window, sems[:npeer], sems[npeer:]):
            cp.start()
        token[...] = jnp.zeros(token.shape, token.dtype)

    hbm = pl.BlockSpec(memory_space=pltpu.HBM)
    sem = pl.BlockSpec(memory_space=pltpu.SEMAPHORE)
    land = lax.empty((npeer,) + blk, g.dtype)
    outs = pl.pallas_call(
        body, name="grads_scatter_start",
        out_shape=(pltpu.SemaphoreType.DMA(()),) * (2 * npeer)
        + (pltpu.HBM(g.shape, g.dtype), pltpu.HBM(land.shape, land.dtype), jax.ShapeDtypeStruct((8, LANE), F32)),
        in_specs=(hbm, hbm),
        out_specs=(sem,) * (2 * npeer) + (hbm, hbm, pl.BlockSpec(memory_space=pltpu.VMEM)),
        input_output_aliases={0: 2 * npeer, 1: 2 * npeer + 1},
        compiler_params=pltpu.CompilerParams(has_side_effects=pltpu.SideEffectType.DATAFLOW_SIDE_EFFECTING),
    )(pltpu.with_memory_space_constraint(g, pltpu.HBM), pltpu.with_memory_space_constraint(land, pltpu.HBM))
    return outs[:2 * npeer], outs[2 * npeer], outs[2 * npeer + 1], outs[2 * npeer + 2]


def _peer_block_copies(g_ref, land_ref, spec, window, send_sems, recv_sems):
    copies = []
    for k in range(1, N_DEV):
        px, py, pc = _peer(k)
        copies.append(pltpu.make_async_remote_copy(
            src_ref=window(g_ref, spec, 4 * px + 2 * py + pc), dst_ref=land_ref.at[k - 1],
            send_sem=send_sems[k - 1], recv_sem=recv_sems[k - 1], device_id=(px, py, pc), device_id_type=MESH))
    return copies


def _scatter_wait(sems, g_thru, land_thru, spec, after):
    window = _rs_windows([spec])
    npeer = N_DEV - 1

    def body(g_ref, land_ref, *rest):
        s = rest[:2 * npeer]
        copies = _peer_block_copies(g_ref, land_ref, spec, window, s[:npeer], s[npeer:])
        for cp in copies:
            cp.wait_send()
        for cp in copies:
            cp.wait_recv()

    hbm = pl.BlockSpec(memory_space=pltpu.HBM)
    sem = pl.BlockSpec(memory_space=pltpu.SEMAPHORE)
    return pl.pallas_call(
        body, name="grads_scatter_wait",
        out_shape=(pltpu.HBM(g_thru.shape, g_thru.dtype), pltpu.HBM(land_thru.shape, land_thru.dtype)),
        in_specs=(hbm, hbm) + (sem,) * (2 * npeer) + (pl.BlockSpec(memory_space=pl.ANY),) * len(after),
        out_specs=(hbm, hbm), input_output_aliases={0: 0, 1: 1},
        compiler_params=pltpu.CompilerParams(has_side_effects=pltpu.SideEffectType.DATAFLOW_SIDE_EFFECTING),
    )(g_thru, land_thru, *sems, *after)[1]


def _final_sum8(g, recv, spec):
    kind, n = spec
    me = _my_index()
    offs = jnp.stack([(n * me) // LANE if kind == "col" else me]).astype(jnp.int32)
    if kind == "col":
        _, T, M, _ = recv.shape
        grid = (T,)
        in_specs = [pl.BlockSpec((1, M, LANE), lambda t, o: (o[0] + t, 0, 0)),
                    pl.BlockSpec((N_DEV - 1, 1, M, LANE), lambda t, o: (0, t, 0, 0))]
        out_spec = pl.BlockSpec((M, LANE), lambda t, o: (0, t))
        out_shape = jax.ShapeDtypeStruct((M, T * LANE), F32)
    else:
        _, nrow, C = recv.shape
        grid = (1,)
        in_specs = [pl.BlockSpec((nrow, C), lambda t, o: (o[0], 0)),
                    pl.BlockSpec((N_DEV - 1, nrow, C), lambda t, o: (0, 0, 0))]
        out_spec = pl.BlockSpec((nrow, C), lambda t, o: (0, 0))
        out_shape = jax.ShapeDtypeStruct((nrow, C), F32)

    def body(o_ref, g_ref, r_ref, out_ref):
        acc = g_ref[0] if kind == "col" else g_ref[...]
        for k in range(N_DEV - 1):
            acc = acc + (r_ref[k, 0] if kind == "col" else r_ref[k]).astype(F32)
        out_ref[...] = acc

    return pl.pallas_call(
        body, name="grads_final_sum8",
        grid_spec=pltpu.PrefetchScalarGridSpec(num_scalar_prefetch=1, grid=grid, in_specs=in_specs,
                                               out_specs=out_spec),
        out_shape=out_shape, compiler_params=_cparams(),
    )(offs, g, recv)


def _all_reduce_small(pack, after):
    R, P = pack.shape

    def body(x_ref, after_ref, o_ref, gat_ref, send_sems, recv_sems):
        x, y, c = lax.axis_index("x"), lax.axis_index("y"), lax.axis_index("c")
        me = 4 * x + 2 * y + c
        gat_ref[me] = x_ref[...]
        copies = []
        for k in range(1, N_DEV):
            peer = (x ^ (k >> 2), y ^ ((k >> 1) & 1), c ^ (k & 1))
            copies.append(pltpu.make_async_remote_copy(
                src_ref=x_ref, dst_ref=gat_ref.at[me],
                send_sem=send_sems.at[k - 1], recv_sem=recv_sems.at[k - 1],
                device_id=peer, device_id_type=MESH))
        for cp in copies:
            cp.start()
        for cp in copies:
            cp.wait_recv()
        for cp in copies:
            cp.wait_send()
        acc = gat_ref[0]
        for d in range(1, N_DEV):
            acc = acc + gat_ref[d]
        o_ref[...] = acc

    vm = pl.BlockSpec(memory_space=pltpu.VMEM)
    return pl.pallas_call(
        body, name="small_all_reduce",
        out_shape=jax.ShapeDtypeStruct((R, P), F32),
        in_specs=[vm, pl.BlockSpec(memory_space=pl.ANY)], out_specs=vm,
        scratch_shapes=[pltpu.VMEM((N_DEV, R, P), F32),
                        pltpu.SemaphoreType.DMA((N_DEV - 1,)), pltpu.SemaphoreType.DMA((N_DEV - 1,))],
    )(pack, after)


def _assemble(slabs, w):
    aligned, sw, total = _slab_geom(w)
    K = slabs.shape[1]
    tr = _row_tile(K, 256)

    def body(s_ref, o_ref):
        o_ref[...] = jnp.zeros(o_ref.shape, BF16)
        for i in range(N_DEV):
            a, off = aligned[i], w * i - aligned[i]
            x = s_ref[i].astype(F32)
            if off:
                x = pltpu.roll(x, off, axis=1)
            o_ref[:, a:a + sw] = (o_ref[:, a:a + sw].astype(F32) + x).astype(BF16)

    return pl.pallas_call(
        body, name="assemble_w_in", grid=(K // tr,),
        in_specs=[pl.BlockSpec((N_DEV, tr, sw), lambda i: (0, i, 0))],
        out_specs=pl.BlockSpec((tr, total), lambda i: (i, 0)),
        out_shape=jax.ShapeDtypeStruct((K, total), BF16),
        compiler_params=_cparams(),
    )(slabs)


def _rmsnorm_fwd(x, g, name):
    S, D = x.shape
    tm = _row_tile(S, 256)

    def body(x_ref, g_ref, h_ref):
        xv = x_ref[...]
        r = lax.rsqrt(jnp.mean(xv * xv, axis=-1, keepdims=True) + RMS_EPS)
        h_ref[...] = ((xv * r) * g_ref[...]).astype(BF16)

    return pl.pallas_call(
        body, name=name, grid=(S // tm,),
        in_specs=[pl.BlockSpec((tm, D), lambda i: (i, 0)), pl.BlockSpec((1, D), lambda i: (0, 0))],
        out_specs=pl.BlockSpec((tm, D), lambda i: (i, 0)),
        out_shape=jax.ShapeDtypeStruct((S, D), BF16),
        compiler_params=_cparams(),
    )(x, g)


def _rmsnorm_bwd(dh, x, g, dres, name):
    S, D = x.shape
    tm = _row_tile(S, 256)

    def body(dh_ref, x_ref, g_ref, dr_ref, dx_ref, dxb_ref, dg_ref):
        xv = x_ref[...]
        r = lax.rsqrt(jnp.mean(xv * xv, axis=-1, keepdims=True) + RMS_EPS)
        xhat = xv * r
        d = dh_ref[...]
        gd = d * g_ref[...]
        dx = r * (gd - xhat * jnp.mean(gd * xhat, axis=-1, keepdims=True)) + dr_ref[...]
        dx_ref[...] = dx
        dxb_ref[...] = dx.astype(BF16)

        @pl.when(pl.program_id(0) == 0)
        def _():
            dg_ref[...] = jnp.zeros(dg_ref.shape, F32)
        dg_ref[...] += jnp.sum(d * xhat, axis=0, keepdims=True)

    row = pl.BlockSpec((tm, D), lambda i: (i, 0))
    vec = pl.BlockSpec((1, D), lambda i: (0, 0))
    return pl.pallas_call(
        body, name=name, grid=(S // tm,),
        in_specs=[row, row, vec, row], out_specs=[row, row, vec],
        out_shape=[jax.ShapeDtypeStruct((S, D), F32), jax.ShapeDtypeStruct((S, D), BF16),
                   jax.ShapeDtypeStruct((1, D), F32)],
        compiler_params=_cparams(),
    )(dh, x, g, dres)


def _adamw(w, g, m, v, name):
    R, C = w.shape
    tr = _row_tile(R, 256)
    c1 = 1.0 - ADAM_B1 ** ADAM_STEP
    c2 = 1.0 - ADAM_B2 ** ADAM_STEP

    def body(w_ref, g_ref, m_ref, v_ref, d_ref, nm_ref, nv_ref):
        gv = g_ref[...]
        nm = ADAM_B1 * m_ref[...] + (1.0 - ADAM_B1) * gv
        nv = ADAM_B2 * v_ref[...] + (1.0 - ADAM_B2) * (gv * gv)
        d_ref[...] = -ADAM_LR * ((nm / c1) / (jnp.sqrt(nv / c2) + ADAM_EPS) + ADAM_WD * w_ref[...])
        nm_ref[...] = nm
        nv_ref[...] = nv

    spec = pl.BlockSpec((tr, C), lambda i: (i, 0))
    return pl.pallas_call(
        body, name=name, grid=(R // tr,),
        in_specs=[spec] * 4, out_specs=[spec] * 3,
        out_shape=[jax.ShapeDtypeStruct((R, C), F32)] * 3,
        compiler_params=_cparams(),
    )(w, g, m, v)


def _proj(h, wfull, col0, ncols, out_dtype, name, rope=None):
    S, K = h.shape
    tm = _row_tile(S, MM_TILE)
    tn = math.gcd(_tile(ncols, MM_TILE), col0) if col0 else _tile(ncols, MM_TILE)
    if rope is not None:
        tn = _tile(math.gcd(ncols, rope[1]), MM_TILE)
    assert ncols % tn == 0 and col0 % tn == 0
    cb = col0 // tn

    def body(*refs):
        if rope is None:
            a_ref, b_ref, o_ref = refs
        else:
            a_ref, b_ref, t_ref, o_ref = refs
        acc = _dot_nn(a_ref[...], b_ref[...])
        if rope is not None:
            t0, t1, t2 = (jnp.tile(t_ref[i], (1, tn // LANE)) for i in range(3))
            roped = (acc * t0 + pltpu.roll(acc, tn - ROT_DIM // 2, axis=1) * t1
                     + pltpu.roll(acc, ROT_DIM // 2, axis=1) * t2)
            acc = jnp.where(pl.program_id(1) < rope[1] // tn, roped, acc)
        o_ref[...] = acc.astype(out_dtype)

    in_specs = [pl.BlockSpec((tm, K), lambda i, j: (i, 0)), pl.BlockSpec((K, tn), lambda i, j: (0, cb + j))]
    args = [h, wfull]
    if rope is not None:
        in_specs.append(pl.BlockSpec((3, tm, LANE), lambda i, j: (0, i, 0)))
        args.append(rope[0])
    return pl.pallas_call(
        body, name=name, grid=(S // tm, ncols // tn),
        in_specs=in_specs, out_specs=pl.BlockSpec((tm, tn), lambda i, j: (i, j)),
        out_shape=jax.ShapeDtypeStruct((S, ncols), out_dtype),
        compiler_params=_cparams(),
    )(*args)


def _out_proj_norm(y, wo, xres, g, name):
    S, W = y.shape
    D = wo.shape[1]
    tm = _row_tile(S, 512)

    def body(a_ref, b_ref, r_ref, g_ref, x_ref, h_ref):
        xv = r_ref[...] + _dot_nn(a_ref[...], b_ref[...])
        x_ref[...] = xv
        r = lax.rsqrt(jnp.mean(xv * xv, axis=-1, keepdims=True) + RMS_EPS)
        h_ref[...] = ((xv * r) * g_ref[...]).astype(BF16)

    row = pl.BlockSpec((tm, D), lambda i: (i, 0))
    return pl.pallas_call(
        body, name=name, grid=(S // tm,),
        in_specs=[pl.BlockSpec((tm, W), lambda i: (i, 0)), pl.BlockSpec((W, D), lambda i: (0, 0)), row,
                  pl.BlockSpec((1, D), lambda i: (0, 0))],
        out_specs=[row, row],
        out_shape=[jax.ShapeDtypeStruct((S, D), F32), jax.ShapeDtypeStruct((S, D), BF16)],
        compiler_params=_cparams(),
    )(y, wo, xres, g)


def _out_proj_loss(y, wo, xres, tgt, g, name):
    S, W = y.shape
    D = wo.shape[1]
    tm = _row_tile(S, 512)

    def body(a_ref, b_ref, r_ref, t_ref, g_ref, dx_ref, dxb_ref, dg_ref, loss_ref):
        xv = r_ref[...] + _dot_nn(a_ref[...], b_ref[...])
        r = lax.rsqrt(jnp.mean(xv * xv, axis=-1, keepdims=True) + RMS_EPS)
        xhat = xv * r
        gv = g_ref[...]
        err = xhat * gv - t_ref[...]
        d = err * (1.0 / D)
        gd = d * gv
        dx = r * (gd - xhat * jnp.mean(gd * xhat, axis=-1, keepdims=True))
        dx_ref[...] = dx
        dxb_ref[...] = dx.astype(BF16)

        @pl.when(pl.program_id(0) == 0)
        def _():
            dg_ref[...] = jnp.zeros(dg_ref.shape, F32)
            loss_ref[...] = jnp.zeros(loss_ref.shape, F32)
        dg_ref[...] += jnp.sum(d * xhat, axis=0, keepdims=True)
        per_tok = jnp.sum(err * err, axis=-1, keepdims=True) * (1.0 / D)
        loss_ref[...] += 0.5 * jnp.sum(per_tok, axis=0, keepdims=True)

    row = pl.BlockSpec((tm, D), lambda i: (i, 0))
    vec = pl.BlockSpec((1, D), lambda i: (0, 0))
    return pl.pallas_call(
        body, name=name, grid=(S // tm,),
        in_specs=[pl.BlockSpec((tm, W), lambda i: (i, 0)), pl.BlockSpec((W, D), lambda i: (0, 0)), row, row, vec],
        out_specs=[row, row, vec, pl.BlockSpec((1, LANE), lambda i: (0, 0))],
        out_shape=[jax.ShapeDtypeStruct((S, D), F32), jax.ShapeDtypeStruct((S, D), BF16),
                   jax.ShapeDtypeStruct((1, D), F32), jax.ShapeDtypeStruct((1, LANE), F32)],
        compiler_params=_cparams(),
    )(y, wo, xres, tgt, g)


def _matmul_nt(parts, wfull, out_rows, name):
    S = parts[0][0].shape[-2]
    tm, tn = _row_tile(S, MM_TILE), _tile(out_rows, MM_TILE)
    plan, lo = [], 0
    for arr, lead, col0 in parts:
        n_p = arr.shape[-1]
        tk = math.gcd(_tile(n_p, 1024), col0) if col0 else _tile(n_p, 1024)
        steps = n_p // tk
        plan.append((lead, col0 // tk, tk, lo, lo + steps))
        lo += steps
    nk = lo
    npart = len(parts)

    def body(*refs):
        a_refs, w_refs = refs[:npart], refs[npart:2 * npart]
        o_ref, acc_ref = refs[2 * npart], refs[2 * npart + 1]
        k = pl.program_id(2)

        @pl.when(k == 0)
        def _():
            acc_ref[...] = jnp.zeros(acc_ref.shape, F32)
        for p, (_, _, _, lo_p, hi_p) in enumerate(plan):
            @pl.when((k >= lo_p) & (k < hi_p))
            def _(p=p):
                acc_ref[...] += _dot_nt(a_refs[p][...], w_refs[p][...])

        @pl.when(k == nk - 1)
        def _():
            o_ref[...] = acc_ref[...]

    in_specs, args = [], []
    for (arr, lead, col0), (_, cb, tk, lo_p, hi_p) in zip(parts, plan):
        def kk(k, lo_p=lo_p, hi_p=hi_p):
            return jnp.clip(k - lo_p, 0, hi_p - lo_p - 1)
        if lead is None:
            in_specs.append(pl.BlockSpec((tm, tk), lambda i, j, k, kk=kk: (i, kk(k))))
        else:
            in_specs.append(pl.BlockSpec((None, tm, tk), lambda i, j, k, kk=kk, lead=lead: (lead, i, kk(k))))
        args.append(arr)
    for (_, cb, tk, lo_p, hi_p) in plan:
        def kk(k, lo_p=lo_p, hi_p=hi_p):
            return jnp.clip(k - lo_p, 0, hi_p - lo_p - 1)
        in_specs.append(pl.BlockSpec((tn, tk), lambda i, j, k, kk=kk, cb=cb: (j, cb + kk(k))))
        args.append(wfull)
    return pl.pallas_call(
        body, name=name, grid=(S // tm, out_rows // tn, nk),
        in_specs=in_specs, out_specs=pl.BlockSpec((tm, tn), lambda i, j, k: (i, j)),
        out_shape=jax.ShapeDtypeStruct((S, out_rows), F32),
        scratch_shapes=[pltpu.VMEM((tm, tn), F32)],
        compiler_params=_cparams(),
    )(*args)


def _matmul_tn(a, parts, total, name, tile_major=False, also_bf16=False):
    S, M = a.shape
    tm, ts = _tile(M, MM_TILE), _row_tile(S, MM_TILE)
    nout = 2 if also_bf16 else 1
    outs = None
    for idx, (arr, lead, col0) in enumerate(parts):
        n_p = arr.shape[-1]
        tn = math.gcd(_tile(n_p, MM_TILE), col0) if col0 else _tile(n_p, MM_TILE)
        cb = col0 // tn
        nk = S // ts

        def body(*refs, nk=nk, tn=tn):
            a_ref, b_ref = refs[0], refs[1]
            o_refs, acc_ref = refs[-1 - nout:-1], refs[-1]
            k = pl.program_id(2)

            @pl.when(k == 0)
            def _():
                acc_ref[...] = jnp.zeros(acc_ref.shape, F32)
            acc_ref[...] += _dot_tn(a_ref[...], b_ref[...])

            @pl.when(k == nk - 1)
            def _():
                for o_ref in o_refs:
                    if tile_major:
                        for t in range(tn // LANE):
                            o_ref[t] = acc_ref[:, LANE * t:LANE * (t + 1)].astype(o_ref.dtype)
                    else:
                        o_ref[...] = acc_ref[...].astype(o_ref.dtype)

        in_specs = [pl.BlockSpec((ts, tm), lambda i, j, k: (k, i))]
        if lead is None:
            in_specs.append(pl.BlockSpec((ts, tn), lambda i, j, k: (k, j)))
        else:
            in_specs.append(pl.BlockSpec((None, ts, tn), lambda i, j, k, lead=lead: (lead, k, j)))
        args = [a, arr]
        aliases = {}
        if outs is not None:
            in_specs += [pl.BlockSpec(memory_space=pl.ANY)] * nout
            args += list(outs)
            aliases = {2 + o: o for o in range(nout)}
        if tile_major:
            out_spec = pl.BlockSpec((tn // LANE, tm, LANE), lambda i, j, k, cb=cb: (cb + j, i, 0))
            shape = (total // LANE, M, LANE)
        else:
            out_spec = pl.BlockSpec((tm, tn), lambda i, j, k, cb=cb: (i, cb + j))
            shape = (M, total)
        outs = pl.pallas_call(
            body, name=f"{name}_{idx}", grid=(M // tm, n_p // tn, nk),
            in_specs=in_specs, out_specs=[out_spec] * nout,
            out_shape=[jax.ShapeDtypeStruct(shape, dt) for dt in (F32, BF16)[:nout]],
            scratch_shapes=[pltpu.VMEM((tm, tn), F32)],
            input_output_aliases=aliases,
            compiler_params=_cparams(),
        )(*args)
    return tuple(outs) if also_bf16 else outs[0]


def _log_sigmoid(z):
    e = jnp.exp(-jnp.abs(z))
    return jnp.minimum(z, 0.0) - jnp.where(e < 1e-4, e * (1.0 - 0.5 * e), jnp.log(1.0 + e))


def _fox_gate_fwd(fl, bias):
    S = fl.shape[0]

    def body(f_ref, b_ref, c_ref):
        row = lax.broadcasted_iota(jnp.int32, (8, LANE), 0)

        def step(i, carry):
            r0 = pl.multiple_of(i * 8, 8)
            t = _log_sigmoid(f_ref[pl.ds(r0, 8), :] + b_ref[...])
            for sh in (1, 2, 4):
                t = t + jnp.where(row >= sh, pltpu.roll(t, sh, axis=0), 0.0)
            t = t + carry
            c_ref[pl.ds(r0, 8), :] = t
            return jnp.sum(jnp.where(row == 7, t, 0.0), axis=0, keepdims=True)

        lax.fori_loop(0, S // 8, step, jnp.zeros((1, LANE), F32))

    vm = pl.BlockSpec(memory_space=pltpu.VMEM)
    return pl.pallas_call(
        body, name="fox_gate_fwd", in_specs=[vm, vm], out_specs=vm,
        out_shape=jax.ShapeDtypeStruct((S, LANE), F32),
        compiler_params=_cparams(),
    )(fl, bias)


def _fox_gate_bwd(fl, bias, dc):
    S = fl.shape[0]

    def body(f_ref, b_ref, d_ref, o_ref, db_ref, acc_ref):
        row = lax.broadcasted_iota(jnp.int32, (8, LANE), 0)
        nt = S // 8

        def step(ii, carry):
            carry_c, carry_b = carry
            r0 = pl.multiple_of((nt - 1 - ii) * 8, 8)
            t = d_ref[pl.ds(r0, 8), :]
            for sh in (1, 2, 4):
                t = t + jnp.where(row < 8 - sh, pltpu.roll(t, 8 - sh, axis=0), 0.0)
            t = t + carry_c
            z = f_ref[pl.ds(r0, 8), :] + b_ref[...]
            dz = t * _sigmoid(-z)
            acc_ref[pl.ds(r0, 8), :] = dz
            first = jnp.sum(jnp.where(row == 0, t, 0.0), axis=0, keepdims=True)
            return first, carry_b + jnp.sum(dz, axis=0, keepdims=True)

        zero = jnp.zeros((1, LANE), F32)
        _, db = lax.fori_loop(0, nt, step, (zero, zero))
        db_ref[...] = db
        o_ref[...] = acc_ref[...].astype(BF16)

    vm = pl.BlockSpec(memory_space=pltpu.VMEM)
    return pl.pallas_call(
        body, name="fox_gate_bwd", in_specs=[vm, vm, vm], out_specs=[vm, vm],
        out_shape=[jax.ShapeDtypeStruct((S, LANE), BF16), jax.ShapeDtypeStruct((1, LANE), F32)],
        scratch_shapes=[pltpu.VMEM((S, LANE), F32)],
        compiler_params=_cparams(),
    )(fl, bias, dc)


def _bias_lanes(col, lane, e, first):
    o0 = HEAD_DIM * (1 - e)
    hi = col.astype(BF16)
    r1 = col - hi.astype(F32)
    mid = r1.astype(BF16)
    lo = (r1 - mid.astype(F32)).astype(BF16)
    d0 = o0 if first else o0 + 3
    t = jnp.where((lane >= o0) & (lane < o0 + 6), jnp.ones(lane.shape, BF16), jnp.zeros(lane.shape, BF16))
    t = jnp.where(lane == d0, hi, t)
    t = jnp.where(lane == d0 + 1, mid, t)
    return jnp.where(lane == d0 + 2, lo, t)


def _fox_fwd(qkv, gate, c, H, gather=()):
    na = len(gather)
    S = qkv.shape[0]
    W = H * HEAD_DIM
    HP = H // 2
    PP = 2 if HP % 2 == 0 else 1
    NE = 2 * PP
    tq = _row_tile(S, 512)
    nq = S // tq
    wb = W // LANE
    scale = HEAD_DIM ** -0.5

    def body(*refs):
        q_ref, k_ref, v_ref, g_ref, c_ref = refs[:5]
        y_ref, o_ref, a_ref = refs[5 + na:8 + na]
        kaug_sc, vaug_sc, qaug_sc, s_sc, mb_sc, m_sc, acc_sc = refs[8 + 2 * na:15 + 2 * na]
        hp, qi = pl.program_id(0), pl.program_id(1)
        if na:
            remote, local = _direct_gather_copies(refs[5:5 + na], refs[8 + na:8 + 2 * na], *refs[15 + 2 * na:])

            @pl.when((hp == 0) & (qi == 0))
            def _():
                for cp in remote + local:
                    cp.start()
        lane = lax.broadcasted_iota(jnp.int32, (tq, LANE), 1)
        own = [lane < HEAD_DIM, lane >= HEAD_DIM]
        rows = lax.broadcasted_iota(jnp.int32, (tq, tq), 0)
        cols = lax.broadcasted_iota(jnp.int32, (tq, tq), 1)

        def bias_lanes(col, e, first):
            return _bias_lanes(col, lane, e % 2, first)

        def head_col(tile, e):
            return jnp.sum(jnp.where(lane == 2 * PP * hp + e, tile, 0.0), axis=1, keepdims=True)

        def tile_of(e):
            return slice(LANE * (e // 2), LANE * (e // 2 + 1))

        @pl.when(qi == 0)
        def _():
            def chunk(i, carry):
                r0 = pl.multiple_of(i * tq, tq)
                cb = c_ref[pl.ds(r0, tq), :]
                for e in range(NE):
                    kb, vb = k_ref[pl.ds(r0, tq), tile_of(e)], v_ref[pl.ds(r0, tq), tile_of(e)]
                    kaug_sc[e, pl.ds(r0, tq), :] = jnp.where(own[e % 2], kb, bias_lanes(-head_col(cb, e), e, False))
                    vaug_sc[e, pl.ds(r0, tq), :] = jnp.where(own[e % 2], vb, jnp.ones((tq, LANE), BF16))
                return carry
            lax.fori_loop(0, nq, chunk, 0)

        crow = c_ref[pl.ds(pl.multiple_of(qi * tq, tq), tq), :]
        ctq = [head_col(crow, e) for e in range(NE)]
        for e in range(NE):
            q = q_ref[:, tile_of(e)] * jnp.asarray(scale, BF16)
            qaug_sc[e] = jnp.where(own[e % 2], q, bias_lanes(ctq[e], e, True))
        m_sc[...] = jnp.full(m_sc.shape, NEG_INF, F32)
        acc_sc[...] = jnp.zeros(acc_sc.shape, F32)

        def scores(blk, slot, masked):
            k0 = pl.multiple_of(blk * tq, tq)
            for e in range(NE):
                s = _dot_nt(qaug_sc[e], kaug_sc[e, pl.ds(k0, tq), :])
                if masked:
                    s = jnp.where(rows >= cols, s, NEG_INF)
                s_sc[slot, e] = s
                mb_sc[slot, e] = jnp.broadcast_to(jnp.max(s, axis=1, keepdims=True), (tq, LANE))

        def accumulate(blk, slot):
            k0 = pl.multiple_of(blk * tq, tq)
            for e in range(NE):
                m_prev = m_sc[e]
                m_new = jnp.maximum(m_prev, mb_sc[slot, e])
                p = jnp.exp(s_sc[slot, e] - jnp.tile(m_new, (1, tq // LANE)))
                acc_sc[e] = jnp.exp(m_prev - m_new) * acc_sc[e] + _dot_nn(p.astype(BF16), vaug_sc[e, pl.ds(k0, tq), :])
                m_sc[e] = m_new

        def block_of(t):
            return jnp.where(t == 0, qi, t - 1)

        scores(qi, 0, True)

        def loop_body(t, carry):
            scores(t, (t + 1) % 2, False)
            accumulate(block_of(t), t % 2)
            return carry

        lax.fori_loop(0, qi, loop_body, 0)
        accumulate(block_of(qi), qi % 2)
        o_e, a_e = [], []
        for e in range(NE):
            acc = acc_sc[e]
            l = pltpu.roll(acc, HEAD_DIM, axis=1)
            o_e.append(acc / l)
            a_e.append(ctq[e] - (m_sc[e] + jnp.log(l)))
        for pp in range(PP):
            o = jnp.where(own[0], o_e[2 * pp], o_e[2 * pp + 1])
            g = g_ref[:, tile_of(2 * pp)]
            y_ref[:, tile_of(2 * pp)] = (o * (g * _sigmoid(g))).astype(BF16)
            o_ref[:, tile_of(2 * pp)] = o.astype(BF16)
            a_ref[pp] = jnp.where(own[0], a_e[2 * pp], a_e[2 * pp + 1])
        if na:
            @pl.when((hp == HP // PP - 1) & (qi == nq - 1))
            def _():
                _wait_all(remote, local)

    any_spec = pl.BlockSpec(memory_space=pl.ANY)
    sems = [pltpu.SemaphoreType.DMA((na, N_DEV - 1)), pltpu.SemaphoreType.DMA((na, N_DEV - 1)),
            pltpu.SemaphoreType.DMA((na,))] if na else []
    wide = PP * LANE
    outs = pl.pallas_call(
        body, name="fox_attn_fwd", grid=(HP // PP, nq),
        in_specs=[pl.BlockSpec((tq, wide), lambda h, i: (i, h)),
                  pl.BlockSpec((S, wide), lambda h, i: (0, wb // PP + h)),
                  pl.BlockSpec((S, wide), lambda h, i: (0, 2 * wb // PP + h)),
                  pl.BlockSpec((tq, wide), lambda h, i: (i, h)),
                  pl.BlockSpec((S, LANE), lambda h, i: (0, 0))] + [any_spec] * na,
        out_specs=[pl.BlockSpec((tq, wide), lambda h, i: (i, h)),
                   pl.BlockSpec((tq, wide), lambda h, i: (i, h)),
                   pl.BlockSpec((PP, tq, LANE), lambda h, i: (h, i, 0))] + [any_spec] * na,
        out_shape=[jax.ShapeDtypeStruct((S, W), BF16), jax.ShapeDtypeStruct((S, W), BF16),
                   jax.ShapeDtypeStruct((HP, S, LANE), F32)]
        + [jax.ShapeDtypeStruct((N_DEV,) + g.shape, g.dtype) for g in gather],
        scratch_shapes=[pltpu.VMEM((NE, S, LANE), BF16), pltpu.VMEM((NE, S, LANE), BF16),
                        pltpu.VMEM((NE, tq, LANE), BF16), pltpu.VMEM((2, NE, tq, tq), F32),
                        pltpu.VMEM((2, NE, tq, LANE), F32), pltpu.VMEM((NE, tq, LANE), F32),
                        pltpu.VMEM((NE, tq, LANE), F32)] + sems,
        compiler_params=_cparams(),
    )(qkv, qkv, qkv, gate, c, *gather)
    return outs[0], outs[1], outs[2], list(outs[3:])


def _fox_out_bwd(dxb, wo, qkv, gate, o, a, H):
    S, D = dxb.shape
    W = H * HEAD_DIM
    tm, tn = _row_tile(S, 512), _tile(W, 512)
    npair = tn // LANE
    scale = HEAD_DIM ** -0.5

    def body(dx_ref, w_ref, q_ref, g_ref, o_ref, a_ref, qa_ref, da_ref, dg_ref):
        dy = _dot_nt(dx_ref[...], w_ref[...])
        lane = lax.broadcasted_iota(jnp.int32, (tm, LANE), 1)
        own = [lane < HEAD_DIM, lane >= HEAD_DIM]
        for p in range(npair):
            cols = slice(LANE * p, LANE * (p + 1))
            q = q_ref[:, cols] * jnp.asarray(scale, BF16)
            dyv, g, ov, at = dy[:, cols], g_ref[:, cols], o_ref[:, cols].astype(F32), a_ref[p]
            sg = _sigmoid(g)
            dob = (dyv * (g * sg)).astype(BF16)
            dg_ref[:, cols] = (dyv * ov * (sg * (1.0 + g * (1.0 - sg)))).astype(BF16)
            prod = dob.astype(F32) * ov
            for e in range(2):
                a_col = jnp.max(jnp.where(own[e], at, -jnp.inf), axis=1, keepdims=True)
                d_col = jnp.sum(jnp.where(own[e], prod, 0.0), axis=1, keepdims=True)
                qa_ref[e, :, cols] = jnp.where(own[e], q, _bias_lanes(a_col, lane, e, True))
                da_ref[e, :, cols] = jnp.where(own[e], dob, _bias_lanes(-d_col, lane, e, True))

    blk = pl.BlockSpec((tm, tn), lambda i, j: (i, j))
    pair = pl.BlockSpec((2, tm, tn), lambda i, j: (0, i, j))
    return pl.pallas_call(
        body, name="fox_out_bwd", grid=(S // tm, W // tn),
        in_specs=[pl.BlockSpec((tm, D), lambda i, j: (i, 0)), pl.BlockSpec((tn, D), lambda i, j: (j, 0)),
                  blk, blk, blk, pl.BlockSpec((npair, tm, LANE), lambda i, j: (j, i, 0))],
        out_specs=[pair, pair, blk],
        out_shape=[jax.ShapeDtypeStruct((2, S, W), BF16), jax.ShapeDtypeStruct((2, S, W), BF16),
                   jax.ShapeDtypeStruct((S, W), BF16)],
        compiler_params=_cparams(),
    )(dxb, wo, qkv, gate, o, a)


def _fox_bwd(qaug, doaug, qkv, c, H, scatter=(), scatter_specs=()):
    na = len(scatter)
    S = qkv.shape[0]
    W = H * HEAD_DIM
    HP = H // 2
    tq = _row_tile(S, 512)
    nq = S // tq
    wb = W // LANE
    scale = HEAD_DIM ** -0.5

    def body(*refs):
        qa_ref, da_ref, k_ref, v_ref, c_ref = refs[:5]
        out_ref, dcr_ref, dcc_ref = refs[5 + na:8 + na]
        dq_sc, dk_sc, dv_sc = refs[8 + 2 * na:11 + 2 * na]
        hp, kj = pl.program_id(0), pl.program_id(1)
        if na:
            remote = _direct_scatter_copies(refs[5:5 + na], refs[8 + na:8 + 2 * na], scatter_specs,
                                            *refs[11 + 2 * na:])

            @pl.when((hp == 0) & (kj == 0))
            def _():
                for cp in remote:
                    cp.start()
        lane = lax.broadcasted_iota(jnp.int32, (tq, LANE), 1)
        own = [lane < HEAD_DIM, lane >= HEAD_DIM]
        rows = lax.broadcasted_iota(jnp.int32, (tq, tq), 0)
        cols = lax.broadcasted_iota(jnp.int32, (tq, tq), 1)

        @pl.when(kj == 0)
        def _():
            dq_sc[...] = jnp.zeros(dq_sc.shape, F32)

        @pl.when((kj == 0) & (hp == 0))
        def _():
            dcr_ref[...] = jnp.zeros(dcr_ref.shape, F32)
            dcc_ref[...] = jnp.zeros(dcc_ref.shape, F32)

        kblk, vblk, cblk = k_ref[...], v_ref[...], c_ref[...]
        one, zero = jnp.ones((tq, LANE), BF16), jnp.zeros((tq, LANE), BF16)
        ka, va = [], []
        for e in range(2):
            o0 = HEAD_DIM * (1 - e)
            c_col = jnp.sum(jnp.where(lane == 2 * hp + e, cblk, 0.0), axis=1, keepdims=True)
            ka.append(jnp.where(own[e], kblk, _bias_lanes(-c_col, lane, e, False)))
            va.append(jnp.where(own[e], vblk, jnp.where((lane >= o0) & (lane < o0 + 3), one, zero)))
        dk_sc[...] = jnp.zeros(dk_sc.shape, F32)
        dv_sc[...] = jnp.zeros(dv_sc.shape, F32)

        def step(i, masked):
            r0 = pl.multiple_of(i * tq, tq)
            for e in range(2):
                qa = qa_ref[e, pl.ds(r0, tq), :]
                da = da_ref[e, pl.ds(r0, tq), :]
                p = jnp.exp(_dot_nt(qa, ka[e]))
                if masked:
                    p = jnp.where(rows >= cols, p, 0.0)
                ds = p * _dot_nt(da, va[e])
                pb, dsb = p.astype(BF16), ds.astype(BF16)
                dv_sc[e] += _dot_tn(pb, da)
                dk_sc[e] += _dot_tn(dsb, qa)
                dq_sc[e, pl.ds(r0, tq), :] += _dot_nn(dsb, ka[e])

        step(kj, True)

        def loop_body(i, carry):
            step(i, False)
            return carry

        lax.fori_loop(kj + 1, nq, loop_body, 0)
        k0 = pl.multiple_of(kj * tq, tq)
        out_ref[1, pl.ds(k0, tq), :] = jnp.where(own[0], dk_sc[0], dk_sc[1]).astype(BF16)
        out_ref[2, pl.ds(k0, tq), :] = jnp.where(own[0], dv_sc[0], dv_sc[1]).astype(BF16)

        def put_lane(ref, r0, e, tile, src_lane):
            col = jnp.sum(jnp.where(lane == src_lane, tile, 0.0), axis=1, keepdims=True)
            ref[pl.ds(r0, tq), :] = jnp.where(lane == 2 * hp + e, col, ref[pl.ds(r0, tq), :])

        for e in range(2):
            put_lane(dcc_ref, k0, e, dk_sc[e], HEAD_DIM * (1 - e) + 3)

        @pl.when(kj == nq - 1)
        def _():
            def chunk(i, carry):
                r0 = pl.multiple_of(i * tq, tq)
                d0, d1 = dq_sc[0, pl.ds(r0, tq), :], dq_sc[1, pl.ds(r0, tq), :]
                out_ref[0, pl.ds(r0, tq), :] = (jnp.where(own[0], d0, d1) * scale).astype(BF16)
                put_lane(dcr_ref, r0, 0, d0, HEAD_DIM)
                put_lane(dcr_ref, r0, 1, d1, 0)
                return carry
            lax.fori_loop(0, nq, chunk, 0)

        if na:
            @pl.when((hp == HP - 1) & (kj == nq - 1))
            def _():
                _wait_all(remote)

    pair = pl.BlockSpec((2, S, LANE), lambda h, j: (0, 0, h))
    vec = pl.BlockSpec((S, LANE), lambda h, j: (0, 0))
    any_spec = pl.BlockSpec(memory_space=pl.ANY)
    sems = [pltpu.SemaphoreType.DMA((na, N_DEV - 1)), pltpu.SemaphoreType.DMA((na, N_DEV - 1))] if na else []
    outs = pl.pallas_call(
        body, name="fox_attn_bwd", grid=(HP, nq),
        in_specs=[pair, pair,
                  pl.BlockSpec((tq, LANE), lambda h, j: (j, wb + h)),
                  pl.BlockSpec((tq, LANE), lambda h, j: (j, 2 * wb + h)),
                  pl.BlockSpec((tq, LANE), lambda h, j: (j, 0))] + [any_spec] * na,
        out_specs=[pl.BlockSpec((3, S, LANE), lambda h, j: (0, 0, h)), vec, vec] + [any_spec] * na,
        out_shape=[jax.ShapeDtypeStruct((3, S, W), BF16), jax.ShapeDtypeStruct((S, LANE), F32),
                   jax.ShapeDtypeStruct((S, LANE), F32)]
        + [jax.ShapeDtypeStruct((N_DEV - 1,) + _scatter_block_shape(g, s), g.dtype)
           for g, s in zip(scatter, scatter_specs)],
        scratch_shapes=[pltpu.VMEM((2, S, LANE), F32), pltpu.VMEM((2, tq, LANE), F32),
                        pltpu.VMEM((2, tq, LANE), F32)] + sems,
        compiler_params=_cparams(),
    )(qaug, doaug, qkv, qkv, c, *scatter)
    return outs[0], outs[1], outs[2], list(outs[3:])


def _swa_pick(blk, half, lane):
    b = blk.astype(F32)
    r = pltpu.roll(b, HEAD_DIM, axis=1)
    return jnp.where(jnp.logical_xor(lane < HEAD_DIM, half == 1), b, r).astype(BF16)


def _swa_stack(t, lane, G):
    pieces = []
    z = jnp.zeros((SWA_BLOCK, LANE), t.dtype)
    for j in range(G // 2):
        tile = t[:, LANE * j:LANE * (j + 1)]
        pieces += [jnp.where(lane < HEAD_DIM, tile, z), jnp.where(lane < HEAD_DIM, z, tile)]
    return jnp.concatenate(pieces, axis=0)


def _swa_unstack(st, lane, G):
    tiles = []
    for j in range(G // 2):
        a = st[2 * j * SWA_BLOCK:(2 * j + 1) * SWA_BLOCK]
        b = st[(2 * j + 1) * SWA_BLOCK:(2 * j + 2) * SWA_BLOCK]
        tiles.append(jnp.where(lane < HEAD_DIM, a, b))
    return jnp.concatenate(tiles, axis=1)


def _swa_mask_bias(G):
    R = G * SWA_BLOCK
    t_loc = jnp.arange(R)[:, None] % SWA_BLOCK
    j_loc = jnp.arange(2 * SWA_BLOCK)[None, :]
    diff = t_loc + SWA_BLOCK - j_loc
    band = (diff >= 0) & (diff < SWA_BLOCK)
    return jnp.stack([jnp.where(band & (j_loc >= SWA_BLOCK), 0.0, NEG_INF),
                      jnp.where(band, 0.0, NEG_INF)]).astype(F32)


def _swa_scores(q, kp, kc, vp, vc, srow, bias, half, head0, G):
    lane = lax.broadcasted_iota(jnp.int32, (SWA_BLOCK, LANE), 1)
    kk = jnp.concatenate([_swa_pick(kp, half, lane), _swa_pick(kc, half, lane)], axis=0)
    vv = jnp.concatenate([_swa_pick(vp, half, lane), _swa_pick(vc, half, lane)], axis=0)
    qstack = _swa_stack(q, lane, G) * jnp.asarray(HEAD_DIM ** -0.5, BF16)
    s = _dot_nt(qstack, kk) + bias
    R = G * SWA_BLOCK
    lane1 = lax.broadcasted_iota(jnp.int32, (1, LANE), 1)
    sink = jnp.concatenate(
        [jnp.broadcast_to(jnp.sum(jnp.where(lane1 == head0 + g, srow, 0.0), axis=1, keepdims=True), (SWA_BLOCK, LANE))
         for g in range(G)], axis=0)
    m = jnp.maximum(jnp.broadcast_to(jnp.max(s, axis=1, keepdims=True), (R, LANE)), sink)
    e = jnp.exp(s - jnp.tile(m, (1, 2)))
    es = jnp.exp(sink - m)
    inv = 1.0 / (jnp.broadcast_to(jnp.sum(e, axis=1, keepdims=True), (R, LANE)) + es)
    return qstack, kk, vv, e * jnp.tile(inv, (1, 2)), es * inv, lane


def _swa_fwd(qkv, gate, sinks, mask_bias, HQ, HKV):
    S = qkv.shape[0]
    G = HQ // HKV
    WQ, KVW = HQ * HEAD_DIM, HKV * HEAD_DIM
    nb = S // SWA_BLOCK
    GW = G * HEAD_DIM
    kb, vb = WQ // LANE, (WQ + KVW) // LANE

    def body(q_ref, kp_ref, kc_ref, vp_ref, vc_ref, g_ref, sink_ref, b_ref, y_ref, o_ref):
        pair = pl.program_id(0)
        for half in range(2):
            cols = slice(GW * half, GW * (half + 1))
            _, _, vv, p, _, lane = _swa_scores(q_ref[:, cols], kp_ref[...], kc_ref[...], vp_ref[...], vc_ref[...],
                                               sink_ref[...], b_ref[0], half, (2 * pair + half) * G, G)
            o = _swa_unstack(_dot_nn(p.astype(BF16), vv), lane, G)
            g = g_ref[:, cols]
            y_ref[:, cols] = (o * (g * _sigmoid(g))).astype(BF16)
            o_ref[:, cols] = o.astype(BF16)

    blk = lambda cb, prev: pl.BlockSpec(
        (SWA_BLOCK, LANE), lambda h, n, cb=cb, prev=prev: (jnp.maximum(n - prev, 0), cb + h))
    qspec = pl.BlockSpec((SWA_BLOCK, 2 * GW), lambda h, n: (n, h))
    return pl.pallas_call(
        body, name="swa_attn_fwd", grid=(HKV // 2, nb),
        in_specs=[qspec, blk(kb, 1), blk(kb, 0), blk(vb, 1), blk(vb, 0), qspec,
                  pl.BlockSpec((1, LANE), lambda h, n: (0, 0)),
                  pl.BlockSpec((1, G * SWA_BLOCK, 2 * SWA_BLOCK), lambda h, n: (jnp.minimum(n, 1), 0, 0))],
        out_specs=[qspec, qspec],
        out_shape=[jax.ShapeDtypeStruct((S, WQ), BF16), jax.ShapeDtypeStruct((S, WQ), BF16)],
        compiler_params=_cparams(),
    )(qkv, qkv, qkv, qkv, qkv, gate, sinks, mask_bias)


def _swa_bwd(qkv, dy, gate, o, sinks, tables, mask_bias, HQ, HKV):
    S = qkv.shape[0]
    G = HQ // HKV
    WQ, KVW = HQ * HEAD_DIM, HKV * HEAD_DIM
    nb = S // SWA_BLOCK
    GW = G * HEAD_DIM
    R = G * SWA_BLOCK
    kb, vb = WQ // LANE, (WQ + KVW) // LANE
    scale = HEAD_DIM ** -0.5
    assert G == 8

    def body(q_ref, kp_ref, kc_ref, vp_ref, vc_ref, dy_ref, g_ref, o_ref, sink_ref, t_ref, b_ref,
             dqg_ref, dkv_ref, dsink_ref, carry_sc):
        pair, n = pl.program_id(0), pl.program_id(1)

        @pl.when(n == 0)
        def _():
            carry_sc[...] = jnp.zeros(carry_sc.shape, F32)
            dsink_ref[...] = jnp.zeros(dsink_ref.shape, F32)

        @pl.when(n < nb)
        def _():
            t0, t1, t2 = (jnp.tile(t_ref[i], (1, GW // LANE)) for i in range(3))
            for half in range(2):
                cols = slice(GW * half, GW * (half + 1))
                qstack, kk, vv, p, psink, lane = _swa_scores(
                    q_ref[:, cols], kp_ref[...], kc_ref[...], vp_ref[...], vc_ref[...], sink_ref[...], b_ref[0],
                    half, (2 * pair + half) * G, G)
                dyv, g, ov = dy_ref[:, cols], g_ref[:, cols], o_ref[:, cols].astype(F32)
                sg = _sigmoid(g)
                dob = (dyv * (g * sg)).astype(BF16)
                dqg_ref[1, :, cols] = (dyv * ov * (sg * (1.0 + g * (1.0 - sg)))).astype(BF16)
                prod = dob.astype(F32) * ov
                dparts = []
                for j in range(G // 2):
                    tile = prod[:, LANE * j:LANE * (j + 1)]
                    for sel in (jnp.where(lane < HEAD_DIM, tile, 0.0), jnp.where(lane < HEAD_DIM, 0.0, tile)):
                        dparts.append(jnp.broadcast_to(jnp.sum(sel, axis=1, keepdims=True), (SWA_BLOCK, LANE)))
                delta = jnp.concatenate(dparts, axis=0)
                dostack = _swa_stack(dob, lane, G)
                ds = p * (_dot_nt(dostack, vv) - jnp.tile(delta, (1, 2)))
                dsb, pb = ds.astype(BF16), p.astype(BF16)
                dq = _swa_unstack(_dot_nn(dsb, kk), lane, G) * scale
                dq = dq * t0 + pltpu.roll(dq * t1, ROT_DIM // 2, axis=1) + pltpu.roll(dq * t2, GW - ROT_DIM // 2, axis=1)
                dqg_ref[0, :, cols] = dq.astype(BF16)
                dkk = _dot_tn(dsb, qstack)
                dvv = _dot_tn(pb, dostack)
                dkk = dkk + pltpu.roll(dkk, HEAD_DIM, axis=1)
                dvv = dvv + pltpu.roll(dvv, HEAD_DIM, axis=1)
                lane2 = lax.broadcasted_iota(jnp.int32, (2 * SWA_BLOCK, LANE), 1)
                comb = jnp.where(lane2 < HEAD_DIM, dkk, dvv)
                dkv_ref[half] = carry_sc[half] + comb[:SWA_BLOCK]
                carry_sc[half] = comb[SWA_BLOCK:]
                sk = psink * delta
                rows = [-jnp.sum(sk[g_ * SWA_BLOCK:(g_ + 1) * SWA_BLOCK], axis=0, keepdims=True) for g_ in range(G)]
                dsink_ref[half] += jnp.concatenate(rows, axis=0)

        @pl.when(n == nb)
        def _():
            dkv_ref[...] = carry_sc[...]

    cl = lambda n: jnp.minimum(n, nb - 1)
    blk = lambda cb, prev: pl.BlockSpec(
        (SWA_BLOCK, LANE), lambda h, n, cb=cb, prev=prev: (jnp.maximum(cl(n) - prev, 0), cb + h))
    qspec = pl.BlockSpec((SWA_BLOCK, 2 * GW), lambda h, n: (cl(n), h))
    return pl.pallas_call(
        body, name="swa_attn_bwd", grid=(HKV // 2, nb + 1),
        in_specs=[qspec, blk(kb, 1), blk(kb, 0), blk(vb, 1), blk(vb, 0), qspec, qspec, qspec,
                  pl.BlockSpec((1, LANE), lambda h, n: (0, 0)),
                  pl.BlockSpec((3, SWA_BLOCK, LANE), lambda h, n: (0, cl(n), 0)),
                  pl.BlockSpec((1, R, 2 * SWA_BLOCK), lambda h, n: (jnp.minimum(n, 1), 0, 0))],
        out_specs=[pl.BlockSpec((2, SWA_BLOCK, 2 * GW), lambda h, n: (0, cl(n), h)),
                   pl.BlockSpec((2, SWA_BLOCK, LANE), lambda h, n: (h, jnp.maximum(n - 1, 0), 0)),
                   pl.BlockSpec((2, 8, LANE), lambda h, n: (h, 0, 0))],
        out_shape=[jax.ShapeDtypeStruct((2, S, WQ), BF16), jax.ShapeDtypeStruct((HKV, S, LANE), F32),
                   jax.ShapeDtypeStruct((HKV, 8, LANE), F32)],
        scratch_shapes=[pltpu.VMEM((2, SWA_BLOCK, LANE), F32)],
        compiler_params=_cparams(),
    )(qkv, qkv, qkv, qkv, qkv, dy, gate, o, sinks, tables, mask_bias)


def _swa_dkv_finish(dkv, tables):
    HKV, S, _ = dkv.shape
    KVW = HKV * HEAD_DIM
    tm = _row_tile(S, 512)
    npair = HKV // 2

    def body(d_ref, t_ref, o_ref):
        lane = lax.broadcasted_iota(jnp.int32, (tm, LANE), 1)
        lo = lane < HEAD_DIM
        for p in range(npair):
            a, b = d_ref[2 * p], d_ref[2 * p + 1]
            tk = jnp.where(lo, a, pltpu.roll(b, HEAD_DIM, axis=1))
            tv = jnp.where(lo, pltpu.roll(a, HEAD_DIM, axis=1), b)
            tk = (tk * t_ref[0] + pltpu.roll(tk * t_ref[1], ROT_DIM // 2, axis=1)
                  + pltpu.roll(tk * t_ref[2], LANE - ROT_DIM // 2, axis=1))
            o_ref[:, LANE * p:LANE * (p + 1)] = tk.astype(BF16)
            o_ref[:, KVW + LANE * p:KVW + LANE * (p + 1)] = tv.astype(BF16)

    return pl.pallas_call(
        body, name="swa_dkv_finish", grid=(S // tm,),
        in_specs=[pl.BlockSpec((HKV, tm, LANE), lambda i: (0, i, 0)), pl.BlockSpec((3, tm, LANE), lambda i: (0, i, 0))],
        out_specs=pl.BlockSpec((tm, 2 * KVW), lambda i: (i, 0)),
        out_shape=jax.ShapeDtypeStruct((S, 2 * KVW), BF16),
        compiler_params=_cparams(),
    )(dkv, tables)


def _rope_tables(S, width):
    half = ROT_DIM // 2
    pos = jnp.arange(S, dtype=F32)
    inv_freq = ROPE_THETA ** (-jnp.arange(half, dtype=F32) / half)
    ang = pos[:, None] * inv_freq[None, :]
    cos, sin = jnp.cos(ang), jnp.sin(ang)
    one = jnp.ones((S, HEAD_DIM - ROT_DIM), F32)
    zero = jnp.zeros((S, HEAD_DIM - ROT_DIM), F32)
    zh = jnp.zeros((S, half), F32)
    t0 = jnp.concatenate([cos, cos, one], axis=1)
    t1 = jnp.concatenate([-sin, zh, zero], axis=1)
    t2 = jnp.concatenate([zh, sin, zero], axis=1)
    return jnp.stack([jnp.tile(t, (1, width // HEAD_DIM)) for t in (t0, t1, t2)])


def _pad_rows(v, row, total_rows=8):
    return jnp.pad(v, ((row, total_rows - row - v.shape[0]), (0, 0)))


def _pad_lanes(v, off, width):
    return jnp.pad(v, ((0, 0), (off, width - off - v.shape[1])))


def kernel(x, norm_g, fox_w_in, fox_b_f, fox_w_out, swa_w_in, swa_sinks, swa_w_out, final_g, loss_target, m_norm_g, m_fox_w_in, m_fox_b_f, m_fox_w_out, m_swa_w_in, m_swa_sinks, m_swa_w_out, m_final_g, v_norm_g, v_fox_w_in, v_fox_b_f, v_fox_w_out, v_swa_w_in, v_swa_sinks, v_swa_w_out, v_final_g):
    S, D = x.shape[1], x.shape[2]
    H = fox_b_f.shape[1]
    W = H * HEAD_DIM
    wf = fox_w_in.shape[2]
    ws = swa_w_in.shape[2]
    HQ = swa_sinks.shape[1]
    WQ = HQ * HEAD_DIM
    KVW = (ws * N_DEV - 2 * WQ) // 2
    HKV = KVW // HEAD_DIM
    rows_o = fox_w_out.shape[1]
    assert wf * N_DEV == 4 * W + H and rows_o * N_DEV == W and H <= LANE and HQ <= LANE
    me = _my_index()

    _, sw_f, np_f = _slab_geom(wf)
    _, sw_s, np_s = _slab_geom(ws)

    def slab(w2d, w, sw):
        return jnp.pad(w2d.astype(BF16), ((0, 0), (0, sw - w)))

    (fi_all,) = _all_gather([slab(fox_w_in[0], wf, sw_f)])
    w_fi = _assemble(fi_all, wf)
    later = [slab(swa_w_in[0], ws, sw_s), fox_w_out[0].astype(BF16), swa_w_out[0].astype(BF16)]

    x0 = x[0]
    g0, g1, gf = norm_g[0:1], norm_g[1:2], final_g[None, :]
    bias = _pad_lanes(fox_b_f, 0, LANE)
    sinks = _pad_lanes(swa_sinks, 0, LANE)
    tab_k = _rope_tables(S, LANE)
    mask_bias = _swa_mask_bias(HQ // HKV)

    h0 = _rmsnorm_fwd(x0, g0, "rmsnorm0")
    qkv0 = _proj(h0, w_fi, 0, 3 * W, BF16, "fox_in_qkv")
    gate0 = _proj(h0, w_fi, 3 * W, W, F32, "fox_in_gate")
    fl = _proj(h0, w_fi, 4 * W, LANE, F32, "fox_in_f")
    c = _fox_gate_fwd(fl, bias)
    y0, o0, a0, (si_all, fo_all, so_all) = _fox_fwd(qkv0, gate0, c, H, gather=later)
    w_si = _assemble(si_all, ws)
    w_fo = fo_all.reshape(W, D)
    w_so = so_all.reshape(WQ, D)
    x1, h1 = _out_proj_norm(y0, w_fo, x0, g1, "fox_out")

    qkv1 = _proj(h1, w_si, 0, WQ + 2 * KVW, BF16, "swa_in_qkv", rope=(tab_k, WQ + KVW))
    gate1 = _proj(h1, w_si, WQ + 2 * KVW, WQ, F32, "swa_in_gate")
    y1, o1 = _swa_fwd(qkv1, gate1, sinks, mask_bias, HQ, HKV)
    dx2, dx2b, dgf, loss_p = _out_proj_loss(y1, w_so, x1, loss_target[0], gf, "swa_out_loss")

    dy1 = _matmul_nt([(dx2b, None, 0)], w_so, WQ, "swa_out_bwd")
    g_so, g_so_h = _matmul_tn(y1, [(dx2b, None, 0)], D, "swa_out_wgrad", also_bf16=True)
    dqg1, dkv1, dsink = _swa_bwd(qkv1, dy1, gate1, o1, sinks, tab_k, mask_bias, HQ, HKV)
    dkv1f = _swa_dkv_finish(dkv1, tab_k)
    parts1 = [(dqg1, 0, 0), (dkv1f, None, WQ), (dqg1, 1, WQ + 2 * KVW)]
    g_si, g_si_h = _matmul_tn(h1, parts1, np_s, "swa_in_wgrad", tile_major=True, also_bf16=True)
    dh1 = _matmul_nt(parts1, w_si, D, "swa_in_bwd")
    dx1, dx1b, dg1 = _rmsnorm_bwd(dh1, x1, g1, dx2, "rmsnorm1_bwd")

    qaug0, doaug0, dgate0 = _fox_out_bwd(dx1b, w_fo, qkv0, gate0, o0, a0, H)
    g_fo, g_fo_h = _matmul_tn(y0, [(dx1b, None, 0)], D, "fox_out_wgrad", also_bf16=True)
    early_specs = [("col", ws), ("row", rows_o), ("row", rows_o)]
    dqkv0, dcr, dcc, early_recv = _fox_bwd(qaug0, doaug0, qkv0, c, H, scatter=[g_si_h, g_fo_h, g_so_h],
                                          scatter_specs=early_specs)
    dfl, dbf = _fox_gate_bwd(fl, bias, dcr - dcc)
    parts0 = [(dqkv0, p, p * W) for p in range(3)] + [(dgate0, None, 3 * W), (dfl, None, 4 * W)]
    g_fi, g_fi_h = _matmul_tn(h0, parts0, np_f, "fox_in_wgrad", tile_major=True, also_bf16=True)
    spec_fi = ("col", wf)
    fi_sems, fi_src, fi_land, token = _scatter_start(g_fi_h, spec_fi)
    parts0[-1] = (dfl + token[0, 0].astype(BF16), None, 4 * W)
    dh0 = _matmul_nt(parts0, w_fi, D, "fox_in_bwd")
    dx0, _, dg0 = _rmsnorm_bwd(dh0, x0, g0, dx1, "rmsnorm0_bwd")

    red_si, gw_fo, gw_so = [_final_sum8(g_, r_, s_)
                            for g_, r_, s_ in zip([g_si, g_fo, g_so], early_recv, early_specs)]
    gw_si = lax.dynamic_slice(red_si, (0, (ws * me) % LANE), (D, ws))

    P = D
    dsink_v = dsink[:, :, 0].reshape(1, HQ)
    row3 = _pad_lanes(dbf[:, :H], 0, P) + _pad_lanes(dsink_v, LANE, P) + _pad_lanes(loss_p[:, :1], 2 * LANE, P)
    pack = _pad_rows(dg0, 0) + _pad_rows(dg1, 1) + _pad_rows(dgf, 2) + _pad_rows(row3, 3)

    d_fo, m_fo, v_fo = _adamw(fox_w_out[0], gw_fo, m_fox_w_out[0], v_fox_w_out[0], "adamw_fox_out")
    d_si, m_si, v_si = _adamw(swa_w_in[0], gw_si, m_swa_w_in[0], v_swa_w_in[0], "adamw_swa_in")
    d_so, m_so, v_so = _adamw(swa_w_out[0], gw_so, m_swa_w_out[0], v_swa_w_out[0], "adamw_swa_out")
    recv_fi = _scatter_wait(fi_sems, fi_src, fi_land, spec_fi, after=[dx0, pack, d_fo, d_si, d_so])
    red_fi = _final_sum8(g_fi, recv_fi, spec_fi)
    gw_fi = lax.dynamic_slice(red_fi, (0, (wf * me) % LANE), (D, wf))
    d_fi, m_fi, v_fi = _adamw(fox_w_in[0], gw_fi, m_fox_w_in[0], v_fox_w_in[0], "adamw_fox_in")

    tot = _all_reduce_small(pack, after=recv_fi)
    loss = tot[3, 2 * LANE]
    g_norm = tot[0:2]
    g_final = tot[2]
    g_bf = tot[3:4, 0:H]
    g_sinks = tot[3:4, LANE:LANE + HQ]

    def small_pack(ng, fg, bf, sk):
        r3 = _pad_lanes(bf, 0, P) + _pad_lanes(sk, LANE, P)
        return _pad_rows(ng, 0) + _pad_rows(fg[None, :], 2) + _pad_rows(r3, 3)

    sd, sm, sv = _adamw(small_pack(norm_g, final_g, fox_b_f, swa_sinks), tot,
                        small_pack(m_norm_g, m_final_g, m_fox_b_f, m_swa_sinks),
                        small_pack(v_norm_g, v_final_g, v_fox_b_f, v_swa_sinks), "adamw_small")

    def unpack(t):
        return t[0:2], t[3:4, 0:H], t[3:4, LANE:LANE + HQ], t[2]

    def group(small, fi, fo, si, so):
        ng, bf, sk, fg = unpack(small)
        return (ng, fi[None], bf, fo[None], si[None], sk, so[None], fg)

    grads = (g_norm, gw_fi[None], g_bf, gw_fo[None], gw_si[None], g_sinks, gw_so[None], g_final)
    return (loss, dx0[None], *grads, *group(sd, d_fi, d_fo, d_si, d_so),
            *group(sm, m_fi, m_fo, m_si, m_so), *group(sv, v_fi, v_fo, v_si, v_so))
```

```python
import math

import jax
import jax.numpy as jnp
from jax import lax
from jax.experimental import pallas as pl
from jax.experimental.pallas import tpu as pltpu

F32 = jnp.float32
BF16 = jnp.bfloat16
MESH = pl.DeviceIdType.MESH

N_DEV = 8
LANE = 128
HEAD_DIM = 64
SWA_BLOCK = 128
NEG_INF = -1e30
RMS_EPS = 1e-6
ROPE_THETA = 500000.0
ROT_DIM = HEAD_DIM // 4
ADAM_LR, ADAM_B1, ADAM_B2, ADAM_EPS, ADAM_WD, ADAM_STEP = 0.001, 0.9, 0.999, 1e-08, 0.01, 10
VMEM_LIMIT = 56 * 1024 * 1024
MM_TILE = 1024


def _cparams(**kw):
    return pltpu.CompilerParams(vmem_limit_bytes=VMEM_LIMIT, **kw)


def _tile(n, cap):
    if n <= cap:
        return n
    t = (cap // LANE) * LANE
    while t > LANE and n % t:
        t -= LANE
    assert n % t == 0, (n, cap)
    return t


def _row_tile(n, cap):
    t = min(n, cap)
    while n % t:
        t //= 2
    return t


def _dot_nn(a, b):
    return jnp.dot(a, b, preferred_element_type=F32)


def _dot_nt(a, b):
    return lax.dot_general(a, b, (((1,), (1,)), ((), ())), preferred_element_type=F32)


def _dot_tn(a, b):
    return lax.dot_general(a, b, (((0,), (0,)), ((), ())), preferred_element_type=F32)


def _sigmoid(g):
    return 1.0 / (1.0 + jnp.exp(-g))


def _slab_geom(w):
    starts = [w * i for i in range(N_DEV)]
    aligned = [LANE * (s // LANE) for s in starts]
    offs = [s - a for s, a in zip(starts, aligned)]
    sw = LANE * (-(-(max(offs) + w) // LANE))
    return aligned, sw, aligned[-1] + sw


def _my_index():
    return 4 * lax.axis_index("x") + 2 * lax.axis_index("y") + lax.axis_index("c")


def _all_gather(arrs):
    n = len(arrs)

    def body(*refs):
        ins, outs = refs[:n], refs[n:2 * n]
        send_sems, recv_sems, local_sems = refs[2 * n:]
        x, y, c = lax.axis_index("x"), lax.axis_index("y"), lax.axis_index("c")
        me, sib = (x, y, c), (x, y, 1 - c)
        chips = [(1 - x, y), (x, 1 - y), (1 - x, 1 - y)]

        def idx(px, py, pc):
            return 4 * px + 2 * py + pc

        def copy(a, k, block, to, src=None):
            dst = outs[a].at[idx(*block)]
            return pltpu.make_async_remote_copy(
                src_ref=dst if src is None else src, dst_ref=dst,
                send_sem=send_sems.at[a, k], recv_sem=recv_sems.at[a, k],
                device_id=to, device_id_type=MESH)

        mine = [pltpu.make_async_copy(ins[a], outs[a].at[idx(*me)], local_sems.at[a]) for a in range(n)]
        for m in mine:
            m.start()
        first = []
        for a in range(n):
            first.append(copy(a, 0, me, sib, src=ins[a]))
            for j, chip in enumerate(chips):
                first.append(copy(a, 1 + j, me, (*chip, c), src=ins[a]))
        for cp in first:
            cp.start()
        passed = []
        for j, chip in enumerate(chips):
            for a in range(n):
                copy(a, 1 + j, (*chip, c), me).wait_recv()
                p = copy(a, 4 + j, (*chip, c), sib)
                p.start()
                passed.append(p)
        for a in range(n):
            copy(a, 0, sib, me).wait_recv()
        for j, chip in enumerate(chips):
            for a in range(n):
                copy(a, 4 + j, (*chip, 1 - c), me).wait_recv()
        for cp in first + passed:
            cp.wait_send()
        for m in mine:
            m.wait()

    any_spec = pl.BlockSpec(memory_space=pl.ANY)
    return pl.pallas_call(
        body, name="weights_all_gather",
        out_shape=[jax.ShapeDtypeStruct((N_DEV,) + a.shape, a.dtype) for a in arrs],
        in_specs=[any_spec] * n, out_specs=[any_spec] * n,
        scratch_shapes=[pltpu.SemaphoreType.DMA((n, 7)), pltpu.SemaphoreType.DMA((n, 7)),
                        pltpu.SemaphoreType.DMA((n,))],
    )(*arrs)


def _rs_windows(specs):
    def window(ref, spec, blk):
        kind, n = spec
        if kind == "col":
            _, sw, _ = _slab_geom(n)
            return ref.at[pl.ds((n * blk) // LANE, sw // LANE)]
        start = pl.multiple_of(n * blk, n)
        return ref.at[pl.ds(start, n), :]
    return window


def _peer(k):
    x, y, c = lax.axis_index("x"), lax.axis_index("y"), lax.axis_index("c")
    return (x ^ (k >> 2), y ^ ((k >> 1) & 1), c ^ (k & 1))


def _direct_gather_copies(ins, outs, send_sems, recv_sems, local_sems):
    me = _my_index()
    remote, local = [], []
    for a, (src, dst) in enumerate(zip(ins, outs)):
        local.append(pltpu.make_async_copy(src, dst.at[me], local_sems.at[a]))
        for k in range(1, N_DEV):
            remote.append(pltpu.make_async_remote_copy(
                src_ref=src, dst_ref=dst.at[me], send_sem=send_sems.at[a, k - 1], recv_sem=recv_sems.at[a, k - 1],
                device_id=_peer(k), device_id_type=MESH))
    return remote, local


def _direct_scatter_copies(ins, outs, specs, send_sems, recv_sems):
    window = _rs_windows(specs)
    remote = []
    for a, (src, dst) in enumerate(zip(ins, outs)):
        for k in range(1, N_DEV):
            px, py, pc = _peer(k)
            remote.append(pltpu.make_async_remote_copy(
                src_ref=window(src, specs[a], 4 * px + 2 * py + pc), dst_ref=dst.at[k - 1],
                send_sem=send_sems.at[a, k - 1], recv_sem=recv_sems.at[a, k - 1],
                device_id=(px, py, pc), device_id_type=MESH))
    return remote


def _scatter_block_shape(g, spec):
    kind, w = spec
    return (_slab_geom(w)[1] // LANE, g.shape[1], LANE) if kind == "col" else (w, g.shape[1])


def _wait_all(remote, local=()):
    for cp in remote:
        cp.wait_recv()
    for cp in remote:
        cp.wait_send()
    for cp in local:
        cp.wait()


def _scatter_start(g, spec):
    blk = _scatter_block_shape(g, spec)
    window = _rs_windows([spec])
    npeer = N_DEV - 1

    def body(g_ref, land_ref, *rest):
        sems = rest[:2 * npeer]
        token = rest[2 * npeer + 2]
        for cp in _peer_block_copies(g_ref, land_ref, spec, window, sems[:npeer], sems[npeer:]):
            cp.start()
        token[...] = jnp.zeros(token.shape, token.dtype)

    hbm = pl.BlockSpec(memory_space=pltpu.HBM)
    sem = pl.BlockSpec(memory_space=pltpu.SEMAPHORE)
    land = lax.empty((npeer,) + blk, g.dtype)
    outs = pl.pallas_call(
        body, name="grads_scatter_start",
        out_shape=(pltpu.SemaphoreType.DMA(()),) * (2 * npeer)
        + (pltpu.HBM(g.shape, g.dtype), pltpu.HBM(land.shape, land.dtype), jax.ShapeDtypeStruct((8, LANE), F32)),
        in_specs=(hbm, hbm),
        out_specs=(sem,) * (2 * npeer) + (hbm, hbm, pl.BlockSpec(memory_space=pltpu.VMEM)),
        input_output_aliases={0: 2 * npeer, 1: 2 * npeer + 1},
        compiler_params=pltpu.CompilerParams(has_side_effects=pltpu.SideEffectType.DATAFLOW_SIDE_EFFECTING),
    )(pltpu.with_memory_space_constraint(g, pltpu.HBM), pltpu.with_memory_space_constraint(land, pltpu.HBM))
    return outs[:2 * npeer], outs[2 * npeer], outs[2 * npeer + 1], outs[2 * npeer + 2]


def _peer_block_copies(g_ref, land_ref, spec, window, send_sems, recv_sems):
    copies = []
    for k in range(1, N_DEV):
        px, py, pc = _peer(k)
        copies.append(pltpu.make_async_remote_copy(
            src_ref=window(g_ref, spec, 4 * px + 2 * py + pc), dst_ref=land_ref.at[k - 1],
            send_sem=send_sems[k - 1], recv_sem=recv_sems[k - 1], device_id=(px, py, pc), device_id_type=MESH))
    return copies


def _scatter_wait(sems, g_thru, land_thru, spec, after):
    window = _rs_windows([spec])
    npeer = N_DEV - 1

    def body(g_ref, land_ref, *rest):
        s = rest[:2 * npeer]
        copies = _peer_block_copies(g_ref, land_ref, spec, window, s[:npeer], s[npeer:])
        for cp in copies:
            cp.wait_send()
        for cp in copies:
            cp.wait_recv()

    hbm = pl.BlockSpec(memory_space=pltpu.HBM)
    sem = pl.BlockSpec(memory_space=pltpu.SEMAPHORE)
    return pl.pallas_call(
        body, name="grads_scatter_wait",
        out_shape=(pltpu.HBM(g_thru.shape, g_thru.dtype), pltpu.HBM(land_thru.shape, land_thru.dtype)),
        in_specs=(hbm, hbm) + (sem,) * (2 * npeer) + (pl.BlockSpec(memory_space=pl.ANY),) * len(after),
        out_specs=(hbm, hbm), input_output_aliases={0: 0, 1: 1},
        compiler_params=pltpu.CompilerParams(has_side_effects=pltpu.SideEffectType.DATAFLOW_SIDE_EFFECTING),
    )(g_thru, land_thru, *sems, *after)[1]


def _final_sum8(g, recv, spec):
    kind, n = spec
    me = _my_index()
    offs = jnp.stack([(n * me) // LANE if kind == "col" else me]).astype(jnp.int32)
    if kind == "col":
        _, T, M, _ = recv.shape
        grid = (T,)
        in_specs = [pl.BlockSpec((1, M, LANE), lambda t, o: (o[0] + t, 0, 0)),
                    pl.BlockSpec((N_DEV - 1, 1, M, LANE), lambda t, o: (0, t, 0, 0))]
        out_spec = pl.BlockSpec((M, LANE), lambda t, o: (0, t))
        out_shape = jax.ShapeDtypeStruct((M, T * LANE), F32)
    else:
        _, nrow, C = recv.shape
        grid = (1,)
        in_specs = [pl.BlockSpec((nrow, C), lambda t, o: (o[0], 0)),
                    pl.BlockSpec((N_DEV - 1, nrow, C), lambda t, o: (0, 0, 0))]
        out_spec = pl.BlockSpec((nrow, C), lambda t, o: (0, 0))
        out_shape = jax.ShapeDtypeStruct((nrow, C), F32)

    def body(o_ref, g_ref, r_ref, out_ref):
        acc = g_ref[0] if kind == "col" else g_ref[...]
        for k in range(N_DEV - 1):
            acc = acc + (r_ref[k, 0] if kind == "col" else r_ref[k]).astype(F32)
        out_ref[...] = acc

    return pl.pallas_call(
        body, name="grads_final_sum8",
        grid_spec=pltpu.PrefetchScalarGridSpec(num_scalar_prefetch=1, grid=grid, in_specs=in_specs,
                                               out_specs=out_spec),
        out_shape=out_shape, compiler_params=_cparams(),
    )(offs, g, recv)


def _all_reduce_small(pack, after):
    R, P = pack.shape

    def body(x_ref, after_ref, o_ref, gat_ref, send_sems, recv_sems):
        x, y, c = lax.axis_index("x"), lax.axis_index("y"), lax.axis_index("c")
        me = 4 * x + 2 * y + c
        gat_ref[me] = x_ref[...]
        copies = []
        for k in range(1, N_DEV):
            peer = (x ^ (k >> 2), y ^ ((k >> 1) & 1), c ^ (k & 1))
            copies.append(pltpu.make_async_remote_copy(
                src_ref=x_ref, dst_ref=gat_ref.at[me],
                send_sem=send_sems.at[k - 1], recv_sem=recv_sems.at[k - 1],
                device_id=peer, device_id_type=MESH))
        for cp in copies:
            cp.start()
        for cp in copies:
            cp.wait_recv()
        for cp in copies:
            cp.wait_send()
        acc = gat_ref[0]
        for d in range(1, N_DEV):
            acc = acc + gat_ref[d]
        o_ref[...] = acc

    vm = pl.BlockSpec(memory_space=pltpu.VMEM)
    return pl.pallas_call(
        body, name="small_all_reduce",
        out_shape=jax.ShapeDtypeStruct((R, P), F32),
        in_specs=[vm, pl.BlockSpec(memory_space=pl.ANY)], out_specs=vm,
        scratch_shapes=[pltpu.VMEM((N_DEV, R, P), F32),
                        pltpu.SemaphoreType.DMA((N_DEV - 1,)), pltpu.SemaphoreType.DMA((N_DEV - 1,))],
    )(pack, after)


def _assemble(slabs, w):
    aligned, sw, total = _slab_geom(w)
    K = slabs.shape[1]
    tr = _row_tile(K, 256)

    def body(s_ref, o_ref):
        o_ref[...] = jnp.zeros(o_ref.shape, BF16)
        for i in range(N_DEV):
            a, off = aligned[i], w * i - aligned[i]
            x = s_ref[i].astype(F32)
            if off:
                x = pltpu.roll(x, off, axis=1)
            o_ref[:, a:a + sw] = (o_ref[:, a:a + sw].astype(F32) + x).astype(BF16)

    return pl.pallas_call(
        body, name="assemble_w_in", grid=(K // tr,),
        in_specs=[pl.BlockSpec((N_DEV, tr, sw), lambda i: (0, i, 0))],
        out_specs=pl.BlockSpec((tr, total), lambda i: (i, 0)),
        out_shape=jax.ShapeDtypeStruct((K, total), BF16),
        compiler_params=_cparams(),
    )(slabs)


def _rmsnorm_fwd(x, g, name):
    S, D = x.shape
    tm = _row_tile(S, 256)

    def body(x_ref, g_ref, h_ref):
        xv = x_ref[...]
        r = lax.rsqrt(jnp.mean(xv * xv, axis=-1, keepdims=True) + RMS_EPS)
        h_ref[...] = ((xv * r) * g_ref[...]).astype(BF16)

    return pl.pallas_call(
        body, name=name, grid=(S // tm,),
        in_specs=[pl.BlockSpec((tm, D), lambda i: (i, 0)), pl.BlockSpec((1, D), lambda i: (0, 0))],
        out_specs=pl.BlockSpec((tm, D), lambda i: (i, 0)),
        out_shape=jax.ShapeDtypeStruct((S, D), BF16),
        compiler_params=_cparams(),
    )(x, g)


def _rmsnorm_bwd(dh, x, g, dres, name):
    S, D = x.shape
    tm = _row_tile(S, 256)

    def body(dh_ref, x_ref, g_ref, dr_ref, dx_ref, dxb_ref, dg_ref):
        xv = x_ref[...]
        r = lax.rsqrt(jnp.mean(xv * xv, axis=-1, keepdims=True) + RMS_EPS)
        xhat = xv * r
        d = dh_ref[...]
        gd = d * g_ref[...]
        dx = r * (gd - xhat * jnp.mean(gd * xhat, axis=-1, keepdims=True)) + dr_ref[...]
        dx_ref[...] = dx
        dxb_ref[...] = dx.astype(BF16)

        @pl.when(pl.program_id(0) == 0)
        def _():
            dg_ref[...] = jnp.zeros(dg_ref.shape, F32)
        dg_ref[...] += jnp.sum(d * xhat, axis=0, keepdims=True)

    row = pl.BlockSpec((tm, D), lambda i: (i, 0))
    vec = pl.BlockSpec((1, D), lambda i: (0, 0))
    return pl.pallas_call(
        body, name=name, grid=(S // tm,),
        in_specs=[row, row, vec, row], out_specs=[row, row, vec],
        out_shape=[jax.ShapeDtypeStruct((S, D), F32), jax.ShapeDtypeStruct((S, D), BF16),
                   jax.ShapeDtypeStruct((1, D), F32)],
        compiler_params=_cparams(),
    )(dh, x, g, dres)


def _adamw(w, g, m, v, name):
    R, C = w.shape
    tr = _row_tile(R, 256)
    c1 = 1.0 - ADAM_B1 ** ADAM_STEP
    c2 = 1.0 - ADAM_B2 ** ADAM_STEP

    def body(w_ref, g_ref, m_ref, v_ref, d_ref, nm_ref, nv_ref):
        gv = g_ref[...]
        nm = ADAM_B1 * m_ref[...] + (1.0 - ADAM_B1) * gv
        nv = ADAM_B2 * v_ref[...] + (1.0 - ADAM_B2) * (gv * gv)
        d_ref[...] = -ADAM_LR * ((nm / c1) / (jnp.sqrt(nv / c2) + ADAM_EPS) + ADAM_WD * w_ref[...])
        nm_ref[...] = nm
        nv_ref[...] = nv

    spec = pl.BlockSpec((tr, C), lambda i: (i, 0))
    return pl.pallas_call(
        body, name=name, grid=(R // tr,),
        in_specs=[spec] * 4, out_specs=[spec] * 3,
        out_shape=[jax.ShapeDtypeStruct((R, C), F32)] * 3,
        compiler_params=_cparams(),
    )(w, g, m, v)


def _proj(h, wfull, col0, ncols, out_dtype, name, rope=None):
    S, K = h.shape
    tm = _row_tile(S, MM_TILE)
    tn = math.gcd(_tile(ncols, MM_TILE), col0) if col0 else _tile(ncols, MM_TILE)
    if rope is not None:
        tn = _tile(math.gcd(ncols, rope[1]), MM_TILE)
    assert ncols % tn == 0 and col0 % tn == 0
    cb = col0 // tn

    def body(*refs):
        if rope is None:
            a_ref, b_ref, o_ref = refs
        else:
            a_ref, b_ref, t_ref, o_ref = refs
        acc = _dot_nn(a_ref[...], b_ref[...])
        if rope is not None:
            t0, t1, t2 = (jnp.tile(t_ref[i], (1, tn // LANE)) for i in range(3))
            roped = (acc * t0 + pltpu.roll(acc, tn - ROT_DIM // 2, axis=1) * t1
                     + pltpu.roll(acc, ROT_DIM // 2, axis=1) * t2)
            acc = jnp.where(pl.program_id(1) < rope[1] // tn, roped, acc)
        o_ref[...] = acc.astype(out_dtype)

    in_specs = [pl.BlockSpec((tm, K), lambda i, j: (i, 0)), pl.BlockSpec((K, tn), lambda i, j: (0, cb + j))]
    args = [h, wfull]
    if rope is not None:
        in_specs.append(pl.BlockSpec((3, tm, LANE), lambda i, j: (0, i, 0)))
        args.append(rope[0])
    return pl.pallas_call(
        body, name=name, grid=(S // tm, ncols // tn),
        in_specs=in_specs, out_specs=pl.BlockSpec((tm, tn), lambda i, j: (i, j)),
        out_shape=jax.ShapeDtypeStruct((S, ncols), out_dtype),
        compiler_params=_cparams(),
    )(*args)


def _out_proj_norm(y, wo, xres, g, name):
    S, W = y.shape
    D = wo.shape[1]
    tm = _row_tile(S, 512)

    def body(a_ref, b_ref, r_ref, g_ref, x_ref, h_ref):
        xv = r_ref[...] + _dot_nn(a_ref[...], b_ref[...])
        x_ref[...] = xv
        r = lax.rsqrt(jnp.mean(xv * xv, axis=-1, keepdims=True) + RMS_EPS)
        h_ref[...] = ((xv * r) * g_ref[...]).astype(BF16)

    row = pl.BlockSpec((tm, D), lambda i: (i, 0))
    return pl.pallas_call(
        body, name=name, grid=(S // tm,),
        in_specs=[pl.BlockSpec((tm, W), lambda i: (i, 0)), pl.BlockSpec((W, D), lambda i: (0, 0)), row,
                  pl.BlockSpec((1, D), lambda i: (0, 0))],
        out_specs=[row, row],
        out_shape=[jax.ShapeDtypeStruct((S, D), F32), jax.ShapeDtypeStruct((S, D), BF16)],
        compiler_params=_cparams(),
    )(y, wo, xres, g)


def _out_proj_loss(y, wo, xres, tgt, g, name):
    S, W = y.shape
    D = wo.shape[1]
    tm = _row_tile(S, 512)

    def body(a_ref, b_ref, r_ref, t_ref, g_ref, dx_ref, dxb_ref, dg_ref, loss_ref):
        xv = r_ref[...] + _dot_nn(a_ref[...], b_ref[...])
        r = lax.rsqrt(jnp.mean(xv * xv, axis=-1, keepdims=True) + RMS_EPS)
        xhat = xv * r
        gv = g_ref[...]
        err = xhat * gv - t_ref[...]
        d = err * (1.0 / D)
        gd = d * gv
        dx = r * (gd - xhat * jnp.mean(gd * xhat, axis=-1, keepdims=True))
        dx_ref[...] = dx
        dxb_ref[...] = dx.astype(BF16)

        @pl.when(pl.program_id(0) == 0)
        def _():
            dg_ref[...] = jnp.zeros(dg_ref.shape, F32)
            loss_ref[...] = jnp.zeros(loss_ref.shape, F32)
        dg_ref[...] += jnp.sum(d * xhat, axis=0, keepdims=True)
        per_tok = jnp.sum(err * err, axis=-1, keepdims=True) * (1.0 / D)
        loss_ref[...] += 0.5 * jnp.sum(per_tok, axis=0, keepdims=True)

    row = pl.BlockSpec((tm, D), lambda i: (i, 0))
    vec = pl.BlockSpec((1, D), lambda i: (0, 0))
    return pl.pallas_call(
        body, name=name, grid=(S // tm,),
        in_specs=[pl.BlockSpec((tm, W), lambda i: (i, 0)), pl.BlockSpec((W, D), lambda i: (0, 0)), row, row, vec],
        out_specs=[row, row, vec, pl.BlockSpec((1, LANE), lambda i: (0, 0))],
        out_shape=[jax.ShapeDtypeStruct((S, D), F32), jax.ShapeDtypeStruct((S, D), BF16),
                   jax.ShapeDtypeStruct((1, D), F32), jax.ShapeDtypeStruct((1, LANE), F32)],
        compiler_params=_cparams(),
    )(y, wo, xres, tgt, g)


def _matmul_nt(parts, wfull, out_rows, name):
    S = parts[0][0].shape[-2]
    tm, tn = _row_tile(S, MM_TILE), _tile(out_rows, MM_TILE)
    plan, lo = [], 0
    for arr, lead, col0 in parts:
        n_p = arr.shape[-1]
        tk = math.gcd(_tile(n_p, 1024), col0) if col0 else _tile(n_p, 1024)
        steps = n_p // tk
        plan.append((lead, col0 // tk, tk, lo, lo + steps))
        lo += steps
    nk = lo
    npart = len(parts)

    def body(*refs):
        a_refs, w_refs = refs[:npart], refs[npart:2 * npart]
        o_ref, acc_ref = refs[2 * npart], refs[2 * npart + 1]
        k = pl.program_id(2)

        @pl.when(k == 0)
        def _():
            acc_ref[...] = jnp.zeros(acc_ref.shape, F32)
        for p, (_, _, _, lo_p, hi_p) in enumerate(plan):
            @pl.when((k >= lo_p) & (k < hi_p))
            def _(p=p):
                acc_ref[...] += _dot_nt(a_refs[p][...], w_refs[p][...])

        @pl.when(k == nk - 1)
        def _():
            o_ref[...] = acc_ref[...]

    in_specs, args = [], []
    for (arr, lead, col0), (_, cb, tk, lo_p, hi_p) in zip(parts, plan):
        def kk(k, lo_p=lo_p, hi_p=hi_p):
            return jnp.clip(k - lo_p, 0, hi_p - lo_p - 1)
        if lead is None:
            in_specs.append(pl.BlockSpec((tm, tk), lambda i, j, k, kk=kk: (i, kk(k))))
        else:
            in_specs.append(pl.BlockSpec((None, tm, tk), lambda i, j, k, kk=kk, lead=lead: (lead, i, kk(k))))
        args.append(arr)
    for (_, cb, tk, lo_p, hi_p) in plan:
        def kk(k, lo_p=lo_p, hi_p=hi_p):
            return jnp.clip(k - lo_p, 0, hi_p - lo_p - 1)
        in_specs.append(pl.BlockSpec((tn, tk), lambda i, j, k, kk=kk, cb=cb: (j, cb + kk(k))))
        args.append(wfull)
    return pl.pallas_call(
        body, name=name, grid=(S // tm, out_rows // tn, nk),
        in_specs=in_specs, out_specs=pl.BlockSpec((tm, tn), lambda i, j, k: (i, j)),
        out_shape=jax.ShapeDtypeStruct((S, out_rows), F32),
        scratch_shapes=[pltpu.VMEM((tm, tn), F32)],
        compiler_params=_cparams(),
    )(*args)


def _matmul_tn(a, parts, total, name, tile_major=False, also_bf16=False):
    S, M = a.shape
    tm, ts = _tile(M, MM_TILE), _row_tile(S, MM_TILE)
    nout = 2 if also_bf16 else 1
    outs = None
    for idx, (arr, lead, col0) in enumerate(parts):
        n_p = arr.shape[-1]
        tn = math.gcd(_tile(n_p, MM_TILE), col0) if col0 else _tile(n_p, MM_TILE)
        cb = col0 // tn
        nk = S // ts

        def body(*refs, nk=nk, tn=tn):
            a_ref, b_ref = refs[0], refs[1]
            o_refs, acc_ref = refs[-1 - nout:-1], refs[-1]
            k = pl.program_id(2)

            @pl.when(k == 0)
            def _():
                acc_ref[...] = jnp.zeros(acc_ref.shape, F32)
            acc_ref[...] += _dot_tn(a_ref[...], b_ref[...])

            @pl.when(k == nk - 1)
            def _():
                for o_ref in o_refs:
                    if tile_major:
                        for t in range(tn // LANE):
                            o_ref[t] = acc_ref[:, LANE * t:LANE * (t + 1)].astype(o_ref.dtype)
                    else:
                        o_ref[...] = acc_ref[...].astype(o_ref.dtype)

        in_specs = [pl.BlockSpec((ts, tm), lambda i, j, k: (k, i))]
        if lead is None:
            in_specs.append(pl.BlockSpec((ts, tn), lambda i, j, k: (k, j)))
        else:
            in_specs.append(pl.BlockSpec((None, ts, tn), lambda i, j, k, lead=lead: (lead, k, j)))
        args = [a, arr]
        aliases = {}
        if outs is not None:
            in_specs += [pl.BlockSpec(memory_space=pl.ANY)] * nout
            args += list(outs)
            aliases = {2 + o: o for o in range(nout)}
        if tile_major:
            out_spec = pl.BlockSpec((tn // LANE, tm, LANE), lambda i, j, k, cb=cb: (cb + j, i, 0))
            shape = (total // LANE, M, LANE)
        else:
            out_spec = pl.BlockSpec((tm, tn), lambda i, j, k, cb=cb: (i, cb + j))
            shape = (M, total)
        outs = pl.pallas_call(
            body, name=f"{name}_{idx}", grid=(M // tm, n_p // tn, nk),
            in_specs=in_specs, out_specs=[out_spec] * nout,
            out_shape=[jax.ShapeDtypeStruct(shape, dt) for dt in (F32, BF16)[:nout]],
            scratch_shapes=[pltpu.VMEM((tm, tn), F32)],
            input_output_aliases=aliases,
            compiler_params=_cparams(),
        )(*args)
    return tuple(outs) if also_bf16 else outs[0]


def _log_sigmoid(z):
    e = jnp.exp(-jnp.abs(z))
    return jnp.minimum(z, 0.0) - jnp.where(e < 1e-4, e * (1.0 - 0.5 * e), jnp.log(1.0 + e))


def _fox_gate_fwd(fl, bias):
    S = fl.shape[0]

    def body(f_ref, b_ref, c_ref):
        row = lax.broadcasted_iota(jnp.int32, (8, LANE), 0)

        def step(i, carry):
            r0 = pl.multiple_of(i * 8, 8)
            t = _log_sigmoid(f_ref[pl.ds(r0, 8), :] + b_ref[...])
            for sh in (1, 2, 4):
                t = t + jnp.where(row >= sh, pltpu.roll(t, sh, axis=0), 0.0)
            t = t + carry
            c_ref[pl.ds(r0, 8), :] = t
            return jnp.sum(jnp.where(row == 7, t, 0.0), axis=0, keepdims=True)

        lax.fori_loop(0, S // 8, step, jnp.zeros((1, LANE), F32))

    vm = pl.BlockSpec(memory_space=pltpu.VMEM)
    return pl.pallas_call(
        body, name="fox_gate_fwd", in_specs=[vm, vm], out_specs=vm,
        out_shape=jax.ShapeDtypeStruct((S, LANE), F32),
        compiler_params=_cparams(),
    )(fl, bias)


def _fox_gate_bwd(fl, bias, dc):
    S = fl.shape[0]

    def body(f_ref, b_ref, d_ref, o_ref, db_ref, acc_ref):
        row = lax.broadcasted_iota(jnp.int32, (8, LANE), 0)
        nt = S // 8

        def step(ii, carry):
            carry_c, carry_b = carry
            r0 = pl.multiple_of((nt - 1 - ii) * 8, 8)
            t = d_ref[pl.ds(r0, 8), :]
            for sh in (1, 2, 4):
                t = t + jnp.where(row < 8 - sh, pltpu.roll(t, 8 - sh, axis=0), 0.0)
            t = t + carry_c
            z = f_ref[pl.ds(r0, 8), :] + b_ref[...]
            dz = t * _sigmoid(-z)
            acc_ref[pl.ds(r0, 8), :] = dz
            first = jnp.sum(jnp.where(row == 0, t, 0.0), axis=0, keepdims=True)
            return first, carry_b + jnp.sum(dz, axis=0, keepdims=True)

        zero = jnp.zeros((1, LANE), F32)
        _, db = lax.fori_loop(0, nt, step, (zero, zero))
        db_ref[...] = db
        o_ref[...] = acc_ref[...].astype(BF16)

    vm = pl.BlockSpec(memory_space=pltpu.VMEM)
    return pl.pallas_call(
        body, name="fox_gate_bwd", in_specs=[vm, vm, vm], out_specs=[vm, vm],
        out_shape=[jax.ShapeDtypeStruct((S, LANE), BF16), jax.ShapeDtypeStruct((1, LANE), F32)],
        scratch_shapes=[pltpu.VMEM((S, LANE), F32)],
        compiler_params=_cparams(),
    )(fl, bias, dc)


def _bias_lanes(col, lane, e, first):
    o0 = HEAD_DIM * (1 - e)
    hi = col.astype(BF16)
    r1 = col - hi.astype(F32)
    mid = r1.astype(BF16)
    lo = (r1 - mid.astype(F32)).astype(BF16)
    d0 = o0 if first else o0 + 3
    t = jnp.where((lane >= o0) & (lane < o0 + 6), jnp.ones(lane.shape, BF16), jnp.zeros(lane.shape, BF16))
    t = jnp.where(lane == d0, hi, t)
    t = jnp.where(lane == d0 + 1, mid, t)
    return jnp.where(lane == d0 + 2, lo, t)


def _fox_fwd(qkvg, c, H, gather=()):
    na = len(gather)
    S = qkvg.shape[0]
    W = H * HEAD_DIM
    HP = H // 2
    PP = 2 if HP % 2 == 0 else 1
    NE = 2 * PP
    tq = _row_tile(S, 512)
    nq = S // tq
    wb = W // LANE
    scale = HEAD_DIM ** -0.5

    def body(*refs):
        q_ref, k_ref, v_ref, g_ref, c_ref = refs[:5]
        y_ref, o_ref, a_ref = refs[5 + na:8 + na]
        kaug_sc, vaug_sc, qaug_sc, s_sc, mb_sc, m_sc, acc_sc = refs[8 + 2 * na:15 + 2 * na]
        hp, qi = pl.program_id(0), pl.program_id(1)
        if na:
            remote, local = _direct_gather_copies(refs[5:5 + na], refs[8 + na:8 + 2 * na], *refs[15 + 2 * na:])

            @pl.when((hp == 0) & (qi == 0))
            def _():
                for cp in remote + local:
                    cp.start()
        lane = lax.broadcasted_iota(jnp.int32, (tq, LANE), 1)
        own = [lane < HEAD_DIM, lane >= HEAD_DIM]
        rows = lax.broadcasted_iota(jnp.int32, (tq, tq), 0)
        cols = lax.broadcasted_iota(jnp.int32, (tq, tq), 1)

        def bias_lanes(col, e, first):
            return _bias_lanes(col, lane, e % 2, first)

        def head_col(tile, e):
            return jnp.sum(jnp.where(lane == 2 * PP * hp + e, tile, 0.0), axis=1, keepdims=True)

        def tile_of(e):
            return slice(LANE * (e // 2), LANE * (e // 2 + 1))

        @pl.when(qi == 0)
        def _():
            def chunk(i, carry):
                r0 = pl.multiple_of(i * tq, tq)
                cb = c_ref[pl.ds(r0, tq), :]
                for e in range(NE):
                    kb, vb = k_ref[pl.ds(r0, tq), tile_of(e)], v_ref[pl.ds(r0, tq), tile_of(e)]
                    kaug_sc[e, pl.ds(r0, tq), :] = jnp.where(own[e % 2], kb, bias_lanes(-head_col(cb, e), e, False))
                    vaug_sc[e, pl.ds(r0, tq), :] = jnp.where(own[e % 2], vb, jnp.ones((tq, LANE), BF16))
                return carry
            lax.fori_loop(0, nq, chunk, 0)

        crow = c_ref[pl.ds(pl.multiple_of(qi * tq, tq), tq), :]
        ctq = [head_col(crow, e) for e in range(NE)]
        for e in range(NE):
            q = q_ref[:, tile_of(e)] * jnp.asarray(scale, BF16)
            qaug_sc[e] = jnp.where(own[e % 2], q, bias_lanes(ctq[e], e, True))
        m_sc[...] = jnp.full(m_sc.shape, NEG_INF, F32)
        acc_sc[...] = jnp.zeros(acc_sc.shape, F32)

        def scores(blk, slot, masked):
            k0 = pl.multiple_of(blk * tq, tq)
            for e in range(NE):
                s = _dot_nt(qaug_sc[e], kaug_sc[e, pl.ds(k0, tq), :])
                if masked:
                    s = jnp.where(rows >= cols, s, NEG_INF)
                s_sc[slot, e] = s
                mb_sc[slot, e] = jnp.broadcast_to(jnp.max(s, axis=1, keepdims=True), (tq, LANE))

        def accumulate(blk, slot):
            k0 = pl.multiple_of(blk * tq, tq)
            for e in range(NE):
                m_prev = m_sc[e]
                m_new = jnp.maximum(m_prev, mb_sc[slot, e])
                p = jnp.exp(s_sc[slot, e] - jnp.tile(m_new, (1, tq // LANE)))
                acc_sc[e] = jnp.exp(m_prev - m_new) * acc_sc[e] + _dot_nn(p.astype(BF16), vaug_sc[e, pl.ds(k0, tq), :])
                m_sc[e] = m_new

        def block_of(t):
            return jnp.where(t == 0, qi, t - 1)

        scores(qi, 0, True)

        def loop_body(t, carry):
            scores(t, (t + 1) % 2, False)
            accumulate(block_of(t), t % 2)
            return carry

        lax.fori_loop(0, qi, loop_body, 0)
        accumulate(block_of(qi), qi % 2)
        o_e, a_e = [], []
        for e in range(NE):
            acc = acc_sc[e]
            l = pltpu.roll(acc, HEAD_DIM, axis=1)
            o_e.append(acc / l)
            a_e.append(ctq[e] - (m_sc[e] + jnp.log(l)))
        for pp in range(PP):
            o = jnp.where(own[0], o_e[2 * pp], o_e[2 * pp + 1])
            g = g_ref[:, tile_of(2 * pp)].astype(F32)
            y_ref[:, tile_of(2 * pp)] = (o * (g * _sigmoid(g))).astype(BF16)
            o_ref[:, tile_of(2 * pp)] = o.astype(BF16)
            a_ref[pp] = jnp.where(own[0], a_e[2 * pp], a_e[2 * pp + 1])
        if na:
            @pl.when((hp == HP // PP - 1) & (qi == nq - 1))
            def _():
                _wait_all(remote, local)

    any_spec = pl.BlockSpec(memory_space=pl.ANY)
    sems = [pltpu.SemaphoreType.DMA((na, N_DEV - 1)), pltpu.SemaphoreType.DMA((na, N_DEV - 1)),
            pltpu.SemaphoreType.DMA((na,))] if na else []
    wide = PP * LANE
    outs = pl.pallas_call(
        body, name="fox_attn_fwd", grid=(HP // PP, nq),
        in_specs=[pl.BlockSpec((tq, wide), lambda h, i: (i, h)),
                  pl.BlockSpec((S, wide), lambda h, i: (0, wb // PP + h)),
                  pl.BlockSpec((S, wide), lambda h, i: (0, 2 * wb // PP + h)),
                  pl.BlockSpec((tq, wide), lambda h, i: (i, 3 * wb // PP + h)),
                  pl.BlockSpec((S, LANE), lambda h, i: (0, 0))] + [any_spec] * na,
        out_specs=[pl.BlockSpec((tq, wide), lambda h, i: (i, h)),
                   pl.BlockSpec((tq, wide), lambda h, i: (i, h)),
                   pl.BlockSpec((PP, tq, LANE), lambda h, i: (h, i, 0))] + [any_spec] * na,
        out_shape=[jax.ShapeDtypeStruct((S, W), BF16), jax.ShapeDtypeStruct((S, W), BF16),
                   jax.ShapeDtypeStruct((HP, S, LANE), F32)]
        + [jax.ShapeDtypeStruct((N_DEV,) + g.shape, g.dtype) for g in gather],
        scratch_shapes=[pltpu.VMEM((NE, S, LANE), BF16), pltpu.VMEM((NE, S, LANE), BF16),
                        pltpu.VMEM((NE, tq, LANE), BF16), pltpu.VMEM((2, NE, tq, tq), F32),
                        pltpu.VMEM((2, NE, tq, LANE), F32), pltpu.VMEM((NE, tq, LANE), F32),
                        pltpu.VMEM((NE, tq, LANE), F32)] + sems,
        compiler_params=_cparams(),
    )(qkvg, qkvg, qkvg, qkvg, c, *gather)
    return outs[0], outs[1], outs[2], list(outs[3:])


def _fox_out_bwd(dxb, wo, qkvg, o, a, H):
    S, D = dxb.shape
    W = H * HEAD_DIM
    tm, tn = _row_tile(S, 512), _tile(W, 512)
    npair = tn // LANE
    scale = HEAD_DIM ** -0.5

    def body(dx_ref, w_ref, q_ref, g_ref, o_ref, a_ref, qa_ref, da_ref, dg_ref):
        dy = _dot_nt(dx_ref[...], w_ref[...])
        lane = lax.broadcasted_iota(jnp.int32, (tm, LANE), 1)
        own = [lane < HEAD_DIM, lane >= HEAD_DIM]
        for p in range(npair):
            cols = slice(LANE * p, LANE * (p + 1))
            q = q_ref[:, cols] * jnp.asarray(scale, BF16)
            dyv, g, ov, at = dy[:, cols], g_ref[:, cols].astype(F32), o_ref[:, cols].astype(F32), a_ref[p]
            sg = _sigmoid(g)
            dob = (dyv * (g * sg)).astype(BF16)
            dg_ref[:, cols] = (dyv * ov * (sg * (1.0 + g * (1.0 - sg)))).astype(BF16)
            prod = dob.astype(F32) * ov
            for e in range(2):
                a_col = jnp.max(jnp.where(own[e], at, -jnp.inf), axis=1, keepdims=True)
                d_col = jnp.sum(jnp.where(own[e], prod, 0.0), axis=1, keepdims=True)
                qa_ref[e, :, cols] = jnp.where(own[e], q, _bias_lanes(a_col, lane, e, True))
                da_ref[e, :, cols] = jnp.where(own[e], dob, _bias_lanes(-d_col, lane, e, True))

    blk = pl.BlockSpec((tm, tn), lambda i, j: (i, j))
    pair = pl.BlockSpec((2, tm, tn), lambda i, j: (0, i, j))
    return pl.pallas_call(
        body, name="fox_out_bwd", grid=(S // tm, W // tn),
        in_specs=[pl.BlockSpec((tm, D), lambda i, j: (i, 0)), pl.BlockSpec((tn, D), lambda i, j: (j, 0)),
                  blk, pl.BlockSpec((tm, tn), lambda i, j: (i, 3 * W // tn + j)), blk,
                  pl.BlockSpec((npair, tm, LANE), lambda i, j: (j, i, 0))],
        out_specs=[pair, pair, blk],
        out_shape=[jax.ShapeDtypeStruct((2, S, W), BF16), jax.ShapeDtypeStruct((2, S, W), BF16),
                   jax.ShapeDtypeStruct((S, W), BF16)],
        compiler_params=_cparams(),
    )(dxb, wo, qkvg, qkvg, o, a)


def _fox_bwd(qaug, doaug, qkv, c, H, scatter=(), scatter_specs=()):
    na = len(scatter)
    S = qkv.shape[0]
    W = H * HEAD_DIM
    HP = H // 2
    tq = _row_tile(S, 512)
    nq = S // tq
    wb = W // LANE
    scale = HEAD_DIM ** -0.5

    def body(*refs):
        qa_ref, da_ref, k_ref, v_ref, c_ref = refs[:5]
        out_ref, dcr_ref, dcc_ref = refs[5 + na:8 + na]
        dq_sc, dk_sc, dv_sc = refs[8 + 2 * na:11 + 2 * na]
        hp, kj = pl.program_id(0), pl.program_id(1)
        if na:
            remote = _direct_scatter_copies(refs[5:5 + na], refs[8 + na:8 + 2 * na], scatter_specs,
                                            *refs[11 + 2 * na:])

            @pl.when((hp == 0) & (kj == 0))
            def _():
                for cp in remote:
                    cp.start()
        lane = lax.broadcasted_iota(jnp.int32, (tq, LANE), 1)
        own = [lane < HEAD_DIM, lane >= HEAD_DIM]
        rows = lax.broadcasted_iota(jnp.int32, (tq, tq), 0)
        cols = lax.broadcasted_iota(jnp.int32, (tq, tq), 1)

        @pl.when(kj == 0)
        def _():
            dq_sc[...] = jnp.zeros(dq_sc.shape, F32)

        @pl.when((kj == 0) & (hp == 0))
        def _():
            dcr_ref[...] = jnp.zeros(dcr_ref.shape, F32)
            dcc_ref[...] = jnp.zeros(dcc_ref.shape, F32)

        kblk, vblk, cblk = k_ref[...], v_ref[...], c_ref[...]
        one, zero = jnp.ones((tq, LANE), BF16), jnp.zeros((tq, LANE), BF16)
        ka, va = [], []
        for e in range(2):
            o0 = HEAD_DIM * (1 - e)
            c_col = jnp.sum(jnp.where(lane == 2 * hp + e, cblk, 0.0), axis=1, keepdims=True)
            ka.append(jnp.where(own[e], kblk, _bias_lanes(-c_col, lane, e, False)))
            va.append(jnp.where(own[e], vblk, jnp.where((lane >= o0) & (lane < o0 + 3), one, zero)))
        dk_sc[...] = jnp.zeros(dk_sc.shape, F32)
        dv_sc[...] = jnp.zeros(dv_sc.shape, F32)

        def step(i, masked):
            r0 = pl.multiple_of(i * tq, tq)
            for e in range(2):
                qa = qa_ref[e, pl.ds(r0, tq), :]
                da = da_ref[e, pl.ds(r0, tq), :]
                p = jnp.exp(_dot_nt(qa, ka[e]))
                if masked:
                    p = jnp.where(rows >= cols, p, 0.0)
                ds = p * _dot_nt(da, va[e])
                pb, dsb = p.astype(BF16), ds.astype(BF16)
                dv_sc[e] += _dot_tn(pb, da)
                dk_sc[e] += _dot_tn(dsb, qa)
                dq_sc[e, pl.ds(r0, tq), :] += _dot_nn(dsb, ka[e])

        step(kj, True)

        def loop_body(i, carry):
            step(i, False)
            return carry

        lax.fori_loop(kj + 1, nq, loop_body, 0)
        k0 = pl.multiple_of(kj * tq, tq)
        out_ref[1, pl.ds(k0, tq), :] = jnp.where(own[0], dk_sc[0], dk_sc[1]).astype(BF16)
        out_ref[2, pl.ds(k0, tq), :] = jnp.where(own[0], dv_sc[0], dv_sc[1]).astype(BF16)

        def put_lane(ref, r0, e, tile, src_lane):
            col = jnp.sum(jnp.where(lane == src_lane, tile, 0.0), axis=1, keepdims=True)
            ref[pl.ds(r0, tq), :] = jnp.where(lane == 2 * hp + e, col, ref[pl.ds(r0, tq), :])

        for e in range(2):
            put_lane(dcc_ref, k0, e, dk_sc[e], HEAD_DIM * (1 - e) + 3)

        @pl.when(kj == nq - 1)
        def _():
            def chunk(i, carry):
                r0 = pl.multiple_of(i * tq, tq)
                d0, d1 = dq_sc[0, pl.ds(r0, tq), :], dq_sc[1, pl.ds(r0, tq), :]
                out_ref[0, pl.ds(r0, tq), :] = (jnp.where(own[0], d0, d1) * scale).astype(BF16)
                put_lane(dcr_ref, r0, 0, d0, HEAD_DIM)
                put_lane(dcr_ref, r0, 1, d1, 0)
                return carry
            lax.fori_loop(0, nq, chunk, 0)

        if na:
            @pl.when((hp == HP - 1) & (kj == nq - 1))
            def _():
                _wait_all(remote)

    pair = pl.BlockSpec((2, S, LANE), lambda h, j: (0, 0, h))
    vec = pl.BlockSpec((S, LANE), lambda h, j: (0, 0))
    any_spec = pl.BlockSpec(memory_space=pl.ANY)
    sems = [pltpu.SemaphoreType.DMA((na, N_DEV - 1)), pltpu.SemaphoreType.DMA((na, N_DEV - 1))] if na else []
    outs = pl.pallas_call(
        body, name="fox_attn_bwd", grid=(HP, nq),
        in_specs=[pair, pair,
                  pl.BlockSpec((tq, LANE), lambda h, j: (j, wb + h)),
                  pl.BlockSpec((tq, LANE), lambda h, j: (j, 2 * wb + h)),
                  pl.BlockSpec((tq, LANE), lambda h, j: (j, 0))] + [any_spec] * na,
        out_specs=[pl.BlockSpec((3, S, LANE), lambda h, j: (0, 0, h)), vec, vec] + [any_spec] * na,
        out_shape=[jax.ShapeDtypeStruct((3, S, W), BF16), jax.ShapeDtypeStruct((S, LANE), F32),
                   jax.ShapeDtypeStruct((S, LANE), F32)]
        + [jax.ShapeDtypeStruct((N_DEV - 1,) + _scatter_block_shape(g, s), g.dtype)
           for g, s in zip(scatter, scatter_specs)],
        scratch_shapes=[pltpu.VMEM((2, S, LANE), F32), pltpu.VMEM((2, tq, LANE), F32),
                        pltpu.VMEM((2, tq, LANE), F32)] + sems,
        compiler_params=_cparams(),
    )(qaug, doaug, qkv, qkv, c, *scatter)
    return outs[0], outs[1], outs[2], list(outs[3:])


def _swa_pick(blk, half, lane):
    b = blk.astype(F32)
    r = pltpu.roll(b, HEAD_DIM, axis=1)
    return jnp.where(jnp.logical_xor(lane < HEAD_DIM, half == 1), b, r).astype(BF16)


def _swa_stack(t, lane, G):
    pieces = []
    z = jnp.zeros((SWA_BLOCK, LANE), t.dtype)
    for j in range(G // 2):
        tile = t[:, LANE * j:LANE * (j + 1)]
        pieces += [jnp.where(lane < HEAD_DIM, tile, z), jnp.where(lane < HEAD_DIM, z, tile)]
    return jnp.concatenate(pieces, axis=0)


def _swa_unstack(st, lane, G):
    tiles = []
    for j in range(G // 2):
        a = st[2 * j * SWA_BLOCK:(2 * j + 1) * SWA_BLOCK]
        b = st[(2 * j + 1) * SWA_BLOCK:(2 * j + 2) * SWA_BLOCK]
        tiles.append(jnp.where(lane < HEAD_DIM, a, b))
    return jnp.concatenate(tiles, axis=1)


def _swa_mask_bias(G):
    R = G * SWA_BLOCK
    t_loc = jnp.arange(R)[:, None] % SWA_BLOCK
    j_loc = jnp.arange(2 * SWA_BLOCK)[None, :]
    diff = t_loc + SWA_BLOCK - j_loc
    band = (diff >= 0) & (diff < SWA_BLOCK)
    return jnp.stack([jnp.where(band & (j_loc >= SWA_BLOCK), 0.0, NEG_INF),
                      jnp.where(band, 0.0, NEG_INF)]).astype(F32)


def _swa_scores(q, kp, kc, vp, vc, srow, bias, half, head0, G):
    lane = lax.broadcasted_iota(jnp.int32, (SWA_BLOCK, LANE), 1)
    kk = jnp.concatenate([_swa_pick(kp, half, lane), _swa_pick(kc, half, lane)], axis=0)
    vv = jnp.concatenate([_swa_pick(vp, half, lane), _swa_pick(vc, half, lane)], axis=0)
    qstack = _swa_stack(q, lane, G) * jnp.asarray(HEAD_DIM ** -0.5, BF16)
    s = _dot_nt(qstack, kk) + bias
    R = G * SWA_BLOCK
    lane1 = lax.broadcasted_iota(jnp.int32, (1, LANE), 1)
    sink = jnp.concatenate(
        [jnp.broadcast_to(jnp.sum(jnp.where(lane1 == head0 + g, srow, 0.0), axis=1, keepdims=True), (SWA_BLOCK, LANE))
         for g in range(G)], axis=0)
    m = jnp.maximum(jnp.broadcast_to(jnp.max(s, axis=1, keepdims=True), (R, LANE)), sink)
    e = jnp.exp(s - jnp.tile(m, (1, 2)))
    es = jnp.exp(sink - m)
    inv = 1.0 / (jnp.broadcast_to(jnp.sum(e, axis=1, keepdims=True), (R, LANE)) + es)
    return qstack, kk, vv, e * jnp.tile(inv, (1, 2)), es * inv, lane


def _swa_fwd(qkv, gate, sinks, mask_bias, HQ, HKV):
    S = qkv.shape[0]
    G = HQ // HKV
    WQ, KVW = HQ * HEAD_DIM, HKV * HEAD_DIM
    nb = S // SWA_BLOCK
    GW = G * HEAD_DIM
    kb, vb = WQ // LANE, (WQ + KVW) // LANE

    def body(q_ref, kp_ref, kc_ref, vp_ref, vc_ref, g_ref, sink_ref, b_ref, y_ref, o_ref):
        pair = pl.program_id(0)
        for half in range(2):
            cols = slice(GW * half, GW * (half + 1))
            _, _, vv, p, _, lane = _swa_scores(q_ref[:, cols], kp_ref[...], kc_ref[...], vp_ref[...], vc_ref[...],
                                               sink_ref[...], b_ref[0], half, (2 * pair + half) * G, G)
            o = _swa_unstack(_dot_nn(p.astype(BF16), vv), lane, G)
            g = g_ref[:, cols].astype(F32)
            y_ref[:, cols] = (o * (g * _sigmoid(g))).astype(BF16)
            o_ref[:, cols] = o.astype(BF16)

    blk = lambda cb, prev: pl.BlockSpec(
        (SWA_BLOCK, LANE), lambda h, n, cb=cb, prev=prev: (jnp.maximum(n - prev, 0), cb + h))
    qspec = pl.BlockSpec((SWA_BLOCK, 2 * GW), lambda h, n: (n, h))
    return pl.pallas_call(
        body, name="swa_attn_fwd", grid=(HKV // 2, nb),
        in_specs=[qspec, blk(kb, 1), blk(kb, 0), blk(vb, 1), blk(vb, 0), qspec,
                  pl.BlockSpec((1, LANE), lambda h, n: (0, 0)),
                  pl.BlockSpec((1, G * SWA_BLOCK, 2 * SWA_BLOCK), lambda h, n: (jnp.minimum(n, 1), 0, 0))],
        out_specs=[qspec, qspec],
        out_shape=[jax.ShapeDtypeStruct((S, WQ), BF16), jax.ShapeDtypeStruct((S, WQ), BF16)],
        compiler_params=_cparams(),
    )(qkv, qkv, qkv, qkv, qkv, gate, sinks, mask_bias)


def _swa_bwd(qkv, dy, gate, o, sinks, tables, mask_bias, HQ, HKV):
    S = qkv.shape[0]
    G = HQ // HKV
    WQ, KVW = HQ * HEAD_DIM, HKV * HEAD_DIM
    nb = S // SWA_BLOCK
    GW = G * HEAD_DIM
    R = G * SWA_BLOCK
    kb, vb = WQ // LANE, (WQ + KVW) // LANE
    scale = HEAD_DIM ** -0.5
    assert G == 8

    def body(q_ref, kp_ref, kc_ref, vp_ref, vc_ref, dy_ref, g_ref, o_ref, sink_ref, t_ref, b_ref,
             dqg_ref, dkv_ref, dsink_ref, carry_sc):
        pair, n = pl.program_id(0), pl.program_id(1)

        @pl.when(n == 0)
        def _():
            carry_sc[...] = jnp.zeros(carry_sc.shape, F32)
            dsink_ref[...] = jnp.zeros(dsink_ref.shape, F32)

        @pl.when(n < nb)
        def _():
            t0, t1, t2 = (jnp.tile(t_ref[i], (1, GW // LANE)) for i in range(3))
            for half in range(2):
                cols = slice(GW * half, GW * (half + 1))
                qstack, kk, vv, p, psink, lane = _swa_scores(
                    q_ref[:, cols], kp_ref[...], kc_ref[...], vp_ref[...], vc_ref[...], sink_ref[...], b_ref[0],
                    half, (2 * pair + half) * G, G)
                dyv, g, ov = dy_ref[:, cols], g_ref[:, cols].astype(F32), o_ref[:, cols].astype(F32)
                sg = _sigmoid(g)
                dob = (dyv * (g * sg)).astype(BF16)
                dqg_ref[1, :, cols] = (dyv * ov * (sg * (1.0 + g * (1.0 - sg)))).astype(BF16)
                prod = dob.astype(F32) * ov
                dparts = []
                for j in range(G // 2):
                    tile = prod[:, LANE * j:LANE * (j + 1)]
                    for sel in (jnp.where(lane < HEAD_DIM, tile, 0.0), jnp.where(lane < HEAD_DIM, 0.0, tile)):
                        dparts.append(jnp.broadcast_to(jnp.sum(sel, axis=1, keepdims=True), (SWA_BLOCK, LANE)))
                delta = jnp.concatenate(dparts, axis=0)
                dostack = _swa_stack(dob, lane, G)
                ds = p * (_dot_nt(dostack, vv) - jnp.tile(delta, (1, 2)))
                dsb, pb = ds.astype(BF16), p.astype(BF16)
                dq = _swa_unstack(_dot_nn(dsb, kk), lane, G) * scale
                dq = dq * t0 + pltpu.roll(dq * t1, ROT_DIM // 2, axis=1) + pltpu.roll(dq * t2, GW - ROT_DIM // 2, axis=1)
                dqg_ref[0, :, cols] = dq.astype(BF16)
                dkk = _dot_tn(dsb, qstack)
                dvv = _dot_tn(pb, dostack)
                dkk = dkk + pltpu.roll(dkk, HEAD_DIM, axis=1)
                dvv = dvv + pltpu.roll(dvv, HEAD_DIM, axis=1)
                lane2 = lax.broadcasted_iota(jnp.int32, (2 * SWA_BLOCK, LANE), 1)
                comb = jnp.where(lane2 < HEAD_DIM, dkk, dvv)
                dkv_ref[half] = carry_sc[half] + comb[:SWA_BLOCK]
                carry_sc[half] = comb[SWA_BLOCK:]
                sk = psink * delta
                rows = [-jnp.sum(sk[g_ * SWA_BLOCK:(g_ + 1) * SWA_BLOCK], axis=0, keepdims=True) for g_ in range(G)]
                dsink_ref[half] += jnp.concatenate(rows, axis=0)

        @pl.when(n == nb)
        def _():
            dkv_ref[...] = carry_sc[...]

    cl = lambda n: jnp.minimum(n, nb - 1)
    blk = lambda cb, prev: pl.BlockSpec(
        (SWA_BLOCK, LANE), lambda h, n, cb=cb, prev=prev: (jnp.maximum(cl(n) - prev, 0), cb + h))
    qspec = pl.BlockSpec((SWA_BLOCK, 2 * GW), lambda h, n: (cl(n), h))
    return pl.pallas_call(
        body, name="swa_attn_bwd", grid=(HKV // 2, nb + 1),
        in_specs=[qspec, blk(kb, 1), blk(kb, 0), blk(vb, 1), blk(vb, 0), qspec, qspec, qspec,
                  pl.BlockSpec((1, LANE), lambda h, n: (0, 0)),
                  pl.BlockSpec((3, SWA_BLOCK, LANE), lambda h, n: (0, cl(n), 0)),
                  pl.BlockSpec((1, R, 2 * SWA_BLOCK), lambda h, n: (jnp.minimum(n, 1), 0, 0))],
        out_specs=[pl.BlockSpec((2, SWA_BLOCK, 2 * GW), lambda h, n: (0, cl(n), h)),
                   pl.BlockSpec((2, SWA_BLOCK, LANE), lambda h, n: (h, jnp.maximum(n - 1, 0), 0)),
                   pl.BlockSpec((2, 8, LANE), lambda h, n: (h, 0, 0))],
        out_shape=[jax.ShapeDtypeStruct((2, S, WQ), BF16), jax.ShapeDtypeStruct((HKV, S, LANE), F32),
                   jax.ShapeDtypeStruct((HKV, 8, LANE), F32)],
        scratch_shapes=[pltpu.VMEM((2, SWA_BLOCK, LANE), F32)],
        compiler_params=_cparams(),
    )(qkv, qkv, qkv, qkv, qkv, dy, gate, o, sinks, tables, mask_bias)


def _swa_dkv_finish(dkv, tables):
    HKV, S, _ = dkv.shape
    KVW = HKV * HEAD_DIM
    tm = _row_tile(S, 512)
    npair = HKV // 2

    def body(d_ref, t_ref, o_ref):
        lane = lax.broadcasted_iota(jnp.int32, (tm, LANE), 1)
        lo = lane < HEAD_DIM
        for p in range(npair):
            a, b = d_ref[2 * p], d_ref[2 * p + 1]
            tk = jnp.where(lo, a, pltpu.roll(b, HEAD_DIM, axis=1))
            tv = jnp.where(lo, pltpu.roll(a, HEAD_DIM, axis=1), b)
            tk = (tk * t_ref[0] + pltpu.roll(tk * t_ref[1], ROT_DIM // 2, axis=1)
                  + pltpu.roll(tk * t_ref[2], LANE - ROT_DIM // 2, axis=1))
            o_ref[:, LANE * p:LANE * (p + 1)] = tk.astype(BF16)
            o_ref[:, KVW + LANE * p:KVW + LANE * (p + 1)] = tv.astype(BF16)

    return pl.pallas_call(
        body, name="swa_dkv_finish", grid=(S // tm,),
        in_specs=[pl.BlockSpec((HKV, tm, LANE), lambda i: (0, i, 0)), pl.BlockSpec((3, tm, LANE), lambda i: (0, i, 0))],
        out_specs=pl.BlockSpec((tm, 2 * KVW), lambda i: (i, 0)),
        out_shape=jax.ShapeDtypeStruct((S, 2 * KVW), BF16),
        compiler_params=_cparams(),
    )(dkv, tables)


def _rope_tables(S, width):
    half = ROT_DIM // 2
    pos = jnp.arange(S, dtype=F32)
    inv_freq = ROPE_THETA ** (-jnp.arange(half, dtype=F32) / half)
    ang = pos[:, None] * inv_freq[None, :]
    cos, sin = jnp.cos(ang), jnp.sin(ang)
    one = jnp.ones((S, HEAD_DIM - ROT_DIM), F32)
    zero = jnp.zeros((S, HEAD_DIM - ROT_DIM), F32)
    zh = jnp.zeros((S, half), F32)
    t0 = jnp.concatenate([cos, cos, one], axis=1)
    t1 = jnp.concatenate([-sin, zh, zero], axis=1)
    t2 = jnp.concatenate([zh, sin, zero], axis=1)
    return jnp.stack([jnp.tile(t, (1, width // HEAD_DIM)) for t in (t0, t1, t2)])


def _pad_rows(v, row, total_rows=8):
    return jnp.pad(v, ((row, total_rows - row - v.shape[0]), (0, 0)))


def _pad_lanes(v, off, width):
    return jnp.pad(v, ((0, 0), (off, width - off - v.shape[1])))


def kernel(x, norm_g, fox_w_in, fox_b_f, fox_w_out, swa_w_in, swa_sinks, swa_w_out, final_g, loss_target, m_norm_g, m_fox_w_in, m_fox_b_f, m_fox_w_out, m_swa_w_in, m_swa_sinks, m_swa_w_out, m_final_g, v_norm_g, v_fox_w_in, v_fox_b_f, v_fox_w_out, v_swa_w_in, v_swa_sinks, v_swa_w_out, v_final_g):
    S, D = x.shape[1], x.shape[2]
    H = fox_b_f.shape[1]
    W = H * HEAD_DIM
    wf = fox_w_in.shape[2]
    ws = swa_w_in.shape[2]
    HQ = swa_sinks.shape[1]
    WQ = HQ * HEAD_DIM
    KVW = (ws * N_DEV - 2 * WQ) // 2
    HKV = KVW // HEAD_DIM
    rows_o = fox_w_out.shape[1]
    assert wf * N_DEV == 4 * W + H and rows_o * N_DEV == W and H <= LANE and HQ <= LANE
    me = _my_index()

    _, sw_f, np_f = _slab_geom(wf)
    _, sw_s, np_s = _slab_geom(ws)

    def slab(w2d, w, sw):
        return jnp.pad(w2d.astype(BF16), ((0, 0), (0, sw - w)))

    (fi_all,) = _all_gather([slab(fox_w_in[0], wf, sw_f)])
    w_fi = _assemble(fi_all, wf)
    later = [slab(swa_w_in[0], ws, sw_s), fox_w_out[0].astype(BF16), swa_w_out[0].astype(BF16)]

    x0 = x[0]
    g0, g1, gf = norm_g[0:1], norm_g[1:2], final_g[None, :]
    bias = _pad_lanes(fox_b_f, 0, LANE)
    sinks = _pad_lanes(swa_sinks, 0, LANE)
    tab_k = _rope_tables(S, LANE)
    mask_bias = _swa_mask_bias(HQ // HKV)

    h0 = _rmsnorm_fwd(x0, g0, "rmsnorm0")
    qkv0 = _proj(h0, w_fi, 0, 4 * W, BF16, "fox_in_qkvg")
    fl = _proj(h0, w_fi, 4 * W, LANE, F32, "fox_in_f")
    c = _fox_gate_fwd(fl, bias)
    y0, o0, a0, (si_all, fo_all, so_all) = _fox_fwd(qkv0, c, H, gather=later)
    w_si = _assemble(si_all, ws)
    w_fo = fo_all.reshape(W, D)
    w_so = so_all.reshape(WQ, D)
    x1, h1 = _out_proj_norm(y0, w_fo, x0, g1, "fox_out")

    qkv1 = _proj(h1, w_si, 0, WQ + 2 * KVW, BF16, "swa_in_qkv", rope=(tab_k, WQ + KVW))
    gate1 = _proj(h1, w_si, WQ + 2 * KVW, WQ, BF16, "swa_in_gate")
    y1, o1 = _swa_fwd(qkv1, gate1, sinks, mask_bias, HQ, HKV)
    dx2, dx2b, dgf, loss_p = _out_proj_loss(y1, w_so, x1, loss_target[0], gf, "swa_out_loss")

    dy1 = _matmul_nt([(dx2b, None, 0)], w_so, WQ, "swa_out_bwd")
    g_so, g_so_h = _matmul_tn(y1, [(dx2b, None, 0)], D, "swa_out_wgrad", also_bf16=True)
    dqg1, dkv1, dsink = _swa_bwd(qkv1, dy1, gate1, o1, sinks, tab_k, mask_bias, HQ, HKV)
    dkv1f = _swa_dkv_finish(dkv1, tab_k)
    parts1 = [(dqg1, 0, 0), (dkv1f, None, WQ), (dqg1, 1, WQ + 2 * KVW)]
    g_si, g_si_h = _matmul_tn(h1, parts1, np_s, "swa_in_wgrad", tile_major=True, also_bf16=True)
    dh1 = _matmul_nt(parts1, w_si, D, "swa_in_bwd")
    dx1, dx1b, dg1 = _rmsnorm_bwd(dh1, x1, g1, dx2, "rmsnorm1_bwd")

    qaug0, doaug0, dgate0 = _fox_out_bwd(dx1b, w_fo, qkv0, o0, a0, H)
    g_fo, g_fo_h = _matmul_tn(y0, [(dx1b, None, 0)], D, "fox_out_wgrad", also_bf16=True)
    early_specs = [("col", ws), ("row", rows_o), ("row", rows_o)]
    dqkv0, dcr, dcc, early_recv = _fox_bwd(qaug0, doaug0, qkv0, c, H, scatter=[g_si_h, g_fo_h, g_so_h],
                                          scatter_specs=early_specs)
    dfl, dbf = _fox_gate_bwd(fl, bias, dcr - dcc)
    parts0 = [(dqkv0, p, p * W) for p in range(3)] + [(dgate0, None, 3 * W), (dfl, None, 4 * W)]
    g_fi, g_fi_h = _matmul_tn(h0, parts0, np_f, "fox_in_wgrad", tile_major=True, also_bf16=True)
    spec_fi = ("col", wf)
    fi_sems, fi_src, fi_land, token = _scatter_start(g_fi_h, spec_fi)
    parts0[-1] = (dfl + token[0, 0].astype(BF16), None, 4 * W)
    dh0 = _matmul_nt(parts0, w_fi, D, "fox_in_bwd")
    dx0, _, dg0 = _rmsnorm_bwd(dh0, x0, g0, dx1, "rmsnorm0_bwd")

    red_si, gw_fo, gw_so = [_final_sum8(g_, r_, s_)
                            for g_, r_, s_ in zip([g_si, g_fo, g_so], early_recv, early_specs)]
    gw_si = lax.dynamic_slice(red_si, (0, (ws * me) % LANE), (D, ws))

    P = D
    dsink_v = dsink[:, :, 0].reshape(1, HQ)
    row3 = _pad_lanes(dbf[:, :H], 0, P) + _pad_lanes(dsink_v, LANE, P) + _pad_lanes(loss_p[:, :1], 2 * LANE, P)
    pack = _pad_rows(dg0, 0) + _pad_rows(dg1, 1) + _pad_rows(dgf, 2) + _pad_rows(row3, 3)

    d_fo, m_fo, v_fo = _adamw(fox_w_out[0], gw_fo, m_fox_w_out[0], v_fox_w_out[0], "adamw_fox_out")
    d_si, m_si, v_si = _adamw(swa_w_in[0], gw_si, m_swa_w_in[0], v_swa_w_in[0], "adamw_swa_in")
    d_so, m_so, v_so = _adamw(swa_w_out[0], gw_so, m_swa_w_out[0], v_swa_w_out[0], "adamw_swa_out")
    recv_fi = _scatter_wait(fi_sems, fi_src, fi_land, spec_fi, after=[dx0, pack, d_fo, d_si, d_so])
    red_fi = _final_sum8(g_fi, recv_fi, spec_fi)
    gw_fi = lax.dynamic_slice(red_fi, (0, (wf * me) % LANE), (D, wf))
    d_fi, m_fi, v_fi = _adamw(fox_w_in[0], gw_fi, m_fox_w_in[0], v_fox_w_in[0], "adamw_fox_in")

    tot = _all_reduce_small(pack, after=recv_fi)
    loss = tot[3, 2 * LANE]
    g_norm = tot[0:2]
    g_final = tot[2]
    g_bf = tot[3:4, 0:H]
    g_sinks = tot[3:4, LANE:LANE + HQ]

    def small_pack(ng, fg, bf, sk):
        r3 = _pad_lanes(bf, 0, P) + _pad_lanes(sk, LANE, P)
        return _pad_rows(ng, 0) + _pad_rows(fg[None, :], 2) + _pad_rows(r3, 3)

    sd, sm, sv = _adamw(small_pack(norm_g, final_g, fox_b_f, swa_sinks), tot,
                        small_pack(m_norm_g, m_final_g, m_fox_b_f, m_swa_sinks),
                        small_pack(v_norm_g, v_final_g, v_fox_b_f, v_swa_sinks), "adamw_small")

    def unpack(t):
        return t[0:2], t[3:4, 0:H], t[3:4, LANE:LANE + HQ], t[2]

    def group(small, fi, fo, si, so):
        ng, bf, sk, fg = unpack(small)
        return (ng, fi[None], bf, fo[None], si[None], sk, so[None], fg)

    grads = (g_norm, gw_fi[None], g_bf, gw_fo[None], gw_si[None], g_sinks, gw_so[None], g_final)
    return (loss, dx0[None], *grads, *group(sd, d_fi, d_fo, d_si, d_so),
            *group(sm, m_fi, m_fo, m_si, m_so), *group(sv, v_fi, v_fo, v_si, v_so))
```

```python
import math

import jax
import jax.numpy as jnp
from jax import lax
from jax.experimental import pallas as pl
from jax.experimental.pallas import tpu as pltpu

F32 = jnp.float32
BF16 = jnp.bfloat16
MESH = pl.DeviceIdType.MESH

N_DEV = 8
LANE = 128
HEAD_DIM = 64
SWA_BLOCK = 128
NEG_INF = -1e30
RMS_EPS = 1e-6
ROPE_THETA = 500000.0
ROT_DIM = HEAD_DIM // 4
ADAM_LR, ADAM_B1, ADAM_B2, ADAM_EPS, ADAM_WD, ADAM_STEP = 0.001, 0.9, 0.999, 1e-08, 0.01, 10
VMEM_LIMIT = 56 * 1024 * 1024
MM_TILE = 1024


def _cparams(**kw):
    return pltpu.CompilerParams(vmem_limit_bytes=VMEM_LIMIT, **kw)


def _tile(n, cap):
    if n <= cap:
        return n
    t = (cap // LANE) * LANE
    while t > LANE and n % t:
        t -= LANE
    assert n % t == 0, (n, cap)
    return t


def _row_tile(n, cap):
    t = min(n, cap)
    while n % t:
        t //= 2
    return t


def _dot_nn(a, b):
    return jnp.dot(a, b, preferred_element_type=F32)


def _dot_nt(a, b):
    return lax.dot_general(a, b, (((1,), (1,)), ((), ())), preferred_element_type=F32)


def _dot_tn(a, b):
    return lax.dot_general(a, b, (((0,), (0,)), ((), ())), preferred_element_type=F32)


def _sigmoid(g):
    return 1.0 / (1.0 + jnp.exp(-g))


def _slab_geom(w):
    starts = [w * i for i in range(N_DEV)]
    aligned = [LANE * (s // LANE) for s in starts]
    offs = [s - a for s, a in zip(starts, aligned)]
    sw = LANE * (-(-(max(offs) + w) // LANE))
    return aligned, sw, aligned[-1] + sw


def _my_index():
    return 4 * lax.axis_index("x") + 2 * lax.axis_index("y") + lax.axis_index("c")


def _all_gather(arrs):
    n = len(arrs)

    def body(*refs):
        ins, outs = refs[:n], refs[n:2 * n]
        send_sems, recv_sems, local_sems = refs[2 * n:]
        x, y, c = lax.axis_index("x"), lax.axis_index("y"), lax.axis_index("c")
        me, sib = (x, y, c), (x, y, 1 - c)
        chips = [(1 - x, y), (x, 1 - y), (1 - x, 1 - y)]

        def idx(px, py, pc):
            return 4 * px + 2 * py + pc

        def copy(a, k, block, to, src=None):
            dst = outs[a].at[idx(*block)]
            return pltpu.make_async_remote_copy(
                src_ref=dst if src is None else src, dst_ref=dst,
                send_sem=send_sems.at[a, k], recv_sem=recv_sems.at[a, k],
                device_id=to, device_id_type=MESH)

        mine = [pltpu.make_async_copy(ins[a], outs[a].at[idx(*me)], local_sems.at[a]) for a in range(n)]
        for m in mine:
            m.start()
        first = []
        for a in range(n):
            first.append(copy(a, 0, me, sib, src=ins[a]))
            for j, chip in enumerate(chips):
                first.append(copy(a, 1 + j, me, (*chip, c), src=ins[a]))
        for cp in first:
            cp.start()
        passed = []
        for j, chip in enumerate(chips):
            for a in range(n):
                copy(a, 1 + j, (*chip, c), me).wait_recv()
                p = copy(a, 4 + j, (*chip, c), sib)
                p.start()
                passed.append(p)
        for a in range(n):
            copy(a, 0, sib, me).wait_recv()
        for j, chip in enumerate(chips):
            for a in range(n):
                copy(a, 4 + j, (*chip, 1 - c), me).wait_recv()
        for cp in first + passed:
            cp.wait_send()
        for m in mine:
            m.wait()

    any_spec = pl.BlockSpec(memory_space=pl.ANY)
    return pl.pallas_call(
        body, name="weights_all_gather",
        out_shape=[jax.ShapeDtypeStruct((N_DEV,) + a.shape, a.dtype) for a in arrs],
        in_specs=[any_spec] * n, out_specs=[any_spec] * n,
        scratch_shapes=[pltpu.SemaphoreType.DMA((n, 7)), pltpu.SemaphoreType.DMA((n, 7)),
                        pltpu.SemaphoreType.DMA((n,))],
    )(*arrs)


def _rs_windows(specs):
    def window(ref, spec, blk):
        kind, n = spec
        if kind == "col":
            _, sw, _ = _slab_geom(n)
            return ref.at[pl.ds((n * blk) // LANE, sw // LANE)]
        start = pl.multiple_of(n * blk, n)
        return ref.at[pl.ds(start, n), :]
    return window


def _peer(k):
    x, y, c = lax.axis_index("x"), lax.axis_index("y"), lax.axis_index("c")
    return (x ^ (k >> 2), y ^ ((k >> 1) & 1), c ^ (k & 1))


def _direct_gather_copies(ins, outs, send_sems, recv_sems, local_sems):
    me = _my_index()
    remote, local = [], []
    for a, (src, dst) in enumerate(zip(ins, outs)):
        local.append(pltpu.make_async_copy(src, dst.at[me], local_sems.at[a]))
        for k in range(1, N_DEV):
            remote.append(pltpu.make_async_remote_copy(
                src_ref=src, dst_ref=dst.at[me], send_sem=send_sems.at[a, k - 1], recv_sem=recv_sems.at[a, k - 1],
                device_id=_peer(k), device_id_type=MESH))
    return remote, local


def _direct_scatter_copies(ins, outs, specs, send_sems, recv_sems):
    window = _rs_windows(specs)
    remote = []
    for a, (src, dst) in enumerate(zip(ins, outs)):
        for k in range(1, N_DEV):
            px, py, pc = _peer(k)
            remote.append(pltpu.make_async_remote_copy(
                src_ref=window(src, specs[a], 4 * px + 2 * py + pc), dst_ref=dst.at[k - 1],
                send_sem=send_sems.at[a, k - 1], recv_sem=recv_sems.at[a, k - 1],
                device_id=(px, py, pc), device_id_type=MESH))
    return remote


def _scatter_block_shape(g, spec):
    kind, w = spec
    return (_slab_geom(w)[1] // LANE, g.shape[1], LANE) if kind == "col" else (w, g.shape[1])


def _wait_all(remote, local=()):
    for cp in remote:
        cp.wait_recv()
    for cp in remote:
        cp.wait_send()
    for cp in local:
        cp.wait()


def _scatter_start(g, spec):
    blk = _scatter_block_shape(g, spec)
    window = _rs_windows([spec])
    npeer = N_DEV - 1

    def body(g_ref, land_ref, *rest):
        sems = rest[:2 * npeer]
        token = rest[2 * npeer + 2]
        for cp in _peer_block_copies(g_ref, land_ref, spec, window, sems[:npeer], sems[npeer:]):
            cp.start()
        token[...] = jnp.zeros(token.shape, token.dtype)

    hbm = pl.BlockSpec(memory_space=pltpu.HBM)
    sem = pl.BlockSpec(memory_space=pltpu.SEMAPHORE)
    land = lax.empty((npeer,) + blk, g.dtype)
    outs = pl.pallas_call(
        body, name="grads_scatter_start",
        out_shape=(pltpu.SemaphoreType.DMA(()),) * (2 * npeer)
        + (pltpu.HBM(g.shape, g.dtype), pltpu.HBM(land.shape, land.dtype), jax.ShapeDtypeStruct((8, LANE), F32)),
        in_specs=(hbm, hbm),
        out_specs=(sem,) * (2 * npeer) + (hbm, hbm, pl.BlockSpec(memory_space=pltpu.VMEM)),
        input_output_aliases={0: 2 * npeer, 1: 2 * npeer + 1},
        compiler_params=pltpu.CompilerParams(has_side_effects=pltpu.SideEffectType.DATAFLOW_SIDE_EFFECTING),
    )(pltpu.with_memory_space_constraint(g, pltpu.HBM), pltpu.with_memory_space_constraint(land, pltpu.HBM))
    return outs[:2 * npeer], outs[2 * npeer], outs[2 * npeer + 1], outs[2 * npeer + 2]


def _peer_block_copies(g_ref, land_ref, spec, window, send_sems, recv_sems):
    copies = []
    for k in range(1, N_DEV):
        px, py, pc = _peer(k)
        copies.append(pltpu.make_async_remote_copy(
            src_ref=window(g_ref, spec, 4 * px + 2 * py + pc), dst_ref=land_ref.at[k - 1],
            send_sem=send_sems[k - 1], recv_sem=recv_sems[k - 1], device_id=(px, py, pc), device_id_type=MESH))
    return copies


def _scatter_wait(sems, g_thru, land_thru, spec, after):
    window = _rs_windows([spec])
    npeer = N_DEV - 1

    def body(g_ref, land_ref, *rest):
        s = rest[:2 * npeer]
        copies = _peer_block_copies(g_ref, land_ref, spec, window, s[:npeer], s[npeer:])
        for cp in copies:
            cp.wait_send()
        for cp in copies:
            cp.wait_recv()

    hbm = pl.BlockSpec(memory_space=pltpu.HBM)
    sem = pl.BlockSpec(memory_space=pltpu.SEMAPHORE)
    return pl.pallas_call(
        body, name="grads_scatter_wait",
        out_shape=(pltpu.HBM(g_thru.shape, g_thru.dtype), pltpu.HBM(land_thru.shape, land_thru.dtype)),
        in_specs=(hbm, hbm) + (sem,) * (2 * npeer) + (pl.BlockSpec(memory_space=pl.ANY),) * len(after),
        out_specs=(hbm, hbm), input_output_aliases={0: 0, 1: 1},
        compiler_params=pltpu.CompilerParams(has_side_effects=pltpu.SideEffectType.DATAFLOW_SIDE_EFFECTING),
    )(g_thru, land_thru, *sems, *after)[1]


def _final_sum8(g, recv, spec):
    kind, n = spec
    me = _my_index()
    offs = jnp.stack([(n * me) // LANE if kind == "col" else me]).astype(jnp.int32)
    if kind == "col":
        _, T, M, _ = recv.shape
        grid = (T,)
        in_specs = [pl.BlockSpec((1, M, LANE), lambda t, o: (o[0] + t, 0, 0)),
                    pl.BlockSpec((N_DEV - 1, 1, M, LANE), lambda t, o: (0, t, 0, 0))]
        out_spec = pl.BlockSpec((M, LANE), lambda t, o: (0, t))
        out_shape = jax.ShapeDtypeStruct((M, T * LANE), F32)
    else:
        _, nrow, C = recv.shape
        grid = (1,)
        in_specs = [pl.BlockSpec((nrow, C), lambda t, o: (o[0], 0)),
                    pl.BlockSpec((N_DEV - 1, nrow, C), lambda t, o: (0, 0, 0))]
        out_spec = pl.BlockSpec((nrow, C), lambda t, o: (0, 0))
        out_shape = jax.ShapeDtypeStruct((nrow, C), F32)

    def body(o_ref, g_ref, r_ref, out_ref):
        acc = g_ref[0] if kind == "col" else g_ref[...]
        for k in range(N_DEV - 1):
            acc = acc + (r_ref[k, 0] if kind == "col" else r_ref[k]).astype(F32)
        out_ref[...] = acc

    return pl.pallas_call(
        body, name="grads_final_sum8",
        grid_spec=pltpu.PrefetchScalarGridSpec(num_scalar_prefetch=1, grid=grid, in_specs=in_specs,
                                               out_specs=out_spec),
        out_shape=out_shape, compiler_params=_cparams(),
    )(offs, g, recv)


def _all_reduce_small(pack, after):
    R, P = pack.shape

    def body(x_ref, after_ref, o_ref, gat_ref, send_sems, recv_sems):
        x, y, c = lax.axis_index("x"), lax.axis_index("y"), lax.axis_index("c")
        me = 4 * x + 2 * y + c
        gat_ref[me] = x_ref[...]
        copies = []
        for k in range(1, N_DEV):
            peer = (x ^ (k >> 2), y ^ ((k >> 1) & 1), c ^ (k & 1))
            copies.append(pltpu.make_async_remote_copy(
                src_ref=x_ref, dst_ref=gat_ref.at[me],
                send_sem=send_sems.at[k - 1], recv_sem=recv_sems.at[k - 1],
                device_id=peer, device_id_type=MESH))
        for cp in copies:
            cp.start()
        for cp in copies:
            cp.wait_recv()
        for cp in copies:
            cp.wait_send()
        acc = gat_ref[0]
        for d in range(1, N_DEV):
            acc = acc + gat_ref[d]
        o_ref[...] = acc

    vm = pl.BlockSpec(memory_space=pltpu.VMEM)
    return pl.pallas_call(
        body, name="small_all_reduce",
        out_shape=jax.ShapeDtypeStruct((R, P), F32),
        in_specs=[vm, pl.BlockSpec(memory_space=pl.ANY)], out_specs=vm,
        scratch_shapes=[pltpu.VMEM((N_DEV, R, P), F32),
                        pltpu.SemaphoreType.DMA((N_DEV - 1,)), pltpu.SemaphoreType.DMA((N_DEV - 1,))],
    )(pack, after)


def _assemble(slabs, w):
    aligned, sw, total = _slab_geom(w)
    K = slabs.shape[1]
    tr = _row_tile(K, 256)

    def body(s_ref, o_ref):
        o_ref[...] = jnp.zeros(o_ref.shape, BF16)
        for i in range(N_DEV):
            a, off = aligned[i], w * i - aligned[i]
            x = s_ref[i].astype(F32)
            if off:
                x = pltpu.roll(x, off, axis=1)
            o_ref[:, a:a + sw] = (o_ref[:, a:a + sw].astype(F32) + x).astype(BF16)

    return pl.pallas_call(
        body, name="assemble_w_in", grid=(K // tr,),
        in_specs=[pl.BlockSpec((N_DEV, tr, sw), lambda i: (0, i, 0))],
        out_specs=pl.BlockSpec((tr, total), lambda i: (i, 0)),
        out_shape=jax.ShapeDtypeStruct((K, total), BF16),
        compiler_params=_cparams(),
    )(slabs)


def _rmsnorm_fwd(x, g, name):
    S, D = x.shape
    tm = _row_tile(S, 256)

    def body(x_ref, g_ref, h_ref):
        xv = x_ref[...]
        r = lax.rsqrt(jnp.mean(xv * xv, axis=-1, keepdims=True) + RMS_EPS)
        h_ref[...] = ((xv * r) * g_ref[...]).astype(BF16)

    return pl.pallas_call(
        body, name=name, grid=(S // tm,),
        in_specs=[pl.BlockSpec((tm, D), lambda i: (i, 0)), pl.BlockSpec((1, D), lambda i: (0, 0))],
        out_specs=pl.BlockSpec((tm, D), lambda i: (i, 0)),
        out_shape=jax.ShapeDtypeStruct((S, D), BF16),
        compiler_params=_cparams(),
    )(x, g)


def _rmsnorm_bwd(dh, x, g, dres, name):
    S, D = x.shape
    tm = _row_tile(S, 256)

    def body(dh_ref, x_ref, g_ref, dr_ref, dx_ref, dxb_ref, dg_ref):
        xv = x_ref[...]
        r = lax.rsqrt(jnp.mean(xv * xv, axis=-1, keepdims=True) + RMS_EPS)
        xhat = xv * r
        d = dh_ref[...]
        gd = d * g_ref[...]
        dx = r * (gd - xhat * jnp.mean(gd * xhat, axis=-1, keepdims=True)) + dr_ref[...]
        dx_ref[...] = dx
        dxb_ref[...] = dx.astype(BF16)

        @pl.when(pl.program_id(0) == 0)
        def _():
            dg_ref[...] = jnp.zeros(dg_ref.shape, F32)
        dg_ref[...] += jnp.sum(d * xhat, axis=0, keepdims=True)

    row = pl.BlockSpec((tm, D), lambda i: (i, 0))
    vec = pl.BlockSpec((1, D), lambda i: (0, 0))
    return pl.pallas_call(
        body, name=name, grid=(S // tm,),
        in_specs=[row, row, vec, row], out_specs=[row, row, vec],
        out_shape=[jax.ShapeDtypeStruct((S, D), F32), jax.ShapeDtypeStruct((S, D), BF16),
                   jax.ShapeDtypeStruct((1, D), F32)],
        compiler_params=_cparams(),
    )(dh, x, g, dres)


def _adamw(w, g, m, v, name):
    R, C = w.shape
    tr = _row_tile(R, 256)
    c1 = 1.0 - ADAM_B1 ** ADAM_STEP
    c2 = 1.0 - ADAM_B2 ** ADAM_STEP

    def body(w_ref, g_ref, m_ref, v_ref, d_ref, nm_ref, nv_ref):
        gv = g_ref[...]
        nm = ADAM_B1 * m_ref[...] + (1.0 - ADAM_B1) * gv
        nv = ADAM_B2 * v_ref[...] + (1.0 - ADAM_B2) * (gv * gv)
        d_ref[...] = -ADAM_LR * ((nm / c1) / (jnp.sqrt(nv / c2) + ADAM_EPS) + ADAM_WD * w_ref[...])
        nm_ref[...] = nm
        nv_ref[...] = nv

    spec = pl.BlockSpec((tr, C), lambda i: (i, 0))
    return pl.pallas_call(
        body, name=name, grid=(R // tr,),
        in_specs=[spec] * 4, out_specs=[spec] * 3,
        out_shape=[jax.ShapeDtypeStruct((R, C), F32)] * 3,
        compiler_params=_cparams(),
    )(w, g, m, v)


def _proj(h, wfull, col0, ncols, out_dtype, name, rope=None):
    S, K = h.shape
    tm = _row_tile(S, MM_TILE)
    tn = math.gcd(_tile(ncols, MM_TILE), col0) if col0 else _tile(ncols, MM_TILE)
    if rope is not None:
        tn = _tile(math.gcd(ncols, rope[1]), MM_TILE)
    assert ncols % tn == 0 and col0 % tn == 0
    cb = col0 // tn

    def body(*refs):
        if rope is None:
            a_ref, b_ref, o_ref = refs
        else:
            a_ref, b_ref, t_ref, o_ref = refs
        acc = _dot_nn(a_ref[...], b_ref[...])
        if rope is not None:
            t0, t1, t2 = (jnp.tile(t_ref[i], (1, tn // LANE)) for i in range(3))
            roped = (acc * t0 + pltpu.roll(acc, tn - ROT_DIM // 2, axis=1) * t1
                     + pltpu.roll(acc, ROT_DIM // 2, axis=1) * t2)
            acc = jnp.where(pl.program_id(1) < rope[1] // tn, roped, acc)
        o_ref[...] = acc.astype(out_dtype)

    in_specs = [pl.BlockSpec((tm, K), lambda i, j: (i, 0)), pl.BlockSpec((K, tn), lambda i, j: (0, cb + j))]
    args = [h, wfull]
    if rope is not None:
        in_specs.append(pl.BlockSpec((3, tm, LANE), lambda i, j: (0, i, 0)))
        args.append(rope[0])
    return pl.pallas_call(
        body, name=name, grid=(S // tm, ncols // tn),
        in_specs=in_specs, out_specs=pl.BlockSpec((tm, tn), lambda i, j: (i, j)),
        out_shape=jax.ShapeDtypeStruct((S, ncols), out_dtype),
        compiler_params=_cparams(),
    )(*args)


def _out_proj_norm(y, wo, xres, g, name):
    S, W = y.shape
    D = wo.shape[1]
    tm = _row_tile(S, 512)

    def body(a_ref, b_ref, r_ref, g_ref, x_ref, h_ref):
        xv = r_ref[...] + _dot_nn(a_ref[...], b_ref[...])
        x_ref[...] = xv
        r = lax.rsqrt(jnp.mean(xv * xv, axis=-1, keepdims=True) + RMS_EPS)
        h_ref[...] = ((xv * r) * g_ref[...]).astype(BF16)

    row = pl.BlockSpec((tm, D), lambda i: (i, 0))
    return pl.pallas_call(
        body, name=name, grid=(S // tm,),
        in_specs=[pl.BlockSpec((tm, W), lambda i: (i, 0)), pl.BlockSpec((W, D), lambda i: (0, 0)), row,
                  pl.BlockSpec((1, D), lambda i: (0, 0))],
        out_specs=[row, row],
        out_shape=[jax.ShapeDtypeStruct((S, D), F32), jax.ShapeDtypeStruct((S, D), BF16)],
        compiler_params=_cparams(),
    )(y, wo, xres, g)


def _out_proj_loss(y, wo, xres, tgt, g, name):
    S, W = y.shape
    D = wo.shape[1]
    tm = _row_tile(S, 512)

    def body(a_ref, b_ref, r_ref, t_ref, g_ref, dx_ref, dxb_ref, dg_ref, loss_ref):
        xv = r_ref[...] + _dot_nn(a_ref[...], b_ref[...])
        r = lax.rsqrt(jnp.mean(xv * xv, axis=-1, keepdims=True) + RMS_EPS)
        xhat = xv * r
        gv = g_ref[...]
        err = xhat * gv - t_ref[...]
        d = err * (1.0 / D)
        gd = d * gv
        dx = r * (gd - xhat * jnp.mean(gd * xhat, axis=-1, keepdims=True))
        dx_ref[...] = dx
        dxb_ref[...] = dx.astype(BF16)

        @pl.when(pl.program_id(0) == 0)
        def _():
            dg_ref[...] = jnp.zeros(dg_ref.shape, F32)
            loss_ref[...] = jnp.zeros(loss_ref.shape, F32)
        dg_ref[...] += jnp.sum(d * xhat, axis=0, keepdims=True)
        per_tok = jnp.sum(err * err, axis=-1, keepdims=True) * (1.0 / D)
        loss_ref[...] += 0.5 * jnp.sum(per_tok, axis=0, keepdims=True)

    row = pl.BlockSpec((tm, D), lambda i: (i, 0))
    vec = pl.BlockSpec((1, D), lambda i: (0, 0))
    return pl.pallas_call(
        body, name=name, grid=(S // tm,),
        in_specs=[pl.BlockSpec((tm, W), lambda i: (i, 0)), pl.BlockSpec((W, D), lambda i: (0, 0)), row, row, vec],
        out_specs=[row, row, vec, pl.BlockSpec((1, LANE), lambda i: (0, 0))],
        out_shape=[jax.ShapeDtypeStruct((S, D), F32), jax.ShapeDtypeStruct((S, D), BF16),
                   jax.ShapeDtypeStruct((1, D), F32), jax.ShapeDtypeStruct((1, LANE), F32)],
        compiler_params=_cparams(),
    )(y, wo, xres, tgt, g)


def _matmul_nt(parts, wfull, out_rows, name):
    S = parts[0][0].shape[-2]
    tm, tn = _row_tile(S, 2 * MM_TILE if len(parts) <= 2 else MM_TILE), _tile(out_rows, MM_TILE)
    plan, lo = [], 0
    for arr, lead, col0 in parts:
        n_p = arr.shape[-1]
        tk = math.gcd(_tile(n_p, 1024), col0) if col0 else _tile(n_p, 1024)
        steps = n_p // tk * (arr.shape[0] if lead == "stack" else 1)
        plan.append((lead, col0 // tk, tk, lo, lo + steps))
        lo += steps
    nk = lo
    npart = len(parts)

    def body(*refs):
        a_refs, w_refs = refs[:npart], refs[npart:2 * npart]
        o_ref, acc_ref = refs[2 * npart], refs[2 * npart + 1]
        k = pl.program_id(2)

        @pl.when(k == 0)
        def _():
            acc_ref[...] = jnp.zeros(acc_ref.shape, F32)
        for p, (_, _, _, lo_p, hi_p) in enumerate(plan):
            @pl.when((k >= lo_p) & (k < hi_p))
            def _(p=p):
                acc_ref[...] += _dot_nt(a_refs[p][...], w_refs[p][...])

        @pl.when(k == nk - 1)
        def _():
            o_ref[...] = acc_ref[...]

    in_specs, args = [], []
    for (arr, lead, col0), (_, cb, tk, lo_p, hi_p) in zip(parts, plan):
        def kk(k, lo_p=lo_p, hi_p=hi_p):
            return jnp.clip(k - lo_p, 0, hi_p - lo_p - 1)
        if lead is None:
            in_specs.append(pl.BlockSpec((tm, tk), lambda i, j, k, kk=kk: (i, kk(k))))
        elif lead == "stack":
            nkb = arr.shape[-1] // tk
            in_specs.append(pl.BlockSpec((None, tm, tk), lambda i, j, k, kk=kk, nkb=nkb: (kk(k) // nkb, i, kk(k) % nkb)))
        else:
            in_specs.append(pl.BlockSpec((None, tm, tk), lambda i, j, k, kk=kk, lead=lead: (lead, i, kk(k))))
        args.append(arr)
    for (_, cb, tk, lo_p, hi_p) in plan:
        def kk(k, lo_p=lo_p, hi_p=hi_p):
            return jnp.clip(k - lo_p, 0, hi_p - lo_p - 1)
        in_specs.append(pl.BlockSpec((tn, tk), lambda i, j, k, kk=kk, cb=cb: (j, cb + kk(k))))
        args.append(wfull)
    return pl.pallas_call(
        body, name=name, grid=(S // tm, out_rows // tn, nk),
        in_specs=in_specs, out_specs=pl.BlockSpec((tm, tn), lambda i, j, k: (i, j)),
        out_shape=jax.ShapeDtypeStruct((S, out_rows), F32),
        scratch_shapes=[pltpu.VMEM((tm, tn), F32)],
        compiler_params=_cparams(),
    )(*args)


def _matmul_tn(a, parts, total, name, tile_major=False, also_bf16=False):
    S, M = a.shape
    tm, ts = _tile(M, MM_TILE), _row_tile(S, 2 * MM_TILE)
    nout = 2 if also_bf16 else 1
    outs = None
    for idx, (arr, lead, col0) in enumerate(parts):
        n_p = arr.shape[-1]
        tn = math.gcd(_tile(n_p, MM_TILE), col0) if col0 else _tile(n_p, MM_TILE)
        cb = col0 // tn
        nk = S // ts
        nb = n_p // tn
        if lead == "stack":
            n_p *= arr.shape[0]

        def body(*refs, nk=nk, tn=tn):
            a_ref, b_ref = refs[0], refs[1]
            o_refs, acc_ref = refs[-1 - nout:-1], refs[-1]
            k = pl.program_id(2)

            @pl.when(k == 0)
            def _():
                acc_ref[...] = jnp.zeros(acc_ref.shape, F32)
            acc_ref[...] += _dot_tn(a_ref[...], b_ref[...])

            @pl.when(k == nk - 1)
            def _():
                for o_ref in o_refs:
                    if tile_major:
                        for t in range(tn // LANE):
                            o_ref[t] = acc_ref[:, LANE * t:LANE * (t + 1)].astype(o_ref.dtype)
                    else:
                        o_ref[...] = acc_ref[...].astype(o_ref.dtype)

        in_specs = [pl.BlockSpec((ts, tm), lambda i, j, k: (k, i))]
        if lead is None:
            in_specs.append(pl.BlockSpec((ts, tn), lambda i, j, k: (k, j)))
        elif lead == "stack":
            in_specs.append(pl.BlockSpec((None, ts, tn), lambda i, j, k, nb=nb: (j // nb, k, j % nb)))
        else:
            in_specs.append(pl.BlockSpec((None, ts, tn), lambda i, j, k, lead=lead: (lead, k, j)))
        args = [a, arr]
        aliases = {}
        if outs is not None:
            in_specs += [pl.BlockSpec(memory_space=pl.ANY)] * nout
            args += list(outs)
            aliases = {2 + o: o for o in range(nout)}
        if tile_major:
            out_spec = pl.BlockSpec((tn // LANE, tm, LANE), lambda i, j, k, cb=cb: (cb + j, i, 0))
            shape = (total // LANE, M, LANE)
        else:
            out_spec = pl.BlockSpec((tm, tn), lambda i, j, k, cb=cb: (i, cb + j))
            shape = (M, total)
        outs = pl.pallas_call(
            body, name=f"{name}_{idx}", grid=(M // tm, n_p // tn, nk),
            in_specs=in_specs, out_specs=[out_spec] * nout,
            out_shape=[jax.ShapeDtypeStruct(shape, dt) for dt in (F32, BF16)[:nout]],
            scratch_shapes=[pltpu.VMEM((tm, tn), F32)],
            input_output_aliases=aliases,
            compiler_params=_cparams(),
        )(*args)
    return tuple(outs) if also_bf16 else outs[0]


def _log_sigmoid(z):
    e = jnp.exp(-jnp.abs(z))
    return jnp.minimum(z, 0.0) - jnp.where(e < 1e-4, e * (1.0 - 0.5 * e), jnp.log(1.0 + e))


def _fox_gate_fwd(fl, bias):
    S = fl.shape[0]

    def body(f_ref, b_ref, c_ref):
        row = lax.broadcasted_iota(jnp.int32, (8, LANE), 0)

        def step(i, carry):
            r0 = pl.multiple_of(i * 8, 8)
            t = _log_sigmoid(f_ref[pl.ds(r0, 8), :] + b_ref[...])
            for sh in (1, 2, 4):
                t = t + jnp.where(row >= sh, pltpu.roll(t, sh, axis=0), 0.0)
            t = t + carry
            c_ref[pl.ds(r0, 8), :] = t
            return jnp.sum(jnp.where(row == 7, t, 0.0), axis=0, keepdims=True)

        lax.fori_loop(0, S // 8, step, jnp.zeros((1, LANE), F32))

    vm = pl.BlockSpec(memory_space=pltpu.VMEM)
    return pl.pallas_call(
        body, name="fox_gate_fwd", in_specs=[vm, vm], out_specs=vm,
        out_shape=jax.ShapeDtypeStruct((S, LANE), F32),
        compiler_params=_cparams(),
    )(fl, bias)


def _fox_gate_bwd(fl, bias, dc):
    S = fl.shape[0]

    def body(f_ref, b_ref, d_ref, o_ref, db_ref, acc_ref):
        row = lax.broadcasted_iota(jnp.int32, (8, LANE), 0)
        nt = S // 8

        def step(ii, carry):
            carry_c, carry_b = carry
            r0 = pl.multiple_of((nt - 1 - ii) * 8, 8)
            t = d_ref[pl.ds(r0, 8), :]
            for sh in (1, 2, 4):
                t = t + jnp.where(row < 8 - sh, pltpu.roll(t, 8 - sh, axis=0), 0.0)
            t = t + carry_c
            z = f_ref[pl.ds(r0, 8), :] + b_ref[...]
            dz = t * _sigmoid(-z)
            acc_ref[pl.ds(r0, 8), :] = dz
            first = jnp.sum(jnp.where(row == 0, t, 0.0), axis=0, keepdims=True)
            return first, carry_b + jnp.sum(dz, axis=0, keepdims=True)

        zero = jnp.zeros((1, LANE), F32)
        _, db = lax.fori_loop(0, nt, step, (zero, zero))
        db_ref[...] = db
        o_ref[...] = acc_ref[...].astype(BF16)

    vm = pl.BlockSpec(memory_space=pltpu.VMEM)
    return pl.pallas_call(
        body, name="fox_gate_bwd", in_specs=[vm, vm, vm], out_specs=[vm, vm],
        out_shape=[jax.ShapeDtypeStruct((S, LANE), BF16), jax.ShapeDtypeStruct((1, LANE), F32)],
        scratch_shapes=[pltpu.VMEM((S, LANE), F32)],
        compiler_params=_cparams(),
    )(fl, bias, dc)


def _bias_lanes(col, lane, e, first):
    o0 = HEAD_DIM * (1 - e)
    hi = col.astype(BF16)
    r1 = col - hi.astype(F32)
    mid = r1.astype(BF16)
    lo = (r1 - mid.astype(F32)).astype(BF16)
    d0 = o0 if first else o0 + 3
    t = jnp.where((lane >= o0) & (lane < o0 + 6), jnp.ones(lane.shape, BF16), jnp.zeros(lane.shape, BF16))
    t = jnp.where(lane == d0, hi, t)
    t = jnp.where(lane == d0 + 1, mid, t)
    return jnp.where(lane == d0 + 2, lo, t)


def _fox_fwd(qkvg, c, H, gather=()):
    na = len(gather)
    S = qkvg.shape[0]
    W = H * HEAD_DIM
    HP = H // 2
    PP = 2 if HP % 2 == 0 else 1
    NE = 2 * PP
    tq = _row_tile(S, 512)
    nq = S // tq
    wb = W // LANE
    scale = HEAD_DIM ** -0.5

    def body(*refs):
        q_ref, k_ref, v_ref, g_ref, c_ref = refs[:5]
        y_ref, o_ref, a_ref = refs[5 + na:8 + na]
        kaug_sc, vaug_sc, qaug_sc, s_sc, mb_sc, m_sc, acc_sc = refs[8 + 2 * na:15 + 2 * na]
        hp, qi = pl.program_id(0), pl.program_id(1)
        if na:
            remote, local = _direct_gather_copies(refs[5:5 + na], refs[8 + na:8 + 2 * na], *refs[15 + 2 * na:])

            @pl.when((hp == 0) & (qi == 0))
            def _():
                for cp in remote + local:
                    cp.start()
        lane = lax.broadcasted_iota(jnp.int32, (tq, LANE), 1)
        own = [lane < HEAD_DIM, lane >= HEAD_DIM]
        rows = lax.broadcasted_iota(jnp.int32, (tq, tq), 0)
        cols = lax.broadcasted_iota(jnp.int32, (tq, tq), 1)

        def bias_lanes(col, e, first):
            return _bias_lanes(col, lane, e % 2, first)

        def head_col(tile, e):
            return jnp.sum(jnp.where(lane == 2 * PP * hp + e, tile, 0.0), axis=1, keepdims=True)

        def tile_of(e):
            return slice(LANE * (e // 2), LANE * (e // 2 + 1))

        @pl.when(qi == 0)
        def _():
            def chunk(i, carry):
                r0 = pl.multiple_of(i * tq, tq)
                cb = c_ref[pl.ds(r0, tq), :]
                for e in range(NE):
                    kb, vb = k_ref[pl.ds(r0, tq), tile_of(e)], v_ref[pl.ds(r0, tq), tile_of(e)]
                    kaug_sc[e, pl.ds(r0, tq), :] = jnp.where(own[e % 2], kb, bias_lanes(-head_col(cb, e), e, False))
                    vaug_sc[e, pl.ds(r0, tq), :] = jnp.where(own[e % 2], vb, jnp.ones((tq, LANE), BF16))
                return carry
            lax.fori_loop(0, nq, chunk, 0)

        crow = c_ref[pl.ds(pl.multiple_of(qi * tq, tq), tq), :]
        ctq = [head_col(crow, e) for e in range(NE)]
        for e in range(NE):
            q = q_ref[:, tile_of(e)] * jnp.asarray(scale, BF16)
            qaug_sc[e] = jnp.where(own[e % 2], q, bias_lanes(ctq[e], e, True))
        m_sc[...] = jnp.full(m_sc.shape, NEG_INF, F32)
        acc_sc[...] = jnp.zeros(acc_sc.shape, F32)

        def scores(blk, slot, masked):
            k0 = pl.multiple_of(blk * tq, tq)
            for e in range(NE):
                s = _dot_nt(qaug_sc[e], kaug_sc[e, pl.ds(k0, tq), :])
                if masked:
                    s = jnp.where(rows >= cols, s, NEG_INF)
                s_sc[slot, e] = s
                mb_sc[slot, e] = jnp.broadcast_to(jnp.max(s, axis=1, keepdims=True), (tq, LANE))

        def accumulate(blk, slot):
            k0 = pl.multiple_of(blk * tq, tq)
            for e in range(NE):
                m_prev = m_sc[e]
                m_new = jnp.maximum(m_prev, mb_sc[slot, e])
                p = jnp.exp(s_sc[slot, e] - jnp.tile(m_new, (1, tq // LANE)))
                acc_sc[e] = jnp.exp(m_prev - m_new) * acc_sc[e] + _dot_nn(p.astype(BF16), vaug_sc[e, pl.ds(k0, tq), :])
                m_sc[e] = m_new

        def block_of(t):
            return jnp.where(t == 0, qi, t - 1)

        scores(qi, 0, True)

        def loop_body(t, carry):
            scores(t, (t + 1) % 2, False)
            accumulate(block_of(t), t % 2)
            return carry

        lax.fori_loop(0, qi, loop_body, 0)
        accumulate(block_of(qi), qi % 2)
        o_e, a_e = [], []
        for e in range(NE):
            acc = acc_sc[e]
            l = pltpu.roll(acc, HEAD_DIM, axis=1)
            o_e.append(acc / l)
            a_e.append(ctq[e] - (m_sc[e] + jnp.log(l)))
        for pp in range(PP):
            o = jnp.where(own[0], o_e[2 * pp], o_e[2 * pp + 1])
            g = g_ref[:, tile_of(2 * pp)].astype(F32)
            y_ref[:, tile_of(2 * pp)] = (o * (g * _sigmoid(g))).astype(BF16)
            o_ref[:, tile_of(2 * pp)] = o.astype(BF16)
            a_ref[pp] = jnp.where(own[0], a_e[2 * pp], a_e[2 * pp + 1])
        if na:
            @pl.when((hp == HP // PP - 1) & (qi == nq - 1))
            def _():
                _wait_all(remote, local)

    any_spec = pl.BlockSpec(memory_space=pl.ANY)
    sems = [pltpu.SemaphoreType.DMA((na, N_DEV - 1)), pltpu.SemaphoreType.DMA((na, N_DEV - 1)),
            pltpu.SemaphoreType.DMA((na,))] if na else []
    wide = PP * LANE
    outs = pl.pallas_call(
        body, name="fox_attn_fwd", grid=(HP // PP, nq),
        in_specs=[pl.BlockSpec((tq, wide), lambda h, i: (i, h)),
                  pl.BlockSpec((S, wide), lambda h, i: (0, wb // PP + h)),
                  pl.BlockSpec((S, wide), lambda h, i: (0, 2 * wb // PP + h)),
                  pl.BlockSpec((tq, wide), lambda h, i: (i, 3 * wb // PP + h)),
                  pl.BlockSpec((S, LANE), lambda h, i: (0, 0))] + [any_spec] * na,
        out_specs=[pl.BlockSpec((tq, wide), lambda h, i: (i, h)),
                   pl.BlockSpec((tq, wide), lambda h, i: (i, h)),
                   pl.BlockSpec((PP, tq, LANE), lambda h, i: (h, i, 0))] + [any_spec] * na,
        out_shape=[jax.ShapeDtypeStruct((S, W), BF16), jax.ShapeDtypeStruct((S, W), BF16),
                   jax.ShapeDtypeStruct((HP, S, LANE), F32)]
        + [jax.ShapeDtypeStruct((N_DEV,) + g.shape, g.dtype) for g in gather],
        scratch_shapes=[pltpu.VMEM((NE, S, LANE), BF16), pltpu.VMEM((NE, S, LANE), BF16),
                        pltpu.VMEM((NE, tq, LANE), BF16), pltpu.VMEM((2, NE, tq, tq), F32),
                        pltpu.VMEM((2, NE, tq, LANE), F32), pltpu.VMEM((NE, tq, LANE), F32),
                        pltpu.VMEM((NE, tq, LANE), F32)] + sems,
        compiler_params=_cparams(),
    )(qkvg, qkvg, qkvg, qkvg, c, *gather)
    return outs[0], outs[1], outs[2], list(outs[3:])


def _fox_out_bwd(dxb, wo, qkvg, o, a, H):
    S, D = dxb.shape
    W = H * HEAD_DIM
    tm, tn = _row_tile(S, 512), _tile(W, 512)
    npair = tn // LANE
    scale = HEAD_DIM ** -0.5

    def body(dx_ref, w_ref, q_ref, g_ref, o_ref, a_ref, qa_ref, da_ref, dg_ref):
        dy = _dot_nt(dx_ref[...], w_ref[...])
        lane = lax.broadcasted_iota(jnp.int32, (tm, LANE), 1)
        own = [lane < HEAD_DIM, lane >= HEAD_DIM]
        for p in range(npair):
            cols = slice(LANE * p, LANE * (p + 1))
            q = q_ref[:, cols] * jnp.asarray(scale, BF16)
            dyv, g, ov, at = dy[:, cols], g_ref[:, cols].astype(F32), o_ref[:, cols].astype(F32), a_ref[p]
            sg = _sigmoid(g)
            dob = (dyv * (g * sg)).astype(BF16)
            dg_ref[:, cols] = (dyv * ov * (sg * (1.0 + g * (1.0 - sg)))).astype(BF16)
            prod = dob.astype(F32) * ov
            for e in range(2):
                a_col = jnp.max(jnp.where(own[e], at, -jnp.inf), axis=1, keepdims=True)
                d_col = jnp.sum(jnp.where(own[e], prod, 0.0), axis=1, keepdims=True)
                qa_ref[e, :, cols] = jnp.where(own[e], q, _bias_lanes(a_col, lane, e, True))
                da_ref[e, :, cols] = jnp.where(own[e], dob, _bias_lanes(-d_col, lane, e, True))

    blk = pl.BlockSpec((tm, tn), lambda i, j: (i, j))
    pair = pl.BlockSpec((2, tm, tn), lambda i, j: (0, i, j))
    return pl.pallas_call(
        body, name="fox_out_bwd", grid=(S // tm, W // tn),
        in_specs=[pl.BlockSpec((tm, D), lambda i, j: (i, 0)), pl.BlockSpec((tn, D), lambda i, j: (j, 0)),
                  blk, pl.BlockSpec((tm, tn), lambda i, j: (i, 3 * W // tn + j)), blk,
                  pl.BlockSpec((npair, tm, LANE), lambda i, j: (j, i, 0))],
        out_specs=[pair, pair, pl.BlockSpec((None, tm, tn), lambda i, j: (3, i, j))],
        out_shape=[jax.ShapeDtypeStruct((2, S, W), BF16), jax.ShapeDtypeStruct((2, S, W), BF16),
                   jax.ShapeDtypeStruct((4, S, W), BF16)],
        compiler_params=_cparams(),
    )(dxb, wo, qkvg, qkvg, o, a)


def _fox_bwd(qaug, doaug, qkv, c, dqkvg, H, scatter=(), scatter_specs=()):
    na = len(scatter)
    S = qkv.shape[0]
    W = H * HEAD_DIM
    HP = H // 2
    tq = _row_tile(S, 512)
    nq = S // tq
    wb = W // LANE
    scale = HEAD_DIM ** -0.5

    def body(*refs):
        qa_ref, da_ref, k_ref, v_ref, c_ref = refs[:5]
        out_ref, dcr_ref, dcc_ref = refs[6 + na:9 + na]
        dq_sc, dk_sc, dv_sc = refs[9 + 2 * na:12 + 2 * na]
        hp, kj = pl.program_id(0), pl.program_id(1)
        if na:
            remote = _direct_scatter_copies(refs[5:5 + na], refs[9 + na:9 + 2 * na], scatter_specs,
                                            *refs[12 + 2 * na:])

            @pl.when((hp == 0) & (kj == 0))
            def _():
                for cp in remote:
                    cp.start()
        lane = lax.broadcasted_iota(jnp.int32, (tq, LANE), 1)
        own = [lane < HEAD_DIM, lane >= HEAD_DIM]
        rows = lax.broadcasted_iota(jnp.int32, (tq, tq), 0)
        cols = lax.broadcasted_iota(jnp.int32, (tq, tq), 1)

        @pl.when(kj == 0)
        def _():
            dq_sc[...] = jnp.zeros(dq_sc.shape, F32)

        @pl.when((kj == 0) & (hp == 0))
        def _():
            dcr_ref[...] = jnp.zeros(dcr_ref.shape, F32)
            dcc_ref[...] = jnp.zeros(dcc_ref.shape, F32)

        kblk, vblk, cblk = k_ref[...], v_ref[...], c_ref[...]
        one, zero = jnp.ones((tq, LANE), BF16), jnp.zeros((tq, LANE), BF16)
        ka, va = [], []
        for e in range(2):
            o0 = HEAD_DIM * (1 - e)
            c_col = jnp.sum(jnp.where(lane == 2 * hp + e, cblk, 0.0), axis=1, keepdims=True)
            ka.append(jnp.where(own[e], kblk, _bias_lanes(-c_col, lane, e, False)))
            va.append(jnp.where(own[e], vblk, jnp.where((lane >= o0) & (lane < o0 + 3), one, zero)))
        dk_sc[...] = jnp.zeros(dk_sc.shape, F32)
        dv_sc[...] = jnp.zeros(dv_sc.shape, F32)

        def step(i, masked):
            r0 = pl.multiple_of(i * tq, tq)
            for e in range(2):
                qa = qa_ref[e, pl.ds(r0, tq), :]
                da = da_ref[e, pl.ds(r0, tq), :]
                p = jnp.exp(_dot_nt(qa, ka[e]))
                if masked:
                    p = jnp.where(rows >= cols, p, 0.0)
                ds = p * _dot_nt(da, va[e])
                pb, dsb = p.astype(BF16), ds.astype(BF16)
                dv_sc[e] += _dot_tn(pb, da)
                dk_sc[e] += _dot_tn(dsb, qa)
                dq_sc[e, pl.ds(r0, tq), :] += _dot_nn(dsb, ka[e])

        step(kj, True)

        def loop_body(i, carry):
            step(i, False)
            return carry

        lax.fori_loop(kj + 1, nq, loop_body, 0)
        k0 = pl.multiple_of(kj * tq, tq)
        out_ref[1, pl.ds(k0, tq), :] = jnp.where(own[0], dk_sc[0], dk_sc[1]).astype(BF16)
        out_ref[2, pl.ds(k0, tq), :] = jnp.where(own[0], dv_sc[0], dv_sc[1]).astype(BF16)

        def put_lane(ref, r0, e, tile, src_lane):
            col = jnp.sum(jnp.where(lane == src_lane, tile, 0.0), axis=1, keepdims=True)
            ref[pl.ds(r0, tq), :] = jnp.where(lane == 2 * hp + e, col, ref[pl.ds(r0, tq), :])

        for e in range(2):
            put_lane(dcc_ref, k0, e, dk_sc[e], HEAD_DIM * (1 - e) + 3)

        @pl.when(kj == nq - 1)
        def _():
            def chunk(i, carry):
                r0 = pl.multiple_of(i * tq, tq)
                d0, d1 = dq_sc[0, pl.ds(r0, tq), :], dq_sc[1, pl.ds(r0, tq), :]
                out_ref[0, pl.ds(r0, tq), :] = (jnp.where(own[0], d0, d1) * scale).astype(BF16)
                put_lane(dcr_ref, r0, 0, d0, HEAD_DIM)
                put_lane(dcr_ref, r0, 1, d1, 0)
                return carry
            lax.fori_loop(0, nq, chunk, 0)

        if na:
            @pl.when((hp == HP - 1) & (kj == nq - 1))
            def _():
                _wait_all(remote)

    pair = pl.BlockSpec((2, S, LANE), lambda h, j: (0, 0, h))
    vec = pl.BlockSpec((S, LANE), lambda h, j: (0, 0))
    any_spec = pl.BlockSpec(memory_space=pl.ANY)
    sems = [pltpu.SemaphoreType.DMA((na, N_DEV - 1)), pltpu.SemaphoreType.DMA((na, N_DEV - 1))] if na else []
    outs = pl.pallas_call(
        body, name="fox_attn_bwd", grid=(HP, nq),
        in_specs=[pair, pair,
                  pl.BlockSpec((tq, LANE), lambda h, j: (j, wb + h)),
                  pl.BlockSpec((tq, LANE), lambda h, j: (j, 2 * wb + h)),
                  pl.BlockSpec((tq, LANE), lambda h, j: (j, 0))] + [any_spec] * (na + 1),
        out_specs=[pl.BlockSpec((3, S, LANE), lambda h, j: (0, 0, h)), vec, vec] + [any_spec] * na,
        out_shape=[jax.ShapeDtypeStruct(dqkvg.shape, BF16), jax.ShapeDtypeStruct((S, LANE), F32),
                   jax.ShapeDtypeStruct((S, LANE), F32)]
        + [jax.ShapeDtypeStruct((N_DEV - 1,) + _scatter_block_shape(g, s), g.dtype)
           for g, s in zip(scatter, scatter_specs)],
        scratch_shapes=[pltpu.VMEM((2, S, LANE), F32), pltpu.VMEM((2, tq, LANE), F32),
                        pltpu.VMEM((2, tq, LANE), F32)] + sems,
        input_output_aliases={5 + na: 0},
        compiler_params=_cparams(),
    )(qaug, doaug, qkv, qkv, c, *scatter, dqkvg)
    return outs[0], outs[1], outs[2], list(outs[3:])


def _swa_pick(blk, half, lane):
    b = blk.astype(F32)
    r = pltpu.roll(b, HEAD_DIM, axis=1)
    return jnp.where(jnp.logical_xor(lane < HEAD_DIM, half == 1), b, r).astype(BF16)


def _swa_stack(t, lane, G):
    pieces = []
    z = jnp.zeros((SWA_BLOCK, LANE), t.dtype)
    for j in range(G // 2):
        tile = t[:, LANE * j:LANE * (j + 1)]
        pieces += [jnp.where(lane < HEAD_DIM, tile, z), jnp.where(lane < HEAD_DIM, z, tile)]
    return jnp.concatenate(pieces, axis=0)


def _swa_unstack(st, lane, G):
    tiles = []
    for j in range(G // 2):
        a = st[2 * j * SWA_BLOCK:(2 * j + 1) * SWA_BLOCK]
        b = st[(2 * j + 1) * SWA_BLOCK:(2 * j + 2) * SWA_BLOCK]
        tiles.append(jnp.where(lane < HEAD_DIM, a, b))
    return jnp.concatenate(tiles, axis=1)


def _swa_mask_bias(G):
    R = G * SWA_BLOCK
    t_loc = jnp.arange(R)[:, None] % SWA_BLOCK
    j_loc = jnp.arange(2 * SWA_BLOCK)[None, :]
    diff = t_loc + SWA_BLOCK - j_loc
    band = (diff >= 0) & (diff < SWA_BLOCK)
    return jnp.stack([jnp.where(band & (j_loc >= SWA_BLOCK), 0.0, NEG_INF),
                      jnp.where(band, 0.0, NEG_INF)]).astype(F32)


def _swa_scores(q, kp, kc, vp, vc, srow, bias, half, head0, G):
    lane = lax.broadcasted_iota(jnp.int32, (SWA_BLOCK, LANE), 1)
    kk = jnp.concatenate([_swa_pick(kp, half, lane), _swa_pick(kc, half, lane)], axis=0)
    vv = jnp.concatenate([_swa_pick(vp, half, lane), _swa_pick(vc, half, lane)], axis=0)
    qstack = _swa_stack(q, lane, G) * jnp.asarray(HEAD_DIM ** -0.5, BF16)
    s = _dot_nt(qstack, kk) + bias
    R = G * SWA_BLOCK
    lane1 = lax.broadcasted_iota(jnp.int32, (1, LANE), 1)
    sink = jnp.concatenate(
        [jnp.broadcast_to(jnp.sum(jnp.where(lane1 == head0 + g, srow, 0.0), axis=1, keepdims=True), (SWA_BLOCK, LANE))
         for g in range(G)], axis=0)
    m = jnp.maximum(jnp.broadcast_to(jnp.max(s, axis=1, keepdims=True), (R, LANE)), sink)
    e = jnp.exp(s - jnp.tile(m, (1, 2)))
    es = jnp.exp(sink - m)
    inv = 1.0 / (jnp.broadcast_to(jnp.sum(e, axis=1, keepdims=True), (R, LANE)) + es)
    return qstack, kk, vv, e * jnp.tile(inv, (1, 2)), es * inv, lane


def _swa_fwd(qkv, gate, sinks, mask_bias, HQ, HKV):
    S = qkv.shape[0]
    G = HQ // HKV
    WQ, KVW = HQ * HEAD_DIM, HKV * HEAD_DIM
    nb = S // SWA_BLOCK
    GW = G * HEAD_DIM
    kb, vb = WQ // LANE, (WQ + KVW) // LANE

    def body(q_ref, kp_ref, kc_ref, vp_ref, vc_ref, g_ref, sink_ref, b_ref, y_ref, o_ref):
        pair = pl.program_id(0)
        for half in range(2):
            cols = slice(GW * half, GW * (half + 1))
            _, _, vv, p, _, lane = _swa_scores(q_ref[:, cols], kp_ref[...], kc_ref[...], vp_ref[...], vc_ref[...],
                                               sink_ref[...], b_ref[0], half, (2 * pair + half) * G, G)
            o = _swa_unstack(_dot_nn(p.astype(BF16), vv), lane, G)
            g = g_ref[:, cols].astype(F32)
            y_ref[:, cols] = (o * (g * _sigmoid(g))).astype(BF16)
            o_ref[:, cols] = o.astype(BF16)

    blk = lambda cb, prev: pl.BlockSpec(
        (SWA_BLOCK, LANE), lambda h, n, cb=cb, prev=prev: (jnp.maximum(n - prev, 0), cb + h))
    qspec = pl.BlockSpec((SWA_BLOCK, 2 * GW), lambda h, n: (n, h))
    return pl.pallas_call(
        body, name="swa_attn_fwd", grid=(HKV // 2, nb),
        in_specs=[qspec, blk(kb, 1), blk(kb, 0), blk(vb, 1), blk(vb, 0), qspec,
                  pl.BlockSpec((1, LANE), lambda h, n: (0, 0)),
                  pl.BlockSpec((1, G * SWA_BLOCK, 2 * SWA_BLOCK), lambda h, n: (jnp.minimum(n, 1), 0, 0))],
        out_specs=[qspec, qspec],
        out_shape=[jax.ShapeDtypeStruct((S, WQ), BF16), jax.ShapeDtypeStruct((S, WQ), BF16)],
        compiler_params=_cparams(),
    )(qkv, qkv, qkv, qkv, qkv, gate, sinks, mask_bias)


def _swa_bwd(qkv, dy, gate, o, sinks, tables, mask_bias, HQ, HKV):
    S = qkv.shape[0]
    G = HQ // HKV
    WQ, KVW = HQ * HEAD_DIM, HKV * HEAD_DIM
    nb = S // SWA_BLOCK
    GW = G * HEAD_DIM
    R = G * SWA_BLOCK
    kb, vb = WQ // LANE, (WQ + KVW) // LANE
    scale = HEAD_DIM ** -0.5
    assert G == 8

    def body(q_ref, kp_ref, kc_ref, vp_ref, vc_ref, dy_ref, g_ref, o_ref, sink_ref, t_ref, b_ref,
             dqg_ref, dkv_ref, dsink_ref, carry_sc):
        pair, n = pl.program_id(0), pl.program_id(1)

        @pl.when(n == 0)
        def _():
            carry_sc[...] = jnp.zeros(carry_sc.shape, F32)
            dsink_ref[...] = jnp.zeros(dsink_ref.shape, F32)

        @pl.when(n < nb)
        def _():
            t0, t1, t2 = (jnp.tile(t_ref[i], (1, GW // LANE)) for i in range(3))
            for half in range(2):
                cols = slice(GW * half, GW * (half + 1))
                qstack, kk, vv, p, psink, lane = _swa_scores(
                    q_ref[:, cols], kp_ref[...], kc_ref[...], vp_ref[...], vc_ref[...], sink_ref[...], b_ref[0],
                    half, (2 * pair + half) * G, G)
                dyv, g, ov = dy_ref[:, cols], g_ref[:, cols].astype(F32), o_ref[:, cols].astype(F32)
                sg = _sigmoid(g)
                dob = (dyv * (g * sg)).astype(BF16)
                dqg_ref[1, :, cols] = (dyv * ov * (sg * (1.0 + g * (1.0 - sg)))).astype(BF16)
                prod = dob.astype(F32) * ov
                dparts = []
                for j in range(G // 2):
                    tile = prod[:, LANE * j:LANE * (j + 1)]
                    for sel in (jnp.where(lane < HEAD_DIM, tile, 0.0), jnp.where(lane < HEAD_DIM, 0.0, tile)):
                        dparts.append(jnp.broadcast_to(jnp.sum(sel, axis=1, keepdims=True), (SWA_BLOCK, LANE)))
                delta = jnp.concatenate(dparts, axis=0)
                dostack = _swa_stack(dob, lane, G)
                ds = p * (_dot_nt(dostack, vv) - jnp.tile(delta, (1, 2)))
                dsb, pb = ds.astype(BF16), p.astype(BF16)
                dq = _swa_unstack(_dot_nn(dsb, kk), lane, G) * scale
                dq = dq * t0 + pltpu.roll(dq * t1, ROT_DIM // 2, axis=1) + pltpu.roll(dq * t2, GW - ROT_DIM // 2, axis=1)
                dqg_ref[0, :, cols] = dq.astype(BF16)
                dkk = _dot_tn(dsb, qstack)
                dvv = _dot_tn(pb, dostack)
                dkk = dkk + pltpu.roll(dkk, HEAD_DIM, axis=1)
                dvv = dvv + pltpu.roll(dvv, HEAD_DIM, axis=1)
                lane2 = lax.broadcasted_iota(jnp.int32, (2 * SWA_BLOCK, LANE), 1)
                comb = jnp.where(lane2 < HEAD_DIM, dkk, dvv)
                dkv_ref[half] = carry_sc[half] + comb[:SWA_BLOCK]
                carry_sc[half] = comb[SWA_BLOCK:]
                sk = psink * delta
                rows = [-jnp.sum(sk[g_ * SWA_BLOCK:(g_ + 1) * SWA_BLOCK], axis=0, keepdims=True) for g_ in range(G)]
                dsink_ref[half] += jnp.concatenate(rows, axis=0)

        @pl.when(n == nb)
        def _():
            dkv_ref[...] = carry_sc[...]

    cl = lambda n: jnp.minimum(n, nb - 1)
    blk = lambda cb, prev: pl.BlockSpec(
        (SWA_BLOCK, LANE), lambda h, n, cb=cb, prev=prev: (jnp.maximum(cl(n) - prev, 0), cb + h))
    qspec = pl.BlockSpec((SWA_BLOCK, 2 * GW), lambda h, n: (cl(n), h))
    return pl.pallas_call(
        body, name="swa_attn_bwd", grid=(HKV // 2, nb + 1),
        in_specs=[qspec, blk(kb, 1), blk(kb, 0), blk(vb, 1), blk(vb, 0), qspec, qspec, qspec,
                  pl.BlockSpec((1, LANE), lambda h, n: (0, 0)),
                  pl.BlockSpec((3, SWA_BLOCK, LANE), lambda h, n: (0, cl(n), 0)),
                  pl.BlockSpec((1, R, 2 * SWA_BLOCK), lambda h, n: (jnp.minimum(n, 1), 0, 0))],
        out_specs=[pl.BlockSpec((2, SWA_BLOCK, 2 * GW), lambda h, n: (0, cl(n), h)),
                   pl.BlockSpec((2, SWA_BLOCK, LANE), lambda h, n: (h, jnp.maximum(n - 1, 0), 0)),
                   pl.BlockSpec((2, 8, LANE), lambda h, n: (h, 0, 0))],
        out_shape=[jax.ShapeDtypeStruct((2, S, WQ), BF16), jax.ShapeDtypeStruct((HKV, S, LANE), F32),
                   jax.ShapeDtypeStruct((HKV, 8, LANE), F32)],
        scratch_shapes=[pltpu.VMEM((2, SWA_BLOCK, LANE), F32)],
        compiler_params=_cparams(),
    )(qkv, qkv, qkv, qkv, qkv, dy, gate, o, sinks, tables, mask_bias)


def _swa_dkv_finish(dkv, tables):
    HKV, S, _ = dkv.shape
    KVW = HKV * HEAD_DIM
    tm = _row_tile(S, 512)
    npair = HKV // 2

    def body(d_ref, t_ref, o_ref):
        lane = lax.broadcasted_iota(jnp.int32, (tm, LANE), 1)
        lo = lane < HEAD_DIM
        for p in range(npair):
            a, b = d_ref[2 * p], d_ref[2 * p + 1]
            tk = jnp.where(lo, a, pltpu.roll(b, HEAD_DIM, axis=1))
            tv = jnp.where(lo, pltpu.roll(a, HEAD_DIM, axis=1), b)
            tk = (tk * t_ref[0] + pltpu.roll(tk * t_ref[1], ROT_DIM // 2, axis=1)
                  + pltpu.roll(tk * t_ref[2], LANE - ROT_DIM // 2, axis=1))
            o_ref[:, LANE * p:LANE * (p + 1)] = tk.astype(BF16)
            o_ref[:, KVW + LANE * p:KVW + LANE * (p + 1)] = tv.astype(BF16)

    return pl.pallas_call(
        body, name="swa_dkv_finish", grid=(S // tm,),
        in_specs=[pl.BlockSpec((HKV, tm, LANE), lambda i: (0, i, 0)), pl.BlockSpec((3, tm, LANE), lambda i: (0, i, 0))],
        out_specs=pl.BlockSpec((tm, 2 * KVW), lambda i: (i, 0)),
        out_shape=jax.ShapeDtypeStruct((S, 2 * KVW), BF16),
        compiler_params=_cparams(),
    )(dkv, tables)


def _rope_tables(S, width):
    half = ROT_DIM // 2
    pos = jnp.arange(S, dtype=F32)
    inv_freq = ROPE_THETA ** (-jnp.arange(half, dtype=F32) / half)
    ang = pos[:, None] * inv_freq[None, :]
    cos, sin = jnp.cos(ang), jnp.sin(ang)
    one = jnp.ones((S, HEAD_DIM - ROT_DIM), F32)
    zero = jnp.zeros((S, HEAD_DIM - ROT_DIM), F32)
    zh = jnp.zeros((S, half), F32)
    t0 = jnp.concatenate([cos, cos, one], axis=1)
    t1 = jnp.concatenate([-sin, zh, zero], axis=1)
    t2 = jnp.concatenate([zh, sin, zero], axis=1)
    return jnp.stack([jnp.tile(t, (1, width // HEAD_DIM)) for t in (t0, t1, t2)])


def _pad_rows(v, row, total_rows=8):
    return jnp.pad(v, ((row, total_rows - row - v.shape[0]), (0, 0)))


def _pad_lanes(v, off, width):
    return jnp.pad(v, ((0, 0), (off, width - off - v.shape[1])))


def kernel(x, norm_g, fox_w_in, fox_b_f, fox_w_out, swa_w_in, swa_sinks, swa_w_out, final_g, loss_target, m_norm_g, m_fox_w_in, m_fox_b_f, m_fox_w_out, m_swa_w_in, m_swa_sinks, m_swa_w_out, m_final_g, v_norm_g, v_fox_w_in, v_fox_b_f, v_fox_w_out, v_swa_w_in, v_swa_sinks, v_swa_w_out, v_final_g):
    S, D = x.shape[1], x.shape[2]
    H = fox_b_f.shape[1]
    W = H * HEAD_DIM
    wf = fox_w_in.shape[2]
    ws = swa_w_in.shape[2]
    HQ = swa_sinks.shape[1]
    WQ = HQ * HEAD_DIM
    KVW = (ws * N_DEV - 2 * WQ) // 2
    HKV = KVW // HEAD_DIM
    rows_o = fox_w_out.shape[1]
    assert wf * N_DEV == 4 * W + H and rows_o * N_DEV == W and H <= LANE and HQ <= LANE
    me = _my_index()

    _, sw_f, np_f = _slab_geom(wf)
    _, sw_s, np_s = _slab_geom(ws)

    def slab(w2d, w, sw):
        return jnp.pad(w2d.astype(BF16), ((0, 0), (0, sw - w)))

    (fi_all,) = _all_gather([slab(fox_w_in[0], wf, sw_f)])
    w_fi = _assemble(fi_all, wf)
    later = [slab(swa_w_in[0], ws, sw_s), fox_w_out[0].astype(BF16), swa_w_out[0].astype(BF16)]

    x0 = x[0]
    g0, g1, gf = norm_g[0:1], norm_g[1:2], final_g[None, :]
    bias = _pad_lanes(fox_b_f, 0, LANE)
    sinks = _pad_lanes(swa_sinks, 0, LANE)
    tab_k = _rope_tables(S, LANE)
    mask_bias = _swa_mask_bias(HQ // HKV)

    h0 = _rmsnorm_fwd(x0, g0, "rmsnorm0")
    qkv0 = _proj(h0, w_fi, 0, 4 * W, BF16, "fox_in_qkvg")
    fl = _proj(h0, w_fi, 4 * W, LANE, F32, "fox_in_f")
    c = _fox_gate_fwd(fl, bias)
    y0, o0, a0, (si_all, fo_all, so_all) = _fox_fwd(qkv0, c, H, gather=later)
    w_si = _assemble(si_all, ws)
    w_fo = fo_all.reshape(W, D)
    w_so = so_all.reshape(WQ, D)
    x1, h1 = _out_proj_norm(y0, w_fo, x0, g1, "fox_out")

    qkv1 = _proj(h1, w_si, 0, WQ + 2 * KVW, BF16, "swa_in_qkv", rope=(tab_k, WQ + KVW))
    gate1 = _proj(h1, w_si, WQ + 2 * KVW, WQ, BF16, "swa_in_gate")
    y1, o1 = _swa_fwd(qkv1, gate1, sinks, mask_bias, HQ, HKV)
    dx2, dx2b, dgf, loss_p = _out_proj_loss(y1, w_so, x1, loss_target[0], gf, "swa_out_loss")

    dy1 = _matmul_nt([(dx2b, None, 0)], w_so, WQ, "swa_out_bwd")
    g_so, g_so_h = _matmul_tn(y1, [(dx2b, None, 0)], D, "swa_out_wgrad", also_bf16=True)
    dqg1, dkv1, dsink = _swa_bwd(qkv1, dy1, gate1, o1, sinks, tab_k, mask_bias, HQ, HKV)
    dkv1f = _swa_dkv_finish(dkv1, tab_k)
    parts1 = [(dqg1, 0, 0), (dkv1f, None, WQ), (dqg1, 1, WQ + 2 * KVW)]
    g_si, g_si_h = _matmul_tn(h1, parts1, np_s, "swa_in_wgrad", tile_major=True, also_bf16=True)
    dh1 = _matmul_nt(parts1, w_si, D, "swa_in_bwd")
    dx1, dx1b, dg1 = _rmsnorm_bwd(dh1, x1, g1, dx2, "rmsnorm1_bwd")

    qaug0, doaug0, dqkvg0 = _fox_out_bwd(dx1b, w_fo, qkv0, o0, a0, H)
    g_fo, g_fo_h = _matmul_tn(y0, [(dx1b, None, 0)], D, "fox_out_wgrad", also_bf16=True)
    early_specs = [("col", ws), ("row", rows_o), ("row", rows_o)]
    dqkvg0, dcr, dcc, early_recv = _fox_bwd(qaug0, doaug0, qkv0, c, dqkvg0, H, scatter=[g_si_h, g_fo_h, g_so_h],
                                           scatter_specs=early_specs)
    dfl, dbf = _fox_gate_bwd(fl, bias, dcr - dcc)
    parts0 = [(dqkvg0, "stack", 0), (dfl, None, 4 * W)]
    g_fi, g_fi_h = _matmul_tn(h0, parts0, np_f, "fox_in_wgrad", tile_major=True, also_bf16=True)
    spec_fi = ("col", wf)
    fi_sems, fi_src, fi_land, token = _scatter_start(g_fi_h, spec_fi)
    parts0[-1] = (dfl + token[0, 0].astype(BF16), None, 4 * W)
    dh0 = _matmul_nt(parts0, w_fi, D, "fox_in_bwd")
    dx0, _, dg0 = _rmsnorm_bwd(dh0, x0, g0, dx1, "rmsnorm0_bwd")

    red_si, gw_fo, gw_so = [_final_sum8(g_, r_, s_)
                            for g_, r_, s_ in zip([g_si, g_fo, g_so], early_recv, early_specs)]
    gw_si = lax.dynamic_slice(red_si, (0, (ws * me) % LANE), (D, ws))

    P = D
    dsink_v = dsink[:, :, 0].reshape(1, HQ)
    row3 = _pad_lanes(dbf[:, :H], 0, P) + _pad_lanes(dsink_v, LANE, P) + _pad_lanes(loss_p[:, :1], 2 * LANE, P)
    pack = _pad_rows(dg0, 0) + _pad_rows(dg1, 1) + _pad_rows(dgf, 2) + _pad_rows(row3, 3)

    d_fo, m_fo, v_fo = _adamw(fox_w_out[0], gw_fo, m_fox_w_out[0], v_fox_w_out[0], "adamw_fox_out")
    d_si, m_si, v_si = _adamw(swa_w_in[0], gw_si, m_swa_w_in[0], v_swa_w_in[0], "adamw_swa_in")
    d_so, m_so, v_so = _adamw(swa_w_out[0], gw_so, m_swa_w_out[0], v_swa_w_out[0], "adamw_swa_out")
    recv_fi = _scatter_wait(fi_sems, fi_src, fi_land, spec_fi, after=[dx0, pack, d_fo, d_si, d_so])
    red_fi = _final_sum8(g_fi, recv_fi, spec_fi)
    gw_fi = lax.dynamic_slice(red_fi, (0, (wf * me) % LANE), (D, wf))
    d_fi, m_fi, v_fi = _adamw(fox_w_in[0], gw_fi, m_fox_w_in[0], v_fox_w_in[0], "adamw_fox_in")

    tot = _all_reduce_small(pack, after=recv_fi)
    loss = tot[3, 2 * LANE]
    g_norm = tot[0:2]
    g_final = tot[2]
    g_bf = tot[3:4, 0:H]
    g_sinks = tot[3:4, LANE:LANE + HQ]

    def small_pack(ng, fg, bf, sk):
        r3 = _pad_lanes(bf, 0, P) + _pad_lanes(sk, LANE, P)
        return _pad_rows(ng, 0) + _pad_rows(fg[None, :], 2) + _pad_rows(r3, 3)

    sd, sm, sv = _adamw(small_pack(norm_g, final_g, fox_b_f, swa_sinks), tot,
                        small_pack(m_norm_g, m_final_g, m_fox_b_f, m_swa_sinks),
                        small_pack(v_norm_g, v_final_g, v_fox_b_f, v_swa_sinks), "adamw_small")

    def unpack(t):
        return t[0:2], t[3:4, 0:H], t[3:4, LANE:LANE + HQ], t[2]

    def group(small, fi, fo, si, so):
        ng, bf, sk, fg = unpack(small)
        return (ng, fi[None], bf, fo[None], si[None], sk, so[None], fg)

    grads = (g_norm, gw_fi[None], g_bf, gw_fo[None], gw_si[None], g_sinks, gw_so[None], g_final)
    return (loss, dx0[None], *grads, *group(sd, d_fi, d_fo, d_si, d_so),
            *group(sm, m_fi, m_fo, m_si, m_so), *group(sv, v_fi, v_fo, v_si, v_so))
```

```python
import math

import jax
import jax.numpy as jnp
from jax import lax
from jax.experimental import pallas as pl
from jax.experimental.pallas import tpu as pltpu

F32 = jnp.float32
BF16 = jnp.bfloat16
MESH = pl.DeviceIdType.MESH

N_DEV = 8
LANE = 128
HEAD_DIM = 64
SWA_BLOCK = 128
NEG_INF = -1e30
RMS_EPS = 1e-6
ROPE_THETA = 500000.0
ROT_DIM = HEAD_DIM // 4
ADAM_LR, ADAM_B1, ADAM_B2, ADAM_EPS, ADAM_WD, ADAM_STEP = 0.001, 0.9, 0.999, 1e-08, 0.01, 10
VMEM_LIMIT = 56 * 1024 * 1024
MM_TILE = 1024


def _cparams(**kw):
    return pltpu.CompilerParams(vmem_limit_bytes=VMEM_LIMIT, **kw)


def _tile(n, cap):
    if n <= cap:
        return n
    t = (cap // LANE) * LANE
    while t > LANE and n % t:
        t -= LANE
    assert n % t == 0, (n, cap)
    return t


def _row_tile(n, cap):
    t = min(n, cap)
    while n % t:
        t //= 2
    return t


def _dot_nn(a, b):
    return jnp.dot(a, b, preferred_element_type=F32)


def _dot_nt(a, b):
    return lax.dot_general(a, b, (((1,), (1,)), ((), ())), preferred_element_type=F32)


def _dot_tn(a, b):
    return lax.dot_general(a, b, (((0,), (0,)), ((), ())), preferred_element_type=F32)


def _sigmoid(g):
    return 1.0 / (1.0 + jnp.exp(-g))


def _slab_geom(w):
    starts = [w * i for i in range(N_DEV)]
    aligned = [LANE * (s // LANE) for s in starts]
    offs = [s - a for s, a in zip(starts, aligned)]
    sw = LANE * (-(-(max(offs) + w) // LANE))
    return aligned, sw, aligned[-1] + sw


def _my_index():
    return 4 * lax.axis_index("x") + 2 * lax.axis_index("y") + lax.axis_index("c")


def _all_gather(arrs):
    n = len(arrs)

    def body(*refs):
        ins, outs = refs[:n], refs[n:2 * n]
        send_sems, recv_sems, local_sems = refs[2 * n:]
        x, y, c = lax.axis_index("x"), lax.axis_index("y"), lax.axis_index("c")
        me, sib = (x, y, c), (x, y, 1 - c)
        chips = [(1 - x, y), (x, 1 - y), (1 - x, 1 - y)]

        def idx(px, py, pc):
            return 4 * px + 2 * py + pc

        def copy(a, k, block, to, src=None):
            dst = outs[a].at[idx(*block)]
            return pltpu.make_async_remote_copy(
                src_ref=dst if src is None else src, dst_ref=dst,
                send_sem=send_sems.at[a, k], recv_sem=recv_sems.at[a, k],
                device_id=to, device_id_type=MESH)

        mine = [pltpu.make_async_copy(ins[a], outs[a].at[idx(*me)], local_sems.at[a]) for a in range(n)]
        for m in mine:
            m.start()
        first = []
        for a in range(n):
            first.append(copy(a, 0, me, sib, src=ins[a]))
            for j, chip in enumerate(chips):
                first.append(copy(a, 1 + j, me, (*chip, c), src=ins[a]))
        for cp in first:
            cp.start()
        passed = []
        for j, chip in enumerate(chips):
            for a in range(n):
                copy(a, 1 + j, (*chip, c), me).wait_recv()
                p = copy(a, 4 + j, (*chip, c), sib)
                p.start()
                passed.append(p)
        for a in range(n):
            copy(a, 0, sib, me).wait_recv()
        for j, chip in enumerate(chips):
            for a in range(n):
                copy(a, 4 + j, (*chip, 1 - c), me).wait_recv()
        for cp in first + passed:
            cp.wait_send()
        for m in mine:
            m.wait()

    any_spec = pl.BlockSpec(memory_space=pl.ANY)
    return pl.pallas_call(
        body, name="weights_all_gather",
        out_shape=[jax.ShapeDtypeStruct((N_DEV,) + a.shape, a.dtype) for a in arrs],
        in_specs=[any_spec] * n, out_specs=[any_spec] * n,
        scratch_shapes=[pltpu.SemaphoreType.DMA((n, 7)), pltpu.SemaphoreType.DMA((n, 7)),
                        pltpu.SemaphoreType.DMA((n,))],
    )(*arrs)


def _rs_windows(specs):
    def window(ref, spec, blk):
        kind, n = spec
        if kind == "col":
            _, sw, _ = _slab_geom(n)
            return ref.at[pl.ds((n * blk) // LANE, sw // LANE)]
        start = pl.multiple_of(n * blk, n)
        return ref.at[pl.ds(start, n), :]
    return window


def _peer(k):
    x, y, c = lax.axis_index("x"), lax.axis_index("y"), lax.axis_index("c")
    return (x ^ (k >> 2), y ^ ((k >> 1) & 1), c ^ (k & 1))


def _direct_gather_copies(ins, outs, send_sems, recv_sems, local_sems):
    me = _my_index()
    remote, local = [], []
    for a, (src, dst) in enumerate(zip(ins, outs)):
        local.append(pltpu.make_async_copy(src, dst.at[me], local_sems.at[a]))
        for k in range(1, N_DEV):
            remote.append(pltpu.make_async_remote_copy(
                src_ref=src, dst_ref=dst.at[me], send_sem=send_sems.at[a, k - 1], recv_sem=recv_sems.at[a, k - 1],
                device_id=_peer(k), device_id_type=MESH))
    return remote, local


def _direct_scatter_copies(ins, outs, specs, send_sems, recv_sems):
    window = _rs_windows(specs)
    remote = []
    for a, (src, dst) in enumerate(zip(ins, outs)):
        for k in range(1, N_DEV):
            px, py, pc = _peer(k)
            remote.append(pltpu.make_async_remote_copy(
                src_ref=window(src, specs[a], 4 * px + 2 * py + pc), dst_ref=dst.at[k - 1],
                send_sem=send_sems.at[a, k - 1], recv_sem=recv_sems.at[a, k - 1],
                device_id=(px, py, pc), device_id_type=MESH))
    return remote


def _scatter_block_shape(g, spec):
    kind, w = spec
    return (_slab_geom(w)[1] // LANE, g.shape[1], LANE) if kind == "col" else (w, g.shape[1])


def _wait_all(remote, local=()):
    for cp in remote:
        cp.wait_recv()
    for cp in remote:
        cp.wait_send()
    for cp in local:
        cp.wait()


def _scatter_start(g, spec):
    blk = _scatter_block_shape(g, spec)
    window = _rs_windows([spec])
    npeer = N_DEV - 1

    def body(g_ref, land_ref, *rest):
        sems = rest[:2 * npeer]
        token = rest[2 * npeer + 2]
        for cp in _peer_block_copies(g_ref, land_ref, spec, window, sems[:npeer], sems[npeer:]):
            cp.start()
        token[...] = jnp.zeros(token.shape, token.dtype)

    hbm = pl.BlockSpec(memory_space=pltpu.HBM)
    sem = pl.BlockSpec(memory_space=pltpu.SEMAPHORE)
    land = lax.empty((npeer,) + blk, g.dtype)
    outs = pl.pallas_call(
        body, name="grads_scatter_start",
        out_shape=(pltpu.SemaphoreType.DMA(()),) * (2 * npeer)
        + (pltpu.HBM(g.shape, g.dtype), pltpu.HBM(land.shape, land.dtype), jax.ShapeDtypeStruct((8, LANE), F32)),
        in_specs=(hbm, hbm),
        out_specs=(sem,) * (2 * npeer) + (hbm, hbm, pl.BlockSpec(memory_space=pltpu.VMEM)),
        input_output_aliases={0: 2 * npeer, 1: 2 * npeer + 1},
        compiler_params=pltpu.CompilerParams(has_side_effects=pltpu.SideEffectType.DATAFLOW_SIDE_EFFECTING),
    )(pltpu.with_memory_space_constraint(g, pltpu.HBM), pltpu.with_memory_space_constraint(land, pltpu.HBM))
    return outs[:2 * npeer], outs[2 * npeer], outs[2 * npeer + 1], outs[2 * npeer + 2]


def _peer_block_copies(g_ref, land_ref, spec, window, send_sems, recv_sems):
    copies = []
    for k in range(1, N_DEV):
        px, py, pc = _peer(k)
        copies.append(pltpu.make_async_remote_copy(
            src_ref=window(g_ref, spec, 4 * px + 2 * py + pc), dst_ref=land_ref.at[k - 1],
            send_sem=send_sems[k - 1], recv_sem=recv_sems[k - 1], device_id=(px, py, pc), device_id_type=MESH))
    return copies


def _scatter_wait(sems, g_thru, land_thru, spec, after):
    window = _rs_windows([spec])
    npeer = N_DEV - 1

    def body(g_ref, land_ref, *rest):
        s = rest[:2 * npeer]
        copies = _peer_block_copies(g_ref, land_ref, spec, window, s[:npeer], s[npeer:])
        for cp in copies:
            cp.wait_send()
        for cp in copies:
            cp.wait_recv()

    hbm = pl.BlockSpec(memory_space=pltpu.HBM)
    sem = pl.BlockSpec(memory_space=pltpu.SEMAPHORE)
    return pl.pallas_call(
        body, name="grads_scatter_wait",
        out_shape=(pltpu.HBM(g_thru.shape, g_thru.dtype), pltpu.HBM(land_thru.shape, land_thru.dtype)),
        in_specs=(hbm, hbm) + (sem,) * (2 * npeer) + (pl.BlockSpec(memory_space=pl.ANY),) * len(after),
        out_specs=(hbm, hbm), input_output_aliases={0: 0, 1: 1},
        compiler_params=pltpu.CompilerParams(has_side_effects=pltpu.SideEffectType.DATAFLOW_SIDE_EFFECTING),
    )(g_thru, land_thru, *sems, *after)[1]


def _final_sum8(g, recv, spec):
    kind, n = spec
    me = _my_index()
    offs = jnp.stack([(n * me) // LANE if kind == "col" else me]).astype(jnp.int32)
    if kind == "col":
        _, T, M, _ = recv.shape
        grid = (T,)
        in_specs = [pl.BlockSpec((1, M, LANE), lambda t, o: (o[0] + t, 0, 0)),
                    pl.BlockSpec((N_DEV - 1, 1, M, LANE), lambda t, o: (0, t, 0, 0))]
        out_spec = pl.BlockSpec((M, LANE), lambda t, o: (0, t))
        out_shape = jax.ShapeDtypeStruct((M, T * LANE), F32)
    else:
        _, nrow, C = recv.shape
        grid = (1,)
        in_specs = [pl.BlockSpec((nrow, C), lambda t, o: (o[0], 0)),
                    pl.BlockSpec((N_DEV - 1, nrow, C), lambda t, o: (0, 0, 0))]
        out_spec = pl.BlockSpec((nrow, C), lambda t, o: (0, 0))
        out_shape = jax.ShapeDtypeStruct((nrow, C), F32)

    def body(o_ref, g_ref, r_ref, out_ref):
        acc = g_ref[0] if kind == "col" else g_ref[...]
        for k in range(N_DEV - 1):
            acc = acc + (r_ref[k, 0] if kind == "col" else r_ref[k]).astype(F32)
        out_ref[...] = acc

    return pl.pallas_call(
        body, name="grads_final_sum8",
        grid_spec=pltpu.PrefetchScalarGridSpec(num_scalar_prefetch=1, grid=grid, in_specs=in_specs,
                                               out_specs=out_spec),
        out_shape=out_shape, compiler_params=_cparams(),
    )(offs, g, recv)


def _all_reduce_small(pack, after):
    R, P = pack.shape

    def body(x_ref, after_ref, o_ref, gat_ref, send_sems, recv_sems):
        x, y, c = lax.axis_index("x"), lax.axis_index("y"), lax.axis_index("c")
        me = 4 * x + 2 * y + c
        gat_ref[me] = x_ref[...]
        copies = []
        for k in range(1, N_DEV):
            peer = (x ^ (k >> 2), y ^ ((k >> 1) & 1), c ^ (k & 1))
            copies.append(pltpu.make_async_remote_copy(
                src_ref=x_ref, dst_ref=gat_ref.at[me],
                send_sem=send_sems.at[k - 1], recv_sem=recv_sems.at[k - 1],
                device_id=peer, device_id_type=MESH))
        for cp in copies:
            cp.start()
        for cp in copies:
            cp.wait_recv()
        for cp in copies:
            cp.wait_send()
        acc = gat_ref[0]
        for d in range(1, N_DEV):
            acc = acc + gat_ref[d]
        o_ref[...] = acc

    vm = pl.BlockSpec(memory_space=pltpu.VMEM)
    return pl.pallas_call(
        body, name="small_all_reduce",
        out_shape=jax.ShapeDtypeStruct((R, P), F32),
        in_specs=[vm, pl.BlockSpec(memory_space=pl.ANY)], out_specs=vm,
        scratch_shapes=[pltpu.VMEM((N_DEV, R, P), F32),
                        pltpu.SemaphoreType.DMA((N_DEV - 1,)), pltpu.SemaphoreType.DMA((N_DEV - 1,))],
    )(pack, after)


def _assemble(slabs, w):
    aligned, sw, total = _slab_geom(w)
    K = slabs.shape[1]
    tr = _row_tile(K, 256)

    def body(s_ref, o_ref):
        o_ref[...] = jnp.zeros(o_ref.shape, BF16)
        for i in range(N_DEV):
            a, off = aligned[i], w * i - aligned[i]
            x = s_ref[i].astype(F32)
            if off:
                x = pltpu.roll(x, off, axis=1)
            o_ref[:, a:a + sw] = (o_ref[:, a:a + sw].astype(F32) + x).astype(BF16)

    return pl.pallas_call(
        body, name="assemble_w_in", grid=(K // tr,),
        in_specs=[pl.BlockSpec((N_DEV, tr, sw), lambda i: (0, i, 0))],
        out_specs=pl.BlockSpec((tr, total), lambda i: (i, 0)),
        out_shape=jax.ShapeDtypeStruct((K, total), BF16),
        compiler_params=_cparams(),
    )(slabs)


def _rmsnorm_fwd(x, g, name):
    S, D = x.shape
    tm = _row_tile(S, 256)

    def body(x_ref, g_ref, h_ref):
        xv = x_ref[...]
        r = lax.rsqrt(jnp.mean(xv * xv, axis=-1, keepdims=True) + RMS_EPS)
        h_ref[...] = ((xv * r) * g_ref[...]).astype(BF16)

    return pl.pallas_call(
        body, name=name, grid=(S // tm,),
        in_specs=[pl.BlockSpec((tm, D), lambda i: (i, 0)), pl.BlockSpec((1, D), lambda i: (0, 0))],
        out_specs=pl.BlockSpec((tm, D), lambda i: (i, 0)),
        out_shape=jax.ShapeDtypeStruct((S, D), BF16),
        compiler_params=_cparams(),
    )(x, g)


def _rmsnorm_bwd(dh, x, g, dres, name):
    S, D = x.shape
    tm = _row_tile(S, 256)

    def body(dh_ref, x_ref, g_ref, dr_ref, dx_ref, dxb_ref, dg_ref):
        xv = x_ref[...]
        r = lax.rsqrt(jnp.mean(xv * xv, axis=-1, keepdims=True) + RMS_EPS)
        xhat = xv * r
        d = dh_ref[...]
        gd = d * g_ref[...]
        dx = r * (gd - xhat * jnp.mean(gd * xhat, axis=-1, keepdims=True)) + dr_ref[...]
        dx_ref[...] = dx
        dxb_ref[...] = dx.astype(BF16)

        @pl.when(pl.program_id(0) == 0)
        def _():
            dg_ref[...] = jnp.zeros(dg_ref.shape, F32)
        dg_ref[...] += jnp.sum(d * xhat, axis=0, keepdims=True)

    row = pl.BlockSpec((tm, D), lambda i: (i, 0))
    vec = pl.BlockSpec((1, D), lambda i: (0, 0))
    return pl.pallas_call(
        body, name=name, grid=(S // tm,),
        in_specs=[row, row, vec, row], out_specs=[row, row, vec],
        out_shape=[jax.ShapeDtypeStruct((S, D), F32), jax.ShapeDtypeStruct((S, D), BF16),
                   jax.ShapeDtypeStruct((1, D), F32)],
        compiler_params=_cparams(),
    )(dh, x, g, dres)


def _adamw(w, g, m, v, name):
    R, C = w.shape
    tr = _row_tile(R, 256)
    c1 = 1.0 - ADAM_B1 ** ADAM_STEP
    c2 = 1.0 - ADAM_B2 ** ADAM_STEP

    def body(w_ref, g_ref, m_ref, v_ref, d_ref, nm_ref, nv_ref):
        gv = g_ref[...]
        nm = ADAM_B1 * m_ref[...] + (1.0 - ADAM_B1) * gv
        nv = ADAM_B2 * v_ref[...] + (1.0 - ADAM_B2) * (gv * gv)
        d_ref[...] = -ADAM_LR * ((nm / c1) / (jnp.sqrt(nv / c2) + ADAM_EPS) + ADAM_WD * w_ref[...])
        nm_ref[...] = nm
        nv_ref[...] = nv

    spec = pl.BlockSpec((tr, C), lambda i: (i, 0))
    return pl.pallas_call(
        body, name=name, grid=(R // tr,),
        in_specs=[spec] * 4, out_specs=[spec] * 3,
        out_shape=[jax.ShapeDtypeStruct((R, C), F32)] * 3,
        compiler_params=_cparams(),
    )(w, g, m, v)


def _proj(h, wfull, col0, ncols, out_dtype, name, rope=None):
    S, K = h.shape
    tm = _row_tile(S, MM_TILE)
    tn = math.gcd(_tile(ncols, MM_TILE), col0) if col0 else _tile(ncols, MM_TILE)
    if rope is not None:
        tn = _tile(math.gcd(ncols, rope[1]), MM_TILE)
    assert ncols % tn == 0 and col0 % tn == 0
    cb = col0 // tn

    def body(*refs):
        if rope is None:
            a_ref, b_ref, o_ref = refs
        else:
            a_ref, b_ref, t_ref, o_ref = refs
        acc = _dot_nn(a_ref[...], b_ref[...])
        if rope is not None:
            t0, t1, t2 = (jnp.tile(t_ref[i], (1, tn // LANE)) for i in range(3))
            roped = (acc * t0 + pltpu.roll(acc, tn - ROT_DIM // 2, axis=1) * t1
                     + pltpu.roll(acc, ROT_DIM // 2, axis=1) * t2)
            acc = jnp.where(pl.program_id(1) < rope[1] // tn, roped, acc)
        o_ref[...] = acc.astype(out_dtype)

    in_specs = [pl.BlockSpec((tm, K), lambda i, j: (i, 0)), pl.BlockSpec((K, tn), lambda i, j: (0, cb + j))]
    args = [h, wfull]
    if rope is not None:
        in_specs.append(pl.BlockSpec((3, tm, LANE), lambda i, j: (0, i, 0)))
        args.append(rope[0])
    return pl.pallas_call(
        body, name=name, grid=(S // tm, ncols // tn),
        in_specs=in_specs, out_specs=pl.BlockSpec((tm, tn), lambda i, j: (i, j)),
        out_shape=jax.ShapeDtypeStruct((S, ncols), out_dtype),
        compiler_params=_cparams(),
    )(*args)


def _out_proj_norm(y, wo, xres, g, name):
    S, W = y.shape
    D = wo.shape[1]
    tm = _row_tile(S, 512)

    def body(a_ref, b_ref, r_ref, g_ref, x_ref, h_ref):
        xv = r_ref[...] + _dot_nn(a_ref[...], b_ref[...])
        x_ref[...] = xv
        r = lax.rsqrt(jnp.mean(xv * xv, axis=-1, keepdims=True) + RMS_EPS)
        h_ref[...] = ((xv * r) * g_ref[...]).astype(BF16)

    row = pl.BlockSpec((tm, D), lambda i: (i, 0))
    return pl.pallas_call(
        body, name=name, grid=(S // tm,),
        in_specs=[pl.BlockSpec((tm, W), lambda i: (i, 0)), pl.BlockSpec((W, D), lambda i: (0, 0)), row,
                  pl.BlockSpec((1, D), lambda i: (0, 0))],
        out_specs=[row, row],
        out_shape=[jax.ShapeDtypeStruct((S, D), F32), jax.ShapeDtypeStruct((S, D), BF16)],
        compiler_params=_cparams(),
    )(y, wo, xres, g)


def _out_proj_loss(y, wo, xres, tgt, g, name):
    S, W = y.shape
    D = wo.shape[1]
    tm = _row_tile(S, 512)

    def body(a_ref, b_ref, r_ref, t_ref, g_ref, dx_ref, dxb_ref, dg_ref, loss_ref):
        xv = r_ref[...] + _dot_nn(a_ref[...], b_ref[...])
        r = lax.rsqrt(jnp.mean(xv * xv, axis=-1, keepdims=True) + RMS_EPS)
        xhat = xv * r
        gv = g_ref[...]
        err = xhat * gv - t_ref[...]
        d = err * (1.0 / D)
        gd = d * gv
        dx = r * (gd - xhat * jnp.mean(gd * xhat, axis=-1, keepdims=True))
        dx_ref[...] = dx
        dxb_ref[...] = dx.astype(BF16)

        @pl.when(pl.program_id(0) == 0)
        def _():
            dg_ref[...] = jnp.zeros(dg_ref.shape, F32)
            loss_ref[...] = jnp.zeros(loss_ref.shape, F32)
        dg_ref[...] += jnp.sum(d * xhat, axis=0, keepdims=True)
        per_tok = jnp.sum(err * err, axis=-1, keepdims=True) * (1.0 / D)
        loss_ref[...] += 0.5 * jnp.sum(per_tok, axis=0, keepdims=True)

    row = pl.BlockSpec((tm, D), lambda i: (i, 0))
    vec = pl.BlockSpec((1, D), lambda i: (0, 0))
    return pl.pallas_call(
        body, name=name, grid=(S // tm,),
        in_specs=[pl.BlockSpec((tm, W), lambda i: (i, 0)), pl.BlockSpec((W, D), lambda i: (0, 0)), row, row, vec],
        out_specs=[row, row, vec, pl.BlockSpec((1, LANE), lambda i: (0, 0))],
        out_shape=[jax.ShapeDtypeStruct((S, D), F32), jax.ShapeDtypeStruct((S, D), BF16),
                   jax.ShapeDtypeStruct((1, D), F32), jax.ShapeDtypeStruct((1, LANE), F32)],
        compiler_params=_cparams(),
    )(y, wo, xres, tgt, g)


def _matmul_nt(parts, wfull, out_rows, name):
    S = parts[0][0].shape[-2]
    tm, tn = _row_tile(S, 2 * MM_TILE if len(parts) <= 2 else MM_TILE), _tile(out_rows, MM_TILE)
    plan, lo = [], 0
    for arr, lead, col0 in parts:
        n_p = arr.shape[-1]
        tk = math.gcd(_tile(n_p, 1024), col0) if col0 else _tile(n_p, 1024)
        steps = n_p // tk * (arr.shape[0] if lead == "stack" else 1)
        plan.append((lead, col0 // tk, tk, lo, lo + steps))
        lo += steps
    nk = lo
    npart = len(parts)

    def body(*refs):
        a_refs, w_refs = refs[:npart], refs[npart:2 * npart]
        o_ref, acc_ref = refs[2 * npart], refs[2 * npart + 1]
        k = pl.program_id(2)

        @pl.when(k == 0)
        def _():
            acc_ref[...] = jnp.zeros(acc_ref.shape, F32)
        for p, (_, _, _, lo_p, hi_p) in enumerate(plan):
            @pl.when((k >= lo_p) & (k < hi_p))
            def _(p=p):
                acc_ref[...] += _dot_nt(a_refs[p][...], w_refs[p][...])

        @pl.when(k == nk - 1)
        def _():
            o_ref[...] = acc_ref[...]

    in_specs, args = [], []
    for (arr, lead, col0), (_, cb, tk, lo_p, hi_p) in zip(parts, plan):
        def kk(k, lo_p=lo_p, hi_p=hi_p):
            return jnp.clip(k - lo_p, 0, hi_p - lo_p - 1)
        if lead is None:
            in_specs.append(pl.BlockSpec((tm, tk), lambda i, j, k, kk=kk: (i, kk(k))))
        elif lead == "stack":
            nkb = arr.shape[-1] // tk
            in_specs.append(pl.BlockSpec((None, tm, tk), lambda i, j, k, kk=kk, nkb=nkb: (kk(k) // nkb, i, kk(k) % nkb)))
        else:
            in_specs.append(pl.BlockSpec((None, tm, tk), lambda i, j, k, kk=kk, lead=lead: (lead, i, kk(k))))
        args.append(arr)
    for (_, cb, tk, lo_p, hi_p) in plan:
        def kk(k, lo_p=lo_p, hi_p=hi_p):
            return jnp.clip(k - lo_p, 0, hi_p - lo_p - 1)
        in_specs.append(pl.BlockSpec((tn, tk), lambda i, j, k, kk=kk, cb=cb: (j, cb + kk(k))))
        args.append(wfull)
    return pl.pallas_call(
        body, name=name, grid=(S // tm, out_rows // tn, nk),
        in_specs=in_specs, out_specs=pl.BlockSpec((tm, tn), lambda i, j, k: (i, j)),
        out_shape=jax.ShapeDtypeStruct((S, out_rows), F32),
        scratch_shapes=[pltpu.VMEM((tm, tn), F32)],
        compiler_params=_cparams(),
    )(*args)


def _matmul_tn(a, parts, total, name, tile_major=False, also_bf16=False):
    S, M = a.shape
    tm, ts = _tile(M, MM_TILE), _row_tile(S, 2 * MM_TILE)
    nout = 2 if also_bf16 else 1
    outs = None
    for idx, (arr, lead, col0) in enumerate(parts):
        n_p = arr.shape[-1]
        tn = math.gcd(_tile(n_p, MM_TILE), col0) if col0 else _tile(n_p, MM_TILE)
        cb = col0 // tn
        nk = S // ts
        nb = n_p // tn
        if lead == "stack":
            n_p *= arr.shape[0]

        def body(*refs, nk=nk, tn=tn):
            a_ref, b_ref = refs[0], refs[1]
            o_refs, acc_ref = refs[-1 - nout:-1], refs[-1]
            k = pl.program_id(2)

            @pl.when(k == 0)
            def _():
                acc_ref[...] = jnp.zeros(acc_ref.shape, F32)
            acc_ref[...] += _dot_tn(a_ref[...], b_ref[...])

            @pl.when(k == nk - 1)
            def _():
                for o_ref in o_refs:
                    if tile_major:
                        for t in range(tn // LANE):
                            o_ref[t] = acc_ref[:, LANE * t:LANE * (t + 1)].astype(o_ref.dtype)
                    else:
                        o_ref[...] = acc_ref[...].astype(o_ref.dtype)

        in_specs = [pl.BlockSpec((ts, tm), lambda i, j, k: (k, i))]
        if lead is None:
            in_specs.append(pl.BlockSpec((ts, tn), lambda i, j, k: (k, j)))
        elif lead == "stack":
            in_specs.append(pl.BlockSpec((None, ts, tn), lambda i, j, k, nb=nb: (j // nb, k, j % nb)))
        else:
            in_specs.append(pl.BlockSpec((None, ts, tn), lambda i, j, k, lead=lead: (lead, k, j)))
        args = [a, arr]
        aliases = {}
        if outs is not None:
            in_specs += [pl.BlockSpec(memory_space=pl.ANY)] * nout
            args += list(outs)
            aliases = {2 + o: o for o in range(nout)}
        if tile_major:
            out_spec = pl.BlockSpec((tn // LANE, tm, LANE), lambda i, j, k, cb=cb: (cb + j, i, 0))
            shape = (total // LANE, M, LANE)
        else:
            out_spec = pl.BlockSpec((tm, tn), lambda i, j, k, cb=cb: (i, cb + j))
            shape = (M, total)
        outs = pl.pallas_call(
            body, name=f"{name}_{idx}", grid=(M // tm, n_p // tn, nk),
            in_specs=in_specs, out_specs=[out_spec] * nout,
            out_shape=[jax.ShapeDtypeStruct(shape, dt) for dt in (F32, BF16)[:nout]],
            scratch_shapes=[pltpu.VMEM((tm, tn), F32)],
            input_output_aliases=aliases,
            compiler_params=_cparams(),
        )(*args)
    return tuple(outs) if also_bf16 else outs[0]


def _log_sigmoid(z):
    e = jnp.exp(-jnp.abs(z))
    return jnp.minimum(z, 0.0) - jnp.where(e < 1e-4, e * (1.0 - 0.5 * e), jnp.log(1.0 + e))


def _fox_gate_fwd(fl, bias):
    S = fl.shape[0]

    def body(f_ref, b_ref, c_ref):
        row = lax.broadcasted_iota(jnp.int32, (8, LANE), 0)

        def step(i, carry):
            r0 = pl.multiple_of(i * 8, 8)
            t = _log_sigmoid(f_ref[pl.ds(r0, 8), :] + b_ref[...])
            for sh in (1, 2, 4):
                t = t + jnp.where(row >= sh, pltpu.roll(t, sh, axis=0), 0.0)
            t = t + carry
            c_ref[pl.ds(r0, 8), :] = t
            return jnp.sum(jnp.where(row == 7, t, 0.0), axis=0, keepdims=True)

        lax.fori_loop(0, S // 8, step, jnp.zeros((1, LANE), F32))

    vm = pl.BlockSpec(memory_space=pltpu.VMEM)
    return pl.pallas_call(
        body, name="fox_gate_fwd", in_specs=[vm, vm], out_specs=vm,
        out_shape=jax.ShapeDtypeStruct((S, LANE), F32),
        compiler_params=_cparams(),
    )(fl, bias)


def _fox_gate_bwd(fl, bias, dc):
    S = fl.shape[0]

    def body(f_ref, b_ref, d_ref, o_ref, db_ref, acc_ref):
        row = lax.broadcasted_iota(jnp.int32, (8, LANE), 0)
        nt = S // 8

        def step(ii, carry):
            carry_c, carry_b = carry
            r0 = pl.multiple_of((nt - 1 - ii) * 8, 8)
            t = d_ref[pl.ds(r0, 8), :]
            for sh in (1, 2, 4):
                t = t + jnp.where(row < 8 - sh, pltpu.roll(t, 8 - sh, axis=0), 0.0)
            t = t + carry_c
            z = f_ref[pl.ds(r0, 8), :] + b_ref[...]
            dz = t * _sigmoid(-z)
            acc_ref[pl.ds(r0, 8), :] = dz
            first = jnp.sum(jnp.where(row == 0, t, 0.0), axis=0, keepdims=True)
            return first, carry_b + jnp.sum(dz, axis=0, keepdims=True)

        zero = jnp.zeros((1, LANE), F32)
        _, db = lax.fori_loop(0, nt, step, (zero, zero))
        db_ref[...] = db
        o_ref[...] = acc_ref[...].astype(BF16)

    vm = pl.BlockSpec(memory_space=pltpu.VMEM)
    return pl.pallas_call(
        body, name="fox_gate_bwd", in_specs=[vm, vm, vm], out_specs=[vm, vm],
        out_shape=[jax.ShapeDtypeStruct((S, LANE), BF16), jax.ShapeDtypeStruct((1, LANE), F32)],
        scratch_shapes=[pltpu.VMEM((S, LANE), F32)],
        compiler_params=_cparams(),
    )(fl, bias, dc)


def _bias_lanes(col, lane, e, first):
    o0 = HEAD_DIM * (1 - e)
    hi = col.astype(BF16)
    r1 = col - hi.astype(F32)
    mid = r1.astype(BF16)
    lo = (r1 - mid.astype(F32)).astype(BF16)
    d0 = o0 if first else o0 + 3
    t = jnp.where((lane >= o0) & (lane < o0 + 6), jnp.ones(lane.shape, BF16), jnp.zeros(lane.shape, BF16))
    t = jnp.where(lane == d0, hi, t)
    t = jnp.where(lane == d0 + 1, mid, t)
    return jnp.where(lane == d0 + 2, lo, t)


def _fox_fwd(qkvg, c, H, gather=()):
    na = len(gather)
    S = qkvg.shape[0]
    W = H * HEAD_DIM
    HP = H // 2
    PP = 2 if HP % 2 == 0 else 1
    NE = 2 * PP
    tq = _row_tile(S, 512)
    nq = S // tq
    wb = W // LANE
    scale = HEAD_DIM ** -0.5

    def body(*refs):
        q_ref, k_ref, v_ref, g_ref, c_ref = refs[:5]
        y_ref, o_ref, a_ref = refs[5 + na:8 + na]
        kaug_sc, vaug_sc, qaug_sc, s_sc, mb_sc, m_sc, acc_sc = refs[8 + 2 * na:15 + 2 * na]
        hp, qi = pl.program_id(0), pl.program_id(1)
        if na:
            remote, local = _direct_gather_copies(refs[5:5 + na], refs[8 + na:8 + 2 * na], *refs[15 + 2 * na:])

            @pl.when((hp == 0) & (qi == 0))
            def _():
                for cp in remote + local:
                    cp.start()
        lane = lax.broadcasted_iota(jnp.int32, (tq, LANE), 1)
        own = [lane < HEAD_DIM, lane >= HEAD_DIM]
        rows = lax.broadcasted_iota(jnp.int32, (tq, tq), 0)
        cols = lax.broadcasted_iota(jnp.int32, (tq, tq), 1)

        def bias_lanes(col, e, first):
            return _bias_lanes(col, lane, e % 2, first)

        def head_col(tile, e):
            return jnp.sum(jnp.where(lane == 2 * PP * hp + e, tile, 0.0), axis=1, keepdims=True)

        def tile_of(e):
            return slice(LANE * (e // 2), LANE * (e // 2 + 1))

        @pl.when(qi == 0)
        def _():
            def chunk(i, carry):
                r0 = pl.multiple_of(i * tq, tq)
                cb = c_ref[pl.ds(r0, tq), :]
                for e in range(NE):
                    kb, vb = k_ref[pl.ds(r0, tq), tile_of(e)], v_ref[pl.ds(r0, tq), tile_of(e)]
                    kaug_sc[e, pl.ds(r0, tq), :] = jnp.where(own[e % 2], kb, bias_lanes(-head_col(cb, e), e, False))
                    vaug_sc[e, pl.ds(r0, tq), :] = jnp.where(own[e % 2], vb, jnp.ones((tq, LANE), BF16))
                return carry
            lax.fori_loop(0, nq, chunk, 0)

        crow = c_ref[pl.ds(pl.multiple_of(qi * tq, tq), tq), :]
        ctq = [head_col(crow, e) for e in range(NE)]
        for e in range(NE):
            q = q_ref[:, tile_of(e)] * jnp.asarray(scale, BF16)
            qaug_sc[e] = jnp.where(own[e % 2], q, bias_lanes(ctq[e], e, True))
        m_sc[...] = jnp.full(m_sc.shape, NEG_INF, F32)
        acc_sc[...] = jnp.zeros(acc_sc.shape, F32)

        def scores(blk, slot, masked):
            k0 = pl.multiple_of(blk * tq, tq)
            for e in range(NE):
                s = _dot_nt(qaug_sc[e], kaug_sc[e, pl.ds(k0, tq), :])
                if masked:
                    s = jnp.where(rows >= cols, s, NEG_INF)
                s_sc[slot, e] = s
                mb_sc[slot, e] = jnp.broadcast_to(jnp.max(s, axis=1, keepdims=True), (tq, LANE))

        def accumulate(blk, slot):
            k0 = pl.multiple_of(blk * tq, tq)
            for e in range(NE):
                m_prev = m_sc[e]
                m_new = jnp.maximum(m_prev, mb_sc[slot, e])
                p = jnp.exp(s_sc[slot, e] - jnp.tile(m_new, (1, tq // LANE)))
                acc_sc[e] = jnp.exp(m_prev - m_new) * acc_sc[e] + _dot_nn(p.astype(BF16), vaug_sc[e, pl.ds(k0, tq), :])
                m_sc[e] = m_new

        def block_of(t):
            return jnp.where(t == 0, qi, t - 1)

        scores(qi, 0, True)

        def loop_body(t, carry):
            scores(t, (t + 1) % 2, False)
            accumulate(block_of(t), t % 2)
            return carry

        lax.fori_loop(0, qi, loop_body, 0)
        accumulate(block_of(qi), qi % 2)
        o_e, a_e = [], []
        for e in range(NE):
            acc = acc_sc[e]
            l = pltpu.roll(acc, HEAD_DIM, axis=1)
            o_e.append(acc / l)
            a_e.append(ctq[e] - (m_sc[e] + jnp.log(l)))
        for pp in range(PP):
            o = jnp.where(own[0], o_e[2 * pp], o_e[2 * pp + 1])
            g = g_ref[:, tile_of(2 * pp)].astype(F32)
            y_ref[:, tile_of(2 * pp)] = (o * (g * _sigmoid(g))).astype(BF16)
            o_ref[:, tile_of(2 * pp)] = o.astype(BF16)
            a_ref[pp] = jnp.where(own[0], a_e[2 * pp], a_e[2 * pp + 1])
        if na:
            @pl.when((hp == HP // PP - 1) & (qi == nq - 1))
            def _():
                _wait_all(remote, local)

    any_spec = pl.BlockSpec(memory_space=pl.ANY)
    sems = [pltpu.SemaphoreType.DMA((na, N_DEV - 1)), pltpu.SemaphoreType.DMA((na, N_DEV - 1)),
            pltpu.SemaphoreType.DMA((na,))] if na else []
    wide = PP * LANE
    outs = pl.pallas_call(
        body, name="fox_attn_fwd", grid=(HP // PP, nq),
        in_specs=[pl.BlockSpec((tq, wide), lambda h, i: (i, h)),
                  pl.BlockSpec((S, wide), lambda h, i: (0, wb // PP + h)),
                  pl.BlockSpec((S, wide), lambda h, i: (0, 2 * wb // PP + h)),
                  pl.BlockSpec((tq, wide), lambda h, i: (i, 3 * wb // PP + h)),
                  pl.BlockSpec((S, LANE), lambda h, i: (0, 0))] + [any_spec] * na,
        out_specs=[pl.BlockSpec((tq, wide), lambda h, i: (i, h)),
                   pl.BlockSpec((tq, wide), lambda h, i: (i, h)),
                   pl.BlockSpec((PP, tq, LANE), lambda h, i: (h, i, 0))] + [any_spec] * na,
        out_shape=[jax.ShapeDtypeStruct((S, W), BF16), jax.ShapeDtypeStruct((S, W), BF16),
                   jax.ShapeDtypeStruct((HP, S, LANE), F32)]
        + [jax.ShapeDtypeStruct((N_DEV,) + g.shape, g.dtype) for g in gather],
        scratch_shapes=[pltpu.VMEM((NE, S, LANE), BF16), pltpu.VMEM((NE, S, LANE), BF16),
                        pltpu.VMEM((NE, tq, LANE), BF16), pltpu.VMEM((2, NE, tq, tq), F32),
                        pltpu.VMEM((2, NE, tq, LANE), F32), pltpu.VMEM((NE, tq, LANE), F32),
                        pltpu.VMEM((NE, tq, LANE), F32)] + sems,
        compiler_params=_cparams(),
    )(qkvg, qkvg, qkvg, qkvg, c, *gather)
    return outs[0], outs[1], outs[2], list(outs[3:])


def _fox_out_bwd(dxb, wo, qkvg, o, a, H):
    S, D = dxb.shape
    W = H * HEAD_DIM
    tm, tn = _row_tile(S, 512), _tile(W, 512)
    npair = tn // LANE
    scale = HEAD_DIM ** -0.5

    def body(dx_ref, w_ref, q_ref, g_ref, o_ref, a_ref, qa_ref, da_ref, dg_ref):
        dy = _dot_nt(dx_ref[...], w_ref[...])
        lane = lax.broadcasted_iota(jnp.int32, (tm, LANE), 1)
        own = [lane < HEAD_DIM, lane >= HEAD_DIM]
        for p in range(npair):
            cols = slice(LANE * p, LANE * (p + 1))
            q = q_ref[:, cols] * jnp.asarray(scale, BF16)
            dyv, g, ov, at = dy[:, cols], g_ref[:, cols].astype(F32), o_ref[:, cols].astype(F32), a_ref[p]
            sg = _sigmoid(g)
            dob = (dyv * (g * sg)).astype(BF16)
            dg_ref[:, cols] = (dyv * ov * (sg * (1.0 + g * (1.0 - sg)))).astype(BF16)
            prod = dob.astype(F32) * ov
            for e in range(2):
                a_col = jnp.max(jnp.where(own[e], at, -jnp.inf), axis=1, keepdims=True)
                d_col = jnp.sum(jnp.where(own[e], prod, 0.0), axis=1, keepdims=True)
                qa_ref[e, :, cols] = jnp.where(own[e], q, _bias_lanes(a_col, lane, e, True))
                da_ref[e, :, cols] = jnp.where(own[e], dob, _bias_lanes(-d_col, lane, e, True))

    blk = pl.BlockSpec((tm, tn), lambda i, j: (i, j))
    pair = pl.BlockSpec((2, tm, tn), lambda i, j: (0, i, j))
    return pl.pallas_call(
        body, name="fox_out_bwd", grid=(S // tm, W // tn),
        in_specs=[pl.BlockSpec((tm, D), lambda i, j: (i, 0)), pl.BlockSpec((tn, D), lambda i, j: (j, 0)),
                  blk, pl.BlockSpec((tm, tn), lambda i, j: (i, 3 * W // tn + j)), blk,
                  pl.BlockSpec((npair, tm, LANE), lambda i, j: (j, i, 0))],
        out_specs=[pair, pair, pl.BlockSpec((None, tm, tn), lambda i, j: (3, i, j))],
        out_shape=[jax.ShapeDtypeStruct((2, S, W), BF16), jax.ShapeDtypeStruct((2, S, W), BF16),
                   jax.ShapeDtypeStruct((4, S, W), BF16)],
        compiler_params=_cparams(),
    )(dxb, wo, qkvg, qkvg, o, a)


def _fox_bwd(qaug, doaug, qkv, c, dqkvg, H, scatter=(), scatter_specs=()):
    na = len(scatter)
    S = qkv.shape[0]
    W = H * HEAD_DIM
    HP = H // 2
    tq = _row_tile(S, 512)
    nq = S // tq
    wb = W // LANE
    scale = HEAD_DIM ** -0.5

    def body(*refs):
        qa_ref, da_ref, k_ref, v_ref, c_ref = refs[:5]
        out_ref, dcr_ref, dcc_ref = refs[6 + na:9 + na]
        dq_sc, dk_sc, dv_sc = refs[9 + 2 * na:12 + 2 * na]
        hp, kj = pl.program_id(0), pl.program_id(1)
        if na:
            remote = _direct_scatter_copies(refs[5:5 + na], refs[9 + na:9 + 2 * na], scatter_specs,
                                            *refs[12 + 2 * na:])

            @pl.when((hp == 0) & (kj == 0))
            def _():
                for cp in remote:
                    cp.start()
        lane = lax.broadcasted_iota(jnp.int32, (tq, LANE), 1)
        own = [lane < HEAD_DIM, lane >= HEAD_DIM]
        rows = lax.broadcasted_iota(jnp.int32, (tq, tq), 0)
        cols = lax.broadcasted_iota(jnp.int32, (tq, tq), 1)

        @pl.when(kj == 0)
        def _():
            dq_sc[...] = jnp.zeros(dq_sc.shape, F32)

        @pl.when((kj == 0) & (hp == 0))
        def _():
            dcr_ref[...] = jnp.zeros(dcr_ref.shape, F32)
            dcc_ref[...] = jnp.zeros(dcc_ref.shape, F32)

        kblk, vblk, cblk = k_ref[...], v_ref[...], c_ref[...]
        one, zero = jnp.ones((tq, LANE), BF16), jnp.zeros((tq, LANE), BF16)
        ka, va = [], []
        for e in range(2):
            o0 = HEAD_DIM * (1 - e)
            c_col = jnp.sum(jnp.where(lane == 2 * hp + e, cblk, 0.0), axis=1, keepdims=True)
            ka.append(jnp.where(own[e], kblk, _bias_lanes(-c_col, lane, e, False)))
            va.append(jnp.where(own[e], vblk, jnp.where((lane >= o0) & (lane < o0 + 3), one, zero)))
        dk_sc[...] = jnp.zeros(dk_sc.shape, F32)
        dv_sc[...] = jnp.zeros(dv_sc.shape, F32)

        def step(i, masked):
            r0 = pl.multiple_of(i * tq, tq)
            for e in range(2):
                qa = qa_ref[e, pl.ds(r0, tq), :]
                da = da_ref[e, pl.ds(r0, tq), :]
                p = jnp.exp(_dot_nt(qa, ka[e]))
                if masked:
                    p = jnp.where(rows >= cols, p, 0.0)
                ds = p * _dot_nt(da, va[e])
                pb, dsb = p.astype(BF16), ds.astype(BF16)
                dv_sc[e] += _dot_tn(pb, da)
                dk_sc[e] += _dot_tn(dsb, qa)
                dq_sc[e, pl.ds(r0, tq), :] += _dot_nn(dsb, ka[e])

        step(kj, True)

        def loop_body(i, carry):
            step(i, False)
            return carry

        lax.fori_loop(kj + 1, nq, loop_body, 0)
        k0 = pl.multiple_of(kj * tq, tq)
        out_ref[1, pl.ds(k0, tq), :] = jnp.where(own[0], dk_sc[0], dk_sc[1]).astype(BF16)
        out_ref[2, pl.ds(k0, tq), :] = jnp.where(own[0], dv_sc[0], dv_sc[1]).astype(BF16)

        def put_lane(ref, r0, e, tile, src_lane):
            col = jnp.sum(jnp.where(lane == src_lane, tile, 0.0), axis=1, keepdims=True)
            ref[pl.ds(r0, tq), :] = jnp.where(lane == 2 * hp + e, col, ref[pl.ds(r0, tq), :])

        for e in range(2):
            put_lane(dcc_ref, k0, e, dk_sc[e], HEAD_DIM * (1 - e) + 3)

        @pl.when(kj == nq - 1)
        def _():
            def chunk(i, carry):
                r0 = pl.multiple_of(i * tq, tq)
                d0, d1 = dq_sc[0, pl.ds(r0, tq), :], dq_sc[1, pl.ds(r0, tq), :]
                out_ref[0, pl.ds(r0, tq), :] = (jnp.where(own[0], d0, d1) * scale).astype(BF16)
                put_lane(dcr_ref, r0, 0, d0, HEAD_DIM)
                put_lane(dcr_ref, r0, 1, d1, 0)
                return carry
            lax.fori_loop(0, nq, chunk, 0)

        if na:
            @pl.when((hp == HP - 1) & (kj == nq - 1))
            def _():
                _wait_all(remote)

    pair = pl.BlockSpec((2, S, LANE), lambda h, j: (0, 0, h))
    vec = pl.BlockSpec((S, LANE), lambda h, j: (0, 0))
    any_spec = pl.BlockSpec(memory_space=pl.ANY)
    sems = [pltpu.SemaphoreType.DMA((na, N_DEV - 1)), pltpu.SemaphoreType.DMA((na, N_DEV - 1))] if na else []
    outs = pl.pallas_call(
        body, name="fox_attn_bwd", grid=(HP, nq),
        in_specs=[pair, pair,
                  pl.BlockSpec((tq, LANE), lambda h, j: (j, wb + h)),
                  pl.BlockSpec((tq, LANE), lambda h, j: (j, 2 * wb + h)),
                  pl.BlockSpec((tq, LANE), lambda h, j: (j, 0))] + [any_spec] * (na + 1),
        out_specs=[pl.BlockSpec((3, S, LANE), lambda h, j: (0, 0, h)), vec, vec] + [any_spec] * na,
        out_shape=[jax.ShapeDtypeStruct(dqkvg.shape, BF16), jax.ShapeDtypeStruct((S, LANE), F32),
                   jax.ShapeDtypeStruct((S, LANE), F32)]
        + [jax.ShapeDtypeStruct((N_DEV - 1,) + _scatter_block_shape(g, s), g.dtype)
           for g, s in zip(scatter, scatter_specs)],
        scratch_shapes=[pltpu.VMEM((2, S, LANE), F32), pltpu.VMEM((2, tq, LANE), F32),
                        pltpu.VMEM((2, tq, LANE), F32)] + sems,
        input_output_aliases={5 + na: 0},
        compiler_params=_cparams(),
    )(qaug, doaug, qkv, qkv, c, *scatter, dqkvg)
    return outs[0], outs[1], outs[2], list(outs[3:])


def _swa_pick(blk, half, lane):
    b = blk.astype(F32)
    r = pltpu.roll(b, HEAD_DIM, axis=1)
    return jnp.where(jnp.logical_xor(lane < HEAD_DIM, half == 1), b, r).astype(BF16)


def _swa_stack(t, lane, G):
    pieces = []
    z = jnp.zeros((SWA_BLOCK, LANE), t.dtype)
    for j in range(G // 2):
        tile = t[:, LANE * j:LANE * (j + 1)]
        pieces += [jnp.where(lane < HEAD_DIM, tile, z), jnp.where(lane < HEAD_DIM, z, tile)]
    return jnp.concatenate(pieces, axis=0)


def _swa_unstack(st, lane, G):
    tiles = []
    for j in range(G // 2):
        a = st[2 * j * SWA_BLOCK:(2 * j + 1) * SWA_BLOCK]
        b = st[(2 * j + 1) * SWA_BLOCK:(2 * j + 2) * SWA_BLOCK]
        tiles.append(jnp.where(lane < HEAD_DIM, a, b))
    return jnp.concatenate(tiles, axis=1)


def _swa_mask_bias(G):
    R = G * SWA_BLOCK
    t_loc = jnp.arange(R)[:, None] % SWA_BLOCK
    j_loc = jnp.arange(2 * SWA_BLOCK)[None, :]
    diff = t_loc + SWA_BLOCK - j_loc
    band = (diff >= 0) & (diff < SWA_BLOCK)
    return jnp.stack([jnp.where(band & (j_loc >= SWA_BLOCK), 0.0, NEG_INF),
                      jnp.where(band, 0.0, NEG_INF)]).astype(F32)


def _swa_scores(q, kp, kc, vp, vc, srow, bias, half, head0, G):
    lane = lax.broadcasted_iota(jnp.int32, (SWA_BLOCK, LANE), 1)
    kk = jnp.concatenate([_swa_pick(kp, half, lane), _swa_pick(kc, half, lane)], axis=0)
    vv = jnp.concatenate([_swa_pick(vp, half, lane), _swa_pick(vc, half, lane)], axis=0)
    qstack = _swa_stack(q, lane, G) * jnp.asarray(HEAD_DIM ** -0.5, BF16)
    s = _dot_nt(qstack, kk) + bias
    R = G * SWA_BLOCK
    lane1 = lax.broadcasted_iota(jnp.int32, (1, LANE), 1)
    sink = jnp.concatenate(
        [jnp.broadcast_to(jnp.sum(jnp.where(lane1 == head0 + g, srow, 0.0), axis=1, keepdims=True), (SWA_BLOCK, LANE))
         for g in range(G)], axis=0)
    m = jnp.maximum(jnp.broadcast_to(jnp.max(s, axis=1, keepdims=True), (R, LANE)), sink)
    e = jnp.exp(s - jnp.tile(m, (1, 2)))
    es = jnp.exp(sink - m)
    inv = 1.0 / (jnp.broadcast_to(jnp.sum(e, axis=1, keepdims=True), (R, LANE)) + es)
    return qstack, kk, vv, e * jnp.tile(inv, (1, 2)), es * inv, lane


def _swa_fwd(q, kv, gate, sinks, mask_bias, HQ, HKV):
    S = q.shape[0]
    G = HQ // HKV
    WQ, KVW = HQ * HEAD_DIM, HKV * HEAD_DIM
    nb = S // SWA_BLOCK
    GW = G * HEAD_DIM
    kb, vb = 0, KVW // LANE
    NH = min(HKV, 4)
    NP = NH // 2

    def body(q_ref, kp_ref, kc_ref, vp_ref, vc_ref, g_ref, sink_ref, b_ref, y_ref, o_ref):
        grp = pl.program_id(0)
        for hh in range(NH):
            cols, kt = slice(GW * hh, GW * (hh + 1)), slice(LANE * (hh // 2), LANE * (hh // 2 + 1))
            _, _, vv, p, _, lane = _swa_scores(q_ref[:, cols], kp_ref[:, kt], kc_ref[:, kt], vp_ref[:, kt], vc_ref[:, kt],
                                               sink_ref[...], b_ref[0], hh % 2, (NH * grp + hh) * G, G)
            o = _swa_unstack(_dot_nn(p.astype(BF16), vv), lane, G)
            g = g_ref[:, cols].astype(F32)
            y_ref[:, cols] = (o * (g * _sigmoid(g))).astype(BF16)
            o_ref[:, cols] = o.astype(BF16)

    blk = lambda cb, prev: pl.BlockSpec(
        (SWA_BLOCK, NP * LANE), lambda h, n, cb=cb, prev=prev: (jnp.maximum(n - prev, 0), cb // NP + h))
    qspec = pl.BlockSpec((SWA_BLOCK, NH * GW), lambda h, n: (n, h))
    return pl.pallas_call(
        body, name="swa_attn_fwd", grid=(HKV // NH, nb),
        in_specs=[qspec, blk(kb, 1), blk(kb, 0), blk(vb, 1), blk(vb, 0), qspec,
                  pl.BlockSpec((1, LANE), lambda h, n: (0, 0)),
                  pl.BlockSpec((1, G * SWA_BLOCK, 2 * SWA_BLOCK), lambda h, n: (jnp.minimum(n, 1), 0, 0))],
        out_specs=[qspec, qspec],
        out_shape=[jax.ShapeDtypeStruct((S, WQ), BF16), jax.ShapeDtypeStruct((S, WQ), BF16)],
        compiler_params=_cparams(),
    )(q, kv, kv, kv, kv, gate, sinks, mask_bias)


def _swa_bwd(q, kv, dy, gate, o, sinks, tables, mask_bias, HQ, HKV):
    S = q.shape[0]
    G = HQ // HKV
    WQ, KVW = HQ * HEAD_DIM, HKV * HEAD_DIM
    nb = S // SWA_BLOCK
    GW = G * HEAD_DIM
    R = G * SWA_BLOCK
    kb, vb = 0, KVW // LANE
    scale = HEAD_DIM ** -0.5
    NH = min(HKV, 4)
    NP = NH // 2
    assert G == 8

    def body(q_ref, kp_ref, kc_ref, vp_ref, vc_ref, dy_ref, g_ref, o_ref, sink_ref, t_ref, b_ref,
             dqg_ref, dkv_ref, dsink_ref, carry_sc):
        grp, n = pl.program_id(0), pl.program_id(1)

        @pl.when(n == 0)
        def _():
            carry_sc[...] = jnp.zeros(carry_sc.shape, F32)
            dsink_ref[...] = jnp.zeros(dsink_ref.shape, F32)

        @pl.when(n < nb)
        def _():
            t0, t1, t2 = (jnp.tile(t_ref[i], (1, GW // LANE)) for i in range(3))
            for hh in range(NH):
                cols, kt = slice(GW * hh, GW * (hh + 1)), slice(LANE * (hh // 2), LANE * (hh // 2 + 1))
                qstack, kk, vv, p, psink, lane = _swa_scores(
                    q_ref[:, cols], kp_ref[:, kt], kc_ref[:, kt], vp_ref[:, kt], vc_ref[:, kt], sink_ref[...], b_ref[0],
                    hh % 2, (NH * grp + hh) * G, G)
                dyv, g, ov = dy_ref[:, cols], g_ref[:, cols].astype(F32), o_ref[:, cols].astype(F32)
                sg = _sigmoid(g)
                dob = (dyv * (g * sg)).astype(BF16)
                dqg_ref[1, :, cols] = (dyv * ov * (sg * (1.0 + g * (1.0 - sg)))).astype(BF16)
                prod = dob.astype(F32) * ov
                dparts = []
                for j in range(G // 2):
                    tile = prod[:, LANE * j:LANE * (j + 1)]
                    for sel in (jnp.where(lane < HEAD_DIM, tile, 0.0), jnp.where(lane < HEAD_DIM, 0.0, tile)):
                        dparts.append(jnp.broadcast_to(jnp.sum(sel, axis=1, keepdims=True), (SWA_BLOCK, LANE)))
                delta = jnp.concatenate(dparts, axis=0)
                dostack = _swa_stack(dob, lane, G)
                ds = p * (_dot_nt(dostack, vv) - jnp.tile(delta, (1, 2)))
                dsb, pb = ds.astype(BF16), p.astype(BF16)
                dq = _swa_unstack(_dot_nn(dsb, kk), lane, G) * scale
                dq = dq * t0 + pltpu.roll(dq * t1, ROT_DIM // 2, axis=1) + pltpu.roll(dq * t2, GW - ROT_DIM // 2, axis=1)
                dqg_ref[0, :, cols] = dq.astype(BF16)
                dkk = _dot_tn(dsb, qstack)
                dvv = _dot_tn(pb, dostack)
                dkk = dkk + pltpu.roll(dkk, HEAD_DIM, axis=1)
                dvv = dvv + pltpu.roll(dvv, HEAD_DIM, axis=1)
                lane2 = lax.broadcasted_iota(jnp.int32, (2 * SWA_BLOCK, LANE), 1)
                comb = jnp.where(lane2 < HEAD_DIM, dkk, dvv)
                dkv_ref[hh] = carry_sc[hh] + comb[:SWA_BLOCK]
                carry_sc[hh] = comb[SWA_BLOCK:]
                sk = psink * delta
                rows = [-jnp.sum(sk[g_ * SWA_BLOCK:(g_ + 1) * SWA_BLOCK], axis=0, keepdims=True) for g_ in range(G)]
                dsink_ref[hh] += jnp.concatenate(rows, axis=0)

        @pl.when(n == nb)
        def _():
            dkv_ref[...] = carry_sc[...]

    cl = lambda n: jnp.minimum(n, nb - 1)
    blk = lambda cb, prev: pl.BlockSpec(
        (SWA_BLOCK, NP * LANE), lambda h, n, cb=cb, prev=prev: (jnp.maximum(cl(n) - prev, 0), cb // NP + h))
    qspec = pl.BlockSpec((SWA_BLOCK, NH * GW), lambda h, n: (cl(n), h))
    return pl.pallas_call(
        body, name="swa_attn_bwd", grid=(HKV // NH, nb + 1),
        in_specs=[qspec, blk(kb, 1), blk(kb, 0), blk(vb, 1), blk(vb, 0), qspec, qspec, qspec,
                  pl.BlockSpec((1, LANE), lambda h, n: (0, 0)),
                  pl.BlockSpec((3, SWA_BLOCK, LANE), lambda h, n: (0, cl(n), 0)),
                  pl.BlockSpec((1, R, 2 * SWA_BLOCK), lambda h, n: (jnp.minimum(n, 1), 0, 0))],
        out_specs=[pl.BlockSpec((2, SWA_BLOCK, NH * GW), lambda h, n: (0, cl(n), h)),
                   pl.BlockSpec((NH, SWA_BLOCK, LANE), lambda h, n: (h, jnp.maximum(n - 1, 0), 0)),
                   pl.BlockSpec((NH, 8, LANE), lambda h, n: (h, 0, 0))],
        out_shape=[jax.ShapeDtypeStruct((2, S, WQ), BF16), jax.ShapeDtypeStruct((HKV, S, LANE), F32),
                   jax.ShapeDtypeStruct((HKV, 8, LANE), F32)],
        scratch_shapes=[pltpu.VMEM((NH, SWA_BLOCK, LANE), F32)],
        compiler_params=_cparams(),
    )(q, kv, kv, kv, kv, dy, gate, o, sinks, tables, mask_bias)


def _swa_dkv_finish(dkv, tables):
    HKV, S, _ = dkv.shape
    KVW = HKV * HEAD_DIM
    tm = _row_tile(S, 512)
    npair = HKV // 2

    def body(d_ref, t_ref, o_ref):
        lane = lax.broadcasted_iota(jnp.int32, (tm, LANE), 1)
        lo = lane < HEAD_DIM
        for p in range(npair):
            a, b = d_ref[2 * p], d_ref[2 * p + 1]
            tk = jnp.where(lo, a, pltpu.roll(b, HEAD_DIM, axis=1))
            tv = jnp.where(lo, pltpu.roll(a, HEAD_DIM, axis=1), b)
            tk = (tk * t_ref[0] + pltpu.roll(tk * t_ref[1], ROT_DIM // 2, axis=1)
                  + pltpu.roll(tk * t_ref[2], LANE - ROT_DIM // 2, axis=1))
            o_ref[:, LANE * p:LANE * (p + 1)] = tk.astype(BF16)
            o_ref[:, KVW + LANE * p:KVW + LANE * (p + 1)] = tv.astype(BF16)

    return pl.pallas_call(
        body, name="swa_dkv_finish", grid=(S // tm,),
        in_specs=[pl.BlockSpec((HKV, tm, LANE), lambda i: (0, i, 0)), pl.BlockSpec((3, tm, LANE), lambda i: (0, i, 0))],
        out_specs=pl.BlockSpec((tm, 2 * KVW), lambda i: (i, 0)),
        out_shape=jax.ShapeDtypeStruct((S, 2 * KVW), BF16),
        compiler_params=_cparams(),
    )(dkv, tables)


def _rope_tables(S, width):
    half = ROT_DIM // 2
    pos = jnp.arange(S, dtype=F32)
    inv_freq = ROPE_THETA ** (-jnp.arange(half, dtype=F32) / half)
    ang = pos[:, None] * inv_freq[None, :]
    cos, sin = jnp.cos(ang), jnp.sin(ang)
    one = jnp.ones((S, HEAD_DIM - ROT_DIM), F32)
    zero = jnp.zeros((S, HEAD_DIM - ROT_DIM), F32)
    zh = jnp.zeros((S, half), F32)
    t0 = jnp.concatenate([cos, cos, one], axis=1)
    t1 = jnp.concatenate([-sin, zh, zero], axis=1)
    t2 = jnp.concatenate([zh, sin, zero], axis=1)
    return jnp.stack([jnp.tile(t, (1, width // HEAD_DIM)) for t in (t0, t1, t2)])


def _pad_rows(v, row, total_rows=8):
    return jnp.pad(v, ((row, total_rows - row - v.shape[0]), (0, 0)))


def _pad_lanes(v, off, width):
    return jnp.pad(v, ((0, 0), (off, width - off - v.shape[1])))


def kernel(x, norm_g, fox_w_in, fox_b_f, fox_w_out, swa_w_in, swa_sinks, swa_w_out, final_g, loss_target, m_norm_g, m_fox_w_in, m_fox_b_f, m_fox_w_out, m_swa_w_in, m_swa_sinks, m_swa_w_out, m_final_g, v_norm_g, v_fox_w_in, v_fox_b_f, v_fox_w_out, v_swa_w_in, v_swa_sinks, v_swa_w_out, v_final_g):
    S, D = x.shape[1], x.shape[2]
    H = fox_b_f.shape[1]
    W = H * HEAD_DIM
    wf = fox_w_in.shape[2]
    ws = swa_w_in.shape[2]
    HQ = swa_sinks.shape[1]
    WQ = HQ * HEAD_DIM
    KVW = (ws * N_DEV - 2 * WQ) // 2
    HKV = KVW // HEAD_DIM
    rows_o = fox_w_out.shape[1]
    assert wf * N_DEV == 4 * W + H and rows_o * N_DEV == W and H <= LANE and HQ <= LANE
    me = _my_index()

    _, sw_f, np_f = _slab_geom(wf)
    _, sw_s, np_s = _slab_geom(ws)

    def slab(w2d, w, sw):
        return jnp.pad(w2d.astype(BF16), ((0, 0), (0, sw - w)))

    (fi_all,) = _all_gather([slab(fox_w_in[0], wf, sw_f)])
    w_fi = _assemble(fi_all, wf)
    later = [slab(swa_w_in[0], ws, sw_s), fox_w_out[0].astype(BF16), swa_w_out[0].astype(BF16)]

    x0 = x[0]
    g0, g1, gf = norm_g[0:1], norm_g[1:2], final_g[None, :]
    bias = _pad_lanes(fox_b_f, 0, LANE)
    sinks = _pad_lanes(swa_sinks, 0, LANE)
    tab_k = _rope_tables(S, LANE)
    mask_bias = _swa_mask_bias(HQ // HKV)

    h0 = _rmsnorm_fwd(x0, g0, "rmsnorm0")
    qkv0 = _proj(h0, w_fi, 0, 4 * W, BF16, "fox_in_qkvg")
    fl = _proj(h0, w_fi, 4 * W, LANE, F32, "fox_in_f")
    c = _fox_gate_fwd(fl, bias)
    y0, o0, a0, (si_all, fo_all, so_all) = _fox_fwd(qkv0, c, H, gather=later)
    w_si = _assemble(si_all, ws)
    w_fo = fo_all.reshape(W, D)
    w_so = so_all.reshape(WQ, D)
    x1, h1 = _out_proj_norm(y0, w_fo, x0, g1, "fox_out")

    q1 = _proj(h1, w_si, 0, WQ, BF16, "swa_in_q", rope=(tab_k, WQ))
    kv1 = _proj(h1, w_si, WQ, 2 * KVW, BF16, "swa_in_kv", rope=(tab_k, KVW))
    gate1 = _proj(h1, w_si, WQ + 2 * KVW, WQ, BF16, "swa_in_gate")
    y1, o1 = _swa_fwd(q1, kv1, gate1, sinks, mask_bias, HQ, HKV)
    dx2, dx2b, dgf, loss_p = _out_proj_loss(y1, w_so, x1, loss_target[0], gf, "swa_out_loss")

    dy1 = _matmul_nt([(dx2b, None, 0)], w_so, WQ, "swa_out_bwd")
    g_so, g_so_h = _matmul_tn(y1, [(dx2b, None, 0)], D, "swa_out_wgrad", also_bf16=True)
    dqg1, dkv1, dsink = _swa_bwd(q1, kv1, dy1, gate1, o1, sinks, tab_k, mask_bias, HQ, HKV)
    dkv1f = _swa_dkv_finish(dkv1, tab_k)
    parts1 = [(dqg1, 0, 0), (dkv1f, None, WQ), (dqg1, 1, WQ + 2 * KVW)]
    g_si, g_si_h = _matmul_tn(h1, parts1, np_s, "swa_in_wgrad", tile_major=True, also_bf16=True)
    dh1 = _matmul_nt(parts1, w_si, D, "swa_in_bwd")
    dx1, dx1b, dg1 = _rmsnorm_bwd(dh1, x1, g1, dx2, "rmsnorm1_bwd")

    qaug0, doaug0, dqkvg0 = _fox_out_bwd(dx1b, w_fo, qkv0, o0, a0, H)
    g_fo, g_fo_h = _matmul_tn(y0, [(dx1b, None, 0)], D, "fox_out_wgrad", also_bf16=True)
    early_specs = [("col", ws), ("row", rows_o), ("row", rows_o)]
    dqkvg0, dcr, dcc, early_recv = _fox_bwd(qaug0, doaug0, qkv0, c, dqkvg0, H, scatter=[g_si_h, g_fo_h, g_so_h],
                                           scatter_specs=early_specs)
    dfl, dbf = _fox_gate_bwd(fl, bias, dcr - dcc)
    parts0 = [(dqkvg0, "stack", 0), (dfl, None, 4 * W)]
    g_fi, g_fi_h = _matmul_tn(h0, parts0, np_f, "fox_in_wgrad", tile_major=True, also_bf16=True)
    spec_fi = ("col", wf)
    fi_sems, fi_src, fi_land, token = _scatter_start(g_fi_h, spec_fi)
    parts0[-1] = (dfl + token[0, 0].astype(BF16), None, 4 * W)
    dh0 = _matmul_nt(parts0, w_fi, D, "fox_in_bwd")
    dx0, _, dg0 = _rmsnorm_bwd(dh0, x0, g0, dx1, "rmsnorm0_bwd")

    red_si, gw_fo, gw_so = [_final_sum8(g_, r_, s_)
                            for g_, r_, s_ in zip([g_si, g_fo, g_so], early_recv, early_specs)]
    gw_si = lax.dynamic_slice(red_si, (0, (ws * me) % LANE), (D, ws))

    P = D
    dsink_v = dsink[:, :, 0].reshape(1, HQ)
    row3 = _pad_lanes(dbf[:, :H], 0, P) + _pad_lanes(dsink_v, LANE, P) + _pad_lanes(loss_p[:, :1], 2 * LANE, P)
    pack = _pad_rows(dg0, 0) + _pad_rows(dg1, 1) + _pad_rows(dgf, 2) + _pad_rows(row3, 3)

    d_fo, m_fo, v_fo = _adamw(fox_w_out[0], gw_fo, m_fox_w_out[0], v_fox_w_out[0], "adamw_fox_out")
    d_si, m_si, v_si = _adamw(swa_w_in[0], gw_si, m_swa_w_in[0], v_swa_w_in[0], "adamw_swa_in")
    d_so, m_so, v_so = _adamw(swa_w_out[0], gw_so, m_swa_w_out[0], v_swa_w_out[0], "adamw_swa_out")
    recv_fi = _scatter_wait(fi_sems, fi_src, fi_land, spec_fi, after=[dx0, pack, d_fo, d_si, d_so])
    red_fi = _final_sum8(g_fi, recv_fi, spec_fi)
    gw_fi = lax.dynamic_slice(red_fi, (0, (wf * me) % LANE), (D, wf))
    d_fi, m_fi, v_fi = _adamw(fox_w_in[0], gw_fi, m_fox_w_in[0], v_fox_w_in[0], "adamw_fox_in")

    tot = _all_reduce_small(pack, after=recv_fi)
    loss = tot[3, 2 * LANE]
    g_norm = tot[0:2]
    g_final = tot[2]
    g_bf = tot[3:4, 0:H]
    g_sinks = tot[3:4, LANE:LANE + HQ]

    def small_pack(ng, fg, bf, sk):
        r3 = _pad_lanes(bf, 0, P) + _pad_lanes(sk, LANE, P)
        return _pad_rows(ng, 0) + _pad_rows(fg[None, :], 2) + _pad_rows(r3, 3)

    sd, sm, sv = _adamw(small_pack(norm_g, final_g, fox_b_f, swa_sinks), tot,
                        small_pack(m_norm_g, m_final_g, m_fox_b_f, m_swa_sinks),
                        small_pack(v_norm_g, v_final_g, v_fox_b_f, v_swa_sinks), "adamw_small")

    def unpack(t):
        return t[0:2], t[3:4, 0:H], t[3:4, LANE:LANE + HQ], t[2]

    def group(small, fi, fo, si, so):
        ng, bf, sk, fg = unpack(small)
        return (ng, fi[None], bf, fo[None], si[None], sk, so[None], fg)

    grads = (g_norm, gw_fi[None], g_bf, gw_fo[None], gw_si[None], g_sinks, gw_so[None], g_final)
    return (loss, dx0[None], *grads, *group(sd, d_fi, d_fo, d_si, d_so),
            *group(sm, m_fi, m_fo, m_si, m_so), *group(sv, v_fi, v_fo, v_si, v_so))
```

```python
import math

import jax
import jax.numpy as jnp
from jax import lax
from jax.experimental import pallas as pl
from jax.experimental.pallas import tpu as pltpu

F32 = jnp.float32
BF16 = jnp.bfloat16
MESH = pl.DeviceIdType.MESH

N_DEV = 8
LANE = 128
HEAD_DIM = 64
SWA_BLOCK = 128
NEG_INF = -1e30
RMS_EPS = 1e-6
ROPE_THETA = 500000.0
ROT_DIM = HEAD_DIM // 4
ADAM_LR, ADAM_B1, ADAM_B2, ADAM_EPS, ADAM_WD, ADAM_STEP = 0.001, 0.9, 0.999, 1e-08, 0.01, 10
VMEM_LIMIT = 56 * 1024 * 1024
MM_TILE = 1024
ATT_TILE = 512
EPI_TILE = 512
ROW_TILE = 256


def _cparams(**kw):
    return pltpu.CompilerParams(vmem_limit_bytes=VMEM_LIMIT, **kw)


def _tile(n, cap):
    if n <= cap:
        return n
    t = (cap // LANE) * LANE
    while t > LANE and n % t:
        t -= LANE
    assert n % t == 0, (n, cap)
    return t


def _row_tile(n, cap):
    t = min(n, cap)
    while n % t:
        t //= 2
    return t


def _dot_nn(a, b):
    return jnp.dot(a, b, preferred_element_type=F32)


def _dot_nt(a, b):
    return lax.dot_general(a, b, (((1,), (1,)), ((), ())), preferred_element_type=F32)


def _dot_tn(a, b):
    return lax.dot_general(a, b, (((0,), (0,)), ((), ())), preferred_element_type=F32)


def _split3(x):
    hi = x.astype(BF16)
    r1 = x - hi.astype(F32)
    mid = r1.astype(BF16)
    return hi, mid, (r1 - mid.astype(F32)).astype(BF16)


def _sigmoid(g):
    return 1.0 / (1.0 + jnp.exp(-g))


def _slab_geom(w):
    starts = [w * i for i in range(N_DEV)]
    aligned = [LANE * (s // LANE) for s in starts]
    offs = [s - a for s, a in zip(starts, aligned)]
    sw = LANE * (-(-(max(offs) + w) // LANE))
    return aligned, sw, aligned[-1] + sw


def _my_index():
    return 4 * lax.axis_index("x") + 2 * lax.axis_index("y") + lax.axis_index("c")


def _all_gather(arrs):
    n = len(arrs)

    def body(*refs):
        ins, outs = refs[:n], refs[n:2 * n]
        send_sems, recv_sems, local_sems = refs[2 * n:]
        x, y, c = lax.axis_index("x"), lax.axis_index("y"), lax.axis_index("c")
        me, sib = (x, y, c), (x, y, 1 - c)
        chips = [(1 - x, y), (x, 1 - y), (1 - x, 1 - y)]

        def idx(px, py, pc):
            return 4 * px + 2 * py + pc

        def copy(a, k, block, to, src=None):
            dst = outs[a].at[idx(*block)]
            return pltpu.make_async_remote_copy(
                src_ref=dst if src is None else src, dst_ref=dst,
                send_sem=send_sems.at[a, k], recv_sem=recv_sems.at[a, k],
                device_id=to, device_id_type=MESH)

        mine = [pltpu.make_async_copy(ins[a], outs[a].at[idx(*me)], local_sems.at[a]) for a in range(n)]
        for m in mine:
            m.start()
        first = []
        for a in range(n):
            first.append(copy(a, 0, me, sib, src=ins[a]))
            for j, chip in enumerate(chips):
                first.append(copy(a, 1 + j, me, (*chip, c), src=ins[a]))
        for cp in first:
            cp.start()
        passed = []
        for j, chip in enumerate(chips):
            for a in range(n):
                copy(a, 1 + j, (*chip, c), me).wait_recv()
                p = copy(a, 4 + j, (*chip, c), sib)
                p.start()
                passed.append(p)
        for a in range(n):
            copy(a, 0, sib, me).wait_recv()
        for j, chip in enumerate(chips):
            for a in range(n):
                copy(a, 4 + j, (*chip, 1 - c), me).wait_recv()
        for cp in first + passed:
            cp.wait_send()
        for m in mine:
            m.wait()

    any_spec = pl.BlockSpec(memory_space=pl.ANY)
    return pl.pallas_call(
        body, name="weights_all_gather",
        out_shape=[jax.ShapeDtypeStruct((N_DEV,) + a.shape, a.dtype) for a in arrs],
        in_specs=[any_spec] * n, out_specs=[any_spec] * n,
        scratch_shapes=[pltpu.SemaphoreType.DMA((n, 7)), pltpu.SemaphoreType.DMA((n, 7)),
                        pltpu.SemaphoreType.DMA((n,))],
    )(*arrs)


def _rs_windows(specs):
    def window(ref, spec, blk):
        kind, n = spec
        if kind == "col":
            _, sw, _ = _slab_geom(n)
            return ref.at[pl.ds((n * blk) // LANE, sw // LANE)]
        start = pl.multiple_of(n * blk, n)
        return ref.at[pl.ds(start, n), :]
    return window


def _peer(k):
    x, y, c = lax.axis_index("x"), lax.axis_index("y"), lax.axis_index("c")
    return (x ^ (k >> 2), y ^ ((k >> 1) & 1), c ^ (k & 1))


def _direct_gather_copies(ins, outs, send_sems, recv_sems, local_sems):
    me = _my_index()
    remote, local = [], []
    for a, (src, dst) in enumerate(zip(ins, outs)):
        local.append(pltpu.make_async_copy(src, dst.at[me], local_sems.at[a]))
        for k in range(1, N_DEV):
            remote.append(pltpu.make_async_remote_copy(
                src_ref=src, dst_ref=dst.at[me], send_sem=send_sems.at[a, k - 1], recv_sem=recv_sems.at[a, k - 1],
                device_id=_peer(k), device_id_type=MESH))
    return remote, local


def _direct_scatter_copies(ins, outs, specs, send_sems, recv_sems):
    window = _rs_windows(specs)
    remote = []
    for a, (src, dst) in enumerate(zip(ins, outs)):
        for k in range(1, N_DEV):
            px, py, pc = _peer(k)
            remote.append(pltpu.make_async_remote_copy(
                src_ref=window(src, specs[a], 4 * px + 2 * py + pc), dst_ref=dst.at[k - 1],
                send_sem=send_sems.at[a, k - 1], recv_sem=recv_sems.at[a, k - 1],
                device_id=(px, py, pc), device_id_type=MESH))
    return remote


def _scatter_block_shape(g, spec):
    kind, w = spec
    return (_slab_geom(w)[1] // LANE, g.shape[1], LANE) if kind == "col" else (w, g.shape[1])


def _wait_all(remote, local=()):
    for cp in remote:
        cp.wait_recv()
    for cp in remote:
        cp.wait_send()
    for cp in local:
        cp.wait()


def _scatter_start(g, spec):
    blk = _scatter_block_shape(g, spec)
    window = _rs_windows([spec])
    npeer = N_DEV - 1

    def body(g_ref, land_ref, *rest):
        sems = rest[:2 * npeer]
        token = rest[2 * npeer + 2]
        for cp in _peer_block_copies(g_ref, land_ref, spec, window, sems[:npeer], sems[npeer:]):
            cp.start()
        token[...] = jnp.zeros(token.shape, token.dtype)

    hbm = pl.BlockSpec(memory_space=pltpu.HBM)
    sem = pl.BlockSpec(memory_space=pltpu.SEMAPHORE)
    land = lax.empty((npeer,) + blk, g.dtype)
    outs = pl.pallas_call(
        body, name="grads_scatter_start",
        out_shape=(pltpu.SemaphoreType.DMA(()),) * (2 * npeer)
        + (pltpu.HBM(g.shape, g.dtype), pltpu.HBM(land.shape, land.dtype), jax.ShapeDtypeStruct((8, LANE), F32)),
        in_specs=(hbm, hbm),
        out_specs=(sem,) * (2 * npeer) + (hbm, hbm, pl.BlockSpec(memory_space=pltpu.VMEM)),
        input_output_aliases={0: 2 * npeer, 1: 2 * npeer + 1},
        compiler_params=pltpu.CompilerParams(has_side_effects=pltpu.SideEffectType.DATAFLOW_SIDE_EFFECTING),
    )(pltpu.with_memory_space_constraint(g, pltpu.HBM), pltpu.with_memory_space_constraint(land, pltpu.HBM))
    return outs[:2 * npeer], outs[2 * npeer], outs[2 * npeer + 1], outs[2 * npeer + 2]


def _peer_block_copies(g_ref, land_ref, spec, window, send_sems, recv_sems):
    copies = []
    for k in range(1, N_DEV):
        px, py, pc = _peer(k)
        copies.append(pltpu.make_async_remote_copy(
            src_ref=window(g_ref, spec, 4 * px + 2 * py + pc), dst_ref=land_ref.at[k - 1],
            send_sem=send_sems[k - 1], recv_sem=recv_sems[k - 1], device_id=(px, py, pc), device_id_type=MESH))
    return copies


def _scatter_wait(sems, g_thru, land_thru, spec, after):
    window = _rs_windows([spec])
    npeer = N_DEV - 1

    def body(g_ref, land_ref, *rest):
        s = rest[:2 * npeer]
        copies = _peer_block_copies(g_ref, land_ref, spec, window, s[:npeer], s[npeer:])
        for cp in copies:
            cp.wait_send()
        for cp in copies:
            cp.wait_recv()

    hbm = pl.BlockSpec(memory_space=pltpu.HBM)
    sem = pl.BlockSpec(memory_space=pltpu.SEMAPHORE)
    return pl.pallas_call(
        body, name="grads_scatter_wait",
        out_shape=(pltpu.HBM(g_thru.shape, g_thru.dtype), pltpu.HBM(land_thru.shape, land_thru.dtype)),
        in_specs=(hbm, hbm) + (sem,) * (2 * npeer) + (pl.BlockSpec(memory_space=pl.ANY),) * len(after),
        out_specs=(hbm, hbm), input_output_aliases={0: 0, 1: 1},
        compiler_params=pltpu.CompilerParams(has_side_effects=pltpu.SideEffectType.DATAFLOW_SIDE_EFFECTING),
    )(g_thru, land_thru, *sems, *after)[1]


def _final_sum8(g, recv, spec):
    kind, n = spec
    me = _my_index()
    offs = jnp.stack([(n * me) // LANE if kind == "col" else me]).astype(jnp.int32)
    if kind == "col":
        _, T, M, _ = recv.shape
        grid = (T,)
        in_specs = [pl.BlockSpec((1, M, LANE), lambda t, o: (o[0] + t, 0, 0)),
                    pl.BlockSpec((N_DEV - 1, 1, M, LANE), lambda t, o: (0, t, 0, 0))]
        out_spec = pl.BlockSpec((M, LANE), lambda t, o: (0, t))
        out_shape = jax.ShapeDtypeStruct((M, T * LANE), F32)
    else:
        _, nrow, C = recv.shape
        grid = (1,)
        in_specs = [pl.BlockSpec((nrow, C), lambda t, o: (o[0], 0)),
                    pl.BlockSpec((N_DEV - 1, nrow, C), lambda t, o: (0, 0, 0))]
        out_spec = pl.BlockSpec((nrow, C), lambda t, o: (0, 0))
        out_shape = jax.ShapeDtypeStruct((nrow, C), F32)

    def body(o_ref, g_ref, r_ref, out_ref):
        acc = g_ref[0] if kind == "col" else g_ref[...]
        for k in range(N_DEV - 1):
            acc = acc + (r_ref[k, 0] if kind == "col" else r_ref[k]).astype(F32)
        out_ref[...] = acc

    return pl.pallas_call(
        body, name="grads_final_sum8",
        grid_spec=pltpu.PrefetchScalarGridSpec(num_scalar_prefetch=1, grid=grid, in_specs=in_specs,
                                               out_specs=out_spec),
        out_shape=out_shape, compiler_params=_cparams(),
    )(offs, g, recv)


def _all_reduce_small(pack, after):
    R, P = pack.shape

    def body(x_ref, after_ref, o_ref, gat_ref, send_sems, recv_sems):
        x, y, c = lax.axis_index("x"), lax.axis_index("y"), lax.axis_index("c")
        me = 4 * x + 2 * y + c
        gat_ref[me] = x_ref[...]
        copies = []
        for k in range(1, N_DEV):
            peer = (x ^ (k >> 2), y ^ ((k >> 1) & 1), c ^ (k & 1))
            copies.append(pltpu.make_async_remote_copy(
                src_ref=x_ref, dst_ref=gat_ref.at[me],
                send_sem=send_sems.at[k - 1], recv_sem=recv_sems.at[k - 1],
                device_id=peer, device_id_type=MESH))
        for cp in copies:
            cp.start()
        for cp in copies:
            cp.wait_recv()
        for cp in copies:
            cp.wait_send()
        acc = gat_ref[0]
        for d in range(1, N_DEV):
            acc = acc + gat_ref[d]
        o_ref[...] = acc

    vm = pl.BlockSpec(memory_space=pltpu.VMEM)
    return pl.pallas_call(
        body, name="small_all_reduce",
        out_shape=jax.ShapeDtypeStruct((R, P), F32),
        in_specs=[vm, pl.BlockSpec(memory_space=pl.ANY)], out_specs=vm,
        scratch_shapes=[pltpu.VMEM((N_DEV, R, P), F32),
                        pltpu.SemaphoreType.DMA((N_DEV - 1,)), pltpu.SemaphoreType.DMA((N_DEV - 1,))],
    )(pack, after)


def _assemble(slabs, w):
    aligned, sw, total = _slab_geom(w)
    K = slabs.shape[1]
    tr = _row_tile(K, ROW_TILE)

    def body(s_ref, o_ref):
        o_ref[...] = jnp.zeros(o_ref.shape, BF16)
        for i in range(N_DEV):
            a, off = aligned[i], w * i - aligned[i]
            x = s_ref[i].astype(F32)
            if off:
                x = pltpu.roll(x, off, axis=1)
            o_ref[:, a:a + sw] = (o_ref[:, a:a + sw].astype(F32) + x).astype(BF16)

    return pl.pallas_call(
        body, name="assemble_w_in", grid=(K // tr,),
        in_specs=[pl.BlockSpec((N_DEV, tr, sw), lambda i: (0, i, 0))],
        out_specs=pl.BlockSpec((tr, total), lambda i: (i, 0)),
        out_shape=jax.ShapeDtypeStruct((K, total), BF16),
        compiler_params=_cparams(),
    )(slabs)


def _rmsnorm_fwd(x, g, name):
    S, D = x.shape
    tm = _row_tile(S, ROW_TILE)

    def body(x_ref, g_ref, h_ref):
        xv = x_ref[...]
        r = lax.rsqrt(jnp.mean(xv * xv, axis=-1, keepdims=True) + RMS_EPS)
        h_ref[...] = ((xv * r) * g_ref[...]).astype(BF16)

    return pl.pallas_call(
        body, name=name, grid=(S // tm,),
        in_specs=[pl.BlockSpec((tm, D), lambda i: (i, 0)), pl.BlockSpec((1, D), lambda i: (0, 0))],
        out_specs=pl.BlockSpec((tm, D), lambda i: (i, 0)),
        out_shape=jax.ShapeDtypeStruct((S, D), BF16),
        compiler_params=_cparams(),
    )(x, g)


def _rmsnorm_bwd(dh, x, g, dres, name):
    S, D = x.shape
    tm = _row_tile(S, ROW_TILE)

    def body(dh_ref, x_ref, g_ref, dr_ref, dx_ref, dxb_ref, dg_ref):
        xv = x_ref[...]
        r = lax.rsqrt(jnp.mean(xv * xv, axis=-1, keepdims=True) + RMS_EPS)
        xhat = xv * r
        d = dh_ref[...]
        gd = d * g_ref[...]
        dx = r * (gd - xhat * jnp.mean(gd * xhat, axis=-1, keepdims=True)) + dr_ref[...]
        dx_ref[...] = dx
        dxb_ref[...] = dx.astype(BF16)

        @pl.when(pl.program_id(0) == 0)
        def _():
            dg_ref[...] = jnp.zeros(dg_ref.shape, F32)
        dg_ref[...] += jnp.sum(d * xhat, axis=0, keepdims=True)

    row = pl.BlockSpec((tm, D), lambda i: (i, 0))
    vec = pl.BlockSpec((1, D), lambda i: (0, 0))
    return pl.pallas_call(
        body, name=name, grid=(S // tm,),
        in_specs=[row, row, vec, row], out_specs=[row, row, vec],
        out_shape=[jax.ShapeDtypeStruct((S, D), F32), jax.ShapeDtypeStruct((S, D), BF16),
                   jax.ShapeDtypeStruct((1, D), F32)],
        compiler_params=_cparams(),
    )(dh, x, g, dres)


def _adamw(w, g, m, v, name):
    R, C = w.shape
    tr = _row_tile(R, ROW_TILE)
    c1 = 1.0 - ADAM_B1 ** ADAM_STEP
    c2 = 1.0 - ADAM_B2 ** ADAM_STEP

    def body(w_ref, g_ref, m_ref, v_ref, d_ref, nm_ref, nv_ref):
        gv = g_ref[...]
        nm = ADAM_B1 * m_ref[...] + (1.0 - ADAM_B1) * gv
        nv = ADAM_B2 * v_ref[...] + (1.0 - ADAM_B2) * (gv * gv)
        d_ref[...] = -ADAM_LR * ((nm / c1) / (jnp.sqrt(nv / c2) + ADAM_EPS) + ADAM_WD * w_ref[...])
        nm_ref[...] = nm
        nv_ref[...] = nv

    spec = pl.BlockSpec((tr, C), lambda i: (i, 0))
    return pl.pallas_call(
        body, name=name, grid=(R // tr,),
        in_specs=[spec] * 4, out_specs=[spec] * 3,
        out_shape=[jax.ShapeDtypeStruct((R, C), F32)] * 3,
        compiler_params=_cparams(),
    )(w, g, m, v)


def _proj(h, wfull, col0, ncols, out_dtype, name, rope=None):
    S, K = h.shape
    tm = _row_tile(S, MM_TILE)
    tn = math.gcd(_tile(ncols, MM_TILE), col0) if col0 else _tile(ncols, MM_TILE)
    if rope is not None:
        tn = _tile(math.gcd(ncols, rope[1]), MM_TILE)
    assert ncols % tn == 0 and col0 % tn == 0
    cb = col0 // tn

    def body(*refs):
        if rope is None:
            a_ref, b_ref, o_ref = refs
        else:
            a_ref, b_ref, t_ref, o_ref = refs
        acc = _dot_nn(a_ref[...], b_ref[...])
        if rope is not None:
            t0, t1, t2 = (jnp.tile(t_ref[i], (1, tn // LANE)) for i in range(3))
            roped = (acc * t0 + pltpu.roll(acc, tn - ROT_DIM // 2, axis=1) * t1
                     + pltpu.roll(acc, ROT_DIM // 2, axis=1) * t2)
            acc = jnp.where(pl.program_id(1) < rope[1] // tn, roped, acc)
        o_ref[...] = acc.astype(out_dtype)

    in_specs = [pl.BlockSpec((tm, K), lambda i, j: (i, 0)), pl.BlockSpec((K, tn), lambda i, j: (0, cb + j))]
    args = [h, wfull]
    if rope is not None:
        in_specs.append(pl.BlockSpec((3, tm, LANE), lambda i, j: (0, i, 0)))
        args.append(rope[0])
    return pl.pallas_call(
        body, name=name, grid=(S // tm, ncols // tn),
        in_specs=in_specs, out_specs=pl.BlockSpec((tm, tn), lambda i, j: (i, j)),
        out_shape=jax.ShapeDtypeStruct((S, ncols), out_dtype),
        compiler_params=_cparams(),
    )(*args)


def _out_proj_norm(y, wo, xres, g, name):
    S, W = y.shape
    D = wo.shape[1]
    tm = _row_tile(S, EPI_TILE)

    def body(a_ref, b_ref, r_ref, g_ref, x_ref, h_ref):
        xv = r_ref[...] + _dot_nn(a_ref[...], b_ref[...])
        x_ref[...] = xv
        r = lax.rsqrt(jnp.mean(xv * xv, axis=-1, keepdims=True) + RMS_EPS)
        h_ref[...] = ((xv * r) * g_ref[...]).astype(BF16)

    row = pl.BlockSpec((tm, D), lambda i: (i, 0))
    return pl.pallas_call(
        body, name=name, grid=(S // tm,),
        in_specs=[pl.BlockSpec((tm, W), lambda i: (i, 0)), pl.BlockSpec((W, D), lambda i: (0, 0)), row,
                  pl.BlockSpec((1, D), lambda i: (0, 0))],
        out_specs=[row, row],
        out_shape=[jax.ShapeDtypeStruct((S, D), F32), jax.ShapeDtypeStruct((S, D), BF16)],
        compiler_params=_cparams(),
    )(y, wo, xres, g)


def _out_proj_loss(y, wo, xres, tgt, g, name):
    S, W = y.shape
    D = wo.shape[1]
    tm = _row_tile(S, EPI_TILE)

    def body(a_ref, b_ref, r_ref, t_ref, g_ref, dx_ref, dxb_ref, dg_ref, loss_ref):
        xv = r_ref[...] + _dot_nn(a_ref[...], b_ref[...])
        r = lax.rsqrt(jnp.mean(xv * xv, axis=-1, keepdims=True) + RMS_EPS)
        xhat = xv * r
        gv = g_ref[...]
        err = xhat * gv - t_ref[...]
        d = err * (1.0 / D)
        gd = d * gv
        dx = r * (gd - xhat * jnp.mean(gd * xhat, axis=-1, keepdims=True))
        dx_ref[...] = dx
        dxb_ref[...] = dx.astype(BF16)

        @pl.when(pl.program_id(0) == 0)
        def _():
            dg_ref[...] = jnp.zeros(dg_ref.shape, F32)
            loss_ref[...] = jnp.zeros(loss_ref.shape, F32)
        dg_ref[...] += jnp.sum(d * xhat, axis=0, keepdims=True)
        per_tok = jnp.sum(err * err, axis=-1, keepdims=True) * (1.0 / D)
        loss_ref[...] += 0.5 * jnp.sum(per_tok, axis=0, keepdims=True)

    row = pl.BlockSpec((tm, D), lambda i: (i, 0))
    vec = pl.BlockSpec((1, D), lambda i: (0, 0))
    return pl.pallas_call(
        body, name=name, grid=(S // tm,),
        in_specs=[pl.BlockSpec((tm, W), lambda i: (i, 0)), pl.BlockSpec((W, D), lambda i: (0, 0)), row, row, vec],
        out_specs=[row, row, vec, pl.BlockSpec((1, LANE), lambda i: (0, 0))],
        out_shape=[jax.ShapeDtypeStruct((S, D), F32), jax.ShapeDtypeStruct((S, D), BF16),
                   jax.ShapeDtypeStruct((1, D), F32), jax.ShapeDtypeStruct((1, LANE), F32)],
        compiler_params=_cparams(),
    )(y, wo, xres, tgt, g)


def _matmul_nt(parts, wfull, out_rows, name):
    S = parts[0][0].shape[-2]
    tm, tn = _row_tile(S, 2 * MM_TILE if len(parts) <= 2 else MM_TILE), _tile(out_rows, MM_TILE)
    plan, lo = [], 0
    for arr, lead, col0 in parts:
        n_p = arr.shape[-1]
        tk = math.gcd(_tile(n_p, MM_TILE), col0) if col0 else _tile(n_p, MM_TILE)
        steps = n_p // tk * (arr.shape[0] if lead == "stack" else 1)
        plan.append((lead, col0 // tk, tk, lo, lo + steps))
        lo += steps
    nk = lo
    npart = len(parts)

    def body(*refs):
        a_refs, w_refs = refs[:npart], refs[npart:2 * npart]
        o_ref, acc_ref = refs[2 * npart], refs[2 * npart + 1]
        k = pl.program_id(2)

        @pl.when(k == 0)
        def _():
            acc_ref[...] = jnp.zeros(acc_ref.shape, F32)
        for p, (_, _, _, lo_p, hi_p) in enumerate(plan):
            @pl.when((k >= lo_p) & (k < hi_p))
            def _(p=p):
                acc_ref[...] += _dot_nt(a_refs[p][...], w_refs[p][...])

        @pl.when(k == nk - 1)
        def _():
            o_ref[...] = acc_ref[...]

    in_specs, args = [], []
    for (arr, lead, col0), (_, cb, tk, lo_p, hi_p) in zip(parts, plan):
        def kk(k, lo_p=lo_p, hi_p=hi_p):
            return jnp.clip(k - lo_p, 0, hi_p - lo_p - 1)
        if lead is None:
            in_specs.append(pl.BlockSpec((tm, tk), lambda i, j, k, kk=kk: (i, kk(k))))
        elif lead == "stack":
            nkb = arr.shape[-1] // tk
            in_specs.append(pl.BlockSpec((None, tm, tk), lambda i, j, k, kk=kk, nkb=nkb: (kk(k) // nkb, i, kk(k) % nkb)))
        else:
            in_specs.append(pl.BlockSpec((None, tm, tk), lambda i, j, k, kk=kk, lead=lead: (lead, i, kk(k))))
        args.append(arr)
    for (_, cb, tk, lo_p, hi_p) in plan:
        def kk(k, lo_p=lo_p, hi_p=hi_p):
            return jnp.clip(k - lo_p, 0, hi_p - lo_p - 1)
        in_specs.append(pl.BlockSpec((tn, tk), lambda i, j, k, kk=kk, cb=cb: (j, cb + kk(k))))
        args.append(wfull)
    return pl.pallas_call(
        body, name=name, grid=(S // tm, out_rows // tn, nk),
        in_specs=in_specs, out_specs=pl.BlockSpec((tm, tn), lambda i, j, k: (i, j)),
        out_shape=jax.ShapeDtypeStruct((S, out_rows), F32),
        scratch_shapes=[pltpu.VMEM((tm, tn), F32)],
        compiler_params=_cparams(),
    )(*args)


def _matmul_tn(a, parts, total, name, tile_major=False, also_bf16=False):
    S, M = a.shape
    tm, ts = _tile(M, MM_TILE), _row_tile(S, 2 * MM_TILE)
    nout = 2 if also_bf16 else 1
    outs = None
    for idx, (arr, lead, col0) in enumerate(parts):
        n_p = arr.shape[-1]
        tn = math.gcd(_tile(n_p, MM_TILE), col0) if col0 else _tile(n_p, MM_TILE)
        cb = col0 // tn
        nk = S // ts
        nb = n_p // tn
        if lead == "stack":
            n_p *= arr.shape[0]

        def body(*refs, nk=nk, tn=tn):
            a_ref, b_ref = refs[0], refs[1]
            o_refs, acc_ref = refs[-1 - nout:-1], refs[-1]
            k = pl.program_id(2)

            @pl.when(k == 0)
            def _():
                acc_ref[...] = jnp.zeros(acc_ref.shape, F32)
            acc_ref[...] += _dot_tn(a_ref[...], b_ref[...])

            @pl.when(k == nk - 1)
            def _():
                for o_ref in o_refs:
                    if tile_major:
                        for t in range(tn // LANE):
                            o_ref[t] = acc_ref[:, LANE * t:LANE * (t + 1)].astype(o_ref.dtype)
                    else:
                        o_ref[...] = acc_ref[...].astype(o_ref.dtype)

        in_specs = [pl.BlockSpec((ts, tm), lambda i, j, k: (k, i))]
        if lead is None:
            in_specs.append(pl.BlockSpec((ts, tn), lambda i, j, k: (k, j)))
        elif lead == "stack":
            in_specs.append(pl.BlockSpec((None, ts, tn), lambda i, j, k, nb=nb: (j // nb, k, j % nb)))
        else:
            in_specs.append(pl.BlockSpec((None, ts, tn), lambda i, j, k, lead=lead: (lead, k, j)))
        args = [a, arr]
        aliases = {}
        if outs is not None:
            in_specs += [pl.BlockSpec(memory_space=pl.ANY)] * nout
            args += list(outs)
            aliases = {2 + o: o for o in range(nout)}
        if tile_major:
            out_spec = pl.BlockSpec((tn // LANE, tm, LANE), lambda i, j, k, cb=cb: (cb + j, i, 0))
            shape = (total // LANE, M, LANE)
        else:
            out_spec = pl.BlockSpec((tm, tn), lambda i, j, k, cb=cb: (i, cb + j))
            shape = (M, total)
        outs = pl.pallas_call(
            body, name=f"{name}_{idx}", grid=(M // tm, n_p // tn, nk),
            in_specs=in_specs, out_specs=[out_spec] * nout,
            out_shape=[jax.ShapeDtypeStruct(shape, dt) for dt in (F32, BF16)[:nout]],
            scratch_shapes=[pltpu.VMEM((tm, tn), F32)],
            input_output_aliases=aliases,
            compiler_params=_cparams(),
        )(*args)
    return tuple(outs) if also_bf16 else outs[0]


def _log_sigmoid(z):
    e = jnp.exp(-jnp.abs(z))
    return jnp.minimum(z, 0.0) - jnp.where(e < 1e-4, e * (1.0 - 0.5 * e), jnp.log(1.0 + e))


def _tri_sum(tri, x):
    hi, mid, lo = _split3(x)
    return _dot_nn(tri, hi) + _dot_nn(tri, mid) + _dot_nn(tri, lo)


def _fox_gate_fwd(fl, bias):
    S = fl.shape[0]

    nb_ = _row_tile(S, ROW_TILE)

    def body(f_ref, b_ref, c_ref):
        ri = lax.broadcasted_iota(jnp.int32, (nb_, nb_), 0)
        ci = lax.broadcasted_iota(jnp.int32, (nb_, nb_), 1)
        tri = jnp.where(ri >= ci, 1.0, 0.0).astype(BF16)
        row = lax.broadcasted_iota(jnp.int32, (nb_, LANE), 0)

        def step(i, carry):
            r0 = pl.multiple_of(i * nb_, nb_)
            t = _tri_sum(tri, _log_sigmoid(f_ref[pl.ds(r0, nb_), :] + b_ref[...])) + carry
            c_ref[pl.ds(r0, nb_), :] = t
            return jnp.sum(jnp.where(row == nb_ - 1, t, 0.0), axis=0, keepdims=True)

        lax.fori_loop(0, S // nb_, step, jnp.zeros((1, LANE), F32))

    vm = pl.BlockSpec(memory_space=pltpu.VMEM)
    return pl.pallas_call(
        body, name="fox_gate_fwd", in_specs=[vm, vm], out_specs=vm,
        out_shape=jax.ShapeDtypeStruct((S, LANE), F32),
        compiler_params=_cparams(),
    )(fl, bias)


def _fox_gate_bwd(fl, bias, dc):
    S = fl.shape[0]

    nb_ = _row_tile(S, ROW_TILE)

    def body(f_ref, b_ref, d_ref, o_ref, db_ref):
        ri = lax.broadcasted_iota(jnp.int32, (nb_, nb_), 0)
        ci = lax.broadcasted_iota(jnp.int32, (nb_, nb_), 1)
        tri = jnp.where(ri <= ci, 1.0, 0.0).astype(BF16)
        row = lax.broadcasted_iota(jnp.int32, (nb_, LANE), 0)
        nt = S // nb_

        def step(ii, carry):
            carry_c, carry_b = carry
            r0 = pl.multiple_of((nt - 1 - ii) * nb_, nb_)
            t = _tri_sum(tri, d_ref[pl.ds(r0, nb_), :]) + carry_c
            dz = t * _sigmoid(-(f_ref[pl.ds(r0, nb_), :] + b_ref[...]))
            o_ref[pl.ds(r0, nb_), :] = dz.astype(BF16)
            first = jnp.sum(jnp.where(row == 0, t, 0.0), axis=0, keepdims=True)
            return first, carry_b + jnp.sum(dz, axis=0, keepdims=True)

        zero = jnp.zeros((1, LANE), F32)
        _, db = lax.fori_loop(0, nt, step, (zero, zero))
        db_ref[...] = db

    vm = pl.BlockSpec(memory_space=pltpu.VMEM)
    return pl.pallas_call(
        body, name="fox_gate_bwd", in_specs=[vm, vm, vm], out_specs=[vm, vm],
        out_shape=[jax.ShapeDtypeStruct((S, LANE), BF16), jax.ShapeDtypeStruct((1, LANE), F32)],
        compiler_params=_cparams(),
    )(fl, bias, dc)


def _bias_lanes(col, lane, e, first):
    o0 = HEAD_DIM * (1 - e)
    hi, mid, lo = _split3(col)
    d0 = o0 if first else o0 + 3
    t = jnp.where((lane >= o0) & (lane < o0 + 6), jnp.ones(lane.shape, BF16), jnp.zeros(lane.shape, BF16))
    t = jnp.where(lane == d0, hi, t)
    t = jnp.where(lane == d0 + 1, mid, t)
    return jnp.where(lane == d0 + 2, lo, t)


def _fox_fwd(qkvg, c, H, gather=()):
    na = len(gather)
    S = qkvg.shape[0]
    W = H * HEAD_DIM
    HP = H // 2
    PP = 2 if HP % 2 == 0 else 1
    NE = 2 * PP
    tq = _row_tile(S, ATT_TILE)
    nq = S // tq
    wb = W // LANE
    scale = HEAD_DIM ** -0.5

    def body(*refs):
        q_ref, k_ref, v_ref, g_ref, c_ref = refs[:5]
        y_ref, o_ref, a_ref = refs[5 + na:8 + na]
        kaug_sc, vaug_sc, qaug_sc, s_sc, mb_sc, m_sc, acc_sc = refs[8 + 2 * na:15 + 2 * na]
        hp, qi = pl.program_id(0), pl.program_id(1)
        if na:
            remote, local = _direct_gather_copies(refs[5:5 + na], refs[8 + na:8 + 2 * na], *refs[15 + 2 * na:])

            @pl.when((hp == 0) & (qi == 0))
            def _():
                for cp in remote + local:
                    cp.start()
        lane = lax.broadcasted_iota(jnp.int32, (tq, LANE), 1)
        own = [lane < HEAD_DIM, lane >= HEAD_DIM]
        rows = lax.broadcasted_iota(jnp.int32, (tq, tq), 0)
        cols = lax.broadcasted_iota(jnp.int32, (tq, tq), 1)

        def bias_lanes(col, e, first):
            return _bias_lanes(col, lane, e % 2, first)

        def head_col(tile, e):
            return jnp.sum(jnp.where(lane == 2 * PP * hp + e, tile, 0.0), axis=1, keepdims=True)

        def tile_of(e):
            return slice(LANE * (e // 2), LANE * (e // 2 + 1))

        @pl.when(qi == 0)
        def _():
            def chunk(i, carry):
                r0 = pl.multiple_of(i * tq, tq)
                cb = c_ref[pl.ds(r0, tq), :]
                for e in range(NE):
                    kb, vb = k_ref[pl.ds(r0, tq), tile_of(e)], v_ref[pl.ds(r0, tq), tile_of(e)]
                    kaug_sc[e, pl.ds(r0, tq), :] = jnp.where(own[e % 2], kb, bias_lanes(-head_col(cb, e), e, False))
                    vaug_sc[e, pl.ds(r0, tq), :] = jnp.where(own[e % 2], vb, jnp.ones((tq, LANE), BF16))
                return carry
            lax.fori_loop(0, nq, chunk, 0)

        crow = c_ref[pl.ds(pl.multiple_of(qi * tq, tq), tq), :]
        ctq = [head_col(crow, e) for e in range(NE)]
        for e in range(NE):
            q = q_ref[:, tile_of(e)] * jnp.asarray(scale, BF16)
            qaug_sc[e] = jnp.where(own[e % 2], q, bias_lanes(ctq[e], e, True))
        m_sc[...] = jnp.full(m_sc.shape, NEG_INF, F32)
        acc_sc[...] = jnp.zeros(acc_sc.shape, F32)

        def scores(blk, slot, masked):
            k0 = pl.multiple_of(blk * tq, tq)
            for e in range(NE):
                s = _dot_nt(qaug_sc[e], kaug_sc[e, pl.ds(k0, tq), :])
                if masked:
                    s = jnp.where(rows >= cols, s, NEG_INF)
                s_sc[slot, e] = s
                mb_sc[slot, e] = jnp.broadcast_to(jnp.max(s, axis=1, keepdims=True), (tq, LANE))

        def accumulate(blk, slot):
            k0 = pl.multiple_of(blk * tq, tq)
            for e in range(NE):
                m_prev = m_sc[e]
                m_new = jnp.maximum(m_prev, mb_sc[slot, e])
                p = jnp.exp(s_sc[slot, e] - jnp.tile(m_new, (1, tq // LANE)))
                acc_sc[e] = jnp.exp(m_prev - m_new) * acc_sc[e] + _dot_nn(p.astype(BF16), vaug_sc[e, pl.ds(k0, tq), :])
                m_sc[e] = m_new

        def block_of(t):
            return jnp.where(t == 0, qi, t - 1)

        scores(qi, 0, True)

        def loop_body(t, carry):
            scores(t, (t + 1) % 2, False)
            accumulate(block_of(t), t % 2)
            return carry

        lax.fori_loop(0, qi, loop_body, 0)
        accumulate(block_of(qi), qi % 2)
        o_e, a_e = [], []
        for e in range(NE):
            acc = acc_sc[e]
            l = pltpu.roll(acc, HEAD_DIM, axis=1)
            o_e.append(acc / l)
            a_e.append(ctq[e] - (m_sc[e] + jnp.log(l)))
        for pp in range(PP):
            o = jnp.where(own[0], o_e[2 * pp], o_e[2 * pp + 1])
            g = g_ref[:, tile_of(2 * pp)].astype(F32)
            y_ref[:, tile_of(2 * pp)] = (o * (g * _sigmoid(g))).astype(BF16)
            o_ref[:, tile_of(2 * pp)] = o.astype(BF16)
            a_ref[pp] = jnp.where(own[0], a_e[2 * pp], a_e[2 * pp + 1])
        if na:
            @pl.when((hp == HP // PP - 1) & (qi == nq - 1))
            def _():
                _wait_all(remote, local)

    any_spec = pl.BlockSpec(memory_space=pl.ANY)
    sems = [pltpu.SemaphoreType.DMA((na, N_DEV - 1)), pltpu.SemaphoreType.DMA((na, N_DEV - 1)),
            pltpu.SemaphoreType.DMA((na,))] if na else []
    wide = PP * LANE
    outs = pl.pallas_call(
        body, name="fox_attn_fwd", grid=(HP // PP, nq),
        in_specs=[pl.BlockSpec((tq, wide), lambda h, i: (i, h)),
                  pl.BlockSpec((S, wide), lambda h, i: (0, wb // PP + h)),
                  pl.BlockSpec((S, wide), lambda h, i: (0, 2 * wb // PP + h)),
                  pl.BlockSpec((tq, wide), lambda h, i: (i, 3 * wb // PP + h)),
                  pl.BlockSpec((S, LANE), lambda h, i: (0, 0))] + [any_spec] * na,
        out_specs=[pl.BlockSpec((tq, wide), lambda h, i: (i, h)),
                   pl.BlockSpec((tq, wide), lambda h, i: (i, h)),
                   pl.BlockSpec((PP, tq, LANE), lambda h, i: (h, i, 0))] + [any_spec] * na,
        out_shape=[jax.ShapeDtypeStruct((S, W), BF16), jax.ShapeDtypeStruct((S, W), BF16),
                   jax.ShapeDtypeStruct((HP, S, LANE), F32)]
        + [jax.ShapeDtypeStruct((N_DEV,) + g.shape, g.dtype) for g in gather],
        scratch_shapes=[pltpu.VMEM((NE, S, LANE), BF16), pltpu.VMEM((NE, S, LANE), BF16),
                        pltpu.VMEM((NE, tq, LANE), BF16), pltpu.VMEM((2, NE, tq, tq), F32),
                        pltpu.VMEM((2, NE, tq, LANE), F32), pltpu.VMEM((NE, tq, LANE), F32),
                        pltpu.VMEM((NE, tq, LANE), F32)] + sems,
        compiler_params=_cparams(),
    )(qkvg, qkvg, qkvg, qkvg, c, *gather)
    return outs[0], outs[1], outs[2], list(outs[3:])


def _fox_out_bwd(dxb, wo, qkvg, o, a, H):
    S, D = dxb.shape
    W = H * HEAD_DIM
    tm, tn = _row_tile(S, EPI_TILE), _tile(W, EPI_TILE)
    npair = tn // LANE
    scale = HEAD_DIM ** -0.5

    def body(dx_ref, w_ref, q_ref, g_ref, o_ref, a_ref, qa_ref, da_ref, dg_ref):
        dy = _dot_nt(dx_ref[...], w_ref[...])
        lane = lax.broadcasted_iota(jnp.int32, (tm, LANE), 1)
        own = [lane < HEAD_DIM, lane >= HEAD_DIM]
        for p in range(npair):
            cols = slice(LANE * p, LANE * (p + 1))
            q = q_ref[:, cols] * jnp.asarray(scale, BF16)
            dyv, g, ov, at = dy[:, cols], g_ref[:, cols].astype(F32), o_ref[:, cols].astype(F32), a_ref[p]
            sg = _sigmoid(g)
            dob = (dyv * (g * sg)).astype(BF16)
            dg_ref[:, cols] = (dyv * ov * (sg * (1.0 + g * (1.0 - sg)))).astype(BF16)
            prod = dob.astype(F32) * ov
            for e in range(2):
                a_col = jnp.max(jnp.where(own[e], at, -jnp.inf), axis=1, keepdims=True)
                d_col = jnp.sum(jnp.where(own[e], prod, 0.0), axis=1, keepdims=True)
                qa_ref[e, :, cols] = jnp.where(own[e], q, _bias_lanes(a_col, lane, e, True))
                da_ref[e, :, cols] = jnp.where(own[e], dob, _bias_lanes(-d_col, lane, e, True))

    blk = pl.BlockSpec((tm, tn), lambda i, j: (i, j))
    pair = pl.BlockSpec((2, tm, tn), lambda i, j: (0, i, j))
    return pl.pallas_call(
        body, name="fox_out_bwd", grid=(S // tm, W // tn),
        in_specs=[pl.BlockSpec((tm, D), lambda i, j: (i, 0)), pl.BlockSpec((tn, D), lambda i, j: (j, 0)),
                  blk, pl.BlockSpec((tm, tn), lambda i, j: (i, 3 * W // tn + j)), blk,
                  pl.BlockSpec((npair, tm, LANE), lambda i, j: (j, i, 0))],
        out_specs=[pair, pair, pl.BlockSpec((None, tm, tn), lambda i, j: (3, i, j))],
        out_shape=[jax.ShapeDtypeStruct((2, S, W), BF16), jax.ShapeDtypeStruct((2, S, W), BF16),
                   jax.ShapeDtypeStruct((4, S, W), BF16)],
        compiler_params=_cparams(),
    )(dxb, wo, qkvg, qkvg, o, a)


def _fox_bwd(qaug, doaug, qkv, c, dqkvg, H, scatter=(), scatter_specs=()):
    na = len(scatter)
    S = qkv.shape[0]
    W = H * HEAD_DIM
    HP = H // 2
    tq = _row_tile(S, ATT_TILE)
    nq = S // tq
    wb = W // LANE
    scale = HEAD_DIM ** -0.5

    def body(*refs):
        qa_ref, da_ref, k_ref, v_ref, c_ref = refs[:5]
        out_ref, dcr_ref, dcc_ref = refs[6 + na:9 + na]
        dq_sc, dk_sc, dv_sc = refs[9 + 2 * na:12 + 2 * na]
        hp, kj = pl.program_id(0), pl.program_id(1)
        if na:
            remote = _direct_scatter_copies(refs[5:5 + na], refs[9 + na:9 + 2 * na], scatter_specs,
                                            *refs[12 + 2 * na:])

            @pl.when((hp == 0) & (kj == 0))
            def _():
                for cp in remote:
                    cp.start()
        lane = lax.broadcasted_iota(jnp.int32, (tq, LANE), 1)
        own = [lane < HEAD_DIM, lane >= HEAD_DIM]
        rows = lax.broadcasted_iota(jnp.int32, (tq, tq), 0)
        cols = lax.broadcasted_iota(jnp.int32, (tq, tq), 1)

        @pl.when(kj == 0)
        def _():
            dq_sc[...] = jnp.zeros(dq_sc.shape, F32)

        @pl.when((kj == 0) & (hp == 0))
        def _():
            dcr_ref[...] = jnp.zeros(dcr_ref.shape, F32)
            dcc_ref[...] = jnp.zeros(dcc_ref.shape, F32)

        kblk, vblk, cblk = k_ref[...], v_ref[...], c_ref[...]
        one, zero = jnp.ones((tq, LANE), BF16), jnp.zeros((tq, LANE), BF16)
        ka, va = [], []
        for e in range(2):
            o0 = HEAD_DIM * (1 - e)
            c_col = jnp.sum(jnp.where(lane == 2 * hp + e, cblk, 0.0), axis=1, keepdims=True)
            ka.append(jnp.where(own[e], kblk, _bias_lanes(-c_col, lane, e, False)))
            va.append(jnp.where(own[e], vblk, jnp.where((lane >= o0) & (lane < o0 + 3), one, zero)))
        dk_sc[...] = jnp.zeros(dk_sc.shape, F32)
        dv_sc[...] = jnp.zeros(dv_sc.shape, F32)

        def step(i, masked):
            r0 = pl.multiple_of(i * tq, tq)
            for e in range(2):
                qa = qa_ref[e, pl.ds(r0, tq), :]
                da = da_ref[e, pl.ds(r0, tq), :]
                p = jnp.exp(_dot_nt(qa, ka[e]))
                if masked:
                    p = jnp.where(rows >= cols, p, 0.0)
                ds = p * _dot_nt(da, va[e])
                pb, dsb = p.astype(BF16), ds.astype(BF16)
                dv_sc[e] += _dot_tn(pb, da)
                dk_sc[e] += _dot_tn(dsb, qa)
                dq_sc[e, pl.ds(r0, tq), :] += _dot_nn(dsb, ka[e])

        step(kj, True)

        def loop_body(i, carry):
            step(i, False)
            return carry

        lax.fori_loop(kj + 1, nq, loop_body, 0)
        k0 = pl.multiple_of(kj * tq, tq)
        out_ref[1, pl.ds(k0, tq), :] = jnp.where(own[0], dk_sc[0], dk_sc[1]).astype(BF16)
        out_ref[2, pl.ds(k0, tq), :] = jnp.where(own[0], dv_sc[0], dv_sc[1]).astype(BF16)

        def put_lane(ref, r0, e, tile, src_lane):
            col = jnp.sum(jnp.where(lane == src_lane, tile, 0.0), axis=1, keepdims=True)
            ref[pl.ds(r0, tq), :] = jnp.where(lane == 2 * hp + e, col, ref[pl.ds(r0, tq), :])

        for e in range(2):
            put_lane(dcc_ref, k0, e, dk_sc[e], HEAD_DIM * (1 - e) + 3)

        @pl.when(kj == nq - 1)
        def _():
            def chunk(i, carry):
                r0 = pl.multiple_of(i * tq, tq)
                d0, d1 = dq_sc[0, pl.ds(r0, tq), :], dq_sc[1, pl.ds(r0, tq), :]
                out_ref[0, pl.ds(r0, tq), :] = (jnp.where(own[0], d0, d1) * scale).astype(BF16)
                put_lane(dcr_ref, r0, 0, d0, HEAD_DIM)
                put_lane(dcr_ref, r0, 1, d1, 0)
                return carry
            lax.fori_loop(0, nq, chunk, 0)

        if na:
            @pl.when((hp == HP - 1) & (kj == nq - 1))
            def _():
                _wait_all(remote)

    pair = pl.BlockSpec((2, S, LANE), lambda h, j: (0, 0, h))
    vec = pl.BlockSpec((S, LANE), lambda h, j: (0, 0))
    any_spec = pl.BlockSpec(memory_space=pl.ANY)
    sems = [pltpu.SemaphoreType.DMA((na, N_DEV - 1)), pltpu.SemaphoreType.DMA((na, N_DEV - 1))] if na else []
    outs = pl.pallas_call(
        body, name="fox_attn_bwd", grid=(HP, nq),
        in_specs=[pair, pair,
                  pl.BlockSpec((tq, LANE), lambda h, j: (j, wb + h)),
                  pl.BlockSpec((tq, LANE), lambda h, j: (j, 2 * wb + h)),
                  pl.BlockSpec((tq, LANE), lambda h, j: (j, 0))] + [any_spec] * (na + 1),
        out_specs=[pl.BlockSpec((3, S, LANE), lambda h, j: (0, 0, h)), vec, vec] + [any_spec] * na,
        out_shape=[jax.ShapeDtypeStruct(dqkvg.shape, BF16), jax.ShapeDtypeStruct((S, LANE), F32),
                   jax.ShapeDtypeStruct((S, LANE), F32)]
        + [jax.ShapeDtypeStruct((N_DEV - 1,) + _scatter_block_shape(g, s), g.dtype)
           for g, s in zip(scatter, scatter_specs)],
        scratch_shapes=[pltpu.VMEM((2, S, LANE), F32), pltpu.VMEM((2, tq, LANE), F32),
                        pltpu.VMEM((2, tq, LANE), F32)] + sems,
        input_output_aliases={5 + na: 0},
        compiler_params=_cparams(),
    )(qaug, doaug, qkv, qkv, c, *scatter, dqkvg)
    return outs[0], outs[1], outs[2], list(outs[3:])


def _swa_pick(blk, half, lane):
    b = blk.astype(F32)
    r = pltpu.roll(b, HEAD_DIM, axis=1)
    return jnp.where(jnp.logical_xor(lane < HEAD_DIM, half == 1), b, r).astype(BF16)


def _swa_stack(t, lane, G):
    pieces = []
    z = jnp.zeros((SWA_BLOCK, LANE), t.dtype)
    for j in range(G // 2):
        tile = t[:, LANE * j:LANE * (j + 1)]
        pieces += [jnp.where(lane < HEAD_DIM, tile, z), jnp.where(lane < HEAD_DIM, z, tile)]
    return jnp.concatenate(pieces, axis=0)


def _swa_unstack(st, lane, G):
    tiles = []
    for j in range(G // 2):
        a = st[2 * j * SWA_BLOCK:(2 * j + 1) * SWA_BLOCK]
        b = st[(2 * j + 1) * SWA_BLOCK:(2 * j + 2) * SWA_BLOCK]
        tiles.append(jnp.where(lane < HEAD_DIM, a, b))
    return jnp.concatenate(tiles, axis=1)


def _swa_mask_bias(G):
    R = G * SWA_BLOCK
    t_loc = jnp.arange(R)[:, None] % SWA_BLOCK
    j_loc = jnp.arange(2 * SWA_BLOCK)[None, :]
    diff = t_loc + SWA_BLOCK - j_loc
    band = (diff >= 0) & (diff < SWA_BLOCK)
    return jnp.stack([jnp.where(band & (j_loc >= SWA_BLOCK), 0.0, NEG_INF),
                      jnp.where(band, 0.0, NEG_INF)]).astype(F32)


def _swa_scores(q, kp, kc, vp, vc, srow, bias, half, head0, G):
    lane = lax.broadcasted_iota(jnp.int32, (SWA_BLOCK, LANE), 1)
    kk = jnp.concatenate([_swa_pick(kp, half, lane), _swa_pick(kc, half, lane)], axis=0)
    vv = jnp.concatenate([_swa_pick(vp, half, lane), _swa_pick(vc, half, lane)], axis=0)
    qstack = _swa_stack(q, lane, G) * jnp.asarray(HEAD_DIM ** -0.5, BF16)
    s = _dot_nt(qstack, kk) + bias
    R = G * SWA_BLOCK
    lane1 = lax.broadcasted_iota(jnp.int32, (1, LANE), 1)
    sink = jnp.concatenate(
        [jnp.broadcast_to(jnp.sum(jnp.where(lane1 == head0 + g, srow, 0.0), axis=1, keepdims=True), (SWA_BLOCK, LANE))
         for g in range(G)], axis=0)
    m = jnp.maximum(jnp.broadcast_to(jnp.max(s, axis=1, keepdims=True), (R, LANE)), sink)
    e = jnp.exp(s - jnp.tile(m, (1, 2)))
    es = jnp.exp(sink - m)
    inv = 1.0 / (jnp.broadcast_to(jnp.sum(e, axis=1, keepdims=True), (R, LANE)) + es)
    return qstack, kk, vv, e * jnp.tile(inv, (1, 2)), es * inv, lane


def _swa_fwd(q, kv, gate, sinks, mask_bias, HQ, HKV):
    S = q.shape[0]
    G = HQ // HKV
    WQ, KVW = HQ * HEAD_DIM, HKV * HEAD_DIM
    nb = S // SWA_BLOCK
    GW = G * HEAD_DIM
    kb, vb = 0, KVW // LANE
    NH = min(HKV, 4)
    NP = NH // 2

    def body(q_ref, kp_ref, kc_ref, vp_ref, vc_ref, g_ref, sink_ref, b_ref, y_ref, o_ref):
        grp = pl.program_id(0)
        for hh in range(NH):
            cols, kt = slice(GW * hh, GW * (hh + 1)), slice(LANE * (hh // 2), LANE * (hh // 2 + 1))
            _, _, vv, p, _, lane = _swa_scores(q_ref[:, cols], kp_ref[:, kt], kc_ref[:, kt], vp_ref[:, kt], vc_ref[:, kt],
                                               sink_ref[...], b_ref[0], hh % 2, (NH * grp + hh) * G, G)
            o = _swa_unstack(_dot_nn(p.astype(BF16), vv), lane, G)
            g = g_ref[:, cols].astype(F32)
            y_ref[:, cols] = (o * (g * _sigmoid(g))).astype(BF16)
            o_ref[:, cols] = o.astype(BF16)

    blk = lambda cb, prev: pl.BlockSpec(
        (SWA_BLOCK, NP * LANE), lambda h, n, cb=cb, prev=prev: (jnp.maximum(n - prev, 0), cb // NP + h))
    qspec = pl.BlockSpec((SWA_BLOCK, NH * GW), lambda h, n: (n, h))
    return pl.pallas_call(
        body, name="swa_attn_fwd", grid=(HKV // NH, nb),
        in_specs=[qspec, blk(kb, 1), blk(kb, 0), blk(vb, 1), blk(vb, 0), qspec,
                  pl.BlockSpec((1, LANE), lambda h, n: (0, 0)),
                  pl.BlockSpec((1, G * SWA_BLOCK, 2 * SWA_BLOCK), lambda h, n: (jnp.minimum(n, 1), 0, 0))],
        out_specs=[qspec, qspec],
        out_shape=[jax.ShapeDtypeStruct((S, WQ), BF16), jax.ShapeDtypeStruct((S, WQ), BF16)],
        compiler_params=_cparams(),
    )(q, kv, kv, kv, kv, gate, sinks, mask_bias)


def _swa_bwd(q, kv, dy, gate, o, sinks, tables, mask_bias, HQ, HKV):
    S = q.shape[0]
    G = HQ // HKV
    WQ, KVW = HQ * HEAD_DIM, HKV * HEAD_DIM
    nb = S // SWA_BLOCK
    GW = G * HEAD_DIM
    R = G * SWA_BLOCK
    kb, vb = 0, KVW // LANE
    scale = HEAD_DIM ** -0.5
    NH = min(HKV, 4)
    NP = NH // 2
    assert G == 8

    def body(q_ref, kp_ref, kc_ref, vp_ref, vc_ref, dy_ref, g_ref, o_ref, sink_ref, t_ref, b_ref,
             dqg_ref, dkv_ref, dsink_ref, carry_sc):
        grp, n = pl.program_id(0), pl.program_id(1)

        @pl.when(n == 0)
        def _():
            carry_sc[...] = jnp.zeros(carry_sc.shape, F32)
            dsink_ref[...] = jnp.zeros(dsink_ref.shape, F32)

        @pl.when(n < nb)
        def _():
            t0, t1, t2 = (jnp.tile(t_ref[i], (1, GW // LANE)) for i in range(3))
            for hh in range(NH):
                cols, kt = slice(GW * hh, GW * (hh + 1)), slice(LANE * (hh // 2), LANE * (hh // 2 + 1))
                qstack, kk, vv, p, psink, lane = _swa_scores(
                    q_ref[:, cols], kp_ref[:, kt], kc_ref[:, kt], vp_ref[:, kt], vc_ref[:, kt], sink_ref[...], b_ref[0],
                    hh % 2, (NH * grp + hh) * G, G)
                dyv, g, ov = dy_ref[:, cols], g_ref[:, cols].astype(F32), o_ref[:, cols].astype(F32)
                sg = _sigmoid(g)
                dob = (dyv * (g * sg)).astype(BF16)
                dqg_ref[1, :, cols] = (dyv * ov * (sg * (1.0 + g * (1.0 - sg)))).astype(BF16)
                prod = dob.astype(F32) * ov
                dparts = []
                for j in range(G // 2):
                    tile = prod[:, LANE * j:LANE * (j + 1)]
                    for sel in (jnp.where(lane < HEAD_DIM, tile, 0.0), jnp.where(lane < HEAD_DIM, 0.0, tile)):
                        dparts.append(jnp.broadcast_to(jnp.sum(sel, axis=1, keepdims=True), (SWA_BLOCK, LANE)))
                delta = jnp.concatenate(dparts, axis=0)
                dostack = _swa_stack(dob, lane, G)
                ds = p * (_dot_nt(dostack, vv) - jnp.tile(delta, (1, 2)))
                dsb, pb = ds.astype(BF16), p.astype(BF16)
                dq = _swa_unstack(_dot_nn(dsb, kk), lane, G) * scale
                dq = dq * t0 + pltpu.roll(dq * t1, ROT_DIM // 2, axis=1) + pltpu.roll(dq * t2, GW - ROT_DIM // 2, axis=1)
                dqg_ref[0, :, cols] = dq.astype(BF16)
                dkk = _dot_tn(dsb, qstack)
                dvv = _dot_tn(pb, dostack)
                dkk = dkk + pltpu.roll(dkk, HEAD_DIM, axis=1)
                dvv = dvv + pltpu.roll(dvv, HEAD_DIM, axis=1)
                lane2 = lax.broadcasted_iota(jnp.int32, (2 * SWA_BLOCK, LANE), 1)
                comb = jnp.where(lane2 < HEAD_DIM, dkk, dvv)
                dkv_ref[hh] = carry_sc[hh] + comb[:SWA_BLOCK]
                carry_sc[hh] = comb[SWA_BLOCK:]
                sk = psink * delta
                rows = [-jnp.sum(sk[g_ * SWA_BLOCK:(g_ + 1) * SWA_BLOCK], axis=0, keepdims=True) for g_ in range(G)]
                dsink_ref[hh] += jnp.concatenate(rows, axis=0)

        @pl.when(n == nb)
        def _():
            dkv_ref[...] = carry_sc[...]

    cl = lambda n: jnp.minimum(n, nb - 1)
    blk = lambda cb, prev: pl.BlockSpec(
        (SWA_BLOCK, NP * LANE), lambda h, n, cb=cb, prev=prev: (jnp.maximum(cl(n) - prev, 0), cb // NP + h))
    qspec = pl.BlockSpec((SWA_BLOCK, NH * GW), lambda h, n: (cl(n), h))
    return pl.pallas_call(
        body, name="swa_attn_bwd", grid=(HKV // NH, nb + 1),
        in_specs=[qspec, blk(kb, 1), blk(kb, 0), blk(vb, 1), blk(vb, 0), qspec, qspec, qspec,
                  pl.BlockSpec((1, LANE), lambda h, n: (0, 0)),
                  pl.BlockSpec((3, SWA_BLOCK, LANE), lambda h, n: (0, cl(n), 0)),
                  pl.BlockSpec((1, R, 2 * SWA_BLOCK), lambda h, n: (jnp.minimum(n, 1), 0, 0))],
        out_specs=[pl.BlockSpec((2, SWA_BLOCK, NH * GW), lambda h, n: (0, cl(n), h)),
                   pl.BlockSpec((NH, SWA_BLOCK, LANE), lambda h, n: (h, jnp.maximum(n - 1, 0), 0)),
                   pl.BlockSpec((NH, 8, LANE), lambda h, n: (h, 0, 0))],
        out_shape=[jax.ShapeDtypeStruct((2, S, WQ), BF16), jax.ShapeDtypeStruct((HKV, S, LANE), F32),
                   jax.ShapeDtypeStruct((HKV, 8, LANE), F32)],
        scratch_shapes=[pltpu.VMEM((NH, SWA_BLOCK, LANE), F32)],
        compiler_params=_cparams(),
    )(q, kv, kv, kv, kv, dy, gate, o, sinks, tables, mask_bias)


def _swa_dkv_finish(dkv, tables):
    HKV, S, _ = dkv.shape
    KVW = HKV * HEAD_DIM
    tm = _row_tile(S, EPI_TILE)
    npair = HKV // 2

    def body(d_ref, t_ref, o_ref):
        lane = lax.broadcasted_iota(jnp.int32, (tm, LANE), 1)
        lo = lane < HEAD_DIM
        for p in range(npair):
            a, b = d_ref[2 * p], d_ref[2 * p + 1]
            tk = jnp.where(lo, a, pltpu.roll(b, HEAD_DIM, axis=1))
            tv = jnp.where(lo, pltpu.roll(a, HEAD_DIM, axis=1), b)
            tk = (tk * t_ref[0] + pltpu.roll(tk * t_ref[1], ROT_DIM // 2, axis=1)
                  + pltpu.roll(tk * t_ref[2], LANE - ROT_DIM // 2, axis=1))
            o_ref[:, LANE * p:LANE * (p + 1)] = tk.astype(BF16)
            o_ref[:, KVW + LANE * p:KVW + LANE * (p + 1)] = tv.astype(BF16)

    return pl.pallas_call(
        body, name="swa_dkv_finish", grid=(S // tm,),
        in_specs=[pl.BlockSpec((HKV, tm, LANE), lambda i: (0, i, 0)), pl.BlockSpec((3, tm, LANE), lambda i: (0, i, 0))],
        out_specs=pl.BlockSpec((tm, 2 * KVW), lambda i: (i, 0)),
        out_shape=jax.ShapeDtypeStruct((S, 2 * KVW), BF16),
        compiler_params=_cparams(),
    )(dkv, tables)


def _rope_tables(S, width):
    half = ROT_DIM // 2
    pos = jnp.arange(S, dtype=F32)
    inv_freq = ROPE_THETA ** (-jnp.arange(half, dtype=F32) / half)
    ang = pos[:, None] * inv_freq[None, :]
    cos, sin = jnp.cos(ang), jnp.sin(ang)
    one = jnp.ones((S, HEAD_DIM - ROT_DIM), F32)
    zero = jnp.zeros((S, HEAD_DIM - ROT_DIM), F32)
    zh = jnp.zeros((S, half), F32)
    t0 = jnp.concatenate([cos, cos, one], axis=1)
    t1 = jnp.concatenate([-sin, zh, zero], axis=1)
    t2 = jnp.concatenate([zh, sin, zero], axis=1)
    return jnp.stack([jnp.tile(t, (1, width // HEAD_DIM)) for t in (t0, t1, t2)])


def _pad_rows(v, row, total_rows=8):
    return jnp.pad(v, ((row, total_rows - row - v.shape[0]), (0, 0)))


def _pad_lanes(v, off, width):
    return jnp.pad(v, ((0, 0), (off, width - off - v.shape[1])))


def kernel(x, norm_g, fox_w_in, fox_b_f, fox_w_out, swa_w_in, swa_sinks, swa_w_out, final_g, loss_target, m_norm_g, m_fox_w_in, m_fox_b_f, m_fox_w_out, m_swa_w_in, m_swa_sinks, m_swa_w_out, m_final_g, v_norm_g, v_fox_w_in, v_fox_b_f, v_fox_w_out, v_swa_w_in, v_swa_sinks, v_swa_w_out, v_final_g):
    S, D = x.shape[1], x.shape[2]
    H = fox_b_f.shape[1]
    W = H * HEAD_DIM
    wf = fox_w_in.shape[2]
    ws = swa_w_in.shape[2]
    HQ = swa_sinks.shape[1]
    WQ = HQ * HEAD_DIM
    KVW = (ws * N_DEV - 2 * WQ) // 2
    HKV = KVW // HEAD_DIM
    rows_o = fox_w_out.shape[1]
    assert wf * N_DEV == 4 * W + H and rows_o * N_DEV == W and H <= LANE and HQ <= LANE
    me = _my_index()

    _, sw_f, np_f = _slab_geom(wf)
    _, sw_s, np_s = _slab_geom(ws)

    def slab(w2d, w, sw):
        return jnp.pad(w2d.astype(BF16), ((0, 0), (0, sw - w)))

    (fi_all,) = _all_gather([slab(fox_w_in[0], wf, sw_f)])
    w_fi = _assemble(fi_all, wf)
    later = [slab(swa_w_in[0], ws, sw_s), fox_w_out[0].astype(BF16), swa_w_out[0].astype(BF16)]

    x0 = x[0]
    g0, g1, gf = norm_g[0:1], norm_g[1:2], final_g[None, :]
    bias = _pad_lanes(fox_b_f, 0, LANE)
    sinks = _pad_lanes(swa_sinks, 0, LANE)
    tab_k = _rope_tables(S, LANE)
    mask_bias = _swa_mask_bias(HQ // HKV)

    h0 = _rmsnorm_fwd(x0, g0, "rmsnorm0")
    qkv0 = _proj(h0, w_fi, 0, 4 * W, BF16, "fox_in_qkvg")
    fl = _proj(h0, w_fi, 4 * W, LANE, F32, "fox_in_f")
    c = _fox_gate_fwd(fl, bias)
    y0, o0, a0, (si_all, fo_all, so_all) = _fox_fwd(qkv0, c, H, gather=later)
    w_si = _assemble(si_all, ws)
    w_fo = fo_all.reshape(W, D)
    w_so = so_all.reshape(WQ, D)
    x1, h1 = _out_proj_norm(y0, w_fo, x0, g1, "fox_out")

    q1 = _proj(h1, w_si, 0, WQ, BF16, "swa_in_q", rope=(tab_k, WQ))
    kv1 = _proj(h1, w_si, WQ, 2 * KVW, BF16, "swa_in_kv", rope=(tab_k, KVW))
    gate1 = _proj(h1, w_si, WQ + 2 * KVW, WQ, BF16, "swa_in_gate")
    y1, o1 = _swa_fwd(q1, kv1, gate1, sinks, mask_bias, HQ, HKV)
    dx2, dx2b, dgf, loss_p = _out_proj_loss(y1, w_so, x1, loss_target[0], gf, "swa_out_loss")

    dy1 = _matmul_nt([(dx2b, None, 0)], w_so, WQ, "swa_out_bwd")
    g_so, g_so_h = _matmul_tn(y1, [(dx2b, None, 0)], D, "swa_out_wgrad", also_bf16=True)
    dqg1, dkv1, dsink = _swa_bwd(q1, kv1, dy1, gate1, o1, sinks, tab_k, mask_bias, HQ, HKV)
    dkv1f = _swa_dkv_finish(dkv1, tab_k)
    parts1 = [(dqg1, 0, 0), (dkv1f, None, WQ), (dqg1, 1, WQ + 2 * KVW)]
    g_si, g_si_h = _matmul_tn(h1, parts1, np_s, "swa_in_wgrad", tile_major=True, also_bf16=True)
    dh1 = _matmul_nt(parts1, w_si, D, "swa_in_bwd")
    dx1, dx1b, dg1 = _rmsnorm_bwd(dh1, x1, g1, dx2, "rmsnorm1_bwd")

    qaug0, doaug0, dqkvg0 = _fox_out_bwd(dx1b, w_fo, qkv0, o0, a0, H)
    g_fo, g_fo_h = _matmul_tn(y0, [(dx1b, None, 0)], D, "fox_out_wgrad", also_bf16=True)
    early_specs = [("col", ws), ("row", rows_o), ("row", rows_o)]
    dqkvg0, dcr, dcc, early_recv = _fox_bwd(qaug0, doaug0, qkv0, c, dqkvg0, H, scatter=[g_si_h, g_fo_h, g_so_h],
                                           scatter_specs=early_specs)
    dfl, dbf = _fox_gate_bwd(fl, bias, dcr - dcc)
    parts0 = [(dqkvg0, "stack", 0), (dfl, None, 4 * W)]
    g_fi, g_fi_h = _matmul_tn(h0, parts0, np_f, "fox_in_wgrad", tile_major=True, also_bf16=True)
    spec_fi = ("col", wf)
    fi_sems, fi_src, fi_land, token = _scatter_start(g_fi_h, spec_fi)
    parts0[-1] = (dfl + token[0, 0].astype(BF16), None, 4 * W)
    dh0 = _matmul_nt(parts0, w_fi, D, "fox_in_bwd")
    dx0, _, dg0 = _rmsnorm_bwd(dh0, x0, g0, dx1, "rmsnorm0_bwd")

    red_si, gw_fo, gw_so = [_final_sum8(g_, r_, s_)
                            for g_, r_, s_ in zip([g_si, g_fo, g_so], early_recv, early_specs)]
    gw_si = lax.dynamic_slice(red_si, (0, (ws * me) % LANE), (D, ws))

    P = D
    dsink_v = dsink[:, :, 0].reshape(1, HQ)
    row3 = _pad_lanes(dbf[:, :H], 0, P) + _pad_lanes(dsink_v, LANE, P) + _pad_lanes(loss_p[:, :1], 2 * LANE, P)
    pack = _pad_rows(dg0, 0) + _pad_rows(dg1, 1) + _pad_rows(dgf, 2) + _pad_rows(row3, 3)

    d_fo, m_fo, v_fo = _adamw(fox_w_out[0], gw_fo, m_fox_w_out[0], v_fox_w_out[0], "adamw_fox_out")
    d_si, m_si, v_si = _adamw(swa_w_in[0], gw_si, m_swa_w_in[0], v_swa_w_in[0], "adamw_swa_in")
    d_so, m_so, v_so = _adamw(swa_w_out[0], gw_so, m_swa_w_out[0], v_swa_w_out[0], "adamw_swa_out")
    recv_fi = _scatter_wait(fi_sems, fi_src, fi_land, spec_fi, after=[dx0, pack, d_fo, d_si, d_so])
    red_fi = _final_sum8(g_fi, recv_fi, spec_fi)
    gw_fi = lax.dynamic_slice(red_fi, (0, (wf * me) % LANE), (D, wf))
    d_fi, m_fi, v_fi = _adamw(fox_w_in[0], gw_fi, m_fox_w_in[0], v_fox_w_in[0], "adamw_fox_in")

    tot = _all_reduce_small(pack, after=recv_fi)
    loss = tot[3, 2 * LANE]
    g_norm = tot[0:2]
    g_final = tot[2]
    g_bf = tot[3:4, 0:H]
    g_sinks = tot[3:4, LANE:LANE + HQ]

    def small_pack(ng, fg, bf, sk):
        r3 = _pad_lanes(bf, 0, P) + _pad_lanes(sk, LANE, P)
        return _pad_rows(ng, 0) + _pad_rows(fg[None, :], 2) + _pad_rows(r3, 3)

    sd, sm, sv = _adamw(small_pack(norm_g, final_g, fox_b_f, swa_sinks), tot,
                        small_pack(m_norm_g, m_final_g, m_fox_b_f, m_swa_sinks),
                        small_pack(v_norm_g, v_final_g, v_fox_b_f, v_swa_sinks), "adamw_small")

    def unpack(t):
        return t[0:2], t[3:4, 0:H], t[3:4, LANE:LANE + HQ], t[2]

    def group(small, fi, fo, si, so):
        ng, bf, sk, fg = unpack(small)
        return (ng, fi[None], bf, fo[None], si[None], sk, so[None], fg)

    grads = (g_norm, gw_fi[None], g_bf, gw_fo[None], gw_si[None], g_sinks, gw_so[None], g_final)
    return (loss, dx0[None], *grads, *group(sd, d_fi, d_fo, d_si, d_so),
            *group(sm, m_fi, m_fo, m_si, m_so), *group(sv, v_fi, v_fo, v_si, v_so))
```

```python
import math

import jax
import jax.numpy as jnp
from jax import lax
from jax.experimental import pallas as pl
from jax.experimental.pallas import tpu as pltpu

F32 = jnp.float32
BF16 = jnp.bfloat16
MESH = pl.DeviceIdType.MESH

N_DEV = 8
LANE = 128
HEAD_DIM = 64
SWA_BLOCK = 128
NEG_INF = -1e30
RMS_EPS = 1e-6
ROPE_THETA = 500000.0
ROT_DIM = HEAD_DIM // 4
ADAM_LR, ADAM_B1, ADAM_B2, ADAM_EPS, ADAM_WD, ADAM_STEP = 0.001, 0.9, 0.999, 1e-08, 0.01, 10
VMEM_LIMIT = 56 * 1024 * 1024
MM_TILE = 1024
ATT_TILE = 512
EPI_TILE = 512
ROW_TILE = 256
ADAM_TILE_ELEMS = 3 << 18


def _cparams(**kw):
    return pltpu.CompilerParams(vmem_limit_bytes=VMEM_LIMIT, **kw)


def _tile(n, cap):
    if n <= cap:
        return n
    t = (cap // LANE) * LANE
    while t > LANE and n % t:
        t -= LANE
    assert n % t == 0, (n, cap)
    return t


def _row_tile(n, cap):
    t = min(n, cap)
    while n % t:
        t //= 2
    return t


def _dot_nn(a, b):
    return jnp.dot(a, b, preferred_element_type=F32)


def _dot_nt(a, b):
    return lax.dot_general(a, b, (((1,), (1,)), ((), ())), preferred_element_type=F32)


def _dot_tn(a, b):
    return lax.dot_general(a, b, (((0,), (0,)), ((), ())), preferred_element_type=F32)


def _split3(x):
    hi = x.astype(BF16)
    r1 = x - hi.astype(F32)
    mid = r1.astype(BF16)
    return hi, mid, (r1 - mid.astype(F32)).astype(BF16)


def _sigmoid(g):
    return 1.0 / (1.0 + jnp.exp(-g))


def _slab_geom(w):
    starts = [w * i for i in range(N_DEV)]
    aligned = [LANE * (s // LANE) for s in starts]
    offs = [s - a for s, a in zip(starts, aligned)]
    sw = LANE * (-(-(max(offs) + w) // LANE))
    return aligned, sw, aligned[-1] + sw


def _my_index():
    return 4 * lax.axis_index("x") + 2 * lax.axis_index("y") + lax.axis_index("c")


def _all_gather(arrs):
    n = len(arrs)

    def body(*refs):
        ins, outs = refs[:n], refs[n:2 * n]
        send_sems, recv_sems, local_sems = refs[2 * n:]
        x, y, c = lax.axis_index("x"), lax.axis_index("y"), lax.axis_index("c")
        me, sib = (x, y, c), (x, y, 1 - c)
        chips = [(1 - x, y), (x, 1 - y), (1 - x, 1 - y)]

        def idx(px, py, pc):
            return 4 * px + 2 * py + pc

        def copy(a, k, block, to, src=None):
            dst = outs[a].at[idx(*block)]
            return pltpu.make_async_remote_copy(
                src_ref=dst if src is None else src, dst_ref=dst,
                send_sem=send_sems.at[a, k], recv_sem=recv_sems.at[a, k],
                device_id=to, device_id_type=MESH)

        mine = [pltpu.make_async_copy(ins[a], outs[a].at[idx(*me)], local_sems.at[a]) for a in range(n)]
        for m in mine:
            m.start()
        first = []
        for a in range(n):
            first.append(copy(a, 0, me, sib, src=ins[a]))
            for j, chip in enumerate(chips):
                first.append(copy(a, 1 + j, me, (*chip, c), src=ins[a]))
        for cp in first:
            cp.start()
        passed = []
        for j, chip in enumerate(chips):
            for a in range(n):
                copy(a, 1 + j, (*chip, c), me).wait_recv()
                p = copy(a, 4 + j, (*chip, c), sib)
                p.start()
                passed.append(p)
        for a in range(n):
            copy(a, 0, sib, me).wait_recv()
        for j, chip in enumerate(chips):
            for a in range(n):
                copy(a, 4 + j, (*chip, 1 - c), me).wait_recv()
        for cp in first + passed:
            cp.wait_send()
        for m in mine:
            m.wait()

    any_spec = pl.BlockSpec(memory_space=pl.ANY)
    return pl.pallas_call(
        body, name="weights_all_gather",
        out_shape=[jax.ShapeDtypeStruct((N_DEV,) + a.shape, a.dtype) for a in arrs],
        in_specs=[any_spec] * n, out_specs=[any_spec] * n,
        scratch_shapes=[pltpu.SemaphoreType.DMA((n, 7)), pltpu.SemaphoreType.DMA((n, 7)),
                        pltpu.SemaphoreType.DMA((n,))],
    )(*arrs)


def _rs_windows(specs):
    def window(ref, spec, blk):
        kind, n = spec
        if kind == "col":
            _, sw, _ = _slab_geom(n)
            return ref.at[pl.ds((n * blk) // LANE, sw // LANE)]
        start = pl.multiple_of(n * blk, n)
        return ref.at[pl.ds(start, n), :]
    return window


def _peer(k):
    x, y, c = lax.axis_index("x"), lax.axis_index("y"), lax.axis_index("c")
    return (x ^ (k >> 2), y ^ ((k >> 1) & 1), c ^ (k & 1))


def _direct_gather_copies(ins, outs, send_sems, recv_sems, local_sems):
    me = _my_index()
    remote, local = [], []
    for a, (src, dst) in enumerate(zip(ins, outs)):
        local.append(pltpu.make_async_copy(src, dst.at[me], local_sems.at[a]))
        for k in range(1, N_DEV):
            remote.append(pltpu.make_async_remote_copy(
                src_ref=src, dst_ref=dst.at[me], send_sem=send_sems.at[a, k - 1], recv_sem=recv_sems.at[a, k - 1],
                device_id=_peer(k), device_id_type=MESH))
    return remote, local


def _direct_scatter_copies(ins, outs, specs, send_sems, recv_sems):
    window = _rs_windows(specs)
    remote = []
    for a, (src, dst) in enumerate(zip(ins, outs)):
        for k in range(1, N_DEV):
            px, py, pc = _peer(k)
            remote.append(pltpu.make_async_remote_copy(
                src_ref=window(src, specs[a], 4 * px + 2 * py + pc), dst_ref=dst.at[k - 1],
                send_sem=send_sems.at[a, k - 1], recv_sem=recv_sems.at[a, k - 1],
                device_id=(px, py, pc), device_id_type=MESH))
    return remote


def _scatter_block_shape(g, spec):
    kind, w = spec
    return (_slab_geom(w)[1] // LANE, g.shape[1], LANE) if kind == "col" else (w, g.shape[1])


def _wait_all(remote, local=()):
    for cp in remote:
        cp.wait_recv()
    for cp in remote:
        cp.wait_send()
    for cp in local:
        cp.wait()


def _scatter_start(g, spec):
    blk = _scatter_block_shape(g, spec)
    window = _rs_windows([spec])
    npeer = N_DEV - 1

    def body(g_ref, land_ref, *rest):
        sems = rest[:2 * npeer]
        token = rest[2 * npeer + 2]
        for cp in _peer_block_copies(g_ref, land_ref, spec, window, sems[:npeer], sems[npeer:]):
            cp.start()
        token[...] = jnp.zeros(token.shape, token.dtype)

    hbm = pl.BlockSpec(memory_space=pltpu.HBM)
    sem = pl.BlockSpec(memory_space=pltpu.SEMAPHORE)
    land = lax.empty((npeer,) + blk, g.dtype)
    outs = pl.pallas_call(
        body, name="grads_scatter_start",
        out_shape=(pltpu.SemaphoreType.DMA(()),) * (2 * npeer)
        + (pltpu.HBM(g.shape, g.dtype), pltpu.HBM(land.shape, land.dtype), jax.ShapeDtypeStruct((8, LANE), F32)),
        in_specs=(hbm, hbm),
        out_specs=(sem,) * (2 * npeer) + (hbm, hbm, pl.BlockSpec(memory_space=pltpu.VMEM)),
        input_output_aliases={0: 2 * npeer, 1: 2 * npeer + 1},
        compiler_params=pltpu.CompilerParams(has_side_effects=pltpu.SideEffectType.DATAFLOW_SIDE_EFFECTING),
    )(pltpu.with_memory_space_constraint(g, pltpu.HBM), pltpu.with_memory_space_constraint(land, pltpu.HBM))
    return outs[:2 * npeer], outs[2 * npeer], outs[2 * npeer + 1], outs[2 * npeer + 2]


def _peer_block_copies(g_ref, land_ref, spec, window, send_sems, recv_sems):
    copies = []
    for k in range(1, N_DEV):
        px, py, pc = _peer(k)
        copies.append(pltpu.make_async_remote_copy(
            src_ref=window(g_ref, spec, 4 * px + 2 * py + pc), dst_ref=land_ref.at[k - 1],
            send_sem=send_sems[k - 1], recv_sem=recv_sems[k - 1], device_id=(px, py, pc), device_id_type=MESH))
    return copies


def _scatter_wait(sems, g_thru, land_thru, spec, after):
    window = _rs_windows([spec])
    npeer = N_DEV - 1

    def body(g_ref, land_ref, *rest):
        s = rest[:2 * npeer]
        copies = _peer_block_copies(g_ref, land_ref, spec, window, s[:npeer], s[npeer:])
        for cp in copies:
            cp.wait_send()
        for cp in copies:
            cp.wait_recv()

    hbm = pl.BlockSpec(memory_space=pltpu.HBM)
    sem = pl.BlockSpec(memory_space=pltpu.SEMAPHORE)
    return pl.pallas_call(
        body, name="grads_scatter_wait",
        out_shape=(pltpu.HBM(g_thru.shape, g_thru.dtype), pltpu.HBM(land_thru.shape, land_thru.dtype)),
        in_specs=(hbm, hbm) + (sem,) * (2 * npeer) + (pl.BlockSpec(memory_space=pl.ANY),) * len(after),
        out_specs=(hbm, hbm), input_output_aliases={0: 0, 1: 1},
        compiler_params=pltpu.CompilerParams(has_side_effects=pltpu.SideEffectType.DATAFLOW_SIDE_EFFECTING),
    )(g_thru, land_thru, *sems, *after)[1]


def _final_sum8(g, recv, spec):
    kind, n = spec
    me = _my_index()
    offs = jnp.stack([(n * me) // LANE if kind == "col" else me]).astype(jnp.int32)
    if kind == "col":
        _, T, M, _ = recv.shape
        grid = (T,)
        in_specs = [pl.BlockSpec((1, M, LANE), lambda t, o: (o[0] + t, 0, 0)),
                    pl.BlockSpec((N_DEV - 1, 1, M, LANE), lambda t, o: (0, t, 0, 0))]
        out_spec = pl.BlockSpec((LANE, M), lambda t, o: (t, 0))
        out_shape = jax.ShapeDtypeStruct((T * LANE, M), F32)
    else:
        _, nrow, C = recv.shape
        grid = (1,)
        in_specs = [pl.BlockSpec((nrow, C), lambda t, o: (o[0], 0)),
                    pl.BlockSpec((N_DEV - 1, nrow, C), lambda t, o: (0, 0, 0))]
        out_spec = pl.BlockSpec((nrow, C), lambda t, o: (0, 0))
        out_shape = jax.ShapeDtypeStruct((nrow, C), F32)

    def body(o_ref, g_ref, r_ref, out_ref):
        acc = g_ref[0] if kind == "col" else g_ref[...]
        for k in range(N_DEV - 1):
            acc = acc + (r_ref[k, 0] if kind == "col" else r_ref[k]).astype(F32)
        out_ref[...] = acc.T if kind == "col" else acc

    return pl.pallas_call(
        body, name="grads_final_sum8",
        grid_spec=pltpu.PrefetchScalarGridSpec(num_scalar_prefetch=1, grid=grid, in_specs=in_specs,
                                               out_specs=out_spec),
        out_shape=out_shape, compiler_params=_cparams(),
    )(offs, g, recv)


def _all_reduce_small(pack, after):
    R, P = pack.shape

    def body(x_ref, after_ref, o_ref, gat_ref, send_sems, recv_sems):
        x, y, c = lax.axis_index("x"), lax.axis_index("y"), lax.axis_index("c")
        me = 4 * x + 2 * y + c
        gat_ref[me] = x_ref[...]
        copies = []
        for k in range(1, N_DEV):
            peer = (x ^ (k >> 2), y ^ ((k >> 1) & 1), c ^ (k & 1))
            copies.append(pltpu.make_async_remote_copy(
                src_ref=x_ref, dst_ref=gat_ref.at[me],
                send_sem=send_sems.at[k - 1], recv_sem=recv_sems.at[k - 1],
                device_id=peer, device_id_type=MESH))
        for cp in copies:
            cp.start()
        for cp in copies:
            cp.wait_recv()
        for cp in copies:
            cp.wait_send()
        acc = gat_ref[0]
        for d in range(1, N_DEV):
            acc = acc + gat_ref[d]
        o_ref[...] = acc

    vm = pl.BlockSpec(memory_space=pltpu.VMEM)
    return pl.pallas_call(
        body, name="small_all_reduce",
        out_shape=jax.ShapeDtypeStruct((R, P), F32),
        in_specs=[vm, pl.BlockSpec(memory_space=pl.ANY)], out_specs=vm,
        scratch_shapes=[pltpu.VMEM((N_DEV, R, P), F32),
                        pltpu.SemaphoreType.DMA((N_DEV - 1,)), pltpu.SemaphoreType.DMA((N_DEV - 1,))],
    )(pack, after)


def _assemble(slabs, w):
    aligned, sw, total = _slab_geom(w)
    K = slabs.shape[1]
    tr = _row_tile(K, ROW_TILE)

    def body(s_ref, o_ref):
        o_ref[...] = jnp.zeros(o_ref.shape, BF16)
        for i in range(N_DEV):
            a, off = aligned[i], w * i - aligned[i]
            x = s_ref[i].astype(F32)
            if off:
                x = pltpu.roll(x, off, axis=1)
            o_ref[:, a:a + sw] = (o_ref[:, a:a + sw].astype(F32) + x).astype(BF16)

    return pl.pallas_call(
        body, name="assemble_w_in", grid=(K // tr,),
        in_specs=[pl.BlockSpec((N_DEV, tr, sw), lambda i: (0, i, 0))],
        out_specs=pl.BlockSpec((tr, total), lambda i: (i, 0)),
        out_shape=jax.ShapeDtypeStruct((K, total), BF16),
        compiler_params=_cparams(),
    )(slabs)


def _rmsnorm_fwd(x, g, name):
    S, D = x.shape
    tm = _row_tile(S, ROW_TILE)

    def body(x_ref, g_ref, h_ref):
        xv = x_ref[...]
        r = lax.rsqrt(jnp.mean(xv * xv, axis=-1, keepdims=True) + RMS_EPS)
        h_ref[...] = ((xv * r) * g_ref[...]).astype(BF16)

    return pl.pallas_call(
        body, name=name, grid=(S // tm,),
        in_specs=[pl.BlockSpec((tm, D), lambda i: (i, 0)), pl.BlockSpec((1, D), lambda i: (0, 0))],
        out_specs=pl.BlockSpec((tm, D), lambda i: (i, 0)),
        out_shape=jax.ShapeDtypeStruct((S, D), BF16),
        compiler_params=_cparams(),
    )(x, g)


def _rmsnorm_bwd(dh, x, g, dres, name):
    S, D = x.shape
    tm = _row_tile(S, ROW_TILE)

    def body(dh_ref, x_ref, g_ref, dr_ref, dx_ref, dxb_ref, dg_ref):
        xv = x_ref[...]
        r = lax.rsqrt(jnp.mean(xv * xv, axis=-1, keepdims=True) + RMS_EPS)
        xhat = xv * r
        d = dh_ref[...]
        gd = d * g_ref[...]
        dx = r * (gd - xhat * jnp.mean(gd * xhat, axis=-1, keepdims=True)) + dr_ref[...]
        dx_ref[...] = dx
        dxb_ref[...] = dx.astype(BF16)

        @pl.when(pl.program_id(0) == 0)
        def _():
            dg_ref[...] = jnp.zeros(dg_ref.shape, F32)
        dg_ref[...] += jnp.sum(d * xhat, axis=0, keepdims=True)

    row = pl.BlockSpec((tm, D), lambda i: (i, 0))
    vec = pl.BlockSpec((1, D), lambda i: (0, 0))
    return pl.pallas_call(
        body, name=name, grid=(S // tm,),
        in_specs=[row, row, vec, row], out_specs=[row, row, vec],
        out_shape=[jax.ShapeDtypeStruct((S, D), F32), jax.ShapeDtypeStruct((S, D), BF16),
                   jax.ShapeDtypeStruct((1, D), F32)],
        compiler_params=_cparams(),
    )(dh, x, g, dres)


def _adamw(w, g, m, v, name):
    R, C = w.shape
    steps = pl.cdiv(R * C, ADAM_TILE_ELEMS)
    tr = R if steps == 1 else pl.cdiv(pl.cdiv(R, steps), 8) * 8
    c1 = 1.0 - ADAM_B1 ** ADAM_STEP
    c2 = 1.0 - ADAM_B2 ** ADAM_STEP

    def body(w_ref, g_ref, m_ref, v_ref, d_ref, nm_ref, nv_ref):
        gv = g_ref[...]
        nm = ADAM_B1 * m_ref[...] + (1.0 - ADAM_B1) * gv
        nv = ADAM_B2 * v_ref[...] + (1.0 - ADAM_B2) * (gv * gv)
        d_ref[...] = -ADAM_LR * ((nm / c1) / (jnp.sqrt(nv / c2) + ADAM_EPS) + ADAM_WD * w_ref[...])
        nm_ref[...] = nm
        nv_ref[...] = nv

    spec = pl.BlockSpec((tr, C), lambda i: (i, 0))
    return pl.pallas_call(
        body, name=name, grid=(pl.cdiv(R, tr),),
        in_specs=[spec] * 4, out_specs=[spec] * 3,
        out_shape=[jax.ShapeDtypeStruct((R, C), F32)] * 3,
        compiler_params=_cparams(),
    )(w, g, m, v)


def _proj(h, wfull, col0, ncols, out_dtype, name, rope=None):
    S, K = h.shape
    tm = _row_tile(S, MM_TILE)
    tn = math.gcd(_tile(ncols, MM_TILE), col0) if col0 else _tile(ncols, MM_TILE)
    if rope is not None:
        tn = _tile(math.gcd(ncols, rope[1]), MM_TILE)
    assert ncols % tn == 0 and col0 % tn == 0
    cb = col0 // tn

    def body(*refs):
        if rope is None:
            a_ref, b_ref, o_ref = refs
        else:
            a_ref, b_ref, t_ref, o_ref = refs
        acc = _dot_nn(a_ref[...], b_ref[...])
        if rope is not None:
            t0, t1, t2 = (jnp.tile(t_ref[i], (1, tn // LANE)) for i in range(3))
            roped = (acc * t0 + pltpu.roll(acc, tn - ROT_DIM // 2, axis=1) * t1
                     + pltpu.roll(acc, ROT_DIM // 2, axis=1) * t2)
            acc = jnp.where(pl.program_id(1) < rope[1] // tn, roped, acc)
        o_ref[...] = acc.astype(out_dtype)

    in_specs = [pl.BlockSpec((tm, K), lambda i, j: (i, 0)), pl.BlockSpec((K, tn), lambda i, j: (0, cb + j))]
    args = [h, wfull]
    if rope is not None:
        in_specs.append(pl.BlockSpec((3, tm, LANE), lambda i, j: (0, i, 0)))
        args.append(rope[0])
    return pl.pallas_call(
        body, name=name, grid=(S // tm, ncols // tn),
        in_specs=in_specs, out_specs=pl.BlockSpec((tm, tn), lambda i, j: (i, j)),
        out_shape=jax.ShapeDtypeStruct((S, ncols), out_dtype),
        compiler_params=_cparams(),
    )(*args)


def _out_proj_norm(y, wo, xres, g, name):
    S, W = y.shape
    D = wo.shape[1]
    tm = _row_tile(S, EPI_TILE)

    def body(a_ref, b_ref, r_ref, g_ref, x_ref, h_ref):
        xv = r_ref[...] + _dot_nn(a_ref[...], b_ref[...])
        x_ref[...] = xv
        r = lax.rsqrt(jnp.mean(xv * xv, axis=-1, keepdims=True) + RMS_EPS)
        h_ref[...] = ((xv * r) * g_ref[...]).astype(BF16)

    row = pl.BlockSpec((tm, D), lambda i: (i, 0))
    return pl.pallas_call(
        body, name=name, grid=(S // tm,),
        in_specs=[pl.BlockSpec((tm, W), lambda i: (i, 0)), pl.BlockSpec((W, D), lambda i: (0, 0)), row,
                  pl.BlockSpec((1, D), lambda i: (0, 0))],
        out_specs=[row, row],
        out_shape=[jax.ShapeDtypeStruct((S, D), F32), jax.ShapeDtypeStruct((S, D), BF16)],
        compiler_params=_cparams(),
    )(y, wo, xres, g)


def _out_proj_loss(y, wo, xres, tgt, g, name):
    S, W = y.shape
    D = wo.shape[1]
    tm = _row_tile(S, EPI_TILE)

    def body(a_ref, b_ref, r_ref, t_ref, g_ref, dx_ref, dxb_ref, dg_ref, loss_ref):
        xv = r_ref[...] + _dot_nn(a_ref[...], b_ref[...])
        r = lax.rsqrt(jnp.mean(xv * xv, axis=-1, keepdims=True) + RMS_EPS)
        xhat = xv * r
        gv = g_ref[...]
        err = xhat * gv - t_ref[...]
        d = err * (1.0 / D)
        gd = d * gv
        dx = r * (gd - xhat * jnp.mean(gd * xhat, axis=-1, keepdims=True))
        dx_ref[...] = dx
        dxb_ref[...] = dx.astype(BF16)

        @pl.when(pl.program_id(0) == 0)
        def _():
            dg_ref[...] = jnp.zeros(dg_ref.shape, F32)
            loss_ref[...] = jnp.zeros(loss_ref.shape, F32)
        dg_ref[...] += jnp.sum(d * xhat, axis=0, keepdims=True)
        per_tok = jnp.sum(err * err, axis=-1, keepdims=True) * (1.0 / D)
        loss_ref[...] += 0.5 * jnp.sum(per_tok, axis=0, keepdims=True)

    row = pl.BlockSpec((tm, D), lambda i: (i, 0))
    vec = pl.BlockSpec((1, D), lambda i: (0, 0))
    return pl.pallas_call(
        body, name=name, grid=(S // tm,),
        in_specs=[pl.BlockSpec((tm, W), lambda i: (i, 0)), pl.BlockSpec((W, D), lambda i: (0, 0)), row, row, vec],
        out_specs=[row, row, vec, pl.BlockSpec((1, LANE), lambda i: (0, 0))],
        out_shape=[jax.ShapeDtypeStruct((S, D), F32), jax.ShapeDtypeStruct((S, D), BF16),
                   jax.ShapeDtypeStruct((1, D), F32), jax.ShapeDtypeStruct((1, LANE), F32)],
        compiler_params=_cparams(),
    )(y, wo, xres, tgt, g)


def _matmul_nt(parts, wfull, out_rows, name):
    S = parts[0][0].shape[-2]
    tm, tn = _row_tile(S, 2 * MM_TILE if len(parts) <= 2 else MM_TILE), _tile(out_rows, MM_TILE)
    plan, lo = [], 0
    for arr, lead, col0 in parts:
        n_p = arr.shape[-1]
        tk = math.gcd(_tile(n_p, MM_TILE), col0) if col0 else _tile(n_p, MM_TILE)
        steps = n_p // tk * (arr.shape[0] if lead == "stack" else 1)
        plan.append((lead, col0 // tk, tk, lo, lo + steps))
        lo += steps
    nk = lo
    npart = len(parts)

    def body(*refs):
        a_refs, w_refs = refs[:npart], refs[npart:2 * npart]
        o_ref, acc_ref = refs[2 * npart], refs[2 * npart + 1]
        k = pl.program_id(2)

        @pl.when(k == 0)
        def _():
            acc_ref[...] = jnp.zeros(acc_ref.shape, F32)
        for p, (_, _, _, lo_p, hi_p) in enumerate(plan):
            @pl.when((k >= lo_p) & (k < hi_p))
            def _(p=p):
                acc_ref[...] += _dot_nt(a_refs[p][...], w_refs[p][...])

        @pl.when(k == nk - 1)
        def _():
            o_ref[...] = acc_ref[...]

    in_specs, args = [], []
    for (arr, lead, col0), (_, cb, tk, lo_p, hi_p) in zip(parts, plan):
        def kk(k, lo_p=lo_p, hi_p=hi_p):
            return jnp.clip(k - lo_p, 0, hi_p - lo_p - 1)
        if lead is None:
            in_specs.append(pl.BlockSpec((tm, tk), lambda i, j, k, kk=kk: (i, kk(k))))
        elif lead == "stack":
            nkb = arr.shape[-1] // tk
            in_specs.append(pl.BlockSpec((None, tm, tk), lambda i, j, k, kk=kk, nkb=nkb: (kk(k) // nkb, i, kk(k) % nkb)))
        else:
            in_specs.append(pl.BlockSpec((None, tm, tk), lambda i, j, k, kk=kk, lead=lead: (lead, i, kk(k))))
        args.append(arr)
    for (_, cb, tk, lo_p, hi_p) in plan:
        def kk(k, lo_p=lo_p, hi_p=hi_p):
            return jnp.clip(k - lo_p, 0, hi_p - lo_p - 1)
        in_specs.append(pl.BlockSpec((tn, tk), lambda i, j, k, kk=kk, cb=cb: (j, cb + kk(k))))
        args.append(wfull)
    return pl.pallas_call(
        body, name=name, grid=(S // tm, out_rows // tn, nk),
        in_specs=in_specs, out_specs=pl.BlockSpec((tm, tn), lambda i, j, k: (i, j)),
        out_shape=jax.ShapeDtypeStruct((S, out_rows), F32),
        scratch_shapes=[pltpu.VMEM((tm, tn), F32)],
        compiler_params=_cparams(),
    )(*args)


def _matmul_tn(a, parts, total, name, tile_major=False, also_bf16=False):
    S, M = a.shape
    tm, ts = _tile(M, MM_TILE), _row_tile(S, 2 * MM_TILE)
    nout = 2 if also_bf16 else 1
    outs = None
    for idx, (arr, lead, col0) in enumerate(parts):
        n_p = arr.shape[-1]
        tn = math.gcd(_tile(n_p, MM_TILE), col0) if col0 else _tile(n_p, MM_TILE)
        cb = col0 // tn
        nk = S // ts
        nb = n_p // tn
        if lead == "stack":
            n_p *= arr.shape[0]

        def body(*refs, nk=nk, tn=tn):
            a_ref, b_ref = refs[0], refs[1]
            o_refs, acc_ref = refs[-1 - nout:-1], refs[-1]
            k = pl.program_id(2)

            @pl.when(k == 0)
            def _():
                acc_ref[...] = jnp.zeros(acc_ref.shape, F32)
            acc_ref[...] += _dot_tn(a_ref[...], b_ref[...])

            @pl.when(k == nk - 1)
            def _():
                for o_ref in o_refs:
                    if tile_major:
                        for t in range(tn // LANE):
                            o_ref[t] = acc_ref[:, LANE * t:LANE * (t + 1)].astype(o_ref.dtype)
                    else:
                        o_ref[...] = acc_ref[...].astype(o_ref.dtype)

        in_specs = [pl.BlockSpec((ts, tm), lambda i, j, k: (k, i))]
        if lead is None:
            in_specs.append(pl.BlockSpec((ts, tn), lambda i, j, k: (k, j)))
        elif lead == "stack":
            in_specs.append(pl.BlockSpec((None, ts, tn), lambda i, j, k, nb=nb: (j // nb, k, j % nb)))
        else:
            in_specs.append(pl.BlockSpec((None, ts, tn), lambda i, j, k, lead=lead: (lead, k, j)))
        args = [a, arr]
        aliases = {}
        if outs is not None:
            in_specs += [pl.BlockSpec(memory_space=pl.ANY)] * nout
            args += list(outs)
            aliases = {2 + o: o for o in range(nout)}
        if tile_major:
            out_spec = pl.BlockSpec((tn // LANE, tm, LANE), lambda i, j, k, cb=cb: (cb + j, i, 0))
            shape = (total // LANE, M, LANE)
        else:
            out_spec = pl.BlockSpec((tm, tn), lambda i, j, k, cb=cb: (i, cb + j))
            shape = (M, total)
        outs = pl.pallas_call(
            body, name=f"{name}_{idx}", grid=(M // tm, n_p // tn, nk),
            in_specs=in_specs, out_specs=[out_spec] * nout,
            out_shape=[jax.ShapeDtypeStruct(shape, dt) for dt in (F32, BF16)[:nout]],
            scratch_shapes=[pltpu.VMEM((tm, tn), F32)],
            input_output_aliases=aliases,
            compiler_params=_cparams(),
        )(*args)
    return tuple(outs) if also_bf16 else outs[0]


def _log_sigmoid(z):
    e = jnp.exp(-jnp.abs(z))
    return jnp.minimum(z, 0.0) - jnp.where(e < 1e-4, e * (1.0 - 0.5 * e), jnp.log(1.0 + e))


def _tri_sum(tri, x):
    hi, mid, lo = _split3(x)
    return _dot_nn(tri, hi) + _dot_nn(tri, mid) + _dot_nn(tri, lo)


def _fox_gate_fwd(fl, bias):
    S = fl.shape[0]

    nb_ = _row_tile(S, ROW_TILE)

    def body(f_ref, b_ref, c_ref):
        ri = lax.broadcasted_iota(jnp.int32, (nb_, nb_), 0)
        ci = lax.broadcasted_iota(jnp.int32, (nb_, nb_), 1)
        tri = jnp.where(ri >= ci, 1.0, 0.0).astype(BF16)
        row = lax.broadcasted_iota(jnp.int32, (nb_, LANE), 0)

        def step(i, carry):
            r0 = pl.multiple_of(i * nb_, nb_)
            t = _tri_sum(tri, _log_sigmoid(f_ref[pl.ds(r0, nb_), :] + b_ref[...])) + carry
            c_ref[pl.ds(r0, nb_), :] = t
            return jnp.sum(jnp.where(row == nb_ - 1, t, 0.0), axis=0, keepdims=True)

        lax.fori_loop(0, S // nb_, step, jnp.zeros((1, LANE), F32))

    vm = pl.BlockSpec(memory_space=pltpu.VMEM)
    return pl.pallas_call(
        body, name="fox_gate_fwd", in_specs=[vm, vm], out_specs=vm,
        out_shape=jax.ShapeDtypeStruct((S, LANE), F32),
        compiler_params=_cparams(),
    )(fl, bias)


def _fox_gate_bwd(fl, bias, dc):
    S = fl.shape[0]

    nb_ = _row_tile(S, ROW_TILE)

    def body(f_ref, b_ref, d_ref, o_ref, db_ref):
        ri = lax.broadcasted_iota(jnp.int32, (nb_, nb_), 0)
        ci = lax.broadcasted_iota(jnp.int32, (nb_, nb_), 1)
        tri = jnp.where(ri <= ci, 1.0, 0.0).astype(BF16)
        row = lax.broadcasted_iota(jnp.int32, (nb_, LANE), 0)
        nt = S // nb_

        def step(ii, carry):
            carry_c, carry_b = carry
            r0 = pl.multiple_of((nt - 1 - ii) * nb_, nb_)
            t = _tri_sum(tri, d_ref[pl.ds(r0, nb_), :]) + carry_c
            dz = t * _sigmoid(-(f_ref[pl.ds(r0, nb_), :] + b_ref[...]))
            o_ref[pl.ds(r0, nb_), :] = dz.astype(BF16)
            first = jnp.sum(jnp.where(row == 0, t, 0.0), axis=0, keepdims=True)
            return first, carry_b + jnp.sum(dz, axis=0, keepdims=True)

        zero = jnp.zeros((1, LANE), F32)
        _, db = lax.fori_loop(0, nt, step, (zero, zero))
        db_ref[...] = db

    vm = pl.BlockSpec(memory_space=pltpu.VMEM)
    return pl.pallas_call(
        body, name="fox_gate_bwd", in_specs=[vm, vm, vm], out_specs=[vm, vm],
        out_shape=[jax.ShapeDtypeStruct((S, LANE), BF16), jax.ShapeDtypeStruct((1, LANE), F32)],
        compiler_params=_cparams(),
    )(fl, bias, dc)


def _bias_lanes(col, lane, e, first):
    o0 = HEAD_DIM * (1 - e)
    hi, mid, lo = _split3(col)
    d0 = o0 if first else o0 + 3
    t = jnp.where((lane >= o0) & (lane < o0 + 6), jnp.ones(lane.shape, BF16), jnp.zeros(lane.shape, BF16))
    t = jnp.where(lane == d0, hi, t)
    t = jnp.where(lane == d0 + 1, mid, t)
    return jnp.where(lane == d0 + 2, lo, t)


def _fox_fwd(qkvg, c, H, gather=()):
    na = len(gather)
    S = qkvg.shape[0]
    W = H * HEAD_DIM
    HP = H // 2
    PP = 2 if HP % 2 == 0 else 1
    NE = 2 * PP
    tq = _row_tile(S, ATT_TILE)
    nq = S // tq
    wb = W // LANE
    scale = HEAD_DIM ** -0.5

    def body(*refs):
        q_ref, k_ref, v_ref, g_ref, c_ref = refs[:5]
        y_ref, o_ref, a_ref = refs[5 + na:8 + na]
        kaug_sc, vaug_sc, qaug_sc, s_sc, mb_sc, m_sc, acc_sc = refs[8 + 2 * na:15 + 2 * na]
        hp, qi = pl.program_id(0), pl.program_id(1)
        if na:
            remote, local = _direct_gather_copies(refs[5:5 + na], refs[8 + na:8 + 2 * na], *refs[15 + 2 * na:])

            @pl.when((hp == 0) & (qi == 0))
            def _():
                for cp in remote + local:
                    cp.start()
        lane = lax.broadcasted_iota(jnp.int32, (tq, LANE), 1)
        own = [lane < HEAD_DIM, lane >= HEAD_DIM]
        rows = lax.broadcasted_iota(jnp.int32, (tq, tq), 0)
        cols = lax.broadcasted_iota(jnp.int32, (tq, tq), 1)

        def bias_lanes(col, e, first):
            return _bias_lanes(col, lane, e % 2, first)

        def head_col(tile, e):
            return jnp.sum(jnp.where(lane == 2 * PP * hp + e, tile, 0.0), axis=1, keepdims=True)

        def tile_of(e):
            return slice(LANE * (e // 2), LANE * (e // 2 + 1))

        @pl.when(qi == 0)
        def _():
            def chunk(i, carry):
                r0 = pl.multiple_of(i * tq, tq)
                cb = c_ref[pl.ds(r0, tq), :]
                for e in range(NE):
                    kb, vb = k_ref[pl.ds(r0, tq), tile_of(e)], v_ref[pl.ds(r0, tq), tile_of(e)]
                    kaug_sc[e, pl.ds(r0, tq), :] = jnp.where(own[e % 2], kb, bias_lanes(-head_col(cb, e), e, False))
                    vaug_sc[e, pl.ds(r0, tq), :] = jnp.where(own[e % 2], vb, jnp.ones((tq, LANE), BF16))
                return carry
            lax.fori_loop(0, nq, chunk, 0)

        crow = c_ref[pl.ds(pl.multiple_of(qi * tq, tq), tq), :]
        ctq = [head_col(crow, e) for e in range(NE)]
        for e in range(NE):
            q = q_ref[:, tile_of(e)] * jnp.asarray(scale, BF16)
            qaug_sc[e] = jnp.where(own[e % 2], q, bias_lanes(ctq[e], e, True))
        m_sc[...] = jnp.full(m_sc.shape, NEG_INF, F32)
        acc_sc[...] = jnp.zeros(acc_sc.shape, F32)

        def scores(blk, slot, masked):
            k0 = pl.multiple_of(blk * tq, tq)
            for e in range(NE):
                s = _dot_nt(qaug_sc[e], kaug_sc[e, pl.ds(k0, tq), :])
                if masked:
                    s = jnp.where(rows >= cols, s, NEG_INF)
                s_sc[slot, e] = s
                mb_sc[slot, e] = jnp.broadcast_to(jnp.max(s, axis=1, keepdims=True), (tq, LANE))

        def accumulate(blk, slot):
            k0 = pl.multiple_of(blk * tq, tq)
            for e in range(NE):
                m_prev = m_sc[e]
                m_new = jnp.maximum(m_prev, mb_sc[slot, e])
                p = jnp.exp(s_sc[slot, e] - jnp.tile(m_new, (1, tq // LANE)))
                acc_sc[e] = jnp.exp(m_prev - m_new) * acc_sc[e] + _dot_nn(p.astype(BF16), vaug_sc[e, pl.ds(k0, tq), :])
                m_sc[e] = m_new

        def block_of(t):
            return jnp.where(t == 0, qi, t - 1)

        scores(qi, 0, True)

        def loop_body(t, carry):
            scores(t, (t + 1) % 2, False)
            accumulate(block_of(t), t % 2)
            return carry

        lax.fori_loop(0, qi, loop_body, 0)
        accumulate(block_of(qi), qi % 2)
        o_e, a_e = [], []
        for e in range(NE):
            acc = acc_sc[e]
            l = pltpu.roll(acc, HEAD_DIM, axis=1)
            o_e.append(acc / l)
            a_e.append(ctq[e] - (m_sc[e] + jnp.log(l)))
        for pp in range(PP):
            o = jnp.where(own[0], o_e[2 * pp], o_e[2 * pp + 1])
            g = g_ref[:, tile_of(2 * pp)].astype(F32)
            y_ref[:, tile_of(2 * pp)] = (o * (g * _sigmoid(g))).astype(BF16)
            o_ref[:, tile_of(2 * pp)] = o.astype(BF16)
            a_ref[pp] = jnp.where(own[0], a_e[2 * pp], a_e[2 * pp + 1])
        if na:
            @pl.when((hp == HP // PP - 1) & (qi == nq - 1))
            def _():
                _wait_all(remote, local)

    any_spec = pl.BlockSpec(memory_space=pl.ANY)
    sems = [pltpu.SemaphoreType.DMA((na, N_DEV - 1)), pltpu.SemaphoreType.DMA((na, N_DEV - 1)),
            pltpu.SemaphoreType.DMA((na,))] if na else []
    wide = PP * LANE
    outs = pl.pallas_call(
        body, name="fox_attn_fwd", grid=(HP // PP, nq),
        in_specs=[pl.BlockSpec((tq, wide), lambda h, i: (i, h)),
                  pl.BlockSpec((S, wide), lambda h, i: (0, wb // PP + h)),
                  pl.BlockSpec((S, wide), lambda h, i: (0, 2 * wb // PP + h)),
                  pl.BlockSpec((tq, wide), lambda h, i: (i, 3 * wb // PP + h)),
                  pl.BlockSpec((S, LANE), lambda h, i: (0, 0))] + [any_spec] * na,
        out_specs=[pl.BlockSpec((tq, wide), lambda h, i: (i, h)),
                   pl.BlockSpec((tq, wide), lambda h, i: (i, h)),
                   pl.BlockSpec((PP, tq, LANE), lambda h, i: (h, i, 0))] + [any_spec] * na,
        out_shape=[jax.ShapeDtypeStruct((S, W), BF16), jax.ShapeDtypeStruct((S, W), BF16),
                   jax.ShapeDtypeStruct((HP, S, LANE), F32)]
        + [jax.ShapeDtypeStruct((N_DEV,) + g.shape, g.dtype) for g in gather],
        scratch_shapes=[pltpu.VMEM((NE, S, LANE), BF16), pltpu.VMEM((NE, S, LANE), BF16),
                        pltpu.VMEM((NE, tq, LANE), BF16), pltpu.VMEM((2, NE, tq, tq), F32),
                        pltpu.VMEM((2, NE, tq, LANE), F32), pltpu.VMEM((NE, tq, LANE), F32),
                        pltpu.VMEM((NE, tq, LANE), F32)] + sems,
        compiler_params=_cparams(),
    )(qkvg, qkvg, qkvg, qkvg, c, *gather)
    return outs[0], outs[1], outs[2], list(outs[3:])


def _fox_out_bwd(dxb, wo, qkvg, o, a, H):
    S, D = dxb.shape
    W = H * HEAD_DIM
    tm, tn = _row_tile(S, EPI_TILE), _tile(W, EPI_TILE)
    npair = tn // LANE
    scale = HEAD_DIM ** -0.5

    def body(dx_ref, w_ref, q_ref, g_ref, o_ref, a_ref, qa_ref, da_ref, dg_ref):
        dy = _dot_nt(dx_ref[...], w_ref[...])
        lane = lax.broadcasted_iota(jnp.int32, (tm, LANE), 1)
        own = [lane < HEAD_DIM, lane >= HEAD_DIM]
        for p in range(npair):
            cols = slice(LANE * p, LANE * (p + 1))
            q = q_ref[:, cols] * jnp.asarray(scale, BF16)
            dyv, g, ov, at = dy[:, cols], g_ref[:, cols].astype(F32), o_ref[:, cols].astype(F32), a_ref[p]
            sg = _sigmoid(g)
            dob = (dyv * (g * sg)).astype(BF16)
            dg_ref[:, cols] = (dyv * ov * (sg * (1.0 + g * (1.0 - sg)))).astype(BF16)
            prod = dob.astype(F32) * ov
            for e in range(2):
                a_col = jnp.max(jnp.where(own[e], at, -jnp.inf), axis=1, keepdims=True)
                d_col = jnp.sum(jnp.where(own[e], prod, 0.0), axis=1, keepdims=True)
                qa_ref[e, :, cols] = jnp.where(own[e], q, _bias_lanes(a_col, lane, e, True))
                da_ref[e, :, cols] = jnp.where(own[e], dob, _bias_lanes(-d_col, lane, e, True))

    blk = pl.BlockSpec((tm, tn), lambda i, j: (i, j))
    pair = pl.BlockSpec((2, tm, tn), lambda i, j: (0, i, j))
    return pl.pallas_call(
        body, name="fox_out_bwd", grid=(S // tm, W // tn),
        in_specs=[pl.BlockSpec((tm, D), lambda i, j: (i, 0)), pl.BlockSpec((tn, D), lambda i, j: (j, 0)),
                  blk, pl.BlockSpec((tm, tn), lambda i, j: (i, 3 * W // tn + j)), blk,
                  pl.BlockSpec((npair, tm, LANE), lambda i, j: (j, i, 0))],
        out_specs=[pair, pair, pl.BlockSpec((None, tm, tn), lambda i, j: (3, i, j))],
        out_shape=[jax.ShapeDtypeStruct((2, S, W), BF16), jax.ShapeDtypeStruct((2, S, W), BF16),
                   jax.ShapeDtypeStruct((4, S, W), BF16)],
        compiler_params=_cparams(),
    )(dxb, wo, qkvg, qkvg, o, a)


def _fox_bwd(qaug, doaug, qkv, c, dqkvg, H, scatter=(), scatter_specs=()):
    na = len(scatter)
    S = qkv.shape[0]
    W = H * HEAD_DIM
    HP = H // 2
    tq = _row_tile(S, ATT_TILE)
    nq = S // tq
    wb = W // LANE
    scale = HEAD_DIM ** -0.5

    def body(*refs):
        qa_ref, da_ref, k_ref, v_ref, c_ref = refs[:5]
        out_ref, dcr_ref, dcc_ref = refs[6 + na:9 + na]
        dq_sc, dk_sc, dv_sc = refs[9 + 2 * na:12 + 2 * na]
        hp, kj = pl.program_id(0), pl.program_id(1)
        if na:
            remote = _direct_scatter_copies(refs[5:5 + na], refs[9 + na:9 + 2 * na], scatter_specs,
                                            *refs[12 + 2 * na:])

            @pl.when((hp == 0) & (kj == 0))
            def _():
                for cp in remote:
                    cp.start()
        lane = lax.broadcasted_iota(jnp.int32, (tq, LANE), 1)
        own = [lane < HEAD_DIM, lane >= HEAD_DIM]
        rows = lax.broadcasted_iota(jnp.int32, (tq, tq), 0)
        cols = lax.broadcasted_iota(jnp.int32, (tq, tq), 1)

        @pl.when(kj == 0)
        def _():
            dq_sc[...] = jnp.zeros(dq_sc.shape, F32)

        @pl.when((kj == 0) & (hp == 0))
        def _():
            dcr_ref[...] = jnp.zeros(dcr_ref.shape, F32)
            dcc_ref[...] = jnp.zeros(dcc_ref.shape, F32)

        kblk, vblk, cblk = k_ref[...], v_ref[...], c_ref[...]
        one, zero = jnp.ones((tq, LANE), BF16), jnp.zeros((tq, LANE), BF16)
        ka, va = [], []
        for e in range(2):
            o0 = HEAD_DIM * (1 - e)
            c_col = jnp.sum(jnp.where(lane == 2 * hp + e, cblk, 0.0), axis=1, keepdims=True)
            ka.append(jnp.where(own[e], kblk, _bias_lanes(-c_col, lane, e, False)))
            va.append(jnp.where(own[e], vblk, jnp.where((lane >= o0) & (lane < o0 + 3), one, zero)))
        dk_sc[...] = jnp.zeros(dk_sc.shape, F32)
        dv_sc[...] = jnp.zeros(dv_sc.shape, F32)

        def step(i, masked):
            r0 = pl.multiple_of(i * tq, tq)
            for e in range(2):
                qa = qa_ref[e, pl.ds(r0, tq), :]
                da = da_ref[e, pl.ds(r0, tq), :]
                p = jnp.exp(_dot_nt(qa, ka[e]))
                if masked:
                    p = jnp.where(rows >= cols, p, 0.0)
                ds = p * _dot_nt(da, va[e])
                pb, dsb = p.astype(BF16), ds.astype(BF16)
                dv_sc[e] += _dot_tn(pb, da)
                dk_sc[e] += _dot_tn(dsb, qa)
                dq_sc[e, pl.ds(r0, tq), :] += _dot_nn(dsb, ka[e])

        step(kj, True)

        def loop_body(i, carry):
            step(i, False)
            return carry

        lax.fori_loop(kj + 1, nq, loop_body, 0)
        k0 = pl.multiple_of(kj * tq, tq)
        out_ref[1, pl.ds(k0, tq), :] = jnp.where(own[0], dk_sc[0], dk_sc[1]).astype(BF16)
        out_ref[2, pl.ds(k0, tq), :] = jnp.where(own[0], dv_sc[0], dv_sc[1]).astype(BF16)

        def put_lane(ref, r0, e, tile, src_lane):
            col = jnp.sum(jnp.where(lane == src_lane, tile, 0.0), axis=1, keepdims=True)
            ref[pl.ds(r0, tq), :] = jnp.where(lane == 2 * hp + e, col, ref[pl.ds(r0, tq), :])

        for e in range(2):
            put_lane(dcc_ref, k0, e, dk_sc[e], HEAD_DIM * (1 - e) + 3)

        @pl.when(kj == nq - 1)
        def _():
            def chunk(i, carry):
                r0 = pl.multiple_of(i * tq, tq)
                d0, d1 = dq_sc[0, pl.ds(r0, tq), :], dq_sc[1, pl.ds(r0, tq), :]
                out_ref[0, pl.ds(r0, tq), :] = (jnp.where(own[0], d0, d1) * scale).astype(BF16)
                put_lane(dcr_ref, r0, 0, d0, HEAD_DIM)
                put_lane(dcr_ref, r0, 1, d1, 0)
                return carry
            lax.fori_loop(0, nq, chunk, 0)

        if na:
            @pl.when((hp == HP - 1) & (kj == nq - 1))
            def _():
                _wait_all(remote)

    pair = pl.BlockSpec((2, S, LANE), lambda h, j: (0, 0, h))
    vec = pl.BlockSpec((S, LANE), lambda h, j: (0, 0))
    any_spec = pl.BlockSpec(memory_space=pl.ANY)
    sems = [pltpu.SemaphoreType.DMA((na, N_DEV - 1)), pltpu.SemaphoreType.DMA((na, N_DEV - 1))] if na else []
    outs = pl.pallas_call(
        body, name="fox_attn_bwd", grid=(HP, nq),
        in_specs=[pair, pair,
                  pl.BlockSpec((tq, LANE), lambda h, j: (j, wb + h)),
                  pl.BlockSpec((tq, LANE), lambda h, j: (j, 2 * wb + h)),
                  pl.BlockSpec((tq, LANE), lambda h, j: (j, 0))] + [any_spec] * (na + 1),
        out_specs=[pl.BlockSpec((3, S, LANE), lambda h, j: (0, 0, h)), vec, vec] + [any_spec] * na,
        out_shape=[jax.ShapeDtypeStruct(dqkvg.shape, BF16), jax.ShapeDtypeStruct((S, LANE), F32),
                   jax.ShapeDtypeStruct((S, LANE), F32)]
        + [jax.ShapeDtypeStruct((N_DEV - 1,) + _scatter_block_shape(g, s), g.dtype)
           for g, s in zip(scatter, scatter_specs)],
        scratch_shapes=[pltpu.VMEM((2, S, LANE), F32), pltpu.VMEM((2, tq, LANE), F32),
                        pltpu.VMEM((2, tq, LANE), F32)] + sems,
        input_output_aliases={5 + na: 0},
        compiler_params=_cparams(),
    )(qaug, doaug, qkv, qkv, c, *scatter, dqkvg)
    return outs[0], outs[1], outs[2], list(outs[3:])


def _swa_pick(blk, half, lane):
    b = blk.astype(F32)
    r = pltpu.roll(b, HEAD_DIM, axis=1)
    return jnp.where(jnp.logical_xor(lane < HEAD_DIM, half == 1), b, r).astype(BF16)


def _swa_stack(t, lane, G):
    pieces = []
    z = jnp.zeros((SWA_BLOCK, LANE), t.dtype)
    for j in range(G // 2):
        tile = t[:, LANE * j:LANE * (j + 1)]
        pieces += [jnp.where(lane < HEAD_DIM, tile, z), jnp.where(lane < HEAD_DIM, z, tile)]
    return jnp.concatenate(pieces, axis=0)


def _swa_unstack(st, lane, G):
    tiles = []
    for j in range(G // 2):
        a = st[2 * j * SWA_BLOCK:(2 * j + 1) * SWA_BLOCK]
        b = st[(2 * j + 1) * SWA_BLOCK:(2 * j + 2) * SWA_BLOCK]
        tiles.append(jnp.where(lane < HEAD_DIM, a, b))
    return jnp.concatenate(tiles, axis=1)


def _swa_mask_bias(G):
    R = G * SWA_BLOCK
    t_loc = jnp.arange(R)[:, None] % SWA_BLOCK
    j_loc = jnp.arange(2 * SWA_BLOCK)[None, :]
    diff = t_loc + SWA_BLOCK - j_loc
    band = (diff >= 0) & (diff < SWA_BLOCK)
    return jnp.stack([jnp.where(band & (j_loc >= SWA_BLOCK), 0.0, NEG_INF),
                      jnp.where(band, 0.0, NEG_INF)]).astype(F32)


def _swa_scores(q, kp, kc, vp, vc, srow, bias, half, head0, G):
    lane = lax.broadcasted_iota(jnp.int32, (SWA_BLOCK, LANE), 1)
    kk = jnp.concatenate([_swa_pick(kp, half, lane), _swa_pick(kc, half, lane)], axis=0)
    vv = jnp.concatenate([_swa_pick(vp, half, lane), _swa_pick(vc, half, lane)], axis=0)
    qstack = _swa_stack(q, lane, G) * jnp.asarray(HEAD_DIM ** -0.5, BF16)
    s = _dot_nt(qstack, kk) + bias
    R = G * SWA_BLOCK
    lane1 = lax.broadcasted_iota(jnp.int32, (1, LANE), 1)
    sink = jnp.concatenate(
        [jnp.broadcast_to(jnp.sum(jnp.where(lane1 == head0 + g, srow, 0.0), axis=1, keepdims=True), (SWA_BLOCK, LANE))
         for g in range(G)], axis=0)
    m = jnp.maximum(jnp.broadcast_to(jnp.max(s, axis=1, keepdims=True), (R, LANE)), sink)
    e = jnp.exp(s - jnp.tile(m, (1, 2)))
    es = jnp.exp(sink - m)
    inv = 1.0 / (jnp.broadcast_to(jnp.sum(e, axis=1, keepdims=True), (R, LANE)) + es)
    return qstack, kk, vv, e * jnp.tile(inv, (1, 2)), es * inv, lane


def _swa_fwd(q, kv, gate, sinks, mask_bias, HQ, HKV):
    S = q.shape[0]
    G = HQ // HKV
    WQ, KVW = HQ * HEAD_DIM, HKV * HEAD_DIM
    nb = S // SWA_BLOCK
    GW = G * HEAD_DIM
    kb, vb = 0, KVW // LANE
    NH = min(HKV, 4)
    NP = NH // 2

    def body(q_ref, kp_ref, kc_ref, vp_ref, vc_ref, g_ref, sink_ref, b_ref, y_ref, o_ref):
        grp = pl.program_id(0)
        for hh in range(NH):
            cols, kt = slice(GW * hh, GW * (hh + 1)), slice(LANE * (hh // 2), LANE * (hh // 2 + 1))
            _, _, vv, p, _, lane = _swa_scores(q_ref[:, cols], kp_ref[:, kt], kc_ref[:, kt], vp_ref[:, kt], vc_ref[:, kt],
                                               sink_ref[...], b_ref[0], hh % 2, (NH * grp + hh) * G, G)
            o = _swa_unstack(_dot_nn(p.astype(BF16), vv), lane, G)
            g = g_ref[:, cols].astype(F32)
            y_ref[:, cols] = (o * (g * _sigmoid(g))).astype(BF16)
            o_ref[:, cols] = o.astype(BF16)

    blk = lambda cb, prev: pl.BlockSpec(
        (SWA_BLOCK, NP * LANE), lambda h, n, cb=cb, prev=prev: (jnp.maximum(n - prev, 0), cb // NP + h))
    qspec = pl.BlockSpec((SWA_BLOCK, NH * GW), lambda h, n: (n, h))
    return pl.pallas_call(
        body, name="swa_attn_fwd", grid=(HKV // NH, nb),
        in_specs=[qspec, blk(kb, 1), blk(kb, 0), blk(vb, 1), blk(vb, 0), qspec,
                  pl.BlockSpec((1, LANE), lambda h, n: (0, 0)),
                  pl.BlockSpec((1, G * SWA_BLOCK, 2 * SWA_BLOCK), lambda h, n: (jnp.minimum(n, 1), 0, 0))],
        out_specs=[qspec, qspec],
        out_shape=[jax.ShapeDtypeStruct((S, WQ), BF16), jax.ShapeDtypeStruct((S, WQ), BF16)],
        compiler_params=_cparams(),
    )(q, kv, kv, kv, kv, gate, sinks, mask_bias)


def _swa_bwd(q, kv, dy, gate, o, sinks, tables, mask_bias, HQ, HKV):
    S = q.shape[0]
    G = HQ // HKV
    WQ, KVW = HQ * HEAD_DIM, HKV * HEAD_DIM
    nb = S // SWA_BLOCK
    GW = G * HEAD_DIM
    R = G * SWA_BLOCK
    kb, vb = 0, KVW // LANE
    scale = HEAD_DIM ** -0.5
    NH = min(HKV, 4)
    NP = NH // 2
    assert G == 8

    def body(q_ref, kp_ref, kc_ref, vp_ref, vc_ref, dy_ref, g_ref, o_ref, sink_ref, t_ref, b_ref,
             dqg_ref, dkv_ref, dsink_ref, carry_sc):
        grp, n = pl.program_id(0), pl.program_id(1)

        @pl.when(n == 0)
        def _():
            carry_sc[...] = jnp.zeros(carry_sc.shape, F32)
            dsink_ref[...] = jnp.zeros(dsink_ref.shape, F32)

        @pl.when(n < nb)
        def _():
            t0, t1, t2 = (jnp.tile(t_ref[i], (1, GW // LANE)) for i in range(3))
            for hh in range(NH):
                cols, kt = slice(GW * hh, GW * (hh + 1)), slice(LANE * (hh // 2), LANE * (hh // 2 + 1))
                qstack, kk, vv, p, psink, lane = _swa_scores(
                    q_ref[:, cols], kp_ref[:, kt], kc_ref[:, kt], vp_ref[:, kt], vc_ref[:, kt], sink_ref[...], b_ref[0],
                    hh % 2, (NH * grp + hh) * G, G)
                dyv, g, ov = dy_ref[:, cols], g_ref[:, cols].astype(F32), o_ref[:, cols].astype(F32)
                sg = _sigmoid(g)
                dob = (dyv * (g * sg)).astype(BF16)
                dqg_ref[1, :, cols] = (dyv * ov * (sg * (1.0 + g * (1.0 - sg)))).astype(BF16)
                prod = dob.astype(F32) * ov
                dparts = []
                for j in range(G // 2):
                    tile = prod[:, LANE * j:LANE * (j + 1)]
                    for sel in (jnp.where(lane < HEAD_DIM, tile, 0.0), jnp.where(lane < HEAD_DIM, 0.0, tile)):
                        dparts.append(jnp.broadcast_to(jnp.sum(sel, axis=1, keepdims=True), (SWA_BLOCK, LANE)))
                delta = jnp.concatenate(dparts, axis=0)
                dostack = _swa_stack(dob, lane, G)
                ds = p * (_dot_nt(dostack, vv) - jnp.tile(delta, (1, 2)))
                dsb, pb = ds.astype(BF16), p.astype(BF16)
                dq = _swa_unstack(_dot_nn(dsb, kk), lane, G) * scale
                dq = dq * t0 + pltpu.roll(dq * t1, ROT_DIM // 2, axis=1) + pltpu.roll(dq * t2, GW - ROT_DIM // 2, axis=1)
                dqg_ref[0, :, cols] = dq.astype(BF16)
                dkk = _dot_tn(dsb, qstack)
                dvv = _dot_tn(pb, dostack)
                dkk = dkk + pltpu.roll(dkk, HEAD_DIM, axis=1)
                dvv = dvv + pltpu.roll(dvv, HEAD_DIM, axis=1)
                lane2 = lax.broadcasted_iota(jnp.int32, (2 * SWA_BLOCK, LANE), 1)
                comb = jnp.where(lane2 < HEAD_DIM, dkk, dvv)
                dkv_ref[hh] = carry_sc[hh] + comb[:SWA_BLOCK]
                carry_sc[hh] = comb[SWA_BLOCK:]
                sk = psink * delta
                rows = [-jnp.sum(sk[g_ * SWA_BLOCK:(g_ + 1) * SWA_BLOCK], axis=0, keepdims=True) for g_ in range(G)]
                dsink_ref[hh] += jnp.concatenate(rows, axis=0)

        @pl.when(n == nb)
        def _():
            dkv_ref[...] = carry_sc[...]

    cl = lambda n: jnp.minimum(n, nb - 1)
    blk = lambda cb, prev: pl.BlockSpec(
        (SWA_BLOCK, NP * LANE), lambda h, n, cb=cb, prev=prev: (jnp.maximum(cl(n) - prev, 0), cb // NP + h))
    qspec = pl.BlockSpec((SWA_BLOCK, NH * GW), lambda h, n: (cl(n), h))
    return pl.pallas_call(
        body, name="swa_attn_bwd", grid=(HKV // NH, nb + 1),
        in_specs=[qspec, blk(kb, 1), blk(kb, 0), blk(vb, 1), blk(vb, 0), qspec, qspec, qspec,
                  pl.BlockSpec((1, LANE), lambda h, n: (0, 0)),
                  pl.BlockSpec((3, SWA_BLOCK, LANE), lambda h, n: (0, cl(n), 0)),
                  pl.BlockSpec((1, R, 2 * SWA_BLOCK), lambda h, n: (jnp.minimum(n, 1), 0, 0))],
        out_specs=[pl.BlockSpec((2, SWA_BLOCK, NH * GW), lambda h, n: (0, cl(n), h)),
                   pl.BlockSpec((NH, SWA_BLOCK, LANE), lambda h, n: (h, jnp.maximum(n - 1, 0), 0)),
                   pl.BlockSpec((NH, 8, LANE), lambda h, n: (h, 0, 0))],
        out_shape=[jax.ShapeDtypeStruct((2, S, WQ), BF16), jax.ShapeDtypeStruct((HKV, S, LANE), F32),
                   jax.ShapeDtypeStruct((HKV, 8, LANE), F32)],
        scratch_shapes=[pltpu.VMEM((NH, SWA_BLOCK, LANE), F32)],
        compiler_params=_cparams(),
    )(q, kv, kv, kv, kv, dy, gate, o, sinks, tables, mask_bias)


def _swa_dkv_finish(dkv, tables):
    HKV, S, _ = dkv.shape
    KVW = HKV * HEAD_DIM
    tm = _row_tile(S, EPI_TILE)
    npair = HKV // 2

    def body(d_ref, t_ref, o_ref):
        lane = lax.broadcasted_iota(jnp.int32, (tm, LANE), 1)
        lo = lane < HEAD_DIM
        for p in range(npair):
            a, b = d_ref[2 * p], d_ref[2 * p + 1]
            tk = jnp.where(lo, a, pltpu.roll(b, HEAD_DIM, axis=1))
            tv = jnp.where(lo, pltpu.roll(a, HEAD_DIM, axis=1), b)
            tk = (tk * t_ref[0] + pltpu.roll(tk * t_ref[1], ROT_DIM // 2, axis=1)
                  + pltpu.roll(tk * t_ref[2], LANE - ROT_DIM // 2, axis=1))
            o_ref[:, LANE * p:LANE * (p + 1)] = tk.astype(BF16)
            o_ref[:, KVW + LANE * p:KVW + LANE * (p + 1)] = tv.astype(BF16)

    return pl.pallas_call(
        body, name="swa_dkv_finish", grid=(S // tm,),
        in_specs=[pl.BlockSpec((HKV, tm, LANE), lambda i: (0, i, 0)), pl.BlockSpec((3, tm, LANE), lambda i: (0, i, 0))],
        out_specs=pl.BlockSpec((tm, 2 * KVW), lambda i: (i, 0)),
        out_shape=jax.ShapeDtypeStruct((S, 2 * KVW), BF16),
        compiler_params=_cparams(),
    )(dkv, tables)


def _rope_tables(S, width):
    half = ROT_DIM // 2
    pos = jnp.arange(S, dtype=F32)
    inv_freq = ROPE_THETA ** (-jnp.arange(half, dtype=F32) / half)
    ang = pos[:, None] * inv_freq[None, :]
    cos, sin = jnp.cos(ang), jnp.sin(ang)
    one = jnp.ones((S, HEAD_DIM - ROT_DIM), F32)
    zero = jnp.zeros((S, HEAD_DIM - ROT_DIM), F32)
    zh = jnp.zeros((S, half), F32)
    t0 = jnp.concatenate([cos, cos, one], axis=1)
    t1 = jnp.concatenate([-sin, zh, zero], axis=1)
    t2 = jnp.concatenate([zh, sin, zero], axis=1)
    return jnp.stack([jnp.tile(t, (1, width // HEAD_DIM)) for t in (t0, t1, t2)])


def _pad_rows(v, row, total_rows=8):
    return jnp.pad(v, ((row, total_rows - row - v.shape[0]), (0, 0)))


def _pad_lanes(v, off, width):
    return jnp.pad(v, ((0, 0), (off, width - off - v.shape[1])))


def kernel(x, norm_g, fox_w_in, fox_b_f, fox_w_out, swa_w_in, swa_sinks, swa_w_out, final_g, loss_target, m_norm_g, m_fox_w_in, m_fox_b_f, m_fox_w_out, m_swa_w_in, m_swa_sinks, m_swa_w_out, m_final_g, v_norm_g, v_fox_w_in, v_fox_b_f, v_fox_w_out, v_swa_w_in, v_swa_sinks, v_swa_w_out, v_final_g):
    S, D = x.shape[1], x.shape[2]
    H = fox_b_f.shape[1]
    W = H * HEAD_DIM
    wf = fox_w_in.shape[2]
    ws = swa_w_in.shape[2]
    HQ = swa_sinks.shape[1]
    WQ = HQ * HEAD_DIM
    KVW = (ws * N_DEV - 2 * WQ) // 2
    HKV = KVW // HEAD_DIM
    rows_o = fox_w_out.shape[1]
    assert wf * N_DEV == 4 * W + H and rows_o * N_DEV == W and H <= LANE and HQ <= LANE
    me = _my_index()

    _, sw_f, np_f = _slab_geom(wf)
    _, sw_s, np_s = _slab_geom(ws)

    def slab(w2d, w, sw):
        return jnp.pad(w2d.astype(BF16), ((0, 0), (0, sw - w)))

    (fi_all,) = _all_gather([slab(fox_w_in[0], wf, sw_f)])
    w_fi = _assemble(fi_all, wf)
    later = [slab(swa_w_in[0], ws, sw_s), fox_w_out[0].astype(BF16), swa_w_out[0].astype(BF16)]

    x0 = x[0]
    g0, g1, gf = norm_g[0:1], norm_g[1:2], final_g[None, :]
    bias = _pad_lanes(fox_b_f, 0, LANE)
    sinks = _pad_lanes(swa_sinks, 0, LANE)
    tab_k = _rope_tables(S, LANE)
    mask_bias = _swa_mask_bias(HQ // HKV)

    h0 = _rmsnorm_fwd(x0, g0, "rmsnorm0")
    qkv0 = _proj(h0, w_fi, 0, 4 * W, BF16, "fox_in_qkvg")
    fl = _proj(h0, w_fi, 4 * W, LANE, F32, "fox_in_f")
    c = _fox_gate_fwd(fl, bias)
    y0, o0, a0, (si_all, fo_all, so_all) = _fox_fwd(qkv0, c, H, gather=later)
    w_si = _assemble(si_all, ws)
    w_fo = fo_all.reshape(W, D)
    w_so = so_all.reshape(WQ, D)
    x1, h1 = _out_proj_norm(y0, w_fo, x0, g1, "fox_out")

    q1 = _proj(h1, w_si, 0, WQ, BF16, "swa_in_q", rope=(tab_k, WQ))
    kv1 = _proj(h1, w_si, WQ, 2 * KVW, BF16, "swa_in_kv", rope=(tab_k, KVW))
    gate1 = _proj(h1, w_si, WQ + 2 * KVW, WQ, BF16, "swa_in_gate")
    y1, o1 = _swa_fwd(q1, kv1, gate1, sinks, mask_bias, HQ, HKV)
    dx2, dx2b, dgf, loss_p = _out_proj_loss(y1, w_so, x1, loss_target[0], gf, "swa_out_loss")

    dy1 = _matmul_nt([(dx2b, None, 0)], w_so, WQ, "swa_out_bwd")
    g_so, g_so_h = _matmul_tn(y1, [(dx2b, None, 0)], D, "swa_out_wgrad", also_bf16=True)
    dqg1, dkv1, dsink = _swa_bwd(q1, kv1, dy1, gate1, o1, sinks, tab_k, mask_bias, HQ, HKV)
    dkv1f = _swa_dkv_finish(dkv1, tab_k)
    parts1 = [(dqg1, 0, 0), (dkv1f, None, WQ), (dqg1, 1, WQ + 2 * KVW)]
    g_si, g_si_h = _matmul_tn(h1, parts1, np_s, "swa_in_wgrad", tile_major=True, also_bf16=True)
    dh1 = _matmul_nt(parts1, w_si, D, "swa_in_bwd")
    dx1, dx1b, dg1 = _rmsnorm_bwd(dh1, x1, g1, dx2, "rmsnorm1_bwd")

    qaug0, doaug0, dqkvg0 = _fox_out_bwd(dx1b, w_fo, qkv0, o0, a0, H)
    g_fo, g_fo_h = _matmul_tn(y0, [(dx1b, None, 0)], D, "fox_out_wgrad", also_bf16=True)
    early_specs = [("col", ws), ("row", rows_o), ("row", rows_o)]
    dqkvg0, dcr, dcc, early_recv = _fox_bwd(qaug0, doaug0, qkv0, c, dqkvg0, H, scatter=[g_si_h, g_fo_h, g_so_h],
                                           scatter_specs=early_specs)
    dfl, dbf = _fox_gate_bwd(fl, bias, dcr - dcc)
    parts0 = [(dqkvg0, "stack", 0), (dfl, None, 4 * W)]
    g_fi, g_fi_h = _matmul_tn(h0, parts0, np_f, "fox_in_wgrad", tile_major=True, also_bf16=True)
    spec_fi = ("col", wf)
    fi_sems, fi_src, fi_land, token = _scatter_start(g_fi_h, spec_fi)
    parts0[-1] = (dfl + token[0, 0].astype(BF16), None, 4 * W)
    dh0 = _matmul_nt(parts0, w_fi, D, "fox_in_bwd")
    dx0, _, dg0 = _rmsnorm_bwd(dh0, x0, g0, dx1, "rmsnorm0_bwd")

    red_si, gw_fo, gw_so = [_final_sum8(g_, r_, s_)
                            for g_, r_, s_ in zip([g_si, g_fo, g_so], early_recv, early_specs)]
    gt_si = lax.dynamic_slice(red_si, ((ws * me) % LANE, 0), (ws, D))

    def t_in(p):
        return jnp.swapaxes(p[0], 0, 1)

    def t_out(t):
        return jnp.swapaxes(t, 0, 1)[None]

    P = D
    dsink_v = dsink[:, :, 0].reshape(1, HQ)
    row3 = _pad_lanes(dbf[:, :H], 0, P) + _pad_lanes(dsink_v, LANE, P) + _pad_lanes(loss_p[:, :1], 2 * LANE, P)
    pack = _pad_rows(dg0, 0) + _pad_rows(dg1, 1) + _pad_rows(dgf, 2) + _pad_rows(row3, 3)

    d_fo, m_fo, v_fo = _adamw(fox_w_out[0], gw_fo, m_fox_w_out[0], v_fox_w_out[0], "adamw_fox_out")
    d_si, m_si, v_si = _adamw(t_in(swa_w_in), gt_si, t_in(m_swa_w_in), t_in(v_swa_w_in), "adamw_swa_in")
    d_so, m_so, v_so = _adamw(swa_w_out[0], gw_so, m_swa_w_out[0], v_swa_w_out[0], "adamw_swa_out")
    recv_fi = _scatter_wait(fi_sems, fi_src, fi_land, spec_fi, after=[dx0, pack, d_fo, d_si, d_so])
    red_fi = _final_sum8(g_fi, recv_fi, spec_fi)
    gt_fi = lax.dynamic_slice(red_fi, ((wf * me) % LANE, 0), (wf, D))

    d_fi, m_fi, v_fi = _adamw(t_in(fox_w_in), gt_fi, t_in(m_fox_w_in), t_in(v_fox_w_in), "adamw_fox_in")
    gw_si, d_si, m_si, v_si = [t_out(t)[0] for t in (gt_si, d_si, m_si, v_si)]
    gw_fi, d_fi, m_fi, v_fi = [t_out(t)[0] for t in (gt_fi, d_fi, m_fi, v_fi)]

    tot = _all_reduce_small(pack, after=recv_fi)
    loss = tot[3, 2 * LANE]
    g_norm = tot[0:2]
    g_final = tot[2]
    g_bf = tot[3:4, 0:H]
    g_sinks = tot[3:4, LANE:LANE + HQ]

    def small_pack(ng, fg, bf, sk):
        r3 = _pad_lanes(bf, 0, P) + _pad_lanes(sk, LANE, P)
        return _pad_rows(ng, 0) + _pad_rows(fg[None, :], 2) + _pad_rows(r3, 3)

    sd, sm, sv = _adamw(small_pack(norm_g, final_g, fox_b_f, swa_sinks), tot,
                        small_pack(m_norm_g, m_final_g, m_fox_b_f, m_swa_sinks),
                        small_pack(v_norm_g, v_final_g, v_fox_b_f, v_swa_sinks), "adamw_small")

    def unpack(t):
        return t[0:2], t[3:4, 0:H], t[3:4, LANE:LANE + HQ], t[2]

    def group(small, fi, fo, si, so):
        ng, bf, sk, fg = unpack(small)
        return (ng, fi[None], bf, fo[None], si[None], sk, so[None], fg)

    grads = (g_norm, gw_fi[None], g_bf, gw_fo[None], gw_si[None], g_sinks, gw_so[None], g_final)
    return (loss, dx0[None], *grads, *group(sd, d_fi, d_fo, d_si, d_so),
            *group(sm, m_fi, m_fo, m_si, m_so), *group(sv, v_fi, v_fo, v_si, v_so))
```

```python
import math

import jax
import jax.numpy as jnp
from jax import lax
from jax.experimental import pallas as pl
from jax.experimental.pallas import tpu as pltpu

F32 = jnp.float32
BF16 = jnp.bfloat16
MESH = pl.DeviceIdType.MESH

N_DEV = 8
LANE = 128
HEAD_DIM = 64
SWA_BLOCK = 128
NEG_INF = -1e30
RMS_EPS = 1e-6
ROPE_THETA = 500000.0
ROT_DIM = HEAD_DIM // 4
ADAM_LR, ADAM_B1, ADAM_B2, ADAM_EPS, ADAM_WD, ADAM_STEP = 0.001, 0.9, 0.999, 1e-08, 0.01, 10
VMEM_LIMIT = 56 * 1024 * 1024
MM_TILE = 1024
ATT_TILE = 512
EPI_TILE = 512
ROW_TILE = 256
ADAM_TILE_ELEMS = 3 << 18


def _cparams(**kw):
    return pltpu.CompilerParams(vmem_limit_bytes=VMEM_LIMIT, **kw)


def _tile(n, cap):
    if n <= cap:
        return n
    t = (cap // LANE) * LANE
    while t > LANE and n % t:
        t -= LANE
    assert n % t == 0, (n, cap)
    return t


def _row_tile(n, cap):
    t = min(n, cap)
    while n % t:
        t //= 2
    return t


def _dot_nn(a, b):
    return jnp.dot(a, b, preferred_element_type=F32)


def _dot_nt(a, b):
    return lax.dot_general(a, b, (((1,), (1,)), ((), ())), preferred_element_type=F32)


def _dot_tn(a, b):
    return lax.dot_general(a, b, (((0,), (0,)), ((), ())), preferred_element_type=F32)


def _split3(x):
    hi = x.astype(BF16)
    r1 = x - hi.astype(F32)
    mid = r1.astype(BF16)
    return hi, mid, (r1 - mid.astype(F32)).astype(BF16)


def _sigmoid(g):
    return 1.0 / (1.0 + jnp.exp(-g))


def _slab_geom(w):
    starts = [w * i for i in range(N_DEV)]
    aligned = [LANE * (s // LANE) for s in starts]
    offs = [s - a for s, a in zip(starts, aligned)]
    sw = LANE * (-(-(max(offs) + w) // LANE))
    return aligned, sw, aligned[-1] + sw


def _my_index():
    return 4 * lax.axis_index("x") + 2 * lax.axis_index("y") + lax.axis_index("c")


def _all_gather(arrs):
    n = len(arrs)

    def body(*refs):
        ins, outs = refs[:n], refs[n:2 * n]
        send_sems, recv_sems, local_sems = refs[2 * n:]
        x, y, c = lax.axis_index("x"), lax.axis_index("y"), lax.axis_index("c")
        me, sib = (x, y, c), (x, y, 1 - c)
        chips = [(1 - x, y), (x, 1 - y), (1 - x, 1 - y)]

        def idx(px, py, pc):
            return 4 * px + 2 * py + pc

        def copy(a, k, block, to, src=None):
            dst = outs[a].at[idx(*block)]
            return pltpu.make_async_remote_copy(
                src_ref=dst if src is None else src, dst_ref=dst,
                send_sem=send_sems.at[a, k], recv_sem=recv_sems.at[a, k],
                device_id=to, device_id_type=MESH)

        mine = [pltpu.make_async_copy(ins[a], outs[a].at[idx(*me)], local_sems.at[a]) for a in range(n)]
        for m in mine:
            m.start()
        first = []
        for a in range(n):
            first.append(copy(a, 0, me, sib, src=ins[a]))
            for j, chip in enumerate(chips):
                first.append(copy(a, 1 + j, me, (*chip, c), src=ins[a]))
        for cp in first:
            cp.start()
        passed = []
        for j, chip in enumerate(chips):
            for a in range(n):
                copy(a, 1 + j, (*chip, c), me).wait_recv()
                p = copy(a, 4 + j, (*chip, c), sib)
                p.start()
                passed.append(p)
        for a in range(n):
            copy(a, 0, sib, me).wait_recv()
        for j, chip in enumerate(chips):
            for a in range(n):
                copy(a, 4 + j, (*chip, 1 - c), me).wait_recv()
        for cp in first + passed:
            cp.wait_send()
        for m in mine:
            m.wait()

    any_spec = pl.BlockSpec(memory_space=pl.ANY)
    return pl.pallas_call(
        body, name="weights_all_gather",
        out_shape=[jax.ShapeDtypeStruct((N_DEV,) + a.shape, a.dtype) for a in arrs],
        in_specs=[any_spec] * n, out_specs=[any_spec] * n,
        scratch_shapes=[pltpu.SemaphoreType.DMA((n, 7)), pltpu.SemaphoreType.DMA((n, 7)),
                        pltpu.SemaphoreType.DMA((n,))],
    )(*arrs)


def _rs_windows(specs):
    def window(ref, spec, blk):
        kind, n = spec
        if kind == "col":
            _, sw, _ = _slab_geom(n)
            return ref.at[pl.ds((n * blk) // LANE, sw // LANE)]
        start = pl.multiple_of(n * blk, n)
        return ref.at[pl.ds(start, n), :]
    return window


def _peer(k):
    x, y, c = lax.axis_index("x"), lax.axis_index("y"), lax.axis_index("c")
    return (x ^ (k >> 2), y ^ ((k >> 1) & 1), c ^ (k & 1))


def _direct_gather_copies(ins, outs, send_sems, recv_sems, local_sems):
    me = _my_index()
    remote, local = [], []
    for a, (src, dst) in enumerate(zip(ins, outs)):
        local.append(pltpu.make_async_copy(src, dst.at[me], local_sems.at[a]))
        for k in range(1, N_DEV):
            remote.append(pltpu.make_async_remote_copy(
                src_ref=src, dst_ref=dst.at[me], send_sem=send_sems.at[a, k - 1], recv_sem=recv_sems.at[a, k - 1],
                device_id=_peer(k), device_id_type=MESH))
    return remote, local


def _direct_scatter_copies(ins, outs, specs, send_sems, recv_sems):
    window = _rs_windows(specs)
    remote = []
    for a, (src, dst) in enumerate(zip(ins, outs)):
        for k in range(1, N_DEV):
            px, py, pc = _peer(k)
            remote.append(pltpu.make_async_remote_copy(
                src_ref=window(src, specs[a], 4 * px + 2 * py + pc), dst_ref=dst.at[k - 1],
                send_sem=send_sems.at[a, k - 1], recv_sem=recv_sems.at[a, k - 1],
                device_id=(px, py, pc), device_id_type=MESH))
    return remote


def _scatter_block_shape(g, spec):
    kind, w = spec
    return (_slab_geom(w)[1] // LANE, g.shape[1], LANE) if kind == "col" else (w, g.shape[1])


def _wait_all(remote, local=()):
    for cp in remote:
        cp.wait_recv()
    for cp in remote:
        cp.wait_send()
    for cp in local:
        cp.wait()


def _scatter_start(g, spec):
    blk = _scatter_block_shape(g, spec)
    window = _rs_windows([spec])
    npeer = N_DEV - 1

    def body(g_ref, land_ref, *rest):
        sems = rest[:2 * npeer]
        token = rest[2 * npeer + 2]
        for cp in _peer_block_copies(g_ref, land_ref, spec, window, sems[:npeer], sems[npeer:]):
            cp.start()
        token[...] = jnp.zeros(token.shape, token.dtype)

    hbm = pl.BlockSpec(memory_space=pltpu.HBM)
    sem = pl.BlockSpec(memory_space=pltpu.SEMAPHORE)
    land = lax.empty((npeer,) + blk, g.dtype)
    outs = pl.pallas_call(
        body, name="grads_scatter_start",
        out_shape=(pltpu.SemaphoreType.DMA(()),) * (2 * npeer)
        + (pltpu.HBM(g.shape, g.dtype), pltpu.HBM(land.shape, land.dtype), jax.ShapeDtypeStruct((8, LANE), F32)),
        in_specs=(hbm, hbm),
        out_specs=(sem,) * (2 * npeer) + (hbm, hbm, pl.BlockSpec(memory_space=pltpu.VMEM)),
        input_output_aliases={0: 2 * npeer, 1: 2 * npeer + 1},
        compiler_params=pltpu.CompilerParams(has_side_effects=pltpu.SideEffectType.DATAFLOW_SIDE_EFFECTING),
    )(pltpu.with_memory_space_constraint(g, pltpu.HBM), pltpu.with_memory_space_constraint(land, pltpu.HBM))
    return outs[:2 * npeer], outs[2 * npeer], outs[2 * npeer + 1], outs[2 * npeer + 2]


def _peer_block_copies(g_ref, land_ref, spec, window, send_sems, recv_sems):
    copies = []
    for k in range(1, N_DEV):
        px, py, pc = _peer(k)
        copies.append(pltpu.make_async_remote_copy(
            src_ref=window(g_ref, spec, 4 * px + 2 * py + pc), dst_ref=land_ref.at[k - 1],
            send_sem=send_sems[k - 1], recv_sem=recv_sems[k - 1], device_id=(px, py, pc), device_id_type=MESH))
    return copies


def _scatter_wait(sems, g_thru, land_thru, spec, after):
    window = _rs_windows([spec])
    npeer = N_DEV - 1

    def body(g_ref, land_ref, *rest):
        s = rest[:2 * npeer]
        copies = _peer_block_copies(g_ref, land_ref, spec, window, s[:npeer], s[npeer:])
        for cp in copies:
            cp.wait_send()
        for cp in copies:
            cp.wait_recv()

    hbm = pl.BlockSpec(memory_space=pltpu.HBM)
    sem = pl.BlockSpec(memory_space=pltpu.SEMAPHORE)
    return pl.pallas_call(
        body, name="grads_scatter_wait",
        out_shape=(pltpu.HBM(g_thru.shape, g_thru.dtype), pltpu.HBM(land_thru.shape, land_thru.dtype)),
        in_specs=(hbm, hbm) + (sem,) * (2 * npeer) + (pl.BlockSpec(memory_space=pl.ANY),) * len(after),
        out_specs=(hbm, hbm), input_output_aliases={0: 0, 1: 1},
        compiler_params=pltpu.CompilerParams(has_side_effects=pltpu.SideEffectType.DATAFLOW_SIDE_EFFECTING),
    )(g_thru, land_thru, *sems, *after)[1]


def _final_sum8(g, recv, spec):
    kind, n = spec
    me = _my_index()
    offs = jnp.stack([(n * me) // LANE if kind == "col" else me]).astype(jnp.int32)
    if kind == "col":
        _, T, M, _ = recv.shape
        grid = (T,)
        in_specs = [pl.BlockSpec((1, M, LANE), lambda t, o: (o[0] + t, 0, 0)),
                    pl.BlockSpec((N_DEV - 1, 1, M, LANE), lambda t, o: (0, t, 0, 0))]
        out_spec = pl.BlockSpec((LANE, M), lambda t, o: (t, 0))
        out_shape = jax.ShapeDtypeStruct((T * LANE, M), F32)
    else:
        _, nrow, C = recv.shape
        grid = (1,)
        in_specs = [pl.BlockSpec((nrow, C), lambda t, o: (o[0], 0)),
                    pl.BlockSpec((N_DEV - 1, nrow, C), lambda t, o: (0, 0, 0))]
        out_spec = pl.BlockSpec((nrow, C), lambda t, o: (0, 0))
        out_shape = jax.ShapeDtypeStruct((nrow, C), F32)

    def body(o_ref, g_ref, r_ref, out_ref):
        acc = g_ref[0] if kind == "col" else g_ref[...]
        for k in range(N_DEV - 1):
            acc = acc + (r_ref[k, 0] if kind == "col" else r_ref[k]).astype(F32)
        out_ref[...] = acc.T if kind == "col" else acc

    return pl.pallas_call(
        body, name="grads_final_sum8",
        grid_spec=pltpu.PrefetchScalarGridSpec(num_scalar_prefetch=1, grid=grid, in_specs=in_specs,
                                               out_specs=out_spec),
        out_shape=out_shape, compiler_params=_cparams(),
    )(offs, g, recv)


def _all_reduce_small(pack, after):
    R, P = pack.shape

    def body(x_ref, after_ref, o_ref, gat_ref, send_sems, recv_sems):
        x, y, c = lax.axis_index("x"), lax.axis_index("y"), lax.axis_index("c")
        me = 4 * x + 2 * y + c
        gat_ref[me] = x_ref[...]
        copies = []
        for k in range(1, N_DEV):
            peer = (x ^ (k >> 2), y ^ ((k >> 1) & 1), c ^ (k & 1))
            copies.append(pltpu.make_async_remote_copy(
                src_ref=x_ref, dst_ref=gat_ref.at[me],
                send_sem=send_sems.at[k - 1], recv_sem=recv_sems.at[k - 1],
                device_id=peer, device_id_type=MESH))
        for cp in copies:
            cp.start()
        for cp in copies:
            cp.wait_recv()
        for cp in copies:
            cp.wait_send()
        acc = gat_ref[0]
        for d in range(1, N_DEV):
            acc = acc + gat_ref[d]
        o_ref[...] = acc

    vm = pl.BlockSpec(memory_space=pltpu.VMEM)
    return pl.pallas_call(
        body, name="small_all_reduce",
        out_shape=jax.ShapeDtypeStruct((R, P), F32),
        in_specs=[vm, pl.BlockSpec(memory_space=pl.ANY)], out_specs=vm,
        scratch_shapes=[pltpu.VMEM((N_DEV, R, P), F32),
                        pltpu.SemaphoreType.DMA((N_DEV - 1,)), pltpu.SemaphoreType.DMA((N_DEV - 1,))],
    )(pack, after)


def _assemble(slabs, w):
    aligned, sw, total = _slab_geom(w)
    K = slabs.shape[1]
    tr = _row_tile(K, ROW_TILE)

    def body(s_ref, o_ref):
        o_ref[...] = jnp.zeros(o_ref.shape, BF16)
        for i in range(N_DEV):
            a, off = aligned[i], w * i - aligned[i]
            x = s_ref[i].astype(F32)
            if off:
                x = pltpu.roll(x, off, axis=1)
            o_ref[:, a:a + sw] = (o_ref[:, a:a + sw].astype(F32) + x).astype(BF16)

    return pl.pallas_call(
        body, name="assemble_w_in", grid=(K // tr,),
        in_specs=[pl.BlockSpec((N_DEV, tr, sw), lambda i: (0, i, 0))],
        out_specs=pl.BlockSpec((tr, total), lambda i: (i, 0)),
        out_shape=jax.ShapeDtypeStruct((K, total), BF16),
        compiler_params=_cparams(),
    )(slabs)


def _rmsnorm_fwd(x, g, name):
    S, D = x.shape
    tm = _row_tile(S, ROW_TILE)

    def body(x_ref, g_ref, h_ref):
        xv = x_ref[...]
        r = lax.rsqrt(jnp.mean(xv * xv, axis=-1, keepdims=True) + RMS_EPS)
        h_ref[...] = ((xv * r) * g_ref[...]).astype(BF16)

    return pl.pallas_call(
        body, name=name, grid=(S // tm,),
        in_specs=[pl.BlockSpec((tm, D), lambda i: (i, 0)), pl.BlockSpec((1, D), lambda i: (0, 0))],
        out_specs=pl.BlockSpec((tm, D), lambda i: (i, 0)),
        out_shape=jax.ShapeDtypeStruct((S, D), BF16),
        compiler_params=_cparams(),
    )(x, g)


def _rmsnorm_bwd(dh, x, g, dres, name):
    S, D = x.shape
    tm = _row_tile(S, ROW_TILE)

    def body(dh_ref, x_ref, g_ref, dr_ref, dx_ref, dxb_ref, dg_ref):
        xv = x_ref[...]
        r = lax.rsqrt(jnp.mean(xv * xv, axis=-1, keepdims=True) + RMS_EPS)
        xhat = xv * r
        d = dh_ref[...]
        gd = d * g_ref[...]
        dx = r * (gd - xhat * jnp.mean(gd * xhat, axis=-1, keepdims=True)) + dr_ref[...]
        dx_ref[...] = dx
        dxb_ref[...] = dx.astype(BF16)

        @pl.when(pl.program_id(0) == 0)
        def _():
            dg_ref[...] = jnp.zeros(dg_ref.shape, F32)
        dg_ref[...] += jnp.sum(d * xhat, axis=0, keepdims=True)

    row = pl.BlockSpec((tm, D), lambda i: (i, 0))
    vec = pl.BlockSpec((1, D), lambda i: (0, 0))
    return pl.pallas_call(
        body, name=name, grid=(S // tm,),
        in_specs=[row, row, vec, row], out_specs=[row, row, vec],
        out_shape=[jax.ShapeDtypeStruct((S, D), F32), jax.ShapeDtypeStruct((S, D), BF16),
                   jax.ShapeDtypeStruct((1, D), F32)],
        compiler_params=_cparams(),
    )(dh, x, g, dres)


def _adamw(w, g, m, v, name):
    R, C = w.shape
    steps = pl.cdiv(R * C, ADAM_TILE_ELEMS)
    tr = R if steps == 1 else pl.cdiv(pl.cdiv(R, steps), 8) * 8
    c1 = 1.0 - ADAM_B1 ** ADAM_STEP
    c2 = 1.0 - ADAM_B2 ** ADAM_STEP

    def body(w_ref, g_ref, m_ref, v_ref, d_ref, nm_ref, nv_ref):
        gv = g_ref[...]
        nm = ADAM_B1 * m_ref[...] + (1.0 - ADAM_B1) * gv
        nv = ADAM_B2 * v_ref[...] + (1.0 - ADAM_B2) * (gv * gv)
        d_ref[...] = -ADAM_LR * ((nm / c1) / (jnp.sqrt(nv / c2) + ADAM_EPS) + ADAM_WD * w_ref[...])
        nm_ref[...] = nm
        nv_ref[...] = nv

    spec = pl.BlockSpec((tr, C), lambda i: (i, 0))
    return pl.pallas_call(
        body, name=name, grid=(pl.cdiv(R, tr),),
        in_specs=[spec] * 4, out_specs=[spec] * 3,
        out_shape=[jax.ShapeDtypeStruct((R, C), F32)] * 3,
        compiler_params=_cparams(),
    )(w, g, m, v)


def _proj(h, wfull, col0, ncols, out_dtype, name, rope=None):
    S, K = h.shape
    tm = _row_tile(S, MM_TILE)
    tn = math.gcd(_tile(ncols, MM_TILE), col0) if col0 else _tile(ncols, MM_TILE)
    if rope is not None:
        tn = _tile(math.gcd(ncols, rope[1]), MM_TILE)
    assert ncols % tn == 0 and col0 % tn == 0
    cb = col0 // tn

    def body(*refs):
        if rope is None:
            a_ref, b_ref, o_ref = refs
        else:
            a_ref, b_ref, t_ref, o_ref = refs
        acc = _dot_nn(a_ref[...], b_ref[...])
        if rope is not None:
            t0, t1, t2 = (jnp.tile(t_ref[i], (1, tn // LANE)) for i in range(3))
            roped = (acc * t0 + pltpu.roll(acc, tn - ROT_DIM // 2, axis=1) * t1
                     + pltpu.roll(acc, ROT_DIM // 2, axis=1) * t2)
            acc = jnp.where(pl.program_id(1) < rope[1] // tn, roped, acc)
        o_ref[...] = acc.astype(out_dtype)

    in_specs = [pl.BlockSpec((tm, K), lambda i, j: (i, 0)), pl.BlockSpec((K, tn), lambda i, j: (0, cb + j))]
    args = [h, wfull]
    if rope is not None:
        in_specs.append(pl.BlockSpec((3, tm, LANE), lambda i, j: (0, i, 0)))
        args.append(rope[0])
    return pl.pallas_call(
        body, name=name, grid=(S // tm, ncols // tn),
        in_specs=in_specs, out_specs=pl.BlockSpec((tm, tn), lambda i, j: (i, j)),
        out_shape=jax.ShapeDtypeStruct((S, ncols), out_dtype),
        compiler_params=_cparams(),
    )(*args)


def _out_proj_norm(y, wo, xres, g, name):
    S, W = y.shape
    D = wo.shape[1]
    tm = _row_tile(S, EPI_TILE)

    def body(a_ref, b_ref, r_ref, g_ref, x_ref, h_ref):
        xv = r_ref[...] + _dot_nn(a_ref[...], b_ref[...])
        x_ref[...] = xv
        r = lax.rsqrt(jnp.mean(xv * xv, axis=-1, keepdims=True) + RMS_EPS)
        h_ref[...] = ((xv * r) * g_ref[...]).astype(BF16)

    row = pl.BlockSpec((tm, D), lambda i: (i, 0))
    return pl.pallas_call(
        body, name=name, grid=(S // tm,),
        in_specs=[pl.BlockSpec((tm, W), lambda i: (i, 0)), pl.BlockSpec((W, D), lambda i: (0, 0)), row,
                  pl.BlockSpec((1, D), lambda i: (0, 0))],
        out_specs=[row, row],
        out_shape=[jax.ShapeDtypeStruct((S, D), F32), jax.ShapeDtypeStruct((S, D), BF16)],
        compiler_params=_cparams(),
    )(y, wo, xres, g)


def _out_proj_loss(y, wo, xres, tgt, g, name):
    S, W = y.shape
    D = wo.shape[1]
    tm = _row_tile(S, EPI_TILE)

    def body(a_ref, b_ref, r_ref, t_ref, g_ref, dx_ref, dxb_ref, dg_ref, loss_ref):
        xv = r_ref[...] + _dot_nn(a_ref[...], b_ref[...])
        r = lax.rsqrt(jnp.mean(xv * xv, axis=-1, keepdims=True) + RMS_EPS)
        xhat = xv * r
        gv = g_ref[...]
        err = xhat * gv - t_ref[...]
        d = err * (1.0 / D)
        gd = d * gv
        dx = r * (gd - xhat * jnp.mean(gd * xhat, axis=-1, keepdims=True))
        dx_ref[...] = dx
        dxb_ref[...] = dx.astype(BF16)

        @pl.when(pl.program_id(0) == 0)
        def _():
            dg_ref[...] = jnp.zeros(dg_ref.shape, F32)
            loss_ref[...] = jnp.zeros(loss_ref.shape, F32)
        dg_ref[...] += jnp.sum(d * xhat, axis=0, keepdims=True)
        per_tok = jnp.sum(err * err, axis=-1, keepdims=True) * (1.0 / D)
        loss_ref[...] += 0.5 * jnp.sum(per_tok, axis=0, keepdims=True)

    row = pl.BlockSpec((tm, D), lambda i: (i, 0))
    vec = pl.BlockSpec((1, D), lambda i: (0, 0))
    return pl.pallas_call(
        body, name=name, grid=(S // tm,),
        in_specs=[pl.BlockSpec((tm, W), lambda i: (i, 0)), pl.BlockSpec((W, D), lambda i: (0, 0)), row, row, vec],
        out_specs=[row, row, vec, pl.BlockSpec((1, LANE), lambda i: (0, 0))],
        out_shape=[jax.ShapeDtypeStruct((S, D), F32), jax.ShapeDtypeStruct((S, D), BF16),
                   jax.ShapeDtypeStruct((1, D), F32), jax.ShapeDtypeStruct((1, LANE), F32)],
        compiler_params=_cparams(),
    )(y, wo, xres, tgt, g)


def _matmul_nt(parts, wfull, out_rows, name):
    S = parts[0][0].shape[-2]
    tm, tn = _row_tile(S, 2 * MM_TILE if len(parts) <= 2 else MM_TILE), _tile(out_rows, MM_TILE)
    plan, lo = [], 0
    for arr, lead, col0 in parts:
        n_p = arr.shape[-1]
        tk = math.gcd(_tile(n_p, MM_TILE), col0) if col0 else _tile(n_p, MM_TILE)
        steps = n_p // tk * (arr.shape[0] if lead == "stack" else 1)
        plan.append((lead, col0 // tk, tk, lo, lo + steps))
        lo += steps
    nk = lo
    npart = len(parts)

    def body(*refs):
        a_refs, w_refs = refs[:npart], refs[npart:2 * npart]
        o_ref, acc_ref = refs[2 * npart], refs[2 * npart + 1]
        k = pl.program_id(2)

        @pl.when(k == 0)
        def _():
            acc_ref[...] = jnp.zeros(acc_ref.shape, F32)
        for p, (_, _, _, lo_p, hi_p) in enumerate(plan):
            @pl.when((k >= lo_p) & (k < hi_p))
            def _(p=p):
                acc_ref[...] += _dot_nt(a_refs[p][...], w_refs[p][...])

        @pl.when(k == nk - 1)
        def _():
            o_ref[...] = acc_ref[...]

    in_specs, args = [], []
    for (arr, lead, col0), (_, cb, tk, lo_p, hi_p) in zip(parts, plan):
        def kk(k, lo_p=lo_p, hi_p=hi_p):
            return jnp.clip(k - lo_p, 0, hi_p - lo_p - 1)
        if lead is None:
            in_specs.append(pl.BlockSpec((tm, tk), lambda i, j, k, kk=kk: (i, kk(k))))
        elif lead == "stack":
            nkb = arr.shape[-1] // tk
            in_specs.append(pl.BlockSpec((None, tm, tk), lambda i, j, k, kk=kk, nkb=nkb: (kk(k) // nkb, i, kk(k) % nkb)))
        else:
            in_specs.append(pl.BlockSpec((None, tm, tk), lambda i, j, k, kk=kk, lead=lead: (lead, i, kk(k))))
        args.append(arr)
    for (_, cb, tk, lo_p, hi_p) in plan:
        def kk(k, lo_p=lo_p, hi_p=hi_p):
            return jnp.clip(k - lo_p, 0, hi_p - lo_p - 1)
        in_specs.append(pl.BlockSpec((tn, tk), lambda i, j, k, kk=kk, cb=cb: (j, cb + kk(k))))
        args.append(wfull)
    return pl.pallas_call(
        body, name=name, grid=(S // tm, out_rows // tn, nk),
        in_specs=in_specs, out_specs=pl.BlockSpec((tm, tn), lambda i, j, k: (i, j)),
        out_shape=jax.ShapeDtypeStruct((S, out_rows), F32),
        scratch_shapes=[pltpu.VMEM((tm, tn), F32)],
        compiler_params=_cparams(),
    )(*args)


def _matmul_tn(a, parts, total, name, tile_major=False, also_bf16=False, rows=None, after=()):
    S = a.shape[0]
    m0, M = rows or (0, a.shape[1])
    tm, ts = _tile(M, MM_TILE), _row_tile(S, 2 * MM_TILE)
    ib = m0 // tm
    nout = 2 if also_bf16 else 1
    outs = None
    for idx, (arr, lead, col0) in enumerate(parts):
        n_p = arr.shape[-1]
        tn = math.gcd(_tile(n_p, MM_TILE), col0) if col0 else _tile(n_p, MM_TILE)
        cb = col0 // tn
        nk = S // ts
        nb = n_p // tn
        if lead == "stack":
            n_p *= arr.shape[0]

        def body(*refs, nk=nk, tn=tn):
            a_ref, b_ref = refs[0], refs[1]
            o_refs, acc_ref = refs[-1 - nout:-1], refs[-1]
            k = pl.program_id(2)

            @pl.when(k == 0)
            def _():
                acc_ref[...] = jnp.zeros(acc_ref.shape, F32)
            acc_ref[...] += _dot_tn(a_ref[...], b_ref[...])

            @pl.when(k == nk - 1)
            def _():
                for o_ref in o_refs:
                    if tile_major:
                        for t in range(tn // LANE):
                            o_ref[t] = acc_ref[:, LANE * t:LANE * (t + 1)].astype(o_ref.dtype)
                    else:
                        o_ref[...] = acc_ref[...].astype(o_ref.dtype)

        in_specs = [pl.BlockSpec((ts, tm), lambda i, j, k: (k, ib + i))]
        if lead is None:
            in_specs.append(pl.BlockSpec((ts, tn), lambda i, j, k: (k, j)))
        elif lead == "stack":
            in_specs.append(pl.BlockSpec((None, ts, tn), lambda i, j, k, nb=nb: (j // nb, k, j % nb)))
        else:
            in_specs.append(pl.BlockSpec((None, ts, tn), lambda i, j, k, lead=lead: (lead, k, j)))
        args = [a, arr]
        aliases = {}
        if outs is not None:
            in_specs += [pl.BlockSpec(memory_space=pl.ANY)] * nout
            args += list(outs)
            aliases = {2 + o: o for o in range(nout)}
        else:
            in_specs += [pl.BlockSpec(memory_space=pl.ANY)] * len(after)
            args += list(after)
        if tile_major:
            out_spec = pl.BlockSpec((tn // LANE, tm, LANE), lambda i, j, k, cb=cb: (cb + j, i, 0))
            shape = (total // LANE, M, LANE)
        else:
            out_spec = pl.BlockSpec((tm, tn), lambda i, j, k, cb=cb: (i, cb + j))
            shape = (M, total)
        outs = pl.pallas_call(
            body, name=f"{name}_{idx}", grid=(M // tm, n_p // tn, nk),
            in_specs=in_specs, out_specs=[out_spec] * nout,
            out_shape=[jax.ShapeDtypeStruct(shape, dt) for dt in (F32, BF16)[:nout]],
            scratch_shapes=[pltpu.VMEM((tm, tn), F32)],
            input_output_aliases=aliases,
            compiler_params=_cparams(),
        )(*args)
    return tuple(outs) if also_bf16 else outs[0]


def _log_sigmoid(z):
    e = jnp.exp(-jnp.abs(z))
    return jnp.minimum(z, 0.0) - jnp.where(e < 1e-4, e * (1.0 - 0.5 * e), jnp.log(1.0 + e))


def _tri_sum(tri, x):
    hi, mid, lo = _split3(x)
    return _dot_nn(tri, hi) + _dot_nn(tri, mid) + _dot_nn(tri, lo)


def _fox_gate_fwd(fl, bias):
    S = fl.shape[0]

    nb_ = _row_tile(S, ROW_TILE)

    def body(f_ref, b_ref, c_ref):
        ri = lax.broadcasted_iota(jnp.int32, (nb_, nb_), 0)
        ci = lax.broadcasted_iota(jnp.int32, (nb_, nb_), 1)
        tri = jnp.where(ri >= ci, 1.0, 0.0).astype(BF16)
        row = lax.broadcasted_iota(jnp.int32, (nb_, LANE), 0)

        def step(i, carry):
            r0 = pl.multiple_of(i * nb_, nb_)
            t = _tri_sum(tri, _log_sigmoid(f_ref[pl.ds(r0, nb_), :] + b_ref[...])) + carry
            c_ref[pl.ds(r0, nb_), :] = t
            return jnp.sum(jnp.where(row == nb_ - 1, t, 0.0), axis=0, keepdims=True)

        lax.fori_loop(0, S // nb_, step, jnp.zeros((1, LANE), F32))

    vm = pl.BlockSpec(memory_space=pltpu.VMEM)
    return pl.pallas_call(
        body, name="fox_gate_fwd", in_specs=[vm, vm], out_specs=vm,
        out_shape=jax.ShapeDtypeStruct((S, LANE), F32),
        compiler_params=_cparams(),
    )(fl, bias)


def _fox_gate_bwd(fl, bias, dc):
    S = fl.shape[0]

    nb_ = _row_tile(S, ROW_TILE)

    def body(f_ref, b_ref, d_ref, o_ref, db_ref):
        ri = lax.broadcasted_iota(jnp.int32, (nb_, nb_), 0)
        ci = lax.broadcasted_iota(jnp.int32, (nb_, nb_), 1)
        tri = jnp.where(ri <= ci, 1.0, 0.0).astype(BF16)
        row = lax.broadcasted_iota(jnp.int32, (nb_, LANE), 0)
        nt = S // nb_

        def step(ii, carry):
            carry_c, carry_b = carry
            r0 = pl.multiple_of((nt - 1 - ii) * nb_, nb_)
            t = _tri_sum(tri, d_ref[pl.ds(r0, nb_), :]) + carry_c
            dz = t * _sigmoid(-(f_ref[pl.ds(r0, nb_), :] + b_ref[...]))
            o_ref[pl.ds(r0, nb_), :] = dz.astype(BF16)
            first = jnp.sum(jnp.where(row == 0, t, 0.0), axis=0, keepdims=True)
            return first, carry_b + jnp.sum(dz, axis=0, keepdims=True)

        zero = jnp.zeros((1, LANE), F32)
        _, db = lax.fori_loop(0, nt, step, (zero, zero))
        db_ref[...] = db

    vm = pl.BlockSpec(memory_space=pltpu.VMEM)
    return pl.pallas_call(
        body, name="fox_gate_bwd", in_specs=[vm, vm, vm], out_specs=[vm, vm],
        out_shape=[jax.ShapeDtypeStruct((S, LANE), BF16), jax.ShapeDtypeStruct((1, LANE), F32)],
        compiler_params=_cparams(),
    )(fl, bias, dc)


def _bias_lanes(col, lane, e, first):
    o0 = HEAD_DIM * (1 - e)
    hi, mid, lo = _split3(col)
    d0 = o0 if first else o0 + 3
    t = jnp.where((lane >= o0) & (lane < o0 + 6), jnp.ones(lane.shape, BF16), jnp.zeros(lane.shape, BF16))
    t = jnp.where(lane == d0, hi, t)
    t = jnp.where(lane == d0 + 1, mid, t)
    return jnp.where(lane == d0 + 2, lo, t)


def _fox_fwd(qkvg, c, H, gather=()):
    na = len(gather)
    S = qkvg.shape[0]
    W = H * HEAD_DIM
    HP = H // 2
    PP = 2 if HP % 2 == 0 else 1
    NE = 2 * PP
    tq = _row_tile(S, ATT_TILE)
    nq = S // tq
    wb = W // LANE
    scale = HEAD_DIM ** -0.5

    def body(*refs):
        q_ref, k_ref, v_ref, g_ref, c_ref = refs[:5]
        y_ref, o_ref, a_ref = refs[5 + na:8 + na]
        kaug_sc, vaug_sc, qaug_sc, s_sc, mb_sc, m_sc, acc_sc = refs[8 + 2 * na:15 + 2 * na]
        hp, qi = pl.program_id(0), pl.program_id(1)
        if na:
            remote, local = _direct_gather_copies(refs[5:5 + na], refs[8 + na:8 + 2 * na], *refs[15 + 2 * na:])

            @pl.when((hp == 0) & (qi == 0))
            def _():
                for cp in remote + local:
                    cp.start()
        lane = lax.broadcasted_iota(jnp.int32, (tq, LANE), 1)
        own = [lane < HEAD_DIM, lane >= HEAD_DIM]
        rows = lax.broadcasted_iota(jnp.int32, (tq, tq), 0)
        cols = lax.broadcasted_iota(jnp.int32, (tq, tq), 1)

        def bias_lanes(col, e, first):
            return _bias_lanes(col, lane, e % 2, first)

        def head_col(tile, e):
            return jnp.sum(jnp.where(lane == 2 * PP * hp + e, tile, 0.0), axis=1, keepdims=True)

        def tile_of(e):
            return slice(LANE * (e // 2), LANE * (e // 2 + 1))

        @pl.when(qi == 0)
        def _():
            def chunk(i, carry):
                r0 = pl.multiple_of(i * tq, tq)
                cb = c_ref[pl.ds(r0, tq), :]
                for e in range(NE):
                    kb, vb = k_ref[pl.ds(r0, tq), tile_of(e)], v_ref[pl.ds(r0, tq), tile_of(e)]
                    kaug_sc[e, pl.ds(r0, tq), :] = jnp.where(own[e % 2], kb, bias_lanes(-head_col(cb, e), e, False))
                    vaug_sc[e, pl.ds(r0, tq), :] = jnp.where(own[e % 2], vb, jnp.ones((tq, LANE), BF16))
                return carry
            lax.fori_loop(0, nq, chunk, 0)

        crow = c_ref[pl.ds(pl.multiple_of(qi * tq, tq), tq), :]
        ctq = [head_col(crow, e) for e in range(NE)]
        for e in range(NE):
            q = q_ref[:, tile_of(e)] * jnp.asarray(scale, BF16)
            qaug_sc[e] = jnp.where(own[e % 2], q, bias_lanes(ctq[e], e, True))
        m_sc[...] = jnp.full(m_sc.shape, NEG_INF, F32)
        acc_sc[...] = jnp.zeros(acc_sc.shape, F32)

        def scores(blk, slot, masked):
            k0 = pl.multiple_of(blk * tq, tq)
            for e in range(NE):
                s = _dot_nt(qaug_sc[e], kaug_sc[e, pl.ds(k0, tq), :])
                if masked:
                    s = jnp.where(rows >= cols, s, NEG_INF)
                s_sc[slot, e] = s
                mb_sc[slot, e] = jnp.broadcast_to(jnp.max(s, axis=1, keepdims=True), (tq, LANE))

        def accumulate(blk, slot):
            k0 = pl.multiple_of(blk * tq, tq)
            for e in range(NE):
                m_prev = m_sc[e]
                m_new = jnp.maximum(m_prev, mb_sc[slot, e])
                p = jnp.exp(s_sc[slot, e] - jnp.tile(m_new, (1, tq // LANE)))
                acc_sc[e] = jnp.exp(m_prev - m_new) * acc_sc[e] + _dot_nn(p.astype(BF16), vaug_sc[e, pl.ds(k0, tq), :])
                m_sc[e] = m_new

        def block_of(t):
            return jnp.where(t == 0, qi, t - 1)

        scores(qi, 0, True)

        def loop_body(t, carry):
            scores(t, (t + 1) % 2, False)
            accumulate(block_of(t), t % 2)
            return carry

        lax.fori_loop(0, qi, loop_body, 0)
        accumulate(block_of(qi), qi % 2)
        o_e, a_e = [], []
        for e in range(NE):
            acc = acc_sc[e]
            l = pltpu.roll(acc, HEAD_DIM, axis=1)
            o_e.append(acc / l)
            a_e.append(ctq[e] - (m_sc[e] + jnp.log(l)))
        for pp in range(PP):
            o = jnp.where(own[0], o_e[2 * pp], o_e[2 * pp + 1])
            g = g_ref[:, tile_of(2 * pp)].astype(F32)
            y_ref[:, tile_of(2 * pp)] = (o * (g * _sigmoid(g))).astype(BF16)
            o_ref[:, tile_of(2 * pp)] = o.astype(BF16)
            a_ref[pp] = jnp.where(own[0], a_e[2 * pp], a_e[2 * pp + 1])
        if na:
            @pl.when((hp == HP // PP - 1) & (qi == nq - 1))
            def _():
                _wait_all(remote, local)

    any_spec = pl.BlockSpec(memory_space=pl.ANY)
    sems = [pltpu.SemaphoreType.DMA((na, N_DEV - 1)), pltpu.SemaphoreType.DMA((na, N_DEV - 1)),
            pltpu.SemaphoreType.DMA((na,))] if na else []
    wide = PP * LANE
    outs = pl.pallas_call(
        body, name="fox_attn_fwd", grid=(HP // PP, nq),
        in_specs=[pl.BlockSpec((tq, wide), lambda h, i: (i, h)),
                  pl.BlockSpec((S, wide), lambda h, i: (0, wb // PP + h)),
                  pl.BlockSpec((S, wide), lambda h, i: (0, 2 * wb // PP + h)),
                  pl.BlockSpec((tq, wide), lambda h, i: (i, 3 * wb // PP + h)),
                  pl.BlockSpec((S, LANE), lambda h, i: (0, 0))] + [any_spec] * na,
        out_specs=[pl.BlockSpec((tq, wide), lambda h, i: (i, h)),
                   pl.BlockSpec((tq, wide), lambda h, i: (i, h)),
                   pl.BlockSpec((PP, tq, LANE), lambda h, i: (h, i, 0))] + [any_spec] * na,
        out_shape=[jax.ShapeDtypeStruct((S, W), BF16), jax.ShapeDtypeStruct((S, W), BF16),
                   jax.ShapeDtypeStruct((HP, S, LANE), F32)]
        + [jax.ShapeDtypeStruct((N_DEV,) + g.shape, g.dtype) for g in gather],
        scratch_shapes=[pltpu.VMEM((NE, S, LANE), BF16), pltpu.VMEM((NE, S, LANE), BF16),
                        pltpu.VMEM((NE, tq, LANE), BF16), pltpu.VMEM((2, NE, tq, tq), F32),
                        pltpu.VMEM((2, NE, tq, LANE), F32), pltpu.VMEM((NE, tq, LANE), F32),
                        pltpu.VMEM((NE, tq, LANE), F32)] + sems,
        compiler_params=_cparams(),
    )(qkvg, qkvg, qkvg, qkvg, c, *gather)
    return outs[0], outs[1], outs[2], list(outs[3:])


def _fox_out_bwd(dxb, wo, qkvg, o, a, H):
    S, D = dxb.shape
    W = H * HEAD_DIM
    tm, tn = _row_tile(S, EPI_TILE), _tile(W, EPI_TILE)
    npair = tn // LANE
    scale = HEAD_DIM ** -0.5

    def body(dx_ref, w_ref, q_ref, g_ref, o_ref, a_ref, qa_ref, da_ref, dg_ref):
        dy = _dot_nt(dx_ref[...], w_ref[...])
        lane = lax.broadcasted_iota(jnp.int32, (tm, LANE), 1)
        own = [lane < HEAD_DIM, lane >= HEAD_DIM]
        for p in range(npair):
            cols = slice(LANE * p, LANE * (p + 1))
            q = q_ref[:, cols] * jnp.asarray(scale, BF16)
            dyv, g, ov, at = dy[:, cols], g_ref[:, cols].astype(F32), o_ref[:, cols].astype(F32), a_ref[p]
            sg = _sigmoid(g)
            dob = (dyv * (g * sg)).astype(BF16)
            dg_ref[:, cols] = (dyv * ov * (sg * (1.0 + g * (1.0 - sg)))).astype(BF16)
            prod = dob.astype(F32) * ov
            for e in range(2):
                a_col = jnp.max(jnp.where(own[e], at, -jnp.inf), axis=1, keepdims=True)
                d_col = jnp.sum(jnp.where(own[e], prod, 0.0), axis=1, keepdims=True)
                qa_ref[e, :, cols] = jnp.where(own[e], q, _bias_lanes(a_col, lane, e, True))
                da_ref[e, :, cols] = jnp.where(own[e], dob, _bias_lanes(-d_col, lane, e, True))

    blk = pl.BlockSpec((tm, tn), lambda i, j: (i, j))
    pair = pl.BlockSpec((2, tm, tn), lambda i, j: (0, i, j))
    return pl.pallas_call(
        body, name="fox_out_bwd", grid=(S // tm, W // tn),
        in_specs=[pl.BlockSpec((tm, D), lambda i, j: (i, 0)), pl.BlockSpec((tn, D), lambda i, j: (j, 0)),
                  blk, pl.BlockSpec((tm, tn), lambda i, j: (i, 3 * W // tn + j)), blk,
                  pl.BlockSpec((npair, tm, LANE), lambda i, j: (j, i, 0))],
        out_specs=[pair, pair, pl.BlockSpec((None, tm, tn), lambda i, j: (3, i, j))],
        out_shape=[jax.ShapeDtypeStruct((2, S, W), BF16), jax.ShapeDtypeStruct((2, S, W), BF16),
                   jax.ShapeDtypeStruct((4, S, W), BF16)],
        compiler_params=_cparams(),
    )(dxb, wo, qkvg, qkvg, o, a)


def _fox_bwd(qaug, doaug, qkv, c, dqkvg, H, scatter=(), scatter_specs=()):
    na = len(scatter)
    S = qkv.shape[0]
    W = H * HEAD_DIM
    HP = H // 2
    tq = _row_tile(S, ATT_TILE)
    nq = S // tq
    wb = W // LANE
    scale = HEAD_DIM ** -0.5

    def body(*refs):
        qa_ref, da_ref, k_ref, v_ref, c_ref = refs[:5]
        out_ref, dcr_ref, dcc_ref = refs[6 + na:9 + na]
        dq_sc, dk_sc, dv_sc = refs[9 + 2 * na:12 + 2 * na]
        hp, kj = pl.program_id(0), pl.program_id(1)
        if na:
            remote = _direct_scatter_copies(refs[5:5 + na], refs[9 + na:9 + 2 * na], scatter_specs,
                                            *refs[12 + 2 * na:])

            @pl.when((hp == 0) & (kj == 0))
            def _():
                for cp in remote:
                    cp.start()
        lane = lax.broadcasted_iota(jnp.int32, (tq, LANE), 1)
        own = [lane < HEAD_DIM, lane >= HEAD_DIM]
        rows = lax.broadcasted_iota(jnp.int32, (tq, tq), 0)
        cols = lax.broadcasted_iota(jnp.int32, (tq, tq), 1)

        @pl.when(kj == 0)
        def _():
            dq_sc[...] = jnp.zeros(dq_sc.shape, F32)

        @pl.when((kj == 0) & (hp == 0))
        def _():
            dcr_ref[...] = jnp.zeros(dcr_ref.shape, F32)
            dcc_ref[...] = jnp.zeros(dcc_ref.shape, F32)

        kblk, vblk, cblk = k_ref[...], v_ref[...], c_ref[...]
        one, zero = jnp.ones((tq, LANE), BF16), jnp.zeros((tq, LANE), BF16)
        ka, va = [], []
        for e in range(2):
            o0 = HEAD_DIM * (1 - e)
            c_col = jnp.sum(jnp.where(lane == 2 * hp + e, cblk, 0.0), axis=1, keepdims=True)
            ka.append(jnp.where(own[e], kblk, _bias_lanes(-c_col, lane, e, False)))
            va.append(jnp.where(own[e], vblk, jnp.where((lane >= o0) & (lane < o0 + 3), one, zero)))
        dk_sc[...] = jnp.zeros(dk_sc.shape, F32)
        dv_sc[...] = jnp.zeros(dv_sc.shape, F32)

        def step(i, masked):
            r0 = pl.multiple_of(i * tq, tq)
            for e in range(2):
                qa = qa_ref[e, pl.ds(r0, tq), :]
                da = da_ref[e, pl.ds(r0, tq), :]
                p = jnp.exp(_dot_nt(qa, ka[e]))
                if masked:
                    p = jnp.where(rows >= cols, p, 0.0)
                ds = p * _dot_nt(da, va[e])
                pb, dsb = p.astype(BF16), ds.astype(BF16)
                dv_sc[e] += _dot_tn(pb, da)
                dk_sc[e] += _dot_tn(dsb, qa)
                dq_sc[e, pl.ds(r0, tq), :] += _dot_nn(dsb, ka[e])

        step(kj, True)

        def loop_body(i, carry):
            step(i, False)
            return carry

        lax.fori_loop(kj + 1, nq, loop_body, 0)
        k0 = pl.multiple_of(kj * tq, tq)
        out_ref[1, pl.ds(k0, tq), :] = jnp.where(own[0], dk_sc[0], dk_sc[1]).astype(BF16)
        out_ref[2, pl.ds(k0, tq), :] = jnp.where(own[0], dv_sc[0], dv_sc[1]).astype(BF16)

        def put_lane(ref, r0, e, tile, src_lane):
            col = jnp.sum(jnp.where(lane == src_lane, tile, 0.0), axis=1, keepdims=True)
            ref[pl.ds(r0, tq), :] = jnp.where(lane == 2 * hp + e, col, ref[pl.ds(r0, tq), :])

        for e in range(2):
            put_lane(dcc_ref, k0, e, dk_sc[e], HEAD_DIM * (1 - e) + 3)

        @pl.when(kj == nq - 1)
        def _():
            def chunk(i, carry):
                r0 = pl.multiple_of(i * tq, tq)
                d0, d1 = dq_sc[0, pl.ds(r0, tq), :], dq_sc[1, pl.ds(r0, tq), :]
                out_ref[0, pl.ds(r0, tq), :] = (jnp.where(own[0], d0, d1) * scale).astype(BF16)
                put_lane(dcr_ref, r0, 0, d0, HEAD_DIM)
                put_lane(dcr_ref, r0, 1, d1, 0)
                return carry
            lax.fori_loop(0, nq, chunk, 0)

        if na:
            @pl.when((hp == HP - 1) & (kj == nq - 1))
            def _():
                _wait_all(remote)

    pair = pl.BlockSpec((2, S, LANE), lambda h, j: (0, 0, h))
    vec = pl.BlockSpec((S, LANE), lambda h, j: (0, 0))
    any_spec = pl.BlockSpec(memory_space=pl.ANY)
    sems = [pltpu.SemaphoreType.DMA((na, N_DEV - 1)), pltpu.SemaphoreType.DMA((na, N_DEV - 1))] if na else []
    outs = pl.pallas_call(
        body, name="fox_attn_bwd", grid=(HP, nq),
        in_specs=[pair, pair,
                  pl.BlockSpec((tq, LANE), lambda h, j: (j, wb + h)),
                  pl.BlockSpec((tq, LANE), lambda h, j: (j, 2 * wb + h)),
                  pl.BlockSpec((tq, LANE), lambda h, j: (j, 0))] + [any_spec] * (na + 1),
        out_specs=[pl.BlockSpec((3, S, LANE), lambda h, j: (0, 0, h)), vec, vec] + [any_spec] * na,
        out_shape=[jax.ShapeDtypeStruct(dqkvg.shape, BF16), jax.ShapeDtypeStruct((S, LANE), F32),
                   jax.ShapeDtypeStruct((S, LANE), F32)]
        + [jax.ShapeDtypeStruct((N_DEV - 1,) + _scatter_block_shape(g, s), g.dtype)
           for g, s in zip(scatter, scatter_specs)],
        scratch_shapes=[pltpu.VMEM((2, S, LANE), F32), pltpu.VMEM((2, tq, LANE), F32),
                        pltpu.VMEM((2, tq, LANE), F32)] + sems,
        input_output_aliases={5 + na: 0},
        compiler_params=_cparams(),
    )(qaug, doaug, qkv, qkv, c, *scatter, dqkvg)
    return outs[0], outs[1], outs[2], list(outs[3:])


def _swa_pick(blk, half, lane):
    b = blk.astype(F32)
    r = pltpu.roll(b, HEAD_DIM, axis=1)
    return jnp.where(jnp.logical_xor(lane < HEAD_DIM, half == 1), b, r).astype(BF16)


def _swa_stack(t, lane, G):
    pieces = []
    z = jnp.zeros((SWA_BLOCK, LANE), t.dtype)
    for j in range(G // 2):
        tile = t[:, LANE * j:LANE * (j + 1)]
        pieces += [jnp.where(lane < HEAD_DIM, tile, z), jnp.where(lane < HEAD_DIM, z, tile)]
    return jnp.concatenate(pieces, axis=0)


def _swa_unstack(st, lane, G):
    tiles = []
    for j in range(G // 2):
        a = st[2 * j * SWA_BLOCK:(2 * j + 1) * SWA_BLOCK]
        b = st[(2 * j + 1) * SWA_BLOCK:(2 * j + 2) * SWA_BLOCK]
        tiles.append(jnp.where(lane < HEAD_DIM, a, b))
    return jnp.concatenate(tiles, axis=1)


def _swa_mask_bias(G):
    R = G * SWA_BLOCK
    t_loc = jnp.arange(R)[:, None] % SWA_BLOCK
    j_loc = jnp.arange(2 * SWA_BLOCK)[None, :]
    diff = t_loc + SWA_BLOCK - j_loc
    band = (diff >= 0) & (diff < SWA_BLOCK)
    return jnp.stack([jnp.where(band & (j_loc >= SWA_BLOCK), 0.0, NEG_INF),
                      jnp.where(band, 0.0, NEG_INF)]).astype(F32)


def _swa_scores(q, kp, kc, vp, vc, srow, bias, half, head0, G):
    lane = lax.broadcasted_iota(jnp.int32, (SWA_BLOCK, LANE), 1)
    kk = jnp.concatenate([_swa_pick(kp, half, lane), _swa_pick(kc, half, lane)], axis=0)
    vv = jnp.concatenate([_swa_pick(vp, half, lane), _swa_pick(vc, half, lane)], axis=0)
    qstack = _swa_stack(q, lane, G) * jnp.asarray(HEAD_DIM ** -0.5, BF16)
    s = _dot_nt(qstack, kk) + bias
    R = G * SWA_BLOCK
    lane1 = lax.broadcasted_iota(jnp.int32, (1, LANE), 1)
    sink = jnp.concatenate(
        [jnp.broadcast_to(jnp.sum(jnp.where(lane1 == head0 + g, srow, 0.0), axis=1, keepdims=True), (SWA_BLOCK, LANE))
         for g in range(G)], axis=0)
    m = jnp.maximum(jnp.broadcast_to(jnp.max(s, axis=1, keepdims=True), (R, LANE)), sink)
    e = jnp.exp(s - jnp.tile(m, (1, 2)))
    es = jnp.exp(sink - m)
    inv = 1.0 / (jnp.broadcast_to(jnp.sum(e, axis=1, keepdims=True), (R, LANE)) + es)
    return qstack, kk, vv, e * jnp.tile(inv, (1, 2)), es * inv, lane


def _swa_fwd(q, kv, gate, sinks, mask_bias, HQ, HKV):
    S = q.shape[0]
    G = HQ // HKV
    WQ, KVW = HQ * HEAD_DIM, HKV * HEAD_DIM
    nb = S // SWA_BLOCK
    GW = G * HEAD_DIM
    kb, vb = 0, KVW // LANE
    NH = min(HKV, 4)
    NP = NH // 2

    def body(q_ref, kp_ref, kc_ref, vp_ref, vc_ref, g_ref, sink_ref, b_ref, y_ref, o_ref):
        grp = pl.program_id(0)
        for hh in range(NH):
            cols, kt = slice(GW * hh, GW * (hh + 1)), slice(LANE * (hh // 2), LANE * (hh // 2 + 1))
            _, _, vv, p, _, lane = _swa_scores(q_ref[:, cols], kp_ref[:, kt], kc_ref[:, kt], vp_ref[:, kt], vc_ref[:, kt],
                                               sink_ref[...], b_ref[0], hh % 2, (NH * grp + hh) * G, G)
            o = _swa_unstack(_dot_nn(p.astype(BF16), vv), lane, G)
            g = g_ref[:, cols].astype(F32)
            y_ref[:, cols] = (o * (g * _sigmoid(g))).astype(BF16)
            o_ref[:, cols] = o.astype(BF16)

    blk = lambda cb, prev: pl.BlockSpec(
        (SWA_BLOCK, NP * LANE), lambda h, n, cb=cb, prev=prev: (jnp.maximum(n - prev, 0), cb // NP + h))
    qspec = pl.BlockSpec((SWA_BLOCK, NH * GW), lambda h, n: (n, h))
    return pl.pallas_call(
        body, name="swa_attn_fwd", grid=(HKV // NH, nb),
        in_specs=[qspec, blk(kb, 1), blk(kb, 0), blk(vb, 1), blk(vb, 0), qspec,
                  pl.BlockSpec((1, LANE), lambda h, n: (0, 0)),
                  pl.BlockSpec((1, G * SWA_BLOCK, 2 * SWA_BLOCK), lambda h, n: (jnp.minimum(n, 1), 0, 0))],
        out_specs=[qspec, qspec],
        out_shape=[jax.ShapeDtypeStruct((S, WQ), BF16), jax.ShapeDtypeStruct((S, WQ), BF16)],
        compiler_params=_cparams(),
    )(q, kv, kv, kv, kv, gate, sinks, mask_bias)


def _swa_bwd(q, kv, dy, gate, o, sinks, tables, mask_bias, HQ, HKV):
    S = q.shape[0]
    G = HQ // HKV
    WQ, KVW = HQ * HEAD_DIM, HKV * HEAD_DIM
    nb = S // SWA_BLOCK
    GW = G * HEAD_DIM
    R = G * SWA_BLOCK
    kb, vb = 0, KVW // LANE
    scale = HEAD_DIM ** -0.5
    NH = min(HKV, 4)
    NP = NH // 2
    assert G == 8

    def body(q_ref, kp_ref, kc_ref, vp_ref, vc_ref, dy_ref, g_ref, o_ref, sink_ref, t_ref, b_ref,
             dqg_ref, dkv_ref, dsink_ref, carry_sc):
        grp, n = pl.program_id(0), pl.program_id(1)

        @pl.when(n == 0)
        def _():
            carry_sc[...] = jnp.zeros(carry_sc.shape, F32)
            dsink_ref[...] = jnp.zeros(dsink_ref.shape, F32)

        @pl.when(n < nb)
        def _():
            t0, t1, t2 = (jnp.tile(t_ref[i], (1, GW // LANE)) for i in range(3))
            for hh in range(NH):
                cols, kt = slice(GW * hh, GW * (hh + 1)), slice(LANE * (hh // 2), LANE * (hh // 2 + 1))
                qstack, kk, vv, p, psink, lane = _swa_scores(
                    q_ref[:, cols], kp_ref[:, kt], kc_ref[:, kt], vp_ref[:, kt], vc_ref[:, kt], sink_ref[...], b_ref[0],
                    hh % 2, (NH * grp + hh) * G, G)
                dyv, g, ov = dy_ref[:, cols], g_ref[:, cols].astype(F32), o_ref[:, cols].astype(F32)
                sg = _sigmoid(g)
                dob = (dyv * (g * sg)).astype(BF16)
                dqg_ref[1, :, cols] = (dyv * ov * (sg * (1.0 + g * (1.0 - sg)))).astype(BF16)
                prod = dob.astype(F32) * ov
                dparts = []
                for j in range(G // 2):
                    tile = prod[:, LANE * j:LANE * (j + 1)]
                    for sel in (jnp.where(lane < HEAD_DIM, tile, 0.0), jnp.where(lane < HEAD_DIM, 0.0, tile)):
                        dparts.append(jnp.broadcast_to(jnp.sum(sel, axis=1, keepdims=True), (SWA_BLOCK, LANE)))
                delta = jnp.concatenate(dparts, axis=0)
                dostack = _swa_stack(dob, lane, G)
                ds = p * (_dot_nt(dostack, vv) - jnp.tile(delta, (1, 2)))
                dsb, pb = ds.astype(BF16), p.astype(BF16)
                dq = _swa_unstack(_dot_nn(dsb, kk), lane, G) * scale
                dq = dq * t0 + pltpu.roll(dq * t1, ROT_DIM // 2, axis=1) + pltpu.roll(dq * t2, GW - ROT_DIM // 2, axis=1)
                dqg_ref[0, :, cols] = dq.astype(BF16)
                dkk = _dot_tn(dsb, qstack)
                dvv = _dot_tn(pb, dostack)
                dkk = dkk + pltpu.roll(dkk, HEAD_DIM, axis=1)
                dvv = dvv + pltpu.roll(dvv, HEAD_DIM, axis=1)
                lane2 = lax.broadcasted_iota(jnp.int32, (2 * SWA_BLOCK, LANE), 1)
                comb = jnp.where(lane2 < HEAD_DIM, dkk, dvv)
                dkv_ref[hh] = carry_sc[hh] + comb[:SWA_BLOCK]
                carry_sc[hh] = comb[SWA_BLOCK:]
                sk = psink * delta
                rows = [-jnp.sum(sk[g_ * SWA_BLOCK:(g_ + 1) * SWA_BLOCK], axis=0, keepdims=True) for g_ in range(G)]
                dsink_ref[hh] += jnp.concatenate(rows, axis=0)

        @pl.when(n == nb)
        def _():
            dkv_ref[...] = carry_sc[...]

    cl = lambda n: jnp.minimum(n, nb - 1)
    blk = lambda cb, prev: pl.BlockSpec(
        (SWA_BLOCK, NP * LANE), lambda h, n, cb=cb, prev=prev: (jnp.maximum(cl(n) - prev, 0), cb // NP + h))
    qspec = pl.BlockSpec((SWA_BLOCK, NH * GW), lambda h, n: (cl(n), h))
    return pl.pallas_call(
        body, name="swa_attn_bwd", grid=(HKV // NH, nb + 1),
        in_specs=[qspec, blk(kb, 1), blk(kb, 0), blk(vb, 1), blk(vb, 0), qspec, qspec, qspec,
                  pl.BlockSpec((1, LANE), lambda h, n: (0, 0)),
                  pl.BlockSpec((3, SWA_BLOCK, LANE), lambda h, n: (0, cl(n), 0)),
                  pl.BlockSpec((1, R, 2 * SWA_BLOCK), lambda h, n: (jnp.minimum(n, 1), 0, 0))],
        out_specs=[pl.BlockSpec((2, SWA_BLOCK, NH * GW), lambda h, n: (0, cl(n), h)),
                   pl.BlockSpec((NH, SWA_BLOCK, LANE), lambda h, n: (h, jnp.maximum(n - 1, 0), 0)),
                   pl.BlockSpec((NH, 8, LANE), lambda h, n: (h, 0, 0))],
        out_shape=[jax.ShapeDtypeStruct((2, S, WQ), BF16), jax.ShapeDtypeStruct((HKV, S, LANE), F32),
                   jax.ShapeDtypeStruct((HKV, 8, LANE), F32)],
        scratch_shapes=[pltpu.VMEM((NH, SWA_BLOCK, LANE), F32)],
        compiler_params=_cparams(),
    )(q, kv, kv, kv, kv, dy, gate, o, sinks, tables, mask_bias)


def _swa_dkv_finish(dkv, tables):
    HKV, S, _ = dkv.shape
    KVW = HKV * HEAD_DIM
    tm = _row_tile(S, EPI_TILE)
    npair = HKV // 2

    def body(d_ref, t_ref, o_ref):
        lane = lax.broadcasted_iota(jnp.int32, (tm, LANE), 1)
        lo = lane < HEAD_DIM
        for p in range(npair):
            a, b = d_ref[2 * p], d_ref[2 * p + 1]
            tk = jnp.where(lo, a, pltpu.roll(b, HEAD_DIM, axis=1))
            tv = jnp.where(lo, pltpu.roll(a, HEAD_DIM, axis=1), b)
            tk = (tk * t_ref[0] + pltpu.roll(tk * t_ref[1], ROT_DIM // 2, axis=1)
                  + pltpu.roll(tk * t_ref[2], LANE - ROT_DIM // 2, axis=1))
            o_ref[:, LANE * p:LANE * (p + 1)] = tk.astype(BF16)
            o_ref[:, KVW + LANE * p:KVW + LANE * (p + 1)] = tv.astype(BF16)

    return pl.pallas_call(
        body, name="swa_dkv_finish", grid=(S // tm,),
        in_specs=[pl.BlockSpec((HKV, tm, LANE), lambda i: (0, i, 0)), pl.BlockSpec((3, tm, LANE), lambda i: (0, i, 0))],
        out_specs=pl.BlockSpec((tm, 2 * KVW), lambda i: (i, 0)),
        out_shape=jax.ShapeDtypeStruct((S, 2 * KVW), BF16),
        compiler_params=_cparams(),
    )(dkv, tables)


def _rope_tables(S, width):
    half = ROT_DIM // 2
    pos = jnp.arange(S, dtype=F32)
    inv_freq = ROPE_THETA ** (-jnp.arange(half, dtype=F32) / half)
    ang = pos[:, None] * inv_freq[None, :]
    cos, sin = jnp.cos(ang), jnp.sin(ang)
    one = jnp.ones((S, HEAD_DIM - ROT_DIM), F32)
    zero = jnp.zeros((S, HEAD_DIM - ROT_DIM), F32)
    zh = jnp.zeros((S, half), F32)
    t0 = jnp.concatenate([cos, cos, one], axis=1)
    t1 = jnp.concatenate([-sin, zh, zero], axis=1)
    t2 = jnp.concatenate([zh, sin, zero], axis=1)
    return jnp.stack([jnp.tile(t, (1, width // HEAD_DIM)) for t in (t0, t1, t2)])


def _pad_rows(v, row, total_rows=8):
    return jnp.pad(v, ((row, total_rows - row - v.shape[0]), (0, 0)))


def _pad_lanes(v, off, width):
    return jnp.pad(v, ((0, 0), (off, width - off - v.shape[1])))


def kernel(x, norm_g, fox_w_in, fox_b_f, fox_w_out, swa_w_in, swa_sinks, swa_w_out, final_g, loss_target, m_norm_g, m_fox_w_in, m_fox_b_f, m_fox_w_out, m_swa_w_in, m_swa_sinks, m_swa_w_out, m_final_g, v_norm_g, v_fox_w_in, v_fox_b_f, v_fox_w_out, v_swa_w_in, v_swa_sinks, v_swa_w_out, v_final_g):
    S, D = x.shape[1], x.shape[2]
    H = fox_b_f.shape[1]
    W = H * HEAD_DIM
    wf = fox_w_in.shape[2]
    ws = swa_w_in.shape[2]
    HQ = swa_sinks.shape[1]
    WQ = HQ * HEAD_DIM
    KVW = (ws * N_DEV - 2 * WQ) // 2
    HKV = KVW // HEAD_DIM
    rows_o = fox_w_out.shape[1]
    assert wf * N_DEV == 4 * W + H and rows_o * N_DEV == W and H <= LANE and HQ <= LANE
    me = _my_index()

    _, sw_f, np_f = _slab_geom(wf)
    _, sw_s, np_s = _slab_geom(ws)

    def slab(w2d, w, sw):
        return jnp.pad(w2d.astype(BF16), ((0, 0), (0, sw - w)))

    (fi_all,) = _all_gather([slab(fox_w_in[0], wf, sw_f)])
    w_fi = _assemble(fi_all, wf)
    later = [slab(swa_w_in[0], ws, sw_s), fox_w_out[0].astype(BF16), swa_w_out[0].astype(BF16)]

    x0 = x[0]
    g0, g1, gf = norm_g[0:1], norm_g[1:2], final_g[None, :]
    bias = _pad_lanes(fox_b_f, 0, LANE)
    sinks = _pad_lanes(swa_sinks, 0, LANE)
    tab_k = _rope_tables(S, LANE)
    mask_bias = _swa_mask_bias(HQ // HKV)

    h0 = _rmsnorm_fwd(x0, g0, "rmsnorm0")
    qkv0 = _proj(h0, w_fi, 0, 4 * W, BF16, "fox_in_qkvg")
    fl = _proj(h0, w_fi, 4 * W, LANE, F32, "fox_in_f")
    c = _fox_gate_fwd(fl, bias)
    y0, o0, a0, (si_all, fo_all, so_all) = _fox_fwd(qkv0, c, H, gather=later)
    w_si = _assemble(si_all, ws)
    w_fo = fo_all.reshape(W, D)
    w_so = so_all.reshape(WQ, D)
    x1, h1 = _out_proj_norm(y0, w_fo, x0, g1, "fox_out")

    q1 = _proj(h1, w_si, 0, WQ, BF16, "swa_in_q", rope=(tab_k, WQ))
    kv1 = _proj(h1, w_si, WQ, 2 * KVW, BF16, "swa_in_kv", rope=(tab_k, KVW))
    gate1 = _proj(h1, w_si, WQ + 2 * KVW, WQ, BF16, "swa_in_gate")
    y1, o1 = _swa_fwd(q1, kv1, gate1, sinks, mask_bias, HQ, HKV)
    dx2, dx2b, dgf, loss_p = _out_proj_loss(y1, w_so, x1, loss_target[0], gf, "swa_out_loss")

    dy1 = _matmul_nt([(dx2b, None, 0)], w_so, WQ, "swa_out_bwd")
    g_so, g_so_h = _matmul_tn(y1, [(dx2b, None, 0)], D, "swa_out_wgrad", also_bf16=True)
    dqg1, dkv1, dsink = _swa_bwd(q1, kv1, dy1, gate1, o1, sinks, tab_k, mask_bias, HQ, HKV)
    dkv1f = _swa_dkv_finish(dkv1, tab_k)
    parts1 = [(dqg1, 0, 0), (dkv1f, None, WQ), (dqg1, 1, WQ + 2 * KVW)]
    g_si, g_si_h = _matmul_tn(h1, parts1, np_s, "swa_in_wgrad", tile_major=True, also_bf16=True)
    dh1 = _matmul_nt(parts1, w_si, D, "swa_in_bwd")
    dx1, dx1b, dg1 = _rmsnorm_bwd(dh1, x1, g1, dx2, "rmsnorm1_bwd")

    qaug0, doaug0, dqkvg0 = _fox_out_bwd(dx1b, w_fo, qkv0, o0, a0, H)
    g_fo, g_fo_h = _matmul_tn(y0, [(dx1b, None, 0)], D, "fox_out_wgrad", also_bf16=True)
    early_specs = [("col", ws), ("row", rows_o), ("row", rows_o)]
    dqkvg0, dcr, dcc, early_recv = _fox_bwd(qaug0, doaug0, qkv0, c, dqkvg0, H, scatter=[g_si_h, g_fo_h, g_so_h],
                                           scatter_specs=early_specs)
    dfl, dbf = _fox_gate_bwd(fl, bias, dcr - dcc)
    parts0 = [(dqkvg0, "stack", 0), (dfl, None, 4 * W)]
    spec_fi = ("col", wf)
    fi_halves, token = [], None
    for half in range(2):
        g_fi, g_fi_h = _matmul_tn(h0, parts0, np_f, f"fox_in_wgrad_rows{half}", tile_major=True, also_bf16=True,
                                  rows=(half * (D // 2), D // 2), after=() if token is None else (token,))
        fi_sems, fi_src, fi_land, token = _scatter_start(g_fi_h, spec_fi)
        fi_halves.append((g_fi, fi_sems, fi_src, fi_land))
    parts0[-1] = (dfl + token[0, 0].astype(BF16), None, 4 * W)
    dh0 = _matmul_nt(parts0, w_fi, D, "fox_in_bwd")
    dx0, _, dg0 = _rmsnorm_bwd(dh0, x0, g0, dx1, "rmsnorm0_bwd")

    red_si, gw_fo, gw_so = [_final_sum8(g_, r_, s_)
                            for g_, r_, s_ in zip([g_si, g_fo, g_so], early_recv, early_specs)]
    gt_si = lax.dynamic_slice(red_si, ((ws * me) % LANE, 0), (ws, D))

    def t_in(p):
        return jnp.swapaxes(p[0], 0, 1)

    def t_out(t):
        return jnp.swapaxes(t, 0, 1)[None]

    P = D
    dsink_v = dsink[:, :, 0].reshape(1, HQ)
    row3 = _pad_lanes(dbf[:, :H], 0, P) + _pad_lanes(dsink_v, LANE, P) + _pad_lanes(loss_p[:, :1], 2 * LANE, P)
    pack = _pad_rows(dg0, 0) + _pad_rows(dg1, 1) + _pad_rows(dgf, 2) + _pad_rows(row3, 3)

    d_fo, m_fo, v_fo = _adamw(fox_w_out[0], gw_fo, m_fox_w_out[0], v_fox_w_out[0], "adamw_fox_out")
    d_si, m_si, v_si = _adamw(t_in(swa_w_in), gt_si, t_in(m_swa_w_in), t_in(v_swa_w_in), "adamw_swa_in")
    d_so, m_so, v_so = _adamw(swa_w_out[0], gw_so, m_swa_w_out[0], v_swa_w_out[0], "adamw_swa_out")
    behind, red_fi = [dx0, pack, d_fo, d_si, d_so], []
    for g_fi, fi_sems, fi_src, fi_land in fi_halves:
        recv_fi = _scatter_wait(fi_sems, fi_src, fi_land, spec_fi, after=behind)
        red_fi.append(_final_sum8(g_fi, recv_fi, spec_fi))
        behind = [recv_fi]
    red_fi = jnp.concatenate(red_fi, axis=1)
    gt_fi = lax.dynamic_slice(red_fi, ((wf * me) % LANE, 0), (wf, D))

    d_fi, m_fi, v_fi = _adamw(t_in(fox_w_in), gt_fi, t_in(m_fox_w_in), t_in(v_fox_w_in), "adamw_fox_in")
    gw_si, d_si, m_si, v_si = [t_out(t)[0] for t in (gt_si, d_si, m_si, v_si)]
    gw_fi, d_fi, m_fi, v_fi = [t_out(t)[0] for t in (gt_fi, d_fi, m_fi, v_fi)]

    tot = _all_reduce_small(pack, after=recv_fi)
    loss = tot[3, 2 * LANE]
    g_norm = tot[0:2]
    g_final = tot[2]
    g_bf = tot[3:4, 0:H]
    g_sinks = tot[3:4, LANE:LANE + HQ]

    def small_pack(ng, fg, bf, sk):
        r3 = _pad_lanes(bf, 0, P) + _pad_lanes(sk, LANE, P)
        return _pad_rows(ng, 0) + _pad_rows(fg[None, :], 2) + _pad_rows(r3, 3)

    sd, sm, sv = _adamw(small_pack(norm_g, final_g, fox_b_f, swa_sinks), tot,
                        small_pack(m_norm_g, m_final_g, m_fox_b_f, m_swa_sinks),
                        small_pack(v_norm_g, v_final_g, v_fox_b_f, v_swa_sinks), "adamw_small")

    def unpack(t):
        return t[0:2], t[3:4, 0:H], t[3:4, LANE:LANE + HQ], t[2]

    def group(small, fi, fo, si, so):
        ng, bf, sk, fg = unpack(small)
        return (ng, fi[None], bf, fo[None], si[None], sk, so[None], fg)

    grads = (g_norm, gw_fi[None], g_bf, gw_fo[None], gw_si[None], g_sinks, gw_so[None], g_final)
    return (loss, dx0[None], *grads, *group(sd, d_fi, d_fo, d_si, d_so),
            *group(sm, m_fi, m_fo, m_si, m_so), *group(sv, v_fi, v_fo, v_si, v_so))
```

```python
import math

import jax
import jax.numpy as jnp
from jax import lax
from jax.experimental import pallas as pl
from jax.experimental.pallas import tpu as pltpu

F32 = jnp.float32
BF16 = jnp.bfloat16
MESH = pl.DeviceIdType.MESH

N_DEV = 8
LANE = 128
HEAD_DIM = 64
SWA_BLOCK = 128
NEG_INF = -1e30
RMS_EPS = 1e-6
ROPE_THETA = 500000.0
ROT_DIM = HEAD_DIM // 4
ADAM_LR, ADAM_B1, ADAM_B2, ADAM_EPS, ADAM_WD, ADAM_STEP = 0.001, 0.9, 0.999, 1e-08, 0.01, 10
VMEM_LIMIT = 56 * 1024 * 1024
MM_TILE = 1024
ATT_TILE = 512
EPI_TILE = 512
ROW_TILE = 256
ADAM_TILE_ELEMS = 3 << 18


def _cparams(**kw):
    return pltpu.CompilerParams(vmem_limit_bytes=VMEM_LIMIT, **kw)


def _tile(n, cap):
    if n <= cap:
        return n
    t = (cap // LANE) * LANE
    while t > LANE and n % t:
        t -= LANE
    assert n % t == 0, (n, cap)
    return t


def _row_tile(n, cap):
    t = min(n, cap)
    while n % t:
        t //= 2
    return t


def _dot_nn(a, b):
    return jnp.dot(a, b, preferred_element_type=F32)


def _dot_nt(a, b):
    return lax.dot_general(a, b, (((1,), (1,)), ((), ())), preferred_element_type=F32)


def _dot_tn(a, b):
    return lax.dot_general(a, b, (((0,), (0,)), ((), ())), preferred_element_type=F32)


def _split3(x):
    hi = x.astype(BF16)
    r1 = x - hi.astype(F32)
    mid = r1.astype(BF16)
    return hi, mid, (r1 - mid.astype(F32)).astype(BF16)


def _sigmoid(g):
    return 1.0 / (1.0 + jnp.exp(-g))


def _slab_geom(w):
    starts = [w * i for i in range(N_DEV)]
    aligned = [LANE * (s // LANE) for s in starts]
    offs = [s - a for s, a in zip(starts, aligned)]
    sw = LANE * (-(-(max(offs) + w) // LANE))
    return aligned, sw, aligned[-1] + sw


def _my_index():
    return 4 * lax.axis_index("x") + 2 * lax.axis_index("y") + lax.axis_index("c")


def _all_gather(arrs):
    n = len(arrs)

    def body(*refs):
        ins, outs = refs[:n], refs[n:2 * n]
        send_sems, recv_sems, local_sems = refs[2 * n:]
        x, y, c = lax.axis_index("x"), lax.axis_index("y"), lax.axis_index("c")
        me, sib = (x, y, c), (x, y, 1 - c)
        chips = [(1 - x, y), (x, 1 - y), (1 - x, 1 - y)]

        def idx(px, py, pc):
            return 4 * px + 2 * py + pc

        def copy(a, k, block, to, src=None):
            dst = outs[a].at[idx(*block)]
            return pltpu.make_async_remote_copy(
                src_ref=dst if src is None else src, dst_ref=dst,
                send_sem=send_sems.at[a, k], recv_sem=recv_sems.at[a, k],
                device_id=to, device_id_type=MESH)

        mine = [pltpu.make_async_copy(ins[a], outs[a].at[idx(*me)], local_sems.at[a]) for a in range(n)]
        for m in mine:
            m.start()
        first = []
        for a in range(n):
            first.append(copy(a, 0, me, sib, src=ins[a]))
            for j, chip in enumerate(chips):
                first.append(copy(a, 1 + j, me, (*chip, c), src=ins[a]))
        for cp in first:
            cp.start()
        passed = []
        for j, chip in enumerate(chips):
            for a in range(n):
                copy(a, 1 + j, (*chip, c), me).wait_recv()
                p = copy(a, 4 + j, (*chip, c), sib)
                p.start()
                passed.append(p)
        for a in range(n):
            copy(a, 0, sib, me).wait_recv()
        for j, chip in enumerate(chips):
            for a in range(n):
                copy(a, 4 + j, (*chip, 1 - c), me).wait_recv()
        for cp in first + passed:
            cp.wait_send()
        for m in mine:
            m.wait()

    any_spec = pl.BlockSpec(memory_space=pl.ANY)
    return pl.pallas_call(
        body, name="weights_all_gather",
        out_shape=[jax.ShapeDtypeStruct((N_DEV,) + a.shape, a.dtype) for a in arrs],
        in_specs=[any_spec] * n, out_specs=[any_spec] * n,
        scratch_shapes=[pltpu.SemaphoreType.DMA((n, 7)), pltpu.SemaphoreType.DMA((n, 7)),
                        pltpu.SemaphoreType.DMA((n,))],
    )(*arrs)


def _rs_windows(specs):
    def window(ref, spec, blk):
        kind, n = spec
        if kind == "col":
            _, sw, _ = _slab_geom(n)
            return ref.at[pl.ds((n * blk) // LANE, sw // LANE)]
        start = pl.multiple_of(n * blk, n)
        return ref.at[pl.ds(start, n), :]
    return window


def _peer(k):
    x, y, c = lax.axis_index("x"), lax.axis_index("y"), lax.axis_index("c")
    return (x ^ (k >> 2), y ^ ((k >> 1) & 1), c ^ (k & 1))


def _direct_gather_copies(ins, outs, send_sems, recv_sems, local_sems):
    me = _my_index()
    remote, local = [], []
    for a, (src, dst) in enumerate(zip(ins, outs)):
        local.append(pltpu.make_async_copy(src, dst.at[me], local_sems.at[a]))
        for k in range(1, N_DEV):
            remote.append(pltpu.make_async_remote_copy(
                src_ref=src, dst_ref=dst.at[me], send_sem=send_sems.at[a, k - 1], recv_sem=recv_sems.at[a, k - 1],
                device_id=_peer(k), device_id_type=MESH))
    return remote, local


def _direct_scatter_copies(ins, outs, specs, send_sems, recv_sems):
    window = _rs_windows(specs)
    remote = []
    for a, (src, dst) in enumerate(zip(ins, outs)):
        for k in range(1, N_DEV):
            px, py, pc = _peer(k)
            remote.append(pltpu.make_async_remote_copy(
                src_ref=window(src, specs[a], 4 * px + 2 * py + pc), dst_ref=dst.at[k - 1],
                send_sem=send_sems.at[a, k - 1], recv_sem=recv_sems.at[a, k - 1],
                device_id=(px, py, pc), device_id_type=MESH))
    return remote


def _scatter_block_shape(g, spec):
    kind, w = spec
    return (_slab_geom(w)[1] // LANE, g.shape[1], LANE) if kind == "col" else (w, g.shape[1])


def _wait_all(remote, local=()):
    for cp in remote:
        cp.wait_recv()
    for cp in remote:
        cp.wait_send()
    for cp in local:
        cp.wait()


def _scatter_start(g, spec):
    blk = _scatter_block_shape(g, spec)
    window = _rs_windows([spec])
    npeer = N_DEV - 1

    def body(g_ref, land_ref, *rest):
        sems = rest[:2 * npeer]
        token = rest[2 * npeer + 2]
        for cp in _peer_block_copies(g_ref, land_ref, spec, window, sems[:npeer], sems[npeer:]):
            cp.start()
        token[...] = jnp.zeros(token.shape, token.dtype)

    hbm = pl.BlockSpec(memory_space=pltpu.HBM)
    sem = pl.BlockSpec(memory_space=pltpu.SEMAPHORE)
    land = lax.empty((npeer,) + blk, g.dtype)
    outs = pl.pallas_call(
        body, name="grads_scatter_start",
        out_shape=(pltpu.SemaphoreType.DMA(()),) * (2 * npeer)
        + (pltpu.HBM(g.shape, g.dtype), pltpu.HBM(land.shape, land.dtype), jax.ShapeDtypeStruct((8, LANE), F32)),
        in_specs=(hbm, hbm),
        out_specs=(sem,) * (2 * npeer) + (hbm, hbm, pl.BlockSpec(memory_space=pltpu.VMEM)),
        input_output_aliases={0: 2 * npeer, 1: 2 * npeer + 1},
        compiler_params=pltpu.CompilerParams(has_side_effects=pltpu.SideEffectType.DATAFLOW_SIDE_EFFECTING),
    )(pltpu.with_memory_space_constraint(g, pltpu.HBM), pltpu.with_memory_space_constraint(land, pltpu.HBM))
    return outs[:2 * npeer], outs[2 * npeer], outs[2 * npeer + 1], outs[2 * npeer + 2]


def _peer_block_copies(g_ref, land_ref, spec, window, send_sems, recv_sems):
    copies = []
    for k in range(1, N_DEV):
        px, py, pc = _peer(k)
        copies.append(pltpu.make_async_remote_copy(
            src_ref=window(g_ref, spec, 4 * px + 2 * py + pc), dst_ref=land_ref.at[k - 1],
            send_sem=send_sems[k - 1], recv_sem=recv_sems[k - 1], device_id=(px, py, pc), device_id_type=MESH))
    return copies


def _scatter_wait(sems, g_thru, land_thru, spec, after):
    window = _rs_windows([spec])
    npeer = N_DEV - 1

    def body(g_ref, land_ref, *rest):
        s = rest[:2 * npeer]
        copies = _peer_block_copies(g_ref, land_ref, spec, window, s[:npeer], s[npeer:])
        for cp in copies:
            cp.wait_send()
        for cp in copies:
            cp.wait_recv()

    hbm = pl.BlockSpec(memory_space=pltpu.HBM)
    sem = pl.BlockSpec(memory_space=pltpu.SEMAPHORE)
    return pl.pallas_call(
        body, name="grads_scatter_wait",
        out_shape=(pltpu.HBM(g_thru.shape, g_thru.dtype), pltpu.HBM(land_thru.shape, land_thru.dtype)),
        in_specs=(hbm, hbm) + (sem,) * (2 * npeer) + (pl.BlockSpec(memory_space=pl.ANY),) * len(after),
        out_specs=(hbm, hbm), input_output_aliases={0: 0, 1: 1},
        compiler_params=pltpu.CompilerParams(has_side_effects=pltpu.SideEffectType.DATAFLOW_SIDE_EFFECTING),
    )(g_thru, land_thru, *sems, *after)[1]


def _final_sum8(g, recv, spec):
    kind, n = spec
    me = _my_index()
    offs = jnp.stack([(n * me) // LANE if kind == "col" else me]).astype(jnp.int32)
    if kind == "col":
        _, T, M, _ = recv.shape
        grid = (T,)
        in_specs = [pl.BlockSpec((1, M, LANE), lambda t, o: (o[0] + t, 0, 0)),
                    pl.BlockSpec((N_DEV - 1, 1, M, LANE), lambda t, o: (0, t, 0, 0))]
        out_spec = pl.BlockSpec((LANE, M), lambda t, o: (t, 0))
        out_shape = jax.ShapeDtypeStruct((T * LANE, M), F32)
    else:
        _, nrow, C = recv.shape
        grid = (1,)
        in_specs = [pl.BlockSpec((nrow, C), lambda t, o: (o[0], 0)),
                    pl.BlockSpec((N_DEV - 1, nrow, C), lambda t, o: (0, 0, 0))]
        out_spec = pl.BlockSpec((nrow, C), lambda t, o: (0, 0))
        out_shape = jax.ShapeDtypeStruct((nrow, C), F32)

    def body(o_ref, g_ref, r_ref, out_ref):
        acc = g_ref[0] if kind == "col" else g_ref[...]
        for k in range(N_DEV - 1):
            acc = acc + (r_ref[k, 0] if kind == "col" else r_ref[k]).astype(F32)
        out_ref[...] = acc.T if kind == "col" else acc

    return pl.pallas_call(
        body, name="grads_final_sum8",
        grid_spec=pltpu.PrefetchScalarGridSpec(num_scalar_prefetch=1, grid=grid, in_specs=in_specs,
                                               out_specs=out_spec),
        out_shape=out_shape, compiler_params=_cparams(),
    )(offs, g, recv)


def _all_reduce_small(pack, after):
    R, P = pack.shape

    def body(x_ref, after_ref, o_ref, gat_ref, send_sems, recv_sems):
        x, y, c = lax.axis_index("x"), lax.axis_index("y"), lax.axis_index("c")
        me = 4 * x + 2 * y + c
        gat_ref[me] = x_ref[...]
        copies = []
        for k in range(1, N_DEV):
            peer = (x ^ (k >> 2), y ^ ((k >> 1) & 1), c ^ (k & 1))
            copies.append(pltpu.make_async_remote_copy(
                src_ref=x_ref, dst_ref=gat_ref.at[me],
                send_sem=send_sems.at[k - 1], recv_sem=recv_sems.at[k - 1],
                device_id=peer, device_id_type=MESH))
        for cp in copies:
            cp.start()
        for cp in copies:
            cp.wait_recv()
        for cp in copies:
            cp.wait_send()
        acc = gat_ref[0]
        for d in range(1, N_DEV):
            acc = acc + gat_ref[d]
        o_ref[...] = acc

    vm = pl.BlockSpec(memory_space=pltpu.VMEM)
    return pl.pallas_call(
        body, name="small_all_reduce",
        out_shape=jax.ShapeDtypeStruct((R, P), F32),
        in_specs=[vm, pl.BlockSpec(memory_space=pl.ANY)], out_specs=vm,
        scratch_shapes=[pltpu.VMEM((N_DEV, R, P), F32),
                        pltpu.SemaphoreType.DMA((N_DEV - 1,)), pltpu.SemaphoreType.DMA((N_DEV - 1,))],
    )(pack, after)


def _assemble(slabs, w):
    aligned, sw, total = _slab_geom(w)
    K = slabs.shape[1]
    tr = _row_tile(K, ROW_TILE)

    def body(s_ref, o_ref):
        o_ref[...] = jnp.zeros(o_ref.shape, BF16)
        for i in range(N_DEV):
            a, off = aligned[i], w * i - aligned[i]
            x = s_ref[i].astype(F32)
            if off:
                x = pltpu.roll(x, off, axis=1)
            o_ref[:, a:a + sw] = (o_ref[:, a:a + sw].astype(F32) + x).astype(BF16)

    return pl.pallas_call(
        body, name="assemble_w_in", grid=(K // tr,),
        in_specs=[pl.BlockSpec((N_DEV, tr, sw), lambda i: (0, i, 0))],
        out_specs=pl.BlockSpec((tr, total), lambda i: (i, 0)),
        out_shape=jax.ShapeDtypeStruct((K, total), BF16),
        compiler_params=_cparams(),
    )(slabs)


def _rmsnorm_fwd(x, g, name):
    S, D = x.shape
    tm = _row_tile(S, ROW_TILE)

    def body(x_ref, g_ref, h_ref):
        xv = x_ref[...]
        r = lax.rsqrt(jnp.mean(xv * xv, axis=-1, keepdims=True) + RMS_EPS)
        h_ref[...] = ((xv * r) * g_ref[...]).astype(BF16)

    return pl.pallas_call(
        body, name=name, grid=(S // tm,),
        in_specs=[pl.BlockSpec((tm, D), lambda i: (i, 0)), pl.BlockSpec((1, D), lambda i: (0, 0))],
        out_specs=pl.BlockSpec((tm, D), lambda i: (i, 0)),
        out_shape=jax.ShapeDtypeStruct((S, D), BF16),
        compiler_params=_cparams(),
    )(x, g)


def _rmsnorm_bwd(dh, x, g, dres, name):
    S, D = x.shape
    tm = _row_tile(S, ROW_TILE)

    def body(dh_ref, x_ref, g_ref, dr_ref, dx_ref, dxb_ref, dg_ref):
        xv = x_ref[...]
        r = lax.rsqrt(jnp.mean(xv * xv, axis=-1, keepdims=True) + RMS_EPS)
        xhat = xv * r
        d = dh_ref[...]
        gd = d * g_ref[...]
        dx = r * (gd - xhat * jnp.mean(gd * xhat, axis=-1, keepdims=True)) + dr_ref[...]
        dx_ref[...] = dx
        dxb_ref[...] = dx.astype(BF16)

        @pl.when(pl.program_id(0) == 0)
        def _():
            dg_ref[...] = jnp.zeros(dg_ref.shape, F32)
        dg_ref[...] += jnp.sum(d * xhat, axis=0, keepdims=True)

    row = pl.BlockSpec((tm, D), lambda i: (i, 0))
    vec = pl.BlockSpec((1, D), lambda i: (0, 0))
    return pl.pallas_call(
        body, name=name, grid=(S // tm,),
        in_specs=[row, row, vec, row], out_specs=[row, row, vec],
        out_shape=[jax.ShapeDtypeStruct((S, D), F32), jax.ShapeDtypeStruct((S, D), BF16),
                   jax.ShapeDtypeStruct((1, D), F32)],
        compiler_params=_cparams(),
    )(dh, x, g, dres)


def _adamw(w, g, m, v, name):
    R, C = w.shape
    steps = pl.cdiv(R * C, ADAM_TILE_ELEMS)
    tr = R if steps == 1 else pl.cdiv(pl.cdiv(R, steps), 8) * 8
    c1 = 1.0 - ADAM_B1 ** ADAM_STEP
    c2 = 1.0 - ADAM_B2 ** ADAM_STEP

    def body(w_ref, g_ref, m_ref, v_ref, d_ref, nm_ref, nv_ref):
        gv = g_ref[...]
        nm = ADAM_B1 * m_ref[...] + (1.0 - ADAM_B1) * gv
        nv = ADAM_B2 * v_ref[...] + (1.0 - ADAM_B2) * (gv * gv)
        d_ref[...] = -ADAM_LR * ((nm / c1) / (jnp.sqrt(nv / c2) + ADAM_EPS) + ADAM_WD * w_ref[...])
        nm_ref[...] = nm
        nv_ref[...] = nv

    spec = pl.BlockSpec((tr, C), lambda i: (i, 0))
    return pl.pallas_call(
        body, name=name, grid=(pl.cdiv(R, tr),),
        in_specs=[spec] * 4, out_specs=[spec] * 3,
        out_shape=[jax.ShapeDtypeStruct((R, C), F32)] * 3,
        compiler_params=_cparams(),
    )(w, g, m, v)


def _proj(h, wfull, col0, ncols, out_dtype, name, rope=None):
    S, K = h.shape
    tm = _row_tile(S, MM_TILE)
    tn = math.gcd(_tile(ncols, MM_TILE), col0) if col0 else _tile(ncols, MM_TILE)
    if rope is not None:
        tn = _tile(math.gcd(ncols, rope[1]), MM_TILE)
    assert ncols % tn == 0 and col0 % tn == 0
    cb = col0 // tn

    def body(*refs):
        if rope is None:
            a_ref, b_ref, o_ref = refs
        else:
            a_ref, b_ref, t_ref, o_ref = refs
        acc = _dot_nn(a_ref[...], b_ref[...])
        if rope is not None:
            t0, t1, t2 = (jnp.tile(t_ref[i], (1, tn // LANE)) for i in range(3))
            roped = (acc * t0 + pltpu.roll(acc, tn - ROT_DIM // 2, axis=1) * t1
                     + pltpu.roll(acc, ROT_DIM // 2, axis=1) * t2)
            acc = jnp.where(pl.program_id(1) < rope[1] // tn, roped, acc)
        o_ref[...] = acc.astype(out_dtype)

    in_specs = [pl.BlockSpec((tm, K), lambda i, j: (i, 0)), pl.BlockSpec((K, tn), lambda i, j: (0, cb + j))]
    args = [h, wfull]
    if rope is not None:
        in_specs.append(pl.BlockSpec((3, tm, LANE), lambda i, j: (0, i, 0)))
        args.append(rope[0])
    return pl.pallas_call(
        body, name=name, grid=(S // tm, ncols // tn),
        in_specs=in_specs, out_specs=pl.BlockSpec((tm, tn), lambda i, j: (i, j)),
        out_shape=jax.ShapeDtypeStruct((S, ncols), out_dtype),
        compiler_params=_cparams(),
    )(*args)


def _out_proj_norm(y, wo, xres, g, name):
    S, W = y.shape
    D = wo.shape[1]
    tm = _row_tile(S, EPI_TILE)

    def body(a_ref, b_ref, r_ref, g_ref, x_ref, h_ref):
        xv = r_ref[...] + _dot_nn(a_ref[...], b_ref[...])
        x_ref[...] = xv
        r = lax.rsqrt(jnp.mean(xv * xv, axis=-1, keepdims=True) + RMS_EPS)
        h_ref[...] = ((xv * r) * g_ref[...]).astype(BF16)

    row = pl.BlockSpec((tm, D), lambda i: (i, 0))
    return pl.pallas_call(
        body, name=name, grid=(S // tm,),
        in_specs=[pl.BlockSpec((tm, W), lambda i: (i, 0)), pl.BlockSpec((W, D), lambda i: (0, 0)), row,
                  pl.BlockSpec((1, D), lambda i: (0, 0))],
        out_specs=[row, row],
        out_shape=[jax.ShapeDtypeStruct((S, D), F32), jax.ShapeDtypeStruct((S, D), BF16)],
        compiler_params=_cparams(),
    )(y, wo, xres, g)


def _out_proj_loss(y, wo, xres, tgt, g, name):
    S, W = y.shape
    D = wo.shape[1]
    tm = _row_tile(S, EPI_TILE)

    def body(a_ref, b_ref, r_ref, t_ref, g_ref, dx_ref, dxb_ref, dg_ref, loss_ref):
        xv = r_ref[...] + _dot_nn(a_ref[...], b_ref[...])
        r = lax.rsqrt(jnp.mean(xv * xv, axis=-1, keepdims=True) + RMS_EPS)
        xhat = xv * r
        gv = g_ref[...]
        err = xhat * gv - t_ref[...]
        d = err * (1.0 / D)
        gd = d * gv
        dx = r * (gd - xhat * jnp.mean(gd * xhat, axis=-1, keepdims=True))
        dx_ref[...] = dx
        dxb_ref[...] = dx.astype(BF16)

        @pl.when(pl.program_id(0) == 0)
        def _():
            dg_ref[...] = jnp.zeros(dg_ref.shape, F32)
            loss_ref[...] = jnp.zeros(loss_ref.shape, F32)
        dg_ref[...] += jnp.sum(d * xhat, axis=0, keepdims=True)
        per_tok = jnp.sum(err * err, axis=-1, keepdims=True) * (1.0 / D)
        loss_ref[...] += 0.5 * jnp.sum(per_tok, axis=0, keepdims=True)

    row = pl.BlockSpec((tm, D), lambda i: (i, 0))
    vec = pl.BlockSpec((1, D), lambda i: (0, 0))
    return pl.pallas_call(
        body, name=name, grid=(S // tm,),
        in_specs=[pl.BlockSpec((tm, W), lambda i: (i, 0)), pl.BlockSpec((W, D), lambda i: (0, 0)), row, row, vec],
        out_specs=[row, row, vec, pl.BlockSpec((1, LANE), lambda i: (0, 0))],
        out_shape=[jax.ShapeDtypeStruct((S, D), F32), jax.ShapeDtypeStruct((S, D), BF16),
                   jax.ShapeDtypeStruct((1, D), F32), jax.ShapeDtypeStruct((1, LANE), F32)],
        compiler_params=_cparams(),
    )(y, wo, xres, tgt, g)


def _matmul_nt(parts, wfull, out_rows, name):
    S = parts[0][0].shape[-2]
    tm, tn = _row_tile(S, 2 * MM_TILE if len(parts) <= 2 else MM_TILE), _tile(out_rows, MM_TILE)
    plan, lo = [], 0
    for arr, lead, col0 in parts:
        n_p = arr.shape[-1]
        tk = math.gcd(_tile(n_p, MM_TILE), col0) if col0 else _tile(n_p, MM_TILE)
        steps = n_p // tk * (arr.shape[0] if lead == "stack" else 1)
        plan.append((lead, col0 // tk, tk, lo, lo + steps))
        lo += steps
    nk = lo
    npart = len(parts)

    def body(*refs):
        a_refs, w_refs = refs[:npart], refs[npart:2 * npart]
        o_ref, acc_ref = refs[2 * npart], refs[2 * npart + 1]
        k = pl.program_id(2)

        @pl.when(k == 0)
        def _():
            acc_ref[...] = jnp.zeros(acc_ref.shape, F32)
        for p, (_, _, _, lo_p, hi_p) in enumerate(plan):
            @pl.when((k >= lo_p) & (k < hi_p))
            def _(p=p):
                acc_ref[...] += _dot_nt(a_refs[p][...], w_refs[p][...])

        @pl.when(k == nk - 1)
        def _():
            o_ref[...] = acc_ref[...]

    in_specs, args = [], []
    for (arr, lead, col0), (_, cb, tk, lo_p, hi_p) in zip(parts, plan):
        def kk(k, lo_p=lo_p, hi_p=hi_p):
            return jnp.clip(k - lo_p, 0, hi_p - lo_p - 1)
        if lead is None:
            in_specs.append(pl.BlockSpec((tm, tk), lambda i, j, k, kk=kk: (i, kk(k))))
        elif lead == "stack":
            nkb = arr.shape[-1] // tk
            in_specs.append(pl.BlockSpec((None, tm, tk), lambda i, j, k, kk=kk, nkb=nkb: (kk(k) // nkb, i, kk(k) % nkb)))
        else:
            in_specs.append(pl.BlockSpec((None, tm, tk), lambda i, j, k, kk=kk, lead=lead: (lead, i, kk(k))))
        args.append(arr)
    for (_, cb, tk, lo_p, hi_p) in plan:
        def kk(k, lo_p=lo_p, hi_p=hi_p):
            return jnp.clip(k - lo_p, 0, hi_p - lo_p - 1)
        in_specs.append(pl.BlockSpec((tn, tk), lambda i, j, k, kk=kk, cb=cb: (j, cb + kk(k))))
        args.append(wfull)
    return pl.pallas_call(
        body, name=name, grid=(S // tm, out_rows // tn, nk),
        in_specs=in_specs, out_specs=pl.BlockSpec((tm, tn), lambda i, j, k: (i, j)),
        out_shape=jax.ShapeDtypeStruct((S, out_rows), F32),
        scratch_shapes=[pltpu.VMEM((tm, tn), F32)],
        compiler_params=_cparams(),
    )(*args)


def _matmul_tn(a, parts, total, name, tile_major=False, also_bf16=False, rows=None, after=()):
    S = a.shape[0]
    m0, M = rows or (0, a.shape[1])
    tm = _tile(M, MM_TILE)
    ib = m0 // tm
    nout = 2 if also_bf16 else 1
    outs = None
    for idx, (arr, lead, col0) in enumerate(parts):
        n_p = arr.shape[-1]
        tn = math.gcd(_tile(n_p, MM_TILE), col0) if col0 else _tile(n_p, MM_TILE)
        cb = col0 // tn
        nb = n_p // tn
        if lead == "stack":
            n_p *= arr.shape[0]

        def body(*refs, tn=tn):
            a_ref, b_ref = refs[0], refs[1]
            o_refs = refs[-nout:]
            acc = _dot_tn(a_ref[...], b_ref[...])
            for o_ref in o_refs:
                if tile_major:
                    for t in range(tn // LANE):
                        o_ref[t] = acc[:, LANE * t:LANE * (t + 1)].astype(o_ref.dtype)
                else:
                    o_ref[...] = acc.astype(o_ref.dtype)

        in_specs = [pl.BlockSpec((S, tm), lambda i, j: (0, ib + i), pipeline_mode=pl.Buffered(1))]
        if lead is None:
            in_specs.append(pl.BlockSpec((S, tn), lambda i, j: (0, j)))
        elif lead == "stack":
            in_specs.append(pl.BlockSpec((None, S, tn), lambda i, j, nb=nb: (j // nb, 0, j % nb)))
        else:
            in_specs.append(pl.BlockSpec((None, S, tn), lambda i, j, lead=lead: (lead, 0, j)))
        args = [a, arr]
        aliases = {}
        if outs is not None:
            in_specs += [pl.BlockSpec(memory_space=pl.ANY)] * nout
            args += list(outs)
            aliases = {2 + o: o for o in range(nout)}
        else:
            in_specs += [pl.BlockSpec(memory_space=pl.ANY)] * len(after)
            args += list(after)
        if tile_major:
            out_spec = pl.BlockSpec((tn // LANE, tm, LANE), lambda i, j, cb=cb: (cb + j, i, 0))
            shape = (total // LANE, M, LANE)
        else:
            out_spec = pl.BlockSpec((tm, tn), lambda i, j, cb=cb: (i, cb + j))
            shape = (M, total)
        outs = pl.pallas_call(
            body, name=f"{name}_{idx}", grid=(M // tm, n_p // tn),
            in_specs=in_specs, out_specs=[out_spec] * nout,
            out_shape=[jax.ShapeDtypeStruct(shape, dt) for dt in (F32, BF16)[:nout]],
            input_output_aliases=aliases,
            compiler_params=_cparams(),
        )(*args)
    return tuple(outs) if also_bf16 else outs[0]


def _log_sigmoid(z):
    e = jnp.exp(-jnp.abs(z))
    return jnp.minimum(z, 0.0) - jnp.where(e < 1e-4, e * (1.0 - 0.5 * e), jnp.log(1.0 + e))


def _tri_sum(tri, x):
    hi, mid, lo = _split3(x)
    return _dot_nn(tri, hi) + _dot_nn(tri, mid) + _dot_nn(tri, lo)


def _fox_gate_fwd(fl, bias):
    S = fl.shape[0]

    nb_ = _row_tile(S, ROW_TILE)

    def body(f_ref, b_ref, c_ref):
        ri = lax.broadcasted_iota(jnp.int32, (nb_, nb_), 0)
        ci = lax.broadcasted_iota(jnp.int32, (nb_, nb_), 1)
        tri = jnp.where(ri >= ci, 1.0, 0.0).astype(BF16)
        row = lax.broadcasted_iota(jnp.int32, (nb_, LANE), 0)

        def step(i, carry):
            r0 = pl.multiple_of(i * nb_, nb_)
            t = _tri_sum(tri, _log_sigmoid(f_ref[pl.ds(r0, nb_), :] + b_ref[...])) + carry
            c_ref[pl.ds(r0, nb_), :] = t
            return jnp.sum(jnp.where(row == nb_ - 1, t, 0.0), axis=0, keepdims=True)

        lax.fori_loop(0, S // nb_, step, jnp.zeros((1, LANE), F32))

    vm = pl.BlockSpec(memory_space=pltpu.VMEM)
    return pl.pallas_call(
        body, name="fox_gate_fwd", in_specs=[vm, vm], out_specs=vm,
        out_shape=jax.ShapeDtypeStruct((S, LANE), F32),
        compiler_params=_cparams(),
    )(fl, bias)


def _fox_gate_bwd(fl, bias, dc):
    S = fl.shape[0]

    nb_ = _row_tile(S, ROW_TILE)

    def body(f_ref, b_ref, d_ref, o_ref, db_ref):
        ri = lax.broadcasted_iota(jnp.int32, (nb_, nb_), 0)
        ci = lax.broadcasted_iota(jnp.int32, (nb_, nb_), 1)
        tri = jnp.where(ri <= ci, 1.0, 0.0).astype(BF16)
        row = lax.broadcasted_iota(jnp.int32, (nb_, LANE), 0)
        nt = S // nb_

        def step(ii, carry):
            carry_c, carry_b = carry
            r0 = pl.multiple_of((nt - 1 - ii) * nb_, nb_)
            t = _tri_sum(tri, d_ref[pl.ds(r0, nb_), :]) + carry_c
            dz = t * _sigmoid(-(f_ref[pl.ds(r0, nb_), :] + b_ref[...]))
            o_ref[pl.ds(r0, nb_), :] = dz.astype(BF16)
            first = jnp.sum(jnp.where(row == 0, t, 0.0), axis=0, keepdims=True)
            return first, carry_b + jnp.sum(dz, axis=0, keepdims=True)

        zero = jnp.zeros((1, LANE), F32)
        _, db = lax.fori_loop(0, nt, step, (zero, zero))
        db_ref[...] = db

    vm = pl.BlockSpec(memory_space=pltpu.VMEM)
    return pl.pallas_call(
        body, name="fox_gate_bwd", in_specs=[vm, vm, vm], out_specs=[vm, vm],
        out_shape=[jax.ShapeDtypeStruct((S, LANE), BF16), jax.ShapeDtypeStruct((1, LANE), F32)],
        compiler_params=_cparams(),
    )(fl, bias, dc)


def _bias_lanes(col, lane, e, first):
    o0 = HEAD_DIM * (1 - e)
    hi, mid, lo = _split3(col)
    d0 = o0 if first else o0 + 3
    t = jnp.where((lane >= o0) & (lane < o0 + 6), jnp.ones(lane.shape, BF16), jnp.zeros(lane.shape, BF16))
    t = jnp.where(lane == d0, hi, t)
    t = jnp.where(lane == d0 + 1, mid, t)
    return jnp.where(lane == d0 + 2, lo, t)


def _fox_fwd(qkvg, c, H, gather=()):
    na = len(gather)
    S = qkvg.shape[0]
    W = H * HEAD_DIM
    HP = H // 2
    PP = 2 if HP % 2 == 0 else 1
    NE = 2 * PP
    tq = _row_tile(S, ATT_TILE)
    nq = S // tq
    wb = W // LANE
    scale = HEAD_DIM ** -0.5

    def body(*refs):
        q_ref, k_ref, v_ref, g_ref, c_ref = refs[:5]
        y_ref, o_ref, a_ref = refs[5 + na:8 + na]
        kaug_sc, vaug_sc, qaug_sc, s_sc, mb_sc, m_sc, acc_sc = refs[8 + 2 * na:15 + 2 * na]
        hp, qi = pl.program_id(0), pl.program_id(1)
        if na:
            remote, local = _direct_gather_copies(refs[5:5 + na], refs[8 + na:8 + 2 * na], *refs[15 + 2 * na:])

            @pl.when((hp == 0) & (qi == 0))
            def _():
                for cp in remote + local:
                    cp.start()
        lane = lax.broadcasted_iota(jnp.int32, (tq, LANE), 1)
        own = [lane < HEAD_DIM, lane >= HEAD_DIM]
        rows = lax.broadcasted_iota(jnp.int32, (tq, tq), 0)
        cols = lax.broadcasted_iota(jnp.int32, (tq, tq), 1)

        def bias_lanes(col, e, first):
            return _bias_lanes(col, lane, e % 2, first)

        def head_col(tile, e):
            return jnp.sum(jnp.where(lane == 2 * PP * hp + e, tile, 0.0), axis=1, keepdims=True)

        def tile_of(e):
            return slice(LANE * (e // 2), LANE * (e // 2 + 1))

        @pl.when(qi == 0)
        def _():
            def chunk(i, carry):
                r0 = pl.multiple_of(i * tq, tq)
                cb = c_ref[pl.ds(r0, tq), :]
                for e in range(NE):
                    kb, vb = k_ref[pl.ds(r0, tq), tile_of(e)], v_ref[pl.ds(r0, tq), tile_of(e)]
                    kaug_sc[e, pl.ds(r0, tq), :] = jnp.where(own[e % 2], kb, bias_lanes(-head_col(cb, e), e, False))
                    vaug_sc[e, pl.ds(r0, tq), :] = jnp.where(own[e % 2], vb, jnp.ones((tq, LANE), BF16))
                return carry
            lax.fori_loop(0, nq, chunk, 0)

        crow = c_ref[pl.ds(pl.multiple_of(qi * tq, tq), tq), :]
        ctq = [head_col(crow, e) for e in range(NE)]
        for e in range(NE):
            q = q_ref[:, tile_of(e)] * jnp.asarray(scale, BF16)
            qaug_sc[e] = jnp.where(own[e % 2], q, bias_lanes(ctq[e], e, True))
        m_sc[...] = jnp.full(m_sc.shape, NEG_INF, F32)
        acc_sc[...] = jnp.zeros(acc_sc.shape, F32)

        def scores(blk, slot, masked):
            k0 = pl.multiple_of(blk * tq, tq)
            for e in range(NE):
                s = _dot_nt(qaug_sc[e], kaug_sc[e, pl.ds(k0, tq), :])
                if masked:
                    s = jnp.where(rows >= cols, s, NEG_INF)
                s_sc[slot, e] = s
                mb_sc[slot, e] = jnp.broadcast_to(jnp.max(s, axis=1, keepdims=True), (tq, LANE))

        def accumulate(blk, slot):
            k0 = pl.multiple_of(blk * tq, tq)
            for e in range(NE):
                m_prev = m_sc[e]
                m_new = jnp.maximum(m_prev, mb_sc[slot, e])
                p = jnp.exp(s_sc[slot, e] - jnp.tile(m_new, (1, tq // LANE)))
                acc_sc[e] = jnp.exp(m_prev - m_new) * acc_sc[e] + _dot_nn(p.astype(BF16), vaug_sc[e, pl.ds(k0, tq), :])
                m_sc[e] = m_new

        def block_of(t):
            return jnp.where(t == 0, qi, t - 1)

        scores(qi, 0, True)

        def loop_body(t, carry):
            scores(t, (t + 1) % 2, False)
            accumulate(block_of(t), t % 2)
            return carry

        lax.fori_loop(0, qi, loop_body, 0)
        accumulate(block_of(qi), qi % 2)
        o_e, a_e = [], []
        for e in range(NE):
            acc = acc_sc[e]
            l = pltpu.roll(acc, HEAD_DIM, axis=1)
            o_e.append(acc / l)
            a_e.append(ctq[e] - (m_sc[e] + jnp.log(l)))
        for pp in range(PP):
            o = jnp.where(own[0], o_e[2 * pp], o_e[2 * pp + 1])
            g = g_ref[:, tile_of(2 * pp)].astype(F32)
            y_ref[:, tile_of(2 * pp)] = (o * (g * _sigmoid(g))).astype(BF16)
            o_ref[:, tile_of(2 * pp)] = o.astype(BF16)
            a_ref[pp] = jnp.where(own[0], a_e[2 * pp], a_e[2 * pp + 1])
        if na:
            @pl.when((hp == HP // PP - 1) & (qi == nq - 1))
            def _():
                _wait_all(remote, local)

    any_spec = pl.BlockSpec(memory_space=pl.ANY)
    sems = [pltpu.SemaphoreType.DMA((na, N_DEV - 1)), pltpu.SemaphoreType.DMA((na, N_DEV - 1)),
            pltpu.SemaphoreType.DMA((na,))] if na else []
    wide = PP * LANE
    outs = pl.pallas_call(
        body, name="fox_attn_fwd", grid=(HP // PP, nq),
        in_specs=[pl.BlockSpec((tq, wide), lambda h, i: (i, h)),
                  pl.BlockSpec((S, wide), lambda h, i: (0, wb // PP + h)),
                  pl.BlockSpec((S, wide), lambda h, i: (0, 2 * wb // PP + h)),
                  pl.BlockSpec((tq, wide), lambda h, i: (i, 3 * wb // PP + h)),
                  pl.BlockSpec((S, LANE), lambda h, i: (0, 0))] + [any_spec] * na,
        out_specs=[pl.BlockSpec((tq, wide), lambda h, i: (i, h)),
                   pl.BlockSpec((tq, wide), lambda h, i: (i, h)),
                   pl.BlockSpec((PP, tq, LANE), lambda h, i: (h, i, 0))] + [any_spec] * na,
        out_shape=[jax.ShapeDtypeStruct((S, W), BF16), jax.ShapeDtypeStruct((S, W), BF16),
                   jax.ShapeDtypeStruct((HP, S, LANE), F32)]
        + [jax.ShapeDtypeStruct((N_DEV,) + g.shape, g.dtype) for g in gather],
        scratch_shapes=[pltpu.VMEM((NE, S, LANE), BF16), pltpu.VMEM((NE, S, LANE), BF16),
                        pltpu.VMEM((NE, tq, LANE), BF16), pltpu.VMEM((2, NE, tq, tq), F32),
                        pltpu.VMEM((2, NE, tq, LANE), F32), pltpu.VMEM((NE, tq, LANE), F32),
                        pltpu.VMEM((NE, tq, LANE), F32)] + sems,
        compiler_params=_cparams(),
    )(qkvg, qkvg, qkvg, qkvg, c, *gather)
    return outs[0], outs[1], outs[2], list(outs[3:])


def _fox_out_bwd(dxb, wo, qkvg, o, a, H):
    S, D = dxb.shape
    W = H * HEAD_DIM
    tm, tn = _row_tile(S, EPI_TILE), _tile(W, EPI_TILE)
    npair = tn // LANE
    scale = HEAD_DIM ** -0.5

    def body(dx_ref, w_ref, q_ref, g_ref, o_ref, a_ref, qa_ref, da_ref, dg_ref):
        dy = _dot_nt(dx_ref[...], w_ref[...])
        lane = lax.broadcasted_iota(jnp.int32, (tm, LANE), 1)
        own = [lane < HEAD_DIM, lane >= HEAD_DIM]
        for p in range(npair):
            cols = slice(LANE * p, LANE * (p + 1))
            q = q_ref[:, cols] * jnp.asarray(scale, BF16)
            dyv, g, ov, at = dy[:, cols], g_ref[:, cols].astype(F32), o_ref[:, cols].astype(F32), a_ref[p]
            sg = _sigmoid(g)
            dob = (dyv * (g * sg)).astype(BF16)
            dg_ref[:, cols] = (dyv * ov * (sg * (1.0 + g * (1.0 - sg)))).astype(BF16)
            prod = dob.astype(F32) * ov
            for e in range(2):
                a_col = jnp.max(jnp.where(own[e], at, -jnp.inf), axis=1, keepdims=True)
                d_col = jnp.sum(jnp.where(own[e], prod, 0.0), axis=1, keepdims=True)
                qa_ref[e, :, cols] = jnp.where(own[e], q, _bias_lanes(a_col, lane, e, True))
                da_ref[e, :, cols] = jnp.where(own[e], dob, _bias_lanes(-d_col, lane, e, True))

    blk = pl.BlockSpec((tm, tn), lambda i, j: (i, j))
    pair = pl.BlockSpec((2, tm, tn), lambda i, j: (0, i, j))
    return pl.pallas_call(
        body, name="fox_out_bwd", grid=(S // tm, W // tn),
        in_specs=[pl.BlockSpec((tm, D), lambda i, j: (i, 0)), pl.BlockSpec((tn, D), lambda i, j: (j, 0)),
                  blk, pl.BlockSpec((tm, tn), lambda i, j: (i, 3 * W // tn + j)), blk,
                  pl.BlockSpec((npair, tm, LANE), lambda i, j: (j, i, 0))],
        out_specs=[pair, pair, pl.BlockSpec((None, tm, tn), lambda i, j: (3, i, j))],
        out_shape=[jax.ShapeDtypeStruct((2, S, W), BF16), jax.ShapeDtypeStruct((2, S, W), BF16),
                   jax.ShapeDtypeStruct((4, S, W), BF16)],
        compiler_params=_cparams(),
    )(dxb, wo, qkvg, qkvg, o, a)


def _fox_bwd(qaug, doaug, qkv, c, dqkvg, H, scatter=(), scatter_specs=()):
    na = len(scatter)
    S = qkv.shape[0]
    W = H * HEAD_DIM
    HP = H // 2
    tq = _row_tile(S, ATT_TILE)
    nq = S // tq
    wb = W // LANE
    scale = HEAD_DIM ** -0.5

    def body(*refs):
        qa_ref, da_ref, k_ref, v_ref, c_ref = refs[:5]
        out_ref, dcr_ref, dcc_ref = refs[6 + na:9 + na]
        dq_sc, dk_sc, dv_sc = refs[9 + 2 * na:12 + 2 * na]
        hp, kj = pl.program_id(0), pl.program_id(1)
        if na:
            remote = _direct_scatter_copies(refs[5:5 + na], refs[9 + na:9 + 2 * na], scatter_specs,
                                            *refs[12 + 2 * na:])

            @pl.when((hp == 0) & (kj == 0))
            def _():
                for cp in remote:
                    cp.start()
        lane = lax.broadcasted_iota(jnp.int32, (tq, LANE), 1)
        own = [lane < HEAD_DIM, lane >= HEAD_DIM]
        rows = lax.broadcasted_iota(jnp.int32, (tq, tq), 0)
        cols = lax.broadcasted_iota(jnp.int32, (tq, tq), 1)

        @pl.when(kj == 0)
        def _():
            dq_sc[...] = jnp.zeros(dq_sc.shape, F32)

        @pl.when((kj == 0) & (hp == 0))
        def _():
            dcr_ref[...] = jnp.zeros(dcr_ref.shape, F32)
            dcc_ref[...] = jnp.zeros(dcc_ref.shape, F32)

        kblk, vblk, cblk = k_ref[...], v_ref[...], c_ref[...]
        one, zero = jnp.ones((tq, LANE), BF16), jnp.zeros((tq, LANE), BF16)
        ka, va = [], []
        for e in range(2):
            o0 = HEAD_DIM * (1 - e)
            c_col = jnp.sum(jnp.where(lane == 2 * hp + e, cblk, 0.0), axis=1, keepdims=True)
            ka.append(jnp.where(own[e], kblk, _bias_lanes(-c_col, lane, e, False)))
            va.append(jnp.where(own[e], vblk, jnp.where((lane >= o0) & (lane < o0 + 3), one, zero)))
        dk_sc[...] = jnp.zeros(dk_sc.shape, F32)
        dv_sc[...] = jnp.zeros(dv_sc.shape, F32)

        def step(i, masked):
            r0 = pl.multiple_of(i * tq, tq)
            for e in range(2):
                qa = qa_ref[e, pl.ds(r0, tq), :]
                da = da_ref[e, pl.ds(r0, tq), :]
                p = jnp.exp(_dot_nt(qa, ka[e]))
                if masked:
                    p = jnp.where(rows >= cols, p, 0.0)
                ds = p * _dot_nt(da, va[e])
                pb, dsb = p.astype(BF16), ds.astype(BF16)
                dv_sc[e] += _dot_tn(pb, da)
                dk_sc[e] += _dot_tn(dsb, qa)
                dq_sc[e, pl.ds(r0, tq), :] += _dot_nn(dsb, ka[e])

        step(kj, True)

        def loop_body(i, carry):
            step(i, False)
            return carry

        lax.fori_loop(kj + 1, nq, loop_body, 0)
        k0 = pl.multiple_of(kj * tq, tq)
        out_ref[1, pl.ds(k0, tq), :] = jnp.where(own[0], dk_sc[0], dk_sc[1]).astype(BF16)
        out_ref[2, pl.ds(k0, tq), :] = jnp.where(own[0], dv_sc[0], dv_sc[1]).astype(BF16)

        def put_lane(ref, r0, e, tile, src_lane):
            col = jnp.sum(jnp.where(lane == src_lane, tile, 0.0), axis=1, keepdims=True)
            ref[pl.ds(r0, tq), :] = jnp.where(lane == 2 * hp + e, col, ref[pl.ds(r0, tq), :])

        for e in range(2):
            put_lane(dcc_ref, k0, e, dk_sc[e], HEAD_DIM * (1 - e) + 3)

        @pl.when(kj == nq - 1)
        def _():
            def chunk(i, carry):
                r0 = pl.multiple_of(i * tq, tq)
                d0, d1 = dq_sc[0, pl.ds(r0, tq), :], dq_sc[1, pl.ds(r0, tq), :]
                out_ref[0, pl.ds(r0, tq), :] = (jnp.where(own[0], d0, d1) * scale).astype(BF16)
                put_lane(dcr_ref, r0, 0, d0, HEAD_DIM)
                put_lane(dcr_ref, r0, 1, d1, 0)
                return carry
            lax.fori_loop(0, nq, chunk, 0)

        if na:
            @pl.when((hp == HP - 1) & (kj == nq - 1))
            def _():
                _wait_all(remote)

    pair = pl.BlockSpec((2, S, LANE), lambda h, j: (0, 0, h))
    vec = pl.BlockSpec((S, LANE), lambda h, j: (0, 0))
    any_spec = pl.BlockSpec(memory_space=pl.ANY)
    sems = [pltpu.SemaphoreType.DMA((na, N_DEV - 1)), pltpu.SemaphoreType.DMA((na, N_DEV - 1))] if na else []
    outs = pl.pallas_call(
        body, name="fox_attn_bwd", grid=(HP, nq),
        in_specs=[pair, pair,
                  pl.BlockSpec((tq, LANE), lambda h, j: (j, wb + h)),
                  pl.BlockSpec((tq, LANE), lambda h, j: (j, 2 * wb + h)),
                  pl.BlockSpec((tq, LANE), lambda h, j: (j, 0))] + [any_spec] * (na + 1),
        out_specs=[pl.BlockSpec((3, S, LANE), lambda h, j: (0, 0, h)), vec, vec] + [any_spec] * na,
        out_shape=[jax.ShapeDtypeStruct(dqkvg.shape, BF16), jax.ShapeDtypeStruct((S, LANE), F32),
                   jax.ShapeDtypeStruct((S, LANE), F32)]
        + [jax.ShapeDtypeStruct((N_DEV - 1,) + _scatter_block_shape(g, s), g.dtype)
           for g, s in zip(scatter, scatter_specs)],
        scratch_shapes=[pltpu.VMEM((2, S, LANE), F32), pltpu.VMEM((2, tq, LANE), F32),
                        pltpu.VMEM((2, tq, LANE), F32)] + sems,
        input_output_aliases={5 + na: 0},
        compiler_params=_cparams(),
    )(qaug, doaug, qkv, qkv, c, *scatter, dqkvg)
    return outs[0], outs[1], outs[2], list(outs[3:])


def _swa_pick(blk, half, lane):
    b = blk.astype(F32)
    r = pltpu.roll(b, HEAD_DIM, axis=1)
    return jnp.where(jnp.logical_xor(lane < HEAD_DIM, half == 1), b, r).astype(BF16)


def _swa_stack(t, lane, G):
    pieces = []
    z = jnp.zeros((SWA_BLOCK, LANE), t.dtype)
    for j in range(G // 2):
        tile = t[:, LANE * j:LANE * (j + 1)]
        pieces += [jnp.where(lane < HEAD_DIM, tile, z), jnp.where(lane < HEAD_DIM, z, tile)]
    return jnp.concatenate(pieces, axis=0)


def _swa_unstack(st, lane, G):
    tiles = []
    for j in range(G // 2):
        a = st[2 * j * SWA_BLOCK:(2 * j + 1) * SWA_BLOCK]
        b = st[(2 * j + 1) * SWA_BLOCK:(2 * j + 2) * SWA_BLOCK]
        tiles.append(jnp.where(lane < HEAD_DIM, a, b))
    return jnp.concatenate(tiles, axis=1)


def _swa_mask_bias(G):
    R = G * SWA_BLOCK
    t_loc = jnp.arange(R)[:, None] % SWA_BLOCK
    j_loc = jnp.arange(2 * SWA_BLOCK)[None, :]
    diff = t_loc + SWA_BLOCK - j_loc
    band = (diff >= 0) & (diff < SWA_BLOCK)
    return jnp.stack([jnp.where(band & (j_loc >= SWA_BLOCK), 0.0, NEG_INF),
                      jnp.where(band, 0.0, NEG_INF)]).astype(F32)


def _swa_scores(q, kp, kc, vp, vc, srow, bias, half, head0, G):
    lane = lax.broadcasted_iota(jnp.int32, (SWA_BLOCK, LANE), 1)
    kk = jnp.concatenate([_swa_pick(kp, half, lane), _swa_pick(kc, half, lane)], axis=0)
    vv = jnp.concatenate([_swa_pick(vp, half, lane), _swa_pick(vc, half, lane)], axis=0)
    qstack = _swa_stack(q, lane, G) * jnp.asarray(HEAD_DIM ** -0.5, BF16)
    s = _dot_nt(qstack, kk) + bias
    R = G * SWA_BLOCK
    lane1 = lax.broadcasted_iota(jnp.int32, (1, LANE), 1)
    sink = jnp.concatenate(
        [jnp.broadcast_to(jnp.sum(jnp.where(lane1 == head0 + g, srow, 0.0), axis=1, keepdims=True), (SWA_BLOCK, LANE))
         for g in range(G)], axis=0)
    m = jnp.maximum(jnp.broadcast_to(jnp.max(s, axis=1, keepdims=True), (R, LANE)), sink)
    e = jnp.exp(s - jnp.tile(m, (1, 2)))
    es = jnp.exp(sink - m)
    inv = 1.0 / (jnp.broadcast_to(jnp.sum(e, axis=1, keepdims=True), (R, LANE)) + es)
    return qstack, kk, vv, e * jnp.tile(inv, (1, 2)), es * inv, lane


def _swa_fwd(q, kv, gate, sinks, mask_bias, HQ, HKV):
    S = q.shape[0]
    G = HQ // HKV
    WQ, KVW = HQ * HEAD_DIM, HKV * HEAD_DIM
    nb = S // SWA_BLOCK
    GW = G * HEAD_DIM
    kb, vb = 0, KVW // LANE
    NH = min(HKV, 4)
    NP = NH // 2

    def body(q_ref, kp_ref, kc_ref, vp_ref, vc_ref, g_ref, sink_ref, b_ref, y_ref, o_ref):
        grp = pl.program_id(0)
        for hh in range(NH):
            cols, kt = slice(GW * hh, GW * (hh + 1)), slice(LANE * (hh // 2), LANE * (hh // 2 + 1))
            _, _, vv, p, _, lane = _swa_scores(q_ref[:, cols], kp_ref[:, kt], kc_ref[:, kt], vp_ref[:, kt], vc_ref[:, kt],
                                               sink_ref[...], b_ref[0], hh % 2, (NH * grp + hh) * G, G)
            o = _swa_unstack(_dot_nn(p.astype(BF16), vv), lane, G)
            g = g_ref[:, cols].astype(F32)
            y_ref[:, cols] = (o * (g * _sigmoid(g))).astype(BF16)
            o_ref[:, cols] = o.astype(BF16)

    blk = lambda cb, prev: pl.BlockSpec(
        (SWA_BLOCK, NP * LANE), lambda h, n, cb=cb, prev=prev: (jnp.maximum(n - prev, 0), cb // NP + h))
    qspec = pl.BlockSpec((SWA_BLOCK, NH * GW), lambda h, n: (n, h))
    return pl.pallas_call(
        body, name="swa_attn_fwd", grid=(HKV // NH, nb),
        in_specs=[qspec, blk(kb, 1), blk(kb, 0), blk(vb, 1), blk(vb, 0), qspec,
                  pl.BlockSpec((1, LANE), lambda h, n: (0, 0)),
                  pl.BlockSpec((1, G * SWA_BLOCK, 2 * SWA_BLOCK), lambda h, n: (jnp.minimum(n, 1), 0, 0))],
        out_specs=[qspec, qspec],
        out_shape=[jax.ShapeDtypeStruct((S, WQ), BF16), jax.ShapeDtypeStruct((S, WQ), BF16)],
        compiler_params=_cparams(),
    )(q, kv, kv, kv, kv, gate, sinks, mask_bias)


def _swa_bwd(q, kv, dy, gate, o, sinks, tables, mask_bias, HQ, HKV):
    S = q.shape[0]
    G = HQ // HKV
    WQ, KVW = HQ * HEAD_DIM, HKV * HEAD_DIM
    nb = S // SWA_BLOCK
    GW = G * HEAD_DIM
    R = G * SWA_BLOCK
    kb, vb = 0, KVW // LANE
    scale = HEAD_DIM ** -0.5
    NH = min(HKV, 4)
    NP = NH // 2
    assert G == 8

    def body(q_ref, kp_ref, kc_ref, vp_ref, vc_ref, dy_ref, g_ref, o_ref, sink_ref, t_ref, b_ref,
             dqg_ref, dkv_ref, dsink_ref, carry_sc):
        grp, n = pl.program_id(0), pl.program_id(1)

        @pl.when(n == 0)
        def _():
            carry_sc[...] = jnp.zeros(carry_sc.shape, F32)
            dsink_ref[...] = jnp.zeros(dsink_ref.shape, F32)

        @pl.when(n < nb)
        def _():
            t0, t1, t2 = (jnp.tile(t_ref[i], (1, GW // LANE)) for i in range(3))
            for hh in range(NH):
                cols, kt = slice(GW * hh, GW * (hh + 1)), slice(LANE * (hh // 2), LANE * (hh // 2 + 1))
                qstack, kk, vv, p, psink, lane = _swa_scores(
                    q_ref[:, cols], kp_ref[:, kt], kc_ref[:, kt], vp_ref[:, kt], vc_ref[:, kt], sink_ref[...], b_ref[0],
                    hh % 2, (NH * grp + hh) * G, G)
                dyv, g, ov = dy_ref[:, cols], g_ref[:, cols].astype(F32), o_ref[:, cols].astype(F32)
                sg = _sigmoid(g)
                dob = (dyv * (g * sg)).astype(BF16)
                dqg_ref[1, :, cols] = (dyv * ov * (sg * (1.0 + g * (1.0 - sg)))).astype(BF16)
                prod = dob.astype(F32) * ov
                dparts = []
                for j in range(G // 2):
                    tile = prod[:, LANE * j:LANE * (j + 1)]
                    for sel in (jnp.where(lane < HEAD_DIM, tile, 0.0), jnp.where(lane < HEAD_DIM, 0.0, tile)):
                        dparts.append(jnp.broadcast_to(jnp.sum(sel, axis=1, keepdims=True), (SWA_BLOCK, LANE)))
                delta = jnp.concatenate(dparts, axis=0)
                dostack = _swa_stack(dob, lane, G)
                ds = p * (_dot_nt(dostack, vv) - jnp.tile(delta, (1, 2)))
                dsb, pb = ds.astype(BF16), p.astype(BF16)
                dq = _swa_unstack(_dot_nn(dsb, kk), lane, G) * scale
                dq = dq * t0 + pltpu.roll(dq * t1, ROT_DIM // 2, axis=1) + pltpu.roll(dq * t2, GW - ROT_DIM // 2, axis=1)
                dqg_ref[0, :, cols] = dq.astype(BF16)
                dkk = _dot_tn(dsb, qstack)
                dvv = _dot_tn(pb, dostack)
                dkk = dkk + pltpu.roll(dkk, HEAD_DIM, axis=1)
                dvv = dvv + pltpu.roll(dvv, HEAD_DIM, axis=1)
                lane2 = lax.broadcasted_iota(jnp.int32, (2 * SWA_BLOCK, LANE), 1)
                comb = jnp.where(lane2 < HEAD_DIM, dkk, dvv)
                dkv_ref[hh] = carry_sc[hh] + comb[:SWA_BLOCK]
                carry_sc[hh] = comb[SWA_BLOCK:]
                sk = psink * delta
                rows = [-jnp.sum(sk[g_ * SWA_BLOCK:(g_ + 1) * SWA_BLOCK], axis=0, keepdims=True) for g_ in range(G)]
                dsink_ref[hh] += jnp.concatenate(rows, axis=0)

        @pl.when(n == nb)
        def _():
            dkv_ref[...] = carry_sc[...]

    cl = lambda n: jnp.minimum(n, nb - 1)
    blk = lambda cb, prev: pl.BlockSpec(
        (SWA_BLOCK, NP * LANE), lambda h, n, cb=cb, prev=prev: (jnp.maximum(cl(n) - prev, 0), cb // NP + h))
    qspec = pl.BlockSpec((SWA_BLOCK, NH * GW), lambda h, n: (cl(n), h))
    return pl.pallas_call(
        body, name="swa_attn_bwd", grid=(HKV // NH, nb + 1),
        in_specs=[qspec, blk(kb, 1), blk(kb, 0), blk(vb, 1), blk(vb, 0), qspec, qspec, qspec,
                  pl.BlockSpec((1, LANE), lambda h, n: (0, 0)),
                  pl.BlockSpec((3, SWA_BLOCK, LANE), lambda h, n: (0, cl(n), 0)),
                  pl.BlockSpec((1, R, 2 * SWA_BLOCK), lambda h, n: (jnp.minimum(n, 1), 0, 0))],
        out_specs=[pl.BlockSpec((2, SWA_BLOCK, NH * GW), lambda h, n: (0, cl(n), h)),
                   pl.BlockSpec((NH, SWA_BLOCK, LANE), lambda h, n: (h, jnp.maximum(n - 1, 0), 0)),
                   pl.BlockSpec((NH, 8, LANE), lambda h, n: (h, 0, 0))],
        out_shape=[jax.ShapeDtypeStruct((2, S, WQ), BF16), jax.ShapeDtypeStruct((HKV, S, LANE), F32),
                   jax.ShapeDtypeStruct((HKV, 8, LANE), F32)],
        scratch_shapes=[pltpu.VMEM((NH, SWA_BLOCK, LANE), F32)],
        compiler_params=_cparams(),
    )(q, kv, kv, kv, kv, dy, gate, o, sinks, tables, mask_bias)


def _swa_dkv_finish(dkv, tables):
    HKV, S, _ = dkv.shape
    KVW = HKV * HEAD_DIM
    tm = _row_tile(S, EPI_TILE)
    npair = HKV // 2

    def body(d_ref, t_ref, o_ref):
        lane = lax.broadcasted_iota(jnp.int32, (tm, LANE), 1)
        lo = lane < HEAD_DIM
        for p in range(npair):
            a, b = d_ref[2 * p], d_ref[2 * p + 1]
            tk = jnp.where(lo, a, pltpu.roll(b, HEAD_DIM, axis=1))
            tv = jnp.where(lo, pltpu.roll(a, HEAD_DIM, axis=1), b)
            tk = (tk * t_ref[0] + pltpu.roll(tk * t_ref[1], ROT_DIM // 2, axis=1)
                  + pltpu.roll(tk * t_ref[2], LANE - ROT_DIM // 2, axis=1))
            o_ref[:, LANE * p:LANE * (p + 1)] = tk.astype(BF16)
            o_ref[:, KVW + LANE * p:KVW + LANE * (p + 1)] = tv.astype(BF16)

    return pl.pallas_call(
        body, name="swa_dkv_finish", grid=(S // tm,),
        in_specs=[pl.BlockSpec((HKV, tm, LANE), lambda i: (0, i, 0)), pl.BlockSpec((3, tm, LANE), lambda i: (0, i, 0))],
        out_specs=pl.BlockSpec((tm, 2 * KVW), lambda i: (i, 0)),
        out_shape=jax.ShapeDtypeStruct((S, 2 * KVW), BF16),
        compiler_params=_cparams(),
    )(dkv, tables)


def _rope_tables(S, width):
    half = ROT_DIM // 2
    pos = jnp.arange(S, dtype=F32)
    inv_freq = ROPE_THETA ** (-jnp.arange(half, dtype=F32) / half)
    ang = pos[:, None] * inv_freq[None, :]
    cos, sin = jnp.cos(ang), jnp.sin(ang)
    one = jnp.ones((S, HEAD_DIM - ROT_DIM), F32)
    zero = jnp.zeros((S, HEAD_DIM - ROT_DIM), F32)
    zh = jnp.zeros((S, half), F32)
    t0 = jnp.concatenate([cos, cos, one], axis=1)
    t1 = jnp.concatenate([-sin, zh, zero], axis=1)
    t2 = jnp.concatenate([zh, sin, zero], axis=1)
    return jnp.stack([jnp.tile(t, (1, width // HEAD_DIM)) for t in (t0, t1, t2)])


def _pad_rows(v, row, total_rows=8):
    return jnp.pad(v, ((row, total_rows - row - v.shape[0]), (0, 0)))


def _pad_lanes(v, off, width):
    return jnp.pad(v, ((0, 0), (off, width - off - v.shape[1])))


def kernel(x, norm_g, fox_w_in, fox_b_f, fox_w_out, swa_w_in, swa_sinks, swa_w_out, final_g, loss_target, m_norm_g, m_fox_w_in, m_fox_b_f, m_fox_w_out, m_swa_w_in, m_swa_sinks, m_swa_w_out, m_final_g, v_norm_g, v_fox_w_in, v_fox_b_f, v_fox_w_out, v_swa_w_in, v_swa_sinks, v_swa_w_out, v_final_g):
    S, D = x.shape[1], x.shape[2]
    H = fox_b_f.shape[1]
    W = H * HEAD_DIM
    wf = fox_w_in.shape[2]
    ws = swa_w_in.shape[2]
    HQ = swa_sinks.shape[1]
    WQ = HQ * HEAD_DIM
    KVW = (ws * N_DEV - 2 * WQ) // 2
    HKV = KVW // HEAD_DIM
    rows_o = fox_w_out.shape[1]
    assert wf * N_DEV == 4 * W + H and rows_o * N_DEV == W and H <= LANE and HQ <= LANE
    me = _my_index()

    _, sw_f, np_f = _slab_geom(wf)
    _, sw_s, np_s = _slab_geom(ws)

    def slab(w2d, w, sw):
        return jnp.pad(w2d.astype(BF16), ((0, 0), (0, sw - w)))

    (fi_all,) = _all_gather([slab(fox_w_in[0], wf, sw_f)])
    w_fi = _assemble(fi_all, wf)
    later = [slab(swa_w_in[0], ws, sw_s), fox_w_out[0].astype(BF16), swa_w_out[0].astype(BF16)]

    x0 = x[0]
    g0, g1, gf = norm_g[0:1], norm_g[1:2], final_g[None, :]
    bias = _pad_lanes(fox_b_f, 0, LANE)
    sinks = _pad_lanes(swa_sinks, 0, LANE)
    tab_k = _rope_tables(S, LANE)
    mask_bias = _swa_mask_bias(HQ // HKV)

    h0 = _rmsnorm_fwd(x0, g0, "rmsnorm0")
    qkv0 = _proj(h0, w_fi, 0, 4 * W, BF16, "fox_in_qkvg")
    fl = _proj(h0, w_fi, 4 * W, LANE, F32, "fox_in_f")
    c = _fox_gate_fwd(fl, bias)
    y0, o0, a0, (si_all, fo_all, so_all) = _fox_fwd(qkv0, c, H, gather=later)
    w_si = _assemble(si_all, ws)
    w_fo = fo_all.reshape(W, D)
    w_so = so_all.reshape(WQ, D)
    x1, h1 = _out_proj_norm(y0, w_fo, x0, g1, "fox_out")

    q1 = _proj(h1, w_si, 0, WQ, BF16, "swa_in_q", rope=(tab_k, WQ))
    kv1 = _proj(h1, w_si, WQ, 2 * KVW, BF16, "swa_in_kv", rope=(tab_k, KVW))
    gate1 = _proj(h1, w_si, WQ + 2 * KVW, WQ, BF16, "swa_in_gate")
    y1, o1 = _swa_fwd(q1, kv1, gate1, sinks, mask_bias, HQ, HKV)
    dx2, dx2b, dgf, loss_p = _out_proj_loss(y1, w_so, x1, loss_target[0], gf, "swa_out_loss")

    dy1 = _matmul_nt([(dx2b, None, 0)], w_so, WQ, "swa_out_bwd")
    g_so, g_so_h = _matmul_tn(y1, [(dx2b, None, 0)], D, "swa_out_wgrad", also_bf16=True)
    dqg1, dkv1, dsink = _swa_bwd(q1, kv1, dy1, gate1, o1, sinks, tab_k, mask_bias, HQ, HKV)
    dkv1f = _swa_dkv_finish(dkv1, tab_k)
    parts1 = [(dqg1, 0, 0), (dkv1f, None, WQ), (dqg1, 1, WQ + 2 * KVW)]
    g_si, g_si_h = _matmul_tn(h1, parts1, np_s, "swa_in_wgrad", tile_major=True, also_bf16=True)
    dh1 = _matmul_nt(parts1, w_si, D, "swa_in_bwd")
    dx1, dx1b, dg1 = _rmsnorm_bwd(dh1, x1, g1, dx2, "rmsnorm1_bwd")

    qaug0, doaug0, dqkvg0 = _fox_out_bwd(dx1b, w_fo, qkv0, o0, a0, H)
    g_fo, g_fo_h = _matmul_tn(y0, [(dx1b, None, 0)], D, "fox_out_wgrad", also_bf16=True)
    early_specs = [("col", ws), ("row", rows_o), ("row", rows_o)]
    dqkvg0, dcr, dcc, early_recv = _fox_bwd(qaug0, doaug0, qkv0, c, dqkvg0, H, scatter=[g_si_h, g_fo_h, g_so_h],
                                           scatter_specs=early_specs)
    dfl, dbf = _fox_gate_bwd(fl, bias, dcr - dcc)
    parts0 = [(dqkvg0, "stack", 0), (dfl, None, 4 * W)]
    spec_fi = ("col", wf)
    fi_halves, token = [], None
    for half in range(2):
        g_fi, g_fi_h = _matmul_tn(h0, parts0, np_f, f"fox_in_wgrad_rows{half}", tile_major=True, also_bf16=True,
                                  rows=(half * (D // 2), D // 2), after=() if token is None else (token,))
        fi_sems, fi_src, fi_land, token = _scatter_start(g_fi_h, spec_fi)
        fi_halves.append((g_fi, fi_sems, fi_src, fi_land))
    parts0[-1] = (dfl + token[0, 0].astype(BF16), None, 4 * W)
    dh0 = _matmul_nt(parts0, w_fi, D, "fox_in_bwd")
    dx0, _, dg0 = _rmsnorm_bwd(dh0, x0, g0, dx1, "rmsnorm0_bwd")

    red_si, gw_fo, gw_so = [_final_sum8(g_, r_, s_)
                            for g_, r_, s_ in zip([g_si, g_fo, g_so], early_recv, early_specs)]
    gt_si = lax.dynamic_slice(red_si, ((ws * me) % LANE, 0), (ws, D))

    def t_in(p):
        return jnp.swapaxes(p[0], 0, 1)

    def t_out(t):
        return jnp.swapaxes(t, 0, 1)[None]

    P = D
    dsink_v = dsink[:, :, 0].reshape(1, HQ)
    row3 = _pad_lanes(dbf[:, :H], 0, P) + _pad_lanes(dsink_v, LANE, P) + _pad_lanes(loss_p[:, :1], 2 * LANE, P)
    pack = _pad_rows(dg0, 0) + _pad_rows(dg1, 1) + _pad_rows(dgf, 2) + _pad_rows(row3, 3)

    d_fo, m_fo, v_fo = _adamw(fox_w_out[0], gw_fo, m_fox_w_out[0], v_fox_w_out[0], "adamw_fox_out")
    d_si, m_si, v_si = _adamw(t_in(swa_w_in), gt_si, t_in(m_swa_w_in), t_in(v_swa_w_in), "adamw_swa_in")
    d_so, m_so, v_so = _adamw(swa_w_out[0], gw_so, m_swa_w_out[0], v_swa_w_out[0], "adamw_swa_out")
    behind, red_fi = [dx0, pack, d_fo, d_si, d_so], []
    for g_fi, fi_sems, fi_src, fi_land in fi_halves:
        recv_fi = _scatter_wait(fi_sems, fi_src, fi_land, spec_fi, after=behind)
        red_fi.append(_final_sum8(g_fi, recv_fi, spec_fi))
        behind = [recv_fi]
    red_fi = jnp.concatenate(red_fi, axis=1)
    gt_fi = lax.dynamic_slice(red_fi, ((wf * me) % LANE, 0), (wf, D))

    d_fi, m_fi, v_fi = _adamw(t_in(fox_w_in), gt_fi, t_in(m_fox_w_in), t_in(v_fox_w_in), "adamw_fox_in")
    gw_si, d_si, m_si, v_si = [t_out(t)[0] for t in (gt_si, d_si, m_si, v_si)]
    gw_fi, d_fi, m_fi, v_fi = [t_out(t)[0] for t in (gt_fi, d_fi, m_fi, v_fi)]

    tot = _all_reduce_small(pack, after=recv_fi)
    loss = tot[3, 2 * LANE]
    g_norm = tot[0:2]
    g_final = tot[2]
    g_bf = tot[3:4, 0:H]
    g_sinks = tot[3:4, LANE:LANE + HQ]

    def small_pack(ng, fg, bf, sk):
        r3 = _pad_lanes(bf, 0, P) + _pad_lanes(sk, LANE, P)
        return _pad_rows(ng, 0) + _pad_rows(fg[None, :], 2) + _pad_rows(r3, 3)

    sd, sm, sv = _adamw(small_pack(norm_g, final_g, fox_b_f, swa_sinks), tot,
                        small_pack(m_norm_g, m_final_g, m_fox_b_f, m_swa_sinks),
                        small_pack(v_norm_g, v_final_g, v_fox_b_f, v_swa_sinks), "adamw_small")

    def unpack(t):
        return t[0:2], t[3:4, 0:H], t[3:4, LANE:LANE + HQ], t[2]

    def group(small, fi, fo, si, so):
        ng, bf, sk, fg = unpack(small)
        return (ng, fi[None], bf, fo[None], si[None], sk, so[None], fg)

    grads = (g_norm, gw_fi[None], g_bf, gw_fo[None], gw_si[None], g_sinks, gw_so[None], g_final)
    return (loss, dx0[None], *grads, *group(sd, d_fi, d_fo, d_si, d_so),
            *group(sm, m_fi, m_fo, m_si, m_so), *group(sv, v_fi, v_fo, v_si, v_so))
```

```python
import math

import jax
import jax.numpy as jnp
from jax import lax
from jax.experimental import pallas as pl
from jax.experimental.pallas import tpu as pltpu

F32 = jnp.float32
BF16 = jnp.bfloat16
MESH = pl.DeviceIdType.MESH

N_DEV = 8
LANE = 128
HEAD_DIM = 64
SWA_BLOCK = 128
NEG_INF = -1e30
RMS_EPS = 1e-6
ROPE_THETA = 500000.0
ROT_DIM = HEAD_DIM // 4
ADAM_LR, ADAM_B1, ADAM_B2, ADAM_EPS, ADAM_WD, ADAM_STEP = 0.001, 0.9, 0.999, 1e-08, 0.01, 10
VMEM_LIMIT = 56 * 1024 * 1024
MM_TILE = 1024
ATT_TILE = 512
EPI_TILE = 512
ROW_TILE = 256
ADAM_TILE_ELEMS = 3 << 18


def _cparams(**kw):
    return pltpu.CompilerParams(vmem_limit_bytes=VMEM_LIMIT, **kw)


def _tile(n, cap):
    if n <= cap:
        return n
    t = (cap // LANE) * LANE
    while t > LANE and n % t:
        t -= LANE
    assert n % t == 0, (n, cap)
    return t


def _row_tile(n, cap):
    t = min(n, cap)
    while n % t:
        t //= 2
    return t


def _dot_nn(a, b):
    return jnp.dot(a, b, preferred_element_type=F32)


def _dot_nt(a, b):
    return lax.dot_general(a, b, (((1,), (1,)), ((), ())), preferred_element_type=F32)


def _dot_tn(a, b):
    return lax.dot_general(a, b, (((0,), (0,)), ((), ())), preferred_element_type=F32)


def _split3(x):
    hi = x.astype(BF16)
    r1 = x - hi.astype(F32)
    mid = r1.astype(BF16)
    return hi, mid, (r1 - mid.astype(F32)).astype(BF16)


def _sigmoid(g):
    return 1.0 / (1.0 + jnp.exp(-g))


def _slab_geom(w):
    starts = [w * i for i in range(N_DEV)]
    aligned = [LANE * (s // LANE) for s in starts]
    offs = [s - a for s, a in zip(starts, aligned)]
    sw = LANE * (-(-(max(offs) + w) // LANE))
    return aligned, sw, aligned[-1] + sw


def _my_index():
    return 4 * lax.axis_index("x") + 2 * lax.axis_index("y") + lax.axis_index("c")


def _all_gather(arrs):
    n = len(arrs)

    def body(*refs):
        ins, outs = refs[:n], refs[n:2 * n]
        send_sems, recv_sems, local_sems = refs[2 * n:]
        x, y, c = lax.axis_index("x"), lax.axis_index("y"), lax.axis_index("c")
        me, sib = (x, y, c), (x, y, 1 - c)
        chips = [(1 - x, y), (x, 1 - y), (1 - x, 1 - y)]

        def idx(px, py, pc):
            return 4 * px + 2 * py + pc

        def copy(a, k, block, to, src=None):
            dst = outs[a].at[idx(*block)]
            return pltpu.make_async_remote_copy(
                src_ref=dst if src is None else src, dst_ref=dst,
                send_sem=send_sems.at[a, k], recv_sem=recv_sems.at[a, k],
                device_id=to, device_id_type=MESH)

        mine = [pltpu.make_async_copy(ins[a], outs[a].at[idx(*me)], local_sems.at[a]) for a in range(n)]
        for m in mine:
            m.start()
        first = []
        for a in range(n):
            first.append(copy(a, 0, me, sib, src=ins[a]))
            for j, chip in enumerate(chips):
                first.append(copy(a, 1 + j, me, (*chip, c), src=ins[a]))
        for cp in first:
            cp.start()
        passed = []
        for j, chip in enumerate(chips):
            for a in range(n):
                copy(a, 1 + j, (*chip, c), me).wait_recv()
                p = copy(a, 4 + j, (*chip, c), sib)
                p.start()
                passed.append(p)
        for a in range(n):
            copy(a, 0, sib, me).wait_recv()
        for j, chip in enumerate(chips):
            for a in range(n):
                copy(a, 4 + j, (*chip, 1 - c), me).wait_recv()
        for cp in first + passed:
            cp.wait_send()
        for m in mine:
            m.wait()

    any_spec = pl.BlockSpec(memory_space=pl.ANY)
    return pl.pallas_call(
        body, name="weights_all_gather",
        out_shape=[jax.ShapeDtypeStruct((N_DEV,) + a.shape, a.dtype) for a in arrs],
        in_specs=[any_spec] * n, out_specs=[any_spec] * n,
        scratch_shapes=[pltpu.SemaphoreType.DMA((n, 7)), pltpu.SemaphoreType.DMA((n, 7)),
                        pltpu.SemaphoreType.DMA((n,))],
    )(*arrs)


def _rs_windows(specs):
    def window(ref, spec, blk):
        kind, n = spec
        if kind == "col":
            _, sw, _ = _slab_geom(n)
            return ref.at[pl.ds((n * blk) // LANE, sw // LANE)]
        start = pl.multiple_of(n * blk, n)
        return ref.at[pl.ds(start, n), :]
    return window


def _peer(k):
    x, y, c = lax.axis_index("x"), lax.axis_index("y"), lax.axis_index("c")
    return (x ^ (k >> 2), y ^ ((k >> 1) & 1), c ^ (k & 1))


def _direct_gather_copies(ins, outs, send_sems, recv_sems, local_sems):
    me = _my_index()
    remote, local = [], []
    for a, (src, dst) in enumerate(zip(ins, outs)):
        local.append(pltpu.make_async_copy(src, dst.at[me], local_sems.at[a]))
        for k in range(1, N_DEV):
            remote.append(pltpu.make_async_remote_copy(
                src_ref=src, dst_ref=dst.at[me], send_sem=send_sems.at[a, k - 1], recv_sem=recv_sems.at[a, k - 1],
                device_id=_peer(k), device_id_type=MESH))
    return remote, local


def _direct_scatter_copies(ins, outs, specs, send_sems, recv_sems):
    window = _rs_windows(specs)
    remote = []
    for a, (src, dst) in enumerate(zip(ins, outs)):
        for k in range(1, N_DEV):
            px, py, pc = _peer(k)
            remote.append(pltpu.make_async_remote_copy(
                src_ref=window(src, specs[a], 4 * px + 2 * py + pc), dst_ref=dst.at[k - 1],
                send_sem=send_sems.at[a, k - 1], recv_sem=recv_sems.at[a, k - 1],
                device_id=(px, py, pc), device_id_type=MESH))
    return remote


def _scatter_block_shape(g, spec):
    kind, w = spec
    return (_slab_geom(w)[1] // LANE, g.shape[1], LANE) if kind == "col" else (w, g.shape[1])


def _wait_all(remote, local=()):
    for cp in remote:
        cp.wait_recv()
    for cp in remote:
        cp.wait_send()
    for cp in local:
        cp.wait()


def _scatter_start(g, spec):
    blk = _scatter_block_shape(g, spec)
    window = _rs_windows([spec])
    npeer = N_DEV - 1

    def body(g_ref, land_ref, *rest):
        sems = rest[:2 * npeer]
        token = rest[2 * npeer + 2]
        for cp in _peer_block_copies(g_ref, land_ref, spec, window, sems[:npeer], sems[npeer:]):
            cp.start()
        token[...] = jnp.zeros(token.shape, token.dtype)

    hbm = pl.BlockSpec(memory_space=pltpu.HBM)
    sem = pl.BlockSpec(memory_space=pltpu.SEMAPHORE)
    land = lax.empty((npeer,) + blk, g.dtype)
    outs = pl.pallas_call(
        body, name="grads_scatter_start",
        out_shape=(pltpu.SemaphoreType.DMA(()),) * (2 * npeer)
        + (pltpu.HBM(g.shape, g.dtype), pltpu.HBM(land.shape, land.dtype), jax.ShapeDtypeStruct((8, LANE), F32)),
        in_specs=(hbm, hbm),
        out_specs=(sem,) * (2 * npeer) + (hbm, hbm, pl.BlockSpec(memory_space=pltpu.VMEM)),
        input_output_aliases={0: 2 * npeer, 1: 2 * npeer + 1},
        compiler_params=pltpu.CompilerParams(has_side_effects=pltpu.SideEffectType.DATAFLOW_SIDE_EFFECTING),
    )(pltpu.with_memory_space_constraint(g, pltpu.HBM), pltpu.with_memory_space_constraint(land, pltpu.HBM))
    return outs[:2 * npeer], outs[2 * npeer], outs[2 * npeer + 1], outs[2 * npeer + 2]


def _peer_block_copies(g_ref, land_ref, spec, window, send_sems, recv_sems):
    copies = []
    for k in range(1, N_DEV):
        px, py, pc = _peer(k)
        copies.append(pltpu.make_async_remote_copy(
            src_ref=window(g_ref, spec, 4 * px + 2 * py + pc), dst_ref=land_ref.at[k - 1],
            send_sem=send_sems[k - 1], recv_sem=recv_sems[k - 1], device_id=(px, py, pc), device_id_type=MESH))
    return copies


def _scatter_wait(sems, g_thru, land_thru, spec, after):
    window = _rs_windows([spec])
    npeer = N_DEV - 1

    def body(g_ref, land_ref, *rest):
        s = rest[:2 * npeer]
        copies = _peer_block_copies(g_ref, land_ref, spec, window, s[:npeer], s[npeer:])
        for cp in copies:
            cp.wait_send()
        for cp in copies:
            cp.wait_recv()

    hbm = pl.BlockSpec(memory_space=pltpu.HBM)
    sem = pl.BlockSpec(memory_space=pltpu.SEMAPHORE)
    return pl.pallas_call(
        body, name="grads_scatter_wait",
        out_shape=(pltpu.HBM(g_thru.shape, g_thru.dtype), pltpu.HBM(land_thru.shape, land_thru.dtype)),
        in_specs=(hbm, hbm) + (sem,) * (2 * npeer) + (pl.BlockSpec(memory_space=pl.ANY),) * len(after),
        out_specs=(hbm, hbm), input_output_aliases={0: 0, 1: 1},
        compiler_params=pltpu.CompilerParams(has_side_effects=pltpu.SideEffectType.DATAFLOW_SIDE_EFFECTING),
    )(g_thru, land_thru, *sems, *after)[1]


def _final_sum8(g, recv, spec):
    kind, n = spec
    me = _my_index()
    offs = jnp.stack([(n * me) // LANE if kind == "col" else me]).astype(jnp.int32)
    if kind == "col":
        _, T, M, _ = recv.shape
        grid = (T,)
        in_specs = [pl.BlockSpec((1, M, LANE), lambda t, o: (o[0] + t, 0, 0)),
                    pl.BlockSpec((N_DEV - 1, 1, M, LANE), lambda t, o: (0, t, 0, 0))]
        out_spec = pl.BlockSpec((LANE, M), lambda t, o: (t, 0))
        out_shape = jax.ShapeDtypeStruct((T * LANE, M), F32)
    else:
        _, nrow, C = recv.shape
        grid = (1,)
        in_specs = [pl.BlockSpec((nrow, C), lambda t, o: (o[0], 0)),
                    pl.BlockSpec((N_DEV - 1, nrow, C), lambda t, o: (0, 0, 0))]
        out_spec = pl.BlockSpec((nrow, C), lambda t, o: (0, 0))
        out_shape = jax.ShapeDtypeStruct((nrow, C), F32)

    def body(o_ref, g_ref, r_ref, out_ref):
        acc = g_ref[0] if kind == "col" else g_ref[...]
        for k in range(N_DEV - 1):
            acc = acc + (r_ref[k, 0] if kind == "col" else r_ref[k]).astype(F32)
        out_ref[...] = acc.T if kind == "col" else acc

    return pl.pallas_call(
        body, name="grads_final_sum8",
        grid_spec=pltpu.PrefetchScalarGridSpec(num_scalar_prefetch=1, grid=grid, in_specs=in_specs,
                                               out_specs=out_spec),
        out_shape=out_shape, compiler_params=_cparams(),
    )(offs, g, recv)


def _all_reduce_small(pack, after):
    R, P = pack.shape

    def body(x_ref, after_ref, o_ref, gat_ref, send_sems, recv_sems):
        x, y, c = lax.axis_index("x"), lax.axis_index("y"), lax.axis_index("c")
        me = 4 * x + 2 * y + c
        gat_ref[me] = x_ref[...]
        copies = []
        for k in range(1, N_DEV):
            peer = (x ^ (k >> 2), y ^ ((k >> 1) & 1), c ^ (k & 1))
            copies.append(pltpu.make_async_remote_copy(
                src_ref=x_ref, dst_ref=gat_ref.at[me],
                send_sem=send_sems.at[k - 1], recv_sem=recv_sems.at[k - 1],
                device_id=peer, device_id_type=MESH))
        for cp in copies:
            cp.start()
        for cp in copies:
            cp.wait_recv()
        for cp in copies:
            cp.wait_send()
        acc = gat_ref[0]
        for d in range(1, N_DEV):
            acc = acc + gat_ref[d]
        o_ref[...] = acc

    vm = pl.BlockSpec(memory_space=pltpu.VMEM)
    return pl.pallas_call(
        body, name="small_all_reduce",
        out_shape=jax.ShapeDtypeStruct((R, P), F32),
        in_specs=[vm, pl.BlockSpec(memory_space=pl.ANY)], out_specs=vm,
        scratch_shapes=[pltpu.VMEM((N_DEV, R, P), F32),
                        pltpu.SemaphoreType.DMA((N_DEV - 1,)), pltpu.SemaphoreType.DMA((N_DEV - 1,))],
    )(pack, after)


def _assemble(slabs, w):
    aligned, sw, total = _slab_geom(w)
    K = slabs.shape[1]
    tr = _row_tile(K, ROW_TILE)

    def body(s_ref, o_ref):
        o_ref[...] = jnp.zeros(o_ref.shape, BF16)
        for i in range(N_DEV):
            a, off = aligned[i], w * i - aligned[i]
            x = s_ref[i]
            if off:
                x = pltpu.roll(x, off, axis=1)
            o_ref[:, a:a + sw] = o_ref[:, a:a + sw] + x

    return pl.pallas_call(
        body, name="assemble_w_in", grid=(K // tr,),
        in_specs=[pl.BlockSpec((N_DEV, tr, sw), lambda i: (0, i, 0))],
        out_specs=pl.BlockSpec((tr, total), lambda i: (i, 0)),
        out_shape=jax.ShapeDtypeStruct((K, total), BF16),
        compiler_params=_cparams(),
    )(slabs)


def _rmsnorm_fwd(x, g, name):
    S, D = x.shape
    tm = _row_tile(S, ROW_TILE)

    def body(x_ref, g_ref, h_ref):
        xv = x_ref[...]
        r = lax.rsqrt(jnp.mean(xv * xv, axis=-1, keepdims=True) + RMS_EPS)
        h_ref[...] = ((xv * r) * g_ref[...]).astype(BF16)

    return pl.pallas_call(
        body, name=name, grid=(S // tm,),
        in_specs=[pl.BlockSpec((tm, D), lambda i: (i, 0)), pl.BlockSpec((1, D), lambda i: (0, 0))],
        out_specs=pl.BlockSpec((tm, D), lambda i: (i, 0)),
        out_shape=jax.ShapeDtypeStruct((S, D), BF16),
        compiler_params=_cparams(),
    )(x, g)


def _rmsnorm_bwd(dh, x, g, dres, name):
    S, D = x.shape
    tm = _row_tile(S, ROW_TILE)

    def body(dh_ref, x_ref, g_ref, dr_ref, dx_ref, dxb_ref, dg_ref):
        xv = x_ref[...]
        r = lax.rsqrt(jnp.mean(xv * xv, axis=-1, keepdims=True) + RMS_EPS)
        xhat = xv * r
        d = dh_ref[...]
        gd = d * g_ref[...]
        dx = r * (gd - xhat * jnp.mean(gd * xhat, axis=-1, keepdims=True)) + dr_ref[...]
        dx_ref[...] = dx
        dxb_ref[...] = dx.astype(BF16)

        @pl.when(pl.program_id(0) == 0)
        def _():
            dg_ref[...] = jnp.zeros(dg_ref.shape, F32)
        dg_ref[...] += jnp.sum(d * xhat, axis=0, keepdims=True)

    row = pl.BlockSpec((tm, D), lambda i: (i, 0))
    vec = pl.BlockSpec((1, D), lambda i: (0, 0))
    return pl.pallas_call(
        body, name=name, grid=(S // tm,),
        in_specs=[row, row, vec, row], out_specs=[row, row, vec],
        out_shape=[jax.ShapeDtypeStruct((S, D), F32), jax.ShapeDtypeStruct((S, D), BF16),
                   jax.ShapeDtypeStruct((1, D), F32)],
        compiler_params=_cparams(),
    )(dh, x, g, dres)


def _adamw(w, g, m, v, name):
    R, C = w.shape
    steps = pl.cdiv(R * C, ADAM_TILE_ELEMS)
    tr = R if steps == 1 else pl.cdiv(pl.cdiv(R, steps), 8) * 8
    c1 = 1.0 - ADAM_B1 ** ADAM_STEP
    c2 = 1.0 - ADAM_B2 ** ADAM_STEP

    def body(w_ref, g_ref, m_ref, v_ref, d_ref, nm_ref, nv_ref):
        gv = g_ref[...]
        nm = ADAM_B1 * m_ref[...] + (1.0 - ADAM_B1) * gv
        nv = ADAM_B2 * v_ref[...] + (1.0 - ADAM_B2) * (gv * gv)
        d_ref[...] = -ADAM_LR * ((nm / c1) / (jnp.sqrt(nv / c2) + ADAM_EPS) + ADAM_WD * w_ref[...])
        nm_ref[...] = nm
        nv_ref[...] = nv

    spec = pl.BlockSpec((tr, C), lambda i: (i, 0))
    return pl.pallas_call(
        body, name=name, grid=(pl.cdiv(R, tr),),
        in_specs=[spec] * 4, out_specs=[spec] * 3,
        out_shape=[jax.ShapeDtypeStruct((R, C), F32)] * 3,
        compiler_params=_cparams(),
    )(w, g, m, v)


def _proj(h, wfull, col0, ncols, out_dtype, name, rope=None):
    S, K = h.shape
    tm = _row_tile(S, MM_TILE)
    tn = math.gcd(_tile(ncols, MM_TILE), col0) if col0 else _tile(ncols, MM_TILE)
    if rope is not None:
        tn = _tile(math.gcd(ncols, rope[1]), MM_TILE)
    assert ncols % tn == 0 and col0 % tn == 0
    cb = col0 // tn

    def body(*refs):
        if rope is None:
            a_ref, b_ref, o_ref = refs
        else:
            a_ref, b_ref, t_ref, o_ref = refs
        acc = _dot_nn(a_ref[...], b_ref[...])
        if rope is not None:
            t0, t1, t2 = (jnp.tile(t_ref[i], (1, tn // LANE)) for i in range(3))
            roped = (acc * t0 + pltpu.roll(acc, tn - ROT_DIM // 2, axis=1) * t1
                     + pltpu.roll(acc, ROT_DIM // 2, axis=1) * t2)
            acc = jnp.where(pl.program_id(1) < rope[1] // tn, roped, acc)
        o_ref[...] = acc.astype(out_dtype)

    in_specs = [pl.BlockSpec((tm, K), lambda i, j: (i, 0)), pl.BlockSpec((K, tn), lambda i, j: (0, cb + j))]
    args = [h, wfull]
    if rope is not None:
        in_specs.append(pl.BlockSpec((3, tm, LANE), lambda i, j: (0, i, 0)))
        args.append(rope[0])
    return pl.pallas_call(
        body, name=name, grid=(S // tm, ncols // tn),
        in_specs=in_specs, out_specs=pl.BlockSpec((tm, tn), lambda i, j: (i, j)),
        out_shape=jax.ShapeDtypeStruct((S, ncols), out_dtype),
        compiler_params=_cparams(),
    )(*args)


def _out_proj_norm(y, wo, xres, g, name):
    S, W = y.shape
    D = wo.shape[1]
    tm = _row_tile(S, EPI_TILE)

    def body(a_ref, b_ref, r_ref, g_ref, x_ref, h_ref):
        xv = r_ref[...] + _dot_nn(a_ref[...], b_ref[...])
        x_ref[...] = xv
        r = lax.rsqrt(jnp.mean(xv * xv, axis=-1, keepdims=True) + RMS_EPS)
        h_ref[...] = ((xv * r) * g_ref[...]).astype(BF16)

    row = pl.BlockSpec((tm, D), lambda i: (i, 0))
    return pl.pallas_call(
        body, name=name, grid=(S // tm,),
        in_specs=[pl.BlockSpec((tm, W), lambda i: (i, 0)), pl.BlockSpec((W, D), lambda i: (0, 0)), row,
                  pl.BlockSpec((1, D), lambda i: (0, 0))],
        out_specs=[row, row],
        out_shape=[jax.ShapeDtypeStruct((S, D), F32), jax.ShapeDtypeStruct((S, D), BF16)],
        compiler_params=_cparams(),
    )(y, wo, xres, g)


def _out_proj_loss(y, wo, xres, tgt, g, name):
    S, W = y.shape
    D = wo.shape[1]
    tm = _row_tile(S, EPI_TILE)

    def body(a_ref, b_ref, r_ref, t_ref, g_ref, dx_ref, dxb_ref, dg_ref, loss_ref):
        xv = r_ref[...] + _dot_nn(a_ref[...], b_ref[...])
        r = lax.rsqrt(jnp.mean(xv * xv, axis=-1, keepdims=True) + RMS_EPS)
        xhat = xv * r
        gv = g_ref[...]
        err = xhat * gv - t_ref[...]
        d = err * (1.0 / D)
        gd = d * gv
        dx = r * (gd - xhat * jnp.mean(gd * xhat, axis=-1, keepdims=True))
        dx_ref[...] = dx
        dxb_ref[...] = dx.astype(BF16)

        @pl.when(pl.program_id(0) == 0)
        def _():
            dg_ref[...] = jnp.zeros(dg_ref.shape, F32)
            loss_ref[...] = jnp.zeros(loss_ref.shape, F32)
        dg_ref[...] += jnp.sum(d * xhat, axis=0, keepdims=True)
        per_tok = jnp.sum(err * err, axis=-1, keepdims=True) * (1.0 / D)
        loss_ref[...] += 0.5 * jnp.sum(per_tok, axis=0, keepdims=True)

    row = pl.BlockSpec((tm, D), lambda i: (i, 0))
    vec = pl.BlockSpec((1, D), lambda i: (0, 0))
    return pl.pallas_call(
        body, name=name, grid=(S // tm,),
        in_specs=[pl.BlockSpec((tm, W), lambda i: (i, 0)), pl.BlockSpec((W, D), lambda i: (0, 0)), row, row, vec],
        out_specs=[row, row, vec, pl.BlockSpec((1, LANE), lambda i: (0, 0))],
        out_shape=[jax.ShapeDtypeStruct((S, D), F32), jax.ShapeDtypeStruct((S, D), BF16),
                   jax.ShapeDtypeStruct((1, D), F32), jax.ShapeDtypeStruct((1, LANE), F32)],
        compiler_params=_cparams(),
    )(y, wo, xres, tgt, g)


def _matmul_nt(parts, wfull, out_rows, name):
    S = parts[0][0].shape[-2]
    tm, tn = _row_tile(S, 2 * MM_TILE if len(parts) <= 2 else MM_TILE), _tile(out_rows, MM_TILE)
    plan, lo = [], 0
    for arr, lead, col0 in parts:
        n_p = arr.shape[-1]
        tk = math.gcd(_tile(n_p, MM_TILE), col0) if col0 else _tile(n_p, MM_TILE)
        steps = n_p // tk * (arr.shape[0] if lead == "stack" else 1)
        plan.append((lead, col0 // tk, tk, lo, lo + steps))
        lo += steps
    nk = lo
    npart = len(parts)

    def body(*refs):
        a_refs, w_refs = refs[:npart], refs[npart:2 * npart]
        o_ref, acc_ref = refs[2 * npart], refs[2 * npart + 1]
        k = pl.program_id(2)

        @pl.when(k == 0)
        def _():
            acc_ref[...] = jnp.zeros(acc_ref.shape, F32)
        for p, (_, _, _, lo_p, hi_p) in enumerate(plan):
            @pl.when((k >= lo_p) & (k < hi_p))
            def _(p=p):
                acc_ref[...] += _dot_nt(a_refs[p][...], w_refs[p][...])

        @pl.when(k == nk - 1)
        def _():
            o_ref[...] = acc_ref[...]

    in_specs, args = [], []
    for (arr, lead, col0), (_, cb, tk, lo_p, hi_p) in zip(parts, plan):
        def kk(k, lo_p=lo_p, hi_p=hi_p):
            return jnp.clip(k - lo_p, 0, hi_p - lo_p - 1)
        if lead is None:
            in_specs.append(pl.BlockSpec((tm, tk), lambda i, j, k, kk=kk: (i, kk(k))))
        elif lead == "stack":
            nkb = arr.shape[-1] // tk
            in_specs.append(pl.BlockSpec((None, tm, tk), lambda i, j, k, kk=kk, nkb=nkb: (kk(k) // nkb, i, kk(k) % nkb)))
        else:
            in_specs.append(pl.BlockSpec((None, tm, tk), lambda i, j, k, kk=kk, lead=lead: (lead, i, kk(k))))
        args.append(arr)
    for (_, cb, tk, lo_p, hi_p) in plan:
        def kk(k, lo_p=lo_p, hi_p=hi_p):
            return jnp.clip(k - lo_p, 0, hi_p - lo_p - 1)
        in_specs.append(pl.BlockSpec((tn, tk), lambda i, j, k, kk=kk, cb=cb: (j, cb + kk(k))))
        args.append(wfull)
    return pl.pallas_call(
        body, name=name, grid=(S // tm, out_rows // tn, nk),
        in_specs=in_specs, out_specs=pl.BlockSpec((tm, tn), lambda i, j, k: (i, j)),
        out_shape=jax.ShapeDtypeStruct((S, out_rows), F32),
        scratch_shapes=[pltpu.VMEM((tm, tn), F32)],
        compiler_params=_cparams(),
    )(*args)


def _matmul_tn(a, parts, total, name, tile_major=False, also_bf16=False, rows=None, after=()):
    S = a.shape[0]
    m0, M = rows or (0, a.shape[1])
    tm = _tile(M, MM_TILE)
    ib = m0 // tm
    nout = 2 if also_bf16 else 1
    outs = None
    for idx, (arr, lead, col0) in enumerate(parts):
        n_p = arr.shape[-1]
        tn = math.gcd(_tile(n_p, MM_TILE), col0) if col0 else _tile(n_p, MM_TILE)
        cb = col0 // tn
        nb = n_p // tn
        if lead == "stack":
            n_p *= arr.shape[0]

        def body(*refs, tn=tn):
            a_ref, b_ref = refs[0], refs[1]
            o_refs = refs[-nout:]
            acc = _dot_tn(a_ref[...], b_ref[...])
            for o_ref in o_refs:
                if tile_major:
                    for t in range(tn // LANE):
                        o_ref[t] = acc[:, LANE * t:LANE * (t + 1)].astype(o_ref.dtype)
                else:
                    o_ref[...] = acc.astype(o_ref.dtype)

        in_specs = [pl.BlockSpec((S, tm), lambda i, j: (0, ib + i), pipeline_mode=pl.Buffered(1))]
        if lead is None:
            in_specs.append(pl.BlockSpec((S, tn), lambda i, j: (0, j)))
        elif lead == "stack":
            in_specs.append(pl.BlockSpec((None, S, tn), lambda i, j, nb=nb: (j // nb, 0, j % nb)))
        else:
            in_specs.append(pl.BlockSpec((None, S, tn), lambda i, j, lead=lead: (lead, 0, j)))
        args = [a, arr]
        aliases = {}
        if outs is not None:
            in_specs += [pl.BlockSpec(memory_space=pl.ANY)] * nout
            args += list(outs)
            aliases = {2 + o: o for o in range(nout)}
        else:
            in_specs += [pl.BlockSpec(memory_space=pl.ANY)] * len(after)
            args += list(after)
        if tile_major:
            out_spec = pl.BlockSpec((tn // LANE, tm, LANE), lambda i, j, cb=cb: (cb + j, i, 0))
            shape = (total // LANE, M, LANE)
        else:
            out_spec = pl.BlockSpec((tm, tn), lambda i, j, cb=cb: (i, cb + j))
            shape = (M, total)
        outs = pl.pallas_call(
            body, name=f"{name}_{idx}", grid=(M // tm, n_p // tn),
            in_specs=in_specs, out_specs=[out_spec] * nout,
            out_shape=[jax.ShapeDtypeStruct(shape, dt) for dt in (F32, BF16)[:nout]],
            input_output_aliases=aliases,
            compiler_params=_cparams(),
        )(*args)
    return tuple(outs) if also_bf16 else outs[0]


def _log_sigmoid(z):
    e = jnp.exp(-jnp.abs(z))
    return jnp.minimum(z, 0.0) - jnp.where(e < 1e-4, e * (1.0 - 0.5 * e), jnp.log(1.0 + e))


def _tri_sum(tri, x):
    hi, mid, lo = _split3(x)
    return _dot_nn(tri, hi) + _dot_nn(tri, mid) + _dot_nn(tri, lo)


def _fox_gate_fwd(fl, bias):
    S = fl.shape[0]

    nb_ = _row_tile(S, ROW_TILE)

    def body(f_ref, b_ref, c_ref):
        ri = lax.broadcasted_iota(jnp.int32, (nb_, nb_), 0)
        ci = lax.broadcasted_iota(jnp.int32, (nb_, nb_), 1)
        tri = jnp.where(ri >= ci, 1.0, 0.0).astype(BF16)
        row = lax.broadcasted_iota(jnp.int32, (nb_, LANE), 0)

        def step(i, carry):
            r0 = pl.multiple_of(i * nb_, nb_)
            t = _tri_sum(tri, _log_sigmoid(f_ref[pl.ds(r0, nb_), :] + b_ref[...])) + carry
            c_ref[pl.ds(r0, nb_), :] = t
            return jnp.sum(jnp.where(row == nb_ - 1, t, 0.0), axis=0, keepdims=True)

        lax.fori_loop(0, S // nb_, step, jnp.zeros((1, LANE), F32))

    vm = pl.BlockSpec(memory_space=pltpu.VMEM)
    return pl.pallas_call(
        body, name="fox_gate_fwd", in_specs=[vm, vm], out_specs=vm,
        out_shape=jax.ShapeDtypeStruct((S, LANE), F32),
        compiler_params=_cparams(),
    )(fl, bias)


def _fox_gate_bwd(fl, bias, dc):
    S = fl.shape[0]

    nb_ = _row_tile(S, ROW_TILE)

    def body(f_ref, b_ref, d_ref, o_ref, db_ref):
        ri = lax.broadcasted_iota(jnp.int32, (nb_, nb_), 0)
        ci = lax.broadcasted_iota(jnp.int32, (nb_, nb_), 1)
        tri = jnp.where(ri <= ci, 1.0, 0.0).astype(BF16)
        row = lax.broadcasted_iota(jnp.int32, (nb_, LANE), 0)
        nt = S // nb_

        def step(ii, carry):
            carry_c, carry_b = carry
            r0 = pl.multiple_of((nt - 1 - ii) * nb_, nb_)
            t = _tri_sum(tri, d_ref[pl.ds(r0, nb_), :]) + carry_c
            dz = t * _sigmoid(-(f_ref[pl.ds(r0, nb_), :] + b_ref[...]))
            o_ref[pl.ds(r0, nb_), :] = dz.astype(BF16)
            first = jnp.sum(jnp.where(row == 0, t, 0.0), axis=0, keepdims=True)
            return first, carry_b + jnp.sum(dz, axis=0, keepdims=True)

        zero = jnp.zeros((1, LANE), F32)
        _, db = lax.fori_loop(0, nt, step, (zero, zero))
        db_ref[...] = db

    vm = pl.BlockSpec(memory_space=pltpu.VMEM)
    return pl.pallas_call(
        body, name="fox_gate_bwd", in_specs=[vm, vm, vm], out_specs=[vm, vm],
        out_shape=[jax.ShapeDtypeStruct((S, LANE), BF16), jax.ShapeDtypeStruct((1, LANE), F32)],
        compiler_params=_cparams(),
    )(fl, bias, dc)


def _bias_lanes(col, lane, e, first):
    o0 = HEAD_DIM * (1 - e)
    hi, mid, lo = _split3(col)
    d0 = o0 if first else o0 + 3
    t = jnp.where((lane >= o0) & (lane < o0 + 6), jnp.ones(lane.shape, BF16), jnp.zeros(lane.shape, BF16))
    t = jnp.where(lane == d0, hi, t)
    t = jnp.where(lane == d0 + 1, mid, t)
    return jnp.where(lane == d0 + 2, lo, t)


def _fox_fwd(qkvg, c, H, gather=()):
    na = len(gather)
    S = qkvg.shape[0]
    W = H * HEAD_DIM
    HP = H // 2
    PP = 2 if HP % 2 == 0 else 1
    NE = 2 * PP
    tq = _row_tile(S, ATT_TILE)
    nq = S // tq
    wb = W // LANE
    scale = HEAD_DIM ** -0.5

    def body(*refs):
        q_ref, k_ref, v_ref, g_ref, c_ref = refs[:5]
        y_ref, o_ref, a_ref = refs[5 + na:8 + na]
        kaug_sc, vaug_sc, qaug_sc, s_sc, mb_sc, m_sc, acc_sc = refs[8 + 2 * na:15 + 2 * na]
        hp, qi = pl.program_id(0), pl.program_id(1)
        if na:
            remote, local = _direct_gather_copies(refs[5:5 + na], refs[8 + na:8 + 2 * na], *refs[15 + 2 * na:])

            @pl.when((hp == 0) & (qi == 0))
            def _():
                for cp in remote + local:
                    cp.start()
        lane = lax.broadcasted_iota(jnp.int32, (tq, LANE), 1)
        own = [lane < HEAD_DIM, lane >= HEAD_DIM]
        rows = lax.broadcasted_iota(jnp.int32, (tq, tq), 0)
        cols = lax.broadcasted_iota(jnp.int32, (tq, tq), 1)

        def bias_lanes(col, e, first):
            return _bias_lanes(col, lane, e % 2, first)

        def head_col(tile, e):
            return jnp.sum(jnp.where(lane == 2 * PP * hp + e, tile, 0.0), axis=1, keepdims=True)

        def tile_of(e):
            return slice(LANE * (e // 2), LANE * (e // 2 + 1))

        @pl.when(qi == 0)
        def _():
            def chunk(i, carry):
                r0 = pl.multiple_of(i * tq, tq)
                cb = c_ref[pl.ds(r0, tq), :]
                for e in range(NE):
                    kb, vb = k_ref[pl.ds(r0, tq), tile_of(e)], v_ref[pl.ds(r0, tq), tile_of(e)]
                    kaug_sc[e, pl.ds(r0, tq), :] = jnp.where(own[e % 2], kb, bias_lanes(-head_col(cb, e), e, False))
                    vaug_sc[e, pl.ds(r0, tq), :] = jnp.where(own[e % 2], vb, jnp.ones((tq, LANE), BF16))
                return carry
            lax.fori_loop(0, nq, chunk, 0)

        crow = c_ref[pl.ds(pl.multiple_of(qi * tq, tq), tq), :]
        ctq = [head_col(crow, e) for e in range(NE)]
        for e in range(NE):
            q = q_ref[:, tile_of(e)] * jnp.asarray(scale, BF16)
            qaug_sc[e] = jnp.where(own[e % 2], q, bias_lanes(ctq[e], e, True))
        m_sc[...] = jnp.full(m_sc.shape, NEG_INF, F32)
        acc_sc[...] = jnp.zeros(acc_sc.shape, F32)

        def scores(blk, slot, masked):
            k0 = pl.multiple_of(blk * tq, tq)
            for e in range(NE):
                s = _dot_nt(qaug_sc[e], kaug_sc[e, pl.ds(k0, tq), :])
                if masked:
                    s = jnp.where(rows >= cols, s, NEG_INF)
                s_sc[slot, e] = s
                mb_sc[slot, e] = jnp.broadcast_to(jnp.max(s, axis=1, keepdims=True), (tq, LANE))

        def accumulate(blk, slot):
            k0 = pl.multiple_of(blk * tq, tq)
            for e in range(NE):
                m_prev = m_sc[e]
                m_new = jnp.maximum(m_prev, mb_sc[slot, e])
                p = jnp.exp(s_sc[slot, e] - jnp.tile(m_new, (1, tq // LANE)))
                acc_sc[e] = jnp.exp(m_prev - m_new) * acc_sc[e] + _dot_nn(p.astype(BF16), vaug_sc[e, pl.ds(k0, tq), :])
                m_sc[e] = m_new

        def block_of(t):
            return jnp.where(t == 0, qi, t - 1)

        scores(qi, 0, True)

        def loop_body(t, carry):
            scores(t, (t + 1) % 2, False)
            accumulate(block_of(t), t % 2)
            return carry

        lax.fori_loop(0, qi, loop_body, 0)
        accumulate(block_of(qi), qi % 2)
        o_e, a_e = [], []
        for e in range(NE):
            acc = acc_sc[e]
            l = pltpu.roll(acc, HEAD_DIM, axis=1)
            o_e.append(acc / l)
            a_e.append(ctq[e] - (m_sc[e] + jnp.log(l)))
        for pp in range(PP):
            o = jnp.where(own[0], o_e[2 * pp], o_e[2 * pp + 1])
            g = g_ref[:, tile_of(2 * pp)].astype(F32)
            y_ref[:, tile_of(2 * pp)] = (o * (g * _sigmoid(g))).astype(BF16)
            o_ref[:, tile_of(2 * pp)] = o.astype(BF16)
            a_ref[pp] = jnp.where(own[0], a_e[2 * pp], a_e[2 * pp + 1])
        if na:
            @pl.when((hp == HP // PP - 1) & (qi == nq - 1))
            def _():
                _wait_all(remote, local)

    any_spec = pl.BlockSpec(memory_space=pl.ANY)
    sems = [pltpu.SemaphoreType.DMA((na, N_DEV - 1)), pltpu.SemaphoreType.DMA((na, N_DEV - 1)),
            pltpu.SemaphoreType.DMA((na,))] if na else []
    wide = PP * LANE
    outs = pl.pallas_call(
        body, name="fox_attn_fwd", grid=(HP // PP, nq),
        in_specs=[pl.BlockSpec((tq, wide), lambda h, i: (i, h)),
                  pl.BlockSpec((S, wide), lambda h, i: (0, wb // PP + h)),
                  pl.BlockSpec((S, wide), lambda h, i: (0, 2 * wb // PP + h)),
                  pl.BlockSpec((tq, wide), lambda h, i: (i, 3 * wb // PP + h)),
                  pl.BlockSpec((S, LANE), lambda h, i: (0, 0))] + [any_spec] * na,
        out_specs=[pl.BlockSpec((tq, wide), lambda h, i: (i, h)),
                   pl.BlockSpec((tq, wide), lambda h, i: (i, h)),
                   pl.BlockSpec((PP, tq, LANE), lambda h, i: (h, i, 0))] + [any_spec] * na,
        out_shape=[jax.ShapeDtypeStruct((S, W), BF16), jax.ShapeDtypeStruct((S, W), BF16),
                   jax.ShapeDtypeStruct((HP, S, LANE), F32)]
        + [jax.ShapeDtypeStruct((N_DEV,) + g.shape, g.dtype) for g in gather],
        scratch_shapes=[pltpu.VMEM((NE, S, LANE), BF16), pltpu.VMEM((NE, S, LANE), BF16),
                        pltpu.VMEM((NE, tq, LANE), BF16), pltpu.VMEM((2, NE, tq, tq), F32),
                        pltpu.VMEM((2, NE, tq, LANE), F32), pltpu.VMEM((NE, tq, LANE), F32),
                        pltpu.VMEM((NE, tq, LANE), F32)] + sems,
        compiler_params=_cparams(),
    )(qkvg, qkvg, qkvg, qkvg, c, *gather)
    return outs[0], outs[1], outs[2], list(outs[3:])


def _fox_out_bwd(dxb, wo, qkvg, o, a, H):
    S, D = dxb.shape
    W = H * HEAD_DIM
    tm, tn = _row_tile(S, EPI_TILE), _tile(W, EPI_TILE)
    npair = tn // LANE
    scale = HEAD_DIM ** -0.5

    def body(dx_ref, w_ref, q_ref, g_ref, o_ref, a_ref, qa_ref, da_ref, dg_ref):
        dy = _dot_nt(dx_ref[...], w_ref[...])
        lane = lax.broadcasted_iota(jnp.int32, (tm, LANE), 1)
        own = [lane < HEAD_DIM, lane >= HEAD_DIM]
        for p in range(npair):
            cols = slice(LANE * p, LANE * (p + 1))
            q = q_ref[:, cols] * jnp.asarray(scale, BF16)
            dyv, g, ov, at = dy[:, cols], g_ref[:, cols].astype(F32), o_ref[:, cols].astype(F32), a_ref[p]
            sg = _sigmoid(g)
            dob = (dyv * (g * sg)).astype(BF16)
            dg_ref[:, cols] = (dyv * ov * (sg * (1.0 + g * (1.0 - sg)))).astype(BF16)
            prod = dob.astype(F32) * ov
            for e in range(2):
                a_col = jnp.max(jnp.where(own[e], at, -jnp.inf), axis=1, keepdims=True)
                d_col = jnp.sum(jnp.where(own[e], prod, 0.0), axis=1, keepdims=True)
                qa_ref[e, :, cols] = jnp.where(own[e], q, _bias_lanes(a_col, lane, e, True))
                da_ref[e, :, cols] = jnp.where(own[e], dob, _bias_lanes(-d_col, lane, e, True))

    blk = pl.BlockSpec((tm, tn), lambda i, j: (i, j))
    pair = pl.BlockSpec((2, tm, tn), lambda i, j: (0, i, j))
    return pl.pallas_call(
        body, name="fox_out_bwd", grid=(S // tm, W // tn),
        in_specs=[pl.BlockSpec((tm, D), lambda i, j: (i, 0)), pl.BlockSpec((tn, D), lambda i, j: (j, 0)),
                  blk, pl.BlockSpec((tm, tn), lambda i, j: (i, 3 * W // tn + j)), blk,
                  pl.BlockSpec((npair, tm, LANE), lambda i, j: (j, i, 0))],
        out_specs=[pair, pair, pl.BlockSpec((None, tm, tn), lambda i, j: (3, i, j))],
        out_shape=[jax.ShapeDtypeStruct((2, S, W), BF16), jax.ShapeDtypeStruct((2, S, W), BF16),
                   jax.ShapeDtypeStruct((4, S, W), BF16)],
        compiler_params=_cparams(),
    )(dxb, wo, qkvg, qkvg, o, a)


def _fox_bwd(qaug, doaug, qkv, c, dqkvg, H, scatter=(), scatter_specs=()):
    na = len(scatter)
    S = qkv.shape[0]
    W = H * HEAD_DIM
    HP = H // 2
    tq = _row_tile(S, ATT_TILE)
    nq = S // tq
    wb = W // LANE
    scale = HEAD_DIM ** -0.5

    def body(*refs):
        qa_ref, da_ref, k_ref, v_ref, c_ref = refs[:5]
        out_ref, dcr_ref, dcc_ref = refs[6 + na:9 + na]
        dq_sc, dk_sc, dv_sc = refs[9 + 2 * na:12 + 2 * na]
        hp, kj = pl.program_id(0), pl.program_id(1)
        if na:
            remote = _direct_scatter_copies(refs[5:5 + na], refs[9 + na:9 + 2 * na], scatter_specs,
                                            *refs[12 + 2 * na:])

            @pl.when((hp == 0) & (kj == 0))
            def _():
                for cp in remote:
                    cp.start()
        lane = lax.broadcasted_iota(jnp.int32, (tq, LANE), 1)
        own = [lane < HEAD_DIM, lane >= HEAD_DIM]
        rows = lax.broadcasted_iota(jnp.int32, (tq, tq), 0)
        cols = lax.broadcasted_iota(jnp.int32, (tq, tq), 1)

        @pl.when(kj == 0)
        def _():
            dq_sc[...] = jnp.zeros(dq_sc.shape, F32)

        @pl.when((kj == 0) & (hp == 0))
        def _():
            dcr_ref[...] = jnp.zeros(dcr_ref.shape, F32)
            dcc_ref[...] = jnp.zeros(dcc_ref.shape, F32)

        kblk, vblk, cblk = k_ref[...], v_ref[...], c_ref[...]
        one, zero = jnp.ones((tq, LANE), BF16), jnp.zeros((tq, LANE), BF16)
        ka, va = [], []
        for e in range(2):
            o0 = HEAD_DIM * (1 - e)
            c_col = jnp.sum(jnp.where(lane == 2 * hp + e, cblk, 0.0), axis=1, keepdims=True)
            ka.append(jnp.where(own[e], kblk, _bias_lanes(-c_col, lane, e, False)))
            va.append(jnp.where(own[e], vblk, jnp.where((lane >= o0) & (lane < o0 + 3), one, zero)))
        dk_sc[...] = jnp.zeros(dk_sc.shape, F32)
        dv_sc[...] = jnp.zeros(dv_sc.shape, F32)

        def step(i, masked):
            r0 = pl.multiple_of(i * tq, tq)
            for e in range(2):
                qa = qa_ref[e, pl.ds(r0, tq), :]
                da = da_ref[e, pl.ds(r0, tq), :]
                p = jnp.exp(_dot_nt(qa, ka[e]))
                if masked:
                    p = jnp.where(rows >= cols, p, 0.0)
                ds = p * _dot_nt(da, va[e])
                pb, dsb = p.astype(BF16), ds.astype(BF16)
                dv_sc[e] += _dot_tn(pb, da)
                dk_sc[e] += _dot_tn(dsb, qa)
                dq_sc[e, pl.ds(r0, tq), :] += _dot_nn(dsb, ka[e])

        step(kj, True)

        def loop_body(i, carry):
            step(i, False)
            return carry

        lax.fori_loop(kj + 1, nq, loop_body, 0)
        k0 = pl.multiple_of(kj * tq, tq)
        out_ref[1, pl.ds(k0, tq), :] = jnp.where(own[0], dk_sc[0], dk_sc[1]).astype(BF16)
        out_ref[2, pl.ds(k0, tq), :] = jnp.where(own[0], dv_sc[0], dv_sc[1]).astype(BF16)

        def put_lane(ref, r0, e, tile, src_lane):
            col = jnp.sum(jnp.where(lane == src_lane, tile, 0.0), axis=1, keepdims=True)
            ref[pl.ds(r0, tq), :] = jnp.where(lane == 2 * hp + e, col, ref[pl.ds(r0, tq), :])

        for e in range(2):
            put_lane(dcc_ref, k0, e, dk_sc[e], HEAD_DIM * (1 - e) + 3)

        @pl.when(kj == nq - 1)
        def _():
            def chunk(i, carry):
                r0 = pl.multiple_of(i * tq, tq)
                d0, d1 = dq_sc[0, pl.ds(r0, tq), :], dq_sc[1, pl.ds(r0, tq), :]
                out_ref[0, pl.ds(r0, tq), :] = (jnp.where(own[0], d0, d1) * scale).astype(BF16)
                put_lane(dcr_ref, r0, 0, d0, HEAD_DIM)
                put_lane(dcr_ref, r0, 1, d1, 0)
                return carry
            lax.fori_loop(0, nq, chunk, 0)

        if na:
            @pl.when((hp == HP - 1) & (kj == nq - 1))
            def _():
                _wait_all(remote)

    pair = pl.BlockSpec((2, S, LANE), lambda h, j: (0, 0, h))
    vec = pl.BlockSpec((S, LANE), lambda h, j: (0, 0))
    any_spec = pl.BlockSpec(memory_space=pl.ANY)
    sems = [pltpu.SemaphoreType.DMA((na, N_DEV - 1)), pltpu.SemaphoreType.DMA((na, N_DEV - 1))] if na else []
    outs = pl.pallas_call(
        body, name="fox_attn_bwd", grid=(HP, nq),
        in_specs=[pair, pair,
                  pl.BlockSpec((tq, LANE), lambda h, j: (j, wb + h)),
                  pl.BlockSpec((tq, LANE), lambda h, j: (j, 2 * wb + h)),
                  pl.BlockSpec((tq, LANE), lambda h, j: (j, 0))] + [any_spec] * (na + 1),
        out_specs=[pl.BlockSpec((3, S, LANE), lambda h, j: (0, 0, h)), vec, vec] + [any_spec] * na,
        out_shape=[jax.ShapeDtypeStruct(dqkvg.shape, BF16), jax.ShapeDtypeStruct((S, LANE), F32),
                   jax.ShapeDtypeStruct((S, LANE), F32)]
        + [jax.ShapeDtypeStruct((N_DEV - 1,) + _scatter_block_shape(g, s), g.dtype)
           for g, s in zip(scatter, scatter_specs)],
        scratch_shapes=[pltpu.VMEM((2, S, LANE), F32), pltpu.VMEM((2, tq, LANE), F32),
                        pltpu.VMEM((2, tq, LANE), F32)] + sems,
        input_output_aliases={5 + na: 0},
        compiler_params=_cparams(),
    )(qaug, doaug, qkv, qkv, c, *scatter, dqkvg)
    return outs[0], outs[1], outs[2], list(outs[3:])


def _swa_pick(blk, half, lane):
    b = blk.astype(F32)
    r = pltpu.roll(b, HEAD_DIM, axis=1)
    return jnp.where(jnp.logical_xor(lane < HEAD_DIM, half == 1), b, r).astype(BF16)


def _swa_stack(t, lane, G):
    pieces = []
    z = jnp.zeros((SWA_BLOCK, LANE), t.dtype)
    for j in range(G // 2):
        tile = t[:, LANE * j:LANE * (j + 1)]
        pieces += [jnp.where(lane < HEAD_DIM, tile, z), jnp.where(lane < HEAD_DIM, z, tile)]
    return jnp.concatenate(pieces, axis=0)


def _swa_unstack(st, lane, G):
    tiles = []
    for j in range(G // 2):
        a = st[2 * j * SWA_BLOCK:(2 * j + 1) * SWA_BLOCK]
        b = st[(2 * j + 1) * SWA_BLOCK:(2 * j + 2) * SWA_BLOCK]
        tiles.append(jnp.where(lane < HEAD_DIM, a, b))
    return jnp.concatenate(tiles, axis=1)


def _swa_mask_bias(G):
    R = G * SWA_BLOCK
    t_loc = jnp.arange(R)[:, None] % SWA_BLOCK
    j_loc = jnp.arange(2 * SWA_BLOCK)[None, :]
    diff = t_loc + SWA_BLOCK - j_loc
    band = (diff >= 0) & (diff < SWA_BLOCK)
    return jnp.stack([jnp.where(band & (j_loc >= SWA_BLOCK), 0.0, NEG_INF),
                      jnp.where(band, 0.0, NEG_INF)]).astype(F32)


def _swa_scores(q, kp, kc, vp, vc, srow, bias, half, head0, G):
    lane = lax.broadcasted_iota(jnp.int32, (SWA_BLOCK, LANE), 1)
    kk = jnp.concatenate([_swa_pick(kp, half, lane), _swa_pick(kc, half, lane)], axis=0)
    vv = jnp.concatenate([_swa_pick(vp, half, lane), _swa_pick(vc, half, lane)], axis=0)
    qstack = _swa_stack(q, lane, G) * jnp.asarray(HEAD_DIM ** -0.5, BF16)
    s = _dot_nt(qstack, kk) + bias
    R = G * SWA_BLOCK
    lane1 = lax.broadcasted_iota(jnp.int32, (1, LANE), 1)
    sink = jnp.concatenate(
        [jnp.broadcast_to(jnp.sum(jnp.where(lane1 == head0 + g, srow, 0.0), axis=1, keepdims=True), (SWA_BLOCK, LANE))
         for g in range(G)], axis=0)
    m = jnp.maximum(jnp.broadcast_to(jnp.max(s, axis=1, keepdims=True), (R, LANE)), sink)
    e = jnp.exp(s - jnp.tile(m, (1, 2)))
    es = jnp.exp(sink - m)
    inv = 1.0 / (jnp.broadcast_to(jnp.sum(e, axis=1, keepdims=True), (R, LANE)) + es)
    return qstack, kk, vv, e * jnp.tile(inv, (1, 2)), es * inv, lane


def _swa_fwd(q, kv, gate, sinks, mask_bias, HQ, HKV):
    S = q.shape[0]
    G = HQ // HKV
    WQ, KVW = HQ * HEAD_DIM, HKV * HEAD_DIM
    nb = S // SWA_BLOCK
    GW = G * HEAD_DIM
    kb, vb = 0, KVW // LANE
    NH = min(HKV, 4)
    NP = NH // 2

    def body(q_ref, kp_ref, kc_ref, vp_ref, vc_ref, g_ref, sink_ref, b_ref, y_ref, o_ref):
        grp = pl.program_id(0)
        for hh in range(NH):
            cols, kt = slice(GW * hh, GW * (hh + 1)), slice(LANE * (hh // 2), LANE * (hh // 2 + 1))
            _, _, vv, p, _, lane = _swa_scores(q_ref[:, cols], kp_ref[:, kt], kc_ref[:, kt], vp_ref[:, kt], vc_ref[:, kt],
                                               sink_ref[...], b_ref[0], hh % 2, (NH * grp + hh) * G, G)
            o = _swa_unstack(_dot_nn(p.astype(BF16), vv), lane, G)
            g = g_ref[:, cols].astype(F32)
            y_ref[:, cols] = (o * (g * _sigmoid(g))).astype(BF16)
            o_ref[:, cols] = o.astype(BF16)

    blk = lambda cb, prev: pl.BlockSpec(
        (SWA_BLOCK, NP * LANE), lambda h, n, cb=cb, prev=prev: (jnp.maximum(n - prev, 0), cb // NP + h))
    qspec = pl.BlockSpec((SWA_BLOCK, NH * GW), lambda h, n: (n, h))
    return pl.pallas_call(
        body, name="swa_attn_fwd", grid=(HKV // NH, nb),
        in_specs=[qspec, blk(kb, 1), blk(kb, 0), blk(vb, 1), blk(vb, 0), qspec,
                  pl.BlockSpec((1, LANE), lambda h, n: (0, 0)),
                  pl.BlockSpec((1, G * SWA_BLOCK, 2 * SWA_BLOCK), lambda h, n: (jnp.minimum(n, 1), 0, 0))],
        out_specs=[qspec, qspec],
        out_shape=[jax.ShapeDtypeStruct((S, WQ), BF16), jax.ShapeDtypeStruct((S, WQ), BF16)],
        compiler_params=_cparams(),
    )(q, kv, kv, kv, kv, gate, sinks, mask_bias)


def _swa_bwd(q, kv, dy, gate, o, sinks, tables, mask_bias, HQ, HKV):
    S = q.shape[0]
    G = HQ // HKV
    WQ, KVW = HQ * HEAD_DIM, HKV * HEAD_DIM
    nb = S // SWA_BLOCK
    GW = G * HEAD_DIM
    R = G * SWA_BLOCK
    kb, vb = 0, KVW // LANE
    scale = HEAD_DIM ** -0.5
    NH = min(HKV, 4)
    NP = NH // 2
    assert G == 8

    def body(q_ref, kp_ref, kc_ref, vp_ref, vc_ref, dy_ref, g_ref, o_ref, sink_ref, t_ref, b_ref,
             dqg_ref, dkv_ref, dsink_ref, carry_sc):
        grp, n = pl.program_id(0), pl.program_id(1)

        @pl.when(n == 0)
        def _():
            carry_sc[...] = jnp.zeros(carry_sc.shape, F32)
            dsink_ref[...] = jnp.zeros(dsink_ref.shape, F32)

        @pl.when(n < nb)
        def _():
            t0, t1, t2 = (jnp.tile(t_ref[i], (1, GW // LANE)) for i in range(3))
            for hh in range(NH):
                cols, kt = slice(GW * hh, GW * (hh + 1)), slice(LANE * (hh // 2), LANE * (hh // 2 + 1))
                qstack, kk, vv, p, psink, lane = _swa_scores(
                    q_ref[:, cols], kp_ref[:, kt], kc_ref[:, kt], vp_ref[:, kt], vc_ref[:, kt], sink_ref[...], b_ref[0],
                    hh % 2, (NH * grp + hh) * G, G)
                dyv, g, ov = dy_ref[:, cols], g_ref[:, cols].astype(F32), o_ref[:, cols].astype(F32)
                sg = _sigmoid(g)
                dob = (dyv * (g * sg)).astype(BF16)
                dqg_ref[1, :, cols] = (dyv * ov * (sg * (1.0 + g * (1.0 - sg)))).astype(BF16)
                prod = dob.astype(F32) * ov
                dparts = []
                for j in range(G // 2):
                    tile = prod[:, LANE * j:LANE * (j + 1)]
                    for sel in (jnp.where(lane < HEAD_DIM, tile, 0.0), jnp.where(lane < HEAD_DIM, 0.0, tile)):
                        dparts.append(jnp.broadcast_to(jnp.sum(sel, axis=1, keepdims=True), (SWA_BLOCK, LANE)))
                delta = jnp.concatenate(dparts, axis=0)
                dostack = _swa_stack(dob, lane, G)
                ds = p * (_dot_nt(dostack, vv) - jnp.tile(delta, (1, 2)))
                dsb, pb = ds.astype(BF16), p.astype(BF16)
                dq = _swa_unstack(_dot_nn(dsb, kk), lane, G) * scale
                dq = dq * t0 + pltpu.roll(dq * t1, ROT_DIM // 2, axis=1) + pltpu.roll(dq * t2, GW - ROT_DIM // 2, axis=1)
                dqg_ref[0, :, cols] = dq.astype(BF16)
                dkk = _dot_tn(dsb, qstack)
                dvv = _dot_tn(pb, dostack)
                dkk = dkk + pltpu.roll(dkk, HEAD_DIM, axis=1)
                dvv = dvv + pltpu.roll(dvv, HEAD_DIM, axis=1)
                lane2 = lax.broadcasted_iota(jnp.int32, (2 * SWA_BLOCK, LANE), 1)
                comb = jnp.where(lane2 < HEAD_DIM, dkk, dvv)
                dkv_ref[hh] = carry_sc[hh] + comb[:SWA_BLOCK]
                carry_sc[hh] = comb[SWA_BLOCK:]
                sk = psink * delta
                rows = [-jnp.sum(sk[g_ * SWA_BLOCK:(g_ + 1) * SWA_BLOCK], axis=0, keepdims=True) for g_ in range(G)]
                dsink_ref[hh] += jnp.concatenate(rows, axis=0)

        @pl.when(n == nb)
        def _():
            dkv_ref[...] = carry_sc[...]

    cl = lambda n: jnp.minimum(n, nb - 1)
    blk = lambda cb, prev: pl.BlockSpec(
        (SWA_BLOCK, NP * LANE), lambda h, n, cb=cb, prev=prev: (jnp.maximum(cl(n) - prev, 0), cb // NP + h))
    qspec = pl.BlockSpec((SWA_BLOCK, NH * GW), lambda h, n: (cl(n), h))
    return pl.pallas_call(
        body, name="swa_attn_bwd", grid=(HKV // NH, nb + 1),
        in_specs=[qspec, blk(kb, 1), blk(kb, 0), blk(vb, 1), blk(vb, 0), qspec, qspec, qspec,
                  pl.BlockSpec((1, LANE), lambda h, n: (0, 0)),
                  pl.BlockSpec((3, SWA_BLOCK, LANE), lambda h, n: (0, cl(n), 0)),
                  pl.BlockSpec((1, R, 2 * SWA_BLOCK), lambda h, n: (jnp.minimum(n, 1), 0, 0))],
        out_specs=[pl.BlockSpec((2, SWA_BLOCK, NH * GW), lambda h, n: (0, cl(n), h)),
                   pl.BlockSpec((NH, SWA_BLOCK, LANE), lambda h, n: (h, jnp.maximum(n - 1, 0), 0)),
                   pl.BlockSpec((NH, 8, LANE), lambda h, n: (h, 0, 0))],
        out_shape=[jax.ShapeDtypeStruct((2, S, WQ), BF16), jax.ShapeDtypeStruct((HKV, S, LANE), F32),
                   jax.ShapeDtypeStruct((HKV, 8, LANE), F32)],
        scratch_shapes=[pltpu.VMEM((NH, SWA_BLOCK, LANE), F32)],
        compiler_params=_cparams(),
    )(q, kv, kv, kv, kv, dy, gate, o, sinks, tables, mask_bias)


def _swa_dkv_finish(dkv, tables):
    HKV, S, _ = dkv.shape
    KVW = HKV * HEAD_DIM
    tm = _row_tile(S, EPI_TILE)
    npair = HKV // 2

    def body(d_ref, t_ref, o_ref):
        lane = lax.broadcasted_iota(jnp.int32, (tm, LANE), 1)
        lo = lane < HEAD_DIM
        for p in range(npair):
            a, b = d_ref[2 * p], d_ref[2 * p + 1]
            tk = jnp.where(lo, a, pltpu.roll(b, HEAD_DIM, axis=1))
            tv = jnp.where(lo, pltpu.roll(a, HEAD_DIM, axis=1), b)
            tk = (tk * t_ref[0] + pltpu.roll(tk * t_ref[1], ROT_DIM // 2, axis=1)
                  + pltpu.roll(tk * t_ref[2], LANE - ROT_DIM // 2, axis=1))
            o_ref[:, LANE * p:LANE * (p + 1)] = tk.astype(BF16)
            o_ref[:, KVW + LANE * p:KVW + LANE * (p + 1)] = tv.astype(BF16)

    return pl.pallas_call(
        body, name="swa_dkv_finish", grid=(S // tm,),
        in_specs=[pl.BlockSpec((HKV, tm, LANE), lambda i: (0, i, 0)), pl.BlockSpec((3, tm, LANE), lambda i: (0, i, 0))],
        out_specs=pl.BlockSpec((tm, 2 * KVW), lambda i: (i, 0)),
        out_shape=jax.ShapeDtypeStruct((S, 2 * KVW), BF16),
        compiler_params=_cparams(),
    )(dkv, tables)


def _rope_tables(S, width):
    half = ROT_DIM // 2
    pos = jnp.arange(S, dtype=F32)
    inv_freq = ROPE_THETA ** (-jnp.arange(half, dtype=F32) / half)
    ang = pos[:, None] * inv_freq[None, :]
    cos, sin = jnp.cos(ang), jnp.sin(ang)
    one = jnp.ones((S, HEAD_DIM - ROT_DIM), F32)
    zero = jnp.zeros((S, HEAD_DIM - ROT_DIM), F32)
    zh = jnp.zeros((S, half), F32)
    t0 = jnp.concatenate([cos, cos, one], axis=1)
    t1 = jnp.concatenate([-sin, zh, zero], axis=1)
    t2 = jnp.concatenate([zh, sin, zero], axis=1)
    return jnp.stack([jnp.tile(t, (1, width // HEAD_DIM)) for t in (t0, t1, t2)])


def _pad_rows(v, row, total_rows=8):
    return jnp.pad(v, ((row, total_rows - row - v.shape[0]), (0, 0)))


def _pad_lanes(v, off, width):
    return jnp.pad(v, ((0, 0), (off, width - off - v.shape[1])))


def kernel(x, norm_g, fox_w_in, fox_b_f, fox_w_out, swa_w_in, swa_sinks, swa_w_out, final_g, loss_target, m_norm_g, m_fox_w_in, m_fox_b_f, m_fox_w_out, m_swa_w_in, m_swa_sinks, m_swa_w_out, m_final_g, v_norm_g, v_fox_w_in, v_fox_b_f, v_fox_w_out, v_swa_w_in, v_swa_sinks, v_swa_w_out, v_final_g):
    S, D = x.shape[1], x.shape[2]
    H = fox_b_f.shape[1]
    W = H * HEAD_DIM
    wf = fox_w_in.shape[2]
    ws = swa_w_in.shape[2]
    HQ = swa_sinks.shape[1]
    WQ = HQ * HEAD_DIM
    KVW = (ws * N_DEV - 2 * WQ) // 2
    HKV = KVW // HEAD_DIM
    rows_o = fox_w_out.shape[1]
    assert wf * N_DEV == 4 * W + H and rows_o * N_DEV == W and H <= LANE and HQ <= LANE
    me = _my_index()

    _, sw_f, np_f = _slab_geom(wf)
    _, sw_s, np_s = _slab_geom(ws)

    def slab(w2d, w, sw):
        return jnp.pad(w2d.astype(BF16), ((0, 0), (0, sw - w)))

    (fi_all,) = _all_gather([slab(fox_w_in[0], wf, sw_f)])
    w_fi = _assemble(fi_all, wf)
    later = [slab(swa_w_in[0], ws, sw_s), fox_w_out[0].astype(BF16), swa_w_out[0].astype(BF16)]

    x0 = x[0]
    g0, g1, gf = norm_g[0:1], norm_g[1:2], final_g[None, :]
    bias = _pad_lanes(fox_b_f, 0, LANE)
    sinks = _pad_lanes(swa_sinks, 0, LANE)
    tab_k = _rope_tables(S, LANE)
    mask_bias = _swa_mask_bias(HQ // HKV)

    h0 = _rmsnorm_fwd(x0, g0, "rmsnorm0")
    qkv0 = _proj(h0, w_fi, 0, 4 * W, BF16, "fox_in_qkvg")
    fl = _proj(h0, w_fi, 4 * W, LANE, F32, "fox_in_f")
    c = _fox_gate_fwd(fl, bias)
    y0, o0, a0, (si_all, fo_all, so_all) = _fox_fwd(qkv0, c, H, gather=later)
    w_si = _assemble(si_all, ws)
    w_fo = fo_all.reshape(W, D)
    w_so = so_all.reshape(WQ, D)
    x1, h1 = _out_proj_norm(y0, w_fo, x0, g1, "fox_out")

    q1 = _proj(h1, w_si, 0, WQ, BF16, "swa_in_q", rope=(tab_k, WQ))
    kv1 = _proj(h1, w_si, WQ, 2 * KVW, BF16, "swa_in_kv", rope=(tab_k, KVW))
    gate1 = _proj(h1, w_si, WQ + 2 * KVW, WQ, BF16, "swa_in_gate")
    y1, o1 = _swa_fwd(q1, kv1, gate1, sinks, mask_bias, HQ, HKV)
    dx2, dx2b, dgf, loss_p = _out_proj_loss(y1, w_so, x1, loss_target[0], gf, "swa_out_loss")

    dy1 = _matmul_nt([(dx2b, None, 0)], w_so, WQ, "swa_out_bwd")
    g_so, g_so_h = _matmul_tn(y1, [(dx2b, None, 0)], D, "swa_out_wgrad", also_bf16=True)
    dqg1, dkv1, dsink = _swa_bwd(q1, kv1, dy1, gate1, o1, sinks, tab_k, mask_bias, HQ, HKV)
    dkv1f = _swa_dkv_finish(dkv1, tab_k)
    parts1 = [(dqg1, 0, 0), (dkv1f, None, WQ), (dqg1, 1, WQ + 2 * KVW)]
    g_si, g_si_h = _matmul_tn(h1, parts1, np_s, "swa_in_wgrad", tile_major=True, also_bf16=True)
    dh1 = _matmul_nt(parts1, w_si, D, "swa_in_bwd")
    dx1, dx1b, dg1 = _rmsnorm_bwd(dh1, x1, g1, dx2, "rmsnorm1_bwd")

    qaug0, doaug0, dqkvg0 = _fox_out_bwd(dx1b, w_fo, qkv0, o0, a0, H)
    g_fo, g_fo_h = _matmul_tn(y0, [(dx1b, None, 0)], D, "fox_out_wgrad", also_bf16=True)
    early_specs = [("col", ws), ("row", rows_o), ("row", rows_o)]
    dqkvg0, dcr, dcc, early_recv = _fox_bwd(qaug0, doaug0, qkv0, c, dqkvg0, H, scatter=[g_si_h, g_fo_h, g_so_h],
                                           scatter_specs=early_specs)
    dfl, dbf = _fox_gate_bwd(fl, bias, dcr - dcc)
    parts0 = [(dqkvg0, "stack", 0), (dfl, None, 4 * W)]
    spec_fi = ("col", wf)
    fi_halves, token = [], None
    for half in range(2):
        g_fi, g_fi_h = _matmul_tn(h0, parts0, np_f, f"fox_in_wgrad_rows{half}", tile_major=True, also_bf16=True,
                                  rows=(half * (D // 2), D // 2), after=() if token is None else (token,))
        fi_sems, fi_src, fi_land, token = _scatter_start(g_fi_h, spec_fi)
        fi_halves.append((g_fi, fi_sems, fi_src, fi_land))
    parts0[-1] = (dfl + token[0, 0].astype(BF16), None, 4 * W)
    dh0 = _matmul_nt(parts0, w_fi, D, "fox_in_bwd")
    dx0, _, dg0 = _rmsnorm_bwd(dh0, x0, g0, dx1, "rmsnorm0_bwd")

    red_si, gw_fo, gw_so = [_final_sum8(g_, r_, s_)
                            for g_, r_, s_ in zip([g_si, g_fo, g_so], early_recv, early_specs)]
    gt_si = lax.dynamic_slice(red_si, ((ws * me) % LANE, 0), (ws, D))

    def t_in(p):
        return jnp.swapaxes(p[0], 0, 1)

    def t_out(t):
        return jnp.swapaxes(t, 0, 1)[None]

    P = D
    dsink_v = dsink[:, :, 0].reshape(1, HQ)
    row3 = _pad_lanes(dbf[:, :H], 0, P) + _pad_lanes(dsink_v, LANE, P) + _pad_lanes(loss_p[:, :1], 2 * LANE, P)
    pack = _pad_rows(dg0, 0) + _pad_rows(dg1, 1) + _pad_rows(dgf, 2) + _pad_rows(row3, 3)

    d_fo, m_fo, v_fo = _adamw(fox_w_out[0], gw_fo, m_fox_w_out[0], v_fox_w_out[0], "adamw_fox_out")
    d_si, m_si, v_si = _adamw(t_in(swa_w_in), gt_si, t_in(m_swa_w_in), t_in(v_swa_w_in), "adamw_swa_in")
    d_so, m_so, v_so = _adamw(swa_w_out[0], gw_so, m_swa_w_out[0], v_swa_w_out[0], "adamw_swa_out")
    behind, red_fi = [dx0, pack, d_fo, d_si, d_so], []
    for g_fi, fi_sems, fi_src, fi_land in fi_halves:
        recv_fi = _scatter_wait(fi_sems, fi_src, fi_land, spec_fi, after=behind)
        red_fi.append(_final_sum8(g_fi, recv_fi, spec_fi))
        behind = [recv_fi]
    red_fi = jnp.concatenate(red_fi, axis=1)
    gt_fi = lax.dynamic_slice(red_fi, ((wf * me) % LANE, 0), (wf, D))

    d_fi, m_fi, v_fi = _adamw(t_in(fox_w_in), gt_fi, t_in(m_fox_w_in), t_in(v_fox_w_in), "adamw_fox_in")
    gw_si, d_si, m_si, v_si = [t_out(t)[0] for t in (gt_si, d_si, m_si, v_si)]
    gw_fi, d_fi, m_fi, v_fi = [t_out(t)[0] for t in (gt_fi, d_fi, m_fi, v_fi)]

    tot = _all_reduce_small(pack, after=recv_fi)
    loss = tot[3, 2 * LANE]
    g_norm = tot[0:2]
    g_final = tot[2]
    g_bf = tot[3:4, 0:H]
    g_sinks = tot[3:4, LANE:LANE + HQ]

    def small_pack(ng, fg, bf, sk):
        r3 = _pad_lanes(bf, 0, P) + _pad_lanes(sk, LANE, P)
        return _pad_rows(ng, 0) + _pad_rows(fg[None, :], 2) + _pad_rows(r3, 3)

    sd, sm, sv = _adamw(small_pack(norm_g, final_g, fox_b_f, swa_sinks), tot,
                        small_pack(m_norm_g, m_final_g, m_fox_b_f, m_swa_sinks),
                        small_pack(v_norm_g, v_final_g, v_fox_b_f, v_swa_sinks), "adamw_small")

    def unpack(t):
        return t[0:2], t[3:4, 0:H], t[3:4, LANE:LANE + HQ], t[2]

    def group(small, fi, fo, si, so):
        ng, bf, sk, fg = unpack(small)
        return (ng, fi[None], bf, fo[None], si[None], sk, so[None], fg)

    grads = (g_norm, gw_fi[None], g_bf, gw_fo[None], gw_si[None], g_sinks, gw_so[None], g_final)
    return (loss, dx0[None], *grads, *group(sd, d_fi, d_fo, d_si, d_so),
            *group(sm, m_fi, m_fo, m_si, m_so), *group(sv, v_fi, v_fo, v_si, v_so))
```

```python
import math

import jax
import jax.numpy as jnp
from jax import lax
from jax.experimental import pallas as pl
from jax.experimental.pallas import tpu as pltpu

F32 = jnp.float32
BF16 = jnp.bfloat16
MESH = pl.DeviceIdType.MESH

N_DEV = 8
LANE = 128
HEAD_DIM = 64
SWA_BLOCK = 128
NEG_INF = -1e30
RMS_EPS = 1e-6
ROPE_THETA = 500000.0
ROT_DIM = HEAD_DIM // 4
ADAM_LR, ADAM_B1, ADAM_B2, ADAM_EPS, ADAM_WD, ADAM_STEP = 0.001, 0.9, 0.999, 1e-08, 0.01, 10
VMEM_LIMIT = 56 * 1024 * 1024
MM_TILE = 1024
ATT_TILE = 512
EPI_TILE = 512
ROW_TILE = 256
ADAM_TILE_ELEMS = 3 << 18


def _cparams(**kw):
    return pltpu.CompilerParams(vmem_limit_bytes=VMEM_LIMIT, **kw)


def _tile(n, cap):
    if n <= cap:
        return n
    t = (cap // LANE) * LANE
    while t > LANE and n % t:
        t -= LANE
    assert n % t == 0, (n, cap)
    return t


def _row_tile(n, cap):
    t = min(n, cap)
    while n % t:
        t //= 2
    return t


def _dot_nn(a, b):
    return jnp.dot(a, b, preferred_element_type=F32)


def _dot_nt(a, b):
    return lax.dot_general(a, b, (((1,), (1,)), ((), ())), preferred_element_type=F32)


def _dot_tn(a, b):
    return lax.dot_general(a, b, (((0,), (0,)), ((), ())), preferred_element_type=F32)


def _split3(x):
    hi = x.astype(BF16)
    r1 = x - hi.astype(F32)
    mid = r1.astype(BF16)
    return hi, mid, (r1 - mid.astype(F32)).astype(BF16)


def _sigmoid(g):
    return 1.0 / (1.0 + jnp.exp(-g))


def _slab_geom(w):
    starts = [w * i for i in range(N_DEV)]
    aligned = [LANE * (s // LANE) for s in starts]
    offs = [s - a for s, a in zip(starts, aligned)]
    sw = LANE * (-(-(max(offs) + w) // LANE))
    return aligned, sw, aligned[-1] + sw


def _my_index():
    return 4 * lax.axis_index("x") + 2 * lax.axis_index("y") + lax.axis_index("c")


def _all_gather(arrs):
    n = len(arrs)

    def body(*refs):
        ins, outs = refs[:n], refs[n:2 * n]
        send_sems, recv_sems, local_sems = refs[2 * n:]
        x, y, c = lax.axis_index("x"), lax.axis_index("y"), lax.axis_index("c")
        me, sib = (x, y, c), (x, y, 1 - c)
        chips = [(1 - x, y), (x, 1 - y), (1 - x, 1 - y)]

        def idx(px, py, pc):
            return 4 * px + 2 * py + pc

        def copy(a, k, block, to, src=None):
            dst = outs[a].at[idx(*block)]
            return pltpu.make_async_remote_copy(
                src_ref=dst if src is None else src, dst_ref=dst,
                send_sem=send_sems.at[a, k], recv_sem=recv_sems.at[a, k],
                device_id=to, device_id_type=MESH)

        mine = [pltpu.make_async_copy(ins[a], outs[a].at[idx(*me)], local_sems.at[a]) for a in range(n)]
        for m in mine:
            m.start()
        first = []
        for a in range(n):
            first.append(copy(a, 0, me, sib, src=ins[a]))
            for j, chip in enumerate(chips):
                first.append(copy(a, 1 + j, me, (*chip, c), src=ins[a]))
        for cp in first:
            cp.start()
        passed = []
        for j, chip in enumerate(chips):
            for a in range(n):
                copy(a, 1 + j, (*chip, c), me).wait_recv()
                p = copy(a, 4 + j, (*chip, c), sib)
                p.start()
                passed.append(p)
        for a in range(n):
            copy(a, 0, sib, me).wait_recv()
        for j, chip in enumerate(chips):
            for a in range(n):
                copy(a, 4 + j, (*chip, 1 - c), me).wait_recv()
        for cp in first + passed:
            cp.wait_send()
        for m in mine:
            m.wait()

    any_spec = pl.BlockSpec(memory_space=pl.ANY)
    return pl.pallas_call(
        body, name="weights_all_gather",
        out_shape=[jax.ShapeDtypeStruct((N_DEV,) + a.shape, a.dtype) for a in arrs],
        in_specs=[any_spec] * n, out_specs=[any_spec] * n,
        scratch_shapes=[pltpu.SemaphoreType.DMA((n, 7)), pltpu.SemaphoreType.DMA((n, 7)),
                        pltpu.SemaphoreType.DMA((n,))],
    )(*arrs)


def _rs_windows(specs):
    def window(ref, spec, blk):
        kind, n = spec
        if kind == "col":
            _, sw, _ = _slab_geom(n)
            return ref.at[pl.ds((n * blk) // LANE, sw // LANE)]
        start = pl.multiple_of(n * blk, n)
        return ref.at[pl.ds(start, n), :]
    return window


def _peer(k):
    x, y, c = lax.axis_index("x"), lax.axis_index("y"), lax.axis_index("c")
    return (x ^ (k >> 2), y ^ ((k >> 1) & 1), c ^ (k & 1))


def _direct_gather_copies(ins, outs, send_sems, recv_sems, local_sems):
    me = _my_index()
    remote, local = [], []
    for a, (src, dst) in enumerate(zip(ins, outs)):
        local.append(pltpu.make_async_copy(src, dst.at[me], local_sems.at[a]))
        for k in range(1, N_DEV):
            remote.append(pltpu.make_async_remote_copy(
                src_ref=src, dst_ref=dst.at[me], send_sem=send_sems.at[a, k - 1], recv_sem=recv_sems.at[a, k - 1],
                device_id=_peer(k), device_id_type=MESH))
    return remote, local


def _direct_scatter_copies(ins, outs, specs, send_sems, recv_sems):
    window = _rs_windows(specs)
    remote = []
    for a, (src, dst) in enumerate(zip(ins, outs)):
        for k in range(1, N_DEV):
            px, py, pc = _peer(k)
            remote.append(pltpu.make_async_remote_copy(
                src_ref=window(src, specs[a], 4 * px + 2 * py + pc), dst_ref=dst.at[k - 1],
                send_sem=send_sems.at[a, k - 1], recv_sem=recv_sems.at[a, k - 1],
                device_id=(px, py, pc), device_id_type=MESH))
    return remote


def _scatter_block_shape(g, spec):
    kind, w = spec
    return (_slab_geom(w)[1] // LANE, g.shape[1], LANE) if kind == "col" else (w, g.shape[1])


def _wait_all(remote, local=()):
    for cp in remote:
        cp.wait_recv()
    for cp in remote:
        cp.wait_send()
    for cp in local:
        cp.wait()


def _scatter_start(g, spec):
    blk = _scatter_block_shape(g, spec)
    window = _rs_windows([spec])
    npeer = N_DEV - 1

    def body(g_ref, land_ref, *rest):
        sems = rest[:2 * npeer]
        token = rest[2 * npeer + 2]
        for cp in _peer_block_copies(g_ref, land_ref, spec, window, sems[:npeer], sems[npeer:]):
            cp.start()
        token[...] = jnp.zeros(token.shape, token.dtype)

    hbm = pl.BlockSpec(memory_space=pltpu.HBM)
    sem = pl.BlockSpec(memory_space=pltpu.SEMAPHORE)
    land = lax.empty((npeer,) + blk, g.dtype)
    outs = pl.pallas_call(
        body, name="grads_scatter_start",
        out_shape=(pltpu.SemaphoreType.DMA(()),) * (2 * npeer)
        + (pltpu.HBM(g.shape, g.dtype), pltpu.HBM(land.shape, land.dtype), jax.ShapeDtypeStruct((8, LANE), F32)),
        in_specs=(hbm, hbm),
        out_specs=(sem,) * (2 * npeer) + (hbm, hbm, pl.BlockSpec(memory_space=pltpu.VMEM)),
        input_output_aliases={0: 2 * npeer, 1: 2 * npeer + 1},
        compiler_params=pltpu.CompilerParams(has_side_effects=pltpu.SideEffectType.DATAFLOW_SIDE_EFFECTING),
    )(pltpu.with_memory_space_constraint(g, pltpu.HBM), pltpu.with_memory_space_constraint(land, pltpu.HBM))
    return outs[:2 * npeer], outs[2 * npeer], outs[2 * npeer + 1], outs[2 * npeer + 2]


def _peer_block_copies(g_ref, land_ref, spec, window, send_sems, recv_sems):
    copies = []
    for k in range(1, N_DEV):
        px, py, pc = _peer(k)
        copies.append(pltpu.make_async_remote_copy(
            src_ref=window(g_ref, spec, 4 * px + 2 * py + pc), dst_ref=land_ref.at[k - 1],
            send_sem=send_sems[k - 1], recv_sem=recv_sems[k - 1], device_id=(px, py, pc), device_id_type=MESH))
    return copies


def _scatter_wait(sems, g_thru, land_thru, spec, after):
    window = _rs_windows([spec])
    npeer = N_DEV - 1

    def body(g_ref, land_ref, *rest):
        s = rest[:2 * npeer]
        copies = _peer_block_copies(g_ref, land_ref, spec, window, s[:npeer], s[npeer:])
        for cp in copies:
            cp.wait_send()
        for cp in copies:
            cp.wait_recv()

    hbm = pl.BlockSpec(memory_space=pltpu.HBM)
    sem = pl.BlockSpec(memory_space=pltpu.SEMAPHORE)
    return pl.pallas_call(
        body, name="grads_scatter_wait",
        out_shape=(pltpu.HBM(g_thru.shape, g_thru.dtype), pltpu.HBM(land_thru.shape, land_thru.dtype)),
        in_specs=(hbm, hbm) + (sem,) * (2 * npeer) + (pl.BlockSpec(memory_space=pl.ANY),) * len(after),
        out_specs=(hbm, hbm), input_output_aliases={0: 0, 1: 1},
        compiler_params=pltpu.CompilerParams(has_side_effects=pltpu.SideEffectType.DATAFLOW_SIDE_EFFECTING),
    )(g_thru, land_thru, *sems, *after)[1]


def _final_sum8(g, recv, spec):
    kind, n = spec
    me = _my_index()
    offs = jnp.stack([(n * me) // LANE if kind == "col" else me]).astype(jnp.int32)
    if kind == "col":
        _, T, M, _ = recv.shape
        grid = (T,)
        in_specs = [pl.BlockSpec((1, M, LANE), lambda t, o: (o[0] + t, 0, 0)),
                    pl.BlockSpec((N_DEV - 1, 1, M, LANE), lambda t, o: (0, t, 0, 0))]
        out_spec = pl.BlockSpec((LANE, M), lambda t, o: (t, 0))
        out_shape = jax.ShapeDtypeStruct((T * LANE, M), F32)
    else:
        _, nrow, C = recv.shape
        grid = (1,)
        in_specs = [pl.BlockSpec((nrow, C), lambda t, o: (o[0], 0)),
                    pl.BlockSpec((N_DEV - 1, nrow, C), lambda t, o: (0, 0, 0))]
        out_spec = pl.BlockSpec((nrow, C), lambda t, o: (0, 0))
        out_shape = jax.ShapeDtypeStruct((nrow, C), F32)

    def body(o_ref, g_ref, r_ref, out_ref):
        acc = g_ref[0] if kind == "col" else g_ref[...]
        for k in range(N_DEV - 1):
            acc = acc + (r_ref[k, 0] if kind == "col" else r_ref[k]).astype(F32)
        out_ref[...] = acc.T if kind == "col" else acc

    return pl.pallas_call(
        body, name="grads_final_sum8",
        grid_spec=pltpu.PrefetchScalarGridSpec(num_scalar_prefetch=1, grid=grid, in_specs=in_specs,
                                               out_specs=out_spec),
        out_shape=out_shape, compiler_params=_cparams(),
    )(offs, g, recv)


def _all_reduce_small(pack, after):
    R, P = pack.shape

    def body(x_ref, after_ref, o_ref, gat_ref, send_sems, recv_sems):
        x, y, c = lax.axis_index("x"), lax.axis_index("y"), lax.axis_index("c")
        me = 4 * x + 2 * y + c
        gat_ref[me] = x_ref[...]
        copies = []
        for k in range(1, N_DEV):
            peer = (x ^ (k >> 2), y ^ ((k >> 1) & 1), c ^ (k & 1))
            copies.append(pltpu.make_async_remote_copy(
                src_ref=x_ref, dst_ref=gat_ref.at[me],
                send_sem=send_sems.at[k - 1], recv_sem=recv_sems.at[k - 1],
                device_id=peer, device_id_type=MESH))
        for cp in copies:
            cp.start()
        for cp in copies:
            cp.wait_recv()
        for cp in copies:
            cp.wait_send()
        acc = gat_ref[0]
        for d in range(1, N_DEV):
            acc = acc + gat_ref[d]
        o_ref[...] = acc

    vm = pl.BlockSpec(memory_space=pltpu.VMEM)
    return pl.pallas_call(
        body, name="small_all_reduce",
        out_shape=jax.ShapeDtypeStruct((R, P), F32),
        in_specs=[vm, pl.BlockSpec(memory_space=pl.ANY)], out_specs=vm,
        scratch_shapes=[pltpu.VMEM((N_DEV, R, P), F32),
                        pltpu.SemaphoreType.DMA((N_DEV - 1,)), pltpu.SemaphoreType.DMA((N_DEV - 1,))],
    )(pack, after)


def _assemble(slabs, w):
    aligned, sw, total = _slab_geom(w)
    K = slabs.shape[1]
    tr = _row_tile(K, ROW_TILE)

    def body(s_ref, o_ref):
        o_ref[...] = jnp.zeros(o_ref.shape, BF16)
        for i in range(N_DEV):
            a, off = aligned[i], w * i - aligned[i]
            x = s_ref[i]
            if off:
                x = pltpu.roll(x, off, axis=1)
            o_ref[:, a:a + sw] = o_ref[:, a:a + sw] + x

    return pl.pallas_call(
        body, name="assemble_w_in", grid=(K // tr,),
        in_specs=[pl.BlockSpec((N_DEV, tr, sw), lambda i: (0, i, 0))],
        out_specs=pl.BlockSpec((tr, total), lambda i: (i, 0)),
        out_shape=jax.ShapeDtypeStruct((K, total), BF16),
        compiler_params=_cparams(),
    )(slabs)


def _rmsnorm_fwd(x, g, name):
    S, D = x.shape
    tm = _row_tile(S, ROW_TILE)

    def body(x_ref, g_ref, h_ref):
        xv = x_ref[...]
        r = lax.rsqrt(jnp.mean(xv * xv, axis=-1, keepdims=True) + RMS_EPS)
        h_ref[...] = ((xv * r) * g_ref[...]).astype(BF16)

    return pl.pallas_call(
        body, name=name, grid=(S // tm,),
        in_specs=[pl.BlockSpec((tm, D), lambda i: (i, 0)), pl.BlockSpec((1, D), lambda i: (0, 0))],
        out_specs=pl.BlockSpec((tm, D), lambda i: (i, 0)),
        out_shape=jax.ShapeDtypeStruct((S, D), BF16),
        compiler_params=_cparams(),
    )(x, g)


def _rmsnorm_bwd(dh, x, g, dres, name):
    S, D = x.shape
    tm = _row_tile(S, ROW_TILE)

    def body(dh_ref, x_ref, g_ref, dr_ref, dx_ref, dxb_ref, dg_ref):
        xv = x_ref[...]
        r = lax.rsqrt(jnp.mean(xv * xv, axis=-1, keepdims=True) + RMS_EPS)
        xhat = xv * r
        d = dh_ref[...]
        gd = d * g_ref[...]
        dx = r * (gd - xhat * jnp.mean(gd * xhat, axis=-1, keepdims=True)) + dr_ref[...]
        dx_ref[...] = dx
        dxb_ref[...] = dx.astype(BF16)

        @pl.when(pl.program_id(0) == 0)
        def _():
            dg_ref[...] = jnp.zeros(dg_ref.shape, F32)
        dg_ref[...] += jnp.sum(d * xhat, axis=0, keepdims=True)

    row = pl.BlockSpec((tm, D), lambda i: (i, 0))
    vec = pl.BlockSpec((1, D), lambda i: (0, 0))
    return pl.pallas_call(
        body, name=name, grid=(S // tm,),
        in_specs=[row, row, vec, row], out_specs=[row, row, vec],
        out_shape=[jax.ShapeDtypeStruct((S, D), F32), jax.ShapeDtypeStruct((S, D), BF16),
                   jax.ShapeDtypeStruct((1, D), F32)],
        compiler_params=_cparams(),
    )(dh, x, g, dres)


def _adamw(w, g, m, v, name):
    R, C = w.shape
    steps = pl.cdiv(R * C, ADAM_TILE_ELEMS)
    tr = R if steps == 1 else pl.cdiv(pl.cdiv(R, steps), 8) * 8
    c1 = 1.0 - ADAM_B1 ** ADAM_STEP
    c2 = 1.0 - ADAM_B2 ** ADAM_STEP

    def body(w_ref, g_ref, m_ref, v_ref, d_ref, nm_ref, nv_ref):
        gv = g_ref[...]
        nm = ADAM_B1 * m_ref[...] + (1.0 - ADAM_B1) * gv
        nv = ADAM_B2 * v_ref[...] + (1.0 - ADAM_B2) * (gv * gv)
        d_ref[...] = -ADAM_LR * ((nm / c1) / (jnp.sqrt(nv / c2) + ADAM_EPS) + ADAM_WD * w_ref[...])
        nm_ref[...] = nm
        nv_ref[...] = nv

    spec = pl.BlockSpec((tr, C), lambda i: (i, 0))
    return pl.pallas_call(
        body, name=name, grid=(pl.cdiv(R, tr),),
        in_specs=[spec] * 4, out_specs=[spec] * 3,
        out_shape=[jax.ShapeDtypeStruct((R, C), F32)] * 3,
        compiler_params=_cparams(),
    )(w, g, m, v)


def _proj(h, wfull, col0, ncols, out_dtype, name, rope=None):
    S, K = h.shape
    tm = _row_tile(S, MM_TILE)
    tn = math.gcd(_tile(ncols, MM_TILE), col0) if col0 else _tile(ncols, MM_TILE)
    if rope is not None:
        tn = _tile(math.gcd(ncols, rope[1]), MM_TILE)
    assert ncols % tn == 0 and col0 % tn == 0
    cb = col0 // tn

    def body(*refs):
        if rope is None:
            a_ref, b_ref, o_ref = refs
        else:
            a_ref, b_ref, t_ref, o_ref = refs
        acc = _dot_nn(a_ref[...], b_ref[...])
        if rope is not None:
            t0, t1, t2 = (jnp.tile(t_ref[i], (1, tn // LANE)) for i in range(3))
            roped = (acc * t0 + pltpu.roll(acc, tn - ROT_DIM // 2, axis=1) * t1
                     + pltpu.roll(acc, ROT_DIM // 2, axis=1) * t2)
            acc = jnp.where(pl.program_id(1) < rope[1] // tn, roped, acc)
        o_ref[...] = acc.astype(out_dtype)

    in_specs = [pl.BlockSpec((tm, K), lambda i, j: (i, 0)), pl.BlockSpec((K, tn), lambda i, j: (0, cb + j))]
    args = [h, wfull]
    if rope is not None:
        in_specs.append(pl.BlockSpec((3, tm, LANE), lambda i, j: (0, i, 0)))
        args.append(rope[0])
    return pl.pallas_call(
        body, name=name, grid=(S // tm, ncols // tn),
        in_specs=in_specs, out_specs=pl.BlockSpec((tm, tn), lambda i, j: (i, j)),
        out_shape=jax.ShapeDtypeStruct((S, ncols), out_dtype),
        compiler_params=_cparams(),
    )(*args)


def _out_proj_norm(y, wo, xres, g, name):
    S, W = y.shape
    D = wo.shape[1]
    tm = _row_tile(S, EPI_TILE)

    def body(a_ref, b_ref, r_ref, g_ref, x_ref, h_ref):
        xv = r_ref[...] + _dot_nn(a_ref[...], b_ref[...])
        x_ref[...] = xv
        r = lax.rsqrt(jnp.mean(xv * xv, axis=-1, keepdims=True) + RMS_EPS)
        h_ref[...] = ((xv * r) * g_ref[...]).astype(BF16)

    row = pl.BlockSpec((tm, D), lambda i: (i, 0))
    return pl.pallas_call(
        body, name=name, grid=(S // tm,),
        in_specs=[pl.BlockSpec((tm, W), lambda i: (i, 0)), pl.BlockSpec((W, D), lambda i: (0, 0)), row,
                  pl.BlockSpec((1, D), lambda i: (0, 0))],
        out_specs=[row, row],
        out_shape=[jax.ShapeDtypeStruct((S, D), F32), jax.ShapeDtypeStruct((S, D), BF16)],
        compiler_params=_cparams(),
    )(y, wo, xres, g)


def _out_proj_loss(y, wo, xres, tgt, g, name):
    S, W = y.shape
    D = wo.shape[1]
    tm = _row_tile(S, EPI_TILE)

    def body(a_ref, b_ref, r_ref, t_ref, g_ref, dx_ref, dxb_ref, dg_ref, loss_ref):
        xv = r_ref[...] + _dot_nn(a_ref[...], b_ref[...])
        r = lax.rsqrt(jnp.mean(xv * xv, axis=-1, keepdims=True) + RMS_EPS)
        xhat = xv * r
        gv = g_ref[...]
        err = xhat * gv - t_ref[...]
        d = err * (1.0 / D)
        gd = d * gv
        dx = r * (gd - xhat * jnp.mean(gd * xhat, axis=-1, keepdims=True))
        dx_ref[...] = dx
        dxb_ref[...] = dx.astype(BF16)

        @pl.when(pl.program_id(0) == 0)
        def _():
            dg_ref[...] = jnp.zeros(dg_ref.shape, F32)
            loss_ref[...] = jnp.zeros(loss_ref.shape, F32)
        dg_ref[...] += jnp.sum(d * xhat, axis=0, keepdims=True)
        per_tok = jnp.sum(err * err, axis=-1, keepdims=True) * (1.0 / D)
        loss_ref[...] += 0.5 * jnp.sum(per_tok, axis=0, keepdims=True)

    row = pl.BlockSpec((tm, D), lambda i: (i, 0))
    vec = pl.BlockSpec((1, D), lambda i: (0, 0))
    return pl.pallas_call(
        body, name=name, grid=(S // tm,),
        in_specs=[pl.BlockSpec((tm, W), lambda i: (i, 0)), pl.BlockSpec((W, D), lambda i: (0, 0)), row, row, vec],
        out_specs=[row, row, vec, pl.BlockSpec((1, LANE), lambda i: (0, 0))],
        out_shape=[jax.ShapeDtypeStruct((S, D), F32), jax.ShapeDtypeStruct((S, D), BF16),
                   jax.ShapeDtypeStruct((1, D), F32), jax.ShapeDtypeStruct((1, LANE), F32)],
        compiler_params=_cparams(),
    )(y, wo, xres, tgt, g)


def _matmul_nt(parts, wfull, out_rows, name):
    S = parts[0][0].shape[-2]
    tm, tn = _row_tile(S, 2 * MM_TILE), _tile(out_rows, MM_TILE)
    plan, lo = [], 0
    for arr, lead, col0 in parts:
        n_p = arr.shape[-1]
        tk = _tile(n_p, 2 * MM_TILE if len(parts) <= 2 else MM_TILE)
        tk = math.gcd(tk, col0) if col0 else tk
        steps = n_p // tk * (arr.shape[0] if lead == "stack" else 1)
        plan.append((lead, col0 // tk, tk, lo, lo + steps))
        lo += steps
    nk = lo
    npart = len(parts)

    def body(*refs):
        a_refs, w_refs, o_ref = refs[:npart], refs[npart:2 * npart], refs[2 * npart]
        k = pl.program_id(2)
        for p, (_, _, _, lo_p, hi_p) in enumerate(plan):
            if lo_p == 0:
                @pl.when(k == 0)
                def _(p=p):
                    o_ref[...] = _dot_nt(a_refs[p][...], w_refs[p][...])
                lo_p = 1
            if hi_p > lo_p:
                @pl.when((k >= lo_p) & (k < hi_p))
                def _(p=p):
                    o_ref[...] += _dot_nt(a_refs[p][...], w_refs[p][...])

    in_specs, args = [], []
    for (arr, lead, col0), (_, cb, tk, lo_p, hi_p) in zip(parts, plan):
        def kk(k, lo_p=lo_p, hi_p=hi_p):
            return jnp.clip(k - lo_p, 0, hi_p - lo_p - 1)
        if lead is None:
            in_specs.append(pl.BlockSpec((tm, tk), lambda i, j, k, kk=kk: (i, kk(k))))
        elif lead == "stack":
            nkb = arr.shape[-1] // tk
            in_specs.append(pl.BlockSpec((None, tm, tk), lambda i, j, k, kk=kk, nkb=nkb: (kk(k) // nkb, i, kk(k) % nkb)))
        else:
            in_specs.append(pl.BlockSpec((None, tm, tk), lambda i, j, k, kk=kk, lead=lead: (lead, i, kk(k))))
        args.append(arr)
    for (_, cb, tk, lo_p, hi_p) in plan:
        def kk(k, lo_p=lo_p, hi_p=hi_p):
            return jnp.clip(k - lo_p, 0, hi_p - lo_p - 1)
        in_specs.append(pl.BlockSpec((tn, tk), lambda i, j, k, kk=kk, cb=cb: (j, cb + kk(k))))
        args.append(wfull)
    return pl.pallas_call(
        body, name=name, grid=(S // tm, out_rows // tn, nk),
        in_specs=in_specs, out_specs=pl.BlockSpec((tm, tn), lambda i, j, k: (i, j)),
        out_shape=jax.ShapeDtypeStruct((S, out_rows), F32),
        compiler_params=_cparams(),
    )(*args)


def _matmul_tn(a, parts, total, name, tile_major=False, also_bf16=False, rows=None, after=()):
    S = a.shape[0]
    m0, M = rows or (0, a.shape[1])
    tm = _tile(M, MM_TILE)
    ib = m0 // tm
    nout = 2 if also_bf16 else 1
    outs = None
    for idx, (arr, lead, col0) in enumerate(parts):
        n_p = arr.shape[-1]
        tn = math.gcd(_tile(n_p, MM_TILE), col0) if col0 else _tile(n_p, MM_TILE)
        cb = col0 // tn
        nb = n_p // tn
        if lead == "stack":
            n_p *= arr.shape[0]

        def body(*refs, tn=tn):
            a_ref, b_ref = refs[0], refs[1]
            o_refs = refs[-nout:]
            acc = _dot_tn(a_ref[...], b_ref[...])
            for o_ref in o_refs:
                if tile_major:
                    for t in range(tn // LANE):
                        o_ref[t] = acc[:, LANE * t:LANE * (t + 1)].astype(o_ref.dtype)
                else:
                    o_ref[...] = acc.astype(o_ref.dtype)

        in_specs = [pl.BlockSpec((S, tm), lambda i, j: (0, ib + i), pipeline_mode=pl.Buffered(1))]
        if lead is None:
            in_specs.append(pl.BlockSpec((S, tn), lambda i, j: (0, j)))
        elif lead == "stack":
            in_specs.append(pl.BlockSpec((None, S, tn), lambda i, j, nb=nb: (j // nb, 0, j % nb)))
        else:
            in_specs.append(pl.BlockSpec((None, S, tn), lambda i, j, lead=lead: (lead, 0, j)))
        args = [a, arr]
        aliases = {}
        if outs is not None:
            in_specs += [pl.BlockSpec(memory_space=pl.ANY)] * nout
            args += list(outs)
            aliases = {2 + o: o for o in range(nout)}
        else:
            in_specs += [pl.BlockSpec(memory_space=pl.ANY)] * len(after)
            args += list(after)
        if tile_major:
            out_spec = pl.BlockSpec((tn // LANE, tm, LANE), lambda i, j, cb=cb: (cb + j, i, 0))
            shape = (total // LANE, M, LANE)
        else:
            out_spec = pl.BlockSpec((tm, tn), lambda i, j, cb=cb: (i, cb + j))
            shape = (M, total)
        outs = pl.pallas_call(
            body, name=f"{name}_{idx}", grid=(M // tm, n_p // tn),
            in_specs=in_specs, out_specs=[out_spec] * nout,
            out_shape=[jax.ShapeDtypeStruct(shape, dt) for dt in (F32, BF16)[:nout]],
            input_output_aliases=aliases,
            compiler_params=_cparams(),
        )(*args)
    return tuple(outs) if also_bf16 else outs[0]


def _log_sigmoid(z):
    e = jnp.exp(-jnp.abs(z))
    return jnp.minimum(z, 0.0) - jnp.where(e < 1e-4, e * (1.0 - 0.5 * e), jnp.log(1.0 + e))


def _tri_sum(tri, x):
    hi, mid, lo = _split3(x)
    return _dot_nn(tri, hi) + _dot_nn(tri, mid) + _dot_nn(tri, lo)


def _fox_gate_fwd(fl, bias):
    S = fl.shape[0]

    nb_ = _row_tile(S, ROW_TILE)

    def body(f_ref, b_ref, c_ref):
        ri = lax.broadcasted_iota(jnp.int32, (nb_, nb_), 0)
        ci = lax.broadcasted_iota(jnp.int32, (nb_, nb_), 1)
        tri = jnp.where(ri >= ci, 1.0, 0.0).astype(BF16)
        row = lax.broadcasted_iota(jnp.int32, (nb_, LANE), 0)

        def step(i, carry):
            r0 = pl.multiple_of(i * nb_, nb_)
            t = _tri_sum(tri, _log_sigmoid(f_ref[pl.ds(r0, nb_), :] + b_ref[...])) + carry
            c_ref[pl.ds(r0, nb_), :] = t
            return jnp.sum(jnp.where(row == nb_ - 1, t, 0.0), axis=0, keepdims=True)

        lax.fori_loop(0, S // nb_, step, jnp.zeros((1, LANE), F32))

    vm = pl.BlockSpec(memory_space=pltpu.VMEM)
    return pl.pallas_call(
        body, name="fox_gate_fwd", in_specs=[vm, vm], out_specs=vm,
        out_shape=jax.ShapeDtypeStruct((S, LANE), F32),
        compiler_params=_cparams(),
    )(fl, bias)


def _fox_gate_bwd(fl, bias, dc):
    S = fl.shape[0]

    nb_ = _row_tile(S, ROW_TILE)

    def body(f_ref, b_ref, d_ref, o_ref, db_ref):
        ri = lax.broadcasted_iota(jnp.int32, (nb_, nb_), 0)
        ci = lax.broadcasted_iota(jnp.int32, (nb_, nb_), 1)
        tri = jnp.where(ri <= ci, 1.0, 0.0).astype(BF16)
        row = lax.broadcasted_iota(jnp.int32, (nb_, LANE), 0)
        nt = S // nb_

        def step(ii, carry):
            carry_c, carry_b = carry
            r0 = pl.multiple_of((nt - 1 - ii) * nb_, nb_)
            t = _tri_sum(tri, d_ref[pl.ds(r0, nb_), :]) + carry_c
            dz = t * _sigmoid(-(f_ref[pl.ds(r0, nb_), :] + b_ref[...]))
            o_ref[pl.ds(r0, nb_), :] = dz.astype(BF16)
            first = jnp.sum(jnp.where(row == 0, t, 0.0), axis=0, keepdims=True)
            return first, carry_b + jnp.sum(dz, axis=0, keepdims=True)

        zero = jnp.zeros((1, LANE), F32)
        _, db = lax.fori_loop(0, nt, step, (zero, zero))
        db_ref[...] = db

    vm = pl.BlockSpec(memory_space=pltpu.VMEM)
    return pl.pallas_call(
        body, name="fox_gate_bwd", in_specs=[vm, vm, vm], out_specs=[vm, vm],
        out_shape=[jax.ShapeDtypeStruct((S, LANE), BF16), jax.ShapeDtypeStruct((1, LANE), F32)],
        compiler_params=_cparams(),
    )(fl, bias, dc)


def _bias_lanes(col, lane, e, first):
    o0 = HEAD_DIM * (1 - e)
    hi, mid, lo = _split3(col)
    d0 = o0 if first else o0 + 3
    t = jnp.where((lane >= o0) & (lane < o0 + 6), jnp.ones(lane.shape, BF16), jnp.zeros(lane.shape, BF16))
    t = jnp.where(lane == d0, hi, t)
    t = jnp.where(lane == d0 + 1, mid, t)
    return jnp.where(lane == d0 + 2, lo, t)


def _fox_fwd(qkvg, c, H, gather=()):
    na = len(gather)
    S = qkvg.shape[0]
    W = H * HEAD_DIM
    HP = H // 2
    PP = 2 if HP % 2 == 0 else 1
    NE = 2 * PP
    tq = _row_tile(S, ATT_TILE)
    nq = S // tq
    wb = W // LANE
    scale = HEAD_DIM ** -0.5

    def body(*refs):
        q_ref, k_ref, v_ref, g_ref, c_ref = refs[:5]
        y_ref, o_ref, a_ref = refs[5 + na:8 + na]
        kaug_sc, vaug_sc, qaug_sc, s_sc, mb_sc, m_sc, acc_sc = refs[8 + 2 * na:15 + 2 * na]
        hp, qi = pl.program_id(0), pl.program_id(1)
        if na:
            remote, local = _direct_gather_copies(refs[5:5 + na], refs[8 + na:8 + 2 * na], *refs[15 + 2 * na:])

            @pl.when((hp == 0) & (qi == 0))
            def _():
                for cp in remote + local:
                    cp.start()
        lane = lax.broadcasted_iota(jnp.int32, (tq, LANE), 1)
        own = [lane < HEAD_DIM, lane >= HEAD_DIM]
        rows = lax.broadcasted_iota(jnp.int32, (tq, tq), 0)
        cols = lax.broadcasted_iota(jnp.int32, (tq, tq), 1)

        def bias_lanes(col, e, first):
            return _bias_lanes(col, lane, e % 2, first)

        def head_col(tile, e):
            return jnp.sum(jnp.where(lane == 2 * PP * hp + e, tile, 0.0), axis=1, keepdims=True)

        def tile_of(e):
            return slice(LANE * (e // 2), LANE * (e // 2 + 1))

        @pl.when(qi == 0)
        def _():
            def chunk(i, carry):
                r0 = pl.multiple_of(i * tq, tq)
                cb = c_ref[pl.ds(r0, tq), :]
                for e in range(NE):
                    kb, vb = k_ref[pl.ds(r0, tq), tile_of(e)], v_ref[pl.ds(r0, tq), tile_of(e)]
                    kaug_sc[e, pl.ds(r0, tq), :] = jnp.where(own[e % 2], kb, bias_lanes(-head_col(cb, e), e, False))
                    vaug_sc[e, pl.ds(r0, tq), :] = jnp.where(own[e % 2], vb, jnp.ones((tq, LANE), BF16))
                return carry
            lax.fori_loop(0, nq, chunk, 0)

        crow = c_ref[pl.ds(pl.multiple_of(qi * tq, tq), tq), :]
        ctq = [head_col(crow, e) for e in range(NE)]
        for e in range(NE):
            q = q_ref[:, tile_of(e)] * jnp.asarray(scale, BF16)
            qaug_sc[e] = jnp.where(own[e % 2], q, bias_lanes(ctq[e], e, True))
        m_sc[...] = jnp.full(m_sc.shape, NEG_INF, F32)
        acc_sc[...] = jnp.zeros(acc_sc.shape, F32)

        def scores(blk, slot, masked):
            k0 = pl.multiple_of(blk * tq, tq)
            for e in range(NE):
                s = _dot_nt(qaug_sc[e], kaug_sc[e, pl.ds(k0, tq), :])
                if masked:
                    s = jnp.where(rows >= cols, s, NEG_INF)
                s_sc[slot, e] = s
                mb_sc[slot, e] = jnp.broadcast_to(jnp.max(s, axis=1, keepdims=True), (tq, LANE))

        def accumulate(blk, slot):
            k0 = pl.multiple_of(blk * tq, tq)
            for e in range(NE):
                m_prev = m_sc[e]
                m_new = jnp.maximum(m_prev, mb_sc[slot, e])
                p = jnp.exp(s_sc[slot, e] - jnp.tile(m_new, (1, tq // LANE)))
                acc_sc[e] = jnp.exp(m_prev - m_new) * acc_sc[e] + _dot_nn(p.astype(BF16), vaug_sc[e, pl.ds(k0, tq), :])
                m_sc[e] = m_new

        def block_of(t):
            return jnp.where(t == 0, qi, t - 1)

        scores(qi, 0, True)

        def loop_body(t, carry):
            scores(t, (t + 1) % 2, False)
            accumulate(block_of(t), t % 2)
            return carry

        lax.fori_loop(0, qi, loop_body, 0)
        accumulate(block_of(qi), qi % 2)
        o_e, a_e = [], []
        for e in range(NE):
            acc = acc_sc[e]
            l = pltpu.roll(acc, HEAD_DIM, axis=1)
            o_e.append(acc / l)
            a_e.append(ctq[e] - (m_sc[e] + jnp.log(l)))
        for pp in range(PP):
            o = jnp.where(own[0], o_e[2 * pp], o_e[2 * pp + 1])
            g = g_ref[:, tile_of(2 * pp)].astype(F32)
            y_ref[:, tile_of(2 * pp)] = (o * (g * _sigmoid(g))).astype(BF16)
            o_ref[:, tile_of(2 * pp)] = o.astype(BF16)
            a_ref[pp] = jnp.where(own[0], a_e[2 * pp], a_e[2 * pp + 1])
        if na:
            @pl.when((hp == HP // PP - 1) & (qi == nq - 1))
            def _():
                _wait_all(remote, local)

    any_spec = pl.BlockSpec(memory_space=pl.ANY)
    sems = [pltpu.SemaphoreType.DMA((na, N_DEV - 1)), pltpu.SemaphoreType.DMA((na, N_DEV - 1)),
            pltpu.SemaphoreType.DMA((na,))] if na else []
    wide = PP * LANE
    outs = pl.pallas_call(
        body, name="fox_attn_fwd", grid=(HP // PP, nq),
        in_specs=[pl.BlockSpec((tq, wide), lambda h, i: (i, h)),
                  pl.BlockSpec((S, wide), lambda h, i: (0, wb // PP + h)),
                  pl.BlockSpec((S, wide), lambda h, i: (0, 2 * wb // PP + h)),
                  pl.BlockSpec((tq, wide), lambda h, i: (i, 3 * wb // PP + h)),
                  pl.BlockSpec((S, LANE), lambda h, i: (0, 0))] + [any_spec] * na,
        out_specs=[pl.BlockSpec((tq, wide), lambda h, i: (i, h)),
                   pl.BlockSpec((tq, wide), lambda h, i: (i, h)),
                   pl.BlockSpec((PP, tq, LANE), lambda h, i: (h, i, 0))] + [any_spec] * na,
        out_shape=[jax.ShapeDtypeStruct((S, W), BF16), jax.ShapeDtypeStruct((S, W), BF16),
                   jax.ShapeDtypeStruct((HP, S, LANE), F32)]
        + [jax.ShapeDtypeStruct((N_DEV,) + g.shape, g.dtype) for g in gather],
        scratch_shapes=[pltpu.VMEM((NE, S, LANE), BF16), pltpu.VMEM((NE, S, LANE), BF16),
                        pltpu.VMEM((NE, tq, LANE), BF16), pltpu.VMEM((2, NE, tq, tq), F32),
                        pltpu.VMEM((2, NE, tq, LANE), F32), pltpu.VMEM((NE, tq, LANE), F32),
                        pltpu.VMEM((NE, tq, LANE), F32)] + sems,
        compiler_params=_cparams(),
    )(qkvg, qkvg, qkvg, qkvg, c, *gather)
    return outs[0], outs[1], outs[2], list(outs[3:])


def _fox_out_bwd(dxb, wo, qkvg, o, a, H):
    S, D = dxb.shape
    W = H * HEAD_DIM
    tm, tn = _row_tile(S, EPI_TILE), _tile(W, EPI_TILE)
    npair = tn // LANE
    scale = HEAD_DIM ** -0.5

    def body(dx_ref, w_ref, q_ref, g_ref, o_ref, a_ref, qa_ref, da_ref, dg_ref):
        dy = _dot_nt(dx_ref[...], w_ref[...])
        lane = lax.broadcasted_iota(jnp.int32, (tm, LANE), 1)
        own = [lane < HEAD_DIM, lane >= HEAD_DIM]
        for p in range(npair):
            cols = slice(LANE * p, LANE * (p + 1))
            q = q_ref[:, cols] * jnp.asarray(scale, BF16)
            dyv, g, ov, at = dy[:, cols], g_ref[:, cols].astype(F32), o_ref[:, cols].astype(F32), a_ref[p]
            sg = _sigmoid(g)
            dob = (dyv * (g * sg)).astype(BF16)
            dg_ref[:, cols] = (dyv * ov * (sg * (1.0 + g * (1.0 - sg)))).astype(BF16)
            prod = dob.astype(F32) * ov
            for e in range(2):
                a_col = jnp.max(jnp.where(own[e], at, -jnp.inf), axis=1, keepdims=True)
                d_col = jnp.sum(jnp.where(own[e], prod, 0.0), axis=1, keepdims=True)
                qa_ref[e, :, cols] = jnp.where(own[e], q, _bias_lanes(a_col, lane, e, True))
                da_ref[e, :, cols] = jnp.where(own[e], dob, _bias_lanes(-d_col, lane, e, True))

    blk = pl.BlockSpec((tm, tn), lambda i, j: (i, j))
    pair = pl.BlockSpec((2, tm, tn), lambda i, j: (0, i, j))
    return pl.pallas_call(
        body, name="fox_out_bwd", grid=(S // tm, W // tn),
        in_specs=[pl.BlockSpec((tm, D), lambda i, j: (i, 0)), pl.BlockSpec((tn, D), lambda i, j: (j, 0)),
                  blk, pl.BlockSpec((tm, tn), lambda i, j: (i, 3 * W // tn + j)), blk,
                  pl.BlockSpec((npair, tm, LANE), lambda i, j: (j, i, 0))],
        out_specs=[pair, pair, pl.BlockSpec((None, tm, tn), lambda i, j: (3, i, j))],
        out_shape=[jax.ShapeDtypeStruct((2, S, W), BF16), jax.ShapeDtypeStruct((2, S, W), BF16),
                   jax.ShapeDtypeStruct((4, S, W), BF16)],
        compiler_params=_cparams(),
    )(dxb, wo, qkvg, qkvg, o, a)


def _fox_bwd(qaug, doaug, qkv, c, dqkvg, H, scatter=(), scatter_specs=()):
    na = len(scatter)
    S = qkv.shape[0]
    W = H * HEAD_DIM
    HP = H // 2
    tq = _row_tile(S, ATT_TILE)
    nq = S // tq
    wb = W // LANE
    scale = HEAD_DIM ** -0.5

    def body(*refs):
        qa_ref, da_ref, k_ref, v_ref, c_ref = refs[:5]
        out_ref, dcr_ref, dcc_ref = refs[6 + na:9 + na]
        dq_sc, dk_sc, dv_sc = refs[9 + 2 * na:12 + 2 * na]
        hp, kj = pl.program_id(0), pl.program_id(1)
        if na:
            remote = _direct_scatter_copies(refs[5:5 + na], refs[9 + na:9 + 2 * na], scatter_specs,
                                            *refs[12 + 2 * na:])

            @pl.when((hp == 0) & (kj == 0))
            def _():
                for cp in remote:
                    cp.start()
        lane = lax.broadcasted_iota(jnp.int32, (tq, LANE), 1)
        own = [lane < HEAD_DIM, lane >= HEAD_DIM]
        rows = lax.broadcasted_iota(jnp.int32, (tq, tq), 0)
        cols = lax.broadcasted_iota(jnp.int32, (tq, tq), 1)

        @pl.when(kj == 0)
        def _():
            dq_sc[...] = jnp.zeros(dq_sc.shape, F32)

        @pl.when((kj == 0) & (hp == 0))
        def _():
            dcr_ref[...] = jnp.zeros(dcr_ref.shape, F32)
            dcc_ref[...] = jnp.zeros(dcc_ref.shape, F32)

        kblk, vblk, cblk = k_ref[...], v_ref[...], c_ref[...]
        one, zero = jnp.ones((tq, LANE), BF16), jnp.zeros((tq, LANE), BF16)
        ka, va = [], []
        for e in range(2):
            o0 = HEAD_DIM * (1 - e)
            c_col = jnp.sum(jnp.where(lane == 2 * hp + e, cblk, 0.0), axis=1, keepdims=True)
            ka.append(jnp.where(own[e], kblk, _bias_lanes(-c_col, lane, e, False)))
            va.append(jnp.where(own[e], vblk, jnp.where((lane >= o0) & (lane < o0 + 3), one, zero)))
        dk_sc[...] = jnp.zeros(dk_sc.shape, F32)
        dv_sc[...] = jnp.zeros(dv_sc.shape, F32)

        def step(i, masked):
            r0 = pl.multiple_of(i * tq, tq)
            for e in range(2):
                qa = qa_ref[e, pl.ds(r0, tq), :]
                da = da_ref[e, pl.ds(r0, tq), :]
                p = jnp.exp(_dot_nt(qa, ka[e]))
                if masked:
                    p = jnp.where(rows >= cols, p, 0.0)
                ds = p * _dot_nt(da, va[e])
                pb, dsb = p.astype(BF16), ds.astype(BF16)
                dv_sc[e] += _dot_tn(pb, da)
                dk_sc[e] += _dot_tn(dsb, qa)
                dq_sc[e, pl.ds(r0, tq), :] += _dot_nn(dsb, ka[e])

        step(kj, True)

        def loop_body(i, carry):
            step(i, False)
            return carry

        lax.fori_loop(kj + 1, nq, loop_body, 0)
        k0 = pl.multiple_of(kj * tq, tq)
        out_ref[1, pl.ds(k0, tq), :] = jnp.where(own[0], dk_sc[0], dk_sc[1]).astype(BF16)
        out_ref[2, pl.ds(k0, tq), :] = jnp.where(own[0], dv_sc[0], dv_sc[1]).astype(BF16)

        def put_lane(ref, r0, e, tile, src_lane):
            col = jnp.sum(jnp.where(lane == src_lane, tile, 0.0), axis=1, keepdims=True)
            ref[pl.ds(r0, tq), :] = jnp.where(lane == 2 * hp + e, col, ref[pl.ds(r0, tq), :])

        for e in range(2):
            put_lane(dcc_ref, k0, e, dk_sc[e], HEAD_DIM * (1 - e) + 3)

        @pl.when(kj == nq - 1)
        def _():
            def chunk(i, carry):
                r0 = pl.multiple_of(i * tq, tq)
                d0, d1 = dq_sc[0, pl.ds(r0, tq), :], dq_sc[1, pl.ds(r0, tq), :]
                out_ref[0, pl.ds(r0, tq), :] = (jnp.where(own[0], d0, d1) * scale).astype(BF16)
                put_lane(dcr_ref, r0, 0, d0, HEAD_DIM)
                put_lane(dcr_ref, r0, 1, d1, 0)
                return carry
            lax.fori_loop(0, nq, chunk, 0)

        if na:
            @pl.when((hp == HP - 1) & (kj == nq - 1))
            def _():
                _wait_all(remote)

    pair = pl.BlockSpec((2, S, LANE), lambda h, j: (0, 0, h))
    vec = pl.BlockSpec((S, LANE), lambda h, j: (0, 0))
    any_spec = pl.BlockSpec(memory_space=pl.ANY)
    sems = [pltpu.SemaphoreType.DMA((na, N_DEV - 1)), pltpu.SemaphoreType.DMA((na, N_DEV - 1))] if na else []
    outs = pl.pallas_call(
        body, name="fox_attn_bwd", grid=(HP, nq),
        in_specs=[pair, pair,
                  pl.BlockSpec((tq, LANE), lambda h, j: (j, wb + h)),
                  pl.BlockSpec((tq, LANE), lambda h, j: (j, 2 * wb + h)),
                  pl.BlockSpec((tq, LANE), lambda h, j: (j, 0))] + [any_spec] * (na + 1),
        out_specs=[pl.BlockSpec((3, S, LANE), lambda h, j: (0, 0, h)), vec, vec] + [any_spec] * na,
        out_shape=[jax.ShapeDtypeStruct(dqkvg.shape, BF16), jax.ShapeDtypeStruct((S, LANE), F32),
                   jax.ShapeDtypeStruct((S, LANE), F32)]
        + [jax.ShapeDtypeStruct((N_DEV - 1,) + _scatter_block_shape(g, s), g.dtype)
           for g, s in zip(scatter, scatter_specs)],
        scratch_shapes=[pltpu.VMEM((2, S, LANE), F32), pltpu.VMEM((2, tq, LANE), F32),
                        pltpu.VMEM((2, tq, LANE), F32)] + sems,
        input_output_aliases={5 + na: 0},
        compiler_params=_cparams(),
    )(qaug, doaug, qkv, qkv, c, *scatter, dqkvg)
    return outs[0], outs[1], outs[2], list(outs[3:])


def _swa_pick(blk, half, lane):
    b = blk.astype(F32)
    r = pltpu.roll(b, HEAD_DIM, axis=1)
    return jnp.where(jnp.logical_xor(lane < HEAD_DIM, half == 1), b, r).astype(BF16)


def _swa_stack(t, lane, G):
    pieces = []
    z = jnp.zeros((SWA_BLOCK, LANE), t.dtype)
    for j in range(G // 2):
        tile = t[:, LANE * j:LANE * (j + 1)]
        pieces += [jnp.where(lane < HEAD_DIM, tile, z), jnp.where(lane < HEAD_DIM, z, tile)]
    return jnp.concatenate(pieces, axis=0)


def _swa_unstack(st, lane, G):
    tiles = []
    for j in range(G // 2):
        a = st[2 * j * SWA_BLOCK:(2 * j + 1) * SWA_BLOCK]
        b = st[(2 * j + 1) * SWA_BLOCK:(2 * j + 2) * SWA_BLOCK]
        tiles.append(jnp.where(lane < HEAD_DIM, a, b))
    return jnp.concatenate(tiles, axis=1)


def _swa_mask_bias(G):
    R = G * SWA_BLOCK
    t_loc = jnp.arange(R)[:, None] % SWA_BLOCK
    j_loc = jnp.arange(2 * SWA_BLOCK)[None, :]
    diff = t_loc + SWA_BLOCK - j_loc
    band = (diff >= 0) & (diff < SWA_BLOCK)
    return jnp.stack([jnp.where(band & (j_loc >= SWA_BLOCK), 0.0, NEG_INF),
                      jnp.where(band, 0.0, NEG_INF)]).astype(F32)


def _swa_scores(q, kp, kc, vp, vc, srow, bias, half, head0, G):
    lane = lax.broadcasted_iota(jnp.int32, (SWA_BLOCK, LANE), 1)
    kk = jnp.concatenate([_swa_pick(kp, half, lane), _swa_pick(kc, half, lane)], axis=0)
    vv = jnp.concatenate([_swa_pick(vp, half, lane), _swa_pick(vc, half, lane)], axis=0)
    qstack = _swa_stack(q, lane, G) * jnp.asarray(HEAD_DIM ** -0.5, BF16)
    s = _dot_nt(qstack, kk) + bias
    R = G * SWA_BLOCK
    lane1 = lax.broadcasted_iota(jnp.int32, (1, LANE), 1)
    sink = jnp.concatenate(
        [jnp.broadcast_to(jnp.sum(jnp.where(lane1 == head0 + g, srow, 0.0), axis=1, keepdims=True), (SWA_BLOCK, LANE))
         for g in range(G)], axis=0)
    m = jnp.maximum(jnp.broadcast_to(jnp.max(s, axis=1, keepdims=True), (R, LANE)), sink)
    e = jnp.exp(s - jnp.tile(m, (1, 2)))
    es = jnp.exp(sink - m)
    inv = 1.0 / (jnp.broadcast_to(jnp.sum(e, axis=1, keepdims=True), (R, LANE)) + es)
    return qstack, kk, vv, e * jnp.tile(inv, (1, 2)), es * inv, lane


def _swa_fwd(q, kv, gate, sinks, mask_bias, HQ, HKV):
    S = q.shape[0]
    G = HQ // HKV
    WQ, KVW = HQ * HEAD_DIM, HKV * HEAD_DIM
    nb = S // SWA_BLOCK
    GW = G * HEAD_DIM
    kb, vb = 0, KVW // LANE
    NH = min(HKV, 4)
    NP = NH // 2

    def body(q_ref, kp_ref, kc_ref, vp_ref, vc_ref, g_ref, sink_ref, b_ref, y_ref, o_ref):
        grp = pl.program_id(0)
        for hh in range(NH):
            cols, kt = slice(GW * hh, GW * (hh + 1)), slice(LANE * (hh // 2), LANE * (hh // 2 + 1))
            _, _, vv, p, _, lane = _swa_scores(q_ref[:, cols], kp_ref[:, kt], kc_ref[:, kt], vp_ref[:, kt], vc_ref[:, kt],
                                               sink_ref[...], b_ref[0], hh % 2, (NH * grp + hh) * G, G)
            o = _swa_unstack(_dot_nn(p.astype(BF16), vv), lane, G)
            g = g_ref[:, cols].astype(F32)
            y_ref[:, cols] = (o * (g * _sigmoid(g))).astype(BF16)
            o_ref[:, cols] = o.astype(BF16)

    blk = lambda cb, prev: pl.BlockSpec(
        (SWA_BLOCK, NP * LANE), lambda h, n, cb=cb, prev=prev: (jnp.maximum(n - prev, 0), cb // NP + h))
    qspec = pl.BlockSpec((SWA_BLOCK, NH * GW), lambda h, n: (n, h))
    return pl.pallas_call(
        body, name="swa_attn_fwd", grid=(HKV // NH, nb),
        in_specs=[qspec, blk(kb, 1), blk(kb, 0), blk(vb, 1), blk(vb, 0), qspec,
                  pl.BlockSpec((1, LANE), lambda h, n: (0, 0)),
                  pl.BlockSpec((1, G * SWA_BLOCK, 2 * SWA_BLOCK), lambda h, n: (jnp.minimum(n, 1), 0, 0))],
        out_specs=[qspec, qspec],
        out_shape=[jax.ShapeDtypeStruct((S, WQ), BF16), jax.ShapeDtypeStruct((S, WQ), BF16)],
        compiler_params=_cparams(),
    )(q, kv, kv, kv, kv, gate, sinks, mask_bias)


def _swa_bwd(q, kv, dy, gate, o, sinks, tables, mask_bias, HQ, HKV):
    S = q.shape[0]
    G = HQ // HKV
    WQ, KVW = HQ * HEAD_DIM, HKV * HEAD_DIM
    nb = S // SWA_BLOCK
    GW = G * HEAD_DIM
    R = G * SWA_BLOCK
    kb, vb = 0, KVW // LANE
    scale = HEAD_DIM ** -0.5
    NH = min(HKV, 4)
    NP = NH // 2
    assert G == 8

    def body(q_ref, kp_ref, kc_ref, vp_ref, vc_ref, dy_ref, g_ref, o_ref, sink_ref, t_ref, b_ref,
             dqg_ref, dkv_ref, dsink_ref, carry_sc):
        grp, n = pl.program_id(0), pl.program_id(1)

        @pl.when(n == 0)
        def _():
            carry_sc[...] = jnp.zeros(carry_sc.shape, F32)
            dsink_ref[...] = jnp.zeros(dsink_ref.shape, F32)

        @pl.when(n < nb)
        def _():
            t0, t1, t2 = (jnp.tile(t_ref[i], (1, GW // LANE)) for i in range(3))
            for hh in range(NH):
                cols, kt = slice(GW * hh, GW * (hh + 1)), slice(LANE * (hh // 2), LANE * (hh // 2 + 1))
                qstack, kk, vv, p, psink, lane = _swa_scores(
                    q_ref[:, cols], kp_ref[:, kt], kc_ref[:, kt], vp_ref[:, kt], vc_ref[:, kt], sink_ref[...], b_ref[0],
                    hh % 2, (NH * grp + hh) * G, G)
                dyv, g, ov = dy_ref[:, cols], g_ref[:, cols].astype(F32), o_ref[:, cols].astype(F32)
                sg = _sigmoid(g)
                dob = (dyv * (g * sg)).astype(BF16)
                dqg_ref[1, :, cols] = (dyv * ov * (sg * (1.0 + g * (1.0 - sg)))).astype(BF16)
                prod = dob.astype(F32) * ov
                dparts = []
                for j in range(G // 2):
                    tile = prod[:, LANE * j:LANE * (j + 1)]
                    for sel in (jnp.where(lane < HEAD_DIM, tile, 0.0), jnp.where(lane < HEAD_DIM, 0.0, tile)):
                        dparts.append(jnp.broadcast_to(jnp.sum(sel, axis=1, keepdims=True), (SWA_BLOCK, LANE)))
                delta = jnp.concatenate(dparts, axis=0)
                dostack = _swa_stack(dob, lane, G)
                ds = p * (_dot_nt(dostack, vv) - jnp.tile(delta, (1, 2)))
                dsb, pb = ds.astype(BF16), p.astype(BF16)
                dq = _swa_unstack(_dot_nn(dsb, kk), lane, G) * scale
                dq = dq * t0 + pltpu.roll(dq * t1, ROT_DIM // 2, axis=1) + pltpu.roll(dq * t2, GW - ROT_DIM // 2, axis=1)
                dqg_ref[0, :, cols] = dq.astype(BF16)
                dkk = _dot_tn(dsb, qstack)
                dvv = _dot_tn(pb, dostack)
                dkk = dkk + pltpu.roll(dkk, HEAD_DIM, axis=1)
                dvv = dvv + pltpu.roll(dvv, HEAD_DIM, axis=1)
                lane2 = lax.broadcasted_iota(jnp.int32, (2 * SWA_BLOCK, LANE), 1)
                comb = jnp.where(lane2 < HEAD_DIM, dkk, dvv)
                dkv_ref[hh] = carry_sc[hh] + comb[:SWA_BLOCK]
                carry_sc[hh] = comb[SWA_BLOCK:]
                sk = psink * delta
                rows = [-jnp.sum(sk[g_ * SWA_BLOCK:(g_ + 1) * SWA_BLOCK], axis=0, keepdims=True) for g_ in range(G)]
                dsink_ref[hh] += jnp.concatenate(rows, axis=0)

        @pl.when(n == nb)
        def _():
            dkv_ref[...] = carry_sc[...]

    cl = lambda n: jnp.minimum(n, nb - 1)
    blk = lambda cb, prev: pl.BlockSpec(
        (SWA_BLOCK, NP * LANE), lambda h, n, cb=cb, prev=prev: (jnp.maximum(cl(n) - prev, 0), cb // NP + h))
    qspec = pl.BlockSpec((SWA_BLOCK, NH * GW), lambda h, n: (cl(n), h))
    return pl.pallas_call(
        body, name="swa_attn_bwd", grid=(HKV // NH, nb + 1),
        in_specs=[qspec, blk(kb, 1), blk(kb, 0), blk(vb, 1), blk(vb, 0), qspec, qspec, qspec,
                  pl.BlockSpec((1, LANE), lambda h, n: (0, 0)),
                  pl.BlockSpec((3, SWA_BLOCK, LANE), lambda h, n: (0, cl(n), 0)),
                  pl.BlockSpec((1, R, 2 * SWA_BLOCK), lambda h, n: (jnp.minimum(n, 1), 0, 0))],
        out_specs=[pl.BlockSpec((2, SWA_BLOCK, NH * GW), lambda h, n: (0, cl(n), h)),
                   pl.BlockSpec((NH, SWA_BLOCK, LANE), lambda h, n: (h, jnp.maximum(n - 1, 0), 0)),
                   pl.BlockSpec((NH, 8, LANE), lambda h, n: (h, 0, 0))],
        out_shape=[jax.ShapeDtypeStruct((2, S, WQ), BF16), jax.ShapeDtypeStruct((HKV, S, LANE), F32),
                   jax.ShapeDtypeStruct((HKV, 8, LANE), F32)],
        scratch_shapes=[pltpu.VMEM((NH, SWA_BLOCK, LANE), F32)],
        compiler_params=_cparams(),
    )(q, kv, kv, kv, kv, dy, gate, o, sinks, tables, mask_bias)


def _swa_dkv_finish(dkv, tables):
    HKV, S, _ = dkv.shape
    KVW = HKV * HEAD_DIM
    tm = _row_tile(S, EPI_TILE)
    npair = HKV // 2

    def body(d_ref, t_ref, o_ref):
        lane = lax.broadcasted_iota(jnp.int32, (tm, LANE), 1)
        lo = lane < HEAD_DIM
        for p in range(npair):
            a, b = d_ref[2 * p], d_ref[2 * p + 1]
            tk = jnp.where(lo, a, pltpu.roll(b, HEAD_DIM, axis=1))
            tv = jnp.where(lo, pltpu.roll(a, HEAD_DIM, axis=1), b)
            tk = (tk * t_ref[0] + pltpu.roll(tk * t_ref[1], ROT_DIM // 2, axis=1)
                  + pltpu.roll(tk * t_ref[2], LANE - ROT_DIM // 2, axis=1))
            o_ref[:, LANE * p:LANE * (p + 1)] = tk.astype(BF16)
            o_ref[:, KVW + LANE * p:KVW + LANE * (p + 1)] = tv.astype(BF16)

    return pl.pallas_call(
        body, name="swa_dkv_finish", grid=(S // tm,),
        in_specs=[pl.BlockSpec((HKV, tm, LANE), lambda i: (0, i, 0)), pl.BlockSpec((3, tm, LANE), lambda i: (0, i, 0))],
        out_specs=pl.BlockSpec((tm, 2 * KVW), lambda i: (i, 0)),
        out_shape=jax.ShapeDtypeStruct((S, 2 * KVW), BF16),
        compiler_params=_cparams(),
    )(dkv, tables)


def _rope_tables(S, width):
    half = ROT_DIM // 2
    pos = jnp.arange(S, dtype=F32)
    inv_freq = ROPE_THETA ** (-jnp.arange(half, dtype=F32) / half)
    ang = pos[:, None] * inv_freq[None, :]
    cos, sin = jnp.cos(ang), jnp.sin(ang)
    one = jnp.ones((S, HEAD_DIM - ROT_DIM), F32)
    zero = jnp.zeros((S, HEAD_DIM - ROT_DIM), F32)
    zh = jnp.zeros((S, half), F32)
    t0 = jnp.concatenate([cos, cos, one], axis=1)
    t1 = jnp.concatenate([-sin, zh, zero], axis=1)
    t2 = jnp.concatenate([zh, sin, zero], axis=1)
    return jnp.stack([jnp.tile(t, (1, width // HEAD_DIM)) for t in (t0, t1, t2)])


def _pad_rows(v, row, total_rows=8):
    return jnp.pad(v, ((row, total_rows - row - v.shape[0]), (0, 0)))


def _pad_lanes(v, off, width):
    return jnp.pad(v, ((0, 0), (off, width - off - v.shape[1])))


def kernel(x, norm_g, fox_w_in, fox_b_f, fox_w_out, swa_w_in, swa_sinks, swa_w_out, final_g, loss_target, m_norm_g, m_fox_w_in, m_fox_b_f, m_fox_w_out, m_swa_w_in, m_swa_sinks, m_swa_w_out, m_final_g, v_norm_g, v_fox_w_in, v_fox_b_f, v_fox_w_out, v_swa_w_in, v_swa_sinks, v_swa_w_out, v_final_g):
    S, D = x.shape[1], x.shape[2]
    H = fox_b_f.shape[1]
    W = H * HEAD_DIM
    wf = fox_w_in.shape[2]
    ws = swa_w_in.shape[2]
    HQ = swa_sinks.shape[1]
    WQ = HQ * HEAD_DIM
    KVW = (ws * N_DEV - 2 * WQ) // 2
    HKV = KVW // HEAD_DIM
    rows_o = fox_w_out.shape[1]
    assert wf * N_DEV == 4 * W + H and rows_o * N_DEV == W and H <= LANE and HQ <= LANE
    me = _my_index()

    _, sw_f, np_f = _slab_geom(wf)
    _, sw_s, np_s = _slab_geom(ws)

    def slab(w2d, w, sw):
        return jnp.pad(w2d.astype(BF16), ((0, 0), (0, sw - w)))

    (fi_all,) = _all_gather([slab(fox_w_in[0], wf, sw_f)])
    w_fi = _assemble(fi_all, wf)
    later = [slab(swa_w_in[0], ws, sw_s), fox_w_out[0].astype(BF16), swa_w_out[0].astype(BF16)]

    x0 = x[0]
    g0, g1, gf = norm_g[0:1], norm_g[1:2], final_g[None, :]
    bias = _pad_lanes(fox_b_f, 0, LANE)
    sinks = _pad_lanes(swa_sinks, 0, LANE)
    tab_k = _rope_tables(S, LANE)
    mask_bias = _swa_mask_bias(HQ // HKV)

    h0 = _rmsnorm_fwd(x0, g0, "rmsnorm0")
    qkv0 = _proj(h0, w_fi, 0, 4 * W, BF16, "fox_in_qkvg")
    fl = _proj(h0, w_fi, 4 * W, LANE, F32, "fox_in_f")
    c = _fox_gate_fwd(fl, bias)
    y0, o0, a0, (si_all, fo_all, so_all) = _fox_fwd(qkv0, c, H, gather=later)
    w_si = _assemble(si_all, ws)
    w_fo = fo_all.reshape(W, D)
    w_so = so_all.reshape(WQ, D)
    x1, h1 = _out_proj_norm(y0, w_fo, x0, g1, "fox_out")

    q1 = _proj(h1, w_si, 0, WQ, BF16, "swa_in_q", rope=(tab_k, WQ))
    kv1 = _proj(h1, w_si, WQ, 2 * KVW, BF16, "swa_in_kv", rope=(tab_k, KVW))
    gate1 = _proj(h1, w_si, WQ + 2 * KVW, WQ, BF16, "swa_in_gate")
    y1, o1 = _swa_fwd(q1, kv1, gate1, sinks, mask_bias, HQ, HKV)
    dx2, dx2b, dgf, loss_p = _out_proj_loss(y1, w_so, x1, loss_target[0], gf, "swa_out_loss")

    dy1 = _matmul_nt([(dx2b, None, 0)], w_so, WQ, "swa_out_bwd")
    g_so, g_so_h = _matmul_tn(y1, [(dx2b, None, 0)], D, "swa_out_wgrad", also_bf16=True)
    dqg1, dkv1, dsink = _swa_bwd(q1, kv1, dy1, gate1, o1, sinks, tab_k, mask_bias, HQ, HKV)
    dkv1f = _swa_dkv_finish(dkv1, tab_k)
    parts1 = [(dqg1, 0, 0), (dkv1f, None, WQ), (dqg1, 1, WQ + 2 * KVW)]
    g_si, g_si_h = _matmul_tn(h1, parts1, np_s, "swa_in_wgrad", tile_major=True, also_bf16=True)
    dh1 = _matmul_nt(parts1, w_si, D, "swa_in_bwd")
    dx1, dx1b, dg1 = _rmsnorm_bwd(dh1, x1, g1, dx2, "rmsnorm1_bwd")

    qaug0, doaug0, dqkvg0 = _fox_out_bwd(dx1b, w_fo, qkv0, o0, a0, H)
    g_fo, g_fo_h = _matmul_tn(y0, [(dx1b, None, 0)], D, "fox_out_wgrad", also_bf16=True)
    early_specs = [("col", ws), ("row", rows_o), ("row", rows_o)]
    dqkvg0, dcr, dcc, early_recv = _fox_bwd(qaug0, doaug0, qkv0, c, dqkvg0, H, scatter=[g_si_h, g_fo_h, g_so_h],
                                           scatter_specs=early_specs)
    dfl, dbf = _fox_gate_bwd(fl, bias, dcr - dcc)
    parts0 = [(dqkvg0, "stack", 0), (dfl, None, 4 * W)]
    spec_fi = ("col", wf)
    fi_halves, token = [], None
    for half in range(2):
        g_fi, g_fi_h = _matmul_tn(h0, parts0, np_f, f"fox_in_wgrad_rows{half}", tile_major=True, also_bf16=True,
                                  rows=(half * (D // 2), D // 2), after=() if token is None else (token,))
        fi_sems, fi_src, fi_land, token = _scatter_start(g_fi_h, spec_fi)
        fi_halves.append((g_fi, fi_sems, fi_src, fi_land))
    parts0[-1] = (dfl + token[0, 0].astype(BF16), None, 4 * W)
    dh0 = _matmul_nt(parts0, w_fi, D, "fox_in_bwd")
    dx0, _, dg0 = _rmsnorm_bwd(dh0, x0, g0, dx1, "rmsnorm0_bwd")

    red_si, gw_fo, gw_so = [_final_sum8(g_, r_, s_)
                            for g_, r_, s_ in zip([g_si, g_fo, g_so], early_recv, early_specs)]
    gt_si = lax.dynamic_slice(red_si, ((ws * me) % LANE, 0), (ws, D))

    def t_in(p):
        return jnp.swapaxes(p[0], 0, 1)

    def t_out(t):
        return jnp.swapaxes(t, 0, 1)[None]

    P = D
    dsink_v = dsink[:, :, 0].reshape(1, HQ)
    row3 = _pad_lanes(dbf[:, :H], 0, P) + _pad_lanes(dsink_v, LANE, P) + _pad_lanes(loss_p[:, :1], 2 * LANE, P)
    pack = _pad_rows(dg0, 0) + _pad_rows(dg1, 1) + _pad_rows(dgf, 2) + _pad_rows(row3, 3)

    d_fo, m_fo, v_fo = _adamw(fox_w_out[0], gw_fo, m_fox_w_out[0], v_fox_w_out[0], "adamw_fox_out")
    d_si, m_si, v_si = _adamw(t_in(swa_w_in), gt_si, t_in(m_swa_w_in), t_in(v_swa_w_in), "adamw_swa_in")
    d_so, m_so, v_so = _adamw(swa_w_out[0], gw_so, m_swa_w_out[0], v_swa_w_out[0], "adamw_swa_out")
    behind, red_fi = [dx0, pack, d_fo, d_si, d_so], []
    for g_fi, fi_sems, fi_src, fi_land in fi_halves:
        recv_fi = _scatter_wait(fi_sems, fi_src, fi_land, spec_fi, after=behind)
        red_fi.append(_final_sum8(g_fi, recv_fi, spec_fi))
        behind = [recv_fi]
    red_fi = jnp.concatenate(red_fi, axis=1)
    gt_fi = lax.dynamic_slice(red_fi, ((wf * me) % LANE, 0), (wf, D))

    d_fi, m_fi, v_fi = _adamw(t_in(fox_w_in), gt_fi, t_in(m_fox_w_in), t_in(v_fox_w_in), "adamw_fox_in")
    gw_si, d_si, m_si, v_si = [t_out(t)[0] for t in (gt_si, d_si, m_si, v_si)]
    gw_fi, d_fi, m_fi, v_fi = [t_out(t)[0] for t in (gt_fi, d_fi, m_fi, v_fi)]

    tot = _all_reduce_small(pack, after=recv_fi)
    loss = tot[3, 2 * LANE]
    g_norm = tot[0:2]
    g_final = tot[2]
    g_bf = tot[3:4, 0:H]
    g_sinks = tot[3:4, LANE:LANE + HQ]

    def small_pack(ng, fg, bf, sk):
        r3 = _pad_lanes(bf, 0, P) + _pad_lanes(sk, LANE, P)
        return _pad_rows(ng, 0) + _pad_rows(fg[None, :], 2) + _pad_rows(r3, 3)

    sd, sm, sv = _adamw(small_pack(norm_g, final_g, fox_b_f, swa_sinks), tot,
                        small_pack(m_norm_g, m_final_g, m_fox_b_f, m_swa_sinks),
                        small_pack(v_norm_g, v_final_g, v_fox_b_f, v_swa_sinks), "adamw_small")

    def unpack(t):
        return t[0:2], t[3:4, 0:H], t[3:4, LANE:LANE + HQ], t[2]

    def group(small, fi, fo, si, so):
        ng, bf, sk, fg = unpack(small)
        return (ng, fi[None], bf, fo[None], si[None], sk, so[None], fg)

    grads = (g_norm, gw_fi[None], g_bf, gw_fo[None], gw_si[None], g_sinks, gw_so[None], g_final)
    return (loss, dx0[None], *grads, *group(sd, d_fi, d_fo, d_si, d_so),
            *group(sm, m_fi, m_fo, m_si, m_so), *group(sv, v_fi, v_fo, v_si, v_so))
```

```python
import math

import jax
import jax.numpy as jnp
from jax import lax
from jax.experimental import pallas as pl
from jax.experimental.pallas import tpu as pltpu

F32 = jnp.float32
BF16 = jnp.bfloat16
MESH = pl.DeviceIdType.MESH

N_DEV = 8
LANE = 128
HEAD_DIM = 64
SWA_BLOCK = 128
NEG_INF = -1e30
RMS_EPS = 1e-6
ROPE_THETA = 500000.0
ROT_DIM = HEAD_DIM // 4
ADAM_LR, ADAM_B1, ADAM_B2, ADAM_EPS, ADAM_WD, ADAM_STEP = 0.001, 0.9, 0.999, 1e-08, 0.01, 10
VMEM_LIMIT = 56 * 1024 * 1024
MM_TILE = 1024
ATT_TILE = 512
EPI_TILE = 512
ROW_TILE = 256
ADAM_TILE_ELEMS = 3 << 18


def _cparams(**kw):
    return pltpu.CompilerParams(vmem_limit_bytes=VMEM_LIMIT, **kw)


def _tile(n, cap):
    if n <= cap:
        return n
    t = (cap // LANE) * LANE
    while t > LANE and n % t:
        t -= LANE
    assert n % t == 0, (n, cap)
    return t


def _row_tile(n, cap):
    t = min(n, cap)
    while n % t:
        t //= 2
    return t


def _dot_nn(a, b):
    return jnp.dot(a, b, preferred_element_type=F32)


def _dot_nt(a, b):
    return lax.dot_general(a, b, (((1,), (1,)), ((), ())), preferred_element_type=F32)


def _dot_tn(a, b):
    return lax.dot_general(a, b, (((0,), (0,)), ((), ())), preferred_element_type=F32)


def _split3(x):
    hi = x.astype(BF16)
    r1 = x - hi.astype(F32)
    mid = r1.astype(BF16)
    return hi, mid, (r1 - mid.astype(F32)).astype(BF16)


def _sigmoid(g):
    return 1.0 / (1.0 + jnp.exp(-g))


def _slab_geom(w):
    starts = [w * i for i in range(N_DEV)]
    aligned = [LANE * (s // LANE) for s in starts]
    offs = [s - a for s, a in zip(starts, aligned)]
    sw = LANE * (-(-(max(offs) + w) // LANE))
    return aligned, sw, aligned[-1] + sw


def _my_index():
    return 4 * lax.axis_index("x") + 2 * lax.axis_index("y") + lax.axis_index("c")


def _all_gather(arrs):
    n = len(arrs)

    def body(*refs):
        ins, outs = refs[:n], refs[n:2 * n]
        send_sems, recv_sems, local_sems = refs[2 * n:]
        x, y, c = lax.axis_index("x"), lax.axis_index("y"), lax.axis_index("c")
        me, sib = (x, y, c), (x, y, 1 - c)
        chips = [(1 - x, y), (x, 1 - y), (1 - x, 1 - y)]

        def idx(px, py, pc):
            return 4 * px + 2 * py + pc

        def copy(a, k, block, to, src=None):
            dst = outs[a].at[idx(*block)]
            return pltpu.make_async_remote_copy(
                src_ref=dst if src is None else src, dst_ref=dst,
                send_sem=send_sems.at[a, k], recv_sem=recv_sems.at[a, k],
                device_id=to, device_id_type=MESH)

        mine = [pltpu.make_async_copy(ins[a], outs[a].at[idx(*me)], local_sems.at[a]) for a in range(n)]
        for m in mine:
            m.start()
        first = []
        for a in range(n):
            first.append(copy(a, 0, me, sib, src=ins[a]))
            for j, chip in enumerate(chips):
                first.append(copy(a, 1 + j, me, (*chip, c), src=ins[a]))
        for cp in first:
            cp.start()
        passed = []
        for j, chip in enumerate(chips):
            for a in range(n):
                copy(a, 1 + j, (*chip, c), me).wait_recv()
                p = copy(a, 4 + j, (*chip, c), sib)
                p.start()
                passed.append(p)
        for a in range(n):
            copy(a, 0, sib, me).wait_recv()
        for j, chip in enumerate(chips):
            for a in range(n):
                copy(a, 4 + j, (*chip, 1 - c), me).wait_recv()
        for cp in first + passed:
            cp.wait_send()
        for m in mine:
            m.wait()

    any_spec = pl.BlockSpec(memory_space=pl.ANY)
    return pl.pallas_call(
        body, name="weights_all_gather",
        out_shape=[jax.ShapeDtypeStruct((N_DEV,) + a.shape, a.dtype) for a in arrs],
        in_specs=[any_spec] * n, out_specs=[any_spec] * n,
        scratch_shapes=[pltpu.SemaphoreType.DMA((n, 7)), pltpu.SemaphoreType.DMA((n, 7)),
                        pltpu.SemaphoreType.DMA((n,))],
    )(*arrs)


def _rs_windows(specs):
    def window(ref, spec, blk):
        kind, n = spec
        if kind == "col":
            _, sw, _ = _slab_geom(n)
            return ref.at[pl.ds((n * blk) // LANE, sw // LANE)]
        start = pl.multiple_of(n * blk, n)
        return ref.at[pl.ds(start, n), :]
    return window


def _peer(k):
    x, y, c = lax.axis_index("x"), lax.axis_index("y"), lax.axis_index("c")
    return (x ^ (k >> 2), y ^ ((k >> 1) & 1), c ^ (k & 1))


def _direct_gather_copies(ins, outs, send_sems, recv_sems, local_sems):
    me = _my_index()
    remote, local = [], []
    for a, (src, dst) in enumerate(zip(ins, outs)):
        local.append(pltpu.make_async_copy(src, dst.at[me], local_sems.at[a]))
        for k in range(1, N_DEV):
            remote.append(pltpu.make_async_remote_copy(
                src_ref=src, dst_ref=dst.at[me], send_sem=send_sems.at[a, k - 1], recv_sem=recv_sems.at[a, k - 1],
                device_id=_peer(k), device_id_type=MESH))
    return remote, local


def _direct_scatter_copies(ins, outs, specs, send_sems, recv_sems):
    window = _rs_windows(specs)
    remote = []
    for a, (src, dst) in enumerate(zip(ins, outs)):
        for k in range(1, N_DEV):
            px, py, pc = _peer(k)
            remote.append(pltpu.make_async_remote_copy(
                src_ref=window(src, specs[a], 4 * px + 2 * py + pc), dst_ref=dst.at[k - 1],
                send_sem=send_sems.at[a, k - 1], recv_sem=recv_sems.at[a, k - 1],
                device_id=(px, py, pc), device_id_type=MESH))
    return remote


def _scatter_block_shape(g, spec):
    kind, w = spec
    return (_slab_geom(w)[1] // LANE, g.shape[1], LANE) if kind == "col" else (w, g.shape[1])


def _wait_all(remote, local=()):
    for cp in remote:
        cp.wait_recv()
    for cp in remote:
        cp.wait_send()
    for cp in local:
        cp.wait()


def _scatter_start(g, spec):
    blk = _scatter_block_shape(g, spec)
    window = _rs_windows([spec])
    npeer = N_DEV - 1

    def body(g_ref, land_ref, *rest):
        sems = rest[:2 * npeer]
        token = rest[2 * npeer + 2]
        for cp in _peer_block_copies(g_ref, land_ref, spec, window, sems[:npeer], sems[npeer:]):
            cp.start()
        token[...] = jnp.zeros(token.shape, token.dtype)

    hbm = pl.BlockSpec(memory_space=pltpu.HBM)
    sem = pl.BlockSpec(memory_space=pltpu.SEMAPHORE)
    land = lax.empty((npeer,) + blk, g.dtype)
    outs = pl.pallas_call(
        body, name="grads_scatter_start",
        out_shape=(pltpu.SemaphoreType.DMA(()),) * (2 * npeer)
        + (pltpu.HBM(g.shape, g.dtype), pltpu.HBM(land.shape, land.dtype), jax.ShapeDtypeStruct((8, LANE), F32)),
        in_specs=(hbm, hbm),
        out_specs=(sem,) * (2 * npeer) + (hbm, hbm, pl.BlockSpec(memory_space=pltpu.VMEM)),
        input_output_aliases={0: 2 * npeer, 1: 2 * npeer + 1},
        compiler_params=pltpu.CompilerParams(has_side_effects=pltpu.SideEffectType.DATAFLOW_SIDE_EFFECTING),
    )(pltpu.with_memory_space_constraint(g, pltpu.HBM), pltpu.with_memory_space_constraint(land, pltpu.HBM))
    return outs[:2 * npeer], outs[2 * npeer], outs[2 * npeer + 1], outs[2 * npeer + 2]


def _peer_block_copies(g_ref, land_ref, spec, window, send_sems, recv_sems):
    copies = []
    for k in range(1, N_DEV):
        px, py, pc = _peer(k)
        copies.append(pltpu.make_async_remote_copy(
            src_ref=window(g_ref, spec, 4 * px + 2 * py + pc), dst_ref=land_ref.at[k - 1],
            send_sem=send_sems[k - 1], recv_sem=recv_sems[k - 1], device_id=(px, py, pc), device_id_type=MESH))
    return copies


def _scatter_wait(sems, g_thru, land_thru, spec, after):
    window = _rs_windows([spec])
    npeer = N_DEV - 1

    def body(g_ref, land_ref, *rest):
        s = rest[:2 * npeer]
        copies = _peer_block_copies(g_ref, land_ref, spec, window, s[:npeer], s[npeer:])
        for cp in copies:
            cp.wait_send()
        for cp in copies:
            cp.wait_recv()

    hbm = pl.BlockSpec(memory_space=pltpu.HBM)
    sem = pl.BlockSpec(memory_space=pltpu.SEMAPHORE)
    return pl.pallas_call(
        body, name="grads_scatter_wait",
        out_shape=(pltpu.HBM(g_thru.shape, g_thru.dtype), pltpu.HBM(land_thru.shape, land_thru.dtype)),
        in_specs=(hbm, hbm) + (sem,) * (2 * npeer) + (pl.BlockSpec(memory_space=pl.ANY),) * len(after),
        out_specs=(hbm, hbm), input_output_aliases={0: 0, 1: 1},
        compiler_params=pltpu.CompilerParams(has_side_effects=pltpu.SideEffectType.DATAFLOW_SIDE_EFFECTING),
    )(g_thru, land_thru, *sems, *after)[1]


def _final_sum8(g, recv, spec):
    kind, n = spec
    me = _my_index()
    offs = jnp.stack([(n * me) // LANE if kind == "col" else me]).astype(jnp.int32)
    if kind == "col":
        _, T, M, _ = recv.shape
        grid = (T,)
        in_specs = [pl.BlockSpec((1, M, LANE), lambda t, o: (o[0] + t, 0, 0)),
                    pl.BlockSpec((N_DEV - 1, 1, M, LANE), lambda t, o: (0, t, 0, 0))]
        out_spec = pl.BlockSpec((LANE, M), lambda t, o: (t, 0))
        out_shape = jax.ShapeDtypeStruct((T * LANE, M), F32)
    else:
        _, nrow, C = recv.shape
        grid = (1,)
        in_specs = [pl.BlockSpec((nrow, C), lambda t, o: (o[0], 0)),
                    pl.BlockSpec((N_DEV - 1, nrow, C), lambda t, o: (0, 0, 0))]
        out_spec = pl.BlockSpec((nrow, C), lambda t, o: (0, 0))
        out_shape = jax.ShapeDtypeStruct((nrow, C), F32)

    def body(o_ref, g_ref, r_ref, out_ref):
        acc = g_ref[0] if kind == "col" else g_ref[...]
        for k in range(N_DEV - 1):
            acc = acc + (r_ref[k, 0] if kind == "col" else r_ref[k]).astype(F32)
        out_ref[...] = acc.T if kind == "col" else acc

    return pl.pallas_call(
        body, name="grads_final_sum8",
        grid_spec=pltpu.PrefetchScalarGridSpec(num_scalar_prefetch=1, grid=grid, in_specs=in_specs,
                                               out_specs=out_spec),
        out_shape=out_shape, compiler_params=_cparams(),
    )(offs, g, recv)


def _all_reduce_small(pack, after):
    R, P = pack.shape

    def body(x_ref, after_ref, o_ref, gat_ref, send_sems, recv_sems):
        x, y, c = lax.axis_index("x"), lax.axis_index("y"), lax.axis_index("c")
        me = 4 * x + 2 * y + c
        gat_ref[me] = x_ref[...]
        copies = []
        for k in range(1, N_DEV):
            peer = (x ^ (k >> 2), y ^ ((k >> 1) & 1), c ^ (k & 1))
            copies.append(pltpu.make_async_remote_copy(
                src_ref=x_ref, dst_ref=gat_ref.at[me],
                send_sem=send_sems.at[k - 1], recv_sem=recv_sems.at[k - 1],
                device_id=peer, device_id_type=MESH))
        for cp in copies:
            cp.start()
        for cp in copies:
            cp.wait_recv()
        for cp in copies:
            cp.wait_send()
        acc = gat_ref[0]
        for d in range(1, N_DEV):
            acc = acc + gat_ref[d]
        o_ref[...] = acc

    vm = pl.BlockSpec(memory_space=pltpu.VMEM)
    return pl.pallas_call(
        body, name="small_all_reduce",
        out_shape=jax.ShapeDtypeStruct((R, P), F32),
        in_specs=[vm, pl.BlockSpec(memory_space=pl.ANY)], out_specs=vm,
        scratch_shapes=[pltpu.VMEM((N_DEV, R, P), F32),
                        pltpu.SemaphoreType.DMA((N_DEV - 1,)), pltpu.SemaphoreType.DMA((N_DEV - 1,))],
    )(pack, after)


def _assemble(slabs, w):
    aligned, sw, total = _slab_geom(w)
    K = slabs.shape[1]
    tr = _row_tile(K, ROW_TILE)

    def body(s_ref, o_ref):
        o_ref[...] = jnp.zeros(o_ref.shape, BF16)
        for i in range(N_DEV):
            a, off = aligned[i], w * i - aligned[i]
            x = s_ref[i]
            if off:
                x = pltpu.roll(x, off, axis=1)
            o_ref[:, a:a + sw] = o_ref[:, a:a + sw] + x

    return pl.pallas_call(
        body, name="assemble_w_in", grid=(K // tr,),
        in_specs=[pl.BlockSpec((N_DEV, tr, sw), lambda i: (0, i, 0))],
        out_specs=pl.BlockSpec((tr, total), lambda i: (i, 0)),
        out_shape=jax.ShapeDtypeStruct((K, total), BF16),
        compiler_params=_cparams(),
    )(slabs)


def _rmsnorm_fwd(x, g, name):
    S, D = x.shape
    tm = _row_tile(S, ROW_TILE)

    def body(x_ref, g_ref, h_ref):
        xv = x_ref[...]
        r = lax.rsqrt(jnp.mean(xv * xv, axis=-1, keepdims=True) + RMS_EPS)
        h_ref[...] = ((xv * r) * g_ref[...]).astype(BF16)

    return pl.pallas_call(
        body, name=name, grid=(S // tm,),
        in_specs=[pl.BlockSpec((tm, D), lambda i: (i, 0)), pl.BlockSpec((1, D), lambda i: (0, 0))],
        out_specs=pl.BlockSpec((tm, D), lambda i: (i, 0)),
        out_shape=jax.ShapeDtypeStruct((S, D), BF16),
        compiler_params=_cparams(),
    )(x, g)


def _rmsnorm_bwd(dh, x, g, dres, name):
    S, D = x.shape
    tm = _row_tile(S, ROW_TILE)

    def body(dh_ref, x_ref, g_ref, dr_ref, dx_ref, dxb_ref, dg_ref):
        xv = x_ref[...]
        r = lax.rsqrt(jnp.mean(xv * xv, axis=-1, keepdims=True) + RMS_EPS)
        xhat = xv * r
        d = dh_ref[...]
        gd = d * g_ref[...]
        dx = r * (gd - xhat * jnp.mean(gd * xhat, axis=-1, keepdims=True)) + dr_ref[...]
        dx_ref[...] = dx
        dxb_ref[...] = dx.astype(BF16)

        @pl.when(pl.program_id(0) == 0)
        def _():
            dg_ref[...] = jnp.zeros(dg_ref.shape, F32)
        dg_ref[...] += jnp.sum(d * xhat, axis=0, keepdims=True)

    row = pl.BlockSpec((tm, D), lambda i: (i, 0))
    vec = pl.BlockSpec((1, D), lambda i: (0, 0))
    return pl.pallas_call(
        body, name=name, grid=(S // tm,),
        in_specs=[row, row, vec, row], out_specs=[row, row, vec],
        out_shape=[jax.ShapeDtypeStruct((S, D), F32), jax.ShapeDtypeStruct((S, D), BF16),
                   jax.ShapeDtypeStruct((1, D), F32)],
        compiler_params=_cparams(),
    )(dh, x, g, dres)


def _adamw(w, g, m, v, name):
    R, C = w.shape
    steps = pl.cdiv(R * C, ADAM_TILE_ELEMS)
    tr = R if steps == 1 else pl.cdiv(pl.cdiv(R, steps), 8) * 8
    c1 = 1.0 - ADAM_B1 ** ADAM_STEP
    c2 = 1.0 - ADAM_B2 ** ADAM_STEP

    def body(w_ref, g_ref, m_ref, v_ref, d_ref, nm_ref, nv_ref):
        gv = g_ref[...]
        nm = ADAM_B1 * m_ref[...] + (1.0 - ADAM_B1) * gv
        nv = ADAM_B2 * v_ref[...] + (1.0 - ADAM_B2) * (gv * gv)
        d_ref[...] = -ADAM_LR * ((nm / c1) / (jnp.sqrt(nv / c2) + ADAM_EPS) + ADAM_WD * w_ref[...])
        nm_ref[...] = nm
        nv_ref[...] = nv

    spec = pl.BlockSpec((tr, C), lambda i: (i, 0))
    return pl.pallas_call(
        body, name=name, grid=(pl.cdiv(R, tr),),
        in_specs=[spec] * 4, out_specs=[spec] * 3,
        out_shape=[jax.ShapeDtypeStruct((R, C), F32)] * 3,
        compiler_params=_cparams(),
    )(w, g, m, v)


def _proj(h, wfull, col0, ncols, out_dtype, name, rope=None):
    S, K = h.shape
    tm = _row_tile(S, MM_TILE)
    tn = math.gcd(_tile(ncols, MM_TILE), col0) if col0 else _tile(ncols, MM_TILE)
    if rope is not None:
        tn = _tile(math.gcd(ncols, rope[1]), MM_TILE)
    assert ncols % tn == 0 and col0 % tn == 0
    cb = col0 // tn

    def body(*refs):
        if rope is None:
            a_ref, b_ref, o_ref = refs
        else:
            a_ref, b_ref, t_ref, o_ref = refs
        acc = _dot_nn(a_ref[...], b_ref[...])
        if rope is not None:
            t0, t1, t2 = (jnp.tile(t_ref[i], (1, tn // LANE)) for i in range(3))
            roped = (acc * t0 + pltpu.roll(acc, tn - ROT_DIM // 2, axis=1) * t1
                     + pltpu.roll(acc, ROT_DIM // 2, axis=1) * t2)
            acc = jnp.where(pl.program_id(1) < rope[1] // tn, roped, acc)
        o_ref[...] = acc.astype(out_dtype)

    in_specs = [pl.BlockSpec((tm, K), lambda i, j: (i, 0)), pl.BlockSpec((K, tn), lambda i, j: (0, cb + j))]
    args = [h, wfull]
    if rope is not None:
        in_specs.append(pl.BlockSpec((3, tm, LANE), lambda i, j: (0, i, 0)))
        args.append(rope[0])
    return pl.pallas_call(
        body, name=name, grid=(S // tm, ncols // tn),
        in_specs=in_specs, out_specs=pl.BlockSpec((tm, tn), lambda i, j: (i, j)),
        out_shape=jax.ShapeDtypeStruct((S, ncols), out_dtype),
        compiler_params=_cparams(),
    )(*args)


def _out_proj_norm(y, wo, xres, g, name):
    S, W = y.shape
    D = wo.shape[1]
    tm = _row_tile(S, EPI_TILE)

    def body(a_ref, b_ref, r_ref, g_ref, x_ref, h_ref):
        xv = r_ref[...] + _dot_nn(a_ref[...], b_ref[...])
        x_ref[...] = xv
        r = lax.rsqrt(jnp.mean(xv * xv, axis=-1, keepdims=True) + RMS_EPS)
        h_ref[...] = ((xv * r) * g_ref[...]).astype(BF16)

    row = pl.BlockSpec((tm, D), lambda i: (i, 0))
    return pl.pallas_call(
        body, name=name, grid=(S // tm,),
        in_specs=[pl.BlockSpec((tm, W), lambda i: (i, 0)), pl.BlockSpec((W, D), lambda i: (0, 0)), row,
                  pl.BlockSpec((1, D), lambda i: (0, 0))],
        out_specs=[row, row],
        out_shape=[jax.ShapeDtypeStruct((S, D), F32), jax.ShapeDtypeStruct((S, D), BF16)],
        compiler_params=_cparams(),
    )(y, wo, xres, g)


def _out_proj_loss(y, wo, xres, tgt, g, name):
    S, W = y.shape
    D = wo.shape[1]
    tm = _row_tile(S, EPI_TILE)

    def body(a_ref, b_ref, r_ref, t_ref, g_ref, dx_ref, dxb_ref, dg_ref, loss_ref):
        xv = r_ref[...] + _dot_nn(a_ref[...], b_ref[...])
        r = lax.rsqrt(jnp.mean(xv * xv, axis=-1, keepdims=True) + RMS_EPS)
        xhat = xv * r
        gv = g_ref[...]
        err = xhat * gv - t_ref[...]
        d = err * (1.0 / D)
        gd = d * gv
        dx = r * (gd - xhat * jnp.mean(gd * xhat, axis=-1, keepdims=True))
        dx_ref[...] = dx
        dxb_ref[...] = dx.astype(BF16)

        @pl.when(pl.program_id(0) == 0)
        def _():
            dg_ref[...] = jnp.zeros(dg_ref.shape, F32)
            loss_ref[...] = jnp.zeros(loss_ref.shape, F32)
        dg_ref[...] += jnp.sum(d * xhat, axis=0, keepdims=True)
        per_tok = jnp.sum(err * err, axis=-1, keepdims=True) * (1.0 / D)
        loss_ref[...] += 0.5 * jnp.sum(per_tok, axis=0, keepdims=True)

    row = pl.BlockSpec((tm, D), lambda i: (i, 0))
    vec = pl.BlockSpec((1, D), lambda i: (0, 0))
    return pl.pallas_call(
        body, name=name, grid=(S // tm,),
        in_specs=[pl.BlockSpec((tm, W), lambda i: (i, 0)), pl.BlockSpec((W, D), lambda i: (0, 0)), row, row, vec],
        out_specs=[row, row, vec, pl.BlockSpec((1, LANE), lambda i: (0, 0))],
        out_shape=[jax.ShapeDtypeStruct((S, D), F32), jax.ShapeDtypeStruct((S, D), BF16),
                   jax.ShapeDtypeStruct((1, D), F32), jax.ShapeDtypeStruct((1, LANE), F32)],
        compiler_params=_cparams(),
    )(y, wo, xres, tgt, g)


def _matmul_nt(parts, wfull, out_rows, name):
    S = parts[0][0].shape[-2]
    tm, tn = _row_tile(S, 2 * MM_TILE), _tile(out_rows, MM_TILE)
    lone = len(parts) == 1 and parts[0][1] != "stack"
    plan, lo = [], 0
    for arr, lead, col0 in parts:
        n_p = arr.shape[-1]
        tk = _tile(n_p, 2 * MM_TILE if lone else MM_TILE)
        tk = math.gcd(tk, col0) if col0 else tk
        steps = n_p // tk * (arr.shape[0] if lead == "stack" else 1)
        plan.append((lead, col0 // tk, tk, lo, lo + steps))
        lo += steps
    nk = lo
    npart = len(parts)

    def body(*refs):
        a_refs, w_refs, o_ref = refs[:npart], refs[npart:2 * npart], refs[2 * npart]
        k = pl.program_id(2)
        for p, (_, _, _, lo_p, hi_p) in enumerate(plan):
            if lo_p == 0:
                @pl.when(k == 0)
                def _(p=p):
                    o_ref[...] = _dot_nt(a_refs[p][...], w_refs[p][...])
                lo_p = 1
            if hi_p > lo_p:
                @pl.when((k >= lo_p) & (k < hi_p))
                def _(p=p):
                    o_ref[...] += _dot_nt(a_refs[p][...], w_refs[p][...])

    in_specs, args = [], []
    for (arr, lead, col0), (_, cb, tk, lo_p, hi_p) in zip(parts, plan):
        def kk(k, lo_p=lo_p, hi_p=hi_p):
            return jnp.clip(k - lo_p, 0, hi_p - lo_p - 1)
        if lead is None:
            in_specs.append(pl.BlockSpec((tm, tk), lambda i, j, k, kk=kk: (i, kk(k))))
        elif lead == "stack":
            nkb = arr.shape[-1] // tk
            in_specs.append(pl.BlockSpec((None, tm, tk), lambda i, j, k, kk=kk, nkb=nkb: (kk(k) // nkb, i, kk(k) % nkb)))
        else:
            in_specs.append(pl.BlockSpec((None, tm, tk), lambda i, j, k, kk=kk, lead=lead: (lead, i, kk(k))))
        args.append(arr)
    for (_, cb, tk, lo_p, hi_p) in plan:
        def kk(k, lo_p=lo_p, hi_p=hi_p):
            return jnp.clip(k - lo_p, 0, hi_p - lo_p - 1)
        in_specs.append(pl.BlockSpec((tn, tk), lambda i, j, k, kk=kk, cb=cb: (j, cb + kk(k))))
        args.append(wfull)
    return pl.pallas_call(
        body, name=name, grid=(S // tm, out_rows // tn, nk),
        in_specs=in_specs, out_specs=pl.BlockSpec((tm, tn), lambda i, j, k: (i, j)),
        out_shape=jax.ShapeDtypeStruct((S, out_rows), F32),
        compiler_params=_cparams(),
    )(*args)


def _matmul_tn(a, parts, total, name, tile_major=False, also_bf16=False, rows=None, after=()):
    S = a.shape[0]
    m0, M = rows or (0, a.shape[1])
    tm = _tile(M, MM_TILE)
    ib = m0 // tm
    nout = 2 if also_bf16 else 1
    outs = None
    for idx, (arr, lead, col0) in enumerate(parts):
        n_p = arr.shape[-1]
        tn = math.gcd(_tile(n_p, MM_TILE), col0) if col0 else _tile(n_p, MM_TILE)
        cb = col0 // tn
        nb = n_p // tn
        if lead == "stack":
            n_p *= arr.shape[0]

        def body(*refs, tn=tn):
            a_ref, b_ref = refs[0], refs[1]
            o_refs = refs[-nout:]
            acc = _dot_tn(a_ref[...], b_ref[...])
            for o_ref in o_refs:
                if tile_major:
                    for t in range(tn // LANE):
                        o_ref[t] = acc[:, LANE * t:LANE * (t + 1)].astype(o_ref.dtype)
                else:
                    o_ref[...] = acc.astype(o_ref.dtype)

        in_specs = [pl.BlockSpec((S, tm), lambda i, j: (0, ib + i), pipeline_mode=pl.Buffered(1))]
        if lead is None:
            in_specs.append(pl.BlockSpec((S, tn), lambda i, j: (0, j)))
        elif lead == "stack":
            in_specs.append(pl.BlockSpec((None, S, tn), lambda i, j, nb=nb: (j // nb, 0, j % nb)))
        else:
            in_specs.append(pl.BlockSpec((None, S, tn), lambda i, j, lead=lead: (lead, 0, j)))
        args = [a, arr]
        aliases = {}
        if outs is not None:
            in_specs += [pl.BlockSpec(memory_space=pl.ANY)] * nout
            args += list(outs)
            aliases = {2 + o: o for o in range(nout)}
        else:
            in_specs += [pl.BlockSpec(memory_space=pl.ANY)] * len(after)
            args += list(after)
        if tile_major:
            out_spec = pl.BlockSpec((tn // LANE, tm, LANE), lambda i, j, cb=cb: (cb + j, i, 0))
            shape = (total // LANE, M, LANE)
        else:
            out_spec = pl.BlockSpec((tm, tn), lambda i, j, cb=cb: (i, cb + j))
            shape = (M, total)
        outs = pl.pallas_call(
            body, name=f"{name}_{idx}", grid=(M // tm, n_p // tn),
            in_specs=in_specs, out_specs=[out_spec] * nout,
            out_shape=[jax.ShapeDtypeStruct(shape, dt) for dt in (F32, BF16)[:nout]],
            input_output_aliases=aliases,
            compiler_params=_cparams(),
        )(*args)
    return tuple(outs) if also_bf16 else outs[0]


def _log_sigmoid(z):
    e = jnp.exp(-jnp.abs(z))
    return jnp.minimum(z, 0.0) - jnp.where(e < 1e-4, e * (1.0 - 0.5 * e), jnp.log(1.0 + e))


def _tri_sum(tri, x):
    hi, mid, lo = _split3(x)
    return _dot_nn(tri, hi) + _dot_nn(tri, mid) + _dot_nn(tri, lo)


def _fox_gate_fwd(fl, bias):
    S = fl.shape[0]

    nb_ = _row_tile(S, ROW_TILE)

    def body(f_ref, b_ref, c_ref):
        ri = lax.broadcasted_iota(jnp.int32, (nb_, nb_), 0)
        ci = lax.broadcasted_iota(jnp.int32, (nb_, nb_), 1)
        tri = jnp.where(ri >= ci, 1.0, 0.0).astype(BF16)
        row = lax.broadcasted_iota(jnp.int32, (nb_, LANE), 0)

        def step(i, carry):
            r0 = pl.multiple_of(i * nb_, nb_)
            t = _tri_sum(tri, _log_sigmoid(f_ref[pl.ds(r0, nb_), :] + b_ref[...])) + carry
            c_ref[pl.ds(r0, nb_), :] = t
            return jnp.sum(jnp.where(row == nb_ - 1, t, 0.0), axis=0, keepdims=True)

        lax.fori_loop(0, S // nb_, step, jnp.zeros((1, LANE), F32))

    vm = pl.BlockSpec(memory_space=pltpu.VMEM)
    return pl.pallas_call(
        body, name="fox_gate_fwd", in_specs=[vm, vm], out_specs=vm,
        out_shape=jax.ShapeDtypeStruct((S, LANE), F32),
        compiler_params=_cparams(),
    )(fl, bias)


def _fox_gate_bwd(fl, bias, dc):
    S = fl.shape[0]

    nb_ = _row_tile(S, ROW_TILE)

    def body(f_ref, b_ref, d_ref, o_ref, db_ref):
        ri = lax.broadcasted_iota(jnp.int32, (nb_, nb_), 0)
        ci = lax.broadcasted_iota(jnp.int32, (nb_, nb_), 1)
        tri = jnp.where(ri <= ci, 1.0, 0.0).astype(BF16)
        row = lax.broadcasted_iota(jnp.int32, (nb_, LANE), 0)
        nt = S // nb_

        def step(ii, carry):
            carry_c, carry_b = carry
            r0 = pl.multiple_of((nt - 1 - ii) * nb_, nb_)
            t = _tri_sum(tri, d_ref[pl.ds(r0, nb_), :]) + carry_c
            dz = t * _sigmoid(-(f_ref[pl.ds(r0, nb_), :] + b_ref[...]))
            o_ref[pl.ds(r0, nb_), :] = dz.astype(BF16)
            first = jnp.sum(jnp.where(row == 0, t, 0.0), axis=0, keepdims=True)
            return first, carry_b + jnp.sum(dz, axis=0, keepdims=True)

        zero = jnp.zeros((1, LANE), F32)
        _, db = lax.fori_loop(0, nt, step, (zero, zero))
        db_ref[...] = db

    vm = pl.BlockSpec(memory_space=pltpu.VMEM)
    return pl.pallas_call(
        body, name="fox_gate_bwd", in_specs=[vm, vm, vm], out_specs=[vm, vm],
        out_shape=[jax.ShapeDtypeStruct((S, LANE), BF16), jax.ShapeDtypeStruct((1, LANE), F32)],
        compiler_params=_cparams(),
    )(fl, bias, dc)


def _bias_lanes(col, lane, e, first):
    o0 = HEAD_DIM * (1 - e)
    hi, mid, lo = _split3(col)
    d0 = o0 if first else o0 + 3
    t = jnp.where((lane >= o0) & (lane < o0 + 6), jnp.ones(lane.shape, BF16), jnp.zeros(lane.shape, BF16))
    t = jnp.where(lane == d0, hi, t)
    t = jnp.where(lane == d0 + 1, mid, t)
    return jnp.where(lane == d0 + 2, lo, t)


def _fox_fwd(qkvg, c, H, gather=()):
    na = len(gather)
    S = qkvg.shape[0]
    W = H * HEAD_DIM
    HP = H // 2
    PP = 2 if HP % 2 == 0 else 1
    NE = 2 * PP
    tq = _row_tile(S, ATT_TILE)
    nq = S // tq
    wb = W // LANE
    scale = HEAD_DIM ** -0.5

    def body(*refs):
        q_ref, k_ref, v_ref, g_ref, c_ref = refs[:5]
        y_ref, o_ref, a_ref = refs[5 + na:8 + na]
        kaug_sc, vaug_sc, qaug_sc, s_sc, mb_sc, m_sc, acc_sc = refs[8 + 2 * na:15 + 2 * na]
        hp, qi = pl.program_id(0), pl.program_id(1)
        if na:
            remote, local = _direct_gather_copies(refs[5:5 + na], refs[8 + na:8 + 2 * na], *refs[15 + 2 * na:])

            @pl.when((hp == 0) & (qi == 0))
            def _():
                for cp in remote + local:
                    cp.start()
        lane = lax.broadcasted_iota(jnp.int32, (tq, LANE), 1)
        own = [lane < HEAD_DIM, lane >= HEAD_DIM]
        rows = lax.broadcasted_iota(jnp.int32, (tq, tq), 0)
        cols = lax.broadcasted_iota(jnp.int32, (tq, tq), 1)

        def bias_lanes(col, e, first):
            return _bias_lanes(col, lane, e % 2, first)

        def head_col(tile, e):
            return jnp.sum(jnp.where(lane == 2 * PP * hp + e, tile, 0.0), axis=1, keepdims=True)

        def tile_of(e):
            return slice(LANE * (e // 2), LANE * (e // 2 + 1))

        @pl.when(qi == 0)
        def _():
            def chunk(i, carry):
                r0 = pl.multiple_of(i * tq, tq)
                cb = c_ref[pl.ds(r0, tq), :]
                for e in range(NE):
                    kb, vb = k_ref[pl.ds(r0, tq), tile_of(e)], v_ref[pl.ds(r0, tq), tile_of(e)]
                    kaug_sc[e, pl.ds(r0, tq), :] = jnp.where(own[e % 2], kb, bias_lanes(-head_col(cb, e), e, False))
                    vaug_sc[e, pl.ds(r0, tq), :] = jnp.where(own[e % 2], vb, jnp.ones((tq, LANE), BF16))
                return carry
            lax.fori_loop(0, nq, chunk, 0)

        crow = c_ref[pl.ds(pl.multiple_of(qi * tq, tq), tq), :]
        ctq = [head_col(crow, e) for e in range(NE)]
        for e in range(NE):
            q = q_ref[:, tile_of(e)] * jnp.asarray(scale, BF16)
            qaug_sc[e] = jnp.where(own[e % 2], q, bias_lanes(ctq[e], e, True))
        m_sc[...] = jnp.full(m_sc.shape, NEG_INF, F32)
        acc_sc[...] = jnp.zeros(acc_sc.shape, F32)

        def scores(blk, slot, masked):
            k0 = pl.multiple_of(blk * tq, tq)
            for e in range(NE):
                s = _dot_nt(qaug_sc[e], kaug_sc[e, pl.ds(k0, tq), :])
                if masked:
                    s = jnp.where(rows >= cols, s, NEG_INF)
                s_sc[slot, e] = s
                mb_sc[slot, e] = jnp.broadcast_to(jnp.max(s, axis=1, keepdims=True), (tq, LANE))

        def accumulate(blk, slot):
            k0 = pl.multiple_of(blk * tq, tq)
            for e in range(NE):
                m_prev = m_sc[e]
                m_new = jnp.maximum(m_prev, mb_sc[slot, e])
                p = jnp.exp(s_sc[slot, e] - jnp.tile(m_new, (1, tq // LANE)))
                acc_sc[e] = jnp.exp(m_prev - m_new) * acc_sc[e] + _dot_nn(p.astype(BF16), vaug_sc[e, pl.ds(k0, tq), :])
                m_sc[e] = m_new

        def block_of(t):
            return jnp.where(t == 0, qi, t - 1)

        scores(qi, 0, True)

        def loop_body(t, carry):
            scores(t, (t + 1) % 2, False)
            accumulate(block_of(t), t % 2)
            return carry

        lax.fori_loop(0, qi, loop_body, 0)
        accumulate(block_of(qi), qi % 2)
        o_e, a_e = [], []
        for e in range(NE):
            acc = acc_sc[e]
            l = pltpu.roll(acc, HEAD_DIM, axis=1)
            o_e.append(acc / l)
            a_e.append(ctq[e] - (m_sc[e] + jnp.log(l)))
        for pp in range(PP):
            o = jnp.where(own[0], o_e[2 * pp], o_e[2 * pp + 1])
            g = g_ref[:, tile_of(2 * pp)].astype(F32)
            y_ref[:, tile_of(2 * pp)] = (o * (g * _sigmoid(g))).astype(BF16)
            o_ref[:, tile_of(2 * pp)] = o.astype(BF16)
            a_ref[pp] = jnp.where(own[0], a_e[2 * pp], a_e[2 * pp + 1])
        if na:
            @pl.when((hp == HP // PP - 1) & (qi == nq - 1))
            def _():
                _wait_all(remote, local)

    any_spec = pl.BlockSpec(memory_space=pl.ANY)
    sems = [pltpu.SemaphoreType.DMA((na, N_DEV - 1)), pltpu.SemaphoreType.DMA((na, N_DEV - 1)),
            pltpu.SemaphoreType.DMA((na,))] if na else []
    wide = PP * LANE
    outs = pl.pallas_call(
        body, name="fox_attn_fwd", grid=(HP // PP, nq),
        in_specs=[pl.BlockSpec((tq, wide), lambda h, i: (i, h)),
                  pl.BlockSpec((S, wide), lambda h, i: (0, wb // PP + h)),
                  pl.BlockSpec((S, wide), lambda h, i: (0, 2 * wb // PP + h)),
                  pl.BlockSpec((tq, wide), lambda h, i: (i, 3 * wb // PP + h)),
                  pl.BlockSpec((S, LANE), lambda h, i: (0, 0))] + [any_spec] * na,
        out_specs=[pl.BlockSpec((tq, wide), lambda h, i: (i, h)),
                   pl.BlockSpec((tq, wide), lambda h, i: (i, h)),
                   pl.BlockSpec((PP, tq, LANE), lambda h, i: (h, i, 0))] + [any_spec] * na,
        out_shape=[jax.ShapeDtypeStruct((S, W), BF16), jax.ShapeDtypeStruct((S, W), BF16),
                   jax.ShapeDtypeStruct((HP, S, LANE), F32)]
        + [jax.ShapeDtypeStruct((N_DEV,) + g.shape, g.dtype) for g in gather],
        scratch_shapes=[pltpu.VMEM((NE, S, LANE), BF16), pltpu.VMEM((NE, S, LANE), BF16),
                        pltpu.VMEM((NE, tq, LANE), BF16), pltpu.VMEM((2, NE, tq, tq), F32),
                        pltpu.VMEM((2, NE, tq, LANE), F32), pltpu.VMEM((NE, tq, LANE), F32),
                        pltpu.VMEM((NE, tq, LANE), F32)] + sems,
        compiler_params=_cparams(),
    )(qkvg, qkvg, qkvg, qkvg, c, *gather)
    return outs[0], outs[1], outs[2], list(outs[3:])


def _fox_out_bwd(dxb, wo, qkvg, o, a, H):
    S, D = dxb.shape
    W = H * HEAD_DIM
    tm, tn = _row_tile(S, EPI_TILE), _tile(W, EPI_TILE)
    npair = tn // LANE
    scale = HEAD_DIM ** -0.5

    def body(dx_ref, w_ref, q_ref, g_ref, o_ref, a_ref, qa_ref, da_ref, dg_ref):
        dy = _dot_nt(dx_ref[...], w_ref[...])
        lane = lax.broadcasted_iota(jnp.int32, (tm, LANE), 1)
        own = [lane < HEAD_DIM, lane >= HEAD_DIM]
        for p in range(npair):
            cols = slice(LANE * p, LANE * (p + 1))
            q = q_ref[:, cols] * jnp.asarray(scale, BF16)
            dyv, g, ov, at = dy[:, cols], g_ref[:, cols].astype(F32), o_ref[:, cols].astype(F32), a_ref[p]
            sg = _sigmoid(g)
            dob = (dyv * (g * sg)).astype(BF16)
            dg_ref[:, cols] = (dyv * ov * (sg * (1.0 + g * (1.0 - sg)))).astype(BF16)
            prod = dob.astype(F32) * ov
            for e in range(2):
                a_col = jnp.max(jnp.where(own[e], at, -jnp.inf), axis=1, keepdims=True)
                d_col = jnp.sum(jnp.where(own[e], prod, 0.0), axis=1, keepdims=True)
                qa_ref[e, :, cols] = jnp.where(own[e], q, _bias_lanes(a_col, lane, e, True))
                da_ref[e, :, cols] = jnp.where(own[e], dob, _bias_lanes(-d_col, lane, e, True))

    blk = pl.BlockSpec((tm, tn), lambda i, j: (i, j))
    pair = pl.BlockSpec((2, tm, tn), lambda i, j: (0, i, j))
    return pl.pallas_call(
        body, name="fox_out_bwd", grid=(S // tm, W // tn),
        in_specs=[pl.BlockSpec((tm, D), lambda i, j: (i, 0)), pl.BlockSpec((tn, D), lambda i, j: (j, 0)),
                  blk, pl.BlockSpec((tm, tn), lambda i, j: (i, 3 * W // tn + j)), blk,
                  pl.BlockSpec((npair, tm, LANE), lambda i, j: (j, i, 0))],
        out_specs=[pair, pair, pl.BlockSpec((None, tm, tn), lambda i, j: (3, i, j))],
        out_shape=[jax.ShapeDtypeStruct((2, S, W), BF16), jax.ShapeDtypeStruct((2, S, W), BF16),
                   jax.ShapeDtypeStruct((4, S, W), BF16)],
        compiler_params=_cparams(),
    )(dxb, wo, qkvg, qkvg, o, a)


def _fox_bwd(qaug, doaug, qkv, c, dqkvg, H, scatter=(), scatter_specs=()):
    na = len(scatter)
    S = qkv.shape[0]
    W = H * HEAD_DIM
    HP = H // 2
    tq = _row_tile(S, ATT_TILE)
    nq = S // tq
    wb = W // LANE
    scale = HEAD_DIM ** -0.5

    def body(*refs):
        qa_ref, da_ref, k_ref, v_ref, c_ref = refs[:5]
        out_ref, dcr_ref, dcc_ref = refs[6 + na:9 + na]
        dq_sc, dk_sc, dv_sc = refs[9 + 2 * na:12 + 2 * na]
        hp, kj = pl.program_id(0), pl.program_id(1)
        if na:
            remote = _direct_scatter_copies(refs[5:5 + na], refs[9 + na:9 + 2 * na], scatter_specs,
                                            *refs[12 + 2 * na:])

            @pl.when((hp == 0) & (kj == 0))
            def _():
                for cp in remote:
                    cp.start()
        lane = lax.broadcasted_iota(jnp.int32, (tq, LANE), 1)
        own = [lane < HEAD_DIM, lane >= HEAD_DIM]
        rows = lax.broadcasted_iota(jnp.int32, (tq, tq), 0)
        cols = lax.broadcasted_iota(jnp.int32, (tq, tq), 1)

        @pl.when(kj == 0)
        def _():
            dq_sc[...] = jnp.zeros(dq_sc.shape, F32)

        @pl.when((kj == 0) & (hp == 0))
        def _():
            dcr_ref[...] = jnp.zeros(dcr_ref.shape, F32)
            dcc_ref[...] = jnp.zeros(dcc_ref.shape, F32)

        kblk, vblk, cblk = k_ref[...], v_ref[...], c_ref[...]
        one, zero = jnp.ones((tq, LANE), BF16), jnp.zeros((tq, LANE), BF16)
        ka, va = [], []
        for e in range(2):
            o0 = HEAD_DIM * (1 - e)
            c_col = jnp.sum(jnp.where(lane == 2 * hp + e, cblk, 0.0), axis=1, keepdims=True)
            ka.append(jnp.where(own[e], kblk, _bias_lanes(-c_col, lane, e, False)))
            va.append(jnp.where(own[e], vblk, jnp.where((lane >= o0) & (lane < o0 + 3), one, zero)))
        dk_sc[...] = jnp.zeros(dk_sc.shape, F32)
        dv_sc[...] = jnp.zeros(dv_sc.shape, F32)

        def step(i, masked):
            r0 = pl.multiple_of(i * tq, tq)
            for e in range(2):
                qa = qa_ref[e, pl.ds(r0, tq), :]
                da = da_ref[e, pl.ds(r0, tq), :]
                p = jnp.exp(_dot_nt(qa, ka[e]))
                if masked:
                    p = jnp.where(rows >= cols, p, 0.0)
                ds = p * _dot_nt(da, va[e])
                pb, dsb = p.astype(BF16), ds.astype(BF16)
                dv_sc[e] += _dot_tn(pb, da)
                dk_sc[e] += _dot_tn(dsb, qa)
                dq_sc[e, pl.ds(r0, tq), :] += _dot_nn(dsb, ka[e])

        step(kj, True)

        def loop_body(i, carry):
            step(i, False)
            return carry

        lax.fori_loop(kj + 1, nq, loop_body, 0)
        k0 = pl.multiple_of(kj * tq, tq)
        out_ref[1, pl.ds(k0, tq), :] = jnp.where(own[0], dk_sc[0], dk_sc[1]).astype(BF16)
        out_ref[2, pl.ds(k0, tq), :] = jnp.where(own[0], dv_sc[0], dv_sc[1]).astype(BF16)

        def put_lane(ref, r0, e, tile, src_lane):
            col = jnp.sum(jnp.where(lane == src_lane, tile, 0.0), axis=1, keepdims=True)
            ref[pl.ds(r0, tq), :] = jnp.where(lane == 2 * hp + e, col, ref[pl.ds(r0, tq), :])

        for e in range(2):
            put_lane(dcc_ref, k0, e, dk_sc[e], HEAD_DIM * (1 - e) + 3)

        @pl.when(kj == nq - 1)
        def _():
            def chunk(i, carry):
                r0 = pl.multiple_of(i * tq, tq)
                d0, d1 = dq_sc[0, pl.ds(r0, tq), :], dq_sc[1, pl.ds(r0, tq), :]
                out_ref[0, pl.ds(r0, tq), :] = (jnp.where(own[0], d0, d1) * scale).astype(BF16)
                put_lane(dcr_ref, r0, 0, d0, HEAD_DIM)
                put_lane(dcr_ref, r0, 1, d1, 0)
                return carry
            lax.fori_loop(0, nq, chunk, 0)

        if na:
            @pl.when((hp == HP - 1) & (kj == nq - 1))
            def _():
                _wait_all(remote)

    pair = pl.BlockSpec((2, S, LANE), lambda h, j: (0, 0, h))
    vec = pl.BlockSpec((S, LANE), lambda h, j: (0, 0))
    any_spec = pl.BlockSpec(memory_space=pl.ANY)
    sems = [pltpu.SemaphoreType.DMA((na, N_DEV - 1)), pltpu.SemaphoreType.DMA((na, N_DEV - 1))] if na else []
    outs = pl.pallas_call(
        body, name="fox_attn_bwd", grid=(HP, nq),
        in_specs=[pair, pair,
                  pl.BlockSpec((tq, LANE), lambda h, j: (j, wb + h)),
                  pl.BlockSpec((tq, LANE), lambda h, j: (j, 2 * wb + h)),
                  pl.BlockSpec((tq, LANE), lambda h, j: (j, 0))] + [any_spec] * (na + 1),
        out_specs=[pl.BlockSpec((3, S, LANE), lambda h, j: (0, 0, h)), vec, vec] + [any_spec] * na,
        out_shape=[jax.ShapeDtypeStruct(dqkvg.shape, BF16), jax.ShapeDtypeStruct((S, LANE), F32),
                   jax.ShapeDtypeStruct((S, LANE), F32)]
        + [jax.ShapeDtypeStruct((N_DEV - 1,) + _scatter_block_shape(g, s), g.dtype)
           for g, s in zip(scatter, scatter_specs)],
        scratch_shapes=[pltpu.VMEM((2, S, LANE), F32), pltpu.VMEM((2, tq, LANE), F32),
                        pltpu.VMEM((2, tq, LANE), F32)] + sems,
        input_output_aliases={5 + na: 0},
        compiler_params=_cparams(),
    )(qaug, doaug, qkv, qkv, c, *scatter, dqkvg)
    return outs[0], outs[1], outs[2], list(outs[3:])


def _swa_pick(blk, half, lane):
    b = blk.astype(F32)
    r = pltpu.roll(b, HEAD_DIM, axis=1)
    return jnp.where(jnp.logical_xor(lane < HEAD_DIM, half == 1), b, r).astype(BF16)


def _swa_stack(t, lane, G):
    pieces = []
    z = jnp.zeros((SWA_BLOCK, LANE), t.dtype)
    for j in range(G // 2):
        tile = t[:, LANE * j:LANE * (j + 1)]
        pieces += [jnp.where(lane < HEAD_DIM, tile, z), jnp.where(lane < HEAD_DIM, z, tile)]
    return jnp.concatenate(pieces, axis=0)


def _swa_unstack(st, lane, G):
    tiles = []
    for j in range(G // 2):
        a = st[2 * j * SWA_BLOCK:(2 * j + 1) * SWA_BLOCK]
        b = st[(2 * j + 1) * SWA_BLOCK:(2 * j + 2) * SWA_BLOCK]
        tiles.append(jnp.where(lane < HEAD_DIM, a, b))
    return jnp.concatenate(tiles, axis=1)


def _swa_mask_bias(G):
    R = G * SWA_BLOCK
    t_loc = jnp.arange(R)[:, None] % SWA_BLOCK
    j_loc = jnp.arange(2 * SWA_BLOCK)[None, :]
    diff = t_loc + SWA_BLOCK - j_loc
    band = (diff >= 0) & (diff < SWA_BLOCK)
    return jnp.stack([jnp.where(band & (j_loc >= SWA_BLOCK), 0.0, NEG_INF),
                      jnp.where(band, 0.0, NEG_INF)]).astype(F32)


def _swa_scores(q, kp, kc, vp, vc, srow, bias, half, head0, G):
    lane = lax.broadcasted_iota(jnp.int32, (SWA_BLOCK, LANE), 1)
    kk = jnp.concatenate([_swa_pick(kp, half, lane), _swa_pick(kc, half, lane)], axis=0)
    vv = jnp.concatenate([_swa_pick(vp, half, lane), _swa_pick(vc, half, lane)], axis=0)
    qstack = _swa_stack(q, lane, G) * jnp.asarray(HEAD_DIM ** -0.5, BF16)
    s = _dot_nt(qstack, kk) + bias
    R = G * SWA_BLOCK
    lane1 = lax.broadcasted_iota(jnp.int32, (1, LANE), 1)
    sink = jnp.concatenate(
        [jnp.broadcast_to(jnp.sum(jnp.where(lane1 == head0 + g, srow, 0.0), axis=1, keepdims=True), (SWA_BLOCK, LANE))
         for g in range(G)], axis=0)
    m = jnp.maximum(jnp.broadcast_to(jnp.max(s, axis=1, keepdims=True), (R, LANE)), sink)
    e = jnp.exp(s - jnp.tile(m, (1, 2)))
    es = jnp.exp(sink - m)
    inv = 1.0 / (jnp.broadcast_to(jnp.sum(e, axis=1, keepdims=True), (R, LANE)) + es)
    return qstack, kk, vv, e * jnp.tile(inv, (1, 2)), es * inv, lane


def _swa_fwd(q, kv, gate, sinks, mask_bias, HQ, HKV):
    S = q.shape[0]
    G = HQ // HKV
    WQ, KVW = HQ * HEAD_DIM, HKV * HEAD_DIM
    nb = S // SWA_BLOCK
    GW = G * HEAD_DIM
    kb, vb = 0, KVW // LANE
    NH = min(HKV, 4)
    NP = NH // 2

    def body(q_ref, kp_ref, kc_ref, vp_ref, vc_ref, g_ref, sink_ref, b_ref, y_ref, o_ref):
        grp = pl.program_id(0)
        for hh in range(NH):
            cols, kt = slice(GW * hh, GW * (hh + 1)), slice(LANE * (hh // 2), LANE * (hh // 2 + 1))
            _, _, vv, p, _, lane = _swa_scores(q_ref[:, cols], kp_ref[:, kt], kc_ref[:, kt], vp_ref[:, kt], vc_ref[:, kt],
                                               sink_ref[...], b_ref[0], hh % 2, (NH * grp + hh) * G, G)
            o = _swa_unstack(_dot_nn(p.astype(BF16), vv), lane, G)
            g = g_ref[:, cols].astype(F32)
            y_ref[:, cols] = (o * (g * _sigmoid(g))).astype(BF16)
            o_ref[:, cols] = o.astype(BF16)

    blk = lambda cb, prev: pl.BlockSpec(
        (SWA_BLOCK, NP * LANE), lambda h, n, cb=cb, prev=prev: (jnp.maximum(n - prev, 0), cb // NP + h))
    qspec = pl.BlockSpec((SWA_BLOCK, NH * GW), lambda h, n: (n, h))
    return pl.pallas_call(
        body, name="swa_attn_fwd", grid=(HKV // NH, nb),
        in_specs=[qspec, blk(kb, 1), blk(kb, 0), blk(vb, 1), blk(vb, 0), qspec,
                  pl.BlockSpec((1, LANE), lambda h, n: (0, 0)),
                  pl.BlockSpec((1, G * SWA_BLOCK, 2 * SWA_BLOCK), lambda h, n: (jnp.minimum(n, 1), 0, 0))],
        out_specs=[qspec, qspec],
        out_shape=[jax.ShapeDtypeStruct((S, WQ), BF16), jax.ShapeDtypeStruct((S, WQ), BF16)],
        compiler_params=_cparams(),
    )(q, kv, kv, kv, kv, gate, sinks, mask_bias)


def _swa_bwd(q, kv, dy, gate, o, sinks, tables, mask_bias, HQ, HKV):
    S = q.shape[0]
    G = HQ // HKV
    WQ, KVW = HQ * HEAD_DIM, HKV * HEAD_DIM
    nb = S // SWA_BLOCK
    GW = G * HEAD_DIM
    R = G * SWA_BLOCK
    kb, vb = 0, KVW // LANE
    scale = HEAD_DIM ** -0.5
    NH = min(HKV, 4)
    NP = NH // 2
    assert G == 8

    def body(q_ref, kp_ref, kc_ref, vp_ref, vc_ref, dy_ref, g_ref, o_ref, sink_ref, t_ref, b_ref,
             dqg_ref, dkv_ref, dsink_ref, carry_sc):
        grp, n = pl.program_id(0), pl.program_id(1)

        @pl.when(n == 0)
        def _():
            carry_sc[...] = jnp.zeros(carry_sc.shape, F32)
            dsink_ref[...] = jnp.zeros(dsink_ref.shape, F32)

        @pl.when(n < nb)
        def _():
            t0, t1, t2 = (jnp.tile(t_ref[i], (1, GW // LANE)) for i in range(3))
            for hh in range(NH):
                cols, kt = slice(GW * hh, GW * (hh + 1)), slice(LANE * (hh // 2), LANE * (hh // 2 + 1))
                qstack, kk, vv, p, psink, lane = _swa_scores(
                    q_ref[:, cols], kp_ref[:, kt], kc_ref[:, kt], vp_ref[:, kt], vc_ref[:, kt], sink_ref[...], b_ref[0],
                    hh % 2, (NH * grp + hh) * G, G)
                dyv, g, ov = dy_ref[:, cols], g_ref[:, cols].astype(F32), o_ref[:, cols].astype(F32)
                sg = _sigmoid(g)
                dob = (dyv * (g * sg)).astype(BF16)
                dqg_ref[1, :, cols] = (dyv * ov * (sg * (1.0 + g * (1.0 - sg)))).astype(BF16)
                prod = dob.astype(F32) * ov
                dparts = []
                for j in range(G // 2):
                    tile = prod[:, LANE * j:LANE * (j + 1)]
                    for sel in (jnp.where(lane < HEAD_DIM, tile, 0.0), jnp.where(lane < HEAD_DIM, 0.0, tile)):
                        dparts.append(jnp.broadcast_to(jnp.sum(sel, axis=1, keepdims=True), (SWA_BLOCK, LANE)))
                delta = jnp.concatenate(dparts, axis=0)
                dostack = _swa_stack(dob, lane, G)
                ds = p * (_dot_nt(dostack, vv) - jnp.tile(delta, (1, 2)))
                dsb, pb = ds.astype(BF16), p.astype(BF16)
                dq = _swa_unstack(_dot_nn(dsb, kk), lane, G) * scale
                dq = dq * t0 + pltpu.roll(dq * t1, ROT_DIM // 2, axis=1) + pltpu.roll(dq * t2, GW - ROT_DIM // 2, axis=1)
                dqg_ref[0, :, cols] = dq.astype(BF16)
                dkk = _dot_tn(dsb, qstack)
                dvv = _dot_tn(pb, dostack)
                dkk = dkk + pltpu.roll(dkk, HEAD_DIM, axis=1)
                dvv = dvv + pltpu.roll(dvv, HEAD_DIM, axis=1)
                lane2 = lax.broadcasted_iota(jnp.int32, (2 * SWA_BLOCK, LANE), 1)
                comb = jnp.where(lane2 < HEAD_DIM, dkk, dvv)
                dkv_ref[hh] = carry_sc[hh] + comb[:SWA_BLOCK]
                carry_sc[hh] = comb[SWA_BLOCK:]
                sk = psink * delta
                rows = [-jnp.sum(sk[g_ * SWA_BLOCK:(g_ + 1) * SWA_BLOCK], axis=0, keepdims=True) for g_ in range(G)]
                dsink_ref[hh] += jnp.concatenate(rows, axis=0)

        @pl.when(n == nb)
        def _():
            dkv_ref[...] = carry_sc[...]

    cl = lambda n: jnp.minimum(n, nb - 1)
    blk = lambda cb, prev: pl.BlockSpec(
        (SWA_BLOCK, NP * LANE), lambda h, n, cb=cb, prev=prev: (jnp.maximum(cl(n) - prev, 0), cb // NP + h))
    qspec = pl.BlockSpec((SWA_BLOCK, NH * GW), lambda h, n: (cl(n), h))
    return pl.pallas_call(
        body, name="swa_attn_bwd", grid=(HKV // NH, nb + 1),
        in_specs=[qspec, blk(kb, 1), blk(kb, 0), blk(vb, 1), blk(vb, 0), qspec, qspec, qspec,
                  pl.BlockSpec((1, LANE), lambda h, n: (0, 0)),
                  pl.BlockSpec((3, SWA_BLOCK, LANE), lambda h, n: (0, cl(n), 0)),
                  pl.BlockSpec((1, R, 2 * SWA_BLOCK), lambda h, n: (jnp.minimum(n, 1), 0, 0))],
        out_specs=[pl.BlockSpec((2, SWA_BLOCK, NH * GW), lambda h, n: (0, cl(n), h)),
                   pl.BlockSpec((NH, SWA_BLOCK, LANE), lambda h, n: (h, jnp.maximum(n - 1, 0), 0)),
                   pl.BlockSpec((NH, 8, LANE), lambda h, n: (h, 0, 0))],
        out_shape=[jax.ShapeDtypeStruct((2, S, WQ), BF16), jax.ShapeDtypeStruct((HKV, S, LANE), F32),
                   jax.ShapeDtypeStruct((HKV, 8, LANE), F32)],
        scratch_shapes=[pltpu.VMEM((NH, SWA_BLOCK, LANE), F32)],
        compiler_params=_cparams(),
    )(q, kv, kv, kv, kv, dy, gate, o, sinks, tables, mask_bias)


def _swa_dkv_finish(dkv, tables):
    HKV, S, _ = dkv.shape
    KVW = HKV * HEAD_DIM
    tm = _row_tile(S, EPI_TILE)
    npair = HKV // 2

    def body(d_ref, t_ref, o_ref):
        lane = lax.broadcasted_iota(jnp.int32, (tm, LANE), 1)
        lo = lane < HEAD_DIM
        for p in range(npair):
            a, b = d_ref[2 * p], d_ref[2 * p + 1]
            tk = jnp.where(lo, a, pltpu.roll(b, HEAD_DIM, axis=1))
            tv = jnp.where(lo, pltpu.roll(a, HEAD_DIM, axis=1), b)
            tk = (tk * t_ref[0] + pltpu.roll(tk * t_ref[1], ROT_DIM // 2, axis=1)
                  + pltpu.roll(tk * t_ref[2], LANE - ROT_DIM // 2, axis=1))
            o_ref[:, LANE * p:LANE * (p + 1)] = tk.astype(BF16)
            o_ref[:, KVW + LANE * p:KVW + LANE * (p + 1)] = tv.astype(BF16)

    return pl.pallas_call(
        body, name="swa_dkv_finish", grid=(S // tm,),
        in_specs=[pl.BlockSpec((HKV, tm, LANE), lambda i: (0, i, 0)), pl.BlockSpec((3, tm, LANE), lambda i: (0, i, 0))],
        out_specs=pl.BlockSpec((tm, 2 * KVW), lambda i: (i, 0)),
        out_shape=jax.ShapeDtypeStruct((S, 2 * KVW), BF16),
        compiler_params=_cparams(),
    )(dkv, tables)


def _rope_tables(S, width):
    half = ROT_DIM // 2
    pos = jnp.arange(S, dtype=F32)
    inv_freq = ROPE_THETA ** (-jnp.arange(half, dtype=F32) / half)
    ang = pos[:, None] * inv_freq[None, :]
    cos, sin = jnp.cos(ang), jnp.sin(ang)
    one = jnp.ones((S, HEAD_DIM - ROT_DIM), F32)
    zero = jnp.zeros((S, HEAD_DIM - ROT_DIM), F32)
    zh = jnp.zeros((S, half), F32)
    t0 = jnp.concatenate([cos, cos, one], axis=1)
    t1 = jnp.concatenate([-sin, zh, zero], axis=1)
    t2 = jnp.concatenate([zh, sin, zero], axis=1)
    return jnp.stack([jnp.tile(t, (1, width // HEAD_DIM)) for t in (t0, t1, t2)])


def _pad_rows(v, row, total_rows=8):
    return jnp.pad(v, ((row, total_rows - row - v.shape[0]), (0, 0)))


def _pad_lanes(v, off, width):
    return jnp.pad(v, ((0, 0), (off, width - off - v.shape[1])))


def kernel(x, norm_g, fox_w_in, fox_b_f, fox_w_out, swa_w_in, swa_sinks, swa_w_out, final_g, loss_target, m_norm_g, m_fox_w_in, m_fox_b_f, m_fox_w_out, m_swa_w_in, m_swa_sinks, m_swa_w_out, m_final_g, v_norm_g, v_fox_w_in, v_fox_b_f, v_fox_w_out, v_swa_w_in, v_swa_sinks, v_swa_w_out, v_final_g):
    S, D = x.shape[1], x.shape[2]
    H = fox_b_f.shape[1]
    W = H * HEAD_DIM
    wf = fox_w_in.shape[2]
    ws = swa_w_in.shape[2]
    HQ = swa_sinks.shape[1]
    WQ = HQ * HEAD_DIM
    KVW = (ws * N_DEV - 2 * WQ) // 2
    HKV = KVW // HEAD_DIM
    rows_o = fox_w_out.shape[1]
    assert wf * N_DEV == 4 * W + H and rows_o * N_DEV == W and H <= LANE and HQ <= LANE
    me = _my_index()

    _, sw_f, np_f = _slab_geom(wf)
    _, sw_s, np_s = _slab_geom(ws)

    def slab(w2d, w, sw):
        return jnp.pad(w2d.astype(BF16), ((0, 0), (0, sw - w)))

    (fi_all,) = _all_gather([slab(fox_w_in[0], wf, sw_f)])
    w_fi = _assemble(fi_all, wf)
    later = [slab(swa_w_in[0], ws, sw_s), fox_w_out[0].astype(BF16), swa_w_out[0].astype(BF16)]

    x0 = x[0]
    g0, g1, gf = norm_g[0:1], norm_g[1:2], final_g[None, :]
    bias = _pad_lanes(fox_b_f, 0, LANE)
    sinks = _pad_lanes(swa_sinks, 0, LANE)
    tab_k = _rope_tables(S, LANE)
    mask_bias = _swa_mask_bias(HQ // HKV)

    h0 = _rmsnorm_fwd(x0, g0, "rmsnorm0")
    qkv0 = _proj(h0, w_fi, 0, 4 * W, BF16, "fox_in_qkvg")
    fl = _proj(h0, w_fi, 4 * W, LANE, F32, "fox_in_f")
    c = _fox_gate_fwd(fl, bias)
    y0, o0, a0, (si_all, fo_all, so_all) = _fox_fwd(qkv0, c, H, gather=later)
    w_si = _assemble(si_all, ws)
    w_fo = fo_all.reshape(W, D)
    w_so = so_all.reshape(WQ, D)
    x1, h1 = _out_proj_norm(y0, w_fo, x0, g1, "fox_out")

    q1 = _proj(h1, w_si, 0, WQ, BF16, "swa_in_q", rope=(tab_k, WQ))
    kv1 = _proj(h1, w_si, WQ, 2 * KVW, BF16, "swa_in_kv", rope=(tab_k, KVW))
    gate1 = _proj(h1, w_si, WQ + 2 * KVW, WQ, BF16, "swa_in_gate")
    y1, o1 = _swa_fwd(q1, kv1, gate1, sinks, mask_bias, HQ, HKV)
    dx2, dx2b, dgf, loss_p = _out_proj_loss(y1, w_so, x1, loss_target[0], gf, "swa_out_loss")

    dy1 = _matmul_nt([(dx2b, None, 0)], w_so, WQ, "swa_out_bwd")
    g_so, g_so_h = _matmul_tn(y1, [(dx2b, None, 0)], D, "swa_out_wgrad", also_bf16=True)
    dqg1, dkv1, dsink = _swa_bwd(q1, kv1, dy1, gate1, o1, sinks, tab_k, mask_bias, HQ, HKV)
    dkv1f = _swa_dkv_finish(dkv1, tab_k)
    parts1 = [(dqg1, 0, 0), (dkv1f, None, WQ), (dqg1, 1, WQ + 2 * KVW)]
    g_si, g_si_h = _matmul_tn(h1, parts1, np_s, "swa_in_wgrad", tile_major=True, also_bf16=True)
    dh1 = _matmul_nt(parts1, w_si, D, "swa_in_bwd")
    dx1, dx1b, dg1 = _rmsnorm_bwd(dh1, x1, g1, dx2, "rmsnorm1_bwd")

    qaug0, doaug0, dqkvg0 = _fox_out_bwd(dx1b, w_fo, qkv0, o0, a0, H)
    g_fo, g_fo_h = _matmul_tn(y0, [(dx1b, None, 0)], D, "fox_out_wgrad", also_bf16=True)
    early_specs = [("col", ws), ("row", rows_o), ("row", rows_o)]
    dqkvg0, dcr, dcc, early_recv = _fox_bwd(qaug0, doaug0, qkv0, c, dqkvg0, H, scatter=[g_si_h, g_fo_h, g_so_h],
                                           scatter_specs=early_specs)
    dfl, dbf = _fox_gate_bwd(fl, bias, dcr - dcc)
    parts0 = [(dqkvg0, "stack", 0), (dfl, None, 4 * W)]
    spec_fi = ("col", wf)
    fi_halves, token = [], None
    for half in range(2):
        g_fi, g_fi_h = _matmul_tn(h0, parts0, np_f, f"fox_in_wgrad_rows{half}", tile_major=True, also_bf16=True,
                                  rows=(half * (D // 2), D // 2), after=() if token is None else (token,))
        fi_sems, fi_src, fi_land, token = _scatter_start(g_fi_h, spec_fi)
        fi_halves.append((g_fi, fi_sems, fi_src, fi_land))
    parts0[-1] = (dfl + token[0, 0].astype(BF16), None, 4 * W)
    dh0 = _matmul_nt(parts0, w_fi, D, "fox_in_bwd")
    dx0, _, dg0 = _rmsnorm_bwd(dh0, x0, g0, dx1, "rmsnorm0_bwd")

    red_si, gw_fo, gw_so = [_final_sum8(g_, r_, s_)
                            for g_, r_, s_ in zip([g_si, g_fo, g_so], early_recv, early_specs)]
    gt_si = lax.dynamic_slice(red_si, ((ws * me) % LANE, 0), (ws, D))

    def t_in(p):
        return jnp.swapaxes(p[0], 0, 1)

    def t_out(t):
        return jnp.swapaxes(t, 0, 1)[None]

    P = D
    dsink_v = dsink[:, :, 0].reshape(1, HQ)
    row3 = _pad_lanes(dbf[:, :H], 0, P) + _pad_lanes(dsink_v, LANE, P) + _pad_lanes(loss_p[:, :1], 2 * LANE, P)
    pack = _pad_rows(dg0, 0) + _pad_rows(dg1, 1) + _pad_rows(dgf, 2) + _pad_rows(row3, 3)

    d_fo, m_fo, v_fo = _adamw(fox_w_out[0], gw_fo, m_fox_w_out[0], v_fox_w_out[0], "adamw_fox_out")
    d_si, m_si, v_si = _adamw(t_in(swa_w_in), gt_si, t_in(m_swa_w_in), t_in(v_swa_w_in), "adamw_swa_in")
    d_so, m_so, v_so = _adamw(swa_w_out[0], gw_so, m_swa_w_out[0], v_swa_w_out[0], "adamw_swa_out")
    behind, red_fi = [dx0, pack, d_fo, d_si, d_so], []
    for g_fi, fi_sems, fi_src, fi_land in fi_halves:
        recv_fi = _scatter_wait(fi_sems, fi_src, fi_land, spec_fi, after=behind)
        red_fi.append(_final_sum8(g_fi, recv_fi, spec_fi))
        behind = [recv_fi]
    red_fi = jnp.concatenate(red_fi, axis=1)
    gt_fi = lax.dynamic_slice(red_fi, ((wf * me) % LANE, 0), (wf, D))

    d_fi, m_fi, v_fi = _adamw(t_in(fox_w_in), gt_fi, t_in(m_fox_w_in), t_in(v_fox_w_in), "adamw_fox_in")
    gw_si, d_si, m_si, v_si = [t_out(t)[0] for t in (gt_si, d_si, m_si, v_si)]
    gw_fi, d_fi, m_fi, v_fi = [t_out(t)[0] for t in (gt_fi, d_fi, m_fi, v_fi)]

    tot = _all_reduce_small(pack, after=recv_fi)
    loss = tot[3, 2 * LANE]
    g_norm = tot[0:2]
    g_final = tot[2]
    g_bf = tot[3:4, 0:H]
    g_sinks = tot[3:4, LANE:LANE + HQ]

    def small_pack(ng, fg, bf, sk):
        r3 = _pad_lanes(bf, 0, P) + _pad_lanes(sk, LANE, P)
        return _pad_rows(ng, 0) + _pad_rows(fg[None, :], 2) + _pad_rows(r3, 3)

    sd, sm, sv = _adamw(small_pack(norm_g, final_g, fox_b_f, swa_sinks), tot,
                        small_pack(m_norm_g, m_final_g, m_fox_b_f, m_swa_sinks),
                        small_pack(v_norm_g, v_final_g, v_fox_b_f, v_swa_sinks), "adamw_small")

    def unpack(t):
        return t[0:2], t[3:4, 0:H], t[3:4, LANE:LANE + HQ], t[2]

    def group(small, fi, fo, si, so):
        ng, bf, sk, fg = unpack(small)
        return (ng, fi[None], bf, fo[None], si[None], sk, so[None], fg)

    grads = (g_norm, gw_fi[None], g_bf, gw_fo[None], gw_si[None], g_sinks, gw_so[None], g_final)
    return (loss, dx0[None], *grads, *group(sd, d_fi, d_fo, d_si, d_so),
            *group(sm, m_fi, m_fo, m_si, m_so), *group(sv, v_fi, v_fo, v_si, v_so))
```

```python
import math

import jax
import jax.numpy as jnp
from jax import lax
from jax.experimental import pallas as pl
from jax.experimental.pallas import tpu as pltpu

F32 = jnp.float32
BF16 = jnp.bfloat16
MESH = pl.DeviceIdType.MESH

N_DEV = 8
LANE = 128
HEAD_DIM = 64
SWA_BLOCK = 128
NEG_INF = -1e30
RMS_EPS = 1e-6
ROPE_THETA = 500000.0
ROT_DIM = HEAD_DIM // 4
ADAM_LR, ADAM_B1, ADAM_B2, ADAM_EPS, ADAM_WD, ADAM_STEP = 0.001, 0.9, 0.999, 1e-08, 0.01, 10
VMEM_LIMIT = 56 * 1024 * 1024
MM_TILE = 1024
ATT_TILE = 512
EPI_TILE = 512
ROW_TILE = 256
ADAM_TILE_ELEMS = 3 << 18


def _cparams(**kw):
    return pltpu.CompilerParams(vmem_limit_bytes=VMEM_LIMIT, **kw)


def _tile(n, cap):
    if n <= cap:
        return n
    t = (cap // LANE) * LANE
    while t > LANE and n % t:
        t -= LANE
    assert n % t == 0, (n, cap)
    return t


def _row_tile(n, cap):
    t = min(n, cap)
    while n % t:
        t //= 2
    return t


def _dot_nn(a, b):
    return jnp.dot(a, b, preferred_element_type=F32)


def _dot_nt(a, b):
    return lax.dot_general(a, b, (((1,), (1,)), ((), ())), preferred_element_type=F32)


def _dot_tn(a, b):
    return lax.dot_general(a, b, (((0,), (0,)), ((), ())), preferred_element_type=F32)


def _split3(x):
    hi = x.astype(BF16)
    r1 = x - hi.astype(F32)
    mid = r1.astype(BF16)
    return hi, mid, (r1 - mid.astype(F32)).astype(BF16)


def _sigmoid(g):
    return 1.0 / (1.0 + jnp.exp(-g))


def _slab_geom(w):
    starts = [w * i for i in range(N_DEV)]
    aligned = [LANE * (s // LANE) for s in starts]
    offs = [s - a for s, a in zip(starts, aligned)]
    sw = LANE * (-(-(max(offs) + w) // LANE))
    return aligned, sw, aligned[-1] + sw


def _my_index():
    return 4 * lax.axis_index("x") + 2 * lax.axis_index("y") + lax.axis_index("c")


def _all_gather(arrs, x, g):
    n = len(arrs)
    S, D = x.shape
    tr = _row_tile(S, ROW_TILE)

    def norm_rows(x_ref, g_ref, h_ref):
        xv = x_ref[...]
        r = lax.rsqrt(jnp.mean(xv * xv, axis=-1, keepdims=True) + RMS_EPS)
        h_ref[...] = ((xv * r) * g_ref[...]).astype(BF16)

    def body(*refs):
        ins, (x_ref, g_ref), outs, h_ref = refs[:n], refs[n:n + 2], refs[n + 2:2 * n + 2], refs[2 * n + 2]
        send_sems, recv_sems, local_sems = refs[2 * n + 3:]
        x, y, c = lax.axis_index("x"), lax.axis_index("y"), lax.axis_index("c")
        me, sib = (x, y, c), (x, y, 1 - c)
        chips = [(1 - x, y), (x, 1 - y), (1 - x, 1 - y)]

        def idx(px, py, pc):
            return 4 * px + 2 * py + pc

        def copy(a, k, block, to, src=None):
            dst = outs[a].at[idx(*block)]
            return pltpu.make_async_remote_copy(
                src_ref=dst if src is None else src, dst_ref=dst,
                send_sem=send_sems.at[a, k], recv_sem=recv_sems.at[a, k],
                device_id=to, device_id_type=MESH)

        mine = [pltpu.make_async_copy(ins[a], outs[a].at[idx(*me)], local_sems.at[a]) for a in range(n)]
        for m in mine:
            m.start()
        first = []
        for a in range(n):
            first.append(copy(a, 0, me, sib, src=ins[a]))
            for j, chip in enumerate(chips):
                first.append(copy(a, 1 + j, me, (*chip, c), src=ins[a]))
        for cp in first:
            cp.start()
        pltpu.emit_pipeline(
            norm_rows, grid=(S // tr,),
            in_specs=[pl.BlockSpec((tr, D), lambda i: (i, 0)), pl.BlockSpec((1, D), lambda i: (0, 0))],
            out_specs=[pl.BlockSpec((tr, D), lambda i: (i, 0))],
        )(x_ref, g_ref, h_ref)
        passed = []
        for j, chip in enumerate(chips):
            for a in range(n):
                copy(a, 1 + j, (*chip, c), me).wait_recv()
                p = copy(a, 4 + j, (*chip, c), sib)
                p.start()
                passed.append(p)
        for a in range(n):
            copy(a, 0, sib, me).wait_recv()
        for j, chip in enumerate(chips):
            for a in range(n):
                copy(a, 4 + j, (*chip, 1 - c), me).wait_recv()
        for cp in first + passed:
            cp.wait_send()
        for m in mine:
            m.wait()

    any_spec = pl.BlockSpec(memory_space=pl.ANY)
    return pl.pallas_call(
        body, name="weights_all_gather",
        out_shape=[jax.ShapeDtypeStruct((N_DEV,) + a.shape, a.dtype) for a in arrs]
        + [jax.ShapeDtypeStruct((S, D), BF16)],
        in_specs=[any_spec] * (n + 2), out_specs=[any_spec] * (n + 1),
        scratch_shapes=[pltpu.SemaphoreType.DMA((n, 7)), pltpu.SemaphoreType.DMA((n, 7)),
                        pltpu.SemaphoreType.DMA((n,))],
        compiler_params=_cparams(),
    )(*arrs, x, g)


def _rs_windows(specs):
    def window(ref, spec, blk):
        kind, n = spec
        if kind == "col":
            _, sw, _ = _slab_geom(n)
            return ref.at[pl.ds((n * blk) // LANE, sw // LANE)]
        start = pl.multiple_of(n * blk, n)
        return ref.at[pl.ds(start, n), :]
    return window


def _peer(k):
    x, y, c = lax.axis_index("x"), lax.axis_index("y"), lax.axis_index("c")
    return (x ^ (k >> 2), y ^ ((k >> 1) & 1), c ^ (k & 1))


def _direct_gather_copies(ins, outs, send_sems, recv_sems, local_sems):
    me = _my_index()
    remote, local = [], []
    for a, (src, dst) in enumerate(zip(ins, outs)):
        local.append(pltpu.make_async_copy(src, dst.at[me], local_sems.at[a]))
        for k in range(1, N_DEV):
            remote.append(pltpu.make_async_remote_copy(
                src_ref=src, dst_ref=dst.at[me], send_sem=send_sems.at[a, k - 1], recv_sem=recv_sems.at[a, k - 1],
                device_id=_peer(k), device_id_type=MESH))
    return remote, local


def _direct_scatter_copies(ins, outs, specs, send_sems, recv_sems):
    window = _rs_windows(specs)
    remote = []
    for a, (src, dst) in enumerate(zip(ins, outs)):
        for k in range(1, N_DEV):
            px, py, pc = _peer(k)
            remote.append(pltpu.make_async_remote_copy(
                src_ref=window(src, specs[a], 4 * px + 2 * py + pc), dst_ref=dst.at[k - 1],
                send_sem=send_sems.at[a, k - 1], recv_sem=recv_sems.at[a, k - 1],
                device_id=(px, py, pc), device_id_type=MESH))
    return remote


def _scatter_block_shape(g, spec):
    kind, w = spec
    return (_slab_geom(w)[1] // LANE, g.shape[1], LANE) if kind == "col" else (w, g.shape[1])


def _wait_all(remote, local=()):
    for cp in remote:
        cp.wait_recv()
    for cp in remote:
        cp.wait_send()
    for cp in local:
        cp.wait()


def _scatter_start(g, spec):
    blk = _scatter_block_shape(g, spec)
    window = _rs_windows([spec])
    npeer = N_DEV - 1

    def body(g_ref, land_ref, *rest):
        sems = rest[:2 * npeer]
        token = rest[2 * npeer + 2]
        for cp in _peer_block_copies(g_ref, land_ref, spec, window, sems[:npeer], sems[npeer:]):
            cp.start()
        token[...] = jnp.zeros(token.shape, token.dtype)

    hbm = pl.BlockSpec(memory_space=pltpu.HBM)
    sem = pl.BlockSpec(memory_space=pltpu.SEMAPHORE)
    land = lax.empty((npeer,) + blk, g.dtype)
    outs = pl.pallas_call(
        body, name="grads_scatter_start",
        out_shape=(pltpu.SemaphoreType.DMA(()),) * (2 * npeer)
        + (pltpu.HBM(g.shape, g.dtype), pltpu.HBM(land.shape, land.dtype), jax.ShapeDtypeStruct((8, LANE), F32)),
        in_specs=(hbm, hbm),
        out_specs=(sem,) * (2 * npeer) + (hbm, hbm, pl.BlockSpec(memory_space=pltpu.VMEM)),
        input_output_aliases={0: 2 * npeer, 1: 2 * npeer + 1},
        compiler_params=pltpu.CompilerParams(has_side_effects=pltpu.SideEffectType.DATAFLOW_SIDE_EFFECTING),
    )(pltpu.with_memory_space_constraint(g, pltpu.HBM), pltpu.with_memory_space_constraint(land, pltpu.HBM))
    return outs[:2 * npeer], outs[2 * npeer], outs[2 * npeer + 1], outs[2 * npeer + 2]


def _peer_block_copies(g_ref, land_ref, spec, window, send_sems, recv_sems):
    copies = []
    for k in range(1, N_DEV):
        px, py, pc = _peer(k)
        copies.append(pltpu.make_async_remote_copy(
            src_ref=window(g_ref, spec, 4 * px + 2 * py + pc), dst_ref=land_ref.at[k - 1],
            send_sem=send_sems[k - 1], recv_sem=recv_sems[k - 1], device_id=(px, py, pc), device_id_type=MESH))
    return copies


def _scatter_wait(sems, g_thru, land_thru, spec, after):
    window = _rs_windows([spec])
    npeer = N_DEV - 1

    def body(g_ref, land_ref, *rest):
        s = rest[:2 * npeer]
        copies = _peer_block_copies(g_ref, land_ref, spec, window, s[:npeer], s[npeer:])
        for cp in copies:
            cp.wait_send()
        for cp in copies:
            cp.wait_recv()

    hbm = pl.BlockSpec(memory_space=pltpu.HBM)
    sem = pl.BlockSpec(memory_space=pltpu.SEMAPHORE)
    return pl.pallas_call(
        body, name="grads_scatter_wait",
        out_shape=(pltpu.HBM(g_thru.shape, g_thru.dtype), pltpu.HBM(land_thru.shape, land_thru.dtype)),
        in_specs=(hbm, hbm) + (sem,) * (2 * npeer) + (pl.BlockSpec(memory_space=pl.ANY),) * len(after),
        out_specs=(hbm, hbm), input_output_aliases={0: 0, 1: 1},
        compiler_params=pltpu.CompilerParams(has_side_effects=pltpu.SideEffectType.DATAFLOW_SIDE_EFFECTING),
    )(g_thru, land_thru, *sems, *after)[1]


def _final_sum8(g, recv, spec):
    kind, n = spec
    me = _my_index()
    offs = jnp.stack([(n * me) // LANE if kind == "col" else me]).astype(jnp.int32)
    if kind == "col":
        _, T, M, _ = recv.shape
        grid = (T,)
        in_specs = [pl.BlockSpec((1, M, LANE), lambda t, o: (o[0] + t, 0, 0)),
                    pl.BlockSpec((N_DEV - 1, 1, M, LANE), lambda t, o: (0, t, 0, 0))]
        out_spec = pl.BlockSpec((LANE, M), lambda t, o: (t, 0))
        out_shape = jax.ShapeDtypeStruct((T * LANE, M), F32)
    else:
        _, nrow, C = recv.shape
        grid = (1,)
        in_specs = [pl.BlockSpec((nrow, C), lambda t, o: (o[0], 0)),
                    pl.BlockSpec((N_DEV - 1, nrow, C), lambda t, o: (0, 0, 0))]
        out_spec = pl.BlockSpec((nrow, C), lambda t, o: (0, 0))
        out_shape = jax.ShapeDtypeStruct((nrow, C), F32)

    def body(o_ref, g_ref, r_ref, out_ref):
        acc = g_ref[0] if kind == "col" else g_ref[...]
        for k in range(N_DEV - 1):
            acc = acc + (r_ref[k, 0] if kind == "col" else r_ref[k]).astype(F32)
        out_ref[...] = acc.T if kind == "col" else acc

    return pl.pallas_call(
        body, name="grads_final_sum8",
        grid_spec=pltpu.PrefetchScalarGridSpec(num_scalar_prefetch=1, grid=grid, in_specs=in_specs,
                                               out_specs=out_spec),
        out_shape=out_shape, compiler_params=_cparams(),
    )(offs, g, recv)


def _all_reduce_small(pack, after):
    R, P = pack.shape

    def body(x_ref, after_ref, o_ref, gat_ref, send_sems, recv_sems):
        x, y, c = lax.axis_index("x"), lax.axis_index("y"), lax.axis_index("c")
        me = 4 * x + 2 * y + c
        gat_ref[me] = x_ref[...]
        copies = []
        for k in range(1, N_DEV):
            peer = (x ^ (k >> 2), y ^ ((k >> 1) & 1), c ^ (k & 1))
            copies.append(pltpu.make_async_remote_copy(
                src_ref=x_ref, dst_ref=gat_ref.at[me],
                send_sem=send_sems.at[k - 1], recv_sem=recv_sems.at[k - 1],
                device_id=peer, device_id_type=MESH))
        for cp in copies:
            cp.start()
        for cp in copies:
            cp.wait_recv()
        for cp in copies:
            cp.wait_send()
        acc = gat_ref[0]
        for d in range(1, N_DEV):
            acc = acc + gat_ref[d]
        o_ref[...] = acc

    vm = pl.BlockSpec(memory_space=pltpu.VMEM)
    return pl.pallas_call(
        body, name="small_all_reduce",
        out_shape=jax.ShapeDtypeStruct((R, P), F32),
        in_specs=[vm, pl.BlockSpec(memory_space=pl.ANY)], out_specs=vm,
        scratch_shapes=[pltpu.VMEM((N_DEV, R, P), F32),
                        pltpu.SemaphoreType.DMA((N_DEV - 1,)), pltpu.SemaphoreType.DMA((N_DEV - 1,))],
    )(pack, after)


def _assemble(slabs, w):
    aligned, sw, total = _slab_geom(w)
    K = slabs.shape[1]
    tr = _row_tile(K, ROW_TILE)

    def body(s_ref, o_ref):
        o_ref[...] = jnp.zeros(o_ref.shape, BF16)
        for i in range(N_DEV):
            a, off = aligned[i], w * i - aligned[i]
            x = s_ref[i]
            if off:
                x = pltpu.roll(x, off, axis=1)
            o_ref[:, a:a + sw] = o_ref[:, a:a + sw] + x

    return pl.pallas_call(
        body, name="assemble_w_in", grid=(K // tr,),
        in_specs=[pl.BlockSpec((N_DEV, tr, sw), lambda i: (0, i, 0))],
        out_specs=pl.BlockSpec((tr, total), lambda i: (i, 0)),
        out_shape=jax.ShapeDtypeStruct((K, total), BF16),
        compiler_params=_cparams(),
    )(slabs)


def _rmsnorm_fwd(x, g, name):
    S, D = x.shape
    tm = _row_tile(S, ROW_TILE)

    def body(x_ref, g_ref, h_ref):
        xv = x_ref[...]
        r = lax.rsqrt(jnp.mean(xv * xv, axis=-1, keepdims=True) + RMS_EPS)
        h_ref[...] = ((xv * r) * g_ref[...]).astype(BF16)

    return pl.pallas_call(
        body, name=name, grid=(S // tm,),
        in_specs=[pl.BlockSpec((tm, D), lambda i: (i, 0)), pl.BlockSpec((1, D), lambda i: (0, 0))],
        out_specs=pl.BlockSpec((tm, D), lambda i: (i, 0)),
        out_shape=jax.ShapeDtypeStruct((S, D), BF16),
        compiler_params=_cparams(),
    )(x, g)


def _rmsnorm_bwd(dh, x, g, dres, name):
    S, D = x.shape
    tm = _row_tile(S, ROW_TILE)

    def body(dh_ref, x_ref, g_ref, dr_ref, dx_ref, dxb_ref, dg_ref):
        xv = x_ref[...]
        r = lax.rsqrt(jnp.mean(xv * xv, axis=-1, keepdims=True) + RMS_EPS)
        xhat = xv * r
        d = dh_ref[...]
        gd = d * g_ref[...]
        dx = r * (gd - xhat * jnp.mean(gd * xhat, axis=-1, keepdims=True)) + dr_ref[...]
        dx_ref[...] = dx
        dxb_ref[...] = dx.astype(BF16)

        @pl.when(pl.program_id(0) == 0)
        def _():
            dg_ref[...] = jnp.zeros(dg_ref.shape, F32)
        dg_ref[...] += jnp.sum(d * xhat, axis=0, keepdims=True)

    row = pl.BlockSpec((tm, D), lambda i: (i, 0))
    vec = pl.BlockSpec((1, D), lambda i: (0, 0))
    return pl.pallas_call(
        body, name=name, grid=(S // tm,),
        in_specs=[row, row, vec, row], out_specs=[row, row, vec],
        out_shape=[jax.ShapeDtypeStruct((S, D), F32), jax.ShapeDtypeStruct((S, D), BF16),
                   jax.ShapeDtypeStruct((1, D), F32)],
        compiler_params=_cparams(),
    )(dh, x, g, dres)


def _adamw(w, g, m, v, name):
    R, C = w.shape
    steps = pl.cdiv(R * C, ADAM_TILE_ELEMS)
    tr = R if steps == 1 else pl.cdiv(pl.cdiv(R, steps), 8) * 8
    c1 = 1.0 - ADAM_B1 ** ADAM_STEP
    c2 = 1.0 - ADAM_B2 ** ADAM_STEP

    def body(w_ref, g_ref, m_ref, v_ref, d_ref, nm_ref, nv_ref):
        gv = g_ref[...]
        nm = ADAM_B1 * m_ref[...] + (1.0 - ADAM_B1) * gv
        nv = ADAM_B2 * v_ref[...] + (1.0 - ADAM_B2) * (gv * gv)
        d_ref[...] = -ADAM_LR * ((nm / c1) / (jnp.sqrt(nv / c2) + ADAM_EPS) + ADAM_WD * w_ref[...])
        nm_ref[...] = nm
        nv_ref[...] = nv

    spec = pl.BlockSpec((tr, C), lambda i: (i, 0))
    return pl.pallas_call(
        body, name=name, grid=(pl.cdiv(R, tr),),
        in_specs=[spec] * 4, out_specs=[spec] * 3,
        out_shape=[jax.ShapeDtypeStruct((R, C), F32)] * 3,
        compiler_params=_cparams(),
    )(w, g, m, v)


def _proj(h, wfull, col0, ncols, out_dtype, name, rope=None):
    S, K = h.shape
    tm = _row_tile(S, MM_TILE)
    tn = math.gcd(_tile(ncols, MM_TILE), col0) if col0 else _tile(ncols, MM_TILE)
    if rope is not None:
        tn = _tile(math.gcd(ncols, rope[1]), MM_TILE)
    assert ncols % tn == 0 and col0 % tn == 0
    cb = col0 // tn

    def body(*refs):
        if rope is None:
            a_ref, b_ref, o_ref = refs
        else:
            a_ref, b_ref, t_ref, o_ref = refs
        acc = _dot_nn(a_ref[...], b_ref[...])
        if rope is not None:
            t0, t1, t2 = (jnp.tile(t_ref[i], (1, tn // LANE)) for i in range(3))
            roped = (acc * t0 + pltpu.roll(acc, tn - ROT_DIM // 2, axis=1) * t1
                     + pltpu.roll(acc, ROT_DIM // 2, axis=1) * t2)
            acc = jnp.where(pl.program_id(1) < rope[1] // tn, roped, acc)
        o_ref[...] = acc.astype(out_dtype)

    in_specs = [pl.BlockSpec((tm, K), lambda i, j: (i, 0)), pl.BlockSpec((K, tn), lambda i, j: (0, cb + j))]
    args = [h, wfull]
    if rope is not None:
        in_specs.append(pl.BlockSpec((3, tm, LANE), lambda i, j: (0, i, 0)))
        args.append(rope[0])
    return pl.pallas_call(
        body, name=name, grid=(S // tm, ncols // tn),
        in_specs=in_specs, out_specs=pl.BlockSpec((tm, tn), lambda i, j: (i, j)),
        out_shape=jax.ShapeDtypeStruct((S, ncols), out_dtype),
        compiler_params=_cparams(),
    )(*args)


def _out_proj_norm(y, wo, xres, g, name):
    S, W = y.shape
    D = wo.shape[1]
    tm = _row_tile(S, EPI_TILE)

    def body(a_ref, b_ref, r_ref, g_ref, x_ref, h_ref):
        xv = r_ref[...] + _dot_nn(a_ref[...], b_ref[...])
        x_ref[...] = xv
        r = lax.rsqrt(jnp.mean(xv * xv, axis=-1, keepdims=True) + RMS_EPS)
        h_ref[...] = ((xv * r) * g_ref[...]).astype(BF16)

    row = pl.BlockSpec((tm, D), lambda i: (i, 0))
    return pl.pallas_call(
        body, name=name, grid=(S // tm,),
        in_specs=[pl.BlockSpec((tm, W), lambda i: (i, 0)), pl.BlockSpec((W, D), lambda i: (0, 0)), row,
                  pl.BlockSpec((1, D), lambda i: (0, 0))],
        out_specs=[row, row],
        out_shape=[jax.ShapeDtypeStruct((S, D), F32), jax.ShapeDtypeStruct((S, D), BF16)],
        compiler_params=_cparams(),
    )(y, wo, xres, g)


def _out_proj_loss(y, wo, xres, tgt, g, name):
    S, W = y.shape
    D = wo.shape[1]
    tm = _row_tile(S, EPI_TILE)

    def body(a_ref, b_ref, r_ref, t_ref, g_ref, dx_ref, dxb_ref, dg_ref, loss_ref):
        xv = r_ref[...] + _dot_nn(a_ref[...], b_ref[...])
        r = lax.rsqrt(jnp.mean(xv * xv, axis=-1, keepdims=True) + RMS_EPS)
        xhat = xv * r
        gv = g_ref[...]
        err = xhat * gv - t_ref[...]
        d = err * (1.0 / D)
        gd = d * gv
        dx = r * (gd - xhat * jnp.mean(gd * xhat, axis=-1, keepdims=True))
        dx_ref[...] = dx
        dxb_ref[...] = dx.astype(BF16)

        @pl.when(pl.program_id(0) == 0)
        def _():
            dg_ref[...] = jnp.zeros(dg_ref.shape, F32)
            loss_ref[...] = jnp.zeros(loss_ref.shape, F32)
        dg_ref[...] += jnp.sum(d * xhat, axis=0, keepdims=True)
        per_tok = jnp.sum(err * err, axis=-1, keepdims=True) * (1.0 / D)
        loss_ref[...] += 0.5 * jnp.sum(per_tok, axis=0, keepdims=True)

    row = pl.BlockSpec((tm, D), lambda i: (i, 0))
    vec = pl.BlockSpec((1, D), lambda i: (0, 0))
    return pl.pallas_call(
        body, name=name, grid=(S // tm,),
        in_specs=[pl.BlockSpec((tm, W), lambda i: (i, 0)), pl.BlockSpec((W, D), lambda i: (0, 0)), row, row, vec],
        out_specs=[row, row, vec, pl.BlockSpec((1, LANE), lambda i: (0, 0))],
        out_shape=[jax.ShapeDtypeStruct((S, D), F32), jax.ShapeDtypeStruct((S, D), BF16),
                   jax.ShapeDtypeStruct((1, D), F32), jax.ShapeDtypeStruct((1, LANE), F32)],
        compiler_params=_cparams(),
    )(y, wo, xres, tgt, g)


def _matmul_nt(parts, wfull, out_rows, name):
    S = parts[0][0].shape[-2]
    tm, tn = _row_tile(S, 2 * MM_TILE), _tile(out_rows, MM_TILE)
    plan, lo = [], 0
    for arr, lead, col0 in parts:
        n_p = arr.shape[-1]
        tk = _tile(n_p, 2 * MM_TILE if len(parts) <= 2 else MM_TILE)
        tk = math.gcd(tk, col0) if col0 else tk
        steps = n_p // tk * (arr.shape[0] if lead == "stack" else 1)
        plan.append((lead, col0 // tk, tk, lo, lo + steps))
        lo += steps
    nk = lo
    npart = len(parts)

    def body(*refs):
        a_refs, w_refs, o_ref = refs[:npart], refs[npart:2 * npart], refs[2 * npart]
        k = pl.program_id(2)
        for p, (_, _, _, lo_p, hi_p) in enumerate(plan):
            if lo_p == 0:
                @pl.when(k == 0)
                def _(p=p):
                    o_ref[...] = _dot_nt(a_refs[p][...], w_refs[p][...])
                lo_p = 1
            if hi_p > lo_p:
                @pl.when((k >= lo_p) & (k < hi_p))
                def _(p=p):
                    o_ref[...] += _dot_nt(a_refs[p][...], w_refs[p][...])

    in_specs, args = [], []
    for (arr, lead, col0), (_, cb, tk, lo_p, hi_p) in zip(parts, plan):
        def kk(k, lo_p=lo_p, hi_p=hi_p):
            return jnp.clip(k - lo_p, 0, hi_p - lo_p - 1)
        if lead is None:
            in_specs.append(pl.BlockSpec((tm, tk), lambda i, j, k, kk=kk: (i, kk(k))))
        elif lead == "stack":
            nkb = arr.shape[-1] // tk
            in_specs.append(pl.BlockSpec((None, tm, tk), lambda i, j, k, kk=kk, nkb=nkb: (kk(k) // nkb, i, kk(k) % nkb)))
        else:
            in_specs.append(pl.BlockSpec((None, tm, tk), lambda i, j, k, kk=kk, lead=lead: (lead, i, kk(k))))
        args.append(arr)
    for (_, cb, tk, lo_p, hi_p) in plan:
        def kk(k, lo_p=lo_p, hi_p=hi_p):
            return jnp.clip(k - lo_p, 0, hi_p - lo_p - 1)
        in_specs.append(pl.BlockSpec((tn, tk), lambda i, j, k, kk=kk, cb=cb: (j, cb + kk(k))))
        args.append(wfull)
    return pl.pallas_call(
        body, name=name, grid=(S // tm, out_rows // tn, nk),
        in_specs=in_specs, out_specs=pl.BlockSpec((tm, tn), lambda i, j, k: (i, j)),
        out_shape=jax.ShapeDtypeStruct((S, out_rows), F32),
        compiler_params=_cparams(),
    )(*args)


def _matmul_tn(a, parts, total, name, tile_major=False, also_bf16=False, rows=None, after=()):
    S = a.shape[0]
    m0, M = rows or (0, a.shape[1])
    tm = _tile(M, MM_TILE)
    ib = m0 // tm
    nout = 2 if also_bf16 else 1
    outs = None
    for idx, (arr, lead, col0) in enumerate(parts):
        n_p = arr.shape[-1]
        tn = math.gcd(_tile(n_p, MM_TILE), col0) if col0 else _tile(n_p, MM_TILE)
        cb = col0 // tn
        nb = n_p // tn
        if lead == "stack":
            n_p *= arr.shape[0]

        def body(*refs, tn=tn):
            a_ref, b_ref = refs[0], refs[1]
            o_refs = refs[-nout:]
            acc = _dot_tn(a_ref[...], b_ref[...])
            for o_ref in o_refs:
                if tile_major:
                    for t in range(tn // LANE):
                        o_ref[t] = acc[:, LANE * t:LANE * (t + 1)].astype(o_ref.dtype)
                else:
                    o_ref[...] = acc.astype(o_ref.dtype)

        in_specs = [pl.BlockSpec((S, tm), lambda i, j: (0, ib + i), pipeline_mode=pl.Buffered(1))]
        if lead is None:
            in_specs.append(pl.BlockSpec((S, tn), lambda i, j: (0, j)))
        elif lead == "stack":
            in_specs.append(pl.BlockSpec((None, S, tn), lambda i, j, nb=nb: (j // nb, 0, j % nb)))
        else:
            in_specs.append(pl.BlockSpec((None, S, tn), lambda i, j, lead=lead: (lead, 0, j)))
        args = [a, arr]
        aliases = {}
        if outs is not None:
            in_specs += [pl.BlockSpec(memory_space=pl.ANY)] * nout
            args += list(outs)
            aliases = {2 + o: o for o in range(nout)}
        else:
            in_specs += [pl.BlockSpec(memory_space=pl.ANY)] * len(after)
            args += list(after)
        if tile_major:
            out_spec = pl.BlockSpec((tn // LANE, tm, LANE), lambda i, j, cb=cb: (cb + j, i, 0))
            shape = (total // LANE, M, LANE)
        else:
            out_spec = pl.BlockSpec((tm, tn), lambda i, j, cb=cb: (i, cb + j))
            shape = (M, total)
        outs = pl.pallas_call(
            body, name=f"{name}_{idx}", grid=(M // tm, n_p // tn),
            in_specs=in_specs, out_specs=[out_spec] * nout,
            out_shape=[jax.ShapeDtypeStruct(shape, dt) for dt in (F32, BF16)[:nout]],
            input_output_aliases=aliases,
            compiler_params=_cparams(),
        )(*args)
    return tuple(outs) if also_bf16 else outs[0]


def _log_sigmoid(z):
    e = jnp.exp(-jnp.abs(z))
    return jnp.minimum(z, 0.0) - jnp.where(e < 1e-4, e * (1.0 - 0.5 * e), jnp.log(1.0 + e))


def _tri_sum(tri, x):
    hi, mid, lo = _split3(x)
    return _dot_nn(tri, hi) + _dot_nn(tri, mid) + _dot_nn(tri, lo)


def _fox_gate_fwd(fl, bias):
    S = fl.shape[0]

    nb_ = _row_tile(S, ROW_TILE)

    def body(f_ref, b_ref, c_ref):
        ri = lax.broadcasted_iota(jnp.int32, (nb_, nb_), 0)
        ci = lax.broadcasted_iota(jnp.int32, (nb_, nb_), 1)
        tri = jnp.where(ri >= ci, 1.0, 0.0).astype(BF16)
        row = lax.broadcasted_iota(jnp.int32, (nb_, LANE), 0)

        def step(i, carry):
            r0 = pl.multiple_of(i * nb_, nb_)
            t = _tri_sum(tri, _log_sigmoid(f_ref[pl.ds(r0, nb_), :] + b_ref[...])) + carry
            c_ref[pl.ds(r0, nb_), :] = t
            return jnp.sum(jnp.where(row == nb_ - 1, t, 0.0), axis=0, keepdims=True)

        lax.fori_loop(0, S // nb_, step, jnp.zeros((1, LANE), F32))

    vm = pl.BlockSpec(memory_space=pltpu.VMEM)
    return pl.pallas_call(
        body, name="fox_gate_fwd", in_specs=[vm, vm], out_specs=vm,
        out_shape=jax.ShapeDtypeStruct((S, LANE), F32),
        compiler_params=_cparams(),
    )(fl, bias)


def _fox_gate_bwd(fl, bias, dc):
    S = fl.shape[0]

    nb_ = _row_tile(S, ROW_TILE)

    def body(f_ref, b_ref, d_ref, o_ref, db_ref):
        ri = lax.broadcasted_iota(jnp.int32, (nb_, nb_), 0)
        ci = lax.broadcasted_iota(jnp.int32, (nb_, nb_), 1)
        tri = jnp.where(ri <= ci, 1.0, 0.0).astype(BF16)
        row = lax.broadcasted_iota(jnp.int32, (nb_, LANE), 0)
        nt = S // nb_

        def step(ii, carry):
            carry_c, carry_b = carry
            r0 = pl.multiple_of((nt - 1 - ii) * nb_, nb_)
            t = _tri_sum(tri, d_ref[pl.ds(r0, nb_), :]) + carry_c
            dz = t * _sigmoid(-(f_ref[pl.ds(r0, nb_), :] + b_ref[...]))
            o_ref[pl.ds(r0, nb_), :] = dz.astype(BF16)
            first = jnp.sum(jnp.where(row == 0, t, 0.0), axis=0, keepdims=True)
            return first, carry_b + jnp.sum(dz, axis=0, keepdims=True)

        zero = jnp.zeros((1, LANE), F32)
        _, db = lax.fori_loop(0, nt, step, (zero, zero))
        db_ref[...] = db

    vm = pl.BlockSpec(memory_space=pltpu.VMEM)
    return pl.pallas_call(
        body, name="fox_gate_bwd", in_specs=[vm, vm, vm], out_specs=[vm, vm],
        out_shape=[jax.ShapeDtypeStruct((S, LANE), BF16), jax.ShapeDtypeStruct((1, LANE), F32)],
        compiler_params=_cparams(),
    )(fl, bias, dc)


def _bias_lanes(col, lane, e, first):
    o0 = HEAD_DIM * (1 - e)
    hi, mid, lo = _split3(col)
    d0 = o0 if first else o0 + 3
    t = jnp.where((lane >= o0) & (lane < o0 + 6), jnp.ones(lane.shape, BF16), jnp.zeros(lane.shape, BF16))
    t = jnp.where(lane == d0, hi, t)
    t = jnp.where(lane == d0 + 1, mid, t)
    return jnp.where(lane == d0 + 2, lo, t)


def _fox_fwd(qkvg, c, H, gather=()):
    na = len(gather)
    S = qkvg.shape[0]
    W = H * HEAD_DIM
    HP = H // 2
    PP = 2 if HP % 2 == 0 else 1
    NE = 2 * PP
    tq = _row_tile(S, ATT_TILE)
    nq = S // tq
    wb = W // LANE
    scale = HEAD_DIM ** -0.5

    def body(*refs):
        q_ref, k_ref, v_ref, g_ref, c_ref = refs[:5]
        y_ref, o_ref, a_ref = refs[5 + na:8 + na]
        kaug_sc, vaug_sc, qaug_sc, s_sc, mb_sc, m_sc, acc_sc = refs[8 + 2 * na:15 + 2 * na]
        hp, qi = pl.program_id(0), pl.program_id(1)
        if na:
            remote, local = _direct_gather_copies(refs[5:5 + na], refs[8 + na:8 + 2 * na], *refs[15 + 2 * na:])

            @pl.when((hp == 0) & (qi == 0))
            def _():
                for cp in remote + local:
                    cp.start()
        lane = lax.broadcasted_iota(jnp.int32, (tq, LANE), 1)
        own = [lane < HEAD_DIM, lane >= HEAD_DIM]
        rows = lax.broadcasted_iota(jnp.int32, (tq, tq), 0)
        cols = lax.broadcasted_iota(jnp.int32, (tq, tq), 1)

        def bias_lanes(col, e, first):
            return _bias_lanes(col, lane, e % 2, first)

        def head_col(tile, e):
            return jnp.sum(jnp.where(lane == 2 * PP * hp + e, tile, 0.0), axis=1, keepdims=True)

        def tile_of(e):
            return slice(LANE * (e // 2), LANE * (e // 2 + 1))

        @pl.when(qi == 0)
        def _():
            def chunk(i, carry):
                r0 = pl.multiple_of(i * tq, tq)
                cb = c_ref[pl.ds(r0, tq), :]
                for e in range(NE):
                    kb, vb = k_ref[pl.ds(r0, tq), tile_of(e)], v_ref[pl.ds(r0, tq), tile_of(e)]
                    kaug_sc[e, pl.ds(r0, tq), :] = jnp.where(own[e % 2], kb, bias_lanes(-head_col(cb, e), e, False))
                    vaug_sc[e, pl.ds(r0, tq), :] = jnp.where(own[e % 2], vb, jnp.ones((tq, LANE), BF16))
                return carry
            lax.fori_loop(0, nq, chunk, 0)

        crow = c_ref[pl.ds(pl.multiple_of(qi * tq, tq), tq), :]
        ctq = [head_col(crow, e) for e in range(NE)]
        for e in range(NE):
            q = q_ref[:, tile_of(e)] * jnp.asarray(scale, BF16)
            qaug_sc[e] = jnp.where(own[e % 2], q, bias_lanes(ctq[e], e, True))
        m_sc[...] = jnp.full(m_sc.shape, NEG_INF, F32)
        acc_sc[...] = jnp.zeros(acc_sc.shape, F32)

        def scores(blk, slot, masked):
            k0 = pl.multiple_of(blk * tq, tq)
            for e in range(NE):
                s = _dot_nt(qaug_sc[e], kaug_sc[e, pl.ds(k0, tq), :])
                if masked:
                    s = jnp.where(rows >= cols, s, NEG_INF)
                s_sc[slot, e] = s
                mb_sc[slot, e] = jnp.broadcast_to(jnp.max(s, axis=1, keepdims=True), (tq, LANE))

        def accumulate(blk, slot):
            k0 = pl.multiple_of(blk * tq, tq)
            for e in range(NE):
                m_prev = m_sc[e]
                m_new = jnp.maximum(m_prev, mb_sc[slot, e])
                p = jnp.exp(s_sc[slot, e] - jnp.tile(m_new, (1, tq // LANE)))
                acc_sc[e] = jnp.exp(m_prev - m_new) * acc_sc[e] + _dot_nn(p.astype(BF16), vaug_sc[e, pl.ds(k0, tq), :])
                m_sc[e] = m_new

        def block_of(t):
            return jnp.where(t == 0, qi, t - 1)

        scores(qi, 0, True)

        def loop_body(t, carry):
            scores(t, (t + 1) % 2, False)
            accumulate(block_of(t), t % 2)
            return carry

        lax.fori_loop(0, qi, loop_body, 0)
        accumulate(block_of(qi), qi % 2)
        o_e, a_e = [], []
        for e in range(NE):
            acc = acc_sc[e]
            l = pltpu.roll(acc, HEAD_DIM, axis=1)
            o_e.append(acc / l)
            a_e.append(ctq[e] - (m_sc[e] + jnp.log(l)))
        for pp in range(PP):
            o = jnp.where(own[0], o_e[2 * pp], o_e[2 * pp + 1])
            g = g_ref[:, tile_of(2 * pp)].astype(F32)
            y_ref[:, tile_of(2 * pp)] = (o * (g * _sigmoid(g))).astype(BF16)
            o_ref[:, tile_of(2 * pp)] = o.astype(BF16)
            a_ref[pp] = jnp.where(own[0], a_e[2 * pp], a_e[2 * pp + 1])
        if na:
            @pl.when((hp == HP // PP - 1) & (qi == nq - 1))
            def _():
                _wait_all(remote, local)

    any_spec = pl.BlockSpec(memory_space=pl.ANY)
    sems = [pltpu.SemaphoreType.DMA((na, N_DEV - 1)), pltpu.SemaphoreType.DMA((na, N_DEV - 1)),
            pltpu.SemaphoreType.DMA((na,))] if na else []
    wide = PP * LANE
    outs = pl.pallas_call(
        body, name="fox_attn_fwd", grid=(HP // PP, nq),
        in_specs=[pl.BlockSpec((tq, wide), lambda h, i: (i, h)),
                  pl.BlockSpec((S, wide), lambda h, i: (0, wb // PP + h)),
                  pl.BlockSpec((S, wide), lambda h, i: (0, 2 * wb // PP + h)),
                  pl.BlockSpec((tq, wide), lambda h, i: (i, 3 * wb // PP + h)),
                  pl.BlockSpec((S, LANE), lambda h, i: (0, 0))] + [any_spec] * na,
        out_specs=[pl.BlockSpec((tq, wide), lambda h, i: (i, h)),
                   pl.BlockSpec((tq, wide), lambda h, i: (i, h)),
                   pl.BlockSpec((PP, tq, LANE), lambda h, i: (h, i, 0))] + [any_spec] * na,
        out_shape=[jax.ShapeDtypeStruct((S, W), BF16), jax.ShapeDtypeStruct((S, W), BF16),
                   jax.ShapeDtypeStruct((HP, S, LANE), F32)]
        + [jax.ShapeDtypeStruct((N_DEV,) + g.shape, g.dtype) for g in gather],
        scratch_shapes=[pltpu.VMEM((NE, S, LANE), BF16), pltpu.VMEM((NE, S, LANE), BF16),
                        pltpu.VMEM((NE, tq, LANE), BF16), pltpu.VMEM((2, NE, tq, tq), F32),
                        pltpu.VMEM((2, NE, tq, LANE), F32), pltpu.VMEM((NE, tq, LANE), F32),
                        pltpu.VMEM((NE, tq, LANE), F32)] + sems,
        compiler_params=_cparams(),
    )(qkvg, qkvg, qkvg, qkvg, c, *gather)
    return outs[0], outs[1], outs[2], list(outs[3:])


def _fox_out_bwd(dxb, wo, qkvg, o, a, H):
    S, D = dxb.shape
    W = H * HEAD_DIM
    tm, tn = _row_tile(S, EPI_TILE), _tile(W, EPI_TILE)
    npair = tn // LANE
    scale = HEAD_DIM ** -0.5

    def body(dx_ref, w_ref, q_ref, g_ref, o_ref, a_ref, qa_ref, da_ref, dg_ref):
        dy = _dot_nt(dx_ref[...], w_ref[...])
        lane = lax.broadcasted_iota(jnp.int32, (tm, LANE), 1)
        own = [lane < HEAD_DIM, lane >= HEAD_DIM]
        for p in range(npair):
            cols = slice(LANE * p, LANE * (p + 1))
            q = q_ref[:, cols] * jnp.asarray(scale, BF16)
            dyv, g, ov, at = dy[:, cols], g_ref[:, cols].astype(F32), o_ref[:, cols].astype(F32), a_ref[p]
            sg = _sigmoid(g)
            dob = (dyv * (g * sg)).astype(BF16)
            dg_ref[:, cols] = (dyv * ov * (sg * (1.0 + g * (1.0 - sg)))).astype(BF16)
            prod = dob.astype(F32) * ov
            for e in range(2):
                a_col = jnp.max(jnp.where(own[e], at, -jnp.inf), axis=1, keepdims=True)
                d_col = jnp.sum(jnp.where(own[e], prod, 0.0), axis=1, keepdims=True)
                qa_ref[e, :, cols] = jnp.where(own[e], q, _bias_lanes(a_col, lane, e, True))
                da_ref[e, :, cols] = jnp.where(own[e], dob, _bias_lanes(-d_col, lane, e, True))

    blk = pl.BlockSpec((tm, tn), lambda i, j: (i, j))
    pair = pl.BlockSpec((2, tm, tn), lambda i, j: (0, i, j))
    return pl.pallas_call(
        body, name="fox_out_bwd", grid=(S // tm, W // tn),
        in_specs=[pl.BlockSpec((tm, D), lambda i, j: (i, 0)), pl.BlockSpec((tn, D), lambda i, j: (j, 0)),
                  blk, pl.BlockSpec((tm, tn), lambda i, j: (i, 3 * W // tn + j)), blk,
                  pl.BlockSpec((npair, tm, LANE), lambda i, j: (j, i, 0))],
        out_specs=[pair, pair, pl.BlockSpec((None, tm, tn), lambda i, j: (3, i, j))],
        out_shape=[jax.ShapeDtypeStruct((2, S, W), BF16), jax.ShapeDtypeStruct((2, S, W), BF16),
                   jax.ShapeDtypeStruct((4, S, W), BF16)],
        compiler_params=_cparams(),
    )(dxb, wo, qkvg, qkvg, o, a)


def _fox_bwd(qaug, doaug, qkv, c, dqkvg, H, scatter=(), scatter_specs=()):
    na = len(scatter)
    S = qkv.shape[0]
    W = H * HEAD_DIM
    HP = H // 2
    tq = _row_tile(S, ATT_TILE)
    nq = S // tq
    wb = W // LANE
    scale = HEAD_DIM ** -0.5

    def body(*refs):
        qa_ref, da_ref, k_ref, v_ref, c_ref = refs[:5]
        out_ref, dcr_ref, dcc_ref = refs[6 + na:9 + na]
        dq_sc, dk_sc, dv_sc = refs[9 + 2 * na:12 + 2 * na]
        hp, kj = pl.program_id(0), pl.program_id(1)
        if na:
            remote = _direct_scatter_copies(refs[5:5 + na], refs[9 + na:9 + 2 * na], scatter_specs,
                                            *refs[12 + 2 * na:])

            @pl.when((hp == 0) & (kj == 0))
            def _():
                for cp in remote:
                    cp.start()
        lane = lax.broadcasted_iota(jnp.int32, (tq, LANE), 1)
        own = [lane < HEAD_DIM, lane >= HEAD_DIM]
        rows = lax.broadcasted_iota(jnp.int32, (tq, tq), 0)
        cols = lax.broadcasted_iota(jnp.int32, (tq, tq), 1)

        @pl.when(kj == 0)
        def _():
            dq_sc[...] = jnp.zeros(dq_sc.shape, F32)

        @pl.when((kj == 0) & (hp == 0))
        def _():
            dcr_ref[...] = jnp.zeros(dcr_ref.shape, F32)
            dcc_ref[...] = jnp.zeros(dcc_ref.shape, F32)

        kblk, vblk, cblk = k_ref[...], v_ref[...], c_ref[...]
        one, zero = jnp.ones((tq, LANE), BF16), jnp.zeros((tq, LANE), BF16)
        ka, va = [], []
        for e in range(2):
            o0 = HEAD_DIM * (1 - e)
            c_col = jnp.sum(jnp.where(lane == 2 * hp + e, cblk, 0.0), axis=1, keepdims=True)
            ka.append(jnp.where(own[e], kblk, _bias_lanes(-c_col, lane, e, False)))
            va.append(jnp.where(own[e], vblk, jnp.where((lane >= o0) & (lane < o0 + 3), one, zero)))
        dk_sc[...] = jnp.zeros(dk_sc.shape, F32)
        dv_sc[...] = jnp.zeros(dv_sc.shape, F32)

        def step(i, masked):
            r0 = pl.multiple_of(i * tq, tq)
            for e in range(2):
                qa = qa_ref[e, pl.ds(r0, tq), :]
                da = da_ref[e, pl.ds(r0, tq), :]
                p = jnp.exp(_dot_nt(qa, ka[e]))
                if masked:
                    p = jnp.where(rows >= cols, p, 0.0)
                ds = p * _dot_nt(da, va[e])
                pb, dsb = p.astype(BF16), ds.astype(BF16)
                dv_sc[e] += _dot_tn(pb, da)
                dk_sc[e] += _dot_tn(dsb, qa)
                dq_sc[e, pl.ds(r0, tq), :] += _dot_nn(dsb, ka[e])

        step(kj, True)

        def loop_body(i, carry):
            step(i, False)
            return carry

        lax.fori_loop(kj + 1, nq, loop_body, 0)
        k0 = pl.multiple_of(kj * tq, tq)
        out_ref[1, pl.ds(k0, tq), :] = jnp.where(own[0], dk_sc[0], dk_sc[1]).astype(BF16)
        out_ref[2, pl.ds(k0, tq), :] = jnp.where(own[0], dv_sc[0], dv_sc[1]).astype(BF16)

        def put_lane(ref, r0, e, tile, src_lane):
            col = jnp.sum(jnp.where(lane == src_lane, tile, 0.0), axis=1, keepdims=True)
            ref[pl.ds(r0, tq), :] = jnp.where(lane == 2 * hp + e, col, ref[pl.ds(r0, tq), :])

        for e in range(2):
            put_lane(dcc_ref, k0, e, dk_sc[e], HEAD_DIM * (1 - e) + 3)

        @pl.when(kj == nq - 1)
        def _():
            def chunk(i, carry):
                r0 = pl.multiple_of(i * tq, tq)
                d0, d1 = dq_sc[0, pl.ds(r0, tq), :], dq_sc[1, pl.ds(r0, tq), :]
                out_ref[0, pl.ds(r0, tq), :] = (jnp.where(own[0], d0, d1) * scale).astype(BF16)
                put_lane(dcr_ref, r0, 0, d0, HEAD_DIM)
                put_lane(dcr_ref, r0, 1, d1, 0)
                return carry
            lax.fori_loop(0, nq, chunk, 0)

        if na:
            @pl.when((hp == HP - 1) & (kj == nq - 1))
            def _():
                _wait_all(remote)

    pair = pl.BlockSpec((2, S, LANE), lambda h, j: (0, 0, h))
    vec = pl.BlockSpec((S, LANE), lambda h, j: (0, 0))
    any_spec = pl.BlockSpec(memory_space=pl.ANY)
    sems = [pltpu.SemaphoreType.DMA((na, N_DEV - 1)), pltpu.SemaphoreType.DMA((na, N_DEV - 1))] if na else []
    outs = pl.pallas_call(
        body, name="fox_attn_bwd", grid=(HP, nq),
        in_specs=[pair, pair,
                  pl.BlockSpec((tq, LANE), lambda h, j: (j, wb + h)),
                  pl.BlockSpec((tq, LANE), lambda h, j: (j, 2 * wb + h)),
                  pl.BlockSpec((tq, LANE), lambda h, j: (j, 0))] + [any_spec] * (na + 1),
        out_specs=[pl.BlockSpec((3, S, LANE), lambda h, j: (0, 0, h)), vec, vec] + [any_spec] * na,
        out_shape=[jax.ShapeDtypeStruct(dqkvg.shape, BF16), jax.ShapeDtypeStruct((S, LANE), F32),
                   jax.ShapeDtypeStruct((S, LANE), F32)]
        + [jax.ShapeDtypeStruct((N_DEV - 1,) + _scatter_block_shape(g, s), g.dtype)
           for g, s in zip(scatter, scatter_specs)],
        scratch_shapes=[pltpu.VMEM((2, S, LANE), F32), pltpu.VMEM((2, tq, LANE), F32),
                        pltpu.VMEM((2, tq, LANE), F32)] + sems,
        input_output_aliases={5 + na: 0},
        compiler_params=_cparams(),
    )(qaug, doaug, qkv, qkv, c, *scatter, dqkvg)
    return outs[0], outs[1], outs[2], list(outs[3:])


def _swa_pick(blk, half, lane):
    b = blk.astype(F32)
    r = pltpu.roll(b, HEAD_DIM, axis=1)
    return jnp.where(jnp.logical_xor(lane < HEAD_DIM, half == 1), b, r).astype(BF16)


def _swa_stack(t, lane, G):
    pieces = []
    z = jnp.zeros((SWA_BLOCK, LANE), t.dtype)
    for j in range(G // 2):
        tile = t[:, LANE * j:LANE * (j + 1)]
        pieces += [jnp.where(lane < HEAD_DIM, tile, z), jnp.where(lane < HEAD_DIM, z, tile)]
    return jnp.concatenate(pieces, axis=0)


def _swa_unstack(st, lane, G):
    tiles = []
    for j in range(G // 2):
        a = st[2 * j * SWA_BLOCK:(2 * j + 1) * SWA_BLOCK]
        b = st[(2 * j + 1) * SWA_BLOCK:(2 * j + 2) * SWA_BLOCK]
        tiles.append(jnp.where(lane < HEAD_DIM, a, b))
    return jnp.concatenate(tiles, axis=1)


def _swa_mask_bias(G):
    R = G * SWA_BLOCK
    t_loc = jnp.arange(R)[:, None] % SWA_BLOCK
    j_loc = jnp.arange(2 * SWA_BLOCK)[None, :]
    diff = t_loc + SWA_BLOCK - j_loc
    band = (diff >= 0) & (diff < SWA_BLOCK)
    return jnp.stack([jnp.where(band & (j_loc >= SWA_BLOCK), 0.0, NEG_INF),
                      jnp.where(band, 0.0, NEG_INF)]).astype(F32)


def _swa_scores(q, kp, kc, vp, vc, srow, bias, half, head0, G):
    lane = lax.broadcasted_iota(jnp.int32, (SWA_BLOCK, LANE), 1)
    kk = jnp.concatenate([_swa_pick(kp, half, lane), _swa_pick(kc, half, lane)], axis=0)
    vv = jnp.concatenate([_swa_pick(vp, half, lane), _swa_pick(vc, half, lane)], axis=0)
    qstack = _swa_stack(q, lane, G) * jnp.asarray(HEAD_DIM ** -0.5, BF16)
    s = _dot_nt(qstack, kk) + bias
    R = G * SWA_BLOCK
    lane1 = lax.broadcasted_iota(jnp.int32, (1, LANE), 1)
    sink = jnp.concatenate(
        [jnp.broadcast_to(jnp.sum(jnp.where(lane1 == head0 + g, srow, 0.0), axis=1, keepdims=True), (SWA_BLOCK, LANE))
         for g in range(G)], axis=0)
    m = jnp.maximum(jnp.broadcast_to(jnp.max(s, axis=1, keepdims=True), (R, LANE)), sink)
    e = jnp.exp(s - jnp.tile(m, (1, 2)))
    es = jnp.exp(sink - m)
    inv = 1.0 / (jnp.broadcast_to(jnp.sum(e, axis=1, keepdims=True), (R, LANE)) + es)
    return qstack, kk, vv, e * jnp.tile(inv, (1, 2)), es * inv, lane


def _swa_fwd(q, kv, gate, sinks, mask_bias, HQ, HKV):
    S = q.shape[0]
    G = HQ // HKV
    WQ, KVW = HQ * HEAD_DIM, HKV * HEAD_DIM
    nb = S // SWA_BLOCK
    GW = G * HEAD_DIM
    kb, vb = 0, KVW // LANE
    NH = min(HKV, 4)
    NP = NH // 2

    def body(q_ref, kp_ref, kc_ref, vp_ref, vc_ref, g_ref, sink_ref, b_ref, y_ref, o_ref):
        grp = pl.program_id(0)
        for hh in range(NH):
            cols, kt = slice(GW * hh, GW * (hh + 1)), slice(LANE * (hh // 2), LANE * (hh // 2 + 1))
            _, _, vv, p, _, lane = _swa_scores(q_ref[:, cols], kp_ref[:, kt], kc_ref[:, kt], vp_ref[:, kt], vc_ref[:, kt],
                                               sink_ref[...], b_ref[0], hh % 2, (NH * grp + hh) * G, G)
            o = _swa_unstack(_dot_nn(p.astype(BF16), vv), lane, G)
            g = g_ref[:, cols].astype(F32)
            y_ref[:, cols] = (o * (g * _sigmoid(g))).astype(BF16)
            o_ref[:, cols] = o.astype(BF16)

    blk = lambda cb, prev: pl.BlockSpec(
        (SWA_BLOCK, NP * LANE), lambda h, n, cb=cb, prev=prev: (jnp.maximum(n - prev, 0), cb // NP + h))
    qspec = pl.BlockSpec((SWA_BLOCK, NH * GW), lambda h, n: (n, h))
    return pl.pallas_call(
        body, name="swa_attn_fwd", grid=(HKV // NH, nb),
        in_specs=[qspec, blk(kb, 1), blk(kb, 0), blk(vb, 1), blk(vb, 0), qspec,
                  pl.BlockSpec((1, LANE), lambda h, n: (0, 0)),
                  pl.BlockSpec((1, G * SWA_BLOCK, 2 * SWA_BLOCK), lambda h, n: (jnp.minimum(n, 1), 0, 0))],
        out_specs=[qspec, qspec],
        out_shape=[jax.ShapeDtypeStruct((S, WQ), BF16), jax.ShapeDtypeStruct((S, WQ), BF16)],
        compiler_params=_cparams(),
    )(q, kv, kv, kv, kv, gate, sinks, mask_bias)


def _swa_bwd(q, kv, dy, gate, o, sinks, tables, mask_bias, HQ, HKV):
    S = q.shape[0]
    G = HQ // HKV
    WQ, KVW = HQ * HEAD_DIM, HKV * HEAD_DIM
    nb = S // SWA_BLOCK
    GW = G * HEAD_DIM
    R = G * SWA_BLOCK
    kb, vb = 0, KVW // LANE
    scale = HEAD_DIM ** -0.5
    NH = min(HKV, 4)
    NP = NH // 2
    assert G == 8

    def body(q_ref, kp_ref, kc_ref, vp_ref, vc_ref, dy_ref, g_ref, o_ref, sink_ref, t_ref, b_ref,
             dqg_ref, dkv_ref, dsink_ref, carry_sc):
        grp, n = pl.program_id(0), pl.program_id(1)

        @pl.when(n == 0)
        def _():
            carry_sc[...] = jnp.zeros(carry_sc.shape, F32)
            dsink_ref[...] = jnp.zeros(dsink_ref.shape, F32)

        @pl.when(n < nb)
        def _():
            t0, t1, t2 = (jnp.tile(t_ref[i], (1, GW // LANE)) for i in range(3))
            for hh in range(NH):
                cols, kt = slice(GW * hh, GW * (hh + 1)), slice(LANE * (hh // 2), LANE * (hh // 2 + 1))
                qstack, kk, vv, p, psink, lane = _swa_scores(
                    q_ref[:, cols], kp_ref[:, kt], kc_ref[:, kt], vp_ref[:, kt], vc_ref[:, kt], sink_ref[...], b_ref[0],
                    hh % 2, (NH * grp + hh) * G, G)
                dyv, g, ov = dy_ref[:, cols], g_ref[:, cols].astype(F32), o_ref[:, cols].astype(F32)
                sg = _sigmoid(g)
                dob = (dyv * (g * sg)).astype(BF16)
                dqg_ref[1, :, cols] = (dyv * ov * (sg * (1.0 + g * (1.0 - sg)))).astype(BF16)
                prod = dob.astype(F32) * ov
                dparts = []
                for j in range(G // 2):
                    tile = prod[:, LANE * j:LANE * (j + 1)]
                    for sel in (jnp.where(lane < HEAD_DIM, tile, 0.0), jnp.where(lane < HEAD_DIM, 0.0, tile)):
                        dparts.append(jnp.broadcast_to(jnp.sum(sel, axis=1, keepdims=True), (SWA_BLOCK, LANE)))
                delta = jnp.concatenate(dparts, axis=0)
                dostack = _swa_stack(dob, lane, G)
                ds = p * (_dot_nt(dostack, vv) - jnp.tile(delta, (1, 2)))
                dsb, pb = ds.astype(BF16), p.astype(BF16)
                dq = _swa_unstack(_dot_nn(dsb, kk), lane, G) * scale
                dq = dq * t0 + pltpu.roll(dq * t1, ROT_DIM // 2, axis=1) + pltpu.roll(dq * t2, GW - ROT_DIM // 2, axis=1)
                dqg_ref[0, :, cols] = dq.astype(BF16)
                dkk = _dot_tn(dsb, qstack)
                dvv = _dot_tn(pb, dostack)
                dkk = dkk + pltpu.roll(dkk, HEAD_DIM, axis=1)
                dvv = dvv + pltpu.roll(dvv, HEAD_DIM, axis=1)
                lane2 = lax.broadcasted_iota(jnp.int32, (2 * SWA_BLOCK, LANE), 1)
                comb = jnp.where(lane2 < HEAD_DIM, dkk, dvv)
                dkv_ref[hh] = carry_sc[hh] + comb[:SWA_BLOCK]
                carry_sc[hh] = comb[SWA_BLOCK:]
                sk = psink * delta
                rows = [-jnp.sum(sk[g_ * SWA_BLOCK:(g_ + 1) * SWA_BLOCK], axis=0, keepdims=True) for g_ in range(G)]
                dsink_ref[hh] += jnp.concatenate(rows, axis=0)

        @pl.when(n == nb)
        def _():
            dkv_ref[...] = carry_sc[...]

    cl = lambda n: jnp.minimum(n, nb - 1)
    blk = lambda cb, prev: pl.BlockSpec(
        (SWA_BLOCK, NP * LANE), lambda h, n, cb=cb, prev=prev: (jnp.maximum(cl(n) - prev, 0), cb // NP + h))
    qspec = pl.BlockSpec((SWA_BLOCK, NH * GW), lambda h, n: (cl(n), h))
    return pl.pallas_call(
        body, name="swa_attn_bwd", grid=(HKV // NH, nb + 1),
        in_specs=[qspec, blk(kb, 1), blk(kb, 0), blk(vb, 1), blk(vb, 0), qspec, qspec, qspec,
                  pl.BlockSpec((1, LANE), lambda h, n: (0, 0)),
                  pl.BlockSpec((3, SWA_BLOCK, LANE), lambda h, n: (0, cl(n), 0)),
                  pl.BlockSpec((1, R, 2 * SWA_BLOCK), lambda h, n: (jnp.minimum(n, 1), 0, 0))],
        out_specs=[pl.BlockSpec((2, SWA_BLOCK, NH * GW), lambda h, n: (0, cl(n), h)),
                   pl.BlockSpec((NH, SWA_BLOCK, LANE), lambda h, n: (h, jnp.maximum(n - 1, 0), 0)),
                   pl.BlockSpec((NH, 8, LANE), lambda h, n: (h, 0, 0))],
        out_shape=[jax.ShapeDtypeStruct((2, S, WQ), BF16), jax.ShapeDtypeStruct((HKV, S, LANE), F32),
                   jax.ShapeDtypeStruct((HKV, 8, LANE), F32)],
        scratch_shapes=[pltpu.VMEM((NH, SWA_BLOCK, LANE), F32)],
        compiler_params=_cparams(),
    )(q, kv, kv, kv, kv, dy, gate, o, sinks, tables, mask_bias)


def _swa_dkv_finish(dkv, tables):
    HKV, S, _ = dkv.shape
    KVW = HKV * HEAD_DIM
    tm = _row_tile(S, EPI_TILE)
    npair = HKV // 2

    def body(d_ref, t_ref, o_ref):
        lane = lax.broadcasted_iota(jnp.int32, (tm, LANE), 1)
        lo = lane < HEAD_DIM
        for p in range(npair):
            a, b = d_ref[2 * p], d_ref[2 * p + 1]
            tk = jnp.where(lo, a, pltpu.roll(b, HEAD_DIM, axis=1))
            tv = jnp.where(lo, pltpu.roll(a, HEAD_DIM, axis=1), b)
            tk = (tk * t_ref[0] + pltpu.roll(tk * t_ref[1], ROT_DIM // 2, axis=1)
                  + pltpu.roll(tk * t_ref[2], LANE - ROT_DIM // 2, axis=1))
            o_ref[:, LANE * p:LANE * (p + 1)] = tk.astype(BF16)
            o_ref[:, KVW + LANE * p:KVW + LANE * (p + 1)] = tv.astype(BF16)

    return pl.pallas_call(
        body, name="swa_dkv_finish", grid=(S // tm,),
        in_specs=[pl.BlockSpec((HKV, tm, LANE), lambda i: (0, i, 0)), pl.BlockSpec((3, tm, LANE), lambda i: (0, i, 0))],
        out_specs=pl.BlockSpec((tm, 2 * KVW), lambda i: (i, 0)),
        out_shape=jax.ShapeDtypeStruct((S, 2 * KVW), BF16),
        compiler_params=_cparams(),
    )(dkv, tables)


def _rope_tables(S, width):
    half = ROT_DIM // 2
    pos = jnp.arange(S, dtype=F32)
    inv_freq = ROPE_THETA ** (-jnp.arange(half, dtype=F32) / half)
    ang = pos[:, None] * inv_freq[None, :]
    cos, sin = jnp.cos(ang), jnp.sin(ang)
    one = jnp.ones((S, HEAD_DIM - ROT_DIM), F32)
    zero = jnp.zeros((S, HEAD_DIM - ROT_DIM), F32)
    zh = jnp.zeros((S, half), F32)
    t0 = jnp.concatenate([cos, cos, one], axis=1)
    t1 = jnp.concatenate([-sin, zh, zero], axis=1)
    t2 = jnp.concatenate([zh, sin, zero], axis=1)
    return jnp.stack([jnp.tile(t, (1, width // HEAD_DIM)) for t in (t0, t1, t2)])


def _pad_rows(v, row, total_rows=8):
    return jnp.pad(v, ((row, total_rows - row - v.shape[0]), (0, 0)))


def _pad_lanes(v, off, width):
    return jnp.pad(v, ((0, 0), (off, width - off - v.shape[1])))


def kernel(x, norm_g, fox_w_in, fox_b_f, fox_w_out, swa_w_in, swa_sinks, swa_w_out, final_g, loss_target, m_norm_g, m_fox_w_in, m_fox_b_f, m_fox_w_out, m_swa_w_in, m_swa_sinks, m_swa_w_out, m_final_g, v_norm_g, v_fox_w_in, v_fox_b_f, v_fox_w_out, v_swa_w_in, v_swa_sinks, v_swa_w_out, v_final_g):
    S, D = x.shape[1], x.shape[2]
    H = fox_b_f.shape[1]
    W = H * HEAD_DIM
    wf = fox_w_in.shape[2]
    ws = swa_w_in.shape[2]
    HQ = swa_sinks.shape[1]
    WQ = HQ * HEAD_DIM
    KVW = (ws * N_DEV - 2 * WQ) // 2
    HKV = KVW // HEAD_DIM
    rows_o = fox_w_out.shape[1]
    assert wf * N_DEV == 4 * W + H and rows_o * N_DEV == W and H <= LANE and HQ <= LANE
    me = _my_index()

    _, sw_f, np_f = _slab_geom(wf)
    _, sw_s, np_s = _slab_geom(ws)

    def slab(w2d, w, sw):
        return jnp.pad(w2d.astype(BF16), ((0, 0), (0, sw - w)))

    x0 = x[0]
    g0, g1, gf = norm_g[0:1], norm_g[1:2], final_g[None, :]
    fi_all, h0 = _all_gather([slab(fox_w_in[0], wf, sw_f)], x0, g0)
    w_fi = _assemble(fi_all, wf)
    later = [slab(swa_w_in[0], ws, sw_s), fox_w_out[0].astype(BF16), swa_w_out[0].astype(BF16)]

    bias = _pad_lanes(fox_b_f, 0, LANE)
    sinks = _pad_lanes(swa_sinks, 0, LANE)
    tab_k = _rope_tables(S, LANE)
    mask_bias = _swa_mask_bias(HQ // HKV)

    qkv0 = _proj(h0, w_fi, 0, 4 * W, BF16, "fox_in_qkvg")
    fl = _proj(h0, w_fi, 4 * W, LANE, F32, "fox_in_f")
    c = _fox_gate_fwd(fl, bias)
    y0, o0, a0, (si_all, fo_all, so_all) = _fox_fwd(qkv0, c, H, gather=later)
    w_si = _assemble(si_all, ws)
    w_fo = fo_all.reshape(W, D)
    w_so = so_all.reshape(WQ, D)
    x1, h1 = _out_proj_norm(y0, w_fo, x0, g1, "fox_out")

    q1 = _proj(h1, w_si, 0, WQ, BF16, "swa_in_q", rope=(tab_k, WQ))
    kv1 = _proj(h1, w_si, WQ, 2 * KVW, BF16, "swa_in_kv", rope=(tab_k, KVW))
    gate1 = _proj(h1, w_si, WQ + 2 * KVW, WQ, BF16, "swa_in_gate")
    y1, o1 = _swa_fwd(q1, kv1, gate1, sinks, mask_bias, HQ, HKV)
    dx2, dx2b, dgf, loss_p = _out_proj_loss(y1, w_so, x1, loss_target[0], gf, "swa_out_loss")

    dy1 = _matmul_nt([(dx2b, None, 0)], w_so, WQ, "swa_out_bwd")
    g_so, g_so_h = _matmul_tn(y1, [(dx2b, None, 0)], D, "swa_out_wgrad", also_bf16=True)
    dqg1, dkv1, dsink = _swa_bwd(q1, kv1, dy1, gate1, o1, sinks, tab_k, mask_bias, HQ, HKV)
    dkv1f = _swa_dkv_finish(dkv1, tab_k)
    parts1 = [(dqg1, 0, 0), (dkv1f, None, WQ), (dqg1, 1, WQ + 2 * KVW)]
    g_si, g_si_h = _matmul_tn(h1, parts1, np_s, "swa_in_wgrad", tile_major=True, also_bf16=True)
    dh1 = _matmul_nt(parts1, w_si, D, "swa_in_bwd")
    dx1, dx1b, dg1 = _rmsnorm_bwd(dh1, x1, g1, dx2, "rmsnorm1_bwd")

    qaug0, doaug0, dqkvg0 = _fox_out_bwd(dx1b, w_fo, qkv0, o0, a0, H)
    g_fo, g_fo_h = _matmul_tn(y0, [(dx1b, None, 0)], D, "fox_out_wgrad", also_bf16=True)
    early_specs = [("col", ws), ("row", rows_o), ("row", rows_o)]
    dqkvg0, dcr, dcc, early_recv = _fox_bwd(qaug0, doaug0, qkv0, c, dqkvg0, H, scatter=[g_si_h, g_fo_h, g_so_h],
                                           scatter_specs=early_specs)
    dfl, dbf = _fox_gate_bwd(fl, bias, dcr - dcc)
    parts0 = [(dqkvg0, "stack", 0), (dfl, None, 4 * W)]
    spec_fi = ("col", wf)
    fi_halves, token = [], None
    for half in range(2):
        g_fi, g_fi_h = _matmul_tn(h0, parts0, np_f, f"fox_in_wgrad_rows{half}", tile_major=True, also_bf16=True,
                                  rows=(half * (D // 2), D // 2), after=() if token is None else (token,))
        fi_sems, fi_src, fi_land, token = _scatter_start(g_fi_h, spec_fi)
        fi_halves.append((g_fi, fi_sems, fi_src, fi_land))
    parts0[-1] = (dfl + token[0, 0].astype(BF16), None, 4 * W)
    dh0 = _matmul_nt(parts0, w_fi, D, "fox_in_bwd")
    dx0, _, dg0 = _rmsnorm_bwd(dh0, x0, g0, dx1, "rmsnorm0_bwd")

    red_si, gw_fo, gw_so = [_final_sum8(g_, r_, s_)
                            for g_, r_, s_ in zip([g_si, g_fo, g_so], early_recv, early_specs)]
    gt_si = lax.dynamic_slice(red_si, ((ws * me) % LANE, 0), (ws, D))

    def t_in(p):
        return jnp.swapaxes(p[0], 0, 1)

    def t_out(t):
        return jnp.swapaxes(t, 0, 1)[None]

    P = D
    dsink_v = dsink[:, :, 0].reshape(1, HQ)
    row3 = _pad_lanes(dbf[:, :H], 0, P) + _pad_lanes(dsink_v, LANE, P) + _pad_lanes(loss_p[:, :1], 2 * LANE, P)
    pack = _pad_rows(dg0, 0) + _pad_rows(dg1, 1) + _pad_rows(dgf, 2) + _pad_rows(row3, 3)

    d_fo, m_fo, v_fo = _adamw(fox_w_out[0], gw_fo, m_fox_w_out[0], v_fox_w_out[0], "adamw_fox_out")
    d_si, m_si, v_si = _adamw(t_in(swa_w_in), gt_si, t_in(m_swa_w_in), t_in(v_swa_w_in), "adamw_swa_in")
    d_so, m_so, v_so = _adamw(swa_w_out[0], gw_so, m_swa_w_out[0], v_swa_w_out[0], "adamw_swa_out")
    behind, red_fi = [dx0, pack, d_fo, d_si, d_so], []
    for g_fi, fi_sems, fi_src, fi_land in fi_halves:
        recv_fi = _scatter_wait(fi_sems, fi_src, fi_land, spec_fi, after=behind)
        red_fi.append(_final_sum8(g_fi, recv_fi, spec_fi))
        behind = [recv_fi]
    red_fi = jnp.concatenate(red_fi, axis=1)
    gt_fi = lax.dynamic_slice(red_fi, ((wf * me) % LANE, 0), (wf, D))

    d_fi, m_fi, v_fi = _adamw(t_in(fox_w_in), gt_fi, t_in(m_fox_w_in), t_in(v_fox_w_in), "adamw_fox_in")
    gw_si, d_si, m_si, v_si = [t_out(t)[0] for t in (gt_si, d_si, m_si, v_si)]
    gw_fi, d_fi, m_fi, v_fi = [t_out(t)[0] for t in (gt_fi, d_fi, m_fi, v_fi)]

    tot = _all_reduce_small(pack, after=recv_fi)
    loss = tot[3, 2 * LANE]
    g_norm = tot[0:2]
    g_final = tot[2]
    g_bf = tot[3:4, 0:H]
    g_sinks = tot[3:4, LANE:LANE + HQ]

    def small_pack(ng, fg, bf, sk):
        r3 = _pad_lanes(bf, 0, P) + _pad_lanes(sk, LANE, P)
        return _pad_rows(ng, 0) + _pad_rows(fg[None, :], 2) + _pad_rows(r3, 3)

    sd, sm, sv = _adamw(small_pack(norm_g, final_g, fox_b_f, swa_sinks), tot,
                        small_pack(m_norm_g, m_final_g, m_fox_b_f, m_swa_sinks),
                        small_pack(v_norm_g, v_final_g, v_fox_b_f, v_swa_sinks), "adamw_small")

    def unpack(t):
        return t[0:2], t[3:4, 0:H], t[3:4, LANE:LANE + HQ], t[2]

    def group(small, fi, fo, si, so):
        ng, bf, sk, fg = unpack(small)
        return (ng, fi[None], bf, fo[None], si[None], sk, so[None], fg)

    grads = (g_norm, gw_fi[None], g_bf, gw_fo[None], gw_si[None], g_sinks, gw_so[None], g_final)
    return (loss, dx0[None], *grads, *group(sd, d_fi, d_fo, d_si, d_so),
            *group(sm, m_fi, m_fo, m_si, m_so), *group(sv, v_fi, v_fo, v_si, v_so))
```

```python
import math

import jax
import jax.numpy as jnp
from jax import lax
from jax.experimental import pallas as pl
from jax.experimental.pallas import tpu as pltpu

F32 = jnp.float32
BF16 = jnp.bfloat16
MESH = pl.DeviceIdType.MESH

N_DEV = 8
LANE = 128
HEAD_DIM = 64
SWA_BLOCK = 128
NEG_INF = -1e30
RMS_EPS = 1e-6
ROPE_THETA = 500000.0
ROT_DIM = HEAD_DIM // 4
ADAM_LR, ADAM_B1, ADAM_B2, ADAM_EPS, ADAM_WD, ADAM_STEP = 0.001, 0.9, 0.999, 1e-08, 0.01, 10
VMEM_LIMIT = 56 * 1024 * 1024
MM_TILE = 1024
ATT_TILE = 512
EPI_TILE = 512
ROW_TILE = 256
ADAM_LANES = 256
ADAM_TILE_ELEMS = 3 << 18


def _cparams(**kw):
    return pltpu.CompilerParams(vmem_limit_bytes=VMEM_LIMIT, **kw)


def _tile(n, cap):
    if n <= cap:
        return n
    t = (cap // LANE) * LANE
    while t > LANE and n % t:
        t -= LANE
    assert n % t == 0, (n, cap)
    return t


def _row_tile(n, cap):
    t = min(n, cap)
    while n % t:
        t //= 2
    return t


def _dot_nn(a, b):
    return jnp.dot(a, b, preferred_element_type=F32)


def _dot_nt(a, b):
    return lax.dot_general(a, b, (((1,), (1,)), ((), ())), preferred_element_type=F32)


def _dot_tn(a, b):
    return lax.dot_general(a, b, (((0,), (0,)), ((), ())), preferred_element_type=F32)


def _split3(x):
    hi = x.astype(BF16)
    r1 = x - hi.astype(F32)
    mid = r1.astype(BF16)
    return hi, mid, (r1 - mid.astype(F32)).astype(BF16)


def _sigmoid(g):
    return 1.0 / (1.0 + jnp.exp(-g))


def _slab_geom(w):
    starts = [w * i for i in range(N_DEV)]
    aligned = [LANE * (s // LANE) for s in starts]
    offs = [s - a for s, a in zip(starts, aligned)]
    sw = LANE * (-(-(max(offs) + w) // LANE))
    return aligned, sw, aligned[-1] + sw


def _my_index():
    return 4 * lax.axis_index("x") + 2 * lax.axis_index("y") + lax.axis_index("c")


def _all_gather(arrs, x, g):
    n = len(arrs)
    S, D = x.shape
    tr = _row_tile(S, ROW_TILE)

    def norm_rows(x_ref, g_ref, h_ref):
        xv = x_ref[...]
        r = lax.rsqrt(jnp.mean(xv * xv, axis=-1, keepdims=True) + RMS_EPS)
        h_ref[...] = ((xv * r) * g_ref[...]).astype(BF16)

    def body(*refs):
        ins, (x_ref, g_ref), outs, h_ref = refs[:n], refs[n:n + 2], refs[n + 2:2 * n + 2], refs[2 * n + 2]
        send_sems, recv_sems, local_sems = refs[2 * n + 3:]
        x, y, c = lax.axis_index("x"), lax.axis_index("y"), lax.axis_index("c")
        me, sib = (x, y, c), (x, y, 1 - c)
        chips = [(1 - x, y), (x, 1 - y), (1 - x, 1 - y)]

        def idx(px, py, pc):
            return 4 * px + 2 * py + pc

        def copy(a, k, block, to, src=None):
            dst = outs[a].at[idx(*block)]
            return pltpu.make_async_remote_copy(
                src_ref=dst if src is None else src, dst_ref=dst,
                send_sem=send_sems.at[a, k], recv_sem=recv_sems.at[a, k],
                device_id=to, device_id_type=MESH)

        mine = [pltpu.make_async_copy(ins[a], outs[a].at[idx(*me)], local_sems.at[a]) for a in range(n)]
        for m in mine:
            m.start()
        first = []
        for a in range(n):
            first.append(copy(a, 0, me, sib, src=ins[a]))
            for j, chip in enumerate(chips):
                first.append(copy(a, 1 + j, me, (*chip, c), src=ins[a]))
        for cp in first:
            cp.start()
        pltpu.emit_pipeline(
            norm_rows, grid=(S // tr,),
            in_specs=[pl.BlockSpec((tr, D), lambda i: (i, 0)), pl.BlockSpec((1, D), lambda i: (0, 0))],
            out_specs=[pl.BlockSpec((tr, D), lambda i: (i, 0))],
        )(x_ref, g_ref, h_ref)
        passed = []
        for j, chip in enumerate(chips):
            for a in range(n):
                copy(a, 1 + j, (*chip, c), me).wait_recv()
                p = copy(a, 4 + j, (*chip, c), sib)
                p.start()
                passed.append(p)
        for a in range(n):
            copy(a, 0, sib, me).wait_recv()
        for j, chip in enumerate(chips):
            for a in range(n):
                copy(a, 4 + j, (*chip, 1 - c), me).wait_recv()
        for cp in first + passed:
            cp.wait_send()
        for m in mine:
            m.wait()

    any_spec = pl.BlockSpec(memory_space=pl.ANY)
    return pl.pallas_call(
        body, name="weights_all_gather",
        out_shape=[jax.ShapeDtypeStruct((N_DEV,) + a.shape, a.dtype) for a in arrs]
        + [jax.ShapeDtypeStruct((S, D), BF16)],
        in_specs=[any_spec] * (n + 2), out_specs=[any_spec] * (n + 1),
        scratch_shapes=[pltpu.SemaphoreType.DMA((n, 7)), pltpu.SemaphoreType.DMA((n, 7)),
                        pltpu.SemaphoreType.DMA((n,))],
        compiler_params=_cparams(),
    )(*arrs, x, g)


def _rs_windows(specs):
    def window(ref, spec, blk):
        kind, n = spec
        if kind == "col":
            _, sw, _ = _slab_geom(n)
            return ref.at[pl.ds((n * blk) // LANE, sw // LANE)]
        start = pl.multiple_of(n * blk, n)
        return ref.at[pl.ds(start, n), :]
    return window


def _peer(k):
    x, y, c = lax.axis_index("x"), lax.axis_index("y"), lax.axis_index("c")
    return (x ^ (k >> 2), y ^ ((k >> 1) & 1), c ^ (k & 1))


def _direct_gather_copies(ins, outs, send_sems, recv_sems, local_sems):
    me = _my_index()
    remote, local = [], []
    for a, (src, dst) in enumerate(zip(ins, outs)):
        local.append(pltpu.make_async_copy(src, dst.at[me], local_sems.at[a]))
        for k in range(1, N_DEV):
            remote.append(pltpu.make_async_remote_copy(
                src_ref=src, dst_ref=dst.at[me], send_sem=send_sems.at[a, k - 1], recv_sem=recv_sems.at[a, k - 1],
                device_id=_peer(k), device_id_type=MESH))
    return remote, local


def _direct_scatter_copies(ins, outs, specs, send_sems, recv_sems):
    window = _rs_windows(specs)
    remote = []
    for a, (src, dst) in enumerate(zip(ins, outs)):
        for k in range(1, N_DEV):
            px, py, pc = _peer(k)
            remote.append(pltpu.make_async_remote_copy(
                src_ref=window(src, specs[a], 4 * px + 2 * py + pc), dst_ref=dst.at[k - 1],
                send_sem=send_sems.at[a, k - 1], recv_sem=recv_sems.at[a, k - 1],
                device_id=(px, py, pc), device_id_type=MESH))
    return remote


def _scatter_block_shape(g, spec):
    kind, w = spec
    return (_slab_geom(w)[1] // LANE, g.shape[1], LANE) if kind == "col" else (w, g.shape[1])


def _wait_all(remote, local=()):
    for cp in remote:
        cp.wait_recv()
    for cp in remote:
        cp.wait_send()
    for cp in local:
        cp.wait()


def _scatter_start(g, spec):
    blk = _scatter_block_shape(g, spec)
    window = _rs_windows([spec])
    npeer = N_DEV - 1

    def body(g_ref, land_ref, *rest):
        sems = rest[:2 * npeer]
        token = rest[2 * npeer + 2]
        for cp in _peer_block_copies(g_ref, land_ref, spec, window, sems[:npeer], sems[npeer:]):
            cp.start()
        token[...] = jnp.zeros(token.shape, token.dtype)

    hbm = pl.BlockSpec(memory_space=pltpu.HBM)
    sem = pl.BlockSpec(memory_space=pltpu.SEMAPHORE)
    land = lax.empty((npeer,) + blk, g.dtype)
    outs = pl.pallas_call(
        body, name="grads_scatter_start",
        out_shape=(pltpu.SemaphoreType.DMA(()),) * (2 * npeer)
        + (pltpu.HBM(g.shape, g.dtype), pltpu.HBM(land.shape, land.dtype), jax.ShapeDtypeStruct((8, LANE), F32)),
        in_specs=(hbm, hbm),
        out_specs=(sem,) * (2 * npeer) + (hbm, hbm, pl.BlockSpec(memory_space=pltpu.VMEM)),
        input_output_aliases={0: 2 * npeer, 1: 2 * npeer + 1},
        compiler_params=pltpu.CompilerParams(has_side_effects=pltpu.SideEffectType.DATAFLOW_SIDE_EFFECTING),
    )(pltpu.with_memory_space_constraint(g, pltpu.HBM), pltpu.with_memory_space_constraint(land, pltpu.HBM))
    return outs[:2 * npeer], outs[2 * npeer], outs[2 * npeer + 1], outs[2 * npeer + 2]


def _peer_block_copies(g_ref, land_ref, spec, window, send_sems, recv_sems):
    copies = []
    for k in range(1, N_DEV):
        px, py, pc = _peer(k)
        copies.append(pltpu.make_async_remote_copy(
            src_ref=window(g_ref, spec, 4 * px + 2 * py + pc), dst_ref=land_ref.at[k - 1],
            send_sem=send_sems[k - 1], recv_sem=recv_sems[k - 1], device_id=(px, py, pc), device_id_type=MESH))
    return copies


def _scatter_wait(sems, g_thru, land_thru, spec, after):
    window = _rs_windows([spec])
    npeer = N_DEV - 1

    def body(g_ref, land_ref, *rest):
        s = rest[:2 * npeer]
        copies = _peer_block_copies(g_ref, land_ref, spec, window, s[:npeer], s[npeer:])
        for cp in copies:
            cp.wait_send()
        for cp in copies:
            cp.wait_recv()

    hbm = pl.BlockSpec(memory_space=pltpu.HBM)
    sem = pl.BlockSpec(memory_space=pltpu.SEMAPHORE)
    return pl.pallas_call(
        body, name="grads_scatter_wait",
        out_shape=(pltpu.HBM(g_thru.shape, g_thru.dtype), pltpu.HBM(land_thru.shape, land_thru.dtype)),
        in_specs=(hbm, hbm) + (sem,) * (2 * npeer) + (pl.BlockSpec(memory_space=pl.ANY),) * len(after),
        out_specs=(hbm, hbm), input_output_aliases={0: 0, 1: 1},
        compiler_params=pltpu.CompilerParams(has_side_effects=pltpu.SideEffectType.DATAFLOW_SIDE_EFFECTING),
    )(g_thru, land_thru, *sems, *after)[1]


def _final_sum8(g, recv, spec):
    kind, n = spec
    me = _my_index()
    offs = jnp.stack([(n * me) // LANE if kind == "col" else me]).astype(jnp.int32)
    if kind == "col":
        _, T, M, _ = recv.shape
        grid = (T,)
        in_specs = [pl.BlockSpec((1, M, LANE), lambda t, o: (o[0] + t, 0, 0)),
                    pl.BlockSpec((N_DEV - 1, 1, M, LANE), lambda t, o: (0, t, 0, 0))]
        out_spec = pl.BlockSpec((LANE, M), lambda t, o: (t, 0))
        out_shape = jax.ShapeDtypeStruct((T * LANE, M), F32)
    else:
        _, nrow, C = recv.shape
        grid = (1,)
        in_specs = [pl.BlockSpec((nrow, C), lambda t, o: (o[0], 0)),
                    pl.BlockSpec((N_DEV - 1, nrow, C), lambda t, o: (0, 0, 0))]
        out_spec = pl.BlockSpec((nrow, C), lambda t, o: (0, 0))
        out_shape = jax.ShapeDtypeStruct((nrow, C), F32)

    def body(o_ref, g_ref, r_ref, out_ref):
        acc = g_ref[0] if kind == "col" else g_ref[...]
        for k in range(N_DEV - 1):
            acc = acc + (r_ref[k, 0] if kind == "col" else r_ref[k]).astype(F32)
        out_ref[...] = acc.T if kind == "col" else acc

    return pl.pallas_call(
        body, name="grads_final_sum8",
        grid_spec=pltpu.PrefetchScalarGridSpec(num_scalar_prefetch=1, grid=grid, in_specs=in_specs,
                                               out_specs=out_spec),
        out_shape=out_shape, compiler_params=_cparams(),
    )(offs, g, recv)


def _all_reduce_small(pack, after, adam):
    R, P = pack.shape
    Ra, Ca = adam[0].shape
    aspec = pl.BlockSpec((Ra, ADAM_LANES), lambda j: (0, j))

    def body(x_ref, after_ref, w_ref, g_ref, m_ref, v_ref, o_ref, d_ref, nm_ref, nv_ref,
             gat_ref, send_sems, recv_sems):
        x, y, c = lax.axis_index("x"), lax.axis_index("y"), lax.axis_index("c")
        me = 4 * x + 2 * y + c
        gat_ref[me] = x_ref[...]
        copies = []
        for k in range(1, N_DEV):
            peer = (x ^ (k >> 2), y ^ ((k >> 1) & 1), c ^ (k & 1))
            copies.append(pltpu.make_async_remote_copy(
                src_ref=x_ref, dst_ref=gat_ref.at[me],
                send_sem=send_sems.at[k - 1], recv_sem=recv_sems.at[k - 1],
                device_id=peer, device_id_type=MESH))
        for cp in copies:
            cp.start()
        pltpu.emit_pipeline(lambda *refs: _adamw_block(*refs), grid=(Ca // ADAM_LANES,),
                            in_specs=[aspec] * 4, out_specs=[aspec] * 3,
                            )(w_ref, g_ref, m_ref, v_ref, d_ref, nm_ref, nv_ref)
        for cp in copies:
            cp.wait_recv()
        for cp in copies:
            cp.wait_send()
        acc = gat_ref[0]
        for d in range(1, N_DEV):
            acc = acc + gat_ref[d]
        o_ref[...] = acc

    vm, hbm = pl.BlockSpec(memory_space=pltpu.VMEM), pl.BlockSpec(memory_space=pl.ANY)
    return pl.pallas_call(
        body, name="small_all_reduce",
        out_shape=[jax.ShapeDtypeStruct((R, P), F32)] + [jax.ShapeDtypeStruct((Ra, Ca), F32)] * 3,
        in_specs=[vm, hbm] + [hbm] * 4, out_specs=[vm] + [hbm] * 3,
        scratch_shapes=[pltpu.VMEM((N_DEV, R, P), F32),
                        pltpu.SemaphoreType.DMA((N_DEV - 1,)), pltpu.SemaphoreType.DMA((N_DEV - 1,))],
        compiler_params=_cparams(),
    )(pack, after, *adam)


def _assemble(slabs, w):
    aligned, sw, total = _slab_geom(w)
    K = slabs.shape[1]
    tr = _row_tile(K, ROW_TILE)

    def body(s_ref, o_ref):
        o_ref[...] = jnp.zeros(o_ref.shape, BF16)
        for i in range(N_DEV):
            a, off = aligned[i], w * i - aligned[i]
            x = s_ref[i]
            if off:
                x = pltpu.roll(x, off, axis=1)
            o_ref[:, a:a + sw] = o_ref[:, a:a + sw] + x

    return pl.pallas_call(
        body, name="assemble_w_in", grid=(K // tr,),
        in_specs=[pl.BlockSpec((N_DEV, tr, sw), lambda i: (0, i, 0))],
        out_specs=pl.BlockSpec((tr, total), lambda i: (i, 0)),
        out_shape=jax.ShapeDtypeStruct((K, total), BF16),
        compiler_params=_cparams(),
    )(slabs)


def _rmsnorm_fwd(x, g, name):
    S, D = x.shape
    tm = _row_tile(S, ROW_TILE)

    def body(x_ref, g_ref, h_ref):
        xv = x_ref[...]
        r = lax.rsqrt(jnp.mean(xv * xv, axis=-1, keepdims=True) + RMS_EPS)
        h_ref[...] = ((xv * r) * g_ref[...]).astype(BF16)

    return pl.pallas_call(
        body, name=name, grid=(S // tm,),
        in_specs=[pl.BlockSpec((tm, D), lambda i: (i, 0)), pl.BlockSpec((1, D), lambda i: (0, 0))],
        out_specs=pl.BlockSpec((tm, D), lambda i: (i, 0)),
        out_shape=jax.ShapeDtypeStruct((S, D), BF16),
        compiler_params=_cparams(),
    )(x, g)


def _rmsnorm_bwd(dh, x, g, dres, name):
    S, D = x.shape
    tm = _row_tile(S, ROW_TILE)

    def body(dh_ref, x_ref, g_ref, dr_ref, dx_ref, dxb_ref, dg_ref):
        xv = x_ref[...]
        r = lax.rsqrt(jnp.mean(xv * xv, axis=-1, keepdims=True) + RMS_EPS)
        xhat = xv * r
        d = dh_ref[...]
        gd = d * g_ref[...]
        dx = r * (gd - xhat * jnp.mean(gd * xhat, axis=-1, keepdims=True)) + dr_ref[...]
        dx_ref[...] = dx
        dxb_ref[...] = dx.astype(BF16)

        @pl.when(pl.program_id(0) == 0)
        def _():
            dg_ref[...] = jnp.zeros(dg_ref.shape, F32)
        dg_ref[...] += jnp.sum(d * xhat, axis=0, keepdims=True)

    row = pl.BlockSpec((tm, D), lambda i: (i, 0))
    vec = pl.BlockSpec((1, D), lambda i: (0, 0))
    return pl.pallas_call(
        body, name=name, grid=(S // tm,),
        in_specs=[row, row, vec, row], out_specs=[row, row, vec],
        out_shape=[jax.ShapeDtypeStruct((S, D), F32), jax.ShapeDtypeStruct((S, D), BF16),
                   jax.ShapeDtypeStruct((1, D), F32)],
        compiler_params=_cparams(),
    )(dh, x, g, dres)


def _adamw_block(w_ref, g_ref, m_ref, v_ref, d_ref, nm_ref, nv_ref):
    c1 = 1.0 - ADAM_B1 ** ADAM_STEP
    c2 = 1.0 - ADAM_B2 ** ADAM_STEP
    gv = g_ref[...]
    nm = ADAM_B1 * m_ref[...] + (1.0 - ADAM_B1) * gv
    nv = ADAM_B2 * v_ref[...] + (1.0 - ADAM_B2) * (gv * gv)
    d_ref[...] = -ADAM_LR * ((nm / c1) / (jnp.sqrt(nv / c2) + ADAM_EPS) + ADAM_WD * w_ref[...])
    nm_ref[...] = nm
    nv_ref[...] = nv


def _adamw(w, g, m, v, name):
    R, C = w.shape
    steps = pl.cdiv(R * C, ADAM_TILE_ELEMS)
    tr = R if steps == 1 else pl.cdiv(pl.cdiv(R, steps), 8) * 8
    spec = pl.BlockSpec((tr, C), lambda i: (i, 0))
    return pl.pallas_call(
        lambda *refs: _adamw_block(*refs), name=name, grid=(pl.cdiv(R, tr),),
        in_specs=[spec] * 4, out_specs=[spec] * 3,
        out_shape=[jax.ShapeDtypeStruct((R, C), F32)] * 3,
        compiler_params=_cparams(),
    )(w, g, m, v)


def _proj(h, wfull, col0, ncols, out_dtype, name, rope=None):
    S, K = h.shape
    tm = _row_tile(S, MM_TILE)
    tn = math.gcd(_tile(ncols, MM_TILE), col0) if col0 else _tile(ncols, MM_TILE)
    if rope is not None:
        tn = _tile(math.gcd(ncols, rope[1]), MM_TILE)
    assert ncols % tn == 0 and col0 % tn == 0
    cb = col0 // tn

    def body(*refs):
        if rope is None:
            a_ref, b_ref, o_ref = refs
        else:
            a_ref, b_ref, t_ref, o_ref = refs
        acc = _dot_nn(a_ref[...], b_ref[...])
        if rope is not None:
            t0, t1, t2 = (jnp.tile(t_ref[i], (1, tn // LANE)) for i in range(3))
            roped = (acc * t0 + pltpu.roll(acc, tn - ROT_DIM // 2, axis=1) * t1
                     + pltpu.roll(acc, ROT_DIM // 2, axis=1) * t2)
            acc = jnp.where(pl.program_id(1) < rope[1] // tn, roped, acc)
        o_ref[...] = acc.astype(out_dtype)

    in_specs = [pl.BlockSpec((tm, K), lambda i, j: (i, 0)), pl.BlockSpec((K, tn), lambda i, j: (0, cb + j))]
    args = [h, wfull]
    if rope is not None:
        in_specs.append(pl.BlockSpec((3, tm, LANE), lambda i, j: (0, i, 0)))
        args.append(rope[0])
    return pl.pallas_call(
        body, name=name, grid=(S // tm, ncols // tn),
        in_specs=in_specs, out_specs=pl.BlockSpec((tm, tn), lambda i, j: (i, j)),
        out_shape=jax.ShapeDtypeStruct((S, ncols), out_dtype),
        compiler_params=_cparams(),
    )(*args)


def _out_proj_norm(y, wo, xres, g, name):
    S, W = y.shape
    D = wo.shape[1]
    tm = _row_tile(S, EPI_TILE)

    def body(a_ref, b_ref, r_ref, g_ref, x_ref, h_ref):
        xv = r_ref[...] + _dot_nn(a_ref[...], b_ref[...])
        x_ref[...] = xv
        r = lax.rsqrt(jnp.mean(xv * xv, axis=-1, keepdims=True) + RMS_EPS)
        h_ref[...] = ((xv * r) * g_ref[...]).astype(BF16)

    row = pl.BlockSpec((tm, D), lambda i: (i, 0))
    return pl.pallas_call(
        body, name=name, grid=(S // tm,),
        in_specs=[pl.BlockSpec((tm, W), lambda i: (i, 0)), pl.BlockSpec((W, D), lambda i: (0, 0)), row,
                  pl.BlockSpec((1, D), lambda i: (0, 0))],
        out_specs=[row, row],
        out_shape=[jax.ShapeDtypeStruct((S, D), F32), jax.ShapeDtypeStruct((S, D), BF16)],
        compiler_params=_cparams(),
    )(y, wo, xres, g)


def _out_proj_loss(y, wo, xres, tgt, g, name):
    S, W = y.shape
    D = wo.shape[1]
    tm = _row_tile(S, EPI_TILE)

    def body(a_ref, b_ref, r_ref, t_ref, g_ref, dx_ref, dxb_ref, dg_ref, loss_ref):
        xv = r_ref[...] + _dot_nn(a_ref[...], b_ref[...])
        r = lax.rsqrt(jnp.mean(xv * xv, axis=-1, keepdims=True) + RMS_EPS)
        xhat = xv * r
        gv = g_ref[...]
        err = xhat * gv - t_ref[...]
        d = err * (1.0 / D)
        gd = d * gv
        dx = r * (gd - xhat * jnp.mean(gd * xhat, axis=-1, keepdims=True))
        dx_ref[...] = dx
        dxb_ref[...] = dx.astype(BF16)

        @pl.when(pl.program_id(0) == 0)
        def _():
            dg_ref[...] = jnp.zeros(dg_ref.shape, F32)
            loss_ref[...] = jnp.zeros(loss_ref.shape, F32)
        dg_ref[...] += jnp.sum(d * xhat, axis=0, keepdims=True)
        per_tok = jnp.sum(err * err, axis=-1, keepdims=True) * (1.0 / D)
        loss_ref[...] += 0.5 * jnp.sum(per_tok, axis=0, keepdims=True)

    row = pl.BlockSpec((tm, D), lambda i: (i, 0))
    vec = pl.BlockSpec((1, D), lambda i: (0, 0))
    return pl.pallas_call(
        body, name=name, grid=(S // tm,),
        in_specs=[pl.BlockSpec((tm, W), lambda i: (i, 0)), pl.BlockSpec((W, D), lambda i: (0, 0)), row, row, vec],
        out_specs=[row, row, vec, pl.BlockSpec((1, LANE), lambda i: (0, 0))],
        out_shape=[jax.ShapeDtypeStruct((S, D), F32), jax.ShapeDtypeStruct((S, D), BF16),
                   jax.ShapeDtypeStruct((1, D), F32), jax.ShapeDtypeStruct((1, LANE), F32)],
        compiler_params=_cparams(),
    )(y, wo, xres, tgt, g)


def _matmul_nt(parts, wfull, out_rows, name):
    S = parts[0][0].shape[-2]
    tm, tn = _row_tile(S, 2 * MM_TILE), _tile(out_rows, MM_TILE)
    plan, lo = [], 0
    for arr, lead, col0 in parts:
        n_p = arr.shape[-1]
        tk = _tile(n_p, 2 * MM_TILE if len(parts) <= 2 else MM_TILE)
        tk = math.gcd(tk, col0) if col0 else tk
        steps = n_p // tk * (arr.shape[0] if lead == "stack" else 1)
        plan.append((lead, col0 // tk, tk, lo, lo + steps))
        lo += steps
    nk = lo
    npart = len(parts)

    def body(*refs):
        a_refs, w_refs, o_ref = refs[:npart], refs[npart:2 * npart], refs[2 * npart]
        k = pl.program_id(2)
        for p, (_, _, _, lo_p, hi_p) in enumerate(plan):
            if lo_p == 0:
                @pl.when(k == 0)
                def _(p=p):
                    o_ref[...] = _dot_nt(a_refs[p][...], w_refs[p][...])
                lo_p = 1
            if hi_p > lo_p:
                @pl.when((k >= lo_p) & (k < hi_p))
                def _(p=p):
                    o_ref[...] += _dot_nt(a_refs[p][...], w_refs[p][...])

    in_specs, args = [], []
    for (arr, lead, col0), (_, cb, tk, lo_p, hi_p) in zip(parts, plan):
        def kk(k, lo_p=lo_p, hi_p=hi_p):
            return jnp.clip(k - lo_p, 0, hi_p - lo_p - 1)
        if lead is None:
            in_specs.append(pl.BlockSpec((tm, tk), lambda i, j, k, kk=kk: (i, kk(k))))
        elif lead == "stack":
            nkb = arr.shape[-1] // tk
            in_specs.append(pl.BlockSpec((None, tm, tk), lambda i, j, k, kk=kk, nkb=nkb: (kk(k) // nkb, i, kk(k) % nkb)))
        else:
            in_specs.append(pl.BlockSpec((None, tm, tk), lambda i, j, k, kk=kk, lead=lead: (lead, i, kk(k))))
        args.append(arr)
    for (_, cb, tk, lo_p, hi_p) in plan:
        def kk(k, lo_p=lo_p, hi_p=hi_p):
            return jnp.clip(k - lo_p, 0, hi_p - lo_p - 1)
        in_specs.append(pl.BlockSpec((tn, tk), lambda i, j, k, kk=kk, cb=cb: (j, cb + kk(k))))
        args.append(wfull)
    return pl.pallas_call(
        body, name=name, grid=(S // tm, out_rows // tn, nk),
        in_specs=in_specs, out_specs=pl.BlockSpec((tm, tn), lambda i, j, k: (i, j)),
        out_shape=jax.ShapeDtypeStruct((S, out_rows), F32),
        compiler_params=_cparams(),
    )(*args)


def _matmul_tn(a, parts, total, name, tile_major=False, also_bf16=False, rows=None, after=()):
    S = a.shape[0]
    m0, M = rows or (0, a.shape[1])
    tm = _tile(M, MM_TILE)
    ib = m0 // tm
    nout = 2 if also_bf16 else 1
    outs = None
    for idx, (arr, lead, col0) in enumerate(parts):
        n_p = arr.shape[-1]
        tn = math.gcd(_tile(n_p, MM_TILE), col0) if col0 else _tile(n_p, MM_TILE)
        cb = col0 // tn
        nb = n_p // tn
        if lead == "stack":
            n_p *= arr.shape[0]

        def body(*refs, tn=tn):
            a_ref, b_ref = refs[0], refs[1]
            o_refs = refs[-nout:]
            acc = _dot_tn(a_ref[...], b_ref[...])
            for o_ref in o_refs:
                if tile_major:
                    for t in range(tn // LANE):
                        o_ref[t] = acc[:, LANE * t:LANE * (t + 1)].astype(o_ref.dtype)
                else:
                    o_ref[...] = acc.astype(o_ref.dtype)

        in_specs = [pl.BlockSpec((S, tm), lambda i, j: (0, ib + i), pipeline_mode=pl.Buffered(1))]
        if lead is None:
            in_specs.append(pl.BlockSpec((S, tn), lambda i, j: (0, j)))
        elif lead == "stack":
            in_specs.append(pl.BlockSpec((None, S, tn), lambda i, j, nb=nb: (j // nb, 0, j % nb)))
        else:
            in_specs.append(pl.BlockSpec((None, S, tn), lambda i, j, lead=lead: (lead, 0, j)))
        args = [a, arr]
        aliases = {}
        if outs is not None:
            in_specs += [pl.BlockSpec(memory_space=pl.ANY)] * nout
            args += list(outs)
            aliases = {2 + o: o for o in range(nout)}
        else:
            in_specs += [pl.BlockSpec(memory_space=pl.ANY)] * len(after)
            args += list(after)
        if tile_major:
            out_spec = pl.BlockSpec((tn // LANE, tm, LANE), lambda i, j, cb=cb: (cb + j, i, 0))
            shape = (total // LANE, M, LANE)
        else:
            out_spec = pl.BlockSpec((tm, tn), lambda i, j, cb=cb: (i, cb + j))
            shape = (M, total)
        outs = pl.pallas_call(
            body, name=f"{name}_{idx}", grid=(M // tm, n_p // tn),
            in_specs=in_specs, out_specs=[out_spec] * nout,
            out_shape=[jax.ShapeDtypeStruct(shape, dt) for dt in (F32, BF16)[:nout]],
            input_output_aliases=aliases,
            compiler_params=_cparams(),
        )(*args)
    return tuple(outs) if also_bf16 else outs[0]


def _log_sigmoid(z):
    e = jnp.exp(-jnp.abs(z))
    return jnp.minimum(z, 0.0) - jnp.where(e < 1e-4, e * (1.0 - 0.5 * e), jnp.log(1.0 + e))


def _tri_sum(tri, x):
    hi, mid, lo = _split3(x)
    return _dot_nn(tri, hi) + _dot_nn(tri, mid) + _dot_nn(tri, lo)


def _fox_gate_fwd(fl, bias):
    S = fl.shape[0]

    nb_ = _row_tile(S, ROW_TILE)

    def body(f_ref, b_ref, c_ref):
        ri = lax.broadcasted_iota(jnp.int32, (nb_, nb_), 0)
        ci = lax.broadcasted_iota(jnp.int32, (nb_, nb_), 1)
        tri = jnp.where(ri >= ci, 1.0, 0.0).astype(BF16)
        row = lax.broadcasted_iota(jnp.int32, (nb_, LANE), 0)

        def step(i, carry):
            r0 = pl.multiple_of(i * nb_, nb_)
            t = _tri_sum(tri, _log_sigmoid(f_ref[pl.ds(r0, nb_), :] + b_ref[...])) + carry
            c_ref[pl.ds(r0, nb_), :] = t
            return jnp.sum(jnp.where(row == nb_ - 1, t, 0.0), axis=0, keepdims=True)

        lax.fori_loop(0, S // nb_, step, jnp.zeros((1, LANE), F32))

    vm = pl.BlockSpec(memory_space=pltpu.VMEM)
    return pl.pallas_call(
        body, name="fox_gate_fwd", in_specs=[vm, vm], out_specs=vm,
        out_shape=jax.ShapeDtypeStruct((S, LANE), F32),
        compiler_params=_cparams(),
    )(fl, bias)


def _fox_gate_bwd(fl, bias, dc):
    S = fl.shape[0]

    nb_ = _row_tile(S, ROW_TILE)

    def body(f_ref, b_ref, d_ref, o_ref, db_ref):
        ri = lax.broadcasted_iota(jnp.int32, (nb_, nb_), 0)
        ci = lax.broadcasted_iota(jnp.int32, (nb_, nb_), 1)
        tri = jnp.where(ri <= ci, 1.0, 0.0).astype(BF16)
        row = lax.broadcasted_iota(jnp.int32, (nb_, LANE), 0)
        nt = S // nb_

        def step(ii, carry):
            carry_c, carry_b = carry
            r0 = pl.multiple_of((nt - 1 - ii) * nb_, nb_)
            t = _tri_sum(tri, d_ref[pl.ds(r0, nb_), :]) + carry_c
            dz = t * _sigmoid(-(f_ref[pl.ds(r0, nb_), :] + b_ref[...]))
            o_ref[pl.ds(r0, nb_), :] = dz.astype(BF16)
            first = jnp.sum(jnp.where(row == 0, t, 0.0), axis=0, keepdims=True)
            return first, carry_b + jnp.sum(dz, axis=0, keepdims=True)

        zero = jnp.zeros((1, LANE), F32)
        _, db = lax.fori_loop(0, nt, step, (zero, zero))
        db_ref[...] = db

    vm = pl.BlockSpec(memory_space=pltpu.VMEM)
    return pl.pallas_call(
        body, name="fox_gate_bwd", in_specs=[vm, vm, vm], out_specs=[vm, vm],
        out_shape=[jax.ShapeDtypeStruct((S, LANE), BF16), jax.ShapeDtypeStruct((1, LANE), F32)],
        compiler_params=_cparams(),
    )(fl, bias, dc)


def _bias_lanes(col, lane, e, first):
    o0 = HEAD_DIM * (1 - e)
    hi, mid, lo = _split3(col)
    d0 = o0 if first else o0 + 3
    t = jnp.where((lane >= o0) & (lane < o0 + 6), jnp.ones(lane.shape, BF16), jnp.zeros(lane.shape, BF16))
    t = jnp.where(lane == d0, hi, t)
    t = jnp.where(lane == d0 + 1, mid, t)
    return jnp.where(lane == d0 + 2, lo, t)


def _fox_fwd(qkvg, c, H, gather=()):
    na = len(gather)
    S = qkvg.shape[0]
    W = H * HEAD_DIM
    HP = H // 2
    PP = 2 if HP % 2 == 0 else 1
    NE = 2 * PP
    tq = _row_tile(S, ATT_TILE)
    nq = S // tq
    wb = W // LANE
    scale = HEAD_DIM ** -0.5

    def body(*refs):
        q_ref, k_ref, v_ref, g_ref, c_ref = refs[:5]
        y_ref, o_ref, a_ref = refs[5 + na:8 + na]
        kaug_sc, vaug_sc, qaug_sc, s_sc, mb_sc, m_sc, acc_sc = refs[8 + 2 * na:15 + 2 * na]
        hp, qi = pl.program_id(0), pl.program_id(1)
        if na:
            remote, local = _direct_gather_copies(refs[5:5 + na], refs[8 + na:8 + 2 * na], *refs[15 + 2 * na:])

            @pl.when((hp == 0) & (qi == 0))
            def _():
                for cp in remote + local:
                    cp.start()
        lane = lax.broadcasted_iota(jnp.int32, (tq, LANE), 1)
        own = [lane < HEAD_DIM, lane >= HEAD_DIM]
        rows = lax.broadcasted_iota(jnp.int32, (tq, tq), 0)
        cols = lax.broadcasted_iota(jnp.int32, (tq, tq), 1)

        def bias_lanes(col, e, first):
            return _bias_lanes(col, lane, e % 2, first)

        def head_col(tile, e):
            return jnp.sum(jnp.where(lane == 2 * PP * hp + e, tile, 0.0), axis=1, keepdims=True)

        def tile_of(e):
            return slice(LANE * (e // 2), LANE * (e // 2 + 1))

        @pl.when(qi == 0)
        def _():
            def chunk(i, carry):
                r0 = pl.multiple_of(i * tq, tq)
                cb = c_ref[pl.ds(r0, tq), :]
                for e in range(NE):
                    kb, vb = k_ref[pl.ds(r0, tq), tile_of(e)], v_ref[pl.ds(r0, tq), tile_of(e)]
                    kaug_sc[e, pl.ds(r0, tq), :] = jnp.where(own[e % 2], kb, bias_lanes(-head_col(cb, e), e, False))
                    vaug_sc[e, pl.ds(r0, tq), :] = jnp.where(own[e % 2], vb, jnp.ones((tq, LANE), BF16))
                return carry
            lax.fori_loop(0, nq, chunk, 0)

        crow = c_ref[pl.ds(pl.multiple_of(qi * tq, tq), tq), :]
        ctq = [head_col(crow, e) for e in range(NE)]
        for e in range(NE):
            q = q_ref[:, tile_of(e)] * jnp.asarray(scale, BF16)
            qaug_sc[e] = jnp.where(own[e % 2], q, bias_lanes(ctq[e], e, True))
        m_sc[...] = jnp.full(m_sc.shape, NEG_INF, F32)
        acc_sc[...] = jnp.zeros(acc_sc.shape, F32)

        def scores(blk, slot, masked):
            k0 = pl.multiple_of(blk * tq, tq)
            for e in range(NE):
                s = _dot_nt(qaug_sc[e], kaug_sc[e, pl.ds(k0, tq), :])
                if masked:
                    s = jnp.where(rows >= cols, s, NEG_INF)
                s_sc[slot, e] = s
                mb_sc[slot, e] = jnp.broadcast_to(jnp.max(s, axis=1, keepdims=True), (tq, LANE))

        def accumulate(blk, slot):
            k0 = pl.multiple_of(blk * tq, tq)
            for e in range(NE):
                m_prev = m_sc[e]
                m_new = jnp.maximum(m_prev, mb_sc[slot, e])
                p = jnp.exp(s_sc[slot, e] - jnp.tile(m_new, (1, tq // LANE)))
                acc_sc[e] = jnp.exp(m_prev - m_new) * acc_sc[e] + _dot_nn(p.astype(BF16), vaug_sc[e, pl.ds(k0, tq), :])
                m_sc[e] = m_new

        def block_of(t):
            return jnp.where(t == 0, qi, t - 1)

        scores(qi, 0, True)

        def loop_body(t, carry):
            scores(t, (t + 1) % 2, False)
            accumulate(block_of(t), t % 2)
            return carry

        lax.fori_loop(0, qi, loop_body, 0)
        accumulate(block_of(qi), qi % 2)
        o_e, a_e = [], []
        for e in range(NE):
            acc = acc_sc[e]
            l = pltpu.roll(acc, HEAD_DIM, axis=1)
            o_e.append(acc / l)
            a_e.append(ctq[e] - (m_sc[e] + jnp.log(l)))
        for pp in range(PP):
            o = jnp.where(own[0], o_e[2 * pp], o_e[2 * pp + 1])
            g = g_ref[:, tile_of(2 * pp)].astype(F32)
            y_ref[:, tile_of(2 * pp)] = (o * (g * _sigmoid(g))).astype(BF16)
            o_ref[:, tile_of(2 * pp)] = o.astype(BF16)
            a_ref[pp] = jnp.where(own[0], a_e[2 * pp], a_e[2 * pp + 1])
        if na:
            @pl.when((hp == HP // PP - 1) & (qi == nq - 1))
            def _():
                _wait_all(remote, local)

    any_spec = pl.BlockSpec(memory_space=pl.ANY)
    sems = [pltpu.SemaphoreType.DMA((na, N_DEV - 1)), pltpu.SemaphoreType.DMA((na, N_DEV - 1)),
            pltpu.SemaphoreType.DMA((na,))] if na else []
    wide = PP * LANE
    outs = pl.pallas_call(
        body, name="fox_attn_fwd", grid=(HP // PP, nq),
        in_specs=[pl.BlockSpec((tq, wide), lambda h, i: (i, h)),
                  pl.BlockSpec((S, wide), lambda h, i: (0, wb // PP + h)),
                  pl.BlockSpec((S, wide), lambda h, i: (0, 2 * wb // PP + h)),
                  pl.BlockSpec((tq, wide), lambda h, i: (i, 3 * wb // PP + h)),
                  pl.BlockSpec((S, LANE), lambda h, i: (0, 0))] + [any_spec] * na,
        out_specs=[pl.BlockSpec((tq, wide), lambda h, i: (i, h)),
                   pl.BlockSpec((tq, wide), lambda h, i: (i, h)),
                   pl.BlockSpec((PP, tq, LANE), lambda h, i: (h, i, 0))] + [any_spec] * na,
        out_shape=[jax.ShapeDtypeStruct((S, W), BF16), jax.ShapeDtypeStruct((S, W), BF16),
                   jax.ShapeDtypeStruct((HP, S, LANE), F32)]
        + [jax.ShapeDtypeStruct((N_DEV,) + g.shape, g.dtype) for g in gather],
        scratch_shapes=[pltpu.VMEM((NE, S, LANE), BF16), pltpu.VMEM((NE, S, LANE), BF16),
                        pltpu.VMEM((NE, tq, LANE), BF16), pltpu.VMEM((2, NE, tq, tq), F32),
                        pltpu.VMEM((2, NE, tq, LANE), F32), pltpu.VMEM((NE, tq, LANE), F32),
                        pltpu.VMEM((NE, tq, LANE), F32)] + sems,
        compiler_params=_cparams(),
    )(qkvg, qkvg, qkvg, qkvg, c, *gather)
    return outs[0], outs[1], outs[2], list(outs[3:])


def _fox_out_bwd(dxb, wo, qkvg, o, a, H):
    S, D = dxb.shape
    W = H * HEAD_DIM
    tm, tn = _row_tile(S, EPI_TILE), _tile(W, EPI_TILE)
    npair = tn // LANE
    scale = HEAD_DIM ** -0.5

    def body(dx_ref, w_ref, q_ref, g_ref, o_ref, a_ref, qa_ref, da_ref, dg_ref):
        dy = _dot_nt(dx_ref[...], w_ref[...])
        lane = lax.broadcasted_iota(jnp.int32, (tm, LANE), 1)
        own = [lane < HEAD_DIM, lane >= HEAD_DIM]
        for p in range(npair):
            cols = slice(LANE * p, LANE * (p + 1))
            q = q_ref[:, cols] * jnp.asarray(scale, BF16)
            dyv, g, ov, at = dy[:, cols], g_ref[:, cols].astype(F32), o_ref[:, cols].astype(F32), a_ref[p]
            sg = _sigmoid(g)
            dob = (dyv * (g * sg)).astype(BF16)
            dg_ref[:, cols] = (dyv * ov * (sg * (1.0 + g * (1.0 - sg)))).astype(BF16)
            prod = dob.astype(F32) * ov
            for e in range(2):
                a_col = jnp.max(jnp.where(own[e], at, -jnp.inf), axis=1, keepdims=True)
                d_col = jnp.sum(jnp.where(own[e], prod, 0.0), axis=1, keepdims=True)
                qa_ref[e, :, cols] = jnp.where(own[e], q, _bias_lanes(a_col, lane, e, True))
                da_ref[e, :, cols] = jnp.where(own[e], dob, _bias_lanes(-d_col, lane, e, True))

    blk = pl.BlockSpec((tm, tn), lambda i, j: (i, j))
    pair = pl.BlockSpec((2, tm, tn), lambda i, j: (0, i, j))
    return pl.pallas_call(
        body, name="fox_out_bwd", grid=(S // tm, W // tn),
        in_specs=[pl.BlockSpec((tm, D), lambda i, j: (i, 0)), pl.BlockSpec((tn, D), lambda i, j: (j, 0)),
                  blk, pl.BlockSpec((tm, tn), lambda i, j: (i, 3 * W // tn + j)), blk,
                  pl.BlockSpec((npair, tm, LANE), lambda i, j: (j, i, 0))],
        out_specs=[pair, pair, pl.BlockSpec((None, tm, tn), lambda i, j: (3, i, j))],
        out_shape=[jax.ShapeDtypeStruct((2, S, W), BF16), jax.ShapeDtypeStruct((2, S, W), BF16),
                   jax.ShapeDtypeStruct((4, S, W), BF16)],
        compiler_params=_cparams(),
    )(dxb, wo, qkvg, qkvg, o, a)


def _fox_bwd(qaug, doaug, qkv, c, dqkvg, H, scatter=(), scatter_specs=()):
    na = len(scatter)
    S = qkv.shape[0]
    W = H * HEAD_DIM
    HP = H // 2
    tq = _row_tile(S, ATT_TILE)
    nq = S // tq
    wb = W // LANE
    scale = HEAD_DIM ** -0.5

    def body(*refs):
        qa_ref, da_ref, k_ref, v_ref, c_ref = refs[:5]
        out_ref, dcr_ref, dcc_ref = refs[6 + na:9 + na]
        dq_sc, dk_sc, dv_sc = refs[9 + 2 * na:12 + 2 * na]
        hp, kj = pl.program_id(0), pl.program_id(1)
        if na:
            remote = _direct_scatter_copies(refs[5:5 + na], refs[9 + na:9 + 2 * na], scatter_specs,
                                            *refs[12 + 2 * na:])

            @pl.when((hp == 0) & (kj == 0))
            def _():
                for cp in remote:
                    cp.start()
        lane = lax.broadcasted_iota(jnp.int32, (tq, LANE), 1)
        own = [lane < HEAD_DIM, lane >= HEAD_DIM]
        rows = lax.broadcasted_iota(jnp.int32, (tq, tq), 0)
        cols = lax.broadcasted_iota(jnp.int32, (tq, tq), 1)

        @pl.when(kj == 0)
        def _():
            dq_sc[...] = jnp.zeros(dq_sc.shape, F32)

        @pl.when((kj == 0) & (hp == 0))
        def _():
            dcr_ref[...] = jnp.zeros(dcr_ref.shape, F32)
            dcc_ref[...] = jnp.zeros(dcc_ref.shape, F32)

        kblk, vblk, cblk = k_ref[...], v_ref[...], c_ref[...]
        one, zero = jnp.ones((tq, LANE), BF16), jnp.zeros((tq, LANE), BF16)
        ka, va = [], []
        for e in range(2):
            o0 = HEAD_DIM * (1 - e)
            c_col = jnp.sum(jnp.where(lane == 2 * hp + e, cblk, 0.0), axis=1, keepdims=True)
            ka.append(jnp.where(own[e], kblk, _bias_lanes(-c_col, lane, e, False)))
            va.append(jnp.where(own[e], vblk, jnp.where((lane >= o0) & (lane < o0 + 3), one, zero)))
        dk_sc[...] = jnp.zeros(dk_sc.shape, F32)
        dv_sc[...] = jnp.zeros(dv_sc.shape, F32)

        def step(i, masked):
            r0 = pl.multiple_of(i * tq, tq)
            for e in range(2):
                qa = qa_ref[e, pl.ds(r0, tq), :]
                da = da_ref[e, pl.ds(r0, tq), :]
                p = jnp.exp(_dot_nt(qa, ka[e]))
                if masked:
                    p = jnp.where(rows >= cols, p, 0.0)
                ds = p * _dot_nt(da, va[e])
                pb, dsb = p.astype(BF16), ds.astype(BF16)
                dv_sc[e] += _dot_tn(pb, da)
                dk_sc[e] += _dot_tn(dsb, qa)
                dq_sc[e, pl.ds(r0, tq), :] += _dot_nn(dsb, ka[e])

        step(kj, True)

        def loop_body(i, carry):
            step(i, False)
            return carry

        lax.fori_loop(kj + 1, nq, loop_body, 0)
        k0 = pl.multiple_of(kj * tq, tq)
        out_ref[1, pl.ds(k0, tq), :] = jnp.where(own[0], dk_sc[0], dk_sc[1]).astype(BF16)
        out_ref[2, pl.ds(k0, tq), :] = jnp.where(own[0], dv_sc[0], dv_sc[1]).astype(BF16)

        def put_lane(ref, r0, e, tile, src_lane):
            col = jnp.sum(jnp.where(lane == src_lane, tile, 0.0), axis=1, keepdims=True)
            ref[pl.ds(r0, tq), :] = jnp.where(lane == 2 * hp + e, col, ref[pl.ds(r0, tq), :])

        for e in range(2):
            put_lane(dcc_ref, k0, e, dk_sc[e], HEAD_DIM * (1 - e) + 3)

        @pl.when(kj == nq - 1)
        def _():
            def chunk(i, carry):
                r0 = pl.multiple_of(i * tq, tq)
                d0, d1 = dq_sc[0, pl.ds(r0, tq), :], dq_sc[1, pl.ds(r0, tq), :]
                out_ref[0, pl.ds(r0, tq), :] = (jnp.where(own[0], d0, d1) * scale).astype(BF16)
                put_lane(dcr_ref, r0, 0, d0, HEAD_DIM)
                put_lane(dcr_ref, r0, 1, d1, 0)
                return carry
            lax.fori_loop(0, nq, chunk, 0)

        if na:
            @pl.when((hp == HP - 1) & (kj == nq - 1))
            def _():
                _wait_all(remote)

    pair = pl.BlockSpec((2, S, LANE), lambda h, j: (0, 0, h))
    vec = pl.BlockSpec((S, LANE), lambda h, j: (0, 0))
    any_spec = pl.BlockSpec(memory_space=pl.ANY)
    sems = [pltpu.SemaphoreType.DMA((na, N_DEV - 1)), pltpu.SemaphoreType.DMA((na, N_DEV - 1))] if na else []
    outs = pl.pallas_call(
        body, name="fox_attn_bwd", grid=(HP, nq),
        in_specs=[pair, pair,
                  pl.BlockSpec((tq, LANE), lambda h, j: (j, wb + h)),
                  pl.BlockSpec((tq, LANE), lambda h, j: (j, 2 * wb + h)),
                  pl.BlockSpec((tq, LANE), lambda h, j: (j, 0))] + [any_spec] * (na + 1),
        out_specs=[pl.BlockSpec((3, S, LANE), lambda h, j: (0, 0, h)), vec, vec] + [any_spec] * na,
        out_shape=[jax.ShapeDtypeStruct(dqkvg.shape, BF16), jax.ShapeDtypeStruct((S, LANE), F32),
                   jax.ShapeDtypeStruct((S, LANE), F32)]
        + [jax.ShapeDtypeStruct((N_DEV - 1,) + _scatter_block_shape(g, s), g.dtype)
           for g, s in zip(scatter, scatter_specs)],
        scratch_shapes=[pltpu.VMEM((2, S, LANE), F32), pltpu.VMEM((2, tq, LANE), F32),
                        pltpu.VMEM((2, tq, LANE), F32)] + sems,
        input_output_aliases={5 + na: 0},
        compiler_params=_cparams(),
    )(qaug, doaug, qkv, qkv, c, *scatter, dqkvg)
    return outs[0], outs[1], outs[2], list(outs[3:])


def _swa_pick(blk, half, lane):
    b = blk.astype(F32)
    r = pltpu.roll(b, HEAD_DIM, axis=1)
    return jnp.where(jnp.logical_xor(lane < HEAD_DIM, half == 1), b, r).astype(BF16)


def _swa_stack(t, lane, G):
    pieces = []
    z = jnp.zeros((SWA_BLOCK, LANE), t.dtype)
    for j in range(G // 2):
        tile = t[:, LANE * j:LANE * (j + 1)]
        pieces += [jnp.where(lane < HEAD_DIM, tile, z), jnp.where(lane < HEAD_DIM, z, tile)]
    return jnp.concatenate(pieces, axis=0)


def _swa_unstack(st, lane, G):
    tiles = []
    for j in range(G // 2):
        a = st[2 * j * SWA_BLOCK:(2 * j + 1) * SWA_BLOCK]
        b = st[(2 * j + 1) * SWA_BLOCK:(2 * j + 2) * SWA_BLOCK]
        tiles.append(jnp.where(lane < HEAD_DIM, a, b))
    return jnp.concatenate(tiles, axis=1)


def _swa_mask_bias(G):
    R = G * SWA_BLOCK
    t_loc = jnp.arange(R)[:, None] % SWA_BLOCK
    j_loc = jnp.arange(2 * SWA_BLOCK)[None, :]
    diff = t_loc + SWA_BLOCK - j_loc
    band = (diff >= 0) & (diff < SWA_BLOCK)
    return jnp.stack([jnp.where(band & (j_loc >= SWA_BLOCK), 0.0, NEG_INF),
                      jnp.where(band, 0.0, NEG_INF)]).astype(F32)


def _swa_scores(q, kp, kc, vp, vc, srow, bias, half, head0, G):
    lane = lax.broadcasted_iota(jnp.int32, (SWA_BLOCK, LANE), 1)
    kk = jnp.concatenate([_swa_pick(kp, half, lane), _swa_pick(kc, half, lane)], axis=0)
    vv = jnp.concatenate([_swa_pick(vp, half, lane), _swa_pick(vc, half, lane)], axis=0)
    qstack = _swa_stack(q, lane, G) * jnp.asarray(HEAD_DIM ** -0.5, BF16)
    s = _dot_nt(qstack, kk) + bias
    R = G * SWA_BLOCK
    lane1 = lax.broadcasted_iota(jnp.int32, (1, LANE), 1)
    sink = jnp.concatenate(
        [jnp.broadcast_to(jnp.sum(jnp.where(lane1 == head0 + g, srow, 0.0), axis=1, keepdims=True), (SWA_BLOCK, LANE))
         for g in range(G)], axis=0)
    m = jnp.maximum(jnp.broadcast_to(jnp.max(s, axis=1, keepdims=True), (R, LANE)), sink)
    e = jnp.exp(s - jnp.tile(m, (1, 2)))
    es = jnp.exp(sink - m)
    inv = 1.0 / (jnp.broadcast_to(jnp.sum(e, axis=1, keepdims=True), (R, LANE)) + es)
    return qstack, kk, vv, e * jnp.tile(inv, (1, 2)), es * inv, lane


def _swa_fwd(q, kv, gate, sinks, mask_bias, HQ, HKV):
    S = q.shape[0]
    G = HQ // HKV
    WQ, KVW = HQ * HEAD_DIM, HKV * HEAD_DIM
    nb = S // SWA_BLOCK
    GW = G * HEAD_DIM
    kb, vb = 0, KVW // LANE
    NH = min(HKV, 4)
    NP = NH // 2

    def body(q_ref, kp_ref, kc_ref, vp_ref, vc_ref, g_ref, sink_ref, b_ref, y_ref, o_ref):
        grp = pl.program_id(0)
        for hh in range(NH):
            cols, kt = slice(GW * hh, GW * (hh + 1)), slice(LANE * (hh // 2), LANE * (hh // 2 + 1))
            _, _, vv, p, _, lane = _swa_scores(q_ref[:, cols], kp_ref[:, kt], kc_ref[:, kt], vp_ref[:, kt], vc_ref[:, kt],
                                               sink_ref[...], b_ref[0], hh % 2, (NH * grp + hh) * G, G)
            o = _swa_unstack(_dot_nn(p.astype(BF16), vv), lane, G)
            g = g_ref[:, cols].astype(F32)
            y_ref[:, cols] = (o * (g * _sigmoid(g))).astype(BF16)
            o_ref[:, cols] = o.astype(BF16)

    blk = lambda cb, prev: pl.BlockSpec(
        (SWA_BLOCK, NP * LANE), lambda h, n, cb=cb, prev=prev: (jnp.maximum(n - prev, 0), cb // NP + h))
    qspec = pl.BlockSpec((SWA_BLOCK, NH * GW), lambda h, n: (n, h))
    return pl.pallas_call(
        body, name="swa_attn_fwd", grid=(HKV // NH, nb),
        in_specs=[qspec, blk(kb, 1), blk(kb, 0), blk(vb, 1), blk(vb, 0), qspec,
                  pl.BlockSpec((1, LANE), lambda h, n: (0, 0)),
                  pl.BlockSpec((1, G * SWA_BLOCK, 2 * SWA_BLOCK), lambda h, n: (jnp.minimum(n, 1), 0, 0))],
        out_specs=[qspec, qspec],
        out_shape=[jax.ShapeDtypeStruct((S, WQ), BF16), jax.ShapeDtypeStruct((S, WQ), BF16)],
        compiler_params=_cparams(),
    )(q, kv, kv, kv, kv, gate, sinks, mask_bias)


def _swa_bwd(q, kv, dy, gate, o, sinks, tables, mask_bias, HQ, HKV):
    S = q.shape[0]
    G = HQ // HKV
    WQ, KVW = HQ * HEAD_DIM, HKV * HEAD_DIM
    nb = S // SWA_BLOCK
    GW = G * HEAD_DIM
    R = G * SWA_BLOCK
    kb, vb = 0, KVW // LANE
    scale = HEAD_DIM ** -0.5
    NH = min(HKV, 4)
    NP = NH // 2
    assert G == 8

    def body(q_ref, kp_ref, kc_ref, vp_ref, vc_ref, dy_ref, g_ref, o_ref, sink_ref, t_ref, b_ref,
             dqg_ref, dkv_ref, dsink_ref, carry_sc):
        grp, n = pl.program_id(0), pl.program_id(1)

        @pl.when(n == 0)
        def _():
            carry_sc[...] = jnp.zeros(carry_sc.shape, F32)
            dsink_ref[...] = jnp.zeros(dsink_ref.shape, F32)

        @pl.when(n < nb)
        def _():
            t0, t1, t2 = (jnp.tile(t_ref[i], (1, GW // LANE)) for i in range(3))
            for hh in range(NH):
                cols, kt = slice(GW * hh, GW * (hh + 1)), slice(LANE * (hh // 2), LANE * (hh // 2 + 1))
                qstack, kk, vv, p, psink, lane = _swa_scores(
                    q_ref[:, cols], kp_ref[:, kt], kc_ref[:, kt], vp_ref[:, kt], vc_ref[:, kt], sink_ref[...], b_ref[0],
                    hh % 2, (NH * grp + hh) * G, G)
                dyv, g, ov = dy_ref[:, cols], g_ref[:, cols].astype(F32), o_ref[:, cols].astype(F32)
                sg = _sigmoid(g)
                dob = (dyv * (g * sg)).astype(BF16)
                dqg_ref[1, :, cols] = (dyv * ov * (sg * (1.0 + g * (1.0 - sg)))).astype(BF16)
                prod = dob.astype(F32) * ov
                dparts = []
                for j in range(G // 2):
                    tile = prod[:, LANE * j:LANE * (j + 1)]
                    for sel in (jnp.where(lane < HEAD_DIM, tile, 0.0), jnp.where(lane < HEAD_DIM, 0.0, tile)):
                        dparts.append(jnp.broadcast_to(jnp.sum(sel, axis=1, keepdims=True), (SWA_BLOCK, LANE)))
                delta = jnp.concatenate(dparts, axis=0)
                dostack = _swa_stack(dob, lane, G)
                ds = p * (_dot_nt(dostack, vv) - jnp.tile(delta, (1, 2)))
                dsb, pb = ds.astype(BF16), p.astype(BF16)
                dq = _swa_unstack(_dot_nn(dsb, kk), lane, G) * scale
                dq = dq * t0 + pltpu.roll(dq * t1, ROT_DIM // 2, axis=1) + pltpu.roll(dq * t2, GW - ROT_DIM // 2, axis=1)
                dqg_ref[0, :, cols] = dq.astype(BF16)
                dkk = _dot_tn(dsb, qstack)
                dvv = _dot_tn(pb, dostack)
                dkk = dkk + pltpu.roll(dkk, HEAD_DIM, axis=1)
                dvv = dvv + pltpu.roll(dvv, HEAD_DIM, axis=1)
                lane2 = lax.broadcasted_iota(jnp.int32, (2 * SWA_BLOCK, LANE), 1)
                comb = jnp.where(lane2 < HEAD_DIM, dkk, dvv)
                dkv_ref[hh] = carry_sc[hh] + comb[:SWA_BLOCK]
                carry_sc[hh] = comb[SWA_BLOCK:]
                sk = psink * delta
                rows = [-jnp.sum(sk[g_ * SWA_BLOCK:(g_ + 1) * SWA_BLOCK], axis=0, keepdims=True) for g_ in range(G)]
                dsink_ref[hh] += jnp.concatenate(rows, axis=0)

        @pl.when(n == nb)
        def _():
            dkv_ref[...] = carry_sc[...]

    cl = lambda n: jnp.minimum(n, nb - 1)
    blk = lambda cb, prev: pl.BlockSpec(
        (SWA_BLOCK, NP * LANE), lambda h, n, cb=cb, prev=prev: (jnp.maximum(cl(n) - prev, 0), cb // NP + h))
    qspec = pl.BlockSpec((SWA_BLOCK, NH * GW), lambda h, n: (cl(n), h))
    return pl.pallas_call(
        body, name="swa_attn_bwd", grid=(HKV // NH, nb + 1),
        in_specs=[qspec, blk(kb, 1), blk(kb, 0), blk(vb, 1), blk(vb, 0), qspec, qspec, qspec,
                  pl.BlockSpec((1, LANE), lambda h, n: (0, 0)),
                  pl.BlockSpec((3, SWA_BLOCK, LANE), lambda h, n: (0, cl(n), 0)),
                  pl.BlockSpec((1, R, 2 * SWA_BLOCK), lambda h, n: (jnp.minimum(n, 1), 0, 0))],
        out_specs=[pl.BlockSpec((2, SWA_BLOCK, NH * GW), lambda h, n: (0, cl(n), h)),
                   pl.BlockSpec((NH, SWA_BLOCK, LANE), lambda h, n: (h, jnp.maximum(n - 1, 0), 0)),
                   pl.BlockSpec((NH, 8, LANE), lambda h, n: (h, 0, 0))],
        out_shape=[jax.ShapeDtypeStruct((2, S, WQ), BF16), jax.ShapeDtypeStruct((HKV, S, LANE), F32),
                   jax.ShapeDtypeStruct((HKV, 8, LANE), F32)],
        scratch_shapes=[pltpu.VMEM((NH, SWA_BLOCK, LANE), F32)],
        compiler_params=_cparams(),
    )(q, kv, kv, kv, kv, dy, gate, o, sinks, tables, mask_bias)


def _swa_dkv_finish(dkv, tables):
    HKV, S, _ = dkv.shape
    KVW = HKV * HEAD_DIM
    tm = _row_tile(S, EPI_TILE)
    npair = HKV // 2

    def body(d_ref, t_ref, o_ref):
        lane = lax.broadcasted_iota(jnp.int32, (tm, LANE), 1)
        lo = lane < HEAD_DIM
        for p in range(npair):
            a, b = d_ref[2 * p], d_ref[2 * p + 1]
            tk = jnp.where(lo, a, pltpu.roll(b, HEAD_DIM, axis=1))
            tv = jnp.where(lo, pltpu.roll(a, HEAD_DIM, axis=1), b)
            tk = (tk * t_ref[0] + pltpu.roll(tk * t_ref[1], ROT_DIM // 2, axis=1)
                  + pltpu.roll(tk * t_ref[2], LANE - ROT_DIM // 2, axis=1))
            o_ref[:, LANE * p:LANE * (p + 1)] = tk.astype(BF16)
            o_ref[:, KVW + LANE * p:KVW + LANE * (p + 1)] = tv.astype(BF16)

    return pl.pallas_call(
        body, name="swa_dkv_finish", grid=(S // tm,),
        in_specs=[pl.BlockSpec((HKV, tm, LANE), lambda i: (0, i, 0)), pl.BlockSpec((3, tm, LANE), lambda i: (0, i, 0))],
        out_specs=pl.BlockSpec((tm, 2 * KVW), lambda i: (i, 0)),
        out_shape=jax.ShapeDtypeStruct((S, 2 * KVW), BF16),
        compiler_params=_cparams(),
    )(dkv, tables)


def _rope_tables(S, width):
    half = ROT_DIM // 2
    pos = jnp.arange(S, dtype=F32)
    inv_freq = ROPE_THETA ** (-jnp.arange(half, dtype=F32) / half)
    ang = pos[:, None] * inv_freq[None, :]
    cos, sin = jnp.cos(ang), jnp.sin(ang)
    one = jnp.ones((S, HEAD_DIM - ROT_DIM), F32)
    zero = jnp.zeros((S, HEAD_DIM - ROT_DIM), F32)
    zh = jnp.zeros((S, half), F32)
    t0 = jnp.concatenate([cos, cos, one], axis=1)
    t1 = jnp.concatenate([-sin, zh, zero], axis=1)
    t2 = jnp.concatenate([zh, sin, zero], axis=1)
    return jnp.stack([jnp.tile(t, (1, width // HEAD_DIM)) for t in (t0, t1, t2)])


def _pad_rows(v, row, total_rows=8):
    return jnp.pad(v, ((row, total_rows - row - v.shape[0]), (0, 0)))


def _pad_lanes(v, off, width):
    return jnp.pad(v, ((0, 0), (off, width - off - v.shape[1])))


def kernel(x, norm_g, fox_w_in, fox_b_f, fox_w_out, swa_w_in, swa_sinks, swa_w_out, final_g, loss_target, m_norm_g, m_fox_w_in, m_fox_b_f, m_fox_w_out, m_swa_w_in, m_swa_sinks, m_swa_w_out, m_final_g, v_norm_g, v_fox_w_in, v_fox_b_f, v_fox_w_out, v_swa_w_in, v_swa_sinks, v_swa_w_out, v_final_g):
    S, D = x.shape[1], x.shape[2]
    H = fox_b_f.shape[1]
    W = H * HEAD_DIM
    wf = fox_w_in.shape[2]
    ws = swa_w_in.shape[2]
    HQ = swa_sinks.shape[1]
    WQ = HQ * HEAD_DIM
    KVW = (ws * N_DEV - 2 * WQ) // 2
    HKV = KVW // HEAD_DIM
    rows_o = fox_w_out.shape[1]
    assert wf * N_DEV == 4 * W + H and rows_o * N_DEV == W and H <= LANE and HQ <= LANE
    me = _my_index()

    _, sw_f, np_f = _slab_geom(wf)
    _, sw_s, np_s = _slab_geom(ws)

    def slab(w2d, w, sw):
        return jnp.pad(w2d.astype(BF16), ((0, 0), (0, sw - w)))

    x0 = x[0]
    g0, g1, gf = norm_g[0:1], norm_g[1:2], final_g[None, :]
    fi_all, h0 = _all_gather([slab(fox_w_in[0], wf, sw_f)], x0, g0)
    w_fi = _assemble(fi_all, wf)
    later = [slab(swa_w_in[0], ws, sw_s), fox_w_out[0].astype(BF16), swa_w_out[0].astype(BF16)]

    bias = _pad_lanes(fox_b_f, 0, LANE)
    sinks = _pad_lanes(swa_sinks, 0, LANE)
    tab_k = _rope_tables(S, LANE)
    mask_bias = _swa_mask_bias(HQ // HKV)

    qkv0 = _proj(h0, w_fi, 0, 4 * W, BF16, "fox_in_qkvg")
    fl = _proj(h0, w_fi, 4 * W, LANE, F32, "fox_in_f")
    c = _fox_gate_fwd(fl, bias)
    y0, o0, a0, (si_all, fo_all, so_all) = _fox_fwd(qkv0, c, H, gather=later)
    w_si = _assemble(si_all, ws)
    w_fo = fo_all.reshape(W, D)
    w_so = so_all.reshape(WQ, D)
    x1, h1 = _out_proj_norm(y0, w_fo, x0, g1, "fox_out")

    q1 = _proj(h1, w_si, 0, WQ, BF16, "swa_in_q", rope=(tab_k, WQ))
    kv1 = _proj(h1, w_si, WQ, 2 * KVW, BF16, "swa_in_kv", rope=(tab_k, KVW))
    gate1 = _proj(h1, w_si, WQ + 2 * KVW, WQ, BF16, "swa_in_gate")
    y1, o1 = _swa_fwd(q1, kv1, gate1, sinks, mask_bias, HQ, HKV)
    dx2, dx2b, dgf, loss_p = _out_proj_loss(y1, w_so, x1, loss_target[0], gf, "swa_out_loss")

    dy1 = _matmul_nt([(dx2b, None, 0)], w_so, WQ, "swa_out_bwd")
    g_so, g_so_h = _matmul_tn(y1, [(dx2b, None, 0)], D, "swa_out_wgrad", also_bf16=True)
    dqg1, dkv1, dsink = _swa_bwd(q1, kv1, dy1, gate1, o1, sinks, tab_k, mask_bias, HQ, HKV)
    dkv1f = _swa_dkv_finish(dkv1, tab_k)
    parts1 = [(dqg1, 0, 0), (dkv1f, None, WQ), (dqg1, 1, WQ + 2 * KVW)]
    g_si, g_si_h = _matmul_tn(h1, parts1, np_s, "swa_in_wgrad", tile_major=True, also_bf16=True)
    dh1 = _matmul_nt(parts1, w_si, D, "swa_in_bwd")
    dx1, dx1b, dg1 = _rmsnorm_bwd(dh1, x1, g1, dx2, "rmsnorm1_bwd")

    qaug0, doaug0, dqkvg0 = _fox_out_bwd(dx1b, w_fo, qkv0, o0, a0, H)
    g_fo, g_fo_h = _matmul_tn(y0, [(dx1b, None, 0)], D, "fox_out_wgrad", also_bf16=True)
    early_specs = [("col", ws), ("row", rows_o), ("row", rows_o)]
    dqkvg0, dcr, dcc, early_recv = _fox_bwd(qaug0, doaug0, qkv0, c, dqkvg0, H, scatter=[g_si_h, g_fo_h, g_so_h],
                                           scatter_specs=early_specs)
    dfl, dbf = _fox_gate_bwd(fl, bias, dcr - dcc)
    parts0 = [(dqkvg0, "stack", 0), (dfl, None, 4 * W)]
    spec_fi = ("col", wf)
    fi_halves, token = [], None
    for half in range(2):
        g_fi, g_fi_h = _matmul_tn(h0, parts0, np_f, f"fox_in_wgrad_rows{half}", tile_major=True, also_bf16=True,
                                  rows=(half * (D // 2), D // 2), after=() if token is None else (token,))
        fi_sems, fi_src, fi_land, token = _scatter_start(g_fi_h, spec_fi)
        fi_halves.append((g_fi, fi_sems, fi_src, fi_land))
    parts0[-1] = (dfl + token[0, 0].astype(BF16), None, 4 * W)
    dh0 = _matmul_nt(parts0, w_fi, D, "fox_in_bwd")
    dx0, _, dg0 = _rmsnorm_bwd(dh0, x0, g0, dx1, "rmsnorm0_bwd")

    red_si, gw_fo, gw_so = [_final_sum8(g_, r_, s_)
                            for g_, r_, s_ in zip([g_si, g_fo, g_so], early_recv, early_specs)]
    gt_si = lax.dynamic_slice(red_si, ((ws * me) % LANE, 0), (ws, D))

    def t_in(p):
        return jnp.swapaxes(p[0], 0, 1)

    def t_out(t):
        return jnp.swapaxes(t, 0, 1)[None]

    P = D
    dsink_v = dsink[:, :, 0].reshape(1, HQ)
    row3 = _pad_lanes(dbf[:, :H], 0, P) + _pad_lanes(dsink_v, LANE, P) + _pad_lanes(loss_p[:, :1], 2 * LANE, P)
    pack = _pad_rows(dg0, 0) + _pad_rows(dg1, 1) + _pad_rows(dgf, 2) + _pad_rows(row3, 3)

    d_fo, m_fo, v_fo = _adamw(fox_w_out[0], gw_fo, m_fox_w_out[0], v_fox_w_out[0], "adamw_fox_out")
    d_si, m_si, v_si = _adamw(t_in(swa_w_in), gt_si, t_in(m_swa_w_in), t_in(v_swa_w_in), "adamw_swa_in")
    d_so, m_so, v_so = _adamw(swa_w_out[0], gw_so, m_swa_w_out[0], v_swa_w_out[0], "adamw_swa_out")
    behind, red_fi = [dx0, pack, d_fo, d_si, d_so], []
    for g_fi, fi_sems, fi_src, fi_land in fi_halves:
        recv_fi = _scatter_wait(fi_sems, fi_src, fi_land, spec_fi, after=behind)
        red_fi.append(_final_sum8(g_fi, recv_fi, spec_fi))
        behind = [recv_fi]
    red_fi = jnp.concatenate(red_fi, axis=1)
    gt_fi = lax.dynamic_slice(red_fi, ((wf * me) % LANE, 0), (wf, D))

    tot, d_fi, m_fi, v_fi = _all_reduce_small(
        pack, recv_fi, (t_in(fox_w_in), gt_fi, t_in(m_fox_w_in), t_in(v_fox_w_in)))
    gw_si, d_si, m_si, v_si = [t_out(t)[0] for t in (gt_si, d_si, m_si, v_si)]
    gw_fi, d_fi, m_fi, v_fi = [t_out(t)[0] for t in (gt_fi, d_fi, m_fi, v_fi)]

    loss = tot[3, 2 * LANE]
    g_norm = tot[0:2]
    g_final = tot[2]
    g_bf = tot[3:4, 0:H]
    g_sinks = tot[3:4, LANE:LANE + HQ]

    def small_pack(ng, fg, bf, sk):
        r3 = _pad_lanes(bf, 0, P) + _pad_lanes(sk, LANE, P)
        return _pad_rows(ng, 0) + _pad_rows(fg[None, :], 2) + _pad_rows(r3, 3)

    sd, sm, sv = _adamw(small_pack(norm_g, final_g, fox_b_f, swa_sinks), tot,
                        small_pack(m_norm_g, m_final_g, m_fox_b_f, m_swa_sinks),
                        small_pack(v_norm_g, v_final_g, v_fox_b_f, v_swa_sinks), "adamw_small")

    def unpack(t):
        return t[0:2], t[3:4, 0:H], t[3:4, LANE:LANE + HQ], t[2]

    def group(small, fi, fo, si, so):
        ng, bf, sk, fg = unpack(small)
        return (ng, fi[None], bf, fo[None], si[None], sk, so[None], fg)

    grads = (g_norm, gw_fi[None], g_bf, gw_fo[None], gw_si[None], g_sinks, gw_so[None], g_final)
    return (loss, dx0[None], *grads, *group(sd, d_fi, d_fo, d_si, d_so),
            *group(sm, m_fi, m_fo, m_si, m_so), *group(sv, v_fi, v_fo, v_si, v_so))
```
